```python
import jax, jax.numpy as jnp
from jax import lax
import numpy as np

D_MODEL = 1024
BATCH = 8
SEQ = 4096
DEPTH = 1

N_META = 16
D_MIX = D_MODEL
D_RG = D_MIX // 2
RG_HEADS = 8
RG_HEAD_DIM = D_RG // RG_HEADS
CONV_W = 4
LRU_C = 8.0
D_HG = D_MIX - D_RG
HG_HEAD_DIM = 128
HG_HEADS = D_HG // HG_HEAD_DIM
HG_CHUNK = 64
D_IN = 2 * D_RG + 4 * D_HG
D_FF = ((8 * D_MODEL // 3 + 255) // 256) * 256
EPS = 1e-6

kernel_name = "hymba_rglru_hgrn2_block"


def rmsnorm(x, g):
    xf = x.astype(jnp.float32)
    y = xf * lax.rsqrt(jnp.mean(xf * xf, axis=-1, keepdims=True) + EPS) * g.astype(jnp.float32)
    return y.astype(x.dtype)


def _lin_combine(e1, e2):
    a1, b1 = e1
    a2, b2 = e2
    return a1 * a2, a2 * b1 + b2


def rg_lru_group(xr, gr, conv_w, conv_b, w_r, b_r, w_i, b_i, lam, norm_g):
    B, L, _ = xr.shape
    xp = jnp.pad(xr.astype(jnp.float32), ((0, 0), (CONV_W - 1, 0), (0, 0)))
    cw = conv_w.astype(jnp.float32)
    xc = conv_b.astype(jnp.float32) + sum(xp[:, j:j + L] * cw[j] for j in range(CONV_W))
    xh = xc.reshape(B, L, RG_HEADS, RG_HEAD_DIM)
    r = jax.nn.sigmoid(jnp.einsum('blhi,hij->blhj', xh, w_r.astype(jnp.float32)).reshape(B, L, D_RG) + b_r.astype(jnp.float32))
    i = jax.nn.sigmoid(jnp.einsum('blhi,hij->blhj', xh, w_i.astype(jnp.float32)).reshape(B, L, D_RG) + b_i.astype(jnp.float32))
    log_a = -LRU_C * jax.nn.softplus(-lam.astype(jnp.float32)) * r
    a = jnp.exp(log_a)
    bx = jnp.sqrt(-jnp.expm1(2.0 * log_a)) * (i * xc)
    _, h = lax.associative_scan(_lin_combine, (a, bx), axis=1)
    y = jax.nn.gelu(gr.astype(jnp.float32)) * h
    return rmsnorm(y, norm_g)


def _to_chunks(t, pad):
    B, L, _ = t.shape
    t = jnp.pad(t, ((0, 0), (pad, 0), (0, 0)))
    n = (L + pad) // HG_CHUNK
    t = t.reshape(B, n, HG_CHUNK, HG_HEADS, HG_HEAD_DIM)
    return jnp.transpose(t, (1, 0, 3, 2, 4))


def _hgrn2_chunk_step(S, inp):
    q, k, v, lf = inp
    b = jnp.cumsum(lf, axis=2)
    inter = jnp.einsum('bhck,bhkv->bhcv', q * jnp.exp(b), S)
    diff = b[:, :, :, None, :] - b[:, :, None, :, :]
    causal = (jnp.arange(HG_CHUNK)[:, None] >= jnp.arange(HG_CHUNK)[None, :])[None, None, :, :, None]
    decay = jnp.where(causal, jnp.exp(jnp.where(causal, diff, 0.0)), 0.0)
    A = jnp.einsum('bhtsk,bhsk->bhts', q[:, :, :, None, :] * decay, k)
    intra = jnp.einsum('bhts,bhsv->bhtv', A, v)
    b_last = b[:, :, -1:, :]
    S_new = jnp.exp(b_last[:, :, 0, :])[..., None] * S + jnp.einsum('bhsk,bhsv->bhkv', k * jnp.exp(b_last - b), v)
    return S_new, inter + intra


def hgrn2_group(hq, hf, hi, hg, lb, norm_g):
    B, L, _ = hq.shape
    lb = lb.astype(jnp.float32)
    q = jax.nn.silu(hq.astype(jnp.float32))
    f = lb + (1.0 - lb) * jax.nn.sigmoid(hf.astype(jnp.float32))
    log_f = jnp.log(f)
    k = 1.0 - f
    v = hi.astype(jnp.float32)
    pad = HG_CHUNK - N_META
    qc, kc, vc, lfc = (_to_chunks(t, pad) for t in (q, k, v, log_f))
    S0 = jnp.zeros((B, HG_HEADS, HG_HEAD_DIM, HG_HEAD_DIM), jnp.float32)
    _, o = lax.scan(_hgrn2_chunk_step, S0, (qc, kc, vc, lfc))
    n = o.shape[0]
    o = jnp.transpose(o, (1, 0, 3, 2, 4)).reshape(B, n * HG_CHUNK, HG_HEADS, HG_HEAD_DIM)[:, pad:]
    o = rmsnorm(o, norm_g).astype(jnp.float32) * jax.nn.silu(hg.astype(jnp.float32).reshape(B, L, HG_HEADS, HG_HEAD_DIM))
    return o.reshape(B, L, D_HG)


def _fwd_setup_inputs(seed: int = 0) -> dict:
    key = jax.random.key(seed)
    ks = jax.random.split(key, 24)
    f32 = jnp.float32
    nrm = lambda k, shape, s: s * jax.random.normal(k, shape, f32)
    u = jax.random.uniform(ks[9], (DEPTH, D_RG), f32, 0.9, 0.999)
    s = u ** (1.0 / LRU_C)
    lru_lambda = jnp.log(s) - jnp.log1p(-s)
    return {
        "x": jax.random.normal(ks[0], (BATCH, SEQ, D_MODEL), f32),
        "meta_tokens": nrm(ks[1], (N_META, D_MODEL), 1.0),
        "mix_norm_g": 1.0 + nrm(ks[2], (DEPTH, D_MODEL), 0.02),
        "w_in": nrm(ks[3], (DEPTH, D_MODEL, D_IN), D_MODEL ** -0.5),
        "conv_w": nrm(ks[4], (DEPTH, CONV_W, D_RG), CONV_W ** -0.5),
        "conv_b": nrm(ks[5], (DEPTH, D_RG), 0.01),
        "w_rgate": nrm(ks[6], (DEPTH, RG_HEADS, RG_HEAD_DIM, RG_HEAD_DIM), RG_HEAD_DIM ** -0.5),
        "b_rgate": nrm(ks[7], (DEPTH, D_RG), 0.01),
        "w_igate": nrm(ks[8], (DEPTH, RG_HEADS, RG_HEAD_DIM, RG_HEAD_DIM), RG_HEAD_DIM ** -0.5),
        "b_igate": nrm(ks[10], (DEPTH, D_RG), 0.01),
        "lru_lambda": lru_lambda,
        "rg_norm_g": 1.0 + nrm(ks[11], (DEPTH, D_RG), 0.02),
        "hg_lower_bound": nrm(ks[12], (DEPTH + 1, D_HG), 0.1),
        "hg_norm_g": 1.0 + nrm(ks[13], (DEPTH, HG_HEAD_DIM), 0.02),
        "w_out": nrm(ks[14], (DEPTH, D_MIX, D_MODEL), D_MIX ** -0.5),
        "ffn_norm_g": 1.0 + nrm(ks[15], (DEPTH, D_MODEL), 0.02),
        "w_gate_up": nrm(ks[16], (DEPTH, D_MODEL, 2 * D_FF), D_MODEL ** -0.5),
        "w_down": nrm(ks[17], (DEPTH, D_FF, D_MODEL), D_FF ** -0.5),
        "final_norm_g": 1.0 + nrm(ks[18], (D_MODEL,), 0.02),
    }


def _fwd_reference(x, meta_tokens, mix_norm_g, w_in, conv_w, conv_b, w_rgate, b_rgate, w_igate, b_igate,
              lru_lambda, rg_norm_g, hg_lower_bound, hg_norm_g, w_out, ffn_norm_g, w_gate_up, w_down,
              final_norm_g):
    B = x.shape[0]
    meta = jnp.broadcast_to(meta_tokens.astype(x.dtype)[None], (B, N_META, D_MODEL))
    h = jnp.concatenate([meta, x], axis=1)
    lbs = jnp.cumsum(jax.nn.softmax(hg_lower_bound.astype(jnp.float32), axis=0), axis=0)
    splits = np.cumsum([D_RG, D_RG, D_HG, D_HG, D_HG])
    for l in range(DEPTH):
        u = rmsnorm(h, mix_norm_g[l])
        p = jnp.einsum('bld,de->ble', u, w_in[l])
        rg_x, rg_g, hq, hf, hi, hg = jnp.split(p, splits, axis=-1)
        y_rg = rg_lru_group(rg_x, rg_g, conv_w[l], conv_b[l], w_rgate[l], b_rgate[l],
                            w_igate[l], b_igate[l], lru_lambda[l], rg_norm_g[l])
        y_hg = hgrn2_group(hq, hf, hi, hg, lbs[l], hg_norm_g[l])
        y = jnp.concatenate([y_rg.astype(h.dtype), y_hg.astype(h.dtype)], axis=-1)
        h = h + jnp.einsum('ble,ed->bld', y, w_out[l])
        v = rmsnorm(h, ffn_norm_g[l])
        gate, up = jnp.split(jnp.einsum('bld,df->blf', v, w_gate_up[l]), 2, axis=-1)
        h = h + jnp.einsum('blf,fd->bld', jax.nn.silu(gate) * up, w_down[l])
    return rmsnorm(h, final_norm_g)[:, N_META:]


import jax as _jax
import jax.numpy as _jnp

TWIN_FORMAT = 'train_step'
FWD_PARAMS = ['x', 'meta_tokens', 'mix_norm_g', 'w_in', 'conv_w', 'conv_b', 'w_rgate', 'b_rgate', 'w_igate', 'b_igate', 'lru_lambda', 'rg_norm_g', 'hg_lower_bound', 'hg_norm_g', 'w_out', 'ffn_norm_g', 'w_gate_up', 'w_down', 'final_norm_g']
TWIN_WEIGHTS = ['meta_tokens', 'mix_norm_g', 'w_in', 'conv_w', 'conv_b', 'w_rgate', 'b_rgate', 'w_igate', 'b_igate', 'lru_lambda', 'rg_norm_g', 'hg_lower_bound', 'hg_norm_g', 'w_out', 'ffn_norm_g', 'w_gate_up', 'w_down', 'final_norm_g']
TWIN_DIFF_INPUT = 'x'
TWIN_INPUTS = ['x', 'meta_tokens', 'mix_norm_g', 'w_in', 'conv_w', 'conv_b', 'w_rgate', 'b_rgate', 'w_igate', 'b_igate', 'lru_lambda', 'rg_norm_g', 'hg_lower_bound', 'hg_norm_g', 'w_out', 'ffn_norm_g', 'w_gate_up', 'w_down', 'final_norm_g', 'loss_target', 'm_meta_tokens', 'm_mix_norm_g', 'm_w_in', 'm_conv_w', 'm_conv_b', 'm_w_rgate', 'm_b_rgate', 'm_w_igate', 'm_b_igate', 'm_lru_lambda', 'm_rg_norm_g', 'm_hg_lower_bound', 'm_hg_norm_g', 'm_w_out', 'm_ffn_norm_g', 'm_w_gate_up', 'm_w_down', 'm_final_norm_g', 'v_meta_tokens', 'v_mix_norm_g', 'v_w_in', 'v_conv_w', 'v_conv_b', 'v_w_rgate', 'v_b_rgate', 'v_w_igate', 'v_b_igate', 'v_lru_lambda', 'v_rg_norm_g', 'v_hg_lower_bound', 'v_hg_norm_g', 'v_w_out', 'v_ffn_norm_g', 'v_w_gate_up', 'v_w_down', 'v_final_norm_g']
TWIN_OUTPUTS = ['loss', 'grad_x', 'grad_meta_tokens', 'grad_mix_norm_g', 'grad_w_in', 'grad_conv_w', 'grad_conv_b', 'grad_w_rgate', 'grad_b_rgate', 'grad_w_igate', 'grad_b_igate', 'grad_lru_lambda', 'grad_rg_norm_g', 'grad_hg_lower_bound', 'grad_hg_norm_g', 'grad_w_out', 'grad_ffn_norm_g', 'grad_w_gate_up', 'grad_w_down', 'grad_final_norm_g', 'delta_meta_tokens', 'delta_mix_norm_g', 'delta_w_in', 'delta_conv_w', 'delta_conv_b', 'delta_w_rgate', 'delta_b_rgate', 'delta_w_igate', 'delta_b_igate', 'delta_lru_lambda', 'delta_rg_norm_g', 'delta_hg_lower_bound', 'delta_hg_norm_g', 'delta_w_out', 'delta_ffn_norm_g', 'delta_w_gate_up', 'delta_w_down', 'delta_final_norm_g', 'new_m_meta_tokens', 'new_m_mix_norm_g', 'new_m_w_in', 'new_m_conv_w', 'new_m_conv_b', 'new_m_w_rgate', 'new_m_b_rgate', 'new_m_w_igate', 'new_m_b_igate', 'new_m_lru_lambda', 'new_m_rg_norm_g', 'new_m_hg_lower_bound', 'new_m_hg_norm_g', 'new_m_w_out', 'new_m_ffn_norm_g', 'new_m_w_gate_up', 'new_m_w_down', 'new_m_final_norm_g', 'new_v_meta_tokens', 'new_v_mix_norm_g', 'new_v_w_in', 'new_v_conv_w', 'new_v_conv_b', 'new_v_w_rgate', 'new_v_b_rgate', 'new_v_w_igate', 'new_v_b_igate', 'new_v_lru_lambda', 'new_v_rg_norm_g', 'new_v_hg_lower_bound', 'new_v_hg_norm_g', 'new_v_w_out', 'new_v_ffn_norm_g', 'new_v_w_gate_up', 'new_v_w_down', 'new_v_final_norm_g']
TWIN_LEAF_KINDS = {'loss': 'loss', 'grad_x': 'grad_x', 'grad_meta_tokens': 'grad_w', 'grad_mix_norm_g': 'grad_w', 'grad_w_in': 'grad_w', 'grad_conv_w': 'grad_w', 'grad_conv_b': 'grad_w', 'grad_w_rgate': 'grad_w', 'grad_b_rgate': 'grad_w', 'grad_w_igate': 'grad_w', 'grad_b_igate': 'grad_w', 'grad_lru_lambda': 'grad_w', 'grad_rg_norm_g': 'grad_w', 'grad_hg_lower_bound': 'grad_w', 'grad_hg_norm_g': 'grad_w', 'grad_w_out': 'grad_w', 'grad_ffn_norm_g': 'grad_w', 'grad_w_gate_up': 'grad_w', 'grad_w_down': 'grad_w', 'grad_final_norm_g': 'grad_w', 'delta_meta_tokens': 'delta_w', 'delta_mix_norm_g': 'delta_w', 'delta_w_in': 'delta_w', 'delta_conv_w': 'delta_w', 'delta_conv_b': 'delta_w', 'delta_w_rgate': 'delta_w', 'delta_b_rgate': 'delta_w', 'delta_w_igate': 'delta_w', 'delta_b_igate': 'delta_w', 'delta_lru_lambda': 'delta_w', 'delta_rg_norm_g': 'delta_w', 'delta_hg_lower_bound': 'delta_w', 'delta_hg_norm_g': 'delta_w', 'delta_w_out': 'delta_w', 'delta_ffn_norm_g': 'delta_w', 'delta_w_gate_up': 'delta_w', 'delta_w_down': 'delta_w', 'delta_final_norm_g': 'delta_w', 'new_m_meta_tokens': 'new_m', 'new_m_mix_norm_g': 'new_m', 'new_m_w_in': 'new_m', 'new_m_conv_w': 'new_m', 'new_m_conv_b': 'new_m', 'new_m_w_rgate': 'new_m', 'new_m_b_rgate': 'new_m', 'new_m_w_igate': 'new_m', 'new_m_b_igate': 'new_m', 'new_m_lru_lambda': 'new_m', 'new_m_rg_norm_g': 'new_m', 'new_m_hg_lower_bound': 'new_m', 'new_m_hg_norm_g': 'new_m', 'new_m_w_out': 'new_m', 'new_m_ffn_norm_g': 'new_m', 'new_m_w_gate_up': 'new_m', 'new_m_w_down': 'new_m', 'new_m_final_norm_g': 'new_m', 'new_v_meta_tokens': 'new_v', 'new_v_mix_norm_g': 'new_v', 'new_v_w_in': 'new_v', 'new_v_conv_w': 'new_v', 'new_v_conv_b': 'new_v', 'new_v_w_rgate': 'new_v', 'new_v_b_rgate': 'new_v', 'new_v_w_igate': 'new_v', 'new_v_b_igate': 'new_v', 'new_v_lru_lambda': 'new_v', 'new_v_rg_norm_g': 'new_v', 'new_v_hg_lower_bound': 'new_v', 'new_v_hg_norm_g': 'new_v', 'new_v_w_out': 'new_v', 'new_v_ffn_norm_g': 'new_v', 'new_v_w_gate_up': 'new_v', 'new_v_w_down': 'new_v', 'new_v_final_norm_g': 'new_v'}


def _forward(args):
    return _fwd_reference(*[args[k] for k in FWD_PARAMS])


def _output_shape():
    out = _jax.eval_shape(lambda: _forward(_fwd_setup_inputs(0)))
    return out.shape, out.dtype

N_MICROBATCH = 1
ADAM_LR = 0.001
ADAM_B1 = 0.9
ADAM_B2 = 0.999
ADAM_EPS = 1e-08
ADAM_WD = 0.01
ADAM_STEP = 10
PER_EXAMPLE_BATCH_AXIS = {'x': 0, 'loss_target': 0}
SHARED_INPUTS = []
_WEIGHT_DTYPES = {'meta_tokens': _jnp.float32, 'mix_norm_g': _jnp.float32, 'w_in': _jnp.float32, 'conv_w': _jnp.float32, 'conv_b': _jnp.float32, 'w_rgate': _jnp.float32, 'b_rgate': _jnp.float32, 'w_igate': _jnp.float32, 'b_igate': _jnp.float32, 'lru_lambda': _jnp.float32, 'rg_norm_g': _jnp.float32, 'hg_lower_bound': _jnp.float32, 'hg_norm_g': _jnp.float32, 'w_out': _jnp.float32, 'ffn_norm_g': _jnp.float32, 'w_gate_up': _jnp.float32, 'w_down': _jnp.float32, 'final_norm_g': _jnp.float32}
MOMENT_SCALE = {'meta_tokens': 7.026204e-03, 'mix_norm_g': 1.895134e-01, 'w_in': 1.019645e-01, 'conv_w': 1.492083e-01, 'conv_b': 1.562583e+00, 'w_rgate': 5.348674e-02, 'b_rgate': 5.029486e-02, 'w_igate': 9.642797e-02, 'b_igate': 5.089040e-02, 'lru_lambda': 9.856181e-02, 'rg_norm_g': 1.524418e-01, 'hg_lower_bound': 8.510708e-03, 'hg_norm_g': 1.837464e-01, 'w_out': 1.241077e-01, 'ffn_norm_g': 1.108729e-01, 'w_gate_up': 4.551418e-02, 'w_down': 7.445977e-02, 'final_norm_g': 3.198431e+01}


def _to_microbatches(a, axis):
    t = _jnp.moveaxis(a, axis, 0)
    t = t.reshape((N_MICROBATCH, t.shape[0] // N_MICROBATCH) + t.shape[1:])
    return _jnp.moveaxis(t, 1, axis + 1)


def setup_inputs(seed: int = 0) -> dict:
    inp = _fwd_setup_inputs(seed)
    key = _jax.random.fold_in(_jax.random.key(seed), 7919)
    shape, _ = _output_shape()
    out = dict(inp)
    out["loss_target"] = _jax.random.normal(_jax.random.fold_in(key, 0), shape, _jnp.float32)
    for i, name in enumerate(TWIN_WEIGHTS):
        w = inp[name].astype(_jnp.float32)
        if MOMENT_SCALE is None:
            s = _jnp.sqrt(_jnp.mean(_jnp.square(w)) + 1e-30)
        else:
            s = MOMENT_SCALE[name]
        km, kv = _jax.random.split(_jax.random.fold_in(key, i + 1))
        out[name] = w
        out["m_" + name] = s * _jax.random.normal(km, w.shape, _jnp.float32)
        out["v_" + name] = (s * s) * _jax.random.uniform(kv, w.shape, _jnp.float32, 0.5, 1.5)
    if N_MICROBATCH > 1:
        for name, axis in PER_EXAMPLE_BATCH_AXIS.items():
            out[name] = _to_microbatches(out[name], axis)
    return {'x': out['x'], 'meta_tokens': out['meta_tokens'], 'mix_norm_g': out['mix_norm_g'], 'w_in': out['w_in'], 'conv_w': out['conv_w'], 'conv_b': out['conv_b'], 'w_rgate': out['w_rgate'], 'b_rgate': out['b_rgate'], 'w_igate': out['w_igate'], 'b_igate': out['b_igate'], 'lru_lambda': out['lru_lambda'], 'rg_norm_g': out['rg_norm_g'], 'hg_lower_bound': out['hg_lower_bound'], 'hg_norm_g': out['hg_norm_g'], 'w_out': out['w_out'], 'ffn_norm_g': out['ffn_norm_g'], 'w_gate_up': out['w_gate_up'], 'w_down': out['w_down'], 'final_norm_g': out['final_norm_g'], 'loss_target': out['loss_target'], 'm_meta_tokens': out['m_meta_tokens'], 'm_mix_norm_g': out['m_mix_norm_g'], 'm_w_in': out['m_w_in'], 'm_conv_w': out['m_conv_w'], 'm_conv_b': out['m_conv_b'], 'm_w_rgate': out['m_w_rgate'], 'm_b_rgate': out['m_b_rgate'], 'm_w_igate': out['m_w_igate'], 'm_b_igate': out['m_b_igate'], 'm_lru_lambda': out['m_lru_lambda'], 'm_rg_norm_g': out['m_rg_norm_g'], 'm_hg_lower_bound': out['m_hg_lower_bound'], 'm_hg_norm_g': out['m_hg_norm_g'], 'm_w_out': out['m_w_out'], 'm_ffn_norm_g': out['m_ffn_norm_g'], 'm_w_gate_up': out['m_w_gate_up'], 'm_w_down': out['m_w_down'], 'm_final_norm_g': out['m_final_norm_g'], 'v_meta_tokens': out['v_meta_tokens'], 'v_mix_norm_g': out['v_mix_norm_g'], 'v_w_in': out['v_w_in'], 'v_conv_w': out['v_conv_w'], 'v_conv_b': out['v_conv_b'], 'v_w_rgate': out['v_w_rgate'], 'v_b_rgate': out['v_b_rgate'], 'v_w_igate': out['v_w_igate'], 'v_b_igate': out['v_b_igate'], 'v_lru_lambda': out['v_lru_lambda'], 'v_rg_norm_g': out['v_rg_norm_g'], 'v_hg_lower_bound': out['v_hg_lower_bound'], 'v_hg_norm_g': out['v_hg_norm_g'], 'v_w_out': out['v_w_out'], 'v_ffn_norm_g': out['v_ffn_norm_g'], 'v_w_gate_up': out['v_w_gate_up'], 'v_w_down': out['v_w_down'], 'v_final_norm_g': out['v_final_norm_g']}


def _loss(weights, diff, rest, loss_target):
    with _jax.named_scope("forward"):
        args = {**rest, TWIN_DIFF_INPUT: diff, **{k: w.astype(_WEIGHT_DTYPES[k]) for k, w in weights.items()}}
        y = _forward(args)
    with _jax.named_scope("loss_head"):
        err = _jnp.square(y.astype(_jnp.float32) - loss_target)
        return 0.5 * _jnp.sum(_jnp.mean(err, axis=-1)) if err.ndim else 0.5 * err


def _adamw(w, g, m, v):
    m = ADAM_B1 * m + (1.0 - ADAM_B1) * g
    v = ADAM_B2 * v + (1.0 - ADAM_B2) * _jnp.square(g)
    m_hat = m / (1.0 - ADAM_B1 ** ADAM_STEP)
    v_hat = v / (1.0 - ADAM_B2 ** ADAM_STEP)
    delta = -ADAM_LR * (m_hat / (_jnp.sqrt(v_hat) + ADAM_EPS) + ADAM_WD * w)
    return delta, m, v


def reference(x, meta_tokens, mix_norm_g, w_in, conv_w, conv_b, w_rgate, b_rgate, w_igate, b_igate, lru_lambda, rg_norm_g, hg_lower_bound, hg_norm_g, w_out, ffn_norm_g, w_gate_up, w_down, final_norm_g, loss_target, m_meta_tokens, m_mix_norm_g, m_w_in, m_conv_w, m_conv_b, m_w_rgate, m_b_rgate, m_w_igate, m_b_igate, m_lru_lambda, m_rg_norm_g, m_hg_lower_bound, m_hg_norm_g, m_w_out, m_ffn_norm_g, m_w_gate_up, m_w_down, m_final_norm_g, v_meta_tokens, v_mix_norm_g, v_w_in, v_conv_w, v_conv_b, v_w_rgate, v_b_rgate, v_w_igate, v_b_igate, v_lru_lambda, v_rg_norm_g, v_hg_lower_bound, v_hg_norm_g, v_w_out, v_ffn_norm_g, v_w_gate_up, v_w_down, v_final_norm_g):
    given = dict(x=x, meta_tokens=meta_tokens, mix_norm_g=mix_norm_g, w_in=w_in, conv_w=conv_w, conv_b=conv_b, w_rgate=w_rgate, b_rgate=b_rgate, w_igate=w_igate, b_igate=b_igate, lru_lambda=lru_lambda, rg_norm_g=rg_norm_g, hg_lower_bound=hg_lower_bound, hg_norm_g=hg_norm_g, w_out=w_out, ffn_norm_g=ffn_norm_g, w_gate_up=w_gate_up, w_down=w_down, final_norm_g=final_norm_g, loss_target=loss_target, m_meta_tokens=m_meta_tokens, m_mix_norm_g=m_mix_norm_g, m_w_in=m_w_in, m_conv_w=m_conv_w, m_conv_b=m_conv_b, m_w_rgate=m_w_rgate, m_b_rgate=m_b_rgate, m_w_igate=m_w_igate, m_b_igate=m_b_igate, m_lru_lambda=m_lru_lambda, m_rg_norm_g=m_rg_norm_g, m_hg_lower_bound=m_hg_lower_bound, m_hg_norm_g=m_hg_norm_g, m_w_out=m_w_out, m_ffn_norm_g=m_ffn_norm_g, m_w_gate_up=m_w_gate_up, m_w_down=m_w_down, m_final_norm_g=m_final_norm_g, v_meta_tokens=v_meta_tokens, v_mix_norm_g=v_mix_norm_g, v_w_in=v_w_in, v_conv_w=v_conv_w, v_conv_b=v_conv_b, v_w_rgate=v_w_rgate, v_b_rgate=v_b_rgate, v_w_igate=v_w_igate, v_b_igate=v_b_igate, v_lru_lambda=v_lru_lambda, v_rg_norm_g=v_rg_norm_g, v_hg_lower_bound=v_hg_lower_bound, v_hg_norm_g=v_hg_norm_g, v_w_out=v_w_out, v_ffn_norm_g=v_ffn_norm_g, v_w_gate_up=v_w_gate_up, v_w_down=v_w_down, v_final_norm_g=v_final_norm_g)
    weights = {n: given[n] for n in TWIN_WEIGHTS}
    shared = {n: given[n] for n in SHARED_INPUTS}
    per_example = {n: given[n] for n in ['x']}
    grad_fn = _jax.value_and_grad(_loss, argnums=(0, 1))

    def one_microbatch(ex, loss_target):
        ex = dict(ex)
        diff = ex.pop(TWIN_DIFF_INPUT)
        return grad_fn(weights, diff, {**shared, **ex}, loss_target)

    if N_MICROBATCH == 1:
        loss, (grad_w, grad_x) = one_microbatch(per_example, given["loss_target"])
    else:
        def body(carry, xs):
            loss_sum, grad_sum = carry
            l_k, (gw_k, gx_k) = one_microbatch(xs[0], xs[1])
            with _jax.named_scope("update"):
                return (loss_sum + l_k, _jax.tree.map(_jnp.add, grad_sum, gw_k)), gx_k

        init = (_jnp.zeros((), _jnp.float32), _jax.tree.map(_jnp.zeros_like, weights))
        (loss, grad_w), grad_x = _jax.lax.scan(body, init, (per_example, given["loss_target"]))
    with _jax.named_scope("update"):
        delta_w, new_m, new_v = {}, {}, {}
        for n in TWIN_WEIGHTS:
            delta_w[n], new_m[n], new_v[n] = _adamw(weights[n], grad_w[n], given["m_" + n], given["v_" + n])
    return (loss, grad_x, *[grad_w[n] for n in TWIN_WEIGHTS], *[delta_w[n] for n in TWIN_WEIGHTS],
            *[new_m[n] for n in TWIN_WEIGHTS], *[new_v[n] for n in TWIN_WEIGHTS])
```

```python
import functools
import math

import jax
import jax.numpy as jnp
from jax import lax
from jax.experimental import pallas as pl
from jax.experimental.pallas import tpu as pltpu

F32 = jnp.float32
BF16 = jnp.bfloat16
HIGHEST = lax.Precision.HIGHEST
MESH = pl.DeviceIdType.MESH

D_MODEL = 1024
D_RG = 512
RG_HEAD_DIM = 64
D_HG = 512
HG_HEAD_DIM = 128
HG_HEADS = 4
CHUNK = 64
SUB = 16
N_SUB = CHUNK // SUB
N_META = 16
PAD = CHUNK - N_META
D_IN = 3072
D_FF = 2816
CONV_W = 4
LRU_C = 8.0
EPS = 1e-6
EXP_CLAMP = 80.0
GELU_C = math.sqrt(2.0 / math.pi)
GELU_A = 0.044715
N_CHIPS = 4

ADAM_LR = 0.001
ADAM_B1 = 0.9
ADAM_B2 = 0.999
ADAM_EPS = 1e-08
ADAM_WD = 0.01
ADAM_STEP = 10

VMEM_LIMIT = 56 * 1024 * 1024


def _params(*sem):
    return pltpu.CompilerParams(dimension_semantics=sem, vmem_limit_bytes=VMEM_LIMIT)


def _row_tile(rows, target):
    best = None
    for t in range(16, min(rows, target) + 1, 16):
        if rows % t == 0:
            best = t
    assert best is not None, rows
    return best


def _sigmoid(x):
    return 1.0 / (1.0 + jnp.exp(-x))


def _dot(a, b):
    return jnp.dot(a, b, preferred_element_type=F32)


def _dot_nt(a, b):
    return lax.dot_general(a, b, (((1,), (1,)), ((), ())), preferred_element_type=F32)


def _dot_tn(a, b):
    return lax.dot_general(a, b, (((0,), (0,)), ((), ())), preferred_element_type=F32)


def _rms(x):
    return lax.rsqrt(jnp.mean(x * x, axis=-1, keepdims=True) + EPS)


def _rms_bwd(dn, n, r):
    return r * (dn - n * jnp.mean(dn * n, axis=-1, keepdims=True))


def _gelu_parts(x):
    t = jnp.tanh(GELU_C * (x + GELU_A * x * x * x))
    g = 0.5 * x * (1.0 + t)
    dg = 0.5 * (1.0 + t) + 0.5 * x * (1.0 - t * t) * GELU_C * (1.0 + 3.0 * GELU_A * x * x)
    return g, dg


def _softplus_neg(lam):
    e = jnp.exp(-jnp.abs(lam))
    w = 1.0 + e
    log1p = jnp.where(w == 1.0, e, jnp.log(w) * e / (w - 1.0))
    return jnp.maximum(-lam, 0.0) + log1p


def _head_mask():
    r = lax.broadcasted_iota(jnp.int32, (D_RG, D_RG), 0) // RG_HEAD_DIM
    c = lax.broadcasted_iota(jnp.int32, (D_RG, D_RG), 1) // RG_HEAD_DIM
    return r == c


def _head_fold():
    r = lax.broadcasted_iota(jnp.int32, (D_RG, RG_HEAD_DIM), 0) % RG_HEAD_DIM
    c = lax.broadcasted_iota(jnp.int32, (D_RG, RG_HEAD_DIM), 1)
    return (r == c).astype(F32)


def _gate_weights(w_r, w_i):
    def body(wr_ref, wi_ref, o_ref):
        fold = _head_fold()
        mask = _head_mask()
        for k, ref in enumerate((wr_ref, wi_ref)):
            full = lax.dot_general(ref[...], fold, (((1,), (1,)), ((), ())),
                                   precision=HIGHEST, preferred_element_type=F32)
            o_ref[:, k * D_RG:(k + 1) * D_RG] = jnp.where(mask, full, 0.0).astype(BF16)

    return pl.pallas_call(
        body, out_shape=jax.ShapeDtypeStruct((D_RG, 2 * D_RG), BF16), name="gate_weights",
    )(w_r, w_i)


def _in_proj(h0, g1, w_in):
    T = h0.shape[0]
    tm = _row_tile(T, 416)

    def body(h_ref, g_ref, w_ref, p_ref, u_ref):
        h = h_ref[...]
        u = (h * _rms(h) * g_ref[...]).astype(BF16)
        u_ref[...] = u
        p_ref[...] = _dot(u, w_ref[...])

    return pl.pallas_call(
        body, grid=(T // tm,),
        in_specs=[pl.BlockSpec((tm, D_MODEL), lambda i: (i, 0)),
                  pl.BlockSpec((1, D_MODEL), lambda i: (0, 0)),
                  pl.BlockSpec((D_MODEL, D_IN), lambda i: (0, 0))],
        out_specs=[pl.BlockSpec((tm, D_IN), lambda i: (i, 0)),
                   pl.BlockSpec((tm, D_MODEL), lambda i: (i, 0))],
        out_shape=[jax.ShapeDtypeStruct((T, D_IN), F32), jax.ShapeDtypeStruct((T, D_MODEL), BF16)],
        name="in_proj", compiler_params=_params("parallel"),
    )(h0, g1, w_in)


def _scan_block_fwd(A, B, rowi):
    for d in (1, 2, 4):
        a_sh = pltpu.roll(A, d, axis=0)
        b_sh = pltpu.roll(B, d, axis=0)
        m = rowi >= d
        B = jnp.where(m, A * b_sh + B, B)
        A = jnp.where(m, A * a_sh, A)
    return A, B


def _scan_block_bwd(A, B, rowi):
    for d in (1, 2, 4):
        a_sh = pltpu.roll(A, 8 - d, axis=0)
        b_sh = pltpu.roll(B, 8 - d, axis=0)
        m = rowi < 8 - d
        B = jnp.where(m, A * b_sh + B, B)
        A = jnp.where(m, A * a_sh, A)
    return A, B


def _rg_gates(xc, w_ref, bg_ref, lam):
    pre = _dot(xc.astype(BF16), w_ref[...]) + bg_ref[...]
    r = _sigmoid(pre[:, :D_RG])
    ig = _sigmoid(pre[:, D_RG:])
    sp = _softplus_neg(lam)
    la = -LRU_C * sp * r
    a = jnp.exp(la)
    th = jnp.tanh(la)
    m = jnp.sqrt(-2.0 * th / (1.0 - th))
    return r, ig, sp, a, m


def _conv(ext, cw_ref, cb_ref, tm):
    xc = cb_ref[...] + cw_ref[0:1, :] * ext[8 - 3:8 - 3 + tm, :]
    for j in range(1, CONV_W):
        xc = xc + cw_ref[j:j + 1, :] * ext[8 - 3 + j:8 - 3 + j + tm, :]
    return xc


def _rg_fwd(p, cw, cb, wg, bg, lam, rg_g):
    T = p.shape[0]
    tm = _row_tile(T, 416)

    def body(xg_ref, cw_ref, cb_ref, w_ref, bg_ref, lam_ref, g_ref, y_ref, h_ref, ext, a_s, b_s, carry):
        i = pl.program_id(0)

        @pl.when(i == 0)
        def _():
            ext[0:8, :] = jnp.zeros((8, D_RG), F32)
            carry[...] = jnp.zeros((1, D_RG), F32)

        ext[8:8 + tm, :] = xg_ref[:, :D_RG]
        xc = _conv(ext, cw_ref, cb_ref, tm)
        r, ig, sp, a, m = _rg_gates(xc, w_ref, bg_ref, lam_ref[...])
        row = i * tm + lax.broadcasted_iota(jnp.int32, (tm, 1), 0)
        a_s[...] = a
        b_s[...] = jnp.where(row >= PAD, m * ig * xc, 0.0)
        rowi = lax.broadcasted_iota(jnp.int32, (8, D_RG), 0)

        def blk(j, c):
            o = pl.multiple_of(j * 8, 8)
            A, B = _scan_block_fwd(a_s[pl.ds(o, 8), :], b_s[pl.ds(o, 8), :], rowi)
            h = B + A * c
            h_ref[pl.ds(o, 8), :] = h
            return h[7:8, :]

        carry[...] = lax.fori_loop(0, tm // 8, blk, carry[...])
        ext[0:8, :] = ext[tm:tm + 8, :]
        g, _ = _gelu_parts(xg_ref[:, D_RG:])
        yy = g * h_ref[...]
        y_ref[...] = yy * _rms(yy) * g_ref[...]

    vec = lambda n: pl.BlockSpec((1, n), lambda i: (0, 0))
    return pl.pallas_call(
        body, grid=(T // tm,),
        in_specs=[pl.BlockSpec((tm, 2 * D_RG), lambda i: (i, 0)),
                  pl.BlockSpec((CONV_W, D_RG), lambda i: (0, 0)), vec(D_RG),
                  pl.BlockSpec((D_RG, 2 * D_RG), lambda i: (0, 0)), vec(2 * D_RG), vec(D_RG), vec(D_RG)],
        out_specs=[pl.BlockSpec((tm, D_RG), lambda i: (i, 0)), pl.BlockSpec((tm, D_RG), lambda i: (i, 0))],
        out_shape=[jax.ShapeDtypeStruct((T, D_RG), F32), jax.ShapeDtypeStruct((T, D_RG), F32)],
        scratch_shapes=[pltpu.VMEM((tm + 8, D_RG), F32), pltpu.VMEM((tm, D_RG), F32),
                        pltpu.VMEM((tm, D_RG), F32), pltpu.VMEM((1, D_RG), F32)],
        name="rg_fwd", compiler_params=_params("arbitrary"),
    )(p, cw, cb, wg, bg, lam, rg_g)


def _tri(lower):
    r = lax.broadcasted_iota(jnp.int32, (CHUNK, CHUNK), 0)
    c = lax.broadcasted_iota(jnp.int32, (CHUNK, CHUNK), 1)
    return ((c <= r) if lower else (c >= r)).astype(F32)


def _hg_gates(hq, hf, lbraw_ref, valid):
    lb = _sigmoid(lbraw_ref[0:1, :] - lbraw_ref[1:2, :])
    sq = _sigmoid(hq)
    q = hq * sq
    sf = _sigmoid(hf)
    f = lb + (1.0 - lb) * sf
    lf = jnp.where(valid, jnp.log(f), 0.0)
    b = jnp.dot(_tri(True), lf, precision=HIGHEST, preferred_element_type=F32)
    return lb, sq, q, sf, f, b


def _hg_head(qh, kh, bh):
    blk = lax.broadcasted_iota(jnp.int32, (CHUNK, 1), 0) // SUB
    b_last = bh[CHUNK - 1:CHUNK, :]
    refs = [bh[SUB * s:SUB * s + 1, :] for s in range(N_SUB)]
    r_sel = refs[N_SUB - 1]
    for s in range(N_SUB - 2, -1, -1):
        r_sel = jnp.where(blk == s, refs[s], r_sel)
    eb = jnp.exp(bh)
    eq = jnp.exp(bh - r_sel)
    ekh = jnp.exp(b_last - bh)
    ek = [jnp.exp(jnp.minimum(refs[s] - bh, EXP_CLAMP)) for s in range(N_SUB)]
    qe = qh * eq
    q_hat = jnp.concatenate([jnp.where(blk == s, qe, 0.0) for s in range(N_SUB)], axis=1)
    k_til = jnp.concatenate([kh * ek[s] for s in range(N_SUB)], axis=1)
    return blk, b_last, eb, eq, ekh, ek, q_hat, k_til


def _causal():
    r = lax.broadcasted_iota(jnp.int32, (CHUNK, CHUNK), 0)
    c = lax.broadcasted_iota(jnp.int32, (CHUNK, CHUNK), 1)
    return r >= c


def _hg_fwd(p, lbraw, hg_g):
    T = p.shape[0]
    n_chunks = T // CHUNK

    def body(hq_ref, hf_ref, hi_ref, hg_ref, lb_ref, g_ref, y_ref, o_ref, st_all_ref, st):
        n = pl.program_id(0)

        @pl.when(n == 0)
        def _():
            st[...] = jnp.zeros_like(st)

        valid = (n * CHUNK + lax.broadcasted_iota(jnp.int32, (CHUNK, 1), 0)) >= PAD
        hq, hf, v, hg = hq_ref[...], hf_ref[...], hi_ref[...], hg_ref[...]
        lb, sq, q, sf, f, b = _hg_gates(hq, hf, lb_ref, valid)
        k = 1.0 - f
        st_all_ref[0] = st[...]
        causal = _causal()
        for h in range(HG_HEADS):
            sl = slice(h * HG_HEAD_DIM, (h + 1) * HG_HEAD_DIM)
            qh, kh, vh, bh = q[:, sl], k[:, sl], v[:, sl], b[:, sl]
            st_h = st[sl, :]
            _, b_last, eb, _, ekh, _, q_hat, k_til = _hg_head(qh, kh, bh)
            vb = vh.astype(BF16)
            inter = _dot_nt((qh * eb).astype(BF16), st_h.astype(BF16))
            att = jnp.where(causal, _dot_nt(q_hat.astype(BF16), k_til.astype(BF16)), 0.0)
            o = inter + _dot(att.astype(BF16), vb)
            st[sl, :] = st_h * jnp.exp(b_last) + _dot_tn(vb, (kh * ekh).astype(BF16))
            o_ref[:, sl] = o
            hgh = hg[:, sl]
            y_ref[:, sl] = o * _rms(o) * g_ref[...] * (hgh * _sigmoid(hgh))

    col = lambda j: pl.BlockSpec((CHUNK, D_HG), lambda n: (n, j))
    return pl.pallas_call(
        body, grid=(n_chunks,),
        in_specs=[col(2), col(3), col(4), col(5),
                  pl.BlockSpec((2, D_HG), lambda n: (0, 0)), pl.BlockSpec((1, HG_HEAD_DIM), lambda n: (0, 0))],
        out_specs=[pl.BlockSpec((CHUNK, D_HG), lambda n: (n, 0)), pl.BlockSpec((CHUNK, D_HG), lambda n: (n, 0)),
                   pl.BlockSpec((1, D_HG, HG_HEAD_DIM), lambda n: (n, 0, 0))],
        out_shape=[jax.ShapeDtypeStruct((T, D_HG), F32), jax.ShapeDtypeStruct((T, D_HG), F32),
                   jax.ShapeDtypeStruct((n_chunks, D_HG, HG_HEAD_DIM), F32)],
        scratch_shapes=[pltpu.VMEM((D_HG, HG_HEAD_DIM), F32)],
        name="hg_fwd", compiler_params=_params("arbitrary"),
    )(p, p, p, p, lbraw, hg_g)


def _out_proj(h0, y_rg, y_hg, w_out, g2):
    T = h0.shape[0]
    tm = _row_tile(T, 832)

    def body(h_ref, yr_ref, yh_ref, w_ref, g_ref, h1_ref, v_ref, y_ref):
        y_ref[:, :D_RG] = yr_ref[...].astype(BF16)
        y_ref[:, D_RG:] = yh_ref[...].astype(BF16)
        h1 = h_ref[...] + _dot(y_ref[...], w_ref[...])
        h1_ref[...] = h1
        v_ref[...] = (h1 * _rms(h1) * g_ref[...]).astype(BF16)

    row = lambda n: pl.BlockSpec((tm, n), lambda i: (i, 0))
    return pl.pallas_call(
        body, grid=(T // tm,),
        in_specs=[row(D_MODEL), row(D_RG), row(D_HG), pl.BlockSpec((D_MODEL, D_MODEL), lambda i: (0, 0)),
                  pl.BlockSpec((1, D_MODEL), lambda i: (0, 0))],
        out_specs=[row(D_MODEL), row(D_MODEL), row(D_MODEL)],
        out_shape=[jax.ShapeDtypeStruct((T, D_MODEL), F32), jax.ShapeDtypeStruct((T, D_MODEL), BF16),
                   jax.ShapeDtypeStruct((T, D_MODEL), BF16)],
        name="out_proj", compiler_params=_params("parallel"),
    )(h0, y_rg, y_hg, w_out, g2)


def _gate_up(v, w_gu):
    T = v.shape[0]
    tm = _row_tile(T, 416)
    tn = D_FF // 2

    def body(v_ref, wg_ref, wu_ref, g_ref, u_ref, act_ref):
        x = v_ref[...]
        g = _dot(x, wg_ref[...])
        u = _dot(x, wu_ref[...])
        g_ref[...] = g
        u_ref[...] = u
        act_ref[...] = (g * _sigmoid(g) * u).astype(BF16)

    blk = pl.BlockSpec((tm, tn), lambda j, i: (i, j))
    return pl.pallas_call(
        body, grid=(2, T // tm),
        in_specs=[pl.BlockSpec((tm, D_MODEL), lambda j, i: (i, 0)),
                  pl.BlockSpec((D_MODEL, tn), lambda j, i: (0, j)),
                  pl.BlockSpec((D_MODEL, tn), lambda j, i: (0, j + 2))],
        out_specs=[blk, blk, blk],
        out_shape=[jax.ShapeDtypeStruct((T, D_FF), F32), jax.ShapeDtypeStruct((T, D_FF), F32),
                   jax.ShapeDtypeStruct((T, D_FF), BF16)],
        name="gate_up", compiler_params=_params("parallel", "parallel"),
    )(v, w_gu, w_gu)


def _down_loss(h1, act, w_down, gf, target):
    T = h1.shape[0]
    tm = _row_tile(T, 832)

    def body(h_ref, a_ref, w_ref, g_ref, t_ref, dh2_ref, dh2b_ref, loss_ref, gg_ref):
        i = pl.program_id(0)

        @pl.when(i == 0)
        def _():
            loss_ref[...] = jnp.zeros_like(loss_ref)
            gg_ref[...] = jnp.zeros_like(gg_ref)

        h2 = h_ref[...] + _dot(a_ref[...], w_ref[...])
        r = _rms(h2)
        n = h2 * r
        gf_ = g_ref[...]
        row = i * tm + lax.broadcasted_iota(jnp.int32, (tm, 1), 0)
        err = jnp.where(row >= PAD + N_META, n * gf_ - t_ref[...], 0.0)
        loss_ref[...] += 0.5 * jnp.sum(jnp.mean(err * err, axis=-1, keepdims=True), axis=0, keepdims=True)
        dy = err * (1.0 / D_MODEL)
        gg_ref[...] += jnp.sum(dy * n, axis=0, keepdims=True)
        dh2 = _rms_bwd(dy * gf_, n, r)
        dh2_ref[...] = dh2
        dh2b_ref[...] = dh2.astype(BF16)

    row_spec = lambda n: pl.BlockSpec((tm, n), lambda i: (i, 0))
    return pl.pallas_call(
        body, grid=(T // tm,),
        in_specs=[row_spec(D_MODEL), row_spec(D_FF), pl.BlockSpec((D_FF, D_MODEL), lambda i: (0, 0)),
                  pl.BlockSpec((1, D_MODEL), lambda i: (0, 0)), row_spec(D_MODEL)],
        out_specs=[row_spec(D_MODEL), row_spec(D_MODEL), pl.BlockSpec((1, 1), lambda i: (0, 0)),
                   pl.BlockSpec((1, D_MODEL), lambda i: (0, 0))],
        out_shape=[jax.ShapeDtypeStruct((T, D_MODEL), F32), jax.ShapeDtypeStruct((T, D_MODEL), BF16),
                   jax.ShapeDtypeStruct((1, 1), F32), jax.ShapeDtypeStruct((1, D_MODEL), F32)],
        name="down_loss", compiler_params=_params("arbitrary"),
    )(h1, act, w_down, gf, target)


def _ffn_bwd_act(dh2b, gate, up, w_down):
    T = dh2b.shape[0]
    tm = _row_tile(T, 208)

    def body(d_ref, g_ref, u_ref, w_ref, dgu_ref):
        dact = _dot_nt(d_ref[...], w_ref[...])
        g = g_ref[...]
        s = _sigmoid(g)
        dgu_ref[:, :D_FF] = (dact * u_ref[...] * s * (1.0 + g * (1.0 - s))).astype(BF16)
        dgu_ref[:, D_FF:] = (dact * g * s).astype(BF16)

    row = lambda n: pl.BlockSpec((tm, n), lambda i: (i, 0))
    return pl.pallas_call(
        body, grid=(T // tm,),
        in_specs=[row(D_MODEL), row(D_FF), row(D_FF), pl.BlockSpec((D_FF, D_MODEL), lambda i: (0, 0))],
        out_specs=row(2 * D_FF),
        out_shape=jax.ShapeDtypeStruct((T, 2 * D_FF), BF16),
        name="ffn_bwd_act", compiler_params=_params("parallel"),
    )(dh2b, gate, up, w_down)


def _ffn_bwd_in(dgu, w_gu, h1, g2, dh2, w_out):
    T = h1.shape[0]
    tm = _row_tile(T, 416)

    def body(dgu_ref, wgu_ref, h_ref, g_ref, d2_ref, wo_ref, dh1_ref, dh1b_ref, dy_ref, gg_ref):
        i = pl.program_id(0)

        @pl.when(i == 0)
        def _():
            gg_ref[...] = jnp.zeros_like(gg_ref)

        dv = _dot_nt(dgu_ref[...], wgu_ref[...])
        h1 = h_ref[...]
        r = _rms(h1)
        n = h1 * r
        gg_ref[...] += jnp.sum(dv * n, axis=0, keepdims=True)
        dh1 = d2_ref[...] + _rms_bwd(dv * g_ref[...], n, r)
        dh1_ref[...] = dh1
        db = dh1.astype(BF16)
        dh1b_ref[...] = db
        dy_ref[...] = _dot_nt(db, wo_ref[...])

    row = lambda n: pl.BlockSpec((tm, n), lambda i: (i, 0))
    return pl.pallas_call(
        body, grid=(T // tm,),
        in_specs=[row(2 * D_FF), pl.BlockSpec((D_MODEL, 2 * D_FF), lambda i: (0, 0)),
                  row(D_MODEL), pl.BlockSpec((1, D_MODEL), lambda i: (0, 0)), row(D_MODEL),
                  pl.BlockSpec((D_MODEL, D_MODEL), lambda i: (0, 0))],
        out_specs=[row(D_MODEL), row(D_MODEL), row(D_MODEL), pl.BlockSpec((1, D_MODEL), lambda i: (0, 0))],
        out_shape=[jax.ShapeDtypeStruct((T, D_MODEL), F32), jax.ShapeDtypeStruct((T, D_MODEL), BF16),
                   jax.ShapeDtypeStruct((T, D_MODEL), F32), jax.ShapeDtypeStruct((1, D_MODEL), F32)],
        name="ffn_bwd_in", compiler_params=_params("arbitrary"),
    )(dgu, w_gu, h1, g2, dh2, w_out)


def _rg_bwd(p, hs, dy, cw, cb, wg, bg, lam, rg_g):
    T = p.shape[0]
    tm = _row_tile(T, 208)
    nt = T // tm
    hb = tm // 8

    def body(xg_ref, xh_ref, h_ref, hh_ref, dy_ref, cw_ref, cb_ref, w_ref, bg_ref, lam_ref, g_ref,
             dp_ref, gcw_ref, gcb_ref, gw_ref, gbg_ref, glam_ref, gg_ref,
             ext, dext, a_s, b_s, d_s, gacc, carry_d, carry_a):
        i = pl.program_id(0)
        t_idx = nt - 1 - i

        @pl.when(i == 0)
        def _():
            dext[tm:tm + 8, :] = jnp.zeros((8, D_RG), F32)
            carry_d[...] = jnp.zeros_like(carry_d)
            carry_a[...] = jnp.zeros_like(carry_a)
            gacc[...] = jnp.zeros_like(gacc)
            for ref in (gcw_ref, gcb_ref, gbg_ref, glam_ref, gg_ref, gw_ref):
                ref[...] = jnp.zeros_like(ref)

        first = t_idx == 0
        ext[0:8, :] = jnp.where(first, 0.0, xh_ref[:, :D_RG])
        ext[8:8 + tm, :] = xg_ref[:, :D_RG]
        xc = _conv(ext, cw_ref, cb_ref, tm)
        lam_ = lam_ref[...]
        r, ig, sp, a, m = _rg_gates(xc, w_ref, bg_ref, lam_)
        row = t_idx * tm + lax.broadcasted_iota(jnp.int32, (tm, 1), 0)
        valid = row >= PAD

        gr = xg_ref[:, D_RG:]
        g, dgelu = _gelu_parts(gr)
        h = h_ref[...]
        yy = g * h
        rr = _rms(yy)
        nn = yy * rr
        dy_ = dy_ref[...]
        gg_ref[...] += jnp.sum(dy_ * nn, axis=0, keepdims=True)
        dyy = _rms_bwd(dy_ * g_ref[...], nn, rr)
        dp_ref[:, D_RG:] = dyy * h * dgelu

        a_s[...] = a
        b_s[...] = dyy * g
        rowi = lax.broadcasted_iota(jnp.int32, (8, D_RG), 0)

        def blk(jj, c):
            cd, ca = c
            o = pl.multiple_of((hb - 1 - jj) * 8, 8)
            a_blk = a_s[pl.ds(o, 8), :]
            a_next = jnp.where(rowi == 7, ca, pltpu.roll(a_blk, 7, axis=0))
            A, B = _scan_block_bwd(a_next, b_s[pl.ds(o, 8), :], rowi)
            d = B + A * cd
            d_s[pl.ds(o, 8), :] = d
            return d[0:1, :], a_blk[0:1, :]

        cd, ca = lax.fori_loop(0, hb, blk, (carry_d[...], carry_a[...]))
        carry_d[...] = cd
        carry_a[...] = ca
        delta = d_s[...]

        h_last_prev = jnp.where(first, 0.0, hh_ref[7:8, :])
        row0 = lax.broadcasted_iota(jnp.int32, (tm, 1), 0) == 0
        h_prev = jnp.where(row0, h_last_prev, pltpu.roll(h, 1, axis=0))
        dbx = jnp.where(valid, delta, 0.0)
        da = delta * h_prev
        di = dbx * m * xc
        dm = dbx * ig * xc
        dla = a * (da - dm * a / m)
        dla = jnp.where(valid, dla, 0.0)
        glam_ref[...] += jnp.sum(dla * r, axis=0, keepdims=True) * (LRU_C * _sigmoid(-lam_))
        dr = (-LRU_C) * sp * dla
        dpre = jnp.concatenate([dr * r * (1.0 - r), di * ig * (1.0 - ig)], axis=1)
        gbg_ref[...] += jnp.sum(dpre, axis=0, keepdims=True)
        dpre_b = dpre.astype(BF16)
        gacc[...] += _dot_tn(xc.astype(BF16), dpre_b)
        dxc = dbx * m * ig + _dot_nt(dpre_b, w_ref[...])
        gcb_ref[...] += jnp.sum(dxc, axis=0, keepdims=True)
        for j in range(CONV_W):
            gcw_ref[j:j + 1, :] += jnp.sum(dxc * ext[8 - 3 + j:8 - 3 + j + tm, :], axis=0, keepdims=True)
        dext[0:tm, :] = dxc
        dxr = cw_ref[0:1, :] * dext[3:3 + tm, :]
        for j in range(1, CONV_W):
            dxr = dxr + cw_ref[j:j + 1, :] * dext[3 - j:3 - j + tm, :]
        dp_ref[:, :D_RG] = dxr
        dext[tm:tm + 8, :] = dext[0:8, :]

        @pl.when(i == nt - 1)
        def _():
            fold = _head_fold()
            mask = _head_mask()
            for k in range(2):
                blockdiag = jnp.where(mask, gacc[:, k * D_RG:(k + 1) * D_RG], 0.0)
                gw_ref[k * D_RG:(k + 1) * D_RG, :] = jnp.dot(blockdiag, fold, precision=HIGHEST,
                                                             preferred_element_type=F32)

    vec = lambda n: pl.BlockSpec((1, n), lambda i: (0, 0))
    rev = lambda n: pl.BlockSpec((tm, n), lambda i: (nt - 1 - i, 0))
    halo = lambda n: pl.BlockSpec((8, n), lambda i: (jnp.maximum((nt - 1 - i) * hb - 1, 0), 0))
    return pl.pallas_call(
        body, grid=(nt,),
        in_specs=[rev(2 * D_RG), halo(2 * D_RG), rev(D_RG), halo(D_RG), rev(D_RG),
                  pl.BlockSpec((CONV_W, D_RG), lambda i: (0, 0)), vec(D_RG),
                  pl.BlockSpec((D_RG, 2 * D_RG), lambda i: (0, 0)), vec(2 * D_RG), vec(D_RG), vec(D_RG)],
        out_specs=[rev(2 * D_RG), pl.BlockSpec((CONV_W, D_RG), lambda i: (0, 0)), vec(D_RG),
                   pl.BlockSpec((2 * D_RG, RG_HEAD_DIM), lambda i: (0, 0)), vec(2 * D_RG), vec(D_RG), vec(D_RG)],
        out_shape=[jax.ShapeDtypeStruct((T, 2 * D_RG), F32), jax.ShapeDtypeStruct((CONV_W, D_RG), F32),
                   jax.ShapeDtypeStruct((1, D_RG), F32), jax.ShapeDtypeStruct((2 * D_RG, RG_HEAD_DIM), F32),
                   jax.ShapeDtypeStruct((1, 2 * D_RG), F32), jax.ShapeDtypeStruct((1, D_RG), F32),
                   jax.ShapeDtypeStruct((1, D_RG), F32)],
        scratch_shapes=[pltpu.VMEM((tm + 8, D_RG), F32), pltpu.VMEM((tm + 8, D_RG), F32),
                        pltpu.VMEM((tm, D_RG), F32), pltpu.VMEM((tm, D_RG), F32), pltpu.VMEM((tm, D_RG), F32),
                        pltpu.VMEM((D_RG, 2 * D_RG), F32), pltpu.VMEM((1, D_RG), F32), pltpu.VMEM((1, D_RG), F32)],
        name="rg_bwd", compiler_params=_params("arbitrary"),
    )(p, p, hs, hs, dy, cw, cb, wg, bg, lam, rg_g)


def _hg_bwd(p, o_all, st_all, dy, lbraw, hg_g):
    T = p.shape[0]
    n_chunks = T // CHUNK

    def body(hq_ref, hf_ref, hi_ref, hg_ref, o_ref, st_ref, dy_ref, lb_ref, g_ref,
             dp_ref, glb_ref, gg_ref, dst):
        i = pl.program_id(0)
        n = n_chunks - 1 - i

        @pl.when(i == 0)
        def _():
            dst[...] = jnp.zeros_like(dst)
            glb_ref[...] = jnp.zeros_like(glb_ref)
            gg_ref[...] = jnp.zeros_like(gg_ref)

        valid = (n * CHUNK + lax.broadcasted_iota(jnp.int32, (CHUNK, 1), 0)) >= PAD
        hq, hf, v, hg = hq_ref[...], hf_ref[...], hi_ref[...], hg_ref[...]
        lb, sq, q, sf, f, b = _hg_gates(hq, hf, lb_ref, valid)
        k = 1.0 - f
        causal = _causal()
        is_last = lax.broadcasted_iota(jnp.int32, (CHUNK, 1), 0) == CHUNK - 1
        g_ = g_ref[...]
        db_parts, dq_parts, dk_parts = [], [], []
        gg = jnp.zeros((1, HG_HEAD_DIM), F32)
        for h in range(HG_HEADS):
            sl = slice(h * HG_HEAD_DIM, (h + 1) * HG_HEAD_DIM)
            qh, kh, vh, bh = q[:, sl], k[:, sl], v[:, sl], b[:, sl]
            o = o_ref[:, sl]
            ro = _rms(o)
            no = o * ro
            hgh = hg[:, sl]
            sg = _sigmoid(hgh)
            dyh = dy_ref[:, sl]
            dp_ref[:, 3 * D_HG + h * HG_HEAD_DIM:3 * D_HG + (h + 1) * HG_HEAD_DIM] = (
                dyh * no * g_ * sg * (1.0 + hgh * (1.0 - sg)))
            dng = dyh * hgh * sg
            gg = gg + jnp.sum(dng * no, axis=0, keepdims=True)
            do = _rms_bwd(dng * g_, no, ro)
            dob = do.astype(BF16)

            st_h = st_ref[0, sl, :]
            dst_h = dst[sl, :]
            blk, b_last, eb, eq, ekh, ek, q_hat, k_til = _hg_head(qh, kh, bh)
            q_til = qh * eb
            k_hat = kh * ekh
            vb = vh.astype(BF16)
            dstb = dst_h.astype(BF16)
            qhb, ktb = q_hat.astype(BF16), k_til.astype(BF16)
            att = jnp.where(causal, _dot_nt(qhb, ktb), 0.0)
            datt = jnp.where(causal, _dot_nt(dob, vb), 0.0).astype(BF16)

            dk_hat = _dot(vb, dstb)
            dv = _dot_nt(k_hat.astype(BF16), dstb) + _dot_tn(att.astype(BF16), dob)
            dq_til = _dot(dob, st_h.astype(BF16))
            e_last = jnp.exp(b_last)
            db_last = (jnp.sum(dk_hat * k_hat, axis=0, keepdims=True)
                       + e_last * jnp.sum(dst_h * st_h, axis=0, keepdims=True))
            dst[sl, :] = dst_h * e_last + _dot_tn(dob, q_til.astype(BF16))

            dq_hat = _dot(datt, ktb)
            dk_til = _dot_tn(datt, qhb)
            dq_sel = dq_hat[:, (N_SUB - 1) * HG_HEAD_DIM:]
            for s in range(N_SUB - 2, -1, -1):
                dq_sel = jnp.where(blk == s, dq_hat[:, s * HG_HEAD_DIM:(s + 1) * HG_HEAD_DIM], dq_sel)
            dq_a = dq_sel * eq
            dk_a = dk_til[:, :HG_HEAD_DIM] * ek[0]
            for s in range(1, N_SUB):
                dk_a = dk_a + dk_til[:, s * HG_HEAD_DIM:(s + 1) * HG_HEAD_DIM] * ek[s]
            db_att = qhb.astype(F32) * dq_hat - ktb.astype(F32) * dk_til
            db = dq_til * q_til - dk_hat * k_hat
            for s in range(N_SUB):
                db = db + db_att[:, s * HG_HEAD_DIM:(s + 1) * HG_HEAD_DIM]
            db_parts.append(jnp.where(is_last, db + db_last, db))
            dq_parts.append(dq_til * eb + dq_a)
            dk_parts.append(dk_hat * ekh + dk_a)
            dp_ref[:, 2 * D_HG + h * HG_HEAD_DIM:2 * D_HG + (h + 1) * HG_HEAD_DIM] = dv

        gg_ref[...] += gg
        db = jnp.concatenate(db_parts, axis=1)
        dq = jnp.concatenate(dq_parts, axis=1)
        dk = jnp.concatenate(dk_parts, axis=1)
        dlf = jnp.where(valid, jnp.dot(_tri(False), db, precision=HIGHEST, preferred_element_type=F32), 0.0)
        dp_ref[:, :D_HG] = dq * sq * (1.0 + hq * (1.0 - sq))
        df = dlf / f - dk
        dlb = jnp.sum(df * (1.0 - sf), axis=0, keepdims=True) * lb * (1.0 - lb)
        glb_ref[0:1, :] += dlb
        glb_ref[1:2, :] += -dlb
        dp_ref[:, D_HG:2 * D_HG] = df * (1.0 - lb) * sf * (1.0 - sf)

    rev = lambda j: pl.BlockSpec((CHUNK, D_HG), lambda i: (n_chunks - 1 - i, j))
    return pl.pallas_call(
        body, grid=(n_chunks,),
        in_specs=[rev(2), rev(3), rev(4), rev(5), rev(0),
                  pl.BlockSpec((1, D_HG, HG_HEAD_DIM), lambda i: (n_chunks - 1 - i, 0, 0)), rev(1),
                  pl.BlockSpec((2, D_HG), lambda i: (0, 0)), pl.BlockSpec((1, HG_HEAD_DIM), lambda i: (0, 0))],
        out_specs=[pl.BlockSpec((CHUNK, 4 * D_HG), lambda i: (n_chunks - 1 - i, 0)),
                   pl.BlockSpec((2, D_HG), lambda i: (0, 0)), pl.BlockSpec((1, HG_HEAD_DIM), lambda i: (0, 0))],
        out_shape=[jax.ShapeDtypeStruct((T, 4 * D_HG), F32), jax.ShapeDtypeStruct((2, D_HG), F32),
                   jax.ShapeDtypeStruct((1, HG_HEAD_DIM), F32)],
        scratch_shapes=[pltpu.VMEM((D_HG, HG_HEAD_DIM), F32)],
        name="hg_bwd", compiler_params=_params("arbitrary"),
    )(p, p, p, p, o_all, st_all, dy, lbraw, hg_g)


def _in_bwd(dp_rg, dp_hg, w_in, h0, g1, dh1):
    T = h0.shape[0]
    tm = _row_tile(T, 416)

    def body(dr_ref, dh_ref, w_ref, h_ref, g_ref, d1_ref, dh0_ref, dpb_ref, gg_ref):
        i = pl.program_id(0)

        @pl.when(i == 0)
        def _():
            gg_ref[...] = jnp.zeros_like(gg_ref)

        dpb_ref[:, :2 * D_RG] = dr_ref[...].astype(BF16)
        dpb_ref[:, 2 * D_RG:] = dh_ref[...].astype(BF16)
        du = _dot_nt(dpb_ref[...], w_ref[...])
        h0_ = h_ref[...]
        r = _rms(h0_)
        n = h0_ * r
        gg_ref[...] += jnp.sum(du * n, axis=0, keepdims=True)
        dh0_ref[...] = d1_ref[...] + _rms_bwd(du * g_ref[...], n, r)

    row = lambda n: pl.BlockSpec((tm, n), lambda i: (i, 0))
    return pl.pallas_call(
        body, grid=(T // tm,),
        in_specs=[row(2 * D_RG), row(4 * D_HG), pl.BlockSpec((D_MODEL, D_IN), lambda i: (0, 0)),
                  row(D_MODEL), pl.BlockSpec((1, D_MODEL), lambda i: (0, 0)), row(D_MODEL)],
        out_specs=[row(D_MODEL), row(D_IN), pl.BlockSpec((1, D_MODEL), lambda i: (0, 0))],
        out_shape=[jax.ShapeDtypeStruct((T, D_MODEL), F32), jax.ShapeDtypeStruct((T, D_IN), BF16),
                   jax.ShapeDtypeStruct((1, D_MODEL), F32)],
        name="in_bwd", compiler_params=_params("arbitrary"),
    )(dp_rg, dp_hg, w_in, h0, g1, dh1)


def _col_tile(cols, target):
    best = None
    for t in range(128, min(cols, target) + 1, 128):
        if cols % t == 0:
            best = t
    assert best is not None, cols
    return best


def _weight_grad(a, b, name):
    T, M = a.shape
    N = b.shape[1]
    tk = _row_tile(T, 832)
    tm = _col_tile(M, 512)
    tn = _col_tile(N, 1536)

    def body(a_ref, b_ref, o_ref):
        @pl.when(pl.program_id(2) == 0)
        def _():
            o_ref[...] = jnp.zeros_like(o_ref)

        o_ref[...] += _dot_tn(a_ref[...], b_ref[...])

    return pl.pallas_call(
        body, grid=(M // tm, N // tn, T // tk),
        in_specs=[pl.BlockSpec((tk, tm), lambda m, n, k: (k, m)), pl.BlockSpec((tk, tn), lambda m, n, k: (k, n))],
        out_specs=pl.BlockSpec((tm, tn), lambda m, n, k: (m, n)),
        out_shape=jax.ShapeDtypeStruct((M, N), F32),
        name=name, compiler_params=_params("parallel", "parallel", "arbitrary"),
    )(a, b)


def _local_step(h0, target, w_in, w_out, w_gu, w_down, small):
    wg = _gate_weights(small["w_rgate"], small["w_igate"])
    bg = jnp.concatenate([small["b_rgate"], small["b_igate"]], axis=1)

    p, u = _in_proj(h0, small["mix_norm_g"], w_in)
    y_rg, hs = _rg_fwd(p, small["conv_w"], small["conv_b"], wg, bg, small["lru_lambda"], small["rg_norm_g"])
    y_hg, o_all, st_all = _hg_fwd(p, small["hg_lower_bound"], small["hg_norm_g"])
    h1, v, yb = _out_proj(h0, y_rg, y_hg, w_out, small["ffn_norm_g"])
    gate, up, act = _gate_up(v, w_gu)
    dh2, dh2b, loss, g_final = _down_loss(h1, act, w_down, small["final_norm_g"], target)

    dgu = _ffn_bwd_act(dh2b, gate, up, w_down)
    dh1, dh1b, dy, g_ffn = _ffn_bwd_in(dgu, w_gu, h1, small["ffn_norm_g"], dh2, w_out)
    dp_rg, g_cw, g_cb, g_wgate, g_bg, g_lam, g_rgn = _rg_bwd(
        p, hs, dy, small["conv_w"], small["conv_b"], wg, bg, small["lru_lambda"], small["rg_norm_g"])
    dp_hg, g_lb, g_hgn = _hg_bwd(p, o_all, st_all, dy, small["hg_lower_bound"], small["hg_norm_g"])
    dh0, dpb, g_mix = _in_bwd(dp_rg, dp_hg, w_in, h0, small["mix_norm_g"], dh1)

    grads = {
        "w_in": _weight_grad(u, dpb, "grad_w_in"),
        "w_out": _weight_grad(yb, dh1b, "grad_w_out"),
        "w_gate_up": _weight_grad(v, dgu, "grad_w_gate_up"),
        "w_down": _weight_grad(act, dh2b, "grad_w_down"),
        "mix_norm_g": g_mix, "conv_w": g_cw, "conv_b": g_cb, "w_gates": g_wgate,
        "b_rgate": g_bg[:, :D_RG], "b_igate": g_bg[:, D_RG:], "lru_lambda": g_lam, "rg_norm_g": g_rgn,
        "hg_lower_bound": g_lb, "hg_norm_g": g_hgn, "ffn_norm_g": g_ffn, "final_norm_g": g_final,
    }
    return loss, dh0, grads


ANY = pl.BlockSpec(memory_space=pl.ANY)
HALF = D_MODEL // 2

BIG = {"w_in": (D_MODEL, D_IN // N_CHIPS, True), "w_gate_up": (D_MODEL, 2 * D_FF // N_CHIPS, True),
       "w_out": (D_MODEL // N_CHIPS, D_MODEL, False), "w_down": (D_FF // N_CHIPS, D_MODEL, False)}
BIG_NAMES = tuple(BIG)
N_BIG = len(BIG_NAMES)


def _full_shape(name):
    rows, cols, by_col = BIG[name]
    return (rows, cols * N_CHIPS) if by_col else (rows * N_CHIPS, cols)


def _place():
    return lax.axis_index("x"), lax.axis_index("y"), lax.axis_index("c")


def _chip_of(x, y, r):
    fx, fy = (r + 1) >> 1, (r + 1) & 1
    return (1 - x if fx else x), (1 - y if fy else y)


def _half_of(ref, by_col, half):
    start = pl.multiple_of(half * HALF, 128)
    return ref.at[pl.ds(start, HALF), :] if by_col else ref.at[:, pl.ds(start, HALF)]


def _shard_of(ref, name, chip):
    rows, cols, by_col = BIG[name]
    if by_col:
        return ref.at[:, pl.ds(pl.multiple_of(chip * cols, 128), cols)]
    return ref.at[pl.ds(pl.multiple_of(chip * rows, 16), rows), :]


def _shard_half_of(ref, name, chip, half):
    rows, cols, by_col = BIG[name]
    start = pl.multiple_of(half * HALF, 128)
    if by_col:
        return ref.at[pl.ds(start, HALF), pl.ds(pl.multiple_of(chip * cols, 128), cols)]
    return ref.at[pl.ds(pl.multiple_of(chip * rows, 16), rows), pl.ds(start, HALF)]


def _remote(src, dst, send_sems, recv_sems, k, dev):
    return pltpu.make_async_remote_copy(src_ref=src, dst_ref=dst, send_sem=send_sems.at[k], recv_sem=recv_sems.at[k],
                                        device_id=dev, device_id_type=MESH)


def _gather_weights(shards, meta_s, cw_s):
    small = (meta_s, cw_s)

    def body(*refs):
        ins, outs = refs[:N_BIG + 2], refs[N_BIG + 2:2 * (N_BIG + 2)]
        send_sems, recv_sems, local_sems = refs[2 * (N_BIG + 2):]
        x, y, c = _place()
        chip = 2 * x + y
        sibling = (x, y, 1 - c)
        others = [_chip_of(x, y, r) for r in range(3)]

        def small_block(full, a, q):
            cols = small[a].shape[1]
            return full.at[:, pl.ds(pl.multiple_of(q * cols, 128), cols)]

        local = []
        for a, name in enumerate(BIG_NAMES):
            local.append(pltpu.make_async_copy(ins[a], _shard_of(outs[a], name, chip), local_sems.at[a]))
        for a in range(2):
            local.append(pltpu.make_async_copy(ins[N_BIG + a], small_block(outs[N_BIG + a], a, chip),
                                               local_sems.at[N_BIG + a]))
        for cp in local:
            cp.start()

        sends = []
        for a, name in enumerate(BIG_NAMES):
            by_col = BIG[name][2]
            for r, (qx, qy) in enumerate(others):
                sends.append(_remote(_half_of(ins[a], by_col, c), _shard_half_of(outs[a], name, chip, c),
                                     send_sems, recv_sems, 6 * a + r, (qx, qy, c)))
        for a in range(2):
            for r, (qx, qy) in enumerate(others):
                sends.append(_remote(ins[N_BIG + a], small_block(outs[N_BIG + a], a, chip),
                                     send_sems, recv_sems, 6 * N_BIG + 3 * a + r, (qx, qy, c)))
        for cp in sends:
            cp.start()

        forwards = []
        for a, name in enumerate(BIG_NAMES):
            for r, (qx, qy) in enumerate(others):
                landed = _shard_half_of(outs[a], name, 2 * qx + qy, c)
                _remote(landed, landed, send_sems, recv_sems, 6 * a + r, (qx, qy, c)).wait_recv()
                fwd = _remote(landed, landed, send_sems, recv_sems, 6 * a + 3 + r, sibling)
                fwd.start()
                forwards.append(fwd)
        for a in range(2):
            for r, (qx, qy) in enumerate(others):
                landed = small_block(outs[N_BIG + a], a, 2 * qx + qy)
                _remote(landed, landed, send_sems, recv_sems, 6 * N_BIG + 3 * a + r, (qx, qy, c)).wait_recv()
        for a, name in enumerate(BIG_NAMES):
            for r, (qx, qy) in enumerate(others):
                landed = _shard_half_of(outs[a], name, 2 * qx + qy, 1 - c)
                _remote(landed, landed, send_sems, recv_sems, 6 * a + 3 + r, sibling).wait_recv()
        for cp in sends + forwards:
            cp.wait_send()
        for cp in local:
            cp.wait()

    n_sems = 6 * N_BIG + 6
    out_shape = [jax.ShapeDtypeStruct(_full_shape(n), BF16) for n in BIG_NAMES]
    out_shape += [jax.ShapeDtypeStruct((s.shape[0], s.shape[1] * N_CHIPS), F32) for s in small]
    return pl.pallas_call(
        body, in_specs=[ANY] * (N_BIG + 2), out_specs=[ANY] * (N_BIG + 2), out_shape=out_shape,
        scratch_shapes=[pltpu.SemaphoreType.DMA((n_sems,)), pltpu.SemaphoreType.DMA((n_sems,)),
                        pltpu.SemaphoreType.DMA((N_BIG + 2,))],
        name="gather_weights",
    )(*[shards[n] for n in BIG_NAMES], meta_s, cw_s)


def _exchange_halves(grads):
    def body(*refs):
        ins, outs = refs[:N_BIG], refs[N_BIG:2 * N_BIG]
        send_sems, recv_sems = refs[2 * N_BIG:]
        x, y, c = _place()
        copies = []
        for a, name in enumerate(BIG_NAMES):
            copies.append(_remote(_half_of(ins[a], BIG[name][2], 1 - c), outs[a], send_sems, recv_sems, a,
                                  (x, y, 1 - c)))
        for cp in copies:
            cp.start()
        for cp in copies:
            cp.wait()

    def half_shape(name):
        r, c_ = _full_shape(name)
        return (HALF, c_) if BIG[name][2] else (r, HALF)

    return pl.pallas_call(
        body, in_specs=[ANY] * N_BIG, out_specs=[ANY] * N_BIG,
        out_shape=[jax.ShapeDtypeStruct(half_shape(n), F32) for n in BIG_NAMES],
        scratch_shapes=[pltpu.SemaphoreType.DMA((N_BIG,)), pltpu.SemaphoreType.DMA((N_BIG,))],
        name="exchange_halves",
    )(*[grads[n] for n in BIG_NAMES])


def _chip_sum(g, got, name, core):
    by_col = BIG[name][2]
    rows, cols = got.shape
    if by_col:
        tr = 128
        g_spec = pl.BlockSpec((tr, cols), lambda i, s: (s[0] * (HALF // tr) + i, 0))
    else:
        tr = _row_tile(rows, 512)
        g_spec = pl.BlockSpec((tr, HALF), lambda i, s: (i, s[0]))
    blk = pl.BlockSpec((tr, cols), lambda i, s: (i, 0))

    def body(s_ref, g_ref, r_ref, f_ref, b_ref):
        t = g_ref[...] + r_ref[...]
        f_ref[...] = t
        b_ref[...] = t.astype(BF16)

    return pl.pallas_call(
        body,
        grid_spec=pltpu.PrefetchScalarGridSpec(num_scalar_prefetch=1, grid=(rows // tr,), in_specs=[g_spec, blk],
                                               out_specs=[blk, blk]),
        out_shape=[jax.ShapeDtypeStruct(got.shape, F32), jax.ShapeDtypeStruct(got.shape, BF16)],
        name="chip_sum_" + name, compiler_params=_params("parallel"),
    )(core, g, got)


def _piece_shape(name):
    rows, cols, by_col = BIG[name]
    return (HALF, cols) if by_col else (rows, HALF)


def _send_chip_sums(sums):
    def body(*refs):
        ins, outs = refs[:N_BIG], refs[N_BIG:2 * N_BIG]
        send_sems, recv_sems = refs[2 * N_BIG:]
        x, y, c = _place()
        copies = []
        for a, name in enumerate(BIG_NAMES):
            for r in range(3):
                qx, qy = _chip_of(x, y, r)
                copies.append(_remote(_shard_of(ins[a], name, 2 * qx + qy), outs[a].at[r], send_sems, recv_sems,
                                      3 * a + r, (qx, qy, c)))
        for cp in copies:
            cp.start()
        for cp in copies:
            cp.wait()

    return pl.pallas_call(
        body, in_specs=[ANY] * N_BIG, out_specs=[ANY] * N_BIG,
        out_shape=[jax.ShapeDtypeStruct((3,) + _piece_shape(n), BF16) for n in BIG_NAMES],
        scratch_shapes=[pltpu.SemaphoreType.DMA((3 * N_BIG,)), pltpu.SemaphoreType.DMA((3 * N_BIG,))],
        name="send_chip_sums",
    )(*[sums[n] for n in BIG_NAMES])


def _total(own, got, name, chip):
    rows, cols, by_col = BIG[name]
    pr, pc = _piece_shape(name)
    tr = _row_tile(pr, 352)
    if by_col:
        own_spec = pl.BlockSpec((tr, pc), lambda i, s: (i, s[0]))
    else:
        own_spec = pl.BlockSpec((tr, pc), lambda i, s: (s[0] * (pr // tr) + i, 0))
    got_spec = lambda r: pl.BlockSpec((None, tr, pc), lambda i, s: (r, i, 0))

    def body(s_ref, o_ref, a_ref, b_ref, c_ref, t_ref):
        t_ref[...] = ((o_ref[...] + a_ref[...].astype(F32)) + b_ref[...].astype(F32)) + c_ref[...].astype(F32)

    return pl.pallas_call(
        body,
        grid_spec=pltpu.PrefetchScalarGridSpec(
            num_scalar_prefetch=1, grid=(pr // tr,), in_specs=[own_spec, got_spec(0), got_spec(1), got_spec(2)],
            out_specs=pl.BlockSpec((tr, pc), lambda i, s: (i, 0))),
        out_shape=jax.ShapeDtypeStruct((pr, pc), F32),
        name="total_" + name, compiler_params=_params("parallel"),
    )(chip, own, got, got, got)


def _share_totals(totals):
    def body(*refs):
        ins, outs = refs[:N_BIG], refs[N_BIG:2 * N_BIG]
        send_sems, recv_sems, local_sems = refs[2 * N_BIG:]
        x, y, c = _place()
        local, copies = [], []
        for a, name in enumerate(BIG_NAMES):
            mine = _half_of(outs[a], BIG[name][2], c)
            local.append(pltpu.make_async_copy(ins[a], mine, local_sems.at[a]))
            copies.append(_remote(ins[a], mine, send_sems, recv_sems, a, (x, y, 1 - c)))
        for cp in local + copies:
            cp.start()
        for a, name in enumerate(BIG_NAMES):
            theirs = _half_of(outs[a], BIG[name][2], 1 - c)
            _remote(theirs, theirs, send_sems, recv_sems, a, (x, y, 1 - c)).wait_recv()
        for cp in copies:
            cp.wait_send()
        for cp in local:
            cp.wait()

    return pl.pallas_call(
        body, in_specs=[ANY] * N_BIG, out_specs=[ANY] * N_BIG,
        out_shape=[jax.ShapeDtypeStruct(BIG[n][:2], F32) for n in BIG_NAMES],
        scratch_shapes=[pltpu.SemaphoreType.DMA((N_BIG,)), pltpu.SemaphoreType.DMA((N_BIG,)),
                        pltpu.SemaphoreType.DMA((N_BIG,))],
        name="share_totals",
    )(*[totals[n] for n in BIG_NAMES])


VEC_ROWS = 32
VEC_ROW = {"mix_norm_g": 0, "conv_b": 1, "b_rgate": 2, "b_igate": 3, "lru_lambda": 4, "rg_norm_g": 5,
           "hg_lower_bound": 6, "hg_norm_g": 8, "ffn_norm_g": 9, "final_norm_g": 10, "loss": 11,
           "conv_w": 12, "meta_tokens": 16}
N_DEV = 8


def _all_reduce_small(pieces, gates):
    names = list(pieces)

    def body(*refs):
        ins = refs[:len(names)]
        g_ref, vec_ref, gsum_ref, send_v, got_v, got_g, send_sems, recv_sems = refs[len(names):]
        x, y, c = _place()
        me = 4 * x + 2 * y + c
        send_v[...] = jnp.zeros_like(send_v)
        for name, ref in zip(names, ins):
            nr, w = ref.shape
            send_v[VEC_ROW[name]:VEC_ROW[name] + nr, 0:w] = ref[...]
        got_v[me] = send_v[...]
        got_g[me] = g_ref[...]
        copies = []
        for r in range(1, N_DEV):
            peer = ((1 - x if r & 4 else x), (1 - y if r & 2 else y), (1 - c if r & 1 else c))
            copies.append(_remote(send_v, got_v.at[me], send_sems, recv_sems, r - 1, peer))
            copies.append(_remote(g_ref, got_g.at[me], send_sems, recv_sems, N_DEV - 1 + r - 1, peer))
        for cp in copies:
            cp.start()
        for cp in copies:
            cp.wait()
        vec = got_v[0]
        gs = got_g[0]
        for s in range(1, N_DEV):
            vec = vec + got_v[s]
            gs = gs + got_g[s]
        vec_ref[...] = vec
        gsum_ref[...] = gs

    vmem = pl.BlockSpec(memory_space=pltpu.VMEM)
    return pl.pallas_call(
        body, in_specs=[vmem] * (len(names) + 1), out_specs=[vmem, vmem],
        out_shape=[jax.ShapeDtypeStruct((VEC_ROWS, D_MODEL), F32), jax.ShapeDtypeStruct(gates.shape, F32)],
        scratch_shapes=[pltpu.VMEM((VEC_ROWS, D_MODEL), F32), pltpu.VMEM((N_DEV, VEC_ROWS, D_MODEL), F32),
                        pltpu.VMEM((N_DEV,) + gates.shape, F32),
                        pltpu.SemaphoreType.DMA((2 * (N_DEV - 1),)), pltpu.SemaphoreType.DMA((2 * (N_DEV - 1),))],
        name="all_reduce_small",
    )(*[pieces[n] for n in names], gates)


def _adamw_math(w, g, m, v):
    m = ADAM_B1 * m + (1.0 - ADAM_B1) * g
    v = ADAM_B2 * v + (1.0 - ADAM_B2) * (g * g)
    m_hat = m / (1.0 - ADAM_B1 ** ADAM_STEP)
    v_hat = v / (1.0 - ADAM_B2 ** ADAM_STEP)
    delta = -ADAM_LR * (m_hat / (jnp.sqrt(v_hat) + ADAM_EPS) + ADAM_WD * w)
    return delta, m, v


def _adamw_big(w, g, m, v, name):
    rows, cols = w.shape
    tr = _row_tile(rows, 352)

    def body(w_ref, g_ref, m_ref, v_ref, d_ref, nm_ref, nv_ref):
        d_ref[...], nm_ref[...], nv_ref[...] = _adamw_math(w_ref[...], g_ref[...], m_ref[...], v_ref[...])

    blk = pl.BlockSpec((tr, cols), lambda i: (i, 0))
    return pl.pallas_call(
        body, grid=(rows // tr,), in_specs=[blk] * 4, out_specs=[blk] * 3,
        out_shape=[jax.ShapeDtypeStruct(w.shape, F32)] * 3,
        name="adamw_" + name, compiler_params=_params("parallel"),
    )(w, g, m, v)


SMALL = {"meta_tokens": (N_META, D_MODEL // N_CHIPS), "mix_norm_g": (1, D_MODEL), "conv_w": (CONV_W, D_RG // N_CHIPS),
         "conv_b": (1, D_RG), "w_rgate": (D_RG, RG_HEAD_DIM), "b_rgate": (1, D_RG), "w_igate": (D_RG, RG_HEAD_DIM),
         "b_igate": (1, D_RG), "lru_lambda": (1, D_RG), "rg_norm_g": (1, D_RG), "hg_lower_bound": (2, D_HG),
         "hg_norm_g": (1, HG_HEAD_DIM), "ffn_norm_g": (1, D_MODEL), "final_norm_g": (1, D_MODEL)}
SMALL_NAMES = tuple(SMALL)
SHARDED_SMALL = ("meta_tokens", "conv_w")


def _adamw_small(vec, gates, w, m, v):
    n = len(SMALL_NAMES)

    def body(*refs):
        vec_ref, gates_ref = refs[:2]
        w_refs, m_refs, v_refs = refs[2:2 + n], refs[2 + n:2 + 2 * n], refs[2 + 2 * n:2 + 3 * n]
        outs = refs[2 + 3 * n:]
        loss_ref = outs[0]
        x, y, _ = _place()
        chip = 2 * x + y
        loss_ref[...] = vec_ref[VEC_ROW["loss"]:VEC_ROW["loss"] + 1, 0:1]

        def update(k, g):
            g_ref, d_ref, nm_ref, nv_ref = outs[1 + 4 * k:5 + 4 * k]
            g_ref[...] = g
            d_ref[...], nm_ref[...], nv_ref[...] = _adamw_math(w_refs[k][...], g, m_refs[k][...], v_refs[k][...])

        for k, name in enumerate(SMALL_NAMES):
            nr, w_ = SMALL[name]
            if name == "w_rgate":
                update(k, gates_ref[0:D_RG, :])
            elif name == "w_igate":
                update(k, gates_ref[D_RG:2 * D_RG, :])
            elif name in SHARDED_SMALL:
                r0 = VEC_ROW[name]
                for q in range(N_CHIPS):
                    @pl.when(chip == q)
                    def _(k=k, r0=r0, nr=nr, w_=w_, q=q):
                        update(k, vec_ref[r0:r0 + nr, q * w_:(q + 1) * w_])
            else:
                r0 = VEC_ROW[name]
                update(k, vec_ref[r0:r0 + nr, 0:w_])

    vmem = pl.BlockSpec(memory_space=pltpu.VMEM)
    out_shape = [jax.ShapeDtypeStruct((1, 1), F32)]
    for name in SMALL_NAMES:
        out_shape += [jax.ShapeDtypeStruct(SMALL[name], F32)] * 4
    outs = pl.pallas_call(
        body, in_specs=[vmem] * (2 + 3 * n), out_specs=[vmem] * len(out_shape), out_shape=out_shape,
        name="adamw_small",
    )(vec, gates, *[w[k] for k in SMALL_NAMES], *[m[k] for k in SMALL_NAMES], *[v[k] for k in SMALL_NAMES])
    loss = outs[0]
    res = {name: tuple(outs[1 + 4 * k:5 + 4 * k]) for k, name in enumerate(SMALL_NAMES)}
    return loss, res


WEIGHT_NAMES = ("meta_tokens", "mix_norm_g", "w_in", "conv_w", "conv_b", "w_rgate", "b_rgate", "w_igate", "b_igate",
                "lru_lambda", "rg_norm_g", "hg_lower_bound", "hg_norm_g", "w_out", "ffn_norm_g", "w_gate_up", "w_down",
                "final_norm_g")


def _to_2d(name, a):
    if name in BIG:
        return a.reshape(BIG[name][:2])
    return a.reshape(SMALL[name])


def kernel(x, meta_tokens, mix_norm_g, w_in, conv_w, conv_b, w_rgate, b_rgate, w_igate, b_igate, lru_lambda, rg_norm_g, hg_lower_bound, hg_norm_g, w_out, ffn_norm_g, w_gate_up, w_down, final_norm_g, loss_target, m_meta_tokens, m_mix_norm_g, m_w_in, m_conv_w, m_conv_b, m_w_rgate, m_b_rgate, m_w_igate, m_b_igate, m_lru_lambda, m_rg_norm_g, m_hg_lower_bound, m_hg_norm_g, m_w_out, m_ffn_norm_g, m_w_gate_up, m_w_down, m_final_norm_g, v_meta_tokens, v_mix_norm_g, v_w_in, v_conv_w, v_conv_b, v_w_rgate, v_b_rgate, v_w_igate, v_b_igate, v_lru_lambda, v_rg_norm_g, v_hg_lower_bound, v_hg_norm_g, v_w_out, v_ffn_norm_g, v_w_gate_up, v_w_down, v_final_norm_g):
    w_raw = dict(zip(WEIGHT_NAMES, (meta_tokens, mix_norm_g, w_in, conv_w, conv_b, w_rgate, b_rgate, w_igate, b_igate,
                                    lru_lambda, rg_norm_g, hg_lower_bound, hg_norm_g, w_out, ffn_norm_g, w_gate_up,
                                    w_down, final_norm_g)))
    m_raw = dict(zip(WEIGHT_NAMES, (m_meta_tokens, m_mix_norm_g, m_w_in, m_conv_w, m_conv_b, m_w_rgate, m_b_rgate,
                                    m_w_igate, m_b_igate, m_lru_lambda, m_rg_norm_g, m_hg_lower_bound, m_hg_norm_g,
                                    m_w_out, m_ffn_norm_g, m_w_gate_up, m_w_down, m_final_norm_g)))
    v_raw = dict(zip(WEIGHT_NAMES, (v_meta_tokens, v_mix_norm_g, v_w_in, v_conv_w, v_conv_b, v_w_rgate, v_b_rgate,
                                    v_w_igate, v_b_igate, v_lru_lambda, v_rg_norm_g, v_hg_lower_bound, v_hg_norm_g,
                                    v_w_out, v_ffn_norm_g, v_w_gate_up, v_w_down, v_final_norm_g)))
    w = {k: _to_2d(k, a) for k, a in w_raw.items()}
    m = {k: _to_2d(k, a) for k, a in m_raw.items()}
    v = {k: _to_2d(k, a) for k, a in v_raw.items()}

    shards = {k: w[k].astype(BF16) for k in BIG_NAMES}
    *full, meta_full, cw_full = _gather_weights(shards, w["meta_tokens"], w["conv_w"])
    full = dict(zip(BIG_NAMES, full))

    seq = x.shape[1]
    h0 = jnp.concatenate([jnp.zeros((PAD, D_MODEL), F32), meta_full, x[0]], axis=0)
    target = jnp.concatenate([jnp.zeros((PAD + N_META, D_MODEL), F32), loss_target[0]], axis=0)
    small = {k: w[k] for k in SMALL_NAMES if k not in SHARDED_SMALL}
    small["conv_w"] = cw_full
    loss, dh0, grads = _local_step(h0, target, full["w_in"], full["w_out"], full["w_gate_up"], full["w_down"], small)

    x_i, y_i, c_i = _place()
    core = jnp.reshape(c_i, (1,)).astype(jnp.int32)
    chip = jnp.reshape(2 * x_i + y_i, (1,)).astype(jnp.int32)
    got = dict(zip(BIG_NAMES, _exchange_halves(grads)))
    sums = {n: _chip_sum(grads[n], got[n], n, core) for n in BIG_NAMES}
    arrived = dict(zip(BIG_NAMES, _send_chip_sums({n: sums[n][1] for n in BIG_NAMES})))
    totals = {n: _total(sums[n][0], arrived[n], n, chip) for n in BIG_NAMES}
    g_big = dict(zip(BIG_NAMES, _share_totals(totals)))

    pieces = {k: grads[k] for k in VEC_ROW if k not in ("loss", "meta_tokens")}
    pieces["loss"] = loss
    pieces["meta_tokens"] = dh0[PAD:PAD + N_META]
    vec, gates = _all_reduce_small(pieces, grads["w_gates"])
    loss_sum, res = _adamw_small(vec, gates, w, m, v)
    for n in BIG_NAMES:
        res[n] = (g_big[n],) + tuple(_adamw_big(w[n], g_big[n], m[n], v[n], n))

    grad_x = dh0[PAD + N_META:].reshape(1, seq, D_MODEL)
    out = [loss_sum.reshape(()), grad_x]
    for j in range(4):
        out += [res[n][j].reshape(w_raw[n].shape) for n in WEIGHT_NAMES]
    return tuple(out)
```

```python
import functools
import math

import jax
import jax.numpy as jnp
from jax import lax
from jax.experimental import pallas as pl
from jax.experimental.pallas import tpu as pltpu

F32 = jnp.float32
BF16 = jnp.bfloat16
HIGHEST = lax.Precision.HIGHEST
MESH = pl.DeviceIdType.MESH

D_MODEL = 1024
D_RG = 512
RG_HEAD_DIM = 64
D_HG = 512
HG_HEAD_DIM = 128
HG_HEADS = 4
CHUNK = 64
SUB = 16
N_SUB = CHUNK // SUB
N_META = 16
PAD = CHUNK - N_META
D_IN = 3072
D_FF = 2816
CONV_W = 4
LRU_C = 8.0
EPS = 1e-6
EXP_CLAMP = 80.0
GELU_C = math.sqrt(2.0 / math.pi)
GELU_A = 0.044715
N_CHIPS = 4

ADAM_LR = 0.001
ADAM_B1 = 0.9
ADAM_B2 = 0.999
ADAM_EPS = 1e-08
ADAM_WD = 0.01
ADAM_STEP = 10

VMEM_LIMIT = 56 * 1024 * 1024


def _params(*sem):
    return pltpu.CompilerParams(dimension_semantics=sem, vmem_limit_bytes=VMEM_LIMIT)


def _row_tile(rows, target):
    best = None
    for t in range(16, min(rows, target) + 1, 16):
        if rows % t == 0:
            best = t
    assert best is not None, rows
    return best


def _sigmoid(x):
    return 1.0 / (1.0 + jnp.exp(-x))


def _dot(a, b):
    return jnp.dot(a, b, preferred_element_type=F32)


def _dot_nt(a, b):
    return lax.dot_general(a, b, (((1,), (1,)), ((), ())), preferred_element_type=F32)


def _dot_tn(a, b):
    return lax.dot_general(a, b, (((0,), (0,)), ((), ())), preferred_element_type=F32)


def _rms(x):
    return lax.rsqrt(jnp.mean(x * x, axis=-1, keepdims=True) + EPS)


def _rms_bwd(dn, n, r):
    return r * (dn - n * jnp.mean(dn * n, axis=-1, keepdims=True))


def _gelu_parts(x):
    t = jnp.tanh(GELU_C * (x + GELU_A * x * x * x))
    g = 0.5 * x * (1.0 + t)
    dg = 0.5 * (1.0 + t) + 0.5 * x * (1.0 - t * t) * GELU_C * (1.0 + 3.0 * GELU_A * x * x)
    return g, dg


def _softplus_neg(lam):
    e = jnp.exp(-jnp.abs(lam))
    w = 1.0 + e
    log1p = jnp.where(w == 1.0, e, jnp.log(w) * e / (w - 1.0))
    return jnp.maximum(-lam, 0.0) + log1p


def _head_mask():
    r = lax.broadcasted_iota(jnp.int32, (D_RG, D_RG), 0) // RG_HEAD_DIM
    c = lax.broadcasted_iota(jnp.int32, (D_RG, D_RG), 1) // RG_HEAD_DIM
    return r == c


def _head_fold():
    r = lax.broadcasted_iota(jnp.int32, (D_RG, RG_HEAD_DIM), 0) % RG_HEAD_DIM
    c = lax.broadcasted_iota(jnp.int32, (D_RG, RG_HEAD_DIM), 1)
    return (r == c).astype(F32)


def _gate_weights(w_r, w_i):
    def body(wr_ref, wi_ref, o_ref):
        fold = _head_fold()
        mask = _head_mask()
        for k, ref in enumerate((wr_ref, wi_ref)):
            full = lax.dot_general(ref[...], fold, (((1,), (1,)), ((), ())),
                                   precision=HIGHEST, preferred_element_type=F32)
            o_ref[:, k * D_RG:(k + 1) * D_RG] = jnp.where(mask, full, 0.0).astype(BF16)

    return pl.pallas_call(
        body, out_shape=jax.ShapeDtypeStruct((D_RG, 2 * D_RG), BF16), name="gate_weights",
    )(w_r, w_i)


def _in_proj(h0, g1, w_in):
    T = h0.shape[0]
    tm = _row_tile(T, 416)

    def body(h_ref, g_ref, w_ref, p_ref, u_ref):
        h = h_ref[...]
        u = (h * _rms(h) * g_ref[...]).astype(BF16)
        u_ref[...] = u
        p_ref[...] = _dot(u, w_ref[...])

    return pl.pallas_call(
        body, grid=(T // tm,),
        in_specs=[pl.BlockSpec((tm, D_MODEL), lambda i: (i, 0)),
                  pl.BlockSpec((1, D_MODEL), lambda i: (0, 0)),
                  pl.BlockSpec((D_MODEL, D_IN), lambda i: (0, 0))],
        out_specs=[pl.BlockSpec((tm, D_IN), lambda i: (i, 0)),
                   pl.BlockSpec((tm, D_MODEL), lambda i: (i, 0))],
        out_shape=[jax.ShapeDtypeStruct((T, D_IN), F32), jax.ShapeDtypeStruct((T, D_MODEL), BF16)],
        name="in_proj", compiler_params=_params("parallel"),
    )(h0, g1, w_in)


def _scan_block_fwd(A, B, rowi):
    for d in (1, 2, 4):
        a_sh = pltpu.roll(A, d, axis=0)
        b_sh = pltpu.roll(B, d, axis=0)
        m = rowi >= d
        B = jnp.where(m, A * b_sh + B, B)
        A = jnp.where(m, A * a_sh, A)
    return A, B


def _scan_block_bwd(A, B, rowi):
    for d in (1, 2, 4):
        a_sh = pltpu.roll(A, 8 - d, axis=0)
        b_sh = pltpu.roll(B, 8 - d, axis=0)
        m = rowi < 8 - d
        B = jnp.where(m, A * b_sh + B, B)
        A = jnp.where(m, A * a_sh, A)
    return A, B


def _rg_gates(xc, w_ref, bg_ref, lam):
    pre = _dot(xc.astype(BF16), w_ref[...]) + bg_ref[...]
    r = _sigmoid(pre[:, :D_RG])
    ig = _sigmoid(pre[:, D_RG:])
    sp = _softplus_neg(lam)
    la = -LRU_C * sp * r
    a = jnp.exp(la)
    th = jnp.tanh(la)
    m = jnp.sqrt(-2.0 * th / (1.0 - th))
    return r, ig, sp, a, m


def _conv(ext, cw_ref, cb_ref, tm):
    xc = cb_ref[...] + cw_ref[0:1, :] * ext[8 - 3:8 - 3 + tm, :]
    for j in range(1, CONV_W):
        xc = xc + cw_ref[j:j + 1, :] * ext[8 - 3 + j:8 - 3 + j + tm, :]
    return xc


def _rg_fwd(p, cw, cb, wg, bg, lam, rg_g):
    T = p.shape[0]
    tm = _row_tile(T, 416)

    def body(xg_ref, cw_ref, cb_ref, w_ref, bg_ref, lam_ref, g_ref, y_ref, h_ref, ext, a_s, b_s, carry):
        i = pl.program_id(0)

        @pl.when(i == 0)
        def _():
            ext[0:8, :] = jnp.zeros((8, D_RG), F32)
            carry[...] = jnp.zeros((1, D_RG), F32)

        ext[8:8 + tm, :] = xg_ref[:, :D_RG]
        xc = _conv(ext, cw_ref, cb_ref, tm)
        r, ig, sp, a, m = _rg_gates(xc, w_ref, bg_ref, lam_ref[...])
        row = i * tm + lax.broadcasted_iota(jnp.int32, (tm, 1), 0)
        a_s[...] = a
        b_s[...] = jnp.where(row >= PAD, m * ig * xc, 0.0)
        rowi = lax.broadcasted_iota(jnp.int32, (8, D_RG), 0)

        def blk(j, c):
            o = pl.multiple_of(j * 8, 8)
            A, B = _scan_block_fwd(a_s[pl.ds(o, 8), :], b_s[pl.ds(o, 8), :], rowi)
            h = B + A * c
            h_ref[pl.ds(o, 8), :] = h
            return h[7:8, :]

        carry[...] = lax.fori_loop(0, tm // 8, blk, carry[...])
        ext[0:8, :] = ext[tm:tm + 8, :]
        g, _ = _gelu_parts(xg_ref[:, D_RG:])
        yy = g * h_ref[...]
        y_ref[...] = yy * _rms(yy) * g_ref[...]

    vec = lambda n: pl.BlockSpec((1, n), lambda i: (0, 0))
    return pl.pallas_call(
        body, grid=(T // tm,),
        in_specs=[pl.BlockSpec((tm, 2 * D_RG), lambda i: (i, 0)),
                  pl.BlockSpec((CONV_W, D_RG), lambda i: (0, 0)), vec(D_RG),
                  pl.BlockSpec((D_RG, 2 * D_RG), lambda i: (0, 0)), vec(2 * D_RG), vec(D_RG), vec(D_RG)],
        out_specs=[pl.BlockSpec((tm, D_RG), lambda i: (i, 0)), pl.BlockSpec((tm, D_RG), lambda i: (i, 0))],
        out_shape=[jax.ShapeDtypeStruct((T, D_RG), F32), jax.ShapeDtypeStruct((T, D_RG), F32)],
        scratch_shapes=[pltpu.VMEM((tm + 8, D_RG), F32), pltpu.VMEM((tm, D_RG), F32),
                        pltpu.VMEM((tm, D_RG), F32), pltpu.VMEM((1, D_RG), F32)],
        name="rg_fwd", compiler_params=_params("arbitrary"),
    )(p, cw, cb, wg, bg, lam, rg_g)


def _tri(lower):
    r = lax.broadcasted_iota(jnp.int32, (CHUNK, CHUNK), 0)
    c = lax.broadcasted_iota(jnp.int32, (CHUNK, CHUNK), 1)
    return ((c <= r) if lower else (c >= r)).astype(F32)


def _hg_gates(hq, hf, lbraw_ref, valid):
    lb = _sigmoid(lbraw_ref[0:1, :] - lbraw_ref[1:2, :])
    sq = _sigmoid(hq)
    q = hq * sq
    sf = _sigmoid(hf)
    f = lb + (1.0 - lb) * sf
    lf = jnp.where(valid, jnp.log(f), 0.0)
    b = jnp.dot(_tri(True), lf, precision=HIGHEST, preferred_element_type=F32)
    return lb, sq, q, sf, f, b


def _hg_head(qh, kh, bh):
    blk = lax.broadcasted_iota(jnp.int32, (CHUNK, 1), 0) // SUB
    b_last = bh[CHUNK - 1:CHUNK, :]
    refs = [bh[SUB * s:SUB * s + 1, :] for s in range(N_SUB)]
    r_sel = refs[N_SUB - 1]
    for s in range(N_SUB - 2, -1, -1):
        r_sel = jnp.where(blk == s, refs[s], r_sel)
    eb = jnp.exp(bh)
    eq = jnp.exp(bh - r_sel)
    ekh = jnp.exp(b_last - bh)
    ek = [jnp.exp(jnp.minimum(refs[s] - bh, EXP_CLAMP)) for s in range(N_SUB)]
    qe = qh * eq
    q_hat = jnp.concatenate([jnp.where(blk == s, qe, 0.0) for s in range(N_SUB)], axis=1)
    k_til = jnp.concatenate([kh * ek[s] for s in range(N_SUB)], axis=1)
    return blk, b_last, eb, eq, ekh, ek, q_hat, k_til


def _causal():
    r = lax.broadcasted_iota(jnp.int32, (CHUNK, CHUNK), 0)
    c = lax.broadcasted_iota(jnp.int32, (CHUNK, CHUNK), 1)
    return r >= c


def _hg_fwd(p, lbraw, hg_g):
    T = p.shape[0]
    n_chunks = T // CHUNK

    def body(hq_ref, hf_ref, hi_ref, hg_ref, lb_ref, g_ref, y_ref, o_ref, st_all_ref, st):
        n = pl.program_id(0)

        @pl.when(n == 0)
        def _():
            st[...] = jnp.zeros_like(st)

        valid = (n * CHUNK + lax.broadcasted_iota(jnp.int32, (CHUNK, 1), 0)) >= PAD
        hq, hf, v, hg = hq_ref[...], hf_ref[...], hi_ref[...], hg_ref[...]
        lb, sq, q, sf, f, b = _hg_gates(hq, hf, lb_ref, valid)
        k = 1.0 - f
        st_all_ref[0] = st[...]
        causal = _causal()
        for h in range(HG_HEADS):
            sl = slice(h * HG_HEAD_DIM, (h + 1) * HG_HEAD_DIM)
            qh, kh, vh, bh = q[:, sl], k[:, sl], v[:, sl], b[:, sl]
            st_h = st[sl, :]
            _, b_last, eb, _, ekh, _, q_hat, k_til = _hg_head(qh, kh, bh)
            vb = vh.astype(BF16)
            inter = _dot_nt((qh * eb).astype(BF16), st_h.astype(BF16))
            att = jnp.where(causal, _dot_nt(q_hat.astype(BF16), k_til.astype(BF16)), 0.0)
            o = inter + _dot(att.astype(BF16), vb)
            st[sl, :] = st_h * jnp.exp(b_last) + _dot_tn(vb, (kh * ekh).astype(BF16))
            o_ref[:, sl] = o
            hgh = hg[:, sl]
            y_ref[:, sl] = o * _rms(o) * g_ref[...] * (hgh * _sigmoid(hgh))

    col = lambda j: pl.BlockSpec((CHUNK, D_HG), lambda n: (n, j))
    return pl.pallas_call(
        body, grid=(n_chunks,),
        in_specs=[col(2), col(3), col(4), col(5),
                  pl.BlockSpec((2, D_HG), lambda n: (0, 0)), pl.BlockSpec((1, HG_HEAD_DIM), lambda n: (0, 0))],
        out_specs=[pl.BlockSpec((CHUNK, D_HG), lambda n: (n, 0)), pl.BlockSpec((CHUNK, D_HG), lambda n: (n, 0)),
                   pl.BlockSpec((1, D_HG, HG_HEAD_DIM), lambda n: (n, 0, 0))],
        out_shape=[jax.ShapeDtypeStruct((T, D_HG), F32), jax.ShapeDtypeStruct((T, D_HG), F32),
                   jax.ShapeDtypeStruct((n_chunks, D_HG, HG_HEAD_DIM), F32)],
        scratch_shapes=[pltpu.VMEM((D_HG, HG_HEAD_DIM), F32)],
        name="hg_fwd", compiler_params=_params("arbitrary"),
    )(p, p, p, p, lbraw, hg_g)


def _out_proj(h0, y_rg, y_hg, w_out, g2):
    T = h0.shape[0]
    tm = _row_tile(T, 832)

    def body(h_ref, yr_ref, yh_ref, w_ref, g_ref, h1_ref, v_ref, y_ref):
        y_ref[:, :D_RG] = yr_ref[...].astype(BF16)
        y_ref[:, D_RG:] = yh_ref[...].astype(BF16)
        h1 = h_ref[...] + _dot(y_ref[...], w_ref[...])
        h1_ref[...] = h1
        v_ref[...] = (h1 * _rms(h1) * g_ref[...]).astype(BF16)

    row = lambda n: pl.BlockSpec((tm, n), lambda i: (i, 0))
    return pl.pallas_call(
        body, grid=(T // tm,),
        in_specs=[row(D_MODEL), row(D_RG), row(D_HG), pl.BlockSpec((D_MODEL, D_MODEL), lambda i: (0, 0)),
                  pl.BlockSpec((1, D_MODEL), lambda i: (0, 0))],
        out_specs=[row(D_MODEL), row(D_MODEL), row(D_MODEL)],
        out_shape=[jax.ShapeDtypeStruct((T, D_MODEL), F32), jax.ShapeDtypeStruct((T, D_MODEL), BF16),
                   jax.ShapeDtypeStruct((T, D_MODEL), BF16)],
        name="out_proj", compiler_params=_params("parallel"),
    )(h0, y_rg, y_hg, w_out, g2)


def _gate_up(v, w_gu):
    T = v.shape[0]
    tm = _row_tile(T, 416)
    tn = D_FF // 2

    def body(v_ref, wg_ref, wu_ref, g_ref, u_ref, act_ref):
        x = v_ref[...]
        g = _dot(x, wg_ref[...])
        u = _dot(x, wu_ref[...])
        g_ref[...] = g
        u_ref[...] = u
        act_ref[...] = (g * _sigmoid(g) * u).astype(BF16)

    blk = pl.BlockSpec((tm, tn), lambda j, i: (i, j))
    return pl.pallas_call(
        body, grid=(2, T // tm),
        in_specs=[pl.BlockSpec((tm, D_MODEL), lambda j, i: (i, 0)),
                  pl.BlockSpec((D_MODEL, tn), lambda j, i: (0, j)),
                  pl.BlockSpec((D_MODEL, tn), lambda j, i: (0, j + 2))],
        out_specs=[blk, blk, blk],
        out_shape=[jax.ShapeDtypeStruct((T, D_FF), F32), jax.ShapeDtypeStruct((T, D_FF), F32),
                   jax.ShapeDtypeStruct((T, D_FF), BF16)],
        name="gate_up", compiler_params=_params("parallel", "parallel"),
    )(v, w_gu, w_gu)


def _down_loss(h1, act, w_down, gf, target):
    T = h1.shape[0]
    tm = _row_tile(T, 832)

    def body(h_ref, a_ref, w_ref, g_ref, t_ref, dh2_ref, dh2b_ref, loss_ref, gg_ref):
        i = pl.program_id(0)

        @pl.when(i == 0)
        def _():
            loss_ref[...] = jnp.zeros_like(loss_ref)
            gg_ref[...] = jnp.zeros_like(gg_ref)

        h2 = h_ref[...] + _dot(a_ref[...], w_ref[...])
        r = _rms(h2)
        n = h2 * r
        gf_ = g_ref[...]
        row = i * tm + lax.broadcasted_iota(jnp.int32, (tm, 1), 0)
        err = jnp.where(row >= PAD + N_META, n * gf_ - t_ref[...], 0.0)
        loss_ref[...] += 0.5 * jnp.sum(jnp.mean(err * err, axis=-1, keepdims=True), axis=0, keepdims=True)
        dy = err * (1.0 / D_MODEL)
        gg_ref[...] += jnp.sum(dy * n, axis=0, keepdims=True)
        dh2 = _rms_bwd(dy * gf_, n, r)
        dh2_ref[...] = dh2
        dh2b_ref[...] = dh2.astype(BF16)

    row_spec = lambda n: pl.BlockSpec((tm, n), lambda i: (i, 0))
    return pl.pallas_call(
        body, grid=(T // tm,),
        in_specs=[row_spec(D_MODEL), row_spec(D_FF), pl.BlockSpec((D_FF, D_MODEL), lambda i: (0, 0)),
                  pl.BlockSpec((1, D_MODEL), lambda i: (0, 0)), row_spec(D_MODEL)],
        out_specs=[row_spec(D_MODEL), row_spec(D_MODEL), pl.BlockSpec((1, 1), lambda i: (0, 0)),
                   pl.BlockSpec((1, D_MODEL), lambda i: (0, 0))],
        out_shape=[jax.ShapeDtypeStruct((T, D_MODEL), F32), jax.ShapeDtypeStruct((T, D_MODEL), BF16),
                   jax.ShapeDtypeStruct((1, 1), F32), jax.ShapeDtypeStruct((1, D_MODEL), F32)],
        name="down_loss", compiler_params=_params("arbitrary"),
    )(h1, act, w_down, gf, target)


def _ffn_bwd_act(dh2b, gate, up, w_down):
    T = dh2b.shape[0]
    tm = _row_tile(T, 208)

    def body(d_ref, g_ref, u_ref, w_ref, dgu_ref):
        dact = _dot_nt(d_ref[...], w_ref[...])
        g = g_ref[...]
        s = _sigmoid(g)
        dgu_ref[:, :D_FF] = (dact * u_ref[...] * s * (1.0 + g * (1.0 - s))).astype(BF16)
        dgu_ref[:, D_FF:] = (dact * g * s).astype(BF16)

    row = lambda n: pl.BlockSpec((tm, n), lambda i: (i, 0))
    return pl.pallas_call(
        body, grid=(T // tm,),
        in_specs=[row(D_MODEL), row(D_FF), row(D_FF), pl.BlockSpec((D_FF, D_MODEL), lambda i: (0, 0))],
        out_specs=row(2 * D_FF),
        out_shape=jax.ShapeDtypeStruct((T, 2 * D_FF), BF16),
        name="ffn_bwd_act", compiler_params=_params("parallel"),
    )(dh2b, gate, up, w_down)


def _ffn_bwd_in(dgu, w_gu, h1, g2, dh2, w_out):
    T = h1.shape[0]
    tm = _row_tile(T, 416)

    def body(dgu_ref, wgu_ref, h_ref, g_ref, d2_ref, wo_ref, dh1_ref, dh1b_ref, dy_ref, gg_ref):
        i = pl.program_id(0)

        @pl.when(i == 0)
        def _():
            gg_ref[...] = jnp.zeros_like(gg_ref)

        dv = _dot_nt(dgu_ref[...], wgu_ref[...])
        h1 = h_ref[...]
        r = _rms(h1)
        n = h1 * r
        gg_ref[...] += jnp.sum(dv * n, axis=0, keepdims=True)
        dh1 = d2_ref[...] + _rms_bwd(dv * g_ref[...], n, r)
        dh1_ref[...] = dh1
        db = dh1.astype(BF16)
        dh1b_ref[...] = db
        dy_ref[...] = _dot_nt(db, wo_ref[...])

    row = lambda n: pl.BlockSpec((tm, n), lambda i: (i, 0))
    return pl.pallas_call(
        body, grid=(T // tm,),
        in_specs=[row(2 * D_FF), pl.BlockSpec((D_MODEL, 2 * D_FF), lambda i: (0, 0)),
                  row(D_MODEL), pl.BlockSpec((1, D_MODEL), lambda i: (0, 0)), row(D_MODEL),
                  pl.BlockSpec((D_MODEL, D_MODEL), lambda i: (0, 0))],
        out_specs=[row(D_MODEL), row(D_MODEL), row(D_MODEL), pl.BlockSpec((1, D_MODEL), lambda i: (0, 0))],
        out_shape=[jax.ShapeDtypeStruct((T, D_MODEL), F32), jax.ShapeDtypeStruct((T, D_MODEL), BF16),
                   jax.ShapeDtypeStruct((T, D_MODEL), F32), jax.ShapeDtypeStruct((1, D_MODEL), F32)],
        name="ffn_bwd_in", compiler_params=_params("arbitrary"),
    )(dgu, w_gu, h1, g2, dh2, w_out)


def _rg_bwd(p, hs, dy, cw, cb, wg, bg, lam, rg_g):
    T = p.shape[0]
    tm = _row_tile(T, 208)
    nt = T // tm
    hb = tm // 8

    def body(xg_ref, xh_ref, h_ref, hh_ref, dy_ref, cw_ref, cb_ref, w_ref, bg_ref, lam_ref, g_ref,
             dp_ref, gcw_ref, gcb_ref, gw_ref, gbg_ref, glam_ref, gg_ref,
             ext, dext, a_s, b_s, d_s, gacc, carry_d, carry_a):
        i = pl.program_id(0)
        t_idx = nt - 1 - i

        @pl.when(i == 0)
        def _():
            dext[tm:tm + 8, :] = jnp.zeros((8, D_RG), F32)
            carry_d[...] = jnp.zeros_like(carry_d)
            carry_a[...] = jnp.zeros_like(carry_a)
            gacc[...] = jnp.zeros_like(gacc)
            for ref in (gcw_ref, gcb_ref, gbg_ref, glam_ref, gg_ref, gw_ref):
                ref[...] = jnp.zeros_like(ref)

        first = t_idx == 0
        ext[0:8, :] = jnp.where(first, 0.0, xh_ref[:, :D_RG])
        ext[8:8 + tm, :] = xg_ref[:, :D_RG]
        xc = _conv(ext, cw_ref, cb_ref, tm)
        lam_ = lam_ref[...]
        r, ig, sp, a, m = _rg_gates(xc, w_ref, bg_ref, lam_)
        row = t_idx * tm + lax.broadcasted_iota(jnp.int32, (tm, 1), 0)
        valid = row >= PAD

        gr = xg_ref[:, D_RG:]
        g, dgelu = _gelu_parts(gr)
        h = h_ref[...]
        yy = g * h
        rr = _rms(yy)
        nn = yy * rr
        dy_ = dy_ref[...]
        gg_ref[...] += jnp.sum(dy_ * nn, axis=0, keepdims=True)
        dyy = _rms_bwd(dy_ * g_ref[...], nn, rr)
        dp_ref[:, D_RG:] = dyy * h * dgelu

        a_s[...] = a
        b_s[...] = dyy * g
        rowi = lax.broadcasted_iota(jnp.int32, (8, D_RG), 0)

        def blk(jj, c):
            cd, ca = c
            o = pl.multiple_of((hb - 1 - jj) * 8, 8)
            a_blk = a_s[pl.ds(o, 8), :]
            a_next = jnp.where(rowi == 7, ca, pltpu.roll(a_blk, 7, axis=0))
            A, B = _scan_block_bwd(a_next, b_s[pl.ds(o, 8), :], rowi)
            d = B + A * cd
            d_s[pl.ds(o, 8), :] = d
            return d[0:1, :], a_blk[0:1, :]

        cd, ca = lax.fori_loop(0, hb, blk, (carry_d[...], carry_a[...]))
        carry_d[...] = cd
        carry_a[...] = ca
        delta = d_s[...]

        h_last_prev = jnp.where(first, 0.0, hh_ref[7:8, :])
        row0 = lax.broadcasted_iota(jnp.int32, (tm, 1), 0) == 0
        h_prev = jnp.where(row0, h_last_prev, pltpu.roll(h, 1, axis=0))
        dbx = jnp.where(valid, delta, 0.0)
        da = delta * h_prev
        di = dbx * m * xc
        dm = dbx * ig * xc
        dla = a * (da - dm * a / m)
        dla = jnp.where(valid, dla, 0.0)
        glam_ref[...] += jnp.sum(dla * r, axis=0, keepdims=True) * (LRU_C * _sigmoid(-lam_))
        dr = (-LRU_C) * sp * dla
        dpre = jnp.concatenate([dr * r * (1.0 - r), di * ig * (1.0 - ig)], axis=1)
        gbg_ref[...] += jnp.sum(dpre, axis=0, keepdims=True)
        dpre_b = dpre.astype(BF16)
        gacc[...] += _dot_tn(xc.astype(BF16), dpre_b)
        dxc = dbx * m * ig + _dot_nt(dpre_b, w_ref[...])
        gcb_ref[...] += jnp.sum(dxc, axis=0, keepdims=True)
        for j in range(CONV_W):
            gcw_ref[j:j + 1, :] += jnp.sum(dxc * ext[8 - 3 + j:8 - 3 + j + tm, :], axis=0, keepdims=True)
        dext[0:tm, :] = dxc
        dxr = cw_ref[0:1, :] * dext[3:3 + tm, :]
        for j in range(1, CONV_W):
            dxr = dxr + cw_ref[j:j + 1, :] * dext[3 - j:3 - j + tm, :]
        dp_ref[:, :D_RG] = dxr
        dext[tm:tm + 8, :] = dext[0:8, :]

        @pl.when(i == nt - 1)
        def _():
            fold = _head_fold()
            mask = _head_mask()
            for k in range(2):
                blockdiag = jnp.where(mask, gacc[:, k * D_RG:(k + 1) * D_RG], 0.0)
                gw_ref[k * D_RG:(k + 1) * D_RG, :] = jnp.dot(blockdiag, fold, precision=HIGHEST,
                                                             preferred_element_type=F32)

    vec = lambda n: pl.BlockSpec((1, n), lambda i: (0, 0))
    rev = lambda n: pl.BlockSpec((tm, n), lambda i: (nt - 1 - i, 0))
    halo = lambda n: pl.BlockSpec((8, n), lambda i: (jnp.maximum((nt - 1 - i) * hb - 1, 0), 0))
    return pl.pallas_call(
        body, grid=(nt,),
        in_specs=[rev(2 * D_RG), halo(2 * D_RG), rev(D_RG), halo(D_RG), rev(D_RG),
                  pl.BlockSpec((CONV_W, D_RG), lambda i: (0, 0)), vec(D_RG),
                  pl.BlockSpec((D_RG, 2 * D_RG), lambda i: (0, 0)), vec(2 * D_RG), vec(D_RG), vec(D_RG)],
        out_specs=[rev(2 * D_RG), pl.BlockSpec((CONV_W, D_RG), lambda i: (0, 0)), vec(D_RG),
                   pl.BlockSpec((2 * D_RG, RG_HEAD_DIM), lambda i: (0, 0)), vec(2 * D_RG), vec(D_RG), vec(D_RG)],
        out_shape=[jax.ShapeDtypeStruct((T, 2 * D_RG), F32), jax.ShapeDtypeStruct((CONV_W, D_RG), F32),
                   jax.ShapeDtypeStruct((1, D_RG), F32), jax.ShapeDtypeStruct((2 * D_RG, RG_HEAD_DIM), F32),
                   jax.ShapeDtypeStruct((1, 2 * D_RG), F32), jax.ShapeDtypeStruct((1, D_RG), F32),
                   jax.ShapeDtypeStruct((1, D_RG), F32)],
        scratch_shapes=[pltpu.VMEM((tm + 8, D_RG), F32), pltpu.VMEM((tm + 8, D_RG), F32),
                        pltpu.VMEM((tm, D_RG), F32), pltpu.VMEM((tm, D_RG), F32), pltpu.VMEM((tm, D_RG), F32),
                        pltpu.VMEM((D_RG, 2 * D_RG), F32), pltpu.VMEM((1, D_RG), F32), pltpu.VMEM((1, D_RG), F32)],
        name="rg_bwd", compiler_params=_params("arbitrary"),
    )(p, p, hs, hs, dy, cw, cb, wg, bg, lam, rg_g)


def _hg_bwd(p, o_all, st_all, dy, lbraw, hg_g):
    T = p.shape[0]
    n_chunks = T // CHUNK

    def body(hq_ref, hf_ref, hi_ref, hg_ref, o_ref, st_ref, dy_ref, lb_ref, g_ref,
             dp_ref, glb_ref, gg_ref, dst):
        i = pl.program_id(0)
        n = n_chunks - 1 - i

        @pl.when(i == 0)
        def _():
            dst[...] = jnp.zeros_like(dst)
            glb_ref[...] = jnp.zeros_like(glb_ref)
            gg_ref[...] = jnp.zeros_like(gg_ref)

        valid = (n * CHUNK + lax.broadcasted_iota(jnp.int32, (CHUNK, 1), 0)) >= PAD
        hq, hf, v, hg = hq_ref[...], hf_ref[...], hi_ref[...], hg_ref[...]
        lb, sq, q, sf, f, b = _hg_gates(hq, hf, lb_ref, valid)
        k = 1.0 - f
        causal = _causal()
        is_last = lax.broadcasted_iota(jnp.int32, (CHUNK, 1), 0) == CHUNK - 1
        g_ = g_ref[...]
        db_parts, dq_parts, dk_parts = [], [], []
        gg = jnp.zeros((1, HG_HEAD_DIM), F32)
        for h in range(HG_HEADS):
            sl = slice(h * HG_HEAD_DIM, (h + 1) * HG_HEAD_DIM)
            qh, kh, vh, bh = q[:, sl], k[:, sl], v[:, sl], b[:, sl]
            o = o_ref[:, sl]
            ro = _rms(o)
            no = o * ro
            hgh = hg[:, sl]
            sg = _sigmoid(hgh)
            dyh = dy_ref[:, sl]
            dp_ref[:, 3 * D_HG + h * HG_HEAD_DIM:3 * D_HG + (h + 1) * HG_HEAD_DIM] = (
                dyh * no * g_ * sg * (1.0 + hgh * (1.0 - sg)))
            dng = dyh * hgh * sg
            gg = gg + jnp.sum(dng * no, axis=0, keepdims=True)
            do = _rms_bwd(dng * g_, no, ro)
            dob = do.astype(BF16)

            st_h = st_ref[0, sl, :]
            dst_h = dst[sl, :]
            blk, b_last, eb, eq, ekh, ek, q_hat, k_til = _hg_head(qh, kh, bh)
            q_til = qh * eb
            k_hat = kh * ekh
            vb = vh.astype(BF16)
            dstb = dst_h.astype(BF16)
            qhb, ktb = q_hat.astype(BF16), k_til.astype(BF16)
            att = jnp.where(causal, _dot_nt(qhb, ktb), 0.0)
            datt = jnp.where(causal, _dot_nt(dob, vb), 0.0).astype(BF16)

            dk_hat = _dot(vb, dstb)
            dv = _dot_nt(k_hat.astype(BF16), dstb) + _dot_tn(att.astype(BF16), dob)
            dq_til = _dot(dob, st_h.astype(BF16))
            e_last = jnp.exp(b_last)
            db_last = (jnp.sum(dk_hat * k_hat, axis=0, keepdims=True)
                       + e_last * jnp.sum(dst_h * st_h, axis=0, keepdims=True))
            dst[sl, :] = dst_h * e_last + _dot_tn(dob, q_til.astype(BF16))

            dq_hat = _dot(datt, ktb)
            dk_til = _dot_tn(datt, qhb)
            dq_sel = dq_hat[:, (N_SUB - 1) * HG_HEAD_DIM:]
            for s in range(N_SUB - 2, -1, -1):
                dq_sel = jnp.where(blk == s, dq_hat[:, s * HG_HEAD_DIM:(s + 1) * HG_HEAD_DIM], dq_sel)
            dq_a = dq_sel * eq
            dk_a = dk_til[:, :HG_HEAD_DIM] * ek[0]
            for s in range(1, N_SUB):
                dk_a = dk_a + dk_til[:, s * HG_HEAD_DIM:(s + 1) * HG_HEAD_DIM] * ek[s]
            db_att = qhb.astype(F32) * dq_hat - ktb.astype(F32) * dk_til
            db = dq_til * q_til - dk_hat * k_hat
            for s in range(N_SUB):
                db = db + db_att[:, s * HG_HEAD_DIM:(s + 1) * HG_HEAD_DIM]
            db_parts.append(jnp.where(is_last, db + db_last, db))
            dq_parts.append(dq_til * eb + dq_a)
            dk_parts.append(dk_hat * ekh + dk_a)
            dp_ref[:, 2 * D_HG + h * HG_HEAD_DIM:2 * D_HG + (h + 1) * HG_HEAD_DIM] = dv

        gg_ref[...] += gg
        db = jnp.concatenate(db_parts, axis=1)
        dq = jnp.concatenate(dq_parts, axis=1)
        dk = jnp.concatenate(dk_parts, axis=1)
        dlf = jnp.where(valid, jnp.dot(_tri(False), db, precision=HIGHEST, preferred_element_type=F32), 0.0)
        dp_ref[:, :D_HG] = dq * sq * (1.0 + hq * (1.0 - sq))
        df = dlf / f - dk
        dlb = jnp.sum(df * (1.0 - sf), axis=0, keepdims=True) * lb * (1.0 - lb)
        glb_ref[0:1, :] += dlb
        glb_ref[1:2, :] += -dlb
        dp_ref[:, D_HG:2 * D_HG] = df * (1.0 - lb) * sf * (1.0 - sf)

    rev = lambda j: pl.BlockSpec((CHUNK, D_HG), lambda i: (n_chunks - 1 - i, j))
    return pl.pallas_call(
        body, grid=(n_chunks,),
        in_specs=[rev(2), rev(3), rev(4), rev(5), rev(0),
                  pl.BlockSpec((1, D_HG, HG_HEAD_DIM), lambda i: (n_chunks - 1 - i, 0, 0)), rev(1),
                  pl.BlockSpec((2, D_HG), lambda i: (0, 0)), pl.BlockSpec((1, HG_HEAD_DIM), lambda i: (0, 0))],
        out_specs=[pl.BlockSpec((CHUNK, 4 * D_HG), lambda i: (n_chunks - 1 - i, 0)),
                   pl.BlockSpec((2, D_HG), lambda i: (0, 0)), pl.BlockSpec((1, HG_HEAD_DIM), lambda i: (0, 0))],
        out_shape=[jax.ShapeDtypeStruct((T, 4 * D_HG), F32), jax.ShapeDtypeStruct((2, D_HG), F32),
                   jax.ShapeDtypeStruct((1, HG_HEAD_DIM), F32)],
        scratch_shapes=[pltpu.VMEM((D_HG, HG_HEAD_DIM), F32)],
        name="hg_bwd", compiler_params=_params("arbitrary"),
    )(p, p, p, p, o_all, st_all, dy, lbraw, hg_g)


def _in_bwd(dp_rg, dp_hg, w_in, h0, g1, dh1):
    T = h0.shape[0]
    tm = _row_tile(T, 416)

    def body(dr_ref, dh_ref, w_ref, h_ref, g_ref, d1_ref, dh0_ref, dpb_ref, gg_ref):
        i = pl.program_id(0)

        @pl.when(i == 0)
        def _():
            gg_ref[...] = jnp.zeros_like(gg_ref)

        dpb_ref[:, :2 * D_RG] = dr_ref[...].astype(BF16)
        dpb_ref[:, 2 * D_RG:] = dh_ref[...].astype(BF16)
        du = _dot_nt(dpb_ref[...], w_ref[...])
        h0_ = h_ref[...]
        r = _rms(h0_)
        n = h0_ * r
        gg_ref[...] += jnp.sum(du * n, axis=0, keepdims=True)
        dh0_ref[...] = d1_ref[...] + _rms_bwd(du * g_ref[...], n, r)

    row = lambda n: pl.BlockSpec((tm, n), lambda i: (i, 0))
    return pl.pallas_call(
        body, grid=(T // tm,),
        in_specs=[row(2 * D_RG), row(4 * D_HG), pl.BlockSpec((D_MODEL, D_IN), lambda i: (0, 0)),
                  row(D_MODEL), pl.BlockSpec((1, D_MODEL), lambda i: (0, 0)), row(D_MODEL)],
        out_specs=[row(D_MODEL), row(D_IN), pl.BlockSpec((1, D_MODEL), lambda i: (0, 0))],
        out_shape=[jax.ShapeDtypeStruct((T, D_MODEL), F32), jax.ShapeDtypeStruct((T, D_IN), BF16),
                   jax.ShapeDtypeStruct((1, D_MODEL), F32)],
        name="in_bwd", compiler_params=_params("arbitrary"),
    )(dp_rg, dp_hg, w_in, h0, g1, dh1)


def _col_tile(cols, target):
    best = None
    for t in range(128, min(cols, target) + 1, 128):
        if cols % t == 0:
            best = t
    assert best is not None, cols
    return best


def _weight_grad(a, b, name):
    T, M = a.shape
    N = b.shape[1]
    tk = _row_tile(T, 832)
    tm = _col_tile(M, 512)
    tn = _col_tile(N, 1536)

    def body(a_ref, b_ref, o_ref):
        @pl.when(pl.program_id(2) == 0)
        def _():
            o_ref[...] = jnp.zeros_like(o_ref)

        o_ref[...] += _dot_tn(a_ref[...], b_ref[...])

    return pl.pallas_call(
        body, grid=(M // tm, N // tn, T // tk),
        in_specs=[pl.BlockSpec((tk, tm), lambda m, n, k: (k, m)), pl.BlockSpec((tk, tn), lambda m, n, k: (k, n))],
        out_specs=pl.BlockSpec((tm, tn), lambda m, n, k: (m, n)),
        out_shape=jax.ShapeDtypeStruct((M, N), F32),
        name=name, compiler_params=_params("parallel", "parallel", "arbitrary"),
    )(a, b)


def _local_step(h0, target, w_in, w_out, w_gu, w_down, small):
    wg = _gate_weights(small["w_rgate"], small["w_igate"])
    bg = jnp.concatenate([small["b_rgate"], small["b_igate"]], axis=1)

    p, u = _in_proj(h0, small["mix_norm_g"], w_in)
    y_rg, hs = _rg_fwd(p, small["conv_w"], small["conv_b"], wg, bg, small["lru_lambda"], small["rg_norm_g"])
    y_hg, o_all, st_all = _hg_fwd(p, small["hg_lower_bound"], small["hg_norm_g"])
    h1, v, yb = _out_proj(h0, y_rg, y_hg, w_out, small["ffn_norm_g"])
    gate, up, act = _gate_up(v, w_gu)
    dh2, dh2b, loss, g_final = _down_loss(h1, act, w_down, small["final_norm_g"], target)

    dgu = _ffn_bwd_act(dh2b, gate, up, w_down)
    dh1, dh1b, dy, g_ffn = _ffn_bwd_in(dgu, w_gu, h1, small["ffn_norm_g"], dh2, w_out)
    dp_rg, g_cw, g_cb, g_wgate, g_bg, g_lam, g_rgn = _rg_bwd(
        p, hs, dy, small["conv_w"], small["conv_b"], wg, bg, small["lru_lambda"], small["rg_norm_g"])
    dp_hg, g_lb, g_hgn = _hg_bwd(p, o_all, st_all, dy, small["hg_lower_bound"], small["hg_norm_g"])
    dh0, dpb, g_mix = _in_bwd(dp_rg, dp_hg, w_in, h0, small["mix_norm_g"], dh1)

    grads = {
        "w_in": _weight_grad(u, dpb, "grad_w_in"),
        "w_out": _weight_grad(yb, dh1b, "grad_w_out"),
        "w_gate_up": _weight_grad(v, dgu, "grad_w_gate_up"),
        "w_down": _weight_grad(act, dh2b, "grad_w_down"),
        "mix_norm_g": g_mix, "conv_w": g_cw, "conv_b": g_cb, "w_gates": g_wgate,
        "b_rgate": g_bg[:, :D_RG], "b_igate": g_bg[:, D_RG:], "lru_lambda": g_lam, "rg_norm_g": g_rgn,
        "hg_lower_bound": g_lb, "hg_norm_g": g_hgn, "ffn_norm_g": g_ffn, "final_norm_g": g_final,
    }
    return loss, dh0, grads


ANY = pl.BlockSpec(memory_space=pl.ANY)
HALF = D_MODEL // 2

BIG = {"w_in": (D_MODEL, D_IN // N_CHIPS, True), "w_gate_up": (D_MODEL, 2 * D_FF // N_CHIPS, True),
       "w_out": (D_MODEL // N_CHIPS, D_MODEL, False), "w_down": (D_FF // N_CHIPS, D_MODEL, False)}
BIG_NAMES = tuple(BIG)
N_BIG = len(BIG_NAMES)


def _full_shape(name):
    rows, cols, by_col = BIG[name]
    return (rows, cols * N_CHIPS) if by_col else (rows * N_CHIPS, cols)


def _place():
    return lax.axis_index("x"), lax.axis_index("y"), lax.axis_index("c")


def _chip_of(x, y, r):
    fx, fy = (r + 1) >> 1, (r + 1) & 1
    return (1 - x if fx else x), (1 - y if fy else y)


def _half_of(ref, by_col, half):
    start = pl.multiple_of(half * HALF, 128)
    return ref.at[pl.ds(start, HALF), :] if by_col else ref.at[:, pl.ds(start, HALF)]


def _shard_of(ref, name, chip):
    rows, cols, by_col = BIG[name]
    if by_col:
        return ref.at[:, pl.ds(pl.multiple_of(chip * cols, 128), cols)]
    return ref.at[pl.ds(pl.multiple_of(chip * rows, 16), rows), :]


def _shard_half_of(ref, name, chip, half):
    rows, cols, by_col = BIG[name]
    start = pl.multiple_of(half * HALF, 128)
    if by_col:
        return ref.at[pl.ds(start, HALF), pl.ds(pl.multiple_of(chip * cols, 128), cols)]
    return ref.at[pl.ds(pl.multiple_of(chip * rows, 16), rows), pl.ds(start, HALF)]


def _remote(src, dst, send_sems, recv_sems, k, dev):
    return pltpu.make_async_remote_copy(src_ref=src, dst_ref=dst, send_sem=send_sems.at[k], recv_sem=recv_sems.at[k],
                                        device_id=dev, device_id_type=MESH)


def _cast_into_full(w_shard, name, chip):
    rows, cols, by_col = BIG[name]
    tr = _row_tile(rows, 352)
    if by_col:
        out_spec = pl.BlockSpec((tr, cols), lambda i, s: (i, s[0]))
    else:
        out_spec = pl.BlockSpec((tr, cols), lambda i, s: (s[0] * (rows // tr) + i, 0))

    def body(s_ref, w_ref, o_ref):
        o_ref[...] = w_ref[...].astype(BF16)

    return pl.pallas_call(
        body,
        grid_spec=pltpu.PrefetchScalarGridSpec(
            num_scalar_prefetch=1, grid=(rows // tr,), in_specs=[pl.BlockSpec((tr, cols), lambda i, s: (i, 0))],
            out_specs=out_spec),
        out_shape=jax.ShapeDtypeStruct(_full_shape(name), BF16),
        name="cast_" + name, compiler_params=_params("parallel"),
    )(chip, w_shard)


def _gather_weights(full, meta_s, cw_s):
    small = (meta_s, cw_s)

    def body(*refs):
        small_in = refs[N_BIG:N_BIG + 2]
        outs = refs[N_BIG + 2:2 * (N_BIG + 2)]
        send_sems, recv_sems, local_sems = refs[2 * (N_BIG + 2):]
        x, y, c = _place()
        chip = 2 * x + y
        sibling = (x, y, 1 - c)
        others = [_chip_of(x, y, r) for r in range(3)]

        def small_block(full_ref, a, q):
            cols = small[a].shape[1]
            return full_ref.at[:, pl.ds(pl.multiple_of(q * cols, 128), cols)]

        local = [pltpu.make_async_copy(small_in[a], small_block(outs[N_BIG + a], a, chip), local_sems.at[a])
                 for a in range(2)]
        for cp in local:
            cp.start()

        sends = []
        for a, name in enumerate(BIG_NAMES):
            mine = _shard_half_of(outs[a], name, chip, c)
            for r, (qx, qy) in enumerate(others):
                sends.append(_remote(mine, mine, send_sems, recv_sems, 6 * a + r, (qx, qy, c)))
        for a in range(2):
            for r, (qx, qy) in enumerate(others):
                sends.append(_remote(small_in[a], small_block(outs[N_BIG + a], a, chip),
                                     send_sems, recv_sems, 6 * N_BIG + 3 * a + r, (qx, qy, c)))
        for cp in sends:
            cp.start()

        forwards = []
        for a, name in enumerate(BIG_NAMES):
            for r, (qx, qy) in enumerate(others):
                landed = _shard_half_of(outs[a], name, 2 * qx + qy, c)
                _remote(landed, landed, send_sems, recv_sems, 6 * a + r, (qx, qy, c)).wait_recv()
                fwd = _remote(landed, landed, send_sems, recv_sems, 6 * a + 3 + r, sibling)
                fwd.start()
                forwards.append(fwd)
        for a in range(2):
            for r, (qx, qy) in enumerate(others):
                landed = small_block(outs[N_BIG + a], a, 2 * qx + qy)
                _remote(landed, landed, send_sems, recv_sems, 6 * N_BIG + 3 * a + r, (qx, qy, c)).wait_recv()
        for a, name in enumerate(BIG_NAMES):
            for r, (qx, qy) in enumerate(others):
                landed = _shard_half_of(outs[a], name, 2 * qx + qy, 1 - c)
                _remote(landed, landed, send_sems, recv_sems, 6 * a + 3 + r, sibling).wait_recv()
        for cp in sends + forwards:
            cp.wait_send()
        for cp in local:
            cp.wait()

    n_sems = 6 * N_BIG + 6
    out_shape = [jax.ShapeDtypeStruct(_full_shape(n), BF16) for n in BIG_NAMES]
    out_shape += [jax.ShapeDtypeStruct((s.shape[0], s.shape[1] * N_CHIPS), F32) for s in small]
    return pl.pallas_call(
        body, in_specs=[ANY] * (N_BIG + 2), out_specs=[ANY] * (N_BIG + 2), out_shape=out_shape,
        input_output_aliases={a: a for a in range(N_BIG)},
        scratch_shapes=[pltpu.SemaphoreType.DMA((n_sems,)), pltpu.SemaphoreType.DMA((n_sems,)),
                        pltpu.SemaphoreType.DMA((2,))],
        name="gather_weights",
    )(*[full[n] for n in BIG_NAMES], meta_s, cw_s)


def _exchange_halves(grads):
    def body(*refs):
        ins, outs = refs[:N_BIG], refs[N_BIG:2 * N_BIG]
        send_sems, recv_sems = refs[2 * N_BIG:]
        x, y, c = _place()
        copies = []
        for a, name in enumerate(BIG_NAMES):
            copies.append(_remote(_half_of(ins[a], BIG[name][2], 1 - c), outs[a], send_sems, recv_sems, a,
                                  (x, y, 1 - c)))
        for cp in copies:
            cp.start()
        for cp in copies:
            cp.wait()

    def half_shape(name):
        r, c_ = _full_shape(name)
        return (HALF, c_) if BIG[name][2] else (r, HALF)

    return pl.pallas_call(
        body, in_specs=[ANY] * N_BIG, out_specs=[ANY] * N_BIG,
        out_shape=[jax.ShapeDtypeStruct(half_shape(n), F32) for n in BIG_NAMES],
        scratch_shapes=[pltpu.SemaphoreType.DMA((N_BIG,)), pltpu.SemaphoreType.DMA((N_BIG,))],
        name="exchange_halves",
    )(*[grads[n] for n in BIG_NAMES])


def _chip_sum(g, got, name, core):
    by_col = BIG[name][2]
    rows, cols = got.shape
    if by_col:
        tr = 128
        g_spec = pl.BlockSpec((tr, cols), lambda i, s: (s[0] * (HALF // tr) + i, 0))
    else:
        tr = _row_tile(rows, 512)
        g_spec = pl.BlockSpec((tr, HALF), lambda i, s: (i, s[0]))
    blk = pl.BlockSpec((tr, cols), lambda i, s: (i, 0))

    def body(s_ref, g_ref, r_ref, f_ref, b_ref):
        t = g_ref[...] + r_ref[...]
        f_ref[...] = t
        b_ref[...] = t.astype(BF16)

    return pl.pallas_call(
        body,
        grid_spec=pltpu.PrefetchScalarGridSpec(num_scalar_prefetch=1, grid=(rows // tr,), in_specs=[g_spec, blk],
                                               out_specs=[blk, blk]),
        out_shape=[jax.ShapeDtypeStruct(got.shape, F32), jax.ShapeDtypeStruct(got.shape, BF16)],
        name="chip_sum_" + name, compiler_params=_params("parallel"),
    )(core, g, got)


def _piece_shape(name):
    rows, cols, by_col = BIG[name]
    return (HALF, cols) if by_col else (rows, HALF)


def _send_chip_sums(sums):
    def body(*refs):
        ins, outs = refs[:N_BIG], refs[N_BIG:2 * N_BIG]
        send_sems, recv_sems = refs[2 * N_BIG:]
        x, y, c = _place()
        copies = []
        for a, name in enumerate(BIG_NAMES):
            for r in range(3):
                qx, qy = _chip_of(x, y, r)
                copies.append(_remote(_shard_of(ins[a], name, 2 * qx + qy), outs[a].at[r], send_sems, recv_sems,
                                      3 * a + r, (qx, qy, c)))
        for cp in copies:
            cp.start()
        for cp in copies:
            cp.wait()

    return pl.pallas_call(
        body, in_specs=[ANY] * N_BIG, out_specs=[ANY] * N_BIG,
        out_shape=[jax.ShapeDtypeStruct((3,) + _piece_shape(n), BF16) for n in BIG_NAMES],
        scratch_shapes=[pltpu.SemaphoreType.DMA((3 * N_BIG,)), pltpu.SemaphoreType.DMA((3 * N_BIG,))],
        name="send_chip_sums",
    )(*[sums[n] for n in BIG_NAMES])


def _total(own, got, name, chip_core):
    rows, cols, by_col = BIG[name]
    pr, pc = _piece_shape(name)
    tr = _row_tile(pr, 352)
    if by_col:
        own_spec = pl.BlockSpec((tr, pc), lambda i, s: (i, s[0]))
    else:
        own_spec = pl.BlockSpec((tr, pc), lambda i, s: (s[0] * (pr // tr) + i, 0))
    got_spec = lambda r: pl.BlockSpec((None, tr, pc), lambda i, s: (r, i, 0))
    if by_col:
        out_spec = pl.BlockSpec((tr, pc), lambda i, s: (s[1] * (pr // tr) + i, 0))
    else:
        out_spec = pl.BlockSpec((tr, pc), lambda i, s: (i, s[1]))

    def body(s_ref, o_ref, a_ref, b_ref, c_ref, t_ref):
        t_ref[...] = ((o_ref[...] + a_ref[...].astype(F32)) + b_ref[...].astype(F32)) + c_ref[...].astype(F32)

    return pl.pallas_call(
        body,
        grid_spec=pltpu.PrefetchScalarGridSpec(
            num_scalar_prefetch=1, grid=(pr // tr,), in_specs=[own_spec, got_spec(0), got_spec(1), got_spec(2)],
            out_specs=out_spec),
        out_shape=jax.ShapeDtypeStruct((rows, cols), F32),
        name="total_" + name, compiler_params=_params("parallel"),
    )(chip_core, own, got, got, got)


def _share_totals(totals):
    def body(*refs):
        outs = refs[N_BIG:2 * N_BIG]
        send_sems, recv_sems = refs[2 * N_BIG:]
        x, y, c = _place()
        copies = []
        for a, name in enumerate(BIG_NAMES):
            mine = _half_of(outs[a], BIG[name][2], c)
            copies.append(_remote(mine, mine, send_sems, recv_sems, a, (x, y, 1 - c)))
        for cp in copies:
            cp.start()
        for a, name in enumerate(BIG_NAMES):
            theirs = _half_of(outs[a], BIG[name][2], 1 - c)
            _remote(theirs, theirs, send_sems, recv_sems, a, (x, y, 1 - c)).wait_recv()
        for cp in copies:
            cp.wait_send()

    return pl.pallas_call(
        body, in_specs=[ANY] * N_BIG, out_specs=[ANY] * N_BIG,
        out_shape=[jax.ShapeDtypeStruct(BIG[n][:2], F32) for n in BIG_NAMES],
        input_output_aliases={a: a for a in range(N_BIG)},
        scratch_shapes=[pltpu.SemaphoreType.DMA((N_BIG,)), pltpu.SemaphoreType.DMA((N_BIG,))],
        name="share_totals",
    )(*[totals[n] for n in BIG_NAMES])


VEC_ROWS = 32
VEC_ROW = {"mix_norm_g": 0, "conv_b": 1, "b_rgate": 2, "b_igate": 3, "lru_lambda": 4, "rg_norm_g": 5,
           "hg_lower_bound": 6, "hg_norm_g": 8, "ffn_norm_g": 9, "final_norm_g": 10, "loss": 11,
           "conv_w": 12, "meta_tokens": 16}
N_DEV = 8


def _all_reduce_small(pieces, gates):
    names = list(pieces)

    def body(*refs):
        ins = refs[:len(names)]
        g_ref, vec_ref, gsum_ref, send_v, got_v, got_g, send_sems, recv_sems = refs[len(names):]
        x, y, c = _place()
        me = 4 * x + 2 * y + c
        send_v[...] = jnp.zeros_like(send_v)
        for name, ref in zip(names, ins):
            nr, w = ref.shape
            send_v[VEC_ROW[name]:VEC_ROW[name] + nr, 0:w] = ref[...]
        got_v[me] = send_v[...]
        got_g[me] = g_ref[...]
        copies = []
        for r in range(1, N_DEV):
            peer = ((1 - x if r & 4 else x), (1 - y if r & 2 else y), (1 - c if r & 1 else c))
            copies.append(_remote(send_v, got_v.at[me], send_sems, recv_sems, r - 1, peer))
            copies.append(_remote(g_ref, got_g.at[me], send_sems, recv_sems, N_DEV - 1 + r - 1, peer))
        for cp in copies:
            cp.start()
        for cp in copies:
            cp.wait()
        vec = got_v[0]
        gs = got_g[0]
        for s in range(1, N_DEV):
            vec = vec + got_v[s]
            gs = gs + got_g[s]
        vec_ref[...] = vec
        gsum_ref[...] = gs

    vmem = pl.BlockSpec(memory_space=pltpu.VMEM)
    return pl.pallas_call(
        body, in_specs=[vmem] * (len(names) + 1), out_specs=[vmem, vmem],
        out_shape=[jax.ShapeDtypeStruct((VEC_ROWS, D_MODEL), F32), jax.ShapeDtypeStruct(gates.shape, F32)],
        scratch_shapes=[pltpu.VMEM((VEC_ROWS, D_MODEL), F32), pltpu.VMEM((N_DEV, VEC_ROWS, D_MODEL), F32),
                        pltpu.VMEM((N_DEV,) + gates.shape, F32),
                        pltpu.SemaphoreType.DMA((2 * (N_DEV - 1),)), pltpu.SemaphoreType.DMA((2 * (N_DEV - 1),))],
        name="all_reduce_small",
    )(*[pieces[n] for n in names], gates)


def _adamw_math(w, g, m, v):
    m = ADAM_B1 * m + (1.0 - ADAM_B1) * g
    v = ADAM_B2 * v + (1.0 - ADAM_B2) * (g * g)
    m_hat = m / (1.0 - ADAM_B1 ** ADAM_STEP)
    v_hat = v / (1.0 - ADAM_B2 ** ADAM_STEP)
    delta = -ADAM_LR * (m_hat / (jnp.sqrt(v_hat) + ADAM_EPS) + ADAM_WD * w)
    return delta, m, v


def _adamw_big(w, g, m, v, name):
    rows, cols = w.shape
    tr = _row_tile(rows, 352)

    def body(w_ref, g_ref, m_ref, v_ref, d_ref, nm_ref, nv_ref):
        d_ref[...], nm_ref[...], nv_ref[...] = _adamw_math(w_ref[...], g_ref[...], m_ref[...], v_ref[...])

    blk = pl.BlockSpec((tr, cols), lambda i: (i, 0))
    return pl.pallas_call(
        body, grid=(rows // tr,), in_specs=[blk] * 4, out_specs=[blk] * 3,
        out_shape=[jax.ShapeDtypeStruct(w.shape, F32)] * 3,
        name="adamw_" + name, compiler_params=_params("parallel"),
    )(w, g, m, v)


SMALL = {"meta_tokens": (N_META, D_MODEL // N_CHIPS), "mix_norm_g": (1, D_MODEL), "conv_w": (CONV_W, D_RG // N_CHIPS),
         "conv_b": (1, D_RG), "w_rgate": (D_RG, RG_HEAD_DIM), "b_rgate": (1, D_RG), "w_igate": (D_RG, RG_HEAD_DIM),
         "b_igate": (1, D_RG), "lru_lambda": (1, D_RG), "rg_norm_g": (1, D_RG), "hg_lower_bound": (2, D_HG),
         "hg_norm_g": (1, HG_HEAD_DIM), "ffn_norm_g": (1, D_MODEL), "final_norm_g": (1, D_MODEL)}
SMALL_NAMES = tuple(SMALL)
SHARDED_SMALL = ("meta_tokens", "conv_w")


def _adamw_small(vec, gates, w, m, v):
    n = len(SMALL_NAMES)

    def body(*refs):
        vec_ref, gates_ref = refs[:2]
        w_refs, m_refs, v_refs = refs[2:2 + n], refs[2 + n:2 + 2 * n], refs[2 + 2 * n:2 + 3 * n]
        outs = refs[2 + 3 * n:]
        loss_ref = outs[0]
        x, y, _ = _place()
        chip = 2 * x + y
        loss_ref[...] = vec_ref[VEC_ROW["loss"]:VEC_ROW["loss"] + 1, 0:1]

        def update(k, g):
            g_ref, d_ref, nm_ref, nv_ref = outs[1 + 4 * k:5 + 4 * k]
            g_ref[...] = g
            d_ref[...], nm_ref[...], nv_ref[...] = _adamw_math(w_refs[k][...], g, m_refs[k][...], v_refs[k][...])

        for k, name in enumerate(SMALL_NAMES):
            nr, w_ = SMALL[name]
            if name == "w_rgate":
                update(k, gates_ref[0:D_RG, :])
            elif name == "w_igate":
                update(k, gates_ref[D_RG:2 * D_RG, :])
            elif name in SHARDED_SMALL:
                r0 = VEC_ROW[name]
                for q in range(N_CHIPS):
                    @pl.when(chip == q)
                    def _(k=k, r0=r0, nr=nr, w_=w_, q=q):
                        update(k, vec_ref[r0:r0 + nr, q * w_:(q + 1) * w_])
            else:
                r0 = VEC_ROW[name]
                update(k, vec_ref[r0:r0 + nr, 0:w_])

    vmem = pl.BlockSpec(memory_space=pltpu.VMEM)
    out_shape = [jax.ShapeDtypeStruct((1, 1), F32)]
    for name in SMALL_NAMES:
        out_shape += [jax.ShapeDtypeStruct(SMALL[name], F32)] * 4
    outs = pl.pallas_call(
        body, in_specs=[vmem] * (2 + 3 * n), out_specs=[vmem] * len(out_shape), out_shape=out_shape,
        name="adamw_small",
    )(vec, gates, *[w[k] for k in SMALL_NAMES], *[m[k] for k in SMALL_NAMES], *[v[k] for k in SMALL_NAMES])
    loss = outs[0]
    res = {name: tuple(outs[1 + 4 * k:5 + 4 * k]) for k, name in enumerate(SMALL_NAMES)}
    return loss, res


WEIGHT_NAMES = ("meta_tokens", "mix_norm_g", "w_in", "conv_w", "conv_b", "w_rgate", "b_rgate", "w_igate", "b_igate",
                "lru_lambda", "rg_norm_g", "hg_lower_bound", "hg_norm_g", "w_out", "ffn_norm_g", "w_gate_up", "w_down",
                "final_norm_g")


def _to_2d(name, a):
    if name in BIG:
        return a.reshape(BIG[name][:2])
    return a.reshape(SMALL[name])


def kernel(x, meta_tokens, mix_norm_g, w_in, conv_w, conv_b, w_rgate, b_rgate, w_igate, b_igate, lru_lambda, rg_norm_g, hg_lower_bound, hg_norm_g, w_out, ffn_norm_g, w_gate_up, w_down, final_norm_g, loss_target, m_meta_tokens, m_mix_norm_g, m_w_in, m_conv_w, m_conv_b, m_w_rgate, m_b_rgate, m_w_igate, m_b_igate, m_lru_lambda, m_rg_norm_g, m_hg_lower_bound, m_hg_norm_g, m_w_out, m_ffn_norm_g, m_w_gate_up, m_w_down, m_final_norm_g, v_meta_tokens, v_mix_norm_g, v_w_in, v_conv_w, v_conv_b, v_w_rgate, v_b_rgate, v_w_igate, v_b_igate, v_lru_lambda, v_rg_norm_g, v_hg_lower_bound, v_hg_norm_g, v_w_out, v_ffn_norm_g, v_w_gate_up, v_w_down, v_final_norm_g):
    w_raw = dict(zip(WEIGHT_NAMES, (meta_tokens, mix_norm_g, w_in, conv_w, conv_b, w_rgate, b_rgate, w_igate, b_igate,
                                    lru_lambda, rg_norm_g, hg_lower_bound, hg_norm_g, w_out, ffn_norm_g, w_gate_up,
                                    w_down, final_norm_g)))
    m_raw = dict(zip(WEIGHT_NAMES, (m_meta_tokens, m_mix_norm_g, m_w_in, m_conv_w, m_conv_b, m_w_rgate, m_b_rgate,
                                    m_w_igate, m_b_igate, m_lru_lambda, m_rg_norm_g, m_hg_lower_bound, m_hg_norm_g,
                                    m_w_out, m_ffn_norm_g, m_w_gate_up, m_w_down, m_final_norm_g)))
    v_raw = dict(zip(WEIGHT_NAMES, (v_meta_tokens, v_mix_norm_g, v_w_in, v_conv_w, v_conv_b, v_w_rgate, v_b_rgate,
                                    v_w_igate, v_b_igate, v_lru_lambda, v_rg_norm_g, v_hg_lower_bound, v_hg_norm_g,
                                    v_w_out, v_ffn_norm_g, v_w_gate_up, v_w_down, v_final_norm_g)))
    w = {k: _to_2d(k, a) for k, a in w_raw.items()}
    m = {k: _to_2d(k, a) for k, a in m_raw.items()}
    v = {k: _to_2d(k, a) for k, a in v_raw.items()}

    x_i, y_i, c_i = _place()
    core = jnp.reshape(c_i, (1,)).astype(jnp.int32)
    chip = jnp.reshape(2 * x_i + y_i, (1,)).astype(jnp.int32)
    chip_core = jnp.concatenate([chip, core])

    placed = {k: _cast_into_full(w[k], k, chip) for k in BIG_NAMES}
    *full, meta_full, cw_full = _gather_weights(placed, w["meta_tokens"], w["conv_w"])
    full = dict(zip(BIG_NAMES, full))

    seq = x.shape[1]
    h0 = jnp.concatenate([jnp.zeros((PAD, D_MODEL), F32), meta_full, x[0]], axis=0)
    target = jnp.concatenate([jnp.zeros((PAD + N_META, D_MODEL), F32), loss_target[0]], axis=0)
    small = {k: w[k] for k in SMALL_NAMES if k not in SHARDED_SMALL}
    small["conv_w"] = cw_full
    loss, dh0, grads = _local_step(h0, target, full["w_in"], full["w_out"], full["w_gate_up"], full["w_down"], small)

    got = dict(zip(BIG_NAMES, _exchange_halves(grads)))
    sums = {n: _chip_sum(grads[n], got[n], n, core) for n in BIG_NAMES}
    arrived = dict(zip(BIG_NAMES, _send_chip_sums({n: sums[n][1] for n in BIG_NAMES})))
    totals = {n: _total(sums[n][0], arrived[n], n, chip_core) for n in BIG_NAMES}
    g_big = dict(zip(BIG_NAMES, _share_totals(totals)))

    pieces = {k: grads[k] for k in VEC_ROW if k not in ("loss", "meta_tokens")}
    pieces["loss"] = loss
    pieces["meta_tokens"] = dh0[PAD:PAD + N_META]
    vec, gates = _all_reduce_small(pieces, grads["w_gates"])
    loss_sum, res = _adamw_small(vec, gates, w, m, v)
    for n in BIG_NAMES:
        res[n] = (g_big[n],) + tuple(_adamw_big(w[n], g_big[n], m[n], v[n], n))

    grad_x = dh0[PAD + N_META:].reshape(1, seq, D_MODEL)
    out = [loss_sum.reshape(()), grad_x]
    for j in range(4):
        out += [res[n][j].reshape(w_raw[n].shape) for n in WEIGHT_NAMES]
    return tuple(out)
```

```python
import functools
import math

import jax
import jax.numpy as jnp
from jax import lax
from jax.experimental import pallas as pl
from jax.experimental.pallas import tpu as pltpu
from jax.experimental.pallas import tpu_sc as plsc

F32 = jnp.float32
BF16 = jnp.bfloat16
HIGHEST = lax.Precision.HIGHEST
MESH = pl.DeviceIdType.MESH

D_MODEL = 1024
D_RG = 512
RG_HEAD_DIM = 64
D_HG = 512
HG_HEAD_DIM = 128
HG_HEADS = 4
CHUNK = 64
SUB = 16
N_SUB = CHUNK // SUB
N_META = 16
PAD = CHUNK - N_META
D_IN = 3072
D_FF = 2816
CONV_W = 4
LRU_C = 8.0
EPS = 1e-6
EXP_CLAMP = 80.0
GELU_C = math.sqrt(2.0 / math.pi)
GELU_A = 0.044715
N_CHIPS = 4

ADAM_LR = 0.001
ADAM_B1 = 0.9
ADAM_B2 = 0.999
ADAM_EPS = 1e-08
ADAM_WD = 0.01
ADAM_STEP = 10

VMEM_LIMIT = 56 * 1024 * 1024


def _params(*sem):
    return pltpu.CompilerParams(dimension_semantics=sem, vmem_limit_bytes=VMEM_LIMIT)


def _row_tile(rows, target):
    best = None
    for t in range(16, min(rows, target) + 1, 16):
        if rows % t == 0:
            best = t
    assert best is not None, rows
    return best


def _sigmoid(x):
    return 1.0 / (1.0 + jnp.exp(-x))


def _dot(a, b):
    return jnp.dot(a, b, preferred_element_type=F32)


def _dot_nt(a, b):
    return lax.dot_general(a, b, (((1,), (1,)), ((), ())), preferred_element_type=F32)


def _dot_tn(a, b):
    return lax.dot_general(a, b, (((0,), (0,)), ((), ())), preferred_element_type=F32)


def _rms(x):
    return lax.rsqrt(jnp.mean(x * x, axis=-1, keepdims=True) + EPS)


def _rms_bwd(dn, n, r):
    return r * (dn - n * jnp.mean(dn * n, axis=-1, keepdims=True))


def _gelu_parts(x):
    t = jnp.tanh(GELU_C * (x + GELU_A * x * x * x))
    g = 0.5 * x * (1.0 + t)
    dg = 0.5 * (1.0 + t) + 0.5 * x * (1.0 - t * t) * GELU_C * (1.0 + 3.0 * GELU_A * x * x)
    return g, dg


def _softplus_neg(lam):
    e = jnp.exp(-jnp.abs(lam))
    w = 1.0 + e
    log1p = jnp.where(w == 1.0, e, jnp.log(w) * e / (w - 1.0))
    return jnp.maximum(-lam, 0.0) + log1p


def _head_mask():
    r = lax.broadcasted_iota(jnp.int32, (D_RG, D_RG), 0) // RG_HEAD_DIM
    c = lax.broadcasted_iota(jnp.int32, (D_RG, D_RG), 1) // RG_HEAD_DIM
    return r == c


def _head_fold():
    r = lax.broadcasted_iota(jnp.int32, (D_RG, RG_HEAD_DIM), 0) % RG_HEAD_DIM
    c = lax.broadcasted_iota(jnp.int32, (D_RG, RG_HEAD_DIM), 1)
    return (r == c).astype(F32)


def _gate_weights(w_r, w_i):
    def body(wr_ref, wi_ref, o_ref):
        fold = _head_fold()
        mask = _head_mask()
        for k, ref in enumerate((wr_ref, wi_ref)):
            full = lax.dot_general(ref[...], fold, (((1,), (1,)), ((), ())),
                                   precision=HIGHEST, preferred_element_type=F32)
            o_ref[:, k * D_RG:(k + 1) * D_RG] = jnp.where(mask, full, 0.0).astype(BF16)

    return pl.pallas_call(
        body, out_shape=jax.ShapeDtypeStruct((D_RG, 2 * D_RG), BF16), name="gate_weights",
    )(w_r, w_i)


def _in_proj(h0, g1, w_in):
    T = h0.shape[0]
    tm = _row_tile(T, 416)

    def body(h_ref, g_ref, w_ref, p_ref, u_ref):
        h = h_ref[...]
        u = (h * _rms(h) * g_ref[...]).astype(BF16)
        u_ref[...] = u
        p_ref[...] = _dot(u, w_ref[...])

    return pl.pallas_call(
        body, grid=(T // tm,),
        in_specs=[pl.BlockSpec((tm, D_MODEL), lambda i: (i, 0)),
                  pl.BlockSpec((1, D_MODEL), lambda i: (0, 0)),
                  pl.BlockSpec((D_MODEL, D_IN), lambda i: (0, 0))],
        out_specs=[pl.BlockSpec((tm, D_IN), lambda i: (i, 0)),
                   pl.BlockSpec((tm, D_MODEL), lambda i: (i, 0))],
        out_shape=[jax.ShapeDtypeStruct((T, D_IN), F32), jax.ShapeDtypeStruct((T, D_MODEL), BF16)],
        name="in_proj", compiler_params=_params("parallel"),
    )(h0, g1, w_in)


def _scan_block_fwd(A, B, rowi):
    for d in (1, 2, 4):
        a_sh = pltpu.roll(A, d, axis=0)
        b_sh = pltpu.roll(B, d, axis=0)
        m = rowi >= d
        B = jnp.where(m, A * b_sh + B, B)
        A = jnp.where(m, A * a_sh, A)
    return A, B


def _scan_block_bwd(A, B, rowi):
    for d in (1, 2, 4):
        a_sh = pltpu.roll(A, 8 - d, axis=0)
        b_sh = pltpu.roll(B, 8 - d, axis=0)
        m = rowi < 8 - d
        B = jnp.where(m, A * b_sh + B, B)
        A = jnp.where(m, A * a_sh, A)
    return A, B


def _rg_gates(xc, w_ref, bg_ref, lam):
    pre = _dot(xc.astype(BF16), w_ref[...]) + bg_ref[...]
    r = _sigmoid(pre[:, :D_RG])
    ig = _sigmoid(pre[:, D_RG:])
    sp = _softplus_neg(lam)
    la = -LRU_C * sp * r
    a = jnp.exp(la)
    th = jnp.tanh(la)
    m = jnp.sqrt(-2.0 * th / (1.0 - th))
    return r, ig, sp, a, m


def _conv(ext, cw_ref, cb_ref, tm):
    xc = cb_ref[...] + cw_ref[0:1, :] * ext[8 - 3:8 - 3 + tm, :]
    for j in range(1, CONV_W):
        xc = xc + cw_ref[j:j + 1, :] * ext[8 - 3 + j:8 - 3 + j + tm, :]
    return xc


def _rg_fwd(p, cw, cb, wg, bg, lam, rg_g):
    T = p.shape[0]
    tm = _row_tile(T, 416)

    def body(xg_ref, cw_ref, cb_ref, w_ref, bg_ref, lam_ref, g_ref, y_ref, h_ref, ext, a_s, b_s, carry):
        i = pl.program_id(0)

        @pl.when(i == 0)
        def _():
            ext[0:8, :] = jnp.zeros((8, D_RG), F32)
            carry[...] = jnp.zeros((1, D_RG), F32)

        ext[8:8 + tm, :] = xg_ref[:, :D_RG]
        xc = _conv(ext, cw_ref, cb_ref, tm)
        r, ig, sp, a, m = _rg_gates(xc, w_ref, bg_ref, lam_ref[...])
        row = i * tm + lax.broadcasted_iota(jnp.int32, (tm, 1), 0)
        a_s[...] = a
        b_s[...] = jnp.where(row >= PAD, m * ig * xc, 0.0)
        rowi = lax.broadcasted_iota(jnp.int32, (8, D_RG), 0)

        def blk(j, c):
            o = pl.multiple_of(j * 8, 8)
            A, B = _scan_block_fwd(a_s[pl.ds(o, 8), :], b_s[pl.ds(o, 8), :], rowi)
            h = B + A * c
            h_ref[pl.ds(o, 8), :] = h
            return h[7:8, :]

        carry[...] = lax.fori_loop(0, tm // 8, blk, carry[...])
        ext[0:8, :] = ext[tm:tm + 8, :]
        g, _ = _gelu_parts(xg_ref[:, D_RG:])
        yy = g * h_ref[...]
        y_ref[...] = yy * _rms(yy) * g_ref[...]

    vec = lambda n: pl.BlockSpec((1, n), lambda i: (0, 0))
    return pl.pallas_call(
        body, grid=(T // tm,),
        in_specs=[pl.BlockSpec((tm, 2 * D_RG), lambda i: (i, 0)),
                  pl.BlockSpec((CONV_W, D_RG), lambda i: (0, 0)), vec(D_RG),
                  pl.BlockSpec((D_RG, 2 * D_RG), lambda i: (0, 0)), vec(2 * D_RG), vec(D_RG), vec(D_RG)],
        out_specs=[pl.BlockSpec((tm, D_RG), lambda i: (i, 0)), pl.BlockSpec((tm, D_RG), lambda i: (i, 0))],
        out_shape=[jax.ShapeDtypeStruct((T, D_RG), F32), jax.ShapeDtypeStruct((T, D_RG), F32)],
        scratch_shapes=[pltpu.VMEM((tm + 8, D_RG), F32), pltpu.VMEM((tm, D_RG), F32),
                        pltpu.VMEM((tm, D_RG), F32), pltpu.VMEM((1, D_RG), F32)],
        name="rg_fwd", compiler_params=_params("arbitrary"),
    )(p, cw, cb, wg, bg, lam, rg_g)


def _tri(lower):
    r = lax.broadcasted_iota(jnp.int32, (CHUNK, CHUNK), 0)
    c = lax.broadcasted_iota(jnp.int32, (CHUNK, CHUNK), 1)
    return ((c <= r) if lower else (c >= r)).astype(F32)


def _hg_gates(hq, hf, lbraw_ref, valid):
    lb = _sigmoid(lbraw_ref[0:1, :] - lbraw_ref[1:2, :])
    sq = _sigmoid(hq)
    q = hq * sq
    sf = _sigmoid(hf)
    f = lb + (1.0 - lb) * sf
    lf = jnp.where(valid, jnp.log(f), 0.0)
    b = jnp.dot(_tri(True), lf, precision=HIGHEST, preferred_element_type=F32)
    return lb, sq, q, sf, f, b


def _hg_head(qh, kh, bh):
    blk = lax.broadcasted_iota(jnp.int32, (CHUNK, 1), 0) // SUB
    b_last = bh[CHUNK - 1:CHUNK, :]
    refs = [bh[SUB * s:SUB * s + 1, :] for s in range(N_SUB)]
    r_sel = refs[N_SUB - 1]
    for s in range(N_SUB - 2, -1, -1):
        r_sel = jnp.where(blk == s, refs[s], r_sel)
    eb = jnp.exp(bh)
    eq = jnp.exp(bh - r_sel)
    ekh = jnp.exp(b_last - bh)
    ek = [jnp.exp(jnp.minimum(refs[s] - bh, EXP_CLAMP)) for s in range(N_SUB)]
    qe = qh * eq
    q_hat = jnp.concatenate([jnp.where(blk == s, qe, 0.0) for s in range(N_SUB)], axis=1)
    k_til = jnp.concatenate([kh * ek[s] for s in range(N_SUB)], axis=1)
    return blk, b_last, eb, eq, ekh, ek, q_hat, k_til


def _causal():
    r = lax.broadcasted_iota(jnp.int32, (CHUNK, CHUNK), 0)
    c = lax.broadcasted_iota(jnp.int32, (CHUNK, CHUNK), 1)
    return r >= c


def _hg_fwd(p, lbraw, hg_g):
    T = p.shape[0]
    n_chunks = T // CHUNK

    def body(hq_ref, hf_ref, hi_ref, hg_ref, lb_ref, g_ref, y_ref, o_ref, st_all_ref, st):
        n = pl.program_id(0)

        @pl.when(n == 0)
        def _():
            st[...] = jnp.zeros_like(st)

        valid = (n * CHUNK + lax.broadcasted_iota(jnp.int32, (CHUNK, 1), 0)) >= PAD
        hq, hf, v, hg = hq_ref[...], hf_ref[...], hi_ref[...], hg_ref[...]
        lb, sq, q, sf, f, b = _hg_gates(hq, hf, lb_ref, valid)
        k = 1.0 - f
        st_all_ref[0] = st[...]
        causal = _causal()
        for h in range(HG_HEADS):
            sl = slice(h * HG_HEAD_DIM, (h + 1) * HG_HEAD_DIM)
            qh, kh, vh, bh = q[:, sl], k[:, sl], v[:, sl], b[:, sl]
            st_h = st[sl, :]
            _, b_last, eb, _, ekh, _, q_hat, k_til = _hg_head(qh, kh, bh)
            vb = vh.astype(BF16)
            inter = _dot_nt((qh * eb).astype(BF16), st_h.astype(BF16))
            att = jnp.where(causal, _dot_nt(q_hat.astype(BF16), k_til.astype(BF16)), 0.0)
            o = inter + _dot(att.astype(BF16), vb)
            st[sl, :] = st_h * jnp.exp(b_last) + _dot_tn(vb, (kh * ekh).astype(BF16))
            o_ref[:, sl] = o
            hgh = hg[:, sl]
            y_ref[:, sl] = o * _rms(o) * g_ref[...] * (hgh * _sigmoid(hgh))

    col = lambda j: pl.BlockSpec((CHUNK, D_HG), lambda n: (n, j))
    return pl.pallas_call(
        body, grid=(n_chunks,),
        in_specs=[col(2), col(3), col(4), col(5),
                  pl.BlockSpec((2, D_HG), lambda n: (0, 0)), pl.BlockSpec((1, HG_HEAD_DIM), lambda n: (0, 0))],
        out_specs=[pl.BlockSpec((CHUNK, D_HG), lambda n: (n, 0)), pl.BlockSpec((CHUNK, D_HG), lambda n: (n, 0)),
                   pl.BlockSpec((1, D_HG, HG_HEAD_DIM), lambda n: (n, 0, 0))],
        out_shape=[jax.ShapeDtypeStruct((T, D_HG), F32), jax.ShapeDtypeStruct((T, D_HG), F32),
                   jax.ShapeDtypeStruct((n_chunks, D_HG, HG_HEAD_DIM), F32)],
        scratch_shapes=[pltpu.VMEM((D_HG, HG_HEAD_DIM), F32)],
        name="hg_fwd", compiler_params=_params("arbitrary"),
    )(p, p, p, p, lbraw, hg_g)


def _out_proj(h0, y_rg, y_hg, w_out, g2):
    T = h0.shape[0]
    tm = _row_tile(T, 832)

    def body(h_ref, yr_ref, yh_ref, w_ref, g_ref, h1_ref, v_ref, y_ref):
        y_ref[:, :D_RG] = yr_ref[...].astype(BF16)
        y_ref[:, D_RG:] = yh_ref[...].astype(BF16)
        h1 = h_ref[...] + _dot(y_ref[...], w_ref[...])
        h1_ref[...] = h1
        v_ref[...] = (h1 * _rms(h1) * g_ref[...]).astype(BF16)

    row = lambda n: pl.BlockSpec((tm, n), lambda i: (i, 0))
    return pl.pallas_call(
        body, grid=(T // tm,),
        in_specs=[row(D_MODEL), row(D_RG), row(D_HG), pl.BlockSpec((D_MODEL, D_MODEL), lambda i: (0, 0)),
                  pl.BlockSpec((1, D_MODEL), lambda i: (0, 0))],
        out_specs=[row(D_MODEL), row(D_MODEL), row(D_MODEL)],
        out_shape=[jax.ShapeDtypeStruct((T, D_MODEL), F32), jax.ShapeDtypeStruct((T, D_MODEL), BF16),
                   jax.ShapeDtypeStruct((T, D_MODEL), BF16)],
        name="out_proj", compiler_params=_params("parallel"),
    )(h0, y_rg, y_hg, w_out, g2)


def _gate_up(v, w_gu):
    T = v.shape[0]
    tm = _row_tile(T, 416)
    tn = D_FF // 2

    def body(v_ref, wg_ref, wu_ref, g_ref, u_ref, act_ref):
        x = v_ref[...]
        g = _dot(x, wg_ref[...])
        u = _dot(x, wu_ref[...])
        g_ref[...] = g
        u_ref[...] = u
        act_ref[...] = (g * _sigmoid(g) * u).astype(BF16)

    blk = pl.BlockSpec((tm, tn), lambda j, i: (i, j))
    return pl.pallas_call(
        body, grid=(2, T // tm),
        in_specs=[pl.BlockSpec((tm, D_MODEL), lambda j, i: (i, 0)),
                  pl.BlockSpec((D_MODEL, tn), lambda j, i: (0, j)),
                  pl.BlockSpec((D_MODEL, tn), lambda j, i: (0, j + 2))],
        out_specs=[blk, blk, blk],
        out_shape=[jax.ShapeDtypeStruct((T, D_FF), F32), jax.ShapeDtypeStruct((T, D_FF), F32),
                   jax.ShapeDtypeStruct((T, D_FF), BF16)],
        name="gate_up", compiler_params=_params("parallel", "parallel"),
    )(v, w_gu, w_gu)


def _down_loss(h1, act, w_down, gf, target):
    T = h1.shape[0]
    tm = _row_tile(T, 832)

    def body(h_ref, a_ref, w_ref, g_ref, t_ref, dh2_ref, dh2b_ref, loss_ref, gg_ref):
        i = pl.program_id(0)

        @pl.when(i == 0)
        def _():
            loss_ref[...] = jnp.zeros_like(loss_ref)
            gg_ref[...] = jnp.zeros_like(gg_ref)

        h2 = h_ref[...] + _dot(a_ref[...], w_ref[...])
        r = _rms(h2)
        n = h2 * r
        gf_ = g_ref[...]
        row = i * tm + lax.broadcasted_iota(jnp.int32, (tm, 1), 0)
        err = jnp.where(row >= PAD + N_META, n * gf_ - t_ref[...], 0.0)
        loss_ref[...] += 0.5 * jnp.sum(jnp.mean(err * err, axis=-1, keepdims=True), axis=0, keepdims=True)
        dy = err * (1.0 / D_MODEL)
        gg_ref[...] += jnp.sum(dy * n, axis=0, keepdims=True)
        dh2 = _rms_bwd(dy * gf_, n, r)
        dh2_ref[...] = dh2
        dh2b_ref[...] = dh2.astype(BF16)

    row_spec = lambda n: pl.BlockSpec((tm, n), lambda i: (i, 0))
    return pl.pallas_call(
        body, grid=(T // tm,),
        in_specs=[row_spec(D_MODEL), row_spec(D_FF), pl.BlockSpec((D_FF, D_MODEL), lambda i: (0, 0)),
                  pl.BlockSpec((1, D_MODEL), lambda i: (0, 0)), row_spec(D_MODEL)],
        out_specs=[row_spec(D_MODEL), row_spec(D_MODEL), pl.BlockSpec((1, 1), lambda i: (0, 0)),
                   pl.BlockSpec((1, D_MODEL), lambda i: (0, 0))],
        out_shape=[jax.ShapeDtypeStruct((T, D_MODEL), F32), jax.ShapeDtypeStruct((T, D_MODEL), BF16),
                   jax.ShapeDtypeStruct((1, 1), F32), jax.ShapeDtypeStruct((1, D_MODEL), F32)],
        name="down_loss", compiler_params=_params("arbitrary"),
    )(h1, act, w_down, gf, target)


def _ffn_bwd_act(dh2b, gate, up, w_down):
    T = dh2b.shape[0]
    tm = _row_tile(T, 208)

    def body(d_ref, g_ref, u_ref, w_ref, dgu_ref):
        dact = _dot_nt(d_ref[...], w_ref[...])
        g = g_ref[...]
        s = _sigmoid(g)
        dgu_ref[:, :D_FF] = (dact * u_ref[...] * s * (1.0 + g * (1.0 - s))).astype(BF16)
        dgu_ref[:, D_FF:] = (dact * g * s).astype(BF16)

    row = lambda n: pl.BlockSpec((tm, n), lambda i: (i, 0))
    return pl.pallas_call(
        body, grid=(T // tm,),
        in_specs=[row(D_MODEL), row(D_FF), row(D_FF), pl.BlockSpec((D_FF, D_MODEL), lambda i: (0, 0))],
        out_specs=row(2 * D_FF),
        out_shape=jax.ShapeDtypeStruct((T, 2 * D_FF), BF16),
        name="ffn_bwd_act", compiler_params=_params("parallel"),
    )(dh2b, gate, up, w_down)


def _ffn_bwd_in(dgu, w_gu, h1, g2, dh2, w_out):
    T = h1.shape[0]
    tm = _row_tile(T, 416)

    def body(dgu_ref, wgu_ref, h_ref, g_ref, d2_ref, wo_ref, dh1_ref, dh1b_ref, dy_ref, gg_ref):
        i = pl.program_id(0)

        @pl.when(i == 0)
        def _():
            gg_ref[...] = jnp.zeros_like(gg_ref)

        dv = _dot_nt(dgu_ref[...], wgu_ref[...])
        h1 = h_ref[...]
        r = _rms(h1)
        n = h1 * r
        gg_ref[...] += jnp.sum(dv * n, axis=0, keepdims=True)
        dh1 = d2_ref[...] + _rms_bwd(dv * g_ref[...], n, r)
        dh1_ref[...] = dh1
        db = dh1.astype(BF16)
        dh1b_ref[...] = db
        dy_ref[...] = _dot_nt(db, wo_ref[...])

    row = lambda n: pl.BlockSpec((tm, n), lambda i: (i, 0))
    return pl.pallas_call(
        body, grid=(T // tm,),
        in_specs=[row(2 * D_FF), pl.BlockSpec((D_MODEL, 2 * D_FF), lambda i: (0, 0)),
                  row(D_MODEL), pl.BlockSpec((1, D_MODEL), lambda i: (0, 0)), row(D_MODEL),
                  pl.BlockSpec((D_MODEL, D_MODEL), lambda i: (0, 0))],
        out_specs=[row(D_MODEL), row(D_MODEL), row(D_MODEL), pl.BlockSpec((1, D_MODEL), lambda i: (0, 0))],
        out_shape=[jax.ShapeDtypeStruct((T, D_MODEL), F32), jax.ShapeDtypeStruct((T, D_MODEL), BF16),
                   jax.ShapeDtypeStruct((T, D_MODEL), F32), jax.ShapeDtypeStruct((1, D_MODEL), F32)],
        name="ffn_bwd_in", compiler_params=_params("arbitrary"),
    )(dgu, w_gu, h1, g2, dh2, w_out)


def _rg_bwd(p, hs, dy, cw, cb, wg, bg, lam, rg_g):
    T = p.shape[0]
    tm = _row_tile(T, 208)
    nt = T // tm
    hb = tm // 8

    def body(xg_ref, xh_ref, h_ref, hh_ref, dy_ref, cw_ref, cb_ref, w_ref, bg_ref, lam_ref, g_ref,
             dp_ref, gcw_ref, gcb_ref, gw_ref, gbg_ref, glam_ref, gg_ref,
             ext, dext, a_s, b_s, d_s, gacc, carry_d, carry_a):
        i = pl.program_id(0)
        t_idx = nt - 1 - i

        @pl.when(i == 0)
        def _():
            dext[tm:tm + 8, :] = jnp.zeros((8, D_RG), F32)
            carry_d[...] = jnp.zeros_like(carry_d)
            carry_a[...] = jnp.zeros_like(carry_a)
            gacc[...] = jnp.zeros_like(gacc)
            for ref in (gcw_ref, gcb_ref, gbg_ref, glam_ref, gg_ref, gw_ref):
                ref[...] = jnp.zeros_like(ref)

        first = t_idx == 0
        ext[0:8, :] = jnp.where(first, 0.0, xh_ref[:, :D_RG])
        ext[8:8 + tm, :] = xg_ref[:, :D_RG]
        xc = _conv(ext, cw_ref, cb_ref, tm)
        lam_ = lam_ref[...]
        r, ig, sp, a, m = _rg_gates(xc, w_ref, bg_ref, lam_)
        row = t_idx * tm + lax.broadcasted_iota(jnp.int32, (tm, 1), 0)
        valid = row >= PAD

        gr = xg_ref[:, D_RG:]
        g, dgelu = _gelu_parts(gr)
        h = h_ref[...]
        yy = g * h
        rr = _rms(yy)
        nn = yy * rr
        dy_ = dy_ref[...]
        gg_ref[...] += jnp.sum(dy_ * nn, axis=0, keepdims=True)
        dyy = _rms_bwd(dy_ * g_ref[...], nn, rr)
        dp_ref[:, D_RG:] = dyy * h * dgelu

        a_s[...] = a
        b_s[...] = dyy * g
        rowi = lax.broadcasted_iota(jnp.int32, (8, D_RG), 0)

        def blk(jj, c):
            cd, ca = c
            o = pl.multiple_of((hb - 1 - jj) * 8, 8)
            a_blk = a_s[pl.ds(o, 8), :]
            a_next = jnp.where(rowi == 7, ca, pltpu.roll(a_blk, 7, axis=0))
            A, B = _scan_block_bwd(a_next, b_s[pl.ds(o, 8), :], rowi)
            d = B + A * cd
            d_s[pl.ds(o, 8), :] = d
            return d[0:1, :], a_blk[0:1, :]

        cd, ca = lax.fori_loop(0, hb, blk, (carry_d[...], carry_a[...]))
        carry_d[...] = cd
        carry_a[...] = ca
        delta = d_s[...]

        h_last_prev = jnp.where(first, 0.0, hh_ref[7:8, :])
        row0 = lax.broadcasted_iota(jnp.int32, (tm, 1), 0) == 0
        h_prev = jnp.where(row0, h_last_prev, pltpu.roll(h, 1, axis=0))
        dbx = jnp.where(valid, delta, 0.0)
        da = delta * h_prev
        di = dbx * m * xc
        dm = dbx * ig * xc
        dla = a * (da - dm * a / m)
        dla = jnp.where(valid, dla, 0.0)
        glam_ref[...] += jnp.sum(dla * r, axis=0, keepdims=True) * (LRU_C * _sigmoid(-lam_))
        dr = (-LRU_C) * sp * dla
        dpre = jnp.concatenate([dr * r * (1.0 - r), di * ig * (1.0 - ig)], axis=1)
        gbg_ref[...] += jnp.sum(dpre, axis=0, keepdims=True)
        dpre_b = dpre.astype(BF16)
        gacc[...] += _dot_tn(xc.astype(BF16), dpre_b)
        dxc = dbx * m * ig + _dot_nt(dpre_b, w_ref[...])
        gcb_ref[...] += jnp.sum(dxc, axis=0, keepdims=True)
        for j in range(CONV_W):
            gcw_ref[j:j + 1, :] += jnp.sum(dxc * ext[8 - 3 + j:8 - 3 + j + tm, :], axis=0, keepdims=True)
        dext[0:tm, :] = dxc
        dxr = cw_ref[0:1, :] * dext[3:3 + tm, :]
        for j in range(1, CONV_W):
            dxr = dxr + cw_ref[j:j + 1, :] * dext[3 - j:3 - j + tm, :]
        dp_ref[:, :D_RG] = dxr
        dext[tm:tm + 8, :] = dext[0:8, :]

        @pl.when(i == nt - 1)
        def _():
            fold = _head_fold()
            mask = _head_mask()
            for k in range(2):
                blockdiag = jnp.where(mask, gacc[:, k * D_RG:(k + 1) * D_RG], 0.0)
                gw_ref[k * D_RG:(k + 1) * D_RG, :] = jnp.dot(blockdiag, fold, precision=HIGHEST,
                                                             preferred_element_type=F32)

    vec = lambda n: pl.BlockSpec((1, n), lambda i: (0, 0))
    rev = lambda n: pl.BlockSpec((tm, n), lambda i: (nt - 1 - i, 0))
    halo = lambda n: pl.BlockSpec((8, n), lambda i: (jnp.maximum((nt - 1 - i) * hb - 1, 0), 0))
    return pl.pallas_call(
        body, grid=(nt,),
        in_specs=[rev(2 * D_RG), halo(2 * D_RG), rev(D_RG), halo(D_RG), rev(D_RG),
                  pl.BlockSpec((CONV_W, D_RG), lambda i: (0, 0)), vec(D_RG),
                  pl.BlockSpec((D_RG, 2 * D_RG), lambda i: (0, 0)), vec(2 * D_RG), vec(D_RG), vec(D_RG)],
        out_specs=[rev(2 * D_RG), pl.BlockSpec((CONV_W, D_RG), lambda i: (0, 0)), vec(D_RG),
                   pl.BlockSpec((2 * D_RG, RG_HEAD_DIM), lambda i: (0, 0)), vec(2 * D_RG), vec(D_RG), vec(D_RG)],
        out_shape=[jax.ShapeDtypeStruct((T, 2 * D_RG), F32), jax.ShapeDtypeStruct((CONV_W, D_RG), F32),
                   jax.ShapeDtypeStruct((1, D_RG), F32), jax.ShapeDtypeStruct((2 * D_RG, RG_HEAD_DIM), F32),
                   jax.ShapeDtypeStruct((1, 2 * D_RG), F32), jax.ShapeDtypeStruct((1, D_RG), F32),
                   jax.ShapeDtypeStruct((1, D_RG), F32)],
        scratch_shapes=[pltpu.VMEM((tm + 8, D_RG), F32), pltpu.VMEM((tm + 8, D_RG), F32),
                        pltpu.VMEM((tm, D_RG), F32), pltpu.VMEM((tm, D_RG), F32), pltpu.VMEM((tm, D_RG), F32),
                        pltpu.VMEM((D_RG, 2 * D_RG), F32), pltpu.VMEM((1, D_RG), F32), pltpu.VMEM((1, D_RG), F32)],
        name="rg_bwd", compiler_params=_params("arbitrary"),
    )(p, p, hs, hs, dy, cw, cb, wg, bg, lam, rg_g)


def _hg_bwd(p, o_all, st_all, dy, lbraw, hg_g):
    T = p.shape[0]
    n_chunks = T // CHUNK

    def body(hq_ref, hf_ref, hi_ref, hg_ref, o_ref, st_ref, dy_ref, lb_ref, g_ref,
             dp_ref, glb_ref, gg_ref, dst):
        i = pl.program_id(0)
        n = n_chunks - 1 - i

        @pl.when(i == 0)
        def _():
            dst[...] = jnp.zeros_like(dst)
            glb_ref[...] = jnp.zeros_like(glb_ref)
            gg_ref[...] = jnp.zeros_like(gg_ref)

        valid = (n * CHUNK + lax.broadcasted_iota(jnp.int32, (CHUNK, 1), 0)) >= PAD
        hq, hf, v, hg = hq_ref[...], hf_ref[...], hi_ref[...], hg_ref[...]
        lb, sq, q, sf, f, b = _hg_gates(hq, hf, lb_ref, valid)
        k = 1.0 - f
        causal = _causal()
        is_last = lax.broadcasted_iota(jnp.int32, (CHUNK, 1), 0) == CHUNK - 1
        g_ = g_ref[...]
        db_parts, dq_parts, dk_parts = [], [], []
        gg = jnp.zeros((1, HG_HEAD_DIM), F32)
        for h in range(HG_HEADS):
            sl = slice(h * HG_HEAD_DIM, (h + 1) * HG_HEAD_DIM)
            qh, kh, vh, bh = q[:, sl], k[:, sl], v[:, sl], b[:, sl]
            o = o_ref[:, sl]
            ro = _rms(o)
            no = o * ro
            hgh = hg[:, sl]
            sg = _sigmoid(hgh)
            dyh = dy_ref[:, sl]
            dp_ref[:, 3 * D_HG + h * HG_HEAD_DIM:3 * D_HG + (h + 1) * HG_HEAD_DIM] = (
                dyh * no * g_ * sg * (1.0 + hgh * (1.0 - sg)))
            dng = dyh * hgh * sg
            gg = gg + jnp.sum(dng * no, axis=0, keepdims=True)
            do = _rms_bwd(dng * g_, no, ro)
            dob = do.astype(BF16)

            st_h = st_ref[0, sl, :]
            dst_h = dst[sl, :]
            blk, b_last, eb, eq, ekh, ek, q_hat, k_til = _hg_head(qh, kh, bh)
            q_til = qh * eb
            k_hat = kh * ekh
            vb = vh.astype(BF16)
            dstb = dst_h.astype(BF16)
            qhb, ktb = q_hat.astype(BF16), k_til.astype(BF16)
            att = jnp.where(causal, _dot_nt(qhb, ktb), 0.0)
            datt = jnp.where(causal, _dot_nt(dob, vb), 0.0).astype(BF16)

            dk_hat = _dot(vb, dstb)
            dv = _dot_nt(k_hat.astype(BF16), dstb) + _dot_tn(att.astype(BF16), dob)
            dq_til = _dot(dob, st_h.astype(BF16))
            e_last = jnp.exp(b_last)
            db_last = (jnp.sum(dk_hat * k_hat, axis=0, keepdims=True)
                       + e_last * jnp.sum(dst_h * st_h, axis=0, keepdims=True))
            dst[sl, :] = dst_h * e_last + _dot_tn(dob, q_til.astype(BF16))

            dq_hat = _dot(datt, ktb)
            dk_til = _dot_tn(datt, qhb)
            dq_sel = dq_hat[:, (N_SUB - 1) * HG_HEAD_DIM:]
            for s in range(N_SUB - 2, -1, -1):
                dq_sel = jnp.where(blk == s, dq_hat[:, s * HG_HEAD_DIM:(s + 1) * HG_HEAD_DIM], dq_sel)
            dq_a = dq_sel * eq
            dk_a = dk_til[:, :HG_HEAD_DIM] * ek[0]
            for s in range(1, N_SUB):
                dk_a = dk_a + dk_til[:, s * HG_HEAD_DIM:(s + 1) * HG_HEAD_DIM] * ek[s]
            db_att = qhb.astype(F32) * dq_hat - ktb.astype(F32) * dk_til
            db = dq_til * q_til - dk_hat * k_hat
            for s in range(N_SUB):
                db = db + db_att[:, s * HG_HEAD_DIM:(s + 1) * HG_HEAD_DIM]
            db_parts.append(jnp.where(is_last, db + db_last, db))
            dq_parts.append(dq_til * eb + dq_a)
            dk_parts.append(dk_hat * ekh + dk_a)
            dp_ref[:, 2 * D_HG + h * HG_HEAD_DIM:2 * D_HG + (h + 1) * HG_HEAD_DIM] = dv

        gg_ref[...] += gg
        db = jnp.concatenate(db_parts, axis=1)
        dq = jnp.concatenate(dq_parts, axis=1)
        dk = jnp.concatenate(dk_parts, axis=1)
        dlf = jnp.where(valid, jnp.dot(_tri(False), db, precision=HIGHEST, preferred_element_type=F32), 0.0)
        dp_ref[:, :D_HG] = dq * sq * (1.0 + hq * (1.0 - sq))
        df = dlf / f - dk
        dlb = jnp.sum(df * (1.0 - sf), axis=0, keepdims=True) * lb * (1.0 - lb)
        glb_ref[0:1, :] += dlb
        glb_ref[1:2, :] += -dlb
        dp_ref[:, D_HG:2 * D_HG] = df * (1.0 - lb) * sf * (1.0 - sf)

    rev = lambda j: pl.BlockSpec((CHUNK, D_HG), lambda i: (n_chunks - 1 - i, j))
    return pl.pallas_call(
        body, grid=(n_chunks,),
        in_specs=[rev(2), rev(3), rev(4), rev(5), rev(0),
                  pl.BlockSpec((1, D_HG, HG_HEAD_DIM), lambda i: (n_chunks - 1 - i, 0, 0)), rev(1),
                  pl.BlockSpec((2, D_HG), lambda i: (0, 0)), pl.BlockSpec((1, HG_HEAD_DIM), lambda i: (0, 0))],
        out_specs=[pl.BlockSpec((CHUNK, 4 * D_HG), lambda i: (n_chunks - 1 - i, 0)),
                   pl.BlockSpec((2, D_HG), lambda i: (0, 0)), pl.BlockSpec((1, HG_HEAD_DIM), lambda i: (0, 0))],
        out_shape=[jax.ShapeDtypeStruct((T, 4 * D_HG), F32), jax.ShapeDtypeStruct((2, D_HG), F32),
                   jax.ShapeDtypeStruct((1, HG_HEAD_DIM), F32)],
        scratch_shapes=[pltpu.VMEM((D_HG, HG_HEAD_DIM), F32)],
        name="hg_bwd", compiler_params=_params("arbitrary"),
    )(p, p, p, p, o_all, st_all, dy, lbraw, hg_g)


def _in_bwd(dp_rg, dp_hg, w_in, h0, g1, dh1):
    T = h0.shape[0]
    tm = _row_tile(T, 416)

    def body(dr_ref, dh_ref, w_ref, h_ref, g_ref, d1_ref, dh0_ref, dpb_ref, gg_ref):
        i = pl.program_id(0)

        @pl.when(i == 0)
        def _():
            gg_ref[...] = jnp.zeros_like(gg_ref)

        dpb_ref[:, :2 * D_RG] = dr_ref[...].astype(BF16)
        dpb_ref[:, 2 * D_RG:] = dh_ref[...].astype(BF16)
        du = _dot_nt(dpb_ref[...], w_ref[...])
        h0_ = h_ref[...]
        r = _rms(h0_)
        n = h0_ * r
        gg_ref[...] += jnp.sum(du * n, axis=0, keepdims=True)
        dh0_ref[...] = d1_ref[...] + _rms_bwd(du * g_ref[...], n, r)

    row = lambda n: pl.BlockSpec((tm, n), lambda i: (i, 0))
    return pl.pallas_call(
        body, grid=(T // tm,),
        in_specs=[row(2 * D_RG), row(4 * D_HG), pl.BlockSpec((D_MODEL, D_IN), lambda i: (0, 0)),
                  row(D_MODEL), pl.BlockSpec((1, D_MODEL), lambda i: (0, 0)), row(D_MODEL)],
        out_specs=[row(D_MODEL), row(D_IN), pl.BlockSpec((1, D_MODEL), lambda i: (0, 0))],
        out_shape=[jax.ShapeDtypeStruct((T, D_MODEL), F32), jax.ShapeDtypeStruct((T, D_IN), BF16),
                   jax.ShapeDtypeStruct((1, D_MODEL), F32)],
        name="in_bwd", compiler_params=_params("arbitrary"),
    )(dp_rg, dp_hg, w_in, h0, g1, dh1)


def _col_tile(cols, target):
    best = None
    for t in range(128, min(cols, target) + 1, 128):
        if cols % t == 0:
            best = t
    assert best is not None, cols
    return best


def _weight_grad(a, b, name):
    T, M = a.shape
    N = b.shape[1]
    tk = _row_tile(T, 832)
    tm = _col_tile(M, 512)
    tn = _col_tile(N, 1536)

    def body(a_ref, b_ref, o_ref):
        @pl.when(pl.program_id(2) == 0)
        def _():
            o_ref[...] = jnp.zeros_like(o_ref)

        o_ref[...] += _dot_tn(a_ref[...], b_ref[...])

    return pl.pallas_call(
        body, grid=(M // tm, N // tn, T // tk),
        in_specs=[pl.BlockSpec((tk, tm), lambda m, n, k: (k, m)), pl.BlockSpec((tk, tn), lambda m, n, k: (k, n))],
        out_specs=pl.BlockSpec((tm, tn), lambda m, n, k: (m, n)),
        out_shape=jax.ShapeDtypeStruct((M, N), F32),
        name=name, compiler_params=_params("parallel", "parallel", "arbitrary"),
    )(a, b)


def _local_step(h0, target, w_in, w_out, w_gu, w_down, small, on_ffn_grads=None):
    wg = _gate_weights(small["w_rgate"], small["w_igate"])
    bg = jnp.concatenate([small["b_rgate"], small["b_igate"]], axis=1)

    p, u = _in_proj(h0, small["mix_norm_g"], w_in)
    y_rg, hs = _rg_fwd(p, small["conv_w"], small["conv_b"], wg, bg, small["lru_lambda"], small["rg_norm_g"])
    y_hg, o_all, st_all = _hg_fwd(p, small["hg_lower_bound"], small["hg_norm_g"])
    h1, v, yb = _out_proj(h0, y_rg, y_hg, w_out, small["ffn_norm_g"])
    gate, up, act = _gate_up(v, w_gu)
    dh2, dh2b, loss, g_final = _down_loss(h1, act, w_down, small["final_norm_g"], target)

    dgu = _ffn_bwd_act(dh2b, gate, up, w_down)
    ffn_grads = {"w_gate_up": _weight_grad(v, dgu, "grad_w_gate_up"),
                 "w_down": _weight_grad(act, dh2b, "grad_w_down")}
    early = on_ffn_grads(ffn_grads) if on_ffn_grads is not None else None
    dh1, dh1b, dy, g_ffn = _ffn_bwd_in(dgu, w_gu, h1, small["ffn_norm_g"], dh2, w_out)
    dp_rg, g_cw, g_cb, g_wgate, g_bg, g_lam, g_rgn = _rg_bwd(
        p, hs, dy, small["conv_w"], small["conv_b"], wg, bg, small["lru_lambda"], small["rg_norm_g"])
    dp_hg, g_lb, g_hgn = _hg_bwd(p, o_all, st_all, dy, small["hg_lower_bound"], small["hg_norm_g"])
    dh0, dpb, g_mix = _in_bwd(dp_rg, dp_hg, w_in, h0, small["mix_norm_g"], dh1)

    grads = {
        "w_in": _weight_grad(u, dpb, "grad_w_in"),
        "w_out": _weight_grad(yb, dh1b, "grad_w_out"),
        "w_gate_up": ffn_grads["w_gate_up"], "w_down": ffn_grads["w_down"],
        "mix_norm_g": g_mix, "conv_w": g_cw, "conv_b": g_cb, "w_gates": g_wgate,
        "b_rgate": g_bg[:, :D_RG], "b_igate": g_bg[:, D_RG:], "lru_lambda": g_lam, "rg_norm_g": g_rgn,
        "hg_lower_bound": g_lb, "hg_norm_g": g_hgn, "ffn_norm_g": g_ffn, "final_norm_g": g_final,
    }
    return loss, dh0, grads, early


ANY = pl.BlockSpec(memory_space=pl.ANY)
HALF = D_MODEL // 2

BIG = {"w_in": (D_MODEL, D_IN // N_CHIPS, True), "w_gate_up": (D_MODEL, 2 * D_FF // N_CHIPS, True),
       "w_out": (D_MODEL // N_CHIPS, D_MODEL, False), "w_down": (D_FF // N_CHIPS, D_MODEL, False)}
BIG_NAMES = tuple(BIG)
N_BIG = len(BIG_NAMES)


def _full_shape(name):
    rows, cols, by_col = BIG[name]
    return (rows, cols * N_CHIPS) if by_col else (rows * N_CHIPS, cols)


def _place():
    return lax.axis_index("x"), lax.axis_index("y"), lax.axis_index("c")


def _chip_of(x, y, r):
    fx, fy = (r + 1) >> 1, (r + 1) & 1
    return (1 - x if fx else x), (1 - y if fy else y)


def _half_of(ref, by_col, half):
    start = pl.multiple_of(half * HALF, 128)
    return ref.at[pl.ds(start, HALF), :] if by_col else ref.at[:, pl.ds(start, HALF)]


def _shard_of(ref, name, chip):
    rows, cols, by_col = BIG[name]
    if by_col:
        return ref.at[:, pl.ds(pl.multiple_of(chip * cols, 128), cols)]
    return ref.at[pl.ds(pl.multiple_of(chip * rows, 16), rows), :]


def _shard_half_of(ref, name, chip, half):
    rows, cols, by_col = BIG[name]
    start = pl.multiple_of(half * HALF, 128)
    if by_col:
        return ref.at[pl.ds(start, HALF), pl.ds(pl.multiple_of(chip * cols, 128), cols)]
    return ref.at[pl.ds(pl.multiple_of(chip * rows, 16), rows), pl.ds(start, HALF)]


def _remote(src, dst, send_sems, recv_sems, k, dev):
    return pltpu.make_async_remote_copy(src_ref=src, dst_ref=dst, send_sem=send_sems.at[k], recv_sem=recv_sems.at[k],
                                        device_id=dev, device_id_type=MESH)


def _cast_into_full(w_shard, name, chip):
    rows, cols, by_col = BIG[name]
    tr = _row_tile(rows, 352)
    if by_col:
        out_spec = pl.BlockSpec((tr, cols), lambda i, s: (i, s[0]))
    else:
        out_spec = pl.BlockSpec((tr, cols), lambda i, s: (s[0] * (rows // tr) + i, 0))

    def body(s_ref, w_ref, o_ref):
        o_ref[...] = w_ref[...].astype(BF16)

    return pl.pallas_call(
        body,
        grid_spec=pltpu.PrefetchScalarGridSpec(
            num_scalar_prefetch=1, grid=(rows // tr,), in_specs=[pl.BlockSpec((tr, cols), lambda i, s: (i, 0))],
            out_specs=out_spec),
        out_shape=jax.ShapeDtypeStruct(_full_shape(name), BF16),
        name="cast_" + name, compiler_params=_params("parallel"),
    )(chip, w_shard)


def _gather_weights(placed, small, names, label, collective_id):
    n, ns = len(names), len(small)
    hbm = pltpu.MemorySpace.HBM
    outs = [jax.new_ref(placed[nm], memory_space=hbm) for nm in names]
    small_in = [jax.new_ref(s, memory_space=hbm) for s in small]
    small_out = [jax.empty_ref(jax.ShapeDtypeStruct((s.shape[0], s.shape[1] * N_CHIPS), F32), memory_space=hbm)
                 for s in small]
    n_sems = 6 * n + 3 * ns

    @pl.kernel(mesh=plsc.ScalarSubcoreMesh(axis_name="seq", num_cores=1), name=label, out_type=(),
               scratch_types=(pltpu.SemaphoreType.DMA((n_sems,)), pltpu.SemaphoreType.DMA((n_sems,)),
                              pltpu.SemaphoreType.DMA((max(ns, 1),))),
               compiler_params=pltpu.CompilerParams(collective_id=collective_id))
    def launch(send_sems, recv_sems, local_sems):
        x, y, c = _place()
        chip = 2 * x + y
        sibling = (x, y, 1 - c)
        others = [_chip_of(x, y, r) for r in range(3)]
        _handshake([(qx, qy, c) for qx, qy in others] + [sibling])

        def small_block(a, q):
            cols = small[a].shape[1]
            return small_out[a].at[:, pl.ds(pl.multiple_of(q * cols, 128), cols)]

        local = [pltpu.make_async_copy(small_in[a], small_block(a, chip), local_sems.at[a]) for a in range(ns)]
        for cp in local:
            cp.start()

        sends = []
        for a, name in enumerate(names):
            mine = _shard_half_of(outs[a], name, chip, c)
            for r, (qx, qy) in enumerate(others):
                sends.append(_remote(mine, mine, send_sems, recv_sems, 6 * a + r, (qx, qy, c)))
        for a in range(ns):
            for r, (qx, qy) in enumerate(others):
                sends.append(_remote(small_in[a], small_block(a, chip), send_sems, recv_sems,
                                     6 * n + 3 * a + r, (qx, qy, c)))
        for cp in sends:
            cp.start()

        forwards = []
        for a, name in enumerate(names):
            for r, (qx, qy) in enumerate(others):
                landed = _shard_half_of(outs[a], name, 2 * qx + qy, c)
                _remote(landed, landed, send_sems, recv_sems, 6 * a + r, (qx, qy, c)).wait_recv()
                fwd = _remote(landed, landed, send_sems, recv_sems, 6 * a + 3 + r, sibling)
                fwd.start()
                forwards.append(fwd)
        for a in range(ns):
            for r, (qx, qy) in enumerate(others):
                landed = small_block(a, 2 * qx + qy)
                _remote(landed, landed, send_sems, recv_sems, 6 * n + 3 * a + r, (qx, qy, c)).wait_recv()
        for a, name in enumerate(names):
            for r, (qx, qy) in enumerate(others):
                landed = _shard_half_of(outs[a], name, 2 * qx + qy, 1 - c)
                _remote(landed, landed, send_sems, recv_sems, 6 * a + 3 + r, sibling).wait_recv()
        for cp in sends + forwards:
            cp.wait_send()
        for cp in local:
            cp.wait()

    launch()
    return {nm: ref[...] for nm, ref in zip(names, outs)}, [ref[...] for ref in small_out]


def _exchange_halves(grads, names, label):
    n = len(names)

    def body(*refs):
        ins, outs = refs[:n], refs[n:2 * n]
        send_sems, recv_sems = refs[2 * n:]
        x, y, c = _place()
        copies = []
        for a, name in enumerate(names):
            copies.append(_remote(_half_of(ins[a], BIG[name][2], 1 - c), outs[a], send_sems, recv_sems, a,
                                  (x, y, 1 - c)))
        for cp in copies:
            cp.start()
        for cp in copies:
            cp.wait()

    def half_shape(name):
        r, c_ = _full_shape(name)
        return (HALF, c_) if BIG[name][2] else (r, HALF)

    got = pl.pallas_call(
        body, in_specs=[ANY] * n, out_specs=[ANY] * n,
        out_shape=[jax.ShapeDtypeStruct(half_shape(nm), F32) for nm in names],
        scratch_shapes=[pltpu.SemaphoreType.DMA((n,)), pltpu.SemaphoreType.DMA((n,))],
        name=label,
    )(*[grads[nm] for nm in names])
    return dict(zip(names, got))


def _chip_sum(g, got, name, core):
    by_col = BIG[name][2]
    rows, cols = got.shape
    if by_col:
        tr = 128
        g_spec = pl.BlockSpec((tr, cols), lambda i, s: (s[0] * (HALF // tr) + i, 0))
    else:
        tr = _row_tile(rows, 512)
        g_spec = pl.BlockSpec((tr, HALF), lambda i, s: (i, s[0]))
    blk = pl.BlockSpec((tr, cols), lambda i, s: (i, 0))

    def body(s_ref, g_ref, r_ref, f_ref, b_ref):
        t = g_ref[...] + r_ref[...]
        f_ref[...] = t
        b_ref[...] = t.astype(BF16)

    return pl.pallas_call(
        body,
        grid_spec=pltpu.PrefetchScalarGridSpec(num_scalar_prefetch=1, grid=(rows // tr,), in_specs=[g_spec, blk],
                                               out_specs=[blk, blk]),
        out_shape=[jax.ShapeDtypeStruct(got.shape, F32), jax.ShapeDtypeStruct(got.shape, BF16)],
        name="chip_sum_" + name, compiler_params=_params("parallel"),
    )(core, g, got)


def _piece_shape(name):
    rows, cols, by_col = BIG[name]
    return (HALF, cols) if by_col else (rows, HALF)


def _handshake(peers):
    barrier = pltpu.get_barrier_semaphore()
    for peer in peers:
        pl.semaphore_signal(barrier, inc=1, device_id=peer, device_id_type=MESH)
    pl.semaphore_wait(barrier, len(peers))


def _send_chip_sums(sums, names, label, collective_id):
    n = len(names)

    def body(*refs):
        ins, outs = refs[:n], refs[n:2 * n]
        send_sems, recv_sems = refs[2 * n:]
        x, y, c = _place()
        others = [_chip_of(x, y, r) for r in range(3)]
        _handshake([(qx, qy, c) for qx, qy in others])
        copies = []
        for a, name in enumerate(names):
            for r, (qx, qy) in enumerate(others):
                copies.append(_remote(_shard_of(ins[a], name, 2 * qx + qy), outs[a].at[r], send_sems, recv_sems,
                                      3 * a + r, (qx, qy, c)))
        for cp in copies:
            cp.start()
        for cp in copies:
            cp.wait()

    return pl.kernel(
        body, mesh=plsc.ScalarSubcoreMesh(axis_name="seq", num_cores=1), name=label,
        out_type=tuple(jax.ShapeDtypeStruct((3,) + _piece_shape(nm), BF16) for nm in names),
        scratch_types=(pltpu.SemaphoreType.DMA((3 * n,)), pltpu.SemaphoreType.DMA((3 * n,))),
        compiler_params=pltpu.CompilerParams(collective_id=collective_id),
    )(*[sums[nm] for nm in names])


def _total(own, got, name, chip_core):
    rows, cols, by_col = BIG[name]
    pr, pc = _piece_shape(name)
    tr = _row_tile(pr, 352)
    if by_col:
        own_spec = pl.BlockSpec((tr, pc), lambda i, s: (i, s[0]))
    else:
        own_spec = pl.BlockSpec((tr, pc), lambda i, s: (s[0] * (pr // tr) + i, 0))
    got_spec = lambda r: pl.BlockSpec((None, tr, pc), lambda i, s: (r, i, 0))
    if by_col:
        out_spec = pl.BlockSpec((tr, pc), lambda i, s: (s[1] * (pr // tr) + i, 0))
    else:
        out_spec = pl.BlockSpec((tr, pc), lambda i, s: (i, s[1]))

    def body(s_ref, o_ref, a_ref, b_ref, c_ref, t_ref):
        t_ref[...] = ((o_ref[...] + a_ref[...].astype(F32)) + b_ref[...].astype(F32)) + c_ref[...].astype(F32)

    return pl.pallas_call(
        body,
        grid_spec=pltpu.PrefetchScalarGridSpec(
            num_scalar_prefetch=1, grid=(pr // tr,), in_specs=[own_spec, got_spec(0), got_spec(1), got_spec(2)],
            out_specs=out_spec),
        out_shape=jax.ShapeDtypeStruct((rows, cols), F32),
        name="total_" + name, compiler_params=_params("parallel"),
    )(chip_core, own, got, got, got)


def _share_totals(totals):
    def body(*refs):
        outs = refs[N_BIG:2 * N_BIG]
        send_sems, recv_sems = refs[2 * N_BIG:]
        x, y, c = _place()
        copies = []
        for a, name in enumerate(BIG_NAMES):
            mine = _half_of(outs[a], BIG[name][2], c)
            copies.append(_remote(mine, mine, send_sems, recv_sems, a, (x, y, 1 - c)))
        for cp in copies:
            cp.start()
        for a, name in enumerate(BIG_NAMES):
            theirs = _half_of(outs[a], BIG[name][2], 1 - c)
            _remote(theirs, theirs, send_sems, recv_sems, a, (x, y, 1 - c)).wait_recv()
        for cp in copies:
            cp.wait_send()

    return pl.pallas_call(
        body, in_specs=[ANY] * N_BIG, out_specs=[ANY] * N_BIG,
        out_shape=[jax.ShapeDtypeStruct(BIG[n][:2], F32) for n in BIG_NAMES],
        input_output_aliases={a: a for a in range(N_BIG)},
        scratch_shapes=[pltpu.SemaphoreType.DMA((N_BIG,)), pltpu.SemaphoreType.DMA((N_BIG,))],
        name="share_totals",
    )(*[totals[n] for n in BIG_NAMES])


VEC_ROWS = 32
VEC_ROW = {"mix_norm_g": 0, "conv_b": 1, "b_rgate": 2, "b_igate": 3, "lru_lambda": 4, "rg_norm_g": 5,
           "hg_lower_bound": 6, "hg_norm_g": 8, "ffn_norm_g": 9, "final_norm_g": 10, "loss": 11,
           "conv_w": 12, "meta_tokens": 16}
N_DEV = 8


def _all_reduce_small(pieces, gates):
    names = list(pieces)

    def body(*refs):
        ins = refs[:len(names)]
        g_ref, vec_ref, gsum_ref, send_v, got_v, got_g, send_sems, recv_sems = refs[len(names):]
        x, y, c = _place()
        me = 4 * x + 2 * y + c
        send_v[...] = jnp.zeros_like(send_v)
        for name, ref in zip(names, ins):
            nr, w = ref.shape
            send_v[VEC_ROW[name]:VEC_ROW[name] + nr, 0:w] = ref[...]
        got_v[me] = send_v[...]
        got_g[me] = g_ref[...]
        copies = []
        for r in range(1, N_DEV):
            peer = ((1 - x if r & 4 else x), (1 - y if r & 2 else y), (1 - c if r & 1 else c))
            copies.append(_remote(send_v, got_v.at[me], send_sems, recv_sems, r - 1, peer))
            copies.append(_remote(g_ref, got_g.at[me], send_sems, recv_sems, N_DEV - 1 + r - 1, peer))
        for cp in copies:
            cp.start()
        for cp in copies:
            cp.wait()
        vec = got_v[0]
        gs = got_g[0]
        for s in range(1, N_DEV):
            vec = vec + got_v[s]
            gs = gs + got_g[s]
        vec_ref[...] = vec
        gsum_ref[...] = gs

    vmem = pl.BlockSpec(memory_space=pltpu.VMEM)
    return pl.pallas_call(
        body, in_specs=[vmem] * (len(names) + 1), out_specs=[vmem, vmem],
        out_shape=[jax.ShapeDtypeStruct((VEC_ROWS, D_MODEL), F32), jax.ShapeDtypeStruct(gates.shape, F32)],
        scratch_shapes=[pltpu.VMEM((VEC_ROWS, D_MODEL), F32), pltpu.VMEM((N_DEV, VEC_ROWS, D_MODEL), F32),
                        pltpu.VMEM((N_DEV,) + gates.shape, F32),
                        pltpu.SemaphoreType.DMA((2 * (N_DEV - 1),)), pltpu.SemaphoreType.DMA((2 * (N_DEV - 1),))],
        name="all_reduce_small",
    )(*[pieces[n] for n in names], gates)


def _adamw_math(w, g, m, v):
    m = ADAM_B1 * m + (1.0 - ADAM_B1) * g
    v = ADAM_B2 * v + (1.0 - ADAM_B2) * (g * g)
    m_hat = m / (1.0 - ADAM_B1 ** ADAM_STEP)
    v_hat = v / (1.0 - ADAM_B2 ** ADAM_STEP)
    delta = -ADAM_LR * (m_hat / (jnp.sqrt(v_hat) + ADAM_EPS) + ADAM_WD * w)
    return delta, m, v


def _adamw_big(w, g, m, v, name):
    rows, cols = w.shape
    tr = _row_tile(rows, 352)

    def body(w_ref, g_ref, m_ref, v_ref, d_ref, nm_ref, nv_ref):
        d_ref[...], nm_ref[...], nv_ref[...] = _adamw_math(w_ref[...], g_ref[...], m_ref[...], v_ref[...])

    blk = pl.BlockSpec((tr, cols), lambda i: (i, 0))
    return pl.pallas_call(
        body, grid=(rows // tr,), in_specs=[blk] * 4, out_specs=[blk] * 3,
        out_shape=[jax.ShapeDtypeStruct(w.shape, F32)] * 3,
        name="adamw_" + name, compiler_params=_params("parallel"),
    )(w, g, m, v)


SMALL = {"meta_tokens": (N_META, D_MODEL // N_CHIPS), "mix_norm_g": (1, D_MODEL), "conv_w": (CONV_W, D_RG // N_CHIPS),
         "conv_b": (1, D_RG), "w_rgate": (D_RG, RG_HEAD_DIM), "b_rgate": (1, D_RG), "w_igate": (D_RG, RG_HEAD_DIM),
         "b_igate": (1, D_RG), "lru_lambda": (1, D_RG), "rg_norm_g": (1, D_RG), "hg_lower_bound": (2, D_HG),
         "hg_norm_g": (1, HG_HEAD_DIM), "ffn_norm_g": (1, D_MODEL), "final_norm_g": (1, D_MODEL)}
SMALL_NAMES = tuple(SMALL)
SHARDED_SMALL = ("meta_tokens", "conv_w")


def _adamw_small(vec, gates, w, m, v):
    n = len(SMALL_NAMES)

    def body(*refs):
        vec_ref, gates_ref = refs[:2]
        w_refs, m_refs, v_refs = refs[2:2 + n], refs[2 + n:2 + 2 * n], refs[2 + 2 * n:2 + 3 * n]
        outs = refs[2 + 3 * n:]
        loss_ref = outs[0]
        x, y, _ = _place()
        chip = 2 * x + y
        loss_ref[...] = vec_ref[VEC_ROW["loss"]:VEC_ROW["loss"] + 1, 0:1]

        def update(k, g):
            g_ref, d_ref, nm_ref, nv_ref = outs[1 + 4 * k:5 + 4 * k]
            g_ref[...] = g
            d_ref[...], nm_ref[...], nv_ref[...] = _adamw_math(w_refs[k][...], g, m_refs[k][...], v_refs[k][...])

        for k, name in enumerate(SMALL_NAMES):
            nr, w_ = SMALL[name]
            if name == "w_rgate":
                update(k, gates_ref[0:D_RG, :])
            elif name == "w_igate":
                update(k, gates_ref[D_RG:2 * D_RG, :])
            elif name in SHARDED_SMALL:
                r0 = VEC_ROW[name]
                for q in range(N_CHIPS):
                    @pl.when(chip == q)
                    def _(k=k, r0=r0, nr=nr, w_=w_, q=q):
                        update(k, vec_ref[r0:r0 + nr, q * w_:(q + 1) * w_])
            else:
                r0 = VEC_ROW[name]
                update(k, vec_ref[r0:r0 + nr, 0:w_])

    vmem = pl.BlockSpec(memory_space=pltpu.VMEM)
    out_shape = [jax.ShapeDtypeStruct((1, 1), F32)]
    for name in SMALL_NAMES:
        out_shape += [jax.ShapeDtypeStruct(SMALL[name], F32)] * 4
    outs = pl.pallas_call(
        body, in_specs=[vmem] * (2 + 3 * n), out_specs=[vmem] * len(out_shape), out_shape=out_shape,
        name="adamw_small",
    )(vec, gates, *[w[k] for k in SMALL_NAMES], *[m[k] for k in SMALL_NAMES], *[v[k] for k in SMALL_NAMES])
    loss = outs[0]
    res = {name: tuple(outs[1 + 4 * k:5 + 4 * k]) for k, name in enumerate(SMALL_NAMES)}
    return loss, res


WEIGHT_NAMES = ("meta_tokens", "mix_norm_g", "w_in", "conv_w", "conv_b", "w_rgate", "b_rgate", "w_igate", "b_igate",
                "lru_lambda", "rg_norm_g", "hg_lower_bound", "hg_norm_g", "w_out", "ffn_norm_g", "w_gate_up", "w_down",
                "final_norm_g")


def _to_2d(name, a):
    if name in BIG:
        return a.reshape(BIG[name][:2])
    return a.reshape(SMALL[name])


def kernel(x, meta_tokens, mix_norm_g, w_in, conv_w, conv_b, w_rgate, b_rgate, w_igate, b_igate, lru_lambda, rg_norm_g, hg_lower_bound, hg_norm_g, w_out, ffn_norm_g, w_gate_up, w_down, final_norm_g, loss_target, m_meta_tokens, m_mix_norm_g, m_w_in, m_conv_w, m_conv_b, m_w_rgate, m_b_rgate, m_w_igate, m_b_igate, m_lru_lambda, m_rg_norm_g, m_hg_lower_bound, m_hg_norm_g, m_w_out, m_ffn_norm_g, m_w_gate_up, m_w_down, m_final_norm_g, v_meta_tokens, v_mix_norm_g, v_w_in, v_conv_w, v_conv_b, v_w_rgate, v_b_rgate, v_w_igate, v_b_igate, v_lru_lambda, v_rg_norm_g, v_hg_lower_bound, v_hg_norm_g, v_w_out, v_ffn_norm_g, v_w_gate_up, v_w_down, v_final_norm_g):
    w_raw = dict(zip(WEIGHT_NAMES, (meta_tokens, mix_norm_g, w_in, conv_w, conv_b, w_rgate, b_rgate, w_igate, b_igate,
                                    lru_lambda, rg_norm_g, hg_lower_bound, hg_norm_g, w_out, ffn_norm_g, w_gate_up,
                                    w_down, final_norm_g)))
    m_raw = dict(zip(WEIGHT_NAMES, (m_meta_tokens, m_mix_norm_g, m_w_in, m_conv_w, m_conv_b, m_w_rgate, m_b_rgate,
                                    m_w_igate, m_b_igate, m_lru_lambda, m_rg_norm_g, m_hg_lower_bound, m_hg_norm_g,
                                    m_w_out, m_ffn_norm_g, m_w_gate_up, m_w_down, m_final_norm_g)))
    v_raw = dict(zip(WEIGHT_NAMES, (v_meta_tokens, v_mix_norm_g, v_w_in, v_conv_w, v_conv_b, v_w_rgate, v_b_rgate,
                                    v_w_igate, v_b_igate, v_lru_lambda, v_rg_norm_g, v_hg_lower_bound, v_hg_norm_g,
                                    v_w_out, v_ffn_norm_g, v_w_gate_up, v_w_down, v_final_norm_g)))
    w = {k: _to_2d(k, a) for k, a in w_raw.items()}
    m = {k: _to_2d(k, a) for k, a in m_raw.items()}
    v = {k: _to_2d(k, a) for k, a in v_raw.items()}

    x_i, y_i, c_i = _place()
    core = jnp.reshape(c_i, (1,)).astype(jnp.int32)
    chip = jnp.reshape(2 * x_i + y_i, (1,)).astype(jnp.int32)
    chip_core = jnp.concatenate([chip, core])

    placed = {k: _cast_into_full(w[k], k, chip) for k in BIG_NAMES}
    first, (meta_full, cw_full) = _gather_weights(placed, [w["meta_tokens"], w["conv_w"]], ("w_in",), "gather_first", 1)
    rest, _ = _gather_weights(placed, [], ("w_out", "w_gate_up", "w_down"), "gather_rest", 2)
    full = {**first, **rest}

    seq = x.shape[1]
    h0 = jnp.concatenate([jnp.zeros((PAD, D_MODEL), F32), meta_full, x[0]], axis=0)
    target = jnp.concatenate([jnp.zeros((PAD + N_META, D_MODEL), F32), loss_target[0]], axis=0)
    small = {k: w[k] for k in SMALL_NAMES if k not in SHARDED_SMALL}
    small["conv_w"] = cw_full

    def reduce_to_chips(grads, names, tag, collective_id):
        got = _exchange_halves(grads, names, "exchange_halves_" + tag)
        sums = {n: _chip_sum(grads[n], got[n], n, core) for n in names}
        arrived = _send_chip_sums({n: sums[n][1] for n in names}, names, "send_chip_sums_" + tag, collective_id)
        return {n: (sums[n][0], a) for n, a in zip(names, arrived)}

    ffn_names, mixer_names = ("w_gate_up", "w_down"), ("w_in", "w_out")
    loss, dh0, grads, parts = _local_step(
        h0, target, full["w_in"], full["w_out"], full["w_gate_up"], full["w_down"], small,
        on_ffn_grads=lambda g: reduce_to_chips(g, ffn_names, "ffn", 3))
    parts.update(reduce_to_chips(grads, mixer_names, "mixer", 4))
    totals = {n: _total(parts[n][0], parts[n][1], n, chip_core) for n in BIG_NAMES}
    g_big = dict(zip(BIG_NAMES, _share_totals(totals)))

    pieces = {k: grads[k] for k in VEC_ROW if k not in ("loss", "meta_tokens")}
    pieces["loss"] = loss
    pieces["meta_tokens"] = dh0[PAD:PAD + N_META]
    vec, gates = _all_reduce_small(pieces, grads["w_gates"])
    loss_sum, res = _adamw_small(vec, gates, w, m, v)
    for n in BIG_NAMES:
        res[n] = (g_big[n],) + tuple(_adamw_big(w[n], g_big[n], m[n], v[n], n))

    grad_x = dh0[PAD + N_META:].reshape(1, seq, D_MODEL)
    out = [loss_sum.reshape(()), grad_x]
    for j in range(4):
        out += [res[n][j].reshape(w_raw[n].shape) for n in WEIGHT_NAMES]
    return tuple(out)
```

```python
import functools
import math

import jax
import jax.numpy as jnp
from jax import lax
from jax.experimental import pallas as pl
from jax.experimental.pallas import tpu as pltpu
from jax.experimental.pallas import tpu_sc as plsc

F32 = jnp.float32
BF16 = jnp.bfloat16
HIGHEST = lax.Precision.HIGHEST
MESH = pl.DeviceIdType.MESH

D_MODEL = 1024
D_RG = 512
RG_HEAD_DIM = 64
D_HG = 512
HG_HEAD_DIM = 128
HG_HEADS = 4
CHUNK = 64
SUB = 16
N_SUB = CHUNK // SUB
N_META = 16
PAD = CHUNK - N_META
D_IN = 3072
D_FF = 2816
CONV_W = 4
LRU_C = 8.0
EPS = 1e-6
EXP_CLAMP = 80.0
GELU_C = math.sqrt(2.0 / math.pi)
GELU_A = 0.044715
N_CHIPS = 4

ADAM_LR = 0.001
ADAM_B1 = 0.9
ADAM_B2 = 0.999
ADAM_EPS = 1e-08
ADAM_WD = 0.01
ADAM_STEP = 10

VMEM_LIMIT = 56 * 1024 * 1024


def _params(*sem):
    return pltpu.CompilerParams(dimension_semantics=sem, vmem_limit_bytes=VMEM_LIMIT)


def _row_tile(rows, target):
    best = None
    for t in range(16, min(rows, target) + 1, 16):
        if rows % t == 0:
            best = t
    assert best is not None, rows
    return best


def _sigmoid(x):
    return 1.0 / (1.0 + jnp.exp(-x))


def _dot(a, b):
    return jnp.dot(a, b, preferred_element_type=F32)


def _dot_nt(a, b):
    return lax.dot_general(a, b, (((1,), (1,)), ((), ())), preferred_element_type=F32)


def _dot_tn(a, b):
    return lax.dot_general(a, b, (((0,), (0,)), ((), ())), preferred_element_type=F32)


def _rms(x):
    return lax.rsqrt(jnp.mean(x * x, axis=-1, keepdims=True) + EPS)


def _rms_bwd(dn, n, r):
    return r * (dn - n * jnp.mean(dn * n, axis=-1, keepdims=True))


def _gelu_parts(x):
    t = jnp.tanh(GELU_C * (x + GELU_A * x * x * x))
    g = 0.5 * x * (1.0 + t)
    dg = 0.5 * (1.0 + t) + 0.5 * x * (1.0 - t * t) * GELU_C * (1.0 + 3.0 * GELU_A * x * x)
    return g, dg


def _softplus_neg(lam):
    e = jnp.exp(-jnp.abs(lam))
    w = 1.0 + e
    log1p = jnp.where(w == 1.0, e, jnp.log(w) * e / (w - 1.0))
    return jnp.maximum(-lam, 0.0) + log1p


def _head_mask():
    r = lax.broadcasted_iota(jnp.int32, (D_RG, D_RG), 0) // RG_HEAD_DIM
    c = lax.broadcasted_iota(jnp.int32, (D_RG, D_RG), 1) // RG_HEAD_DIM
    return r == c


def _head_fold():
    r = lax.broadcasted_iota(jnp.int32, (D_RG, RG_HEAD_DIM), 0) % RG_HEAD_DIM
    c = lax.broadcasted_iota(jnp.int32, (D_RG, RG_HEAD_DIM), 1)
    return (r == c).astype(F32)


def _gate_weights(w_r, w_i):
    def body(wr_ref, wi_ref, o_ref):
        fold = _head_fold()
        mask = _head_mask()
        for k, ref in enumerate((wr_ref, wi_ref)):
            full = lax.dot_general(ref[...], fold, (((1,), (1,)), ((), ())),
                                   precision=HIGHEST, preferred_element_type=F32)
            o_ref[:, k * D_RG:(k + 1) * D_RG] = jnp.where(mask, full, 0.0).astype(BF16)

    return pl.pallas_call(
        body, out_shape=jax.ShapeDtypeStruct((D_RG, 2 * D_RG), BF16), name="gate_weights",
    )(w_r, w_i)


def _in_proj(h0, g1, w_in):
    T = h0.shape[0]
    tm = _row_tile(T, 416)

    def body(h_ref, g_ref, w_ref, p_ref, u_ref):
        h = h_ref[...]
        u = (h * _rms(h) * g_ref[...]).astype(BF16)
        u_ref[...] = u
        p_ref[...] = _dot(u, w_ref[...])

    return pl.pallas_call(
        body, grid=(T // tm,),
        in_specs=[pl.BlockSpec((tm, D_MODEL), lambda i: (i, 0)),
                  pl.BlockSpec((1, D_MODEL), lambda i: (0, 0)),
                  pl.BlockSpec((D_MODEL, D_IN), lambda i: (0, 0))],
        out_specs=[pl.BlockSpec((tm, D_IN), lambda i: (i, 0)),
                   pl.BlockSpec((tm, D_MODEL), lambda i: (i, 0))],
        out_shape=[jax.ShapeDtypeStruct((T, D_IN), F32), jax.ShapeDtypeStruct((T, D_MODEL), BF16)],
        name="in_proj", compiler_params=_params("parallel"),
    )(h0, g1, w_in)


def _scan_block_fwd(A, B, rowi):
    for d in (1, 2, 4):
        a_sh = pltpu.roll(A, d, axis=0)
        b_sh = pltpu.roll(B, d, axis=0)
        m = rowi >= d
        B = jnp.where(m, A * b_sh + B, B)
        A = jnp.where(m, A * a_sh, A)
    return A, B


def _scan_block_bwd(A, B, rowi):
    for d in (1, 2, 4):
        a_sh = pltpu.roll(A, 8 - d, axis=0)
        b_sh = pltpu.roll(B, 8 - d, axis=0)
        m = rowi < 8 - d
        B = jnp.where(m, A * b_sh + B, B)
        A = jnp.where(m, A * a_sh, A)
    return A, B


def _rg_gates(xc, w_ref, bg_ref, lam):
    pre = _dot(xc.astype(BF16), w_ref[...]) + bg_ref[...]
    r = _sigmoid(pre[:, :D_RG])
    ig = _sigmoid(pre[:, D_RG:])
    sp = _softplus_neg(lam)
    la = -LRU_C * sp * r
    a = jnp.exp(la)
    th = jnp.tanh(la)
    m = jnp.sqrt(-2.0 * th / (1.0 - th))
    return r, ig, sp, a, m


def _conv(ext, cw_ref, cb_ref, tm):
    xc = cb_ref[...] + cw_ref[0:1, :] * ext[8 - 3:8 - 3 + tm, :]
    for j in range(1, CONV_W):
        xc = xc + cw_ref[j:j + 1, :] * ext[8 - 3 + j:8 - 3 + j + tm, :]
    return xc


def _rg_fwd(p, cw, cb, wg, bg, lam, rg_g):
    T = p.shape[0]
    tm = _row_tile(T, 416)

    def body(xg_ref, cw_ref, cb_ref, w_ref, bg_ref, lam_ref, g_ref, y_ref, h_ref, ext, a_s, b_s, carry):
        i = pl.program_id(0)

        @pl.when(i == 0)
        def _():
            ext[0:8, :] = jnp.zeros((8, D_RG), F32)
            carry[...] = jnp.zeros((1, D_RG), F32)

        ext[8:8 + tm, :] = xg_ref[:, :D_RG]
        xc = _conv(ext, cw_ref, cb_ref, tm)
        r, ig, sp, a, m = _rg_gates(xc, w_ref, bg_ref, lam_ref[...])
        row = i * tm + lax.broadcasted_iota(jnp.int32, (tm, 1), 0)
        a_s[...] = a
        b_s[...] = jnp.where(row >= PAD, m * ig * xc, 0.0)
        rowi = lax.broadcasted_iota(jnp.int32, (8, D_RG), 0)

        def blk(j, c):
            o = pl.multiple_of(j * 8, 8)
            A, B = _scan_block_fwd(a_s[pl.ds(o, 8), :], b_s[pl.ds(o, 8), :], rowi)
            h = B + A * c
            h_ref[pl.ds(o, 8), :] = h
            return h[7:8, :]

        carry[...] = lax.fori_loop(0, tm // 8, blk, carry[...])
        ext[0:8, :] = ext[tm:tm + 8, :]
        g, _ = _gelu_parts(xg_ref[:, D_RG:])
        yy = g * h_ref[...]
        y_ref[...] = yy * _rms(yy) * g_ref[...]

    vec = lambda n: pl.BlockSpec((1, n), lambda i: (0, 0))
    return pl.pallas_call(
        body, grid=(T // tm,),
        in_specs=[pl.BlockSpec((tm, 2 * D_RG), lambda i: (i, 0)),
                  pl.BlockSpec((CONV_W, D_RG), lambda i: (0, 0)), vec(D_RG),
                  pl.BlockSpec((D_RG, 2 * D_RG), lambda i: (0, 0)), vec(2 * D_RG), vec(D_RG), vec(D_RG)],
        out_specs=[pl.BlockSpec((tm, D_RG), lambda i: (i, 0)), pl.BlockSpec((tm, D_RG), lambda i: (i, 0))],
        out_shape=[jax.ShapeDtypeStruct((T, D_RG), F32), jax.ShapeDtypeStruct((T, D_RG), F32)],
        scratch_shapes=[pltpu.VMEM((tm + 8, D_RG), F32), pltpu.VMEM((tm, D_RG), F32),
                        pltpu.VMEM((tm, D_RG), F32), pltpu.VMEM((1, D_RG), F32)],
        name="rg_fwd", compiler_params=_params("arbitrary"),
    )(p, cw, cb, wg, bg, lam, rg_g)


def _tri(lower):
    r = lax.broadcasted_iota(jnp.int32, (CHUNK, CHUNK), 0)
    c = lax.broadcasted_iota(jnp.int32, (CHUNK, CHUNK), 1)
    return ((c <= r) if lower else (c >= r)).astype(F32)


def _hg_gates(hq, hf, lbraw_ref, valid):
    lb = _sigmoid(lbraw_ref[0:1, :] - lbraw_ref[1:2, :])
    sq = _sigmoid(hq)
    q = hq * sq
    sf = _sigmoid(hf)
    f = lb + (1.0 - lb) * sf
    lf = jnp.where(valid, jnp.log(f), 0.0)
    b = jnp.dot(_tri(True), lf, precision=HIGHEST, preferred_element_type=F32)
    return lb, sq, q, sf, f, b


def _hg_head(qh, kh, bh):
    blk = lax.broadcasted_iota(jnp.int32, (CHUNK, 1), 0) // SUB
    b_last = bh[CHUNK - 1:CHUNK, :]
    refs = [bh[SUB * s:SUB * s + 1, :] for s in range(N_SUB)]
    r_sel = refs[N_SUB - 1]
    for s in range(N_SUB - 2, -1, -1):
        r_sel = jnp.where(blk == s, refs[s], r_sel)
    eb = jnp.exp(bh)
    eq = jnp.exp(bh - r_sel)
    ekh = jnp.exp(b_last - bh)
    ek = [jnp.exp(jnp.minimum(refs[s] - bh, EXP_CLAMP)) for s in range(N_SUB)]
    qe = qh * eq
    q_hat = jnp.concatenate([jnp.where(blk == s, qe, 0.0) for s in range(N_SUB)], axis=1)
    k_til = jnp.concatenate([kh * ek[s] for s in range(N_SUB)], axis=1)
    return blk, b_last, eb, eq, ekh, ek, q_hat, k_til


def _causal():
    r = lax.broadcasted_iota(jnp.int32, (CHUNK, CHUNK), 0)
    c = lax.broadcasted_iota(jnp.int32, (CHUNK, CHUNK), 1)
    return r >= c


def _chunks_per_step(n_chunks):
    for c in (5, 4, 3, 2):
        if n_chunks % c == 0:
            return c
    return 1


def _hg_fwd(p, lbraw, hg_g):
    T = p.shape[0]
    n_chunks = T // CHUNK
    cps = _chunks_per_step(n_chunks)
    rows = cps * CHUNK

    def body(hq_ref, hf_ref, hi_ref, hg_ref, lb_ref, g_ref, y_ref, o_ref, st_all_ref, st):
        i = pl.program_id(0)

        @pl.when(i == 0)
        def _():
            st[...] = jnp.zeros_like(st)

        def chunk(j, carry):
            rs = pl.ds(pl.multiple_of(j * CHUNK, CHUNK), CHUNK)
            chunk_body(i * cps + j, hq_ref.at[rs, :], hf_ref.at[rs, :], hi_ref.at[rs, :], hg_ref.at[rs, :], lb_ref,
                       g_ref, y_ref.at[rs, :], o_ref.at[rs, :], st_all_ref.at[pl.ds(j, 1)], st)
            return carry

        lax.fori_loop(0, cps, chunk, 0)

    def chunk_body(n, hq_ref, hf_ref, hi_ref, hg_ref, lb_ref, g_ref, y_ref, o_ref, st_all_ref, st):
        valid = (n * CHUNK + lax.broadcasted_iota(jnp.int32, (CHUNK, 1), 0)) >= PAD
        hq, hf, v, hg = hq_ref[...], hf_ref[...], hi_ref[...], hg_ref[...]
        lb, sq, q, sf, f, b = _hg_gates(hq, hf, lb_ref, valid)
        k = 1.0 - f
        st_all_ref[0] = st[...]
        causal = _causal()
        for h in range(HG_HEADS):
            sl = slice(h * HG_HEAD_DIM, (h + 1) * HG_HEAD_DIM)
            qh, kh, vh, bh = q[:, sl], k[:, sl], v[:, sl], b[:, sl]
            st_h = st[sl, :]
            _, b_last, eb, _, ekh, _, q_hat, k_til = _hg_head(qh, kh, bh)
            vb = vh.astype(BF16)
            inter = _dot_nt((qh * eb).astype(BF16), st_h.astype(BF16))
            att = jnp.where(causal, _dot_nt(q_hat.astype(BF16), k_til.astype(BF16)), 0.0)
            o = inter + _dot(att.astype(BF16), vb)
            st[sl, :] = st_h * jnp.exp(b_last) + _dot_tn(vb, (kh * ekh).astype(BF16))
            o_ref[:, sl] = o
            hgh = hg[:, sl]
            y_ref[:, sl] = o * _rms(o) * g_ref[...] * (hgh * _sigmoid(hgh))

    col = lambda j: pl.BlockSpec((rows, D_HG), lambda n: (n, j))
    return pl.pallas_call(
        body, grid=(n_chunks // cps,),
        in_specs=[col(2), col(3), col(4), col(5),
                  pl.BlockSpec((2, D_HG), lambda n: (0, 0)), pl.BlockSpec((1, HG_HEAD_DIM), lambda n: (0, 0))],
        out_specs=[pl.BlockSpec((rows, D_HG), lambda n: (n, 0)), pl.BlockSpec((rows, D_HG), lambda n: (n, 0)),
                   pl.BlockSpec((cps, D_HG, HG_HEAD_DIM), lambda n: (n, 0, 0))],
        out_shape=[jax.ShapeDtypeStruct((T, D_HG), F32), jax.ShapeDtypeStruct((T, D_HG), F32),
                   jax.ShapeDtypeStruct((n_chunks, D_HG, HG_HEAD_DIM), F32)],
        scratch_shapes=[pltpu.VMEM((D_HG, HG_HEAD_DIM), F32)],
        name="hg_fwd", compiler_params=_params("arbitrary"),
    )(p, p, p, p, lbraw, hg_g)


def _out_proj(h0, y_rg, y_hg, w_out, g2):
    T = h0.shape[0]
    tm = _row_tile(T, 832)

    def body(h_ref, yr_ref, yh_ref, w_ref, g_ref, h1_ref, v_ref, y_ref):
        y_ref[:, :D_RG] = yr_ref[...].astype(BF16)
        y_ref[:, D_RG:] = yh_ref[...].astype(BF16)
        h1 = h_ref[...] + _dot(y_ref[...], w_ref[...])
        h1_ref[...] = h1
        v_ref[...] = (h1 * _rms(h1) * g_ref[...]).astype(BF16)

    row = lambda n: pl.BlockSpec((tm, n), lambda i: (i, 0))
    return pl.pallas_call(
        body, grid=(T // tm,),
        in_specs=[row(D_MODEL), row(D_RG), row(D_HG), pl.BlockSpec((D_MODEL, D_MODEL), lambda i: (0, 0)),
                  pl.BlockSpec((1, D_MODEL), lambda i: (0, 0))],
        out_specs=[row(D_MODEL), row(D_MODEL), row(D_MODEL)],
        out_shape=[jax.ShapeDtypeStruct((T, D_MODEL), F32), jax.ShapeDtypeStruct((T, D_MODEL), BF16),
                   jax.ShapeDtypeStruct((T, D_MODEL), BF16)],
        name="out_proj", compiler_params=_params("parallel"),
    )(h0, y_rg, y_hg, w_out, g2)


def _gate_up(v, w_gu):
    T = v.shape[0]
    tm = _row_tile(T, 416)
    tn = D_FF // 2

    def body(v_ref, wg_ref, wu_ref, g_ref, u_ref, act_ref):
        x = v_ref[...]
        g = _dot(x, wg_ref[...])
        u = _dot(x, wu_ref[...])
        g_ref[...] = g
        u_ref[...] = u
        act_ref[...] = (g * _sigmoid(g) * u).astype(BF16)

    blk = pl.BlockSpec((tm, tn), lambda j, i: (i, j))
    return pl.pallas_call(
        body, grid=(2, T // tm),
        in_specs=[pl.BlockSpec((tm, D_MODEL), lambda j, i: (i, 0)),
                  pl.BlockSpec((D_MODEL, tn), lambda j, i: (0, j)),
                  pl.BlockSpec((D_MODEL, tn), lambda j, i: (0, j + 2))],
        out_specs=[blk, blk, blk],
        out_shape=[jax.ShapeDtypeStruct((T, D_FF), F32), jax.ShapeDtypeStruct((T, D_FF), F32),
                   jax.ShapeDtypeStruct((T, D_FF), BF16)],
        name="gate_up", compiler_params=_params("parallel", "parallel"),
    )(v, w_gu, w_gu)


def _down_loss(h1, act, w_down, gf, target):
    T = h1.shape[0]
    tm = _row_tile(T, 832)

    def body(h_ref, a_ref, w_ref, g_ref, t_ref, dh2_ref, dh2b_ref, loss_ref, gg_ref):
        i = pl.program_id(0)

        @pl.when(i == 0)
        def _():
            loss_ref[...] = jnp.zeros_like(loss_ref)
            gg_ref[...] = jnp.zeros_like(gg_ref)

        h2 = h_ref[...] + _dot(a_ref[...], w_ref[...])
        r = _rms(h2)
        n = h2 * r
        gf_ = g_ref[...]
        row = i * tm + lax.broadcasted_iota(jnp.int32, (tm, 1), 0)
        err = jnp.where(row >= PAD + N_META, n * gf_ - t_ref[...], 0.0)
        loss_ref[...] += 0.5 * jnp.sum(jnp.mean(err * err, axis=-1, keepdims=True), axis=0, keepdims=True)
        dy = err * (1.0 / D_MODEL)
        gg_ref[...] += jnp.sum(dy * n, axis=0, keepdims=True)
        dh2 = _rms_bwd(dy * gf_, n, r)
        dh2_ref[...] = dh2
        dh2b_ref[...] = dh2.astype(BF16)

    row_spec = lambda n: pl.BlockSpec((tm, n), lambda i: (i, 0))
    return pl.pallas_call(
        body, grid=(T // tm,),
        in_specs=[row_spec(D_MODEL), row_spec(D_FF), pl.BlockSpec((D_FF, D_MODEL), lambda i: (0, 0)),
                  pl.BlockSpec((1, D_MODEL), lambda i: (0, 0)), row_spec(D_MODEL)],
        out_specs=[row_spec(D_MODEL), row_spec(D_MODEL), pl.BlockSpec((1, 1), lambda i: (0, 0)),
                   pl.BlockSpec((1, D_MODEL), lambda i: (0, 0))],
        out_shape=[jax.ShapeDtypeStruct((T, D_MODEL), F32), jax.ShapeDtypeStruct((T, D_MODEL), BF16),
                   jax.ShapeDtypeStruct((1, 1), F32), jax.ShapeDtypeStruct((1, D_MODEL), F32)],
        name="down_loss", compiler_params=_params("arbitrary"),
    )(h1, act, w_down, gf, target)


def _ffn_bwd_act(dh2b, gate, up, w_down):
    T = dh2b.shape[0]
    tm = _row_tile(T, 208)

    def body(d_ref, g_ref, u_ref, w_ref, dgu_ref):
        dact = _dot_nt(d_ref[...], w_ref[...])
        g = g_ref[...]
        s = _sigmoid(g)
        dgu_ref[:, :D_FF] = (dact * u_ref[...] * s * (1.0 + g * (1.0 - s))).astype(BF16)
        dgu_ref[:, D_FF:] = (dact * g * s).astype(BF16)

    row = lambda n: pl.BlockSpec((tm, n), lambda i: (i, 0))
    return pl.pallas_call(
        body, grid=(T // tm,),
        in_specs=[row(D_MODEL), row(D_FF), row(D_FF), pl.BlockSpec((D_FF, D_MODEL), lambda i: (0, 0))],
        out_specs=row(2 * D_FF),
        out_shape=jax.ShapeDtypeStruct((T, 2 * D_FF), BF16),
        name="ffn_bwd_act", compiler_params=_params("parallel"),
    )(dh2b, gate, up, w_down)


def _ffn_bwd_in(dgu, w_gu, h1, g2, dh2, w_out):
    T = h1.shape[0]
    tm = _row_tile(T, 416)

    def body(dgu_ref, wgu_ref, h_ref, g_ref, d2_ref, wo_ref, dh1_ref, dh1b_ref, dy_ref, gg_ref):
        i = pl.program_id(0)

        @pl.when(i == 0)
        def _():
            gg_ref[...] = jnp.zeros_like(gg_ref)

        dv = _dot_nt(dgu_ref[...], wgu_ref[...])
        h1 = h_ref[...]
        r = _rms(h1)
        n = h1 * r
        gg_ref[...] += jnp.sum(dv * n, axis=0, keepdims=True)
        dh1 = d2_ref[...] + _rms_bwd(dv * g_ref[...], n, r)
        dh1_ref[...] = dh1
        db = dh1.astype(BF16)
        dh1b_ref[...] = db
        dy_ref[...] = _dot_nt(db, wo_ref[...])

    row = lambda n: pl.BlockSpec((tm, n), lambda i: (i, 0))
    return pl.pallas_call(
        body, grid=(T // tm,),
        in_specs=[row(2 * D_FF), pl.BlockSpec((D_MODEL, 2 * D_FF), lambda i: (0, 0)),
                  row(D_MODEL), pl.BlockSpec((1, D_MODEL), lambda i: (0, 0)), row(D_MODEL),
                  pl.BlockSpec((D_MODEL, D_MODEL), lambda i: (0, 0))],
        out_specs=[row(D_MODEL), row(D_MODEL), row(D_MODEL), pl.BlockSpec((1, D_MODEL), lambda i: (0, 0))],
        out_shape=[jax.ShapeDtypeStruct((T, D_MODEL), F32), jax.ShapeDtypeStruct((T, D_MODEL), BF16),
                   jax.ShapeDtypeStruct((T, D_MODEL), F32), jax.ShapeDtypeStruct((1, D_MODEL), F32)],
        name="ffn_bwd_in", compiler_params=_params("arbitrary"),
    )(dgu, w_gu, h1, g2, dh2, w_out)


def _rg_bwd(p, hs, dy, cw, cb, wg, bg, lam, rg_g):
    T = p.shape[0]
    tm = _row_tile(T, 208)
    nt = T // tm
    hb = tm // 8

    def body(xg_ref, xh_ref, h_ref, hh_ref, dy_ref, cw_ref, cb_ref, w_ref, bg_ref, lam_ref, g_ref,
             dp_ref, gcw_ref, gcb_ref, gw_ref, gbg_ref, glam_ref, gg_ref,
             ext, dext, a_s, b_s, d_s, gacc, carry_d, carry_a):
        i = pl.program_id(0)
        t_idx = nt - 1 - i

        @pl.when(i == 0)
        def _():
            dext[tm:tm + 8, :] = jnp.zeros((8, D_RG), F32)
            carry_d[...] = jnp.zeros_like(carry_d)
            carry_a[...] = jnp.zeros_like(carry_a)
            gacc[...] = jnp.zeros_like(gacc)
            for ref in (gcw_ref, gcb_ref, gbg_ref, glam_ref, gg_ref, gw_ref):
                ref[...] = jnp.zeros_like(ref)

        first = t_idx == 0
        ext[0:8, :] = jnp.where(first, 0.0, xh_ref[:, :D_RG])
        ext[8:8 + tm, :] = xg_ref[:, :D_RG]
        xc = _conv(ext, cw_ref, cb_ref, tm)
        lam_ = lam_ref[...]
        r, ig, sp, a, m = _rg_gates(xc, w_ref, bg_ref, lam_)
        row = t_idx * tm + lax.broadcasted_iota(jnp.int32, (tm, 1), 0)
        valid = row >= PAD

        gr = xg_ref[:, D_RG:]
        g, dgelu = _gelu_parts(gr)
        h = h_ref[...]
        yy = g * h
        rr = _rms(yy)
        nn = yy * rr
        dy_ = dy_ref[...]
        gg_ref[...] += jnp.sum(dy_ * nn, axis=0, keepdims=True)
        dyy = _rms_bwd(dy_ * g_ref[...], nn, rr)
        dp_ref[:, D_RG:] = dyy * h * dgelu

        a_s[...] = a
        b_s[...] = dyy * g
        rowi = lax.broadcasted_iota(jnp.int32, (8, D_RG), 0)

        def blk(jj, c):
            cd, ca = c
            o = pl.multiple_of((hb - 1 - jj) * 8, 8)
            a_blk = a_s[pl.ds(o, 8), :]
            a_next = jnp.where(rowi == 7, ca, pltpu.roll(a_blk, 7, axis=0))
            A, B = _scan_block_bwd(a_next, b_s[pl.ds(o, 8), :], rowi)
            d = B + A * cd
            d_s[pl.ds(o, 8), :] = d
            return d[0:1, :], a_blk[0:1, :]

        cd, ca = lax.fori_loop(0, hb, blk, (carry_d[...], carry_a[...]))
        carry_d[...] = cd
        carry_a[...] = ca
        delta = d_s[...]

        h_last_prev = jnp.where(first, 0.0, hh_ref[7:8, :])
        row0 = lax.broadcasted_iota(jnp.int32, (tm, 1), 0) == 0
        h_prev = jnp.where(row0, h_last_prev, pltpu.roll(h, 1, axis=0))
        dbx = jnp.where(valid, delta, 0.0)
        da = delta * h_prev
        di = dbx * m * xc
        dm = dbx * ig * xc
        dla = a * (da - dm * a / m)
        dla = jnp.where(valid, dla, 0.0)
        glam_ref[...] += jnp.sum(dla * r, axis=0, keepdims=True) * (LRU_C * _sigmoid(-lam_))
        dr = (-LRU_C) * sp * dla
        dpre = jnp.concatenate([dr * r * (1.0 - r), di * ig * (1.0 - ig)], axis=1)
        gbg_ref[...] += jnp.sum(dpre, axis=0, keepdims=True)
        dpre_b = dpre.astype(BF16)
        gacc[...] += _dot_tn(xc.astype(BF16), dpre_b)
        dxc = dbx * m * ig + _dot_nt(dpre_b, w_ref[...])
        gcb_ref[...] += jnp.sum(dxc, axis=0, keepdims=True)
        for j in range(CONV_W):
            gcw_ref[j:j + 1, :] += jnp.sum(dxc * ext[8 - 3 + j:8 - 3 + j + tm, :], axis=0, keepdims=True)
        dext[0:tm, :] = dxc
        dxr = cw_ref[0:1, :] * dext[3:3 + tm, :]
        for j in range(1, CONV_W):
            dxr = dxr + cw_ref[j:j + 1, :] * dext[3 - j:3 - j + tm, :]
        dp_ref[:, :D_RG] = dxr
        dext[tm:tm + 8, :] = dext[0:8, :]

        @pl.when(i == nt - 1)
        def _():
            fold = _head_fold()
            mask = _head_mask()
            for k in range(2):
                blockdiag = jnp.where(mask, gacc[:, k * D_RG:(k + 1) * D_RG], 0.0)
                gw_ref[k * D_RG:(k + 1) * D_RG, :] = jnp.dot(blockdiag, fold, precision=HIGHEST,
                                                             preferred_element_type=F32)

    vec = lambda n: pl.BlockSpec((1, n), lambda i: (0, 0))
    rev = lambda n: pl.BlockSpec((tm, n), lambda i: (nt - 1 - i, 0))
    halo = lambda n: pl.BlockSpec((8, n), lambda i: (jnp.maximum((nt - 1 - i) * hb - 1, 0), 0))
    return pl.pallas_call(
        body, grid=(nt,),
        in_specs=[rev(2 * D_RG), halo(2 * D_RG), rev(D_RG), halo(D_RG), rev(D_RG),
                  pl.BlockSpec((CONV_W, D_RG), lambda i: (0, 0)), vec(D_RG),
                  pl.BlockSpec((D_RG, 2 * D_RG), lambda i: (0, 0)), vec(2 * D_RG), vec(D_RG), vec(D_RG)],
        out_specs=[rev(2 * D_RG), pl.BlockSpec((CONV_W, D_RG), lambda i: (0, 0)), vec(D_RG),
                   pl.BlockSpec((2 * D_RG, RG_HEAD_DIM), lambda i: (0, 0)), vec(2 * D_RG), vec(D_RG), vec(D_RG)],
        out_shape=[jax.ShapeDtypeStruct((T, 2 * D_RG), F32), jax.ShapeDtypeStruct((CONV_W, D_RG), F32),
                   jax.ShapeDtypeStruct((1, D_RG), F32), jax.ShapeDtypeStruct((2 * D_RG, RG_HEAD_DIM), F32),
                   jax.ShapeDtypeStruct((1, 2 * D_RG), F32), jax.ShapeDtypeStruct((1, D_RG), F32),
                   jax.ShapeDtypeStruct((1, D_RG), F32)],
        scratch_shapes=[pltpu.VMEM((tm + 8, D_RG), F32), pltpu.VMEM((tm + 8, D_RG), F32),
                        pltpu.VMEM((tm, D_RG), F32), pltpu.VMEM((tm, D_RG), F32), pltpu.VMEM((tm, D_RG), F32),
                        pltpu.VMEM((D_RG, 2 * D_RG), F32), pltpu.VMEM((1, D_RG), F32), pltpu.VMEM((1, D_RG), F32)],
        name="rg_bwd", compiler_params=_params("arbitrary"),
    )(p, p, hs, hs, dy, cw, cb, wg, bg, lam, rg_g)


def _hg_bwd(p, o_all, st_all, dy, lbraw, hg_g):
    T = p.shape[0]
    n_chunks = T // CHUNK
    cps = _chunks_per_step(n_chunks)
    rows = cps * CHUNK
    n_steps = n_chunks // cps

    def body(hq_ref, hf_ref, hi_ref, hg_ref, o_ref, st_ref, dy_ref, lb_ref, g_ref,
             dp_ref, glb_ref, gg_ref, dst):
        i = pl.program_id(0)

        @pl.when(i == 0)
        def _():
            dst[...] = jnp.zeros_like(dst)
            glb_ref[...] = jnp.zeros_like(glb_ref)
            gg_ref[...] = jnp.zeros_like(gg_ref)

        def chunk(jj, carry):
            j = cps - 1 - jj
            rs = pl.ds(pl.multiple_of(j * CHUNK, CHUNK), CHUNK)
            chunk_body((n_steps - 1 - i) * cps + j, hq_ref.at[rs, :], hf_ref.at[rs, :], hi_ref.at[rs, :],
                       hg_ref.at[rs, :], o_ref.at[rs, :], st_ref.at[pl.ds(j, 1)], dy_ref.at[rs, :], lb_ref, g_ref,
                       dp_ref.at[rs, :], glb_ref, gg_ref, dst)
            return carry

        lax.fori_loop(0, cps, chunk, 0)

    def chunk_body(n, hq_ref, hf_ref, hi_ref, hg_ref, o_ref, st_ref, dy_ref, lb_ref, g_ref,
                   dp_ref, glb_ref, gg_ref, dst):
        valid = (n * CHUNK + lax.broadcasted_iota(jnp.int32, (CHUNK, 1), 0)) >= PAD
        hq, hf, v, hg = hq_ref[...], hf_ref[...], hi_ref[...], hg_ref[...]
        lb, sq, q, sf, f, b = _hg_gates(hq, hf, lb_ref, valid)
        k = 1.0 - f
        causal = _causal()
        is_last = lax.broadcasted_iota(jnp.int32, (CHUNK, 1), 0) == CHUNK - 1
        g_ = g_ref[...]
        db_parts, dq_parts, dk_parts = [], [], []
        gg = jnp.zeros((1, HG_HEAD_DIM), F32)
        for h in range(HG_HEADS):
            sl = slice(h * HG_HEAD_DIM, (h + 1) * HG_HEAD_DIM)
            qh, kh, vh, bh = q[:, sl], k[:, sl], v[:, sl], b[:, sl]
            o = o_ref[:, sl]
            ro = _rms(o)
            no = o * ro
            hgh = hg[:, sl]
            sg = _sigmoid(hgh)
            dyh = dy_ref[:, sl]
            dp_ref[:, 3 * D_HG + h * HG_HEAD_DIM:3 * D_HG + (h + 1) * HG_HEAD_DIM] = (
                dyh * no * g_ * sg * (1.0 + hgh * (1.0 - sg)))
            dng = dyh * hgh * sg
            gg = gg + jnp.sum(dng * no, axis=0, keepdims=True)
            do = _rms_bwd(dng * g_, no, ro)
            dob = do.astype(BF16)

            st_h = st_ref[0, sl, :]
            dst_h = dst[sl, :]
            blk, b_last, eb, eq, ekh, ek, q_hat, k_til = _hg_head(qh, kh, bh)
            q_til = qh * eb
            k_hat = kh * ekh
            vb = vh.astype(BF16)
            dstb = dst_h.astype(BF16)
            qhb, ktb = q_hat.astype(BF16), k_til.astype(BF16)
            att = jnp.where(causal, _dot_nt(qhb, ktb), 0.0)
            datt = jnp.where(causal, _dot_nt(dob, vb), 0.0).astype(BF16)

            dk_hat = _dot(vb, dstb)
            dv = _dot_nt(k_hat.astype(BF16), dstb) + _dot_tn(att.astype(BF16), dob)
            dq_til = _dot(dob, st_h.astype(BF16))
            e_last = jnp.exp(b_last)
            db_last = (jnp.sum(dk_hat * k_hat, axis=0, keepdims=True)
                       + e_last * jnp.sum(dst_h * st_h, axis=0, keepdims=True))
            dst[sl, :] = dst_h * e_last + _dot_tn(dob, q_til.astype(BF16))

            dq_hat = _dot(datt, ktb)
            dk_til = _dot_tn(datt, qhb)
            dq_sel = dq_hat[:, (N_SUB - 1) * HG_HEAD_DIM:]
            for s in range(N_SUB - 2, -1, -1):
                dq_sel = jnp.where(blk == s, dq_hat[:, s * HG_HEAD_DIM:(s + 1) * HG_HEAD_DIM], dq_sel)
            dq_a = dq_sel * eq
            dk_a = dk_til[:, :HG_HEAD_DIM] * ek[0]
            for s in range(1, N_SUB):
                dk_a = dk_a + dk_til[:, s * HG_HEAD_DIM:(s + 1) * HG_HEAD_DIM] * ek[s]
            db_att = qhb.astype(F32) * dq_hat - ktb.astype(F32) * dk_til
            db = dq_til * q_til - dk_hat * k_hat
            for s in range(N_SUB):
                db = db + db_att[:, s * HG_HEAD_DIM:(s + 1) * HG_HEAD_DIM]
            db_parts.append(jnp.where(is_last, db + db_last, db))
            dq_parts.append(dq_til * eb + dq_a)
            dk_parts.append(dk_hat * ekh + dk_a)
            dp_ref[:, 2 * D_HG + h * HG_HEAD_DIM:2 * D_HG + (h + 1) * HG_HEAD_DIM] = dv

        gg_ref[...] += gg
        db = jnp.concatenate(db_parts, axis=1)
        dq = jnp.concatenate(dq_parts, axis=1)
        dk = jnp.concatenate(dk_parts, axis=1)
        dlf = jnp.where(valid, jnp.dot(_tri(False), db, precision=HIGHEST, preferred_element_type=F32), 0.0)
        dp_ref[:, :D_HG] = dq * sq * (1.0 + hq * (1.0 - sq))
        df = dlf / f - dk
        dlb = jnp.sum(df * (1.0 - sf), axis=0, keepdims=True) * lb * (1.0 - lb)
        glb_ref[0:1, :] += dlb
        glb_ref[1:2, :] += -dlb
        dp_ref[:, D_HG:2 * D_HG] = df * (1.0 - lb) * sf * (1.0 - sf)

    rev = lambda j: pl.BlockSpec((rows, D_HG), lambda i: (n_steps - 1 - i, j))
    return pl.pallas_call(
        body, grid=(n_steps,),
        in_specs=[rev(2), rev(3), rev(4), rev(5), rev(0),
                  pl.BlockSpec((cps, D_HG, HG_HEAD_DIM), lambda i: (n_steps - 1 - i, 0, 0)), rev(1),
                  pl.BlockSpec((2, D_HG), lambda i: (0, 0)), pl.BlockSpec((1, HG_HEAD_DIM), lambda i: (0, 0))],
        out_specs=[pl.BlockSpec((rows, 4 * D_HG), lambda i: (n_steps - 1 - i, 0)),
                   pl.BlockSpec((2, D_HG), lambda i: (0, 0)), pl.BlockSpec((1, HG_HEAD_DIM), lambda i: (0, 0))],
        out_shape=[jax.ShapeDtypeStruct((T, 4 * D_HG), F32), jax.ShapeDtypeStruct((2, D_HG), F32),
                   jax.ShapeDtypeStruct((1, HG_HEAD_DIM), F32)],
        scratch_shapes=[pltpu.VMEM((D_HG, HG_HEAD_DIM), F32)],
        name="hg_bwd", compiler_params=_params("arbitrary"),
    )(p, p, p, p, o_all, st_all, dy, lbraw, hg_g)


def _in_bwd(dp_rg, dp_hg, w_in, h0, g1, dh1):
    T = h0.shape[0]
    tm = _row_tile(T, 416)

    def body(dr_ref, dh_ref, w_ref, h_ref, g_ref, d1_ref, dh0_ref, dpb_ref, gg_ref):
        i = pl.program_id(0)

        @pl.when(i == 0)
        def _():
            gg_ref[...] = jnp.zeros_like(gg_ref)

        dpb_ref[:, :2 * D_RG] = dr_ref[...].astype(BF16)
        dpb_ref[:, 2 * D_RG:] = dh_ref[...].astype(BF16)
        du = _dot_nt(dpb_ref[...], w_ref[...])
        h0_ = h_ref[...]
        r = _rms(h0_)
        n = h0_ * r
        gg_ref[...] += jnp.sum(du * n, axis=0, keepdims=True)
        dh0_ref[...] = d1_ref[...] + _rms_bwd(du * g_ref[...], n, r)

    row = lambda n: pl.BlockSpec((tm, n), lambda i: (i, 0))
    return pl.pallas_call(
        body, grid=(T // tm,),
        in_specs=[row(2 * D_RG), row(4 * D_HG), pl.BlockSpec((D_MODEL, D_IN), lambda i: (0, 0)),
                  row(D_MODEL), pl.BlockSpec((1, D_MODEL), lambda i: (0, 0)), row(D_MODEL)],
        out_specs=[row(D_MODEL), row(D_IN), pl.BlockSpec((1, D_MODEL), lambda i: (0, 0))],
        out_shape=[jax.ShapeDtypeStruct((T, D_MODEL), F32), jax.ShapeDtypeStruct((T, D_IN), BF16),
                   jax.ShapeDtypeStruct((1, D_MODEL), F32)],
        name="in_bwd", compiler_params=_params("arbitrary"),
    )(dp_rg, dp_hg, w_in, h0, g1, dh1)


def _col_tile(cols, target):
    best = None
    for t in range(128, min(cols, target) + 1, 128):
        if cols % t == 0:
            best = t
    assert best is not None, cols
    return best


def _weight_grad(a, b, name):
    T, M = a.shape
    N = b.shape[1]
    tk = _row_tile(T, 832)
    tm = _col_tile(M, 512)
    tn = _col_tile(N, 1536)

    def body(a_ref, b_ref, o_ref):
        @pl.when(pl.program_id(2) == 0)
        def _():
            o_ref[...] = jnp.zeros_like(o_ref)

        o_ref[...] += _dot_tn(a_ref[...], b_ref[...])

    return pl.pallas_call(
        body, grid=(M // tm, N // tn, T // tk),
        in_specs=[pl.BlockSpec((tk, tm), lambda m, n, k: (k, m)), pl.BlockSpec((tk, tn), lambda m, n, k: (k, n))],
        out_specs=pl.BlockSpec((tm, tn), lambda m, n, k: (m, n)),
        out_shape=jax.ShapeDtypeStruct((M, N), F32),
        name=name, compiler_params=_params("parallel", "parallel", "arbitrary"),
    )(a, b)


def _local_step(h0, target, w_in, w_out, w_gu, w_down, small, on_ffn_grads=None):
    wg = _gate_weights(small["w_rgate"], small["w_igate"])
    bg = jnp.concatenate([small["b_rgate"], small["b_igate"]], axis=1)

    p, u = _in_proj(h0, small["mix_norm_g"], w_in)
    y_rg, hs = _rg_fwd(p, small["conv_w"], small["conv_b"], wg, bg, small["lru_lambda"], small["rg_norm_g"])
    y_hg, o_all, st_all = _hg_fwd(p, small["hg_lower_bound"], small["hg_norm_g"])
    h1, v, yb = _out_proj(h0, y_rg, y_hg, w_out, small["ffn_norm_g"])
    gate, up, act = _gate_up(v, w_gu)
    dh2, dh2b, loss, g_final = _down_loss(h1, act, w_down, small["final_norm_g"], target)

    dgu = _ffn_bwd_act(dh2b, gate, up, w_down)
    ffn_grads = {"w_gate_up": _weight_grad(v, dgu, "grad_w_gate_up"),
                 "w_down": _weight_grad(act, dh2b, "grad_w_down")}
    stages = on_ffn_grads(ffn_grads) if on_ffn_grads is not None else None
    dh1, dh1b, dy, g_ffn = _ffn_bwd_in(dgu, w_gu, h1, small["ffn_norm_g"], dh2, w_out)
    early = None
    if stages is not None:
        chip_sums, send = stages
        sums = chip_sums()
        (dh1, dh1b, dy), sums = lax.optimization_barrier(((dh1, dh1b, dy), sums))
        early = send(sums)
    dp_rg, g_cw, g_cb, g_wgate, g_bg, g_lam, g_rgn = _rg_bwd(
        p, hs, dy, small["conv_w"], small["conv_b"], wg, bg, small["lru_lambda"], small["rg_norm_g"])
    dp_hg, g_lb, g_hgn = _hg_bwd(p, o_all, st_all, dy, small["hg_lower_bound"], small["hg_norm_g"])
    dh0, dpb, g_mix = _in_bwd(dp_rg, dp_hg, w_in, h0, small["mix_norm_g"], dh1)

    grads = {
        "w_in": _weight_grad(u, dpb, "grad_w_in"),
        "w_out": _weight_grad(yb, dh1b, "grad_w_out"),
        "w_gate_up": ffn_grads["w_gate_up"], "w_down": ffn_grads["w_down"],
        "mix_norm_g": g_mix, "conv_w": g_cw, "conv_b": g_cb, "w_gates": g_wgate,
        "b_rgate": g_bg[:, :D_RG], "b_igate": g_bg[:, D_RG:], "lru_lambda": g_lam, "rg_norm_g": g_rgn,
        "hg_lower_bound": g_lb, "hg_norm_g": g_hgn, "ffn_norm_g": g_ffn, "final_norm_g": g_final,
    }
    return loss, dh0, grads, early


ANY = pl.BlockSpec(memory_space=pl.ANY)
HALF = D_MODEL // 2

BIG = {"w_in": (D_MODEL, D_IN // N_CHIPS, True), "w_gate_up": (D_MODEL, 2 * D_FF // N_CHIPS, True),
       "w_out": (D_MODEL // N_CHIPS, D_MODEL, False), "w_down": (D_FF // N_CHIPS, D_MODEL, False)}
BIG_NAMES = tuple(BIG)
N_BIG = len(BIG_NAMES)


def _full_shape(name):
    rows, cols, by_col = BIG[name]
    return (rows, cols * N_CHIPS) if by_col else (rows * N_CHIPS, cols)


def _place():
    return lax.axis_index("x"), lax.axis_index("y"), lax.axis_index("c")


def _chip_of(x, y, r):
    fx, fy = (r + 1) >> 1, (r + 1) & 1
    return (1 - x if fx else x), (1 - y if fy else y)


def _half_of(ref, by_col, half):
    start = pl.multiple_of(half * HALF, 128)
    return ref.at[pl.ds(start, HALF), :] if by_col else ref.at[:, pl.ds(start, HALF)]


def _shard_of(ref, name, chip):
    rows, cols, by_col = BIG[name]
    if by_col:
        return ref.at[:, pl.ds(pl.multiple_of(chip * cols, 128), cols)]
    return ref.at[pl.ds(pl.multiple_of(chip * rows, 16), rows), :]


def _shard_half_of(ref, name, chip, half):
    rows, cols, by_col = BIG[name]
    start = pl.multiple_of(half * HALF, 128)
    if by_col:
        return ref.at[pl.ds(start, HALF), pl.ds(pl.multiple_of(chip * cols, 128), cols)]
    return ref.at[pl.ds(pl.multiple_of(chip * rows, 16), rows), pl.ds(start, HALF)]


def _remote(src, dst, send_sems, recv_sems, k, dev):
    return pltpu.make_async_remote_copy(src_ref=src, dst_ref=dst, send_sem=send_sems.at[k], recv_sem=recv_sems.at[k],
                                        device_id=dev, device_id_type=MESH)


def _cast_into_full(w_shard, name, chip):
    rows, cols, by_col = BIG[name]
    tr = _row_tile(rows, 352)
    if by_col:
        out_spec = pl.BlockSpec((tr, cols), lambda i, s: (i, s[0]))
    else:
        out_spec = pl.BlockSpec((tr, cols), lambda i, s: (s[0] * (rows // tr) + i, 0))

    def body(s_ref, w_ref, o_ref):
        o_ref[...] = w_ref[...].astype(BF16)

    return pl.pallas_call(
        body,
        grid_spec=pltpu.PrefetchScalarGridSpec(
            num_scalar_prefetch=1, grid=(rows // tr,), in_specs=[pl.BlockSpec((tr, cols), lambda i, s: (i, 0))],
            out_specs=out_spec),
        out_shape=jax.ShapeDtypeStruct(_full_shape(name), BF16),
        name="cast_" + name, compiler_params=_params("parallel"),
    )(chip, w_shard)


def _gather_weights(placed, small, names, label, collective_id):
    n, ns = len(names), len(small)
    hbm = pltpu.MemorySpace.HBM
    outs = [jax.new_ref(placed[nm], memory_space=hbm) for nm in names]
    small_in = [jax.new_ref(s, memory_space=hbm) for s in small]
    small_out = [jax.empty_ref(jax.ShapeDtypeStruct((s.shape[0], s.shape[1] * N_CHIPS), F32), memory_space=hbm)
                 for s in small]
    n_sems = 6 * n + 3 * ns

    @pl.kernel(mesh=plsc.ScalarSubcoreMesh(axis_name="seq", num_cores=1), name=label, out_type=(),
               scratch_types=(pltpu.SemaphoreType.DMA((n_sems,)), pltpu.SemaphoreType.DMA((n_sems,)),
                              pltpu.SemaphoreType.DMA((max(ns, 1),))),
               compiler_params=pltpu.CompilerParams(collective_id=collective_id))
    def launch(send_sems, recv_sems, local_sems):
        x, y, c = _place()
        chip = 2 * x + y
        sibling = (x, y, 1 - c)
        others = [_chip_of(x, y, r) for r in range(3)]
        _handshake([(qx, qy, c) for qx, qy in others] + [sibling])

        def small_block(a, q):
            cols = small[a].shape[1]
            return small_out[a].at[:, pl.ds(pl.multiple_of(q * cols, 128), cols)]

        local = [pltpu.make_async_copy(small_in[a], small_block(a, chip), local_sems.at[a]) for a in range(ns)]
        for cp in local:
            cp.start()

        sends = []
        for a, name in enumerate(names):
            mine = _shard_half_of(outs[a], name, chip, c)
            for r, (qx, qy) in enumerate(others):
                sends.append(_remote(mine, mine, send_sems, recv_sems, 6 * a + r, (qx, qy, c)))
        for a in range(ns):
            for r, (qx, qy) in enumerate(others):
                sends.append(_remote(small_in[a], small_block(a, chip), send_sems, recv_sems,
                                     6 * n + 3 * a + r, (qx, qy, c)))
        for cp in sends:
            cp.start()

        forwards = []
        for a, name in enumerate(names):
            for r, (qx, qy) in enumerate(others):
                landed = _shard_half_of(outs[a], name, 2 * qx + qy, c)
                _remote(landed, landed, send_sems, recv_sems, 6 * a + r, (qx, qy, c)).wait_recv()
                fwd = _remote(landed, landed, send_sems, recv_sems, 6 * a + 3 + r, sibling)
                fwd.start()
                forwards.append(fwd)
        for a in range(ns):
            for r, (qx, qy) in enumerate(others):
                landed = small_block(a, 2 * qx + qy)
                _remote(landed, landed, send_sems, recv_sems, 6 * n + 3 * a + r, (qx, qy, c)).wait_recv()
        for a, name in enumerate(names):
            for r, (qx, qy) in enumerate(others):
                landed = _shard_half_of(outs[a], name, 2 * qx + qy, 1 - c)
                _remote(landed, landed, send_sems, recv_sems, 6 * a + 3 + r, sibling).wait_recv()
        for cp in sends + forwards:
            cp.wait_send()
        for cp in local:
            cp.wait()

    launch()
    return {nm: ref[...] for nm, ref in zip(names, outs)}, [ref[...] for ref in small_out]


def _exchange_halves(grads, names, label, collective_id):
    n = len(names)
    sequencer = collective_id is not None

    def body(*refs):
        ins, outs = refs[:n], refs[n:2 * n]
        send_sems, recv_sems = refs[2 * n:]
        x, y, c = _place()
        if sequencer:
            _handshake([(x, y, 1 - c)])
        copies = []
        for a, name in enumerate(names):
            copies.append(_remote(_half_of(ins[a], BIG[name][2], 1 - c), outs[a], send_sems, recv_sems, a,
                                  (x, y, 1 - c)))
        for cp in copies:
            cp.start()
        for cp in copies:
            cp.wait()

    def half_shape(name):
        r, c_ = _full_shape(name)
        return (HALF, c_) if BIG[name][2] else (r, HALF)

    out_type = tuple(jax.ShapeDtypeStruct(half_shape(nm), F32) for nm in names)
    sems = (pltpu.SemaphoreType.DMA((n,)), pltpu.SemaphoreType.DMA((n,)))
    operands = [grads[nm] for nm in names]
    if sequencer:
        got = pl.kernel(
            body, mesh=plsc.ScalarSubcoreMesh(axis_name="seq", num_cores=1), name=label, out_type=out_type,
            scratch_types=sems, compiler_params=pltpu.CompilerParams(collective_id=collective_id),
        )(*operands)
    else:
        got = pl.pallas_call(
            body, in_specs=[ANY] * n, out_specs=[ANY] * n, out_shape=list(out_type), scratch_shapes=list(sems),
            name=label,
        )(*operands)
    return dict(zip(names, got))


def _chip_sum(g, got, name, core):
    by_col = BIG[name][2]
    rows, cols = got.shape
    if by_col:
        tr = 128
        g_spec = pl.BlockSpec((tr, cols), lambda i, s: (s[0] * (HALF // tr) + i, 0))
    else:
        tr = _row_tile(rows, 512)
        g_spec = pl.BlockSpec((tr, HALF), lambda i, s: (i, s[0]))
    blk = pl.BlockSpec((tr, cols), lambda i, s: (i, 0))

    def body(s_ref, g_ref, r_ref, f_ref, b_ref):
        t = g_ref[...] + r_ref[...]
        f_ref[...] = t
        b_ref[...] = t.astype(BF16)

    return pl.pallas_call(
        body,
        grid_spec=pltpu.PrefetchScalarGridSpec(num_scalar_prefetch=1, grid=(rows // tr,), in_specs=[g_spec, blk],
                                               out_specs=[blk, blk]),
        out_shape=[jax.ShapeDtypeStruct(got.shape, F32), jax.ShapeDtypeStruct(got.shape, BF16)],
        name="chip_sum_" + name, compiler_params=_params("parallel"),
    )(core, g, got)


def _piece_shape(name):
    rows, cols, by_col = BIG[name]
    return (HALF, cols) if by_col else (rows, HALF)


def _handshake(peers):
    barrier = pltpu.get_barrier_semaphore()
    for peer in peers:
        pl.semaphore_signal(barrier, inc=1, device_id=peer, device_id_type=MESH)
    pl.semaphore_wait(barrier, len(peers))


def _send_chip_sums(sums, names, label, collective_id):
    n = len(names)

    def body(*refs):
        ins, outs = refs[:n], refs[n:2 * n]
        send_sems, recv_sems = refs[2 * n:]
        x, y, c = _place()
        others = [_chip_of(x, y, r) for r in range(3)]
        _handshake([(qx, qy, c) for qx, qy in others])
        copies = []
        for a, name in enumerate(names):
            for r, (qx, qy) in enumerate(others):
                copies.append(_remote(_shard_of(ins[a], name, 2 * qx + qy), outs[a].at[r], send_sems, recv_sems,
                                      3 * a + r, (qx, qy, c)))
        for cp in copies:
            cp.start()
        for cp in copies:
            cp.wait()

    return pl.kernel(
        body, mesh=plsc.ScalarSubcoreMesh(axis_name="seq", num_cores=1), name=label,
        out_type=tuple(jax.ShapeDtypeStruct((3,) + _piece_shape(nm), BF16) for nm in names),
        scratch_types=(pltpu.SemaphoreType.DMA((3 * n,)), pltpu.SemaphoreType.DMA((3 * n,))),
        compiler_params=pltpu.CompilerParams(collective_id=collective_id),
    )(*[sums[nm] for nm in names])


def _total(own, got, name, chip_core):
    rows, cols, by_col = BIG[name]
    pr, pc = _piece_shape(name)
    tr = _row_tile(pr, 352)
    if by_col:
        own_spec = pl.BlockSpec((tr, pc), lambda i, s: (i, s[0]))
    else:
        own_spec = pl.BlockSpec((tr, pc), lambda i, s: (s[0] * (pr // tr) + i, 0))
    got_spec = lambda r: pl.BlockSpec((None, tr, pc), lambda i, s: (r, i, 0))
    if by_col:
        out_spec = pl.BlockSpec((tr, pc), lambda i, s: (s[1] * (pr // tr) + i, 0))
    else:
        out_spec = pl.BlockSpec((tr, pc), lambda i, s: (i, s[1]))

    def body(s_ref, o_ref, a_ref, b_ref, c_ref, t_ref):
        t_ref[...] = ((o_ref[...] + a_ref[...].astype(F32)) + b_ref[...].astype(F32)) + c_ref[...].astype(F32)

    return pl.pallas_call(
        body,
        grid_spec=pltpu.PrefetchScalarGridSpec(
            num_scalar_prefetch=1, grid=(pr // tr,), in_specs=[own_spec, got_spec(0), got_spec(1), got_spec(2)],
            out_specs=out_spec),
        out_shape=jax.ShapeDtypeStruct((rows, cols), F32),
        name="total_" + name, compiler_params=_params("parallel"),
    )(chip_core, own, got, got, got)


def _share_totals(totals):
    def body(*refs):
        outs = refs[N_BIG:2 * N_BIG]
        send_sems, recv_sems = refs[2 * N_BIG:]
        x, y, c = _place()
        copies = []
        for a, name in enumerate(BIG_NAMES):
            mine = _half_of(outs[a], BIG[name][2], c)
            copies.append(_remote(mine, mine, send_sems, recv_sems, a, (x, y, 1 - c)))
        for cp in copies:
            cp.start()
        for a, name in enumerate(BIG_NAMES):
            theirs = _half_of(outs[a], BIG[name][2], 1 - c)
            _remote(theirs, theirs, send_sems, recv_sems, a, (x, y, 1 - c)).wait_recv()
        for cp in copies:
            cp.wait_send()

    return pl.pallas_call(
        body, in_specs=[ANY] * N_BIG, out_specs=[ANY] * N_BIG,
        out_shape=[jax.ShapeDtypeStruct(BIG[n][:2], F32) for n in BIG_NAMES],
        input_output_aliases={a: a for a in range(N_BIG)},
        scratch_shapes=[pltpu.SemaphoreType.DMA((N_BIG,)), pltpu.SemaphoreType.DMA((N_BIG,))],
        name="share_totals",
    )(*[totals[n] for n in BIG_NAMES])


VEC_ROWS = 32
VEC_ROW = {"mix_norm_g": 0, "conv_b": 1, "b_rgate": 2, "b_igate": 3, "lru_lambda": 4, "rg_norm_g": 5,
           "hg_lower_bound": 6, "hg_norm_g": 8, "ffn_norm_g": 9, "final_norm_g": 10, "loss": 11,
           "conv_w": 12, "meta_tokens": 16}
N_DEV = 8


def _all_reduce_small(pieces, gates):
    names = list(pieces)
    hv, hg = VEC_ROWS // 2, gates.shape[0] // 2

    def body(*refs):
        ins = refs[:len(names)]
        (g_ref, vec_ref, gsum_ref, mine_v, sib_v, sib_g, chip_v, chip_g, got_v, got_g,
         send_sems, recv_sems) = refs[len(names):]
        x, y, c = _place()
        chip = 2 * x + y
        sibling = (x, y, 1 - c)
        mine_v[...] = jnp.zeros_like(mine_v)
        for name, ref in zip(names, ins):
            nr, w = ref.shape
            mine_v[VEC_ROW[name]:VEC_ROW[name] + nr, 0:w] = ref[...]

        swap = [_remote(mine_v, sib_v, send_sems, recv_sems, 0, sibling),
                _remote(g_ref, sib_g, send_sems, recv_sems, 1, sibling)]
        for cp in swap:
            cp.start()
        for cp in swap:
            cp.wait()
        chip_v[...] = mine_v[...] + sib_v[...]
        chip_g[...] = g_ref[...] + sib_g[...]

        rows_v = pl.ds(pl.multiple_of(c * hv, 8), hv)
        rows_g = pl.ds(pl.multiple_of(c * hg, 8), hg)
        got_v[chip] = chip_v[rows_v, :]
        got_g[chip] = chip_g[rows_g, :]
        sends = []
        for r in range(3):
            qx, qy = _chip_of(x, y, r)
            sends.append(_remote(chip_v.at[rows_v, :], got_v.at[chip], send_sems, recv_sems, 2 + r, (qx, qy, c)))
            sends.append(_remote(chip_g.at[rows_g, :], got_g.at[chip], send_sems, recv_sems, 5 + r, (qx, qy, c)))
        for cp in sends:
            cp.start()
        for cp in sends:
            cp.wait()
        vec_ref[rows_v, :] = ((got_v[0] + got_v[1]) + got_v[2]) + got_v[3]
        gsum_ref[rows_g, :] = ((got_g[0] + got_g[1]) + got_g[2]) + got_g[3]

        back = [_remote(vec_ref.at[rows_v, :], vec_ref.at[rows_v, :], send_sems, recv_sems, 8, sibling),
                _remote(gsum_ref.at[rows_g, :], gsum_ref.at[rows_g, :], send_sems, recv_sems, 9, sibling)]
        for cp in back:
            cp.start()
        theirs_v = vec_ref.at[pl.ds(pl.multiple_of((1 - c) * hv, 8), hv), :]
        theirs_g = gsum_ref.at[pl.ds(pl.multiple_of((1 - c) * hg, 8), hg), :]
        _remote(theirs_v, theirs_v, send_sems, recv_sems, 8, sibling).wait_recv()
        _remote(theirs_g, theirs_g, send_sems, recv_sems, 9, sibling).wait_recv()
        for cp in back:
            cp.wait_send()

    vmem = pl.BlockSpec(memory_space=pltpu.VMEM)
    n_sems = 10
    return pl.pallas_call(
        body, in_specs=[vmem] * (len(names) + 1), out_specs=[vmem, vmem],
        out_shape=[jax.ShapeDtypeStruct((VEC_ROWS, D_MODEL), F32), jax.ShapeDtypeStruct(gates.shape, F32)],
        scratch_shapes=[pltpu.VMEM((VEC_ROWS, D_MODEL), F32), pltpu.VMEM((VEC_ROWS, D_MODEL), F32),
                        pltpu.VMEM(gates.shape, F32), pltpu.VMEM((VEC_ROWS, D_MODEL), F32),
                        pltpu.VMEM(gates.shape, F32), pltpu.VMEM((N_CHIPS, hv, D_MODEL), F32),
                        pltpu.VMEM((N_CHIPS, hg) + gates.shape[1:], F32),
                        pltpu.SemaphoreType.DMA((n_sems,)), pltpu.SemaphoreType.DMA((n_sems,))],
        name="all_reduce_small",
    )(*[pieces[n] for n in names], gates)


def _adamw_math(w, g, m, v):
    m = ADAM_B1 * m + (1.0 - ADAM_B1) * g
    v = ADAM_B2 * v + (1.0 - ADAM_B2) * (g * g)
    m_hat = m / (1.0 - ADAM_B1 ** ADAM_STEP)
    v_hat = v / (1.0 - ADAM_B2 ** ADAM_STEP)
    delta = -ADAM_LR * (m_hat / (jnp.sqrt(v_hat) + ADAM_EPS) + ADAM_WD * w)
    return delta, m, v


def _adamw_big(w, g, m, v, name):
    rows, cols = w.shape
    tr = _row_tile(rows, 352)

    def body(w_ref, g_ref, m_ref, v_ref, d_ref, nm_ref, nv_ref):
        d_ref[...], nm_ref[...], nv_ref[...] = _adamw_math(w_ref[...], g_ref[...], m_ref[...], v_ref[...])

    blk = pl.BlockSpec((tr, cols), lambda i: (i, 0))
    return pl.pallas_call(
        body, grid=(rows // tr,), in_specs=[blk] * 4, out_specs=[blk] * 3,
        out_shape=[jax.ShapeDtypeStruct(w.shape, F32)] * 3,
        name="adamw_" + name, compiler_params=_params("parallel"),
    )(w, g, m, v)


SMALL = {"meta_tokens": (N_META, D_MODEL // N_CHIPS), "mix_norm_g": (1, D_MODEL), "conv_w": (CONV_W, D_RG // N_CHIPS),
         "conv_b": (1, D_RG), "w_rgate": (D_RG, RG_HEAD_DIM), "b_rgate": (1, D_RG), "w_igate": (D_RG, RG_HEAD_DIM),
         "b_igate": (1, D_RG), "lru_lambda": (1, D_RG), "rg_norm_g": (1, D_RG), "hg_lower_bound": (2, D_HG),
         "hg_norm_g": (1, HG_HEAD_DIM), "ffn_norm_g": (1, D_MODEL), "final_norm_g": (1, D_MODEL)}
SMALL_NAMES = tuple(SMALL)
SHARDED_SMALL = ("meta_tokens", "conv_w")


def _adamw_small(vec, gates, w, m, v):
    n = len(SMALL_NAMES)

    def body(*refs):
        vec_ref, gates_ref = refs[:2]
        w_refs, m_refs, v_refs = refs[2:2 + n], refs[2 + n:2 + 2 * n], refs[2 + 2 * n:2 + 3 * n]
        outs = refs[2 + 3 * n:]
        loss_ref = outs[0]
        x, y, _ = _place()
        chip = 2 * x + y
        loss_ref[...] = vec_ref[VEC_ROW["loss"]:VEC_ROW["loss"] + 1, 0:1]

        def update(k, g):
            g_ref, d_ref, nm_ref, nv_ref = outs[1 + 4 * k:5 + 4 * k]
            g_ref[...] = g
            d_ref[...], nm_ref[...], nv_ref[...] = _adamw_math(w_refs[k][...], g, m_refs[k][...], v_refs[k][...])

        for k, name in enumerate(SMALL_NAMES):
            nr, w_ = SMALL[name]
            if name == "w_rgate":
                update(k, gates_ref[0:D_RG, :])
            elif name == "w_igate":
                update(k, gates_ref[D_RG:2 * D_RG, :])
            elif name in SHARDED_SMALL:
                r0 = VEC_ROW[name]
                for q in range(N_CHIPS):
                    @pl.when(chip == q)
                    def _(k=k, r0=r0, nr=nr, w_=w_, q=q):
                        update(k, vec_ref[r0:r0 + nr, q * w_:(q + 1) * w_])
            else:
                r0 = VEC_ROW[name]
                update(k, vec_ref[r0:r0 + nr, 0:w_])

    vmem = pl.BlockSpec(memory_space=pltpu.VMEM)
    out_shape = [jax.ShapeDtypeStruct((1, 1), F32)]
    for name in SMALL_NAMES:
        out_shape += [jax.ShapeDtypeStruct(SMALL[name], F32)] * 4
    outs = pl.pallas_call(
        body, in_specs=[vmem] * (2 + 3 * n), out_specs=[vmem] * len(out_shape), out_shape=out_shape,
        name="adamw_small",
    )(vec, gates, *[w[k] for k in SMALL_NAMES], *[m[k] for k in SMALL_NAMES], *[v[k] for k in SMALL_NAMES])
    loss = outs[0]
    res = {name: tuple(outs[1 + 4 * k:5 + 4 * k]) for k, name in enumerate(SMALL_NAMES)}
    return loss, res


WEIGHT_NAMES = ("meta_tokens", "mix_norm_g", "w_in", "conv_w", "conv_b", "w_rgate", "b_rgate", "w_igate", "b_igate",
                "lru_lambda", "rg_norm_g", "hg_lower_bound", "hg_norm_g", "w_out", "ffn_norm_g", "w_gate_up", "w_down",
                "final_norm_g")


def _to_2d(name, a):
    if name in BIG:
        return a.reshape(BIG[name][:2])
    return a.reshape(SMALL[name])


def kernel(x, meta_tokens, mix_norm_g, w_in, conv_w, conv_b, w_rgate, b_rgate, w_igate, b_igate, lru_lambda, rg_norm_g, hg_lower_bound, hg_norm_g, w_out, ffn_norm_g, w_gate_up, w_down, final_norm_g, loss_target, m_meta_tokens, m_mix_norm_g, m_w_in, m_conv_w, m_conv_b, m_w_rgate, m_b_rgate, m_w_igate, m_b_igate, m_lru_lambda, m_rg_norm_g, m_hg_lower_bound, m_hg_norm_g, m_w_out, m_ffn_norm_g, m_w_gate_up, m_w_down, m_final_norm_g, v_meta_tokens, v_mix_norm_g, v_w_in, v_conv_w, v_conv_b, v_w_rgate, v_b_rgate, v_w_igate, v_b_igate, v_lru_lambda, v_rg_norm_g, v_hg_lower_bound, v_hg_norm_g, v_w_out, v_ffn_norm_g, v_w_gate_up, v_w_down, v_final_norm_g):
    w_raw = dict(zip(WEIGHT_NAMES, (meta_tokens, mix_norm_g, w_in, conv_w, conv_b, w_rgate, b_rgate, w_igate, b_igate,
                                    lru_lambda, rg_norm_g, hg_lower_bound, hg_norm_g, w_out, ffn_norm_g, w_gate_up,
                                    w_down, final_norm_g)))
    m_raw = dict(zip(WEIGHT_NAMES, (m_meta_tokens, m_mix_norm_g, m_w_in, m_conv_w, m_conv_b, m_w_rgate, m_b_rgate,
                                    m_w_igate, m_b_igate, m_lru_lambda, m_rg_norm_g, m_hg_lower_bound, m_hg_norm_g,
                                    m_w_out, m_ffn_norm_g, m_w_gate_up, m_w_down, m_final_norm_g)))
    v_raw = dict(zip(WEIGHT_NAMES, (v_meta_tokens, v_mix_norm_g, v_w_in, v_conv_w, v_conv_b, v_w_rgate, v_b_rgate,
                                    v_w_igate, v_b_igate, v_lru_lambda, v_rg_norm_g, v_hg_lower_bound, v_hg_norm_g,
                                    v_w_out, v_ffn_norm_g, v_w_gate_up, v_w_down, v_final_norm_g)))
    w = {k: _to_2d(k, a) for k, a in w_raw.items()}
    m = {k: _to_2d(k, a) for k, a in m_raw.items()}
    v = {k: _to_2d(k, a) for k, a in v_raw.items()}

    x_i, y_i, c_i = _place()
    core = jnp.reshape(c_i, (1,)).astype(jnp.int32)
    chip = jnp.reshape(2 * x_i + y_i, (1,)).astype(jnp.int32)
    chip_core = jnp.concatenate([chip, core])

    placed = {k: _cast_into_full(w[k], k, chip) for k in BIG_NAMES}
    first, (meta_full, cw_full) = _gather_weights(placed, [w["meta_tokens"], w["conv_w"]], ("w_in",), "gather_first", 1)
    rest, _ = _gather_weights(placed, [], ("w_out", "w_gate_up", "w_down"), "gather_rest", 2)
    full = {**first, **rest}

    seq = x.shape[1]
    h0 = jnp.concatenate([jnp.zeros((PAD, D_MODEL), F32), meta_full, x[0]], axis=0)
    target = jnp.concatenate([jnp.zeros((PAD + N_META, D_MODEL), F32), loss_target[0]], axis=0)
    small = {k: w[k] for k in SMALL_NAMES if k not in SHARDED_SMALL}
    small["conv_w"] = cw_full

    def reduce_to_chips(grads, names, tag, collective_ids):
        got = _exchange_halves(grads, names, "exchange_halves_" + tag, collective_ids[0])

        def chip_sums():
            return {n: _chip_sum(grads[n], got[n], n, core) for n in names}

        def send(sums):
            arrived = _send_chip_sums({n: sums[n][1] for n in names}, names, "send_chip_sums_" + tag,
                                      collective_ids[1])
            return {n: (sums[n][0], a) for n, a in zip(names, arrived)}

        return chip_sums, send

    ffn_names, mixer_names = ("w_gate_up", "w_down"), ("w_in", "w_out")
    loss, dh0, grads, parts = _local_step(
        h0, target, full["w_in"], full["w_out"], full["w_gate_up"], full["w_down"], small,
        on_ffn_grads=lambda g: reduce_to_chips(g, ffn_names, "ffn", (3, 4)))
    chip_sums, send = reduce_to_chips(grads, mixer_names, "mixer", (None, 5))
    parts.update(send(chip_sums()))
    totals = {n: _total(parts[n][0], parts[n][1], n, chip_core) for n in BIG_NAMES}
    g_big = dict(zip(BIG_NAMES, _share_totals(totals)))

    pieces = {k: grads[k] for k in VEC_ROW if k not in ("loss", "meta_tokens")}
    pieces["loss"] = loss
    pieces["meta_tokens"] = dh0[PAD:PAD + N_META]
    vec, gates = _all_reduce_small(pieces, grads["w_gates"])
    loss_sum, res = _adamw_small(vec, gates, w, m, v)
    for n in BIG_NAMES:
        res[n] = (g_big[n],) + tuple(_adamw_big(w[n], g_big[n], m[n], v[n], n))

    grad_x = dh0[PAD + N_META:].reshape(1, seq, D_MODEL)
    out = [loss_sum.reshape(()), grad_x]
    for j in range(4):
        out += [res[n][j].reshape(w_raw[n].shape) for n in WEIGHT_NAMES]
    return tuple(out)
```

```python
import functools
import math

import jax
import jax.numpy as jnp
from jax import lax
from jax.experimental import pallas as pl
from jax.experimental.pallas import tpu as pltpu
from jax.experimental.pallas import tpu_sc as plsc

F32 = jnp.float32
BF16 = jnp.bfloat16
HIGHEST = lax.Precision.HIGHEST
MESH = pl.DeviceIdType.MESH

D_MODEL = 1024
D_RG = 512
RG_HEAD_DIM = 64
D_HG = 512
HG_HEAD_DIM = 128
HG_HEADS = 4
CHUNK = 64
SUB = 16
N_SUB = CHUNK // SUB
N_META = 16
PAD = CHUNK - N_META
D_IN = 3072
D_FF = 2816
CONV_W = 4
LRU_C = 8.0
EPS = 1e-6
EXP_CLAMP = 80.0
GELU_C = math.sqrt(2.0 / math.pi)
GELU_A = 0.044715
N_CHIPS = 4

ADAM_LR = 0.001
ADAM_B1 = 0.9
ADAM_B2 = 0.999
ADAM_EPS = 1e-08
ADAM_WD = 0.01
ADAM_STEP = 10

VMEM_LIMIT = 56 * 1024 * 1024


def _params(*sem):
    return pltpu.CompilerParams(dimension_semantics=sem, vmem_limit_bytes=VMEM_LIMIT)


def _row_tile(rows, target):
    best = None
    for t in range(16, min(rows, target) + 1, 16):
        if rows % t == 0:
            best = t
    assert best is not None, rows
    return best


def _sigmoid(x):
    return 1.0 / (1.0 + jnp.exp(-x))


def _sigmoid_fast(x):
    return pl.reciprocal(1.0 + jnp.exp(-x), approx=True)


def _dot(a, b):
    return jnp.dot(a, b, preferred_element_type=F32)


def _dot_nt(a, b):
    return lax.dot_general(a, b, (((1,), (1,)), ((), ())), preferred_element_type=F32)


def _dot_tn(a, b):
    return lax.dot_general(a, b, (((0,), (0,)), ((), ())), preferred_element_type=F32)


def _rms(x):
    return lax.rsqrt(jnp.mean(x * x, axis=-1, keepdims=True) + EPS)


def _rms_bwd(dn, n, r):
    return r * (dn - n * jnp.mean(dn * n, axis=-1, keepdims=True))


def _gelu_parts(x):
    t = jnp.tanh(GELU_C * (x + GELU_A * x * x * x))
    g = 0.5 * x * (1.0 + t)
    dg = 0.5 * (1.0 + t) + 0.5 * x * (1.0 - t * t) * GELU_C * (1.0 + 3.0 * GELU_A * x * x)
    return g, dg


def _softplus_neg(lam):
    e = jnp.exp(-jnp.abs(lam))
    w = 1.0 + e
    log1p = jnp.where(w == 1.0, e, jnp.log(w) * e / (w - 1.0))
    return jnp.maximum(-lam, 0.0) + log1p


def _head_mask():
    r = lax.broadcasted_iota(jnp.int32, (D_RG, D_RG), 0) // RG_HEAD_DIM
    c = lax.broadcasted_iota(jnp.int32, (D_RG, D_RG), 1) // RG_HEAD_DIM
    return r == c


def _head_fold():
    r = lax.broadcasted_iota(jnp.int32, (D_RG, RG_HEAD_DIM), 0) % RG_HEAD_DIM
    c = lax.broadcasted_iota(jnp.int32, (D_RG, RG_HEAD_DIM), 1)
    return (r == c).astype(F32)


def _gate_weights(w_r, w_i):
    def body(wr_ref, wi_ref, o_ref):
        fold = _head_fold()
        mask = _head_mask()
        for k, ref in enumerate((wr_ref, wi_ref)):
            full = lax.dot_general(ref[...], fold, (((1,), (1,)), ((), ())),
                                   precision=HIGHEST, preferred_element_type=F32)
            o_ref[:, k * D_RG:(k + 1) * D_RG] = jnp.where(mask, full, 0.0).astype(BF16)

    return pl.pallas_call(
        body, out_shape=jax.ShapeDtypeStruct((D_RG, 2 * D_RG), BF16), name="gate_weights",
    )(w_r, w_i)


def _in_proj(h0, g1, w_in):
    T = h0.shape[0]
    tm = _row_tile(T, 416)

    def body(h_ref, g_ref, w_ref, p_ref, u_ref):
        h = h_ref[...]
        u = (h * _rms(h) * g_ref[...]).astype(BF16)
        u_ref[...] = u
        p_ref[...] = _dot(u, w_ref[...])

    return pl.pallas_call(
        body, grid=(T // tm,),
        in_specs=[pl.BlockSpec((tm, D_MODEL), lambda i: (i, 0)),
                  pl.BlockSpec((1, D_MODEL), lambda i: (0, 0)),
                  pl.BlockSpec((D_MODEL, D_IN), lambda i: (0, 0))],
        out_specs=[pl.BlockSpec((tm, D_IN), lambda i: (i, 0)),
                   pl.BlockSpec((tm, D_MODEL), lambda i: (i, 0))],
        out_shape=[jax.ShapeDtypeStruct((T, D_IN), F32), jax.ShapeDtypeStruct((T, D_MODEL), BF16)],
        name="in_proj", compiler_params=_params("parallel"),
    )(h0, g1, w_in)


def _scan_block_fwd(A, B, rowi):
    for d in (1, 2, 4):
        a_sh = pltpu.roll(A, d, axis=0)
        b_sh = pltpu.roll(B, d, axis=0)
        m = rowi >= d
        B = jnp.where(m, A * b_sh + B, B)
        A = jnp.where(m, A * a_sh, A)
    return A, B


def _scan_block_bwd(A, B, rowi):
    for d in (1, 2, 4):
        a_sh = pltpu.roll(A, 8 - d, axis=0)
        b_sh = pltpu.roll(B, 8 - d, axis=0)
        m = rowi < 8 - d
        B = jnp.where(m, A * b_sh + B, B)
        A = jnp.where(m, A * a_sh, A)
    return A, B


def _rg_gates(xc, w_ref, bg_ref, lam):
    pre = _dot(xc.astype(BF16), w_ref[...]) + bg_ref[...]
    r = _sigmoid(pre[:, :D_RG])
    ig = _sigmoid(pre[:, D_RG:])
    sp = _softplus_neg(lam)
    la = -LRU_C * sp * r
    a = jnp.exp(la)
    th = jnp.tanh(la)
    m = jnp.sqrt(-2.0 * th / (1.0 - th))
    return r, ig, sp, a, m


def _conv(ext, cw_ref, cb_ref, tm):
    xc = cb_ref[...] + cw_ref[0:1, :] * ext[8 - 3:8 - 3 + tm, :]
    for j in range(1, CONV_W):
        xc = xc + cw_ref[j:j + 1, :] * ext[8 - 3 + j:8 - 3 + j + tm, :]
    return xc


def _rg_fwd(p, cw, cb, wg, bg, lam, rg_g):
    T = p.shape[0]
    tm = _row_tile(T, 416)

    def body(xg_ref, cw_ref, cb_ref, w_ref, bg_ref, lam_ref, g_ref, y_ref, h_ref, ext, a_s, b_s, carry):
        i = pl.program_id(0)

        @pl.when(i == 0)
        def _():
            ext[0:8, :] = jnp.zeros((8, D_RG), F32)
            carry[...] = jnp.zeros((1, D_RG), F32)

        ext[8:8 + tm, :] = xg_ref[:, :D_RG]
        xc = _conv(ext, cw_ref, cb_ref, tm)
        r, ig, sp, a, m = _rg_gates(xc, w_ref, bg_ref, lam_ref[...])
        row = i * tm + lax.broadcasted_iota(jnp.int32, (tm, 1), 0)
        a_s[...] = a
        b_s[...] = jnp.where(row >= PAD, m * ig * xc, 0.0)
        rowi = lax.broadcasted_iota(jnp.int32, (8, D_RG), 0)

        def blk(j, c):
            o = pl.multiple_of(j * 8, 8)
            A, B = _scan_block_fwd(a_s[pl.ds(o, 8), :], b_s[pl.ds(o, 8), :], rowi)
            h = B + A * c
            h_ref[pl.ds(o, 8), :] = h
            return h[7:8, :]

        carry[...] = lax.fori_loop(0, tm // 8, blk, carry[...])
        ext[0:8, :] = ext[tm:tm + 8, :]
        g, _ = _gelu_parts(xg_ref[:, D_RG:])
        yy = g * h_ref[...]
        y_ref[...] = (yy * _rms(yy) * g_ref[...]).astype(BF16)

    vec = lambda n: pl.BlockSpec((1, n), lambda i: (0, 0))
    return pl.pallas_call(
        body, grid=(T // tm,),
        in_specs=[pl.BlockSpec((tm, 2 * D_RG), lambda i: (i, 0)),
                  pl.BlockSpec((CONV_W, D_RG), lambda i: (0, 0)), vec(D_RG),
                  pl.BlockSpec((D_RG, 2 * D_RG), lambda i: (0, 0)), vec(2 * D_RG), vec(D_RG), vec(D_RG)],
        out_specs=[pl.BlockSpec((tm, D_RG), lambda i: (i, 0)), pl.BlockSpec((tm, D_RG), lambda i: (i, 0))],
        out_shape=[jax.ShapeDtypeStruct((T, D_RG), BF16), jax.ShapeDtypeStruct((T, D_RG), F32)],
        scratch_shapes=[pltpu.VMEM((tm + 8, D_RG), F32), pltpu.VMEM((tm, D_RG), F32),
                        pltpu.VMEM((tm, D_RG), F32), pltpu.VMEM((1, D_RG), F32)],
        name="rg_fwd", compiler_params=_params("arbitrary"),
    )(p, cw, cb, wg, bg, lam, rg_g)


def _tri(lower):
    r = lax.broadcasted_iota(jnp.int32, (CHUNK, CHUNK), 0)
    c = lax.broadcasted_iota(jnp.int32, (CHUNK, CHUNK), 1)
    return ((c <= r) if lower else (c >= r)).astype(F32)


def _hg_gates(hq, hf, lbraw_ref, valid):
    lb = _sigmoid(lbraw_ref[0:1, :] - lbraw_ref[1:2, :])
    sq = _sigmoid(hq)
    q = hq * sq
    sf = _sigmoid(hf)
    f = lb + (1.0 - lb) * sf
    lf = jnp.where(valid, jnp.log(f), 0.0)
    b = jnp.dot(_tri(True), lf, precision=HIGHEST, preferred_element_type=F32)
    return lb, sq, q, sf, f, b


def _hg_head(qh, kh, bh):
    blk = lax.broadcasted_iota(jnp.int32, (CHUNK, 1), 0) // SUB
    b_last = bh[CHUNK - 1:CHUNK, :]
    refs = [bh[SUB * s:SUB * s + 1, :] for s in range(N_SUB)]
    r_sel = refs[N_SUB - 1]
    for s in range(N_SUB - 2, -1, -1):
        r_sel = jnp.where(blk == s, refs[s], r_sel)
    eb = jnp.exp(bh)
    eq = jnp.exp(bh - r_sel)
    ekh = jnp.exp(b_last - bh)
    ek = [jnp.exp(jnp.minimum(refs[s] - bh, EXP_CLAMP)) for s in range(N_SUB)]
    qe = qh * eq
    q_hat = jnp.concatenate([jnp.where(blk == s, qe, 0.0) for s in range(N_SUB)], axis=1)
    k_til = jnp.concatenate([kh * ek[s] for s in range(N_SUB)], axis=1)
    return blk, b_last, eb, eq, ekh, ek, q_hat, k_til


def _causal():
    r = lax.broadcasted_iota(jnp.int32, (CHUNK, CHUNK), 0)
    c = lax.broadcasted_iota(jnp.int32, (CHUNK, CHUNK), 1)
    return r >= c


def _chunks_per_step(n_chunks):
    for c in (5, 4, 3, 2):
        if n_chunks % c == 0:
            return c
    return 1


def _hg_fwd(p, lbraw, hg_g):
    T = p.shape[0]
    n_chunks = T // CHUNK
    cps = _chunks_per_step(n_chunks)
    rows = cps * CHUNK

    def body(hq_ref, hf_ref, hi_ref, hg_ref, lb_ref, g_ref, y_ref, o_ref, st_all_ref, st):
        i = pl.program_id(0)

        @pl.when(i == 0)
        def _():
            st[...] = jnp.zeros_like(st)

        def chunk(j, carry):
            rs = pl.ds(pl.multiple_of(j * CHUNK, CHUNK), CHUNK)
            chunk_body(i * cps + j, hq_ref.at[rs, :], hf_ref.at[rs, :], hi_ref.at[rs, :], hg_ref.at[rs, :], lb_ref,
                       g_ref, y_ref.at[rs, :], o_ref.at[rs, :], st_all_ref.at[pl.ds(j, 1)], st)
            return carry

        lax.fori_loop(0, cps, chunk, 0)

    def chunk_body(n, hq_ref, hf_ref, hi_ref, hg_ref, lb_ref, g_ref, y_ref, o_ref, st_all_ref, st):
        valid = (n * CHUNK + lax.broadcasted_iota(jnp.int32, (CHUNK, 1), 0)) >= PAD
        hq, hf, v, hg = hq_ref[...], hf_ref[...], hi_ref[...], hg_ref[...]
        lb, sq, q, sf, f, b = _hg_gates(hq, hf, lb_ref, valid)
        k = 1.0 - f
        st_all_ref[0] = st[...]
        causal = _causal()
        for h in range(HG_HEADS):
            sl = slice(h * HG_HEAD_DIM, (h + 1) * HG_HEAD_DIM)
            qh, kh, vh, bh = q[:, sl], k[:, sl], v[:, sl], b[:, sl]
            st_h = st[sl, :]
            _, b_last, eb, _, ekh, _, q_hat, k_til = _hg_head(qh, kh, bh)
            vb = vh.astype(BF16)
            inter = _dot_nt((qh * eb).astype(BF16), st_h.astype(BF16))
            att = jnp.where(causal, _dot_nt(q_hat.astype(BF16), k_til.astype(BF16)), 0.0)
            o = inter + _dot(att.astype(BF16), vb)
            st[sl, :] = st_h * jnp.exp(b_last) + _dot_tn(vb, (kh * ekh).astype(BF16))
            o_ref[:, sl] = o
            hgh = hg[:, sl]
            y_ref[:, sl] = (o * _rms(o) * g_ref[...] * (hgh * _sigmoid(hgh))).astype(BF16)

    col = lambda j: pl.BlockSpec((rows, D_HG), lambda n: (n, j))
    return pl.pallas_call(
        body, grid=(n_chunks // cps,),
        in_specs=[col(2), col(3), col(4), col(5),
                  pl.BlockSpec((2, D_HG), lambda n: (0, 0)), pl.BlockSpec((1, HG_HEAD_DIM), lambda n: (0, 0))],
        out_specs=[pl.BlockSpec((rows, D_HG), lambda n: (n, 0)), pl.BlockSpec((rows, D_HG), lambda n: (n, 0)),
                   pl.BlockSpec((cps, D_HG, HG_HEAD_DIM), lambda n: (n, 0, 0))],
        out_shape=[jax.ShapeDtypeStruct((T, D_HG), BF16), jax.ShapeDtypeStruct((T, D_HG), F32),
                   jax.ShapeDtypeStruct((n_chunks, D_HG, HG_HEAD_DIM), F32)],
        scratch_shapes=[pltpu.VMEM((D_HG, HG_HEAD_DIM), F32)],
        name="hg_fwd", compiler_params=_params("arbitrary"),
    )(p, p, p, p, lbraw, hg_g)


def _out_proj(h0, y_rg, y_hg, w_out, g2):
    T = h0.shape[0]
    tm = _row_tile(T, 832)

    def body(h_ref, yr_ref, yh_ref, w_ref, g_ref, h1_ref, v_ref, y_ref):
        y_ref[:, :D_RG] = yr_ref[...]
        y_ref[:, D_RG:] = yh_ref[...]
        h1 = h_ref[...] + _dot(y_ref[...], w_ref[...])
        h1_ref[...] = h1
        v_ref[...] = (h1 * _rms(h1) * g_ref[...]).astype(BF16)

    row = lambda n: pl.BlockSpec((tm, n), lambda i: (i, 0))
    return pl.pallas_call(
        body, grid=(T // tm,),
        in_specs=[row(D_MODEL), row(D_RG), row(D_HG), pl.BlockSpec((D_MODEL, D_MODEL), lambda i: (0, 0)),
                  pl.BlockSpec((1, D_MODEL), lambda i: (0, 0))],
        out_specs=[row(D_MODEL), row(D_MODEL), row(D_MODEL)],
        out_shape=[jax.ShapeDtypeStruct((T, D_MODEL), F32), jax.ShapeDtypeStruct((T, D_MODEL), BF16),
                   jax.ShapeDtypeStruct((T, D_MODEL), BF16)],
        name="out_proj", compiler_params=_params("parallel"),
    )(h0, y_rg, y_hg, w_out, g2)


def _gate_up(v, w_gu):
    T = v.shape[0]
    tm = _row_tile(T, 416)

    def body(v_ref, w_ref, gu_ref, act_ref):
        gu = _dot(v_ref[...], w_ref[...])
        gu_ref[...] = gu.astype(BF16)
        g = gu[:, :D_FF]
        act_ref[...] = (g * _sigmoid_fast(g) * gu[:, D_FF:]).astype(BF16)

    row = lambda n: pl.BlockSpec((tm, n), lambda i: (i, 0))
    return pl.pallas_call(
        body, grid=(T // tm,),
        in_specs=[row(D_MODEL), pl.BlockSpec((D_MODEL, 2 * D_FF), lambda i: (0, 0))],
        out_specs=[row(2 * D_FF), row(D_FF)],
        out_shape=[jax.ShapeDtypeStruct((T, 2 * D_FF), BF16), jax.ShapeDtypeStruct((T, D_FF), BF16)],
        name="gate_up", compiler_params=_params("parallel"),
    )(v, w_gu)


def _down_loss(h1, act, w_down, gf, target):
    T = h1.shape[0]
    tm = _row_tile(T, 832)

    def body(h_ref, a_ref, w_ref, g_ref, t_ref, dh2_ref, dh2b_ref, loss_ref, gg_ref):
        i = pl.program_id(0)

        @pl.when(i == 0)
        def _():
            loss_ref[...] = jnp.zeros_like(loss_ref)
            gg_ref[...] = jnp.zeros_like(gg_ref)

        h2 = h_ref[...] + _dot(a_ref[...], w_ref[...])
        r = _rms(h2)
        n = h2 * r
        gf_ = g_ref[...]
        row = i * tm + lax.broadcasted_iota(jnp.int32, (tm, 1), 0)
        err = jnp.where(row >= PAD + N_META, n * gf_ - t_ref[...], 0.0)
        loss_ref[...] += 0.5 * jnp.sum(jnp.mean(err * err, axis=-1, keepdims=True), axis=0, keepdims=True)
        dy = err * (1.0 / D_MODEL)
        gg_ref[...] += jnp.sum(dy * n, axis=0, keepdims=True)
        dh2 = _rms_bwd(dy * gf_, n, r)
        dh2_ref[...] = dh2
        dh2b_ref[...] = dh2.astype(BF16)

    row_spec = lambda n: pl.BlockSpec((tm, n), lambda i: (i, 0))
    return pl.pallas_call(
        body, grid=(T // tm,),
        in_specs=[row_spec(D_MODEL), row_spec(D_FF), pl.BlockSpec((D_FF, D_MODEL), lambda i: (0, 0)),
                  pl.BlockSpec((1, D_MODEL), lambda i: (0, 0)), row_spec(D_MODEL)],
        out_specs=[row_spec(D_MODEL), row_spec(D_MODEL), pl.BlockSpec((1, 1), lambda i: (0, 0)),
                   pl.BlockSpec((1, D_MODEL), lambda i: (0, 0))],
        out_shape=[jax.ShapeDtypeStruct((T, D_MODEL), F32), jax.ShapeDtypeStruct((T, D_MODEL), BF16),
                   jax.ShapeDtypeStruct((1, 1), F32), jax.ShapeDtypeStruct((1, D_MODEL), F32)],
        name="down_loss", compiler_params=_params("arbitrary"),
    )(h1, act, w_down, gf, target)


def _ffn_bwd_act(dh2b, gu, w_down):
    T = dh2b.shape[0]
    tm = _row_tile(T, 416)

    def body(d_ref, gu_ref, w_ref, dgu_ref):
        dact = _dot_nt(d_ref[...], w_ref[...])
        g = gu_ref[:, :D_FF].astype(F32)
        u = gu_ref[:, D_FF:].astype(F32)
        s = _sigmoid_fast(g)
        dgu_ref[:, :D_FF] = (dact * u * s * (1.0 + g * (1.0 - s))).astype(BF16)
        dgu_ref[:, D_FF:] = (dact * g * s).astype(BF16)

    row = lambda n: pl.BlockSpec((tm, n), lambda i: (i, 0))
    return pl.pallas_call(
        body, grid=(T // tm,),
        in_specs=[row(D_MODEL), row(2 * D_FF), pl.BlockSpec((D_FF, D_MODEL), lambda i: (0, 0))],
        out_specs=row(2 * D_FF),
        out_shape=jax.ShapeDtypeStruct((T, 2 * D_FF), BF16),
        name="ffn_bwd_act", compiler_params=_params("parallel"),
    )(dh2b, gu, w_down)


def _ffn_bwd_in(dgu, w_gu, h1, g2, dh2, w_out):
    T = h1.shape[0]
    tm = _row_tile(T, 416)

    def body(dgu_ref, wgu_ref, h_ref, g_ref, d2_ref, wo_ref, dh1_ref, dh1b_ref, dy_ref, gg_ref):
        i = pl.program_id(0)

        @pl.when(i == 0)
        def _():
            gg_ref[...] = jnp.zeros_like(gg_ref)

        dv = _dot_nt(dgu_ref[...], wgu_ref[...])
        h1 = h_ref[...]
        r = _rms(h1)
        n = h1 * r
        gg_ref[...] += jnp.sum(dv * n, axis=0, keepdims=True)
        dh1 = d2_ref[...] + _rms_bwd(dv * g_ref[...], n, r)
        dh1_ref[...] = dh1
        db = dh1.astype(BF16)
        dh1b_ref[...] = db
        dy_ref[...] = _dot_nt(db, wo_ref[...])

    row = lambda n: pl.BlockSpec((tm, n), lambda i: (i, 0))
    return pl.pallas_call(
        body, grid=(T // tm,),
        in_specs=[row(2 * D_FF), pl.BlockSpec((D_MODEL, 2 * D_FF), lambda i: (0, 0)),
                  row(D_MODEL), pl.BlockSpec((1, D_MODEL), lambda i: (0, 0)), row(D_MODEL),
                  pl.BlockSpec((D_MODEL, D_MODEL), lambda i: (0, 0))],
        out_specs=[row(D_MODEL), row(D_MODEL), row(D_MODEL), pl.BlockSpec((1, D_MODEL), lambda i: (0, 0))],
        out_shape=[jax.ShapeDtypeStruct((T, D_MODEL), F32), jax.ShapeDtypeStruct((T, D_MODEL), BF16),
                   jax.ShapeDtypeStruct((T, D_MODEL), F32), jax.ShapeDtypeStruct((1, D_MODEL), F32)],
        name="ffn_bwd_in", compiler_params=_params("arbitrary"),
    )(dgu, w_gu, h1, g2, dh2, w_out)


def _rg_bwd(p, hs, dy, dp, cw, cb, wg, bg, lam, rg_g):
    T = p.shape[0]
    tm = _row_tile(T, 208)
    nt = T // tm
    hb = tm // 8

    def body(xg_ref, xh_ref, h_ref, hh_ref, dy_ref, dp_in_ref, cw_ref, cb_ref, w_ref, bg_ref, lam_ref, g_ref,
             dp_ref, gcw_ref, gcb_ref, gw_ref, gbg_ref, glam_ref, gg_ref,
             ext, dext, a_s, b_s, d_s, gacc, carry_d, carry_a):
        i = pl.program_id(0)
        t_idx = nt - 1 - i

        @pl.when(i == 0)
        def _():
            dext[tm:tm + 8, :] = jnp.zeros((8, D_RG), F32)
            carry_d[...] = jnp.zeros_like(carry_d)
            carry_a[...] = jnp.zeros_like(carry_a)
            gacc[...] = jnp.zeros_like(gacc)
            for ref in (gcw_ref, gcb_ref, gbg_ref, glam_ref, gg_ref, gw_ref):
                ref[...] = jnp.zeros_like(ref)

        first = t_idx == 0
        ext[0:8, :] = jnp.where(first, 0.0, xh_ref[:, :D_RG])
        ext[8:8 + tm, :] = xg_ref[:, :D_RG]
        xc = _conv(ext, cw_ref, cb_ref, tm)
        lam_ = lam_ref[...]
        r, ig, sp, a, m = _rg_gates(xc, w_ref, bg_ref, lam_)
        row = t_idx * tm + lax.broadcasted_iota(jnp.int32, (tm, 1), 0)
        valid = row >= PAD

        gr = xg_ref[:, D_RG:]
        g, dgelu = _gelu_parts(gr)
        h = h_ref[...]
        yy = g * h
        rr = _rms(yy)
        nn = yy * rr
        dy_ = dy_ref[...]
        gg_ref[...] += jnp.sum(dy_ * nn, axis=0, keepdims=True)
        dyy = _rms_bwd(dy_ * g_ref[...], nn, rr)
        dp_ref[:, D_RG:] = (dyy * h * dgelu).astype(BF16)

        a_s[...] = a
        b_s[...] = dyy * g
        rowi = lax.broadcasted_iota(jnp.int32, (8, D_RG), 0)

        def blk(jj, c):
            cd, ca = c
            o = pl.multiple_of((hb - 1 - jj) * 8, 8)
            a_blk = a_s[pl.ds(o, 8), :]
            a_next = jnp.where(rowi == 7, ca, pltpu.roll(a_blk, 7, axis=0))
            A, B = _scan_block_bwd(a_next, b_s[pl.ds(o, 8), :], rowi)
            d = B + A * cd
            d_s[pl.ds(o, 8), :] = d
            return d[0:1, :], a_blk[0:1, :]

        cd, ca = lax.fori_loop(0, hb, blk, (carry_d[...], carry_a[...]))
        carry_d[...] = cd
        carry_a[...] = ca
        delta = d_s[...]

        h_last_prev = jnp.where(first, 0.0, hh_ref[7:8, :])
        row0 = lax.broadcasted_iota(jnp.int32, (tm, 1), 0) == 0
        h_prev = jnp.where(row0, h_last_prev, pltpu.roll(h, 1, axis=0))
        dbx = jnp.where(valid, delta, 0.0)
        da = delta * h_prev
        di = dbx * m * xc
        dm = dbx * ig * xc
        dla = a * (da - dm * a / m)
        dla = jnp.where(valid, dla, 0.0)
        glam_ref[...] += jnp.sum(dla * r, axis=0, keepdims=True) * (LRU_C * _sigmoid(-lam_))
        dr = (-LRU_C) * sp * dla
        dpre = jnp.concatenate([dr * r * (1.0 - r), di * ig * (1.0 - ig)], axis=1)
        gbg_ref[...] += jnp.sum(dpre, axis=0, keepdims=True)
        dpre_b = dpre.astype(BF16)
        gacc[...] += _dot_tn(xc.astype(BF16), dpre_b)
        dxc = dbx * m * ig + _dot_nt(dpre_b, w_ref[...])
        gcb_ref[...] += jnp.sum(dxc, axis=0, keepdims=True)
        for j in range(CONV_W):
            gcw_ref[j:j + 1, :] += jnp.sum(dxc * ext[8 - 3 + j:8 - 3 + j + tm, :], axis=0, keepdims=True)
        dext[0:tm, :] = dxc
        dxr = cw_ref[0:1, :] * dext[3:3 + tm, :]
        for j in range(1, CONV_W):
            dxr = dxr + cw_ref[j:j + 1, :] * dext[3 - j:3 - j + tm, :]
        dp_ref[:, :D_RG] = dxr.astype(BF16)
        dext[tm:tm + 8, :] = dext[0:8, :]

        @pl.when(i == nt - 1)
        def _():
            fold = _head_fold()
            mask = _head_mask()
            for k in range(2):
                blockdiag = jnp.where(mask, gacc[:, k * D_RG:(k + 1) * D_RG], 0.0)
                gw_ref[k * D_RG:(k + 1) * D_RG, :] = jnp.dot(blockdiag, fold, precision=HIGHEST,
                                                             preferred_element_type=F32)

    vec = lambda n: pl.BlockSpec((1, n), lambda i: (0, 0))
    rev = lambda n: pl.BlockSpec((tm, n), lambda i: (nt - 1 - i, 0))
    halo = lambda n: pl.BlockSpec((8, n), lambda i: (jnp.maximum((nt - 1 - i) * hb - 1, 0), 0))
    return pl.pallas_call(
        body, grid=(nt,),
        in_specs=[rev(2 * D_RG), halo(2 * D_RG), rev(D_RG), halo(D_RG), rev(D_RG), ANY,
                  pl.BlockSpec((CONV_W, D_RG), lambda i: (0, 0)), vec(D_RG),
                  pl.BlockSpec((D_RG, 2 * D_RG), lambda i: (0, 0)), vec(2 * D_RG), vec(D_RG), vec(D_RG)],
        out_specs=[rev(2 * D_RG), pl.BlockSpec((CONV_W, D_RG), lambda i: (0, 0)), vec(D_RG),
                   pl.BlockSpec((2 * D_RG, RG_HEAD_DIM), lambda i: (0, 0)), vec(2 * D_RG), vec(D_RG), vec(D_RG)],
        input_output_aliases={5: 0},
        out_shape=[jax.ShapeDtypeStruct((T, D_IN), BF16), jax.ShapeDtypeStruct((CONV_W, D_RG), F32),
                   jax.ShapeDtypeStruct((1, D_RG), F32), jax.ShapeDtypeStruct((2 * D_RG, RG_HEAD_DIM), F32),
                   jax.ShapeDtypeStruct((1, 2 * D_RG), F32), jax.ShapeDtypeStruct((1, D_RG), F32),
                   jax.ShapeDtypeStruct((1, D_RG), F32)],
        scratch_shapes=[pltpu.VMEM((tm + 8, D_RG), F32), pltpu.VMEM((tm + 8, D_RG), F32),
                        pltpu.VMEM((tm, D_RG), F32), pltpu.VMEM((tm, D_RG), F32), pltpu.VMEM((tm, D_RG), F32),
                        pltpu.VMEM((D_RG, 2 * D_RG), F32), pltpu.VMEM((1, D_RG), F32), pltpu.VMEM((1, D_RG), F32)],
        name="rg_bwd", compiler_params=_params("arbitrary"),
    )(p, p, hs, hs, dy, dp, cw, cb, wg, bg, lam, rg_g)


def _hg_bwd(p, o_all, st_all, dy, lbraw, hg_g):
    T = p.shape[0]
    n_chunks = T // CHUNK
    cps = _chunks_per_step(n_chunks)
    rows = cps * CHUNK
    n_steps = n_chunks // cps

    def body(hq_ref, hf_ref, hi_ref, hg_ref, o_ref, st_ref, dy_ref, lb_ref, g_ref,
             dp_ref, glb_ref, gg_ref, dst):
        i = pl.program_id(0)

        @pl.when(i == 0)
        def _():
            dst[...] = jnp.zeros_like(dst)
            glb_ref[...] = jnp.zeros_like(glb_ref)
            gg_ref[...] = jnp.zeros_like(gg_ref)

        dp_ref[:, :2 * D_RG] = jnp.zeros((rows, 2 * D_RG), BF16)

        def chunk(jj, carry):
            j = cps - 1 - jj
            rs = pl.ds(pl.multiple_of(j * CHUNK, CHUNK), CHUNK)
            chunk_body((n_steps - 1 - i) * cps + j, hq_ref.at[rs, :], hf_ref.at[rs, :], hi_ref.at[rs, :],
                       hg_ref.at[rs, :], o_ref.at[rs, :], st_ref.at[pl.ds(j, 1)], dy_ref.at[rs, :], lb_ref, g_ref,
                       dp_ref.at[rs, pl.ds(2 * D_RG, 4 * D_HG)], glb_ref, gg_ref, dst)
            return carry

        lax.fori_loop(0, cps, chunk, 0)

    def chunk_body(n, hq_ref, hf_ref, hi_ref, hg_ref, o_ref, st_ref, dy_ref, lb_ref, g_ref,
                   dp_ref, glb_ref, gg_ref, dst):
        valid = (n * CHUNK + lax.broadcasted_iota(jnp.int32, (CHUNK, 1), 0)) >= PAD
        hq, hf, v, hg = hq_ref[...], hf_ref[...], hi_ref[...], hg_ref[...]
        lb, sq, q, sf, f, b = _hg_gates(hq, hf, lb_ref, valid)
        k = 1.0 - f
        causal = _causal()
        is_last = lax.broadcasted_iota(jnp.int32, (CHUNK, 1), 0) == CHUNK - 1
        g_ = g_ref[...]
        db_parts, dq_parts, dk_parts = [], [], []
        gg = jnp.zeros((1, HG_HEAD_DIM), F32)
        for h in range(HG_HEADS):
            sl = slice(h * HG_HEAD_DIM, (h + 1) * HG_HEAD_DIM)
            qh, kh, vh, bh = q[:, sl], k[:, sl], v[:, sl], b[:, sl]
            o = o_ref[:, sl]
            ro = _rms(o)
            no = o * ro
            hgh = hg[:, sl]
            sg = _sigmoid(hgh)
            dyh = dy_ref[:, sl]
            dp_ref[:, 3 * D_HG + h * HG_HEAD_DIM:3 * D_HG + (h + 1) * HG_HEAD_DIM] = (
                dyh * no * g_ * sg * (1.0 + hgh * (1.0 - sg))).astype(BF16)
            dng = dyh * hgh * sg
            gg = gg + jnp.sum(dng * no, axis=0, keepdims=True)
            do = _rms_bwd(dng * g_, no, ro)
            dob = do.astype(BF16)

            st_h = st_ref[0, sl, :]
            dst_h = dst[sl, :]
            blk, b_last, eb, eq, ekh, ek, q_hat, k_til = _hg_head(qh, kh, bh)
            q_til = qh * eb
            k_hat = kh * ekh
            vb = vh.astype(BF16)
            dstb = dst_h.astype(BF16)
            qhb, ktb = q_hat.astype(BF16), k_til.astype(BF16)
            att = jnp.where(causal, _dot_nt(qhb, ktb), 0.0)
            datt = jnp.where(causal, _dot_nt(dob, vb), 0.0).astype(BF16)

            dk_hat = _dot(vb, dstb)
            dv = _dot_nt(k_hat.astype(BF16), dstb) + _dot_tn(att.astype(BF16), dob)
            dq_til = _dot(dob, st_h.astype(BF16))
            e_last = jnp.exp(b_last)
            db_last = (jnp.sum(dk_hat * k_hat, axis=0, keepdims=True)
                       + e_last * jnp.sum(dst_h * st_h, axis=0, keepdims=True))
            dst[sl, :] = dst_h * e_last + _dot_tn(dob, q_til.astype(BF16))

            dq_hat = _dot(datt, ktb)
            dk_til = _dot_tn(datt, qhb)
            dq_sel = dq_hat[:, (N_SUB - 1) * HG_HEAD_DIM:]
            for s in range(N_SUB - 2, -1, -1):
                dq_sel = jnp.where(blk == s, dq_hat[:, s * HG_HEAD_DIM:(s + 1) * HG_HEAD_DIM], dq_sel)
            dq_a = dq_sel * eq
            dk_a = dk_til[:, :HG_HEAD_DIM] * ek[0]
            for s in range(1, N_SUB):
                dk_a = dk_a + dk_til[:, s * HG_HEAD_DIM:(s + 1) * HG_HEAD_DIM] * ek[s]
            db_att = qhb.astype(F32) * dq_hat - ktb.astype(F32) * dk_til
            db = dq_til * q_til - dk_hat * k_hat
            for s in range(N_SUB):
                db = db + db_att[:, s * HG_HEAD_DIM:(s + 1) * HG_HEAD_DIM]
            db_parts.append(jnp.where(is_last, db + db_last, db))
            dq_parts.append(dq_til * eb + dq_a)
            dk_parts.append(dk_hat * ekh + dk_a)
            dp_ref[:, 2 * D_HG + h * HG_HEAD_DIM:2 * D_HG + (h + 1) * HG_HEAD_DIM] = dv.astype(BF16)

        gg_ref[...] += gg
        db = jnp.concatenate(db_parts, axis=1)
        dq = jnp.concatenate(dq_parts, axis=1)
        dk = jnp.concatenate(dk_parts, axis=1)
        dlf = jnp.where(valid, jnp.dot(_tri(False), db, precision=HIGHEST, preferred_element_type=F32), 0.0)
        dp_ref[:, :D_HG] = (dq * sq * (1.0 + hq * (1.0 - sq))).astype(BF16)
        df = dlf / f - dk
        dlb = jnp.sum(df * (1.0 - sf), axis=0, keepdims=True) * lb * (1.0 - lb)
        glb_ref[0:1, :] += dlb
        glb_ref[1:2, :] += -dlb
        dp_ref[:, D_HG:2 * D_HG] = (df * (1.0 - lb) * sf * (1.0 - sf)).astype(BF16)

    rev = lambda j: pl.BlockSpec((rows, D_HG), lambda i: (n_steps - 1 - i, j))
    return pl.pallas_call(
        body, grid=(n_steps,),
        in_specs=[rev(2), rev(3), rev(4), rev(5), rev(0),
                  pl.BlockSpec((cps, D_HG, HG_HEAD_DIM), lambda i: (n_steps - 1 - i, 0, 0)), rev(1),
                  pl.BlockSpec((2, D_HG), lambda i: (0, 0)), pl.BlockSpec((1, HG_HEAD_DIM), lambda i: (0, 0))],
        out_specs=[pl.BlockSpec((rows, D_IN), lambda i: (n_steps - 1 - i, 0)),
                   pl.BlockSpec((2, D_HG), lambda i: (0, 0)), pl.BlockSpec((1, HG_HEAD_DIM), lambda i: (0, 0))],
        out_shape=[jax.ShapeDtypeStruct((T, D_IN), BF16), jax.ShapeDtypeStruct((2, D_HG), F32),
                   jax.ShapeDtypeStruct((1, HG_HEAD_DIM), F32)],
        scratch_shapes=[pltpu.VMEM((D_HG, HG_HEAD_DIM), F32)],
        name="hg_bwd", compiler_params=_params("arbitrary"),
    )(p, p, p, p, o_all, st_all, dy, lbraw, hg_g)


def _in_bwd(dp, w_in, h0, g1, dh1):
    T = h0.shape[0]
    tm = _row_tile(T, 416)

    def body(dp_ref, w_ref, h_ref, g_ref, d1_ref, dh0_ref, gg_ref):
        i = pl.program_id(0)

        @pl.when(i == 0)
        def _():
            gg_ref[...] = jnp.zeros_like(gg_ref)

        du = _dot_nt(dp_ref[...], w_ref[...])
        h0_ = h_ref[...]
        r = _rms(h0_)
        n = h0_ * r
        gg_ref[...] += jnp.sum(du * n, axis=0, keepdims=True)
        dh0_ref[...] = d1_ref[...] + _rms_bwd(du * g_ref[...], n, r)

    row = lambda n: pl.BlockSpec((tm, n), lambda i: (i, 0))
    return pl.pallas_call(
        body, grid=(T // tm,),
        in_specs=[row(D_IN), pl.BlockSpec((D_MODEL, D_IN), lambda i: (0, 0)),
                  row(D_MODEL), pl.BlockSpec((1, D_MODEL), lambda i: (0, 0)), row(D_MODEL)],
        out_specs=[row(D_MODEL), pl.BlockSpec((1, D_MODEL), lambda i: (0, 0))],
        out_shape=[jax.ShapeDtypeStruct((T, D_MODEL), F32), jax.ShapeDtypeStruct((1, D_MODEL), F32)],
        name="in_bwd", compiler_params=_params("arbitrary"),
    )(dp, w_in, h0, g1, dh1)


def _col_tile(cols, target):
    best = None
    for t in range(128, min(cols, target) + 1, 128):
        if cols % t == 0:
            best = t
    assert best is not None, cols
    return best


MXU_DIM = 256


def _mxu_tile(cols, target):
    best = None
    for t in range(MXU_DIM, min(cols, target) + 1, MXU_DIM):
        if cols % t == 0:
            best = t
    assert best is not None, cols
    return best


def _weight_grad(a, b, name):
    T, M = a.shape
    N = b.shape[1]
    tm = _col_tile(M, 1408)
    tn = _mxu_tile(N, 768 if tm <= 1024 else 512)

    def body(a_ref, b_ref, o_ref):
        o_ref[...] = _dot_tn(a_ref[...], b_ref[...])

    return pl.pallas_call(
        body, grid=(M // tm, N // tn),
        in_specs=[pl.BlockSpec((T, tm), lambda m, n: (0, m)), pl.BlockSpec((T, tn), lambda m, n: (0, n))],
        out_specs=pl.BlockSpec((tm, tn), lambda m, n: (m, n)),
        out_shape=jax.ShapeDtypeStruct((M, N), F32),
        name=name, compiler_params=_params("parallel", "parallel"),
    )(a, b)


def _local_step(h0, target, w_in, w_out, w_gu, w_down, small, on_ffn_grads=None, on_mixer_grads=None):
    wg = _gate_weights(small["w_rgate"], small["w_igate"])
    bg = jnp.concatenate([small["b_rgate"], small["b_igate"]], axis=1)

    p, u = _in_proj(h0, small["mix_norm_g"], w_in)
    y_rg, hs = _rg_fwd(p, small["conv_w"], small["conv_b"], wg, bg, small["lru_lambda"], small["rg_norm_g"])
    y_hg, o_all, st_all = _hg_fwd(p, small["hg_lower_bound"], small["hg_norm_g"])
    h1, v, yb = _out_proj(h0, y_rg, y_hg, w_out, small["ffn_norm_g"])
    gu, act = _gate_up(v, w_gu)
    dh2, dh2b, loss, g_final = _down_loss(h1, act, w_down, small["final_norm_g"], target)

    dgu = _ffn_bwd_act(dh2b, gu, w_down)
    ffn_grads = {"w_gate_up": _weight_grad(v, dgu, "grad_w_gate_up"),
                 "w_down": _weight_grad(act, dh2b, "grad_w_down")}
    stages = on_ffn_grads(ffn_grads) if on_ffn_grads is not None else None
    dh1, dh1b, dy, g_ffn = _ffn_bwd_in(dgu, w_gu, h1, small["ffn_norm_g"], dh2, w_out)
    early = late = None
    if stages is not None:
        chip_sums, send = stages
        sums = chip_sums()
        (dh1, dh1b, dy), sums = lax.optimization_barrier(((dh1, dh1b, dy), sums))
        early = send(sums)
    dp, g_lb, g_hgn = _hg_bwd(p, o_all, st_all, dy, small["hg_lower_bound"], small["hg_norm_g"])
    dp, g_cw, g_cb, g_wgate, g_bg, g_lam, g_rgn = _rg_bwd(
        p, hs, dy, dp, small["conv_w"], small["conv_b"], wg, bg, small["lru_lambda"], small["rg_norm_g"])
    mixer_grads = {"w_in": _weight_grad(u, dp, "grad_w_in"), "w_out": _weight_grad(yb, dh1b, "grad_w_out")}
    if on_mixer_grads is not None:
        chip_sums, send = on_mixer_grads(mixer_grads)
        sums = chip_sums()
        (dp, dh1), sums = lax.optimization_barrier(((dp, dh1), sums))
        late = send(sums)
    dh0, g_mix = _in_bwd(dp, w_in, h0, small["mix_norm_g"], dh1)

    grads = {
        "w_in": mixer_grads["w_in"], "w_out": mixer_grads["w_out"],
        "w_gate_up": ffn_grads["w_gate_up"], "w_down": ffn_grads["w_down"],
        "mix_norm_g": g_mix, "conv_w": g_cw, "conv_b": g_cb, "w_gates": g_wgate,
        "b_rgate": g_bg[:, :D_RG], "b_igate": g_bg[:, D_RG:], "lru_lambda": g_lam, "rg_norm_g": g_rgn,
        "hg_lower_bound": g_lb, "hg_norm_g": g_hgn, "ffn_norm_g": g_ffn, "final_norm_g": g_final,
    }
    return loss, dh0, grads, early, late


ANY = pl.BlockSpec(memory_space=pl.ANY)
HALF = D_MODEL // 2

BIG = {"w_in": (D_MODEL, D_IN // N_CHIPS, True), "w_gate_up": (D_MODEL, 2 * D_FF // N_CHIPS, True),
       "w_out": (D_MODEL // N_CHIPS, D_MODEL, False), "w_down": (D_FF // N_CHIPS, D_MODEL, False)}
BIG_NAMES = tuple(BIG)
N_BIG = len(BIG_NAMES)


def _full_shape(name):
    rows, cols, by_col = BIG[name]
    return (rows, cols * N_CHIPS) if by_col else (rows * N_CHIPS, cols)


def _place():
    return lax.axis_index("x"), lax.axis_index("y"), lax.axis_index("c")


def _chip_of(x, y, r):
    fx, fy = (r + 1) >> 1, (r + 1) & 1
    return (1 - x if fx else x), (1 - y if fy else y)


def _half_of(ref, by_col, half):
    start = pl.multiple_of(half * HALF, 128)
    return ref.at[pl.ds(start, HALF), :] if by_col else ref.at[:, pl.ds(start, HALF)]


def _shard_of(ref, name, chip):
    rows, cols, by_col = BIG[name]
    if by_col:
        return ref.at[:, pl.ds(pl.multiple_of(chip * cols, 128), cols)]
    return ref.at[pl.ds(pl.multiple_of(chip * rows, 16), rows), :]


def _shard_half_of(ref, name, chip, half):
    rows, cols, by_col = BIG[name]
    start = pl.multiple_of(half * HALF, 128)
    if by_col:
        return ref.at[pl.ds(start, HALF), pl.ds(pl.multiple_of(chip * cols, 128), cols)]
    return ref.at[pl.ds(pl.multiple_of(chip * rows, 16), rows), pl.ds(start, HALF)]


def _remote(src, dst, send_sems, recv_sems, k, dev):
    return pltpu.make_async_remote_copy(src_ref=src, dst_ref=dst, send_sem=send_sems.at[k], recv_sem=recv_sems.at[k],
                                        device_id=dev, device_id_type=MESH)


def _cast_into_full(w_shard, name, chip):
    rows, cols, by_col = BIG[name]
    tr = _row_tile(rows, 352)
    if by_col:
        out_spec = pl.BlockSpec((tr, cols), lambda i, s: (i, s[0]))
    else:
        out_spec = pl.BlockSpec((tr, cols), lambda i, s: (s[0] * (rows // tr) + i, 0))

    def body(s_ref, w_ref, o_ref):
        o_ref[...] = w_ref[...].astype(BF16)

    return pl.pallas_call(
        body,
        grid_spec=pltpu.PrefetchScalarGridSpec(
            num_scalar_prefetch=1, grid=(rows // tr,), in_specs=[pl.BlockSpec((tr, cols), lambda i, s: (i, 0))],
            out_specs=out_spec),
        out_shape=jax.ShapeDtypeStruct(_full_shape(name), BF16),
        name="cast_" + name, compiler_params=_params("parallel"),
    )(chip, w_shard)


def _gather_weights(placed, small, names, label, collective_id):
    n, ns = len(names), len(small)
    hbm = pltpu.MemorySpace.HBM
    outs = [jax.new_ref(placed[nm], memory_space=hbm) for nm in names]
    small_in = [jax.new_ref(s, memory_space=hbm) for s in small]
    small_out = [jax.empty_ref(jax.ShapeDtypeStruct((s.shape[0], s.shape[1] * N_CHIPS), F32), memory_space=hbm)
                 for s in small]
    n_sems = 6 * n + 3 * ns

    @pl.kernel(mesh=plsc.ScalarSubcoreMesh(axis_name="seq", num_cores=1), name=label, out_type=(),
               scratch_types=(pltpu.SemaphoreType.DMA((n_sems,)), pltpu.SemaphoreType.DMA((n_sems,)),
                              pltpu.SemaphoreType.DMA((max(ns, 1),))),
               compiler_params=pltpu.CompilerParams(collective_id=collective_id))
    def launch(send_sems, recv_sems, local_sems):
        x, y, c = _place()
        chip = 2 * x + y
        sibling = (x, y, 1 - c)
        others = [_chip_of(x, y, r) for r in range(3)]
        _handshake([(qx, qy, c) for qx, qy in others] + [sibling])

        def small_block(a, q):
            cols = small[a].shape[1]
            return small_out[a].at[:, pl.ds(pl.multiple_of(q * cols, 128), cols)]

        local = [pltpu.make_async_copy(small_in[a], small_block(a, chip), local_sems.at[a]) for a in range(ns)]
        for cp in local:
            cp.start()

        sends = []
        for a, name in enumerate(names):
            mine = _shard_half_of(outs[a], name, chip, c)
            for r, (qx, qy) in enumerate(others):
                sends.append(_remote(mine, mine, send_sems, recv_sems, 6 * a + r, (qx, qy, c)))
        for a in range(ns):
            for r, (qx, qy) in enumerate(others):
                sends.append(_remote(small_in[a], small_block(a, chip), send_sems, recv_sems,
                                     6 * n + 3 * a + r, (qx, qy, c)))
        for cp in sends:
            cp.start()

        forwards = []
        for a, name in enumerate(names):
            for r, (qx, qy) in enumerate(others):
                landed = _shard_half_of(outs[a], name, 2 * qx + qy, c)
                _remote(landed, landed, send_sems, recv_sems, 6 * a + r, (qx, qy, c)).wait_recv()
                fwd = _remote(landed, landed, send_sems, recv_sems, 6 * a + 3 + r, sibling)
                fwd.start()
                forwards.append(fwd)
        for a in range(ns):
            for r, (qx, qy) in enumerate(others):
                landed = small_block(a, 2 * qx + qy)
                _remote(landed, landed, send_sems, recv_sems, 6 * n + 3 * a + r, (qx, qy, c)).wait_recv()
        for a, name in enumerate(names):
            for r, (qx, qy) in enumerate(others):
                landed = _shard_half_of(outs[a], name, 2 * qx + qy, 1 - c)
                _remote(landed, landed, send_sems, recv_sems, 6 * a + 3 + r, sibling).wait_recv()
        for cp in sends + forwards:
            cp.wait_send()
        for cp in local:
            cp.wait()

    launch()
    return {nm: ref[...] for nm, ref in zip(names, outs)}, [ref[...] for ref in small_out]


def _exchange_halves(grads, names, label, collective_id):
    n = len(names)
    sequencer = collective_id is not None

    def body(*refs):
        ins, outs = refs[:n], refs[n:2 * n]
        send_sems, recv_sems = refs[2 * n:]
        x, y, c = _place()
        if sequencer:
            _handshake([(x, y, 1 - c)])
        copies = []
        for a, name in enumerate(names):
            copies.append(_remote(_half_of(ins[a], BIG[name][2], 1 - c), outs[a], send_sems, recv_sems, a,
                                  (x, y, 1 - c)))
        for cp in copies:
            cp.start()
        for cp in copies:
            cp.wait()

    def half_shape(name):
        r, c_ = _full_shape(name)
        return (HALF, c_) if BIG[name][2] else (r, HALF)

    out_type = tuple(jax.ShapeDtypeStruct(half_shape(nm), F32) for nm in names)
    sems = (pltpu.SemaphoreType.DMA((n,)), pltpu.SemaphoreType.DMA((n,)))
    operands = [grads[nm] for nm in names]
    if sequencer:
        got = pl.kernel(
            body, mesh=plsc.ScalarSubcoreMesh(axis_name="seq", num_cores=1), name=label, out_type=out_type,
            scratch_types=sems, compiler_params=pltpu.CompilerParams(collective_id=collective_id),
        )(*operands)
    else:
        got = pl.pallas_call(
            body, in_specs=[ANY] * n, out_specs=[ANY] * n, out_shape=list(out_type), scratch_shapes=list(sems),
            name=label,
        )(*operands)
    return dict(zip(names, got))


def _chip_sum(g, got, name, core):
    by_col = BIG[name][2]
    rows, cols = got.shape
    if by_col:
        tr = 128
        g_spec = pl.BlockSpec((tr, cols), lambda i, s: (s[0] * (HALF // tr) + i, 0))
    else:
        tr = _row_tile(rows, 512)
        g_spec = pl.BlockSpec((tr, HALF), lambda i, s: (i, s[0]))
    blk = pl.BlockSpec((tr, cols), lambda i, s: (i, 0))

    def body(s_ref, g_ref, r_ref, f_ref, b_ref):
        t = g_ref[...] + r_ref[...]
        f_ref[...] = t
        b_ref[...] = t.astype(BF16)

    return pl.pallas_call(
        body,
        grid_spec=pltpu.PrefetchScalarGridSpec(num_scalar_prefetch=1, grid=(rows // tr,), in_specs=[g_spec, blk],
                                               out_specs=[blk, blk]),
        out_shape=[jax.ShapeDtypeStruct(got.shape, F32), jax.ShapeDtypeStruct(got.shape, BF16)],
        name="chip_sum_" + name, compiler_params=_params("parallel"),
    )(core, g, got)


def _piece_shape(name):
    rows, cols, by_col = BIG[name]
    return (HALF, cols) if by_col else (rows, HALF)


def _handshake(peers):
    barrier = pltpu.get_barrier_semaphore()
    for peer in peers:
        pl.semaphore_signal(barrier, inc=1, device_id=peer, device_id_type=MESH)
    pl.semaphore_wait(barrier, len(peers))


def _send_chip_sums(sums, names, label, collective_id):
    n = len(names)

    def body(*refs):
        ins, outs = refs[:n], refs[n:2 * n]
        send_sems, recv_sems = refs[2 * n:]
        x, y, c = _place()
        others = [_chip_of(x, y, r) for r in range(3)]
        _handshake([(qx, qy, c) for qx, qy in others])
        copies = []
        for a, name in enumerate(names):
            for r, (qx, qy) in enumerate(others):
                copies.append(_remote(_shard_of(ins[a], name, 2 * qx + qy), outs[a].at[r], send_sems, recv_sems,
                                      3 * a + r, (qx, qy, c)))
        for cp in copies:
            cp.start()
        for cp in copies:
            cp.wait()

    return pl.kernel(
        body, mesh=plsc.ScalarSubcoreMesh(axis_name="seq", num_cores=1), name=label,
        out_type=tuple(jax.ShapeDtypeStruct((3,) + _piece_shape(nm), BF16) for nm in names),
        scratch_types=(pltpu.SemaphoreType.DMA((3 * n,)), pltpu.SemaphoreType.DMA((3 * n,))),
        compiler_params=pltpu.CompilerParams(collective_id=collective_id),
    )(*[sums[nm] for nm in names])


def _total(own, got, name, chip_core):
    rows, cols, by_col = BIG[name]
    pr, pc = _piece_shape(name)
    tr = _row_tile(pr, 352)
    if by_col:
        own_spec = pl.BlockSpec((tr, pc), lambda i, s: (i, s[0]))
    else:
        own_spec = pl.BlockSpec((tr, pc), lambda i, s: (s[0] * (pr // tr) + i, 0))
    got_spec = lambda r: pl.BlockSpec((None, tr, pc), lambda i, s: (r, i, 0))
    if by_col:
        out_spec = pl.BlockSpec((tr, pc), lambda i, s: (s[1] * (pr // tr) + i, 0))
    else:
        out_spec = pl.BlockSpec((tr, pc), lambda i, s: (i, s[1]))

    def body(s_ref, o_ref, a_ref, b_ref, c_ref, t_ref):
        t_ref[...] = ((o_ref[...] + a_ref[...].astype(F32)) + b_ref[...].astype(F32)) + c_ref[...].astype(F32)

    return pl.pallas_call(
        body,
        grid_spec=pltpu.PrefetchScalarGridSpec(
            num_scalar_prefetch=1, grid=(pr // tr,), in_specs=[own_spec, got_spec(0), got_spec(1), got_spec(2)],
            out_specs=out_spec),
        out_shape=jax.ShapeDtypeStruct((rows, cols), F32),
        name="total_" + name, compiler_params=_params("parallel"),
    )(chip_core, own, got, got, got)


def _share_totals(totals):
    def body(*refs):
        outs = refs[N_BIG:2 * N_BIG]
        send_sems, recv_sems = refs[2 * N_BIG:]
        x, y, c = _place()
        copies = []
        for a, name in enumerate(BIG_NAMES):
            mine = _half_of(outs[a], BIG[name][2], c)
            copies.append(_remote(mine, mine, send_sems, recv_sems, a, (x, y, 1 - c)))
        for cp in copies:
            cp.start()
        for a, name in enumerate(BIG_NAMES):
            theirs = _half_of(outs[a], BIG[name][2], 1 - c)
            _remote(theirs, theirs, send_sems, recv_sems, a, (x, y, 1 - c)).wait_recv()
        for cp in copies:
            cp.wait_send()

    return pl.pallas_call(
        body, in_specs=[ANY] * N_BIG, out_specs=[ANY] * N_BIG,
        out_shape=[jax.ShapeDtypeStruct(BIG[n][:2], F32) for n in BIG_NAMES],
        input_output_aliases={a: a for a in range(N_BIG)},
        scratch_shapes=[pltpu.SemaphoreType.DMA((N_BIG,)), pltpu.SemaphoreType.DMA((N_BIG,))],
        name="share_totals",
    )(*[totals[n] for n in BIG_NAMES])


VEC_ROWS = 32
VEC_ROW = {"mix_norm_g": 0, "conv_b": 1, "b_rgate": 2, "b_igate": 3, "lru_lambda": 4, "rg_norm_g": 5,
           "hg_lower_bound": 6, "hg_norm_g": 8, "ffn_norm_g": 9, "final_norm_g": 10, "loss": 11,
           "conv_w": 12, "meta_tokens": 16}
N_DEV = 8


def _all_reduce_small(pieces, gates):
    names = list(pieces)
    hv, hg = VEC_ROWS // 2, gates.shape[0] // 2

    def body(*refs):
        ins = refs[:len(names)]
        (g_ref, vec_ref, gsum_ref, mine_v, sib_v, sib_g, chip_v, chip_g, got_v, got_g,
         send_sems, recv_sems) = refs[len(names):]
        x, y, c = _place()
        chip = 2 * x + y
        sibling = (x, y, 1 - c)
        mine_v[...] = jnp.zeros_like(mine_v)
        for name, ref in zip(names, ins):
            nr, w = ref.shape
            mine_v[VEC_ROW[name]:VEC_ROW[name] + nr, 0:w] = ref[...]

        swap = [_remote(mine_v, sib_v, send_sems, recv_sems, 0, sibling),
                _remote(g_ref, sib_g, send_sems, recv_sems, 1, sibling)]
        for cp in swap:
            cp.start()
        for cp in swap:
            cp.wait()
        chip_v[...] = mine_v[...] + sib_v[...]
        chip_g[...] = g_ref[...] + sib_g[...]

        rows_v = pl.ds(pl.multiple_of(c * hv, 8), hv)
        rows_g = pl.ds(pl.multiple_of(c * hg, 8), hg)
        got_v[chip] = chip_v[rows_v, :]
        got_g[chip] = chip_g[rows_g, :]
        sends = []
        for r in range(3):
            qx, qy = _chip_of(x, y, r)
            sends.append(_remote(chip_v.at[rows_v, :], got_v.at[chip], send_sems, recv_sems, 2 + r, (qx, qy, c)))
            sends.append(_remote(chip_g.at[rows_g, :], got_g.at[chip], send_sems, recv_sems, 5 + r, (qx, qy, c)))
        for cp in sends:
            cp.start()
        for cp in sends:
            cp.wait()
        vec_ref[rows_v, :] = ((got_v[0] + got_v[1]) + got_v[2]) + got_v[3]
        gsum_ref[rows_g, :] = ((got_g[0] + got_g[1]) + got_g[2]) + got_g[3]

        back = [_remote(vec_ref.at[rows_v, :], vec_ref.at[rows_v, :], send_sems, recv_sems, 8, sibling),
                _remote(gsum_ref.at[rows_g, :], gsum_ref.at[rows_g, :], send_sems, recv_sems, 9, sibling)]
        for cp in back:
            cp.start()
        theirs_v = vec_ref.at[pl.ds(pl.multiple_of((1 - c) * hv, 8), hv), :]
        theirs_g = gsum_ref.at[pl.ds(pl.multiple_of((1 - c) * hg, 8), hg), :]
        _remote(theirs_v, theirs_v, send_sems, recv_sems, 8, sibling).wait_recv()
        _remote(theirs_g, theirs_g, send_sems, recv_sems, 9, sibling).wait_recv()
        for cp in back:
            cp.wait_send()

    vmem = pl.BlockSpec(memory_space=pltpu.VMEM)
    n_sems = 10
    return pl.pallas_call(
        body, in_specs=[vmem] * (len(names) + 1), out_specs=[vmem, vmem],
        out_shape=[jax.ShapeDtypeStruct((VEC_ROWS, D_MODEL), F32), jax.ShapeDtypeStruct(gates.shape, F32)],
        scratch_shapes=[pltpu.VMEM((VEC_ROWS, D_MODEL), F32), pltpu.VMEM((VEC_ROWS, D_MODEL), F32),
                        pltpu.VMEM(gates.shape, F32), pltpu.VMEM((VEC_ROWS, D_MODEL), F32),
                        pltpu.VMEM(gates.shape, F32), pltpu.VMEM((N_CHIPS, hv, D_MODEL), F32),
                        pltpu.VMEM((N_CHIPS, hg) + gates.shape[1:], F32),
                        pltpu.SemaphoreType.DMA((n_sems,)), pltpu.SemaphoreType.DMA((n_sems,))],
        name="all_reduce_small",
    )(*[pieces[n] for n in names], gates)


def _adamw_math(w, g, m, v):
    m = ADAM_B1 * m + (1.0 - ADAM_B1) * g
    v = ADAM_B2 * v + (1.0 - ADAM_B2) * (g * g)
    m_hat = m / (1.0 - ADAM_B1 ** ADAM_STEP)
    v_hat = v / (1.0 - ADAM_B2 ** ADAM_STEP)
    delta = -ADAM_LR * (m_hat / (jnp.sqrt(v_hat) + ADAM_EPS) + ADAM_WD * w)
    return delta, m, v


def _adamw_big(w, g, m, v, name):
    rows, cols = w.shape
    tr = _row_tile(rows, 352)

    def body(w_ref, g_ref, m_ref, v_ref, d_ref, nm_ref, nv_ref):
        d_ref[...], nm_ref[...], nv_ref[...] = _adamw_math(w_ref[...], g_ref[...], m_ref[...], v_ref[...])

    blk = pl.BlockSpec((tr, cols), lambda i: (i, 0))
    return pl.pallas_call(
        body, grid=(rows // tr,), in_specs=[blk] * 4, out_specs=[blk] * 3,
        out_shape=[jax.ShapeDtypeStruct(w.shape, F32)] * 3,
        name="adamw_" + name, compiler_params=_params("parallel"),
    )(w, g, m, v)


SMALL = {"meta_tokens": (N_META, D_MODEL // N_CHIPS), "mix_norm_g": (1, D_MODEL), "conv_w": (CONV_W, D_RG // N_CHIPS),
         "conv_b": (1, D_RG), "w_rgate": (D_RG, RG_HEAD_DIM), "b_rgate": (1, D_RG), "w_igate": (D_RG, RG_HEAD_DIM),
         "b_igate": (1, D_RG), "lru_lambda": (1, D_RG), "rg_norm_g": (1, D_RG), "hg_lower_bound": (2, D_HG),
         "hg_norm_g": (1, HG_HEAD_DIM), "ffn_norm_g": (1, D_MODEL), "final_norm_g": (1, D_MODEL)}
SMALL_NAMES = tuple(SMALL)
SHARDED_SMALL = ("meta_tokens", "conv_w")


def _adamw_small(vec, gates, w, m, v):
    n = len(SMALL_NAMES)

    def body(*refs):
        vec_ref, gates_ref = refs[:2]
        w_refs, m_refs, v_refs = refs[2:2 + n], refs[2 + n:2 + 2 * n], refs[2 + 2 * n:2 + 3 * n]
        outs = refs[2 + 3 * n:]
        loss_ref = outs[0]
        x, y, _ = _place()
        chip = 2 * x + y
        loss_ref[...] = vec_ref[VEC_ROW["loss"]:VEC_ROW["loss"] + 1, 0:1]

        def update(k, g):
            g_ref, d_ref, nm_ref, nv_ref = outs[1 + 4 * k:5 + 4 * k]
            g_ref[...] = g
            d_ref[...], nm_ref[...], nv_ref[...] = _adamw_math(w_refs[k][...], g, m_refs[k][...], v_refs[k][...])

        for k, name in enumerate(SMALL_NAMES):
            nr, w_ = SMALL[name]
            if name == "w_rgate":
                update(k, gates_ref[0:D_RG, :])
            elif name == "w_igate":
                update(k, gates_ref[D_RG:2 * D_RG, :])
            elif name in SHARDED_SMALL:
                r0 = VEC_ROW[name]
                for q in range(N_CHIPS):
                    @pl.when(chip == q)
                    def _(k=k, r0=r0, nr=nr, w_=w_, q=q):
                        update(k, vec_ref[r0:r0 + nr, q * w_:(q + 1) * w_])
            else:
                r0 = VEC_ROW[name]
                update(k, vec_ref[r0:r0 + nr, 0:w_])

    vmem = pl.BlockSpec(memory_space=pltpu.VMEM)
    out_shape = [jax.ShapeDtypeStruct((1, 1), F32)]
    for name in SMALL_NAMES:
        out_shape += [jax.ShapeDtypeStruct(SMALL[name], F32)] * 4
    outs = pl.pallas_call(
        body, in_specs=[vmem] * (2 + 3 * n), out_specs=[vmem] * len(out_shape), out_shape=out_shape,
        name="adamw_small",
    )(vec, gates, *[w[k] for k in SMALL_NAMES], *[m[k] for k in SMALL_NAMES], *[v[k] for k in SMALL_NAMES])
    loss = outs[0]
    res = {name: tuple(outs[1 + 4 * k:5 + 4 * k]) for k, name in enumerate(SMALL_NAMES)}
    return loss, res


WEIGHT_NAMES = ("meta_tokens", "mix_norm_g", "w_in", "conv_w", "conv_b", "w_rgate", "b_rgate", "w_igate", "b_igate",
                "lru_lambda", "rg_norm_g", "hg_lower_bound", "hg_norm_g", "w_out", "ffn_norm_g", "w_gate_up", "w_down",
                "final_norm_g")


def _to_2d(name, a):
    if name in BIG:
        return a.reshape(BIG[name][:2])
    return a.reshape(SMALL[name])


def kernel(x, meta_tokens, mix_norm_g, w_in, conv_w, conv_b, w_rgate, b_rgate, w_igate, b_igate, lru_lambda, rg_norm_g, hg_lower_bound, hg_norm_g, w_out, ffn_norm_g, w_gate_up, w_down, final_norm_g, loss_target, m_meta_tokens, m_mix_norm_g, m_w_in, m_conv_w, m_conv_b, m_w_rgate, m_b_rgate, m_w_igate, m_b_igate, m_lru_lambda, m_rg_norm_g, m_hg_lower_bound, m_hg_norm_g, m_w_out, m_ffn_norm_g, m_w_gate_up, m_w_down, m_final_norm_g, v_meta_tokens, v_mix_norm_g, v_w_in, v_conv_w, v_conv_b, v_w_rgate, v_b_rgate, v_w_igate, v_b_igate, v_lru_lambda, v_rg_norm_g, v_hg_lower_bound, v_hg_norm_g, v_w_out, v_ffn_norm_g, v_w_gate_up, v_w_down, v_final_norm_g):
    w_raw = dict(zip(WEIGHT_NAMES, (meta_tokens, mix_norm_g, w_in, conv_w, conv_b, w_rgate, b_rgate, w_igate, b_igate,
                                    lru_lambda, rg_norm_g, hg_lower_bound, hg_norm_g, w_out, ffn_norm_g, w_gate_up,
                                    w_down, final_norm_g)))
    m_raw = dict(zip(WEIGHT_NAMES, (m_meta_tokens, m_mix_norm_g, m_w_in, m_conv_w, m_conv_b, m_w_rgate, m_b_rgate,
                                    m_w_igate, m_b_igate, m_lru_lambda, m_rg_norm_g, m_hg_lower_bound, m_hg_norm_g,
                                    m_w_out, m_ffn_norm_g, m_w_gate_up, m_w_down, m_final_norm_g)))
    v_raw = dict(zip(WEIGHT_NAMES, (v_meta_tokens, v_mix_norm_g, v_w_in, v_conv_w, v_conv_b, v_w_rgate, v_b_rgate,
                                    v_w_igate, v_b_igate, v_lru_lambda, v_rg_norm_g, v_hg_lower_bound, v_hg_norm_g,
                                    v_w_out, v_ffn_norm_g, v_w_gate_up, v_w_down, v_final_norm_g)))
    w = {k: _to_2d(k, a) for k, a in w_raw.items()}
    m = {k: _to_2d(k, a) for k, a in m_raw.items()}
    v = {k: _to_2d(k, a) for k, a in v_raw.items()}

    x_i, y_i, c_i = _place()
    core = jnp.reshape(c_i, (1,)).astype(jnp.int32)
    chip = jnp.reshape(2 * x_i + y_i, (1,)).astype(jnp.int32)
    chip_core = jnp.concatenate([chip, core])

    placed = {k: _cast_into_full(w[k], k, chip) for k in BIG_NAMES}
    first, (meta_full, cw_full) = _gather_weights(placed, [w["meta_tokens"], w["conv_w"]], ("w_in",), "gather_first", 1)
    rest, _ = _gather_weights(placed, [], ("w_out", "w_gate_up", "w_down"), "gather_rest", 2)
    full = {**first, **rest}

    seq = x.shape[1]
    h0 = jnp.concatenate([jnp.zeros((PAD, D_MODEL), F32), meta_full, x[0]], axis=0)
    target = jnp.concatenate([jnp.zeros((PAD + N_META, D_MODEL), F32), loss_target[0]], axis=0)
    small = {k: w[k] for k in SMALL_NAMES if k not in SHARDED_SMALL}
    small["conv_w"] = cw_full

    def reduce_to_chips(grads, names, tag, collective_ids):
        got = _exchange_halves(grads, names, "exchange_halves_" + tag, collective_ids[0])

        def chip_sums():
            return {n: _chip_sum(grads[n], got[n], n, core) for n in names}

        def send(sums):
            arrived = _send_chip_sums({n: sums[n][1] for n in names}, names, "send_chip_sums_" + tag,
                                      collective_ids[1])
            return {n: (sums[n][0], a) for n, a in zip(names, arrived)}

        return chip_sums, send

    ffn_names, mixer_names = ("w_gate_up", "w_down"), ("w_in", "w_out")
    loss, dh0, grads, parts, parts_mixer = _local_step(
        h0, target, full["w_in"], full["w_out"], full["w_gate_up"], full["w_down"], small,
        on_ffn_grads=lambda g: reduce_to_chips(g, ffn_names, "ffn", (3, 4)),
        on_mixer_grads=lambda g: reduce_to_chips(g, mixer_names, "mixer", (None, 5)))
    parts.update(parts_mixer)
    totals = {n: _total(parts[n][0], parts[n][1], n, chip_core) for n in BIG_NAMES}
    g_big = dict(zip(BIG_NAMES, _share_totals(totals)))

    pieces = {k: grads[k] for k in VEC_ROW if k not in ("loss", "meta_tokens")}
    pieces["loss"] = loss
    pieces["meta_tokens"] = dh0[PAD:PAD + N_META]
    vec, gates = _all_reduce_small(pieces, grads["w_gates"])
    loss_sum, res = _adamw_small(vec, gates, w, m, v)
    for n in BIG_NAMES:
        res[n] = (g_big[n],) + tuple(_adamw_big(w[n], g_big[n], m[n], v[n], n))

    grad_x = dh0[PAD + N_META:].reshape(1, seq, D_MODEL)
    out = [loss_sum.reshape(()), grad_x]
    for j in range(4):
        out += [res[n][j].reshape(w_raw[n].shape) for n in WEIGHT_NAMES]
    return tuple(out)
```

```python
import functools
import math

import jax
import jax.numpy as jnp
from jax import lax
from jax.experimental import pallas as pl
from jax.experimental.pallas import tpu as pltpu
from jax.experimental.pallas import tpu_sc as plsc

F32 = jnp.float32
BF16 = jnp.bfloat16
HIGHEST = lax.Precision.HIGHEST
MESH = pl.DeviceIdType.MESH

D_MODEL = 1024
D_RG = 512
RG_HEAD_DIM = 64
D_HG = 512
HG_HEAD_DIM = 128
HG_HEADS = 4
CHUNK = 64
SUB = 16
N_SUB = CHUNK // SUB
N_META = 16
PAD = CHUNK - N_META
D_IN = 3072
D_FF = 2816
CONV_W = 4
LRU_C = 8.0
EPS = 1e-6
EXP_CLAMP = 80.0
GELU_C = math.sqrt(2.0 / math.pi)
GELU_A = 0.044715
N_CHIPS = 4

ADAM_LR = 0.001
ADAM_B1 = 0.9
ADAM_B2 = 0.999
ADAM_EPS = 1e-08
ADAM_WD = 0.01
ADAM_STEP = 10

VMEM_LIMIT = 56 * 1024 * 1024


def _params(*sem):
    return pltpu.CompilerParams(dimension_semantics=sem, vmem_limit_bytes=VMEM_LIMIT)


def _row_tile(rows, target):
    best = None
    for t in range(16, min(rows, target) + 1, 16):
        if rows % t == 0:
            best = t
    assert best is not None, rows
    return best


def _sigmoid(x):
    return 1.0 / (1.0 + jnp.exp(-x))


def _sigmoid_fast(x):
    return pl.reciprocal(1.0 + jnp.exp(-x), approx=True)


def _dot(a, b):
    return jnp.dot(a, b, preferred_element_type=F32)


def _dot_nt(a, b):
    return lax.dot_general(a, b, (((1,), (1,)), ((), ())), preferred_element_type=F32)


def _dot_tn(a, b):
    return lax.dot_general(a, b, (((0,), (0,)), ((), ())), preferred_element_type=F32)


def _rms(x):
    return lax.rsqrt(jnp.mean(x * x, axis=-1, keepdims=True) + EPS)


def _rms_bwd(dn, n, r):
    return r * (dn - n * jnp.mean(dn * n, axis=-1, keepdims=True))


def _gelu_parts(x):
    t = jnp.tanh(GELU_C * (x + GELU_A * x * x * x))
    g = 0.5 * x * (1.0 + t)
    dg = 0.5 * (1.0 + t) + 0.5 * x * (1.0 - t * t) * GELU_C * (1.0 + 3.0 * GELU_A * x * x)
    return g, dg


def _softplus_neg(lam):
    e = jnp.exp(-jnp.abs(lam))
    w = 1.0 + e
    log1p = jnp.where(w == 1.0, e, jnp.log(w) * e / (w - 1.0))
    return jnp.maximum(-lam, 0.0) + log1p


def _head_mask():
    r = lax.broadcasted_iota(jnp.int32, (D_RG, D_RG), 0) // RG_HEAD_DIM
    c = lax.broadcasted_iota(jnp.int32, (D_RG, D_RG), 1) // RG_HEAD_DIM
    return r == c


def _head_fold():
    r = lax.broadcasted_iota(jnp.int32, (D_RG, RG_HEAD_DIM), 0) % RG_HEAD_DIM
    c = lax.broadcasted_iota(jnp.int32, (D_RG, RG_HEAD_DIM), 1)
    return (r == c).astype(F32)


def _gate_weights(w_r, w_i):
    def body(wr_ref, wi_ref, o_ref):
        fold = _head_fold()
        mask = _head_mask()
        for k, ref in enumerate((wr_ref, wi_ref)):
            full = lax.dot_general(ref[...], fold, (((1,), (1,)), ((), ())),
                                   precision=HIGHEST, preferred_element_type=F32)
            o_ref[:, k * D_RG:(k + 1) * D_RG] = jnp.where(mask, full, 0.0).astype(BF16)

    return pl.pallas_call(
        body, out_shape=jax.ShapeDtypeStruct((D_RG, 2 * D_RG), BF16), name="gate_weights",
    )(w_r, w_i)


def _in_proj(h0, g1, w_in):
    T = h0.shape[0]
    tm = _row_tile(T, 416)

    def body(h_ref, g_ref, w_ref, p_ref, u_ref):
        h = h_ref[...]
        u = (h * _rms(h) * g_ref[...]).astype(BF16)
        u_ref[...] = u
        p_ref[...] = _dot(u, w_ref[...])

    return pl.pallas_call(
        body, grid=(T // tm,),
        in_specs=[pl.BlockSpec((tm, D_MODEL), lambda i: (i, 0)),
                  pl.BlockSpec((1, D_MODEL), lambda i: (0, 0)),
                  pl.BlockSpec((D_MODEL, D_IN), lambda i: (0, 0))],
        out_specs=[pl.BlockSpec((tm, D_IN), lambda i: (i, 0)),
                   pl.BlockSpec((tm, D_MODEL), lambda i: (i, 0))],
        out_shape=[jax.ShapeDtypeStruct((T, D_IN), F32), jax.ShapeDtypeStruct((T, D_MODEL), BF16)],
        name="in_proj", compiler_params=_params("parallel"),
    )(h0, g1, w_in)


def _scan_block_fwd(A, B, rowi):
    for d in (1, 2, 4):
        a_sh = pltpu.roll(A, d, axis=0)
        b_sh = pltpu.roll(B, d, axis=0)
        m = rowi >= d
        B = jnp.where(m, A * b_sh + B, B)
        A = jnp.where(m, A * a_sh, A)
    return A, B


def _scan_block_bwd(A, B, rowi):
    for d in (1, 2, 4):
        a_sh = pltpu.roll(A, 8 - d, axis=0)
        b_sh = pltpu.roll(B, 8 - d, axis=0)
        m = rowi < 8 - d
        B = jnp.where(m, A * b_sh + B, B)
        A = jnp.where(m, A * a_sh, A)
    return A, B


def _rg_gates(xc, w_ref, bg_ref, lam):
    pre = _dot(xc.astype(BF16), w_ref[...]) + bg_ref[...]
    r = _sigmoid(pre[:, :D_RG])
    ig = _sigmoid(pre[:, D_RG:])
    sp = _softplus_neg(lam)
    la = -LRU_C * sp * r
    a = jnp.exp(la)
    th = jnp.tanh(la)
    m = jnp.sqrt(-2.0 * th / (1.0 - th))
    return r, ig, sp, a, m


def _conv(ext, cw_ref, cb_ref, tm):
    xc = cb_ref[...] + cw_ref[0:1, :] * ext[8 - 3:8 - 3 + tm, :]
    for j in range(1, CONV_W):
        xc = xc + cw_ref[j:j + 1, :] * ext[8 - 3 + j:8 - 3 + j + tm, :]
    return xc


def _rg_fwd(p, cw, cb, wg, bg, lam, rg_g):
    T = p.shape[0]
    tm = _row_tile(T, 416)

    def body(xg_ref, cw_ref, cb_ref, w_ref, bg_ref, lam_ref, g_ref, y_ref, h_ref, ext, a_s, b_s, carry):
        i = pl.program_id(0)

        @pl.when(i == 0)
        def _():
            ext[0:8, :] = jnp.zeros((8, D_RG), F32)
            carry[...] = jnp.zeros((1, D_RG), F32)

        ext[8:8 + tm, :] = xg_ref[:, :D_RG]
        xc = _conv(ext, cw_ref, cb_ref, tm)
        r, ig, sp, a, m = _rg_gates(xc, w_ref, bg_ref, lam_ref[...])
        row = i * tm + lax.broadcasted_iota(jnp.int32, (tm, 1), 0)
        a_s[...] = a
        b_s[...] = jnp.where(row >= PAD, m * ig * xc, 0.0)
        rowi = lax.broadcasted_iota(jnp.int32, (8, D_RG), 0)

        def blk(j, c):
            o = pl.multiple_of(j * 8, 8)
            A, B = _scan_block_fwd(a_s[pl.ds(o, 8), :], b_s[pl.ds(o, 8), :], rowi)
            h = B + A * c
            h_ref[pl.ds(o, 8), :] = h
            return h[7:8, :]

        carry[...] = lax.fori_loop(0, tm // 8, blk, carry[...], unroll=4)
        ext[0:8, :] = ext[tm:tm + 8, :]
        g, _ = _gelu_parts(xg_ref[:, D_RG:])
        yy = g * h_ref[...]
        y_ref[...] = (yy * _rms(yy) * g_ref[...]).astype(BF16)

    vec = lambda n: pl.BlockSpec((1, n), lambda i: (0, 0))
    return pl.pallas_call(
        body, grid=(T // tm,),
        in_specs=[pl.BlockSpec((tm, 2 * D_RG), lambda i: (i, 0)),
                  pl.BlockSpec((CONV_W, D_RG), lambda i: (0, 0)), vec(D_RG),
                  pl.BlockSpec((D_RG, 2 * D_RG), lambda i: (0, 0)), vec(2 * D_RG), vec(D_RG), vec(D_RG)],
        out_specs=[pl.BlockSpec((tm, D_RG), lambda i: (i, 0)), pl.BlockSpec((tm, D_RG), lambda i: (i, 0))],
        out_shape=[jax.ShapeDtypeStruct((T, D_RG), BF16), jax.ShapeDtypeStruct((T, D_RG), F32)],
        scratch_shapes=[pltpu.VMEM((tm + 8, D_RG), F32), pltpu.VMEM((tm, D_RG), F32),
                        pltpu.VMEM((tm, D_RG), F32), pltpu.VMEM((1, D_RG), F32)],
        name="rg_fwd", compiler_params=_params("arbitrary"),
    )(p, cw, cb, wg, bg, lam, rg_g)


def _tri(lower):
    r = lax.broadcasted_iota(jnp.int32, (CHUNK, CHUNK), 0)
    c = lax.broadcasted_iota(jnp.int32, (CHUNK, CHUNK), 1)
    return ((c <= r) if lower else (c >= r)).astype(F32)


def _hg_gates(hq, hf, lbraw_ref, valid):
    lb = _sigmoid(lbraw_ref[0:1, :] - lbraw_ref[1:2, :])
    sq = _sigmoid(hq)
    q = hq * sq
    sf = _sigmoid(hf)
    f = lb + (1.0 - lb) * sf
    lf = jnp.where(valid, jnp.log(f), 0.0)
    b = jnp.dot(_tri(True), lf, precision=HIGHEST, preferred_element_type=F32)
    return lb, sq, q, sf, f, b


def _hg_head(qh, kh, bh):
    blk = lax.broadcasted_iota(jnp.int32, (CHUNK, 1), 0) // SUB
    b_last = bh[CHUNK - 1:CHUNK, :]
    refs = [bh[SUB * s:SUB * s + 1, :] for s in range(N_SUB)]
    r_sel = refs[N_SUB - 1]
    for s in range(N_SUB - 2, -1, -1):
        r_sel = jnp.where(blk == s, refs[s], r_sel)
    eb = jnp.exp(bh)
    eq = jnp.exp(bh - r_sel)
    ekh = jnp.exp(b_last - bh)
    ek = [jnp.exp(jnp.minimum(refs[s] - bh, EXP_CLAMP)) for s in range(N_SUB)]
    qe = qh * eq
    q_hat = jnp.concatenate([jnp.where(blk == s, qe, 0.0) for s in range(N_SUB)], axis=1)
    k_til = jnp.concatenate([kh * ek[s] for s in range(N_SUB)], axis=1)
    return blk, b_last, eb, eq, ekh, ek, q_hat, k_til


def _causal():
    r = lax.broadcasted_iota(jnp.int32, (CHUNK, CHUNK), 0)
    c = lax.broadcasted_iota(jnp.int32, (CHUNK, CHUNK), 1)
    return r >= c


def _chunks_per_step(n_chunks):
    for c in (5, 4, 3, 2):
        if n_chunks % c == 0:
            return c
    return 1


def _hg_fwd(p, lbraw, hg_g):
    T = p.shape[0]
    n_chunks = T // CHUNK
    cps = _chunks_per_step(n_chunks)
    rows = cps * CHUNK

    def body(hq_ref, hf_ref, hi_ref, hg_ref, lb_ref, g_ref, y_ref, o_ref, st_all_ref, st):
        i = pl.program_id(0)

        @pl.when(i == 0)
        def _():
            st[...] = jnp.zeros_like(st)

        def chunk(j, carry):
            rs = pl.ds(pl.multiple_of(j * CHUNK, CHUNK), CHUNK)
            chunk_body(i * cps + j, hq_ref.at[rs, :], hf_ref.at[rs, :], hi_ref.at[rs, :], hg_ref.at[rs, :], lb_ref,
                       g_ref, y_ref.at[rs, :], o_ref.at[rs, :], st_all_ref.at[pl.ds(j, 1)], st)
            return carry

        lax.fori_loop(0, cps, chunk, 0, unroll=True)

    def chunk_body(n, hq_ref, hf_ref, hi_ref, hg_ref, lb_ref, g_ref, y_ref, o_ref, st_all_ref, st):
        valid = (n * CHUNK + lax.broadcasted_iota(jnp.int32, (CHUNK, 1), 0)) >= PAD
        hq, hf, v, hg = hq_ref[...], hf_ref[...], hi_ref[...], hg_ref[...]
        lb, sq, q, sf, f, b = _hg_gates(hq, hf, lb_ref, valid)
        k = 1.0 - f
        st_all_ref[0] = st[...]
        causal = _causal()
        v_t = v.T.astype(BF16)
        heads = [slice(h * HG_HEAD_DIM, (h + 1) * HG_HEAD_DIM) for h in range(HG_HEADS)]
        fac = []
        for sl in heads:
            qh, kh, bh = q[:, sl], k[:, sl], b[:, sl]
            _, b_last, eb, _, ekh, _, q_hat, k_til = _hg_head(qh, kh, bh)
            fac.append((jnp.exp(b_last), (qh * eb).astype(BF16), q_hat.astype(BF16), k_til.astype(BF16),
                        (kh * ekh).astype(BF16), v[:, sl].astype(BF16)))
        raw = []
        for sl, (_, q_til, q_hat, k_til, k_hat, _) in zip(heads, fac):
            st_h = st[sl, :]
            raw.append((_dot_nt(q_til, st_h.astype(BF16)), _dot_nt(q_hat, k_til), _dot(v_t[sl, :], k_hat), st_h))
        for sl, (e_last, _, _, _, _, vb), (inter, att, upd, st_h) in zip(heads, fac, raw):
            o = inter + _dot(jnp.where(causal, att, 0.0).astype(BF16), vb)
            st[sl, :] = st_h * e_last + upd
            o_ref[:, sl] = o
            hgh = hg[:, sl]
            y_ref[:, sl] = (o * _rms(o) * g_ref[...] * (hgh * _sigmoid(hgh))).astype(BF16)

    col = lambda j: pl.BlockSpec((rows, D_HG), lambda n: (n, j))
    return pl.pallas_call(
        body, grid=(n_chunks // cps,),
        in_specs=[col(2), col(3), col(4), col(5),
                  pl.BlockSpec((2, D_HG), lambda n: (0, 0)), pl.BlockSpec((1, HG_HEAD_DIM), lambda n: (0, 0))],
        out_specs=[pl.BlockSpec((rows, D_HG), lambda n: (n, 0)), pl.BlockSpec((rows, D_HG), lambda n: (n, 0)),
                   pl.BlockSpec((cps, D_HG, HG_HEAD_DIM), lambda n: (n, 0, 0))],
        out_shape=[jax.ShapeDtypeStruct((T, D_HG), BF16), jax.ShapeDtypeStruct((T, D_HG), F32),
                   jax.ShapeDtypeStruct((n_chunks, D_HG, HG_HEAD_DIM), F32)],
        scratch_shapes=[pltpu.VMEM((D_HG, HG_HEAD_DIM), F32)],
        name="hg_fwd", compiler_params=_params("arbitrary"),
    )(p, p, p, p, lbraw, hg_g)


def _out_proj(h0, y_rg, y_hg, w_out, g2):
    T = h0.shape[0]
    tm = _row_tile(T, 832)

    def body(h_ref, yr_ref, yh_ref, w_ref, g_ref, h1_ref, v_ref, y_ref):
        y_ref[:, :D_RG] = yr_ref[...]
        y_ref[:, D_RG:] = yh_ref[...]
        h1 = h_ref[...] + _dot(y_ref[...], w_ref[...])
        h1_ref[...] = h1
        v_ref[...] = (h1 * _rms(h1) * g_ref[...]).astype(BF16)

    row = lambda n: pl.BlockSpec((tm, n), lambda i: (i, 0))
    return pl.pallas_call(
        body, grid=(T // tm,),
        in_specs=[row(D_MODEL), row(D_RG), row(D_HG), pl.BlockSpec((D_MODEL, D_MODEL), lambda i: (0, 0)),
                  pl.BlockSpec((1, D_MODEL), lambda i: (0, 0))],
        out_specs=[row(D_MODEL), row(D_MODEL), row(D_MODEL)],
        out_shape=[jax.ShapeDtypeStruct((T, D_MODEL), F32), jax.ShapeDtypeStruct((T, D_MODEL), BF16),
                   jax.ShapeDtypeStruct((T, D_MODEL), BF16)],
        name="out_proj", compiler_params=_params("parallel"),
    )(h0, y_rg, y_hg, w_out, g2)


def _gate_up(v, w_gu):
    T = v.shape[0]
    tm = _row_tile(T, 416)

    def body(v_ref, w_ref, gu_ref, act_ref):
        gu = _dot(v_ref[...], w_ref[...])
        gu_ref[...] = gu.astype(BF16)
        g = gu[:, :D_FF]
        act_ref[...] = (g * _sigmoid_fast(g) * gu[:, D_FF:]).astype(BF16)

    row = lambda n: pl.BlockSpec((tm, n), lambda i: (i, 0))
    return pl.pallas_call(
        body, grid=(T // tm,),
        in_specs=[row(D_MODEL), pl.BlockSpec((D_MODEL, 2 * D_FF), lambda i: (0, 0))],
        out_specs=[row(2 * D_FF), row(D_FF)],
        out_shape=[jax.ShapeDtypeStruct((T, 2 * D_FF), BF16), jax.ShapeDtypeStruct((T, D_FF), BF16)],
        name="gate_up", compiler_params=_params("parallel"),
    )(v, w_gu)


def _down_loss(h1, act, w_down, gf, target):
    T = h1.shape[0]
    tm = _row_tile(T, 832)

    def body(h_ref, a_ref, w_ref, g_ref, t_ref, dh2_ref, dh2b_ref, loss_ref, gg_ref):
        i = pl.program_id(0)

        @pl.when(i == 0)
        def _():
            loss_ref[...] = jnp.zeros_like(loss_ref)
            gg_ref[...] = jnp.zeros_like(gg_ref)

        h2 = h_ref[...] + _dot(a_ref[...], w_ref[...])
        r = _rms(h2)
        n = h2 * r
        gf_ = g_ref[...]
        row = i * tm + lax.broadcasted_iota(jnp.int32, (tm, 1), 0)
        err = jnp.where(row >= PAD + N_META, n * gf_ - t_ref[...], 0.0)
        loss_ref[...] += 0.5 * jnp.sum(jnp.mean(err * err, axis=-1, keepdims=True), axis=0, keepdims=True)
        dy = err * (1.0 / D_MODEL)
        gg_ref[...] += jnp.sum(dy * n, axis=0, keepdims=True)
        dh2 = _rms_bwd(dy * gf_, n, r)
        dh2_ref[...] = dh2
        dh2b_ref[...] = dh2.astype(BF16)

    row_spec = lambda n: pl.BlockSpec((tm, n), lambda i: (i, 0))
    return pl.pallas_call(
        body, grid=(T // tm,),
        in_specs=[row_spec(D_MODEL), row_spec(D_FF), pl.BlockSpec((D_FF, D_MODEL), lambda i: (0, 0)),
                  pl.BlockSpec((1, D_MODEL), lambda i: (0, 0)), row_spec(D_MODEL)],
        out_specs=[row_spec(D_MODEL), row_spec(D_MODEL), pl.BlockSpec((1, 1), lambda i: (0, 0)),
                   pl.BlockSpec((1, D_MODEL), lambda i: (0, 0))],
        out_shape=[jax.ShapeDtypeStruct((T, D_MODEL), F32), jax.ShapeDtypeStruct((T, D_MODEL), BF16),
                   jax.ShapeDtypeStruct((1, 1), F32), jax.ShapeDtypeStruct((1, D_MODEL), F32)],
        name="down_loss", compiler_params=_params("arbitrary"),
    )(h1, act, w_down, gf, target)


def _ffn_bwd_act(dh2b, gu, w_down):
    T = dh2b.shape[0]
    tm = _row_tile(T, 416)

    def body(d_ref, gu_ref, w_ref, dgu_ref):
        dact = _dot_nt(d_ref[...], w_ref[...])
        g = gu_ref[:, :D_FF].astype(F32)
        u = gu_ref[:, D_FF:].astype(F32)
        s = _sigmoid_fast(g)
        dgu_ref[:, :D_FF] = (dact * u * s * (1.0 + g * (1.0 - s))).astype(BF16)
        dgu_ref[:, D_FF:] = (dact * g * s).astype(BF16)

    row = lambda n: pl.BlockSpec((tm, n), lambda i: (i, 0))
    return pl.pallas_call(
        body, grid=(T // tm,),
        in_specs=[row(D_MODEL), row(2 * D_FF), pl.BlockSpec((D_FF, D_MODEL), lambda i: (0, 0))],
        out_specs=row(2 * D_FF),
        out_shape=jax.ShapeDtypeStruct((T, 2 * D_FF), BF16),
        name="ffn_bwd_act", compiler_params=_params("parallel"),
    )(dh2b, gu, w_down)


def _ffn_bwd_in(dgu, w_gu, h1, g2, dh2, w_out):
    T = h1.shape[0]
    tm = _row_tile(T, 416)

    def body(dgu_ref, wgu_ref, h_ref, g_ref, d2_ref, wo_ref, dh1_ref, dh1b_ref, dy_ref, gg_ref):
        i = pl.program_id(0)

        @pl.when(i == 0)
        def _():
            gg_ref[...] = jnp.zeros_like(gg_ref)

        dv = _dot_nt(dgu_ref[...], wgu_ref[...])
        h1 = h_ref[...]
        r = _rms(h1)
        n = h1 * r
        gg_ref[...] += jnp.sum(dv * n, axis=0, keepdims=True)
        dh1 = d2_ref[...] + _rms_bwd(dv * g_ref[...], n, r)
        dh1_ref[...] = dh1
        db = dh1.astype(BF16)
        dh1b_ref[...] = db
        dy_ref[...] = _dot_nt(db, wo_ref[...])

    row = lambda n: pl.BlockSpec((tm, n), lambda i: (i, 0))
    return pl.pallas_call(
        body, grid=(T // tm,),
        in_specs=[row(2 * D_FF), pl.BlockSpec((D_MODEL, 2 * D_FF), lambda i: (0, 0)),
                  row(D_MODEL), pl.BlockSpec((1, D_MODEL), lambda i: (0, 0)), row(D_MODEL),
                  pl.BlockSpec((D_MODEL, D_MODEL), lambda i: (0, 0))],
        out_specs=[row(D_MODEL), row(D_MODEL), row(D_MODEL), pl.BlockSpec((1, D_MODEL), lambda i: (0, 0))],
        out_shape=[jax.ShapeDtypeStruct((T, D_MODEL), F32), jax.ShapeDtypeStruct((T, D_MODEL), BF16),
                   jax.ShapeDtypeStruct((T, D_MODEL), F32), jax.ShapeDtypeStruct((1, D_MODEL), F32)],
        name="ffn_bwd_in", compiler_params=_params("arbitrary"),
    )(dgu, w_gu, h1, g2, dh2, w_out)


def _rg_bwd(p, hs, dy, dp, cw, cb, wg, bg, lam, rg_g):
    T = p.shape[0]
    tm = _row_tile(T, 208)
    nt = T // tm
    hb = tm // 8

    def body(xg_ref, xh_ref, h_ref, hh_ref, dy_ref, dp_in_ref, cw_ref, cb_ref, w_ref, bg_ref, lam_ref, g_ref,
             dp_ref, gcw_ref, gcb_ref, gw_ref, gbg_ref, glam_ref, gg_ref,
             ext, dext, a_s, b_s, d_s, gacc, carry_d, carry_a):
        i = pl.program_id(0)
        t_idx = nt - 1 - i

        @pl.when(i == 0)
        def _():
            dext[tm:tm + 8, :] = jnp.zeros((8, D_RG), F32)
            carry_d[...] = jnp.zeros_like(carry_d)
            carry_a[...] = jnp.zeros_like(carry_a)
            gacc[...] = jnp.zeros_like(gacc)
            for ref in (gcw_ref, gcb_ref, gbg_ref, glam_ref, gg_ref, gw_ref):
                ref[...] = jnp.zeros_like(ref)

        first = t_idx == 0
        ext[0:8, :] = jnp.where(first, 0.0, xh_ref[:, :D_RG])
        ext[8:8 + tm, :] = xg_ref[:, :D_RG]
        xc = _conv(ext, cw_ref, cb_ref, tm)
        lam_ = lam_ref[...]
        r, ig, sp, a, m = _rg_gates(xc, w_ref, bg_ref, lam_)
        row = t_idx * tm + lax.broadcasted_iota(jnp.int32, (tm, 1), 0)
        valid = row >= PAD

        gr = xg_ref[:, D_RG:]
        g, dgelu = _gelu_parts(gr)
        h = h_ref[...]
        yy = g * h
        rr = _rms(yy)
        nn = yy * rr
        dy_ = dy_ref[...]
        gg_ref[...] += jnp.sum(dy_ * nn, axis=0, keepdims=True)
        dyy = _rms_bwd(dy_ * g_ref[...], nn, rr)
        dp_ref[:, D_RG:] = (dyy * h * dgelu).astype(BF16)

        a_s[...] = a
        b_s[...] = dyy * g
        rowi = lax.broadcasted_iota(jnp.int32, (8, D_RG), 0)

        def blk(jj, c):
            cd, ca = c
            o = pl.multiple_of((hb - 1 - jj) * 8, 8)
            a_blk = a_s[pl.ds(o, 8), :]
            a_next = jnp.where(rowi == 7, ca, pltpu.roll(a_blk, 7, axis=0))
            A, B = _scan_block_bwd(a_next, b_s[pl.ds(o, 8), :], rowi)
            d = B + A * cd
            d_s[pl.ds(o, 8), :] = d
            return d[0:1, :], a_blk[0:1, :]

        cd, ca = lax.fori_loop(0, hb, blk, (carry_d[...], carry_a[...]), unroll=2)
        carry_d[...] = cd
        carry_a[...] = ca
        delta = d_s[...]

        h_last_prev = jnp.where(first, 0.0, hh_ref[7:8, :])
        row0 = lax.broadcasted_iota(jnp.int32, (tm, 1), 0) == 0
        h_prev = jnp.where(row0, h_last_prev, pltpu.roll(h, 1, axis=0))
        dbx = jnp.where(valid, delta, 0.0)
        da = delta * h_prev
        di = dbx * m * xc
        dm = dbx * ig * xc
        dla = a * (da - dm * a / m)
        dla = jnp.where(valid, dla, 0.0)
        glam_ref[...] += jnp.sum(dla * r, axis=0, keepdims=True) * (LRU_C * _sigmoid(-lam_))
        dr = (-LRU_C) * sp * dla
        dpre = jnp.concatenate([dr * r * (1.0 - r), di * ig * (1.0 - ig)], axis=1)
        gbg_ref[...] += jnp.sum(dpre, axis=0, keepdims=True)
        dpre_b = dpre.astype(BF16)
        gacc[...] += _dot_tn(xc.astype(BF16), dpre_b)
        dxc = dbx * m * ig + _dot_nt(dpre_b, w_ref[...])
        gcb_ref[...] += jnp.sum(dxc, axis=0, keepdims=True)
        for j in range(CONV_W):
            gcw_ref[j:j + 1, :] += jnp.sum(dxc * ext[8 - 3 + j:8 - 3 + j + tm, :], axis=0, keepdims=True)
        dext[0:tm, :] = dxc
        dxr = cw_ref[0:1, :] * dext[3:3 + tm, :]
        for j in range(1, CONV_W):
            dxr = dxr + cw_ref[j:j + 1, :] * dext[3 - j:3 - j + tm, :]
        dp_ref[:, :D_RG] = dxr.astype(BF16)
        dext[tm:tm + 8, :] = dext[0:8, :]

        @pl.when(i == nt - 1)
        def _():
            fold = _head_fold()
            mask = _head_mask()
            for k in range(2):
                blockdiag = jnp.where(mask, gacc[:, k * D_RG:(k + 1) * D_RG], 0.0)
                gw_ref[k * D_RG:(k + 1) * D_RG, :] = jnp.dot(blockdiag, fold, precision=HIGHEST,
                                                             preferred_element_type=F32)

    vec = lambda n: pl.BlockSpec((1, n), lambda i: (0, 0))
    rev = lambda n: pl.BlockSpec((tm, n), lambda i: (nt - 1 - i, 0))
    halo = lambda n: pl.BlockSpec((8, n), lambda i: (jnp.maximum((nt - 1 - i) * hb - 1, 0), 0))
    return pl.pallas_call(
        body, grid=(nt,),
        in_specs=[rev(2 * D_RG), halo(2 * D_RG), rev(D_RG), halo(D_RG), rev(D_RG), ANY,
                  pl.BlockSpec((CONV_W, D_RG), lambda i: (0, 0)), vec(D_RG),
                  pl.BlockSpec((D_RG, 2 * D_RG), lambda i: (0, 0)), vec(2 * D_RG), vec(D_RG), vec(D_RG)],
        out_specs=[rev(2 * D_RG), pl.BlockSpec((CONV_W, D_RG), lambda i: (0, 0)), vec(D_RG),
                   pl.BlockSpec((2 * D_RG, RG_HEAD_DIM), lambda i: (0, 0)), vec(2 * D_RG), vec(D_RG), vec(D_RG)],
        input_output_aliases={5: 0},
        out_shape=[jax.ShapeDtypeStruct((T, D_IN), BF16), jax.ShapeDtypeStruct((CONV_W, D_RG), F32),
                   jax.ShapeDtypeStruct((1, D_RG), F32), jax.ShapeDtypeStruct((2 * D_RG, RG_HEAD_DIM), F32),
                   jax.ShapeDtypeStruct((1, 2 * D_RG), F32), jax.ShapeDtypeStruct((1, D_RG), F32),
                   jax.ShapeDtypeStruct((1, D_RG), F32)],
        scratch_shapes=[pltpu.VMEM((tm + 8, D_RG), F32), pltpu.VMEM((tm + 8, D_RG), F32),
                        pltpu.VMEM((tm, D_RG), F32), pltpu.VMEM((tm, D_RG), F32), pltpu.VMEM((tm, D_RG), F32),
                        pltpu.VMEM((D_RG, 2 * D_RG), F32), pltpu.VMEM((1, D_RG), F32), pltpu.VMEM((1, D_RG), F32)],
        name="rg_bwd", compiler_params=_params("arbitrary"),
    )(p, p, hs, hs, dy, dp, cw, cb, wg, bg, lam, rg_g)


def _hg_bwd(p, o_all, st_all, dy, lbraw, hg_g):
    T = p.shape[0]
    n_chunks = T // CHUNK
    cps = _chunks_per_step(n_chunks)
    rows = cps * CHUNK
    n_steps = n_chunks // cps

    def body(hq_ref, hf_ref, hi_ref, hg_ref, o_ref, st_ref, dy_ref, lb_ref, g_ref,
             dp_ref, glb_ref, gg_ref, dst):
        i = pl.program_id(0)

        @pl.when(i == 0)
        def _():
            dst[...] = jnp.zeros_like(dst)
            glb_ref[...] = jnp.zeros_like(glb_ref)
            gg_ref[...] = jnp.zeros_like(gg_ref)

        dp_ref[:, :2 * D_RG] = jnp.zeros((rows, 2 * D_RG), BF16)

        def chunk(jj, carry):
            j = cps - 1 - jj
            rs = pl.ds(pl.multiple_of(j * CHUNK, CHUNK), CHUNK)
            chunk_body((n_steps - 1 - i) * cps + j, hq_ref.at[rs, :], hf_ref.at[rs, :], hi_ref.at[rs, :],
                       hg_ref.at[rs, :], o_ref.at[rs, :], st_ref.at[pl.ds(j, 1)], dy_ref.at[rs, :], lb_ref, g_ref,
                       dp_ref.at[rs, pl.ds(2 * D_RG, 4 * D_HG)], glb_ref, gg_ref, dst)
            return carry

        lax.fori_loop(0, cps, chunk, 0, unroll=True)

    def chunk_body(n, hq_ref, hf_ref, hi_ref, hg_ref, o_ref, st_ref, dy_ref, lb_ref, g_ref,
                   dp_ref, glb_ref, gg_ref, dst):
        valid = (n * CHUNK + lax.broadcasted_iota(jnp.int32, (CHUNK, 1), 0)) >= PAD
        hq, hf, v, hg = hq_ref[...], hf_ref[...], hi_ref[...], hg_ref[...]
        lb, sq, q, sf, f, b = _hg_gates(hq, hf, lb_ref, valid)
        k = 1.0 - f
        causal = _causal()
        r_i = lax.broadcasted_iota(jnp.int32, (CHUNK, CHUNK), 0)
        c_i = lax.broadcasted_iota(jnp.int32, (CHUNK, CHUNK), 1)
        causal_t = r_i <= c_i
        is_last = lax.broadcasted_iota(jnp.int32, (CHUNK, 1), 0) == CHUNK - 1
        g_ = g_ref[...]
        db_parts, dq_parts, dk_parts = [], [], []
        gg = jnp.zeros((1, HG_HEAD_DIM), F32)
        heads = [slice(h * HG_HEAD_DIM, (h + 1) * HG_HEAD_DIM) for h in range(HG_HEADS)]

        do_parts = []
        for h, sl in enumerate(heads):
            o = o_ref[:, sl]
            ro = _rms(o)
            no = o * ro
            hgh = hg[:, sl]
            sg = _sigmoid(hgh)
            dyh = dy_ref[:, sl]
            dp_ref[:, 3 * D_HG + h * HG_HEAD_DIM:3 * D_HG + (h + 1) * HG_HEAD_DIM] = (
                dyh * no * g_ * sg * (1.0 + hgh * (1.0 - sg))).astype(BF16)
            dng = dyh * hgh * sg
            gg = gg + jnp.sum(dng * no, axis=0, keepdims=True)
            do_parts.append(_rms_bwd(dng * g_, no, ro))
        do_t = jnp.concatenate(do_parts, axis=1).T.astype(BF16)

        fac = []
        for sl, do in zip(heads, do_parts):
            qh, kh, bh = q[:, sl], k[:, sl], b[:, sl]
            blk, b_last, eb, eq, ekh, ek, q_hat, k_til = _hg_head(qh, kh, bh)
            fac.append(dict(qh=qh, kh=kh, blk=blk, e_last=jnp.exp(b_last), eb=eb, eq=eq, ekh=ekh, ek=ek,
                            q_til=qh * eb, k_hat=kh * ekh, qhb=q_hat.astype(BF16), ktb=k_til.astype(BF16),
                            vb=v[:, sl].astype(BF16), dob=do.astype(BF16)))

        first = []
        for sl, t in zip(heads, fac):
            st_h = st_ref[0, sl, :]
            dst_h = dst[sl, :]
            dstb = dst_h.astype(BF16)
            first.append(dict(
                att_t=_dot_nt(t["ktb"], t["qhb"]), datt=_dot_nt(t["dob"], t["vb"]),
                datt_t=_dot_nt(t["vb"], t["dob"]), dk_hat=_dot(t["vb"], dstb),
                dv=_dot_nt(t["k_hat"].astype(BF16), dstb), dq_til=_dot(t["dob"], st_h.astype(BF16)),
                state=t["e_last"] * jnp.sum(dst_h * st_h, axis=0, keepdims=True)))
            dst[sl, :] = dst_h * t["e_last"] + _dot(do_t[sl, :], t["q_til"].astype(BF16))

        for h, (t, m) in enumerate(zip(fac, first)):
            qh, kh, blk, eb, eq, ekh, ek = t["qh"], t["kh"], t["blk"], t["eb"], t["eq"], t["ekh"], t["ek"]
            q_til, k_hat, qhb, ktb, dob = t["q_til"], t["k_hat"], t["qhb"], t["ktb"], t["dob"]
            dk_hat, dq_til = m["dk_hat"], m["dq_til"]
            dv = m["dv"] + _dot(jnp.where(causal_t, m["att_t"], 0.0).astype(BF16), dob)
            dq_hat = _dot(jnp.where(causal, m["datt"], 0.0).astype(BF16), ktb)
            dk_til = _dot(jnp.where(causal_t, m["datt_t"], 0.0).astype(BF16), qhb)
            db_last = jnp.sum(dk_hat * k_hat, axis=0, keepdims=True) + m["state"]
            dq_sel = dq_hat[:, (N_SUB - 1) * HG_HEAD_DIM:]
            for s in range(N_SUB - 2, -1, -1):
                dq_sel = jnp.where(blk == s, dq_hat[:, s * HG_HEAD_DIM:(s + 1) * HG_HEAD_DIM], dq_sel)
            dq_a = dq_sel * eq
            dk_a = dk_til[:, :HG_HEAD_DIM] * ek[0]
            for s in range(1, N_SUB):
                dk_a = dk_a + dk_til[:, s * HG_HEAD_DIM:(s + 1) * HG_HEAD_DIM] * ek[s]
            db_att = qhb.astype(F32) * dq_hat - ktb.astype(F32) * dk_til
            db = dq_til * q_til - dk_hat * k_hat
            for s in range(N_SUB):
                db = db + db_att[:, s * HG_HEAD_DIM:(s + 1) * HG_HEAD_DIM]
            db_parts.append(jnp.where(is_last, db + db_last, db))
            dq_parts.append(dq_til * eb + dq_a)
            dk_parts.append(dk_hat * ekh + dk_a)
            dp_ref[:, 2 * D_HG + h * HG_HEAD_DIM:2 * D_HG + (h + 1) * HG_HEAD_DIM] = dv.astype(BF16)

        gg_ref[...] += gg
        db = jnp.concatenate(db_parts, axis=1)
        dq = jnp.concatenate(dq_parts, axis=1)
        dk = jnp.concatenate(dk_parts, axis=1)
        dlf = jnp.where(valid, jnp.dot(_tri(False), db, precision=HIGHEST, preferred_element_type=F32), 0.0)
        dp_ref[:, :D_HG] = (dq * sq * (1.0 + hq * (1.0 - sq))).astype(BF16)
        df = dlf / f - dk
        dlb = jnp.sum(df * (1.0 - sf), axis=0, keepdims=True) * lb * (1.0 - lb)
        glb_ref[0:1, :] += dlb
        glb_ref[1:2, :] += -dlb
        dp_ref[:, D_HG:2 * D_HG] = (df * (1.0 - lb) * sf * (1.0 - sf)).astype(BF16)

    rev = lambda j: pl.BlockSpec((rows, D_HG), lambda i: (n_steps - 1 - i, j))
    return pl.pallas_call(
        body, grid=(n_steps,),
        in_specs=[rev(2), rev(3), rev(4), rev(5), rev(0),
                  pl.BlockSpec((cps, D_HG, HG_HEAD_DIM), lambda i: (n_steps - 1 - i, 0, 0)), rev(1),
                  pl.BlockSpec((2, D_HG), lambda i: (0, 0)), pl.BlockSpec((1, HG_HEAD_DIM), lambda i: (0, 0))],
        out_specs=[pl.BlockSpec((rows, D_IN), lambda i: (n_steps - 1 - i, 0)),
                   pl.BlockSpec((2, D_HG), lambda i: (0, 0)), pl.BlockSpec((1, HG_HEAD_DIM), lambda i: (0, 0))],
        out_shape=[jax.ShapeDtypeStruct((T, D_IN), BF16), jax.ShapeDtypeStruct((2, D_HG), F32),
                   jax.ShapeDtypeStruct((1, HG_HEAD_DIM), F32)],
        scratch_shapes=[pltpu.VMEM((D_HG, HG_HEAD_DIM), F32)],
        name="hg_bwd", compiler_params=_params("arbitrary"),
    )(p, p, p, p, o_all, st_all, dy, lbraw, hg_g)


def _in_bwd(dp, w_in, h0, g1, dh1):
    T = h0.shape[0]
    tm = _row_tile(T, 416)

    def body(dp_ref, w_ref, h_ref, g_ref, d1_ref, dh0_ref, gg_ref):
        i = pl.program_id(0)

        @pl.when(i == 0)
        def _():
            gg_ref[...] = jnp.zeros_like(gg_ref)

        du = _dot_nt(dp_ref[...], w_ref[...])
        h0_ = h_ref[...]
        r = _rms(h0_)
        n = h0_ * r
        gg_ref[...] += jnp.sum(du * n, axis=0, keepdims=True)
        dh0_ref[...] = d1_ref[...] + _rms_bwd(du * g_ref[...], n, r)

    row = lambda n: pl.BlockSpec((tm, n), lambda i: (i, 0))
    return pl.pallas_call(
        body, grid=(T // tm,),
        in_specs=[row(D_IN), pl.BlockSpec((D_MODEL, D_IN), lambda i: (0, 0)),
                  row(D_MODEL), pl.BlockSpec((1, D_MODEL), lambda i: (0, 0)), row(D_MODEL)],
        out_specs=[row(D_MODEL), pl.BlockSpec((1, D_MODEL), lambda i: (0, 0))],
        out_shape=[jax.ShapeDtypeStruct((T, D_MODEL), F32), jax.ShapeDtypeStruct((1, D_MODEL), F32)],
        name="in_bwd", compiler_params=_params("arbitrary"),
    )(dp, w_in, h0, g1, dh1)


def _col_tile(cols, target):
    best = None
    for t in range(128, min(cols, target) + 1, 128):
        if cols % t == 0:
            best = t
    assert best is not None, cols
    return best


MXU_DIM = 256


def _mxu_tile(cols, target):
    best = None
    for t in range(MXU_DIM, min(cols, target) + 1, MXU_DIM):
        if cols % t == 0:
            best = t
    assert best is not None, cols
    return best


def _weight_grad(a, b, name):
    T, M = a.shape
    N = b.shape[1]
    tm = _col_tile(M, 1408)
    tn = _mxu_tile(N, 768 if tm <= 1024 else 512)

    def body(a_ref, b_ref, o_ref):
        o_ref[...] = _dot_tn(a_ref[...], b_ref[...])

    return pl.pallas_call(
        body, grid=(M // tm, N // tn),
        in_specs=[pl.BlockSpec((T, tm), lambda m, n: (0, m)), pl.BlockSpec((T, tn), lambda m, n: (0, n))],
        out_specs=pl.BlockSpec((tm, tn), lambda m, n: (m, n)),
        out_shape=jax.ShapeDtypeStruct((M, N), F32),
        name=name, compiler_params=_params("parallel", "parallel"),
    )(a, b)


def _local_step(h0, target, w_in, w_out, w_gu, w_down, small, on_ffn_grads=None, on_mixer_grads=None):
    wg = _gate_weights(small["w_rgate"], small["w_igate"])
    bg = jnp.concatenate([small["b_rgate"], small["b_igate"]], axis=1)

    p, u = _in_proj(h0, small["mix_norm_g"], w_in)
    y_rg, hs = _rg_fwd(p, small["conv_w"], small["conv_b"], wg, bg, small["lru_lambda"], small["rg_norm_g"])
    y_hg, o_all, st_all = _hg_fwd(p, small["hg_lower_bound"], small["hg_norm_g"])
    h1, v, yb = _out_proj(h0, y_rg, y_hg, w_out, small["ffn_norm_g"])
    gu, act = _gate_up(v, w_gu)
    dh2, dh2b, loss, g_final = _down_loss(h1, act, w_down, small["final_norm_g"], target)

    dgu = _ffn_bwd_act(dh2b, gu, w_down)
    ffn_grads = {"w_gate_up": _weight_grad(v, dgu, "grad_w_gate_up"),
                 "w_down": _weight_grad(act, dh2b, "grad_w_down")}
    stages = on_ffn_grads(ffn_grads) if on_ffn_grads is not None else None
    dh1, dh1b, dy, g_ffn = _ffn_bwd_in(dgu, w_gu, h1, small["ffn_norm_g"], dh2, w_out)
    early = late = None
    if stages is not None:
        chip_sums, send = stages
        sums = chip_sums()
        (dh1, dh1b, dy), sums = lax.optimization_barrier(((dh1, dh1b, dy), sums))
        early = send(sums)
    dp, g_lb, g_hgn = _hg_bwd(p, o_all, st_all, dy, small["hg_lower_bound"], small["hg_norm_g"])
    dp, g_cw, g_cb, g_wgate, g_bg, g_lam, g_rgn = _rg_bwd(
        p, hs, dy, dp, small["conv_w"], small["conv_b"], wg, bg, small["lru_lambda"], small["rg_norm_g"])
    mixer_grads = {"w_in": _weight_grad(u, dp, "grad_w_in"), "w_out": _weight_grad(yb, dh1b, "grad_w_out")}
    if on_mixer_grads is not None:
        chip_sums, send = on_mixer_grads(mixer_grads)
        sums = chip_sums()
        (dp, dh1), sums = lax.optimization_barrier(((dp, dh1), sums))
        late = send(sums)
    dh0, g_mix = _in_bwd(dp, w_in, h0, small["mix_norm_g"], dh1)

    grads = {
        "w_in": mixer_grads["w_in"], "w_out": mixer_grads["w_out"],
        "w_gate_up": ffn_grads["w_gate_up"], "w_down": ffn_grads["w_down"],
        "mix_norm_g": g_mix, "conv_w": g_cw, "conv_b": g_cb, "w_gates": g_wgate,
        "b_rgate": g_bg[:, :D_RG], "b_igate": g_bg[:, D_RG:], "lru_lambda": g_lam, "rg_norm_g": g_rgn,
        "hg_lower_bound": g_lb, "hg_norm_g": g_hgn, "ffn_norm_g": g_ffn, "final_norm_g": g_final,
    }
    return loss, dh0, grads, early, late


ANY = pl.BlockSpec(memory_space=pl.ANY)
HALF = D_MODEL // 2

BIG = {"w_in": (D_MODEL, D_IN // N_CHIPS, True), "w_gate_up": (D_MODEL, 2 * D_FF // N_CHIPS, True),
       "w_out": (D_MODEL // N_CHIPS, D_MODEL, False), "w_down": (D_FF // N_CHIPS, D_MODEL, False)}
BIG_NAMES = tuple(BIG)
N_BIG = len(BIG_NAMES)


def _full_shape(name):
    rows, cols, by_col = BIG[name]
    return (rows, cols * N_CHIPS) if by_col else (rows * N_CHIPS, cols)


def _place():
    return lax.axis_index("x"), lax.axis_index("y"), lax.axis_index("c")


def _chip_of(x, y, r):
    fx, fy = (r + 1) >> 1, (r + 1) & 1
    return (1 - x if fx else x), (1 - y if fy else y)


def _half_of(ref, by_col, half):
    start = pl.multiple_of(half * HALF, 128)
    return ref.at[pl.ds(start, HALF), :] if by_col else ref.at[:, pl.ds(start, HALF)]


def _shard_of(ref, name, chip):
    rows, cols, by_col = BIG[name]
    if by_col:
        return ref.at[:, pl.ds(pl.multiple_of(chip * cols, 128), cols)]
    return ref.at[pl.ds(pl.multiple_of(chip * rows, 16), rows), :]


def _shard_half_of(ref, name, chip, half):
    rows, cols, by_col = BIG[name]
    start = pl.multiple_of(half * HALF, 128)
    if by_col:
        return ref.at[pl.ds(start, HALF), pl.ds(pl.multiple_of(chip * cols, 128), cols)]
    return ref.at[pl.ds(pl.multiple_of(chip * rows, 16), rows), pl.ds(start, HALF)]


def _remote(src, dst, send_sems, recv_sems, k, dev):
    return pltpu.make_async_remote_copy(src_ref=src, dst_ref=dst, send_sem=send_sems.at[k], recv_sem=recv_sems.at[k],
                                        device_id=dev, device_id_type=MESH)


def _cast_into_full(w_shard, name, chip):
    rows, cols, by_col = BIG[name]
    tr = _row_tile(rows, 352)
    if by_col:
        out_spec = pl.BlockSpec((tr, cols), lambda i, s: (i, s[0]))
    else:
        out_spec = pl.BlockSpec((tr, cols), lambda i, s: (s[0] * (rows // tr) + i, 0))

    def body(s_ref, w_ref, o_ref):
        o_ref[...] = w_ref[...].astype(BF16)

    return pl.pallas_call(
        body,
        grid_spec=pltpu.PrefetchScalarGridSpec(
            num_scalar_prefetch=1, grid=(rows // tr,), in_specs=[pl.BlockSpec((tr, cols), lambda i, s: (i, 0))],
            out_specs=out_spec),
        out_shape=jax.ShapeDtypeStruct(_full_shape(name), BF16),
        name="cast_" + name, compiler_params=_params("parallel"),
    )(chip, w_shard)


def _gather_weights(placed, small, names, label, collective_id):
    n, ns = len(names), len(small)
    hbm = pltpu.MemorySpace.HBM
    outs = [jax.new_ref(placed[nm], memory_space=hbm) for nm in names]
    small_in = [jax.new_ref(s, memory_space=hbm) for s in small]
    small_out = [jax.empty_ref(jax.ShapeDtypeStruct((s.shape[0], s.shape[1] * N_CHIPS), F32), memory_space=hbm)
                 for s in small]
    n_sems = 6 * n + 3 * ns

    @pl.kernel(mesh=plsc.ScalarSubcoreMesh(axis_name="seq", num_cores=1), name=label, out_type=(),
               scratch_types=(pltpu.SemaphoreType.DMA((n_sems,)), pltpu.SemaphoreType.DMA((n_sems,)),
                              pltpu.SemaphoreType.DMA((max(ns, 1),))),
               compiler_params=pltpu.CompilerParams(collective_id=collective_id))
    def launch(send_sems, recv_sems, local_sems):
        x, y, c = _place()
        chip = 2 * x + y
        sibling = (x, y, 1 - c)
        others = [_chip_of(x, y, r) for r in range(3)]
        _handshake([(qx, qy, c) for qx, qy in others] + [sibling])

        def small_block(a, q):
            cols = small[a].shape[1]
            return small_out[a].at[:, pl.ds(pl.multiple_of(q * cols, 128), cols)]

        local = [pltpu.make_async_copy(small_in[a], small_block(a, chip), local_sems.at[a]) for a in range(ns)]
        for cp in local:
            cp.start()

        sends = []
        for a, name in enumerate(names):
            mine = _shard_half_of(outs[a], name, chip, c)
            for r, (qx, qy) in enumerate(others):
                sends.append(_remote(mine, mine, send_sems, recv_sems, 6 * a + r, (qx, qy, c)))
        for a in range(ns):
            for r, (qx, qy) in enumerate(others):
                sends.append(_remote(small_in[a], small_block(a, chip), send_sems, recv_sems,
                                     6 * n + 3 * a + r, (qx, qy, c)))
        for cp in sends:
            cp.start()

        forwards = []
        for a, name in enumerate(names):
            for r, (qx, qy) in enumerate(others):
                landed = _shard_half_of(outs[a], name, 2 * qx + qy, c)
                _remote(landed, landed, send_sems, recv_sems, 6 * a + r, (qx, qy, c)).wait_recv()
                fwd = _remote(landed, landed, send_sems, recv_sems, 6 * a + 3 + r, sibling)
                fwd.start()
                forwards.append(fwd)
        for a in range(ns):
            for r, (qx, qy) in enumerate(others):
                landed = small_block(a, 2 * qx + qy)
                _remote(landed, landed, send_sems, recv_sems, 6 * n + 3 * a + r, (qx, qy, c)).wait_recv()
        for a, name in enumerate(names):
            for r, (qx, qy) in enumerate(others):
                landed = _shard_half_of(outs[a], name, 2 * qx + qy, 1 - c)
                _remote(landed, landed, send_sems, recv_sems, 6 * a + 3 + r, sibling).wait_recv()
        for cp in sends + forwards:
            cp.wait_send()
        for cp in local:
            cp.wait()

    launch()
    return {nm: ref[...] for nm, ref in zip(names, outs)}, [ref[...] for ref in small_out]


def _exchange_halves(grads, names, label, collective_id):
    n = len(names)
    sequencer = collective_id is not None

    def body(*refs):
        ins, outs = refs[:n], refs[n:2 * n]
        send_sems, recv_sems = refs[2 * n:]
        x, y, c = _place()
        if sequencer:
            _handshake([(x, y, 1 - c)])
        copies = []
        for a, name in enumerate(names):
            copies.append(_remote(_half_of(ins[a], BIG[name][2], 1 - c), outs[a], send_sems, recv_sems, a,
                                  (x, y, 1 - c)))
        for cp in copies:
            cp.start()
        for cp in copies:
            cp.wait()

    def half_shape(name):
        r, c_ = _full_shape(name)
        return (HALF, c_) if BIG[name][2] else (r, HALF)

    out_type = tuple(jax.ShapeDtypeStruct(half_shape(nm), F32) for nm in names)
    sems = (pltpu.SemaphoreType.DMA((n,)), pltpu.SemaphoreType.DMA((n,)))
    operands = [grads[nm] for nm in names]
    if sequencer:
        got = pl.kernel(
            body, mesh=plsc.ScalarSubcoreMesh(axis_name="seq", num_cores=1), name=label, out_type=out_type,
            scratch_types=sems, compiler_params=pltpu.CompilerParams(collective_id=collective_id),
        )(*operands)
    else:
        got = pl.pallas_call(
            body, in_specs=[ANY] * n, out_specs=[ANY] * n, out_shape=list(out_type), scratch_shapes=list(sems),
            name=label,
        )(*operands)
    return dict(zip(names, got))


def _chip_sum(g, got, name, core):
    by_col = BIG[name][2]
    rows, cols = got.shape
    if by_col:
        tr = 128
        g_spec = pl.BlockSpec((tr, cols), lambda i, s: (s[0] * (HALF // tr) + i, 0))
    else:
        tr = _row_tile(rows, 512)
        g_spec = pl.BlockSpec((tr, HALF), lambda i, s: (i, s[0]))
    blk = pl.BlockSpec((tr, cols), lambda i, s: (i, 0))

    def body(s_ref, g_ref, r_ref, f_ref, b_ref):
        t = g_ref[...] + r_ref[...]
        f_ref[...] = t
        b_ref[...] = t.astype(BF16)

    return pl.pallas_call(
        body,
        grid_spec=pltpu.PrefetchScalarGridSpec(num_scalar_prefetch=1, grid=(rows // tr,), in_specs=[g_spec, blk],
                                               out_specs=[blk, blk]),
        out_shape=[jax.ShapeDtypeStruct(got.shape, F32), jax.ShapeDtypeStruct(got.shape, BF16)],
        name="chip_sum_" + name, compiler_params=_params("parallel"),
    )(core, g, got)


def _piece_shape(name):
    rows, cols, by_col = BIG[name]
    return (HALF, cols) if by_col else (rows, HALF)


def _handshake(peers):
    barrier = pltpu.get_barrier_semaphore()
    for peer in peers:
        pl.semaphore_signal(barrier, inc=1, device_id=peer, device_id_type=MESH)
    pl.semaphore_wait(barrier, len(peers))


def _send_chip_sums(sums, names, label, collective_id):
    n = len(names)

    def body(*refs):
        ins, outs = refs[:n], refs[n:2 * n]
        send_sems, recv_sems = refs[2 * n:]
        x, y, c = _place()
        others = [_chip_of(x, y, r) for r in range(3)]
        _handshake([(qx, qy, c) for qx, qy in others])
        copies = []
        for a, name in enumerate(names):
            for r, (qx, qy) in enumerate(others):
                copies.append(_remote(_shard_of(ins[a], name, 2 * qx + qy), outs[a].at[r], send_sems, recv_sems,
                                      3 * a + r, (qx, qy, c)))
        for cp in copies:
            cp.start()
        for cp in copies:
            cp.wait()

    return pl.kernel(
        body, mesh=plsc.ScalarSubcoreMesh(axis_name="seq", num_cores=1), name=label,
        out_type=tuple(jax.ShapeDtypeStruct((3,) + _piece_shape(nm), BF16) for nm in names),
        scratch_types=(pltpu.SemaphoreType.DMA((3 * n,)), pltpu.SemaphoreType.DMA((3 * n,))),
        compiler_params=pltpu.CompilerParams(collective_id=collective_id),
    )(*[sums[nm] for nm in names])


def _total(own, got, name, chip_core):
    rows, cols, by_col = BIG[name]
    pr, pc = _piece_shape(name)
    tr = _row_tile(pr, 352)
    if by_col:
        own_spec = pl.BlockSpec((tr, pc), lambda i, s: (i, s[0]))
    else:
        own_spec = pl.BlockSpec((tr, pc), lambda i, s: (s[0] * (pr // tr) + i, 0))
    got_spec = lambda r: pl.BlockSpec((None, tr, pc), lambda i, s: (r, i, 0))
    if by_col:
        out_spec = pl.BlockSpec((tr, pc), lambda i, s: (s[1] * (pr // tr) + i, 0))
    else:
        out_spec = pl.BlockSpec((tr, pc), lambda i, s: (i, s[1]))

    def body(s_ref, o_ref, a_ref, b_ref, c_ref, t_ref):
        t_ref[...] = ((o_ref[...] + a_ref[...].astype(F32)) + b_ref[...].astype(F32)) + c_ref[...].astype(F32)

    return pl.pallas_call(
        body,
        grid_spec=pltpu.PrefetchScalarGridSpec(
            num_scalar_prefetch=1, grid=(pr // tr,), in_specs=[own_spec, got_spec(0), got_spec(1), got_spec(2)],
            out_specs=out_spec),
        out_shape=jax.ShapeDtypeStruct((rows, cols), F32),
        name="total_" + name, compiler_params=_params("parallel"),
    )(chip_core, own, got, got, got)


def _share_totals(totals):
    def body(*refs):
        outs = refs[N_BIG:2 * N_BIG]
        send_sems, recv_sems = refs[2 * N_BIG:]
        x, y, c = _place()
        copies = []
        for a, name in enumerate(BIG_NAMES):
            mine = _half_of(outs[a], BIG[name][2], c)
            copies.append(_remote(mine, mine, send_sems, recv_sems, a, (x, y, 1 - c)))
        for cp in copies:
            cp.start()
        for a, name in enumerate(BIG_NAMES):
            theirs = _half_of(outs[a], BIG[name][2], 1 - c)
            _remote(theirs, theirs, send_sems, recv_sems, a, (x, y, 1 - c)).wait_recv()
        for cp in copies:
            cp.wait_send()

    return pl.pallas_call(
        body, in_specs=[ANY] * N_BIG, out_specs=[ANY] * N_BIG,
        out_shape=[jax.ShapeDtypeStruct(BIG[n][:2], F32) for n in BIG_NAMES],
        input_output_aliases={a: a for a in range(N_BIG)},
        scratch_shapes=[pltpu.SemaphoreType.DMA((N_BIG,)), pltpu.SemaphoreType.DMA((N_BIG,))],
        name="share_totals",
    )(*[totals[n] for n in BIG_NAMES])


VEC_ROWS = 32
VEC_ROW = {"mix_norm_g": 0, "conv_b": 1, "b_rgate": 2, "b_igate": 3, "lru_lambda": 4, "rg_norm_g": 5,
           "hg_lower_bound": 6, "hg_norm_g": 8, "ffn_norm_g": 9, "final_norm_g": 10, "loss": 11,
           "conv_w": 12, "meta_tokens": 16}
N_DEV = 8


def _all_reduce_small(pieces, gates):
    names = list(pieces)
    hv, hg = VEC_ROWS // 2, gates.shape[0] // 2

    def body(*refs):
        ins = refs[:len(names)]
        (g_ref, vec_ref, gsum_ref, mine_v, sib_v, sib_g, chip_v, chip_g, got_v, got_g,
         send_sems, recv_sems) = refs[len(names):]
        x, y, c = _place()
        chip = 2 * x + y
        sibling = (x, y, 1 - c)
        mine_v[...] = jnp.zeros_like(mine_v)
        for name, ref in zip(names, ins):
            nr, w = ref.shape
            mine_v[VEC_ROW[name]:VEC_ROW[name] + nr, 0:w] = ref[...]

        swap = [_remote(mine_v, sib_v, send_sems, recv_sems, 0, sibling),
                _remote(g_ref, sib_g, send_sems, recv_sems, 1, sibling)]
        for cp in swap:
            cp.start()
        for cp in swap:
            cp.wait()
        chip_v[...] = mine_v[...] + sib_v[...]
        chip_g[...] = g_ref[...] + sib_g[...]

        rows_v = pl.ds(pl.multiple_of(c * hv, 8), hv)
        rows_g = pl.ds(pl.multiple_of(c * hg, 8), hg)
        got_v[chip] = chip_v[rows_v, :]
        got_g[chip] = chip_g[rows_g, :]
        sends = []
        for r in range(3):
            qx, qy = _chip_of(x, y, r)
            sends.append(_remote(chip_v.at[rows_v, :], got_v.at[chip], send_sems, recv_sems, 2 + r, (qx, qy, c)))
            sends.append(_remote(chip_g.at[rows_g, :], got_g.at[chip], send_sems, recv_sems, 5 + r, (qx, qy, c)))
        for cp in sends:
            cp.start()
        for cp in sends:
            cp.wait()
        vec_ref[rows_v, :] = ((got_v[0] + got_v[1]) + got_v[2]) + got_v[3]
        gsum_ref[rows_g, :] = ((got_g[0] + got_g[1]) + got_g[2]) + got_g[3]

        back = [_remote(vec_ref.at[rows_v, :], vec_ref.at[rows_v, :], send_sems, recv_sems, 8, sibling),
                _remote(gsum_ref.at[rows_g, :], gsum_ref.at[rows_g, :], send_sems, recv_sems, 9, sibling)]
        for cp in back:
            cp.start()
        theirs_v = vec_ref.at[pl.ds(pl.multiple_of((1 - c) * hv, 8), hv), :]
        theirs_g = gsum_ref.at[pl.ds(pl.multiple_of((1 - c) * hg, 8), hg), :]
        _remote(theirs_v, theirs_v, send_sems, recv_sems, 8, sibling).wait_recv()
        _remote(theirs_g, theirs_g, send_sems, recv_sems, 9, sibling).wait_recv()
        for cp in back:
            cp.wait_send()

    vmem = pl.BlockSpec(memory_space=pltpu.VMEM)
    n_sems = 10
    return pl.pallas_call(
        body, in_specs=[vmem] * (len(names) + 1), out_specs=[vmem, vmem],
        out_shape=[jax.ShapeDtypeStruct((VEC_ROWS, D_MODEL), F32), jax.ShapeDtypeStruct(gates.shape, F32)],
        scratch_shapes=[pltpu.VMEM((VEC_ROWS, D_MODEL), F32), pltpu.VMEM((VEC_ROWS, D_MODEL), F32),
                        pltpu.VMEM(gates.shape, F32), pltpu.VMEM((VEC_ROWS, D_MODEL), F32),
                        pltpu.VMEM(gates.shape, F32), pltpu.VMEM((N_CHIPS, hv, D_MODEL), F32),
                        pltpu.VMEM((N_CHIPS, hg) + gates.shape[1:], F32),
                        pltpu.SemaphoreType.DMA((n_sems,)), pltpu.SemaphoreType.DMA((n_sems,))],
        name="all_reduce_small",
    )(*[pieces[n] for n in names], gates)


def _adamw_math(w, g, m, v):
    m = ADAM_B1 * m + (1.0 - ADAM_B1) * g
    v = ADAM_B2 * v + (1.0 - ADAM_B2) * (g * g)
    m_hat = m / (1.0 - ADAM_B1 ** ADAM_STEP)
    v_hat = v / (1.0 - ADAM_B2 ** ADAM_STEP)
    delta = -ADAM_LR * (m_hat / (jnp.sqrt(v_hat) + ADAM_EPS) + ADAM_WD * w)
    return delta, m, v


def _adamw_big(w, g, m, v, name):
    rows, cols = w.shape
    tr = _row_tile(rows, 352)

    def body(w_ref, g_ref, m_ref, v_ref, d_ref, nm_ref, nv_ref):
        d_ref[...], nm_ref[...], nv_ref[...] = _adamw_math(w_ref[...], g_ref[...], m_ref[...], v_ref[...])

    blk = pl.BlockSpec((tr, cols), lambda i: (i, 0))
    return pl.pallas_call(
        body, grid=(rows // tr,), in_specs=[blk] * 4, out_specs=[blk] * 3,
        out_shape=[jax.ShapeDtypeStruct(w.shape, F32)] * 3,
        name="adamw_" + name, compiler_params=_params("parallel"),
    )(w, g, m, v)


SMALL = {"meta_tokens": (N_META, D_MODEL // N_CHIPS), "mix_norm_g": (1, D_MODEL), "conv_w": (CONV_W, D_RG // N_CHIPS),
         "conv_b": (1, D_RG), "w_rgate": (D_RG, RG_HEAD_DIM), "b_rgate": (1, D_RG), "w_igate": (D_RG, RG_HEAD_DIM),
         "b_igate": (1, D_RG), "lru_lambda": (1, D_RG), "rg_norm_g": (1, D_RG), "hg_lower_bound": (2, D_HG),
         "hg_norm_g": (1, HG_HEAD_DIM), "ffn_norm_g": (1, D_MODEL), "final_norm_g": (1, D_MODEL)}
SMALL_NAMES = tuple(SMALL)
SHARDED_SMALL = ("meta_tokens", "conv_w")


def _adamw_small(vec, gates, w, m, v):
    n = len(SMALL_NAMES)

    def body(*refs):
        vec_ref, gates_ref = refs[:2]
        w_refs, m_refs, v_refs = refs[2:2 + n], refs[2 + n:2 + 2 * n], refs[2 + 2 * n:2 + 3 * n]
        outs = refs[2 + 3 * n:]
        loss_ref = outs[0]
        x, y, _ = _place()
        chip = 2 * x + y
        loss_ref[...] = vec_ref[VEC_ROW["loss"]:VEC_ROW["loss"] + 1, 0:1]

        def update(k, g):
            g_ref, d_ref, nm_ref, nv_ref = outs[1 + 4 * k:5 + 4 * k]
            g_ref[...] = g
            d_ref[...], nm_ref[...], nv_ref[...] = _adamw_math(w_refs[k][...], g, m_refs[k][...], v_refs[k][...])

        for k, name in enumerate(SMALL_NAMES):
            nr, w_ = SMALL[name]
            if name == "w_rgate":
                update(k, gates_ref[0:D_RG, :])
            elif name == "w_igate":
                update(k, gates_ref[D_RG:2 * D_RG, :])
            elif name in SHARDED_SMALL:
                r0 = VEC_ROW[name]
                for q in range(N_CHIPS):
                    @pl.when(chip == q)
                    def _(k=k, r0=r0, nr=nr, w_=w_, q=q):
                        update(k, vec_ref[r0:r0 + nr, q * w_:(q + 1) * w_])
            else:
                r0 = VEC_ROW[name]
                update(k, vec_ref[r0:r0 + nr, 0:w_])

    vmem = pl.BlockSpec(memory_space=pltpu.VMEM)
    out_shape = [jax.ShapeDtypeStruct((1, 1), F32)]
    for name in SMALL_NAMES:
        out_shape += [jax.ShapeDtypeStruct(SMALL[name], F32)] * 4
    outs = pl.pallas_call(
        body, in_specs=[vmem] * (2 + 3 * n), out_specs=[vmem] * len(out_shape), out_shape=out_shape,
        name="adamw_small",
    )(vec, gates, *[w[k] for k in SMALL_NAMES], *[m[k] for k in SMALL_NAMES], *[v[k] for k in SMALL_NAMES])
    loss = outs[0]
    res = {name: tuple(outs[1 + 4 * k:5 + 4 * k]) for k, name in enumerate(SMALL_NAMES)}
    return loss, res


WEIGHT_NAMES = ("meta_tokens", "mix_norm_g", "w_in", "conv_w", "conv_b", "w_rgate", "b_rgate", "w_igate", "b_igate",
                "lru_lambda", "rg_norm_g", "hg_lower_bound", "hg_norm_g", "w_out", "ffn_norm_g", "w_gate_up", "w_down",
                "final_norm_g")


def _to_2d(name, a):
    if name in BIG:
        return a.reshape(BIG[name][:2])
    return a.reshape(SMALL[name])


def kernel(x, meta_tokens, mix_norm_g, w_in, conv_w, conv_b, w_rgate, b_rgate, w_igate, b_igate, lru_lambda, rg_norm_g, hg_lower_bound, hg_norm_g, w_out, ffn_norm_g, w_gate_up, w_down, final_norm_g, loss_target, m_meta_tokens, m_mix_norm_g, m_w_in, m_conv_w, m_conv_b, m_w_rgate, m_b_rgate, m_w_igate, m_b_igate, m_lru_lambda, m_rg_norm_g, m_hg_lower_bound, m_hg_norm_g, m_w_out, m_ffn_norm_g, m_w_gate_up, m_w_down, m_final_norm_g, v_meta_tokens, v_mix_norm_g, v_w_in, v_conv_w, v_conv_b, v_w_rgate, v_b_rgate, v_w_igate, v_b_igate, v_lru_lambda, v_rg_norm_g, v_hg_lower_bound, v_hg_norm_g, v_w_out, v_ffn_norm_g, v_w_gate_up, v_w_down, v_final_norm_g):
    w_raw = dict(zip(WEIGHT_NAMES, (meta_tokens, mix_norm_g, w_in, conv_w, conv_b, w_rgate, b_rgate, w_igate, b_igate,
                                    lru_lambda, rg_norm_g, hg_lower_bound, hg_norm_g, w_out, ffn_norm_g, w_gate_up,
                                    w_down, final_norm_g)))
    m_raw = dict(zip(WEIGHT_NAMES, (m_meta_tokens, m_mix_norm_g, m_w_in, m_conv_w, m_conv_b, m_w_rgate, m_b_rgate,
                                    m_w_igate, m_b_igate, m_lru_lambda, m_rg_norm_g, m_hg_lower_bound, m_hg_norm_g,
                                    m_w_out, m_ffn_norm_g, m_w_gate_up, m_w_down, m_final_norm_g)))
    v_raw = dict(zip(WEIGHT_NAMES, (v_meta_tokens, v_mix_norm_g, v_w_in, v_conv_w, v_conv_b, v_w_rgate, v_b_rgate,
                                    v_w_igate, v_b_igate, v_lru_lambda, v_rg_norm_g, v_hg_lower_bound, v_hg_norm_g,
                                    v_w_out, v_ffn_norm_g, v_w_gate_up, v_w_down, v_final_norm_g)))
    w = {k: _to_2d(k, a) for k, a in w_raw.items()}
    m = {k: _to_2d(k, a) for k, a in m_raw.items()}
    v = {k: _to_2d(k, a) for k, a in v_raw.items()}

    x_i, y_i, c_i = _place()
    core = jnp.reshape(c_i, (1,)).astype(jnp.int32)
    chip = jnp.reshape(2 * x_i + y_i, (1,)).astype(jnp.int32)
    chip_core = jnp.concatenate([chip, core])

    placed = {k: _cast_into_full(w[k], k, chip) for k in BIG_NAMES}
    first, (meta_full, cw_full) = _gather_weights(placed, [w["meta_tokens"], w["conv_w"]], ("w_in",), "gather_first", 1)
    rest, _ = _gather_weights(placed, [], ("w_out", "w_gate_up", "w_down"), "gather_rest", 2)
    full = {**first, **rest}

    seq = x.shape[1]
    h0 = jnp.concatenate([jnp.zeros((PAD, D_MODEL), F32), meta_full, x[0]], axis=0)
    target = jnp.concatenate([jnp.zeros((PAD + N_META, D_MODEL), F32), loss_target[0]], axis=0)
    small = {k: w[k] for k in SMALL_NAMES if k not in SHARDED_SMALL}
    small["conv_w"] = cw_full

    def reduce_to_chips(grads, names, tag, collective_ids):
        got = _exchange_halves(grads, names, "exchange_halves_" + tag, collective_ids[0])

        def chip_sums():
            return {n: _chip_sum(grads[n], got[n], n, core) for n in names}

        def send(sums):
            arrived = _send_chip_sums({n: sums[n][1] for n in names}, names, "send_chip_sums_" + tag,
                                      collective_ids[1])
            return {n: (sums[n][0], a) for n, a in zip(names, arrived)}

        return chip_sums, send

    ffn_names, mixer_names = ("w_gate_up", "w_down"), ("w_in", "w_out")
    loss, dh0, grads, parts, parts_mixer = _local_step(
        h0, target, full["w_in"], full["w_out"], full["w_gate_up"], full["w_down"], small,
        on_ffn_grads=lambda g: reduce_to_chips(g, ffn_names, "ffn", (3, 4)),
        on_mixer_grads=lambda g: reduce_to_chips(g, mixer_names, "mixer", (None, 5)))
    parts.update(parts_mixer)
    totals = {n: _total(parts[n][0], parts[n][1], n, chip_core) for n in BIG_NAMES}
    g_big = dict(zip(BIG_NAMES, _share_totals(totals)))

    pieces = {k: grads[k] for k in VEC_ROW if k not in ("loss", "meta_tokens")}
    pieces["loss"] = loss
    pieces["meta_tokens"] = dh0[PAD:PAD + N_META]
    vec, gates = _all_reduce_small(pieces, grads["w_gates"])
    loss_sum, res = _adamw_small(vec, gates, w, m, v)
    for n in BIG_NAMES:
        res[n] = (g_big[n],) + tuple(_adamw_big(w[n], g_big[n], m[n], v[n], n))

    grad_x = dh0[PAD + N_META:].reshape(1, seq, D_MODEL)
    out = [loss_sum.reshape(()), grad_x]
    for j in range(4):
        out += [res[n][j].reshape(w_raw[n].shape) for n in WEIGHT_NAMES]
    return tuple(out)
```

```python
import functools
import math

import jax
import jax.numpy as jnp
from jax import lax
from jax.experimental import pallas as pl
from jax.experimental.pallas import tpu as pltpu
from jax.experimental.pallas import tpu_sc as plsc

F32 = jnp.float32
BF16 = jnp.bfloat16
HIGHEST = lax.Precision.HIGHEST
MESH = pl.DeviceIdType.MESH

D_MODEL = 1024
D_RG = 512
RG_HEAD_DIM = 64
D_HG = 512
HG_HEAD_DIM = 128
HG_HEADS = 4
CHUNK = 64
SUB = 16
N_SUB = CHUNK // SUB
N_META = 16
PAD = CHUNK - N_META
D_IN = 3072
D_FF = 2816
CONV_W = 4
LRU_C = 8.0
EPS = 1e-6
EXP_CLAMP = 80.0
GELU_C = math.sqrt(2.0 / math.pi)
GELU_A = 0.044715
N_CHIPS = 4

ADAM_LR = 0.001
ADAM_B1 = 0.9
ADAM_B2 = 0.999
ADAM_EPS = 1e-08
ADAM_WD = 0.01
ADAM_STEP = 10

VMEM_LIMIT = 56 * 1024 * 1024


def _params(*sem):
    return pltpu.CompilerParams(dimension_semantics=sem, vmem_limit_bytes=VMEM_LIMIT)


def _row_tile(rows, target):
    best = None
    for t in range(16, min(rows, target) + 1, 16):
        if rows % t == 0:
            best = t
    assert best is not None, rows
    return best


def _sigmoid(x):
    return 0.5 * jnp.tanh(0.5 * x) + 0.5


def _dot(a, b):
    return jnp.dot(a, b, preferred_element_type=F32)


def _dot_nt(a, b):
    return lax.dot_general(a, b, (((1,), (1,)), ((), ())), preferred_element_type=F32)


def _dot_tn(a, b):
    return lax.dot_general(a, b, (((0,), (0,)), ((), ())), preferred_element_type=F32)


def _rms(x):
    return lax.rsqrt(jnp.mean(x * x, axis=-1, keepdims=True) + EPS)


def _rms_bwd(dn, n, r):
    return r * (dn - n * jnp.mean(dn * n, axis=-1, keepdims=True))


def _gelu_parts(x):
    t = jnp.tanh(GELU_C * (x + GELU_A * x * x * x))
    g = 0.5 * x * (1.0 + t)
    dg = 0.5 * (1.0 + t) + 0.5 * x * (1.0 - t * t) * GELU_C * (1.0 + 3.0 * GELU_A * x * x)
    return g, dg


def _softplus_neg(lam):
    e = jnp.exp(-jnp.abs(lam))
    w = 1.0 + e
    log1p = jnp.where(w == 1.0, e, jnp.log(w) * e / (w - 1.0))
    return jnp.maximum(-lam, 0.0) + log1p


def _head_mask():
    r = lax.broadcasted_iota(jnp.int32, (D_RG, D_RG), 0) // RG_HEAD_DIM
    c = lax.broadcasted_iota(jnp.int32, (D_RG, D_RG), 1) // RG_HEAD_DIM
    return r == c


def _head_fold():
    r = lax.broadcasted_iota(jnp.int32, (D_RG, RG_HEAD_DIM), 0) % RG_HEAD_DIM
    c = lax.broadcasted_iota(jnp.int32, (D_RG, RG_HEAD_DIM), 1)
    return (r == c).astype(F32)


def _gate_weights(w_r, w_i):
    def body(wr_ref, wi_ref, o_ref):
        fold = _head_fold()
        mask = _head_mask()
        for k, ref in enumerate((wr_ref, wi_ref)):
            full = lax.dot_general(ref[...], fold, (((1,), (1,)), ((), ())),
                                   precision=HIGHEST, preferred_element_type=F32)
            o_ref[:, k * D_RG:(k + 1) * D_RG] = jnp.where(mask, full, 0.0).astype(BF16)

    return pl.pallas_call(
        body, out_shape=jax.ShapeDtypeStruct((D_RG, 2 * D_RG), BF16), name="gate_weights",
    )(w_r, w_i)


def _in_proj(h0, g1, w_in):
    T = h0.shape[0]
    tm = _row_tile(T, 416)

    def body(h_ref, g_ref, w_ref, p_ref, u_ref):
        h = h_ref[...]
        u = (h * _rms(h) * g_ref[...]).astype(BF16)
        u_ref[...] = u
        p_ref[...] = _dot(u, w_ref[...])

    return pl.pallas_call(
        body, grid=(T // tm,),
        in_specs=[pl.BlockSpec((tm, D_MODEL), lambda i: (i, 0)),
                  pl.BlockSpec((1, D_MODEL), lambda i: (0, 0)),
                  pl.BlockSpec((D_MODEL, D_IN), lambda i: (0, 0))],
        out_specs=[pl.BlockSpec((tm, D_IN), lambda i: (i, 0)),
                   pl.BlockSpec((tm, D_MODEL), lambda i: (i, 0))],
        out_shape=[jax.ShapeDtypeStruct((T, D_IN), F32), jax.ShapeDtypeStruct((T, D_MODEL), BF16)],
        name="in_proj", compiler_params=_params("parallel"),
    )(h0, g1, w_in)


def _scan_block_fwd(A, B, rowi):
    for d in (1, 2, 4):
        a_sh = pltpu.roll(A, d, axis=0)
        b_sh = pltpu.roll(B, d, axis=0)
        m = rowi >= d
        B = jnp.where(m, A * b_sh + B, B)
        A = jnp.where(m, A * a_sh, A)
    return A, B


def _scan_block_bwd(A, B, rowi):
    for d in (1, 2, 4):
        a_sh = pltpu.roll(A, 8 - d, axis=0)
        b_sh = pltpu.roll(B, 8 - d, axis=0)
        m = rowi < 8 - d
        B = jnp.where(m, A * b_sh + B, B)
        A = jnp.where(m, A * a_sh, A)
    return A, B


def _rg_gates(xc, w_ref, bg_ref, lam):
    pre = _dot(xc.astype(BF16), w_ref[...]) + bg_ref[...]
    r = _sigmoid(pre[:, :D_RG])
    ig = _sigmoid(pre[:, D_RG:])
    sp = _softplus_neg(lam)
    la = -LRU_C * sp * r
    a = jnp.exp(la)
    th = jnp.tanh(la)
    u = 1.0 - th
    rc = pl.reciprocal(u, approx=True)
    rc = rc * (2.0 - u * rc)
    rc = rc * (2.0 - u * rc)
    m2 = -2.0 * th * rc
    inv_m = lax.rsqrt(jnp.maximum(m2, 1e-30))
    return r, ig, sp, a, m2 * inv_m, inv_m


def _conv(ext, cw_ref, cb_ref, tm):
    xc = cb_ref[...] + cw_ref[0:1, :] * ext[8 - 3:8 - 3 + tm, :]
    for j in range(1, CONV_W):
        xc = xc + cw_ref[j:j + 1, :] * ext[8 - 3 + j:8 - 3 + j + tm, :]
    return xc


def _scan_unroll(blocks):
    return 4 if blocks % 4 == 0 else 2 if blocks % 2 == 0 else 1


def _rg_fwd(p, cw, cb, wg, bg, lam, rg_g):
    T = p.shape[0]
    tm = _row_tile(T, 416)
    unroll = _scan_unroll(tm // 8)

    def body(xg_ref, cw_ref, cb_ref, w_ref, bg_ref, lam_ref, g_ref, y_ref, h_ref, ext, a_s, b_s, carry):
        i = pl.program_id(0)

        @pl.when(i == 0)
        def _():
            ext[0:8, :] = jnp.zeros((8, D_RG), F32)
            carry[...] = jnp.zeros((1, D_RG), F32)

        ext[8:8 + tm, :] = xg_ref[:, :D_RG]
        xc = _conv(ext, cw_ref, cb_ref, tm)
        r, ig, sp, a, m, _ = _rg_gates(xc, w_ref, bg_ref, lam_ref[...])
        row = i * tm + lax.broadcasted_iota(jnp.int32, (tm, 1), 0)
        a_s[...] = a
        b_s[...] = jnp.where(row >= PAD, m * ig * xc, 0.0)
        rowi = lax.broadcasted_iota(jnp.int32, (8, D_RG), 0)

        def blk(j, c):
            for u in range(unroll):
                o = pl.multiple_of((j * unroll + u) * 8, 8)
                A, B = _scan_block_fwd(a_s[pl.ds(o, 8), :], b_s[pl.ds(o, 8), :], rowi)
                h = B + A * c
                h_ref[pl.ds(o, 8), :] = h
                c = h[7:8, :]
            return c

        carry[...] = lax.fori_loop(0, tm // (8 * unroll), blk, carry[...])
        ext[0:8, :] = ext[tm:tm + 8, :]
        g, _ = _gelu_parts(xg_ref[:, D_RG:])
        yy = g * h_ref[...]
        y_ref[...] = (yy * _rms(yy) * g_ref[...]).astype(BF16)

    vec = lambda n: pl.BlockSpec((1, n), lambda i: (0, 0))
    return pl.pallas_call(
        body, grid=(T // tm,),
        in_specs=[pl.BlockSpec((tm, 2 * D_RG), lambda i: (i, 0)),
                  pl.BlockSpec((CONV_W, D_RG), lambda i: (0, 0)), vec(D_RG),
                  pl.BlockSpec((D_RG, 2 * D_RG), lambda i: (0, 0)), vec(2 * D_RG), vec(D_RG), vec(D_RG)],
        out_specs=[pl.BlockSpec((tm, D_RG), lambda i: (i, 0)), pl.BlockSpec((tm, D_RG), lambda i: (i, 0))],
        out_shape=[jax.ShapeDtypeStruct((T, D_RG), BF16), jax.ShapeDtypeStruct((T, D_RG), F32)],
        scratch_shapes=[pltpu.VMEM((tm + 8, D_RG), F32), pltpu.VMEM((tm, D_RG), F32),
                        pltpu.VMEM((tm, D_RG), F32), pltpu.VMEM((1, D_RG), F32)],
        name="rg_fwd", compiler_params=_params("arbitrary"),
    )(p, cw, cb, wg, bg, lam, rg_g)


def _tri(lower):
    r = lax.broadcasted_iota(jnp.int32, (CHUNK, CHUNK), 0)
    c = lax.broadcasted_iota(jnp.int32, (CHUNK, CHUNK), 1)
    return ((c <= r) if lower else (c >= r)).astype(F32)


def _hg_gates(hq, hf, lbraw_ref, valid):
    lb = _sigmoid(lbraw_ref[0:1, :] - lbraw_ref[1:2, :])
    sq = _sigmoid(hq)
    q = hq * sq
    sf = _sigmoid(hf)
    f = lb + (1.0 - lb) * sf
    lf = jnp.where(valid, jnp.log(f), 0.0)
    b = jnp.dot(_tri(True), lf, precision=HIGHEST, preferred_element_type=F32)
    return lb, sq, q, sf, f, b


def _hg_head(qh, kh, bh):
    blk = lax.broadcasted_iota(jnp.int32, (CHUNK, 1), 0) // SUB
    b_last = bh[CHUNK - 1:CHUNK, :]
    refs = [bh[SUB * s:SUB * s + 1, :] for s in range(N_SUB)]
    r_sel = refs[N_SUB - 1]
    for s in range(N_SUB - 2, -1, -1):
        r_sel = jnp.where(blk == s, refs[s], r_sel)
    eb = jnp.exp(bh)
    eq = jnp.exp(bh - r_sel)
    ekh = jnp.exp(b_last - bh)
    ek = [jnp.exp(jnp.minimum(refs[s] - bh, EXP_CLAMP)) for s in range(N_SUB)]
    qe = qh * eq
    q_hat = jnp.concatenate([jnp.where(blk == s, qe, 0.0) for s in range(N_SUB)], axis=1)
    k_til = jnp.concatenate([kh * ek[s] for s in range(N_SUB)], axis=1)
    return blk, b_last, eb, eq, ekh, ek, q_hat, k_til


def _causal():
    r = lax.broadcasted_iota(jnp.int32, (CHUNK, CHUNK), 0)
    c = lax.broadcasted_iota(jnp.int32, (CHUNK, CHUNK), 1)
    return r >= c


def _chunks_per_step(n_chunks):
    for c in (5, 4, 3, 2):
        if n_chunks % c == 0:
            return c
    return 1


def _hg_fwd(p, lbraw, hg_g):
    T = p.shape[0]
    n_chunks = T // CHUNK
    cps = _chunks_per_step(n_chunks)
    rows = cps * CHUNK

    def body(hq_ref, hf_ref, hi_ref, hg_ref, lb_ref, g_ref, y_ref, o_ref, st_all_ref, st):
        i = pl.program_id(0)

        @pl.when(i == 0)
        def _():
            st[...] = jnp.zeros_like(st)

        def chunk(j, carry):
            rs = pl.ds(pl.multiple_of(j * CHUNK, CHUNK), CHUNK)
            chunk_body(i * cps + j, hq_ref.at[rs, :], hf_ref.at[rs, :], hi_ref.at[rs, :], hg_ref.at[rs, :], lb_ref,
                       g_ref, y_ref.at[rs, :], o_ref.at[rs, :], st_all_ref.at[pl.ds(j, 1)], st)
            return carry

        lax.fori_loop(0, cps, chunk, 0, unroll=True)

    def chunk_body(n, hq_ref, hf_ref, hi_ref, hg_ref, lb_ref, g_ref, y_ref, o_ref, st_all_ref, st):
        valid = (n * CHUNK + lax.broadcasted_iota(jnp.int32, (CHUNK, 1), 0)) >= PAD
        hq, hf, v, hg = hq_ref[...], hf_ref[...], hi_ref[...], hg_ref[...]
        lb, sq, q, sf, f, b = _hg_gates(hq, hf, lb_ref, valid)
        k = 1.0 - f
        st_all_ref[0] = st[...]
        causal = _causal()
        v_t = v.T.astype(BF16)
        heads = [slice(h * HG_HEAD_DIM, (h + 1) * HG_HEAD_DIM) for h in range(HG_HEADS)]
        fac = []
        for sl in heads:
            qh, kh, bh = q[:, sl], k[:, sl], b[:, sl]
            _, b_last, eb, _, ekh, _, q_hat, k_til = _hg_head(qh, kh, bh)
            fac.append((jnp.exp(b_last), (qh * eb).astype(BF16), q_hat.astype(BF16), k_til.astype(BF16),
                        (kh * ekh).astype(BF16), v[:, sl].astype(BF16)))
        raw = []
        for sl, (_, q_til, q_hat, k_til, k_hat, _) in zip(heads, fac):
            st_h = st[sl, :]
            raw.append((_dot_nt(q_til, st_h.astype(BF16)), _dot_nt(q_hat, k_til), _dot(v_t[sl, :], k_hat), st_h))
        for sl, (e_last, _, _, _, _, vb), (inter, att, upd, st_h) in zip(heads, fac, raw):
            o = inter + _dot(jnp.where(causal, att, 0.0).astype(BF16), vb)
            st[sl, :] = st_h * e_last + upd
            o_ref[:, sl] = o
            hgh = hg[:, sl]
            y_ref[:, sl] = (o * _rms(o) * g_ref[...] * (hgh * _sigmoid(hgh))).astype(BF16)

    col = lambda j: pl.BlockSpec((rows, D_HG), lambda n: (n, j))
    return pl.pallas_call(
        body, grid=(n_chunks // cps,),
        in_specs=[col(2), col(3), col(4), col(5),
                  pl.BlockSpec((2, D_HG), lambda n: (0, 0)), pl.BlockSpec((1, HG_HEAD_DIM), lambda n: (0, 0))],
        out_specs=[pl.BlockSpec((rows, D_HG), lambda n: (n, 0)), pl.BlockSpec((rows, D_HG), lambda n: (n, 0)),
                   pl.BlockSpec((cps, D_HG, HG_HEAD_DIM), lambda n: (n, 0, 0))],
        out_shape=[jax.ShapeDtypeStruct((T, D_HG), BF16), jax.ShapeDtypeStruct((T, D_HG), F32),
                   jax.ShapeDtypeStruct((n_chunks, D_HG, HG_HEAD_DIM), F32)],
        scratch_shapes=[pltpu.VMEM((D_HG, HG_HEAD_DIM), F32)],
        name="hg_fwd", compiler_params=_params("arbitrary"),
    )(p, p, p, p, lbraw, hg_g)


def _out_proj(h0, y_rg, y_hg, w_out, g2):
    T = h0.shape[0]
    tm = _row_tile(T, 832)

    def body(h_ref, yr_ref, yh_ref, w_ref, g_ref, h1_ref, v_ref, y_ref):
        y_ref[:, :D_RG] = yr_ref[...]
        y_ref[:, D_RG:] = yh_ref[...]
        h1 = h_ref[...] + _dot(y_ref[...], w_ref[...])
        h1_ref[...] = h1
        v_ref[...] = (h1 * _rms(h1) * g_ref[...]).astype(BF16)

    row = lambda n: pl.BlockSpec((tm, n), lambda i: (i, 0))
    return pl.pallas_call(
        body, grid=(T // tm,),
        in_specs=[row(D_MODEL), row(D_RG), row(D_HG), pl.BlockSpec((D_MODEL, D_MODEL), lambda i: (0, 0)),
                  pl.BlockSpec((1, D_MODEL), lambda i: (0, 0))],
        out_specs=[row(D_MODEL), row(D_MODEL), row(D_MODEL)],
        out_shape=[jax.ShapeDtypeStruct((T, D_MODEL), F32), jax.ShapeDtypeStruct((T, D_MODEL), BF16),
                   jax.ShapeDtypeStruct((T, D_MODEL), BF16)],
        name="out_proj", compiler_params=_params("parallel"),
    )(h0, y_rg, y_hg, w_out, g2)


def _gate_up(v, w_gu):
    T = v.shape[0]
    tm = _row_tile(T, 416)

    def body(v_ref, w_ref, gu_ref, act_ref):
        gu = _dot(v_ref[...], w_ref[...])
        gu_ref[...] = gu.astype(BF16)
        g = gu[:, :D_FF]
        act_ref[...] = (g * _sigmoid(g) * gu[:, D_FF:]).astype(BF16)

    row = lambda n: pl.BlockSpec((tm, n), lambda i: (i, 0))
    return pl.pallas_call(
        body, grid=(T // tm,),
        in_specs=[row(D_MODEL), pl.BlockSpec((D_MODEL, 2 * D_FF), lambda i: (0, 0))],
        out_specs=[row(2 * D_FF), row(D_FF)],
        out_shape=[jax.ShapeDtypeStruct((T, 2 * D_FF), BF16), jax.ShapeDtypeStruct((T, D_FF), BF16)],
        name="gate_up", compiler_params=_params("parallel"),
    )(v, w_gu)


def _down_loss(h1, act, w_down, gf, target):
    T = h1.shape[0]
    tm = _row_tile(T, 832)

    def body(h_ref, a_ref, w_ref, g_ref, t_ref, dh2_ref, dh2b_ref, loss_ref, gg_ref):
        i = pl.program_id(0)

        @pl.when(i == 0)
        def _():
            loss_ref[...] = jnp.zeros_like(loss_ref)
            gg_ref[...] = jnp.zeros_like(gg_ref)

        h2 = h_ref[...] + _dot(a_ref[...], w_ref[...])
        r = _rms(h2)
        n = h2 * r
        gf_ = g_ref[...]
        row = i * tm + lax.broadcasted_iota(jnp.int32, (tm, 1), 0)
        err = jnp.where(row >= PAD + N_META, n * gf_ - t_ref[...], 0.0)
        loss_ref[...] += 0.5 * jnp.sum(jnp.mean(err * err, axis=-1, keepdims=True), axis=0, keepdims=True)
        dy = err * (1.0 / D_MODEL)
        gg_ref[...] += jnp.sum(dy * n, axis=0, keepdims=True)
        dh2 = _rms_bwd(dy * gf_, n, r)
        dh2_ref[...] = dh2
        dh2b_ref[...] = dh2.astype(BF16)

    row_spec = lambda n: pl.BlockSpec((tm, n), lambda i: (i, 0))
    return pl.pallas_call(
        body, grid=(T // tm,),
        in_specs=[row_spec(D_MODEL), row_spec(D_FF), pl.BlockSpec((D_FF, D_MODEL), lambda i: (0, 0)),
                  pl.BlockSpec((1, D_MODEL), lambda i: (0, 0)), row_spec(D_MODEL)],
        out_specs=[row_spec(D_MODEL), row_spec(D_MODEL), pl.BlockSpec((1, 1), lambda i: (0, 0)),
                   pl.BlockSpec((1, D_MODEL), lambda i: (0, 0))],
        out_shape=[jax.ShapeDtypeStruct((T, D_MODEL), F32), jax.ShapeDtypeStruct((T, D_MODEL), BF16),
                   jax.ShapeDtypeStruct((1, 1), F32), jax.ShapeDtypeStruct((1, D_MODEL), F32)],
        name="down_loss", compiler_params=_params("arbitrary"),
    )(h1, act, w_down, gf, target)


def _ffn_bwd_act(dh2b, gu, w_down):
    T = dh2b.shape[0]
    tm = _row_tile(T, 416)

    def body(d_ref, gu_ref, w_ref, dgu_ref):
        dact = _dot_nt(d_ref[...], w_ref[...])
        g = gu_ref[:, :D_FF].astype(F32)
        u = gu_ref[:, D_FF:].astype(F32)
        s = _sigmoid(g)
        dgu_ref[:, :D_FF] = (dact * u * s * (1.0 + g * (1.0 - s))).astype(BF16)
        dgu_ref[:, D_FF:] = (dact * g * s).astype(BF16)

    row = lambda n: pl.BlockSpec((tm, n), lambda i: (i, 0))
    return pl.pallas_call(
        body, grid=(T // tm,),
        in_specs=[row(D_MODEL), row(2 * D_FF), pl.BlockSpec((D_FF, D_MODEL), lambda i: (0, 0))],
        out_specs=row(2 * D_FF),
        out_shape=jax.ShapeDtypeStruct((T, 2 * D_FF), BF16),
        name="ffn_bwd_act", compiler_params=_params("parallel"),
    )(dh2b, gu, w_down)


def _ffn_bwd_in(dgu, w_gu, h1, g2, dh2, w_out):
    T = h1.shape[0]
    tm = _row_tile(T, 416)

    def body(dgu_ref, wgu_ref, h_ref, g_ref, d2_ref, wo_ref, dh1_ref, dh1b_ref, dy_ref, gg_ref):
        i = pl.program_id(0)

        @pl.when(i == 0)
        def _():
            gg_ref[...] = jnp.zeros_like(gg_ref)

        dv = _dot_nt(dgu_ref[...], wgu_ref[...])
        h1 = h_ref[...]
        r = _rms(h1)
        n = h1 * r
        gg_ref[...] += jnp.sum(dv * n, axis=0, keepdims=True)
        dh1 = d2_ref[...] + _rms_bwd(dv * g_ref[...], n, r)
        dh1_ref[...] = dh1
        db = dh1.astype(BF16)
        dh1b_ref[...] = db
        dy_ref[...] = _dot_nt(db, wo_ref[...])

    row = lambda n: pl.BlockSpec((tm, n), lambda i: (i, 0))
    return pl.pallas_call(
        body, grid=(T // tm,),
        in_specs=[row(2 * D_FF), pl.BlockSpec((D_MODEL, 2 * D_FF), lambda i: (0, 0)),
                  row(D_MODEL), pl.BlockSpec((1, D_MODEL), lambda i: (0, 0)), row(D_MODEL),
                  pl.BlockSpec((D_MODEL, D_MODEL), lambda i: (0, 0))],
        out_specs=[row(D_MODEL), row(D_MODEL), row(D_MODEL), pl.BlockSpec((1, D_MODEL), lambda i: (0, 0))],
        out_shape=[jax.ShapeDtypeStruct((T, D_MODEL), F32), jax.ShapeDtypeStruct((T, D_MODEL), BF16),
                   jax.ShapeDtypeStruct((T, D_MODEL), F32), jax.ShapeDtypeStruct((1, D_MODEL), F32)],
        name="ffn_bwd_in", compiler_params=_params("arbitrary"),
    )(dgu, w_gu, h1, g2, dh2, w_out)


def _rg_bwd(p, hs, dy, dp, cw, cb, wg, bg, lam, rg_g):
    T = p.shape[0]
    tm = _row_tile(T, 208)
    nt = T // tm
    hb = tm // 8
    unroll = _scan_unroll(hb)

    def body(xg_ref, xh_ref, h_ref, hh_ref, dy_ref, dp_in_ref, cw_ref, cb_ref, w_ref, bg_ref, lam_ref, g_ref,
             dp_ref, gcw_ref, gcb_ref, gw_ref, gbg_ref, glam_ref, gg_ref,
             ext, dext, a_s, b_s, d_s, gacc, carry_d, carry_a):
        i = pl.program_id(0)
        t_idx = nt - 1 - i

        @pl.when(i == 0)
        def _():
            dext[tm:tm + 8, :] = jnp.zeros((8, D_RG), F32)
            carry_d[...] = jnp.zeros_like(carry_d)
            carry_a[...] = jnp.zeros_like(carry_a)
            gacc[...] = jnp.zeros_like(gacc)
            for ref in (gcw_ref, gcb_ref, gbg_ref, glam_ref, gg_ref, gw_ref):
                ref[...] = jnp.zeros_like(ref)

        first = t_idx == 0
        ext[0:8, :] = jnp.where(first, 0.0, xh_ref[:, :D_RG])
        ext[8:8 + tm, :] = xg_ref[:, :D_RG]
        xc = _conv(ext, cw_ref, cb_ref, tm)
        lam_ = lam_ref[...]
        r, ig, sp, a, m, inv_m = _rg_gates(xc, w_ref, bg_ref, lam_)
        row = t_idx * tm + lax.broadcasted_iota(jnp.int32, (tm, 1), 0)
        valid = row >= PAD

        gr = xg_ref[:, D_RG:]
        g, dgelu = _gelu_parts(gr)
        h = h_ref[...]
        yy = g * h
        rr = _rms(yy)
        nn = yy * rr
        dy_ = dy_ref[...]
        gg_ref[...] += jnp.sum(dy_ * nn, axis=0, keepdims=True)
        dyy = _rms_bwd(dy_ * g_ref[...], nn, rr)
        dp_ref[:, D_RG:] = (dyy * h * dgelu).astype(BF16)

        a_s[...] = a
        b_s[...] = dyy * g
        rowi = lax.broadcasted_iota(jnp.int32, (8, D_RG), 0)

        def blk(jj, c):
            cd, ca = c
            for u in range(unroll):
                o = pl.multiple_of((hb - 1 - (jj * unroll + u)) * 8, 8)
                a_blk = a_s[pl.ds(o, 8), :]
                a_next = jnp.where(rowi == 7, ca, pltpu.roll(a_blk, 7, axis=0))
                A, B = _scan_block_bwd(a_next, b_s[pl.ds(o, 8), :], rowi)
                d = B + A * cd
                d_s[pl.ds(o, 8), :] = d
                cd, ca = d[0:1, :], a_blk[0:1, :]
            return cd, ca

        cd, ca = lax.fori_loop(0, hb // unroll, blk, (carry_d[...], carry_a[...]))
        carry_d[...] = cd
        carry_a[...] = ca
        delta = d_s[...]

        h_last_prev = jnp.where(first, 0.0, hh_ref[7:8, :])
        row0 = lax.broadcasted_iota(jnp.int32, (tm, 1), 0) == 0
        h_prev = jnp.where(row0, h_last_prev, pltpu.roll(h, 1, axis=0))
        dbx = jnp.where(valid, delta, 0.0)
        da = delta * h_prev
        di = dbx * m * xc
        dm = dbx * ig * xc
        dla = a * (da - dm * a * inv_m)
        dla = jnp.where(valid, dla, 0.0)
        glam_ref[...] += jnp.sum(dla * r, axis=0, keepdims=True) * (LRU_C / (1.0 + jnp.exp(lam_)))
        dr = (-LRU_C) * sp * dla
        dpre = jnp.concatenate([dr * r * (1.0 - r), di * ig * (1.0 - ig)], axis=1)
        gbg_ref[...] += jnp.sum(dpre, axis=0, keepdims=True)
        dpre_b = dpre.astype(BF16)
        gacc[...] += _dot_tn(xc.astype(BF16), dpre_b)
        dxc = dbx * m * ig + _dot_nt(dpre_b, w_ref[...])
        gcb_ref[...] += jnp.sum(dxc, axis=0, keepdims=True)
        for j in range(CONV_W):
            gcw_ref[j:j + 1, :] += jnp.sum(dxc * ext[8 - 3 + j:8 - 3 + j + tm, :], axis=0, keepdims=True)
        dext[0:tm, :] = dxc
        dxr = cw_ref[0:1, :] * dext[3:3 + tm, :]
        for j in range(1, CONV_W):
            dxr = dxr + cw_ref[j:j + 1, :] * dext[3 - j:3 - j + tm, :]
        dp_ref[:, :D_RG] = dxr.astype(BF16)
        dext[tm:tm + 8, :] = dext[0:8, :]

        @pl.when(i == nt - 1)
        def _():
            fold = _head_fold()
            mask = _head_mask()
            for k in range(2):
                blockdiag = jnp.where(mask, gacc[:, k * D_RG:(k + 1) * D_RG], 0.0)
                gw_ref[k * D_RG:(k + 1) * D_RG, :] = jnp.dot(blockdiag, fold, precision=HIGHEST,
                                                             preferred_element_type=F32)

    vec = lambda n: pl.BlockSpec((1, n), lambda i: (0, 0))
    rev = lambda n: pl.BlockSpec((tm, n), lambda i: (nt - 1 - i, 0))
    halo = lambda n: pl.BlockSpec((8, n), lambda i: (jnp.maximum((nt - 1 - i) * hb - 1, 0), 0))
    return pl.pallas_call(
        body, grid=(nt,),
        in_specs=[rev(2 * D_RG), halo(2 * D_RG), rev(D_RG), halo(D_RG), rev(D_RG), ANY,
                  pl.BlockSpec((CONV_W, D_RG), lambda i: (0, 0)), vec(D_RG),
                  pl.BlockSpec((D_RG, 2 * D_RG), lambda i: (0, 0)), vec(2 * D_RG), vec(D_RG), vec(D_RG)],
        out_specs=[rev(2 * D_RG), pl.BlockSpec((CONV_W, D_RG), lambda i: (0, 0)), vec(D_RG),
                   pl.BlockSpec((2 * D_RG, RG_HEAD_DIM), lambda i: (0, 0)), vec(2 * D_RG), vec(D_RG), vec(D_RG)],
        input_output_aliases={5: 0},
        out_shape=[jax.ShapeDtypeStruct((T, D_IN), BF16), jax.ShapeDtypeStruct((CONV_W, D_RG), F32),
                   jax.ShapeDtypeStruct((1, D_RG), F32), jax.ShapeDtypeStruct((2 * D_RG, RG_HEAD_DIM), F32),
                   jax.ShapeDtypeStruct((1, 2 * D_RG), F32), jax.ShapeDtypeStruct((1, D_RG), F32),
                   jax.ShapeDtypeStruct((1, D_RG), F32)],
        scratch_shapes=[pltpu.VMEM((tm + 8, D_RG), F32), pltpu.VMEM((tm + 8, D_RG), F32),
                        pltpu.VMEM((tm, D_RG), F32), pltpu.VMEM((tm, D_RG), F32), pltpu.VMEM((tm, D_RG), F32),
                        pltpu.VMEM((D_RG, 2 * D_RG), F32), pltpu.VMEM((1, D_RG), F32), pltpu.VMEM((1, D_RG), F32)],
        name="rg_bwd", compiler_params=_params("arbitrary"),
    )(p, p, hs, hs, dy, dp, cw, cb, wg, bg, lam, rg_g)


def _hg_bwd(p, o_all, st_all, dy, lbraw, hg_g):
    T = p.shape[0]
    n_chunks = T // CHUNK
    cps = _chunks_per_step(n_chunks)
    rows = cps * CHUNK
    n_steps = n_chunks // cps

    def body(hq_ref, hf_ref, hi_ref, hg_ref, o_ref, st_ref, dy_ref, lb_ref, g_ref,
             dp_ref, glb_ref, gg_ref, dst):
        i = pl.program_id(0)

        @pl.when(i == 0)
        def _():
            dst[...] = jnp.zeros_like(dst)
            glb_ref[...] = jnp.zeros_like(glb_ref)
            gg_ref[...] = jnp.zeros_like(gg_ref)

        dp_ref[:, :2 * D_RG] = jnp.zeros((rows, 2 * D_RG), BF16)

        def chunk(jj, carry):
            j = cps - 1 - jj
            rs = pl.ds(pl.multiple_of(j * CHUNK, CHUNK), CHUNK)
            chunk_body((n_steps - 1 - i) * cps + j, hq_ref.at[rs, :], hf_ref.at[rs, :], hi_ref.at[rs, :],
                       hg_ref.at[rs, :], o_ref.at[rs, :], st_ref.at[pl.ds(j, 1)], dy_ref.at[rs, :], lb_ref, g_ref,
                       dp_ref.at[rs, pl.ds(2 * D_RG, 4 * D_HG)], glb_ref, gg_ref, dst)
            return carry

        lax.fori_loop(0, cps, chunk, 0, unroll=True)

    def chunk_body(n, hq_ref, hf_ref, hi_ref, hg_ref, o_ref, st_ref, dy_ref, lb_ref, g_ref,
                   dp_ref, glb_ref, gg_ref, dst):
        valid = (n * CHUNK + lax.broadcasted_iota(jnp.int32, (CHUNK, 1), 0)) >= PAD
        hq, hf, v, hg = hq_ref[...], hf_ref[...], hi_ref[...], hg_ref[...]
        lb, sq, q, sf, f, b = _hg_gates(hq, hf, lb_ref, valid)
        k = 1.0 - f
        causal = _causal()
        r_i = lax.broadcasted_iota(jnp.int32, (CHUNK, CHUNK), 0)
        c_i = lax.broadcasted_iota(jnp.int32, (CHUNK, CHUNK), 1)
        causal_t = r_i <= c_i
        is_last = lax.broadcasted_iota(jnp.int32, (CHUNK, 1), 0) == CHUNK - 1
        g_ = g_ref[...]
        db_parts, dq_parts, dk_parts = [], [], []
        gg = jnp.zeros((1, HG_HEAD_DIM), F32)
        heads = [slice(h * HG_HEAD_DIM, (h + 1) * HG_HEAD_DIM) for h in range(HG_HEADS)]

        do_parts = []
        for h, sl in enumerate(heads):
            o = o_ref[:, sl]
            ro = _rms(o)
            no = o * ro
            hgh = hg[:, sl]
            sg = _sigmoid(hgh)
            dyh = dy_ref[:, sl]
            dp_ref[:, 3 * D_HG + h * HG_HEAD_DIM:3 * D_HG + (h + 1) * HG_HEAD_DIM] = (
                dyh * no * g_ * sg * (1.0 + hgh * (1.0 - sg))).astype(BF16)
            dng = dyh * hgh * sg
            gg = gg + jnp.sum(dng * no, axis=0, keepdims=True)
            do_parts.append(_rms_bwd(dng * g_, no, ro))
        do_t = jnp.concatenate(do_parts, axis=1).T.astype(BF16)

        fac = []
        for sl, do in zip(heads, do_parts):
            qh, kh, bh = q[:, sl], k[:, sl], b[:, sl]
            blk, b_last, eb, eq, ekh, ek, q_hat, k_til = _hg_head(qh, kh, bh)
            fac.append(dict(qh=qh, kh=kh, blk=blk, e_last=jnp.exp(b_last), eb=eb, eq=eq, ekh=ekh, ek=ek,
                            q_til=qh * eb, k_hat=kh * ekh, qhb=q_hat.astype(BF16), ktb=k_til.astype(BF16),
                            vb=v[:, sl].astype(BF16), dob=do.astype(BF16)))

        first = []
        for sl, t in zip(heads, fac):
            st_h = st_ref[0, sl, :]
            dst_h = dst[sl, :]
            dstb = dst_h.astype(BF16)
            first.append(dict(
                att_t=_dot_nt(t["ktb"], t["qhb"]), datt=_dot_nt(t["dob"], t["vb"]),
                datt_t=_dot_nt(t["vb"], t["dob"]), dk_hat=_dot(t["vb"], dstb),
                dv=_dot_nt(t["k_hat"].astype(BF16), dstb), dq_til=_dot(t["dob"], st_h.astype(BF16)),
                state=t["e_last"] * jnp.sum(dst_h * st_h, axis=0, keepdims=True)))
            dst[sl, :] = dst_h * t["e_last"] + _dot(do_t[sl, :], t["q_til"].astype(BF16))

        for h, (t, m) in enumerate(zip(fac, first)):
            qh, kh, blk, eb, eq, ekh, ek = t["qh"], t["kh"], t["blk"], t["eb"], t["eq"], t["ekh"], t["ek"]
            q_til, k_hat, qhb, ktb, dob = t["q_til"], t["k_hat"], t["qhb"], t["ktb"], t["dob"]
            dk_hat, dq_til = m["dk_hat"], m["dq_til"]
            dv = m["dv"] + _dot(jnp.where(causal_t, m["att_t"], 0.0).astype(BF16), dob)
            dq_hat = _dot(jnp.where(causal, m["datt"], 0.0).astype(BF16), ktb)
            dk_til = _dot(jnp.where(causal_t, m["datt_t"], 0.0).astype(BF16), qhb)
            db_last = jnp.sum(dk_hat * k_hat, axis=0, keepdims=True) + m["state"]
            dq_sel = dq_hat[:, (N_SUB - 1) * HG_HEAD_DIM:]
            for s in range(N_SUB - 2, -1, -1):
                dq_sel = jnp.where(blk == s, dq_hat[:, s * HG_HEAD_DIM:(s + 1) * HG_HEAD_DIM], dq_sel)
            dq_a = dq_sel * eq
            dk_a = dk_til[:, :HG_HEAD_DIM] * ek[0]
            for s in range(1, N_SUB):
                dk_a = dk_a + dk_til[:, s * HG_HEAD_DIM:(s + 1) * HG_HEAD_DIM] * ek[s]
            db_att = qhb.astype(F32) * dq_hat - ktb.astype(F32) * dk_til
            db = dq_til * q_til - dk_hat * k_hat
            for s in range(N_SUB):
                db = db + db_att[:, s * HG_HEAD_DIM:(s + 1) * HG_HEAD_DIM]
            db_parts.append(jnp.where(is_last, db + db_last, db))
            dq_parts.append(dq_til * eb + dq_a)
            dk_parts.append(dk_hat * ekh + dk_a)
            dp_ref[:, 2 * D_HG + h * HG_HEAD_DIM:2 * D_HG + (h + 1) * HG_HEAD_DIM] = dv.astype(BF16)

        gg_ref[...] += gg
        db = jnp.concatenate(db_parts, axis=1)
        dq = jnp.concatenate(dq_parts, axis=1)
        dk = jnp.concatenate(dk_parts, axis=1)
        dlf = jnp.where(valid, jnp.dot(_tri(False), db, precision=HIGHEST, preferred_element_type=F32), 0.0)
        dp_ref[:, :D_HG] = (dq * sq * (1.0 + hq * (1.0 - sq))).astype(BF16)
        df = dlf / f - dk
        dlb = jnp.sum(df * (1.0 - sf), axis=0, keepdims=True) * lb * (1.0 - lb)
        glb_ref[0:1, :] += dlb
        glb_ref[1:2, :] += -dlb
        dp_ref[:, D_HG:2 * D_HG] = (df * (1.0 - lb) * sf * (1.0 - sf)).astype(BF16)

    rev = lambda j: pl.BlockSpec((rows, D_HG), lambda i: (n_steps - 1 - i, j))
    return pl.pallas_call(
        body, grid=(n_steps,),
        in_specs=[rev(2), rev(3), rev(4), rev(5), rev(0),
                  pl.BlockSpec((cps, D_HG, HG_HEAD_DIM), lambda i: (n_steps - 1 - i, 0, 0)), rev(1),
                  pl.BlockSpec((2, D_HG), lambda i: (0, 0)), pl.BlockSpec((1, HG_HEAD_DIM), lambda i: (0, 0))],
        out_specs=[pl.BlockSpec((rows, D_IN), lambda i: (n_steps - 1 - i, 0)),
                   pl.BlockSpec((2, D_HG), lambda i: (0, 0)), pl.BlockSpec((1, HG_HEAD_DIM), lambda i: (0, 0))],
        out_shape=[jax.ShapeDtypeStruct((T, D_IN), BF16), jax.ShapeDtypeStruct((2, D_HG), F32),
                   jax.ShapeDtypeStruct((1, HG_HEAD_DIM), F32)],
        scratch_shapes=[pltpu.VMEM((D_HG, HG_HEAD_DIM), F32)],
        name="hg_bwd", compiler_params=_params("arbitrary"),
    )(p, p, p, p, o_all, st_all, dy, lbraw, hg_g)


def _in_bwd(dp, w_in, h0, g1, dh1):
    T = h0.shape[0]
    tm = _row_tile(T, 416)

    def body(dp_ref, w_ref, h_ref, g_ref, d1_ref, dh0_ref, gg_ref):
        i = pl.program_id(0)

        @pl.when(i == 0)
        def _():
            gg_ref[...] = jnp.zeros_like(gg_ref)

        du = _dot_nt(dp_ref[...], w_ref[...])
        h0_ = h_ref[...]
        r = _rms(h0_)
        n = h0_ * r
        gg_ref[...] += jnp.sum(du * n, axis=0, keepdims=True)
        dh0_ref[...] = d1_ref[...] + _rms_bwd(du * g_ref[...], n, r)

    row = lambda n: pl.BlockSpec((tm, n), lambda i: (i, 0))
    return pl.pallas_call(
        body, grid=(T // tm,),
        in_specs=[row(D_IN), pl.BlockSpec((D_MODEL, D_IN), lambda i: (0, 0)),
                  row(D_MODEL), pl.BlockSpec((1, D_MODEL), lambda i: (0, 0)), row(D_MODEL)],
        out_specs=[row(D_MODEL), pl.BlockSpec((1, D_MODEL), lambda i: (0, 0))],
        out_shape=[jax.ShapeDtypeStruct((T, D_MODEL), F32), jax.ShapeDtypeStruct((1, D_MODEL), F32)],
        name="in_bwd", compiler_params=_params("arbitrary"),
    )(dp, w_in, h0, g1, dh1)


def _col_tile(cols, target):
    best = None
    for t in range(128, min(cols, target) + 1, 128):
        if cols % t == 0:
            best = t
    assert best is not None, cols
    return best


MXU_DIM = 256


def _mxu_tile(cols, target):
    best = None
    for t in range(MXU_DIM, min(cols, target) + 1, MXU_DIM):
        if cols % t == 0:
            best = t
    assert best is not None, cols
    return best


def _weight_grad(a, b, name):
    T, M = a.shape
    N = b.shape[1]
    tm = _col_tile(M, 1408)
    tn = _mxu_tile(N, 768 if tm <= 1024 else 512)

    def body(a_ref, b_ref, o_ref):
        o_ref[...] = _dot_tn(a_ref[...], b_ref[...])

    return pl.pallas_call(
        body, grid=(M // tm, N // tn),
        in_specs=[pl.BlockSpec((T, tm), lambda m, n: (0, m)), pl.BlockSpec((T, tn), lambda m, n: (0, n))],
        out_specs=pl.BlockSpec((tm, tn), lambda m, n: (m, n)),
        out_shape=jax.ShapeDtypeStruct((M, N), F32),
        name=name, compiler_params=_params("parallel", "parallel"),
    )(a, b)


def _local_step(h0, target, w_in, w_out, w_gu, w_down, small, on_ffn_grads=None, on_mixer_grads=None):
    wg = _gate_weights(small["w_rgate"], small["w_igate"])
    bg = jnp.concatenate([small["b_rgate"], small["b_igate"]], axis=1)

    p, u = _in_proj(h0, small["mix_norm_g"], w_in)
    y_rg, hs = _rg_fwd(p, small["conv_w"], small["conv_b"], wg, bg, small["lru_lambda"], small["rg_norm_g"])
    y_hg, o_all, st_all = _hg_fwd(p, small["hg_lower_bound"], small["hg_norm_g"])
    h1, v, yb = _out_proj(h0, y_rg, y_hg, w_out, small["ffn_norm_g"])
    gu, act = _gate_up(v, w_gu)
    dh2, dh2b, loss, g_final = _down_loss(h1, act, w_down, small["final_norm_g"], target)

    dgu = _ffn_bwd_act(dh2b, gu, w_down)
    ffn_grads = {"w_gate_up": _weight_grad(v, dgu, "grad_w_gate_up"),
                 "w_down": _weight_grad(act, dh2b, "grad_w_down")}
    stages = on_ffn_grads(ffn_grads) if on_ffn_grads is not None else None
    dh1, dh1b, dy, g_ffn = _ffn_bwd_in(dgu, w_gu, h1, small["ffn_norm_g"], dh2, w_out)
    early = late = None
    if stages is not None:
        chip_sums, send = stages
        sums = chip_sums()
        (dh1, dh1b, dy), sums = lax.optimization_barrier(((dh1, dh1b, dy), sums))
        early = send(sums)
    dp, g_lb, g_hgn = _hg_bwd(p, o_all, st_all, dy, small["hg_lower_bound"], small["hg_norm_g"])
    dp, g_cw, g_cb, g_wgate, g_bg, g_lam, g_rgn = _rg_bwd(
        p, hs, dy, dp, small["conv_w"], small["conv_b"], wg, bg, small["lru_lambda"], small["rg_norm_g"])
    mixer_grads = {"w_in": _weight_grad(u, dp, "grad_w_in"), "w_out": _weight_grad(yb, dh1b, "grad_w_out")}
    if on_mixer_grads is not None:
        chip_sums, send = on_mixer_grads(mixer_grads)
        sums = chip_sums()
        (dp, dh1), sums = lax.optimization_barrier(((dp, dh1), sums))
        late = send(sums)
    dh0, g_mix = _in_bwd(dp, w_in, h0, small["mix_norm_g"], dh1)

    grads = {
        "w_in": mixer_grads["w_in"], "w_out": mixer_grads["w_out"],
        "w_gate_up": ffn_grads["w_gate_up"], "w_down": ffn_grads["w_down"],
        "mix_norm_g": g_mix, "conv_w": g_cw, "conv_b": g_cb, "w_gates": g_wgate,
        "b_rgate": g_bg[:, :D_RG], "b_igate": g_bg[:, D_RG:], "lru_lambda": g_lam, "rg_norm_g": g_rgn,
        "hg_lower_bound": g_lb, "hg_norm_g": g_hgn, "ffn_norm_g": g_ffn, "final_norm_g": g_final,
    }
    return loss, dh0, grads, early, late


ANY = pl.BlockSpec(memory_space=pl.ANY)
HALF = D_MODEL // 2

BIG = {"w_in": (D_MODEL, D_IN // N_CHIPS, True), "w_gate_up": (D_MODEL, 2 * D_FF // N_CHIPS, True),
       "w_out": (D_MODEL // N_CHIPS, D_MODEL, False), "w_down": (D_FF // N_CHIPS, D_MODEL, False)}
BIG_NAMES = tuple(BIG)
N_BIG = len(BIG_NAMES)


def _full_shape(name):
    rows, cols, by_col = BIG[name]
    return (rows, cols * N_CHIPS) if by_col else (rows * N_CHIPS, cols)


def _place():
    return lax.axis_index("x"), lax.axis_index("y"), lax.axis_index("c")


def _chip_of(x, y, r):
    fx, fy = (r + 1) >> 1, (r + 1) & 1
    return (1 - x if fx else x), (1 - y if fy else y)


def _half_of(ref, by_col, half):
    start = pl.multiple_of(half * HALF, 128)
    return ref.at[pl.ds(start, HALF), :] if by_col else ref.at[:, pl.ds(start, HALF)]


def _shard_of(ref, name, chip):
    rows, cols, by_col = BIG[name]
    if by_col:
        return ref.at[:, pl.ds(pl.multiple_of(chip * cols, 128), cols)]
    return ref.at[pl.ds(pl.multiple_of(chip * rows, 16), rows), :]


def _shard_half_of(ref, name, chip, half):
    rows, cols, by_col = BIG[name]
    start = pl.multiple_of(half * HALF, 128)
    if by_col:
        return ref.at[pl.ds(start, HALF), pl.ds(pl.multiple_of(chip * cols, 128), cols)]
    return ref.at[pl.ds(pl.multiple_of(chip * rows, 16), rows), pl.ds(start, HALF)]


def _remote(src, dst, send_sems, recv_sems, k, dev):
    return pltpu.make_async_remote_copy(src_ref=src, dst_ref=dst, send_sem=send_sems.at[k], recv_sem=recv_sems.at[k],
                                        device_id=dev, device_id_type=MESH)


def _cast_into_full(w_shard, name, chip):
    rows, cols, by_col = BIG[name]
    tr = _row_tile(rows, 352)
    if by_col:
        out_spec = pl.BlockSpec((tr, cols), lambda i, s: (i, s[0]))
    else:
        out_spec = pl.BlockSpec((tr, cols), lambda i, s: (s[0] * (rows // tr) + i, 0))

    def body(s_ref, w_ref, o_ref):
        o_ref[...] = w_ref[...].astype(BF16)

    return pl.pallas_call(
        body,
        grid_spec=pltpu.PrefetchScalarGridSpec(
            num_scalar_prefetch=1, grid=(rows // tr,), in_specs=[pl.BlockSpec((tr, cols), lambda i, s: (i, 0))],
            out_specs=out_spec),
        out_shape=jax.ShapeDtypeStruct(_full_shape(name), BF16),
        name="cast_" + name, compiler_params=_params("parallel"),
    )(chip, w_shard)


def _gather_weights(placed, small, names, label, collective_id):
    n, ns = len(names), len(small)
    hbm = pltpu.MemorySpace.HBM
    outs = [jax.new_ref(placed[nm], memory_space=hbm) for nm in names]
    small_in = [jax.new_ref(s, memory_space=hbm) for s in small]
    small_out = [jax.empty_ref(jax.ShapeDtypeStruct((s.shape[0], s.shape[1] * N_CHIPS), F32), memory_space=hbm)
                 for s in small]
    n_sems = 6 * n + 3 * ns

    @pl.kernel(mesh=plsc.ScalarSubcoreMesh(axis_name="seq", num_cores=1), name=label, out_type=(),
               scratch_types=(pltpu.SemaphoreType.DMA((n_sems,)), pltpu.SemaphoreType.DMA((n_sems,)),
                              pltpu.SemaphoreType.DMA((max(ns, 1),))),
               compiler_params=pltpu.CompilerParams(collective_id=collective_id))
    def launch(send_sems, recv_sems, local_sems):
        x, y, c = _place()
        chip = 2 * x + y
        sibling = (x, y, 1 - c)
        others = [_chip_of(x, y, r) for r in range(3)]
        _handshake([(qx, qy, c) for qx, qy in others] + [sibling])

        def small_block(a, q):
            cols = small[a].shape[1]
            return small_out[a].at[:, pl.ds(pl.multiple_of(q * cols, 128), cols)]

        local = [pltpu.make_async_copy(small_in[a], small_block(a, chip), local_sems.at[a]) for a in range(ns)]
        for cp in local:
            cp.start()

        sends = []
        for a, name in enumerate(names):
            mine = _shard_half_of(outs[a], name, chip, c)
            for r, (qx, qy) in enumerate(others):
                sends.append(_remote(mine, mine, send_sems, recv_sems, 6 * a + r, (qx, qy, c)))
        for a in range(ns):
            for r, (qx, qy) in enumerate(others):
                sends.append(_remote(small_in[a], small_block(a, chip), send_sems, recv_sems,
                                     6 * n + 3 * a + r, (qx, qy, c)))
        for cp in sends:
            cp.start()

        forwards = []
        for a, name in enumerate(names):
            for r, (qx, qy) in enumerate(others):
                landed = _shard_half_of(outs[a], name, 2 * qx + qy, c)
                _remote(landed, landed, send_sems, recv_sems, 6 * a + r, (qx, qy, c)).wait_recv()
                fwd = _remote(landed, landed, send_sems, recv_sems, 6 * a + 3 + r, sibling)
                fwd.start()
                forwards.append(fwd)
        for a in range(ns):
            for r, (qx, qy) in enumerate(others):
                landed = small_block(a, 2 * qx + qy)
                _remote(landed, landed, send_sems, recv_sems, 6 * n + 3 * a + r, (qx, qy, c)).wait_recv()
        for a, name in enumerate(names):
            for r, (qx, qy) in enumerate(others):
                landed = _shard_half_of(outs[a], name, 2 * qx + qy, 1 - c)
                _remote(landed, landed, send_sems, recv_sems, 6 * a + 3 + r, sibling).wait_recv()
        for cp in sends + forwards:
            cp.wait_send()
        for cp in local:
            cp.wait()

    launch()
    return {nm: ref[...] for nm, ref in zip(names, outs)}, [ref[...] for ref in small_out]


def _exchange_halves(grads, names, label, collective_id):
    n = len(names)
    sequencer = collective_id is not None

    def body(*refs):
        ins, outs = refs[:n], refs[n:2 * n]
        send_sems, recv_sems = refs[2 * n:]
        x, y, c = _place()
        if sequencer:
            _handshake([(x, y, 1 - c)])
        copies = []
        for a, name in enumerate(names):
            copies.append(_remote(_half_of(ins[a], BIG[name][2], 1 - c), outs[a], send_sems, recv_sems, a,
                                  (x, y, 1 - c)))
        for cp in copies:
            cp.start()
        for cp in copies:
            cp.wait()

    def half_shape(name):
        r, c_ = _full_shape(name)
        return (HALF, c_) if BIG[name][2] else (r, HALF)

    out_type = tuple(jax.ShapeDtypeStruct(half_shape(nm), F32) for nm in names)
    sems = (pltpu.SemaphoreType.DMA((n,)), pltpu.SemaphoreType.DMA((n,)))
    operands = [grads[nm] for nm in names]
    if sequencer:
        got = pl.kernel(
            body, mesh=plsc.ScalarSubcoreMesh(axis_name="seq", num_cores=1), name=label, out_type=out_type,
            scratch_types=sems, compiler_params=pltpu.CompilerParams(collective_id=collective_id),
        )(*operands)
    else:
        got = pl.pallas_call(
            body, in_specs=[ANY] * n, out_specs=[ANY] * n, out_shape=list(out_type), scratch_shapes=list(sems),
            name=label,
        )(*operands)
    return dict(zip(names, got))


def _chip_sum(g, got, name, core):
    by_col = BIG[name][2]
    rows, cols = got.shape
    if by_col:
        tr = 128
        g_spec = pl.BlockSpec((tr, cols), lambda i, s: (s[0] * (HALF // tr) + i, 0))
    else:
        tr = _row_tile(rows, 512)
        g_spec = pl.BlockSpec((tr, HALF), lambda i, s: (i, s[0]))
    blk = pl.BlockSpec((tr, cols), lambda i, s: (i, 0))

    def body(s_ref, g_ref, r_ref, f_ref, b_ref):
        t = g_ref[...] + r_ref[...]
        f_ref[...] = t
        b_ref[...] = t.astype(BF16)

    return pl.pallas_call(
        body,
        grid_spec=pltpu.PrefetchScalarGridSpec(num_scalar_prefetch=1, grid=(rows // tr,), in_specs=[g_spec, blk],
                                               out_specs=[blk, blk]),
        out_shape=[jax.ShapeDtypeStruct(got.shape, F32), jax.ShapeDtypeStruct(got.shape, BF16)],
        name="chip_sum_" + name, compiler_params=_params("parallel"),
    )(core, g, got)


def _piece_shape(name):
    rows, cols, by_col = BIG[name]
    return (HALF, cols) if by_col else (rows, HALF)


def _handshake(peers):
    barrier = pltpu.get_barrier_semaphore()
    for peer in peers:
        pl.semaphore_signal(barrier, inc=1, device_id=peer, device_id_type=MESH)
    pl.semaphore_wait(barrier, len(peers))


def _send_chip_sums(sums, names, label, collective_id):
    n = len(names)

    def body(*refs):
        ins, outs = refs[:n], refs[n:2 * n]
        send_sems, recv_sems = refs[2 * n:]
        x, y, c = _place()
        others = [_chip_of(x, y, r) for r in range(3)]
        _handshake([(qx, qy, c) for qx, qy in others])
        copies = []
        for a, name in enumerate(names):
            for r, (qx, qy) in enumerate(others):
                copies.append(_remote(_shard_of(ins[a], name, 2 * qx + qy), outs[a].at[r], send_sems, recv_sems,
                                      3 * a + r, (qx, qy, c)))
        for cp in copies:
            cp.start()
        for cp in copies:
            cp.wait()

    return pl.kernel(
        body, mesh=plsc.ScalarSubcoreMesh(axis_name="seq", num_cores=1), name=label,
        out_type=tuple(jax.ShapeDtypeStruct((3,) + _piece_shape(nm), BF16) for nm in names),
        scratch_types=(pltpu.SemaphoreType.DMA((3 * n,)), pltpu.SemaphoreType.DMA((3 * n,))),
        compiler_params=pltpu.CompilerParams(collective_id=collective_id),
    )(*[sums[nm] for nm in names])


def _total(own, got, name, chip_core):
    rows, cols, by_col = BIG[name]
    pr, pc = _piece_shape(name)
    tr = _row_tile(pr, 352)
    if by_col:
        own_spec = pl.BlockSpec((tr, pc), lambda i, s: (i, s[0]))
    else:
        own_spec = pl.BlockSpec((tr, pc), lambda i, s: (s[0] * (pr // tr) + i, 0))
    got_spec = lambda r: pl.BlockSpec((None, tr, pc), lambda i, s: (r, i, 0))
    if by_col:
        out_spec = pl.BlockSpec((tr, pc), lambda i, s: (s[1] * (pr // tr) + i, 0))
    else:
        out_spec = pl.BlockSpec((tr, pc), lambda i, s: (i, s[1]))

    def body(s_ref, o_ref, a_ref, b_ref, c_ref, t_ref):
        t_ref[...] = ((o_ref[...] + a_ref[...].astype(F32)) + b_ref[...].astype(F32)) + c_ref[...].astype(F32)

    return pl.pallas_call(
        body,
        grid_spec=pltpu.PrefetchScalarGridSpec(
            num_scalar_prefetch=1, grid=(pr // tr,), in_specs=[own_spec, got_spec(0), got_spec(1), got_spec(2)],
            out_specs=out_spec),
        out_shape=jax.ShapeDtypeStruct((rows, cols), F32),
        name="total_" + name, compiler_params=_params("parallel"),
    )(chip_core, own, got, got, got)


def _share_totals(totals):
    def body(*refs):
        outs = refs[N_BIG:2 * N_BIG]
        send_sems, recv_sems = refs[2 * N_BIG:]
        x, y, c = _place()
        copies = []
        for a, name in enumerate(BIG_NAMES):
            mine = _half_of(outs[a], BIG[name][2], c)
            copies.append(_remote(mine, mine, send_sems, recv_sems, a, (x, y, 1 - c)))
        for cp in copies:
            cp.start()
        for a, name in enumerate(BIG_NAMES):
            theirs = _half_of(outs[a], BIG[name][2], 1 - c)
            _remote(theirs, theirs, send_sems, recv_sems, a, (x, y, 1 - c)).wait_recv()
        for cp in copies:
            cp.wait_send()

    return pl.pallas_call(
        body, in_specs=[ANY] * N_BIG, out_specs=[ANY] * N_BIG,
        out_shape=[jax.ShapeDtypeStruct(BIG[n][:2], F32) for n in BIG_NAMES],
        input_output_aliases={a: a for a in range(N_BIG)},
        scratch_shapes=[pltpu.SemaphoreType.DMA((N_BIG,)), pltpu.SemaphoreType.DMA((N_BIG,))],
        name="share_totals",
    )(*[totals[n] for n in BIG_NAMES])


VEC_ROWS = 32
VEC_ROW = {"mix_norm_g": 0, "conv_b": 1, "b_rgate": 2, "b_igate": 3, "lru_lambda": 4, "rg_norm_g": 5,
           "hg_lower_bound": 6, "hg_norm_g": 8, "ffn_norm_g": 9, "final_norm_g": 10, "loss": 11,
           "conv_w": 12, "meta_tokens": 16}
N_DEV = 8


def _all_reduce_small(pieces, gates):
    names = list(pieces)
    hv, hg = VEC_ROWS // 2, gates.shape[0] // 2

    def body(*refs):
        ins = refs[:len(names)]
        (g_ref, vec_ref, gsum_ref, mine_v, sib_v, sib_g, chip_v, chip_g, got_v, got_g,
         send_sems, recv_sems) = refs[len(names):]
        x, y, c = _place()
        chip = 2 * x + y
        sibling = (x, y, 1 - c)
        mine_v[...] = jnp.zeros_like(mine_v)
        for name, ref in zip(names, ins):
            nr, w = ref.shape
            mine_v[VEC_ROW[name]:VEC_ROW[name] + nr, 0:w] = ref[...]

        swap = [_remote(mine_v, sib_v, send_sems, recv_sems, 0, sibling),
                _remote(g_ref, sib_g, send_sems, recv_sems, 1, sibling)]
        for cp in swap:
            cp.start()
        for cp in swap:
            cp.wait()
        chip_v[...] = mine_v[...] + sib_v[...]
        chip_g[...] = g_ref[...] + sib_g[...]

        rows_v = pl.ds(pl.multiple_of(c * hv, 8), hv)
        rows_g = pl.ds(pl.multiple_of(c * hg, 8), hg)
        got_v[chip] = chip_v[rows_v, :]
        got_g[chip] = chip_g[rows_g, :]
        sends = []
        for r in range(3):
            qx, qy = _chip_of(x, y, r)
            sends.append(_remote(chip_v.at[rows_v, :], got_v.at[chip], send_sems, recv_sems, 2 + r, (qx, qy, c)))
            sends.append(_remote(chip_g.at[rows_g, :], got_g.at[chip], send_sems, recv_sems, 5 + r, (qx, qy, c)))
        for cp in sends:
            cp.start()
        for cp in sends:
            cp.wait()
        vec_ref[rows_v, :] = ((got_v[0] + got_v[1]) + got_v[2]) + got_v[3]
        gsum_ref[rows_g, :] = ((got_g[0] + got_g[1]) + got_g[2]) + got_g[3]

        back = [_remote(vec_ref.at[rows_v, :], vec_ref.at[rows_v, :], send_sems, recv_sems, 8, sibling),
                _remote(gsum_ref.at[rows_g, :], gsum_ref.at[rows_g, :], send_sems, recv_sems, 9, sibling)]
        for cp in back:
            cp.start()
        theirs_v = vec_ref.at[pl.ds(pl.multiple_of((1 - c) * hv, 8), hv), :]
        theirs_g = gsum_ref.at[pl.ds(pl.multiple_of((1 - c) * hg, 8), hg), :]
        _remote(theirs_v, theirs_v, send_sems, recv_sems, 8, sibling).wait_recv()
        _remote(theirs_g, theirs_g, send_sems, recv_sems, 9, sibling).wait_recv()
        for cp in back:
            cp.wait_send()

    vmem = pl.BlockSpec(memory_space=pltpu.VMEM)
    n_sems = 10
    return pl.pallas_call(
        body, in_specs=[vmem] * (len(names) + 1), out_specs=[vmem, vmem],
        out_shape=[jax.ShapeDtypeStruct((VEC_ROWS, D_MODEL), F32), jax.ShapeDtypeStruct(gates.shape, F32)],
        scratch_shapes=[pltpu.VMEM((VEC_ROWS, D_MODEL), F32), pltpu.VMEM((VEC_ROWS, D_MODEL), F32),
                        pltpu.VMEM(gates.shape, F32), pltpu.VMEM((VEC_ROWS, D_MODEL), F32),
                        pltpu.VMEM(gates.shape, F32), pltpu.VMEM((N_CHIPS, hv, D_MODEL), F32),
                        pltpu.VMEM((N_CHIPS, hg) + gates.shape[1:], F32),
                        pltpu.SemaphoreType.DMA((n_sems,)), pltpu.SemaphoreType.DMA((n_sems,))],
        name="all_reduce_small",
    )(*[pieces[n] for n in names], gates)


def _adamw_math(w, g, m, v):
    m = ADAM_B1 * m + (1.0 - ADAM_B1) * g
    v = ADAM_B2 * v + (1.0 - ADAM_B2) * (g * g)
    m_hat = m / (1.0 - ADAM_B1 ** ADAM_STEP)
    v_hat = v / (1.0 - ADAM_B2 ** ADAM_STEP)
    delta = -ADAM_LR * (m_hat / (jnp.sqrt(v_hat) + ADAM_EPS) + ADAM_WD * w)
    return delta, m, v


def _adamw_big(w, g, m, v, name):
    rows, cols = w.shape
    tr = _row_tile(rows, 352)

    def body(w_ref, g_ref, m_ref, v_ref, d_ref, nm_ref, nv_ref):
        d_ref[...], nm_ref[...], nv_ref[...] = _adamw_math(w_ref[...], g_ref[...], m_ref[...], v_ref[...])

    blk = pl.BlockSpec((tr, cols), lambda i: (i, 0))
    return pl.pallas_call(
        body, grid=(rows // tr,), in_specs=[blk] * 4, out_specs=[blk] * 3,
        out_shape=[jax.ShapeDtypeStruct(w.shape, F32)] * 3,
        name="adamw_" + name, compiler_params=_params("parallel"),
    )(w, g, m, v)


SMALL = {"meta_tokens": (N_META, D_MODEL // N_CHIPS), "mix_norm_g": (1, D_MODEL), "conv_w": (CONV_W, D_RG // N_CHIPS),
         "conv_b": (1, D_RG), "w_rgate": (D_RG, RG_HEAD_DIM), "b_rgate": (1, D_RG), "w_igate": (D_RG, RG_HEAD_DIM),
         "b_igate": (1, D_RG), "lru_lambda": (1, D_RG), "rg_norm_g": (1, D_RG), "hg_lower_bound": (2, D_HG),
         "hg_norm_g": (1, HG_HEAD_DIM), "ffn_norm_g": (1, D_MODEL), "final_norm_g": (1, D_MODEL)}
SMALL_NAMES = tuple(SMALL)
SHARDED_SMALL = ("meta_tokens", "conv_w")


def _adamw_small(vec, gates, w, m, v):
    n = len(SMALL_NAMES)

    def body(*refs):
        vec_ref, gates_ref = refs[:2]
        w_refs, m_refs, v_refs = refs[2:2 + n], refs[2 + n:2 + 2 * n], refs[2 + 2 * n:2 + 3 * n]
        outs = refs[2 + 3 * n:]
        loss_ref = outs[0]
        x, y, _ = _place()
        chip = 2 * x + y
        loss_ref[...] = vec_ref[VEC_ROW["loss"]:VEC_ROW["loss"] + 1, 0:1]

        def update(k, g):
            g_ref, d_ref, nm_ref, nv_ref = outs[1 + 4 * k:5 + 4 * k]
            g_ref[...] = g
            d_ref[...], nm_ref[...], nv_ref[...] = _adamw_math(w_refs[k][...], g, m_refs[k][...], v_refs[k][...])

        for k, name in enumerate(SMALL_NAMES):
            nr, w_ = SMALL[name]
            if name == "w_rgate":
                update(k, gates_ref[0:D_RG, :])
            elif name == "w_igate":
                update(k, gates_ref[D_RG:2 * D_RG, :])
            elif name in SHARDED_SMALL:
                r0 = VEC_ROW[name]
                for q in range(N_CHIPS):
                    @pl.when(chip == q)
                    def _(k=k, r0=r0, nr=nr, w_=w_, q=q):
                        update(k, vec_ref[r0:r0 + nr, q * w_:(q + 1) * w_])
            else:
                r0 = VEC_ROW[name]
                update(k, vec_ref[r0:r0 + nr, 0:w_])

    vmem = pl.BlockSpec(memory_space=pltpu.VMEM)
    out_shape = [jax.ShapeDtypeStruct((1, 1), F32)]
    for name in SMALL_NAMES:
        out_shape += [jax.ShapeDtypeStruct(SMALL[name], F32)] * 4
    outs = pl.pallas_call(
        body, in_specs=[vmem] * (2 + 3 * n), out_specs=[vmem] * len(out_shape), out_shape=out_shape,
        name="adamw_small",
    )(vec, gates, *[w[k] for k in SMALL_NAMES], *[m[k] for k in SMALL_NAMES], *[v[k] for k in SMALL_NAMES])
    loss = outs[0]
    res = {name: tuple(outs[1 + 4 * k:5 + 4 * k]) for k, name in enumerate(SMALL_NAMES)}
    return loss, res


WEIGHT_NAMES = ("meta_tokens", "mix_norm_g", "w_in", "conv_w", "conv_b", "w_rgate", "b_rgate", "w_igate", "b_igate",
                "lru_lambda", "rg_norm_g", "hg_lower_bound", "hg_norm_g", "w_out", "ffn_norm_g", "w_gate_up", "w_down",
                "final_norm_g")


def _to_2d(name, a):
    if name in BIG:
        return a.reshape(BIG[name][:2])
    return a.reshape(SMALL[name])


def kernel(x, meta_tokens, mix_norm_g, w_in, conv_w, conv_b, w_rgate, b_rgate, w_igate, b_igate, lru_lambda, rg_norm_g, hg_lower_bound, hg_norm_g, w_out, ffn_norm_g, w_gate_up, w_down, final_norm_g, loss_target, m_meta_tokens, m_mix_norm_g, m_w_in, m_conv_w, m_conv_b, m_w_rgate, m_b_rgate, m_w_igate, m_b_igate, m_lru_lambda, m_rg_norm_g, m_hg_lower_bound, m_hg_norm_g, m_w_out, m_ffn_norm_g, m_w_gate_up, m_w_down, m_final_norm_g, v_meta_tokens, v_mix_norm_g, v_w_in, v_conv_w, v_conv_b, v_w_rgate, v_b_rgate, v_w_igate, v_b_igate, v_lru_lambda, v_rg_norm_g, v_hg_lower_bound, v_hg_norm_g, v_w_out, v_ffn_norm_g, v_w_gate_up, v_w_down, v_final_norm_g):
    w_raw = dict(zip(WEIGHT_NAMES, (meta_tokens, mix_norm_g, w_in, conv_w, conv_b, w_rgate, b_rgate, w_igate, b_igate,
                                    lru_lambda, rg_norm_g, hg_lower_bound, hg_norm_g, w_out, ffn_norm_g, w_gate_up,
                                    w_down, final_norm_g)))
    m_raw = dict(zip(WEIGHT_NAMES, (m_meta_tokens, m_mix_norm_g, m_w_in, m_conv_w, m_conv_b, m_w_rgate, m_b_rgate,
                                    m_w_igate, m_b_igate, m_lru_lambda, m_rg_norm_g, m_hg_lower_bound, m_hg_norm_g,
                                    m_w_out, m_ffn_norm_g, m_w_gate_up, m_w_down, m_final_norm_g)))
    v_raw = dict(zip(WEIGHT_NAMES, (v_meta_tokens, v_mix_norm_g, v_w_in, v_conv_w, v_conv_b, v_w_rgate, v_b_rgate,
                                    v_w_igate, v_b_igate, v_lru_lambda, v_rg_norm_g, v_hg_lower_bound, v_hg_norm_g,
                                    v_w_out, v_ffn_norm_g, v_w_gate_up, v_w_down, v_final_norm_g)))
    w = {k: _to_2d(k, a) for k, a in w_raw.items()}
    m = {k: _to_2d(k, a) for k, a in m_raw.items()}
    v = {k: _to_2d(k, a) for k, a in v_raw.items()}

    x_i, y_i, c_i = _place()
    core = jnp.reshape(c_i, (1,)).astype(jnp.int32)
    chip = jnp.reshape(2 * x_i + y_i, (1,)).astype(jnp.int32)
    chip_core = jnp.concatenate([chip, core])

    placed = {k: _cast_into_full(w[k], k, chip) for k in BIG_NAMES}
    first, (meta_full, cw_full) = _gather_weights(placed, [w["meta_tokens"], w["conv_w"]], ("w_in",), "gather_first", 1)
    rest, _ = _gather_weights(placed, [], ("w_out", "w_gate_up", "w_down"), "gather_rest", 2)
    full = {**first, **rest}

    seq = x.shape[1]
    h0 = jnp.concatenate([jnp.zeros((PAD, D_MODEL), F32), meta_full, x[0]], axis=0)
    target = jnp.concatenate([jnp.zeros((PAD + N_META, D_MODEL), F32), loss_target[0]], axis=0)
    small = {k: w[k] for k in SMALL_NAMES if k not in SHARDED_SMALL}
    small["conv_w"] = cw_full

    def reduce_to_chips(grads, names, tag, collective_ids):
        got = _exchange_halves(grads, names, "exchange_halves_" + tag, collective_ids[0])

        def chip_sums():
            return {n: _chip_sum(grads[n], got[n], n, core) for n in names}

        def send(sums):
            arrived = _send_chip_sums({n: sums[n][1] for n in names}, names, "send_chip_sums_" + tag,
                                      collective_ids[1])
            return {n: (sums[n][0], a) for n, a in zip(names, arrived)}

        return chip_sums, send

    ffn_names, mixer_names = ("w_gate_up", "w_down"), ("w_in", "w_out")
    loss, dh0, grads, parts, parts_mixer = _local_step(
        h0, target, full["w_in"], full["w_out"], full["w_gate_up"], full["w_down"], small,
        on_ffn_grads=lambda g: reduce_to_chips(g, ffn_names, "ffn", (3, 4)),
        on_mixer_grads=lambda g: reduce_to_chips(g, mixer_names, "mixer", (None, 5)))
    parts.update(parts_mixer)
    totals = {n: _total(parts[n][0], parts[n][1], n, chip_core) for n in BIG_NAMES}
    g_big = dict(zip(BIG_NAMES, _share_totals(totals)))

    pieces = {k: grads[k] for k in VEC_ROW if k not in ("loss", "meta_tokens")}
    pieces["loss"] = loss
    pieces["meta_tokens"] = dh0[PAD:PAD + N_META]
    vec, gates = _all_reduce_small(pieces, grads["w_gates"])
    loss_sum, res = _adamw_small(vec, gates, w, m, v)
    for n in BIG_NAMES:
        res[n] = (g_big[n],) + tuple(_adamw_big(w[n], g_big[n], m[n], v[n], n))

    grad_x = dh0[PAD + N_META:].reshape(1, seq, D_MODEL)
    out = [loss_sum.reshape(()), grad_x]
    for j in range(4):
        out += [res[n][j].reshape(w_raw[n].shape) for n in WEIGHT_NAMES]
    return tuple(out)
```

```python
import functools
import math

import jax
import jax.numpy as jnp
from jax import lax
from jax.experimental import pallas as pl
from jax.experimental.pallas import tpu as pltpu
from jax.experimental.pallas import tpu_sc as plsc

F32 = jnp.float32
BF16 = jnp.bfloat16
HIGHEST = lax.Precision.HIGHEST
MESH = pl.DeviceIdType.MESH

D_MODEL = 1024
D_RG = 512
RG_HEAD_DIM = 64
D_HG = 512
HG_HEAD_DIM = 128
HG_HEADS = 4
CHUNK = 64
SUB = 16
N_SUB = CHUNK // SUB
N_META = 16
PAD = CHUNK - N_META
D_IN = 3072
D_FF = 2816
CONV_W = 4
LRU_C = 8.0
EPS = 1e-6
EXP_CLAMP = 80.0
GELU_C = math.sqrt(2.0 / math.pi)
GELU_A = 0.044715
N_CHIPS = 4

ADAM_LR = 0.001
ADAM_B1 = 0.9
ADAM_B2 = 0.999
ADAM_EPS = 1e-08
ADAM_WD = 0.01
ADAM_STEP = 10

VMEM_LIMIT = 56 * 1024 * 1024


def _params(*sem):
    return pltpu.CompilerParams(dimension_semantics=sem, vmem_limit_bytes=VMEM_LIMIT)


def _row_tile(rows, target):
    best = None
    for t in range(16, min(rows, target) + 1, 16):
        if rows % t == 0:
            best = t
    assert best is not None, rows
    return best


def _sigmoid(x):
    return 0.5 * jnp.tanh(0.5 * x) + 0.5


def _dot(a, b):
    return jnp.dot(a, b, preferred_element_type=F32)


def _dot_nt(a, b):
    return lax.dot_general(a, b, (((1,), (1,)), ((), ())), preferred_element_type=F32)


def _dot_tn(a, b):
    return lax.dot_general(a, b, (((0,), (0,)), ((), ())), preferred_element_type=F32)


def _rms(x):
    return lax.rsqrt(jnp.mean(x * x, axis=-1, keepdims=True) + EPS)


def _rms_bwd(dn, n, r):
    return r * (dn - n * jnp.mean(dn * n, axis=-1, keepdims=True))


def _gelu_parts(x):
    t = jnp.tanh(GELU_C * (x + GELU_A * x * x * x))
    g = 0.5 * x * (1.0 + t)
    dg = 0.5 * (1.0 + t) + 0.5 * x * (1.0 - t * t) * GELU_C * (1.0 + 3.0 * GELU_A * x * x)
    return g, dg


def _softplus_neg(lam):
    e = jnp.exp(-jnp.abs(lam))
    w = 1.0 + e
    log1p = jnp.where(w == 1.0, e, jnp.log(w) * e / (w - 1.0))
    return jnp.maximum(-lam, 0.0) + log1p


def _head_mask():
    r = lax.broadcasted_iota(jnp.int32, (D_RG, D_RG), 0) // RG_HEAD_DIM
    c = lax.broadcasted_iota(jnp.int32, (D_RG, D_RG), 1) // RG_HEAD_DIM
    return r == c


def _head_fold():
    r = lax.broadcasted_iota(jnp.int32, (D_RG, RG_HEAD_DIM), 0) % RG_HEAD_DIM
    c = lax.broadcasted_iota(jnp.int32, (D_RG, RG_HEAD_DIM), 1)
    return (r == c).astype(F32)


def _gate_weights(w_r, w_i):
    def body(wr_ref, wi_ref, o_ref):
        fold = _head_fold()
        mask = _head_mask()
        for k, ref in enumerate((wr_ref, wi_ref)):
            full = lax.dot_general(ref[...], fold, (((1,), (1,)), ((), ())),
                                   precision=HIGHEST, preferred_element_type=F32)
            o_ref[:, k * D_RG:(k + 1) * D_RG] = jnp.where(mask, full, 0.0).astype(BF16)

    return pl.pallas_call(
        body, out_shape=jax.ShapeDtypeStruct((D_RG, 2 * D_RG), BF16), name="gate_weights",
    )(w_r, w_i)


def _in_proj(h0, g1, w_in):
    T = h0.shape[0]
    tm = _row_tile(T, 416)

    def body(h_ref, g_ref, w_ref, p_ref, u_ref):
        h = h_ref[...]
        u = (h * _rms(h) * g_ref[...]).astype(BF16)
        u_ref[...] = u
        p_ref[...] = _dot(u, w_ref[...])

    return pl.pallas_call(
        body, grid=(T // tm,),
        in_specs=[pl.BlockSpec((tm, D_MODEL), lambda i: (i, 0)),
                  pl.BlockSpec((1, D_MODEL), lambda i: (0, 0)),
                  pl.BlockSpec((D_MODEL, D_IN), lambda i: (0, 0))],
        out_specs=[pl.BlockSpec((tm, D_IN), lambda i: (i, 0)),
                   pl.BlockSpec((tm, D_MODEL), lambda i: (i, 0))],
        out_shape=[jax.ShapeDtypeStruct((T, D_IN), F32), jax.ShapeDtypeStruct((T, D_MODEL), BF16)],
        name="in_proj", compiler_params=_params("parallel"),
    )(h0, g1, w_in)


def _scan_block_fwd(A, B, rowi):
    for d in (1, 2, 4):
        a_sh = pltpu.roll(A, d, axis=0)
        b_sh = pltpu.roll(B, d, axis=0)
        m = rowi >= d
        B = jnp.where(m, A * b_sh + B, B)
        A = jnp.where(m, A * a_sh, A)
    return A, B


def _scan_block_bwd(A, B, rowi):
    for d in (1, 2, 4):
        a_sh = pltpu.roll(A, 8 - d, axis=0)
        b_sh = pltpu.roll(B, 8 - d, axis=0)
        m = rowi < 8 - d
        B = jnp.where(m, A * b_sh + B, B)
        A = jnp.where(m, A * a_sh, A)
    return A, B


def _rg_gates(xc, w_ref, bg_ref, lam):
    pre = _dot(xc.astype(BF16), w_ref[...]) + bg_ref[...]
    r = _sigmoid(pre[:, :D_RG])
    ig = _sigmoid(pre[:, D_RG:])
    sp = _softplus_neg(lam)
    la = -LRU_C * sp * r
    a = jnp.exp(la)
    th = jnp.tanh(la)
    u = 1.0 - th
    rc = pl.reciprocal(u, approx=True)
    rc = rc * (2.0 - u * rc)
    rc = rc * (2.0 - u * rc)
    m2 = -2.0 * th * rc
    inv_m = lax.rsqrt(jnp.maximum(m2, 1e-30))
    return r, ig, sp, a, m2 * inv_m, inv_m


def _conv(ext, cw_ref, cb_ref, tm):
    xc = cb_ref[...] + cw_ref[0:1, :] * ext[8 - 3:8 - 3 + tm, :]
    for j in range(1, CONV_W):
        xc = xc + cw_ref[j:j + 1, :] * ext[8 - 3 + j:8 - 3 + j + tm, :]
    return xc


def _scan_unroll(blocks):
    return 4 if blocks % 4 == 0 else 2 if blocks % 2 == 0 else 1


def _rg_fwd(p, cw, cb, wg, bg, lam, rg_g):
    T = p.shape[0]
    tm = _row_tile(T, 416)
    unroll = _scan_unroll(tm // 8)

    def body(xg_ref, cw_ref, cb_ref, w_ref, bg_ref, lam_ref, g_ref, y_ref, h_ref, ext, a_s, b_s, carry):
        i = pl.program_id(0)

        @pl.when(i == 0)
        def _():
            ext[0:8, :] = jnp.zeros((8, D_RG), F32)
            carry[...] = jnp.zeros((1, D_RG), F32)

        ext[8:8 + tm, :] = xg_ref[:, :D_RG]
        xc = _conv(ext, cw_ref, cb_ref, tm)
        r, ig, sp, a, m, _ = _rg_gates(xc, w_ref, bg_ref, lam_ref[...])
        row = i * tm + lax.broadcasted_iota(jnp.int32, (tm, 1), 0)
        a_s[...] = a
        b_s[...] = jnp.where(row >= PAD, m * ig * xc, 0.0)
        rowi = lax.broadcasted_iota(jnp.int32, (8, D_RG), 0)

        def blk(j, c):
            for u in range(unroll):
                o = pl.multiple_of((j * unroll + u) * 8, 8)
                A, B = _scan_block_fwd(a_s[pl.ds(o, 8), :], b_s[pl.ds(o, 8), :], rowi)
                h = B + A * c
                h_ref[pl.ds(o, 8), :] = h
                c = h[7:8, :]
            return c

        carry[...] = lax.fori_loop(0, tm // (8 * unroll), blk, carry[...])
        ext[0:8, :] = ext[tm:tm + 8, :]
        g, _ = _gelu_parts(xg_ref[:, D_RG:])
        yy = g * h_ref[...]
        y_ref[...] = (yy * _rms(yy) * g_ref[...]).astype(BF16)

    vec = lambda n: pl.BlockSpec((1, n), lambda i: (0, 0))
    return pl.pallas_call(
        body, grid=(T // tm,),
        in_specs=[pl.BlockSpec((tm, 2 * D_RG), lambda i: (i, 0)),
                  pl.BlockSpec((CONV_W, D_RG), lambda i: (0, 0)), vec(D_RG),
                  pl.BlockSpec((D_RG, 2 * D_RG), lambda i: (0, 0)), vec(2 * D_RG), vec(D_RG), vec(D_RG)],
        out_specs=[pl.BlockSpec((tm, D_RG), lambda i: (i, 0)), pl.BlockSpec((tm, D_RG), lambda i: (i, 0))],
        out_shape=[jax.ShapeDtypeStruct((T, D_RG), BF16), jax.ShapeDtypeStruct((T, D_RG), F32)],
        scratch_shapes=[pltpu.VMEM((tm + 8, D_RG), F32), pltpu.VMEM((tm, D_RG), F32),
                        pltpu.VMEM((tm, D_RG), F32), pltpu.VMEM((1, D_RG), F32)],
        name="rg_fwd", compiler_params=_params("arbitrary"),
    )(p, cw, cb, wg, bg, lam, rg_g)


def _tri(lower):
    r = lax.broadcasted_iota(jnp.int32, (CHUNK, CHUNK), 0)
    c = lax.broadcasted_iota(jnp.int32, (CHUNK, CHUNK), 1)
    return ((c <= r) if lower else (c >= r)).astype(F32)


def _hg_gates(hq, hf, lbraw_ref, valid):
    lb = _sigmoid(lbraw_ref[0:1, :] - lbraw_ref[1:2, :])
    sq = _sigmoid(hq)
    q = hq * sq
    sf = _sigmoid(hf)
    f = lb + (1.0 - lb) * sf
    lf = jnp.where(valid, jnp.log(f), 0.0)
    b = jnp.dot(_tri(True), lf, precision=HIGHEST, preferred_element_type=F32)
    return lb, sq, q, sf, f, b


def _hg_head(qh, kh, bh):
    blk = lax.broadcasted_iota(jnp.int32, (CHUNK, 1), 0) // SUB
    b_last = bh[CHUNK - 1:CHUNK, :]
    refs = [bh[SUB * s:SUB * s + 1, :] for s in range(N_SUB)]
    r_sel = refs[N_SUB - 1]
    for s in range(N_SUB - 2, -1, -1):
        r_sel = jnp.where(blk == s, refs[s], r_sel)
    eb = jnp.exp(bh)
    eq = jnp.exp(bh - r_sel)
    ekh = jnp.exp(b_last - bh)
    ek = [jnp.exp(jnp.minimum(refs[s] - bh, EXP_CLAMP)) for s in range(N_SUB)]
    qe = qh * eq
    q_hat = jnp.concatenate([jnp.where(blk == s, qe, 0.0) for s in range(N_SUB)], axis=1)
    k_til = jnp.concatenate([kh * ek[s] for s in range(N_SUB)], axis=1)
    return blk, b_last, eb, eq, ekh, ek, q_hat, k_til


def _causal():
    r = lax.broadcasted_iota(jnp.int32, (CHUNK, CHUNK), 0)
    c = lax.broadcasted_iota(jnp.int32, (CHUNK, CHUNK), 1)
    return r >= c


def _chunks_per_step(n_chunks):
    for c in (5, 4, 3, 2):
        if n_chunks % c == 0:
            return c
    return 1


def _hg_fwd(p, lbraw, hg_g):
    T = p.shape[0]
    n_chunks = T // CHUNK
    cps = _chunks_per_step(n_chunks)
    rows = cps * CHUNK

    def body(hq_ref, hf_ref, hi_ref, hg_ref, lb_ref, g_ref, y_ref, o_ref, st_all_ref, st):
        i = pl.program_id(0)

        @pl.when(i == 0)
        def _():
            st[...] = jnp.zeros_like(st)

        def chunk(j, carry):
            rs = pl.ds(pl.multiple_of(j * CHUNK, CHUNK), CHUNK)
            chunk_body(i * cps + j, hq_ref.at[rs, :], hf_ref.at[rs, :], hi_ref.at[rs, :], hg_ref.at[rs, :], lb_ref,
                       g_ref, y_ref.at[rs, :], o_ref.at[rs, :], st_all_ref.at[pl.ds(j, 1)], st)
            return carry

        lax.fori_loop(0, cps, chunk, 0, unroll=True)

    def chunk_body(n, hq_ref, hf_ref, hi_ref, hg_ref, lb_ref, g_ref, y_ref, o_ref, st_all_ref, st):
        valid = (n * CHUNK + lax.broadcasted_iota(jnp.int32, (CHUNK, 1), 0)) >= PAD
        hq, hf, v, hg = hq_ref[...], hf_ref[...], hi_ref[...], hg_ref[...]
        lb, sq, q, sf, f, b = _hg_gates(hq, hf, lb_ref, valid)
        k = 1.0 - f
        st_all_ref[0] = st[...]
        causal = _causal()
        v_t = v.T.astype(BF16)
        heads = [slice(h * HG_HEAD_DIM, (h + 1) * HG_HEAD_DIM) for h in range(HG_HEADS)]
        fac = []
        for sl in heads:
            qh, kh, bh = q[:, sl], k[:, sl], b[:, sl]
            _, b_last, eb, _, ekh, _, q_hat, k_til = _hg_head(qh, kh, bh)
            fac.append((jnp.exp(b_last), (qh * eb).astype(BF16), q_hat.astype(BF16), k_til.astype(BF16),
                        (kh * ekh).astype(BF16), v[:, sl].astype(BF16)))
        raw = []
        for sl, (_, q_til, q_hat, k_til, k_hat, _) in zip(heads, fac):
            st_h = st[sl, :]
            raw.append((_dot_nt(q_til, st_h.astype(BF16)), _dot_nt(q_hat, k_til), _dot(v_t[sl, :], k_hat), st_h))
        for sl, (e_last, _, _, _, _, vb), (inter, att, upd, st_h) in zip(heads, fac, raw):
            o = inter + _dot(jnp.where(causal, att, 0.0).astype(BF16), vb)
            st[sl, :] = st_h * e_last + upd
            o_ref[:, sl] = o
            hgh = hg[:, sl]
            y_ref[:, sl] = (o * _rms(o) * g_ref[...] * (hgh * _sigmoid(hgh))).astype(BF16)

    col = lambda j: pl.BlockSpec((rows, D_HG), lambda n: (n, j))
    return pl.pallas_call(
        body, grid=(n_chunks // cps,),
        in_specs=[col(2), col(3), col(4), col(5),
                  pl.BlockSpec((2, D_HG), lambda n: (0, 0)), pl.BlockSpec((1, HG_HEAD_DIM), lambda n: (0, 0))],
        out_specs=[pl.BlockSpec((rows, D_HG), lambda n: (n, 0)), pl.BlockSpec((rows, D_HG), lambda n: (n, 0)),
                   pl.BlockSpec((cps, D_HG, HG_HEAD_DIM), lambda n: (n, 0, 0))],
        out_shape=[jax.ShapeDtypeStruct((T, D_HG), BF16), jax.ShapeDtypeStruct((T, D_HG), F32),
                   jax.ShapeDtypeStruct((n_chunks, D_HG, HG_HEAD_DIM), F32)],
        scratch_shapes=[pltpu.VMEM((D_HG, HG_HEAD_DIM), F32)],
        name="hg_fwd", compiler_params=_params("arbitrary"),
    )(p, p, p, p, lbraw, hg_g)


def _out_proj(h0, y_rg, y_hg, w_out, g2):
    T = h0.shape[0]
    tm = _row_tile(T, 832)

    def body(h_ref, yr_ref, yh_ref, w_ref, g_ref, h1_ref, v_ref, y_ref):
        y_ref[:, :D_RG] = yr_ref[...]
        y_ref[:, D_RG:] = yh_ref[...]
        h1 = h_ref[...] + _dot(y_ref[...], w_ref[...])
        h1_ref[...] = h1
        v_ref[...] = (h1 * _rms(h1) * g_ref[...]).astype(BF16)

    row = lambda n: pl.BlockSpec((tm, n), lambda i: (i, 0))
    return pl.pallas_call(
        body, grid=(T // tm,),
        in_specs=[row(D_MODEL), row(D_RG), row(D_HG), pl.BlockSpec((D_MODEL, D_MODEL), lambda i: (0, 0)),
                  pl.BlockSpec((1, D_MODEL), lambda i: (0, 0))],
        out_specs=[row(D_MODEL), row(D_MODEL), row(D_MODEL)],
        out_shape=[jax.ShapeDtypeStruct((T, D_MODEL), F32), jax.ShapeDtypeStruct((T, D_MODEL), BF16),
                   jax.ShapeDtypeStruct((T, D_MODEL), BF16)],
        name="out_proj", compiler_params=_params("parallel"),
    )(h0, y_rg, y_hg, w_out, g2)


def _gate_up(v, w_gu):
    T = v.shape[0]
    tm = _row_tile(T, 416)

    def body(v_ref, w_ref, gu_ref, act_ref):
        gu = _dot(v_ref[...], w_ref[...])
        gu_ref[...] = gu.astype(BF16)
        g = gu[:, :D_FF]
        act_ref[...] = (g * _sigmoid(g) * gu[:, D_FF:]).astype(BF16)

    row = lambda n: pl.BlockSpec((tm, n), lambda i: (i, 0))
    return pl.pallas_call(
        body, grid=(T // tm,),
        in_specs=[row(D_MODEL), pl.BlockSpec((D_MODEL, 2 * D_FF), lambda i: (0, 0))],
        out_specs=[row(2 * D_FF), row(D_FF)],
        out_shape=[jax.ShapeDtypeStruct((T, 2 * D_FF), BF16), jax.ShapeDtypeStruct((T, D_FF), BF16)],
        name="gate_up", compiler_params=_params("parallel"),
    )(v, w_gu)


def _down_loss(h1, act, w_down, gf, target):
    T = h1.shape[0]
    tm = _row_tile(T, 832)

    def body(h_ref, a_ref, w_ref, g_ref, t_ref, dh2_ref, dh2b_ref, loss_ref, gg_ref):
        i = pl.program_id(0)

        @pl.when(i == 0)
        def _():
            loss_ref[...] = jnp.zeros_like(loss_ref)
            gg_ref[...] = jnp.zeros_like(gg_ref)

        h2 = h_ref[...] + _dot(a_ref[...], w_ref[...])
        r = _rms(h2)
        n = h2 * r
        gf_ = g_ref[...]
        row = i * tm + lax.broadcasted_iota(jnp.int32, (tm, 1), 0)
        err = jnp.where(row >= PAD + N_META, n * gf_ - t_ref[...], 0.0)
        loss_ref[...] += 0.5 * jnp.sum(jnp.mean(err * err, axis=-1, keepdims=True), axis=0, keepdims=True)
        dy = err * (1.0 / D_MODEL)
        gg_ref[...] += jnp.sum(dy * n, axis=0, keepdims=True)
        dh2 = _rms_bwd(dy * gf_, n, r)
        dh2_ref[...] = dh2
        dh2b_ref[...] = dh2.astype(BF16)

    row_spec = lambda n: pl.BlockSpec((tm, n), lambda i: (i, 0))
    return pl.pallas_call(
        body, grid=(T // tm,),
        in_specs=[row_spec(D_MODEL), row_spec(D_FF), pl.BlockSpec((D_FF, D_MODEL), lambda i: (0, 0)),
                  pl.BlockSpec((1, D_MODEL), lambda i: (0, 0)), row_spec(D_MODEL)],
        out_specs=[row_spec(D_MODEL), row_spec(D_MODEL), pl.BlockSpec((1, 1), lambda i: (0, 0)),
                   pl.BlockSpec((1, D_MODEL), lambda i: (0, 0))],
        out_shape=[jax.ShapeDtypeStruct((T, D_MODEL), F32), jax.ShapeDtypeStruct((T, D_MODEL), BF16),
                   jax.ShapeDtypeStruct((1, 1), F32), jax.ShapeDtypeStruct((1, D_MODEL), F32)],
        name="down_loss", compiler_params=_params("arbitrary"),
    )(h1, act, w_down, gf, target)


def _ffn_bwd_act(dh2b, gu, w_down):
    T = dh2b.shape[0]
    tm = _row_tile(T, 416)

    def body(d_ref, gu_ref, w_ref, dgu_ref):
        dact = _dot_nt(d_ref[...], w_ref[...])
        g = gu_ref[:, :D_FF].astype(F32)
        u = gu_ref[:, D_FF:].astype(F32)
        s = _sigmoid(g)
        dgu_ref[:, :D_FF] = (dact * u * s * (1.0 + g * (1.0 - s))).astype(BF16)
        dgu_ref[:, D_FF:] = (dact * g * s).astype(BF16)

    row = lambda n: pl.BlockSpec((tm, n), lambda i: (i, 0))
    return pl.pallas_call(
        body, grid=(T // tm,),
        in_specs=[row(D_MODEL), row(2 * D_FF), pl.BlockSpec((D_FF, D_MODEL), lambda i: (0, 0))],
        out_specs=row(2 * D_FF),
        out_shape=jax.ShapeDtypeStruct((T, 2 * D_FF), BF16),
        name="ffn_bwd_act", compiler_params=_params("parallel"),
    )(dh2b, gu, w_down)


def _ffn_bwd_in(dgu, w_gu, h1, g2, dh2, w_out):
    T = h1.shape[0]
    tm = _row_tile(T, 416)

    def body(dgu_ref, wgu_ref, h_ref, g_ref, d2_ref, wo_ref, dh1_ref, dh1b_ref, dy_ref, gg_ref):
        i = pl.program_id(0)

        @pl.when(i == 0)
        def _():
            gg_ref[...] = jnp.zeros_like(gg_ref)

        dv = _dot_nt(dgu_ref[...], wgu_ref[...])
        h1 = h_ref[...]
        r = _rms(h1)
        n = h1 * r
        gg_ref[...] += jnp.sum(dv * n, axis=0, keepdims=True)
        dh1 = d2_ref[...] + _rms_bwd(dv * g_ref[...], n, r)
        dh1_ref[...] = dh1
        db = dh1.astype(BF16)
        dh1b_ref[...] = db
        dy_ref[...] = _dot_nt(db, wo_ref[...])

    row = lambda n: pl.BlockSpec((tm, n), lambda i: (i, 0))
    return pl.pallas_call(
        body, grid=(T // tm,),
        in_specs=[row(2 * D_FF), pl.BlockSpec((D_MODEL, 2 * D_FF), lambda i: (0, 0)),
                  row(D_MODEL), pl.BlockSpec((1, D_MODEL), lambda i: (0, 0)), row(D_MODEL),
                  pl.BlockSpec((D_MODEL, D_MODEL), lambda i: (0, 0))],
        out_specs=[row(D_MODEL), row(D_MODEL), row(D_MODEL), pl.BlockSpec((1, D_MODEL), lambda i: (0, 0))],
        out_shape=[jax.ShapeDtypeStruct((T, D_MODEL), F32), jax.ShapeDtypeStruct((T, D_MODEL), BF16),
                   jax.ShapeDtypeStruct((T, D_MODEL), F32), jax.ShapeDtypeStruct((1, D_MODEL), F32)],
        name="ffn_bwd_in", compiler_params=_params("arbitrary"),
    )(dgu, w_gu, h1, g2, dh2, w_out)


def _rg_bwd(p, hs, dy, dp, cw, cb, wg, bg, lam, rg_g):
    T = p.shape[0]
    tm = _row_tile(T, 208)
    nt = T // tm
    hb = tm // 8
    unroll = _scan_unroll(hb)

    def body(xg_ref, xh_ref, h_ref, hh_ref, dy_ref, dp_in_ref, cw_ref, cb_ref, w_ref, bg_ref, lam_ref, g_ref,
             dp_ref, gcw_ref, gcb_ref, gw_ref, gbg_ref, glam_ref, gg_ref,
             ext, dext, a_s, b_s, d_s, gacc, carry_d, carry_a):
        i = pl.program_id(0)
        t_idx = nt - 1 - i

        @pl.when(i == 0)
        def _():
            dext[tm:tm + 8, :] = jnp.zeros((8, D_RG), F32)
            carry_d[...] = jnp.zeros_like(carry_d)
            carry_a[...] = jnp.zeros_like(carry_a)
            gacc[...] = jnp.zeros_like(gacc)
            for ref in (gcw_ref, gcb_ref, gbg_ref, glam_ref, gg_ref, gw_ref):
                ref[...] = jnp.zeros_like(ref)

        first = t_idx == 0
        ext[0:8, :] = jnp.where(first, 0.0, xh_ref[:, :D_RG])
        ext[8:8 + tm, :] = xg_ref[:, :D_RG]
        xc = _conv(ext, cw_ref, cb_ref, tm)
        lam_ = lam_ref[...]
        r, ig, sp, a, m, inv_m = _rg_gates(xc, w_ref, bg_ref, lam_)
        row = t_idx * tm + lax.broadcasted_iota(jnp.int32, (tm, 1), 0)
        valid = row >= PAD

        gr = xg_ref[:, D_RG:]
        g, dgelu = _gelu_parts(gr)
        h = h_ref[...]
        yy = g * h
        rr = _rms(yy)
        nn = yy * rr
        dy_ = dy_ref[...]
        gg_ref[...] += jnp.sum(dy_ * nn, axis=0, keepdims=True)
        dyy = _rms_bwd(dy_ * g_ref[...], nn, rr)
        dp_ref[:, D_RG:] = (dyy * h * dgelu).astype(BF16)

        a_s[...] = a
        b_s[...] = dyy * g
        rowi = lax.broadcasted_iota(jnp.int32, (8, D_RG), 0)

        def blk(jj, c):
            cd, ca = c
            for u in range(unroll):
                o = pl.multiple_of((hb - 1 - (jj * unroll + u)) * 8, 8)
                a_blk = a_s[pl.ds(o, 8), :]
                a_next = jnp.where(rowi == 7, ca, pltpu.roll(a_blk, 7, axis=0))
                A, B = _scan_block_bwd(a_next, b_s[pl.ds(o, 8), :], rowi)
                d = B + A * cd
                d_s[pl.ds(o, 8), :] = d
                cd, ca = d[0:1, :], a_blk[0:1, :]
            return cd, ca

        cd, ca = lax.fori_loop(0, hb // unroll, blk, (carry_d[...], carry_a[...]))
        carry_d[...] = cd
        carry_a[...] = ca
        delta = d_s[...]

        h_last_prev = jnp.where(first, 0.0, hh_ref[7:8, :])
        row0 = lax.broadcasted_iota(jnp.int32, (tm, 1), 0) == 0
        h_prev = jnp.where(row0, h_last_prev, pltpu.roll(h, 1, axis=0))
        dbx = jnp.where(valid, delta, 0.0)
        da = delta * h_prev
        di = dbx * m * xc
        dm = dbx * ig * xc
        dla = a * (da - dm * a * inv_m)
        dla = jnp.where(valid, dla, 0.0)
        glam_ref[...] += jnp.sum(dla * r, axis=0, keepdims=True) * (LRU_C / (1.0 + jnp.exp(lam_)))
        dr = (-LRU_C) * sp * dla
        dpre = jnp.concatenate([dr * r * (1.0 - r), di * ig * (1.0 - ig)], axis=1)
        gbg_ref[...] += jnp.sum(dpre, axis=0, keepdims=True)
        dpre_b = dpre.astype(BF16)
        gacc[...] += _dot_tn(xc.astype(BF16), dpre_b)
        dxc = dbx * m * ig + _dot_nt(dpre_b, w_ref[...])
        gcb_ref[...] += jnp.sum(dxc, axis=0, keepdims=True)
        for j in range(CONV_W):
            gcw_ref[j:j + 1, :] += jnp.sum(dxc * ext[8 - 3 + j:8 - 3 + j + tm, :], axis=0, keepdims=True)
        dext[0:tm, :] = dxc
        dxr = cw_ref[0:1, :] * dext[3:3 + tm, :]
        for j in range(1, CONV_W):
            dxr = dxr + cw_ref[j:j + 1, :] * dext[3 - j:3 - j + tm, :]
        dp_ref[:, :D_RG] = dxr.astype(BF16)
        dext[tm:tm + 8, :] = dext[0:8, :]

        @pl.when(i == nt - 1)
        def _():
            fold = _head_fold()
            mask = _head_mask()
            for k in range(2):
                blockdiag = jnp.where(mask, gacc[:, k * D_RG:(k + 1) * D_RG], 0.0)
                gw_ref[k * D_RG:(k + 1) * D_RG, :] = jnp.dot(blockdiag, fold, precision=HIGHEST,
                                                             preferred_element_type=F32)

    vec = lambda n: pl.BlockSpec((1, n), lambda i: (0, 0))
    rev = lambda n: pl.BlockSpec((tm, n), lambda i: (nt - 1 - i, 0))
    halo = lambda n: pl.BlockSpec((8, n), lambda i: (jnp.maximum((nt - 1 - i) * hb - 1, 0), 0))
    return pl.pallas_call(
        body, grid=(nt,),
        in_specs=[rev(2 * D_RG), halo(2 * D_RG), rev(D_RG), halo(D_RG), rev(D_RG), ANY,
                  pl.BlockSpec((CONV_W, D_RG), lambda i: (0, 0)), vec(D_RG),
                  pl.BlockSpec((D_RG, 2 * D_RG), lambda i: (0, 0)), vec(2 * D_RG), vec(D_RG), vec(D_RG)],
        out_specs=[rev(2 * D_RG), pl.BlockSpec((CONV_W, D_RG), lambda i: (0, 0)), vec(D_RG),
                   pl.BlockSpec((2 * D_RG, RG_HEAD_DIM), lambda i: (0, 0)), vec(2 * D_RG), vec(D_RG), vec(D_RG)],
        input_output_aliases={5: 0},
        out_shape=[jax.ShapeDtypeStruct((T, D_IN), BF16), jax.ShapeDtypeStruct((CONV_W, D_RG), F32),
                   jax.ShapeDtypeStruct((1, D_RG), F32), jax.ShapeDtypeStruct((2 * D_RG, RG_HEAD_DIM), F32),
                   jax.ShapeDtypeStruct((1, 2 * D_RG), F32), jax.ShapeDtypeStruct((1, D_RG), F32),
                   jax.ShapeDtypeStruct((1, D_RG), F32)],
        scratch_shapes=[pltpu.VMEM((tm + 8, D_RG), F32), pltpu.VMEM((tm + 8, D_RG), F32),
                        pltpu.VMEM((tm, D_RG), F32), pltpu.VMEM((tm, D_RG), F32), pltpu.VMEM((tm, D_RG), F32),
                        pltpu.VMEM((D_RG, 2 * D_RG), F32), pltpu.VMEM((1, D_RG), F32), pltpu.VMEM((1, D_RG), F32)],
        name="rg_bwd", compiler_params=_params("arbitrary"),
    )(p, p, hs, hs, dy, dp, cw, cb, wg, bg, lam, rg_g)


def _hg_bwd(p, o_all, st_all, dy, lbraw, hg_g):
    T = p.shape[0]
    n_chunks = T // CHUNK
    cps = _chunks_per_step(n_chunks)
    rows = cps * CHUNK
    n_steps = n_chunks // cps

    def body(hq_ref, hf_ref, hi_ref, hg_ref, o_ref, st_ref, dy_ref, lb_ref, g_ref,
             dp_ref, glb_ref, gg_ref, dst):
        i = pl.program_id(0)

        @pl.when(i == 0)
        def _():
            dst[...] = jnp.zeros_like(dst)
            glb_ref[...] = jnp.zeros_like(glb_ref)
            gg_ref[...] = jnp.zeros_like(gg_ref)

        dp_ref[:, :2 * D_RG] = jnp.zeros((rows, 2 * D_RG), BF16)

        def chunk(jj, carry):
            j = cps - 1 - jj
            rs = pl.ds(pl.multiple_of(j * CHUNK, CHUNK), CHUNK)
            chunk_body((n_steps - 1 - i) * cps + j, hq_ref.at[rs, :], hf_ref.at[rs, :], hi_ref.at[rs, :],
                       hg_ref.at[rs, :], o_ref.at[rs, :], st_ref.at[pl.ds(j, 1)], dy_ref.at[rs, :], lb_ref, g_ref,
                       dp_ref.at[rs, pl.ds(2 * D_RG, 4 * D_HG)], glb_ref, gg_ref, dst)
            return carry

        lax.fori_loop(0, cps, chunk, 0, unroll=True)

    def chunk_body(n, hq_ref, hf_ref, hi_ref, hg_ref, o_ref, st_ref, dy_ref, lb_ref, g_ref,
                   dp_ref, glb_ref, gg_ref, dst):
        valid = (n * CHUNK + lax.broadcasted_iota(jnp.int32, (CHUNK, 1), 0)) >= PAD
        hq, hf, v, hg = hq_ref[...], hf_ref[...], hi_ref[...], hg_ref[...]
        lb, sq, q, sf, f, b = _hg_gates(hq, hf, lb_ref, valid)
        k = 1.0 - f
        causal = _causal()
        r_i = lax.broadcasted_iota(jnp.int32, (CHUNK, CHUNK), 0)
        c_i = lax.broadcasted_iota(jnp.int32, (CHUNK, CHUNK), 1)
        causal_t = r_i <= c_i
        is_last = lax.broadcasted_iota(jnp.int32, (CHUNK, 1), 0) == CHUNK - 1
        g_ = g_ref[...]
        db_parts, dq_parts, dk_parts = [], [], []
        gg = jnp.zeros((1, HG_HEAD_DIM), F32)
        heads = [slice(h * HG_HEAD_DIM, (h + 1) * HG_HEAD_DIM) for h in range(HG_HEADS)]

        do_parts = []
        for h, sl in enumerate(heads):
            o = o_ref[:, sl]
            ro = _rms(o)
            no = o * ro
            hgh = hg[:, sl]
            sg = _sigmoid(hgh)
            dyh = dy_ref[:, sl]
            dp_ref[:, 3 * D_HG + h * HG_HEAD_DIM:3 * D_HG + (h + 1) * HG_HEAD_DIM] = (
                dyh * no * g_ * sg * (1.0 + hgh * (1.0 - sg))).astype(BF16)
            dng = dyh * hgh * sg
            gg = gg + jnp.sum(dng * no, axis=0, keepdims=True)
            do_parts.append(_rms_bwd(dng * g_, no, ro))
        do_t = jnp.concatenate(do_parts, axis=1).T.astype(BF16)

        fac = []
        for sl, do in zip(heads, do_parts):
            qh, kh, bh = q[:, sl], k[:, sl], b[:, sl]
            blk, b_last, eb, eq, ekh, ek, q_hat, k_til = _hg_head(qh, kh, bh)
            fac.append(dict(qh=qh, kh=kh, blk=blk, e_last=jnp.exp(b_last), eb=eb, eq=eq, ekh=ekh, ek=ek,
                            q_til=qh * eb, k_hat=kh * ekh, qhb=q_hat.astype(BF16), ktb=k_til.astype(BF16),
                            vb=v[:, sl].astype(BF16), dob=do.astype(BF16)))

        first = []
        for sl, t in zip(heads, fac):
            st_h = st_ref[0, sl, :]
            dst_h = dst[sl, :]
            dstb = dst_h.astype(BF16)
            first.append(dict(
                att_t=_dot_nt(t["ktb"], t["qhb"]), datt=_dot_nt(t["dob"], t["vb"]),
                datt_t=_dot_nt(t["vb"], t["dob"]), dk_hat=_dot(t["vb"], dstb),
                dv=_dot_nt(t["k_hat"].astype(BF16), dstb), dq_til=_dot(t["dob"], st_h.astype(BF16)),
                state=t["e_last"] * jnp.sum(dst_h * st_h, axis=0, keepdims=True)))
            dst[sl, :] = dst_h * t["e_last"] + _dot(do_t[sl, :], t["q_til"].astype(BF16))

        for h, (t, m) in enumerate(zip(fac, first)):
            qh, kh, blk, eb, eq, ekh, ek = t["qh"], t["kh"], t["blk"], t["eb"], t["eq"], t["ekh"], t["ek"]
            q_til, k_hat, qhb, ktb, dob = t["q_til"], t["k_hat"], t["qhb"], t["ktb"], t["dob"]
            dk_hat, dq_til = m["dk_hat"], m["dq_til"]
            dv = m["dv"] + _dot(jnp.where(causal_t, m["att_t"], 0.0).astype(BF16), dob)
            dq_hat = _dot(jnp.where(causal, m["datt"], 0.0).astype(BF16), ktb)
            dk_til = _dot(jnp.where(causal_t, m["datt_t"], 0.0).astype(BF16), qhb)
            db_last = jnp.sum(dk_hat * k_hat, axis=0, keepdims=True) + m["state"]
            dq_sel = dq_hat[:, (N_SUB - 1) * HG_HEAD_DIM:]
            for s in range(N_SUB - 2, -1, -1):
                dq_sel = jnp.where(blk == s, dq_hat[:, s * HG_HEAD_DIM:(s + 1) * HG_HEAD_DIM], dq_sel)
            dq_a = dq_sel * eq
            dk_a = dk_til[:, :HG_HEAD_DIM] * ek[0]
            for s in range(1, N_SUB):
                dk_a = dk_a + dk_til[:, s * HG_HEAD_DIM:(s + 1) * HG_HEAD_DIM] * ek[s]
            db_att = qhb.astype(F32) * dq_hat - ktb.astype(F32) * dk_til
            db = dq_til * q_til - dk_hat * k_hat
            for s in range(N_SUB):
                db = db + db_att[:, s * HG_HEAD_DIM:(s + 1) * HG_HEAD_DIM]
            db_parts.append(jnp.where(is_last, db + db_last, db))
            dq_parts.append(dq_til * eb + dq_a)
            dk_parts.append(dk_hat * ekh + dk_a)
            dp_ref[:, 2 * D_HG + h * HG_HEAD_DIM:2 * D_HG + (h + 1) * HG_HEAD_DIM] = dv.astype(BF16)

        gg_ref[...] += gg
        db = jnp.concatenate(db_parts, axis=1)
        dq = jnp.concatenate(dq_parts, axis=1)
        dk = jnp.concatenate(dk_parts, axis=1)
        dlf = jnp.where(valid, jnp.dot(_tri(False), db, precision=HIGHEST, preferred_element_type=F32), 0.0)
        dp_ref[:, :D_HG] = (dq * sq * (1.0 + hq * (1.0 - sq))).astype(BF16)
        df = dlf / f - dk
        dlb = jnp.sum(df * (1.0 - sf), axis=0, keepdims=True) * lb * (1.0 - lb)
        glb_ref[0:1, :] += dlb
        glb_ref[1:2, :] += -dlb
        dp_ref[:, D_HG:2 * D_HG] = (df * (1.0 - lb) * sf * (1.0 - sf)).astype(BF16)

    rev = lambda j: pl.BlockSpec((rows, D_HG), lambda i: (n_steps - 1 - i, j))
    return pl.pallas_call(
        body, grid=(n_steps,),
        in_specs=[rev(2), rev(3), rev(4), rev(5), rev(0),
                  pl.BlockSpec((cps, D_HG, HG_HEAD_DIM), lambda i: (n_steps - 1 - i, 0, 0)), rev(1),
                  pl.BlockSpec((2, D_HG), lambda i: (0, 0)), pl.BlockSpec((1, HG_HEAD_DIM), lambda i: (0, 0))],
        out_specs=[pl.BlockSpec((rows, D_IN), lambda i: (n_steps - 1 - i, 0)),
                   pl.BlockSpec((2, D_HG), lambda i: (0, 0)), pl.BlockSpec((1, HG_HEAD_DIM), lambda i: (0, 0))],
        out_shape=[jax.ShapeDtypeStruct((T, D_IN), BF16), jax.ShapeDtypeStruct((2, D_HG), F32),
                   jax.ShapeDtypeStruct((1, HG_HEAD_DIM), F32)],
        scratch_shapes=[pltpu.VMEM((D_HG, HG_HEAD_DIM), F32)],
        name="hg_bwd", compiler_params=_params("arbitrary"),
    )(p, p, p, p, o_all, st_all, dy, lbraw, hg_g)


def _in_bwd(dp, w_in, h0, g1, dh1):
    T = h0.shape[0]
    tm = _row_tile(T, 416)

    def body(dp_ref, w_ref, h_ref, g_ref, d1_ref, dh0_ref, gg_ref):
        i = pl.program_id(0)

        @pl.when(i == 0)
        def _():
            gg_ref[...] = jnp.zeros_like(gg_ref)

        du = _dot_nt(dp_ref[...], w_ref[...])
        h0_ = h_ref[...]
        r = _rms(h0_)
        n = h0_ * r
        gg_ref[...] += jnp.sum(du * n, axis=0, keepdims=True)
        dh0_ref[...] = d1_ref[...] + _rms_bwd(du * g_ref[...], n, r)

    row = lambda n: pl.BlockSpec((tm, n), lambda i: (i, 0))
    return pl.pallas_call(
        body, grid=(T // tm,),
        in_specs=[row(D_IN), pl.BlockSpec((D_MODEL, D_IN), lambda i: (0, 0)),
                  row(D_MODEL), pl.BlockSpec((1, D_MODEL), lambda i: (0, 0)), row(D_MODEL)],
        out_specs=[row(D_MODEL), pl.BlockSpec((1, D_MODEL), lambda i: (0, 0))],
        out_shape=[jax.ShapeDtypeStruct((T, D_MODEL), F32), jax.ShapeDtypeStruct((1, D_MODEL), F32)],
        name="in_bwd", compiler_params=_params("arbitrary"),
    )(dp, w_in, h0, g1, dh1)


def _col_tile(cols, target):
    best = None
    for t in range(128, min(cols, target) + 1, 128):
        if cols % t == 0:
            best = t
    assert best is not None, cols
    return best


MXU_DIM = 256


def _mxu_tile(cols, target):
    best = None
    for t in range(MXU_DIM, min(cols, target) + 1, MXU_DIM):
        if cols % t == 0:
            best = t
    assert best is not None, cols
    return best


def _weight_grad(a, b, name):
    T, M = a.shape
    N = b.shape[1]
    tm = _col_tile(M, 1408)
    tn = _mxu_tile(N, 768 if tm <= 1024 else 512)

    def body(a_ref, b_ref, o_ref):
        o_ref[...] = _dot_tn(a_ref[...], b_ref[...])

    return pl.pallas_call(
        body, grid=(M // tm, N // tn),
        in_specs=[pl.BlockSpec((T, tm), lambda m, n: (0, m)), pl.BlockSpec((T, tn), lambda m, n: (0, n))],
        out_specs=pl.BlockSpec((tm, tn), lambda m, n: (m, n)),
        out_shape=jax.ShapeDtypeStruct((M, N), F32),
        name=name, compiler_params=_params("parallel", "parallel"),
    )(a, b)


def _local_step(h0, target, w_in, w_out, w_gu, w_down, small, on_ffn_grads=None, on_mixer_grads=None):
    wg = _gate_weights(small["w_rgate"], small["w_igate"])
    bg = jnp.concatenate([small["b_rgate"], small["b_igate"]], axis=1)

    p, u = _in_proj(h0, small["mix_norm_g"], w_in)
    y_rg, hs = _rg_fwd(p, small["conv_w"], small["conv_b"], wg, bg, small["lru_lambda"], small["rg_norm_g"])
    y_hg, o_all, st_all = _hg_fwd(p, small["hg_lower_bound"], small["hg_norm_g"])
    h1, v, yb = _out_proj(h0, y_rg, y_hg, w_out, small["ffn_norm_g"])
    gu, act = _gate_up(v, w_gu)
    dh2, dh2b, loss, g_final = _down_loss(h1, act, w_down, small["final_norm_g"], target)

    dgu = _ffn_bwd_act(dh2b, gu, w_down)
    ffn_grads = {"w_gate_up": _weight_grad(v, dgu, "grad_w_gate_up"),
                 "w_down": _weight_grad(act, dh2b, "grad_w_down")}
    stages = on_ffn_grads(ffn_grads) if on_ffn_grads is not None else None
    dh1, dh1b, dy, g_ffn = _ffn_bwd_in(dgu, w_gu, h1, small["ffn_norm_g"], dh2, w_out)
    early = late = None
    if stages is not None:
        chip_sums, send = stages
        sums = chip_sums()
        (dh1, dh1b, dy), sums = lax.optimization_barrier(((dh1, dh1b, dy), sums))
        early = send(sums)
    dp, g_lb, g_hgn = _hg_bwd(p, o_all, st_all, dy, small["hg_lower_bound"], small["hg_norm_g"])
    dp, g_cw, g_cb, g_wgate, g_bg, g_lam, g_rgn = _rg_bwd(
        p, hs, dy, dp, small["conv_w"], small["conv_b"], wg, bg, small["lru_lambda"], small["rg_norm_g"])
    mixer_grads = {"w_in": _weight_grad(u, dp, "grad_w_in"), "w_out": _weight_grad(yb, dh1b, "grad_w_out")}
    if on_mixer_grads is not None:
        chip_sums, send = on_mixer_grads(mixer_grads)
        sums = chip_sums()
        (dp, dh1), sums = lax.optimization_barrier(((dp, dh1), sums))
        late = send(sums)
    dh0, g_mix = _in_bwd(dp, w_in, h0, small["mix_norm_g"], dh1)

    grads = {
        "w_in": mixer_grads["w_in"], "w_out": mixer_grads["w_out"],
        "w_gate_up": ffn_grads["w_gate_up"], "w_down": ffn_grads["w_down"],
        "mix_norm_g": g_mix, "conv_w": g_cw, "conv_b": g_cb, "w_gates": g_wgate,
        "b_rgate": g_bg[:, :D_RG], "b_igate": g_bg[:, D_RG:], "lru_lambda": g_lam, "rg_norm_g": g_rgn,
        "hg_lower_bound": g_lb, "hg_norm_g": g_hgn, "ffn_norm_g": g_ffn, "final_norm_g": g_final,
    }
    return loss, dh0, grads, early, late


ANY = pl.BlockSpec(memory_space=pl.ANY)
HALF = D_MODEL // 2

BIG = {"w_in": (D_MODEL, D_IN // N_CHIPS, True), "w_gate_up": (D_MODEL, 2 * D_FF // N_CHIPS, True),
       "w_out": (D_MODEL // N_CHIPS, D_MODEL, False), "w_down": (D_FF // N_CHIPS, D_MODEL, False)}
BIG_NAMES = tuple(BIG)
N_BIG = len(BIG_NAMES)


def _full_shape(name):
    rows, cols, by_col = BIG[name]
    return (rows, cols * N_CHIPS) if by_col else (rows * N_CHIPS, cols)


def _place():
    return lax.axis_index("x"), lax.axis_index("y"), lax.axis_index("c")


def _chip_of(x, y, r):
    fx, fy = (r + 1) >> 1, (r + 1) & 1
    return (1 - x if fx else x), (1 - y if fy else y)


def _half_of(ref, by_col, half):
    start = pl.multiple_of(half * HALF, 128)
    return ref.at[pl.ds(start, HALF), :] if by_col else ref.at[:, pl.ds(start, HALF)]


def _shard_of(ref, name, chip):
    rows, cols, by_col = BIG[name]
    if by_col:
        return ref.at[:, pl.ds(pl.multiple_of(chip * cols, 128), cols)]
    return ref.at[pl.ds(pl.multiple_of(chip * rows, 16), rows), :]


def _shard_half_of(ref, name, chip, half):
    rows, cols, by_col = BIG[name]
    start = pl.multiple_of(half * HALF, 128)
    if by_col:
        return ref.at[pl.ds(start, HALF), pl.ds(pl.multiple_of(chip * cols, 128), cols)]
    return ref.at[pl.ds(pl.multiple_of(chip * rows, 16), rows), pl.ds(start, HALF)]


def _remote(src, dst, send_sems, recv_sems, k, dev):
    return pltpu.make_async_remote_copy(src_ref=src, dst_ref=dst, send_sem=send_sems.at[k], recv_sem=recv_sems.at[k],
                                        device_id=dev, device_id_type=MESH)


def _cast_into_full(w, chip):
    steps = 4
    in_specs, out_specs = [], []
    for name in BIG_NAMES:
        rows, cols, by_col = BIG[name]
        tr = rows // steps
        in_specs.append(pl.BlockSpec((tr, cols), lambda i, s: (i, 0)))
        if by_col:
            out_specs.append(pl.BlockSpec((tr, cols), lambda i, s: (i, s[0])))
        else:
            out_specs.append(pl.BlockSpec((tr, cols), lambda i, s: (s[0] * steps + i, 0)))

    def body(s_ref, *refs):
        for a in range(N_BIG):
            refs[N_BIG + a][...] = refs[a][...].astype(BF16)

    placed = pl.pallas_call(
        body,
        grid_spec=pltpu.PrefetchScalarGridSpec(num_scalar_prefetch=1, grid=(steps,), in_specs=in_specs,
                                               out_specs=out_specs),
        out_shape=[jax.ShapeDtypeStruct(_full_shape(name), BF16) for name in BIG_NAMES],
        name="place_shards", compiler_params=_params("parallel"),
    )(chip, *[w[name] for name in BIG_NAMES])
    return dict(zip(BIG_NAMES, placed))


def _gather_weights(placed, small, names, label, collective_id):
    n, ns = len(names), len(small)
    hbm = pltpu.MemorySpace.HBM
    outs = [jax.new_ref(placed[nm], memory_space=hbm) for nm in names]
    small_in = [jax.new_ref(s, memory_space=hbm) for s in small]
    small_out = [jax.empty_ref(jax.ShapeDtypeStruct((s.shape[0], s.shape[1] * N_CHIPS), F32), memory_space=hbm)
                 for s in small]
    n_sems = 6 * n + 3 * ns

    @pl.kernel(mesh=plsc.ScalarSubcoreMesh(axis_name="seq", num_cores=1), name=label, out_type=(),
               scratch_types=(pltpu.SemaphoreType.DMA((n_sems,)), pltpu.SemaphoreType.DMA((n_sems,)),
                              pltpu.SemaphoreType.DMA((max(ns, 1),))),
               compiler_params=pltpu.CompilerParams(collective_id=collective_id))
    def launch(send_sems, recv_sems, local_sems):
        x, y, c = _place()
        chip = 2 * x + y
        sibling = (x, y, 1 - c)
        others = [_chip_of(x, y, r) for r in range(3)]
        _handshake([(qx, qy, c) for qx, qy in others] + [sibling])

        def small_block(a, q):
            cols = small[a].shape[1]
            return small_out[a].at[:, pl.ds(pl.multiple_of(q * cols, 128), cols)]

        local = [pltpu.make_async_copy(small_in[a], small_block(a, chip), local_sems.at[a]) for a in range(ns)]
        for cp in local:
            cp.start()

        sends = []
        for a, name in enumerate(names):
            mine = _shard_half_of(outs[a], name, chip, c)
            for r, (qx, qy) in enumerate(others):
                sends.append(_remote(mine, mine, send_sems, recv_sems, 6 * a + r, (qx, qy, c)))
        for a in range(ns):
            for r, (qx, qy) in enumerate(others):
                sends.append(_remote(small_in[a], small_block(a, chip), send_sems, recv_sems,
                                     6 * n + 3 * a + r, (qx, qy, c)))
        for cp in sends:
            cp.start()

        forwards = []
        for a, name in enumerate(names):
            for r, (qx, qy) in enumerate(others):
                landed = _shard_half_of(outs[a], name, 2 * qx + qy, c)
                _remote(landed, landed, send_sems, recv_sems, 6 * a + r, (qx, qy, c)).wait_recv()
                fwd = _remote(landed, landed, send_sems, recv_sems, 6 * a + 3 + r, sibling)
                fwd.start()
                forwards.append(fwd)
        for a in range(ns):
            for r, (qx, qy) in enumerate(others):
                landed = small_block(a, 2 * qx + qy)
                _remote(landed, landed, send_sems, recv_sems, 6 * n + 3 * a + r, (qx, qy, c)).wait_recv()
        for a, name in enumerate(names):
            for r, (qx, qy) in enumerate(others):
                landed = _shard_half_of(outs[a], name, 2 * qx + qy, 1 - c)
                _remote(landed, landed, send_sems, recv_sems, 6 * a + 3 + r, sibling).wait_recv()
        for cp in sends + forwards:
            cp.wait_send()
        for cp in local:
            cp.wait()

    launch()
    return {nm: ref[...] for nm, ref in zip(names, outs)}, [ref[...] for ref in small_out]


def _exchange_halves(grads, names, label, collective_id):
    n = len(names)
    sequencer = collective_id is not None

    def body(*refs):
        ins, outs = refs[:n], refs[n:2 * n]
        send_sems, recv_sems = refs[2 * n:]
        x, y, c = _place()
        if sequencer:
            _handshake([(x, y, 1 - c)])
        copies = []
        for a, name in enumerate(names):
            copies.append(_remote(_half_of(ins[a], BIG[name][2], 1 - c), outs[a], send_sems, recv_sems, a,
                                  (x, y, 1 - c)))
        for cp in copies:
            cp.start()
        for cp in copies:
            cp.wait()

    def half_shape(name):
        r, c_ = _full_shape(name)
        return (HALF, c_) if BIG[name][2] else (r, HALF)

    out_type = tuple(jax.ShapeDtypeStruct(half_shape(nm), F32) for nm in names)
    sems = (pltpu.SemaphoreType.DMA((n,)), pltpu.SemaphoreType.DMA((n,)))
    operands = [grads[nm] for nm in names]
    if sequencer:
        got = pl.kernel(
            body, mesh=plsc.ScalarSubcoreMesh(axis_name="seq", num_cores=1), name=label, out_type=out_type,
            scratch_types=sems, compiler_params=pltpu.CompilerParams(collective_id=collective_id),
        )(*operands)
    else:
        got = pl.pallas_call(
            body, in_specs=[ANY] * n, out_specs=[ANY] * n, out_shape=list(out_type), scratch_shapes=list(sems),
            name=label,
        )(*operands)
    return dict(zip(names, got))


def _chip_sum(grads, got, names, core, label):
    n = len(names)
    steps = 4
    g_specs, blks = [], []
    for name in names:
        rows, cols = got[name].shape
        tr = rows // steps
        if BIG[name][2]:
            g_specs.append(pl.BlockSpec((tr, cols), lambda i, s: (s[0] * steps + i, 0)))
        else:
            g_specs.append(pl.BlockSpec((tr, HALF), lambda i, s: (i, s[0])))
        blks.append(pl.BlockSpec((tr, cols), lambda i, s: (i, 0)))

    def body(s_ref, *refs):
        for a in range(n):
            t = refs[a][...] + refs[n + a][...]
            refs[2 * n + a][...] = t
            refs[3 * n + a][...] = t.astype(BF16)

    out = pl.pallas_call(
        body,
        grid_spec=pltpu.PrefetchScalarGridSpec(num_scalar_prefetch=1, grid=(steps,), in_specs=g_specs + blks,
                                               out_specs=blks + blks),
        out_shape=([jax.ShapeDtypeStruct(got[nm].shape, F32) for nm in names]
                   + [jax.ShapeDtypeStruct(got[nm].shape, BF16) for nm in names]),
        name=label, compiler_params=_params("parallel"),
    )(core, *[grads[nm] for nm in names], *[got[nm] for nm in names])
    return {nm: (out[a], out[n + a]) for a, nm in enumerate(names)}


def _piece_shape(name):
    rows, cols, by_col = BIG[name]
    return (HALF, cols) if by_col else (rows, HALF)


def _handshake(peers):
    barrier = pltpu.get_barrier_semaphore()
    for peer in peers:
        pl.semaphore_signal(barrier, inc=1, device_id=peer, device_id_type=MESH)
    pl.semaphore_wait(barrier, len(peers))


def _send_chip_sums(sums, names, label, collective_id):
    n = len(names)

    def body(*refs):
        ins, outs = refs[:n], refs[n:2 * n]
        send_sems, recv_sems = refs[2 * n:]
        x, y, c = _place()
        others = [_chip_of(x, y, r) for r in range(3)]
        _handshake([(qx, qy, c) for qx, qy in others])
        copies = []
        for a, name in enumerate(names):
            for r, (qx, qy) in enumerate(others):
                copies.append(_remote(_shard_of(ins[a], name, 2 * qx + qy), outs[a].at[r], send_sems, recv_sems,
                                      3 * a + r, (qx, qy, c)))
        for cp in copies:
            cp.start()
        for cp in copies:
            cp.wait()

    return pl.kernel(
        body, mesh=plsc.ScalarSubcoreMesh(axis_name="seq", num_cores=1), name=label,
        out_type=tuple(jax.ShapeDtypeStruct((3,) + _piece_shape(nm), BF16) for nm in names),
        scratch_types=(pltpu.SemaphoreType.DMA((3 * n,)), pltpu.SemaphoreType.DMA((3 * n,))),
        compiler_params=pltpu.CompilerParams(collective_id=collective_id),
    )(*[sums[nm] for nm in names])


def _total(parts, chip_core):
    steps = 2
    in_specs, out_specs, operands = [], [], []
    for name in BIG_NAMES:
        by_col = BIG[name][2]
        pr, pc = _piece_shape(name)
        tr = pr // steps
        if by_col:
            in_specs.append(pl.BlockSpec((tr, pc), lambda i, s: (i, s[0])))
            out_specs.append(pl.BlockSpec((tr, pc), lambda i, s: (s[1] * steps + i, 0)))
        else:
            in_specs.append(pl.BlockSpec((tr, pc), lambda i, s: (s[0] * steps + i, 0)))
            out_specs.append(pl.BlockSpec((tr, pc), lambda i, s: (i, s[1])))
        for r in range(3):
            in_specs.append(pl.BlockSpec((None, tr, pc), lambda i, s, r=r: (r, i, 0)))
        own, got = parts[name]
        operands += [own, got, got, got]

    def body(s_ref, *refs):
        for a in range(N_BIG):
            o_ref, a_ref, b_ref, c_ref = refs[4 * a:4 * a + 4]
            refs[4 * N_BIG + a][...] = (((o_ref[...] + a_ref[...].astype(F32)) + b_ref[...].astype(F32))
                                        + c_ref[...].astype(F32))

    totals = pl.pallas_call(
        body,
        grid_spec=pltpu.PrefetchScalarGridSpec(num_scalar_prefetch=1, grid=(steps,), in_specs=in_specs,
                                               out_specs=out_specs),
        out_shape=[jax.ShapeDtypeStruct(BIG[name][:2], F32) for name in BIG_NAMES],
        name="totals", compiler_params=_params("parallel"),
    )(chip_core, *operands)
    return dict(zip(BIG_NAMES, totals))


def _share_totals(totals):
    def body(*refs):
        outs = refs[N_BIG:2 * N_BIG]
        send_sems, recv_sems = refs[2 * N_BIG:]
        x, y, c = _place()
        copies = []
        for a, name in enumerate(BIG_NAMES):
            mine = _half_of(outs[a], BIG[name][2], c)
            copies.append(_remote(mine, mine, send_sems, recv_sems, a, (x, y, 1 - c)))
        for cp in copies:
            cp.start()
        for a, name in enumerate(BIG_NAMES):
            theirs = _half_of(outs[a], BIG[name][2], 1 - c)
            _remote(theirs, theirs, send_sems, recv_sems, a, (x, y, 1 - c)).wait_recv()
        for cp in copies:
            cp.wait_send()

    return pl.pallas_call(
        body, in_specs=[ANY] * N_BIG, out_specs=[ANY] * N_BIG,
        out_shape=[jax.ShapeDtypeStruct(BIG[n][:2], F32) for n in BIG_NAMES],
        input_output_aliases={a: a for a in range(N_BIG)},
        scratch_shapes=[pltpu.SemaphoreType.DMA((N_BIG,)), pltpu.SemaphoreType.DMA((N_BIG,))],
        name="share_totals",
    )(*[totals[n] for n in BIG_NAMES])


VEC_ROWS = 32
VEC_ROW = {"mix_norm_g": 0, "conv_b": 1, "b_rgate": 2, "b_igate": 3, "lru_lambda": 4, "rg_norm_g": 5,
           "hg_lower_bound": 6, "hg_norm_g": 8, "ffn_norm_g": 9, "final_norm_g": 10, "loss": 11,
           "conv_w": 12, "meta_tokens": 16}
N_DEV = 8


def _all_reduce_small(pieces, gates):
    names = list(pieces)
    hv, hg = VEC_ROWS // 2, gates.shape[0] // 2

    def body(*refs):
        ins = refs[:len(names)]
        (g_ref, vec_ref, gsum_ref, mine_v, sib_v, sib_g, chip_v, chip_g, got_v, got_g,
         send_sems, recv_sems) = refs[len(names):]
        x, y, c = _place()
        chip = 2 * x + y
        sibling = (x, y, 1 - c)
        mine_v[...] = jnp.zeros_like(mine_v)
        for name, ref in zip(names, ins):
            nr, w = ref.shape
            mine_v[VEC_ROW[name]:VEC_ROW[name] + nr, 0:w] = ref[...]

        swap = [_remote(mine_v, sib_v, send_sems, recv_sems, 0, sibling),
                _remote(g_ref, sib_g, send_sems, recv_sems, 1, sibling)]
        for cp in swap:
            cp.start()
        for cp in swap:
            cp.wait()
        chip_v[...] = mine_v[...] + sib_v[...]
        chip_g[...] = g_ref[...] + sib_g[...]

        rows_v = pl.ds(pl.multiple_of(c * hv, 8), hv)
        rows_g = pl.ds(pl.multiple_of(c * hg, 8), hg)
        got_v[chip] = chip_v[rows_v, :]
        got_g[chip] = chip_g[rows_g, :]
        sends = []
        for r in range(3):
            qx, qy = _chip_of(x, y, r)
            sends.append(_remote(chip_v.at[rows_v, :], got_v.at[chip], send_sems, recv_sems, 2 + r, (qx, qy, c)))
            sends.append(_remote(chip_g.at[rows_g, :], got_g.at[chip], send_sems, recv_sems, 5 + r, (qx, qy, c)))
        for cp in sends:
            cp.start()
        for cp in sends:
            cp.wait()
        vec_ref[rows_v, :] = ((got_v[0] + got_v[1]) + got_v[2]) + got_v[3]
        gsum_ref[rows_g, :] = ((got_g[0] + got_g[1]) + got_g[2]) + got_g[3]

        back = [_remote(vec_ref.at[rows_v, :], vec_ref.at[rows_v, :], send_sems, recv_sems, 8, sibling),
                _remote(gsum_ref.at[rows_g, :], gsum_ref.at[rows_g, :], send_sems, recv_sems, 9, sibling)]
        for cp in back:
            cp.start()
        theirs_v = vec_ref.at[pl.ds(pl.multiple_of((1 - c) * hv, 8), hv), :]
        theirs_g = gsum_ref.at[pl.ds(pl.multiple_of((1 - c) * hg, 8), hg), :]
        _remote(theirs_v, theirs_v, send_sems, recv_sems, 8, sibling).wait_recv()
        _remote(theirs_g, theirs_g, send_sems, recv_sems, 9, sibling).wait_recv()
        for cp in back:
            cp.wait_send()

    vmem = pl.BlockSpec(memory_space=pltpu.VMEM)
    n_sems = 10
    return pl.pallas_call(
        body, in_specs=[vmem] * (len(names) + 1), out_specs=[vmem, vmem],
        out_shape=[jax.ShapeDtypeStruct((VEC_ROWS, D_MODEL), F32), jax.ShapeDtypeStruct(gates.shape, F32)],
        scratch_shapes=[pltpu.VMEM((VEC_ROWS, D_MODEL), F32), pltpu.VMEM((VEC_ROWS, D_MODEL), F32),
                        pltpu.VMEM(gates.shape, F32), pltpu.VMEM((VEC_ROWS, D_MODEL), F32),
                        pltpu.VMEM(gates.shape, F32), pltpu.VMEM((N_CHIPS, hv, D_MODEL), F32),
                        pltpu.VMEM((N_CHIPS, hg) + gates.shape[1:], F32),
                        pltpu.SemaphoreType.DMA((n_sems,)), pltpu.SemaphoreType.DMA((n_sems,))],
        name="all_reduce_small",
    )(*[pieces[n] for n in names], gates)


def _adamw_math(w, g, m, v):
    m = ADAM_B1 * m + (1.0 - ADAM_B1) * g
    v = ADAM_B2 * v + (1.0 - ADAM_B2) * (g * g)
    m_hat = m / (1.0 - ADAM_B1 ** ADAM_STEP)
    v_hat = v / (1.0 - ADAM_B2 ** ADAM_STEP)
    delta = -ADAM_LR * (m_hat / (jnp.sqrt(v_hat) + ADAM_EPS) + ADAM_WD * w)
    return delta, m, v


def _adamw_big(w, g, m, v):
    steps = 8
    blks = []
    for name in BIG_NAMES:
        rows, cols, _ = BIG[name]
        blks.append(pl.BlockSpec((rows // steps, cols), lambda i: (i, 0)))

    def body(*refs):
        ins, outs = refs[:4 * N_BIG], refs[4 * N_BIG:]
        for a in range(N_BIG):
            w_ref, g_ref, m_ref, v_ref = (ins[k * N_BIG + a] for k in range(4))
            d, nm, nv = _adamw_math(w_ref[...], g_ref[...], m_ref[...], v_ref[...])
            outs[a][...] = d
            outs[N_BIG + a][...] = nm
            outs[2 * N_BIG + a][...] = nv

    shapes = [jax.ShapeDtypeStruct(BIG[name][:2], F32) for name in BIG_NAMES]
    out = pl.pallas_call(
        body, grid=(steps,), in_specs=blks * 4, out_specs=blks * 3, out_shape=shapes * 3,
        name="adamw_big", compiler_params=_params("parallel"),
    )(*[t[name] for t in (w, g, m, v) for name in BIG_NAMES])
    return {name: (out[a], out[N_BIG + a], out[2 * N_BIG + a]) for a, name in enumerate(BIG_NAMES)}


SMALL = {"meta_tokens": (N_META, D_MODEL // N_CHIPS), "mix_norm_g": (1, D_MODEL), "conv_w": (CONV_W, D_RG // N_CHIPS),
         "conv_b": (1, D_RG), "w_rgate": (D_RG, RG_HEAD_DIM), "b_rgate": (1, D_RG), "w_igate": (D_RG, RG_HEAD_DIM),
         "b_igate": (1, D_RG), "lru_lambda": (1, D_RG), "rg_norm_g": (1, D_RG), "hg_lower_bound": (2, D_HG),
         "hg_norm_g": (1, HG_HEAD_DIM), "ffn_norm_g": (1, D_MODEL), "final_norm_g": (1, D_MODEL)}
SMALL_NAMES = tuple(SMALL)
SHARDED_SMALL = ("meta_tokens", "conv_w")


def _adamw_small(vec, gates, w, m, v):
    n = len(SMALL_NAMES)

    def body(*refs):
        vec_ref, gates_ref = refs[:2]
        w_refs, m_refs, v_refs = refs[2:2 + n], refs[2 + n:2 + 2 * n], refs[2 + 2 * n:2 + 3 * n]
        outs = refs[2 + 3 * n:]
        loss_ref = outs[0]
        x, y, _ = _place()
        chip = 2 * x + y
        loss_ref[...] = vec_ref[VEC_ROW["loss"]:VEC_ROW["loss"] + 1, 0:1]

        def update(k, g):
            g_ref, d_ref, nm_ref, nv_ref = outs[1 + 4 * k:5 + 4 * k]
            g_ref[...] = g
            d_ref[...], nm_ref[...], nv_ref[...] = _adamw_math(w_refs[k][...], g, m_refs[k][...], v_refs[k][...])

        for k, name in enumerate(SMALL_NAMES):
            nr, w_ = SMALL[name]
            if name == "w_rgate":
                update(k, gates_ref[0:D_RG, :])
            elif name == "w_igate":
                update(k, gates_ref[D_RG:2 * D_RG, :])
            elif name in SHARDED_SMALL:
                r0 = VEC_ROW[name]
                for q in range(N_CHIPS):
                    @pl.when(chip == q)
                    def _(k=k, r0=r0, nr=nr, w_=w_, q=q):
                        update(k, vec_ref[r0:r0 + nr, q * w_:(q + 1) * w_])
            else:
                r0 = VEC_ROW[name]
                update(k, vec_ref[r0:r0 + nr, 0:w_])

    vmem = pl.BlockSpec(memory_space=pltpu.VMEM)
    out_shape = [jax.ShapeDtypeStruct((1, 1), F32)]
    for name in SMALL_NAMES:
        out_shape += [jax.ShapeDtypeStruct(SMALL[name], F32)] * 4
    outs = pl.pallas_call(
        body, in_specs=[vmem] * (2 + 3 * n), out_specs=[vmem] * len(out_shape), out_shape=out_shape,
        name="adamw_small",
    )(vec, gates, *[w[k] for k in SMALL_NAMES], *[m[k] for k in SMALL_NAMES], *[v[k] for k in SMALL_NAMES])
    loss = outs[0]
    res = {name: tuple(outs[1 + 4 * k:5 + 4 * k]) for k, name in enumerate(SMALL_NAMES)}
    return loss, res


WEIGHT_NAMES = ("meta_tokens", "mix_norm_g", "w_in", "conv_w", "conv_b", "w_rgate", "b_rgate", "w_igate", "b_igate",
                "lru_lambda", "rg_norm_g", "hg_lower_bound", "hg_norm_g", "w_out", "ffn_norm_g", "w_gate_up", "w_down",
                "final_norm_g")


def _to_2d(name, a):
    if name in BIG:
        return a.reshape(BIG[name][:2])
    return a.reshape(SMALL[name])


def kernel(x, meta_tokens, mix_norm_g, w_in, conv_w, conv_b, w_rgate, b_rgate, w_igate, b_igate, lru_lambda, rg_norm_g, hg_lower_bound, hg_norm_g, w_out, ffn_norm_g, w_gate_up, w_down, final_norm_g, loss_target, m_meta_tokens, m_mix_norm_g, m_w_in, m_conv_w, m_conv_b, m_w_rgate, m_b_rgate, m_w_igate, m_b_igate, m_lru_lambda, m_rg_norm_g, m_hg_lower_bound, m_hg_norm_g, m_w_out, m_ffn_norm_g, m_w_gate_up, m_w_down, m_final_norm_g, v_meta_tokens, v_mix_norm_g, v_w_in, v_conv_w, v_conv_b, v_w_rgate, v_b_rgate, v_w_igate, v_b_igate, v_lru_lambda, v_rg_norm_g, v_hg_lower_bound, v_hg_norm_g, v_w_out, v_ffn_norm_g, v_w_gate_up, v_w_down, v_final_norm_g):
    w_raw = dict(zip(WEIGHT_NAMES, (meta_tokens, mix_norm_g, w_in, conv_w, conv_b, w_rgate, b_rgate, w_igate, b_igate,
                                    lru_lambda, rg_norm_g, hg_lower_bound, hg_norm_g, w_out, ffn_norm_g, w_gate_up,
                                    w_down, final_norm_g)))
    m_raw = dict(zip(WEIGHT_NAMES, (m_meta_tokens, m_mix_norm_g, m_w_in, m_conv_w, m_conv_b, m_w_rgate, m_b_rgate,
                                    m_w_igate, m_b_igate, m_lru_lambda, m_rg_norm_g, m_hg_lower_bound, m_hg_norm_g,
                                    m_w_out, m_ffn_norm_g, m_w_gate_up, m_w_down, m_final_norm_g)))
    v_raw = dict(zip(WEIGHT_NAMES, (v_meta_tokens, v_mix_norm_g, v_w_in, v_conv_w, v_conv_b, v_w_rgate, v_b_rgate,
                                    v_w_igate, v_b_igate, v_lru_lambda, v_rg_norm_g, v_hg_lower_bound, v_hg_norm_g,
                                    v_w_out, v_ffn_norm_g, v_w_gate_up, v_w_down, v_final_norm_g)))
    w = {k: _to_2d(k, a) for k, a in w_raw.items()}
    m = {k: _to_2d(k, a) for k, a in m_raw.items()}
    v = {k: _to_2d(k, a) for k, a in v_raw.items()}

    x_i, y_i, c_i = _place()
    core = jnp.reshape(c_i, (1,)).astype(jnp.int32)
    chip = jnp.reshape(2 * x_i + y_i, (1,)).astype(jnp.int32)
    chip_core = jnp.concatenate([chip, core])

    placed = _cast_into_full(w, chip)
    first, (meta_full, cw_full) = _gather_weights(placed, [w["meta_tokens"], w["conv_w"]], ("w_in",), "gather_first", 1)
    rest, _ = _gather_weights(placed, [], ("w_out", "w_gate_up", "w_down"), "gather_rest", 2)
    full = {**first, **rest}

    seq = x.shape[1]
    h0 = jnp.concatenate([jnp.zeros((PAD, D_MODEL), F32), meta_full, x[0]], axis=0)
    target = jnp.concatenate([jnp.zeros((PAD + N_META, D_MODEL), F32), loss_target[0]], axis=0)
    small = {k: w[k] for k in SMALL_NAMES if k not in SHARDED_SMALL}
    small["conv_w"] = cw_full

    def reduce_to_chips(grads, names, tag, collective_ids):
        got = _exchange_halves(grads, names, "exchange_halves_" + tag, collective_ids[0])

        def chip_sums():
            return _chip_sum(grads, got, names, core, "chip_sum_" + tag)

        def send(sums):
            arrived = _send_chip_sums({n: sums[n][1] for n in names}, names, "send_chip_sums_" + tag,
                                      collective_ids[1])
            return {n: (sums[n][0], a) for n, a in zip(names, arrived)}

        return chip_sums, send

    ffn_names, mixer_names = ("w_gate_up", "w_down"), ("w_in", "w_out")
    loss, dh0, grads, parts, parts_mixer = _local_step(
        h0, target, full["w_in"], full["w_out"], full["w_gate_up"], full["w_down"], small,
        on_ffn_grads=lambda g: reduce_to_chips(g, ffn_names, "ffn", (3, 4)),
        on_mixer_grads=lambda g: reduce_to_chips(g, mixer_names, "mixer", (None, 5)))
    parts.update(parts_mixer)
    totals = _total(parts, chip_core)
    g_big = dict(zip(BIG_NAMES, _share_totals(totals)))

    pieces = {k: grads[k] for k in VEC_ROW if k not in ("loss", "meta_tokens")}
    pieces["loss"] = loss
    pieces["meta_tokens"] = dh0[PAD:PAD + N_META]
    vec, gates = _all_reduce_small(pieces, grads["w_gates"])
    loss_sum, res = _adamw_small(vec, gates, w, m, v)
    updates = _adamw_big(w, g_big, m, v)
    for n in BIG_NAMES:
        res[n] = (g_big[n],) + updates[n]

    grad_x = dh0[PAD + N_META:].reshape(1, seq, D_MODEL)
    out = [loss_sum.reshape(()), grad_x]
    for j in range(4):
        out += [res[n][j].reshape(w_raw[n].shape) for n in WEIGHT_NAMES]
    return tuple(out)
```

```python
import functools
import math

import jax
import jax.numpy as jnp
from jax import lax
from jax.experimental import pallas as pl
from jax.experimental.pallas import tpu as pltpu
from jax.experimental.pallas import tpu_sc as plsc

F32 = jnp.float32
BF16 = jnp.bfloat16
HIGHEST = lax.Precision.HIGHEST
MESH = pl.DeviceIdType.MESH

D_MODEL = 1024
D_RG = 512
RG_HEAD_DIM = 64
D_HG = 512
HG_HEAD_DIM = 128
HG_HEADS = 4
CHUNK = 64
SUB = 16
N_SUB = CHUNK // SUB
N_META = 16
PAD = CHUNK - N_META
D_IN = 3072
D_FF = 2816
CONV_W = 4
LRU_C = 8.0
EPS = 1e-6
EXP_CLAMP = 80.0
GELU_C = math.sqrt(2.0 / math.pi)
GELU_A = 0.044715
N_CHIPS = 4

ADAM_LR = 0.001
ADAM_B1 = 0.9
ADAM_B2 = 0.999
ADAM_EPS = 1e-08
ADAM_WD = 0.01
ADAM_STEP = 10

VMEM_LIMIT = 56 * 1024 * 1024


def _params(*sem):
    return pltpu.CompilerParams(dimension_semantics=sem, vmem_limit_bytes=VMEM_LIMIT)


def _row_tile(rows, target):
    best = None
    for t in range(16, min(rows, target) + 1, 16):
        if rows % t == 0:
            best = t
    assert best is not None, rows
    return best


def _sigmoid(x):
    return 0.5 * jnp.tanh(0.5 * x) + 0.5


def _dot(a, b):
    return jnp.dot(a, b, preferred_element_type=F32)


def _dot_nt(a, b):
    return lax.dot_general(a, b, (((1,), (1,)), ((), ())), preferred_element_type=F32)


def _dot_tn(a, b):
    return lax.dot_general(a, b, (((0,), (0,)), ((), ())), preferred_element_type=F32)


def _rms(x):
    return lax.rsqrt(jnp.mean(x * x, axis=-1, keepdims=True) + EPS)


def _rms_bwd(dn, n, r):
    return r * (dn - n * jnp.mean(dn * n, axis=-1, keepdims=True))


def _gelu_parts(x):
    t = jnp.tanh(GELU_C * (x + GELU_A * x * x * x))
    g = 0.5 * x * (1.0 + t)
    dg = 0.5 * (1.0 + t) + 0.5 * x * (1.0 - t * t) * GELU_C * (1.0 + 3.0 * GELU_A * x * x)
    return g, dg


def _softplus_neg(lam):
    e = jnp.exp(-jnp.abs(lam))
    w = 1.0 + e
    log1p = jnp.where(w == 1.0, e, jnp.log(w) * e / (w - 1.0))
    return jnp.maximum(-lam, 0.0) + log1p


def _head_mask():
    r = lax.broadcasted_iota(jnp.int32, (D_RG, D_RG), 0) // RG_HEAD_DIM
    c = lax.broadcasted_iota(jnp.int32, (D_RG, D_RG), 1) // RG_HEAD_DIM
    return r == c


def _head_fold():
    r = lax.broadcasted_iota(jnp.int32, (D_RG, RG_HEAD_DIM), 0) % RG_HEAD_DIM
    c = lax.broadcasted_iota(jnp.int32, (D_RG, RG_HEAD_DIM), 1)
    return (r == c).astype(F32)


def _gate_weights(w_r, w_i):
    def body(wr_ref, wi_ref, o_ref):
        fold = _head_fold()
        mask = _head_mask()
        for k, ref in enumerate((wr_ref, wi_ref)):
            full = lax.dot_general(ref[...], fold, (((1,), (1,)), ((), ())),
                                   precision=HIGHEST, preferred_element_type=F32)
            o_ref[:, k * D_RG:(k + 1) * D_RG] = jnp.where(mask, full, 0.0).astype(BF16)

    return pl.pallas_call(
        body, out_shape=jax.ShapeDtypeStruct((D_RG, 2 * D_RG), BF16), name="gate_weights",
    )(w_r, w_i)


HEAD = PAD + N_META


def _window_copies(seq_hbm, buf, sems, tm):
    def first(to_vmem):
        seq, vm = seq_hbm.at[pl.ds(0, tm - HEAD)], buf.at[0, pl.ds(HEAD, tm - HEAD)]
        return pltpu.make_async_copy(seq, vm, sems.at[0]) if to_vmem else pltpu.make_async_copy(vm, seq, sems.at[0])

    def later(j, slot, to_vmem):
        seq, vm = seq_hbm.at[pl.ds(pl.multiple_of(j * tm - HEAD, 8), tm)], buf.at[slot]
        if to_vmem:
            return pltpu.make_async_copy(seq, vm, sems.at[slot])
        return pltpu.make_async_copy(vm, seq, sems.at[slot])

    return first, later


def _fetch_window(seq_hbm, buf, sems, i, n_steps, tm):
    first, later = _window_copies(seq_hbm, buf, sems, tm)
    slot = i % 2

    @pl.when(i == 0)
    def _():
        first(True).start()

    if n_steps > 1:
        @pl.when(i + 1 < n_steps)
        def _():
            later(i + 1, 1 - slot, True).start()

    @pl.when(i == 0)
    def _():
        first(True).wait()

    if n_steps > 1:
        @pl.when(i > 0)
        def _():
            later(i, slot, True).wait()

    return slot


def _in_proj(x, meta, g1, w_in):
    T = x.shape[0] + HEAD
    tm = _row_tile(T, 416)
    n_steps = T // tm

    def body(x_hbm, meta_ref, g_ref, w_ref, p_ref, u_ref, h_ref, buf, sems):
        i = pl.program_id(0)
        slot = _fetch_window(x_hbm, buf, sems, i, n_steps, tm)

        @pl.when(i == 0)
        def _():
            buf[0, 0:PAD, :] = jnp.zeros((PAD, D_MODEL), F32)
            buf[0, PAD:HEAD, :] = meta_ref[...]

        h = buf[slot]
        h_ref[...] = h
        u = (h * _rms(h) * g_ref[...]).astype(BF16)
        u_ref[...] = u
        p_ref[...] = _dot(u, w_ref[...])

    return pl.pallas_call(
        body, grid=(n_steps,),
        in_specs=[pl.BlockSpec(memory_space=pl.ANY),
                  pl.BlockSpec((N_META, D_MODEL), lambda i: (0, 0)),
                  pl.BlockSpec((1, D_MODEL), lambda i: (0, 0)),
                  pl.BlockSpec((D_MODEL, D_IN), lambda i: (0, 0))],
        out_specs=[pl.BlockSpec((tm, D_IN), lambda i: (i, 0)),
                   pl.BlockSpec((tm, D_MODEL), lambda i: (i, 0)),
                   pl.BlockSpec((tm, D_MODEL), lambda i: (i, 0))],
        out_shape=[jax.ShapeDtypeStruct((T, D_IN), F32), jax.ShapeDtypeStruct((T, D_MODEL), BF16),
                   jax.ShapeDtypeStruct((T, D_MODEL), F32)],
        scratch_shapes=[pltpu.VMEM((2, tm, D_MODEL), F32), pltpu.SemaphoreType.DMA((2,))],
        name="in_proj", compiler_params=_params("arbitrary"),
    )(x, meta, g1, w_in)


def _scan_block_fwd(A, B, rowi):
    for d in (1, 2, 4):
        a_sh = pltpu.roll(A, d, axis=0)
        b_sh = pltpu.roll(B, d, axis=0)
        m = rowi >= d
        B = jnp.where(m, A * b_sh + B, B)
        A = jnp.where(m, A * a_sh, A)
    return A, B


def _scan_block_bwd(A, B, rowi):
    for d in (1, 2, 4):
        a_sh = pltpu.roll(A, 8 - d, axis=0)
        b_sh = pltpu.roll(B, 8 - d, axis=0)
        m = rowi < 8 - d
        B = jnp.where(m, A * b_sh + B, B)
        A = jnp.where(m, A * a_sh, A)
    return A, B


def _rg_gates(xc, w_ref, bg_ref, lam):
    pre = _dot(xc.astype(BF16), w_ref[...]) + bg_ref[...]
    r = _sigmoid(pre[:, :D_RG])
    ig = _sigmoid(pre[:, D_RG:])
    sp = _softplus_neg(lam)
    la = -LRU_C * sp * r
    a = jnp.exp(la)
    th = jnp.tanh(la)
    u = 1.0 - th
    rc = pl.reciprocal(u, approx=True)
    rc = rc * (2.0 - u * rc)
    rc = rc * (2.0 - u * rc)
    m2 = -2.0 * th * rc
    inv_m = lax.rsqrt(jnp.maximum(m2, 1e-30))
    return r, ig, sp, a, m2 * inv_m, inv_m


def _conv(ext, cw_ref, cb_ref, tm):
    xc = cb_ref[...] + cw_ref[0:1, :] * ext[8 - 3:8 - 3 + tm, :]
    for j in range(1, CONV_W):
        xc = xc + cw_ref[j:j + 1, :] * ext[8 - 3 + j:8 - 3 + j + tm, :]
    return xc


def _scan_unroll(blocks):
    return 4 if blocks % 4 == 0 else 2 if blocks % 2 == 0 else 1


def _rg_fwd(p, cw, cb, wg, bg, lam, rg_g):
    T = p.shape[0]
    tm = _row_tile(T, 416)
    unroll = _scan_unroll(tm // 8)

    def body(xg_ref, cw_ref, cb_ref, w_ref, bg_ref, lam_ref, g_ref, y_ref, h_ref, ext, a_s, b_s, carry):
        i = pl.program_id(0)

        @pl.when(i == 0)
        def _():
            ext[0:8, :] = jnp.zeros((8, D_RG), F32)
            carry[...] = jnp.zeros((1, D_RG), F32)

        ext[8:8 + tm, :] = xg_ref[:, :D_RG]
        xc = _conv(ext, cw_ref, cb_ref, tm)
        r, ig, sp, a, m, _ = _rg_gates(xc, w_ref, bg_ref, lam_ref[...])
        row = i * tm + lax.broadcasted_iota(jnp.int32, (tm, 1), 0)
        a_s[...] = a
        b_s[...] = jnp.where(row >= PAD, m * ig * xc, 0.0)
        rowi = lax.broadcasted_iota(jnp.int32, (8, D_RG), 0)

        def blk(j, c):
            for u in range(unroll):
                o = pl.multiple_of((j * unroll + u) * 8, 8)
                A, B = _scan_block_fwd(a_s[pl.ds(o, 8), :], b_s[pl.ds(o, 8), :], rowi)
                h = B + A * c
                h_ref[pl.ds(o, 8), :] = h
                c = h[7:8, :]
            return c

        carry[...] = lax.fori_loop(0, tm // (8 * unroll), blk, carry[...])
        ext[0:8, :] = ext[tm:tm + 8, :]
        g, _ = _gelu_parts(xg_ref[:, D_RG:])
        yy = g * h_ref[...]
        y_ref[...] = (yy * _rms(yy) * g_ref[...]).astype(BF16)

    vec = lambda n: pl.BlockSpec((1, n), lambda i: (0, 0))
    return pl.pallas_call(
        body, grid=(T // tm,),
        in_specs=[pl.BlockSpec((tm, 2 * D_RG), lambda i: (i, 0)),
                  pl.BlockSpec((CONV_W, D_RG), lambda i: (0, 0)), vec(D_RG),
                  pl.BlockSpec((D_RG, 2 * D_RG), lambda i: (0, 0)), vec(2 * D_RG), vec(D_RG), vec(D_RG)],
        out_specs=[pl.BlockSpec((tm, D_RG), lambda i: (i, 0)), pl.BlockSpec((tm, D_RG), lambda i: (i, 0))],
        out_shape=[jax.ShapeDtypeStruct((T, D_RG), BF16), jax.ShapeDtypeStruct((T, D_RG), F32)],
        scratch_shapes=[pltpu.VMEM((tm + 8, D_RG), F32), pltpu.VMEM((tm, D_RG), F32),
                        pltpu.VMEM((tm, D_RG), F32), pltpu.VMEM((1, D_RG), F32)],
        name="rg_fwd", compiler_params=_params("arbitrary"),
    )(p, cw, cb, wg, bg, lam, rg_g)


def _tri(lower):
    r = lax.broadcasted_iota(jnp.int32, (CHUNK, CHUNK), 0)
    c = lax.broadcasted_iota(jnp.int32, (CHUNK, CHUNK), 1)
    return ((c <= r) if lower else (c >= r)).astype(F32)


def _hg_gates(hq, hf, lbraw_ref, valid):
    lb = _sigmoid(lbraw_ref[0:1, :] - lbraw_ref[1:2, :])
    sq = _sigmoid(hq)
    q = hq * sq
    sf = _sigmoid(hf)
    f = lb + (1.0 - lb) * sf
    lf = jnp.where(valid, jnp.log(f), 0.0)
    b = jnp.dot(_tri(True), lf, precision=HIGHEST, preferred_element_type=F32)
    return lb, sq, q, sf, f, b


def _hg_head(qh, kh, bh):
    blk = lax.broadcasted_iota(jnp.int32, (CHUNK, 1), 0) // SUB
    b_last = bh[CHUNK - 1:CHUNK, :]
    refs = [bh[SUB * s:SUB * s + 1, :] for s in range(N_SUB)]
    r_sel = refs[N_SUB - 1]
    for s in range(N_SUB - 2, -1, -1):
        r_sel = jnp.where(blk == s, refs[s], r_sel)
    eb = jnp.exp(bh)
    eq = jnp.exp(bh - r_sel)
    ekh = jnp.exp(b_last - bh)
    ek = [jnp.exp(jnp.minimum(refs[s] - bh, EXP_CLAMP)) for s in range(N_SUB)]
    qe = qh * eq
    q_hat = jnp.concatenate([jnp.where(blk == s, qe, 0.0) for s in range(N_SUB)], axis=1)
    k_til = jnp.concatenate([kh * ek[s] for s in range(N_SUB)], axis=1)
    return blk, b_last, eb, eq, ekh, ek, q_hat, k_til


def _causal():
    r = lax.broadcasted_iota(jnp.int32, (CHUNK, CHUNK), 0)
    c = lax.broadcasted_iota(jnp.int32, (CHUNK, CHUNK), 1)
    return r >= c


def _chunks_per_step(n_chunks):
    for c in (5, 4, 3, 2):
        if n_chunks % c == 0:
            return c
    return 1


def _hg_fwd(p, lbraw, hg_g):
    T = p.shape[0]
    n_chunks = T // CHUNK
    cps = _chunks_per_step(n_chunks)
    rows = cps * CHUNK

    def body(hq_ref, hf_ref, hi_ref, hg_ref, lb_ref, g_ref, y_ref, o_ref, st_all_ref, st):
        i = pl.program_id(0)

        @pl.when(i == 0)
        def _():
            st[...] = jnp.zeros_like(st)

        def chunk(j, carry):
            rs = pl.ds(pl.multiple_of(j * CHUNK, CHUNK), CHUNK)
            chunk_body(i * cps + j, hq_ref.at[rs, :], hf_ref.at[rs, :], hi_ref.at[rs, :], hg_ref.at[rs, :], lb_ref,
                       g_ref, y_ref.at[rs, :], o_ref.at[rs, :], st_all_ref.at[pl.ds(j, 1)], st)
            return carry

        lax.fori_loop(0, cps, chunk, 0, unroll=True)

    def chunk_body(n, hq_ref, hf_ref, hi_ref, hg_ref, lb_ref, g_ref, y_ref, o_ref, st_all_ref, st):
        valid = (n * CHUNK + lax.broadcasted_iota(jnp.int32, (CHUNK, 1), 0)) >= PAD
        hq, hf, v, hg = hq_ref[...], hf_ref[...], hi_ref[...], hg_ref[...]
        lb, sq, q, sf, f, b = _hg_gates(hq, hf, lb_ref, valid)
        k = 1.0 - f
        st_all_ref[0] = st[...]
        causal = _causal()
        v_t = v.T.astype(BF16)
        heads = [slice(h * HG_HEAD_DIM, (h + 1) * HG_HEAD_DIM) for h in range(HG_HEADS)]
        fac = []
        for sl in heads:
            qh, kh, bh = q[:, sl], k[:, sl], b[:, sl]
            _, b_last, eb, _, ekh, _, q_hat, k_til = _hg_head(qh, kh, bh)
            fac.append((jnp.exp(b_last), (qh * eb).astype(BF16), q_hat.astype(BF16), k_til.astype(BF16),
                        (kh * ekh).astype(BF16), v[:, sl].astype(BF16)))
        raw = []
        for sl, (_, q_til, q_hat, k_til, k_hat, _) in zip(heads, fac):
            st_h = st[sl, :]
            raw.append((_dot_nt(q_til, st_h.astype(BF16)), _dot_nt(q_hat, k_til), _dot(v_t[sl, :], k_hat), st_h))
        for sl, (e_last, _, _, _, _, vb), (inter, att, upd, st_h) in zip(heads, fac, raw):
            o = inter + _dot(jnp.where(causal, att, 0.0).astype(BF16), vb)
            st[sl, :] = st_h * e_last + upd
            o_ref[:, sl] = o
            hgh = hg[:, sl]
            y_ref[:, sl] = (o * _rms(o) * g_ref[...] * (hgh * _sigmoid(hgh))).astype(BF16)

    col = lambda j: pl.BlockSpec((rows, D_HG), lambda n: (n, j))
    return pl.pallas_call(
        body, grid=(n_chunks // cps,),
        in_specs=[col(2), col(3), col(4), col(5),
                  pl.BlockSpec((2, D_HG), lambda n: (0, 0)), pl.BlockSpec((1, HG_HEAD_DIM), lambda n: (0, 0))],
        out_specs=[pl.BlockSpec((rows, D_HG), lambda n: (n, 0)), pl.BlockSpec((rows, D_HG), lambda n: (n, 0)),
                   pl.BlockSpec((cps, D_HG, HG_HEAD_DIM), lambda n: (n, 0, 0))],
        out_shape=[jax.ShapeDtypeStruct((T, D_HG), BF16), jax.ShapeDtypeStruct((T, D_HG), F32),
                   jax.ShapeDtypeStruct((n_chunks, D_HG, HG_HEAD_DIM), F32)],
        scratch_shapes=[pltpu.VMEM((D_HG, HG_HEAD_DIM), F32)],
        name="hg_fwd", compiler_params=_params("arbitrary"),
    )(p, p, p, p, lbraw, hg_g)


def _out_proj(h0, y_rg, y_hg, w_out, g2):
    T = h0.shape[0]
    tm = _row_tile(T, 832)

    def body(h_ref, yr_ref, yh_ref, w_ref, g_ref, h1_ref, v_ref, y_ref):
        y_ref[:, :D_RG] = yr_ref[...]
        y_ref[:, D_RG:] = yh_ref[...]
        h1 = h_ref[...] + _dot(y_ref[...], w_ref[...])
        h1_ref[...] = h1
        v_ref[...] = (h1 * _rms(h1) * g_ref[...]).astype(BF16)

    row = lambda n: pl.BlockSpec((tm, n), lambda i: (i, 0))
    return pl.pallas_call(
        body, grid=(T // tm,),
        in_specs=[row(D_MODEL), row(D_RG), row(D_HG), pl.BlockSpec((D_MODEL, D_MODEL), lambda i: (0, 0)),
                  pl.BlockSpec((1, D_MODEL), lambda i: (0, 0))],
        out_specs=[row(D_MODEL), row(D_MODEL), row(D_MODEL)],
        out_shape=[jax.ShapeDtypeStruct((T, D_MODEL), F32), jax.ShapeDtypeStruct((T, D_MODEL), BF16),
                   jax.ShapeDtypeStruct((T, D_MODEL), BF16)],
        name="out_proj", compiler_params=_params("parallel"),
    )(h0, y_rg, y_hg, w_out, g2)


def _gate_up(v, w_gu):
    T = v.shape[0]
    tm = _row_tile(T, 416)

    def body(v_ref, w_ref, gu_ref, act_ref):
        gu = _dot(v_ref[...], w_ref[...])
        gu_ref[...] = gu.astype(BF16)
        g = gu[:, :D_FF]
        act_ref[...] = (g * _sigmoid(g) * gu[:, D_FF:]).astype(BF16)

    row = lambda n: pl.BlockSpec((tm, n), lambda i: (i, 0))
    return pl.pallas_call(
        body, grid=(T // tm,),
        in_specs=[row(D_MODEL), pl.BlockSpec((D_MODEL, 2 * D_FF), lambda i: (0, 0))],
        out_specs=[row(2 * D_FF), row(D_FF)],
        out_shape=[jax.ShapeDtypeStruct((T, 2 * D_FF), BF16), jax.ShapeDtypeStruct((T, D_FF), BF16)],
        name="gate_up", compiler_params=_params("parallel"),
    )(v, w_gu)


def _down_loss(h1, act, w_down, gf, target):
    T = h1.shape[0]
    tm = _row_tile(T, 832)
    n_steps = T // tm

    def body(h_ref, a_ref, w_ref, g_ref, t_hbm, dh2_ref, dh2b_ref, loss_ref, gg_ref, tbuf, sems):
        i = pl.program_id(0)
        slot = _fetch_window(t_hbm, tbuf, sems, i, n_steps, tm)

        @pl.when(i == 0)
        def _():
            loss_ref[...] = jnp.zeros_like(loss_ref)
            gg_ref[...] = jnp.zeros_like(gg_ref)
            tbuf[0, 0:HEAD, :] = jnp.zeros((HEAD, D_MODEL), F32)

        h2 = h_ref[...] + _dot(a_ref[...], w_ref[...])
        r = _rms(h2)
        n = h2 * r
        gf_ = g_ref[...]
        row = i * tm + lax.broadcasted_iota(jnp.int32, (tm, 1), 0)
        err = jnp.where(row >= HEAD, n * gf_ - tbuf[slot], 0.0)
        loss_ref[...] += 0.5 * jnp.sum(jnp.mean(err * err, axis=-1, keepdims=True), axis=0, keepdims=True)
        dy = err * (1.0 / D_MODEL)
        gg_ref[...] += jnp.sum(dy * n, axis=0, keepdims=True)
        dh2 = _rms_bwd(dy * gf_, n, r)
        dh2_ref[...] = dh2
        dh2b_ref[...] = dh2.astype(BF16)

    row_spec = lambda n: pl.BlockSpec((tm, n), lambda i: (i, 0))
    return pl.pallas_call(
        body, grid=(T // tm,),
        in_specs=[row_spec(D_MODEL), row_spec(D_FF), pl.BlockSpec((D_FF, D_MODEL), lambda i: (0, 0)),
                  pl.BlockSpec((1, D_MODEL), lambda i: (0, 0)), pl.BlockSpec(memory_space=pl.ANY)],
        out_specs=[row_spec(D_MODEL), row_spec(D_MODEL), pl.BlockSpec((1, 1), lambda i: (0, 0)),
                   pl.BlockSpec((1, D_MODEL), lambda i: (0, 0))],
        out_shape=[jax.ShapeDtypeStruct((T, D_MODEL), F32), jax.ShapeDtypeStruct((T, D_MODEL), BF16),
                   jax.ShapeDtypeStruct((1, 1), F32), jax.ShapeDtypeStruct((1, D_MODEL), F32)],
        scratch_shapes=[pltpu.VMEM((2, tm, D_MODEL), F32), pltpu.SemaphoreType.DMA((2,))],
        name="down_loss", compiler_params=_params("arbitrary"),
    )(h1, act, w_down, gf, target)


def _ffn_bwd_act(dh2b, gu, w_down):
    T = dh2b.shape[0]
    tm = _row_tile(T, 416)

    def body(d_ref, gu_ref, w_ref, dgu_ref):
        dact = _dot_nt(d_ref[...], w_ref[...])
        g = gu_ref[:, :D_FF].astype(F32)
        u = gu_ref[:, D_FF:].astype(F32)
        s = _sigmoid(g)
        dgu_ref[:, :D_FF] = (dact * u * s * (1.0 + g * (1.0 - s))).astype(BF16)
        dgu_ref[:, D_FF:] = (dact * g * s).astype(BF16)

    row = lambda n: pl.BlockSpec((tm, n), lambda i: (i, 0))
    return pl.pallas_call(
        body, grid=(T // tm,),
        in_specs=[row(D_MODEL), row(2 * D_FF), pl.BlockSpec((D_FF, D_MODEL), lambda i: (0, 0))],
        out_specs=row(2 * D_FF),
        out_shape=jax.ShapeDtypeStruct((T, 2 * D_FF), BF16),
        name="ffn_bwd_act", compiler_params=_params("parallel"),
    )(dh2b, gu, w_down)


def _ffn_bwd_in(dgu, w_gu, h1, g2, dh2, w_out):
    T = h1.shape[0]
    tm = _row_tile(T, 416)

    def body(dgu_ref, wgu_ref, h_ref, g_ref, d2_ref, wo_ref, dh1_ref, dh1b_ref, dy_ref, gg_ref):
        i = pl.program_id(0)

        @pl.when(i == 0)
        def _():
            gg_ref[...] = jnp.zeros_like(gg_ref)

        dv = _dot_nt(dgu_ref[...], wgu_ref[...])
        h1 = h_ref[...]
        r = _rms(h1)
        n = h1 * r
        gg_ref[...] += jnp.sum(dv * n, axis=0, keepdims=True)
        dh1 = d2_ref[...] + _rms_bwd(dv * g_ref[...], n, r)
        dh1_ref[...] = dh1
        db = dh1.astype(BF16)
        dh1b_ref[...] = db
        dy_ref[...] = _dot_nt(db, wo_ref[...])

    row = lambda n: pl.BlockSpec((tm, n), lambda i: (i, 0))
    return pl.pallas_call(
        body, grid=(T // tm,),
        in_specs=[row(2 * D_FF), pl.BlockSpec((D_MODEL, 2 * D_FF), lambda i: (0, 0)),
                  row(D_MODEL), pl.BlockSpec((1, D_MODEL), lambda i: (0, 0)), row(D_MODEL),
                  pl.BlockSpec((D_MODEL, D_MODEL), lambda i: (0, 0))],
        out_specs=[row(D_MODEL), row(D_MODEL), row(D_MODEL), pl.BlockSpec((1, D_MODEL), lambda i: (0, 0))],
        out_shape=[jax.ShapeDtypeStruct((T, D_MODEL), F32), jax.ShapeDtypeStruct((T, D_MODEL), BF16),
                   jax.ShapeDtypeStruct((T, D_MODEL), F32), jax.ShapeDtypeStruct((1, D_MODEL), F32)],
        name="ffn_bwd_in", compiler_params=_params("arbitrary"),
    )(dgu, w_gu, h1, g2, dh2, w_out)


def _rg_bwd(p, hs, dy, dp, cw, cb, wg, bg, lam, rg_g):
    T = p.shape[0]
    tm = _row_tile(T, 208)
    nt = T // tm
    hb = tm // 8
    unroll = _scan_unroll(hb)

    def body(xg_ref, xh_ref, h_ref, hh_ref, dy_ref, dp_in_ref, cw_ref, cb_ref, w_ref, bg_ref, lam_ref, g_ref,
             dp_ref, gcw_ref, gcb_ref, gw_ref, gbg_ref, glam_ref, gg_ref,
             ext, dext, a_s, b_s, d_s, gacc, carry_d, carry_a):
        i = pl.program_id(0)
        t_idx = nt - 1 - i

        @pl.when(i == 0)
        def _():
            dext[tm:tm + 8, :] = jnp.zeros((8, D_RG), F32)
            carry_d[...] = jnp.zeros_like(carry_d)
            carry_a[...] = jnp.zeros_like(carry_a)
            gacc[...] = jnp.zeros_like(gacc)
            for ref in (gcw_ref, gcb_ref, gbg_ref, glam_ref, gg_ref, gw_ref):
                ref[...] = jnp.zeros_like(ref)

        first = t_idx == 0
        ext[0:8, :] = jnp.where(first, 0.0, xh_ref[:, :D_RG])
        ext[8:8 + tm, :] = xg_ref[:, :D_RG]
        xc = _conv(ext, cw_ref, cb_ref, tm)
        lam_ = lam_ref[...]
        r, ig, sp, a, m, inv_m = _rg_gates(xc, w_ref, bg_ref, lam_)
        row = t_idx * tm + lax.broadcasted_iota(jnp.int32, (tm, 1), 0)
        valid = row >= PAD

        gr = xg_ref[:, D_RG:]
        g, dgelu = _gelu_parts(gr)
        h = h_ref[...]
        yy = g * h
        rr = _rms(yy)
        nn = yy * rr
        dy_ = dy_ref[...]
        gg_ref[...] += jnp.sum(dy_ * nn, axis=0, keepdims=True)
        dyy = _rms_bwd(dy_ * g_ref[...], nn, rr)
        dp_ref[:, D_RG:] = (dyy * h * dgelu).astype(BF16)

        a_s[...] = a
        b_s[...] = dyy * g
        rowi = lax.broadcasted_iota(jnp.int32, (8, D_RG), 0)

        def blk(jj, c):
            cd, ca = c
            for u in range(unroll):
                o = pl.multiple_of((hb - 1 - (jj * unroll + u)) * 8, 8)
                a_blk = a_s[pl.ds(o, 8), :]
                a_next = jnp.where(rowi == 7, ca, pltpu.roll(a_blk, 7, axis=0))
                A, B = _scan_block_bwd(a_next, b_s[pl.ds(o, 8), :], rowi)
                d = B + A * cd
                d_s[pl.ds(o, 8), :] = d
                cd, ca = d[0:1, :], a_blk[0:1, :]
            return cd, ca

        cd, ca = lax.fori_loop(0, hb // unroll, blk, (carry_d[...], carry_a[...]))
        carry_d[...] = cd
        carry_a[...] = ca
        delta = d_s[...]

        h_last_prev = jnp.where(first, 0.0, hh_ref[7:8, :])
        row0 = lax.broadcasted_iota(jnp.int32, (tm, 1), 0) == 0
        h_prev = jnp.where(row0, h_last_prev, pltpu.roll(h, 1, axis=0))
        dbx = jnp.where(valid, delta, 0.0)
        da = delta * h_prev
        di = dbx * m * xc
        dm = dbx * ig * xc
        dla = a * (da - dm * a * inv_m)
        dla = jnp.where(valid, dla, 0.0)
        glam_ref[...] += jnp.sum(dla * r, axis=0, keepdims=True) * (LRU_C / (1.0 + jnp.exp(lam_)))
        dr = (-LRU_C) * sp * dla
        dpre = jnp.concatenate([dr * r * (1.0 - r), di * ig * (1.0 - ig)], axis=1)
        gbg_ref[...] += jnp.sum(dpre, axis=0, keepdims=True)
        dpre_b = dpre.astype(BF16)
        gacc[...] += _dot_tn(xc.astype(BF16), dpre_b)
        dxc = dbx * m * ig + _dot_nt(dpre_b, w_ref[...])
        gcb_ref[...] += jnp.sum(dxc, axis=0, keepdims=True)
        for j in range(CONV_W):
            gcw_ref[j:j + 1, :] += jnp.sum(dxc * ext[8 - 3 + j:8 - 3 + j + tm, :], axis=0, keepdims=True)
        dext[0:tm, :] = dxc
        dxr = cw_ref[0:1, :] * dext[3:3 + tm, :]
        for j in range(1, CONV_W):
            dxr = dxr + cw_ref[j:j + 1, :] * dext[3 - j:3 - j + tm, :]
        dp_ref[:, :D_RG] = dxr.astype(BF16)
        dext[tm:tm + 8, :] = dext[0:8, :]

        @pl.when(i == nt - 1)
        def _():
            fold = _head_fold()
            mask = _head_mask()
            for k in range(2):
                blockdiag = jnp.where(mask, gacc[:, k * D_RG:(k + 1) * D_RG], 0.0)
                gw_ref[k * D_RG:(k + 1) * D_RG, :] = jnp.dot(blockdiag, fold, precision=HIGHEST,
                                                             preferred_element_type=F32)

    vec = lambda n: pl.BlockSpec((1, n), lambda i: (0, 0))
    rev = lambda n: pl.BlockSpec((tm, n), lambda i: (nt - 1 - i, 0))
    halo = lambda n: pl.BlockSpec((8, n), lambda i: (jnp.maximum((nt - 1 - i) * hb - 1, 0), 0))
    return pl.pallas_call(
        body, grid=(nt,),
        in_specs=[rev(2 * D_RG), halo(2 * D_RG), rev(D_RG), halo(D_RG), rev(D_RG), ANY,
                  pl.BlockSpec((CONV_W, D_RG), lambda i: (0, 0)), vec(D_RG),
                  pl.BlockSpec((D_RG, 2 * D_RG), lambda i: (0, 0)), vec(2 * D_RG), vec(D_RG), vec(D_RG)],
        out_specs=[rev(2 * D_RG), pl.BlockSpec((CONV_W, D_RG), lambda i: (0, 0)), vec(D_RG),
                   pl.BlockSpec((2 * D_RG, RG_HEAD_DIM), lambda i: (0, 0)), vec(2 * D_RG), vec(D_RG), vec(D_RG)],
        input_output_aliases={5: 0},
        out_shape=[jax.ShapeDtypeStruct((T, D_IN), BF16), jax.ShapeDtypeStruct((CONV_W, D_RG), F32),
                   jax.ShapeDtypeStruct((1, D_RG), F32), jax.ShapeDtypeStruct((2 * D_RG, RG_HEAD_DIM), F32),
                   jax.ShapeDtypeStruct((1, 2 * D_RG), F32), jax.ShapeDtypeStruct((1, D_RG), F32),
                   jax.ShapeDtypeStruct((1, D_RG), F32)],
        scratch_shapes=[pltpu.VMEM((tm + 8, D_RG), F32), pltpu.VMEM((tm + 8, D_RG), F32),
                        pltpu.VMEM((tm, D_RG), F32), pltpu.VMEM((tm, D_RG), F32), pltpu.VMEM((tm, D_RG), F32),
                        pltpu.VMEM((D_RG, 2 * D_RG), F32), pltpu.VMEM((1, D_RG), F32), pltpu.VMEM((1, D_RG), F32)],
        name="rg_bwd", compiler_params=_params("arbitrary"),
    )(p, p, hs, hs, dy, dp, cw, cb, wg, bg, lam, rg_g)


def _hg_bwd(p, o_all, st_all, dy, lbraw, hg_g):
    T = p.shape[0]
    n_chunks = T // CHUNK
    cps = _chunks_per_step(n_chunks)
    rows = cps * CHUNK
    n_steps = n_chunks // cps

    def body(hq_ref, hf_ref, hi_ref, hg_ref, o_ref, st_ref, dy_ref, lb_ref, g_ref,
             dp_ref, glb_ref, gg_ref, dst):
        i = pl.program_id(0)

        @pl.when(i == 0)
        def _():
            dst[...] = jnp.zeros_like(dst)
            glb_ref[...] = jnp.zeros_like(glb_ref)
            gg_ref[...] = jnp.zeros_like(gg_ref)

        dp_ref[:, :2 * D_RG] = jnp.zeros((rows, 2 * D_RG), BF16)

        def chunk(jj, carry):
            j = cps - 1 - jj
            rs = pl.ds(pl.multiple_of(j * CHUNK, CHUNK), CHUNK)
            chunk_body((n_steps - 1 - i) * cps + j, hq_ref.at[rs, :], hf_ref.at[rs, :], hi_ref.at[rs, :],
                       hg_ref.at[rs, :], o_ref.at[rs, :], st_ref.at[pl.ds(j, 1)], dy_ref.at[rs, :], lb_ref, g_ref,
                       dp_ref.at[rs, pl.ds(2 * D_RG, 4 * D_HG)], glb_ref, gg_ref, dst)
            return carry

        lax.fori_loop(0, cps, chunk, 0, unroll=True)

    def chunk_body(n, hq_ref, hf_ref, hi_ref, hg_ref, o_ref, st_ref, dy_ref, lb_ref, g_ref,
                   dp_ref, glb_ref, gg_ref, dst):
        valid = (n * CHUNK + lax.broadcasted_iota(jnp.int32, (CHUNK, 1), 0)) >= PAD
        hq, hf, v, hg = hq_ref[...], hf_ref[...], hi_ref[...], hg_ref[...]
        lb, sq, q, sf, f, b = _hg_gates(hq, hf, lb_ref, valid)
        k = 1.0 - f
        causal = _causal()
        r_i = lax.broadcasted_iota(jnp.int32, (CHUNK, CHUNK), 0)
        c_i = lax.broadcasted_iota(jnp.int32, (CHUNK, CHUNK), 1)
        causal_t = r_i <= c_i
        is_last = lax.broadcasted_iota(jnp.int32, (CHUNK, 1), 0) == CHUNK - 1
        g_ = g_ref[...]
        db_parts, dq_parts, dk_parts = [], [], []
        gg = jnp.zeros((1, HG_HEAD_DIM), F32)
        heads = [slice(h * HG_HEAD_DIM, (h + 1) * HG_HEAD_DIM) for h in range(HG_HEADS)]

        do_parts = []
        for h, sl in enumerate(heads):
            o = o_ref[:, sl]
            ro = _rms(o)
            no = o * ro
            hgh = hg[:, sl]
            sg = _sigmoid(hgh)
            dyh = dy_ref[:, sl]
            dp_ref[:, 3 * D_HG + h * HG_HEAD_DIM:3 * D_HG + (h + 1) * HG_HEAD_DIM] = (
                dyh * no * g_ * sg * (1.0 + hgh * (1.0 - sg))).astype(BF16)
            dng = dyh * hgh * sg
            gg = gg + jnp.sum(dng * no, axis=0, keepdims=True)
            do_parts.append(_rms_bwd(dng * g_, no, ro))
        do_t = jnp.concatenate(do_parts, axis=1).T.astype(BF16)

        fac = []
        for sl, do in zip(heads, do_parts):
            qh, kh, bh = q[:, sl], k[:, sl], b[:, sl]
            blk, b_last, eb, eq, ekh, ek, q_hat, k_til = _hg_head(qh, kh, bh)
            fac.append(dict(qh=qh, kh=kh, blk=blk, e_last=jnp.exp(b_last), eb=eb, eq=eq, ekh=ekh, ek=ek,
                            q_til=qh * eb, k_hat=kh * ekh, qhb=q_hat.astype(BF16), ktb=k_til.astype(BF16),
                            vb=v[:, sl].astype(BF16), dob=do.astype(BF16)))

        first = []
        for sl, t in zip(heads, fac):
            st_h = st_ref[0, sl, :]
            dst_h = dst[sl, :]
            dstb = dst_h.astype(BF16)
            first.append(dict(
                att_t=_dot_nt(t["ktb"], t["qhb"]), datt=_dot_nt(t["dob"], t["vb"]),
                datt_t=_dot_nt(t["vb"], t["dob"]), dk_hat=_dot(t["vb"], dstb),
                dv=_dot_nt(t["k_hat"].astype(BF16), dstb), dq_til=_dot(t["dob"], st_h.astype(BF16)),
                state=t["e_last"] * jnp.sum(dst_h * st_h, axis=0, keepdims=True)))
            dst[sl, :] = dst_h * t["e_last"] + _dot(do_t[sl, :], t["q_til"].astype(BF16))

        for h, (t, m) in enumerate(zip(fac, first)):
            qh, kh, blk, eb, eq, ekh, ek = t["qh"], t["kh"], t["blk"], t["eb"], t["eq"], t["ekh"], t["ek"]
            q_til, k_hat, qhb, ktb, dob = t["q_til"], t["k_hat"], t["qhb"], t["ktb"], t["dob"]
            dk_hat, dq_til = m["dk_hat"], m["dq_til"]
            dv = m["dv"] + _dot(jnp.where(causal_t, m["att_t"], 0.0).astype(BF16), dob)
            dq_hat = _dot(jnp.where(causal, m["datt"], 0.0).astype(BF16), ktb)
            dk_til = _dot(jnp.where(causal_t, m["datt_t"], 0.0).astype(BF16), qhb)
            db_last = jnp.sum(dk_hat * k_hat, axis=0, keepdims=True) + m["state"]
            dq_sel = dq_hat[:, (N_SUB - 1) * HG_HEAD_DIM:]
            for s in range(N_SUB - 2, -1, -1):
                dq_sel = jnp.where(blk == s, dq_hat[:, s * HG_HEAD_DIM:(s + 1) * HG_HEAD_DIM], dq_sel)
            dq_a = dq_sel * eq
            dk_a = dk_til[:, :HG_HEAD_DIM] * ek[0]
            for s in range(1, N_SUB):
                dk_a = dk_a + dk_til[:, s * HG_HEAD_DIM:(s + 1) * HG_HEAD_DIM] * ek[s]
            db_att = qhb.astype(F32) * dq_hat - ktb.astype(F32) * dk_til
            db = dq_til * q_til - dk_hat * k_hat
            for s in range(N_SUB):
                db = db + db_att[:, s * HG_HEAD_DIM:(s + 1) * HG_HEAD_DIM]
            db_parts.append(jnp.where(is_last, db + db_last, db))
            dq_parts.append(dq_til * eb + dq_a)
            dk_parts.append(dk_hat * ekh + dk_a)
            dp_ref[:, 2 * D_HG + h * HG_HEAD_DIM:2 * D_HG + (h + 1) * HG_HEAD_DIM] = dv.astype(BF16)

        gg_ref[...] += gg
        db = jnp.concatenate(db_parts, axis=1)
        dq = jnp.concatenate(dq_parts, axis=1)
        dk = jnp.concatenate(dk_parts, axis=1)
        dlf = jnp.where(valid, jnp.dot(_tri(False), db, precision=HIGHEST, preferred_element_type=F32), 0.0)
        dp_ref[:, :D_HG] = (dq * sq * (1.0 + hq * (1.0 - sq))).astype(BF16)
        df = dlf / f - dk
        dlb = jnp.sum(df * (1.0 - sf), axis=0, keepdims=True) * lb * (1.0 - lb)
        glb_ref[0:1, :] += dlb
        glb_ref[1:2, :] += -dlb
        dp_ref[:, D_HG:2 * D_HG] = (df * (1.0 - lb) * sf * (1.0 - sf)).astype(BF16)

    rev = lambda j: pl.BlockSpec((rows, D_HG), lambda i: (n_steps - 1 - i, j))
    return pl.pallas_call(
        body, grid=(n_steps,),
        in_specs=[rev(2), rev(3), rev(4), rev(5), rev(0),
                  pl.BlockSpec((cps, D_HG, HG_HEAD_DIM), lambda i: (n_steps - 1 - i, 0, 0)), rev(1),
                  pl.BlockSpec((2, D_HG), lambda i: (0, 0)), pl.BlockSpec((1, HG_HEAD_DIM), lambda i: (0, 0))],
        out_specs=[pl.BlockSpec((rows, D_IN), lambda i: (n_steps - 1 - i, 0)),
                   pl.BlockSpec((2, D_HG), lambda i: (0, 0)), pl.BlockSpec((1, HG_HEAD_DIM), lambda i: (0, 0))],
        out_shape=[jax.ShapeDtypeStruct((T, D_IN), BF16), jax.ShapeDtypeStruct((2, D_HG), F32),
                   jax.ShapeDtypeStruct((1, HG_HEAD_DIM), F32)],
        scratch_shapes=[pltpu.VMEM((D_HG, HG_HEAD_DIM), F32)],
        name="hg_bwd", compiler_params=_params("arbitrary"),
    )(p, p, p, p, o_all, st_all, dy, lbraw, hg_g)


def _in_bwd(dp, w_in, h0, g1, dh1):
    T = h0.shape[0]
    tm = _row_tile(T, 416)
    n_steps = T // tm

    def body(dp_ref, w_ref, h_ref, g_ref, d1_ref, gx_hbm, gmeta_ref, gg_ref, buf, sems):
        i = pl.program_id(0)
        first, later = _window_copies(gx_hbm, buf, sems, tm)
        slot = i % 2

        @pl.when(i == 0)
        def _():
            gg_ref[...] = jnp.zeros_like(gg_ref)

        if n_steps > 2:
            @pl.when(i == 2)
            def _():
                first(False).wait()

            @pl.when(i > 2)
            def _():
                later(i - 2, slot, False).wait()

        du = _dot_nt(dp_ref[...], w_ref[...])
        h0_ = h_ref[...]
        r = _rms(h0_)
        n = h0_ * r
        gg_ref[...] += jnp.sum(du * n, axis=0, keepdims=True)
        dh0 = d1_ref[...] + _rms_bwd(du * g_ref[...], n, r)
        buf[slot] = dh0

        @pl.when(i == 0)
        def _():
            gmeta_ref[...] = dh0[PAD:HEAD, :]
            first(False).start()

        if n_steps > 1:
            @pl.when(i > 0)
            def _():
                later(i, slot, False).start()

        @pl.when(i == n_steps - 1)
        def _():
            if n_steps == 1:
                first(False).wait()
            else:
                if n_steps == 2:
                    first(False).wait()
                else:
                    later(i - 1, 1 - slot, False).wait()
                later(i, slot, False).wait()

    row = lambda n: pl.BlockSpec((tm, n), lambda i: (i, 0))
    return pl.pallas_call(
        body, grid=(n_steps,),
        in_specs=[row(D_IN), pl.BlockSpec((D_MODEL, D_IN), lambda i: (0, 0)),
                  row(D_MODEL), pl.BlockSpec((1, D_MODEL), lambda i: (0, 0)), row(D_MODEL)],
        out_specs=[pl.BlockSpec(memory_space=pl.ANY), pl.BlockSpec((N_META, D_MODEL), lambda i: (0, 0)),
                   pl.BlockSpec((1, D_MODEL), lambda i: (0, 0))],
        out_shape=[jax.ShapeDtypeStruct((T - HEAD, D_MODEL), F32), jax.ShapeDtypeStruct((N_META, D_MODEL), F32),
                   jax.ShapeDtypeStruct((1, D_MODEL), F32)],
        scratch_shapes=[pltpu.VMEM((2, tm, D_MODEL), F32), pltpu.SemaphoreType.DMA((2,))],
        name="in_bwd", compiler_params=_params("arbitrary"),
    )(dp, w_in, h0, g1, dh1)


def _col_tile(cols, target):
    best = None
    for t in range(128, min(cols, target) + 1, 128):
        if cols % t == 0:
            best = t
    assert best is not None, cols
    return best


MXU_DIM = 256


def _mxu_tile(cols, target):
    best = None
    for t in range(MXU_DIM, min(cols, target) + 1, MXU_DIM):
        if cols % t == 0:
            best = t
    assert best is not None, cols
    return best


def _weight_grad(a, b, name):
    T, M = a.shape
    N = b.shape[1]
    tm = _col_tile(M, 1408)
    tn = _mxu_tile(N, 768 if tm <= 1024 else 512)

    def body(a_ref, b_ref, o_ref):
        o_ref[...] = _dot_tn(a_ref[...], b_ref[...])

    return pl.pallas_call(
        body, grid=(M // tm, N // tn),
        in_specs=[pl.BlockSpec((T, tm), lambda m, n: (0, m)), pl.BlockSpec((T, tn), lambda m, n: (0, n))],
        out_specs=pl.BlockSpec((tm, tn), lambda m, n: (m, n)),
        out_shape=jax.ShapeDtypeStruct((M, N), F32),
        name=name, compiler_params=_params("parallel", "parallel"),
    )(a, b)


def _local_step(x, meta, target, w_in, w_out, w_gu, w_down, small, on_ffn_grads=None, on_mixer_grads=None):
    wg = _gate_weights(small["w_rgate"], small["w_igate"])
    bg = jnp.concatenate([small["b_rgate"], small["b_igate"]], axis=1)

    p, u, h0 = _in_proj(x, meta, small["mix_norm_g"], w_in)
    y_rg, hs = _rg_fwd(p, small["conv_w"], small["conv_b"], wg, bg, small["lru_lambda"], small["rg_norm_g"])
    y_hg, o_all, st_all = _hg_fwd(p, small["hg_lower_bound"], small["hg_norm_g"])
    h1, v, yb = _out_proj(h0, y_rg, y_hg, w_out, small["ffn_norm_g"])
    gu, act = _gate_up(v, w_gu)
    dh2, dh2b, loss, g_final = _down_loss(h1, act, w_down, small["final_norm_g"], target)

    dgu = _ffn_bwd_act(dh2b, gu, w_down)
    ffn_grads = {"w_gate_up": _weight_grad(v, dgu, "grad_w_gate_up"),
                 "w_down": _weight_grad(act, dh2b, "grad_w_down")}
    stages = on_ffn_grads(ffn_grads) if on_ffn_grads is not None else None
    dh1, dh1b, dy, g_ffn = _ffn_bwd_in(dgu, w_gu, h1, small["ffn_norm_g"], dh2, w_out)
    early = late = None
    if stages is not None:
        chip_sums, send = stages
        sums = chip_sums()
        (dh1, dh1b, dy), sums = lax.optimization_barrier(((dh1, dh1b, dy), sums))
        early = send(sums)
    dp, g_lb, g_hgn = _hg_bwd(p, o_all, st_all, dy, small["hg_lower_bound"], small["hg_norm_g"])
    dp, g_cw, g_cb, g_wgate, g_bg, g_lam, g_rgn = _rg_bwd(
        p, hs, dy, dp, small["conv_w"], small["conv_b"], wg, bg, small["lru_lambda"], small["rg_norm_g"])
    mixer_grads = {"w_in": _weight_grad(u, dp, "grad_w_in"), "w_out": _weight_grad(yb, dh1b, "grad_w_out")}
    if on_mixer_grads is not None:
        chip_sums, send = on_mixer_grads(mixer_grads)
        sums = chip_sums()
        (dp, dh1), sums = lax.optimization_barrier(((dp, dh1), sums))
        late = send(sums)
    grad_x, g_meta, g_mix = _in_bwd(dp, w_in, h0, small["mix_norm_g"], dh1)

    grads = {
        "w_in": mixer_grads["w_in"], "w_out": mixer_grads["w_out"],
        "w_gate_up": ffn_grads["w_gate_up"], "w_down": ffn_grads["w_down"],
        "meta_tokens": g_meta, "mix_norm_g": g_mix, "conv_w": g_cw, "conv_b": g_cb, "w_gates": g_wgate,
        "b_rgate": g_bg[:, :D_RG], "b_igate": g_bg[:, D_RG:], "lru_lambda": g_lam, "rg_norm_g": g_rgn,
        "hg_lower_bound": g_lb, "hg_norm_g": g_hgn, "ffn_norm_g": g_ffn, "final_norm_g": g_final,
    }
    return loss, grad_x, grads, early, late


ANY = pl.BlockSpec(memory_space=pl.ANY)
HALF = D_MODEL // 2

BIG = {"w_in": (D_MODEL, D_IN // N_CHIPS, True), "w_gate_up": (D_MODEL, 2 * D_FF // N_CHIPS, True),
       "w_out": (D_MODEL // N_CHIPS, D_MODEL, False), "w_down": (D_FF // N_CHIPS, D_MODEL, False)}
BIG_NAMES = tuple(BIG)
N_BIG = len(BIG_NAMES)


def _full_shape(name):
    rows, cols, by_col = BIG[name]
    return (rows, cols * N_CHIPS) if by_col else (rows * N_CHIPS, cols)


def _place():
    return lax.axis_index("x"), lax.axis_index("y"), lax.axis_index("c")


def _chip_of(x, y, r):
    fx, fy = (r + 1) >> 1, (r + 1) & 1
    return (1 - x if fx else x), (1 - y if fy else y)


def _half_of(ref, by_col, half):
    start = pl.multiple_of(half * HALF, 128)
    return ref.at[pl.ds(start, HALF), :] if by_col else ref.at[:, pl.ds(start, HALF)]


def _shard_of(ref, name, chip):
    rows, cols, by_col = BIG[name]
    if by_col:
        return ref.at[:, pl.ds(pl.multiple_of(chip * cols, 128), cols)]
    return ref.at[pl.ds(pl.multiple_of(chip * rows, 16), rows), :]


def _shard_half_of(ref, name, chip, half):
    rows, cols, by_col = BIG[name]
    start = pl.multiple_of(half * HALF, 128)
    if by_col:
        return ref.at[pl.ds(start, HALF), pl.ds(pl.multiple_of(chip * cols, 128), cols)]
    return ref.at[pl.ds(pl.multiple_of(chip * rows, 16), rows), pl.ds(start, HALF)]


def _remote(src, dst, send_sems, recv_sems, k, dev):
    return pltpu.make_async_remote_copy(src_ref=src, dst_ref=dst, send_sem=send_sems.at[k], recv_sem=recv_sems.at[k],
                                        device_id=dev, device_id_type=MESH)


def _cast_into_full(w, chip):
    steps = 4
    in_specs, out_specs = [], []
    for name in BIG_NAMES:
        rows, cols, by_col = BIG[name]
        tr = rows // steps
        in_specs.append(pl.BlockSpec((tr, cols), lambda i, s: (i, 0)))
        if by_col:
            out_specs.append(pl.BlockSpec((tr, cols), lambda i, s: (i, s[0])))
        else:
            out_specs.append(pl.BlockSpec((tr, cols), lambda i, s: (s[0] * steps + i, 0)))

    def body(s_ref, *refs):
        for a in range(N_BIG):
            refs[N_BIG + a][...] = refs[a][...].astype(BF16)

    placed = pl.pallas_call(
        body,
        grid_spec=pltpu.PrefetchScalarGridSpec(num_scalar_prefetch=1, grid=(steps,), in_specs=in_specs,
                                               out_specs=out_specs),
        out_shape=[jax.ShapeDtypeStruct(_full_shape(name), BF16) for name in BIG_NAMES],
        name="place_shards", compiler_params=_params("parallel"),
    )(chip, *[w[name] for name in BIG_NAMES])
    return dict(zip(BIG_NAMES, placed))


def _gather_weights(placed, small, names, label, collective_id):
    n, ns = len(names), len(small)
    hbm = pltpu.MemorySpace.HBM
    outs = [jax.new_ref(placed[nm], memory_space=hbm) for nm in names]
    small_in = [jax.new_ref(s, memory_space=hbm) for s in small]
    small_out = [jax.empty_ref(jax.ShapeDtypeStruct((s.shape[0], s.shape[1] * N_CHIPS), F32), memory_space=hbm)
                 for s in small]
    n_sems = 6 * n + 3 * ns

    @pl.kernel(mesh=plsc.ScalarSubcoreMesh(axis_name="seq", num_cores=1), name=label, out_type=(),
               scratch_types=(pltpu.SemaphoreType.DMA((n_sems,)), pltpu.SemaphoreType.DMA((n_sems,)),
                              pltpu.SemaphoreType.DMA((max(ns, 1),))),
               compiler_params=pltpu.CompilerParams(collective_id=collective_id))
    def launch(send_sems, recv_sems, local_sems):
        x, y, c = _place()
        chip = 2 * x + y
        sibling = (x, y, 1 - c)
        others = [_chip_of(x, y, r) for r in range(3)]
        _handshake([(qx, qy, c) for qx, qy in others] + [sibling])

        def small_block(a, q):
            cols = small[a].shape[1]
            return small_out[a].at[:, pl.ds(pl.multiple_of(q * cols, 128), cols)]

        local = [pltpu.make_async_copy(small_in[a], small_block(a, chip), local_sems.at[a]) for a in range(ns)]
        for cp in local:
            cp.start()

        sends = []
        for a, name in enumerate(names):
            mine = _shard_half_of(outs[a], name, chip, c)
            for r, (qx, qy) in enumerate(others):
                sends.append(_remote(mine, mine, send_sems, recv_sems, 6 * a + r, (qx, qy, c)))
        for a in range(ns):
            for r, (qx, qy) in enumerate(others):
                sends.append(_remote(small_in[a], small_block(a, chip), send_sems, recv_sems,
                                     6 * n + 3 * a + r, (qx, qy, c)))
        for cp in sends:
            cp.start()

        forwards = []
        for a, name in enumerate(names):
            for r, (qx, qy) in enumerate(others):
                landed = _shard_half_of(outs[a], name, 2 * qx + qy, c)
                _remote(landed, landed, send_sems, recv_sems, 6 * a + r, (qx, qy, c)).wait_recv()
                fwd = _remote(landed, landed, send_sems, recv_sems, 6 * a + 3 + r, sibling)
                fwd.start()
                forwards.append(fwd)
        for a in range(ns):
            for r, (qx, qy) in enumerate(others):
                landed = small_block(a, 2 * qx + qy)
                _remote(landed, landed, send_sems, recv_sems, 6 * n + 3 * a + r, (qx, qy, c)).wait_recv()
        for a, name in enumerate(names):
            for r, (qx, qy) in enumerate(others):
                landed = _shard_half_of(outs[a], name, 2 * qx + qy, 1 - c)
                _remote(landed, landed, send_sems, recv_sems, 6 * a + 3 + r, sibling).wait_recv()
        for cp in sends + forwards:
            cp.wait_send()
        for cp in local:
            cp.wait()

    launch()
    return {nm: ref[...] for nm, ref in zip(names, outs)}, [ref[...] for ref in small_out]


def _exchange_halves(grads, names, label, collective_id):
    n = len(names)
    sequencer = collective_id is not None

    def body(*refs):
        ins, outs = refs[:n], refs[n:2 * n]
        send_sems, recv_sems = refs[2 * n:]
        x, y, c = _place()
        if sequencer:
            _handshake([(x, y, 1 - c)])
        copies = []
        for a, name in enumerate(names):
            copies.append(_remote(_half_of(ins[a], BIG[name][2], 1 - c), outs[a], send_sems, recv_sems, a,
                                  (x, y, 1 - c)))
        for cp in copies:
            cp.start()
        for cp in copies:
            cp.wait()

    def half_shape(name):
        r, c_ = _full_shape(name)
        return (HALF, c_) if BIG[name][2] else (r, HALF)

    out_type = tuple(jax.ShapeDtypeStruct(half_shape(nm), F32) for nm in names)
    sems = (pltpu.SemaphoreType.DMA((n,)), pltpu.SemaphoreType.DMA((n,)))
    operands = [grads[nm] for nm in names]
    if sequencer:
        got = pl.kernel(
            body, mesh=plsc.ScalarSubcoreMesh(axis_name="seq", num_cores=1), name=label, out_type=out_type,
            scratch_types=sems, compiler_params=pltpu.CompilerParams(collective_id=collective_id),
        )(*operands)
    else:
        got = pl.pallas_call(
            body, in_specs=[ANY] * n, out_specs=[ANY] * n, out_shape=list(out_type), scratch_shapes=list(sems),
            name=label,
        )(*operands)
    return dict(zip(names, got))


def _chip_sum(grads, got, names, core, label):
    n = len(names)
    steps = 4
    g_specs, blks = [], []
    for name in names:
        rows, cols = got[name].shape
        tr = rows // steps
        if BIG[name][2]:
            g_specs.append(pl.BlockSpec((tr, cols), lambda i, s: (s[0] * steps + i, 0)))
        else:
            g_specs.append(pl.BlockSpec((tr, HALF), lambda i, s: (i, s[0])))
        blks.append(pl.BlockSpec((tr, cols), lambda i, s: (i, 0)))

    def body(s_ref, *refs):
        for a in range(n):
            t = refs[a][...] + refs[n + a][...]
            refs[2 * n + a][...] = t
            refs[3 * n + a][...] = t.astype(BF16)

    out = pl.pallas_call(
        body,
        grid_spec=pltpu.PrefetchScalarGridSpec(num_scalar_prefetch=1, grid=(steps,), in_specs=g_specs + blks,
                                               out_specs=blks + blks),
        out_shape=([jax.ShapeDtypeStruct(got[nm].shape, F32) for nm in names]
                   + [jax.ShapeDtypeStruct(got[nm].shape, BF16) for nm in names]),
        name=label, compiler_params=_params("parallel"),
    )(core, *[grads[nm] for nm in names], *[got[nm] for nm in names])
    return {nm: (out[a], out[n + a]) for a, nm in enumerate(names)}


def _piece_shape(name):
    rows, cols, by_col = BIG[name]
    return (HALF, cols) if by_col else (rows, HALF)


def _handshake(peers):
    barrier = pltpu.get_barrier_semaphore()
    for peer in peers:
        pl.semaphore_signal(barrier, inc=1, device_id=peer, device_id_type=MESH)
    pl.semaphore_wait(barrier, len(peers))


def _send_chip_sums(sums, names, label, collective_id):
    n = len(names)

    def body(*refs):
        ins, outs = refs[:n], refs[n:2 * n]
        send_sems, recv_sems = refs[2 * n:]
        x, y, c = _place()
        others = [_chip_of(x, y, r) for r in range(3)]
        _handshake([(qx, qy, c) for qx, qy in others])
        copies = []
        for a, name in enumerate(names):
            for r, (qx, qy) in enumerate(others):
                copies.append(_remote(_shard_of(ins[a], name, 2 * qx + qy), outs[a].at[r], send_sems, recv_sems,
                                      3 * a + r, (qx, qy, c)))
        for cp in copies:
            cp.start()
        for cp in copies:
            cp.wait()

    return pl.kernel(
        body, mesh=plsc.ScalarSubcoreMesh(axis_name="seq", num_cores=1), name=label,
        out_type=tuple(jax.ShapeDtypeStruct((3,) + _piece_shape(nm), BF16) for nm in names),
        scratch_types=(pltpu.SemaphoreType.DMA((3 * n,)), pltpu.SemaphoreType.DMA((3 * n,))),
        compiler_params=pltpu.CompilerParams(collective_id=collective_id),
    )(*[sums[nm] for nm in names])


def _total(parts, chip_core):
    steps = 2
    in_specs, out_specs, operands = [], [], []
    for name in BIG_NAMES:
        by_col = BIG[name][2]
        pr, pc = _piece_shape(name)
        tr = pr // steps
        if by_col:
            in_specs.append(pl.BlockSpec((tr, pc), lambda i, s: (i, s[0])))
            out_specs.append(pl.BlockSpec((tr, pc), lambda i, s: (s[1] * steps + i, 0)))
        else:
            in_specs.append(pl.BlockSpec((tr, pc), lambda i, s: (s[0] * steps + i, 0)))
            out_specs.append(pl.BlockSpec((tr, pc), lambda i, s: (i, s[1])))
        for r in range(3):
            in_specs.append(pl.BlockSpec((None, tr, pc), lambda i, s, r=r: (r, i, 0)))
        own, got = parts[name]
        operands += [own, got, got, got]

    def body(s_ref, *refs):
        for a in range(N_BIG):
            o_ref, a_ref, b_ref, c_ref = refs[4 * a:4 * a + 4]
            refs[4 * N_BIG + a][...] = (((o_ref[...] + a_ref[...].astype(F32)) + b_ref[...].astype(F32))
                                        + c_ref[...].astype(F32))

    totals = pl.pallas_call(
        body,
        grid_spec=pltpu.PrefetchScalarGridSpec(num_scalar_prefetch=1, grid=(steps,), in_specs=in_specs,
                                               out_specs=out_specs),
        out_shape=[jax.ShapeDtypeStruct(BIG[name][:2], F32) for name in BIG_NAMES],
        name="totals", compiler_params=_params("parallel"),
    )(chip_core, *operands)
    return dict(zip(BIG_NAMES, totals))


def _share_totals(totals):
    def body(*refs):
        outs = refs[N_BIG:2 * N_BIG]
        send_sems, recv_sems = refs[2 * N_BIG:]
        x, y, c = _place()
        copies = []
        for a, name in enumerate(BIG_NAMES):
            mine = _half_of(outs[a], BIG[name][2], c)
            copies.append(_remote(mine, mine, send_sems, recv_sems, a, (x, y, 1 - c)))
        for cp in copies:
            cp.start()
        for a, name in enumerate(BIG_NAMES):
            theirs = _half_of(outs[a], BIG[name][2], 1 - c)
            _remote(theirs, theirs, send_sems, recv_sems, a, (x, y, 1 - c)).wait_recv()
        for cp in copies:
            cp.wait_send()

    return pl.pallas_call(
        body, in_specs=[ANY] * N_BIG, out_specs=[ANY] * N_BIG,
        out_shape=[jax.ShapeDtypeStruct(BIG[n][:2], F32) for n in BIG_NAMES],
        input_output_aliases={a: a for a in range(N_BIG)},
        scratch_shapes=[pltpu.SemaphoreType.DMA((N_BIG,)), pltpu.SemaphoreType.DMA((N_BIG,))],
        name="share_totals",
    )(*[totals[n] for n in BIG_NAMES])


VEC_ROWS = 32
VEC_ROW = {"mix_norm_g": 0, "conv_b": 1, "b_rgate": 2, "b_igate": 3, "lru_lambda": 4, "rg_norm_g": 5,
           "hg_lower_bound": 6, "hg_norm_g": 8, "ffn_norm_g": 9, "final_norm_g": 10, "loss": 11,
           "conv_w": 12, "meta_tokens": 16}
N_DEV = 8


def _all_reduce_small(pieces, gates):
    names = list(pieces)
    hv, hg = VEC_ROWS // 2, gates.shape[0] // 2

    def body(*refs):
        ins = refs[:len(names)]
        (g_ref, vec_ref, gsum_ref, mine_v, sib_v, sib_g, chip_v, chip_g, got_v, got_g,
         send_sems, recv_sems) = refs[len(names):]
        x, y, c = _place()
        chip = 2 * x + y
        sibling = (x, y, 1 - c)
        mine_v[...] = jnp.zeros_like(mine_v)
        for name, ref in zip(names, ins):
            nr, w = ref.shape
            mine_v[VEC_ROW[name]:VEC_ROW[name] + nr, 0:w] = ref[...]

        swap = [_remote(mine_v, sib_v, send_sems, recv_sems, 0, sibling),
                _remote(g_ref, sib_g, send_sems, recv_sems, 1, sibling)]
        for cp in swap:
            cp.start()
        for cp in swap:
            cp.wait()
        chip_v[...] = mine_v[...] + sib_v[...]
        chip_g[...] = g_ref[...] + sib_g[...]

        rows_v = pl.ds(pl.multiple_of(c * hv, 8), hv)
        rows_g = pl.ds(pl.multiple_of(c * hg, 8), hg)
        got_v[chip] = chip_v[rows_v, :]
        got_g[chip] = chip_g[rows_g, :]
        sends = []
        for r in range(3):
            qx, qy = _chip_of(x, y, r)
            sends.append(_remote(chip_v.at[rows_v, :], got_v.at[chip], send_sems, recv_sems, 2 + r, (qx, qy, c)))
            sends.append(_remote(chip_g.at[rows_g, :], got_g.at[chip], send_sems, recv_sems, 5 + r, (qx, qy, c)))
        for cp in sends:
            cp.start()
        for cp in sends:
            cp.wait()
        vec_ref[rows_v, :] = ((got_v[0] + got_v[1]) + got_v[2]) + got_v[3]
        gsum_ref[rows_g, :] = ((got_g[0] + got_g[1]) + got_g[2]) + got_g[3]

        back = [_remote(vec_ref.at[rows_v, :], vec_ref.at[rows_v, :], send_sems, recv_sems, 8, sibling),
                _remote(gsum_ref.at[rows_g, :], gsum_ref.at[rows_g, :], send_sems, recv_sems, 9, sibling)]
        for cp in back:
            cp.start()
        theirs_v = vec_ref.at[pl.ds(pl.multiple_of((1 - c) * hv, 8), hv), :]
        theirs_g = gsum_ref.at[pl.ds(pl.multiple_of((1 - c) * hg, 8), hg), :]
        _remote(theirs_v, theirs_v, send_sems, recv_sems, 8, sibling).wait_recv()
        _remote(theirs_g, theirs_g, send_sems, recv_sems, 9, sibling).wait_recv()
        for cp in back:
            cp.wait_send()

    vmem = pl.BlockSpec(memory_space=pltpu.VMEM)
    n_sems = 10
    return pl.pallas_call(
        body, in_specs=[vmem] * (len(names) + 1), out_specs=[vmem, vmem],
        out_shape=[jax.ShapeDtypeStruct((VEC_ROWS, D_MODEL), F32), jax.ShapeDtypeStruct(gates.shape, F32)],
        scratch_shapes=[pltpu.VMEM((VEC_ROWS, D_MODEL), F32), pltpu.VMEM((VEC_ROWS, D_MODEL), F32),
                        pltpu.VMEM(gates.shape, F32), pltpu.VMEM((VEC_ROWS, D_MODEL), F32),
                        pltpu.VMEM(gates.shape, F32), pltpu.VMEM((N_CHIPS, hv, D_MODEL), F32),
                        pltpu.VMEM((N_CHIPS, hg) + gates.shape[1:], F32),
                        pltpu.SemaphoreType.DMA((n_sems,)), pltpu.SemaphoreType.DMA((n_sems,))],
        name="all_reduce_small",
    )(*[pieces[n] for n in names], gates)


def _adamw_math(w, g, m, v):
    m = ADAM_B1 * m + (1.0 - ADAM_B1) * g
    v = ADAM_B2 * v + (1.0 - ADAM_B2) * (g * g)
    m_hat = m / (1.0 - ADAM_B1 ** ADAM_STEP)
    v_hat = v / (1.0 - ADAM_B2 ** ADAM_STEP)
    delta = -ADAM_LR * (m_hat / (jnp.sqrt(v_hat) + ADAM_EPS) + ADAM_WD * w)
    return delta, m, v


def _adamw_big(w, g, m, v):
    steps = 8
    blks = []
    for name in BIG_NAMES:
        rows, cols, _ = BIG[name]
        blks.append(pl.BlockSpec((rows // steps, cols), lambda i: (i, 0)))

    def body(*refs):
        ins, outs = refs[:4 * N_BIG], refs[4 * N_BIG:]
        for a in range(N_BIG):
            w_ref, g_ref, m_ref, v_ref = (ins[k * N_BIG + a] for k in range(4))
            d, nm, nv = _adamw_math(w_ref[...], g_ref[...], m_ref[...], v_ref[...])
            outs[a][...] = d
            outs[N_BIG + a][...] = nm
            outs[2 * N_BIG + a][...] = nv

    shapes = [jax.ShapeDtypeStruct(BIG[name][:2], F32) for name in BIG_NAMES]
    out = pl.pallas_call(
        body, grid=(steps,), in_specs=blks * 4, out_specs=blks * 3, out_shape=shapes * 3,
        name="adamw_big", compiler_params=_params("parallel"),
    )(*[t[name] for t in (w, g, m, v) for name in BIG_NAMES])
    return {name: (out[a], out[N_BIG + a], out[2 * N_BIG + a]) for a, name in enumerate(BIG_NAMES)}


SMALL = {"meta_tokens": (N_META, D_MODEL // N_CHIPS), "mix_norm_g": (1, D_MODEL), "conv_w": (CONV_W, D_RG // N_CHIPS),
         "conv_b": (1, D_RG), "w_rgate": (D_RG, RG_HEAD_DIM), "b_rgate": (1, D_RG), "w_igate": (D_RG, RG_HEAD_DIM),
         "b_igate": (1, D_RG), "lru_lambda": (1, D_RG), "rg_norm_g": (1, D_RG), "hg_lower_bound": (2, D_HG),
         "hg_norm_g": (1, HG_HEAD_DIM), "ffn_norm_g": (1, D_MODEL), "final_norm_g": (1, D_MODEL)}
SMALL_NAMES = tuple(SMALL)
SHARDED_SMALL = ("meta_tokens", "conv_w")


def _adamw_small(vec, gates, w, m, v):
    n = len(SMALL_NAMES)

    def body(*refs):
        vec_ref, gates_ref = refs[:2]
        w_refs, m_refs, v_refs = refs[2:2 + n], refs[2 + n:2 + 2 * n], refs[2 + 2 * n:2 + 3 * n]
        outs = refs[2 + 3 * n:]
        loss_ref = outs[0]
        x, y, _ = _place()
        chip = 2 * x + y
        loss_ref[...] = vec_ref[VEC_ROW["loss"]:VEC_ROW["loss"] + 1, 0:1]

        def update(k, g):
            g_ref, d_ref, nm_ref, nv_ref = outs[1 + 4 * k:5 + 4 * k]
            g_ref[...] = g
            d_ref[...], nm_ref[...], nv_ref[...] = _adamw_math(w_refs[k][...], g, m_refs[k][...], v_refs[k][...])

        for k, name in enumerate(SMALL_NAMES):
            nr, w_ = SMALL[name]
            if name == "w_rgate":
                update(k, gates_ref[0:D_RG, :])
            elif name == "w_igate":
                update(k, gates_ref[D_RG:2 * D_RG, :])
            elif name in SHARDED_SMALL:
                r0 = VEC_ROW[name]
                for q in range(N_CHIPS):
                    @pl.when(chip == q)
                    def _(k=k, r0=r0, nr=nr, w_=w_, q=q):
                        update(k, vec_ref[r0:r0 + nr, q * w_:(q + 1) * w_])
            else:
                r0 = VEC_ROW[name]
                update(k, vec_ref[r0:r0 + nr, 0:w_])

    vmem = pl.BlockSpec(memory_space=pltpu.VMEM)
    out_shape = [jax.ShapeDtypeStruct((1, 1), F32)]
    for name in SMALL_NAMES:
        out_shape += [jax.ShapeDtypeStruct(SMALL[name], F32)] * 4
    outs = pl.pallas_call(
        body, in_specs=[vmem] * (2 + 3 * n), out_specs=[vmem] * len(out_shape), out_shape=out_shape,
        name="adamw_small",
    )(vec, gates, *[w[k] for k in SMALL_NAMES], *[m[k] for k in SMALL_NAMES], *[v[k] for k in SMALL_NAMES])
    loss = outs[0]
    res = {name: tuple(outs[1 + 4 * k:5 + 4 * k]) for k, name in enumerate(SMALL_NAMES)}
    return loss, res


WEIGHT_NAMES = ("meta_tokens", "mix_norm_g", "w_in", "conv_w", "conv_b", "w_rgate", "b_rgate", "w_igate", "b_igate",
                "lru_lambda", "rg_norm_g", "hg_lower_bound", "hg_norm_g", "w_out", "ffn_norm_g", "w_gate_up", "w_down",
                "final_norm_g")


def _to_2d(name, a):
    if name in BIG:
        return a.reshape(BIG[name][:2])
    return a.reshape(SMALL[name])


def kernel(x, meta_tokens, mix_norm_g, w_in, conv_w, conv_b, w_rgate, b_rgate, w_igate, b_igate, lru_lambda, rg_norm_g, hg_lower_bound, hg_norm_g, w_out, ffn_norm_g, w_gate_up, w_down, final_norm_g, loss_target, m_meta_tokens, m_mix_norm_g, m_w_in, m_conv_w, m_conv_b, m_w_rgate, m_b_rgate, m_w_igate, m_b_igate, m_lru_lambda, m_rg_norm_g, m_hg_lower_bound, m_hg_norm_g, m_w_out, m_ffn_norm_g, m_w_gate_up, m_w_down, m_final_norm_g, v_meta_tokens, v_mix_norm_g, v_w_in, v_conv_w, v_conv_b, v_w_rgate, v_b_rgate, v_w_igate, v_b_igate, v_lru_lambda, v_rg_norm_g, v_hg_lower_bound, v_hg_norm_g, v_w_out, v_ffn_norm_g, v_w_gate_up, v_w_down, v_final_norm_g):
    w_raw = dict(zip(WEIGHT_NAMES, (meta_tokens, mix_norm_g, w_in, conv_w, conv_b, w_rgate, b_rgate, w_igate, b_igate,
                                    lru_lambda, rg_norm_g, hg_lower_bound, hg_norm_g, w_out, ffn_norm_g, w_gate_up,
                                    w_down, final_norm_g)))
    m_raw = dict(zip(WEIGHT_NAMES, (m_meta_tokens, m_mix_norm_g, m_w_in, m_conv_w, m_conv_b, m_w_rgate, m_b_rgate,
                                    m_w_igate, m_b_igate, m_lru_lambda, m_rg_norm_g, m_hg_lower_bound, m_hg_norm_g,
                                    m_w_out, m_ffn_norm_g, m_w_gate_up, m_w_down, m_final_norm_g)))
    v_raw = dict(zip(WEIGHT_NAMES, (v_meta_tokens, v_mix_norm_g, v_w_in, v_conv_w, v_conv_b, v_w_rgate, v_b_rgate,
                                    v_w_igate, v_b_igate, v_lru_lambda, v_rg_norm_g, v_hg_lower_bound, v_hg_norm_g,
                                    v_w_out, v_ffn_norm_g, v_w_gate_up, v_w_down, v_final_norm_g)))
    w = {k: _to_2d(k, a) for k, a in w_raw.items()}
    m = {k: _to_2d(k, a) for k, a in m_raw.items()}
    v = {k: _to_2d(k, a) for k, a in v_raw.items()}

    x_i, y_i, c_i = _place()
    core = jnp.reshape(c_i, (1,)).astype(jnp.int32)
    chip = jnp.reshape(2 * x_i + y_i, (1,)).astype(jnp.int32)
    chip_core = jnp.concatenate([chip, core])

    placed = _cast_into_full(w, chip)
    first, (meta_full, cw_full) = _gather_weights(placed, [w["meta_tokens"], w["conv_w"]], ("w_in",), "gather_first", 1)
    rest, _ = _gather_weights(placed, [], ("w_out", "w_gate_up", "w_down"), "gather_rest", 2)
    full = {**first, **rest}

    seq = x.shape[1]
    small ={k: w[k] for k in SMALL_NAMES if k not in SHARDED_SMALL}
    small["conv_w"] = cw_full

    def reduce_to_chips(grads, names, tag, collective_ids):
        got = _exchange_halves(grads, names, "exchange_halves_" + tag, collective_ids[0])

        def chip_sums():
            return _chip_sum(grads, got, names, core, "chip_sum_" + tag)

        def send(sums):
            arrived = _send_chip_sums({n: sums[n][1] for n in names}, names, "send_chip_sums_" + tag,
                                      collective_ids[1])
            return {n: (sums[n][0], a) for n, a in zip(names, arrived)}

        return chip_sums, send

    ffn_names, mixer_names = ("w_gate_up", "w_down"), ("w_in", "w_out")
    loss, grad_x, grads, parts, parts_mixer = _local_step(
        x.reshape(seq, D_MODEL), meta_full, loss_target.reshape(seq, D_MODEL),
        full["w_in"], full["w_out"], full["w_gate_up"], full["w_down"], small,
        on_ffn_grads=lambda g: reduce_to_chips(g, ffn_names, "ffn", (3, 4)),
        on_mixer_grads=lambda g: reduce_to_chips(g, mixer_names, "mixer", (None, 5)))
    parts.update(parts_mixer)
    totals = _total(parts, chip_core)
    g_big = dict(zip(BIG_NAMES, _share_totals(totals)))

    pieces = {k: grads[k] for k in VEC_ROW if k != "loss"}
    pieces["loss"] = loss
    vec, gates = _all_reduce_small(pieces, grads["w_gates"])
    loss_sum, res = _adamw_small(vec, gates, w, m, v)
    updates = _adamw_big(w, g_big, m, v)
    for n in BIG_NAMES:
        res[n] = (g_big[n],) + updates[n]

    out = [loss_sum.reshape(()), grad_x.reshape(1, seq, D_MODEL)]
    for j in range(4):
        out += [res[n][j].reshape(w_raw[n].shape) for n in WEIGHT_NAMES]
    return tuple(out)
```

```python
import functools
import math

import jax
import jax.numpy as jnp
from jax import lax
from jax.experimental import pallas as pl
from jax.experimental.pallas import tpu as pltpu
from jax.experimental.pallas import tpu_sc as plsc

F32 = jnp.float32
BF16 = jnp.bfloat16
HIGHEST = lax.Precision.HIGHEST
MESH = pl.DeviceIdType.MESH

D_MODEL = 1024
D_RG = 512
RG_HEAD_DIM = 64
D_HG = 512
HG_HEAD_DIM = 128
HG_HEADS = 4
CHUNK = 64
SUB = 16
N_SUB = CHUNK // SUB
N_META = 16
PAD = CHUNK - N_META
D_IN = 3072
D_FF = 2816
CONV_W = 4
LRU_C = 8.0
EPS = 1e-6
EXP_CLAMP = 80.0
GELU_C = math.sqrt(2.0 / math.pi)
GELU_A = 0.044715
N_CHIPS = 4

ADAM_LR = 0.001
ADAM_B1 = 0.9
ADAM_B2 = 0.999
ADAM_EPS = 1e-08
ADAM_WD = 0.01
ADAM_STEP = 10

VMEM_LIMIT = 56 * 1024 * 1024


def _params(*sem):
    return pltpu.CompilerParams(dimension_semantics=sem, vmem_limit_bytes=VMEM_LIMIT)


def _row_tile(rows, target):
    best = None
    for t in range(16, min(rows, target) + 1, 16):
        if rows % t == 0:
            best = t
    assert best is not None, rows
    return best


def _sigmoid(x):
    return 0.5 * jnp.tanh(0.5 * x) + 0.5


def _dot(a, b):
    return jnp.dot(a, b, preferred_element_type=F32)


def _dot_nt(a, b):
    return lax.dot_general(a, b, (((1,), (1,)), ((), ())), preferred_element_type=F32)


def _dot_tn(a, b):
    return lax.dot_general(a, b, (((0,), (0,)), ((), ())), preferred_element_type=F32)


def _rms(x):
    return lax.rsqrt(jnp.mean(x * x, axis=-1, keepdims=True) + EPS)


def _rms_bwd(dn, n, r):
    return r * (dn - n * jnp.mean(dn * n, axis=-1, keepdims=True))


def _gelu_parts(x):
    t = jnp.tanh(GELU_C * (x + GELU_A * x * x * x))
    g = 0.5 * x * (1.0 + t)
    dg = 0.5 * (1.0 + t) + 0.5 * x * (1.0 - t * t) * GELU_C * (1.0 + 3.0 * GELU_A * x * x)
    return g, dg


def _softplus_neg(lam):
    e = jnp.exp(-jnp.abs(lam))
    w = 1.0 + e
    log1p = jnp.where(w == 1.0, e, jnp.log(w) * e / (w - 1.0))
    return jnp.maximum(-lam, 0.0) + log1p


def _head_mask():
    r = lax.broadcasted_iota(jnp.int32, (D_RG, D_RG), 0) // RG_HEAD_DIM
    c = lax.broadcasted_iota(jnp.int32, (D_RG, D_RG), 1) // RG_HEAD_DIM
    return r == c


def _head_fold():
    r = lax.broadcasted_iota(jnp.int32, (D_RG, RG_HEAD_DIM), 0) % RG_HEAD_DIM
    c = lax.broadcasted_iota(jnp.int32, (D_RG, RG_HEAD_DIM), 1)
    return (r == c).astype(F32)


def _gate_weights(w_r, w_i):
    def body(wr_ref, wi_ref, o_ref):
        fold = _head_fold()
        mask = _head_mask()
        for k, ref in enumerate((wr_ref, wi_ref)):
            full = lax.dot_general(ref[...], fold, (((1,), (1,)), ((), ())),
                                   precision=HIGHEST, preferred_element_type=F32)
            o_ref[:, k * D_RG:(k + 1) * D_RG] = jnp.where(mask, full, 0.0).astype(BF16)

    return pl.pallas_call(
        body, out_shape=jax.ShapeDtypeStruct((D_RG, 2 * D_RG), BF16), name="gate_weights",
    )(w_r, w_i)


HEAD = PAD + N_META


def _window_copies(seq_hbm, buf, sems, tm):
    def first(to_vmem):
        seq, vm = seq_hbm.at[pl.ds(0, tm - HEAD)], buf.at[0, pl.ds(HEAD, tm - HEAD)]
        return pltpu.make_async_copy(seq, vm, sems.at[0]) if to_vmem else pltpu.make_async_copy(vm, seq, sems.at[0])

    def later(j, slot, to_vmem):
        seq, vm = seq_hbm.at[pl.ds(pl.multiple_of(j * tm - HEAD, 8), tm)], buf.at[slot]
        if to_vmem:
            return pltpu.make_async_copy(seq, vm, sems.at[slot])
        return pltpu.make_async_copy(vm, seq, sems.at[slot])

    return first, later


def _fetch_window(seq_hbm, buf, sems, i, n_steps, tm):
    first, later = _window_copies(seq_hbm, buf, sems, tm)
    slot = i % 2

    @pl.when(i == 0)
    def _():
        first(True).start()

    if n_steps > 1:
        @pl.when(i + 1 < n_steps)
        def _():
            later(i + 1, 1 - slot, True).start()

    @pl.when(i == 0)
    def _():
        first(True).wait()

    if n_steps > 1:
        @pl.when(i > 0)
        def _():
            later(i, slot, True).wait()

    return slot


def _in_proj(x, meta, g1, w_in):
    T = x.shape[0] + HEAD
    tm = _row_tile(T, 416)
    n_steps = T // tm

    def body(x_hbm, meta_ref, g_ref, w_ref, p_ref, u_ref, h_ref, buf, sems):
        i = pl.program_id(0)
        slot = _fetch_window(x_hbm, buf, sems, i, n_steps, tm)

        @pl.when(i == 0)
        def _():
            buf[0, 0:PAD, :] = jnp.zeros((PAD, D_MODEL), F32)
            buf[0, PAD:HEAD, :] = meta_ref[...]

        h = buf[slot]
        h_ref[...] = h
        u = (h * _rms(h) * g_ref[...]).astype(BF16)
        u_ref[...] = u
        p_ref[...] = _dot(u, w_ref[...])

    return pl.pallas_call(
        body, grid=(n_steps,),
        in_specs=[pl.BlockSpec(memory_space=pl.ANY),
                  pl.BlockSpec((N_META, D_MODEL), lambda i: (0, 0)),
                  pl.BlockSpec((1, D_MODEL), lambda i: (0, 0)),
                  pl.BlockSpec((D_MODEL, D_IN), lambda i: (0, 0))],
        out_specs=[pl.BlockSpec((tm, D_IN), lambda i: (i, 0)),
                   pl.BlockSpec((tm, D_MODEL), lambda i: (i, 0)),
                   pl.BlockSpec((tm, D_MODEL), lambda i: (i, 0))],
        out_shape=[jax.ShapeDtypeStruct((T, D_IN), F32), jax.ShapeDtypeStruct((T, D_MODEL), BF16),
                   jax.ShapeDtypeStruct((T, D_MODEL), F32)],
        scratch_shapes=[pltpu.VMEM((2, tm, D_MODEL), F32), pltpu.SemaphoreType.DMA((2,))],
        name="in_proj", compiler_params=_params("arbitrary"),
    )(x, meta, g1, w_in)


def _scan_block_fwd(A, B, rowi):
    for d in (1, 2, 4):
        a_sh = pltpu.roll(A, d, axis=0)
        b_sh = pltpu.roll(B, d, axis=0)
        m = rowi >= d
        B = jnp.where(m, A * b_sh + B, B)
        A = jnp.where(m, A * a_sh, A)
    return A, B


def _scan_block_bwd(A, B, rowi):
    for d in (1, 2, 4):
        a_sh = pltpu.roll(A, 8 - d, axis=0)
        b_sh = pltpu.roll(B, 8 - d, axis=0)
        m = rowi < 8 - d
        B = jnp.where(m, A * b_sh + B, B)
        A = jnp.where(m, A * a_sh, A)
    return A, B


def _rg_gates(xc, w_ref, bg_ref, lam):
    pre = _dot(xc.astype(BF16), w_ref[...]) + bg_ref[...]
    r = _sigmoid(pre[:, :D_RG])
    ig = _sigmoid(pre[:, D_RG:])
    sp = _softplus_neg(lam)
    la = -LRU_C * sp * r
    a = jnp.exp(la)
    th = jnp.tanh(la)
    u = 1.0 - th
    rc = pl.reciprocal(u, approx=True)
    rc = rc * (2.0 - u * rc)
    rc = rc * (2.0 - u * rc)
    m2 = -2.0 * th * rc
    inv_m = lax.rsqrt(jnp.maximum(m2, 1e-30))
    return r, ig, sp, a, m2 * inv_m, inv_m


def _conv(ext, cw_ref, cb_ref, tm):
    xc = cb_ref[...] + cw_ref[0:1, :] * ext[8 - 3:8 - 3 + tm, :]
    for j in range(1, CONV_W):
        xc = xc + cw_ref[j:j + 1, :] * ext[8 - 3 + j:8 - 3 + j + tm, :]
    return xc


def _scan_unroll(blocks):
    return 4 if blocks % 4 == 0 else 2 if blocks % 2 == 0 else 1


def _rg_fwd(p, cw, cb, wg, bg, lam, rg_g):
    T = p.shape[0]
    tm = _row_tile(T, 832)
    unroll = _scan_unroll(tm // 8)

    def body(xg_ref, cw_ref, cb_ref, w_ref, bg_ref, lam_ref, g_ref, y_ref, h_ref, ext, a_s, b_s, carry):
        i = pl.program_id(0)

        @pl.when(i == 0)
        def _():
            ext[0:8, :] = jnp.zeros((8, D_RG), F32)
            carry[...] = jnp.zeros((1, D_RG), F32)

        ext[8:8 + tm, :] = xg_ref[:, :D_RG]
        xc = _conv(ext, cw_ref, cb_ref, tm)
        r, ig, sp, a, m, _ = _rg_gates(xc, w_ref, bg_ref, lam_ref[...])
        row = i * tm + lax.broadcasted_iota(jnp.int32, (tm, 1), 0)
        a_s[...] = a
        b_s[...] = jnp.where(row >= PAD, m * ig * xc, 0.0)
        rowi = lax.broadcasted_iota(jnp.int32, (8, D_RG), 0)

        def blk(j, c):
            for u in range(unroll):
                o = pl.multiple_of((j * unroll + u) * 8, 8)
                A, B = _scan_block_fwd(a_s[pl.ds(o, 8), :], b_s[pl.ds(o, 8), :], rowi)
                h = B + A * c
                h_ref[pl.ds(o, 8), :] = h
                c = h[7:8, :]
            return c

        carry[...] = lax.fori_loop(0, tm // (8 * unroll), blk, carry[...])
        ext[0:8, :] = ext[tm:tm + 8, :]
        g, _ = _gelu_parts(xg_ref[:, D_RG:])
        yy = g * h_ref[...]
        y_ref[...] = (yy * _rms(yy) * g_ref[...]).astype(BF16)

    vec = lambda n: pl.BlockSpec((1, n), lambda i: (0, 0))
    return pl.pallas_call(
        body, grid=(T // tm,),
        in_specs=[pl.BlockSpec((tm, 2 * D_RG), lambda i: (i, 0)),
                  pl.BlockSpec((CONV_W, D_RG), lambda i: (0, 0)), vec(D_RG),
                  pl.BlockSpec((D_RG, 2 * D_RG), lambda i: (0, 0)), vec(2 * D_RG), vec(D_RG), vec(D_RG)],
        out_specs=[pl.BlockSpec((tm, D_RG), lambda i: (i, 0)), pl.BlockSpec((tm, D_RG), lambda i: (i, 0))],
        out_shape=[jax.ShapeDtypeStruct((T, D_RG), BF16), jax.ShapeDtypeStruct((T, D_RG), F32)],
        scratch_shapes=[pltpu.VMEM((tm + 8, D_RG), F32), pltpu.VMEM((tm, D_RG), F32),
                        pltpu.VMEM((tm, D_RG), F32), pltpu.VMEM((1, D_RG), F32)],
        name="rg_fwd", compiler_params=_params("arbitrary"),
    )(p, cw, cb, wg, bg, lam, rg_g)


def _tri(lower):
    r = lax.broadcasted_iota(jnp.int32, (CHUNK, CHUNK), 0)
    c = lax.broadcasted_iota(jnp.int32, (CHUNK, CHUNK), 1)
    return ((c <= r) if lower else (c >= r)).astype(F32)


def _hg_gates(hq, hf, lbraw_ref, valid):
    lb = _sigmoid(lbraw_ref[0:1, :] - lbraw_ref[1:2, :])
    sq = _sigmoid(hq)
    q = hq * sq
    sf = _sigmoid(hf)
    f = lb + (1.0 - lb) * sf
    lf = jnp.where(valid, jnp.log(f), 0.0)
    b = jnp.dot(_tri(True), lf, precision=HIGHEST, preferred_element_type=F32)
    return lb, sq, q, sf, f, b


def _hg_head(qh, kh, bh):
    blk = lax.broadcasted_iota(jnp.int32, (CHUNK, 1), 0) // SUB
    b_last = bh[CHUNK - 1:CHUNK, :]
    refs = [bh[SUB * s:SUB * s + 1, :] for s in range(N_SUB)]
    r_sel = refs[N_SUB - 1]
    for s in range(N_SUB - 2, -1, -1):
        r_sel = jnp.where(blk == s, refs[s], r_sel)
    eb = jnp.exp(bh)
    eq = jnp.exp(bh - r_sel)
    ekh = jnp.exp(b_last - bh)
    ek = [jnp.exp(jnp.minimum(refs[s] - bh, EXP_CLAMP)) for s in range(N_SUB)]
    qe = qh * eq
    q_hat = jnp.concatenate([jnp.where(blk == s, qe, 0.0) for s in range(N_SUB)], axis=1)
    k_til = jnp.concatenate([kh * ek[s] for s in range(N_SUB)], axis=1)
    return blk, b_last, eb, eq, ekh, ek, q_hat, k_til


def _causal():
    r = lax.broadcasted_iota(jnp.int32, (CHUNK, CHUNK), 0)
    c = lax.broadcasted_iota(jnp.int32, (CHUNK, CHUNK), 1)
    return r >= c


def _chunks_per_step(n_chunks):
    for c in (5, 4, 3, 2):
        if n_chunks % c == 0:
            return c
    return 1


def _hg_fwd(p, lbraw, hg_g):
    T = p.shape[0]
    n_chunks = T // CHUNK
    cps = _chunks_per_step(n_chunks)
    rows = cps * CHUNK

    def body(hq_ref, hf_ref, hi_ref, hg_ref, lb_ref, g_ref, y_ref, o_ref, st_all_ref, st):
        i = pl.program_id(0)

        @pl.when(i == 0)
        def _():
            st[...] = jnp.zeros_like(st)

        def chunk(j, carry):
            rs = pl.ds(pl.multiple_of(j * CHUNK, CHUNK), CHUNK)
            chunk_body(i * cps + j, hq_ref.at[rs, :], hf_ref.at[rs, :], hi_ref.at[rs, :], hg_ref.at[rs, :], lb_ref,
                       g_ref, y_ref.at[rs, :], o_ref.at[rs, :], st_all_ref.at[pl.ds(j, 1)], st)
            return carry

        lax.fori_loop(0, cps, chunk, 0, unroll=True)

    def chunk_body(n, hq_ref, hf_ref, hi_ref, hg_ref, lb_ref, g_ref, y_ref, o_ref, st_all_ref, st):
        valid = (n * CHUNK + lax.broadcasted_iota(jnp.int32, (CHUNK, 1), 0)) >= PAD
        hq, hf, v, hg = hq_ref[...], hf_ref[...], hi_ref[...], hg_ref[...]
        lb, sq, q, sf, f, b = _hg_gates(hq, hf, lb_ref, valid)
        k = 1.0 - f
        st_all_ref[0] = st[...]
        causal = _causal()
        v_t = v.T.astype(BF16)
        heads = [slice(h * HG_HEAD_DIM, (h + 1) * HG_HEAD_DIM) for h in range(HG_HEADS)]
        fac = []
        for sl in heads:
            qh, kh, bh = q[:, sl], k[:, sl], b[:, sl]
            _, b_last, eb, _, ekh, _, q_hat, k_til = _hg_head(qh, kh, bh)
            fac.append((jnp.exp(b_last), (qh * eb).astype(BF16), q_hat.astype(BF16), k_til.astype(BF16),
                        (kh * ekh).astype(BF16), v[:, sl].astype(BF16)))
        raw = []
        for sl, (_, q_til, q_hat, k_til, k_hat, _) in zip(heads, fac):
            st_h = st[sl, :]
            raw.append((_dot_nt(q_til, st_h.astype(BF16)), _dot_nt(q_hat, k_til), _dot(v_t[sl, :], k_hat), st_h))
        for sl, (e_last, _, _, _, _, vb), (inter, att, upd, st_h) in zip(heads, fac, raw):
            o = inter + _dot(jnp.where(causal, att, 0.0).astype(BF16), vb)
            st[sl, :] = st_h * e_last + upd
            o_ref[:, sl] = o
            hgh = hg[:, sl]
            y_ref[:, sl] = (o * _rms(o) * g_ref[...] * (hgh * _sigmoid(hgh))).astype(BF16)

    col = lambda j: pl.BlockSpec((rows, D_HG), lambda n: (n, j))
    return pl.pallas_call(
        body, grid=(n_chunks // cps,),
        in_specs=[col(2), col(3), col(4), col(5),
                  pl.BlockSpec((2, D_HG), lambda n: (0, 0)), pl.BlockSpec((1, HG_HEAD_DIM), lambda n: (0, 0))],
        out_specs=[pl.BlockSpec((rows, D_HG), lambda n: (n, 0)), pl.BlockSpec((rows, D_HG), lambda n: (n, 0)),
                   pl.BlockSpec((cps, D_HG, HG_HEAD_DIM), lambda n: (n, 0, 0))],
        out_shape=[jax.ShapeDtypeStruct((T, D_HG), BF16), jax.ShapeDtypeStruct((T, D_HG), F32),
                   jax.ShapeDtypeStruct((n_chunks, D_HG, HG_HEAD_DIM), F32)],
        scratch_shapes=[pltpu.VMEM((D_HG, HG_HEAD_DIM), F32)],
        name="hg_fwd", compiler_params=_params("arbitrary"),
    )(p, p, p, p, lbraw, hg_g)


def _out_proj(h0, y_rg, y_hg, w_out, g2):
    T = h0.shape[0]
    tm = _row_tile(T, 832)

    def body(h_ref, yr_ref, yh_ref, w_ref, g_ref, h1_ref, v_ref, y_ref):
        y_ref[:, :D_RG] = yr_ref[...]
        y_ref[:, D_RG:] = yh_ref[...]
        h1 = h_ref[...] + _dot(y_ref[...], w_ref[...])
        h1_ref[...] = h1
        v_ref[...] = (h1 * _rms(h1) * g_ref[...]).astype(BF16)

    row = lambda n: pl.BlockSpec((tm, n), lambda i: (i, 0))
    return pl.pallas_call(
        body, grid=(T // tm,),
        in_specs=[row(D_MODEL), row(D_RG), row(D_HG), pl.BlockSpec((D_MODEL, D_MODEL), lambda i: (0, 0)),
                  pl.BlockSpec((1, D_MODEL), lambda i: (0, 0))],
        out_specs=[row(D_MODEL), row(D_MODEL), row(D_MODEL)],
        out_shape=[jax.ShapeDtypeStruct((T, D_MODEL), F32), jax.ShapeDtypeStruct((T, D_MODEL), BF16),
                   jax.ShapeDtypeStruct((T, D_MODEL), BF16)],
        name="out_proj", compiler_params=_params("parallel"),
    )(h0, y_rg, y_hg, w_out, g2)


def _gate_up(v, w_gu):
    T = v.shape[0]
    tm = _row_tile(T, 416)

    def body(v_ref, w_ref, gu_ref, act_ref):
        gu = _dot(v_ref[...], w_ref[...])
        gu_ref[...] = gu.astype(BF16)
        g = gu[:, :D_FF]
        act_ref[...] = (g * _sigmoid(g) * gu[:, D_FF:]).astype(BF16)

    row = lambda n: pl.BlockSpec((tm, n), lambda i: (i, 0))
    return pl.pallas_call(
        body, grid=(T // tm,),
        in_specs=[row(D_MODEL), pl.BlockSpec((D_MODEL, 2 * D_FF), lambda i: (0, 0))],
        out_specs=[row(2 * D_FF), row(D_FF)],
        out_shape=[jax.ShapeDtypeStruct((T, 2 * D_FF), BF16), jax.ShapeDtypeStruct((T, D_FF), BF16)],
        name="gate_up", compiler_params=_params("parallel"),
    )(v, w_gu)


def _down_loss(h1, act, w_down, gf, target):
    T = h1.shape[0]
    tm = _row_tile(T, 832)
    n_steps = T // tm

    def body(h_ref, a_ref, w_ref, g_ref, t_hbm, dh2_ref, dh2b_ref, loss_ref, gg_ref, tbuf, sems):
        i = pl.program_id(0)
        slot = _fetch_window(t_hbm, tbuf, sems, i, n_steps, tm)

        @pl.when(i == 0)
        def _():
            loss_ref[...] = jnp.zeros_like(loss_ref)
            gg_ref[...] = jnp.zeros_like(gg_ref)
            tbuf[0, 0:HEAD, :] = jnp.zeros((HEAD, D_MODEL), F32)

        h2 = h_ref[...] + _dot(a_ref[...], w_ref[...])
        r = _rms(h2)
        n = h2 * r
        gf_ = g_ref[...]
        row = i * tm + lax.broadcasted_iota(jnp.int32, (tm, 1), 0)
        err = jnp.where(row >= HEAD, n * gf_ - tbuf[slot], 0.0)
        loss_ref[...] += 0.5 * jnp.sum(jnp.mean(err * err, axis=-1, keepdims=True), axis=0, keepdims=True)
        dy = err * (1.0 / D_MODEL)
        gg_ref[...] += jnp.sum(dy * n, axis=0, keepdims=True)
        dh2 = _rms_bwd(dy * gf_, n, r)
        dh2_ref[...] = dh2
        dh2b_ref[...] = dh2.astype(BF16)

    row_spec = lambda n: pl.BlockSpec((tm, n), lambda i: (i, 0))
    return pl.pallas_call(
        body, grid=(T // tm,),
        in_specs=[row_spec(D_MODEL), row_spec(D_FF), pl.BlockSpec((D_FF, D_MODEL), lambda i: (0, 0)),
                  pl.BlockSpec((1, D_MODEL), lambda i: (0, 0)), pl.BlockSpec(memory_space=pl.ANY)],
        out_specs=[row_spec(D_MODEL), row_spec(D_MODEL), pl.BlockSpec((1, 1), lambda i: (0, 0)),
                   pl.BlockSpec((1, D_MODEL), lambda i: (0, 0))],
        out_shape=[jax.ShapeDtypeStruct((T, D_MODEL), F32), jax.ShapeDtypeStruct((T, D_MODEL), BF16),
                   jax.ShapeDtypeStruct((1, 1), F32), jax.ShapeDtypeStruct((1, D_MODEL), F32)],
        scratch_shapes=[pltpu.VMEM((2, tm, D_MODEL), F32), pltpu.SemaphoreType.DMA((2,))],
        name="down_loss", compiler_params=_params("arbitrary"),
    )(h1, act, w_down, gf, target)


def _ffn_bwd_act(dh2b, gu, w_down):
    T = dh2b.shape[0]
    tm = _row_tile(T, 416)

    def body(d_ref, gu_ref, w_ref, dgu_ref):
        dact = _dot_nt(d_ref[...], w_ref[...])
        g = gu_ref[:, :D_FF].astype(F32)
        u = gu_ref[:, D_FF:].astype(F32)
        s = _sigmoid(g)
        dgu_ref[:, :D_FF] = (dact * u * s * (1.0 + g * (1.0 - s))).astype(BF16)
        dgu_ref[:, D_FF:] = (dact * g * s).astype(BF16)

    row = lambda n: pl.BlockSpec((tm, n), lambda i: (i, 0))
    return pl.pallas_call(
        body, grid=(T // tm,),
        in_specs=[row(D_MODEL), row(2 * D_FF), pl.BlockSpec((D_FF, D_MODEL), lambda i: (0, 0))],
        out_specs=row(2 * D_FF),
        out_shape=jax.ShapeDtypeStruct((T, 2 * D_FF), BF16),
        name="ffn_bwd_act", compiler_params=_params("parallel"),
    )(dh2b, gu, w_down)


def _ffn_bwd_in(dgu, w_gu, h1, g2, dh2, w_out):
    T = h1.shape[0]
    tm = _row_tile(T, 416)

    def body(dgu_ref, wgu_ref, h_ref, g_ref, d2_ref, wo_ref, dh1_ref, dh1b_ref, dy_ref, gg_ref):
        i = pl.program_id(0)

        @pl.when(i == 0)
        def _():
            gg_ref[...] = jnp.zeros_like(gg_ref)

        dv = _dot_nt(dgu_ref[...], wgu_ref[...])
        h1 = h_ref[...]
        r = _rms(h1)
        n = h1 * r
        gg_ref[...] += jnp.sum(dv * n, axis=0, keepdims=True)
        dh1 = d2_ref[...] + _rms_bwd(dv * g_ref[...], n, r)
        dh1_ref[...] = dh1
        db = dh1.astype(BF16)
        dh1b_ref[...] = db
        dy_ref[...] = _dot_nt(db, wo_ref[...])

    row = lambda n: pl.BlockSpec((tm, n), lambda i: (i, 0))
    return pl.pallas_call(
        body, grid=(T // tm,),
        in_specs=[row(2 * D_FF), pl.BlockSpec((D_MODEL, 2 * D_FF), lambda i: (0, 0)),
                  row(D_MODEL), pl.BlockSpec((1, D_MODEL), lambda i: (0, 0)), row(D_MODEL),
                  pl.BlockSpec((D_MODEL, D_MODEL), lambda i: (0, 0))],
        out_specs=[row(D_MODEL), row(D_MODEL), row(D_MODEL), pl.BlockSpec((1, D_MODEL), lambda i: (0, 0))],
        out_shape=[jax.ShapeDtypeStruct((T, D_MODEL), F32), jax.ShapeDtypeStruct((T, D_MODEL), BF16),
                   jax.ShapeDtypeStruct((T, D_MODEL), F32), jax.ShapeDtypeStruct((1, D_MODEL), F32)],
        name="ffn_bwd_in", compiler_params=_params("arbitrary"),
    )(dgu, w_gu, h1, g2, dh2, w_out)


def _rg_bwd(p, hs, dy, dp, cw, cb, wg, bg, lam, rg_g):
    T = p.shape[0]
    tm = _row_tile(T, 832)
    nt = T // tm
    hb = tm // 8
    unroll = _scan_unroll(hb)

    def body(xg_ref, xh_ref, h_ref, hh_ref, dy_ref, dp_in_ref, cw_ref, cb_ref, w_ref, bg_ref, lam_ref, g_ref,
             dp_ref, gcw_ref, gcb_ref, gw_ref, gbg_ref, glam_ref, gg_ref,
             ext, dext, a_s, b_s, d_s, gacc, carry_d, carry_a):
        i = pl.program_id(0)
        t_idx = nt - 1 - i

        @pl.when(i == 0)
        def _():
            dext[tm:tm + 8, :] = jnp.zeros((8, D_RG), F32)
            carry_d[...] = jnp.zeros_like(carry_d)
            carry_a[...] = jnp.zeros_like(carry_a)
            gacc[...] = jnp.zeros_like(gacc)
            for ref in (gcw_ref, gcb_ref, gbg_ref, glam_ref, gg_ref, gw_ref):
                ref[...] = jnp.zeros_like(ref)

        first = t_idx == 0
        ext[0:8, :] = jnp.where(first, 0.0, xh_ref[:, :D_RG])
        ext[8:8 + tm, :] = xg_ref[:, :D_RG]
        xc = _conv(ext, cw_ref, cb_ref, tm)
        lam_ = lam_ref[...]
        r, ig, sp, a, m, inv_m = _rg_gates(xc, w_ref, bg_ref, lam_)
        row = t_idx * tm + lax.broadcasted_iota(jnp.int32, (tm, 1), 0)
        valid = row >= PAD

        gr = xg_ref[:, D_RG:]
        g, dgelu = _gelu_parts(gr)
        h = h_ref[...]
        yy = g * h
        rr = _rms(yy)
        nn = yy * rr
        dy_ = dy_ref[...]
        gg_ref[...] += jnp.sum(dy_ * nn, axis=0, keepdims=True)
        dyy = _rms_bwd(dy_ * g_ref[...], nn, rr)
        dp_ref[:, D_RG:] = (dyy * h * dgelu).astype(BF16)

        a_s[...] = a
        b_s[...] = dyy * g
        rowi = lax.broadcasted_iota(jnp.int32, (8, D_RG), 0)

        def blk(jj, c):
            cd, ca = c
            for u in range(unroll):
                o = pl.multiple_of((hb - 1 - (jj * unroll + u)) * 8, 8)
                a_blk = a_s[pl.ds(o, 8), :]
                a_next = jnp.where(rowi == 7, ca, pltpu.roll(a_blk, 7, axis=0))
                A, B = _scan_block_bwd(a_next, b_s[pl.ds(o, 8), :], rowi)
                d = B + A * cd
                d_s[pl.ds(o, 8), :] = d
                cd, ca = d[0:1, :], a_blk[0:1, :]
            return cd, ca

        cd, ca = lax.fori_loop(0, hb // unroll, blk, (carry_d[...], carry_a[...]))
        carry_d[...] = cd
        carry_a[...] = ca
        delta = d_s[...]

        h_last_prev = jnp.where(first, 0.0, hh_ref[7:8, :])
        row0 = lax.broadcasted_iota(jnp.int32, (tm, 1), 0) == 0
        h_prev = jnp.where(row0, h_last_prev, pltpu.roll(h, 1, axis=0))
        dbx = jnp.where(valid, delta, 0.0)
        da = delta * h_prev
        di = dbx * m * xc
        dm = dbx * ig * xc
        dla = a * (da - dm * a * inv_m)
        dla = jnp.where(valid, dla, 0.0)
        glam_ref[...] += jnp.sum(dla * r, axis=0, keepdims=True) * (LRU_C / (1.0 + jnp.exp(lam_)))
        dr = (-LRU_C) * sp * dla
        dpre = jnp.concatenate([dr * r * (1.0 - r), di * ig * (1.0 - ig)], axis=1)
        gbg_ref[...] += jnp.sum(dpre, axis=0, keepdims=True)
        dpre_b = dpre.astype(BF16)
        gacc[...] += _dot_tn(xc.astype(BF16), dpre_b)
        dxc = dbx * m * ig + _dot_nt(dpre_b, w_ref[...])
        gcb_ref[...] += jnp.sum(dxc, axis=0, keepdims=True)
        for j in range(CONV_W):
            gcw_ref[j:j + 1, :] += jnp.sum(dxc * ext[8 - 3 + j:8 - 3 + j + tm, :], axis=0, keepdims=True)
        dext[0:tm, :] = dxc
        dxr = cw_ref[0:1, :] * dext[3:3 + tm, :]
        for j in range(1, CONV_W):
            dxr = dxr + cw_ref[j:j + 1, :] * dext[3 - j:3 - j + tm, :]
        dp_ref[:, :D_RG] = dxr.astype(BF16)
        dext[tm:tm + 8, :] = dext[0:8, :]

        @pl.when(i == nt - 1)
        def _():
            fold = _head_fold()
            mask = _head_mask()
            for k in range(2):
                blockdiag = jnp.where(mask, gacc[:, k * D_RG:(k + 1) * D_RG], 0.0)
                gw_ref[k * D_RG:(k + 1) * D_RG, :] = jnp.dot(blockdiag, fold, precision=HIGHEST,
                                                             preferred_element_type=F32)

    vec = lambda n: pl.BlockSpec((1, n), lambda i: (0, 0))
    rev = lambda n: pl.BlockSpec((tm, n), lambda i: (nt - 1 - i, 0))
    halo = lambda n: pl.BlockSpec((8, n), lambda i: (jnp.maximum((nt - 1 - i) * hb - 1, 0), 0))
    return pl.pallas_call(
        body, grid=(nt,),
        in_specs=[rev(2 * D_RG), halo(2 * D_RG), rev(D_RG), halo(D_RG), rev(D_RG), ANY,
                  pl.BlockSpec((CONV_W, D_RG), lambda i: (0, 0)), vec(D_RG),
                  pl.BlockSpec((D_RG, 2 * D_RG), lambda i: (0, 0)), vec(2 * D_RG), vec(D_RG), vec(D_RG)],
        out_specs=[rev(2 * D_RG), pl.BlockSpec((CONV_W, D_RG), lambda i: (0, 0)), vec(D_RG),
                   pl.BlockSpec((2 * D_RG, RG_HEAD_DIM), lambda i: (0, 0)), vec(2 * D_RG), vec(D_RG), vec(D_RG)],
        input_output_aliases={5: 0},
        out_shape=[jax.ShapeDtypeStruct((T, D_IN), BF16), jax.ShapeDtypeStruct((CONV_W, D_RG), F32),
                   jax.ShapeDtypeStruct((1, D_RG), F32), jax.ShapeDtypeStruct((2 * D_RG, RG_HEAD_DIM), F32),
                   jax.ShapeDtypeStruct((1, 2 * D_RG), F32), jax.ShapeDtypeStruct((1, D_RG), F32),
                   jax.ShapeDtypeStruct((1, D_RG), F32)],
        scratch_shapes=[pltpu.VMEM((tm + 8, D_RG), F32), pltpu.VMEM((tm + 8, D_RG), F32),
                        pltpu.VMEM((tm, D_RG), F32), pltpu.VMEM((tm, D_RG), F32), pltpu.VMEM((tm, D_RG), F32),
                        pltpu.VMEM((D_RG, 2 * D_RG), F32), pltpu.VMEM((1, D_RG), F32), pltpu.VMEM((1, D_RG), F32)],
        name="rg_bwd", compiler_params=_params("arbitrary"),
    )(p, p, hs, hs, dy, dp, cw, cb, wg, bg, lam, rg_g)


def _hg_bwd(p, o_all, st_all, dy, lbraw, hg_g):
    T = p.shape[0]
    n_chunks = T // CHUNK
    cps = _chunks_per_step(n_chunks)
    rows = cps * CHUNK
    n_steps = n_chunks // cps

    def body(hq_ref, hf_ref, hi_ref, hg_ref, o_ref, st_ref, dy_ref, lb_ref, g_ref,
             dp_ref, glb_ref, gg_ref, dst):
        i = pl.program_id(0)

        @pl.when(i == 0)
        def _():
            dst[...] = jnp.zeros_like(dst)
            glb_ref[...] = jnp.zeros_like(glb_ref)
            gg_ref[...] = jnp.zeros_like(gg_ref)

        dp_ref[:, :2 * D_RG] = jnp.zeros((rows, 2 * D_RG), BF16)

        def chunk(jj, carry):
            j = cps - 1 - jj
            rs = pl.ds(pl.multiple_of(j * CHUNK, CHUNK), CHUNK)
            chunk_body((n_steps - 1 - i) * cps + j, hq_ref.at[rs, :], hf_ref.at[rs, :], hi_ref.at[rs, :],
                       hg_ref.at[rs, :], o_ref.at[rs, :], st_ref.at[pl.ds(j, 1)], dy_ref.at[rs, :], lb_ref, g_ref,
                       dp_ref.at[rs, pl.ds(2 * D_RG, 4 * D_HG)], glb_ref, gg_ref, dst)
            return carry

        lax.fori_loop(0, cps, chunk, 0, unroll=True)

    def chunk_body(n, hq_ref, hf_ref, hi_ref, hg_ref, o_ref, st_ref, dy_ref, lb_ref, g_ref,
                   dp_ref, glb_ref, gg_ref, dst):
        valid = (n * CHUNK + lax.broadcasted_iota(jnp.int32, (CHUNK, 1), 0)) >= PAD
        hq, hf, v, hg = hq_ref[...], hf_ref[...], hi_ref[...], hg_ref[...]
        lb, sq, q, sf, f, b = _hg_gates(hq, hf, lb_ref, valid)
        k = 1.0 - f
        causal = _causal()
        r_i = lax.broadcasted_iota(jnp.int32, (CHUNK, CHUNK), 0)
        c_i = lax.broadcasted_iota(jnp.int32, (CHUNK, CHUNK), 1)
        causal_t = r_i <= c_i
        is_last = lax.broadcasted_iota(jnp.int32, (CHUNK, 1), 0) == CHUNK - 1
        g_ = g_ref[...]
        db_parts, dq_parts, dk_parts = [], [], []
        gg = jnp.zeros((1, HG_HEAD_DIM), F32)
        heads = [slice(h * HG_HEAD_DIM, (h + 1) * HG_HEAD_DIM) for h in range(HG_HEADS)]

        do_parts = []
        for h, sl in enumerate(heads):
            o = o_ref[:, sl]
            ro = _rms(o)
            no = o * ro
            hgh = hg[:, sl]
            sg = _sigmoid(hgh)
            dyh = dy_ref[:, sl]
            dp_ref[:, 3 * D_HG + h * HG_HEAD_DIM:3 * D_HG + (h + 1) * HG_HEAD_DIM] = (
                dyh * no * g_ * sg * (1.0 + hgh * (1.0 - sg))).astype(BF16)
            dng = dyh * hgh * sg
            gg = gg + jnp.sum(dng * no, axis=0, keepdims=True)
            do_parts.append(_rms_bwd(dng * g_, no, ro))
        do_t = jnp.concatenate(do_parts, axis=1).T.astype(BF16)

        fac = []
        for sl, do in zip(heads, do_parts):
            qh, kh, bh = q[:, sl], k[:, sl], b[:, sl]
            blk, b_last, eb, eq, ekh, ek, q_hat, k_til = _hg_head(qh, kh, bh)
            fac.append(dict(qh=qh, kh=kh, blk=blk, e_last=jnp.exp(b_last), eb=eb, eq=eq, ekh=ekh, ek=ek,
                            q_til=qh * eb, k_hat=kh * ekh, qhb=q_hat.astype(BF16), ktb=k_til.astype(BF16),
                            vb=v[:, sl].astype(BF16), dob=do.astype(BF16)))

        first = []
        for sl, t in zip(heads, fac):
            st_h = st_ref[0, sl, :]
            dst_h = dst[sl, :]
            dstb = dst_h.astype(BF16)
            first.append(dict(
                att_t=_dot_nt(t["ktb"], t["qhb"]), datt=_dot_nt(t["dob"], t["vb"]),
                datt_t=_dot_nt(t["vb"], t["dob"]), dk_hat=_dot(t["vb"], dstb),
                dv=_dot_nt(t["k_hat"].astype(BF16), dstb), dq_til=_dot(t["dob"], st_h.astype(BF16)),
                state=t["e_last"] * jnp.sum(dst_h * st_h, axis=0, keepdims=True)))
            dst[sl, :] = dst_h * t["e_last"] + _dot(do_t[sl, :], t["q_til"].astype(BF16))

        for h, (t, m) in enumerate(zip(fac, first)):
            qh, kh, blk, eb, eq, ekh, ek = t["qh"], t["kh"], t["blk"], t["eb"], t["eq"], t["ekh"], t["ek"]
            q_til, k_hat, qhb, ktb, dob = t["q_til"], t["k_hat"], t["qhb"], t["ktb"], t["dob"]
            dk_hat, dq_til = m["dk_hat"], m["dq_til"]
            dv = m["dv"] + _dot(jnp.where(causal_t, m["att_t"], 0.0).astype(BF16), dob)
            dq_hat = _dot(jnp.where(causal, m["datt"], 0.0).astype(BF16), ktb)
            dk_til = _dot(jnp.where(causal_t, m["datt_t"], 0.0).astype(BF16), qhb)
            db_last = jnp.sum(dk_hat * k_hat, axis=0, keepdims=True) + m["state"]
            dq_sel = dq_hat[:, (N_SUB - 1) * HG_HEAD_DIM:]
            for s in range(N_SUB - 2, -1, -1):
                dq_sel = jnp.where(blk == s, dq_hat[:, s * HG_HEAD_DIM:(s + 1) * HG_HEAD_DIM], dq_sel)
            dq_a = dq_sel * eq
            dk_a = dk_til[:, :HG_HEAD_DIM] * ek[0]
            for s in range(1, N_SUB):
                dk_a = dk_a + dk_til[:, s * HG_HEAD_DIM:(s + 1) * HG_HEAD_DIM] * ek[s]
            db_att = qhb.astype(F32) * dq_hat - ktb.astype(F32) * dk_til
            db = dq_til * q_til - dk_hat * k_hat
            for s in range(N_SUB):
                db = db + db_att[:, s * HG_HEAD_DIM:(s + 1) * HG_HEAD_DIM]
            db_parts.append(jnp.where(is_last, db + db_last, db))
            dq_parts.append(dq_til * eb + dq_a)
            dk_parts.append(dk_hat * ekh + dk_a)
            dp_ref[:, 2 * D_HG + h * HG_HEAD_DIM:2 * D_HG + (h + 1) * HG_HEAD_DIM] = dv.astype(BF16)

        gg_ref[...] += gg
        db = jnp.concatenate(db_parts, axis=1)
        dq = jnp.concatenate(dq_parts, axis=1)
        dk = jnp.concatenate(dk_parts, axis=1)
        dlf = jnp.where(valid, jnp.dot(_tri(False), db, precision=HIGHEST, preferred_element_type=F32), 0.0)
        dp_ref[:, :D_HG] = (dq * sq * (1.0 + hq * (1.0 - sq))).astype(BF16)
        df = dlf / f - dk
        dlb = jnp.sum(df * (1.0 - sf), axis=0, keepdims=True) * lb * (1.0 - lb)
        glb_ref[0:1, :] += dlb
        glb_ref[1:2, :] += -dlb
        dp_ref[:, D_HG:2 * D_HG] = (df * (1.0 - lb) * sf * (1.0 - sf)).astype(BF16)

    rev = lambda j: pl.BlockSpec((rows, D_HG), lambda i: (n_steps - 1 - i, j))
    return pl.pallas_call(
        body, grid=(n_steps,),
        in_specs=[rev(2), rev(3), rev(4), rev(5), rev(0),
                  pl.BlockSpec((cps, D_HG, HG_HEAD_DIM), lambda i: (n_steps - 1 - i, 0, 0)), rev(1),
                  pl.BlockSpec((2, D_HG), lambda i: (0, 0)), pl.BlockSpec((1, HG_HEAD_DIM), lambda i: (0, 0))],
        out_specs=[pl.BlockSpec((rows, D_IN), lambda i: (n_steps - 1 - i, 0)),
                   pl.BlockSpec((2, D_HG), lambda i: (0, 0)), pl.BlockSpec((1, HG_HEAD_DIM), lambda i: (0, 0))],
        out_shape=[jax.ShapeDtypeStruct((T, D_IN), BF16), jax.ShapeDtypeStruct((2, D_HG), F32),
                   jax.ShapeDtypeStruct((1, HG_HEAD_DIM), F32)],
        scratch_shapes=[pltpu.VMEM((D_HG, HG_HEAD_DIM), F32)],
        name="hg_bwd", compiler_params=_params("arbitrary"),
    )(p, p, p, p, o_all, st_all, dy, lbraw, hg_g)


def _in_bwd(dp, w_in, h0, g1, dh1):
    T = h0.shape[0]
    tm = _row_tile(T, 416)
    n_steps = T // tm

    def body(dp_ref, w_ref, h_ref, g_ref, d1_ref, gx_hbm, gmeta_ref, gg_ref, buf, sems):
        i = pl.program_id(0)
        first, later = _window_copies(gx_hbm, buf, sems, tm)
        slot = i % 2

        @pl.when(i == 0)
        def _():
            gg_ref[...] = jnp.zeros_like(gg_ref)

        if n_steps > 2:
            @pl.when(i == 2)
            def _():
                first(False).wait()

            @pl.when(i > 2)
            def _():
                later(i - 2, slot, False).wait()

        du = _dot_nt(dp_ref[...], w_ref[...])
        h0_ = h_ref[...]
        r = _rms(h0_)
        n = h0_ * r
        gg_ref[...] += jnp.sum(du * n, axis=0, keepdims=True)
        dh0 = d1_ref[...] + _rms_bwd(du * g_ref[...], n, r)
        buf[slot] = dh0

        @pl.when(i == 0)
        def _():
            gmeta_ref[...] = dh0[PAD:HEAD, :]
            first(False).start()

        if n_steps > 1:
            @pl.when(i > 0)
            def _():
                later(i, slot, False).start()

        @pl.when(i == n_steps - 1)
        def _():
            if n_steps == 1:
                first(False).wait()
            else:
                if n_steps == 2:
                    first(False).wait()
                else:
                    later(i - 1, 1 - slot, False).wait()
                later(i, slot, False).wait()

    row = lambda n: pl.BlockSpec((tm, n), lambda i: (i, 0))
    return pl.pallas_call(
        body, grid=(n_steps,),
        in_specs=[row(D_IN), pl.BlockSpec((D_MODEL, D_IN), lambda i: (0, 0)),
                  row(D_MODEL), pl.BlockSpec((1, D_MODEL), lambda i: (0, 0)), row(D_MODEL)],
        out_specs=[pl.BlockSpec(memory_space=pl.ANY), pl.BlockSpec((N_META, D_MODEL), lambda i: (0, 0)),
                   pl.BlockSpec((1, D_MODEL), lambda i: (0, 0))],
        out_shape=[jax.ShapeDtypeStruct((T - HEAD, D_MODEL), F32), jax.ShapeDtypeStruct((N_META, D_MODEL), F32),
                   jax.ShapeDtypeStruct((1, D_MODEL), F32)],
        scratch_shapes=[pltpu.VMEM((2, tm, D_MODEL), F32), pltpu.SemaphoreType.DMA((2,))],
        name="in_bwd", compiler_params=_params("arbitrary"),
    )(dp, w_in, h0, g1, dh1)


def _col_tile(cols, target):
    best = None
    for t in range(128, min(cols, target) + 1, 128):
        if cols % t == 0:
            best = t
    assert best is not None, cols
    return best


MXU_DIM = 256


def _mxu_tile(cols, target):
    best = None
    for t in range(MXU_DIM, min(cols, target) + 1, MXU_DIM):
        if cols % t == 0:
            best = t
    assert best is not None, cols
    return best


def _weight_grad(a, b, name):
    T, M = a.shape
    N = b.shape[1]
    tm = _col_tile(M, 1408)
    tn = _mxu_tile(N, 768 if tm <= 1024 else 512)

    def body(a_ref, b_ref, o_ref):
        o_ref[...] = _dot_tn(a_ref[...], b_ref[...])

    return pl.pallas_call(
        body, grid=(M // tm, N // tn),
        in_specs=[pl.BlockSpec((T, tm), lambda m, n: (0, m)), pl.BlockSpec((T, tn), lambda m, n: (0, n))],
        out_specs=pl.BlockSpec((tm, tn), lambda m, n: (m, n)),
        out_shape=jax.ShapeDtypeStruct((M, N), F32),
        name=name, compiler_params=_params("parallel", "parallel"),
    )(a, b)


def _local_step(x, meta, target, w_in, w_out, w_gu, w_down, small, on_ffn_grads=None, on_mixer_grads=None):
    wg = _gate_weights(small["w_rgate"], small["w_igate"])
    bg = jnp.concatenate([small["b_rgate"], small["b_igate"]], axis=1)

    p, u, h0 = _in_proj(x, meta, small["mix_norm_g"], w_in)
    y_rg, hs = _rg_fwd(p, small["conv_w"], small["conv_b"], wg, bg, small["lru_lambda"], small["rg_norm_g"])
    y_hg, o_all, st_all = _hg_fwd(p, small["hg_lower_bound"], small["hg_norm_g"])
    h1, v, yb = _out_proj(h0, y_rg, y_hg, w_out, small["ffn_norm_g"])
    gu, act = _gate_up(v, w_gu)
    dh2, dh2b, loss, g_final = _down_loss(h1, act, w_down, small["final_norm_g"], target)

    dgu = _ffn_bwd_act(dh2b, gu, w_down)
    ffn_grads = {"w_gate_up": _weight_grad(v, dgu, "grad_w_gate_up"),
                 "w_down": _weight_grad(act, dh2b, "grad_w_down")}
    stages = on_ffn_grads(ffn_grads) if on_ffn_grads is not None else None
    dh1, dh1b, dy, g_ffn = _ffn_bwd_in(dgu, w_gu, h1, small["ffn_norm_g"], dh2, w_out)
    early = late = None
    if stages is not None:
        chip_sums, send = stages
        sums = chip_sums()
        (dh1, dh1b, dy), sums = lax.optimization_barrier(((dh1, dh1b, dy), sums))
        early = send(sums)
    dp, g_lb, g_hgn = _hg_bwd(p, o_all, st_all, dy, small["hg_lower_bound"], small["hg_norm_g"])
    dp, g_cw, g_cb, g_wgate, g_bg, g_lam, g_rgn = _rg_bwd(
        p, hs, dy, dp, small["conv_w"], small["conv_b"], wg, bg, small["lru_lambda"], small["rg_norm_g"])
    mixer_grads = {"w_in": _weight_grad(u, dp, "grad_w_in"), "w_out": _weight_grad(yb, dh1b, "grad_w_out")}
    if on_mixer_grads is not None:
        chip_sums, send = on_mixer_grads(mixer_grads)
        sums = chip_sums()
        (dp, dh1), sums = lax.optimization_barrier(((dp, dh1), sums))
        late = send(sums)
    grad_x, g_meta, g_mix = _in_bwd(dp, w_in, h0, small["mix_norm_g"], dh1)

    grads = {
        "w_in": mixer_grads["w_in"], "w_out": mixer_grads["w_out"],
        "w_gate_up": ffn_grads["w_gate_up"], "w_down": ffn_grads["w_down"],
        "meta_tokens": g_meta, "mix_norm_g": g_mix, "conv_w": g_cw, "conv_b": g_cb, "w_gates": g_wgate,
        "b_rgate": g_bg[:, :D_RG], "b_igate": g_bg[:, D_RG:], "lru_lambda": g_lam, "rg_norm_g": g_rgn,
        "hg_lower_bound": g_lb, "hg_norm_g": g_hgn, "ffn_norm_g": g_ffn, "final_norm_g": g_final,
    }
    return loss, grad_x, grads, early, late


ANY = pl.BlockSpec(memory_space=pl.ANY)
HALF = D_MODEL // 2

BIG = {"w_in": (D_MODEL, D_IN // N_CHIPS, True), "w_gate_up": (D_MODEL, 2 * D_FF // N_CHIPS, True),
       "w_out": (D_MODEL // N_CHIPS, D_MODEL, False), "w_down": (D_FF // N_CHIPS, D_MODEL, False)}
BIG_NAMES = tuple(BIG)
N_BIG = len(BIG_NAMES)


def _full_shape(name):
    rows, cols, by_col = BIG[name]
    return (rows, cols * N_CHIPS) if by_col else (rows * N_CHIPS, cols)


def _place():
    return lax.axis_index("x"), lax.axis_index("y"), lax.axis_index("c")


def _chip_of(x, y, r):
    fx, fy = (r + 1) >> 1, (r + 1) & 1
    return (1 - x if fx else x), (1 - y if fy else y)


def _half_of(ref, by_col, half):
    start = pl.multiple_of(half * HALF, 128)
    return ref.at[pl.ds(start, HALF), :] if by_col else ref.at[:, pl.ds(start, HALF)]


def _shard_of(ref, name, chip):
    rows, cols, by_col = BIG[name]
    if by_col:
        return ref.at[:, pl.ds(pl.multiple_of(chip * cols, 128), cols)]
    return ref.at[pl.ds(pl.multiple_of(chip * rows, 16), rows), :]


def _shard_half_of(ref, name, chip, half):
    rows, cols, by_col = BIG[name]
    start = pl.multiple_of(half * HALF, 128)
    if by_col:
        return ref.at[pl.ds(start, HALF), pl.ds(pl.multiple_of(chip * cols, 128), cols)]
    return ref.at[pl.ds(pl.multiple_of(chip * rows, 16), rows), pl.ds(start, HALF)]


def _remote(src, dst, send_sems, recv_sems, k, dev):
    return pltpu.make_async_remote_copy(src_ref=src, dst_ref=dst, send_sem=send_sems.at[k], recv_sem=recv_sems.at[k],
                                        device_id=dev, device_id_type=MESH)


def _cast_into_full(w, chip):
    steps = 4
    in_specs, out_specs = [], []
    for name in BIG_NAMES:
        rows, cols, by_col = BIG[name]
        tr = rows // steps
        in_specs.append(pl.BlockSpec((tr, cols), lambda i, s: (i, 0)))
        if by_col:
            out_specs.append(pl.BlockSpec((tr, cols), lambda i, s: (i, s[0])))
        else:
            out_specs.append(pl.BlockSpec((tr, cols), lambda i, s: (s[0] * steps + i, 0)))

    def body(s_ref, *refs):
        for a in range(N_BIG):
            refs[N_BIG + a][...] = refs[a][...].astype(BF16)

    placed = pl.pallas_call(
        body,
        grid_spec=pltpu.PrefetchScalarGridSpec(num_scalar_prefetch=1, grid=(steps,), in_specs=in_specs,
                                               out_specs=out_specs),
        out_shape=[jax.ShapeDtypeStruct(_full_shape(name), BF16) for name in BIG_NAMES],
        name="place_shards", compiler_params=_params("parallel"),
    )(chip, *[w[name] for name in BIG_NAMES])
    return dict(zip(BIG_NAMES, placed))


def _gather_weights(placed, small, names, label, collective_id):
    n, ns = len(names), len(small)
    hbm = pltpu.MemorySpace.HBM
    outs = [jax.new_ref(placed[nm], memory_space=hbm) for nm in names]
    small_in = [jax.new_ref(s, memory_space=hbm) for s in small]
    small_out = [jax.empty_ref(jax.ShapeDtypeStruct((s.shape[0], s.shape[1] * N_CHIPS), F32), memory_space=hbm)
                 for s in small]
    n_sems = 6 * n + 3 * ns

    @pl.kernel(mesh=plsc.ScalarSubcoreMesh(axis_name="seq", num_cores=1), name=label, out_type=(),
               scratch_types=(pltpu.SemaphoreType.DMA((n_sems,)), pltpu.SemaphoreType.DMA((n_sems,)),
                              pltpu.SemaphoreType.DMA((max(ns, 1),))),
               compiler_params=pltpu.CompilerParams(collective_id=collective_id))
    def launch(send_sems, recv_sems, local_sems):
        x, y, c = _place()
        chip = 2 * x + y
        sibling = (x, y, 1 - c)
        others = [_chip_of(x, y, r) for r in range(3)]
        _handshake([(qx, qy, c) for qx, qy in others] + [sibling])

        def small_block(a, q):
            cols = small[a].shape[1]
            return small_out[a].at[:, pl.ds(pl.multiple_of(q * cols, 128), cols)]

        local = [pltpu.make_async_copy(small_in[a], small_block(a, chip), local_sems.at[a]) for a in range(ns)]
        for cp in local:
            cp.start()

        sends = []
        for a, name in enumerate(names):
            mine = _shard_half_of(outs[a], name, chip, c)
            for r, (qx, qy) in enumerate(others):
                sends.append(_remote(mine, mine, send_sems, recv_sems, 6 * a + r, (qx, qy, c)))
        for a in range(ns):
            for r, (qx, qy) in enumerate(others):
                sends.append(_remote(small_in[a], small_block(a, chip), send_sems, recv_sems,
                                     6 * n + 3 * a + r, (qx, qy, c)))
        for cp in sends:
            cp.start()

        forwards = []
        for a, name in enumerate(names):
            for r, (qx, qy) in enumerate(others):
                landed = _shard_half_of(outs[a], name, 2 * qx + qy, c)
                _remote(landed, landed, send_sems, recv_sems, 6 * a + r, (qx, qy, c)).wait_recv()
                fwd = _remote(landed, landed, send_sems, recv_sems, 6 * a + 3 + r, sibling)
                fwd.start()
                forwards.append(fwd)
        for a in range(ns):
            for r, (qx, qy) in enumerate(others):
                landed = small_block(a, 2 * qx + qy)
                _remote(landed, landed, send_sems, recv_sems, 6 * n + 3 * a + r, (qx, qy, c)).wait_recv()
        for a, name in enumerate(names):
            for r, (qx, qy) in enumerate(others):
                landed = _shard_half_of(outs[a], name, 2 * qx + qy, 1 - c)
                _remote(landed, landed, send_sems, recv_sems, 6 * a + 3 + r, sibling).wait_recv()
        for cp in sends + forwards:
            cp.wait_send()
        for cp in local:
            cp.wait()

    launch()
    return {nm: ref[...] for nm, ref in zip(names, outs)}, [ref[...] for ref in small_out]


def _exchange_halves(grads, names, label, collective_id):
    n = len(names)
    sequencer = collective_id is not None

    def body(*refs):
        ins, outs = refs[:n], refs[n:2 * n]
        send_sems, recv_sems = refs[2 * n:]
        x, y, c = _place()
        if sequencer:
            _handshake([(x, y, 1 - c)])
        copies = []
        for a, name in enumerate(names):
            copies.append(_remote(_half_of(ins[a], BIG[name][2], 1 - c), outs[a], send_sems, recv_sems, a,
                                  (x, y, 1 - c)))
        for cp in copies:
            cp.start()
        for cp in copies:
            cp.wait()

    def half_shape(name):
        r, c_ = _full_shape(name)
        return (HALF, c_) if BIG[name][2] else (r, HALF)

    out_type = tuple(jax.ShapeDtypeStruct(half_shape(nm), F32) for nm in names)
    sems = (pltpu.SemaphoreType.DMA((n,)), pltpu.SemaphoreType.DMA((n,)))
    operands = [grads[nm] for nm in names]
    if sequencer:
        got = pl.kernel(
            body, mesh=plsc.ScalarSubcoreMesh(axis_name="seq", num_cores=1), name=label, out_type=out_type,
            scratch_types=sems, compiler_params=pltpu.CompilerParams(collective_id=collective_id),
        )(*operands)
    else:
        got = pl.pallas_call(
            body, in_specs=[ANY] * n, out_specs=[ANY] * n, out_shape=list(out_type), scratch_shapes=list(sems),
            name=label,
        )(*operands)
    return dict(zip(names, got))


def _chip_sum(grads, got, names, core, label):
    n = len(names)
    steps = 4
    g_specs, blks = [], []
    for name in names:
        rows, cols = got[name].shape
        tr = rows // steps
        if BIG[name][2]:
            g_specs.append(pl.BlockSpec((tr, cols), lambda i, s: (s[0] * steps + i, 0)))
        else:
            g_specs.append(pl.BlockSpec((tr, HALF), lambda i, s: (i, s[0])))
        blks.append(pl.BlockSpec((tr, cols), lambda i, s: (i, 0)))

    def body(s_ref, *refs):
        for a in range(n):
            t = refs[a][...] + refs[n + a][...]
            refs[2 * n + a][...] = t
            refs[3 * n + a][...] = t.astype(BF16)

    out = pl.pallas_call(
        body,
        grid_spec=pltpu.PrefetchScalarGridSpec(num_scalar_prefetch=1, grid=(steps,), in_specs=g_specs + blks,
                                               out_specs=blks + blks),
        out_shape=([jax.ShapeDtypeStruct(got[nm].shape, F32) for nm in names]
                   + [jax.ShapeDtypeStruct(got[nm].shape, BF16) for nm in names]),
        name=label, compiler_params=_params("parallel"),
    )(core, *[grads[nm] for nm in names], *[got[nm] for nm in names])
    return {nm: (out[a], out[n + a]) for a, nm in enumerate(names)}


def _piece_shape(name):
    rows, cols, by_col = BIG[name]
    return (HALF, cols) if by_col else (rows, HALF)


def _handshake(peers):
    barrier = pltpu.get_barrier_semaphore()
    for peer in peers:
        pl.semaphore_signal(barrier, inc=1, device_id=peer, device_id_type=MESH)
    pl.semaphore_wait(barrier, len(peers))


def _send_chip_sums(sums, names, label, collective_id):
    n = len(names)

    def body(*refs):
        ins, outs = refs[:n], refs[n:2 * n]
        send_sems, recv_sems = refs[2 * n:]
        x, y, c = _place()
        others = [_chip_of(x, y, r) for r in range(3)]
        _handshake([(qx, qy, c) for qx, qy in others])
        copies = []
        for a, name in enumerate(names):
            for r, (qx, qy) in enumerate(others):
                copies.append(_remote(_shard_of(ins[a], name, 2 * qx + qy), outs[a].at[r], send_sems, recv_sems,
                                      3 * a + r, (qx, qy, c)))
        for cp in copies:
            cp.start()
        for cp in copies:
            cp.wait()

    return pl.kernel(
        body, mesh=plsc.ScalarSubcoreMesh(axis_name="seq", num_cores=1), name=label,
        out_type=tuple(jax.ShapeDtypeStruct((3,) + _piece_shape(nm), BF16) for nm in names),
        scratch_types=(pltpu.SemaphoreType.DMA((3 * n,)), pltpu.SemaphoreType.DMA((3 * n,))),
        compiler_params=pltpu.CompilerParams(collective_id=collective_id),
    )(*[sums[nm] for nm in names])


def _total(parts, chip_core):
    steps = 2
    in_specs, out_specs, operands = [], [], []
    for name in BIG_NAMES:
        by_col = BIG[name][2]
        pr, pc = _piece_shape(name)
        tr = pr // steps
        if by_col:
            in_specs.append(pl.BlockSpec((tr, pc), lambda i, s: (i, s[0])))
            out_specs.append(pl.BlockSpec((tr, pc), lambda i, s: (s[1] * steps + i, 0)))
        else:
            in_specs.append(pl.BlockSpec((tr, pc), lambda i, s: (s[0] * steps + i, 0)))
            out_specs.append(pl.BlockSpec((tr, pc), lambda i, s: (i, s[1])))
        for r in range(3):
            in_specs.append(pl.BlockSpec((None, tr, pc), lambda i, s, r=r: (r, i, 0)))
        own, got = parts[name]
        operands += [own, got, got, got]

    def body(s_ref, *refs):
        for a in range(N_BIG):
            o_ref, a_ref, b_ref, c_ref = refs[4 * a:4 * a + 4]
            refs[4 * N_BIG + a][...] = (((o_ref[...] + a_ref[...].astype(F32)) + b_ref[...].astype(F32))
                                        + c_ref[...].astype(F32))

    totals = pl.pallas_call(
        body,
        grid_spec=pltpu.PrefetchScalarGridSpec(num_scalar_prefetch=1, grid=(steps,), in_specs=in_specs,
                                               out_specs=out_specs),
        out_shape=[jax.ShapeDtypeStruct(BIG[name][:2], F32) for name in BIG_NAMES],
        name="totals", compiler_params=_params("parallel"),
    )(chip_core, *operands)
    return dict(zip(BIG_NAMES, totals))


def _share_totals(totals):
    def body(*refs):
        outs = refs[N_BIG:2 * N_BIG]
        send_sems, recv_sems = refs[2 * N_BIG:]
        x, y, c = _place()
        copies = []
        for a, name in enumerate(BIG_NAMES):
            mine = _half_of(outs[a], BIG[name][2], c)
            copies.append(_remote(mine, mine, send_sems, recv_sems, a, (x, y, 1 - c)))
        for cp in copies:
            cp.start()
        for a, name in enumerate(BIG_NAMES):
            theirs = _half_of(outs[a], BIG[name][2], 1 - c)
            _remote(theirs, theirs, send_sems, recv_sems, a, (x, y, 1 - c)).wait_recv()
        for cp in copies:
            cp.wait_send()

    return pl.pallas_call(
        body, in_specs=[ANY] * N_BIG, out_specs=[ANY] * N_BIG,
        out_shape=[jax.ShapeDtypeStruct(BIG[n][:2], F32) for n in BIG_NAMES],
        input_output_aliases={a: a for a in range(N_BIG)},
        scratch_shapes=[pltpu.SemaphoreType.DMA((N_BIG,)), pltpu.SemaphoreType.DMA((N_BIG,))],
        name="share_totals",
    )(*[totals[n] for n in BIG_NAMES])


VEC_ROWS = 32
VEC_ROW = {"mix_norm_g": 0, "conv_b": 1, "b_rgate": 2, "b_igate": 3, "lru_lambda": 4, "rg_norm_g": 5,
           "hg_lower_bound": 6, "hg_norm_g": 8, "ffn_norm_g": 9, "final_norm_g": 10, "loss": 11,
           "conv_w": 12, "meta_tokens": 16}
N_DEV = 8


def _all_reduce_small(pieces, gates):
    names = list(pieces)
    hv, hg = VEC_ROWS // 2, gates.shape[0] // 2

    def body(*refs):
        ins = refs[:len(names)]
        (g_ref, vec_ref, gsum_ref, mine_v, sib_v, sib_g, chip_v, chip_g, got_v, got_g,
         send_sems, recv_sems) = refs[len(names):]
        x, y, c = _place()
        chip = 2 * x + y
        sibling = (x, y, 1 - c)
        mine_v[...] = jnp.zeros_like(mine_v)
        for name, ref in zip(names, ins):
            nr, w = ref.shape
            mine_v[VEC_ROW[name]:VEC_ROW[name] + nr, 0:w] = ref[...]

        swap = [_remote(mine_v, sib_v, send_sems, recv_sems, 0, sibling),
                _remote(g_ref, sib_g, send_sems, recv_sems, 1, sibling)]
        for cp in swap:
            cp.start()
        for cp in swap:
            cp.wait()
        chip_v[...] = mine_v[...] + sib_v[...]
        chip_g[...] = g_ref[...] + sib_g[...]

        rows_v = pl.ds(pl.multiple_of(c * hv, 8), hv)
        rows_g = pl.ds(pl.multiple_of(c * hg, 8), hg)
        got_v[chip] = chip_v[rows_v, :]
        got_g[chip] = chip_g[rows_g, :]
        sends = []
        for r in range(3):
            qx, qy = _chip_of(x, y, r)
            sends.append(_remote(chip_v.at[rows_v, :], got_v.at[chip], send_sems, recv_sems, 2 + r, (qx, qy, c)))
            sends.append(_remote(chip_g.at[rows_g, :], got_g.at[chip], send_sems, recv_sems, 5 + r, (qx, qy, c)))
        for cp in sends:
            cp.start()
        for cp in sends:
            cp.wait()
        vec_ref[rows_v, :] = ((got_v[0] + got_v[1]) + got_v[2]) + got_v[3]
        gsum_ref[rows_g, :] = ((got_g[0] + got_g[1]) + got_g[2]) + got_g[3]

        back = [_remote(vec_ref.at[rows_v, :], vec_ref.at[rows_v, :], send_sems, recv_sems, 8, sibling),
                _remote(gsum_ref.at[rows_g, :], gsum_ref.at[rows_g, :], send_sems, recv_sems, 9, sibling)]
        for cp in back:
            cp.start()
        theirs_v = vec_ref.at[pl.ds(pl.multiple_of((1 - c) * hv, 8), hv), :]
        theirs_g = gsum_ref.at[pl.ds(pl.multiple_of((1 - c) * hg, 8), hg), :]
        _remote(theirs_v, theirs_v, send_sems, recv_sems, 8, sibling).wait_recv()
        _remote(theirs_g, theirs_g, send_sems, recv_sems, 9, sibling).wait_recv()
        for cp in back:
            cp.wait_send()

    vmem = pl.BlockSpec(memory_space=pltpu.VMEM)
    n_sems = 10
    return pl.pallas_call(
        body, in_specs=[vmem] * (len(names) + 1), out_specs=[vmem, vmem],
        out_shape=[jax.ShapeDtypeStruct((VEC_ROWS, D_MODEL), F32), jax.ShapeDtypeStruct(gates.shape, F32)],
        scratch_shapes=[pltpu.VMEM((VEC_ROWS, D_MODEL), F32), pltpu.VMEM((VEC_ROWS, D_MODEL), F32),
                        pltpu.VMEM(gates.shape, F32), pltpu.VMEM((VEC_ROWS, D_MODEL), F32),
                        pltpu.VMEM(gates.shape, F32), pltpu.VMEM((N_CHIPS, hv, D_MODEL), F32),
                        pltpu.VMEM((N_CHIPS, hg) + gates.shape[1:], F32),
                        pltpu.SemaphoreType.DMA((n_sems,)), pltpu.SemaphoreType.DMA((n_sems,))],
        name="all_reduce_small",
    )(*[pieces[n] for n in names], gates)


def _adamw_math(w, g, m, v):
    m = ADAM_B1 * m + (1.0 - ADAM_B1) * g
    v = ADAM_B2 * v + (1.0 - ADAM_B2) * (g * g)
    m_hat = m / (1.0 - ADAM_B1 ** ADAM_STEP)
    v_hat = v / (1.0 - ADAM_B2 ** ADAM_STEP)
    delta = -ADAM_LR * (m_hat / (jnp.sqrt(v_hat) + ADAM_EPS) + ADAM_WD * w)
    return delta, m, v


def _adamw_big(w, g, m, v):
    steps = 8
    blks = []
    for name in BIG_NAMES:
        rows, cols, _ = BIG[name]
        blks.append(pl.BlockSpec((rows // steps, cols), lambda i: (i, 0)))

    def body(*refs):
        ins, outs = refs[:4 * N_BIG], refs[4 * N_BIG:]
        for a in range(N_BIG):
            w_ref, g_ref, m_ref, v_ref = (ins[k * N_BIG + a] for k in range(4))
            d, nm, nv = _adamw_math(w_ref[...], g_ref[...], m_ref[...], v_ref[...])
            outs[a][...] = d
            outs[N_BIG + a][...] = nm
            outs[2 * N_BIG + a][...] = nv

    shapes = [jax.ShapeDtypeStruct(BIG[name][:2], F32) for name in BIG_NAMES]
    out = pl.pallas_call(
        body, grid=(steps,), in_specs=blks * 4, out_specs=blks * 3, out_shape=shapes * 3,
        name="adamw_big", compiler_params=_params("parallel"),
    )(*[t[name] for t in (w, g, m, v) for name in BIG_NAMES])
    return {name: (out[a], out[N_BIG + a], out[2 * N_BIG + a]) for a, name in enumerate(BIG_NAMES)}


SMALL = {"meta_tokens": (N_META, D_MODEL // N_CHIPS), "mix_norm_g": (1, D_MODEL), "conv_w": (CONV_W, D_RG // N_CHIPS),
         "conv_b": (1, D_RG), "w_rgate": (D_RG, RG_HEAD_DIM), "b_rgate": (1, D_RG), "w_igate": (D_RG, RG_HEAD_DIM),
         "b_igate": (1, D_RG), "lru_lambda": (1, D_RG), "rg_norm_g": (1, D_RG), "hg_lower_bound": (2, D_HG),
         "hg_norm_g": (1, HG_HEAD_DIM), "ffn_norm_g": (1, D_MODEL), "final_norm_g": (1, D_MODEL)}
SMALL_NAMES = tuple(SMALL)
SHARDED_SMALL = ("meta_tokens", "conv_w")


def _adamw_small(vec, gates, w, m, v):
    n = len(SMALL_NAMES)

    def body(*refs):
        vec_ref, gates_ref = refs[:2]
        w_refs, m_refs, v_refs = refs[2:2 + n], refs[2 + n:2 + 2 * n], refs[2 + 2 * n:2 + 3 * n]
        outs = refs[2 + 3 * n:]
        loss_ref = outs[0]
        x, y, _ = _place()
        chip = 2 * x + y
        loss_ref[...] = vec_ref[VEC_ROW["loss"]:VEC_ROW["loss"] + 1, 0:1]

        def update(k, g):
            g_ref, d_ref, nm_ref, nv_ref = outs[1 + 4 * k:5 + 4 * k]
            g_ref[...] = g
            d_ref[...], nm_ref[...], nv_ref[...] = _adamw_math(w_refs[k][...], g, m_refs[k][...], v_refs[k][...])

        for k, name in enumerate(SMALL_NAMES):
            nr, w_ = SMALL[name]
            if name == "w_rgate":
                update(k, gates_ref[0:D_RG, :])
            elif name == "w_igate":
                update(k, gates_ref[D_RG:2 * D_RG, :])
            elif name in SHARDED_SMALL:
                r0 = VEC_ROW[name]
                for q in range(N_CHIPS):
                    @pl.when(chip == q)
                    def _(k=k, r0=r0, nr=nr, w_=w_, q=q):
                        update(k, vec_ref[r0:r0 + nr, q * w_:(q + 1) * w_])
            else:
                r0 = VEC_ROW[name]
                update(k, vec_ref[r0:r0 + nr, 0:w_])

    vmem = pl.BlockSpec(memory_space=pltpu.VMEM)
    out_shape = [jax.ShapeDtypeStruct((1, 1), F32)]
    for name in SMALL_NAMES:
        out_shape += [jax.ShapeDtypeStruct(SMALL[name], F32)] * 4
    outs = pl.pallas_call(
        body, in_specs=[vmem] * (2 + 3 * n), out_specs=[vmem] * len(out_shape), out_shape=out_shape,
        name="adamw_small",
    )(vec, gates, *[w[k] for k in SMALL_NAMES], *[m[k] for k in SMALL_NAMES], *[v[k] for k in SMALL_NAMES])
    loss = outs[0]
    res = {name: tuple(outs[1 + 4 * k:5 + 4 * k]) for k, name in enumerate(SMALL_NAMES)}
    return loss, res


WEIGHT_NAMES = ("meta_tokens", "mix_norm_g", "w_in", "conv_w", "conv_b", "w_rgate", "b_rgate", "w_igate", "b_igate",
                "lru_lambda", "rg_norm_g", "hg_lower_bound", "hg_norm_g", "w_out", "ffn_norm_g", "w_gate_up", "w_down",
                "final_norm_g")


def _to_2d(name, a):
    if name in BIG:
        return a.reshape(BIG[name][:2])
    return a.reshape(SMALL[name])


def kernel(x, meta_tokens, mix_norm_g, w_in, conv_w, conv_b, w_rgate, b_rgate, w_igate, b_igate, lru_lambda, rg_norm_g, hg_lower_bound, hg_norm_g, w_out, ffn_norm_g, w_gate_up, w_down, final_norm_g, loss_target, m_meta_tokens, m_mix_norm_g, m_w_in, m_conv_w, m_conv_b, m_w_rgate, m_b_rgate, m_w_igate, m_b_igate, m_lru_lambda, m_rg_norm_g, m_hg_lower_bound, m_hg_norm_g, m_w_out, m_ffn_norm_g, m_w_gate_up, m_w_down, m_final_norm_g, v_meta_tokens, v_mix_norm_g, v_w_in, v_conv_w, v_conv_b, v_w_rgate, v_b_rgate, v_w_igate, v_b_igate, v_lru_lambda, v_rg_norm_g, v_hg_lower_bound, v_hg_norm_g, v_w_out, v_ffn_norm_g, v_w_gate_up, v_w_down, v_final_norm_g):
    w_raw = dict(zip(WEIGHT_NAMES, (meta_tokens, mix_norm_g, w_in, conv_w, conv_b, w_rgate, b_rgate, w_igate, b_igate,
                                    lru_lambda, rg_norm_g, hg_lower_bound, hg_norm_g, w_out, ffn_norm_g, w_gate_up,
                                    w_down, final_norm_g)))
    m_raw = dict(zip(WEIGHT_NAMES, (m_meta_tokens, m_mix_norm_g, m_w_in, m_conv_w, m_conv_b, m_w_rgate, m_b_rgate,
                                    m_w_igate, m_b_igate, m_lru_lambda, m_rg_norm_g, m_hg_lower_bound, m_hg_norm_g,
                                    m_w_out, m_ffn_norm_g, m_w_gate_up, m_w_down, m_final_norm_g)))
    v_raw = dict(zip(WEIGHT_NAMES, (v_meta_tokens, v_mix_norm_g, v_w_in, v_conv_w, v_conv_b, v_w_rgate, v_b_rgate,
                                    v_w_igate, v_b_igate, v_lru_lambda, v_rg_norm_g, v_hg_lower_bound, v_hg_norm_g,
                                    v_w_out, v_ffn_norm_g, v_w_gate_up, v_w_down, v_final_norm_g)))
    w = {k: _to_2d(k, a) for k, a in w_raw.items()}
    m = {k: _to_2d(k, a) for k, a in m_raw.items()}
    v = {k: _to_2d(k, a) for k, a in v_raw.items()}

    x_i, y_i, c_i = _place()
    core = jnp.reshape(c_i, (1,)).astype(jnp.int32)
    chip = jnp.reshape(2 * x_i + y_i, (1,)).astype(jnp.int32)
    chip_core = jnp.concatenate([chip, core])

    placed = _cast_into_full(w, chip)
    first, (meta_full, cw_full) = _gather_weights(placed, [w["meta_tokens"], w["conv_w"]], ("w_in",), "gather_first", 1)
    rest, _ = _gather_weights(placed, [], ("w_out", "w_gate_up", "w_down"), "gather_rest", 2)
    full = {**first, **rest}

    seq = x.shape[1]
    small ={k: w[k] for k in SMALL_NAMES if k not in SHARDED_SMALL}
    small["conv_w"] = cw_full

    def reduce_to_chips(grads, names, tag, collective_ids):
        got = _exchange_halves(grads, names, "exchange_halves_" + tag, collective_ids[0])

        def chip_sums():
            return _chip_sum(grads, got, names, core, "chip_sum_" + tag)

        def send(sums):
            arrived = _send_chip_sums({n: sums[n][1] for n in names}, names, "send_chip_sums_" + tag,
                                      collective_ids[1])
            return {n: (sums[n][0], a) for n, a in zip(names, arrived)}

        return chip_sums, send

    ffn_names, mixer_names = ("w_gate_up", "w_down"), ("w_in", "w_out")
    loss, grad_x, grads, parts, parts_mixer = _local_step(
        x.reshape(seq, D_MODEL), meta_full, loss_target.reshape(seq, D_MODEL),
        full["w_in"], full["w_out"], full["w_gate_up"], full["w_down"], small,
        on_ffn_grads=lambda g: reduce_to_chips(g, ffn_names, "ffn", (3, 4)),
        on_mixer_grads=lambda g: reduce_to_chips(g, mixer_names, "mixer", (None, 5)))
    parts.update(parts_mixer)
    totals = _total(parts, chip_core)
    g_big = dict(zip(BIG_NAMES, _share_totals(totals)))

    pieces = {k: grads[k] for k in VEC_ROW if k != "loss"}
    pieces["loss"] = loss
    vec, gates = _all_reduce_small(pieces, grads["w_gates"])
    loss_sum, res = _adamw_small(vec, gates, w, m, v)
    updates = _adamw_big(w, g_big, m, v)
    for n in BIG_NAMES:
        res[n] = (g_big[n],) + updates[n]

    out = [loss_sum.reshape(()), grad_x.reshape(1, seq, D_MODEL)]
    for j in range(4):
        out += [res[n][j].reshape(w_raw[n].shape) for n in WEIGHT_NAMES]
    return tuple(out)
```

```python
import functools
import math

import jax
import jax.numpy as jnp
from jax import lax
from jax.experimental import pallas as pl
from jax.experimental.pallas import tpu as pltpu
from jax.experimental.pallas import tpu_sc as plsc

F32 = jnp.float32
BF16 = jnp.bfloat16
HIGHEST = lax.Precision.HIGHEST
MESH = pl.DeviceIdType.MESH

D_MODEL = 1024
D_RG = 512
RG_HEAD_DIM = 64
D_HG = 512
HG_HEAD_DIM = 128
HG_HEADS = 4
CHUNK = 64
SUB = 16
N_SUB = CHUNK // SUB
N_META = 16
PAD = CHUNK - N_META
D_IN = 3072
D_FF = 2816
CONV_W = 4
LRU_C = 8.0
EPS = 1e-6
EXP_CLAMP = 80.0
GELU_C = math.sqrt(2.0 / math.pi)
GELU_A = 0.044715
N_CHIPS = 4

ADAM_LR = 0.001
ADAM_B1 = 0.9
ADAM_B2 = 0.999
ADAM_EPS = 1e-08
ADAM_WD = 0.01
ADAM_STEP = 10

VMEM_LIMIT = 56 * 1024 * 1024


def _params(*sem):
    return pltpu.CompilerParams(dimension_semantics=sem, vmem_limit_bytes=VMEM_LIMIT)


def _row_tile(rows, target):
    best = None
    for t in range(16, min(rows, target) + 1, 16):
        if rows % t == 0:
            best = t
    assert best is not None, rows
    return best


def _sigmoid(x):
    return 0.5 * jnp.tanh(0.5 * x) + 0.5


def _dot(a, b):
    return jnp.dot(a, b, preferred_element_type=F32)


def _dot_nt(a, b):
    return lax.dot_general(a, b, (((1,), (1,)), ((), ())), preferred_element_type=F32)


def _dot_tn(a, b):
    return lax.dot_general(a, b, (((0,), (0,)), ((), ())), preferred_element_type=F32)


def _rms(x):
    return lax.rsqrt(jnp.mean(x * x, axis=-1, keepdims=True) + EPS)


def _rms_bwd(dn, n, r):
    return r * (dn - n * jnp.mean(dn * n, axis=-1, keepdims=True))


def _gelu_parts(x):
    t = jnp.tanh(GELU_C * (x + GELU_A * x * x * x))
    g = 0.5 * x * (1.0 + t)
    dg = 0.5 * (1.0 + t) + 0.5 * x * (1.0 - t * t) * GELU_C * (1.0 + 3.0 * GELU_A * x * x)
    return g, dg


def _softplus_neg(lam):
    e = jnp.exp(-jnp.abs(lam))
    w = 1.0 + e
    log1p = jnp.where(w == 1.0, e, jnp.log(w) * e / (w - 1.0))
    return jnp.maximum(-lam, 0.0) + log1p


def _head_mask():
    r = lax.broadcasted_iota(jnp.int32, (D_RG, D_RG), 0) // RG_HEAD_DIM
    c = lax.broadcasted_iota(jnp.int32, (D_RG, D_RG), 1) // RG_HEAD_DIM
    return r == c


def _head_fold():
    r = lax.broadcasted_iota(jnp.int32, (D_RG, RG_HEAD_DIM), 0) % RG_HEAD_DIM
    c = lax.broadcasted_iota(jnp.int32, (D_RG, RG_HEAD_DIM), 1)
    return (r == c).astype(F32)


def _gate_weights(w_r, w_i):
    def body(wr_ref, wi_ref, o_ref):
        fold = _head_fold()
        mask = _head_mask()
        for k, ref in enumerate((wr_ref, wi_ref)):
            full = lax.dot_general(ref[...], fold, (((1,), (1,)), ((), ())),
                                   precision=HIGHEST, preferred_element_type=F32)
            o_ref[:, k * D_RG:(k + 1) * D_RG] = jnp.where(mask, full, 0.0).astype(BF16)

    return pl.pallas_call(
        body, out_shape=jax.ShapeDtypeStruct((D_RG, 2 * D_RG), BF16), name="gate_weights",
    )(w_r, w_i)


HEAD = PAD + N_META


def _window_copies(seq_hbm, buf, sems, tm):
    def first(to_vmem):
        seq, vm = seq_hbm.at[pl.ds(0, tm - HEAD)], buf.at[0, pl.ds(HEAD, tm - HEAD)]
        return pltpu.make_async_copy(seq, vm, sems.at[0]) if to_vmem else pltpu.make_async_copy(vm, seq, sems.at[0])

    def later(j, slot, to_vmem):
        seq, vm = seq_hbm.at[pl.ds(pl.multiple_of(j * tm - HEAD, 8), tm)], buf.at[slot]
        if to_vmem:
            return pltpu.make_async_copy(seq, vm, sems.at[slot])
        return pltpu.make_async_copy(vm, seq, sems.at[slot])

    return first, later


def _fetch_window(seq_hbm, buf, sems, i, n_steps, tm):
    first, later = _window_copies(seq_hbm, buf, sems, tm)
    slot = i % 2

    @pl.when(i == 0)
    def _():
        first(True).start()

    if n_steps > 1:
        @pl.when(i + 1 < n_steps)
        def _():
            later(i + 1, 1 - slot, True).start()

    @pl.when(i == 0)
    def _():
        first(True).wait()

    if n_steps > 1:
        @pl.when(i > 0)
        def _():
            later(i, slot, True).wait()

    return slot


def _in_proj(x, meta, g1, w_in):
    T = x.shape[0] + HEAD
    tm = _row_tile(T, 416)
    n_steps = T // tm

    def body(x_hbm, meta_ref, g_ref, w_ref, p_ref, u_ref, h_ref, buf, sems):
        i = pl.program_id(0)
        slot = _fetch_window(x_hbm, buf, sems, i, n_steps, tm)

        @pl.when(i == 0)
        def _():
            buf[0, 0:PAD, :] = jnp.zeros((PAD, D_MODEL), F32)
            buf[0, PAD:HEAD, :] = meta_ref[...]

        h = buf[slot]
        h_ref[...] = h
        u = (h * _rms(h) * g_ref[...]).astype(BF16)
        u_ref[...] = u
        p_ref[...] = _dot(u, w_ref[...])

    return pl.pallas_call(
        body, grid=(n_steps,),
        in_specs=[pl.BlockSpec(memory_space=pl.ANY),
                  pl.BlockSpec((N_META, D_MODEL), lambda i: (0, 0)),
                  pl.BlockSpec((1, D_MODEL), lambda i: (0, 0)),
                  pl.BlockSpec((D_MODEL, D_IN), lambda i: (0, 0))],
        out_specs=[pl.BlockSpec((tm, D_IN), lambda i: (i, 0)),
                   pl.BlockSpec((tm, D_MODEL), lambda i: (i, 0)),
                   pl.BlockSpec((tm, D_MODEL), lambda i: (i, 0))],
        out_shape=[jax.ShapeDtypeStruct((T, D_IN), F32), jax.ShapeDtypeStruct((T, D_MODEL), BF16),
                   jax.ShapeDtypeStruct((T, D_MODEL), F32)],
        scratch_shapes=[pltpu.VMEM((2, tm, D_MODEL), F32), pltpu.SemaphoreType.DMA((2,))],
        name="in_proj", compiler_params=_params("arbitrary"),
    )(x, meta, g1, w_in)


def _scan_block_fwd(A, B, rowi):
    for d in (1, 2, 4):
        a_sh = pltpu.roll(A, d, axis=0)
        b_sh = pltpu.roll(B, d, axis=0)
        m = rowi >= d
        B = jnp.where(m, A * b_sh + B, B)
        A = jnp.where(m, A * a_sh, A)
    return A, B


def _scan_block_bwd(A, B, rowi):
    for d in (1, 2, 4):
        a_sh = pltpu.roll(A, 8 - d, axis=0)
        b_sh = pltpu.roll(B, 8 - d, axis=0)
        m = rowi < 8 - d
        B = jnp.where(m, A * b_sh + B, B)
        A = jnp.where(m, A * a_sh, A)
    return A, B


def _rg_gates(xc, w_ref, bg_ref, lam):
    pre = _dot(xc.astype(BF16), w_ref[...]) + bg_ref[...]
    r = _sigmoid(pre[:, :D_RG])
    ig = _sigmoid(pre[:, D_RG:])
    sp = _softplus_neg(lam)
    la = -LRU_C * sp * r
    a = jnp.exp(la)
    th = jnp.tanh(la)
    u = 1.0 - th
    rc = pl.reciprocal(u, approx=True)
    rc = rc * (2.0 - u * rc)
    rc = rc * (2.0 - u * rc)
    m2 = -2.0 * th * rc
    inv_m = lax.rsqrt(jnp.maximum(m2, 1e-30))
    return r, ig, sp, a, m2 * inv_m, inv_m


def _conv(ext, cw_ref, cb_ref, tm):
    xc = cb_ref[...] + cw_ref[0:1, :] * ext[8 - 3:8 - 3 + tm, :]
    for j in range(1, CONV_W):
        xc = xc + cw_ref[j:j + 1, :] * ext[8 - 3 + j:8 - 3 + j + tm, :]
    return xc


def _scan_unroll(blocks):
    return 4 if blocks % 4 == 0 else 2 if blocks % 2 == 0 else 1


def _rg_fwd(p, cw, cb, wg, bg, lam, rg_g):
    T = p.shape[0]
    tm = _row_tile(T, 832)
    unroll = _scan_unroll(tm // 8)

    def body(xg_ref, cw_ref, cb_ref, w_ref, bg_ref, lam_ref, g_ref, y_ref, h_ref, ext, a_s, b_s, carry):
        i = pl.program_id(0)

        @pl.when(i == 0)
        def _():
            ext[0:8, :] = jnp.zeros((8, D_RG), F32)
            carry[...] = jnp.zeros((1, D_RG), F32)

        ext[8:8 + tm, :] = xg_ref[:, :D_RG]
        xc = _conv(ext, cw_ref, cb_ref, tm)
        r, ig, sp, a, m, _ = _rg_gates(xc, w_ref, bg_ref, lam_ref[...])
        row = i * tm + lax.broadcasted_iota(jnp.int32, (tm, 1), 0)
        a_s[...] = a
        b_s[...] = jnp.where(row >= PAD, m * ig * xc, 0.0)
        rowi = lax.broadcasted_iota(jnp.int32, (8, D_RG), 0)

        def blk(j, c):
            for u in range(unroll):
                o = pl.multiple_of((j * unroll + u) * 8, 8)
                A, B = _scan_block_fwd(a_s[pl.ds(o, 8), :], b_s[pl.ds(o, 8), :], rowi)
                h = B + A * c
                h_ref[pl.ds(o, 8), :] = h
                c = h[7:8, :]
            return c

        carry[...] = lax.fori_loop(0, tm // (8 * unroll), blk, carry[...])
        ext[0:8, :] = ext[tm:tm + 8, :]
        g, _ = _gelu_parts(xg_ref[:, D_RG:])
        yy = g * h_ref[...]
        y_ref[...] = (yy * _rms(yy) * g_ref[...]).astype(BF16)

    vec = lambda n: pl.BlockSpec((1, n), lambda i: (0, 0))
    return pl.pallas_call(
        body, grid=(T // tm,),
        in_specs=[pl.BlockSpec((tm, 2 * D_RG), lambda i: (i, 0)),
                  pl.BlockSpec((CONV_W, D_RG), lambda i: (0, 0)), vec(D_RG),
                  pl.BlockSpec((D_RG, 2 * D_RG), lambda i: (0, 0)), vec(2 * D_RG), vec(D_RG), vec(D_RG)],
        out_specs=[pl.BlockSpec((tm, D_RG), lambda i: (i, 0)), pl.BlockSpec((tm, D_RG), lambda i: (i, 0))],
        out_shape=[jax.ShapeDtypeStruct((T, D_RG), BF16), jax.ShapeDtypeStruct((T, D_RG), F32)],
        scratch_shapes=[pltpu.VMEM((tm + 8, D_RG), F32), pltpu.VMEM((tm, D_RG), F32),
                        pltpu.VMEM((tm, D_RG), F32), pltpu.VMEM((1, D_RG), F32)],
        name="rg_fwd", compiler_params=_params("arbitrary"),
    )(p, cw, cb, wg, bg, lam, rg_g)


def _tri(lower):
    r = lax.broadcasted_iota(jnp.int32, (CHUNK, CHUNK), 0)
    c = lax.broadcasted_iota(jnp.int32, (CHUNK, CHUNK), 1)
    return ((c <= r) if lower else (c >= r)).astype(F32)


def _hg_gates(hq, hf, lbraw_ref, valid):
    lb = _sigmoid(lbraw_ref[0:1, :] - lbraw_ref[1:2, :])
    sq = _sigmoid(hq)
    q = hq * sq
    sf = _sigmoid(hf)
    f = lb + (1.0 - lb) * sf
    lf = jnp.where(valid, jnp.log(f), 0.0)
    b = jnp.dot(_tri(True), lf, precision=HIGHEST, preferred_element_type=F32)
    return lb, sq, q, sf, f, b


def _hg_head(qh, kh, bh):
    blk = lax.broadcasted_iota(jnp.int32, (CHUNK, 1), 0) // SUB
    b_last = bh[CHUNK - 1:CHUNK, :]
    refs = [bh[SUB * s:SUB * s + 1, :] for s in range(N_SUB)]
    r_sel = refs[N_SUB - 1]
    for s in range(N_SUB - 2, -1, -1):
        r_sel = jnp.where(blk == s, refs[s], r_sel)
    eb = jnp.exp(bh)
    eq = jnp.exp(bh - r_sel)
    ekh = jnp.exp(b_last - bh)
    ek = [jnp.exp(jnp.minimum(refs[s] - bh, EXP_CLAMP)) for s in range(N_SUB)]
    qe = qh * eq
    q_hat = jnp.concatenate([jnp.where(blk == s, qe, 0.0) for s in range(N_SUB)], axis=1)
    k_til = jnp.concatenate([kh * ek[s] for s in range(N_SUB)], axis=1)
    return blk, b_last, eb, eq, ekh, ek, q_hat, k_til


def _causal():
    r = lax.broadcasted_iota(jnp.int32, (CHUNK, CHUNK), 0)
    c = lax.broadcasted_iota(jnp.int32, (CHUNK, CHUNK), 1)
    return r >= c


def _chunks_per_step(n_chunks):
    for c in (5, 4, 3, 2):
        if n_chunks % c == 0:
            return c
    return 1


def _hg_fwd(p, lbraw, hg_g):
    T = p.shape[0]
    n_chunks = T // CHUNK
    cps = _chunks_per_step(n_chunks)
    rows = cps * CHUNK

    def body(hq_ref, hf_ref, hi_ref, hg_ref, lb_ref, g_ref, y_ref, o_ref, st_all_ref, st):
        i = pl.program_id(0)

        @pl.when(i == 0)
        def _():
            st[...] = jnp.zeros_like(st)

        def chunk(j, carry):
            rs = pl.ds(pl.multiple_of(j * CHUNK, CHUNK), CHUNK)
            chunk_body(i * cps + j, hq_ref.at[rs, :], hf_ref.at[rs, :], hi_ref.at[rs, :], hg_ref.at[rs, :], lb_ref,
                       g_ref, y_ref.at[rs, :], o_ref.at[rs, :], st_all_ref.at[pl.ds(j, 1)], st)
            return carry

        lax.fori_loop(0, cps, chunk, 0, unroll=True)

    def chunk_body(n, hq_ref, hf_ref, hi_ref, hg_ref, lb_ref, g_ref, y_ref, o_ref, st_all_ref, st):
        valid = (n * CHUNK + lax.broadcasted_iota(jnp.int32, (CHUNK, 1), 0)) >= PAD
        hq, hf, v, hg = hq_ref[...], hf_ref[...], hi_ref[...], hg_ref[...]
        lb, sq, q, sf, f, b = _hg_gates(hq, hf, lb_ref, valid)
        k = 1.0 - f
        st_all_ref[0] = st[...]
        causal = _causal()
        v_t = v.T.astype(BF16)
        heads = [slice(h * HG_HEAD_DIM, (h + 1) * HG_HEAD_DIM) for h in range(HG_HEADS)]
        fac = []
        for sl in heads:
            qh, kh, bh = q[:, sl], k[:, sl], b[:, sl]
            _, b_last, eb, _, ekh, _, q_hat, k_til = _hg_head(qh, kh, bh)
            fac.append((jnp.exp(b_last), (qh * eb).astype(BF16), q_hat.astype(BF16), k_til.astype(BF16),
                        (kh * ekh).astype(BF16), v[:, sl].astype(BF16)))
        raw = []
        for sl, (_, q_til, q_hat, k_til, k_hat, _) in zip(heads, fac):
            st_h = st[sl, :]
            raw.append((_dot_nt(q_til, st_h.astype(BF16)), _dot_nt(q_hat, k_til), _dot(v_t[sl, :], k_hat), st_h))
        for sl, (e_last, _, _, _, _, vb), (inter, att, upd, st_h) in zip(heads, fac, raw):
            o = inter + _dot(jnp.where(causal, att, 0.0).astype(BF16), vb)
            st[sl, :] = st_h * e_last + upd
            o_ref[:, sl] = o
            hgh = hg[:, sl]
            y_ref[:, sl] = (o * _rms(o) * g_ref[...] * (hgh * _sigmoid(hgh))).astype(BF16)

    col = lambda j: pl.BlockSpec((rows, D_HG), lambda n: (n, j))
    return pl.pallas_call(
        body, grid=(n_chunks // cps,),
        in_specs=[col(2), col(3), col(4), col(5),
                  pl.BlockSpec((2, D_HG), lambda n: (0, 0)), pl.BlockSpec((1, HG_HEAD_DIM), lambda n: (0, 0))],
        out_specs=[pl.BlockSpec((rows, D_HG), lambda n: (n, 0)), pl.BlockSpec((rows, D_HG), lambda n: (n, 0)),
                   pl.BlockSpec((cps, D_HG, HG_HEAD_DIM), lambda n: (n, 0, 0))],
        out_shape=[jax.ShapeDtypeStruct((T, D_HG), BF16), jax.ShapeDtypeStruct((T, D_HG), F32),
                   jax.ShapeDtypeStruct((n_chunks, D_HG, HG_HEAD_DIM), F32)],
        scratch_shapes=[pltpu.VMEM((D_HG, HG_HEAD_DIM), F32)],
        name="hg_fwd", compiler_params=_params("arbitrary"),
    )(p, p, p, p, lbraw, hg_g)


def _out_proj(h0, y_rg, y_hg, w_out, g2):
    T = h0.shape[0]
    tm = _row_tile(T, 832)

    def body(h_ref, yr_ref, yh_ref, w_ref, g_ref, h1_ref, v_ref, y_ref):
        y_ref[:, :D_RG] = yr_ref[...]
        y_ref[:, D_RG:] = yh_ref[...]
        h1 = h_ref[...] + _dot(y_ref[...], w_ref[...])
        h1_ref[...] = h1
        v_ref[...] = (h1 * _rms(h1) * g_ref[...]).astype(BF16)

    row = lambda n: pl.BlockSpec((tm, n), lambda i: (i, 0))
    return pl.pallas_call(
        body, grid=(T // tm,),
        in_specs=[row(D_MODEL), row(D_RG), row(D_HG), pl.BlockSpec((D_MODEL, D_MODEL), lambda i: (0, 0)),
                  pl.BlockSpec((1, D_MODEL), lambda i: (0, 0))],
        out_specs=[row(D_MODEL), row(D_MODEL), row(D_MODEL)],
        out_shape=[jax.ShapeDtypeStruct((T, D_MODEL), F32), jax.ShapeDtypeStruct((T, D_MODEL), BF16),
                   jax.ShapeDtypeStruct((T, D_MODEL), BF16)],
        name="out_proj", compiler_params=_params("parallel"),
    )(h0, y_rg, y_hg, w_out, g2)


def _gate_up(v, w_gu):
    T = v.shape[0]
    tm = _row_tile(T, 416)

    def body(v_ref, w_ref, gu_ref, act_ref):
        gu = _dot(v_ref[...], w_ref[...])
        gu_ref[...] = gu.astype(BF16)
        g = gu[:, :D_FF]
        act_ref[...] = (g * _sigmoid(g) * gu[:, D_FF:]).astype(BF16)

    row = lambda n: pl.BlockSpec((tm, n), lambda i: (i, 0))
    return pl.pallas_call(
        body, grid=(T // tm,),
        in_specs=[row(D_MODEL), pl.BlockSpec((D_MODEL, 2 * D_FF), lambda i: (0, 0))],
        out_specs=[row(2 * D_FF), row(D_FF)],
        out_shape=[jax.ShapeDtypeStruct((T, 2 * D_FF), BF16), jax.ShapeDtypeStruct((T, D_FF), BF16)],
        name="gate_up", compiler_params=_params("parallel"),
    )(v, w_gu)


def _down_loss(h1, act, w_down, gf, target):
    T = h1.shape[0]
    tm = _row_tile(T, 832)
    n_steps = T // tm

    def body(h_ref, a_ref, w_ref, g_ref, t_hbm, dh2_ref, dh2b_ref, loss_ref, gg_ref, tbuf, sems):
        i = pl.program_id(0)
        slot = _fetch_window(t_hbm, tbuf, sems, i, n_steps, tm)

        @pl.when(i == 0)
        def _():
            loss_ref[...] = jnp.zeros_like(loss_ref)
            gg_ref[...] = jnp.zeros_like(gg_ref)
            tbuf[0, 0:HEAD, :] = jnp.zeros((HEAD, D_MODEL), F32)

        h2 = h_ref[...] + _dot(a_ref[...], w_ref[...])
        r = _rms(h2)
        n = h2 * r
        gf_ = g_ref[...]
        row = i * tm + lax.broadcasted_iota(jnp.int32, (tm, 1), 0)
        err = jnp.where(row >= HEAD, n * gf_ - tbuf[slot], 0.0)
        loss_ref[...] += 0.5 * jnp.sum(jnp.mean(err * err, axis=-1, keepdims=True), axis=0, keepdims=True)
        dy = err * (1.0 / D_MODEL)
        gg_ref[...] += jnp.sum(dy * n, axis=0, keepdims=True)
        dh2 = _rms_bwd(dy * gf_, n, r)
        dh2_ref[...] = dh2
        dh2b_ref[...] = dh2.astype(BF16)

    row_spec = lambda n: pl.BlockSpec((tm, n), lambda i: (i, 0))
    return pl.pallas_call(
        body, grid=(T // tm,),
        in_specs=[row_spec(D_MODEL), row_spec(D_FF), pl.BlockSpec((D_FF, D_MODEL), lambda i: (0, 0)),
                  pl.BlockSpec((1, D_MODEL), lambda i: (0, 0)), pl.BlockSpec(memory_space=pl.ANY)],
        out_specs=[row_spec(D_MODEL), row_spec(D_MODEL), pl.BlockSpec((1, 1), lambda i: (0, 0)),
                   pl.BlockSpec((1, D_MODEL), lambda i: (0, 0))],
        out_shape=[jax.ShapeDtypeStruct((T, D_MODEL), F32), jax.ShapeDtypeStruct((T, D_MODEL), BF16),
                   jax.ShapeDtypeStruct((1, 1), F32), jax.ShapeDtypeStruct((1, D_MODEL), F32)],
        scratch_shapes=[pltpu.VMEM((2, tm, D_MODEL), F32), pltpu.SemaphoreType.DMA((2,))],
        name="down_loss", compiler_params=_params("arbitrary"),
    )(h1, act, w_down, gf, target)


def _ffn_bwd_act(dh2b, gu, w_down):
    T = dh2b.shape[0]
    tm = _row_tile(T, 416)

    def body(d_ref, gu_ref, w_ref, dgu_ref):
        dact = _dot_nt(d_ref[...], w_ref[...]).astype(BF16)
        g = gu_ref[:, :D_FF]
        u = gu_ref[:, D_FF:]
        s = _sigmoid(g)
        dgu_ref[:, :D_FF] = dact * u * (s * (1.0 + g * (1.0 - s)))
        dgu_ref[:, D_FF:] = dact * (g * s)

    row = lambda n: pl.BlockSpec((tm, n), lambda i: (i, 0))
    return pl.pallas_call(
        body, grid=(T // tm,),
        in_specs=[row(D_MODEL), row(2 * D_FF), pl.BlockSpec((D_FF, D_MODEL), lambda i: (0, 0))],
        out_specs=row(2 * D_FF),
        out_shape=jax.ShapeDtypeStruct((T, 2 * D_FF), BF16),
        name="ffn_bwd_act", compiler_params=_params("parallel"),
    )(dh2b, gu, w_down)


def _ffn_bwd_in(dgu, w_gu, h1, g2, dh2, w_out):
    T = h1.shape[0]
    tm = _row_tile(T, 416)

    def body(dgu_ref, wgu_ref, h_ref, g_ref, d2_ref, wo_ref, dh1_ref, dh1b_ref, dy_ref, gg_ref):
        i = pl.program_id(0)

        @pl.when(i == 0)
        def _():
            gg_ref[...] = jnp.zeros_like(gg_ref)

        dv = _dot_nt(dgu_ref[...], wgu_ref[...])
        h1 = h_ref[...]
        r = _rms(h1)
        n = h1 * r
        gg_ref[...] += jnp.sum(dv * n, axis=0, keepdims=True)
        dh1 = d2_ref[...] + _rms_bwd(dv * g_ref[...], n, r)
        dh1_ref[...] = dh1
        db = dh1.astype(BF16)
        dh1b_ref[...] = db
        dy_ref[...] = _dot_nt(db, wo_ref[...])

    row = lambda n: pl.BlockSpec((tm, n), lambda i: (i, 0))
    return pl.pallas_call(
        body, grid=(T // tm,),
        in_specs=[row(2 * D_FF), pl.BlockSpec((D_MODEL, 2 * D_FF), lambda i: (0, 0)),
                  row(D_MODEL), pl.BlockSpec((1, D_MODEL), lambda i: (0, 0)), row(D_MODEL),
                  pl.BlockSpec((D_MODEL, D_MODEL), lambda i: (0, 0))],
        out_specs=[row(D_MODEL), row(D_MODEL), row(D_MODEL), pl.BlockSpec((1, D_MODEL), lambda i: (0, 0))],
        out_shape=[jax.ShapeDtypeStruct((T, D_MODEL), F32), jax.ShapeDtypeStruct((T, D_MODEL), BF16),
                   jax.ShapeDtypeStruct((T, D_MODEL), F32), jax.ShapeDtypeStruct((1, D_MODEL), F32)],
        name="ffn_bwd_in", compiler_params=_params("arbitrary"),
    )(dgu, w_gu, h1, g2, dh2, w_out)


def _rg_bwd(p, hs, dy, dp, cw, cb, wg, bg, lam, rg_g):
    T = p.shape[0]
    tm = _row_tile(T, 832)
    nt = T // tm
    hb = tm // 8
    unroll = _scan_unroll(hb)

    def body(xg_ref, xh_ref, h_ref, hh_ref, dy_ref, dp_in_ref, cw_ref, cb_ref, w_ref, bg_ref, lam_ref, g_ref,
             dp_ref, gcw_ref, gcb_ref, gw_ref, gbg_ref, glam_ref, gg_ref,
             ext, dext, a_s, b_s, d_s, gacc, carry_d, carry_a):
        i = pl.program_id(0)
        t_idx = nt - 1 - i

        @pl.when(i == 0)
        def _():
            dext[tm:tm + 8, :] = jnp.zeros((8, D_RG), F32)
            carry_d[...] = jnp.zeros_like(carry_d)
            carry_a[...] = jnp.zeros_like(carry_a)
            gacc[...] = jnp.zeros_like(gacc)
            for ref in (gcw_ref, gcb_ref, gbg_ref, glam_ref, gg_ref, gw_ref):
                ref[...] = jnp.zeros_like(ref)

        first = t_idx == 0
        ext[0:8, :] = jnp.where(first, 0.0, xh_ref[:, :D_RG])
        ext[8:8 + tm, :] = xg_ref[:, :D_RG]
        xc = _conv(ext, cw_ref, cb_ref, tm)
        lam_ = lam_ref[...]
        r, ig, sp, a, m, inv_m = _rg_gates(xc, w_ref, bg_ref, lam_)
        row = t_idx * tm + lax.broadcasted_iota(jnp.int32, (tm, 1), 0)
        valid = row >= PAD

        gr = xg_ref[:, D_RG:]
        g, dgelu = _gelu_parts(gr)
        h = h_ref[...]
        yy = g * h
        rr = _rms(yy)
        nn = yy * rr
        dy_ = dy_ref[...]
        gg_ref[...] += jnp.sum(dy_ * nn, axis=0, keepdims=True)
        dyy = _rms_bwd(dy_ * g_ref[...], nn, rr)
        dp_ref[:, D_RG:] = (dyy * h * dgelu).astype(BF16)

        a_s[...] = a
        b_s[...] = dyy * g
        rowi = lax.broadcasted_iota(jnp.int32, (8, D_RG), 0)

        def blk(jj, c):
            cd, ca = c
            for u in range(unroll):
                o = pl.multiple_of((hb - 1 - (jj * unroll + u)) * 8, 8)
                a_blk = a_s[pl.ds(o, 8), :]
                a_next = jnp.where(rowi == 7, ca, pltpu.roll(a_blk, 7, axis=0))
                A, B = _scan_block_bwd(a_next, b_s[pl.ds(o, 8), :], rowi)
                d = B + A * cd
                d_s[pl.ds(o, 8), :] = d
                cd, ca = d[0:1, :], a_blk[0:1, :]
            return cd, ca

        cd, ca = lax.fori_loop(0, hb // unroll, blk, (carry_d[...], carry_a[...]))
        carry_d[...] = cd
        carry_a[...] = ca
        delta = d_s[...]

        h_last_prev = jnp.where(first, 0.0, hh_ref[7:8, :])
        row0 = lax.broadcasted_iota(jnp.int32, (tm, 1), 0) == 0
        h_prev = jnp.where(row0, h_last_prev, pltpu.roll(h, 1, axis=0))
        dbx = jnp.where(valid, delta, 0.0)
        da = delta * h_prev
        di = dbx * m * xc
        dm = dbx * ig * xc
        dla = a * (da - dm * a * inv_m)
        dla = jnp.where(valid, dla, 0.0)
        glam_ref[...] += jnp.sum(dla * r, axis=0, keepdims=True) * (LRU_C / (1.0 + jnp.exp(lam_)))
        dr = (-LRU_C) * sp * dla
        dpre = jnp.concatenate([dr * r * (1.0 - r), di * ig * (1.0 - ig)], axis=1)
        gbg_ref[...] += jnp.sum(dpre, axis=0, keepdims=True)
        dpre_b = dpre.astype(BF16)
        gacc[...] += _dot_tn(xc.astype(BF16), dpre_b)
        dxc = dbx * m * ig + _dot_nt(dpre_b, w_ref[...])
        gcb_ref[...] += jnp.sum(dxc, axis=0, keepdims=True)
        for j in range(CONV_W):
            gcw_ref[j:j + 1, :] += jnp.sum(dxc * ext[8 - 3 + j:8 - 3 + j + tm, :], axis=0, keepdims=True)
        dext[0:tm, :] = dxc
        dxr = cw_ref[0:1, :] * dext[3:3 + tm, :]
        for j in range(1, CONV_W):
            dxr = dxr + cw_ref[j:j + 1, :] * dext[3 - j:3 - j + tm, :]
        dp_ref[:, :D_RG] = dxr.astype(BF16)
        dext[tm:tm + 8, :] = dext[0:8, :]

        @pl.when(i == nt - 1)
        def _():
            fold = _head_fold()
            mask = _head_mask()
            for k in range(2):
                blockdiag = jnp.where(mask, gacc[:, k * D_RG:(k + 1) * D_RG], 0.0)
                gw_ref[k * D_RG:(k + 1) * D_RG, :] = jnp.dot(blockdiag, fold, precision=HIGHEST,
                                                             preferred_element_type=F32)

    vec = lambda n: pl.BlockSpec((1, n), lambda i: (0, 0))
    rev = lambda n: pl.BlockSpec((tm, n), lambda i: (nt - 1 - i, 0))
    halo = lambda n: pl.BlockSpec((8, n), lambda i: (jnp.maximum((nt - 1 - i) * hb - 1, 0), 0))
    return pl.pallas_call(
        body, grid=(nt,),
        in_specs=[rev(2 * D_RG), halo(2 * D_RG), rev(D_RG), halo(D_RG), rev(D_RG), ANY,
                  pl.BlockSpec((CONV_W, D_RG), lambda i: (0, 0)), vec(D_RG),
                  pl.BlockSpec((D_RG, 2 * D_RG), lambda i: (0, 0)), vec(2 * D_RG), vec(D_RG), vec(D_RG)],
        out_specs=[rev(2 * D_RG), pl.BlockSpec((CONV_W, D_RG), lambda i: (0, 0)), vec(D_RG),
                   pl.BlockSpec((2 * D_RG, RG_HEAD_DIM), lambda i: (0, 0)), vec(2 * D_RG), vec(D_RG), vec(D_RG)],
        input_output_aliases={5: 0},
        out_shape=[jax.ShapeDtypeStruct((T, D_IN), BF16), jax.ShapeDtypeStruct((CONV_W, D_RG), F32),
                   jax.ShapeDtypeStruct((1, D_RG), F32), jax.ShapeDtypeStruct((2 * D_RG, RG_HEAD_DIM), F32),
                   jax.ShapeDtypeStruct((1, 2 * D_RG), F32), jax.ShapeDtypeStruct((1, D_RG), F32),
                   jax.ShapeDtypeStruct((1, D_RG), F32)],
        scratch_shapes=[pltpu.VMEM((tm + 8, D_RG), F32), pltpu.VMEM((tm + 8, D_RG), F32),
                        pltpu.VMEM((tm, D_RG), F32), pltpu.VMEM((tm, D_RG), F32), pltpu.VMEM((tm, D_RG), F32),
                        pltpu.VMEM((D_RG, 2 * D_RG), F32), pltpu.VMEM((1, D_RG), F32), pltpu.VMEM((1, D_RG), F32)],
        name="rg_bwd", compiler_params=_params("arbitrary"),
    )(p, p, hs, hs, dy, dp, cw, cb, wg, bg, lam, rg_g)


def _hg_bwd(p, o_all, st_all, dy, lbraw, hg_g):
    T = p.shape[0]
    n_chunks = T // CHUNK
    cps = _chunks_per_step(n_chunks)
    rows = cps * CHUNK
    n_steps = n_chunks // cps

    def body(hq_ref, hf_ref, hi_ref, hg_ref, o_ref, st_ref, dy_ref, lb_ref, g_ref,
             dp_ref, glb_ref, gg_ref, dst):
        i = pl.program_id(0)

        @pl.when(i == 0)
        def _():
            dst[...] = jnp.zeros_like(dst)
            glb_ref[...] = jnp.zeros_like(glb_ref)
            gg_ref[...] = jnp.zeros_like(gg_ref)

        dp_ref[:, :2 * D_RG] = jnp.zeros((rows, 2 * D_RG), BF16)

        def chunk(jj, carry):
            j = cps - 1 - jj
            rs = pl.ds(pl.multiple_of(j * CHUNK, CHUNK), CHUNK)
            chunk_body((n_steps - 1 - i) * cps + j, hq_ref.at[rs, :], hf_ref.at[rs, :], hi_ref.at[rs, :],
                       hg_ref.at[rs, :], o_ref.at[rs, :], st_ref.at[pl.ds(j, 1)], dy_ref.at[rs, :], lb_ref, g_ref,
                       dp_ref.at[rs, pl.ds(2 * D_RG, 4 * D_HG)], glb_ref, gg_ref, dst)
            return carry

        lax.fori_loop(0, cps, chunk, 0, unroll=True)

    def chunk_body(n, hq_ref, hf_ref, hi_ref, hg_ref, o_ref, st_ref, dy_ref, lb_ref, g_ref,
                   dp_ref, glb_ref, gg_ref, dst):
        valid = (n * CHUNK + lax.broadcasted_iota(jnp.int32, (CHUNK, 1), 0)) >= PAD
        hq, hf, v, hg = hq_ref[...], hf_ref[...], hi_ref[...], hg_ref[...]
        lb, sq, q, sf, f, b = _hg_gates(hq, hf, lb_ref, valid)
        k = 1.0 - f
        causal = _causal()
        r_i = lax.broadcasted_iota(jnp.int32, (CHUNK, CHUNK), 0)
        c_i = lax.broadcasted_iota(jnp.int32, (CHUNK, CHUNK), 1)
        causal_t = r_i <= c_i
        is_last = lax.broadcasted_iota(jnp.int32, (CHUNK, 1), 0) == CHUNK - 1
        g_ = g_ref[...]
        db_parts, dq_parts, dk_parts = [], [], []
        gg = jnp.zeros((1, HG_HEAD_DIM), F32)
        heads = [slice(h * HG_HEAD_DIM, (h + 1) * HG_HEAD_DIM) for h in range(HG_HEADS)]

        do_parts = []
        for h, sl in enumerate(heads):
            o = o_ref[:, sl]
            ro = _rms(o)
            no = o * ro
            hgh = hg[:, sl]
            sg = _sigmoid(hgh)
            dyh = dy_ref[:, sl]
            dp_ref[:, 3 * D_HG + h * HG_HEAD_DIM:3 * D_HG + (h + 1) * HG_HEAD_DIM] = (
                dyh * no * g_ * sg * (1.0 + hgh * (1.0 - sg))).astype(BF16)
            dng = dyh * hgh * sg
            gg = gg + jnp.sum(dng * no, axis=0, keepdims=True)
            do_parts.append(_rms_bwd(dng * g_, no, ro))
        do_t = jnp.concatenate(do_parts, axis=1).T.astype(BF16)

        fac = []
        for sl, do in zip(heads, do_parts):
            qh, kh, bh = q[:, sl], k[:, sl], b[:, sl]
            blk, b_last, eb, eq, ekh, ek, q_hat, k_til = _hg_head(qh, kh, bh)
            fac.append(dict(qh=qh, kh=kh, blk=blk, e_last=jnp.exp(b_last), eb=eb, eq=eq, ekh=ekh, ek=ek,
                            q_til=qh * eb, k_hat=kh * ekh, qhb=q_hat.astype(BF16), ktb=k_til.astype(BF16),
                            vb=v[:, sl].astype(BF16), dob=do.astype(BF16)))

        first = []
        for sl, t in zip(heads, fac):
            st_h = st_ref[0, sl, :]
            dst_h = dst[sl, :]
            dstb = dst_h.astype(BF16)
            first.append(dict(
                att_t=_dot_nt(t["ktb"], t["qhb"]), datt=_dot_nt(t["dob"], t["vb"]),
                datt_t=_dot_nt(t["vb"], t["dob"]), dk_hat=_dot(t["vb"], dstb),
                dv=_dot_nt(t["k_hat"].astype(BF16), dstb), dq_til=_dot(t["dob"], st_h.astype(BF16)),
                state=t["e_last"] * jnp.sum(dst_h * st_h, axis=0, keepdims=True)))
            dst[sl, :] = dst_h * t["e_last"] + _dot(do_t[sl, :], t["q_til"].astype(BF16))

        for h, (t, m) in enumerate(zip(fac, first)):
            qh, kh, blk, eb, eq, ekh, ek = t["qh"], t["kh"], t["blk"], t["eb"], t["eq"], t["ekh"], t["ek"]
            q_til, k_hat, qhb, ktb, dob = t["q_til"], t["k_hat"], t["qhb"], t["ktb"], t["dob"]
            dk_hat, dq_til = m["dk_hat"], m["dq_til"]
            dv = m["dv"] + _dot(jnp.where(causal_t, m["att_t"], 0.0).astype(BF16), dob)
            dq_hat = _dot(jnp.where(causal, m["datt"], 0.0).astype(BF16), ktb)
            dk_til = _dot(jnp.where(causal_t, m["datt_t"], 0.0).astype(BF16), qhb)
            db_last = jnp.sum(dk_hat * k_hat, axis=0, keepdims=True) + m["state"]
            dq_sel = dq_hat[:, (N_SUB - 1) * HG_HEAD_DIM:]
            for s in range(N_SUB - 2, -1, -1):
                dq_sel = jnp.where(blk == s, dq_hat[:, s * HG_HEAD_DIM:(s + 1) * HG_HEAD_DIM], dq_sel)
            dq_a = dq_sel * eq
            dk_a = dk_til[:, :HG_HEAD_DIM] * ek[0]
            for s in range(1, N_SUB):
                dk_a = dk_a + dk_til[:, s * HG_HEAD_DIM:(s + 1) * HG_HEAD_DIM] * ek[s]
            db_att = qhb.astype(F32) * dq_hat - ktb.astype(F32) * dk_til
            db = dq_til * q_til - dk_hat * k_hat
            for s in range(N_SUB):
                db = db + db_att[:, s * HG_HEAD_DIM:(s + 1) * HG_HEAD_DIM]
            db_parts.append(jnp.where(is_last, db + db_last, db))
            dq_parts.append(dq_til * eb + dq_a)
            dk_parts.append(dk_hat * ekh + dk_a)
            dp_ref[:, 2 * D_HG + h * HG_HEAD_DIM:2 * D_HG + (h + 1) * HG_HEAD_DIM] = dv.astype(BF16)

        gg_ref[...] += gg
        db = jnp.concatenate(db_parts, axis=1)
        dq = jnp.concatenate(dq_parts, axis=1)
        dk = jnp.concatenate(dk_parts, axis=1)
        dlf = jnp.where(valid, jnp.dot(_tri(False), db, precision=HIGHEST, preferred_element_type=F32), 0.0)
        dp_ref[:, :D_HG] = (dq * sq * (1.0 + hq * (1.0 - sq))).astype(BF16)
        df = dlf / f - dk
        dlb = jnp.sum(df * (1.0 - sf), axis=0, keepdims=True) * lb * (1.0 - lb)
        glb_ref[0:1, :] += dlb
        glb_ref[1:2, :] += -dlb
        dp_ref[:, D_HG:2 * D_HG] = (df * (1.0 - lb) * sf * (1.0 - sf)).astype(BF16)

    rev = lambda j: pl.BlockSpec((rows, D_HG), lambda i: (n_steps - 1 - i, j))
    return pl.pallas_call(
        body, grid=(n_steps,),
        in_specs=[rev(2), rev(3), rev(4), rev(5), rev(0),
                  pl.BlockSpec((cps, D_HG, HG_HEAD_DIM), lambda i: (n_steps - 1 - i, 0, 0)), rev(1),
                  pl.BlockSpec((2, D_HG), lambda i: (0, 0)), pl.BlockSpec((1, HG_HEAD_DIM), lambda i: (0, 0))],
        out_specs=[pl.BlockSpec((rows, D_IN), lambda i: (n_steps - 1 - i, 0)),
                   pl.BlockSpec((2, D_HG), lambda i: (0, 0)), pl.BlockSpec((1, HG_HEAD_DIM), lambda i: (0, 0))],
        out_shape=[jax.ShapeDtypeStruct((T, D_IN), BF16), jax.ShapeDtypeStruct((2, D_HG), F32),
                   jax.ShapeDtypeStruct((1, HG_HEAD_DIM), F32)],
        scratch_shapes=[pltpu.VMEM((D_HG, HG_HEAD_DIM), F32)],
        name="hg_bwd", compiler_params=_params("arbitrary"),
    )(p, p, p, p, o_all, st_all, dy, lbraw, hg_g)


def _in_bwd(dp, w_in, h0, g1, dh1):
    T = h0.shape[0]
    tm = _row_tile(T, 416)
    n_steps = T // tm

    def body(dp_ref, w_ref, h_ref, g_ref, d1_ref, gx_hbm, gmeta_ref, gg_ref, buf, sems):
        i = pl.program_id(0)
        first, later = _window_copies(gx_hbm, buf, sems, tm)
        slot = i % 2

        @pl.when(i == 0)
        def _():
            gg_ref[...] = jnp.zeros_like(gg_ref)

        if n_steps > 2:
            @pl.when(i == 2)
            def _():
                first(False).wait()

            @pl.when(i > 2)
            def _():
                later(i - 2, slot, False).wait()

        du = _dot_nt(dp_ref[...], w_ref[...])
        h0_ = h_ref[...]
        r = _rms(h0_)
        n = h0_ * r
        gg_ref[...] += jnp.sum(du * n, axis=0, keepdims=True)
        dh0 = d1_ref[...] + _rms_bwd(du * g_ref[...], n, r)
        buf[slot] = dh0

        @pl.when(i == 0)
        def _():
            gmeta_ref[...] = dh0[PAD:HEAD, :]
            first(False).start()

        if n_steps > 1:
            @pl.when(i > 0)
            def _():
                later(i, slot, False).start()

        @pl.when(i == n_steps - 1)
        def _():
            if n_steps == 1:
                first(False).wait()
            else:
                if n_steps == 2:
                    first(False).wait()
                else:
                    later(i - 1, 1 - slot, False).wait()
                later(i, slot, False).wait()

    row = lambda n: pl.BlockSpec((tm, n), lambda i: (i, 0))
    return pl.pallas_call(
        body, grid=(n_steps,),
        in_specs=[row(D_IN), pl.BlockSpec((D_MODEL, D_IN), lambda i: (0, 0)),
                  row(D_MODEL), pl.BlockSpec((1, D_MODEL), lambda i: (0, 0)), row(D_MODEL)],
        out_specs=[pl.BlockSpec(memory_space=pl.ANY), pl.BlockSpec((N_META, D_MODEL), lambda i: (0, 0)),
                   pl.BlockSpec((1, D_MODEL), lambda i: (0, 0))],
        out_shape=[jax.ShapeDtypeStruct((T - HEAD, D_MODEL), F32), jax.ShapeDtypeStruct((N_META, D_MODEL), F32),
                   jax.ShapeDtypeStruct((1, D_MODEL), F32)],
        scratch_shapes=[pltpu.VMEM((2, tm, D_MODEL), F32), pltpu.SemaphoreType.DMA((2,))],
        name="in_bwd", compiler_params=_params("arbitrary"),
    )(dp, w_in, h0, g1, dh1)


def _col_tile(cols, target):
    best = None
    for t in range(128, min(cols, target) + 1, 128):
        if cols % t == 0:
            best = t
    assert best is not None, cols
    return best


MXU_DIM = 256


def _mxu_tile(cols, target):
    best = None
    for t in range(MXU_DIM, min(cols, target) + 1, MXU_DIM):
        if cols % t == 0:
            best = t
    assert best is not None, cols
    return best


def _weight_grad(a, b, name):
    T, M = a.shape
    N = b.shape[1]
    tm = _col_tile(M, 1408)
    tn = _mxu_tile(N, 768 if tm <= 1024 else 512)

    def body(a_ref, b_ref, o_ref):
        o_ref[...] = _dot_tn(a_ref[...], b_ref[...])

    return pl.pallas_call(
        body, grid=(M // tm, N // tn),
        in_specs=[pl.BlockSpec((T, tm), lambda m, n: (0, m)), pl.BlockSpec((T, tn), lambda m, n: (0, n))],
        out_specs=pl.BlockSpec((tm, tn), lambda m, n: (m, n)),
        out_shape=jax.ShapeDtypeStruct((M, N), F32),
        name=name, compiler_params=_params("parallel", "parallel"),
    )(a, b)


def _local_step(x, meta, target, w_in, w_out, w_gu, w_down, small, on_ffn_grads=None, on_mixer_grads=None):
    wg = _gate_weights(small["w_rgate"], small["w_igate"])
    bg = jnp.concatenate([small["b_rgate"], small["b_igate"]], axis=1)

    p, u, h0 = _in_proj(x, meta, small["mix_norm_g"], w_in)
    y_rg, hs = _rg_fwd(p, small["conv_w"], small["conv_b"], wg, bg, small["lru_lambda"], small["rg_norm_g"])
    y_hg, o_all, st_all = _hg_fwd(p, small["hg_lower_bound"], small["hg_norm_g"])
    h1, v, yb = _out_proj(h0, y_rg, y_hg, w_out, small["ffn_norm_g"])
    gu, act = _gate_up(v, w_gu)
    dh2, dh2b, loss, g_final = _down_loss(h1, act, w_down, small["final_norm_g"], target)

    dgu = _ffn_bwd_act(dh2b, gu, w_down)
    ffn_grads = {"w_gate_up": _weight_grad(v, dgu, "grad_w_gate_up"),
                 "w_down": _weight_grad(act, dh2b, "grad_w_down")}
    stages = on_ffn_grads(ffn_grads) if on_ffn_grads is not None else None
    dh1, dh1b, dy, g_ffn = _ffn_bwd_in(dgu, w_gu, h1, small["ffn_norm_g"], dh2, w_out)
    early = late = None
    if stages is not None:
        chip_sums, send = stages
        sums = chip_sums()
        (dh1, dh1b, dy), sums = lax.optimization_barrier(((dh1, dh1b, dy), sums))
        early = send(sums)
    dp, g_lb, g_hgn = _hg_bwd(p, o_all, st_all, dy, small["hg_lower_bound"], small["hg_norm_g"])
    dp, g_cw, g_cb, g_wgate, g_bg, g_lam, g_rgn = _rg_bwd(
        p, hs, dy, dp, small["conv_w"], small["conv_b"], wg, bg, small["lru_lambda"], small["rg_norm_g"])
    mixer_grads = {"w_in": _weight_grad(u, dp, "grad_w_in"), "w_out": _weight_grad(yb, dh1b, "grad_w_out")}
    if on_mixer_grads is not None:
        chip_sums, send = on_mixer_grads(mixer_grads)
        sums = chip_sums()
        (dp, dh1), sums = lax.optimization_barrier(((dp, dh1), sums))
        late = send(sums)
    grad_x, g_meta, g_mix = _in_bwd(dp, w_in, h0, small["mix_norm_g"], dh1)

    grads = {
        "w_in": mixer_grads["w_in"], "w_out": mixer_grads["w_out"],
        "w_gate_up": ffn_grads["w_gate_up"], "w_down": ffn_grads["w_down"],
        "meta_tokens": g_meta, "mix_norm_g": g_mix, "conv_w": g_cw, "conv_b": g_cb, "w_gates": g_wgate,
        "b_rgate": g_bg[:, :D_RG], "b_igate": g_bg[:, D_RG:], "lru_lambda": g_lam, "rg_norm_g": g_rgn,
        "hg_lower_bound": g_lb, "hg_norm_g": g_hgn, "ffn_norm_g": g_ffn, "final_norm_g": g_final,
    }
    return loss, grad_x, grads, early, late


ANY = pl.BlockSpec(memory_space=pl.ANY)
HALF = D_MODEL // 2

BIG = {"w_in": (D_MODEL, D_IN // N_CHIPS, True), "w_gate_up": (D_MODEL, 2 * D_FF // N_CHIPS, True),
       "w_out": (D_MODEL // N_CHIPS, D_MODEL, False), "w_down": (D_FF // N_CHIPS, D_MODEL, False)}
BIG_NAMES = tuple(BIG)
N_BIG = len(BIG_NAMES)


def _full_shape(name):
    rows, cols, by_col = BIG[name]
    return (rows, cols * N_CHIPS) if by_col else (rows * N_CHIPS, cols)


def _place():
    return lax.axis_index("x"), lax.axis_index("y"), lax.axis_index("c")


def _chip_of(x, y, r):
    fx, fy = (r + 1) >> 1, (r + 1) & 1
    return (1 - x if fx else x), (1 - y if fy else y)


def _half_of(ref, by_col, half):
    start = pl.multiple_of(half * HALF, 128)
    return ref.at[pl.ds(start, HALF), :] if by_col else ref.at[:, pl.ds(start, HALF)]


def _shard_of(ref, name, chip):
    rows, cols, by_col = BIG[name]
    if by_col:
        return ref.at[:, pl.ds(pl.multiple_of(chip * cols, 128), cols)]
    return ref.at[pl.ds(pl.multiple_of(chip * rows, 16), rows), :]


def _shard_half_of(ref, name, chip, half):
    rows, cols, by_col = BIG[name]
    start = pl.multiple_of(half * HALF, 128)
    if by_col:
        return ref.at[pl.ds(start, HALF), pl.ds(pl.multiple_of(chip * cols, 128), cols)]
    return ref.at[pl.ds(pl.multiple_of(chip * rows, 16), rows), pl.ds(start, HALF)]


def _remote(src, dst, send_sems, recv_sems, k, dev):
    return pltpu.make_async_remote_copy(src_ref=src, dst_ref=dst, send_sem=send_sems.at[k], recv_sem=recv_sems.at[k],
                                        device_id=dev, device_id_type=MESH)


def _cast_into_full(w, chip):
    steps = 4
    in_specs, out_specs = [], []
    for name in BIG_NAMES:
        rows, cols, by_col = BIG[name]
        tr = rows // steps
        in_specs.append(pl.BlockSpec((tr, cols), lambda i, s: (i, 0)))
        if by_col:
            out_specs.append(pl.BlockSpec((tr, cols), lambda i, s: (i, s[0])))
        else:
            out_specs.append(pl.BlockSpec((tr, cols), lambda i, s: (s[0] * steps + i, 0)))

    def body(s_ref, *refs):
        for a in range(N_BIG):
            refs[N_BIG + a][...] = refs[a][...].astype(BF16)

    placed = pl.pallas_call(
        body,
        grid_spec=pltpu.PrefetchScalarGridSpec(num_scalar_prefetch=1, grid=(steps,), in_specs=in_specs,
                                               out_specs=out_specs),
        out_shape=[jax.ShapeDtypeStruct(_full_shape(name), BF16) for name in BIG_NAMES],
        name="place_shards", compiler_params=_params("parallel"),
    )(chip, *[w[name] for name in BIG_NAMES])
    return dict(zip(BIG_NAMES, placed))


def _gather_weights(placed, small, names, label, collective_id):
    n, ns = len(names), len(small)
    hbm = pltpu.MemorySpace.HBM
    outs = [jax.new_ref(placed[nm], memory_space=hbm) for nm in names]
    small_in = [jax.new_ref(s, memory_space=hbm) for s in small]
    small_out = [jax.empty_ref(jax.ShapeDtypeStruct((s.shape[0], s.shape[1] * N_CHIPS), F32), memory_space=hbm)
                 for s in small]
    n_sems = 6 * n + 3 * ns

    @pl.kernel(mesh=plsc.ScalarSubcoreMesh(axis_name="seq", num_cores=1), name=label, out_type=(),
               scratch_types=(pltpu.SemaphoreType.DMA((n_sems,)), pltpu.SemaphoreType.DMA((n_sems,)),
                              pltpu.SemaphoreType.DMA((max(ns, 1),))),
               compiler_params=pltpu.CompilerParams(collective_id=collective_id))
    def launch(send_sems, recv_sems, local_sems):
        x, y, c = _place()
        chip = 2 * x + y
        sibling = (x, y, 1 - c)
        others = [_chip_of(x, y, r) for r in range(3)]
        _handshake([(qx, qy, c) for qx, qy in others] + [sibling])

        def small_block(a, q):
            cols = small[a].shape[1]
            return small_out[a].at[:, pl.ds(pl.multiple_of(q * cols, 128), cols)]

        local = [pltpu.make_async_copy(small_in[a], small_block(a, chip), local_sems.at[a]) for a in range(ns)]
        for cp in local:
            cp.start()

        sends = []
        for a, name in enumerate(names):
            mine = _shard_half_of(outs[a], name, chip, c)
            for r, (qx, qy) in enumerate(others):
                sends.append(_remote(mine, mine, send_sems, recv_sems, 6 * a + r, (qx, qy, c)))
        for a in range(ns):
            for r, (qx, qy) in enumerate(others):
                sends.append(_remote(small_in[a], small_block(a, chip), send_sems, recv_sems,
                                     6 * n + 3 * a + r, (qx, qy, c)))
        for cp in sends:
            cp.start()

        forwards = []
        for a, name in enumerate(names):
            for r, (qx, qy) in enumerate(others):
                landed = _shard_half_of(outs[a], name, 2 * qx + qy, c)
                _remote(landed, landed, send_sems, recv_sems, 6 * a + r, (qx, qy, c)).wait_recv()
                fwd = _remote(landed, landed, send_sems, recv_sems, 6 * a + 3 + r, sibling)
                fwd.start()
                forwards.append(fwd)
        for a in range(ns):
            for r, (qx, qy) in enumerate(others):
                landed = small_block(a, 2 * qx + qy)
                _remote(landed, landed, send_sems, recv_sems, 6 * n + 3 * a + r, (qx, qy, c)).wait_recv()
        for a, name in enumerate(names):
            for r, (qx, qy) in enumerate(others):
                landed = _shard_half_of(outs[a], name, 2 * qx + qy, 1 - c)
                _remote(landed, landed, send_sems, recv_sems, 6 * a + 3 + r, sibling).wait_recv()
        for cp in sends + forwards:
            cp.wait_send()
        for cp in local:
            cp.wait()

    launch()
    return {nm: ref[...] for nm, ref in zip(names, outs)}, [ref[...] for ref in small_out]


def _exchange_halves(grads, names, label, collective_id):
    n = len(names)
    sequencer = collective_id is not None

    def body(*refs):
        ins, outs = refs[:n], refs[n:2 * n]
        send_sems, recv_sems = refs[2 * n:]
        x, y, c = _place()
        if sequencer:
            _handshake([(x, y, 1 - c)])
        copies = []
        for a, name in enumerate(names):
            copies.append(_remote(_half_of(ins[a], BIG[name][2], 1 - c), outs[a], send_sems, recv_sems, a,
                                  (x, y, 1 - c)))
        for cp in copies:
            cp.start()
        for cp in copies:
            cp.wait()

    def half_shape(name):
        r, c_ = _full_shape(name)
        return (HALF, c_) if BIG[name][2] else (r, HALF)

    out_type = tuple(jax.ShapeDtypeStruct(half_shape(nm), F32) for nm in names)
    sems = (pltpu.SemaphoreType.DMA((n,)), pltpu.SemaphoreType.DMA((n,)))
    operands = [grads[nm] for nm in names]
    if sequencer:
        got = pl.kernel(
            body, mesh=plsc.ScalarSubcoreMesh(axis_name="seq", num_cores=1), name=label, out_type=out_type,
            scratch_types=sems, compiler_params=pltpu.CompilerParams(collective_id=collective_id),
        )(*operands)
    else:
        got = pl.pallas_call(
            body, in_specs=[ANY] * n, out_specs=[ANY] * n, out_shape=list(out_type), scratch_shapes=list(sems),
            name=label,
        )(*operands)
    return dict(zip(names, got))


def _chip_sum(grads, got, names, core, label):
    n = len(names)
    steps = 4
    g_specs, blks = [], []
    for name in names:
        rows, cols = got[name].shape
        tr = rows // steps
        if BIG[name][2]:
            g_specs.append(pl.BlockSpec((tr, cols), lambda i, s: (s[0] * steps + i, 0)))
        else:
            g_specs.append(pl.BlockSpec((tr, HALF), lambda i, s: (i, s[0])))
        blks.append(pl.BlockSpec((tr, cols), lambda i, s: (i, 0)))

    def body(s_ref, *refs):
        for a in range(n):
            t = refs[a][...] + refs[n + a][...]
            refs[2 * n + a][...] = t
            refs[3 * n + a][...] = t.astype(BF16)

    out = pl.pallas_call(
        body,
        grid_spec=pltpu.PrefetchScalarGridSpec(num_scalar_prefetch=1, grid=(steps,), in_specs=g_specs + blks,
                                               out_specs=blks + blks),
        out_shape=([jax.ShapeDtypeStruct(got[nm].shape, F32) for nm in names]
                   + [jax.ShapeDtypeStruct(got[nm].shape, BF16) for nm in names]),
        name=label, compiler_params=_params("parallel"),
    )(core, *[grads[nm] for nm in names], *[got[nm] for nm in names])
    return {nm: (out[a], out[n + a]) for a, nm in enumerate(names)}


def _piece_shape(name):
    rows, cols, by_col = BIG[name]
    return (HALF, cols) if by_col else (rows, HALF)


def _handshake(peers):
    barrier = pltpu.get_barrier_semaphore()
    for peer in peers:
        pl.semaphore_signal(barrier, inc=1, device_id=peer, device_id_type=MESH)
    pl.semaphore_wait(barrier, len(peers))


def _send_chip_sums(sums, names, label, collective_id):
    n = len(names)

    def body(*refs):
        ins, outs = refs[:n], refs[n:2 * n]
        send_sems, recv_sems = refs[2 * n:]
        x, y, c = _place()
        others = [_chip_of(x, y, r) for r in range(3)]
        _handshake([(qx, qy, c) for qx, qy in others])
        copies = []
        for a, name in enumerate(names):
            for r, (qx, qy) in enumerate(others):
                copies.append(_remote(_shard_of(ins[a], name, 2 * qx + qy), outs[a].at[r], send_sems, recv_sems,
                                      3 * a + r, (qx, qy, c)))
        for cp in copies:
            cp.start()
        for cp in copies:
            cp.wait()

    return pl.kernel(
        body, mesh=plsc.ScalarSubcoreMesh(axis_name="seq", num_cores=1), name=label,
        out_type=tuple(jax.ShapeDtypeStruct((3,) + _piece_shape(nm), BF16) for nm in names),
        scratch_types=(pltpu.SemaphoreType.DMA((3 * n,)), pltpu.SemaphoreType.DMA((3 * n,))),
        compiler_params=pltpu.CompilerParams(collective_id=collective_id),
    )(*[sums[nm] for nm in names])


def _total(parts, chip_core):
    steps = 2
    in_specs, out_specs, operands = [], [], []
    for name in BIG_NAMES:
        by_col = BIG[name][2]
        pr, pc = _piece_shape(name)
        tr = pr // steps
        if by_col:
            in_specs.append(pl.BlockSpec((tr, pc), lambda i, s: (i, s[0])))
            out_specs.append(pl.BlockSpec((tr, pc), lambda i, s: (s[1] * steps + i, 0)))
        else:
            in_specs.append(pl.BlockSpec((tr, pc), lambda i, s: (s[0] * steps + i, 0)))
            out_specs.append(pl.BlockSpec((tr, pc), lambda i, s: (i, s[1])))
        for r in range(3):
            in_specs.append(pl.BlockSpec((None, tr, pc), lambda i, s, r=r: (r, i, 0)))
        own, got = parts[name]
        operands += [own, got, got, got]

    def body(s_ref, *refs):
        for a in range(N_BIG):
            o_ref, a_ref, b_ref, c_ref = refs[4 * a:4 * a + 4]
            refs[4 * N_BIG + a][...] = (((o_ref[...] + a_ref[...].astype(F32)) + b_ref[...].astype(F32))
                                        + c_ref[...].astype(F32))

    totals = pl.pallas_call(
        body,
        grid_spec=pltpu.PrefetchScalarGridSpec(num_scalar_prefetch=1, grid=(steps,), in_specs=in_specs,
                                               out_specs=out_specs),
        out_shape=[jax.ShapeDtypeStruct(BIG[name][:2], F32) for name in BIG_NAMES],
        name="totals", compiler_params=_params("parallel"),
    )(chip_core, *operands)
    return dict(zip(BIG_NAMES, totals))


def _share_totals(totals):
    def body(*refs):
        outs = refs[N_BIG:2 * N_BIG]
        send_sems, recv_sems = refs[2 * N_BIG:]
        x, y, c = _place()
        copies = []
        for a, name in enumerate(BIG_NAMES):
            mine = _half_of(outs[a], BIG[name][2], c)
            copies.append(_remote(mine, mine, send_sems, recv_sems, a, (x, y, 1 - c)))
        for cp in copies:
            cp.start()
        for a, name in enumerate(BIG_NAMES):
            theirs = _half_of(outs[a], BIG[name][2], 1 - c)
            _remote(theirs, theirs, send_sems, recv_sems, a, (x, y, 1 - c)).wait_recv()
        for cp in copies:
            cp.wait_send()

    return pl.pallas_call(
        body, in_specs=[ANY] * N_BIG, out_specs=[ANY] * N_BIG,
        out_shape=[jax.ShapeDtypeStruct(BIG[n][:2], F32) for n in BIG_NAMES],
        input_output_aliases={a: a for a in range(N_BIG)},
        scratch_shapes=[pltpu.SemaphoreType.DMA((N_BIG,)), pltpu.SemaphoreType.DMA((N_BIG,))],
        name="share_totals",
    )(*[totals[n] for n in BIG_NAMES])


VEC_ROWS = 32
VEC_ROW = {"mix_norm_g": 0, "conv_b": 1, "b_rgate": 2, "b_igate": 3, "lru_lambda": 4, "rg_norm_g": 5,
           "hg_lower_bound": 6, "hg_norm_g": 8, "ffn_norm_g": 9, "final_norm_g": 10, "loss": 11,
           "conv_w": 12, "meta_tokens": 16}
N_DEV = 8


def _all_reduce_small(pieces, gates):
    names = list(pieces)
    hv, hg = VEC_ROWS // 2, gates.shape[0] // 2

    def body(*refs):
        ins = refs[:len(names)]
        (g_ref, vec_ref, gsum_ref, mine_v, sib_v, sib_g, chip_v, chip_g, got_v, got_g,
         send_sems, recv_sems) = refs[len(names):]
        x, y, c = _place()
        chip = 2 * x + y
        sibling = (x, y, 1 - c)
        mine_v[...] = jnp.zeros_like(mine_v)
        for name, ref in zip(names, ins):
            nr, w = ref.shape
            mine_v[VEC_ROW[name]:VEC_ROW[name] + nr, 0:w] = ref[...]

        swap = [_remote(mine_v, sib_v, send_sems, recv_sems, 0, sibling),
                _remote(g_ref, sib_g, send_sems, recv_sems, 1, sibling)]
        for cp in swap:
            cp.start()
        for cp in swap:
            cp.wait()
        chip_v[...] = mine_v[...] + sib_v[...]
        chip_g[...] = g_ref[...] + sib_g[...]

        rows_v = pl.ds(pl.multiple_of(c * hv, 8), hv)
        rows_g = pl.ds(pl.multiple_of(c * hg, 8), hg)
        got_v[chip] = chip_v[rows_v, :]
        got_g[chip] = chip_g[rows_g, :]
        sends = []
        for r in range(3):
            qx, qy = _chip_of(x, y, r)
            sends.append(_remote(chip_v.at[rows_v, :], got_v.at[chip], send_sems, recv_sems, 2 + r, (qx, qy, c)))
            sends.append(_remote(chip_g.at[rows_g, :], got_g.at[chip], send_sems, recv_sems, 5 + r, (qx, qy, c)))
        for cp in sends:
            cp.start()
        for cp in sends:
            cp.wait()
        vec_ref[rows_v, :] = ((got_v[0] + got_v[1]) + got_v[2]) + got_v[3]
        gsum_ref[rows_g, :] = ((got_g[0] + got_g[1]) + got_g[2]) + got_g[3]

        back = [_remote(vec_ref.at[rows_v, :], vec_ref.at[rows_v, :], send_sems, recv_sems, 8, sibling),
                _remote(gsum_ref.at[rows_g, :], gsum_ref.at[rows_g, :], send_sems, recv_sems, 9, sibling)]
        for cp in back:
            cp.start()
        theirs_v = vec_ref.at[pl.ds(pl.multiple_of((1 - c) * hv, 8), hv), :]
        theirs_g = gsum_ref.at[pl.ds(pl.multiple_of((1 - c) * hg, 8), hg), :]
        _remote(theirs_v, theirs_v, send_sems, recv_sems, 8, sibling).wait_recv()
        _remote(theirs_g, theirs_g, send_sems, recv_sems, 9, sibling).wait_recv()
        for cp in back:
            cp.wait_send()

    vmem = pl.BlockSpec(memory_space=pltpu.VMEM)
    n_sems = 10
    return pl.pallas_call(
        body, in_specs=[vmem] * (len(names) + 1), out_specs=[vmem, vmem],
        out_shape=[jax.ShapeDtypeStruct((VEC_ROWS, D_MODEL), F32), jax.ShapeDtypeStruct(gates.shape, F32)],
        scratch_shapes=[pltpu.VMEM((VEC_ROWS, D_MODEL), F32), pltpu.VMEM((VEC_ROWS, D_MODEL), F32),
                        pltpu.VMEM(gates.shape, F32), pltpu.VMEM((VEC_ROWS, D_MODEL), F32),
                        pltpu.VMEM(gates.shape, F32), pltpu.VMEM((N_CHIPS, hv, D_MODEL), F32),
                        pltpu.VMEM((N_CHIPS, hg) + gates.shape[1:], F32),
                        pltpu.SemaphoreType.DMA((n_sems,)), pltpu.SemaphoreType.DMA((n_sems,))],
        name="all_reduce_small",
    )(*[pieces[n] for n in names], gates)


def _adamw_math(w, g, m, v):
    m = ADAM_B1 * m + (1.0 - ADAM_B1) * g
    v = ADAM_B2 * v + (1.0 - ADAM_B2) * (g * g)
    m_hat = m / (1.0 - ADAM_B1 ** ADAM_STEP)
    v_hat = v / (1.0 - ADAM_B2 ** ADAM_STEP)
    delta = -ADAM_LR * (m_hat / (jnp.sqrt(v_hat) + ADAM_EPS) + ADAM_WD * w)
    return delta, m, v


def _adamw_big(w, g, m, v):
    steps = 8
    blks = []
    for name in BIG_NAMES:
        rows, cols, _ = BIG[name]
        blks.append(pl.BlockSpec((rows // steps, cols), lambda i: (i, 0)))

    def body(*refs):
        ins, outs = refs[:4 * N_BIG], refs[4 * N_BIG:]
        for a in range(N_BIG):
            w_ref, g_ref, m_ref, v_ref = (ins[k * N_BIG + a] for k in range(4))
            d, nm, nv = _adamw_math(w_ref[...], g_ref[...], m_ref[...], v_ref[...])
            outs[a][...] = d
            outs[N_BIG + a][...] = nm
            outs[2 * N_BIG + a][...] = nv

    shapes = [jax.ShapeDtypeStruct(BIG[name][:2], F32) for name in BIG_NAMES]
    out = pl.pallas_call(
        body, grid=(steps,), in_specs=blks * 4, out_specs=blks * 3, out_shape=shapes * 3,
        name="adamw_big", compiler_params=_params("parallel"),
    )(*[t[name] for t in (w, g, m, v) for name in BIG_NAMES])
    return {name: (out[a], out[N_BIG + a], out[2 * N_BIG + a]) for a, name in enumerate(BIG_NAMES)}


SMALL = {"meta_tokens": (N_META, D_MODEL // N_CHIPS), "mix_norm_g": (1, D_MODEL), "conv_w": (CONV_W, D_RG // N_CHIPS),
         "conv_b": (1, D_RG), "w_rgate": (D_RG, RG_HEAD_DIM), "b_rgate": (1, D_RG), "w_igate": (D_RG, RG_HEAD_DIM),
         "b_igate": (1, D_RG), "lru_lambda": (1, D_RG), "rg_norm_g": (1, D_RG), "hg_lower_bound": (2, D_HG),
         "hg_norm_g": (1, HG_HEAD_DIM), "ffn_norm_g": (1, D_MODEL), "final_norm_g": (1, D_MODEL)}
SMALL_NAMES = tuple(SMALL)
SHARDED_SMALL = ("meta_tokens", "conv_w")


def _adamw_small(vec, gates, w, m, v):
    n = len(SMALL_NAMES)

    def body(*refs):
        vec_ref, gates_ref = refs[:2]
        w_refs, m_refs, v_refs = refs[2:2 + n], refs[2 + n:2 + 2 * n], refs[2 + 2 * n:2 + 3 * n]
        outs = refs[2 + 3 * n:]
        loss_ref = outs[0]
        x, y, _ = _place()
        chip = 2 * x + y
        loss_ref[...] = vec_ref[VEC_ROW["loss"]:VEC_ROW["loss"] + 1, 0:1]

        def update(k, g):
            g_ref, d_ref, nm_ref, nv_ref = outs[1 + 4 * k:5 + 4 * k]
            g_ref[...] = g
            d_ref[...], nm_ref[...], nv_ref[...] = _adamw_math(w_refs[k][...], g, m_refs[k][...], v_refs[k][...])

        for k, name in enumerate(SMALL_NAMES):
            nr, w_ = SMALL[name]
            if name == "w_rgate":
                update(k, gates_ref[0:D_RG, :])
            elif name == "w_igate":
                update(k, gates_ref[D_RG:2 * D_RG, :])
            elif name in SHARDED_SMALL:
                r0 = VEC_ROW[name]
                for q in range(N_CHIPS):
                    @pl.when(chip == q)
                    def _(k=k, r0=r0, nr=nr, w_=w_, q=q):
                        update(k, vec_ref[r0:r0 + nr, q * w_:(q + 1) * w_])
            else:
                r0 = VEC_ROW[name]
                update(k, vec_ref[r0:r0 + nr, 0:w_])

    vmem = pl.BlockSpec(memory_space=pltpu.VMEM)
    out_shape = [jax.ShapeDtypeStruct((1, 1), F32)]
    for name in SMALL_NAMES:
        out_shape += [jax.ShapeDtypeStruct(SMALL[name], F32)] * 4
    outs = pl.pallas_call(
        body, in_specs=[vmem] * (2 + 3 * n), out_specs=[vmem] * len(out_shape), out_shape=out_shape,
        name="adamw_small",
    )(vec, gates, *[w[k] for k in SMALL_NAMES], *[m[k] for k in SMALL_NAMES], *[v[k] for k in SMALL_NAMES])
    loss = outs[0]
    res = {name: tuple(outs[1 + 4 * k:5 + 4 * k]) for k, name in enumerate(SMALL_NAMES)}
    return loss, res


WEIGHT_NAMES = ("meta_tokens", "mix_norm_g", "w_in", "conv_w", "conv_b", "w_rgate", "b_rgate", "w_igate", "b_igate",
                "lru_lambda", "rg_norm_g", "hg_lower_bound", "hg_norm_g", "w_out", "ffn_norm_g", "w_gate_up", "w_down",
                "final_norm_g")


def _to_2d(name, a):
    if name in BIG:
        return a.reshape(BIG[name][:2])
    return a.reshape(SMALL[name])


def kernel(x, meta_tokens, mix_norm_g, w_in, conv_w, conv_b, w_rgate, b_rgate, w_igate, b_igate, lru_lambda, rg_norm_g, hg_lower_bound, hg_norm_g, w_out, ffn_norm_g, w_gate_up, w_down, final_norm_g, loss_target, m_meta_tokens, m_mix_norm_g, m_w_in, m_conv_w, m_conv_b, m_w_rgate, m_b_rgate, m_w_igate, m_b_igate, m_lru_lambda, m_rg_norm_g, m_hg_lower_bound, m_hg_norm_g, m_w_out, m_ffn_norm_g, m_w_gate_up, m_w_down, m_final_norm_g, v_meta_tokens, v_mix_norm_g, v_w_in, v_conv_w, v_conv_b, v_w_rgate, v_b_rgate, v_w_igate, v_b_igate, v_lru_lambda, v_rg_norm_g, v_hg_lower_bound, v_hg_norm_g, v_w_out, v_ffn_norm_g, v_w_gate_up, v_w_down, v_final_norm_g):
    w_raw = dict(zip(WEIGHT_NAMES, (meta_tokens, mix_norm_g, w_in, conv_w, conv_b, w_rgate, b_rgate, w_igate, b_igate,
                                    lru_lambda, rg_norm_g, hg_lower_bound, hg_norm_g, w_out, ffn_norm_g, w_gate_up,
                                    w_down, final_norm_g)))
    m_raw = dict(zip(WEIGHT_NAMES, (m_meta_tokens, m_mix_norm_g, m_w_in, m_conv_w, m_conv_b, m_w_rgate, m_b_rgate,
                                    m_w_igate, m_b_igate, m_lru_lambda, m_rg_norm_g, m_hg_lower_bound, m_hg_norm_g,
                                    m_w_out, m_ffn_norm_g, m_w_gate_up, m_w_down, m_final_norm_g)))
    v_raw = dict(zip(WEIGHT_NAMES, (v_meta_tokens, v_mix_norm_g, v_w_in, v_conv_w, v_conv_b, v_w_rgate, v_b_rgate,
                                    v_w_igate, v_b_igate, v_lru_lambda, v_rg_norm_g, v_hg_lower_bound, v_hg_norm_g,
                                    v_w_out, v_ffn_norm_g, v_w_gate_up, v_w_down, v_final_norm_g)))
    w = {k: _to_2d(k, a) for k, a in w_raw.items()}
    m = {k: _to_2d(k, a) for k, a in m_raw.items()}
    v = {k: _to_2d(k, a) for k, a in v_raw.items()}

    x_i, y_i, c_i = _place()
    core = jnp.reshape(c_i, (1,)).astype(jnp.int32)
    chip = jnp.reshape(2 * x_i + y_i, (1,)).astype(jnp.int32)
    chip_core = jnp.concatenate([chip, core])

    placed = _cast_into_full(w, chip)
    first, (meta_full, cw_full) = _gather_weights(placed, [w["meta_tokens"], w["conv_w"]], ("w_in",), "gather_first", 1)
    rest, _ = _gather_weights(placed, [], ("w_out", "w_gate_up", "w_down"), "gather_rest", 2)
    full = {**first, **rest}

    seq = x.shape[1]
    small ={k: w[k] for k in SMALL_NAMES if k not in SHARDED_SMALL}
    small["conv_w"] = cw_full

    def reduce_to_chips(grads, names, tag, collective_ids):
        got = _exchange_halves(grads, names, "exchange_halves_" + tag, collective_ids[0])

        def chip_sums():
            return _chip_sum(grads, got, names, core, "chip_sum_" + tag)

        def send(sums):
            arrived = _send_chip_sums({n: sums[n][1] for n in names}, names, "send_chip_sums_" + tag,
                                      collective_ids[1])
            return {n: (sums[n][0], a) for n, a in zip(names, arrived)}

        return chip_sums, send

    ffn_names, mixer_names = ("w_gate_up", "w_down"), ("w_in", "w_out")
    loss, grad_x, grads, parts, parts_mixer = _local_step(
        x.reshape(seq, D_MODEL), meta_full, loss_target.reshape(seq, D_MODEL),
        full["w_in"], full["w_out"], full["w_gate_up"], full["w_down"], small,
        on_ffn_grads=lambda g: reduce_to_chips(g, ffn_names, "ffn", (3, 4)),
        on_mixer_grads=lambda g: reduce_to_chips(g, mixer_names, "mixer", (None, 5)))
    parts.update(parts_mixer)
    totals = _total(parts, chip_core)
    g_big = dict(zip(BIG_NAMES, _share_totals(totals)))

    pieces = {k: grads[k] for k in VEC_ROW if k != "loss"}
    pieces["loss"] = loss
    vec, gates = _all_reduce_small(pieces, grads["w_gates"])
    loss_sum, res = _adamw_small(vec, gates, w, m, v)
    updates = _adamw_big(w, g_big, m, v)
    for n in BIG_NAMES:
        res[n] = (g_big[n],) + updates[n]

    out = [loss_sum.reshape(()), grad_x.reshape(1, seq, D_MODEL)]
    for j in range(4):
        out += [res[n][j].reshape(w_raw[n].shape) for n in WEIGHT_NAMES]
    return tuple(out)
```

```python
import functools
import math

import jax
import jax.numpy as jnp
from jax import lax
from jax.experimental import pallas as pl
from jax.experimental.pallas import tpu as pltpu
from jax.experimental.pallas import tpu_sc as plsc

F32 = jnp.float32
BF16 = jnp.bfloat16
HIGHEST = lax.Precision.HIGHEST
MESH = pl.DeviceIdType.MESH

D_MODEL = 1024
D_RG = 512
RG_HEAD_DIM = 64
D_HG = 512
HG_HEAD_DIM = 128
HG_HEADS = 4
CHUNK = 64
SUB = 16
N_SUB = CHUNK // SUB
N_META = 16
PAD = CHUNK - N_META
D_IN = 3072
D_FF = 2816
CONV_W = 4
LRU_C = 8.0
EPS = 1e-6
EXP_CLAMP = 80.0
GELU_C = math.sqrt(2.0 / math.pi)
GELU_A = 0.044715
N_CHIPS = 4

ADAM_LR = 0.001
ADAM_B1 = 0.9
ADAM_B2 = 0.999
ADAM_EPS = 1e-08
ADAM_WD = 0.01
ADAM_STEP = 10

VMEM_LIMIT = 56 * 1024 * 1024


def _params(*sem):
    return pltpu.CompilerParams(dimension_semantics=sem, vmem_limit_bytes=VMEM_LIMIT)


def _row_tile(rows, target):
    best = None
    for t in range(16, min(rows, target) + 1, 16):
        if rows % t == 0:
            best = t
    assert best is not None, rows
    return best


def _sigmoid(x):
    return 0.5 * jnp.tanh(0.5 * x) + 0.5


def _dot(a, b):
    return jnp.dot(a, b, preferred_element_type=F32)


def _dot_nt(a, b):
    return lax.dot_general(a, b, (((1,), (1,)), ((), ())), preferred_element_type=F32)


def _dot_tn(a, b):
    return lax.dot_general(a, b, (((0,), (0,)), ((), ())), preferred_element_type=F32)


def _rms(x):
    return lax.rsqrt(jnp.mean(x * x, axis=-1, keepdims=True) + EPS)


def _rms_bwd(dn, n, r):
    return r * (dn - n * jnp.mean(dn * n, axis=-1, keepdims=True))


def _gelu_parts(x):
    t = jnp.tanh(GELU_C * (x + GELU_A * x * x * x))
    g = 0.5 * x * (1.0 + t)
    dg = 0.5 * (1.0 + t) + 0.5 * x * (1.0 - t * t) * GELU_C * (1.0 + 3.0 * GELU_A * x * x)
    return g, dg


def _softplus_neg(lam):
    e = jnp.exp(-jnp.abs(lam))
    w = 1.0 + e
    log1p = jnp.where(w == 1.0, e, jnp.log(w) * e / (w - 1.0))
    return jnp.maximum(-lam, 0.0) + log1p


def _head_mask():
    r = lax.broadcasted_iota(jnp.int32, (D_RG, D_RG), 0) // RG_HEAD_DIM
    c = lax.broadcasted_iota(jnp.int32, (D_RG, D_RG), 1) // RG_HEAD_DIM
    return r == c


def _head_fold():
    r = lax.broadcasted_iota(jnp.int32, (D_RG, RG_HEAD_DIM), 0) % RG_HEAD_DIM
    c = lax.broadcasted_iota(jnp.int32, (D_RG, RG_HEAD_DIM), 1)
    return (r == c).astype(F32)


def _gate_weights(w_r, w_i):
    def body(wr_ref, wi_ref, o_ref):
        fold = _head_fold()
        mask = _head_mask()
        for k, ref in enumerate((wr_ref, wi_ref)):
            full = lax.dot_general(ref[...], fold, (((1,), (1,)), ((), ())),
                                   precision=HIGHEST, preferred_element_type=F32)
            o_ref[:, k * D_RG:(k + 1) * D_RG] = jnp.where(mask, full, 0.0).astype(BF16)

    return pl.pallas_call(
        body, out_shape=jax.ShapeDtypeStruct((D_RG, 2 * D_RG), BF16), name="gate_weights",
    )(w_r, w_i)


HEAD = PAD + N_META


def _window_copies(seq_hbm, buf, sems, tm):
    def first(to_vmem):
        seq, vm = seq_hbm.at[pl.ds(0, tm - HEAD)], buf.at[0, pl.ds(HEAD, tm - HEAD)]
        return pltpu.make_async_copy(seq, vm, sems.at[0]) if to_vmem else pltpu.make_async_copy(vm, seq, sems.at[0])

    def later(j, slot, to_vmem):
        seq, vm = seq_hbm.at[pl.ds(pl.multiple_of(j * tm - HEAD, 8), tm)], buf.at[slot]
        if to_vmem:
            return pltpu.make_async_copy(seq, vm, sems.at[slot])
        return pltpu.make_async_copy(vm, seq, sems.at[slot])

    return first, later


def _fetch_window(seq_hbm, buf, sems, i, n_steps, tm):
    first, later = _window_copies(seq_hbm, buf, sems, tm)
    slot = i % 2

    @pl.when(i == 0)
    def _():
        first(True).start()

    if n_steps > 1:
        @pl.when(i + 1 < n_steps)
        def _():
            later(i + 1, 1 - slot, True).start()

    @pl.when(i == 0)
    def _():
        first(True).wait()

    if n_steps > 1:
        @pl.when(i > 0)
        def _():
            later(i, slot, True).wait()

    return slot


def _in_proj(x, meta, g1, w_in):
    T = x.shape[0] + HEAD
    tm = _row_tile(T, 416)
    n_steps = T // tm

    def body(x_hbm, meta_ref, g_ref, w_ref, p_ref, u_ref, h_ref, buf, sems):
        i = pl.program_id(0)
        slot = _fetch_window(x_hbm, buf, sems, i, n_steps, tm)

        @pl.when(i == 0)
        def _():
            buf[0, 0:PAD, :] = jnp.zeros((PAD, D_MODEL), F32)
            buf[0, PAD:HEAD, :] = meta_ref[...]

        h = buf[slot]
        h_ref[...] = h
        u = (h * _rms(h) * g_ref[...]).astype(BF16)
        u_ref[...] = u
        p_ref[...] = _dot(u, w_ref[...])

    return pl.pallas_call(
        body, grid=(n_steps,),
        in_specs=[pl.BlockSpec(memory_space=pl.ANY),
                  pl.BlockSpec((N_META, D_MODEL), lambda i: (0, 0)),
                  pl.BlockSpec((1, D_MODEL), lambda i: (0, 0)),
                  pl.BlockSpec((D_MODEL, D_IN), lambda i: (0, 0))],
        out_specs=[pl.BlockSpec((tm, D_IN), lambda i: (i, 0)),
                   pl.BlockSpec((tm, D_MODEL), lambda i: (i, 0)),
                   pl.BlockSpec((tm, D_MODEL), lambda i: (i, 0))],
        out_shape=[jax.ShapeDtypeStruct((T, D_IN), F32), jax.ShapeDtypeStruct((T, D_MODEL), BF16),
                   jax.ShapeDtypeStruct((T, D_MODEL), F32)],
        scratch_shapes=[pltpu.VMEM((2, tm, D_MODEL), F32), pltpu.SemaphoreType.DMA((2,))],
        name="in_proj", compiler_params=_params("arbitrary"),
    )(x, meta, g1, w_in)


def _scan_block_fwd(A, B, rowi):
    for d in (1, 2, 4):
        a_sh = pltpu.roll(A, d, axis=0)
        b_sh = pltpu.roll(B, d, axis=0)
        m = rowi >= d
        B = jnp.where(m, A * b_sh + B, B)
        A = jnp.where(m, A * a_sh, A)
    return A, B


def _scan_block_bwd(A, B, rowi):
    for d in (1, 2, 4):
        a_sh = pltpu.roll(A, 8 - d, axis=0)
        b_sh = pltpu.roll(B, 8 - d, axis=0)
        m = rowi < 8 - d
        B = jnp.where(m, A * b_sh + B, B)
        A = jnp.where(m, A * a_sh, A)
    return A, B


def _rg_gates(xc, w_ref, bg_ref, lam):
    pre = _dot(xc.astype(BF16), w_ref[...]) + bg_ref[...]
    r = _sigmoid(pre[:, :D_RG])
    ig = _sigmoid(pre[:, D_RG:])
    sp = _softplus_neg(lam)
    la = -LRU_C * sp * r
    a = jnp.exp(la)
    th = jnp.tanh(la)
    u = 1.0 - th
    rc = pl.reciprocal(u, approx=True)
    rc = rc * (2.0 - u * rc)
    rc = rc * (2.0 - u * rc)
    m2 = -2.0 * th * rc
    inv_m = lax.rsqrt(jnp.maximum(m2, 1e-30))
    return r, ig, sp, a, m2 * inv_m, inv_m


def _conv(ext, cw_ref, cb_ref, tm):
    xc = cb_ref[...] + cw_ref[0:1, :] * ext[8 - 3:8 - 3 + tm, :]
    for j in range(1, CONV_W):
        xc = xc + cw_ref[j:j + 1, :] * ext[8 - 3 + j:8 - 3 + j + tm, :]
    return xc


def _scan_unroll(blocks):
    return 4 if blocks % 4 == 0 else 2 if blocks % 2 == 0 else 1


def _rg_fwd(p, cw, cb, wg, bg, lam, rg_g):
    T = p.shape[0]
    tm = _row_tile(T, 832)
    unroll = _scan_unroll(tm // 8)

    def body(xg_ref, cw_ref, cb_ref, w_ref, bg_ref, lam_ref, g_ref, y_ref, h_ref, ext, a_s, b_s, carry):
        i = pl.program_id(0)

        @pl.when(i == 0)
        def _():
            ext[0:8, :] = jnp.zeros((8, D_RG), F32)
            carry[...] = jnp.zeros((1, D_RG), F32)

        ext[8:8 + tm, :] = xg_ref[:, :D_RG]
        xc = _conv(ext, cw_ref, cb_ref, tm)
        r, ig, sp, a, m, _ = _rg_gates(xc, w_ref, bg_ref, lam_ref[...])
        row = i * tm + lax.broadcasted_iota(jnp.int32, (tm, 1), 0)
        a_s[...] = a
        b_s[...] = jnp.where(row >= PAD, m * ig * xc, 0.0)
        rowi = lax.broadcasted_iota(jnp.int32, (8, D_RG), 0)

        def blk(j, c):
            for u in range(unroll):
                o = pl.multiple_of((j * unroll + u) * 8, 8)
                A, B = _scan_block_fwd(a_s[pl.ds(o, 8), :], b_s[pl.ds(o, 8), :], rowi)
                h = B + A * c
                h_ref[pl.ds(o, 8), :] = h
                c = h[7:8, :]
            return c

        carry[...] = lax.fori_loop(0, tm // (8 * unroll), blk, carry[...])
        ext[0:8, :] = ext[tm:tm + 8, :]
        g, _ = _gelu_parts(xg_ref[:, D_RG:])
        yy = g * h_ref[...]
        y_ref[...] = (yy * _rms(yy) * g_ref[...]).astype(BF16)

    vec = lambda n: pl.BlockSpec((1, n), lambda i: (0, 0))
    return pl.pallas_call(
        body, grid=(T // tm,),
        in_specs=[pl.BlockSpec((tm, 2 * D_RG), lambda i: (i, 0)),
                  pl.BlockSpec((CONV_W, D_RG), lambda i: (0, 0)), vec(D_RG),
                  pl.BlockSpec((D_RG, 2 * D_RG), lambda i: (0, 0)), vec(2 * D_RG), vec(D_RG), vec(D_RG)],
        out_specs=[pl.BlockSpec((tm, D_RG), lambda i: (i, 0)), pl.BlockSpec((tm, D_RG), lambda i: (i, 0))],
        out_shape=[jax.ShapeDtypeStruct((T, D_RG), BF16), jax.ShapeDtypeStruct((T, D_RG), F32)],
        scratch_shapes=[pltpu.VMEM((tm + 8, D_RG), F32), pltpu.VMEM((tm, D_RG), F32),
                        pltpu.VMEM((tm, D_RG), F32), pltpu.VMEM((1, D_RG), F32)],
        name="rg_fwd", compiler_params=_params("arbitrary"),
    )(p, cw, cb, wg, bg, lam, rg_g)


def _tri(lower):
    r = lax.broadcasted_iota(jnp.int32, (CHUNK, CHUNK), 0)
    c = lax.broadcasted_iota(jnp.int32, (CHUNK, CHUNK), 1)
    return ((c <= r) if lower else (c >= r)).astype(F32)


def _hg_gates(hq, hf, lbraw_ref, valid):
    lb = _sigmoid(lbraw_ref[0:1, :] - lbraw_ref[1:2, :])
    sq = _sigmoid(hq)
    q = hq * sq
    sf = _sigmoid(hf)
    f = lb + (1.0 - lb) * sf
    lf = jnp.where(valid, jnp.log(f), 0.0)
    b = jnp.dot(_tri(True), lf, precision=HIGHEST, preferred_element_type=F32)
    return lb, sq, q, sf, f, b


def _hg_head(qh, kh, bh):
    blk = lax.broadcasted_iota(jnp.int32, (CHUNK, 1), 0) // SUB
    b_last = bh[CHUNK - 1:CHUNK, :]
    refs = [bh[SUB * s:SUB * s + 1, :] for s in range(N_SUB)]
    r_sel = refs[N_SUB - 1]
    for s in range(N_SUB - 2, -1, -1):
        r_sel = jnp.where(blk == s, refs[s], r_sel)
    eb = jnp.exp(bh)
    eq = jnp.exp(bh - r_sel)
    ekh = jnp.exp(b_last - bh)
    ek = [jnp.exp(jnp.minimum(refs[s] - bh, EXP_CLAMP)) for s in range(N_SUB)]
    qe = qh * eq
    q_hat = jnp.concatenate([jnp.where(blk == s, qe, 0.0) for s in range(N_SUB)], axis=1)
    k_til = jnp.concatenate([kh * ek[s] for s in range(N_SUB)], axis=1)
    return blk, b_last, eb, eq, ekh, ek, q_hat, k_til


def _causal():
    r = lax.broadcasted_iota(jnp.int32, (CHUNK, CHUNK), 0)
    c = lax.broadcasted_iota(jnp.int32, (CHUNK, CHUNK), 1)
    return r >= c


def _chunks_per_step(n_chunks):
    for c in (5, 4, 3, 2):
        if n_chunks % c == 0:
            return c
    return 1


def _hg_fwd(p, lbraw, hg_g):
    T = p.shape[0]
    n_chunks = T // CHUNK
    cps = _chunks_per_step(n_chunks)
    rows = cps * CHUNK

    def body(hq_ref, hf_ref, hi_ref, hg_ref, lb_ref, g_ref, y_ref, o_ref, st_all_ref, st):
        i = pl.program_id(0)

        @pl.when(i == 0)
        def _():
            st[...] = jnp.zeros_like(st)

        def chunk(j, carry):
            rs = pl.ds(pl.multiple_of(j * CHUNK, CHUNK), CHUNK)
            chunk_body(i * cps + j, hq_ref.at[rs, :], hf_ref.at[rs, :], hi_ref.at[rs, :], hg_ref.at[rs, :], lb_ref,
                       g_ref, y_ref.at[rs, :], o_ref.at[rs, :], st_all_ref.at[pl.ds(j, 1)], st)
            return carry

        lax.fori_loop(0, cps, chunk, 0, unroll=True)

    def chunk_body(n, hq_ref, hf_ref, hi_ref, hg_ref, lb_ref, g_ref, y_ref, o_ref, st_all_ref, st):
        valid = (n * CHUNK + lax.broadcasted_iota(jnp.int32, (CHUNK, 1), 0)) >= PAD
        hq, hf, v, hg = hq_ref[...], hf_ref[...], hi_ref[...], hg_ref[...]
        lb, sq, q, sf, f, b = _hg_gates(hq, hf, lb_ref, valid)
        k = 1.0 - f
        st_all_ref[0] = st[...]
        causal = _causal()
        v_t = v.T.astype(BF16)
        heads = [slice(h * HG_HEAD_DIM, (h + 1) * HG_HEAD_DIM) for h in range(HG_HEADS)]
        fac = []
        for sl in heads:
            qh, kh, bh = q[:, sl], k[:, sl], b[:, sl]
            _, b_last, eb, _, ekh, _, q_hat, k_til = _hg_head(qh, kh, bh)
            fac.append((jnp.exp(b_last), (qh * eb).astype(BF16), q_hat.astype(BF16), k_til.astype(BF16),
                        (kh * ekh).astype(BF16), v[:, sl].astype(BF16)))
        raw = []
        for sl, (_, q_til, q_hat, k_til, k_hat, _) in zip(heads, fac):
            st_h = st[sl, :]
            raw.append((_dot_nt(q_til, st_h.astype(BF16)), _dot_nt(q_hat, k_til), _dot(v_t[sl, :], k_hat), st_h))
        for sl, (e_last, _, _, _, _, vb), (inter, att, upd, st_h) in zip(heads, fac, raw):
            o = inter + _dot(jnp.where(causal, att, 0.0).astype(BF16), vb)
            st[sl, :] = st_h * e_last + upd
            o_ref[:, sl] = o
            hgh = hg[:, sl]
            y_ref[:, sl] = (o * _rms(o) * g_ref[...] * (hgh * _sigmoid(hgh))).astype(BF16)

    col = lambda j: pl.BlockSpec((rows, D_HG), lambda n: (n, j))
    return pl.pallas_call(
        body, grid=(n_chunks // cps,),
        in_specs=[col(2), col(3), col(4), col(5),
                  pl.BlockSpec((2, D_HG), lambda n: (0, 0)), pl.BlockSpec((1, HG_HEAD_DIM), lambda n: (0, 0))],
        out_specs=[pl.BlockSpec((rows, D_HG), lambda n: (n, 0)), pl.BlockSpec((rows, D_HG), lambda n: (n, 0)),
                   pl.BlockSpec((cps, D_HG, HG_HEAD_DIM), lambda n: (n, 0, 0))],
        out_shape=[jax.ShapeDtypeStruct((T, D_HG), BF16), jax.ShapeDtypeStruct((T, D_HG), F32),
                   jax.ShapeDtypeStruct((n_chunks, D_HG, HG_HEAD_DIM), F32)],
        scratch_shapes=[pltpu.VMEM((D_HG, HG_HEAD_DIM), F32)],
        name="hg_fwd", compiler_params=_params("arbitrary"),
    )(p, p, p, p, lbraw, hg_g)


def _ffn_fwd(h0, y_rg, y_hg, w_out, g2, w_gu, w_down, gf, target):
    T = h0.shape[0]
    tm = _row_tile(T, 320)
    n_steps = T // tm

    def body(h_ref, yr_ref, yh_ref, wo_ref, g2_ref, wgu_ref, wd_ref, gf_ref, t_hbm,
             h1_ref, v_ref, y_ref, gu_ref, act_ref, dh2_ref, dh2b_ref, loss_ref, gg_ref, tbuf, sems):
        i = pl.program_id(0)
        slot = _fetch_window(t_hbm, tbuf, sems, i, n_steps, tm)

        @pl.when(i == 0)
        def _():
            loss_ref[...] = jnp.zeros_like(loss_ref)
            gg_ref[...] = jnp.zeros_like(gg_ref)
            tbuf[0, 0:HEAD, :] = jnp.zeros((HEAD, D_MODEL), F32)

        y_ref[:, :D_RG] = yr_ref[...]
        y_ref[:, D_RG:] = yh_ref[...]
        h1 = h_ref[...] + _dot(y_ref[...], wo_ref[...])
        h1_ref[...] = h1
        v = (h1 * _rms(h1) * g2_ref[...]).astype(BF16)
        v_ref[...] = v

        gu = _dot(v, wgu_ref[...])
        gu_ref[...] = gu.astype(BF16)
        g = gu[:, :D_FF]
        act = (g * _sigmoid(g) * gu[:, D_FF:]).astype(BF16)
        act_ref[...] = act

        h2 = h1 + _dot(act, wd_ref[...])
        r = _rms(h2)
        n = h2 * r
        gf_ = gf_ref[...]
        row = i * tm + lax.broadcasted_iota(jnp.int32, (tm, 1), 0)
        err = jnp.where(row >= HEAD, n * gf_ - tbuf[slot], 0.0)
        loss_ref[...] += 0.5 * jnp.sum(jnp.mean(err * err, axis=-1, keepdims=True), axis=0, keepdims=True)
        dy = err * (1.0 / D_MODEL)
        gg_ref[...] += jnp.sum(dy * n, axis=0, keepdims=True)
        dh2 = _rms_bwd(dy * gf_, n, r)
        dh2_ref[...] = dh2
        dh2b_ref[...] = dh2.astype(BF16)

    row_spec = lambda n: pl.BlockSpec((tm, n), lambda i: (i, 0))
    vec = pl.BlockSpec((1, D_MODEL), lambda i: (0, 0))
    return pl.pallas_call(
        body, grid=(n_steps,),
        in_specs=[row_spec(D_MODEL), row_spec(D_RG), row_spec(D_HG), _resident((D_MODEL, D_MODEL)), vec,
                  _resident((D_MODEL, 2 * D_FF)), _resident((D_FF, D_MODEL)), vec,
                  pl.BlockSpec(memory_space=pl.ANY)],
        out_specs=[row_spec(D_MODEL), row_spec(D_MODEL), row_spec(D_MODEL), row_spec(2 * D_FF), row_spec(D_FF),
                   row_spec(D_MODEL), row_spec(D_MODEL), pl.BlockSpec((1, 1), lambda i: (0, 0)), vec],
        out_shape=[jax.ShapeDtypeStruct((T, D_MODEL), F32), jax.ShapeDtypeStruct((T, D_MODEL), BF16),
                   jax.ShapeDtypeStruct((T, D_MODEL), BF16), jax.ShapeDtypeStruct((T, 2 * D_FF), BF16),
                   jax.ShapeDtypeStruct((T, D_FF), BF16), jax.ShapeDtypeStruct((T, D_MODEL), F32),
                   jax.ShapeDtypeStruct((T, D_MODEL), BF16), jax.ShapeDtypeStruct((1, 1), F32),
                   jax.ShapeDtypeStruct((1, D_MODEL), F32)],
        scratch_shapes=[pltpu.VMEM((2, tm, D_MODEL), F32), pltpu.SemaphoreType.DMA((2,))],
        name="ffn_fwd", compiler_params=_params("arbitrary"),
    )(h0, y_rg, y_hg, w_out, g2, w_gu, w_down, gf, target)


def _out_proj(h0, y_rg, y_hg, w_out, g2):
    T = h0.shape[0]
    tm = _row_tile(T, 832)

    def body(h_ref, yr_ref, yh_ref, w_ref, g_ref, h1_ref, v_ref, y_ref):
        y_ref[:, :D_RG] = yr_ref[...]
        y_ref[:, D_RG:] = yh_ref[...]
        h1 = h_ref[...] + _dot(y_ref[...], w_ref[...])
        h1_ref[...] = h1
        v_ref[...] = (h1 * _rms(h1) * g_ref[...]).astype(BF16)

    row = lambda n: pl.BlockSpec((tm, n), lambda i: (i, 0))
    return pl.pallas_call(
        body, grid=(T // tm,),
        in_specs=[row(D_MODEL), row(D_RG), row(D_HG), pl.BlockSpec((D_MODEL, D_MODEL), lambda i: (0, 0)),
                  pl.BlockSpec((1, D_MODEL), lambda i: (0, 0))],
        out_specs=[row(D_MODEL), row(D_MODEL), row(D_MODEL)],
        out_shape=[jax.ShapeDtypeStruct((T, D_MODEL), F32), jax.ShapeDtypeStruct((T, D_MODEL), BF16),
                   jax.ShapeDtypeStruct((T, D_MODEL), BF16)],
        name="out_proj", compiler_params=_params("parallel"),
    )(h0, y_rg, y_hg, w_out, g2)


def _gate_up(v, w_gu):
    T = v.shape[0]
    tm = _row_tile(T, 416)

    def body(v_ref, w_ref, gu_ref, act_ref):
        gu = _dot(v_ref[...], w_ref[...])
        gu_ref[...] = gu.astype(BF16)
        g = gu[:, :D_FF]
        act_ref[...] = (g * _sigmoid(g) * gu[:, D_FF:]).astype(BF16)

    row = lambda n: pl.BlockSpec((tm, n), lambda i: (i, 0))
    return pl.pallas_call(
        body, grid=(T // tm,),
        in_specs=[row(D_MODEL), pl.BlockSpec((D_MODEL, 2 * D_FF), lambda i: (0, 0))],
        out_specs=[row(2 * D_FF), row(D_FF)],
        out_shape=[jax.ShapeDtypeStruct((T, 2 * D_FF), BF16), jax.ShapeDtypeStruct((T, D_FF), BF16)],
        name="gate_up", compiler_params=_params("parallel"),
    )(v, w_gu)


def _down_loss(h1, act, w_down, gf, target):
    T = h1.shape[0]
    tm = _row_tile(T, 832)
    n_steps = T // tm

    def body(h_ref, a_ref, w_ref, g_ref, t_hbm, dh2_ref, dh2b_ref, loss_ref, gg_ref, tbuf, sems):
        i = pl.program_id(0)
        slot = _fetch_window(t_hbm, tbuf, sems, i, n_steps, tm)

        @pl.when(i == 0)
        def _():
            loss_ref[...] = jnp.zeros_like(loss_ref)
            gg_ref[...] = jnp.zeros_like(gg_ref)
            tbuf[0, 0:HEAD, :] = jnp.zeros((HEAD, D_MODEL), F32)

        h2 = h_ref[...] + _dot(a_ref[...], w_ref[...])
        r = _rms(h2)
        n = h2 * r
        gf_ = g_ref[...]
        row = i * tm + lax.broadcasted_iota(jnp.int32, (tm, 1), 0)
        err = jnp.where(row >= HEAD, n * gf_ - tbuf[slot], 0.0)
        loss_ref[...] += 0.5 * jnp.sum(jnp.mean(err * err, axis=-1, keepdims=True), axis=0, keepdims=True)
        dy = err * (1.0 / D_MODEL)
        gg_ref[...] += jnp.sum(dy * n, axis=0, keepdims=True)
        dh2 = _rms_bwd(dy * gf_, n, r)
        dh2_ref[...] = dh2
        dh2b_ref[...] = dh2.astype(BF16)

    row_spec = lambda n: pl.BlockSpec((tm, n), lambda i: (i, 0))
    return pl.pallas_call(
        body, grid=(T // tm,),
        in_specs=[row_spec(D_MODEL), row_spec(D_FF), pl.BlockSpec((D_FF, D_MODEL), lambda i: (0, 0)),
                  pl.BlockSpec((1, D_MODEL), lambda i: (0, 0)), pl.BlockSpec(memory_space=pl.ANY)],
        out_specs=[row_spec(D_MODEL), row_spec(D_MODEL), pl.BlockSpec((1, 1), lambda i: (0, 0)),
                   pl.BlockSpec((1, D_MODEL), lambda i: (0, 0))],
        out_shape=[jax.ShapeDtypeStruct((T, D_MODEL), F32), jax.ShapeDtypeStruct((T, D_MODEL), BF16),
                   jax.ShapeDtypeStruct((1, 1), F32), jax.ShapeDtypeStruct((1, D_MODEL), F32)],
        scratch_shapes=[pltpu.VMEM((2, tm, D_MODEL), F32), pltpu.SemaphoreType.DMA((2,))],
        name="down_loss", compiler_params=_params("arbitrary"),
    )(h1, act, w_down, gf, target)


def _resident(shape):
    return pl.BlockSpec(shape, lambda i: (0,) * len(shape), pipeline_mode=pl.Buffered(1))


def _ffn_bwd(dh2b, gu, w_down, w_gu, h1, g2, dh2, w_out):
    T = h1.shape[0]
    tm = _row_tile(T, 320)

    def body(d_ref, gu_ref, wd_ref, wgu_ref, h_ref, g_ref, d2_ref, wo_ref, dgu_ref, dh1_ref, dh1b_ref, dy_ref, gg_ref):
        i = pl.program_id(0)

        @pl.when(i == 0)
        def _():
            gg_ref[...] = jnp.zeros_like(gg_ref)

        dact = _dot_nt(d_ref[...], wd_ref[...]).astype(BF16)
        g = gu_ref[:, :D_FF]
        u = gu_ref[:, D_FF:]
        s = _sigmoid(g)
        dgu_ref[:, :D_FF] = dact * u * (s * (1.0 + g * (1.0 - s)))
        dgu_ref[:, D_FF:] = dact * (g * s)

        dv = _dot_nt(dgu_ref[...], wgu_ref[...])
        h1_ = h_ref[...]
        r = _rms(h1_)
        n = h1_ * r
        gg_ref[...] += jnp.sum(dv * n, axis=0, keepdims=True)
        dh1 = d2_ref[...] + _rms_bwd(dv * g_ref[...], n, r)
        dh1_ref[...] = dh1
        db = dh1.astype(BF16)
        dh1b_ref[...] = db
        dy_ref[...] = _dot_nt(db, wo_ref[...])

    row = lambda n: pl.BlockSpec((tm, n), lambda i: (i, 0))
    return pl.pallas_call(
        body, grid=(T // tm,),
        in_specs=[row(D_MODEL), row(2 * D_FF), _resident((D_FF, D_MODEL)), _resident((D_MODEL, 2 * D_FF)),
                  row(D_MODEL), pl.BlockSpec((1, D_MODEL), lambda i: (0, 0)), row(D_MODEL),
                  _resident((D_MODEL, D_MODEL))],
        out_specs=[row(2 * D_FF), row(D_MODEL), row(D_MODEL), row(D_MODEL),
                   pl.BlockSpec((1, D_MODEL), lambda i: (0, 0))],
        out_shape=[jax.ShapeDtypeStruct((T, 2 * D_FF), BF16), jax.ShapeDtypeStruct((T, D_MODEL), F32),
                   jax.ShapeDtypeStruct((T, D_MODEL), BF16), jax.ShapeDtypeStruct((T, D_MODEL), F32),
                   jax.ShapeDtypeStruct((1, D_MODEL), F32)],
        name="ffn_bwd", compiler_params=_params("arbitrary"),
    )(dh2b, gu, w_down, w_gu, h1, g2, dh2, w_out)


def _ffn_bwd_act(dh2b, gu, w_down):
    T = dh2b.shape[0]
    tm = _row_tile(T, 416)

    def body(d_ref, gu_ref, w_ref, dgu_ref):
        dact = _dot_nt(d_ref[...], w_ref[...]).astype(BF16)
        g = gu_ref[:, :D_FF]
        u = gu_ref[:, D_FF:]
        s = _sigmoid(g)
        dgu_ref[:, :D_FF] = dact * u * (s * (1.0 + g * (1.0 - s)))
        dgu_ref[:, D_FF:] = dact * (g * s)

    row = lambda n: pl.BlockSpec((tm, n), lambda i: (i, 0))
    return pl.pallas_call(
        body, grid=(T // tm,),
        in_specs=[row(D_MODEL), row(2 * D_FF), pl.BlockSpec((D_FF, D_MODEL), lambda i: (0, 0))],
        out_specs=row(2 * D_FF),
        out_shape=jax.ShapeDtypeStruct((T, 2 * D_FF), BF16),
        name="ffn_bwd_act", compiler_params=_params("parallel"),
    )(dh2b, gu, w_down)


def _ffn_bwd_in(dgu, w_gu, h1, g2, dh2, w_out):
    T = h1.shape[0]
    tm = _row_tile(T, 416)

    def body(dgu_ref, wgu_ref, h_ref, g_ref, d2_ref, wo_ref, dh1_ref, dh1b_ref, dy_ref, gg_ref):
        i = pl.program_id(0)

        @pl.when(i == 0)
        def _():
            gg_ref[...] = jnp.zeros_like(gg_ref)

        dv = _dot_nt(dgu_ref[...], wgu_ref[...])
        h1 = h_ref[...]
        r = _rms(h1)
        n = h1 * r
        gg_ref[...] += jnp.sum(dv * n, axis=0, keepdims=True)
        dh1 = d2_ref[...] + _rms_bwd(dv * g_ref[...], n, r)
        dh1_ref[...] = dh1
        db = dh1.astype(BF16)
        dh1b_ref[...] = db
        dy_ref[...] = _dot_nt(db, wo_ref[...])

    row = lambda n: pl.BlockSpec((tm, n), lambda i: (i, 0))
    return pl.pallas_call(
        body, grid=(T // tm,),
        in_specs=[row(2 * D_FF), pl.BlockSpec((D_MODEL, 2 * D_FF), lambda i: (0, 0)),
                  row(D_MODEL), pl.BlockSpec((1, D_MODEL), lambda i: (0, 0)), row(D_MODEL),
                  pl.BlockSpec((D_MODEL, D_MODEL), lambda i: (0, 0))],
        out_specs=[row(D_MODEL), row(D_MODEL), row(D_MODEL), pl.BlockSpec((1, D_MODEL), lambda i: (0, 0))],
        out_shape=[jax.ShapeDtypeStruct((T, D_MODEL), F32), jax.ShapeDtypeStruct((T, D_MODEL), BF16),
                   jax.ShapeDtypeStruct((T, D_MODEL), F32), jax.ShapeDtypeStruct((1, D_MODEL), F32)],
        name="ffn_bwd_in", compiler_params=_params("arbitrary"),
    )(dgu, w_gu, h1, g2, dh2, w_out)


def _rg_bwd(p, hs, dy, dp, cw, cb, wg, bg, lam, rg_g):
    T = p.shape[0]
    tm = _row_tile(T, 832)
    nt = T // tm
    hb = tm // 8
    unroll = _scan_unroll(hb)

    def body(xg_ref, xh_ref, h_ref, hh_ref, dy_ref, dp_in_ref, cw_ref, cb_ref, w_ref, bg_ref, lam_ref, g_ref,
             dp_ref, gcw_ref, gcb_ref, gw_ref, gbg_ref, glam_ref, gg_ref,
             ext, dext, a_s, b_s, d_s, gacc, carry_d, carry_a):
        i = pl.program_id(0)
        t_idx = nt - 1 - i

        @pl.when(i == 0)
        def _():
            dext[tm:tm + 8, :] = jnp.zeros((8, D_RG), F32)
            carry_d[...] = jnp.zeros_like(carry_d)
            carry_a[...] = jnp.zeros_like(carry_a)
            gacc[...] = jnp.zeros_like(gacc)
            for ref in (gcw_ref, gcb_ref, gbg_ref, glam_ref, gg_ref, gw_ref):
                ref[...] = jnp.zeros_like(ref)

        first = t_idx == 0
        ext[0:8, :] = jnp.where(first, 0.0, xh_ref[:, :D_RG])
        ext[8:8 + tm, :] = xg_ref[:, :D_RG]
        xc = _conv(ext, cw_ref, cb_ref, tm)
        lam_ = lam_ref[...]
        r, ig, sp, a, m, inv_m = _rg_gates(xc, w_ref, bg_ref, lam_)
        row = t_idx * tm + lax.broadcasted_iota(jnp.int32, (tm, 1), 0)
        valid = row >= PAD

        gr = xg_ref[:, D_RG:]
        g, dgelu = _gelu_parts(gr)
        h = h_ref[...]
        yy = g * h
        rr = _rms(yy)
        nn = yy * rr
        dy_ = dy_ref[...]
        gg_ref[...] += jnp.sum(dy_ * nn, axis=0, keepdims=True)
        dyy = _rms_bwd(dy_ * g_ref[...], nn, rr)
        dp_ref[:, D_RG:] = (dyy * h * dgelu).astype(BF16)

        a_s[...] = a
        b_s[...] = dyy * g
        rowi = lax.broadcasted_iota(jnp.int32, (8, D_RG), 0)

        def blk(jj, c):
            cd, ca = c
            for u in range(unroll):
                o = pl.multiple_of((hb - 1 - (jj * unroll + u)) * 8, 8)
                a_blk = a_s[pl.ds(o, 8), :]
                a_next = jnp.where(rowi == 7, ca, pltpu.roll(a_blk, 7, axis=0))
                A, B = _scan_block_bwd(a_next, b_s[pl.ds(o, 8), :], rowi)
                d = B + A * cd
                d_s[pl.ds(o, 8), :] = d
                cd, ca = d[0:1, :], a_blk[0:1, :]
            return cd, ca

        cd, ca = lax.fori_loop(0, hb // unroll, blk, (carry_d[...], carry_a[...]))
        carry_d[...] = cd
        carry_a[...] = ca
        delta = d_s[...]

        h_last_prev = jnp.where(first, 0.0, hh_ref[7:8, :])
        row0 = lax.broadcasted_iota(jnp.int32, (tm, 1), 0) == 0
        h_prev = jnp.where(row0, h_last_prev, pltpu.roll(h, 1, axis=0))
        dbx = jnp.where(valid, delta, 0.0)
        da = delta * h_prev
        di = dbx * m * xc
        dm = dbx * ig * xc
        dla = a * (da - dm * a * inv_m)
        dla = jnp.where(valid, dla, 0.0)
        glam_ref[...] += jnp.sum(dla * r, axis=0, keepdims=True) * (LRU_C / (1.0 + jnp.exp(lam_)))
        dr = (-LRU_C) * sp * dla
        dpre = jnp.concatenate([dr * r * (1.0 - r), di * ig * (1.0 - ig)], axis=1)
        gbg_ref[...] += jnp.sum(dpre, axis=0, keepdims=True)
        dpre_b = dpre.astype(BF16)
        gacc[...] += _dot_tn(xc.astype(BF16), dpre_b)
        dxc = dbx * m * ig + _dot_nt(dpre_b, w_ref[...])
        gcb_ref[...] += jnp.sum(dxc, axis=0, keepdims=True)
        for j in range(CONV_W):
            gcw_ref[j:j + 1, :] += jnp.sum(dxc * ext[8 - 3 + j:8 - 3 + j + tm, :], axis=0, keepdims=True)
        dext[0:tm, :] = dxc
        dxr = cw_ref[0:1, :] * dext[3:3 + tm, :]
        for j in range(1, CONV_W):
            dxr = dxr + cw_ref[j:j + 1, :] * dext[3 - j:3 - j + tm, :]
        dp_ref[:, :D_RG] = dxr.astype(BF16)
        dext[tm:tm + 8, :] = dext[0:8, :]

        @pl.when(i == nt - 1)
        def _():
            fold = _head_fold()
            mask = _head_mask()
            for k in range(2):
                blockdiag = jnp.where(mask, gacc[:, k * D_RG:(k + 1) * D_RG], 0.0)
                gw_ref[k * D_RG:(k + 1) * D_RG, :] = jnp.dot(blockdiag, fold, precision=HIGHEST,
                                                             preferred_element_type=F32)

    vec = lambda n: pl.BlockSpec((1, n), lambda i: (0, 0))
    rev = lambda n: pl.BlockSpec((tm, n), lambda i: (nt - 1 - i, 0))
    halo = lambda n: pl.BlockSpec((8, n), lambda i: (jnp.maximum((nt - 1 - i) * hb - 1, 0), 0))
    return pl.pallas_call(
        body, grid=(nt,),
        in_specs=[rev(2 * D_RG), halo(2 * D_RG), rev(D_RG), halo(D_RG), rev(D_RG), ANY,
                  pl.BlockSpec((CONV_W, D_RG), lambda i: (0, 0)), vec(D_RG),
                  pl.BlockSpec((D_RG, 2 * D_RG), lambda i: (0, 0)), vec(2 * D_RG), vec(D_RG), vec(D_RG)],
        out_specs=[rev(2 * D_RG), pl.BlockSpec((CONV_W, D_RG), lambda i: (0, 0)), vec(D_RG),
                   pl.BlockSpec((2 * D_RG, RG_HEAD_DIM), lambda i: (0, 0)), vec(2 * D_RG), vec(D_RG), vec(D_RG)],
        input_output_aliases={5: 0},
        out_shape=[jax.ShapeDtypeStruct((T, D_IN), BF16), jax.ShapeDtypeStruct((CONV_W, D_RG), F32),
                   jax.ShapeDtypeStruct((1, D_RG), F32), jax.ShapeDtypeStruct((2 * D_RG, RG_HEAD_DIM), F32),
                   jax.ShapeDtypeStruct((1, 2 * D_RG), F32), jax.ShapeDtypeStruct((1, D_RG), F32),
                   jax.ShapeDtypeStruct((1, D_RG), F32)],
        scratch_shapes=[pltpu.VMEM((tm + 8, D_RG), F32), pltpu.VMEM((tm + 8, D_RG), F32),
                        pltpu.VMEM((tm, D_RG), F32), pltpu.VMEM((tm, D_RG), F32), pltpu.VMEM((tm, D_RG), F32),
                        pltpu.VMEM((D_RG, 2 * D_RG), F32), pltpu.VMEM((1, D_RG), F32), pltpu.VMEM((1, D_RG), F32)],
        name="rg_bwd", compiler_params=_params("arbitrary"),
    )(p, p, hs, hs, dy, dp, cw, cb, wg, bg, lam, rg_g)


def _hg_bwd(p, o_all, st_all, dy, lbraw, hg_g):
    T = p.shape[0]
    n_chunks = T // CHUNK
    cps = _chunks_per_step(n_chunks)
    rows = cps * CHUNK
    n_steps = n_chunks // cps

    def body(hq_ref, hf_ref, hi_ref, hg_ref, o_ref, st_ref, dy_ref, lb_ref, g_ref,
             dp_ref, glb_ref, gg_ref, dst):
        i = pl.program_id(0)

        @pl.when(i == 0)
        def _():
            dst[...] = jnp.zeros_like(dst)
            glb_ref[...] = jnp.zeros_like(glb_ref)
            gg_ref[...] = jnp.zeros_like(gg_ref)

        dp_ref[:, :2 * D_RG] = jnp.zeros((rows, 2 * D_RG), BF16)

        def chunk(jj, carry):
            j = cps - 1 - jj
            rs = pl.ds(pl.multiple_of(j * CHUNK, CHUNK), CHUNK)
            chunk_body((n_steps - 1 - i) * cps + j, hq_ref.at[rs, :], hf_ref.at[rs, :], hi_ref.at[rs, :],
                       hg_ref.at[rs, :], o_ref.at[rs, :], st_ref.at[pl.ds(j, 1)], dy_ref.at[rs, :], lb_ref, g_ref,
                       dp_ref.at[rs, pl.ds(2 * D_RG, 4 * D_HG)], glb_ref, gg_ref, dst)
            return carry

        lax.fori_loop(0, cps, chunk, 0, unroll=True)

    def chunk_body(n, hq_ref, hf_ref, hi_ref, hg_ref, o_ref, st_ref, dy_ref, lb_ref, g_ref,
                   dp_ref, glb_ref, gg_ref, dst):
        valid = (n * CHUNK + lax.broadcasted_iota(jnp.int32, (CHUNK, 1), 0)) >= PAD
        hq, hf, v, hg = hq_ref[...], hf_ref[...], hi_ref[...], hg_ref[...]
        lb, sq, q, sf, f, b = _hg_gates(hq, hf, lb_ref, valid)
        k = 1.0 - f
        causal = _causal()
        r_i = lax.broadcasted_iota(jnp.int32, (CHUNK, CHUNK), 0)
        c_i = lax.broadcasted_iota(jnp.int32, (CHUNK, CHUNK), 1)
        causal_t = r_i <= c_i
        is_last = lax.broadcasted_iota(jnp.int32, (CHUNK, 1), 0) == CHUNK - 1
        g_ = g_ref[...]
        db_parts, dq_parts, dk_parts = [], [], []
        gg = jnp.zeros((1, HG_HEAD_DIM), F32)
        heads = [slice(h * HG_HEAD_DIM, (h + 1) * HG_HEAD_DIM) for h in range(HG_HEADS)]

        do_parts = []
        for h, sl in enumerate(heads):
            o = o_ref[:, sl]
            ro = _rms(o)
            no = o * ro
            hgh = hg[:, sl]
            sg = _sigmoid(hgh)
            dyh = dy_ref[:, sl]
            dp_ref[:, 3 * D_HG + h * HG_HEAD_DIM:3 * D_HG + (h + 1) * HG_HEAD_DIM] = (
                dyh * no * g_ * sg * (1.0 + hgh * (1.0 - sg))).astype(BF16)
            dng = dyh * hgh * sg
            gg = gg + jnp.sum(dng * no, axis=0, keepdims=True)
            do_parts.append(_rms_bwd(dng * g_, no, ro))
        do_t = jnp.concatenate(do_parts, axis=1).T.astype(BF16)

        fac = []
        for sl, do in zip(heads, do_parts):
            qh, kh, bh = q[:, sl], k[:, sl], b[:, sl]
            blk, b_last, eb, eq, ekh, ek, q_hat, k_til = _hg_head(qh, kh, bh)
            fac.append(dict(qh=qh, kh=kh, blk=blk, e_last=jnp.exp(b_last), eb=eb, eq=eq, ekh=ekh, ek=ek,
                            q_til=qh * eb, k_hat=kh * ekh, qhb=q_hat.astype(BF16), ktb=k_til.astype(BF16),
                            vb=v[:, sl].astype(BF16), dob=do.astype(BF16)))

        first = []
        for sl, t in zip(heads, fac):
            st_h = st_ref[0, sl, :]
            dst_h = dst[sl, :]
            dstb = dst_h.astype(BF16)
            first.append(dict(
                att_t=_dot_nt(t["ktb"], t["qhb"]), datt=_dot_nt(t["dob"], t["vb"]),
                datt_t=_dot_nt(t["vb"], t["dob"]), dk_hat=_dot(t["vb"], dstb),
                dv=_dot_nt(t["k_hat"].astype(BF16), dstb), dq_til=_dot(t["dob"], st_h.astype(BF16)),
                state=t["e_last"] * jnp.sum(dst_h * st_h, axis=0, keepdims=True)))
            dst[sl, :] = dst_h * t["e_last"] + _dot(do_t[sl, :], t["q_til"].astype(BF16))

        for h, (t, m) in enumerate(zip(fac, first)):
            qh, kh, blk, eb, eq, ekh, ek = t["qh"], t["kh"], t["blk"], t["eb"], t["eq"], t["ekh"], t["ek"]
            q_til, k_hat, qhb, ktb, dob = t["q_til"], t["k_hat"], t["qhb"], t["ktb"], t["dob"]
            dk_hat, dq_til = m["dk_hat"], m["dq_til"]
            dv = m["dv"] + _dot(jnp.where(causal_t, m["att_t"], 0.0).astype(BF16), dob)
            dq_hat = _dot(jnp.where(causal, m["datt"], 0.0).astype(BF16), ktb)
            dk_til = _dot(jnp.where(causal_t, m["datt_t"], 0.0).astype(BF16), qhb)
            db_last = jnp.sum(dk_hat * k_hat, axis=0, keepdims=True) + m["state"]
            dq_sel = dq_hat[:, (N_SUB - 1) * HG_HEAD_DIM:]
            for s in range(N_SUB - 2, -1, -1):
                dq_sel = jnp.where(blk == s, dq_hat[:, s * HG_HEAD_DIM:(s + 1) * HG_HEAD_DIM], dq_sel)
            dq_a = dq_sel * eq
            dk_a = dk_til[:, :HG_HEAD_DIM] * ek[0]
            for s in range(1, N_SUB):
                dk_a = dk_a + dk_til[:, s * HG_HEAD_DIM:(s + 1) * HG_HEAD_DIM] * ek[s]
            db_att = qhb.astype(F32) * dq_hat - ktb.astype(F32) * dk_til
            db = dq_til * q_til - dk_hat * k_hat
            for s in range(N_SUB):
                db = db + db_att[:, s * HG_HEAD_DIM:(s + 1) * HG_HEAD_DIM]
            db_parts.append(jnp.where(is_last, db + db_last, db))
            dq_parts.append(dq_til * eb + dq_a)
            dk_parts.append(dk_hat * ekh + dk_a)
            dp_ref[:, 2 * D_HG + h * HG_HEAD_DIM:2 * D_HG + (h + 1) * HG_HEAD_DIM] = dv.astype(BF16)

        gg_ref[...] += gg
        db = jnp.concatenate(db_parts, axis=1)
        dq = jnp.concatenate(dq_parts, axis=1)
        dk = jnp.concatenate(dk_parts, axis=1)
        dlf = jnp.where(valid, jnp.dot(_tri(False), db, precision=HIGHEST, preferred_element_type=F32), 0.0)
        dp_ref[:, :D_HG] = (dq * sq * (1.0 + hq * (1.0 - sq))).astype(BF16)
        df = dlf / f - dk
        dlb = jnp.sum(df * (1.0 - sf), axis=0, keepdims=True) * lb * (1.0 - lb)
        glb_ref[0:1, :] += dlb
        glb_ref[1:2, :] += -dlb
        dp_ref[:, D_HG:2 * D_HG] = (df * (1.0 - lb) * sf * (1.0 - sf)).astype(BF16)

    rev = lambda j: pl.BlockSpec((rows, D_HG), lambda i: (n_steps - 1 - i, j))
    return pl.pallas_call(
        body, grid=(n_steps,),
        in_specs=[rev(2), rev(3), rev(4), rev(5), rev(0),
                  pl.BlockSpec((cps, D_HG, HG_HEAD_DIM), lambda i: (n_steps - 1 - i, 0, 0)), rev(1),
                  pl.BlockSpec((2, D_HG), lambda i: (0, 0)), pl.BlockSpec((1, HG_HEAD_DIM), lambda i: (0, 0))],
        out_specs=[pl.BlockSpec((rows, D_IN), lambda i: (n_steps - 1 - i, 0)),
                   pl.BlockSpec((2, D_HG), lambda i: (0, 0)), pl.BlockSpec((1, HG_HEAD_DIM), lambda i: (0, 0))],
        out_shape=[jax.ShapeDtypeStruct((T, D_IN), BF16), jax.ShapeDtypeStruct((2, D_HG), F32),
                   jax.ShapeDtypeStruct((1, HG_HEAD_DIM), F32)],
        scratch_shapes=[pltpu.VMEM((D_HG, HG_HEAD_DIM), F32)],
        name="hg_bwd", compiler_params=_params("arbitrary"),
    )(p, p, p, p, o_all, st_all, dy, lbraw, hg_g)


def _in_bwd(dp, w_in, h0, g1, dh1):
    T = h0.shape[0]
    tm = _row_tile(T, 416)
    n_steps = T // tm

    def body(dp_ref, w_ref, h_ref, g_ref, d1_ref, gx_hbm, gmeta_ref, gg_ref, buf, sems):
        i = pl.program_id(0)
        first, later = _window_copies(gx_hbm, buf, sems, tm)
        slot = i % 2

        @pl.when(i == 0)
        def _():
            gg_ref[...] = jnp.zeros_like(gg_ref)

        if n_steps > 2:
            @pl.when(i == 2)
            def _():
                first(False).wait()

            @pl.when(i > 2)
            def _():
                later(i - 2, slot, False).wait()

        du = _dot_nt(dp_ref[...], w_ref[...])
        h0_ = h_ref[...]
        r = _rms(h0_)
        n = h0_ * r
        gg_ref[...] += jnp.sum(du * n, axis=0, keepdims=True)
        dh0 = d1_ref[...] + _rms_bwd(du * g_ref[...], n, r)
        buf[slot] = dh0

        @pl.when(i == 0)
        def _():
            gmeta_ref[...] = dh0[PAD:HEAD, :]
            first(False).start()

        if n_steps > 1:
            @pl.when(i > 0)
            def _():
                later(i, slot, False).start()

        @pl.when(i == n_steps - 1)
        def _():
            if n_steps == 1:
                first(False).wait()
            else:
                if n_steps == 2:
                    first(False).wait()
                else:
                    later(i - 1, 1 - slot, False).wait()
                later(i, slot, False).wait()

    row = lambda n: pl.BlockSpec((tm, n), lambda i: (i, 0))
    return pl.pallas_call(
        body, grid=(n_steps,),
        in_specs=[row(D_IN), pl.BlockSpec((D_MODEL, D_IN), lambda i: (0, 0)),
                  row(D_MODEL), pl.BlockSpec((1, D_MODEL), lambda i: (0, 0)), row(D_MODEL)],
        out_specs=[pl.BlockSpec(memory_space=pl.ANY), pl.BlockSpec((N_META, D_MODEL), lambda i: (0, 0)),
                   pl.BlockSpec((1, D_MODEL), lambda i: (0, 0))],
        out_shape=[jax.ShapeDtypeStruct((T - HEAD, D_MODEL), F32), jax.ShapeDtypeStruct((N_META, D_MODEL), F32),
                   jax.ShapeDtypeStruct((1, D_MODEL), F32)],
        scratch_shapes=[pltpu.VMEM((2, tm, D_MODEL), F32), pltpu.SemaphoreType.DMA((2,))],
        name="in_bwd", compiler_params=_params("arbitrary"),
    )(dp, w_in, h0, g1, dh1)


def _col_tile(cols, target):
    best = None
    for t in range(128, min(cols, target) + 1, 128):
        if cols % t == 0:
            best = t
    assert best is not None, cols
    return best


MXU_DIM = 256


def _mxu_tile(cols, target):
    best = None
    for t in range(MXU_DIM, min(cols, target) + 1, MXU_DIM):
        if cols % t == 0:
            best = t
    assert best is not None, cols
    return best


def _weight_grad(a, b, name):
    T, M = a.shape
    N = b.shape[1]
    tm = _col_tile(M, 1408)
    tn = _mxu_tile(N, 768 if tm <= 1024 else 512)

    def body(a_ref, b_ref, o_ref):
        o_ref[...] = _dot_tn(a_ref[...], b_ref[...])

    return pl.pallas_call(
        body, grid=(M // tm, N // tn),
        in_specs=[pl.BlockSpec((T, tm), lambda m, n: (0, m)), pl.BlockSpec((T, tn), lambda m, n: (0, n))],
        out_specs=pl.BlockSpec((tm, tn), lambda m, n: (m, n)),
        out_shape=jax.ShapeDtypeStruct((M, N), F32),
        name=name, compiler_params=_params("parallel", "parallel"),
    )(a, b)


def _local_step(x, meta, target, w_in, w_out, w_gu, w_down, small, on_ffn_grads=None, on_mixer_grads=None):
    wg = _gate_weights(small["w_rgate"], small["w_igate"])
    bg = jnp.concatenate([small["b_rgate"], small["b_igate"]], axis=1)

    p, u, h0 = _in_proj(x, meta, small["mix_norm_g"], w_in)
    y_rg, hs = _rg_fwd(p, small["conv_w"], small["conv_b"], wg, bg, small["lru_lambda"], small["rg_norm_g"])
    y_hg, o_all, st_all = _hg_fwd(p, small["hg_lower_bound"], small["hg_norm_g"])
    h1, v, yb, gu, act, dh2, dh2b, loss, g_final = _ffn_fwd(
        h0, y_rg, y_hg, w_out, small["ffn_norm_g"], w_gu, w_down, small["final_norm_g"], target)

    g_w_down = _weight_grad(act, dh2b, "grad_w_down")
    dgu, dh1, dh1b, dy, g_ffn = _ffn_bwd(dh2b, gu, w_down, w_gu, h1, small["ffn_norm_g"], dh2, w_out)
    ffn_grads = {"w_gate_up": _weight_grad(v, dgu, "grad_w_gate_up"), "w_down": g_w_down}
    stages = on_ffn_grads(ffn_grads) if on_ffn_grads is not None else None
    dp, g_lb, g_hgn = _hg_bwd(p, o_all, st_all, dy, small["hg_lower_bound"], small["hg_norm_g"])
    early = late = None
    if stages is not None:
        chip_sums, send = stages
        sums = chip_sums()
        (dp, dy), sums = lax.optimization_barrier(((dp, dy), sums))
        early = send(sums)
    dp, g_cw, g_cb, g_wgate, g_bg, g_lam, g_rgn = _rg_bwd(
        p, hs, dy, dp, small["conv_w"], small["conv_b"], wg, bg, small["lru_lambda"], small["rg_norm_g"])
    mixer_grads = {"w_in": _weight_grad(u, dp, "grad_w_in"), "w_out": _weight_grad(yb, dh1b, "grad_w_out")}
    if on_mixer_grads is not None:
        chip_sums, send = on_mixer_grads(mixer_grads)
        sums = chip_sums()
        (dp, dh1), sums = lax.optimization_barrier(((dp, dh1), sums))
        late = send(sums)
    grad_x, g_meta, g_mix = _in_bwd(dp, w_in, h0, small["mix_norm_g"], dh1)

    grads = {
        "w_in": mixer_grads["w_in"], "w_out": mixer_grads["w_out"],
        "w_gate_up": ffn_grads["w_gate_up"], "w_down": ffn_grads["w_down"],
        "meta_tokens": g_meta, "mix_norm_g": g_mix, "conv_w": g_cw, "conv_b": g_cb, "w_gates": g_wgate,
        "b_rgate": g_bg[:, :D_RG], "b_igate": g_bg[:, D_RG:], "lru_lambda": g_lam, "rg_norm_g": g_rgn,
        "hg_lower_bound": g_lb, "hg_norm_g": g_hgn, "ffn_norm_g": g_ffn, "final_norm_g": g_final,
    }
    return loss, grad_x, grads, early, late


ANY = pl.BlockSpec(memory_space=pl.ANY)
HALF = D_MODEL // 2

BIG = {"w_in": (D_MODEL, D_IN // N_CHIPS, True), "w_gate_up": (D_MODEL, 2 * D_FF // N_CHIPS, True),
       "w_out": (D_MODEL // N_CHIPS, D_MODEL, False), "w_down": (D_FF // N_CHIPS, D_MODEL, False)}
BIG_NAMES = tuple(BIG)
N_BIG = len(BIG_NAMES)


def _full_shape(name):
    rows, cols, by_col = BIG[name]
    return (rows, cols * N_CHIPS) if by_col else (rows * N_CHIPS, cols)


def _place():
    return lax.axis_index("x"), lax.axis_index("y"), lax.axis_index("c")


def _chip_of(x, y, r):
    fx, fy = (r + 1) >> 1, (r + 1) & 1
    return (1 - x if fx else x), (1 - y if fy else y)


def _half_of(ref, by_col, half):
    start = pl.multiple_of(half * HALF, 128)
    return ref.at[pl.ds(start, HALF), :] if by_col else ref.at[:, pl.ds(start, HALF)]


def _shard_of(ref, name, chip):
    rows, cols, by_col = BIG[name]
    if by_col:
        return ref.at[:, pl.ds(pl.multiple_of(chip * cols, 128), cols)]
    return ref.at[pl.ds(pl.multiple_of(chip * rows, 16), rows), :]


def _shard_half_of(ref, name, chip, half):
    rows, cols, by_col = BIG[name]
    start = pl.multiple_of(half * HALF, 128)
    if by_col:
        return ref.at[pl.ds(start, HALF), pl.ds(pl.multiple_of(chip * cols, 128), cols)]
    return ref.at[pl.ds(pl.multiple_of(chip * rows, 16), rows), pl.ds(start, HALF)]


def _remote(src, dst, send_sems, recv_sems, k, dev):
    return pltpu.make_async_remote_copy(src_ref=src, dst_ref=dst, send_sem=send_sems.at[k], recv_sem=recv_sems.at[k],
                                        device_id=dev, device_id_type=MESH)


def _cast_into_full(w, chip):
    steps = 4
    in_specs, out_specs = [], []
    for name in BIG_NAMES:
        rows, cols, by_col = BIG[name]
        tr = rows // steps
        in_specs.append(pl.BlockSpec((tr, cols), lambda i, s: (i, 0)))
        if by_col:
            out_specs.append(pl.BlockSpec((tr, cols), lambda i, s: (i, s[0])))
        else:
            out_specs.append(pl.BlockSpec((tr, cols), lambda i, s: (s[0] * steps + i, 0)))

    def body(s_ref, *refs):
        for a in range(N_BIG):
            refs[N_BIG + a][...] = refs[a][...].astype(BF16)

    placed = pl.pallas_call(
        body,
        grid_spec=pltpu.PrefetchScalarGridSpec(num_scalar_prefetch=1, grid=(steps,), in_specs=in_specs,
                                               out_specs=out_specs),
        out_shape=[jax.ShapeDtypeStruct(_full_shape(name), BF16) for name in BIG_NAMES],
        name="place_shards", compiler_params=_params("parallel"),
    )(chip, *[w[name] for name in BIG_NAMES])
    return dict(zip(BIG_NAMES, placed))


def _gather_weights(placed, small, names, label, collective_id):
    n, ns = len(names), len(small)
    hbm = pltpu.MemorySpace.HBM
    outs = [jax.new_ref(placed[nm], memory_space=hbm) for nm in names]
    small_in = [jax.new_ref(s, memory_space=hbm) for s in small]
    small_out = [jax.empty_ref(jax.ShapeDtypeStruct((s.shape[0], s.shape[1] * N_CHIPS), F32), memory_space=hbm)
                 for s in small]
    n_sems = 6 * n + 3 * ns

    @pl.kernel(mesh=plsc.ScalarSubcoreMesh(axis_name="seq", num_cores=1), name=label, out_type=(),
               scratch_types=(pltpu.SemaphoreType.DMA((n_sems,)), pltpu.SemaphoreType.DMA((n_sems,)),
                              pltpu.SemaphoreType.DMA((max(ns, 1),))),
               compiler_params=pltpu.CompilerParams(collective_id=collective_id))
    def launch(send_sems, recv_sems, local_sems):
        x, y, c = _place()
        chip = 2 * x + y
        sibling = (x, y, 1 - c)
        others = [_chip_of(x, y, r) for r in range(3)]
        _handshake([(qx, qy, c) for qx, qy in others] + [sibling])

        def small_block(a, q):
            cols = small[a].shape[1]
            return small_out[a].at[:, pl.ds(pl.multiple_of(q * cols, 128), cols)]

        local = [pltpu.make_async_copy(small_in[a], small_block(a, chip), local_sems.at[a]) for a in range(ns)]
        for cp in local:
            cp.start()

        sends = []
        for a, name in enumerate(names):
            mine = _shard_half_of(outs[a], name, chip, c)
            for r, (qx, qy) in enumerate(others):
                sends.append(_remote(mine, mine, send_sems, recv_sems, 6 * a + r, (qx, qy, c)))
        for a in range(ns):
            for r, (qx, qy) in enumerate(others):
                sends.append(_remote(small_in[a], small_block(a, chip), send_sems, recv_sems,
                                     6 * n + 3 * a + r, (qx, qy, c)))
        for cp in sends:
            cp.start()

        forwards = []
        for a, name in enumerate(names):
            for r, (qx, qy) in enumerate(others):
                landed = _shard_half_of(outs[a], name, 2 * qx + qy, c)
                _remote(landed, landed, send_sems, recv_sems, 6 * a + r, (qx, qy, c)).wait_recv()
                fwd = _remote(landed, landed, send_sems, recv_sems, 6 * a + 3 + r, sibling)
                fwd.start()
                forwards.append(fwd)
        for a in range(ns):
            for r, (qx, qy) in enumerate(others):
                landed = small_block(a, 2 * qx + qy)
                _remote(landed, landed, send_sems, recv_sems, 6 * n + 3 * a + r, (qx, qy, c)).wait_recv()
        for a, name in enumerate(names):
            for r, (qx, qy) in enumerate(others):
                landed = _shard_half_of(outs[a], name, 2 * qx + qy, 1 - c)
                _remote(landed, landed, send_sems, recv_sems, 6 * a + 3 + r, sibling).wait_recv()
        for cp in sends + forwards:
            cp.wait_send()
        for cp in local:
            cp.wait()

    launch()
    return {nm: ref[...] for nm, ref in zip(names, outs)}, [ref[...] for ref in small_out]


def _exchange_halves(grads, names, label, collective_id):
    n = len(names)
    sequencer = collective_id is not None

    def body(*refs):
        ins, outs = refs[:n], refs[n:2 * n]
        send_sems, recv_sems = refs[2 * n:]
        x, y, c = _place()
        if sequencer:
            _handshake([(x, y, 1 - c)])
        copies = []
        for a, name in enumerate(names):
            copies.append(_remote(_half_of(ins[a], BIG[name][2], 1 - c), outs[a], send_sems, recv_sems, a,
                                  (x, y, 1 - c)))
        for cp in copies:
            cp.start()
        for cp in copies:
            cp.wait()

    def half_shape(name):
        r, c_ = _full_shape(name)
        return (HALF, c_) if BIG[name][2] else (r, HALF)

    out_type = tuple(jax.ShapeDtypeStruct(half_shape(nm), F32) for nm in names)
    sems = (pltpu.SemaphoreType.DMA((n,)), pltpu.SemaphoreType.DMA((n,)))
    operands = [grads[nm] for nm in names]
    if sequencer:
        got = pl.kernel(
            body, mesh=plsc.ScalarSubcoreMesh(axis_name="seq", num_cores=1), name=label, out_type=out_type,
            scratch_types=sems, compiler_params=pltpu.CompilerParams(collective_id=collective_id),
        )(*operands)
    else:
        got = pl.pallas_call(
            body, in_specs=[ANY] * n, out_specs=[ANY] * n, out_shape=list(out_type), scratch_shapes=list(sems),
            name=label,
        )(*operands)
    return dict(zip(names, got))


def _chip_sum(grads, got, names, core, label):
    n = len(names)
    steps = 4
    g_specs, blks = [], []
    for name in names:
        rows, cols = got[name].shape
        tr = rows // steps
        if BIG[name][2]:
            g_specs.append(pl.BlockSpec((tr, cols), lambda i, s: (s[0] * steps + i, 0)))
        else:
            g_specs.append(pl.BlockSpec((tr, HALF), lambda i, s: (i, s[0])))
        blks.append(pl.BlockSpec((tr, cols), lambda i, s: (i, 0)))

    def body(s_ref, *refs):
        for a in range(n):
            t = refs[a][...] + refs[n + a][...]
            refs[2 * n + a][...] = t
            refs[3 * n + a][...] = t.astype(BF16)

    out = pl.pallas_call(
        body,
        grid_spec=pltpu.PrefetchScalarGridSpec(num_scalar_prefetch=1, grid=(steps,), in_specs=g_specs + blks,
                                               out_specs=blks + blks),
        out_shape=([jax.ShapeDtypeStruct(got[nm].shape, F32) for nm in names]
                   + [jax.ShapeDtypeStruct(got[nm].shape, BF16) for nm in names]),
        name=label, compiler_params=_params("parallel"),
    )(core, *[grads[nm] for nm in names], *[got[nm] for nm in names])
    return {nm: (out[a], out[n + a]) for a, nm in enumerate(names)}


def _piece_shape(name):
    rows, cols, by_col = BIG[name]
    return (HALF, cols) if by_col else (rows, HALF)


def _handshake(peers):
    barrier = pltpu.get_barrier_semaphore()
    for peer in peers:
        pl.semaphore_signal(barrier, inc=1, device_id=peer, device_id_type=MESH)
    pl.semaphore_wait(barrier, len(peers))


def _send_chip_sums(sums, names, label, collective_id):
    n = len(names)

    def body(*refs):
        ins, outs = refs[:n], refs[n:2 * n]
        send_sems, recv_sems = refs[2 * n:]
        x, y, c = _place()
        others = [_chip_of(x, y, r) for r in range(3)]
        _handshake([(qx, qy, c) for qx, qy in others])
        copies = []
        for a, name in enumerate(names):
            for r, (qx, qy) in enumerate(others):
                copies.append(_remote(_shard_of(ins[a], name, 2 * qx + qy), outs[a].at[r], send_sems, recv_sems,
                                      3 * a + r, (qx, qy, c)))
        for cp in copies:
            cp.start()
        for cp in copies:
            cp.wait()

    return pl.kernel(
        body, mesh=plsc.ScalarSubcoreMesh(axis_name="seq", num_cores=1), name=label,
        out_type=tuple(jax.ShapeDtypeStruct((3,) + _piece_shape(nm), BF16) for nm in names),
        scratch_types=(pltpu.SemaphoreType.DMA((3 * n,)), pltpu.SemaphoreType.DMA((3 * n,))),
        compiler_params=pltpu.CompilerParams(collective_id=collective_id),
    )(*[sums[nm] for nm in names])


def _total(parts, chip_core):
    steps = 2
    in_specs, out_specs, operands = [], [], []
    for name in BIG_NAMES:
        by_col = BIG[name][2]
        pr, pc = _piece_shape(name)
        tr = pr // steps
        if by_col:
            in_specs.append(pl.BlockSpec((tr, pc), lambda i, s: (i, s[0])))
            out_specs.append(pl.BlockSpec((tr, pc), lambda i, s: (s[1] * steps + i, 0)))
        else:
            in_specs.append(pl.BlockSpec((tr, pc), lambda i, s: (s[0] * steps + i, 0)))
            out_specs.append(pl.BlockSpec((tr, pc), lambda i, s: (i, s[1])))
        for r in range(3):
            in_specs.append(pl.BlockSpec((None, tr, pc), lambda i, s, r=r: (r, i, 0)))
        own, got = parts[name]
        operands += [own, got, got, got]

    def body(s_ref, *refs):
        for a in range(N_BIG):
            o_ref, a_ref, b_ref, c_ref = refs[4 * a:4 * a + 4]
            refs[4 * N_BIG + a][...] = (((o_ref[...] + a_ref[...].astype(F32)) + b_ref[...].astype(F32))
                                        + c_ref[...].astype(F32))

    totals = pl.pallas_call(
        body,
        grid_spec=pltpu.PrefetchScalarGridSpec(num_scalar_prefetch=1, grid=(steps,), in_specs=in_specs,
                                               out_specs=out_specs),
        out_shape=[jax.ShapeDtypeStruct(BIG[name][:2], F32) for name in BIG_NAMES],
        name="totals", compiler_params=_params("parallel"),
    )(chip_core, *operands)
    return dict(zip(BIG_NAMES, totals))


def _share_totals(totals):
    def body(*refs):
        outs = refs[N_BIG:2 * N_BIG]
        send_sems, recv_sems = refs[2 * N_BIG:]
        x, y, c = _place()
        copies = []
        for a, name in enumerate(BIG_NAMES):
            mine = _half_of(outs[a], BIG[name][2], c)
            copies.append(_remote(mine, mine, send_sems, recv_sems, a, (x, y, 1 - c)))
        for cp in copies:
            cp.start()
        for a, name in enumerate(BIG_NAMES):
            theirs = _half_of(outs[a], BIG[name][2], 1 - c)
            _remote(theirs, theirs, send_sems, recv_sems, a, (x, y, 1 - c)).wait_recv()
        for cp in copies:
            cp.wait_send()

    return pl.pallas_call(
        body, in_specs=[ANY] * N_BIG, out_specs=[ANY] * N_BIG,
        out_shape=[jax.ShapeDtypeStruct(BIG[n][:2], F32) for n in BIG_NAMES],
        input_output_aliases={a: a for a in range(N_BIG)},
        scratch_shapes=[pltpu.SemaphoreType.DMA((N_BIG,)), pltpu.SemaphoreType.DMA((N_BIG,))],
        name="share_totals",
    )(*[totals[n] for n in BIG_NAMES])


VEC_ROWS = 32
VEC_ROW = {"mix_norm_g": 0, "conv_b": 1, "b_rgate": 2, "b_igate": 3, "lru_lambda": 4, "rg_norm_g": 5,
           "hg_lower_bound": 6, "hg_norm_g": 8, "ffn_norm_g": 9, "final_norm_g": 10, "loss": 11,
           "conv_w": 12, "meta_tokens": 16}
N_DEV = 8


def _all_reduce_small(pieces, gates):
    names = list(pieces)
    hv, hg = VEC_ROWS // 2, gates.shape[0] // 2

    def body(*refs):
        ins = refs[:len(names)]
        (g_ref, vec_ref, gsum_ref, mine_v, sib_v, sib_g, chip_v, chip_g, got_v, got_g,
         send_sems, recv_sems) = refs[len(names):]
        x, y, c = _place()
        chip = 2 * x + y
        sibling = (x, y, 1 - c)
        mine_v[...] = jnp.zeros_like(mine_v)
        for name, ref in zip(names, ins):
            nr, w = ref.shape
            mine_v[VEC_ROW[name]:VEC_ROW[name] + nr, 0:w] = ref[...]

        swap = [_remote(mine_v, sib_v, send_sems, recv_sems, 0, sibling),
                _remote(g_ref, sib_g, send_sems, recv_sems, 1, sibling)]
        for cp in swap:
            cp.start()
        for cp in swap:
            cp.wait()
        chip_v[...] = mine_v[...] + sib_v[...]
        chip_g[...] = g_ref[...] + sib_g[...]

        rows_v = pl.ds(pl.multiple_of(c * hv, 8), hv)
        rows_g = pl.ds(pl.multiple_of(c * hg, 8), hg)
        got_v[chip] = chip_v[rows_v, :]
        got_g[chip] = chip_g[rows_g, :]
        sends = []
        for r in range(3):
            qx, qy = _chip_of(x, y, r)
            sends.append(_remote(chip_v.at[rows_v, :], got_v.at[chip], send_sems, recv_sems, 2 + r, (qx, qy, c)))
            sends.append(_remote(chip_g.at[rows_g, :], got_g.at[chip], send_sems, recv_sems, 5 + r, (qx, qy, c)))
        for cp in sends:
            cp.start()
        for cp in sends:
            cp.wait()
        vec_ref[rows_v, :] = ((got_v[0] + got_v[1]) + got_v[2]) + got_v[3]
        gsum_ref[rows_g, :] = ((got_g[0] + got_g[1]) + got_g[2]) + got_g[3]

        back = [_remote(vec_ref.at[rows_v, :], vec_ref.at[rows_v, :], send_sems, recv_sems, 8, sibling),
                _remote(gsum_ref.at[rows_g, :], gsum_ref.at[rows_g, :], send_sems, recv_sems, 9, sibling)]
        for cp in back:
            cp.start()
        theirs_v = vec_ref.at[pl.ds(pl.multiple_of((1 - c) * hv, 8), hv), :]
        theirs_g = gsum_ref.at[pl.ds(pl.multiple_of((1 - c) * hg, 8), hg), :]
        _remote(theirs_v, theirs_v, send_sems, recv_sems, 8, sibling).wait_recv()
        _remote(theirs_g, theirs_g, send_sems, recv_sems, 9, sibling).wait_recv()
        for cp in back:
            cp.wait_send()

    vmem = pl.BlockSpec(memory_space=pltpu.VMEM)
    n_sems = 10
    return pl.pallas_call(
        body, in_specs=[vmem] * (len(names) + 1), out_specs=[vmem, vmem],
        out_shape=[jax.ShapeDtypeStruct((VEC_ROWS, D_MODEL), F32), jax.ShapeDtypeStruct(gates.shape, F32)],
        scratch_shapes=[pltpu.VMEM((VEC_ROWS, D_MODEL), F32), pltpu.VMEM((VEC_ROWS, D_MODEL), F32),
                        pltpu.VMEM(gates.shape, F32), pltpu.VMEM((VEC_ROWS, D_MODEL), F32),
                        pltpu.VMEM(gates.shape, F32), pltpu.VMEM((N_CHIPS, hv, D_MODEL), F32),
                        pltpu.VMEM((N_CHIPS, hg) + gates.shape[1:], F32),
                        pltpu.SemaphoreType.DMA((n_sems,)), pltpu.SemaphoreType.DMA((n_sems,))],
        name="all_reduce_small",
    )(*[pieces[n] for n in names], gates)


def _adamw_math(w, g, m, v):
    m = ADAM_B1 * m + (1.0 - ADAM_B1) * g
    v = ADAM_B2 * v + (1.0 - ADAM_B2) * (g * g)
    m_hat = m / (1.0 - ADAM_B1 ** ADAM_STEP)
    v_hat = v / (1.0 - ADAM_B2 ** ADAM_STEP)
    delta = -ADAM_LR * (m_hat / (jnp.sqrt(v_hat) + ADAM_EPS) + ADAM_WD * w)
    return delta, m, v


def _adamw_big(w, g, m, v):
    steps = 8
    blks = []
    for name in BIG_NAMES:
        rows, cols, _ = BIG[name]
        blks.append(pl.BlockSpec((rows // steps, cols), lambda i: (i, 0)))

    def body(*refs):
        ins, outs = refs[:4 * N_BIG], refs[4 * N_BIG:]
        for a in range(N_BIG):
            w_ref, g_ref, m_ref, v_ref = (ins[k * N_BIG + a] for k in range(4))
            d, nm, nv = _adamw_math(w_ref[...], g_ref[...], m_ref[...], v_ref[...])
            outs[a][...] = d
            outs[N_BIG + a][...] = nm
            outs[2 * N_BIG + a][...] = nv

    shapes = [jax.ShapeDtypeStruct(BIG[name][:2], F32) for name in BIG_NAMES]
    out = pl.pallas_call(
        body, grid=(steps,), in_specs=blks * 4, out_specs=blks * 3, out_shape=shapes * 3,
        name="adamw_big", compiler_params=_params("parallel"),
    )(*[t[name] for t in (w, g, m, v) for name in BIG_NAMES])
    return {name: (out[a], out[N_BIG + a], out[2 * N_BIG + a]) for a, name in enumerate(BIG_NAMES)}


SMALL = {"meta_tokens": (N_META, D_MODEL // N_CHIPS), "mix_norm_g": (1, D_MODEL), "conv_w": (CONV_W, D_RG // N_CHIPS),
         "conv_b": (1, D_RG), "w_rgate": (D_RG, RG_HEAD_DIM), "b_rgate": (1, D_RG), "w_igate": (D_RG, RG_HEAD_DIM),
         "b_igate": (1, D_RG), "lru_lambda": (1, D_RG), "rg_norm_g": (1, D_RG), "hg_lower_bound": (2, D_HG),
         "hg_norm_g": (1, HG_HEAD_DIM), "ffn_norm_g": (1, D_MODEL), "final_norm_g": (1, D_MODEL)}
SMALL_NAMES = tuple(SMALL)
SHARDED_SMALL = ("meta_tokens", "conv_w")


def _adamw_small(vec, gates, w, m, v):
    n = len(SMALL_NAMES)

    def body(*refs):
        vec_ref, gates_ref = refs[:2]
        w_refs, m_refs, v_refs = refs[2:2 + n], refs[2 + n:2 + 2 * n], refs[2 + 2 * n:2 + 3 * n]
        outs = refs[2 + 3 * n:]
        loss_ref = outs[0]
        x, y, _ = _place()
        chip = 2 * x + y
        loss_ref[...] = vec_ref[VEC_ROW["loss"]:VEC_ROW["loss"] + 1, 0:1]

        def update(k, g):
            g_ref, d_ref, nm_ref, nv_ref = outs[1 + 4 * k:5 + 4 * k]
            g_ref[...] = g
            d_ref[...], nm_ref[...], nv_ref[...] = _adamw_math(w_refs[k][...], g, m_refs[k][...], v_refs[k][...])

        for k, name in enumerate(SMALL_NAMES):
            nr, w_ = SMALL[name]
            if name == "w_rgate":
                update(k, gates_ref[0:D_RG, :])
            elif name == "w_igate":
                update(k, gates_ref[D_RG:2 * D_RG, :])
            elif name in SHARDED_SMALL:
                r0 = VEC_ROW[name]
                for q in range(N_CHIPS):
                    @pl.when(chip == q)
                    def _(k=k, r0=r0, nr=nr, w_=w_, q=q):
                        update(k, vec_ref[r0:r0 + nr, q * w_:(q + 1) * w_])
            else:
                r0 = VEC_ROW[name]
                update(k, vec_ref[r0:r0 + nr, 0:w_])

    vmem = pl.BlockSpec(memory_space=pltpu.VMEM)
    out_shape = [jax.ShapeDtypeStruct((1, 1), F32)]
    for name in SMALL_NAMES:
        out_shape += [jax.ShapeDtypeStruct(SMALL[name], F32)] * 4
    outs = pl.pallas_call(
        body, in_specs=[vmem] * (2 + 3 * n), out_specs=[vmem] * len(out_shape), out_shape=out_shape,
        name="adamw_small",
    )(vec, gates, *[w[k] for k in SMALL_NAMES], *[m[k] for k in SMALL_NAMES], *[v[k] for k in SMALL_NAMES])
    loss = outs[0]
    res = {name: tuple(outs[1 + 4 * k:5 + 4 * k]) for k, name in enumerate(SMALL_NAMES)}
    return loss, res


WEIGHT_NAMES = ("meta_tokens", "mix_norm_g", "w_in", "conv_w", "conv_b", "w_rgate", "b_rgate", "w_igate", "b_igate",
                "lru_lambda", "rg_norm_g", "hg_lower_bound", "hg_norm_g", "w_out", "ffn_norm_g", "w_gate_up", "w_down",
                "final_norm_g")


def _to_2d(name, a):
    if name in BIG:
        return a.reshape(BIG[name][:2])
    return a.reshape(SMALL[name])


def kernel(x, meta_tokens, mix_norm_g, w_in, conv_w, conv_b, w_rgate, b_rgate, w_igate, b_igate, lru_lambda, rg_norm_g, hg_lower_bound, hg_norm_g, w_out, ffn_norm_g, w_gate_up, w_down, final_norm_g, loss_target, m_meta_tokens, m_mix_norm_g, m_w_in, m_conv_w, m_conv_b, m_w_rgate, m_b_rgate, m_w_igate, m_b_igate, m_lru_lambda, m_rg_norm_g, m_hg_lower_bound, m_hg_norm_g, m_w_out, m_ffn_norm_g, m_w_gate_up, m_w_down, m_final_norm_g, v_meta_tokens, v_mix_norm_g, v_w_in, v_conv_w, v_conv_b, v_w_rgate, v_b_rgate, v_w_igate, v_b_igate, v_lru_lambda, v_rg_norm_g, v_hg_lower_bound, v_hg_norm_g, v_w_out, v_ffn_norm_g, v_w_gate_up, v_w_down, v_final_norm_g):
    w_raw = dict(zip(WEIGHT_NAMES, (meta_tokens, mix_norm_g, w_in, conv_w, conv_b, w_rgate, b_rgate, w_igate, b_igate,
                                    lru_lambda, rg_norm_g, hg_lower_bound, hg_norm_g, w_out, ffn_norm_g, w_gate_up,
                                    w_down, final_norm_g)))
    m_raw = dict(zip(WEIGHT_NAMES, (m_meta_tokens, m_mix_norm_g, m_w_in, m_conv_w, m_conv_b, m_w_rgate, m_b_rgate,
                                    m_w_igate, m_b_igate, m_lru_lambda, m_rg_norm_g, m_hg_lower_bound, m_hg_norm_g,
                                    m_w_out, m_ffn_norm_g, m_w_gate_up, m_w_down, m_final_norm_g)))
    v_raw = dict(zip(WEIGHT_NAMES, (v_meta_tokens, v_mix_norm_g, v_w_in, v_conv_w, v_conv_b, v_w_rgate, v_b_rgate,
                                    v_w_igate, v_b_igate, v_lru_lambda, v_rg_norm_g, v_hg_lower_bound, v_hg_norm_g,
                                    v_w_out, v_ffn_norm_g, v_w_gate_up, v_w_down, v_final_norm_g)))
    w = {k: _to_2d(k, a) for k, a in w_raw.items()}
    m = {k: _to_2d(k, a) for k, a in m_raw.items()}
    v = {k: _to_2d(k, a) for k, a in v_raw.items()}

    x_i, y_i, c_i = _place()
    core = jnp.reshape(c_i, (1,)).astype(jnp.int32)
    chip = jnp.reshape(2 * x_i + y_i, (1,)).astype(jnp.int32)
    chip_core = jnp.concatenate([chip, core])

    placed = _cast_into_full(w, chip)
    first, (meta_full, cw_full) = _gather_weights(placed, [w["meta_tokens"], w["conv_w"]], ("w_in",), "gather_first", 1)
    rest, _ = _gather_weights(placed, [], ("w_out", "w_gate_up", "w_down"), "gather_rest", 2)
    full = {**first, **rest}

    seq = x.shape[1]
    small ={k: w[k] for k in SMALL_NAMES if k not in SHARDED_SMALL}
    small["conv_w"] = cw_full

    def reduce_to_chips(grads, names, tag, collective_ids):
        got = _exchange_halves(grads, names, "exchange_halves_" + tag, collective_ids[0])

        def chip_sums():
            return _chip_sum(grads, got, names, core, "chip_sum_" + tag)

        def send(sums):
            arrived = _send_chip_sums({n: sums[n][1] for n in names}, names, "send_chip_sums_" + tag,
                                      collective_ids[1])
            return {n: (sums[n][0], a) for n, a in zip(names, arrived)}

        return chip_sums, send

    ffn_names, mixer_names = ("w_gate_up", "w_down"), ("w_in", "w_out")
    loss, grad_x, grads, parts, parts_mixer = _local_step(
        x.reshape(seq, D_MODEL), meta_full, loss_target.reshape(seq, D_MODEL),
        full["w_in"], full["w_out"], full["w_gate_up"], full["w_down"], small,
        on_ffn_grads=lambda g: reduce_to_chips(g, ffn_names, "ffn", (3, 4)),
        on_mixer_grads=lambda g: reduce_to_chips(g, mixer_names, "mixer", (None, 5)))
    parts.update(parts_mixer)
    totals = _total(parts, chip_core)
    g_big = dict(zip(BIG_NAMES, _share_totals(totals)))

    pieces = {k: grads[k] for k in VEC_ROW if k != "loss"}
    pieces["loss"] = loss
    vec, gates = _all_reduce_small(pieces, grads["w_gates"])
    loss_sum, res = _adamw_small(vec, gates, w, m, v)
    updates = _adamw_big(w, g_big, m, v)
    for n in BIG_NAMES:
        res[n] = (g_big[n],) + updates[n]

    out = [loss_sum.reshape(()), grad_x.reshape(1, seq, D_MODEL)]
    for j in range(4):
        out += [res[n][j].reshape(w_raw[n].shape) for n in WEIGHT_NAMES]
    return tuple(out)
```

```python
import functools
import math

import jax
import jax.numpy as jnp
from jax import lax
from jax.experimental import pallas as pl
from jax.experimental.pallas import tpu as pltpu
from jax.experimental.pallas import tpu_sc as plsc

F32 = jnp.float32
BF16 = jnp.bfloat16
HIGHEST = lax.Precision.HIGHEST
MESH = pl.DeviceIdType.MESH

D_MODEL = 1024
D_RG = 512
RG_HEAD_DIM = 64
D_HG = 512
HG_HEAD_DIM = 128
HG_HEADS = 4
CHUNK = 64
SUB = 16
N_SUB = CHUNK // SUB
N_META = 16
PAD = CHUNK - N_META
D_IN = 3072
D_FF = 2816
CONV_W = 4
LRU_C = 8.0
EPS = 1e-6
EXP_CLAMP = 80.0
GELU_C = math.sqrt(2.0 / math.pi)
GELU_A = 0.044715
N_CHIPS = 4

ADAM_LR = 0.001
ADAM_B1 = 0.9
ADAM_B2 = 0.999
ADAM_EPS = 1e-08
ADAM_WD = 0.01
ADAM_STEP = 10

VMEM_LIMIT = 56 * 1024 * 1024


def _params(*sem):
    return pltpu.CompilerParams(dimension_semantics=sem, vmem_limit_bytes=VMEM_LIMIT)


def _row_tile(rows, target):
    best = None
    for t in range(16, min(rows, target) + 1, 16):
        if rows % t == 0:
            best = t
    assert best is not None, rows
    return best


def _sigmoid(x):
    return 0.5 * jnp.tanh(0.5 * x) + 0.5


def _dot(a, b):
    return jnp.dot(a, b, preferred_element_type=F32)


def _dot_nt(a, b):
    return lax.dot_general(a, b, (((1,), (1,)), ((), ())), preferred_element_type=F32)


def _dot_tn(a, b):
    return lax.dot_general(a, b, (((0,), (0,)), ((), ())), preferred_element_type=F32)


def _rms(x):
    return lax.rsqrt(jnp.mean(x * x, axis=-1, keepdims=True) + EPS)


def _rms_bwd(dn, n, r):
    return r * (dn - n * jnp.mean(dn * n, axis=-1, keepdims=True))


def _gelu_parts(x):
    t = jnp.tanh(GELU_C * (x + GELU_A * x * x * x))
    g = 0.5 * x * (1.0 + t)
    dg = 0.5 * (1.0 + t) + 0.5 * x * (1.0 - t * t) * GELU_C * (1.0 + 3.0 * GELU_A * x * x)
    return g, dg


def _softplus_neg(lam):
    e = jnp.exp(-jnp.abs(lam))
    w = 1.0 + e
    log1p = jnp.where(w == 1.0, e, jnp.log(w) * e / (w - 1.0))
    return jnp.maximum(-lam, 0.0) + log1p


def _head_mask():
    r = lax.broadcasted_iota(jnp.int32, (D_RG, D_RG), 0) // RG_HEAD_DIM
    c = lax.broadcasted_iota(jnp.int32, (D_RG, D_RG), 1) // RG_HEAD_DIM
    return r == c


def _head_fold():
    r = lax.broadcasted_iota(jnp.int32, (D_RG, RG_HEAD_DIM), 0) % RG_HEAD_DIM
    c = lax.broadcasted_iota(jnp.int32, (D_RG, RG_HEAD_DIM), 1)
    return (r == c).astype(F32)


def _gate_weights(w_r, w_i):
    def body(wr_ref, wi_ref, o_ref):
        fold = _head_fold()
        mask = _head_mask()
        for k, ref in enumerate((wr_ref, wi_ref)):
            full = lax.dot_general(ref[...], fold, (((1,), (1,)), ((), ())),
                                   precision=HIGHEST, preferred_element_type=F32)
            o_ref[:, k * D_RG:(k + 1) * D_RG] = jnp.where(mask, full, 0.0).astype(BF16)

    return pl.pallas_call(
        body, out_shape=jax.ShapeDtypeStruct((D_RG, 2 * D_RG), BF16), name="gate_weights",
    )(w_r, w_i)


HEAD = PAD + N_META


def _window_copies(seq_hbm, buf, sems, tm):
    def first(to_vmem):
        seq, vm = seq_hbm.at[pl.ds(0, tm - HEAD)], buf.at[0, pl.ds(HEAD, tm - HEAD)]
        return pltpu.make_async_copy(seq, vm, sems.at[0]) if to_vmem else pltpu.make_async_copy(vm, seq, sems.at[0])

    def later(j, slot, to_vmem):
        seq, vm = seq_hbm.at[pl.ds(pl.multiple_of(j * tm - HEAD, 8), tm)], buf.at[slot]
        if to_vmem:
            return pltpu.make_async_copy(seq, vm, sems.at[slot])
        return pltpu.make_async_copy(vm, seq, sems.at[slot])

    return first, later


def _fetch_window(seq_hbm, buf, sems, i, n_steps, tm):
    first, later = _window_copies(seq_hbm, buf, sems, tm)
    slot = i % 2

    @pl.when(i == 0)
    def _():
        first(True).start()

    if n_steps > 1:
        @pl.when(i + 1 < n_steps)
        def _():
            later(i + 1, 1 - slot, True).start()

    @pl.when(i == 0)
    def _():
        first(True).wait()

    if n_steps > 1:
        @pl.when(i > 0)
        def _():
            later(i, slot, True).wait()

    return slot


def _in_proj_local(x, meta, g1, w_own, chip):
    T = x.shape[0] + HEAD
    tm = _row_tile(T, 416)
    n_steps = T // tm
    cols = BIG["w_in"][1]

    def body(s_ref, x_hbm, meta_ref, g_ref, w_ref, p_ref, u_ref, h_ref, buf, sems, wb):
        i = pl.program_id(0)
        slot = _fetch_window(x_hbm, buf, sems, i, n_steps, tm)

        @pl.when(i == 0)
        def _():
            buf[0, 0:PAD, :] = jnp.zeros((PAD, D_MODEL), F32)
            buf[0, PAD:HEAD, :] = meta_ref[...]
            wb[...] = w_ref[...].astype(BF16)

        h = buf[slot]
        h_ref[...] = h
        u = (h * _rms(h) * g_ref[...]).astype(BF16)
        u_ref[...] = u
        p_ref[...] = _dot(u, wb[...])

    return pl.pallas_call(
        body,
        grid_spec=pltpu.PrefetchScalarGridSpec(
            num_scalar_prefetch=1, grid=(n_steps,),
            in_specs=[pl.BlockSpec(memory_space=pl.ANY),
                      pl.BlockSpec((N_META, D_MODEL), lambda i, s: (0, 0)),
                      pl.BlockSpec((1, D_MODEL), lambda i, s: (0, 0)),
                      pl.BlockSpec((D_MODEL, cols), lambda i, s: (0, 0))],
            out_specs=[pl.BlockSpec((tm, cols), lambda i, s: (i, s[0])),
                       pl.BlockSpec((tm, D_MODEL), lambda i, s: (i, 0)),
                       pl.BlockSpec((tm, D_MODEL), lambda i, s: (i, 0))],
            scratch_shapes=[pltpu.VMEM((2, tm, D_MODEL), F32), pltpu.SemaphoreType.DMA((2,)),
                            pltpu.VMEM((D_MODEL, cols), BF16)]),
        out_shape=[jax.ShapeDtypeStruct((T, D_IN), F32), jax.ShapeDtypeStruct((T, D_MODEL), BF16),
                   jax.ShapeDtypeStruct((T, D_MODEL), F32)],
        name="in_proj_local", compiler_params=_params("arbitrary"),
    )(chip, x, meta, g1, w_own)


def _in_proj_rest(u, w_in, p, chip):
    T = u.shape[0]
    tm = _row_tile(T, 416)
    cols = BIG["w_in"][1]
    block = lambda j, s: (s[0] + 1 + j) % N_CHIPS

    def body(s_ref, u_ref, w_ref, p_in_ref, p_ref):
        p_ref[...] = _dot(u_ref[...], w_ref[...])

    return pl.pallas_call(
        body,
        grid_spec=pltpu.PrefetchScalarGridSpec(
            num_scalar_prefetch=1, grid=(N_CHIPS - 1, T // tm),
            in_specs=[pl.BlockSpec((tm, D_MODEL), lambda j, i, s: (i, 0)),
                      pl.BlockSpec((D_MODEL, cols), lambda j, i, s: (0, block(j, s))), ANY],
            out_specs=pl.BlockSpec((tm, cols), lambda j, i, s: (i, block(j, s)))),
        out_shape=jax.ShapeDtypeStruct((T, D_IN), F32),
        input_output_aliases={3: 0},
        name="in_proj_rest", compiler_params=_params("arbitrary", "arbitrary"),
    )(chip, u, w_in, p)


def _scan_block_fwd(A, B, rowi):
    for d in (1, 2, 4):
        a_sh = pltpu.roll(A, d, axis=0)
        b_sh = pltpu.roll(B, d, axis=0)
        m = rowi >= d
        B = jnp.where(m, A * b_sh + B, B)
        A = jnp.where(m, A * a_sh, A)
    return A, B


def _scan_block_bwd(A, B, rowi):
    for d in (1, 2, 4):
        a_sh = pltpu.roll(A, 8 - d, axis=0)
        b_sh = pltpu.roll(B, 8 - d, axis=0)
        m = rowi < 8 - d
        B = jnp.where(m, A * b_sh + B, B)
        A = jnp.where(m, A * a_sh, A)
    return A, B


def _rg_gates(xc, w_ref, bg_ref, lam):
    pre = _dot(xc.astype(BF16), w_ref[...]) + bg_ref[...]
    r = _sigmoid(pre[:, :D_RG])
    ig = _sigmoid(pre[:, D_RG:])
    sp = _softplus_neg(lam)
    la = -LRU_C * sp * r
    a = jnp.exp(la)
    th = jnp.tanh(la)
    u = 1.0 - th
    rc = pl.reciprocal(u, approx=True)
    rc = rc * (2.0 - u * rc)
    rc = rc * (2.0 - u * rc)
    m2 = -2.0 * th * rc
    inv_m = lax.rsqrt(jnp.maximum(m2, 1e-30))
    return r, ig, sp, a, m2 * inv_m, inv_m


def _conv(ext, cw_ref, cb_ref, tm):
    xc = cb_ref[...] + cw_ref[0:1, :] * ext[8 - 3:8 - 3 + tm, :]
    for j in range(1, CONV_W):
        xc = xc + cw_ref[j:j + 1, :] * ext[8 - 3 + j:8 - 3 + j + tm, :]
    return xc


def _scan_unroll(blocks):
    return 4 if blocks % 4 == 0 else 2 if blocks % 2 == 0 else 1


def _rg_fwd(p, cw, cb, wg, bg, lam, rg_g):
    T = p.shape[0]
    tm = _row_tile(T, 832)
    unroll = _scan_unroll(tm // 8)

    def body(xg_ref, cw_ref, cb_ref, w_ref, bg_ref, lam_ref, g_ref, y_ref, h_ref, ext, a_s, b_s, carry):
        i = pl.program_id(0)

        @pl.when(i == 0)
        def _():
            ext[0:8, :] = jnp.zeros((8, D_RG), F32)
            carry[...] = jnp.zeros((1, D_RG), F32)

        ext[8:8 + tm, :] = xg_ref[:, :D_RG]
        xc = _conv(ext, cw_ref, cb_ref, tm)
        r, ig, sp, a, m, _ = _rg_gates(xc, w_ref, bg_ref, lam_ref[...])
        row = i * tm + lax.broadcasted_iota(jnp.int32, (tm, 1), 0)
        a_s[...] = a
        b_s[...] = jnp.where(row >= PAD, m * ig * xc, 0.0)
        rowi = lax.broadcasted_iota(jnp.int32, (8, D_RG), 0)

        def blk(j, c):
            for u in range(unroll):
                o = pl.multiple_of((j * unroll + u) * 8, 8)
                A, B = _scan_block_fwd(a_s[pl.ds(o, 8), :], b_s[pl.ds(o, 8), :], rowi)
                h = B + A * c
                h_ref[pl.ds(o, 8), :] = h
                c = h[7:8, :]
            return c

        carry[...] = lax.fori_loop(0, tm // (8 * unroll), blk, carry[...])
        ext[0:8, :] = ext[tm:tm + 8, :]
        g, _ = _gelu_parts(xg_ref[:, D_RG:])
        yy = g * h_ref[...]
        y_ref[...] = (yy * _rms(yy) * g_ref[...]).astype(BF16)

    vec = lambda n: pl.BlockSpec((1, n), lambda i: (0, 0))
    return pl.pallas_call(
        body, grid=(T // tm,),
        in_specs=[pl.BlockSpec((tm, 2 * D_RG), lambda i: (i, 0)),
                  pl.BlockSpec((CONV_W, D_RG), lambda i: (0, 0)), vec(D_RG),
                  pl.BlockSpec((D_RG, 2 * D_RG), lambda i: (0, 0)), vec(2 * D_RG), vec(D_RG), vec(D_RG)],
        out_specs=[pl.BlockSpec((tm, D_RG), lambda i: (i, 0)), pl.BlockSpec((tm, D_RG), lambda i: (i, 0))],
        out_shape=[jax.ShapeDtypeStruct((T, D_RG), BF16), jax.ShapeDtypeStruct((T, D_RG), F32)],
        scratch_shapes=[pltpu.VMEM((tm + 8, D_RG), F32), pltpu.VMEM((tm, D_RG), F32),
                        pltpu.VMEM((tm, D_RG), F32), pltpu.VMEM((1, D_RG), F32)],
        name="rg_fwd", compiler_params=_params("arbitrary"),
    )(p, cw, cb, wg, bg, lam, rg_g)


def _tri(lower):
    r = lax.broadcasted_iota(jnp.int32, (CHUNK, CHUNK), 0)
    c = lax.broadcasted_iota(jnp.int32, (CHUNK, CHUNK), 1)
    return ((c <= r) if lower else (c >= r)).astype(F32)


def _hg_gates(hq, hf, lbraw_ref, valid):
    lb = _sigmoid(lbraw_ref[0:1, :] - lbraw_ref[1:2, :])
    sq = _sigmoid(hq)
    q = hq * sq
    sf = _sigmoid(hf)
    f = lb + (1.0 - lb) * sf
    lf = jnp.where(valid, jnp.log(f), 0.0)
    b = jnp.dot(_tri(True), lf, precision=HIGHEST, preferred_element_type=F32)
    return lb, sq, q, sf, f, b


def _hg_head(qh, kh, bh):
    blk = lax.broadcasted_iota(jnp.int32, (CHUNK, 1), 0) // SUB
    b_last = bh[CHUNK - 1:CHUNK, :]
    refs = [bh[SUB * s:SUB * s + 1, :] for s in range(N_SUB)]
    r_sel = refs[N_SUB - 1]
    for s in range(N_SUB - 2, -1, -1):
        r_sel = jnp.where(blk == s, refs[s], r_sel)
    eb = jnp.exp(bh)
    eq = jnp.exp(bh - r_sel)
    ekh = jnp.exp(b_last - bh)
    ek = [jnp.exp(jnp.minimum(refs[s] - bh, EXP_CLAMP)) for s in range(N_SUB)]
    qe = qh * eq
    q_hat = jnp.concatenate([jnp.where(blk == s, qe, 0.0) for s in range(N_SUB)], axis=1)
    k_til = jnp.concatenate([kh * ek[s] for s in range(N_SUB)], axis=1)
    return blk, b_last, eb, eq, ekh, ek, q_hat, k_til


def _causal():
    r = lax.broadcasted_iota(jnp.int32, (CHUNK, CHUNK), 0)
    c = lax.broadcasted_iota(jnp.int32, (CHUNK, CHUNK), 1)
    return r >= c


def _chunks_per_step(n_chunks):
    for c in (5, 4, 3, 2):
        if n_chunks % c == 0:
            return c
    return 1


def _hg_fwd(p, lbraw, hg_g):
    T = p.shape[0]
    n_chunks = T // CHUNK
    cps = _chunks_per_step(n_chunks)
    rows = cps * CHUNK

    def body(hq_ref, hf_ref, hi_ref, hg_ref, lb_ref, g_ref, y_ref, o_ref, st_all_ref, st):
        i = pl.program_id(0)

        @pl.when(i == 0)
        def _():
            st[...] = jnp.zeros_like(st)

        def chunk(j, carry):
            rs = pl.ds(pl.multiple_of(j * CHUNK, CHUNK), CHUNK)
            chunk_body(i * cps + j, hq_ref.at[rs, :], hf_ref.at[rs, :], hi_ref.at[rs, :], hg_ref.at[rs, :], lb_ref,
                       g_ref, y_ref.at[rs, :], o_ref.at[rs, :], st_all_ref.at[pl.ds(j, 1)], st)
            return carry

        lax.fori_loop(0, cps, chunk, 0, unroll=True)

    def chunk_body(n, hq_ref, hf_ref, hi_ref, hg_ref, lb_ref, g_ref, y_ref, o_ref, st_all_ref, st):
        valid = (n * CHUNK + lax.broadcasted_iota(jnp.int32, (CHUNK, 1), 0)) >= PAD
        hq, hf, v, hg = hq_ref[...], hf_ref[...], hi_ref[...], hg_ref[...]
        lb, sq, q, sf, f, b = _hg_gates(hq, hf, lb_ref, valid)
        k = 1.0 - f
        st_all_ref[0] = st[...]
        causal = _causal()
        v_t = v.T.astype(BF16)
        heads = [slice(h * HG_HEAD_DIM, (h + 1) * HG_HEAD_DIM) for h in range(HG_HEADS)]
        fac = []
        for sl in heads:
            qh, kh, bh = q[:, sl], k[:, sl], b[:, sl]
            _, b_last, eb, _, ekh, _, q_hat, k_til = _hg_head(qh, kh, bh)
            fac.append((jnp.exp(b_last), (qh * eb).astype(BF16), q_hat.astype(BF16), k_til.astype(BF16),
                        (kh * ekh).astype(BF16), v[:, sl].astype(BF16)))
        raw = []
        for sl, (_, q_til, q_hat, k_til, k_hat, _) in zip(heads, fac):
            st_h = st[sl, :]
            raw.append((_dot_nt(q_til, st_h.astype(BF16)), _dot_nt(q_hat, k_til), _dot(v_t[sl, :], k_hat), st_h))
        for sl, (e_last, _, _, _, _, vb), (inter, att, upd, st_h) in zip(heads, fac, raw):
            o = inter + _dot(jnp.where(causal, att, 0.0).astype(BF16), vb)
            st[sl, :] = st_h * e_last + upd
            o_ref[:, sl] = o
            hgh = hg[:, sl]
            y_ref[:, sl] = (o * _rms(o) * g_ref[...] * (hgh * _sigmoid(hgh))).astype(BF16)

    col = lambda j: pl.BlockSpec((rows, D_HG), lambda n: (n, j))
    return pl.pallas_call(
        body, grid=(n_chunks // cps,),
        in_specs=[col(2), col(3), col(4), col(5),
                  pl.BlockSpec((2, D_HG), lambda n: (0, 0)), pl.BlockSpec((1, HG_HEAD_DIM), lambda n: (0, 0))],
        out_specs=[pl.BlockSpec((rows, D_HG), lambda n: (n, 0)), pl.BlockSpec((rows, D_HG), lambda n: (n, 0)),
                   pl.BlockSpec((cps, D_HG, HG_HEAD_DIM), lambda n: (n, 0, 0))],
        out_shape=[jax.ShapeDtypeStruct((T, D_HG), BF16), jax.ShapeDtypeStruct((T, D_HG), F32),
                   jax.ShapeDtypeStruct((n_chunks, D_HG, HG_HEAD_DIM), F32)],
        scratch_shapes=[pltpu.VMEM((D_HG, HG_HEAD_DIM), F32)],
        name="hg_fwd", compiler_params=_params("arbitrary"),
    )(p, p, p, p, lbraw, hg_g)


def _ffn_fwd(h0, y_rg, y_hg, w_out, g2, w_gu, w_down, gf, target):
    T = h0.shape[0]
    tm = _row_tile(T, 320)
    n_steps = T // tm

    def body(h_ref, yr_ref, yh_ref, wo_ref, g2_ref, wgu_ref, wd_ref, gf_ref, t_hbm,
             h1_ref, v_ref, y_ref, gu_ref, act_ref, dh2_ref, dh2b_ref, loss_ref, gg_ref, tbuf, sems):
        i = pl.program_id(0)
        slot = _fetch_window(t_hbm, tbuf, sems, i, n_steps, tm)

        @pl.when(i == 0)
        def _():
            loss_ref[...] = jnp.zeros_like(loss_ref)
            gg_ref[...] = jnp.zeros_like(gg_ref)
            tbuf[0, 0:HEAD, :] = jnp.zeros((HEAD, D_MODEL), F32)

        y_ref[:, :D_RG] = yr_ref[...]
        y_ref[:, D_RG:] = yh_ref[...]
        h1 = h_ref[...] + _dot(y_ref[...], wo_ref[...])
        h1_ref[...] = h1
        v = (h1 * _rms(h1) * g2_ref[...]).astype(BF16)
        v_ref[...] = v

        gu = _dot(v, wgu_ref[...])
        gu_ref[...] = gu.astype(BF16)
        g = gu[:, :D_FF]
        act = (g * _sigmoid(g) * gu[:, D_FF:]).astype(BF16)
        act_ref[...] = act

        h2 = h1 + _dot(act, wd_ref[...])
        r = _rms(h2)
        n = h2 * r
        gf_ = gf_ref[...]
        row = i * tm + lax.broadcasted_iota(jnp.int32, (tm, 1), 0)
        err = jnp.where(row >= HEAD, n * gf_ - tbuf[slot], 0.0)
        loss_ref[...] += 0.5 * jnp.sum(jnp.mean(err * err, axis=-1, keepdims=True), axis=0, keepdims=True)
        dy = err * (1.0 / D_MODEL)
        gg_ref[...] += jnp.sum(dy * n, axis=0, keepdims=True)
        dh2 = _rms_bwd(dy * gf_, n, r)
        dh2_ref[...] = dh2
        dh2b_ref[...] = dh2.astype(BF16)

    row_spec = lambda n: pl.BlockSpec((tm, n), lambda i: (i, 0))
    vec = pl.BlockSpec((1, D_MODEL), lambda i: (0, 0))
    return pl.pallas_call(
        body, grid=(n_steps,),
        in_specs=[row_spec(D_MODEL), row_spec(D_RG), row_spec(D_HG), _resident((D_MODEL, D_MODEL)), vec,
                  _resident((D_MODEL, 2 * D_FF)), _resident((D_FF, D_MODEL)), vec,
                  pl.BlockSpec(memory_space=pl.ANY)],
        out_specs=[row_spec(D_MODEL), row_spec(D_MODEL), row_spec(D_MODEL), row_spec(2 * D_FF), row_spec(D_FF),
                   row_spec(D_MODEL), row_spec(D_MODEL), pl.BlockSpec((1, 1), lambda i: (0, 0)), vec],
        out_shape=[jax.ShapeDtypeStruct((T, D_MODEL), F32), jax.ShapeDtypeStruct((T, D_MODEL), BF16),
                   jax.ShapeDtypeStruct((T, D_MODEL), BF16), jax.ShapeDtypeStruct((T, 2 * D_FF), BF16),
                   jax.ShapeDtypeStruct((T, D_FF), BF16), jax.ShapeDtypeStruct((T, D_MODEL), F32),
                   jax.ShapeDtypeStruct((T, D_MODEL), BF16), jax.ShapeDtypeStruct((1, 1), F32),
                   jax.ShapeDtypeStruct((1, D_MODEL), F32)],
        scratch_shapes=[pltpu.VMEM((2, tm, D_MODEL), F32), pltpu.SemaphoreType.DMA((2,))],
        name="ffn_fwd", compiler_params=_params("arbitrary"),
    )(h0, y_rg, y_hg, w_out, g2, w_gu, w_down, gf, target)


def _resident(shape):
    return pl.BlockSpec(shape, lambda i: (0,) * len(shape), pipeline_mode=pl.Buffered(1))


def _ffn_bwd(dh2b, gu, w_down, w_gu, h1, g2, dh2, w_out):
    T = h1.shape[0]
    tm = _row_tile(T, 320)

    def body(d_ref, gu_ref, wd_ref, wgu_ref, h_ref, g_ref, d2_ref, wo_ref, dgu_ref, dh1_ref, dh1b_ref, dy_ref, gg_ref):
        i = pl.program_id(0)

        @pl.when(i == 0)
        def _():
            gg_ref[...] = jnp.zeros_like(gg_ref)

        dact = _dot_nt(d_ref[...], wd_ref[...]).astype(BF16)
        g = gu_ref[:, :D_FF]
        u = gu_ref[:, D_FF:]
        s = _sigmoid(g)
        dgu_ref[:, :D_FF] = dact * u * (s * (1.0 + g * (1.0 - s)))
        dgu_ref[:, D_FF:] = dact * (g * s)

        dv = _dot_nt(dgu_ref[...], wgu_ref[...])
        h1_ = h_ref[...]
        r = _rms(h1_)
        n = h1_ * r
        gg_ref[...] += jnp.sum(dv * n, axis=0, keepdims=True)
        dh1 = d2_ref[...] + _rms_bwd(dv * g_ref[...], n, r)
        dh1_ref[...] = dh1
        db = dh1.astype(BF16)
        dh1b_ref[...] = db
        dy_ref[...] = _dot_nt(db, wo_ref[...])

    row = lambda n: pl.BlockSpec((tm, n), lambda i: (i, 0))
    return pl.pallas_call(
        body, grid=(T // tm,),
        in_specs=[row(D_MODEL), row(2 * D_FF), _resident((D_FF, D_MODEL)), _resident((D_MODEL, 2 * D_FF)),
                  row(D_MODEL), pl.BlockSpec((1, D_MODEL), lambda i: (0, 0)), row(D_MODEL),
                  _resident((D_MODEL, D_MODEL))],
        out_specs=[row(2 * D_FF), row(D_MODEL), row(D_MODEL), row(D_MODEL),
                   pl.BlockSpec((1, D_MODEL), lambda i: (0, 0))],
        out_shape=[jax.ShapeDtypeStruct((T, 2 * D_FF), BF16), jax.ShapeDtypeStruct((T, D_MODEL), F32),
                   jax.ShapeDtypeStruct((T, D_MODEL), BF16), jax.ShapeDtypeStruct((T, D_MODEL), F32),
                   jax.ShapeDtypeStruct((1, D_MODEL), F32)],
        name="ffn_bwd", compiler_params=_params("arbitrary"),
    )(dh2b, gu, w_down, w_gu, h1, g2, dh2, w_out)


def _rg_bwd(p, hs, dy, dp, cw, cb, wg, bg, lam, rg_g):
    T = p.shape[0]
    tm = _row_tile(T, 832)
    nt = T // tm
    hb = tm // 8
    unroll = _scan_unroll(hb)

    def body(xg_ref, xh_ref, h_ref, hh_ref, dy_ref, dp_in_ref, cw_ref, cb_ref, w_ref, bg_ref, lam_ref, g_ref,
             dp_ref, gcw_ref, gcb_ref, gw_ref, gbg_ref, glam_ref, gg_ref,
             ext, dext, a_s, b_s, d_s, gacc, carry_d, carry_a):
        i = pl.program_id(0)
        t_idx = nt - 1 - i

        @pl.when(i == 0)
        def _():
            dext[tm:tm + 8, :] = jnp.zeros((8, D_RG), F32)
            carry_d[...] = jnp.zeros_like(carry_d)
            carry_a[...] = jnp.zeros_like(carry_a)
            gacc[...] = jnp.zeros_like(gacc)
            for ref in (gcw_ref, gcb_ref, gbg_ref, glam_ref, gg_ref, gw_ref):
                ref[...] = jnp.zeros_like(ref)

        first = t_idx == 0
        ext[0:8, :] = jnp.where(first, 0.0, xh_ref[:, :D_RG])
        ext[8:8 + tm, :] = xg_ref[:, :D_RG]
        xc = _conv(ext, cw_ref, cb_ref, tm)
        lam_ = lam_ref[...]
        r, ig, sp, a, m, inv_m = _rg_gates(xc, w_ref, bg_ref, lam_)
        row = t_idx * tm + lax.broadcasted_iota(jnp.int32, (tm, 1), 0)
        valid = row >= PAD

        gr = xg_ref[:, D_RG:]
        g, dgelu = _gelu_parts(gr)
        h = h_ref[...]
        yy = g * h
        rr = _rms(yy)
        nn = yy * rr
        dy_ = dy_ref[...]
        gg_ref[...] += jnp.sum(dy_ * nn, axis=0, keepdims=True)
        dyy = _rms_bwd(dy_ * g_ref[...], nn, rr)
        dp_ref[:, D_RG:] = (dyy * h * dgelu).astype(BF16)

        a_s[...] = a
        b_s[...] = dyy * g
        rowi = lax.broadcasted_iota(jnp.int32, (8, D_RG), 0)

        def blk(jj, c):
            cd, ca = c
            for u in range(unroll):
                o = pl.multiple_of((hb - 1 - (jj * unroll + u)) * 8, 8)
                a_blk = a_s[pl.ds(o, 8), :]
                a_next = jnp.where(rowi == 7, ca, pltpu.roll(a_blk, 7, axis=0))
                A, B = _scan_block_bwd(a_next, b_s[pl.ds(o, 8), :], rowi)
                d = B + A * cd
                d_s[pl.ds(o, 8), :] = d
                cd, ca = d[0:1, :], a_blk[0:1, :]
            return cd, ca

        cd, ca = lax.fori_loop(0, hb // unroll, blk, (carry_d[...], carry_a[...]))
        carry_d[...] = cd
        carry_a[...] = ca
        delta = d_s[...]

        h_last_prev = jnp.where(first, 0.0, hh_ref[7:8, :])
        row0 = lax.broadcasted_iota(jnp.int32, (tm, 1), 0) == 0
        h_prev = jnp.where(row0, h_last_prev, pltpu.roll(h, 1, axis=0))
        dbx = jnp.where(valid, delta, 0.0)
        da = delta * h_prev
        di = dbx * m * xc
        dm = dbx * ig * xc
        dla = a * (da - dm * a * inv_m)
        dla = jnp.where(valid, dla, 0.0)
        glam_ref[...] += jnp.sum(dla * r, axis=0, keepdims=True) * (LRU_C / (1.0 + jnp.exp(lam_)))
        dr = (-LRU_C) * sp * dla
        dpre = jnp.concatenate([dr * r * (1.0 - r), di * ig * (1.0 - ig)], axis=1)
        gbg_ref[...] += jnp.sum(dpre, axis=0, keepdims=True)
        dpre_b = dpre.astype(BF16)
        gacc[...] += _dot_tn(xc.astype(BF16), dpre_b)
        dxc = dbx * m * ig + _dot_nt(dpre_b, w_ref[...])
        gcb_ref[...] += jnp.sum(dxc, axis=0, keepdims=True)
        for j in range(CONV_W):
            gcw_ref[j:j + 1, :] += jnp.sum(dxc * ext[8 - 3 + j:8 - 3 + j + tm, :], axis=0, keepdims=True)
        dext[0:tm, :] = dxc
        dxr = cw_ref[0:1, :] * dext[3:3 + tm, :]
        for j in range(1, CONV_W):
            dxr = dxr + cw_ref[j:j + 1, :] * dext[3 - j:3 - j + tm, :]
        dp_ref[:, :D_RG] = dxr.astype(BF16)
        dext[tm:tm + 8, :] = dext[0:8, :]

        @pl.when(i == nt - 1)
        def _():
            fold = _head_fold()
            mask = _head_mask()
            for k in range(2):
                blockdiag = jnp.where(mask, gacc[:, k * D_RG:(k + 1) * D_RG], 0.0)
                gw_ref[k * D_RG:(k + 1) * D_RG, :] = jnp.dot(blockdiag, fold, precision=HIGHEST,
                                                             preferred_element_type=F32)

    vec = lambda n: pl.BlockSpec((1, n), lambda i: (0, 0))
    rev = lambda n: pl.BlockSpec((tm, n), lambda i: (nt - 1 - i, 0))
    halo = lambda n: pl.BlockSpec((8, n), lambda i: (jnp.maximum((nt - 1 - i) * hb - 1, 0), 0))
    return pl.pallas_call(
        body, grid=(nt,),
        in_specs=[rev(2 * D_RG), halo(2 * D_RG), rev(D_RG), halo(D_RG), rev(D_RG), ANY,
                  pl.BlockSpec((CONV_W, D_RG), lambda i: (0, 0)), vec(D_RG),
                  pl.BlockSpec((D_RG, 2 * D_RG), lambda i: (0, 0)), vec(2 * D_RG), vec(D_RG), vec(D_RG)],
        out_specs=[rev(2 * D_RG), pl.BlockSpec((CONV_W, D_RG), lambda i: (0, 0)), vec(D_RG),
                   pl.BlockSpec((2 * D_RG, RG_HEAD_DIM), lambda i: (0, 0)), vec(2 * D_RG), vec(D_RG), vec(D_RG)],
        input_output_aliases={5: 0},
        out_shape=[jax.ShapeDtypeStruct((T, D_IN), BF16), jax.ShapeDtypeStruct((CONV_W, D_RG), F32),
                   jax.ShapeDtypeStruct((1, D_RG), F32), jax.ShapeDtypeStruct((2 * D_RG, RG_HEAD_DIM), F32),
                   jax.ShapeDtypeStruct((1, 2 * D_RG), F32), jax.ShapeDtypeStruct((1, D_RG), F32),
                   jax.ShapeDtypeStruct((1, D_RG), F32)],
        scratch_shapes=[pltpu.VMEM((tm + 8, D_RG), F32), pltpu.VMEM((tm + 8, D_RG), F32),
                        pltpu.VMEM((tm, D_RG), F32), pltpu.VMEM((tm, D_RG), F32), pltpu.VMEM((tm, D_RG), F32),
                        pltpu.VMEM((D_RG, 2 * D_RG), F32), pltpu.VMEM((1, D_RG), F32), pltpu.VMEM((1, D_RG), F32)],
        name="rg_bwd", compiler_params=_params("arbitrary"),
    )(p, p, hs, hs, dy, dp, cw, cb, wg, bg, lam, rg_g)


def _hg_bwd(p, o_all, st_all, dy, lbraw, hg_g):
    T = p.shape[0]
    n_chunks = T // CHUNK
    cps = _chunks_per_step(n_chunks)
    rows = cps * CHUNK
    n_steps = n_chunks // cps

    def body(hq_ref, hf_ref, hi_ref, hg_ref, o_ref, st_ref, dy_ref, lb_ref, g_ref,
             dp_ref, glb_ref, gg_ref, dst):
        i = pl.program_id(0)

        @pl.when(i == 0)
        def _():
            dst[...] = jnp.zeros_like(dst)
            glb_ref[...] = jnp.zeros_like(glb_ref)
            gg_ref[...] = jnp.zeros_like(gg_ref)

        dp_ref[:, :2 * D_RG] = jnp.zeros((rows, 2 * D_RG), BF16)

        def chunk(jj, carry):
            j = cps - 1 - jj
            rs = pl.ds(pl.multiple_of(j * CHUNK, CHUNK), CHUNK)
            chunk_body((n_steps - 1 - i) * cps + j, hq_ref.at[rs, :], hf_ref.at[rs, :], hi_ref.at[rs, :],
                       hg_ref.at[rs, :], o_ref.at[rs, :], st_ref.at[pl.ds(j, 1)], dy_ref.at[rs, :], lb_ref, g_ref,
                       dp_ref.at[rs, pl.ds(2 * D_RG, 4 * D_HG)], glb_ref, gg_ref, dst)
            return carry

        lax.fori_loop(0, cps, chunk, 0, unroll=True)

    def chunk_body(n, hq_ref, hf_ref, hi_ref, hg_ref, o_ref, st_ref, dy_ref, lb_ref, g_ref,
                   dp_ref, glb_ref, gg_ref, dst):
        valid = (n * CHUNK + lax.broadcasted_iota(jnp.int32, (CHUNK, 1), 0)) >= PAD
        hq, hf, v, hg = hq_ref[...], hf_ref[...], hi_ref[...], hg_ref[...]
        lb, sq, q, sf, f, b = _hg_gates(hq, hf, lb_ref, valid)
        k = 1.0 - f
        causal = _causal()
        r_i = lax.broadcasted_iota(jnp.int32, (CHUNK, CHUNK), 0)
        c_i = lax.broadcasted_iota(jnp.int32, (CHUNK, CHUNK), 1)
        causal_t = r_i <= c_i
        is_last = lax.broadcasted_iota(jnp.int32, (CHUNK, 1), 0) == CHUNK - 1
        g_ = g_ref[...]
        db_parts, dq_parts, dk_parts = [], [], []
        gg = jnp.zeros((1, HG_HEAD_DIM), F32)
        heads = [slice(h * HG_HEAD_DIM, (h + 1) * HG_HEAD_DIM) for h in range(HG_HEADS)]

        do_parts = []
        for h, sl in enumerate(heads):
            o = o_ref[:, sl]
            ro = _rms(o)
            no = o * ro
            hgh = hg[:, sl]
            sg = _sigmoid(hgh)
            dyh = dy_ref[:, sl]
            dp_ref[:, 3 * D_HG + h * HG_HEAD_DIM:3 * D_HG + (h + 1) * HG_HEAD_DIM] = (
                dyh * no * g_ * sg * (1.0 + hgh * (1.0 - sg))).astype(BF16)
            dng = dyh * hgh * sg
            gg = gg + jnp.sum(dng * no, axis=0, keepdims=True)
            do_parts.append(_rms_bwd(dng * g_, no, ro))
        do_t = jnp.concatenate(do_parts, axis=1).T.astype(BF16)

        fac = []
        for sl, do in zip(heads, do_parts):
            qh, kh, bh = q[:, sl], k[:, sl], b[:, sl]
            blk, b_last, eb, eq, ekh, ek, q_hat, k_til = _hg_head(qh, kh, bh)
            fac.append(dict(qh=qh, kh=kh, blk=blk, e_last=jnp.exp(b_last), eb=eb, eq=eq, ekh=ekh, ek=ek,
                            q_til=qh * eb, k_hat=kh * ekh, qhb=q_hat.astype(BF16), ktb=k_til.astype(BF16),
                            vb=v[:, sl].astype(BF16), dob=do.astype(BF16)))

        first = []
        for sl, t in zip(heads, fac):
            st_h = st_ref[0, sl, :]
            dst_h = dst[sl, :]
            dstb = dst_h.astype(BF16)
            first.append(dict(
                att_t=_dot_nt(t["ktb"], t["qhb"]), datt=_dot_nt(t["dob"], t["vb"]),
                datt_t=_dot_nt(t["vb"], t["dob"]), dk_hat=_dot(t["vb"], dstb),
                dv=_dot_nt(t["k_hat"].astype(BF16), dstb), dq_til=_dot(t["dob"], st_h.astype(BF16)),
                state=t["e_last"] * jnp.sum(dst_h * st_h, axis=0, keepdims=True)))
            dst[sl, :] = dst_h * t["e_last"] + _dot(do_t[sl, :], t["q_til"].astype(BF16))

        for h, (t, m) in enumerate(zip(fac, first)):
            qh, kh, blk, eb, eq, ekh, ek = t["qh"], t["kh"], t["blk"], t["eb"], t["eq"], t["ekh"], t["ek"]
            q_til, k_hat, qhb, ktb, dob = t["q_til"], t["k_hat"], t["qhb"], t["ktb"], t["dob"]
            dk_hat, dq_til = m["dk_hat"], m["dq_til"]
            dv = m["dv"] + _dot(jnp.where(causal_t, m["att_t"], 0.0).astype(BF16), dob)
            dq_hat = _dot(jnp.where(causal, m["datt"], 0.0).astype(BF16), ktb)
            dk_til = _dot(jnp.where(causal_t, m["datt_t"], 0.0).astype(BF16), qhb)
            db_last = jnp.sum(dk_hat * k_hat, axis=0, keepdims=True) + m["state"]
            dq_sel = dq_hat[:, (N_SUB - 1) * HG_HEAD_DIM:]
            for s in range(N_SUB - 2, -1, -1):
                dq_sel = jnp.where(blk == s, dq_hat[:, s * HG_HEAD_DIM:(s + 1) * HG_HEAD_DIM], dq_sel)
            dq_a = dq_sel * eq
            dk_a = dk_til[:, :HG_HEAD_DIM] * ek[0]
            for s in range(1, N_SUB):
                dk_a = dk_a + dk_til[:, s * HG_HEAD_DIM:(s + 1) * HG_HEAD_DIM] * ek[s]
            db_att = qhb.astype(F32) * dq_hat - ktb.astype(F32) * dk_til
            db = dq_til * q_til - dk_hat * k_hat
            for s in range(N_SUB):
                db = db + db_att[:, s * HG_HEAD_DIM:(s + 1) * HG_HEAD_DIM]
            db_parts.append(jnp.where(is_last, db + db_last, db))
            dq_parts.append(dq_til * eb + dq_a)
            dk_parts.append(dk_hat * ekh + dk_a)
            dp_ref[:, 2 * D_HG + h * HG_HEAD_DIM:2 * D_HG + (h + 1) * HG_HEAD_DIM] = dv.astype(BF16)

        gg_ref[...] += gg
        db = jnp.concatenate(db_parts, axis=1)
        dq = jnp.concatenate(dq_parts, axis=1)
        dk = jnp.concatenate(dk_parts, axis=1)
        dlf = jnp.where(valid, jnp.dot(_tri(False), db, precision=HIGHEST, preferred_element_type=F32), 0.0)
        dp_ref[:, :D_HG] = (dq * sq * (1.0 + hq * (1.0 - sq))).astype(BF16)
        df = dlf / f - dk
        dlb = jnp.sum(df * (1.0 - sf), axis=0, keepdims=True) * lb * (1.0 - lb)
        glb_ref[0:1, :] += dlb
        glb_ref[1:2, :] += -dlb
        dp_ref[:, D_HG:2 * D_HG] = (df * (1.0 - lb) * sf * (1.0 - sf)).astype(BF16)

    rev = lambda j: pl.BlockSpec((rows, D_HG), lambda i: (n_steps - 1 - i, j))
    return pl.pallas_call(
        body, grid=(n_steps,),
        in_specs=[rev(2), rev(3), rev(4), rev(5), rev(0),
                  pl.BlockSpec((cps, D_HG, HG_HEAD_DIM), lambda i: (n_steps - 1 - i, 0, 0)), rev(1),
                  pl.BlockSpec((2, D_HG), lambda i: (0, 0)), pl.BlockSpec((1, HG_HEAD_DIM), lambda i: (0, 0))],
        out_specs=[pl.BlockSpec((rows, D_IN), lambda i: (n_steps - 1 - i, 0)),
                   pl.BlockSpec((2, D_HG), lambda i: (0, 0)), pl.BlockSpec((1, HG_HEAD_DIM), lambda i: (0, 0))],
        out_shape=[jax.ShapeDtypeStruct((T, D_IN), BF16), jax.ShapeDtypeStruct((2, D_HG), F32),
                   jax.ShapeDtypeStruct((1, HG_HEAD_DIM), F32)],
        scratch_shapes=[pltpu.VMEM((D_HG, HG_HEAD_DIM), F32)],
        name="hg_bwd", compiler_params=_params("arbitrary"),
    )(p, p, p, p, o_all, st_all, dy, lbraw, hg_g)


def _in_bwd(dp, w_in, h0, g1, dh1):
    T = h0.shape[0]
    tm = _row_tile(T, 416)
    n_steps = T // tm

    def body(dp_ref, w_ref, h_ref, g_ref, d1_ref, gx_hbm, gmeta_ref, gg_ref, buf, sems):
        i = pl.program_id(0)
        first, later = _window_copies(gx_hbm, buf, sems, tm)
        slot = i % 2

        @pl.when(i == 0)
        def _():
            gg_ref[...] = jnp.zeros_like(gg_ref)

        if n_steps > 2:
            @pl.when(i == 2)
            def _():
                first(False).wait()

            @pl.when(i > 2)
            def _():
                later(i - 2, slot, False).wait()

        du = _dot_nt(dp_ref[...], w_ref[...])
        h0_ = h_ref[...]
        r = _rms(h0_)
        n = h0_ * r
        gg_ref[...] += jnp.sum(du * n, axis=0, keepdims=True)
        dh0 = d1_ref[...] + _rms_bwd(du * g_ref[...], n, r)
        buf[slot] = dh0

        @pl.when(i == 0)
        def _():
            gmeta_ref[...] = dh0[PAD:HEAD, :]
            first(False).start()

        if n_steps > 1:
            @pl.when(i > 0)
            def _():
                later(i, slot, False).start()

        @pl.when(i == n_steps - 1)
        def _():
            if n_steps == 1:
                first(False).wait()
            else:
                if n_steps == 2:
                    first(False).wait()
                else:
                    later(i - 1, 1 - slot, False).wait()
                later(i, slot, False).wait()

    row = lambda n: pl.BlockSpec((tm, n), lambda i: (i, 0))
    return pl.pallas_call(
        body, grid=(n_steps,),
        in_specs=[row(D_IN), pl.BlockSpec((D_MODEL, D_IN), lambda i: (0, 0)),
                  row(D_MODEL), pl.BlockSpec((1, D_MODEL), lambda i: (0, 0)), row(D_MODEL)],
        out_specs=[pl.BlockSpec(memory_space=pl.ANY), pl.BlockSpec((N_META, D_MODEL), lambda i: (0, 0)),
                   pl.BlockSpec((1, D_MODEL), lambda i: (0, 0))],
        out_shape=[jax.ShapeDtypeStruct((T - HEAD, D_MODEL), F32), jax.ShapeDtypeStruct((N_META, D_MODEL), F32),
                   jax.ShapeDtypeStruct((1, D_MODEL), F32)],
        scratch_shapes=[pltpu.VMEM((2, tm, D_MODEL), F32), pltpu.SemaphoreType.DMA((2,))],
        name="in_bwd", compiler_params=_params("arbitrary"),
    )(dp, w_in, h0, g1, dh1)


def _col_tile(cols, target):
    best = None
    for t in range(128, min(cols, target) + 1, 128):
        if cols % t == 0:
            best = t
    assert best is not None, cols
    return best


MXU_DIM = 256


def _mxu_tile(cols, target):
    best = None
    for t in range(MXU_DIM, min(cols, target) + 1, MXU_DIM):
        if cols % t == 0:
            best = t
    assert best is not None, cols
    return best


def _weight_grad(a, b, name):
    T, M = a.shape
    N = b.shape[1]
    tm = _col_tile(M, 1408)
    tn = _mxu_tile(N, 768 if tm <= 1024 else 512)

    def body(a_ref, b_ref, o_ref):
        o_ref[...] = _dot_tn(a_ref[...], b_ref[...])

    return pl.pallas_call(
        body, grid=(M // tm, N // tn),
        in_specs=[pl.BlockSpec((T, tm), lambda m, n: (0, m)), pl.BlockSpec((T, tn), lambda m, n: (0, n))],
        out_specs=pl.BlockSpec((tm, tn), lambda m, n: (m, n)),
        out_shape=jax.ShapeDtypeStruct((M, N), F32),
        name=name, compiler_params=_params("parallel", "parallel"),
    )(a, b)


def _local_step(x, meta, target, w_in_own, w_in, w_out, w_gu, w_down, small, chip, on_ffn_grads=None,
                on_mixer_grads=None):
    wg = _gate_weights(small["w_rgate"], small["w_igate"])
    bg = jnp.concatenate([small["b_rgate"], small["b_igate"]], axis=1)

    p, u, h0 = _in_proj_local(x, meta, small["mix_norm_g"], w_in_own, chip)
    p = _in_proj_rest(u, w_in, p, chip)
    y_rg, hs = _rg_fwd(p, small["conv_w"], small["conv_b"], wg, bg, small["lru_lambda"], small["rg_norm_g"])
    y_hg, o_all, st_all = _hg_fwd(p, small["hg_lower_bound"], small["hg_norm_g"])
    h1, v, yb, gu, act, dh2, dh2b, loss, g_final = _ffn_fwd(
        h0, y_rg, y_hg, w_out, small["ffn_norm_g"], w_gu, w_down, small["final_norm_g"], target)

    g_w_down = _weight_grad(act, dh2b, "grad_w_down")
    dgu, dh1, dh1b, dy, g_ffn = _ffn_bwd(dh2b, gu, w_down, w_gu, h1, small["ffn_norm_g"], dh2, w_out)
    ffn_grads = {"w_gate_up": _weight_grad(v, dgu, "grad_w_gate_up"), "w_down": g_w_down}
    stages = on_ffn_grads(ffn_grads) if on_ffn_grads is not None else None
    dp, g_lb, g_hgn = _hg_bwd(p, o_all, st_all, dy, small["hg_lower_bound"], small["hg_norm_g"])
    early = late = None
    if stages is not None:
        chip_sums, send = stages
        sums = chip_sums()
        (dp, dy), sums = lax.optimization_barrier(((dp, dy), sums))
        early = send(sums)
    dp, g_cw, g_cb, g_wgate, g_bg, g_lam, g_rgn = _rg_bwd(
        p, hs, dy, dp, small["conv_w"], small["conv_b"], wg, bg, small["lru_lambda"], small["rg_norm_g"])
    mixer_grads = {"w_in": _weight_grad(u, dp, "grad_w_in"), "w_out": _weight_grad(yb, dh1b, "grad_w_out")}
    if on_mixer_grads is not None:
        chip_sums, send = on_mixer_grads(mixer_grads)
        sums = chip_sums()
        (dp, dh1), sums = lax.optimization_barrier(((dp, dh1), sums))
        late = send(sums)
    grad_x, g_meta, g_mix = _in_bwd(dp, w_in, h0, small["mix_norm_g"], dh1)

    grads = {
        "w_in": mixer_grads["w_in"], "w_out": mixer_grads["w_out"],
        "w_gate_up": ffn_grads["w_gate_up"], "w_down": ffn_grads["w_down"],
        "meta_tokens": g_meta, "mix_norm_g": g_mix, "conv_w": g_cw, "conv_b": g_cb, "w_gates": g_wgate,
        "b_rgate": g_bg[:, :D_RG], "b_igate": g_bg[:, D_RG:], "lru_lambda": g_lam, "rg_norm_g": g_rgn,
        "hg_lower_bound": g_lb, "hg_norm_g": g_hgn, "ffn_norm_g": g_ffn, "final_norm_g": g_final,
    }
    return loss, grad_x, grads, early, late


ANY = pl.BlockSpec(memory_space=pl.ANY)
HALF = D_MODEL // 2

BIG = {"w_in": (D_MODEL, D_IN // N_CHIPS, True), "w_gate_up": (D_MODEL, 2 * D_FF // N_CHIPS, True),
       "w_out": (D_MODEL // N_CHIPS, D_MODEL, False), "w_down": (D_FF // N_CHIPS, D_MODEL, False)}
BIG_NAMES = tuple(BIG)
N_BIG = len(BIG_NAMES)


def _full_shape(name):
    rows, cols, by_col = BIG[name]
    return (rows, cols * N_CHIPS) if by_col else (rows * N_CHIPS, cols)


def _place():
    return lax.axis_index("x"), lax.axis_index("y"), lax.axis_index("c")


def _chip_of(x, y, r):
    fx, fy = (r + 1) >> 1, (r + 1) & 1
    return (1 - x if fx else x), (1 - y if fy else y)


def _half_of(ref, by_col, half):
    start = pl.multiple_of(half * HALF, 128)
    return ref.at[pl.ds(start, HALF), :] if by_col else ref.at[:, pl.ds(start, HALF)]


def _shard_of(ref, name, chip):
    rows, cols, by_col = BIG[name]
    if by_col:
        return ref.at[:, pl.ds(pl.multiple_of(chip * cols, 128), cols)]
    return ref.at[pl.ds(pl.multiple_of(chip * rows, 16), rows), :]


def _shard_half_of(ref, name, chip, half):
    rows, cols, by_col = BIG[name]
    start = pl.multiple_of(half * HALF, 128)
    if by_col:
        return ref.at[pl.ds(start, HALF), pl.ds(pl.multiple_of(chip * cols, 128), cols)]
    return ref.at[pl.ds(pl.multiple_of(chip * rows, 16), rows), pl.ds(start, HALF)]


def _remote(src, dst, send_sems, recv_sems, k, dev):
    return pltpu.make_async_remote_copy(src_ref=src, dst_ref=dst, send_sem=send_sems.at[k], recv_sem=recv_sems.at[k],
                                        device_id=dev, device_id_type=MESH)


def _place_shards(w, small, chip):
    steps = 4
    ns = len(small)
    in_specs, out_specs = [], []
    for name in BIG_NAMES:
        rows, cols, by_col = BIG[name]
        tr = rows // steps
        in_specs.append(pl.BlockSpec((tr, cols), lambda i, s: (i, 0)))
        if by_col:
            out_specs.append(pl.BlockSpec((tr, cols), lambda i, s: (i, s[0])))
        else:
            out_specs.append(pl.BlockSpec((tr, cols), lambda i, s: (s[0] * steps + i, 0)))

    def body(s_ref, *refs):
        ins, small_in = refs[:N_BIG], refs[N_BIG:N_BIG + ns]
        outs, small_out = refs[N_BIG + ns:2 * N_BIG + ns], refs[2 * N_BIG + ns:2 * (N_BIG + ns)]
        send_sems, recv_sems, local_sems = refs[2 * (N_BIG + ns):]
        i = pl.program_id(0)
        x, y, c = _place()
        chip_ = 2 * x + y
        others = [_chip_of(x, y, r) for r in range(3)]

        def block(a, q):
            cols = small[a].shape[1]
            return small_out[a].at[:, pl.ds(pl.multiple_of(q * cols, 128), cols)]

        def local(a):
            return pltpu.make_async_copy(small_in[a], block(a, chip_), local_sems.at[a])

        def remote(a, r):
            qx, qy = others[r]
            return _remote(small_in[a], block(a, chip_), send_sems, recv_sems, 3 * a + r, (qx, qy, c))

        @pl.when(i == 0)
        def _():
            for a in range(ns):
                local(a).start()
                for r in range(3):
                    remote(a, r).start()

        for a in range(N_BIG):
            outs[a][...] = ins[a][...].astype(BF16)

        @pl.when(i == steps - 1)
        def _():
            for a in range(ns):
                for r, (qx, qy) in enumerate(others):
                    landed = block(a, 2 * qx + qy)
                    _remote(landed, landed, send_sems, recv_sems, 3 * a + r, (qx, qy, c)).wait_recv()
                for r in range(3):
                    remote(a, r).wait_send()
                local(a).wait()

    out = pl.pallas_call(
        body,
        grid_spec=pltpu.PrefetchScalarGridSpec(
            num_scalar_prefetch=1, grid=(steps,), in_specs=in_specs + [ANY] * ns, out_specs=out_specs + [ANY] * ns,
            scratch_shapes=[pltpu.SemaphoreType.DMA((3 * ns,)), pltpu.SemaphoreType.DMA((3 * ns,)),
                            pltpu.SemaphoreType.DMA((ns,))]),
        out_shape=([jax.ShapeDtypeStruct(_full_shape(name), BF16) for name in BIG_NAMES]
                   + [jax.ShapeDtypeStruct((s.shape[0], s.shape[1] * N_CHIPS), F32) for s in small]),
        name="place_shards", compiler_params=_params("arbitrary"),
    )(chip, *[w[name] for name in BIG_NAMES], *small)
    return dict(zip(BIG_NAMES, out[:N_BIG])), list(out[N_BIG:])


def _gather_weights(placed, small, names, label, collective_id):
    n, ns = len(names), len(small)
    hbm = pltpu.MemorySpace.HBM
    outs = [jax.new_ref(placed[nm], memory_space=hbm) for nm in names]
    small_in = [jax.new_ref(s, memory_space=hbm) for s in small]
    small_out = [jax.empty_ref(jax.ShapeDtypeStruct((s.shape[0], s.shape[1] * N_CHIPS), F32), memory_space=hbm)
                 for s in small]
    n_sems = 6 * n + 3 * ns

    @pl.kernel(mesh=plsc.ScalarSubcoreMesh(axis_name="seq", num_cores=1), name=label, out_type=(),
               scratch_types=(pltpu.SemaphoreType.DMA((n_sems,)), pltpu.SemaphoreType.DMA((n_sems,)),
                              pltpu.SemaphoreType.DMA((max(ns, 1),))),
               compiler_params=pltpu.CompilerParams(collective_id=collective_id))
    def launch(send_sems, recv_sems, local_sems):
        x, y, c = _place()
        chip = 2 * x + y
        sibling = (x, y, 1 - c)
        others = [_chip_of(x, y, r) for r in range(3)]
        _handshake([(qx, qy, c) for qx, qy in others] + [sibling])

        def small_block(a, q):
            cols = small[a].shape[1]
            return small_out[a].at[:, pl.ds(pl.multiple_of(q * cols, 128), cols)]

        local = [pltpu.make_async_copy(small_in[a], small_block(a, chip), local_sems.at[a]) for a in range(ns)]
        for cp in local:
            cp.start()

        sends = []
        for a, name in enumerate(names):
            mine = _shard_half_of(outs[a], name, chip, c)
            for r, (qx, qy) in enumerate(others):
                sends.append(_remote(mine, mine, send_sems, recv_sems, 6 * a + r, (qx, qy, c)))
        for a in range(ns):
            for r, (qx, qy) in enumerate(others):
                sends.append(_remote(small_in[a], small_block(a, chip), send_sems, recv_sems,
                                     6 * n + 3 * a + r, (qx, qy, c)))
        for cp in sends:
            cp.start()

        forwards = []
        for a, name in enumerate(names):
            for r, (qx, qy) in enumerate(others):
                landed = _shard_half_of(outs[a], name, 2 * qx + qy, c)
                _remote(landed, landed, send_sems, recv_sems, 6 * a + r, (qx, qy, c)).wait_recv()
                fwd = _remote(landed, landed, send_sems, recv_sems, 6 * a + 3 + r, sibling)
                fwd.start()
                forwards.append(fwd)
        for a in range(ns):
            for r, (qx, qy) in enumerate(others):
                landed = small_block(a, 2 * qx + qy)
                _remote(landed, landed, send_sems, recv_sems, 6 * n + 3 * a + r, (qx, qy, c)).wait_recv()
        for a, name in enumerate(names):
            for r, (qx, qy) in enumerate(others):
                landed = _shard_half_of(outs[a], name, 2 * qx + qy, 1 - c)
                _remote(landed, landed, send_sems, recv_sems, 6 * a + 3 + r, sibling).wait_recv()
        for cp in sends + forwards:
            cp.wait_send()
        for cp in local:
            cp.wait()

    launch()
    return {nm: ref[...] for nm, ref in zip(names, outs)}, [ref[...] for ref in small_out]


def _exchange_halves(grads, names, label, collective_id):
    n = len(names)
    sequencer = collective_id is not None

    def body(*refs):
        ins, outs = refs[:n], refs[n:2 * n]
        send_sems, recv_sems = refs[2 * n:]
        x, y, c = _place()
        if sequencer:
            _handshake([(x, y, 1 - c)])
        copies = []
        for a, name in enumerate(names):
            copies.append(_remote(_half_of(ins[a], BIG[name][2], 1 - c), outs[a], send_sems, recv_sems, a,
                                  (x, y, 1 - c)))
        for cp in copies:
            cp.start()
        for cp in copies:
            cp.wait()

    def half_shape(name):
        r, c_ = _full_shape(name)
        return (HALF, c_) if BIG[name][2] else (r, HALF)

    out_type = tuple(jax.ShapeDtypeStruct(half_shape(nm), F32) for nm in names)
    sems = (pltpu.SemaphoreType.DMA((n,)), pltpu.SemaphoreType.DMA((n,)))
    operands = [grads[nm] for nm in names]
    if sequencer:
        got = pl.kernel(
            body, mesh=plsc.ScalarSubcoreMesh(axis_name="seq", num_cores=1), name=label, out_type=out_type,
            scratch_types=sems, compiler_params=pltpu.CompilerParams(collective_id=collective_id),
        )(*operands)
    else:
        got = pl.pallas_call(
            body, in_specs=[ANY] * n, out_specs=[ANY] * n, out_shape=list(out_type), scratch_shapes=list(sems),
            name=label,
        )(*operands)
    return dict(zip(names, got))


def _chip_sum(grads, got, names, core, label):
    n = len(names)
    steps = 4
    g_specs, blks = [], []
    for name in names:
        rows, cols = got[name].shape
        tr = rows // steps
        if BIG[name][2]:
            g_specs.append(pl.BlockSpec((tr, cols), lambda i, s: (s[0] * steps + i, 0)))
        else:
            g_specs.append(pl.BlockSpec((tr, HALF), lambda i, s: (i, s[0])))
        blks.append(pl.BlockSpec((tr, cols), lambda i, s: (i, 0)))

    def body(s_ref, *refs):
        for a in range(n):
            t = refs[a][...] + refs[n + a][...]
            refs[2 * n + a][...] = t
            refs[3 * n + a][...] = t.astype(BF16)

    out = pl.pallas_call(
        body,
        grid_spec=pltpu.PrefetchScalarGridSpec(num_scalar_prefetch=1, grid=(steps,), in_specs=g_specs + blks,
                                               out_specs=blks + blks),
        out_shape=([jax.ShapeDtypeStruct(got[nm].shape, F32) for nm in names]
                   + [jax.ShapeDtypeStruct(got[nm].shape, BF16) for nm in names]),
        name=label, compiler_params=_params("parallel"),
    )(core, *[grads[nm] for nm in names], *[got[nm] for nm in names])
    return {nm: (out[a], out[n + a]) for a, nm in enumerate(names)}


def _piece_shape(name):
    rows, cols, by_col = BIG[name]
    return (HALF, cols) if by_col else (rows, HALF)


def _handshake(peers):
    barrier = pltpu.get_barrier_semaphore()
    for peer in peers:
        pl.semaphore_signal(barrier, inc=1, device_id=peer, device_id_type=MESH)
    pl.semaphore_wait(barrier, len(peers))


def _send_chip_sums(sums, names, label, collective_id):
    n = len(names)

    def body(*refs):
        ins, outs = refs[:n], refs[n:2 * n]
        send_sems, recv_sems = refs[2 * n:]
        x, y, c = _place()
        others = [_chip_of(x, y, r) for r in range(3)]
        _handshake([(qx, qy, c) for qx, qy in others])
        copies = []
        for a, name in enumerate(names):
            for r, (qx, qy) in enumerate(others):
                copies.append(_remote(_shard_of(ins[a], name, 2 * qx + qy), outs[a].at[r], send_sems, recv_sems,
                                      3 * a + r, (qx, qy, c)))
        for cp in copies:
            cp.start()
        for cp in copies:
            cp.wait()

    return pl.kernel(
        body, mesh=plsc.ScalarSubcoreMesh(axis_name="seq", num_cores=1), name=label,
        out_type=tuple(jax.ShapeDtypeStruct((3,) + _piece_shape(nm), BF16) for nm in names),
        scratch_types=(pltpu.SemaphoreType.DMA((3 * n,)), pltpu.SemaphoreType.DMA((3 * n,))),
        compiler_params=pltpu.CompilerParams(collective_id=collective_id),
    )(*[sums[nm] for nm in names])


def _total(parts, chip_core):
    steps = 2
    in_specs, out_specs, operands = [], [], []
    for name in BIG_NAMES:
        by_col = BIG[name][2]
        pr, pc = _piece_shape(name)
        tr = pr // steps
        if by_col:
            in_specs.append(pl.BlockSpec((tr, pc), lambda i, s: (i, s[0])))
            out_specs.append(pl.BlockSpec((tr, pc), lambda i, s: (s[1] * steps + i, 0)))
        else:
            in_specs.append(pl.BlockSpec((tr, pc), lambda i, s: (s[0] * steps + i, 0)))
            out_specs.append(pl.BlockSpec((tr, pc), lambda i, s: (i, s[1])))
        for r in range(3):
            in_specs.append(pl.BlockSpec((None, tr, pc), lambda i, s, r=r: (r, i, 0)))
        own, got = parts[name]
        operands += [own, got, got, got]

    def body(s_ref, *refs):
        for a in range(N_BIG):
            o_ref, a_ref, b_ref, c_ref = refs[4 * a:4 * a + 4]
            refs[4 * N_BIG + a][...] = (((o_ref[...] + a_ref[...].astype(F32)) + b_ref[...].astype(F32))
                                        + c_ref[...].astype(F32))

    totals = pl.pallas_call(
        body,
        grid_spec=pltpu.PrefetchScalarGridSpec(num_scalar_prefetch=1, grid=(steps,), in_specs=in_specs,
                                               out_specs=out_specs),
        out_shape=[jax.ShapeDtypeStruct(BIG[name][:2], F32) for name in BIG_NAMES],
        name="totals", compiler_params=_params("parallel"),
    )(chip_core, *operands)
    return dict(zip(BIG_NAMES, totals))


def _share_totals(totals):
    def body(*refs):
        outs = refs[N_BIG:2 * N_BIG]
        send_sems, recv_sems = refs[2 * N_BIG:]
        x, y, c = _place()
        copies = []
        for a, name in enumerate(BIG_NAMES):
            mine = _half_of(outs[a], BIG[name][2], c)
            copies.append(_remote(mine, mine, send_sems, recv_sems, a, (x, y, 1 - c)))
        for cp in copies:
            cp.start()
        for a, name in enumerate(BIG_NAMES):
            theirs = _half_of(outs[a], BIG[name][2], 1 - c)
            _remote(theirs, theirs, send_sems, recv_sems, a, (x, y, 1 - c)).wait_recv()
        for cp in copies:
            cp.wait_send()

    return pl.pallas_call(
        body, in_specs=[ANY] * N_BIG, out_specs=[ANY] * N_BIG,
        out_shape=[jax.ShapeDtypeStruct(BIG[n][:2], F32) for n in BIG_NAMES],
        input_output_aliases={a: a for a in range(N_BIG)},
        scratch_shapes=[pltpu.SemaphoreType.DMA((N_BIG,)), pltpu.SemaphoreType.DMA((N_BIG,))],
        name="share_totals",
    )(*[totals[n] for n in BIG_NAMES])


VEC_ROWS = 32
VEC_ROW = {"mix_norm_g": 0, "conv_b": 1, "b_rgate": 2, "b_igate": 3, "lru_lambda": 4, "rg_norm_g": 5,
           "hg_lower_bound": 6, "hg_norm_g": 8, "ffn_norm_g": 9, "final_norm_g": 10, "loss": 11,
           "conv_w": 12, "meta_tokens": 16}
N_DEV = 8


def _all_reduce_small(pieces, gates):
    names = list(pieces)
    hv, hg = VEC_ROWS // 2, gates.shape[0] // 2

    def body(*refs):
        ins = refs[:len(names)]
        (g_ref, vec_ref, gsum_ref, mine_v, sib_v, sib_g, chip_v, chip_g, got_v, got_g,
         send_sems, recv_sems) = refs[len(names):]
        x, y, c = _place()
        chip = 2 * x + y
        sibling = (x, y, 1 - c)
        mine_v[...] = jnp.zeros_like(mine_v)
        for name, ref in zip(names, ins):
            nr, w = ref.shape
            mine_v[VEC_ROW[name]:VEC_ROW[name] + nr, 0:w] = ref[...]

        swap = [_remote(mine_v, sib_v, send_sems, recv_sems, 0, sibling),
                _remote(g_ref, sib_g, send_sems, recv_sems, 1, sibling)]
        for cp in swap:
            cp.start()
        for cp in swap:
            cp.wait()
        chip_v[...] = mine_v[...] + sib_v[...]
        chip_g[...] = g_ref[...] + sib_g[...]

        rows_v = pl.ds(pl.multiple_of(c * hv, 8), hv)
        rows_g = pl.ds(pl.multiple_of(c * hg, 8), hg)
        got_v[chip] = chip_v[rows_v, :]
        got_g[chip] = chip_g[rows_g, :]
        sends = []
        for r in range(3):
            qx, qy = _chip_of(x, y, r)
            sends.append(_remote(chip_v.at[rows_v, :], got_v.at[chip], send_sems, recv_sems, 2 + r, (qx, qy, c)))
            sends.append(_remote(chip_g.at[rows_g, :], got_g.at[chip], send_sems, recv_sems, 5 + r, (qx, qy, c)))
        for cp in sends:
            cp.start()
        for cp in sends:
            cp.wait()
        vec_ref[rows_v, :] = ((got_v[0] + got_v[1]) + got_v[2]) + got_v[3]
        gsum_ref[rows_g, :] = ((got_g[0] + got_g[1]) + got_g[2]) + got_g[3]

        back = [_remote(vec_ref.at[rows_v, :], vec_ref.at[rows_v, :], send_sems, recv_sems, 8, sibling),
                _remote(gsum_ref.at[rows_g, :], gsum_ref.at[rows_g, :], send_sems, recv_sems, 9, sibling)]
        for cp in back:
            cp.start()
        theirs_v = vec_ref.at[pl.ds(pl.multiple_of((1 - c) * hv, 8), hv), :]
        theirs_g = gsum_ref.at[pl.ds(pl.multiple_of((1 - c) * hg, 8), hg), :]
        _remote(theirs_v, theirs_v, send_sems, recv_sems, 8, sibling).wait_recv()
        _remote(theirs_g, theirs_g, send_sems, recv_sems, 9, sibling).wait_recv()
        for cp in back:
            cp.wait_send()

    vmem = pl.BlockSpec(memory_space=pltpu.VMEM)
    n_sems = 10
    return pl.pallas_call(
        body, in_specs=[vmem] * (len(names) + 1), out_specs=[vmem, vmem],
        out_shape=[jax.ShapeDtypeStruct((VEC_ROWS, D_MODEL), F32), jax.ShapeDtypeStruct(gates.shape, F32)],
        scratch_shapes=[pltpu.VMEM((VEC_ROWS, D_MODEL), F32), pltpu.VMEM((VEC_ROWS, D_MODEL), F32),
                        pltpu.VMEM(gates.shape, F32), pltpu.VMEM((VEC_ROWS, D_MODEL), F32),
                        pltpu.VMEM(gates.shape, F32), pltpu.VMEM((N_CHIPS, hv, D_MODEL), F32),
                        pltpu.VMEM((N_CHIPS, hg) + gates.shape[1:], F32),
                        pltpu.SemaphoreType.DMA((n_sems,)), pltpu.SemaphoreType.DMA((n_sems,))],
        name="all_reduce_small",
    )(*[pieces[n] for n in names], gates)


def _adamw_math(w, g, m, v):
    m = ADAM_B1 * m + (1.0 - ADAM_B1) * g
    v = ADAM_B2 * v + (1.0 - ADAM_B2) * (g * g)
    m_hat = m / (1.0 - ADAM_B1 ** ADAM_STEP)
    v_hat = v / (1.0 - ADAM_B2 ** ADAM_STEP)
    delta = -ADAM_LR * (m_hat / (jnp.sqrt(v_hat) + ADAM_EPS) + ADAM_WD * w)
    return delta, m, v


def _adamw_big(w, g, m, v):
    steps = 8
    blks = []
    for name in BIG_NAMES:
        rows, cols, _ = BIG[name]
        blks.append(pl.BlockSpec((rows // steps, cols), lambda i: (i, 0)))

    def body(*refs):
        ins, outs = refs[:4 * N_BIG], refs[4 * N_BIG:]
        for a in range(N_BIG):
            w_ref, g_ref, m_ref, v_ref = (ins[k * N_BIG + a] for k in range(4))
            d, nm, nv = _adamw_math(w_ref[...], g_ref[...], m_ref[...], v_ref[...])
            outs[a][...] = d
            outs[N_BIG + a][...] = nm
            outs[2 * N_BIG + a][...] = nv

    shapes = [jax.ShapeDtypeStruct(BIG[name][:2], F32) for name in BIG_NAMES]
    out = pl.pallas_call(
        body, grid=(steps,), in_specs=blks * 4, out_specs=blks * 3, out_shape=shapes * 3,
        name="adamw_big", compiler_params=_params("parallel"),
    )(*[t[name] for t in (w, g, m, v) for name in BIG_NAMES])
    return {name: (out[a], out[N_BIG + a], out[2 * N_BIG + a]) for a, name in enumerate(BIG_NAMES)}


SMALL = {"meta_tokens": (N_META, D_MODEL // N_CHIPS), "mix_norm_g": (1, D_MODEL), "conv_w": (CONV_W, D_RG // N_CHIPS),
         "conv_b": (1, D_RG), "w_rgate": (D_RG, RG_HEAD_DIM), "b_rgate": (1, D_RG), "w_igate": (D_RG, RG_HEAD_DIM),
         "b_igate": (1, D_RG), "lru_lambda": (1, D_RG), "rg_norm_g": (1, D_RG), "hg_lower_bound": (2, D_HG),
         "hg_norm_g": (1, HG_HEAD_DIM), "ffn_norm_g": (1, D_MODEL), "final_norm_g": (1, D_MODEL)}
SMALL_NAMES = tuple(SMALL)
SHARDED_SMALL = ("meta_tokens", "conv_w")


def _adamw_small(vec, gates, w, m, v):
    n = len(SMALL_NAMES)

    def body(*refs):
        vec_ref, gates_ref = refs[:2]
        w_refs, m_refs, v_refs = refs[2:2 + n], refs[2 + n:2 + 2 * n], refs[2 + 2 * n:2 + 3 * n]
        outs = refs[2 + 3 * n:]
        loss_ref = outs[0]
        x, y, _ = _place()
        chip = 2 * x + y
        loss_ref[...] = vec_ref[VEC_ROW["loss"]:VEC_ROW["loss"] + 1, 0:1]

        def update(k, g):
            g_ref, d_ref, nm_ref, nv_ref = outs[1 + 4 * k:5 + 4 * k]
            g_ref[...] = g
            d_ref[...], nm_ref[...], nv_ref[...] = _adamw_math(w_refs[k][...], g, m_refs[k][...], v_refs[k][...])

        for k, name in enumerate(SMALL_NAMES):
            nr, w_ = SMALL[name]
            if name == "w_rgate":
                update(k, gates_ref[0:D_RG, :])
            elif name == "w_igate":
                update(k, gates_ref[D_RG:2 * D_RG, :])
            elif name in SHARDED_SMALL:
                r0 = VEC_ROW[name]
                for q in range(N_CHIPS):
                    @pl.when(chip == q)
                    def _(k=k, r0=r0, nr=nr, w_=w_, q=q):
                        update(k, vec_ref[r0:r0 + nr, q * w_:(q + 1) * w_])
            else:
                r0 = VEC_ROW[name]
                update(k, vec_ref[r0:r0 + nr, 0:w_])

    vmem = pl.BlockSpec(memory_space=pltpu.VMEM)
    out_shape = [jax.ShapeDtypeStruct((1, 1), F32)]
    for name in SMALL_NAMES:
        out_shape += [jax.ShapeDtypeStruct(SMALL[name], F32)] * 4
    outs = pl.pallas_call(
        body, in_specs=[vmem] * (2 + 3 * n), out_specs=[vmem] * len(out_shape), out_shape=out_shape,
        name="adamw_small",
    )(vec, gates, *[w[k] for k in SMALL_NAMES], *[m[k] for k in SMALL_NAMES], *[v[k] for k in SMALL_NAMES])
    loss = outs[0]
    res = {name: tuple(outs[1 + 4 * k:5 + 4 * k]) for k, name in enumerate(SMALL_NAMES)}
    return loss, res


WEIGHT_NAMES = ("meta_tokens", "mix_norm_g", "w_in", "conv_w", "conv_b", "w_rgate", "b_rgate", "w_igate", "b_igate",
                "lru_lambda", "rg_norm_g", "hg_lower_bound", "hg_norm_g", "w_out", "ffn_norm_g", "w_gate_up", "w_down",
                "final_norm_g")


def _to_2d(name, a):
    if name in BIG:
        return a.reshape(BIG[name][:2])
    return a.reshape(SMALL[name])


def kernel(x, meta_tokens, mix_norm_g, w_in, conv_w, conv_b, w_rgate, b_rgate, w_igate, b_igate, lru_lambda, rg_norm_g, hg_lower_bound, hg_norm_g, w_out, ffn_norm_g, w_gate_up, w_down, final_norm_g, loss_target, m_meta_tokens, m_mix_norm_g, m_w_in, m_conv_w, m_conv_b, m_w_rgate, m_b_rgate, m_w_igate, m_b_igate, m_lru_lambda, m_rg_norm_g, m_hg_lower_bound, m_hg_norm_g, m_w_out, m_ffn_norm_g, m_w_gate_up, m_w_down, m_final_norm_g, v_meta_tokens, v_mix_norm_g, v_w_in, v_conv_w, v_conv_b, v_w_rgate, v_b_rgate, v_w_igate, v_b_igate, v_lru_lambda, v_rg_norm_g, v_hg_lower_bound, v_hg_norm_g, v_w_out, v_ffn_norm_g, v_w_gate_up, v_w_down, v_final_norm_g):
    w_raw = dict(zip(WEIGHT_NAMES, (meta_tokens, mix_norm_g, w_in, conv_w, conv_b, w_rgate, b_rgate, w_igate, b_igate,
                                    lru_lambda, rg_norm_g, hg_lower_bound, hg_norm_g, w_out, ffn_norm_g, w_gate_up,
                                    w_down, final_norm_g)))
    m_raw = dict(zip(WEIGHT_NAMES, (m_meta_tokens, m_mix_norm_g, m_w_in, m_conv_w, m_conv_b, m_w_rgate, m_b_rgate,
                                    m_w_igate, m_b_igate, m_lru_lambda, m_rg_norm_g, m_hg_lower_bound, m_hg_norm_g,
                                    m_w_out, m_ffn_norm_g, m_w_gate_up, m_w_down, m_final_norm_g)))
    v_raw = dict(zip(WEIGHT_NAMES, (v_meta_tokens, v_mix_norm_g, v_w_in, v_conv_w, v_conv_b, v_w_rgate, v_b_rgate,
                                    v_w_igate, v_b_igate, v_lru_lambda, v_rg_norm_g, v_hg_lower_bound, v_hg_norm_g,
                                    v_w_out, v_ffn_norm_g, v_w_gate_up, v_w_down, v_final_norm_g)))
    w = {k: _to_2d(k, a) for k, a in w_raw.items()}
    m = {k: _to_2d(k, a) for k, a in m_raw.items()}
    v = {k: _to_2d(k, a) for k, a in v_raw.items()}

    x_i, y_i, c_i = _place()
    core = jnp.reshape(c_i, (1,)).astype(jnp.int32)
    chip = jnp.reshape(2 * x_i + y_i, (1,)).astype(jnp.int32)
    chip_core = jnp.concatenate([chip, core])

    placed, (meta_full, cw_full) = _place_shards(w, [w["meta_tokens"], w["conv_w"]], chip)
    first, _ = _gather_weights(placed, [], ("w_in",), "gather_first", 1)
    rest, _ = _gather_weights(placed, [], ("w_out", "w_gate_up", "w_down"), "gather_rest", 2)
    full = {**first, **rest}

    seq = x.shape[1]
    small ={k: w[k] for k in SMALL_NAMES if k not in SHARDED_SMALL}
    small["conv_w"] = cw_full

    def reduce_to_chips(grads, names, tag, collective_ids):
        got = _exchange_halves(grads, names, "exchange_halves_" + tag, collective_ids[0])

        def chip_sums():
            return _chip_sum(grads, got, names, core, "chip_sum_" + tag)

        def send(sums):
            arrived = _send_chip_sums({n: sums[n][1] for n in names}, names, "send_chip_sums_" + tag,
                                      collective_ids[1])
            return {n: (sums[n][0], a) for n, a in zip(names, arrived)}

        return chip_sums, send

    ffn_names, mixer_names = ("w_gate_up", "w_down"), ("w_in", "w_out")
    loss, grad_x, grads, parts, parts_mixer = _local_step(
        x.reshape(seq, D_MODEL), meta_full, loss_target.reshape(seq, D_MODEL),
        w["w_in"], full["w_in"], full["w_out"], full["w_gate_up"], full["w_down"], small, chip,
        on_ffn_grads=lambda g: reduce_to_chips(g, ffn_names, "ffn", (3, 4)),
        on_mixer_grads=lambda g: reduce_to_chips(g, mixer_names, "mixer", (None, 5)))
    parts.update(parts_mixer)
    totals = _total(parts, chip_core)
    g_big = dict(zip(BIG_NAMES, _share_totals(totals)))

    pieces = {k: grads[k] for k in VEC_ROW if k != "loss"}
    pieces["loss"] = loss
    vec, gates = _all_reduce_small(pieces, grads["w_gates"])
    loss_sum, res = _adamw_small(vec, gates, w, m, v)
    updates = _adamw_big(w, g_big, m, v)
    for n in BIG_NAMES:
        res[n] = (g_big[n],) + updates[n]

    out = [loss_sum.reshape(()), grad_x.reshape(1, seq, D_MODEL)]
    for j in range(4):
        out += [res[n][j].reshape(w_raw[n].shape) for n in WEIGHT_NAMES]
    return tuple(out)
```

```python
import functools
import math

import jax
import jax.numpy as jnp
from jax import lax
from jax.experimental import pallas as pl
from jax.experimental.pallas import tpu as pltpu
from jax.experimental.pallas import tpu_sc as plsc

F32 = jnp.float32
BF16 = jnp.bfloat16
HIGHEST = lax.Precision.HIGHEST
MESH = pl.DeviceIdType.MESH

D_MODEL = 1024
D_RG = 512
RG_HEAD_DIM = 64
D_HG = 512
HG_HEAD_DIM = 128
HG_HEADS = 4
CHUNK = 64
SUB = 16
N_SUB = CHUNK // SUB
N_META = 16
PAD = CHUNK - N_META
D_IN = 3072
D_FF = 2816
CONV_W = 4
LRU_C = 8.0
EPS = 1e-6
EXP_CLAMP = 80.0
GELU_C = math.sqrt(2.0 / math.pi)
GELU_A = 0.044715
N_CHIPS = 4

ADAM_LR = 0.001
ADAM_B1 = 0.9
ADAM_B2 = 0.999
ADAM_EPS = 1e-08
ADAM_WD = 0.01
ADAM_STEP = 10

VMEM_LIMIT = 56 * 1024 * 1024


def _params(*sem):
    return pltpu.CompilerParams(dimension_semantics=sem, vmem_limit_bytes=VMEM_LIMIT)


def _row_tile(rows, target):
    best = None
    for t in range(16, min(rows, target) + 1, 16):
        if rows % t == 0:
            best = t
    assert best is not None, rows
    return best


def _sigmoid(x):
    return 0.5 * jnp.tanh(0.5 * x) + 0.5


def _dot(a, b):
    return jnp.dot(a, b, preferred_element_type=F32)


def _dot_nt(a, b):
    return lax.dot_general(a, b, (((1,), (1,)), ((), ())), preferred_element_type=F32)


def _dot_tn(a, b):
    return lax.dot_general(a, b, (((0,), (0,)), ((), ())), preferred_element_type=F32)


def _rms(x):
    return lax.rsqrt(jnp.mean(x * x, axis=-1, keepdims=True) + EPS)


def _rms_bwd(dn, n, r):
    return r * (dn - n * jnp.mean(dn * n, axis=-1, keepdims=True))


def _gelu_parts(x):
    t = jnp.tanh(GELU_C * (x + GELU_A * x * x * x))
    g = 0.5 * x * (1.0 + t)
    dg = 0.5 * (1.0 + t) + 0.5 * x * (1.0 - t * t) * GELU_C * (1.0 + 3.0 * GELU_A * x * x)
    return g, dg


def _softplus_neg(lam):
    e = jnp.exp(-jnp.abs(lam))
    w = 1.0 + e
    log1p = jnp.where(w == 1.0, e, jnp.log(w) * e / (w - 1.0))
    return jnp.maximum(-lam, 0.0) + log1p


def _head_mask():
    r = lax.broadcasted_iota(jnp.int32, (D_RG, D_RG), 0) // RG_HEAD_DIM
    c = lax.broadcasted_iota(jnp.int32, (D_RG, D_RG), 1) // RG_HEAD_DIM
    return r == c


def _head_fold():
    r = lax.broadcasted_iota(jnp.int32, (D_RG, RG_HEAD_DIM), 0) % RG_HEAD_DIM
    c = lax.broadcasted_iota(jnp.int32, (D_RG, RG_HEAD_DIM), 1)
    return (r == c).astype(F32)


def _gate_weights(w_r, w_i):
    def body(wr_ref, wi_ref, o_ref):
        fold = _head_fold()
        mask = _head_mask()
        for k, ref in enumerate((wr_ref, wi_ref)):
            full = lax.dot_general(ref[...], fold, (((1,), (1,)), ((), ())),
                                   precision=HIGHEST, preferred_element_type=F32)
            o_ref[:, k * D_RG:(k + 1) * D_RG] = jnp.where(mask, full, 0.0).astype(BF16)

    return pl.pallas_call(
        body, out_shape=jax.ShapeDtypeStruct((D_RG, 2 * D_RG), BF16), name="gate_weights",
    )(w_r, w_i)


HEAD = PAD + N_META


def _window_copies(seq_hbm, buf, sems, tm):
    def first(to_vmem):
        seq, vm = seq_hbm.at[pl.ds(0, tm - HEAD)], buf.at[0, pl.ds(HEAD, tm - HEAD)]
        return pltpu.make_async_copy(seq, vm, sems.at[0]) if to_vmem else pltpu.make_async_copy(vm, seq, sems.at[0])

    def later(j, slot, to_vmem):
        seq, vm = seq_hbm.at[pl.ds(pl.multiple_of(j * tm - HEAD, 8), tm)], buf.at[slot]
        if to_vmem:
            return pltpu.make_async_copy(seq, vm, sems.at[slot])
        return pltpu.make_async_copy(vm, seq, sems.at[slot])

    return first, later


def _fetch_window(seq_hbm, buf, sems, i, n_steps, tm):
    first, later = _window_copies(seq_hbm, buf, sems, tm)
    slot = i % 2

    @pl.when(i == 0)
    def _():
        first(True).start()

    if n_steps > 1:
        @pl.when(i + 1 < n_steps)
        def _():
            later(i + 1, 1 - slot, True).start()

    @pl.when(i == 0)
    def _():
        first(True).wait()

    if n_steps > 1:
        @pl.when(i > 0)
        def _():
            later(i, slot, True).wait()

    return slot


def _in_proj_local(x, meta, g1, w_own, chip):
    T = x.shape[0] + HEAD
    tm = _row_tile(T, 832)
    n_steps = T // tm
    cols = BIG["w_in"][1]

    def body(s_ref, x_hbm, meta_ref, g_ref, w_ref, p_ref, u_ref, h_ref, buf, sems, wb):
        i = pl.program_id(0)
        slot = _fetch_window(x_hbm, buf, sems, i, n_steps, tm)

        @pl.when(i == 0)
        def _():
            buf[0, 0:PAD, :] = jnp.zeros((PAD, D_MODEL), F32)
            buf[0, PAD:HEAD, :] = meta_ref[...]
            wb[...] = w_ref[...].astype(BF16)

        h = buf[slot]
        h_ref[...] = h
        u = (h * _rms(h) * g_ref[...]).astype(BF16)
        u_ref[...] = u
        p_ref[...] = _dot(u, wb[...])

    return pl.pallas_call(
        body,
        grid_spec=pltpu.PrefetchScalarGridSpec(
            num_scalar_prefetch=1, grid=(n_steps,),
            in_specs=[pl.BlockSpec(memory_space=pl.ANY),
                      pl.BlockSpec((N_META, D_MODEL), lambda i, s: (0, 0)),
                      pl.BlockSpec((1, D_MODEL), lambda i, s: (0, 0)),
                      pl.BlockSpec((D_MODEL, cols), lambda i, s: (0, 0))],
            out_specs=[pl.BlockSpec((tm, cols), lambda i, s: (i, s[0])),
                       pl.BlockSpec((tm, D_MODEL), lambda i, s: (i, 0)),
                       pl.BlockSpec((tm, D_MODEL), lambda i, s: (i, 0))],
            scratch_shapes=[pltpu.VMEM((2, tm, D_MODEL), F32), pltpu.SemaphoreType.DMA((2,)),
                            pltpu.VMEM((D_MODEL, cols), BF16)]),
        out_shape=[jax.ShapeDtypeStruct((T, D_IN), F32), jax.ShapeDtypeStruct((T, D_MODEL), BF16),
                   jax.ShapeDtypeStruct((T, D_MODEL), F32)],
        name="in_proj_local", compiler_params=_params("arbitrary"),
    )(chip, x, meta, g1, w_own)


def _in_proj_rest(u, w_in, p, chip):
    T = u.shape[0]
    tm = _row_tile(T, 2080)
    cols = BIG["w_in"][1]
    block = lambda j, s: (s[0] + 1 + j) % N_CHIPS

    def body(s_ref, u_ref, w_ref, p_in_ref, p_ref):
        p_ref[...] = _dot(u_ref[...], w_ref[...])

    return pl.pallas_call(
        body,
        grid_spec=pltpu.PrefetchScalarGridSpec(
            num_scalar_prefetch=1, grid=(N_CHIPS - 1, T // tm),
            in_specs=[pl.BlockSpec((tm, D_MODEL), lambda j, i, s: (i, 0)),
                      pl.BlockSpec((D_MODEL, cols), lambda j, i, s: (0, block(j, s))), ANY],
            out_specs=pl.BlockSpec((tm, cols), lambda j, i, s: (i, block(j, s)))),
        out_shape=jax.ShapeDtypeStruct((T, D_IN), F32),
        input_output_aliases={3: 0},
        name="in_proj_rest", compiler_params=_params("arbitrary", "arbitrary"),
    )(chip, u, w_in, p)


def _scan_block_fwd(A, B, rowi):
    for d in (1, 2, 4):
        a_sh = pltpu.roll(A, d, axis=0)
        b_sh = pltpu.roll(B, d, axis=0)
        m = rowi >= d
        B = jnp.where(m, A * b_sh + B, B)
        A = jnp.where(m, A * a_sh, A)
    return A, B


def _scan_block_bwd(A, B, rowi):
    for d in (1, 2, 4):
        a_sh = pltpu.roll(A, 8 - d, axis=0)
        b_sh = pltpu.roll(B, 8 - d, axis=0)
        m = rowi < 8 - d
        B = jnp.where(m, A * b_sh + B, B)
        A = jnp.where(m, A * a_sh, A)
    return A, B


def _rg_gates(xc, w_ref, bg_ref, lam):
    pre = _dot(xc.astype(BF16), w_ref[...]) + bg_ref[...]
    r = _sigmoid(pre[:, :D_RG])
    ig = _sigmoid(pre[:, D_RG:])
    sp = _softplus_neg(lam)
    la = -LRU_C * sp * r
    a = jnp.exp(la)
    th = jnp.tanh(la)
    u = 1.0 - th
    rc = pl.reciprocal(u, approx=True)
    rc = rc * (2.0 - u * rc)
    rc = rc * (2.0 - u * rc)
    m2 = -2.0 * th * rc
    inv_m = lax.rsqrt(jnp.maximum(m2, 1e-30))
    return r, ig, sp, a, m2 * inv_m, inv_m


def _conv(ext, cw_ref, cb_ref, tm):
    xc = cb_ref[...] + cw_ref[0:1, :] * ext[8 - 3:8 - 3 + tm, :]
    for j in range(1, CONV_W):
        xc = xc + cw_ref[j:j + 1, :] * ext[8 - 3 + j:8 - 3 + j + tm, :]
    return xc


def _scan_unroll(blocks):
    return 4 if blocks % 4 == 0 else 2 if blocks % 2 == 0 else 1


def _rg_fwd(p, cw, cb, wg, bg, lam, rg_g):
    T = p.shape[0]
    tm = _row_tile(T, 832)
    unroll = _scan_unroll(tm // 8)

    def body(xg_ref, cw_ref, cb_ref, w_ref, bg_ref, lam_ref, g_ref, y_ref, h_ref, ext, a_s, b_s, carry):
        i = pl.program_id(0)

        @pl.when(i == 0)
        def _():
            ext[0:8, :] = jnp.zeros((8, D_RG), F32)
            carry[...] = jnp.zeros((1, D_RG), F32)

        ext[8:8 + tm, :] = xg_ref[:, :D_RG]
        xc = _conv(ext, cw_ref, cb_ref, tm)
        r, ig, sp, a, m, _ = _rg_gates(xc, w_ref, bg_ref, lam_ref[...])
        row = i * tm + lax.broadcasted_iota(jnp.int32, (tm, 1), 0)
        a_s[...] = a
        b_s[...] = jnp.where(row >= PAD, m * ig * xc, 0.0)
        rowi = lax.broadcasted_iota(jnp.int32, (8, D_RG), 0)

        def blk(j, c):
            for u in range(unroll):
                o = pl.multiple_of((j * unroll + u) * 8, 8)
                A, B = _scan_block_fwd(a_s[pl.ds(o, 8), :], b_s[pl.ds(o, 8), :], rowi)
                h = B + A * c
                h_ref[pl.ds(o, 8), :] = h
                c = h[7:8, :]
            return c

        carry[...] = lax.fori_loop(0, tm // (8 * unroll), blk, carry[...])
        ext[0:8, :] = ext[tm:tm + 8, :]
        g, _ = _gelu_parts(xg_ref[:, D_RG:])
        yy = g * h_ref[...]
        y_ref[...] = (yy * _rms(yy) * g_ref[...]).astype(BF16)

    vec = lambda n: pl.BlockSpec((1, n), lambda i: (0, 0))
    return pl.pallas_call(
        body, grid=(T // tm,),
        in_specs=[pl.BlockSpec((tm, 2 * D_RG), lambda i: (i, 0)),
                  pl.BlockSpec((CONV_W, D_RG), lambda i: (0, 0)), vec(D_RG),
                  pl.BlockSpec((D_RG, 2 * D_RG), lambda i: (0, 0)), vec(2 * D_RG), vec(D_RG), vec(D_RG)],
        out_specs=[pl.BlockSpec((tm, D_RG), lambda i: (i, 0)), pl.BlockSpec((tm, D_RG), lambda i: (i, 0))],
        out_shape=[jax.ShapeDtypeStruct((T, D_RG), BF16), jax.ShapeDtypeStruct((T, D_RG), F32)],
        scratch_shapes=[pltpu.VMEM((tm + 8, D_RG), F32), pltpu.VMEM((tm, D_RG), F32),
                        pltpu.VMEM((tm, D_RG), F32), pltpu.VMEM((1, D_RG), F32)],
        name="rg_fwd", compiler_params=_params("arbitrary"),
    )(p, cw, cb, wg, bg, lam, rg_g)


def _tri(lower):
    r = lax.broadcasted_iota(jnp.int32, (CHUNK, CHUNK), 0)
    c = lax.broadcasted_iota(jnp.int32, (CHUNK, CHUNK), 1)
    return ((c <= r) if lower else (c >= r)).astype(F32)


def _hg_gates(hq, hf, lbraw_ref, valid):
    lb = _sigmoid(lbraw_ref[0:1, :] - lbraw_ref[1:2, :])
    sq = _sigmoid(hq)
    q = hq * sq
    sf = _sigmoid(hf)
    f = lb + (1.0 - lb) * sf
    lf = jnp.where(valid, jnp.log(f), 0.0)
    b = jnp.dot(_tri(True), lf, precision=HIGHEST, preferred_element_type=F32)
    return lb, sq, q, sf, f, b


def _hg_head(qh, kh, bh):
    blk = lax.broadcasted_iota(jnp.int32, (CHUNK, 1), 0) // SUB
    b_last = bh[CHUNK - 1:CHUNK, :]
    refs = [bh[SUB * s:SUB * s + 1, :] for s in range(N_SUB)]
    r_sel = refs[N_SUB - 1]
    for s in range(N_SUB - 2, -1, -1):
        r_sel = jnp.where(blk == s, refs[s], r_sel)
    eb = jnp.exp(bh)
    eq = jnp.exp(bh - r_sel)
    ekh = jnp.exp(b_last - bh)
    ek = [jnp.exp(jnp.minimum(refs[s] - bh, EXP_CLAMP)) for s in range(N_SUB)]
    qe = qh * eq
    q_hat = jnp.concatenate([jnp.where(blk == s, qe, 0.0) for s in range(N_SUB)], axis=1)
    k_til = jnp.concatenate([kh * ek[s] for s in range(N_SUB)], axis=1)
    return blk, b_last, eb, eq, ekh, ek, q_hat, k_til


def _causal():
    r = lax.broadcasted_iota(jnp.int32, (CHUNK, CHUNK), 0)
    c = lax.broadcasted_iota(jnp.int32, (CHUNK, CHUNK), 1)
    return r >= c


def _chunks_per_step(n_chunks):
    for c in (5, 4, 3, 2):
        if n_chunks % c == 0:
            return c
    return 1


def _hg_fwd(p, lbraw, hg_g):
    T = p.shape[0]
    n_chunks = T // CHUNK
    cps = _chunks_per_step(n_chunks)
    rows = cps * CHUNK

    def body(hq_ref, hf_ref, hi_ref, hg_ref, lb_ref, g_ref, y_ref, o_ref, st_all_ref, st):
        i = pl.program_id(0)

        @pl.when(i == 0)
        def _():
            st[...] = jnp.zeros_like(st)

        def chunk(j, carry):
            rs = pl.ds(pl.multiple_of(j * CHUNK, CHUNK), CHUNK)
            chunk_body(i * cps + j, hq_ref.at[rs, :], hf_ref.at[rs, :], hi_ref.at[rs, :], hg_ref.at[rs, :], lb_ref,
                       g_ref, y_ref.at[rs, :], o_ref.at[rs, :], st_all_ref.at[pl.ds(j, 1)], st)
            return carry

        lax.fori_loop(0, cps, chunk, 0, unroll=True)

    def chunk_body(n, hq_ref, hf_ref, hi_ref, hg_ref, lb_ref, g_ref, y_ref, o_ref, st_all_ref, st):
        valid = (n * CHUNK + lax.broadcasted_iota(jnp.int32, (CHUNK, 1), 0)) >= PAD
        hq, hf, v, hg = hq_ref[...], hf_ref[...], hi_ref[...], hg_ref[...]
        lb, sq, q, sf, f, b = _hg_gates(hq, hf, lb_ref, valid)
        k = 1.0 - f
        st_all_ref[0] = st[...]
        causal = _causal()
        v_t = v.T.astype(BF16)
        heads = [slice(h * HG_HEAD_DIM, (h + 1) * HG_HEAD_DIM) for h in range(HG_HEADS)]
        fac = []
        for sl in heads:
            qh, kh, bh = q[:, sl], k[:, sl], b[:, sl]
            _, b_last, eb, _, ekh, _, q_hat, k_til = _hg_head(qh, kh, bh)
            fac.append((jnp.exp(b_last), (qh * eb).astype(BF16), q_hat.astype(BF16), k_til.astype(BF16),
                        (kh * ekh).astype(BF16), v[:, sl].astype(BF16)))
        raw = []
        for sl, (_, q_til, q_hat, k_til, k_hat, _) in zip(heads, fac):
            st_h = st[sl, :]
            raw.append((_dot_nt(q_til, st_h.astype(BF16)), _dot_nt(q_hat, k_til), _dot(v_t[sl, :], k_hat), st_h))
        for sl, (e_last, _, _, _, _, vb), (inter, att, upd, st_h) in zip(heads, fac, raw):
            o = inter + _dot(jnp.where(causal, att, 0.0).astype(BF16), vb)
            st[sl, :] = st_h * e_last + upd
            o_ref[:, sl] = o
            hgh = hg[:, sl]
            y_ref[:, sl] = (o * _rms(o) * g_ref[...] * (hgh * _sigmoid(hgh))).astype(BF16)

    col = lambda j: pl.BlockSpec((rows, D_HG), lambda n: (n, j))
    return pl.pallas_call(
        body, grid=(n_chunks // cps,),
        in_specs=[col(2), col(3), col(4), col(5),
                  pl.BlockSpec((2, D_HG), lambda n: (0, 0)), pl.BlockSpec((1, HG_HEAD_DIM), lambda n: (0, 0))],
        out_specs=[pl.BlockSpec((rows, D_HG), lambda n: (n, 0)), pl.BlockSpec((rows, D_HG), lambda n: (n, 0)),
                   pl.BlockSpec((cps, D_HG, HG_HEAD_DIM), lambda n: (n, 0, 0))],
        out_shape=[jax.ShapeDtypeStruct((T, D_HG), BF16), jax.ShapeDtypeStruct((T, D_HG), F32),
                   jax.ShapeDtypeStruct((n_chunks, D_HG, HG_HEAD_DIM), F32)],
        scratch_shapes=[pltpu.VMEM((D_HG, HG_HEAD_DIM), F32)],
        name="hg_fwd", compiler_params=_params("arbitrary"),
    )(p, p, p, p, lbraw, hg_g)


def _ffn_fwd(h0, y_rg, y_hg, w_out, g2, w_gu, w_down, gf, target):
    T = h0.shape[0]
    tm = _row_tile(T, 320)
    n_steps = T // tm

    def body(h_ref, yr_ref, yh_ref, wo_ref, g2_ref, wgu_ref, wd_ref, gf_ref, t_hbm,
             h1_ref, v_ref, y_ref, gu_ref, act_ref, dh2_ref, dh2b_ref, loss_ref, gg_ref, tbuf, sems):
        i = pl.program_id(0)
        slot = _fetch_window(t_hbm, tbuf, sems, i, n_steps, tm)

        @pl.when(i == 0)
        def _():
            loss_ref[...] = jnp.zeros_like(loss_ref)
            gg_ref[...] = jnp.zeros_like(gg_ref)
            tbuf[0, 0:HEAD, :] = jnp.zeros((HEAD, D_MODEL), F32)

        y_ref[:, :D_RG] = yr_ref[...]
        y_ref[:, D_RG:] = yh_ref[...]
        h1 = h_ref[...] + _dot(y_ref[...], wo_ref[...])
        h1_ref[...] = h1
        v = (h1 * _rms(h1) * g2_ref[...]).astype(BF16)
        v_ref[...] = v

        gu = _dot(v, wgu_ref[...])
        gu_ref[...] = gu.astype(BF16)
        g = gu[:, :D_FF]
        act = (g * _sigmoid(g) * gu[:, D_FF:]).astype(BF16)
        act_ref[...] = act

        h2 = h1 + _dot(act, wd_ref[...])
        r = _rms(h2)
        n = h2 * r
        gf_ = gf_ref[...]
        row = i * tm + lax.broadcasted_iota(jnp.int32, (tm, 1), 0)
        err = jnp.where(row >= HEAD, n * gf_ - tbuf[slot], 0.0)
        loss_ref[...] += 0.5 * jnp.sum(jnp.mean(err * err, axis=-1, keepdims=True), axis=0, keepdims=True)
        dy = err * (1.0 / D_MODEL)
        gg_ref[...] += jnp.sum(dy * n, axis=0, keepdims=True)
        dh2 = _rms_bwd(dy * gf_, n, r)
        dh2_ref[...] = dh2
        dh2b_ref[...] = dh2.astype(BF16)

    row_spec = lambda n: pl.BlockSpec((tm, n), lambda i: (i, 0))
    vec = pl.BlockSpec((1, D_MODEL), lambda i: (0, 0))
    return pl.pallas_call(
        body, grid=(n_steps,),
        in_specs=[row_spec(D_MODEL), row_spec(D_RG), row_spec(D_HG), _resident((D_MODEL, D_MODEL)), vec,
                  _resident((D_MODEL, 2 * D_FF)), _resident((D_FF, D_MODEL)), vec,
                  pl.BlockSpec(memory_space=pl.ANY)],
        out_specs=[row_spec(D_MODEL), row_spec(D_MODEL), row_spec(D_MODEL), row_spec(2 * D_FF), row_spec(D_FF),
                   row_spec(D_MODEL), row_spec(D_MODEL), pl.BlockSpec((1, 1), lambda i: (0, 0)), vec],
        out_shape=[jax.ShapeDtypeStruct((T, D_MODEL), F32), jax.ShapeDtypeStruct((T, D_MODEL), BF16),
                   jax.ShapeDtypeStruct((T, D_MODEL), BF16), jax.ShapeDtypeStruct((T, 2 * D_FF), BF16),
                   jax.ShapeDtypeStruct((T, D_FF), BF16), jax.ShapeDtypeStruct((T, D_MODEL), F32),
                   jax.ShapeDtypeStruct((T, D_MODEL), BF16), jax.ShapeDtypeStruct((1, 1), F32),
                   jax.ShapeDtypeStruct((1, D_MODEL), F32)],
        scratch_shapes=[pltpu.VMEM((2, tm, D_MODEL), F32), pltpu.SemaphoreType.DMA((2,))],
        name="ffn_fwd", compiler_params=_params("arbitrary"),
    )(h0, y_rg, y_hg, w_out, g2, w_gu, w_down, gf, target)


def _resident(shape):
    return pl.BlockSpec(shape, lambda i: (0,) * len(shape), pipeline_mode=pl.Buffered(1))


def _ffn_bwd(dh2b, gu, w_down, w_gu, h1, g2, dh2, w_out):
    T = h1.shape[0]
    tm = _row_tile(T, 320)

    def body(d_ref, gu_ref, wd_ref, wgu_ref, h_ref, g_ref, d2_ref, wo_ref, dgu_ref, dh1_ref, dh1b_ref, dy_ref, gg_ref):
        i = pl.program_id(0)

        @pl.when(i == 0)
        def _():
            gg_ref[...] = jnp.zeros_like(gg_ref)

        dact = _dot_nt(d_ref[...], wd_ref[...]).astype(BF16)
        g = gu_ref[:, :D_FF]
        u = gu_ref[:, D_FF:]
        s = _sigmoid(g)
        dgu_ref[:, :D_FF] = dact * u * (s * (1.0 + g * (1.0 - s)))
        dgu_ref[:, D_FF:] = dact * (g * s)

        dv = _dot_nt(dgu_ref[...], wgu_ref[...])
        h1_ = h_ref[...]
        r = _rms(h1_)
        n = h1_ * r
        gg_ref[...] += jnp.sum(dv * n, axis=0, keepdims=True)
        dh1 = d2_ref[...] + _rms_bwd(dv * g_ref[...], n, r)
        dh1_ref[...] = dh1
        db = dh1.astype(BF16)
        dh1b_ref[...] = db
        dy_ref[...] = _dot_nt(db, wo_ref[...])

    row = lambda n: pl.BlockSpec((tm, n), lambda i: (i, 0))
    return pl.pallas_call(
        body, grid=(T // tm,),
        in_specs=[row(D_MODEL), row(2 * D_FF), _resident((D_FF, D_MODEL)), _resident((D_MODEL, 2 * D_FF)),
                  row(D_MODEL), pl.BlockSpec((1, D_MODEL), lambda i: (0, 0)), row(D_MODEL),
                  _resident((D_MODEL, D_MODEL))],
        out_specs=[row(2 * D_FF), row(D_MODEL), row(D_MODEL), row(D_MODEL),
                   pl.BlockSpec((1, D_MODEL), lambda i: (0, 0))],
        out_shape=[jax.ShapeDtypeStruct((T, 2 * D_FF), BF16), jax.ShapeDtypeStruct((T, D_MODEL), F32),
                   jax.ShapeDtypeStruct((T, D_MODEL), BF16), jax.ShapeDtypeStruct((T, D_MODEL), F32),
                   jax.ShapeDtypeStruct((1, D_MODEL), F32)],
        name="ffn_bwd", compiler_params=_params("arbitrary"),
    )(dh2b, gu, w_down, w_gu, h1, g2, dh2, w_out)


def _rg_bwd(p, hs, dy, dp, cw, cb, wg, bg, lam, rg_g):
    T = p.shape[0]
    tm = _row_tile(T, 832)
    nt = T // tm
    hb = tm // 8
    unroll = _scan_unroll(hb)

    def body(xg_ref, xh_ref, h_ref, hh_ref, dy_ref, dp_in_ref, cw_ref, cb_ref, w_ref, bg_ref, lam_ref, g_ref,
             dp_ref, gcw_ref, gcb_ref, gw_ref, gbg_ref, glam_ref, gg_ref,
             ext, dext, a_s, b_s, d_s, gacc, carry_d, carry_a):
        i = pl.program_id(0)
        t_idx = nt - 1 - i

        @pl.when(i == 0)
        def _():
            dext[tm:tm + 8, :] = jnp.zeros((8, D_RG), F32)
            carry_d[...] = jnp.zeros_like(carry_d)
            carry_a[...] = jnp.zeros_like(carry_a)
            gacc[...] = jnp.zeros_like(gacc)
            for ref in (gcw_ref, gcb_ref, gbg_ref, glam_ref, gg_ref, gw_ref):
                ref[...] = jnp.zeros_like(ref)

        first = t_idx == 0
        ext[0:8, :] = jnp.where(first, 0.0, xh_ref[:, :D_RG])
        ext[8:8 + tm, :] = xg_ref[:, :D_RG]
        xc = _conv(ext, cw_ref, cb_ref, tm)
        lam_ = lam_ref[...]
        r, ig, sp, a, m, inv_m = _rg_gates(xc, w_ref, bg_ref, lam_)
        row = t_idx * tm + lax.broadcasted_iota(jnp.int32, (tm, 1), 0)
        valid = row >= PAD

        gr = xg_ref[:, D_RG:]
        g, dgelu = _gelu_parts(gr)
        h = h_ref[...]
        yy = g * h
        rr = _rms(yy)
        nn = yy * rr
        dy_ = dy_ref[...]
        gg_ref[...] += jnp.sum(dy_ * nn, axis=0, keepdims=True)
        dyy = _rms_bwd(dy_ * g_ref[...], nn, rr)
        dp_ref[:, D_RG:] = (dyy * h * dgelu).astype(BF16)

        a_s[...] = a
        b_s[...] = dyy * g
        rowi = lax.broadcasted_iota(jnp.int32, (8, D_RG), 0)

        def blk(jj, c):
            cd, ca = c
            for u in range(unroll):
                o = pl.multiple_of((hb - 1 - (jj * unroll + u)) * 8, 8)
                a_blk = a_s[pl.ds(o, 8), :]
                a_next = jnp.where(rowi == 7, ca, pltpu.roll(a_blk, 7, axis=0))
                A, B = _scan_block_bwd(a_next, b_s[pl.ds(o, 8), :], rowi)
                d = B + A * cd
                d_s[pl.ds(o, 8), :] = d
                cd, ca = d[0:1, :], a_blk[0:1, :]
            return cd, ca

        cd, ca = lax.fori_loop(0, hb // unroll, blk, (carry_d[...], carry_a[...]))
        carry_d[...] = cd
        carry_a[...] = ca
        delta = d_s[...]

        h_last_prev = jnp.where(first, 0.0, hh_ref[7:8, :])
        row0 = lax.broadcasted_iota(jnp.int32, (tm, 1), 0) == 0
        h_prev = jnp.where(row0, h_last_prev, pltpu.roll(h, 1, axis=0))
        dbx = jnp.where(valid, delta, 0.0)
        da = delta * h_prev
        di = dbx * m * xc
        dm = dbx * ig * xc
        dla = a * (da - dm * a * inv_m)
        dla = jnp.where(valid, dla, 0.0)
        glam_ref[...] += jnp.sum(dla * r, axis=0, keepdims=True) * (LRU_C / (1.0 + jnp.exp(lam_)))
        dr = (-LRU_C) * sp * dla
        dpre = jnp.concatenate([dr * r * (1.0 - r), di * ig * (1.0 - ig)], axis=1)
        gbg_ref[...] += jnp.sum(dpre, axis=0, keepdims=True)
        dpre_b = dpre.astype(BF16)
        gacc[...] += _dot_tn(xc.astype(BF16), dpre_b)
        dxc = dbx * m * ig + _dot_nt(dpre_b, w_ref[...])
        gcb_ref[...] += jnp.sum(dxc, axis=0, keepdims=True)
        for j in range(CONV_W):
            gcw_ref[j:j + 1, :] += jnp.sum(dxc * ext[8 - 3 + j:8 - 3 + j + tm, :], axis=0, keepdims=True)
        dext[0:tm, :] = dxc
        dxr = cw_ref[0:1, :] * dext[3:3 + tm, :]
        for j in range(1, CONV_W):
            dxr = dxr + cw_ref[j:j + 1, :] * dext[3 - j:3 - j + tm, :]
        dp_ref[:, :D_RG] = dxr.astype(BF16)
        dext[tm:tm + 8, :] = dext[0:8, :]

        @pl.when(i == nt - 1)
        def _():
            fold = _head_fold()
            mask = _head_mask()
            for k in range(2):
                blockdiag = jnp.where(mask, gacc[:, k * D_RG:(k + 1) * D_RG], 0.0)
                gw_ref[k * D_RG:(k + 1) * D_RG, :] = jnp.dot(blockdiag, fold, precision=HIGHEST,
                                                             preferred_element_type=F32)

    vec = lambda n: pl.BlockSpec((1, n), lambda i: (0, 0))
    rev = lambda n: pl.BlockSpec((tm, n), lambda i: (nt - 1 - i, 0))
    halo = lambda n: pl.BlockSpec((8, n), lambda i: (jnp.maximum((nt - 1 - i) * hb - 1, 0), 0))
    return pl.pallas_call(
        body, grid=(nt,),
        in_specs=[rev(2 * D_RG), halo(2 * D_RG), rev(D_RG), halo(D_RG), rev(D_RG), ANY,
                  pl.BlockSpec((CONV_W, D_RG), lambda i: (0, 0)), vec(D_RG),
                  pl.BlockSpec((D_RG, 2 * D_RG), lambda i: (0, 0)), vec(2 * D_RG), vec(D_RG), vec(D_RG)],
        out_specs=[rev(2 * D_RG), pl.BlockSpec((CONV_W, D_RG), lambda i: (0, 0)), vec(D_RG),
                   pl.BlockSpec((2 * D_RG, RG_HEAD_DIM), lambda i: (0, 0)), vec(2 * D_RG), vec(D_RG), vec(D_RG)],
        input_output_aliases={5: 0},
        out_shape=[jax.ShapeDtypeStruct((T, D_IN), BF16), jax.ShapeDtypeStruct((CONV_W, D_RG), F32),
                   jax.ShapeDtypeStruct((1, D_RG), F32), jax.ShapeDtypeStruct((2 * D_RG, RG_HEAD_DIM), F32),
                   jax.ShapeDtypeStruct((1, 2 * D_RG), F32), jax.ShapeDtypeStruct((1, D_RG), F32),
                   jax.ShapeDtypeStruct((1, D_RG), F32)],
        scratch_shapes=[pltpu.VMEM((tm + 8, D_RG), F32), pltpu.VMEM((tm + 8, D_RG), F32),
                        pltpu.VMEM((tm, D_RG), F32), pltpu.VMEM((tm, D_RG), F32), pltpu.VMEM((tm, D_RG), F32),
                        pltpu.VMEM((D_RG, 2 * D_RG), F32), pltpu.VMEM((1, D_RG), F32), pltpu.VMEM((1, D_RG), F32)],
        name="rg_bwd", compiler_params=_params("arbitrary"),
    )(p, p, hs, hs, dy, dp, cw, cb, wg, bg, lam, rg_g)


def _hg_bwd(p, o_all, st_all, dy, lbraw, hg_g):
    T = p.shape[0]
    n_chunks = T // CHUNK
    cps = _chunks_per_step(n_chunks)
    rows = cps * CHUNK
    n_steps = n_chunks // cps

    def body(hq_ref, hf_ref, hi_ref, hg_ref, o_ref, st_ref, dy_ref, lb_ref, g_ref,
             dp_ref, glb_ref, gg_ref, dst):
        i = pl.program_id(0)

        @pl.when(i == 0)
        def _():
            dst[...] = jnp.zeros_like(dst)
            glb_ref[...] = jnp.zeros_like(glb_ref)
            gg_ref[...] = jnp.zeros_like(gg_ref)

        dp_ref[:, :2 * D_RG] = jnp.zeros((rows, 2 * D_RG), BF16)

        def chunk(jj, carry):
            j = cps - 1 - jj
            rs = pl.ds(pl.multiple_of(j * CHUNK, CHUNK), CHUNK)
            chunk_body((n_steps - 1 - i) * cps + j, hq_ref.at[rs, :], hf_ref.at[rs, :], hi_ref.at[rs, :],
                       hg_ref.at[rs, :], o_ref.at[rs, :], st_ref.at[pl.ds(j, 1)], dy_ref.at[rs, :], lb_ref, g_ref,
                       dp_ref.at[rs, pl.ds(2 * D_RG, 4 * D_HG)], glb_ref, gg_ref, dst)
            return carry

        lax.fori_loop(0, cps, chunk, 0, unroll=True)

    def chunk_body(n, hq_ref, hf_ref, hi_ref, hg_ref, o_ref, st_ref, dy_ref, lb_ref, g_ref,
                   dp_ref, glb_ref, gg_ref, dst):
        valid = (n * CHUNK + lax.broadcasted_iota(jnp.int32, (CHUNK, 1), 0)) >= PAD
        hq, hf, v, hg = hq_ref[...], hf_ref[...], hi_ref[...], hg_ref[...]
        lb, sq, q, sf, f, b = _hg_gates(hq, hf, lb_ref, valid)
        k = 1.0 - f
        causal = _causal()
        r_i = lax.broadcasted_iota(jnp.int32, (CHUNK, CHUNK), 0)
        c_i = lax.broadcasted_iota(jnp.int32, (CHUNK, CHUNK), 1)
        causal_t = r_i <= c_i
        is_last = lax.broadcasted_iota(jnp.int32, (CHUNK, 1), 0) == CHUNK - 1
        g_ = g_ref[...]
        db_parts, dq_parts, dk_parts = [], [], []
        gg = jnp.zeros((1, HG_HEAD_DIM), F32)
        heads = [slice(h * HG_HEAD_DIM, (h + 1) * HG_HEAD_DIM) for h in range(HG_HEADS)]

        do_parts = []
        for h, sl in enumerate(heads):
            o = o_ref[:, sl]
            ro = _rms(o)
            no = o * ro
            hgh = hg[:, sl]
            sg = _sigmoid(hgh)
            dyh = dy_ref[:, sl]
            dp_ref[:, 3 * D_HG + h * HG_HEAD_DIM:3 * D_HG + (h + 1) * HG_HEAD_DIM] = (
                dyh * no * g_ * sg * (1.0 + hgh * (1.0 - sg))).astype(BF16)
            dng = dyh * hgh * sg
            gg = gg + jnp.sum(dng * no, axis=0, keepdims=True)
            do_parts.append(_rms_bwd(dng * g_, no, ro))
        do_t = jnp.concatenate(do_parts, axis=1).T.astype(BF16)

        fac = []
        for sl, do in zip(heads, do_parts):
            qh, kh, bh = q[:, sl], k[:, sl], b[:, sl]
            blk, b_last, eb, eq, ekh, ek, q_hat, k_til = _hg_head(qh, kh, bh)
            fac.append(dict(qh=qh, kh=kh, blk=blk, e_last=jnp.exp(b_last), eb=eb, eq=eq, ekh=ekh, ek=ek,
                            q_til=qh * eb, k_hat=kh * ekh, qhb=q_hat.astype(BF16), ktb=k_til.astype(BF16),
                            vb=v[:, sl].astype(BF16), dob=do.astype(BF16)))

        first = []
        for sl, t in zip(heads, fac):
            st_h = st_ref[0, sl, :]
            dst_h = dst[sl, :]
            dstb = dst_h.astype(BF16)
            first.append(dict(
                att_t=_dot_nt(t["ktb"], t["qhb"]), datt=_dot_nt(t["dob"], t["vb"]),
                datt_t=_dot_nt(t["vb"], t["dob"]), dk_hat=_dot(t["vb"], dstb),
                dv=_dot_nt(t["k_hat"].astype(BF16), dstb), dq_til=_dot(t["dob"], st_h.astype(BF16)),
                state=t["e_last"] * jnp.sum(dst_h * st_h, axis=0, keepdims=True)))
            dst[sl, :] = dst_h * t["e_last"] + _dot(do_t[sl, :], t["q_til"].astype(BF16))

        for h, (t, m) in enumerate(zip(fac, first)):
            qh, kh, blk, eb, eq, ekh, ek = t["qh"], t["kh"], t["blk"], t["eb"], t["eq"], t["ekh"], t["ek"]
            q_til, k_hat, qhb, ktb, dob = t["q_til"], t["k_hat"], t["qhb"], t["ktb"], t["dob"]
            dk_hat, dq_til = m["dk_hat"], m["dq_til"]
            dv = m["dv"] + _dot(jnp.where(causal_t, m["att_t"], 0.0).astype(BF16), dob)
            dq_hat = _dot(jnp.where(causal, m["datt"], 0.0).astype(BF16), ktb)
            dk_til = _dot(jnp.where(causal_t, m["datt_t"], 0.0).astype(BF16), qhb)
            db_last = jnp.sum(dk_hat * k_hat, axis=0, keepdims=True) + m["state"]
            dq_sel = dq_hat[:, (N_SUB - 1) * HG_HEAD_DIM:]
            for s in range(N_SUB - 2, -1, -1):
                dq_sel = jnp.where(blk == s, dq_hat[:, s * HG_HEAD_DIM:(s + 1) * HG_HEAD_DIM], dq_sel)
            dq_a = dq_sel * eq
            dk_a = dk_til[:, :HG_HEAD_DIM] * ek[0]
            for s in range(1, N_SUB):
                dk_a = dk_a + dk_til[:, s * HG_HEAD_DIM:(s + 1) * HG_HEAD_DIM] * ek[s]
            db_att = qhb.astype(F32) * dq_hat - ktb.astype(F32) * dk_til
            db = dq_til * q_til - dk_hat * k_hat
            for s in range(N_SUB):
                db = db + db_att[:, s * HG_HEAD_DIM:(s + 1) * HG_HEAD_DIM]
            db_parts.append(jnp.where(is_last, db + db_last, db))
            dq_parts.append(dq_til * eb + dq_a)
            dk_parts.append(dk_hat * ekh + dk_a)
            dp_ref[:, 2 * D_HG + h * HG_HEAD_DIM:2 * D_HG + (h + 1) * HG_HEAD_DIM] = dv.astype(BF16)

        gg_ref[...] += gg
        db = jnp.concatenate(db_parts, axis=1)
        dq = jnp.concatenate(dq_parts, axis=1)
        dk = jnp.concatenate(dk_parts, axis=1)
        dlf = jnp.where(valid, jnp.dot(_tri(False), db, precision=HIGHEST, preferred_element_type=F32), 0.0)
        dp_ref[:, :D_HG] = (dq * sq * (1.0 + hq * (1.0 - sq))).astype(BF16)
        df = dlf / f - dk
        dlb = jnp.sum(df * (1.0 - sf), axis=0, keepdims=True) * lb * (1.0 - lb)
        glb_ref[0:1, :] += dlb
        glb_ref[1:2, :] += -dlb
        dp_ref[:, D_HG:2 * D_HG] = (df * (1.0 - lb) * sf * (1.0 - sf)).astype(BF16)

    rev = lambda j: pl.BlockSpec((rows, D_HG), lambda i: (n_steps - 1 - i, j))
    return pl.pallas_call(
        body, grid=(n_steps,),
        in_specs=[rev(2), rev(3), rev(4), rev(5), rev(0),
                  pl.BlockSpec((cps, D_HG, HG_HEAD_DIM), lambda i: (n_steps - 1 - i, 0, 0)), rev(1),
                  pl.BlockSpec((2, D_HG), lambda i: (0, 0)), pl.BlockSpec((1, HG_HEAD_DIM), lambda i: (0, 0))],
        out_specs=[pl.BlockSpec((rows, D_IN), lambda i: (n_steps - 1 - i, 0)),
                   pl.BlockSpec((2, D_HG), lambda i: (0, 0)), pl.BlockSpec((1, HG_HEAD_DIM), lambda i: (0, 0))],
        out_shape=[jax.ShapeDtypeStruct((T, D_IN), BF16), jax.ShapeDtypeStruct((2, D_HG), F32),
                   jax.ShapeDtypeStruct((1, HG_HEAD_DIM), F32)],
        scratch_shapes=[pltpu.VMEM((D_HG, HG_HEAD_DIM), F32)],
        name="hg_bwd", compiler_params=_params("arbitrary"),
    )(p, p, p, p, o_all, st_all, dy, lbraw, hg_g)


def _in_bwd(dp, w_in, h0, g1, dh1):
    T = h0.shape[0]
    tm = _row_tile(T, 416)
    n_steps = T // tm

    def body(dp_ref, w_ref, h_ref, g_ref, d1_ref, gx_hbm, gmeta_ref, gg_ref, buf, sems):
        i = pl.program_id(0)
        first, later = _window_copies(gx_hbm, buf, sems, tm)
        slot = i % 2

        @pl.when(i == 0)
        def _():
            gg_ref[...] = jnp.zeros_like(gg_ref)

        if n_steps > 2:
            @pl.when(i == 2)
            def _():
                first(False).wait()

            @pl.when(i > 2)
            def _():
                later(i - 2, slot, False).wait()

        du = _dot_nt(dp_ref[...], w_ref[...])
        h0_ = h_ref[...]
        r = _rms(h0_)
        n = h0_ * r
        gg_ref[...] += jnp.sum(du * n, axis=0, keepdims=True)
        dh0 = d1_ref[...] + _rms_bwd(du * g_ref[...], n, r)
        buf[slot] = dh0

        @pl.when(i == 0)
        def _():
            gmeta_ref[...] = dh0[PAD:HEAD, :]
            first(False).start()

        if n_steps > 1:
            @pl.when(i > 0)
            def _():
                later(i, slot, False).start()

        @pl.when(i == n_steps - 1)
        def _():
            if n_steps == 1:
                first(False).wait()
            else:
                if n_steps == 2:
                    first(False).wait()
                else:
                    later(i - 1, 1 - slot, False).wait()
                later(i, slot, False).wait()

    row = lambda n: pl.BlockSpec((tm, n), lambda i: (i, 0))
    return pl.pallas_call(
        body, grid=(n_steps,),
        in_specs=[row(D_IN), pl.BlockSpec((D_MODEL, D_IN), lambda i: (0, 0)),
                  row(D_MODEL), pl.BlockSpec((1, D_MODEL), lambda i: (0, 0)), row(D_MODEL)],
        out_specs=[pl.BlockSpec(memory_space=pl.ANY), pl.BlockSpec((N_META, D_MODEL), lambda i: (0, 0)),
                   pl.BlockSpec((1, D_MODEL), lambda i: (0, 0))],
        out_shape=[jax.ShapeDtypeStruct((T - HEAD, D_MODEL), F32), jax.ShapeDtypeStruct((N_META, D_MODEL), F32),
                   jax.ShapeDtypeStruct((1, D_MODEL), F32)],
        scratch_shapes=[pltpu.VMEM((2, tm, D_MODEL), F32), pltpu.SemaphoreType.DMA((2,))],
        name="in_bwd", compiler_params=_params("arbitrary"),
    )(dp, w_in, h0, g1, dh1)


def _col_tile(cols, target):
    best = None
    for t in range(128, min(cols, target) + 1, 128):
        if cols % t == 0:
            best = t
    assert best is not None, cols
    return best


MXU_DIM = 256


def _mxu_tile(cols, target):
    best = None
    for t in range(MXU_DIM, min(cols, target) + 1, MXU_DIM):
        if cols % t == 0:
            best = t
    assert best is not None, cols
    return best


def _weight_grad(a, b, name):
    T, M = a.shape
    N = b.shape[1]
    tm = _col_tile(M, 1408)
    tn = _mxu_tile(N, 768 if tm <= 1024 else 512)

    def body(a_ref, b_ref, o_ref):
        o_ref[...] = _dot_tn(a_ref[...], b_ref[...])

    return pl.pallas_call(
        body, grid=(M // tm, N // tn),
        in_specs=[pl.BlockSpec((T, tm), lambda m, n: (0, m)), pl.BlockSpec((T, tn), lambda m, n: (0, n))],
        out_specs=pl.BlockSpec((tm, tn), lambda m, n: (m, n)),
        out_shape=jax.ShapeDtypeStruct((M, N), F32),
        name=name, compiler_params=_params("parallel", "parallel"),
    )(a, b)


def _local_step(x, meta, target, w_in_own, w_in, w_out, w_gu, w_down, small, chip, on_ffn_grads=None,
                on_mixer_grads=None):
    wg = _gate_weights(small["w_rgate"], small["w_igate"])
    bg = jnp.concatenate([small["b_rgate"], small["b_igate"]], axis=1)

    p, u, h0 = _in_proj_local(x, meta, small["mix_norm_g"], w_in_own, chip)
    p = _in_proj_rest(u, w_in, p, chip)
    y_rg, hs = _rg_fwd(p, small["conv_w"], small["conv_b"], wg, bg, small["lru_lambda"], small["rg_norm_g"])
    y_hg, o_all, st_all = _hg_fwd(p, small["hg_lower_bound"], small["hg_norm_g"])
    h1, v, yb, gu, act, dh2, dh2b, loss, g_final = _ffn_fwd(
        h0, y_rg, y_hg, w_out, small["ffn_norm_g"], w_gu, w_down, small["final_norm_g"], target)

    g_w_down = _weight_grad(act, dh2b, "grad_w_down")
    dgu, dh1, dh1b, dy, g_ffn = _ffn_bwd(dh2b, gu, w_down, w_gu, h1, small["ffn_norm_g"], dh2, w_out)
    ffn_grads = {"w_gate_up": _weight_grad(v, dgu, "grad_w_gate_up"), "w_down": g_w_down}
    stages = on_ffn_grads(ffn_grads) if on_ffn_grads is not None else None
    dp, g_lb, g_hgn = _hg_bwd(p, o_all, st_all, dy, small["hg_lower_bound"], small["hg_norm_g"])
    early = late = None
    if stages is not None:
        chip_sums, send = stages
        sums = chip_sums()
        (dp, dy), sums = lax.optimization_barrier(((dp, dy), sums))
        early = send(sums)
    dp, g_cw, g_cb, g_wgate, g_bg, g_lam, g_rgn = _rg_bwd(
        p, hs, dy, dp, small["conv_w"], small["conv_b"], wg, bg, small["lru_lambda"], small["rg_norm_g"])
    mixer_grads = {"w_in": _weight_grad(u, dp, "grad_w_in"), "w_out": _weight_grad(yb, dh1b, "grad_w_out")}
    if on_mixer_grads is not None:
        chip_sums, send = on_mixer_grads(mixer_grads)
        sums = chip_sums()
        (dp, dh1), sums = lax.optimization_barrier(((dp, dh1), sums))
        late = send(sums)
    grad_x, g_meta, g_mix = _in_bwd(dp, w_in, h0, small["mix_norm_g"], dh1)

    grads = {
        "w_in": mixer_grads["w_in"], "w_out": mixer_grads["w_out"],
        "w_gate_up": ffn_grads["w_gate_up"], "w_down": ffn_grads["w_down"],
        "meta_tokens": g_meta, "mix_norm_g": g_mix, "conv_w": g_cw, "conv_b": g_cb, "w_gates": g_wgate,
        "b_rgate": g_bg[:, :D_RG], "b_igate": g_bg[:, D_RG:], "lru_lambda": g_lam, "rg_norm_g": g_rgn,
        "hg_lower_bound": g_lb, "hg_norm_g": g_hgn, "ffn_norm_g": g_ffn, "final_norm_g": g_final,
    }
    return loss, grad_x, grads, early, late


ANY = pl.BlockSpec(memory_space=pl.ANY)
HALF = D_MODEL // 2

BIG = {"w_in": (D_MODEL, D_IN // N_CHIPS, True), "w_gate_up": (D_MODEL, 2 * D_FF // N_CHIPS, True),
       "w_out": (D_MODEL // N_CHIPS, D_MODEL, False), "w_down": (D_FF // N_CHIPS, D_MODEL, False)}
BIG_NAMES = tuple(BIG)
N_BIG = len(BIG_NAMES)


def _full_shape(name):
    rows, cols, by_col = BIG[name]
    return (rows, cols * N_CHIPS) if by_col else (rows * N_CHIPS, cols)


def _place():
    return lax.axis_index("x"), lax.axis_index("y"), lax.axis_index("c")


def _chip_of(x, y, r):
    fx, fy = (r + 1) >> 1, (r + 1) & 1
    return (1 - x if fx else x), (1 - y if fy else y)


def _half_of(ref, by_col, half):
    start = pl.multiple_of(half * HALF, 128)
    return ref.at[pl.ds(start, HALF), :] if by_col else ref.at[:, pl.ds(start, HALF)]


def _shard_of(ref, name, chip):
    rows, cols, by_col = BIG[name]
    if by_col:
        return ref.at[:, pl.ds(pl.multiple_of(chip * cols, 128), cols)]
    return ref.at[pl.ds(pl.multiple_of(chip * rows, 16), rows), :]


def _shard_half_of(ref, name, chip, half):
    rows, cols, by_col = BIG[name]
    start = pl.multiple_of(half * HALF, 128)
    if by_col:
        return ref.at[pl.ds(start, HALF), pl.ds(pl.multiple_of(chip * cols, 128), cols)]
    return ref.at[pl.ds(pl.multiple_of(chip * rows, 16), rows), pl.ds(start, HALF)]


def _remote(src, dst, send_sems, recv_sems, k, dev):
    return pltpu.make_async_remote_copy(src_ref=src, dst_ref=dst, send_sem=send_sems.at[k], recv_sem=recv_sems.at[k],
                                        device_id=dev, device_id_type=MESH)


def _place_shards(w, small, chip):
    steps = 4
    ns = len(small)
    in_specs, out_specs = [], []
    for name in BIG_NAMES:
        rows, cols, by_col = BIG[name]
        tr = rows // steps
        in_specs.append(pl.BlockSpec((tr, cols), lambda i, s: (i, 0)))
        if by_col:
            out_specs.append(pl.BlockSpec((tr, cols), lambda i, s: (i, s[0])))
        else:
            out_specs.append(pl.BlockSpec((tr, cols), lambda i, s: (s[0] * steps + i, 0)))

    def body(s_ref, *refs):
        ins, small_in = refs[:N_BIG], refs[N_BIG:N_BIG + ns]
        outs, small_out = refs[N_BIG + ns:2 * N_BIG + ns], refs[2 * N_BIG + ns:2 * (N_BIG + ns)]
        send_sems, recv_sems, local_sems = refs[2 * (N_BIG + ns):]
        i = pl.program_id(0)
        x, y, c = _place()
        chip_ = 2 * x + y
        others = [_chip_of(x, y, r) for r in range(3)]

        def block(a, q):
            cols = small[a].shape[1]
            return small_out[a].at[:, pl.ds(pl.multiple_of(q * cols, 128), cols)]

        def local(a):
            return pltpu.make_async_copy(small_in[a], block(a, chip_), local_sems.at[a])

        def remote(a, r):
            qx, qy = others[r]
            return _remote(small_in[a], block(a, chip_), send_sems, recv_sems, 3 * a + r, (qx, qy, c))

        @pl.when(i == 0)
        def _():
            for a in range(ns):
                local(a).start()
                for r in range(3):
                    remote(a, r).start()

        for a in range(N_BIG):
            outs[a][...] = ins[a][...].astype(BF16)

        @pl.when(i == steps - 1)
        def _():
            for a in range(ns):
                for r, (qx, qy) in enumerate(others):
                    landed = block(a, 2 * qx + qy)
                    _remote(landed, landed, send_sems, recv_sems, 3 * a + r, (qx, qy, c)).wait_recv()
                for r in range(3):
                    remote(a, r).wait_send()
                local(a).wait()

    out = pl.pallas_call(
        body,
        grid_spec=pltpu.PrefetchScalarGridSpec(
            num_scalar_prefetch=1, grid=(steps,), in_specs=in_specs + [ANY] * ns, out_specs=out_specs + [ANY] * ns,
            scratch_shapes=[pltpu.SemaphoreType.DMA((3 * ns,)), pltpu.SemaphoreType.DMA((3 * ns,)),
                            pltpu.SemaphoreType.DMA((ns,))]),
        out_shape=([jax.ShapeDtypeStruct(_full_shape(name), BF16) for name in BIG_NAMES]
                   + [jax.ShapeDtypeStruct((s.shape[0], s.shape[1] * N_CHIPS), F32) for s in small]),
        name="place_shards", compiler_params=_params("arbitrary"),
    )(chip, *[w[name] for name in BIG_NAMES], *small)
    return dict(zip(BIG_NAMES, out[:N_BIG])), list(out[N_BIG:])


def _gather_weights(placed, small, names, label, collective_id):
    n, ns = len(names), len(small)
    hbm = pltpu.MemorySpace.HBM
    outs = [jax.new_ref(placed[nm], memory_space=hbm) for nm in names]
    small_in = [jax.new_ref(s, memory_space=hbm) for s in small]
    small_out = [jax.empty_ref(jax.ShapeDtypeStruct((s.shape[0], s.shape[1] * N_CHIPS), F32), memory_space=hbm)
                 for s in small]
    n_sems = 6 * n + 3 * ns

    @pl.kernel(mesh=plsc.ScalarSubcoreMesh(axis_name="seq", num_cores=1), name=label, out_type=(),
               scratch_types=(pltpu.SemaphoreType.DMA((n_sems,)), pltpu.SemaphoreType.DMA((n_sems,)),
                              pltpu.SemaphoreType.DMA((max(ns, 1),))),
               compiler_params=pltpu.CompilerParams(collective_id=collective_id))
    def launch(send_sems, recv_sems, local_sems):
        x, y, c = _place()
        chip = 2 * x + y
        sibling = (x, y, 1 - c)
        others = [_chip_of(x, y, r) for r in range(3)]
        _handshake([(qx, qy, c) for qx, qy in others] + [sibling])

        def small_block(a, q):
            cols = small[a].shape[1]
            return small_out[a].at[:, pl.ds(pl.multiple_of(q * cols, 128), cols)]

        local = [pltpu.make_async_copy(small_in[a], small_block(a, chip), local_sems.at[a]) for a in range(ns)]
        for cp in local:
            cp.start()

        sends = []
        for a, name in enumerate(names):
            mine = _shard_half_of(outs[a], name, chip, c)
            for r, (qx, qy) in enumerate(others):
                sends.append(_remote(mine, mine, send_sems, recv_sems, 6 * a + r, (qx, qy, c)))
        for a in range(ns):
            for r, (qx, qy) in enumerate(others):
                sends.append(_remote(small_in[a], small_block(a, chip), send_sems, recv_sems,
                                     6 * n + 3 * a + r, (qx, qy, c)))
        for cp in sends:
            cp.start()

        forwards = []
        for a, name in enumerate(names):
            for r, (qx, qy) in enumerate(others):
                landed = _shard_half_of(outs[a], name, 2 * qx + qy, c)
                _remote(landed, landed, send_sems, recv_sems, 6 * a + r, (qx, qy, c)).wait_recv()
                fwd = _remote(landed, landed, send_sems, recv_sems, 6 * a + 3 + r, sibling)
                fwd.start()
                forwards.append(fwd)
        for a in range(ns):
            for r, (qx, qy) in enumerate(others):
                landed = small_block(a, 2 * qx + qy)
                _remote(landed, landed, send_sems, recv_sems, 6 * n + 3 * a + r, (qx, qy, c)).wait_recv()
        for a, name in enumerate(names):
            for r, (qx, qy) in enumerate(others):
                landed = _shard_half_of(outs[a], name, 2 * qx + qy, 1 - c)
                _remote(landed, landed, send_sems, recv_sems, 6 * a + 3 + r, sibling).wait_recv()
        for cp in sends + forwards:
            cp.wait_send()
        for cp in local:
            cp.wait()

    launch()
    return {nm: ref[...] for nm, ref in zip(names, outs)}, [ref[...] for ref in small_out]


def _exchange_halves(grads, names, label, collective_id):
    n = len(names)
    sequencer = collective_id is not None

    def body(*refs):
        ins, outs = refs[:n], refs[n:2 * n]
        send_sems, recv_sems = refs[2 * n:]
        x, y, c = _place()
        if sequencer:
            _handshake([(x, y, 1 - c)])
        copies = []
        for a, name in enumerate(names):
            copies.append(_remote(_half_of(ins[a], BIG[name][2], 1 - c), outs[a], send_sems, recv_sems, a,
                                  (x, y, 1 - c)))
        for cp in copies:
            cp.start()
        for cp in copies:
            cp.wait()

    def half_shape(name):
        r, c_ = _full_shape(name)
        return (HALF, c_) if BIG[name][2] else (r, HALF)

    out_type = tuple(jax.ShapeDtypeStruct(half_shape(nm), F32) for nm in names)
    sems = (pltpu.SemaphoreType.DMA((n,)), pltpu.SemaphoreType.DMA((n,)))
    operands = [grads[nm] for nm in names]
    if sequencer:
        got = pl.kernel(
            body, mesh=plsc.ScalarSubcoreMesh(axis_name="seq", num_cores=1), name=label, out_type=out_type,
            scratch_types=sems, compiler_params=pltpu.CompilerParams(collective_id=collective_id),
        )(*operands)
    else:
        got = pl.pallas_call(
            body, in_specs=[ANY] * n, out_specs=[ANY] * n, out_shape=list(out_type), scratch_shapes=list(sems),
            name=label,
        )(*operands)
    return dict(zip(names, got))


def _chip_sum(grads, got, names, core, label):
    n = len(names)
    steps = 4
    g_specs, blks = [], []
    for name in names:
        rows, cols = got[name].shape
        tr = rows // steps
        if BIG[name][2]:
            g_specs.append(pl.BlockSpec((tr, cols), lambda i, s: (s[0] * steps + i, 0)))
        else:
            g_specs.append(pl.BlockSpec((tr, HALF), lambda i, s: (i, s[0])))
        blks.append(pl.BlockSpec((tr, cols), lambda i, s: (i, 0)))

    def body(s_ref, *refs):
        for a in range(n):
            t = refs[a][...] + refs[n + a][...]
            refs[2 * n + a][...] = t
            refs[3 * n + a][...] = t.astype(BF16)

    out = pl.pallas_call(
        body,
        grid_spec=pltpu.PrefetchScalarGridSpec(num_scalar_prefetch=1, grid=(steps,), in_specs=g_specs + blks,
                                               out_specs=blks + blks),
        out_shape=([jax.ShapeDtypeStruct(got[nm].shape, F32) for nm in names]
                   + [jax.ShapeDtypeStruct(got[nm].shape, BF16) for nm in names]),
        name=label, compiler_params=_params("parallel"),
    )(core, *[grads[nm] for nm in names], *[got[nm] for nm in names])
    return {nm: (out[a], out[n + a]) for a, nm in enumerate(names)}


def _piece_shape(name):
    rows, cols, by_col = BIG[name]
    return (HALF, cols) if by_col else (rows, HALF)


def _handshake(peers):
    barrier = pltpu.get_barrier_semaphore()
    for peer in peers:
        pl.semaphore_signal(barrier, inc=1, device_id=peer, device_id_type=MESH)
    pl.semaphore_wait(barrier, len(peers))


def _send_chip_sums(sums, names, label, collective_id):
    n = len(names)

    def body(*refs):
        ins, outs = refs[:n], refs[n:2 * n]
        send_sems, recv_sems = refs[2 * n:]
        x, y, c = _place()
        others = [_chip_of(x, y, r) for r in range(3)]
        _handshake([(qx, qy, c) for qx, qy in others])
        copies = []
        for a, name in enumerate(names):
            for r, (qx, qy) in enumerate(others):
                copies.append(_remote(_shard_of(ins[a], name, 2 * qx + qy), outs[a].at[r], send_sems, recv_sems,
                                      3 * a + r, (qx, qy, c)))
        for cp in copies:
            cp.start()
        for cp in copies:
            cp.wait()

    return pl.kernel(
        body, mesh=plsc.ScalarSubcoreMesh(axis_name="seq", num_cores=1), name=label,
        out_type=tuple(jax.ShapeDtypeStruct((3,) + _piece_shape(nm), BF16) for nm in names),
        scratch_types=(pltpu.SemaphoreType.DMA((3 * n,)), pltpu.SemaphoreType.DMA((3 * n,))),
        compiler_params=pltpu.CompilerParams(collective_id=collective_id),
    )(*[sums[nm] for nm in names])


def _total(parts, chip_core):
    steps = 2
    in_specs, out_specs, operands = [], [], []
    for name in BIG_NAMES:
        by_col = BIG[name][2]
        pr, pc = _piece_shape(name)
        tr = pr // steps
        if by_col:
            in_specs.append(pl.BlockSpec((tr, pc), lambda i, s: (i, s[0])))
            out_specs.append(pl.BlockSpec((tr, pc), lambda i, s: (s[1] * steps + i, 0)))
        else:
            in_specs.append(pl.BlockSpec((tr, pc), lambda i, s: (s[0] * steps + i, 0)))
            out_specs.append(pl.BlockSpec((tr, pc), lambda i, s: (i, s[1])))
        for r in range(3):
            in_specs.append(pl.BlockSpec((None, tr, pc), lambda i, s, r=r: (r, i, 0)))
        own, got = parts[name]
        operands += [own, got, got, got]

    def body(s_ref, *refs):
        for a in range(N_BIG):
            o_ref, a_ref, b_ref, c_ref = refs[4 * a:4 * a + 4]
            refs[4 * N_BIG + a][...] = (((o_ref[...] + a_ref[...].astype(F32)) + b_ref[...].astype(F32))
                                        + c_ref[...].astype(F32))

    totals = pl.pallas_call(
        body,
        grid_spec=pltpu.PrefetchScalarGridSpec(num_scalar_prefetch=1, grid=(steps,), in_specs=in_specs,
                                               out_specs=out_specs),
        out_shape=[jax.ShapeDtypeStruct(BIG[name][:2], F32) for name in BIG_NAMES],
        name="totals", compiler_params=_params("parallel"),
    )(chip_core, *operands)
    return dict(zip(BIG_NAMES, totals))


def _share_totals(totals):
    def body(*refs):
        outs = refs[N_BIG:2 * N_BIG]
        send_sems, recv_sems = refs[2 * N_BIG:]
        x, y, c = _place()
        copies = []
        for a, name in enumerate(BIG_NAMES):
            mine = _half_of(outs[a], BIG[name][2], c)
            copies.append(_remote(mine, mine, send_sems, recv_sems, a, (x, y, 1 - c)))
        for cp in copies:
            cp.start()
        for a, name in enumerate(BIG_NAMES):
            theirs = _half_of(outs[a], BIG[name][2], 1 - c)
            _remote(theirs, theirs, send_sems, recv_sems, a, (x, y, 1 - c)).wait_recv()
        for cp in copies:
            cp.wait_send()

    return pl.pallas_call(
        body, in_specs=[ANY] * N_BIG, out_specs=[ANY] * N_BIG,
        out_shape=[jax.ShapeDtypeStruct(BIG[n][:2], F32) for n in BIG_NAMES],
        input_output_aliases={a: a for a in range(N_BIG)},
        scratch_shapes=[pltpu.SemaphoreType.DMA((N_BIG,)), pltpu.SemaphoreType.DMA((N_BIG,))],
        name="share_totals",
    )(*[totals[n] for n in BIG_NAMES])


VEC_ROWS = 32
VEC_ROW = {"mix_norm_g": 0, "conv_b": 1, "b_rgate": 2, "b_igate": 3, "lru_lambda": 4, "rg_norm_g": 5,
           "hg_lower_bound": 6, "hg_norm_g": 8, "ffn_norm_g": 9, "final_norm_g": 10, "loss": 11,
           "conv_w": 12, "meta_tokens": 16}
N_DEV = 8


def _all_reduce_small(pieces, gates):
    names = list(pieces)
    hv, hg = VEC_ROWS // 2, gates.shape[0] // 2

    def body(*refs):
        ins = refs[:len(names)]
        (g_ref, vec_ref, gsum_ref, mine_v, sib_v, sib_g, chip_v, chip_g, got_v, got_g,
         send_sems, recv_sems) = refs[len(names):]
        x, y, c = _place()
        chip = 2 * x + y
        sibling = (x, y, 1 - c)
        mine_v[...] = jnp.zeros_like(mine_v)
        for name, ref in zip(names, ins):
            nr, w = ref.shape
            mine_v[VEC_ROW[name]:VEC_ROW[name] + nr, 0:w] = ref[...]

        swap = [_remote(mine_v, sib_v, send_sems, recv_sems, 0, sibling),
                _remote(g_ref, sib_g, send_sems, recv_sems, 1, sibling)]
        for cp in swap:
            cp.start()
        for cp in swap:
            cp.wait()
        chip_v[...] = mine_v[...] + sib_v[...]
        chip_g[...] = g_ref[...] + sib_g[...]

        rows_v = pl.ds(pl.multiple_of(c * hv, 8), hv)
        rows_g = pl.ds(pl.multiple_of(c * hg, 8), hg)
        got_v[chip] = chip_v[rows_v, :]
        got_g[chip] = chip_g[rows_g, :]
        sends = []
        for r in range(3):
            qx, qy = _chip_of(x, y, r)
            sends.append(_remote(chip_v.at[rows_v, :], got_v.at[chip], send_sems, recv_sems, 2 + r, (qx, qy, c)))
            sends.append(_remote(chip_g.at[rows_g, :], got_g.at[chip], send_sems, recv_sems, 5 + r, (qx, qy, c)))
        for cp in sends:
            cp.start()
        for cp in sends:
            cp.wait()
        vec_ref[rows_v, :] = ((got_v[0] + got_v[1]) + got_v[2]) + got_v[3]
        gsum_ref[rows_g, :] = ((got_g[0] + got_g[1]) + got_g[2]) + got_g[3]

        back = [_remote(vec_ref.at[rows_v, :], vec_ref.at[rows_v, :], send_sems, recv_sems, 8, sibling),
                _remote(gsum_ref.at[rows_g, :], gsum_ref.at[rows_g, :], send_sems, recv_sems, 9, sibling)]
        for cp in back:
            cp.start()
        theirs_v = vec_ref.at[pl.ds(pl.multiple_of((1 - c) * hv, 8), hv), :]
        theirs_g = gsum_ref.at[pl.ds(pl.multiple_of((1 - c) * hg, 8), hg), :]
        _remote(theirs_v, theirs_v, send_sems, recv_sems, 8, sibling).wait_recv()
        _remote(theirs_g, theirs_g, send_sems, recv_sems, 9, sibling).wait_recv()
        for cp in back:
            cp.wait_send()

    vmem = pl.BlockSpec(memory_space=pltpu.VMEM)
    n_sems = 10
    return pl.pallas_call(
        body, in_specs=[vmem] * (len(names) + 1), out_specs=[vmem, vmem],
        out_shape=[jax.ShapeDtypeStruct((VEC_ROWS, D_MODEL), F32), jax.ShapeDtypeStruct(gates.shape, F32)],
        scratch_shapes=[pltpu.VMEM((VEC_ROWS, D_MODEL), F32), pltpu.VMEM((VEC_ROWS, D_MODEL), F32),
                        pltpu.VMEM(gates.shape, F32), pltpu.VMEM((VEC_ROWS, D_MODEL), F32),
                        pltpu.VMEM(gates.shape, F32), pltpu.VMEM((N_CHIPS, hv, D_MODEL), F32),
                        pltpu.VMEM((N_CHIPS, hg) + gates.shape[1:], F32),
                        pltpu.SemaphoreType.DMA((n_sems,)), pltpu.SemaphoreType.DMA((n_sems,))],
        name="all_reduce_small",
    )(*[pieces[n] for n in names], gates)


def _adamw_math(w, g, m, v):
    m = ADAM_B1 * m + (1.0 - ADAM_B1) * g
    v = ADAM_B2 * v + (1.0 - ADAM_B2) * (g * g)
    m_hat = m / (1.0 - ADAM_B1 ** ADAM_STEP)
    v_hat = v / (1.0 - ADAM_B2 ** ADAM_STEP)
    delta = -ADAM_LR * (m_hat / (jnp.sqrt(v_hat) + ADAM_EPS) + ADAM_WD * w)
    return delta, m, v


def _adamw_big(w, g, m, v):
    steps = 8
    blks = []
    for name in BIG_NAMES:
        rows, cols, _ = BIG[name]
        blks.append(pl.BlockSpec((rows // steps, cols), lambda i: (i, 0)))

    def body(*refs):
        ins, outs = refs[:4 * N_BIG], refs[4 * N_BIG:]
        for a in range(N_BIG):
            w_ref, g_ref, m_ref, v_ref = (ins[k * N_BIG + a] for k in range(4))
            d, nm, nv = _adamw_math(w_ref[...], g_ref[...], m_ref[...], v_ref[...])
            outs[a][...] = d
            outs[N_BIG + a][...] = nm
            outs[2 * N_BIG + a][...] = nv

    shapes = [jax.ShapeDtypeStruct(BIG[name][:2], F32) for name in BIG_NAMES]
    out = pl.pallas_call(
        body, grid=(steps,), in_specs=blks * 4, out_specs=blks * 3, out_shape=shapes * 3,
        name="adamw_big", compiler_params=_params("parallel"),
    )(*[t[name] for t in (w, g, m, v) for name in BIG_NAMES])
    return {name: (out[a], out[N_BIG + a], out[2 * N_BIG + a]) for a, name in enumerate(BIG_NAMES)}


SMALL = {"meta_tokens": (N_META, D_MODEL // N_CHIPS), "mix_norm_g": (1, D_MODEL), "conv_w": (CONV_W, D_RG // N_CHIPS),
         "conv_b": (1, D_RG), "w_rgate": (D_RG, RG_HEAD_DIM), "b_rgate": (1, D_RG), "w_igate": (D_RG, RG_HEAD_DIM),
         "b_igate": (1, D_RG), "lru_lambda": (1, D_RG), "rg_norm_g": (1, D_RG), "hg_lower_bound": (2, D_HG),
         "hg_norm_g": (1, HG_HEAD_DIM), "ffn_norm_g": (1, D_MODEL), "final_norm_g": (1, D_MODEL)}
SMALL_NAMES = tuple(SMALL)
SHARDED_SMALL = ("meta_tokens", "conv_w")


def _adamw_small(vec, gates, w, m, v):
    n = len(SMALL_NAMES)

    def body(*refs):
        vec_ref, gates_ref = refs[:2]
        w_refs, m_refs, v_refs = refs[2:2 + n], refs[2 + n:2 + 2 * n], refs[2 + 2 * n:2 + 3 * n]
        outs = refs[2 + 3 * n:]
        loss_ref = outs[0]
        x, y, _ = _place()
        chip = 2 * x + y
        loss_ref[...] = vec_ref[VEC_ROW["loss"]:VEC_ROW["loss"] + 1, 0:1]

        def update(k, g):
            g_ref, d_ref, nm_ref, nv_ref = outs[1 + 4 * k:5 + 4 * k]
            g_ref[...] = g
            d_ref[...], nm_ref[...], nv_ref[...] = _adamw_math(w_refs[k][...], g, m_refs[k][...], v_refs[k][...])

        for k, name in enumerate(SMALL_NAMES):
            nr, w_ = SMALL[name]
            if name == "w_rgate":
                update(k, gates_ref[0:D_RG, :])
            elif name == "w_igate":
                update(k, gates_ref[D_RG:2 * D_RG, :])
            elif name in SHARDED_SMALL:
                r0 = VEC_ROW[name]
                for q in range(N_CHIPS):
                    @pl.when(chip == q)
                    def _(k=k, r0=r0, nr=nr, w_=w_, q=q):
                        update(k, vec_ref[r0:r0 + nr, q * w_:(q + 1) * w_])
            else:
                r0 = VEC_ROW[name]
                update(k, vec_ref[r0:r0 + nr, 0:w_])

    vmem = pl.BlockSpec(memory_space=pltpu.VMEM)
    out_shape = [jax.ShapeDtypeStruct((1, 1), F32)]
    for name in SMALL_NAMES:
        out_shape += [jax.ShapeDtypeStruct(SMALL[name], F32)] * 4
    outs = pl.pallas_call(
        body, in_specs=[vmem] * (2 + 3 * n), out_specs=[vmem] * len(out_shape), out_shape=out_shape,
        name="adamw_small",
    )(vec, gates, *[w[k] for k in SMALL_NAMES], *[m[k] for k in SMALL_NAMES], *[v[k] for k in SMALL_NAMES])
    loss = outs[0]
    res = {name: tuple(outs[1 + 4 * k:5 + 4 * k]) for k, name in enumerate(SMALL_NAMES)}
    return loss, res


WEIGHT_NAMES = ("meta_tokens", "mix_norm_g", "w_in", "conv_w", "conv_b", "w_rgate", "b_rgate", "w_igate", "b_igate",
                "lru_lambda", "rg_norm_g", "hg_lower_bound", "hg_norm_g", "w_out", "ffn_norm_g", "w_gate_up", "w_down",
                "final_norm_g")


def _to_2d(name, a):
    if name in BIG:
        return a.reshape(BIG[name][:2])
    return a.reshape(SMALL[name])


def kernel(x, meta_tokens, mix_norm_g, w_in, conv_w, conv_b, w_rgate, b_rgate, w_igate, b_igate, lru_lambda, rg_norm_g, hg_lower_bound, hg_norm_g, w_out, ffn_norm_g, w_gate_up, w_down, final_norm_g, loss_target, m_meta_tokens, m_mix_norm_g, m_w_in, m_conv_w, m_conv_b, m_w_rgate, m_b_rgate, m_w_igate, m_b_igate, m_lru_lambda, m_rg_norm_g, m_hg_lower_bound, m_hg_norm_g, m_w_out, m_ffn_norm_g, m_w_gate_up, m_w_down, m_final_norm_g, v_meta_tokens, v_mix_norm_g, v_w_in, v_conv_w, v_conv_b, v_w_rgate, v_b_rgate, v_w_igate, v_b_igate, v_lru_lambda, v_rg_norm_g, v_hg_lower_bound, v_hg_norm_g, v_w_out, v_ffn_norm_g, v_w_gate_up, v_w_down, v_final_norm_g):
    w_raw = dict(zip(WEIGHT_NAMES, (meta_tokens, mix_norm_g, w_in, conv_w, conv_b, w_rgate, b_rgate, w_igate, b_igate,
                                    lru_lambda, rg_norm_g, hg_lower_bound, hg_norm_g, w_out, ffn_norm_g, w_gate_up,
                                    w_down, final_norm_g)))
    m_raw = dict(zip(WEIGHT_NAMES, (m_meta_tokens, m_mix_norm_g, m_w_in, m_conv_w, m_conv_b, m_w_rgate, m_b_rgate,
                                    m_w_igate, m_b_igate, m_lru_lambda, m_rg_norm_g, m_hg_lower_bound, m_hg_norm_g,
                                    m_w_out, m_ffn_norm_g, m_w_gate_up, m_w_down, m_final_norm_g)))
    v_raw = dict(zip(WEIGHT_NAMES, (v_meta_tokens, v_mix_norm_g, v_w_in, v_conv_w, v_conv_b, v_w_rgate, v_b_rgate,
                                    v_w_igate, v_b_igate, v_lru_lambda, v_rg_norm_g, v_hg_lower_bound, v_hg_norm_g,
                                    v_w_out, v_ffn_norm_g, v_w_gate_up, v_w_down, v_final_norm_g)))
    w = {k: _to_2d(k, a) for k, a in w_raw.items()}
    m = {k: _to_2d(k, a) for k, a in m_raw.items()}
    v = {k: _to_2d(k, a) for k, a in v_raw.items()}

    x_i, y_i, c_i = _place()
    core = jnp.reshape(c_i, (1,)).astype(jnp.int32)
    chip = jnp.reshape(2 * x_i + y_i, (1,)).astype(jnp.int32)
    chip_core = jnp.concatenate([chip, core])

    placed, (meta_full, cw_full) = _place_shards(w, [w["meta_tokens"], w["conv_w"]], chip)
    first, _ = _gather_weights(placed, [], ("w_in",), "gather_first", 1)
    rest, _ = _gather_weights(placed, [], ("w_out", "w_gate_up", "w_down"), "gather_rest", 2)
    full = {**first, **rest}

    seq = x.shape[1]
    small ={k: w[k] for k in SMALL_NAMES if k not in SHARDED_SMALL}
    small["conv_w"] = cw_full

    def reduce_to_chips(grads, names, tag, collective_ids):
        got = _exchange_halves(grads, names, "exchange_halves_" + tag, collective_ids[0])

        def chip_sums():
            return _chip_sum(grads, got, names, core, "chip_sum_" + tag)

        def send(sums):
            arrived = _send_chip_sums({n: sums[n][1] for n in names}, names, "send_chip_sums_" + tag,
                                      collective_ids[1])
            return {n: (sums[n][0], a) for n, a in zip(names, arrived)}

        return chip_sums, send

    ffn_names, mixer_names = ("w_gate_up", "w_down"), ("w_in", "w_out")
    loss, grad_x, grads, parts, parts_mixer = _local_step(
        x.reshape(seq, D_MODEL), meta_full, loss_target.reshape(seq, D_MODEL),
        w["w_in"], full["w_in"], full["w_out"], full["w_gate_up"], full["w_down"], small, chip,
        on_ffn_grads=lambda g: reduce_to_chips(g, ffn_names, "ffn", (3, 4)),
        on_mixer_grads=lambda g: reduce_to_chips(g, mixer_names, "mixer", (None, 5)))
    parts.update(parts_mixer)
    totals = _total(parts, chip_core)
    g_big = dict(zip(BIG_NAMES, _share_totals(totals)))

    pieces = {k: grads[k] for k in VEC_ROW if k != "loss"}
    pieces["loss"] = loss
    vec, gates = _all_reduce_small(pieces, grads["w_gates"])
    loss_sum, res = _adamw_small(vec, gates, w, m, v)
    updates = _adamw_big(w, g_big, m, v)
    for n in BIG_NAMES:
        res[n] = (g_big[n],) + updates[n]

    out = [loss_sum.reshape(()), grad_x.reshape(1, seq, D_MODEL)]
    for j in range(4):
        out += [res[n][j].reshape(w_raw[n].shape) for n in WEIGHT_NAMES]
    return tuple(out)
```

```python
import functools
import math

import jax
import jax.numpy as jnp
from jax import lax
from jax.experimental import pallas as pl
from jax.experimental.pallas import tpu as pltpu
from jax.experimental.pallas import tpu_sc as plsc

F32 = jnp.float32
BF16 = jnp.bfloat16
HIGHEST = lax.Precision.HIGHEST
MESH = pl.DeviceIdType.MESH

D_MODEL = 1024
D_RG = 512
RG_HEAD_DIM = 64
D_HG = 512
HG_HEAD_DIM = 128
HG_HEADS = 4
CHUNK = 64
SUB = 16
N_SUB = CHUNK // SUB
N_META = 16
PAD = CHUNK - N_META
D_IN = 3072
D_FF = 2816
CONV_W = 4
LRU_C = 8.0
EPS = 1e-6
EXP_CLAMP = 80.0
GELU_C = math.sqrt(2.0 / math.pi)
GELU_A = 0.044715
N_CHIPS = 4

ADAM_LR = 0.001
ADAM_B1 = 0.9
ADAM_B2 = 0.999
ADAM_EPS = 1e-08
ADAM_WD = 0.01
ADAM_STEP = 10

VMEM_LIMIT = 56 * 1024 * 1024


def _params(*sem):
    return pltpu.CompilerParams(dimension_semantics=sem, vmem_limit_bytes=VMEM_LIMIT)


def _row_tile(rows, target):
    best = None
    for t in range(16, min(rows, target) + 1, 16):
        if rows % t == 0:
            best = t
    assert best is not None, rows
    return best


def _sigmoid(x):
    return 0.5 * jnp.tanh(0.5 * x) + 0.5


def _dot(a, b):
    return jnp.dot(a, b, preferred_element_type=F32)


def _dot_nt(a, b):
    return lax.dot_general(a, b, (((1,), (1,)), ((), ())), preferred_element_type=F32)


def _dot_tn(a, b):
    return lax.dot_general(a, b, (((0,), (0,)), ((), ())), preferred_element_type=F32)


def _rms(x):
    return lax.rsqrt(jnp.mean(x * x, axis=-1, keepdims=True) + EPS)


def _rms_bwd(dn, n, r):
    return r * (dn - n * jnp.mean(dn * n, axis=-1, keepdims=True))


def _gelu_parts(x):
    t = jnp.tanh(GELU_C * (x + GELU_A * x * x * x))
    g = 0.5 * x * (1.0 + t)
    dg = 0.5 * (1.0 + t) + 0.5 * x * (1.0 - t * t) * GELU_C * (1.0 + 3.0 * GELU_A * x * x)
    return g, dg


def _softplus_neg(lam):
    e = jnp.exp(-jnp.abs(lam))
    w = 1.0 + e
    log1p = jnp.where(w == 1.0, e, jnp.log(w) * e / (w - 1.0))
    return jnp.maximum(-lam, 0.0) + log1p


def _head_mask():
    r = lax.broadcasted_iota(jnp.int32, (D_RG, D_RG), 0) // RG_HEAD_DIM
    c = lax.broadcasted_iota(jnp.int32, (D_RG, D_RG), 1) // RG_HEAD_DIM
    return r == c


def _head_fold():
    r = lax.broadcasted_iota(jnp.int32, (D_RG, RG_HEAD_DIM), 0) % RG_HEAD_DIM
    c = lax.broadcasted_iota(jnp.int32, (D_RG, RG_HEAD_DIM), 1)
    return (r == c).astype(F32)


def _gate_weights(w_r, w_i):
    def body(wr_ref, wi_ref, o_ref):
        fold = _head_fold()
        mask = _head_mask()
        for k, ref in enumerate((wr_ref, wi_ref)):
            full = lax.dot_general(ref[...], fold, (((1,), (1,)), ((), ())),
                                   precision=HIGHEST, preferred_element_type=F32)
            o_ref[:, k * D_RG:(k + 1) * D_RG] = jnp.where(mask, full, 0.0).astype(BF16)

    return pl.pallas_call(
        body, out_shape=jax.ShapeDtypeStruct((D_RG, 2 * D_RG), BF16), name="gate_weights",
    )(w_r, w_i)


HEAD = PAD + N_META


def _window_copies(seq_hbm, buf, sems, tm):
    def first(to_vmem):
        seq, vm = seq_hbm.at[pl.ds(0, tm - HEAD)], buf.at[0, pl.ds(HEAD, tm - HEAD)]
        return pltpu.make_async_copy(seq, vm, sems.at[0]) if to_vmem else pltpu.make_async_copy(vm, seq, sems.at[0])

    def later(j, slot, to_vmem):
        seq, vm = seq_hbm.at[pl.ds(pl.multiple_of(j * tm - HEAD, 8), tm)], buf.at[slot]
        if to_vmem:
            return pltpu.make_async_copy(seq, vm, sems.at[slot])
        return pltpu.make_async_copy(vm, seq, sems.at[slot])

    return first, later


def _fetch_window(seq_hbm, buf, sems, i, n_steps, tm):
    first, later = _window_copies(seq_hbm, buf, sems, tm)
    slot = i % 2

    @pl.when(i == 0)
    def _():
        first(True).start()

    if n_steps > 1:
        @pl.when(i + 1 < n_steps)
        def _():
            later(i + 1, 1 - slot, True).start()

    @pl.when(i == 0)
    def _():
        first(True).wait()

    if n_steps > 1:
        @pl.when(i > 0)
        def _():
            later(i, slot, True).wait()

    return slot


def _in_proj_local(x, meta, g1, w_own, chip):
    T = x.shape[0] + HEAD
    tm = _row_tile(T, 832)
    n_steps = T // tm
    cols = BIG["w_in"][1]

    def body(s_ref, x_hbm, meta_ref, g_ref, w_ref, p_ref, u_ref, h_ref, buf, sems, wb):
        i = pl.program_id(0)
        slot = _fetch_window(x_hbm, buf, sems, i, n_steps, tm)

        @pl.when(i == 0)
        def _():
            buf[0, 0:PAD, :] = jnp.zeros((PAD, D_MODEL), F32)
            buf[0, PAD:HEAD, :] = meta_ref[...]
            wb[...] = w_ref[...].astype(BF16)

        h = buf[slot]
        h_ref[...] = h
        u = (h * _rms(h) * g_ref[...]).astype(BF16)
        u_ref[...] = u
        p_ref[...] = _dot(u, wb[...])

    return pl.pallas_call(
        body,
        grid_spec=pltpu.PrefetchScalarGridSpec(
            num_scalar_prefetch=1, grid=(n_steps,),
            in_specs=[pl.BlockSpec(memory_space=pl.ANY),
                      pl.BlockSpec((N_META, D_MODEL), lambda i, s: (0, 0)),
                      pl.BlockSpec((1, D_MODEL), lambda i, s: (0, 0)),
                      pl.BlockSpec((D_MODEL, cols), lambda i, s: (0, 0))],
            out_specs=[pl.BlockSpec((tm, cols), lambda i, s: (i, s[0])),
                       pl.BlockSpec((tm, D_MODEL), lambda i, s: (i, 0)),
                       pl.BlockSpec((tm, D_MODEL), lambda i, s: (i, 0))],
            scratch_shapes=[pltpu.VMEM((2, tm, D_MODEL), F32), pltpu.SemaphoreType.DMA((2,)),
                            pltpu.VMEM((D_MODEL, cols), BF16)]),
        out_shape=[jax.ShapeDtypeStruct((T, D_IN), F32), jax.ShapeDtypeStruct((T, D_MODEL), BF16),
                   jax.ShapeDtypeStruct((T, D_MODEL), F32)],
        name="in_proj_local", compiler_params=_params("arbitrary"),
    )(chip, x, meta, g1, w_own)


def _in_proj_rest(u, w_in, p, chip):
    T = u.shape[0]
    tm = _row_tile(T, 2080)
    cols = BIG["w_in"][1]
    block = lambda j, s: (s[0] + 1 + j) % N_CHIPS

    def body(s_ref, u_ref, w_ref, p_in_ref, p_ref):
        p_ref[...] = _dot(u_ref[...], w_ref[...])

    return pl.pallas_call(
        body,
        grid_spec=pltpu.PrefetchScalarGridSpec(
            num_scalar_prefetch=1, grid=(N_CHIPS - 1, T // tm),
            in_specs=[pl.BlockSpec((tm, D_MODEL), lambda j, i, s: (i, 0)),
                      pl.BlockSpec((D_MODEL, cols), lambda j, i, s: (0, block(j, s))), ANY],
            out_specs=pl.BlockSpec((tm, cols), lambda j, i, s: (i, block(j, s)))),
        out_shape=jax.ShapeDtypeStruct((T, D_IN), F32),
        input_output_aliases={3: 0},
        name="in_proj_rest", compiler_params=_params("arbitrary", "arbitrary"),
    )(chip, u, w_in, p)


def _scan_block_fwd(A, B, rowi):
    for d in (1, 2, 4):
        a_sh = pltpu.roll(A, d, axis=0)
        b_sh = pltpu.roll(B, d, axis=0)
        m = rowi >= d
        B = jnp.where(m, A * b_sh + B, B)
        A = jnp.where(m, A * a_sh, A)
    return A, B


def _scan_block_bwd(A, B, rowi):
    for d in (1, 2, 4):
        a_sh = pltpu.roll(A, 8 - d, axis=0)
        b_sh = pltpu.roll(B, 8 - d, axis=0)
        m = rowi < 8 - d
        B = jnp.where(m, A * b_sh + B, B)
        A = jnp.where(m, A * a_sh, A)
    return A, B


def _rg_gates(xc, w_ref, bg_ref, lam):
    pre = _dot(xc.astype(BF16), w_ref[...]) + bg_ref[...]
    r = _sigmoid(pre[:, :D_RG])
    ig = _sigmoid(pre[:, D_RG:])
    sp = _softplus_neg(lam)
    la = -LRU_C * sp * r
    a = jnp.exp(la)
    th = jnp.tanh(la)
    u = 1.0 - th
    rc = pl.reciprocal(u, approx=True)
    rc = rc * (2.0 - u * rc)
    rc = rc * (2.0 - u * rc)
    m2 = -2.0 * th * rc
    inv_m = lax.rsqrt(jnp.maximum(m2, 1e-30))
    return r, ig, sp, a, m2 * inv_m, inv_m


def _conv(ext, cw_ref, cb_ref, tm):
    xc = cb_ref[...] + cw_ref[0:1, :] * ext[8 - 3:8 - 3 + tm, :]
    for j in range(1, CONV_W):
        xc = xc + cw_ref[j:j + 1, :] * ext[8 - 3 + j:8 - 3 + j + tm, :]
    return xc


def _scan_unroll(blocks):
    return 4 if blocks % 4 == 0 else 2 if blocks % 2 == 0 else 1


def _rg_fwd(p, cw, cb, wg, bg, lam, rg_g):
    T = p.shape[0]
    tm = _row_tile(T, 832)
    unroll = _scan_unroll(tm // 8)

    def body(xg_ref, cw_ref, cb_ref, w_ref, bg_ref, lam_ref, g_ref, y_ref, h_ref, ext, a_s, b_s, carry):
        i = pl.program_id(0)

        @pl.when(i == 0)
        def _():
            ext[0:8, :] = jnp.zeros((8, D_RG), F32)
            carry[...] = jnp.zeros((1, D_RG), F32)

        ext[8:8 + tm, :] = xg_ref[:, :D_RG]
        xc = _conv(ext, cw_ref, cb_ref, tm)
        r, ig, sp, a, m, _ = _rg_gates(xc, w_ref, bg_ref, lam_ref[...])
        row = i * tm + lax.broadcasted_iota(jnp.int32, (tm, 1), 0)
        a_s[...] = a
        b_s[...] = jnp.where(row >= PAD, m * ig * xc, 0.0)
        rowi = lax.broadcasted_iota(jnp.int32, (8, D_RG), 0)

        def blk(j, c):
            for u in range(unroll):
                o = pl.multiple_of((j * unroll + u) * 8, 8)
                A, B = _scan_block_fwd(a_s[pl.ds(o, 8), :], b_s[pl.ds(o, 8), :], rowi)
                h = B + A * c
                h_ref[pl.ds(o, 8), :] = h
                c = h[7:8, :]
            return c

        carry[...] = lax.fori_loop(0, tm // (8 * unroll), blk, carry[...])
        ext[0:8, :] = ext[tm:tm + 8, :]
        g, _ = _gelu_parts(xg_ref[:, D_RG:])
        yy = g * h_ref[...]
        y_ref[...] = (yy * _rms(yy) * g_ref[...]).astype(BF16)

    vec = lambda n: pl.BlockSpec((1, n), lambda i: (0, 0))
    return pl.pallas_call(
        body, grid=(T // tm,),
        in_specs=[pl.BlockSpec((tm, 2 * D_RG), lambda i: (i, 0)),
                  pl.BlockSpec((CONV_W, D_RG), lambda i: (0, 0)), vec(D_RG),
                  pl.BlockSpec((D_RG, 2 * D_RG), lambda i: (0, 0)), vec(2 * D_RG), vec(D_RG), vec(D_RG)],
        out_specs=[pl.BlockSpec((tm, D_RG), lambda i: (i, 0)), pl.BlockSpec((tm, D_RG), lambda i: (i, 0))],
        out_shape=[jax.ShapeDtypeStruct((T, D_RG), BF16), jax.ShapeDtypeStruct((T, D_RG), F32)],
        scratch_shapes=[pltpu.VMEM((tm + 8, D_RG), F32), pltpu.VMEM((tm, D_RG), F32),
                        pltpu.VMEM((tm, D_RG), F32), pltpu.VMEM((1, D_RG), F32)],
        name="rg_fwd", compiler_params=_params("arbitrary"),
    )(p, cw, cb, wg, bg, lam, rg_g)


def _tri(lower):
    r = lax.broadcasted_iota(jnp.int32, (CHUNK, CHUNK), 0)
    c = lax.broadcasted_iota(jnp.int32, (CHUNK, CHUNK), 1)
    return ((c <= r) if lower else (c >= r)).astype(F32)


def _hg_gates(hq, hf, lbraw_ref, valid):
    lb = _sigmoid(lbraw_ref[0:1, :] - lbraw_ref[1:2, :])
    sq = _sigmoid(hq)
    q = hq * sq
    sf = _sigmoid(hf)
    f = lb + (1.0 - lb) * sf
    lf = jnp.where(valid, jnp.log(f), 0.0)
    b = jnp.dot(_tri(True), lf, precision=HIGHEST, preferred_element_type=F32)
    return lb, sq, q, sf, f, b


def _hg_head(qh, kh, bh):
    blk = lax.broadcasted_iota(jnp.int32, (CHUNK, 1), 0) // SUB
    b_last = bh[CHUNK - 1:CHUNK, :]
    refs = [bh[SUB * s:SUB * s + 1, :] for s in range(N_SUB)]
    r_sel = refs[N_SUB - 1]
    for s in range(N_SUB - 2, -1, -1):
        r_sel = jnp.where(blk == s, refs[s], r_sel)
    eb = jnp.exp(bh)
    eq = jnp.exp(bh - r_sel)
    ekh = jnp.exp(b_last - bh)
    ek = [jnp.exp(jnp.minimum(refs[s] - bh, EXP_CLAMP)) for s in range(N_SUB)]
    qe = qh * eq
    q_hat = jnp.concatenate([jnp.where(blk == s, qe, 0.0) for s in range(N_SUB)], axis=1)
    k_til = jnp.concatenate([kh * ek[s] for s in range(N_SUB)], axis=1)
    return blk, b_last, eb, eq, ekh, ek, q_hat, k_til


def _causal():
    r = lax.broadcasted_iota(jnp.int32, (CHUNK, CHUNK), 0)
    c = lax.broadcasted_iota(jnp.int32, (CHUNK, CHUNK), 1)
    return r >= c


def _chunks_per_step(n_chunks):
    for c in (13, 5, 4, 3, 2):
        if n_chunks % c == 0:
            return c
    return 1


def _hg_fwd(p, lbraw, hg_g):
    T = p.shape[0]
    n_chunks = T // CHUNK
    cps = _chunks_per_step(n_chunks)
    rows = cps * CHUNK

    def body(hq_ref, hf_ref, hi_ref, hg_ref, lb_ref, g_ref, y_ref, o_ref, st_all_ref, st):
        i = pl.program_id(0)

        @pl.when(i == 0)
        def _():
            st[...] = jnp.zeros_like(st)

        def chunk(j, carry):
            rs = pl.ds(pl.multiple_of(j * CHUNK, CHUNK), CHUNK)
            chunk_body(i * cps + j, hq_ref.at[rs, :], hf_ref.at[rs, :], hi_ref.at[rs, :], hg_ref.at[rs, :], lb_ref,
                       g_ref, y_ref.at[rs, :], o_ref.at[rs, :], st_all_ref.at[pl.ds(j, 1)], st)
            return carry

        lax.fori_loop(0, cps, chunk, 0, unroll=True)

    def chunk_body(n, hq_ref, hf_ref, hi_ref, hg_ref, lb_ref, g_ref, y_ref, o_ref, st_all_ref, st):
        valid = (n * CHUNK + lax.broadcasted_iota(jnp.int32, (CHUNK, 1), 0)) >= PAD
        hq, hf, v, hg = hq_ref[...], hf_ref[...], hi_ref[...], hg_ref[...]
        lb, sq, q, sf, f, b = _hg_gates(hq, hf, lb_ref, valid)
        k = 1.0 - f
        st_all_ref[0] = st[...]
        causal = _causal()
        v_t = v.T.astype(BF16)
        heads = [slice(h * HG_HEAD_DIM, (h + 1) * HG_HEAD_DIM) for h in range(HG_HEADS)]
        fac = []
        for sl in heads:
            qh, kh, bh = q[:, sl], k[:, sl], b[:, sl]
            _, b_last, eb, _, ekh, _, q_hat, k_til = _hg_head(qh, kh, bh)
            fac.append((jnp.exp(b_last), (qh * eb).astype(BF16), q_hat.astype(BF16), k_til.astype(BF16),
                        (kh * ekh).astype(BF16), v[:, sl].astype(BF16)))
        raw = []
        for sl, (_, q_til, q_hat, k_til, k_hat, _) in zip(heads, fac):
            st_h = st[sl, :]
            raw.append((_dot_nt(q_til, st_h.astype(BF16)), _dot_nt(q_hat, k_til), _dot(v_t[sl, :], k_hat), st_h))
        for sl, (e_last, _, _, _, _, vb), (inter, att, upd, st_h) in zip(heads, fac, raw):
            o = inter + _dot(jnp.where(causal, att, 0.0).astype(BF16), vb)
            st[sl, :] = st_h * e_last + upd
            o_ref[:, sl] = o
            hgh = hg[:, sl]
            y_ref[:, sl] = (o * _rms(o) * g_ref[...] * (hgh * _sigmoid(hgh))).astype(BF16)

    col = lambda j: pl.BlockSpec((rows, D_HG), lambda n: (n, j))
    return pl.pallas_call(
        body, grid=(n_chunks // cps,),
        in_specs=[col(2), col(3), col(4), col(5),
                  pl.BlockSpec((2, D_HG), lambda n: (0, 0)), pl.BlockSpec((1, HG_HEAD_DIM), lambda n: (0, 0))],
        out_specs=[pl.BlockSpec((rows, D_HG), lambda n: (n, 0)), pl.BlockSpec((rows, D_HG), lambda n: (n, 0)),
                   pl.BlockSpec((cps, D_HG, HG_HEAD_DIM), lambda n: (n, 0, 0))],
        out_shape=[jax.ShapeDtypeStruct((T, D_HG), BF16), jax.ShapeDtypeStruct((T, D_HG), F32),
                   jax.ShapeDtypeStruct((n_chunks, D_HG, HG_HEAD_DIM), F32)],
        scratch_shapes=[pltpu.VMEM((D_HG, HG_HEAD_DIM), F32)],
        name="hg_fwd", compiler_params=_params("arbitrary"),
    )(p, p, p, p, lbraw, hg_g)


def _ffn_fwd(h0, y_rg, y_hg, w_out, g2, w_gu, w_down, gf, target):
    T = h0.shape[0]
    tm = _row_tile(T, 320)
    n_steps = T // tm

    def body(h_ref, yr_ref, yh_ref, wo_ref, g2_ref, wgu_ref, wd_ref, gf_ref, t_hbm,
             h1_ref, v_ref, y_ref, gu_ref, act_ref, dh2_ref, dh2b_ref, loss_ref, gg_ref, tbuf, sems):
        i = pl.program_id(0)
        slot = _fetch_window(t_hbm, tbuf, sems, i, n_steps, tm)

        @pl.when(i == 0)
        def _():
            loss_ref[...] = jnp.zeros_like(loss_ref)
            gg_ref[...] = jnp.zeros_like(gg_ref)
            tbuf[0, 0:HEAD, :] = jnp.zeros((HEAD, D_MODEL), F32)

        y_ref[:, :D_RG] = yr_ref[...]
        y_ref[:, D_RG:] = yh_ref[...]
        h1 = h_ref[...] + _dot(y_ref[...], wo_ref[...])
        h1_ref[...] = h1
        v = (h1 * _rms(h1) * g2_ref[...]).astype(BF16)
        v_ref[...] = v

        gu = _dot(v, wgu_ref[...])
        gu_ref[...] = gu.astype(BF16)
        g = gu[:, :D_FF]
        act = (g * _sigmoid(g) * gu[:, D_FF:]).astype(BF16)
        act_ref[...] = act

        h2 = h1 + _dot(act, wd_ref[...])
        r = _rms(h2)
        n = h2 * r
        gf_ = gf_ref[...]
        row = i * tm + lax.broadcasted_iota(jnp.int32, (tm, 1), 0)
        err = jnp.where(row >= HEAD, n * gf_ - tbuf[slot], 0.0)
        loss_ref[...] += 0.5 * jnp.sum(jnp.mean(err * err, axis=-1, keepdims=True), axis=0, keepdims=True)
        dy = err * (1.0 / D_MODEL)
        gg_ref[...] += jnp.sum(dy * n, axis=0, keepdims=True)
        dh2 = _rms_bwd(dy * gf_, n, r)
        dh2_ref[...] = dh2
        dh2b_ref[...] = dh2.astype(BF16)

    row_spec = lambda n: pl.BlockSpec((tm, n), lambda i: (i, 0))
    vec = pl.BlockSpec((1, D_MODEL), lambda i: (0, 0))
    return pl.pallas_call(
        body, grid=(n_steps,),
        in_specs=[row_spec(D_MODEL), row_spec(D_RG), row_spec(D_HG), _resident((D_MODEL, D_MODEL)), vec,
                  _resident((D_MODEL, 2 * D_FF)), _resident((D_FF, D_MODEL)), vec,
                  pl.BlockSpec(memory_space=pl.ANY)],
        out_specs=[row_spec(D_MODEL), row_spec(D_MODEL), row_spec(D_MODEL), row_spec(2 * D_FF), row_spec(D_FF),
                   row_spec(D_MODEL), row_spec(D_MODEL), pl.BlockSpec((1, 1), lambda i: (0, 0)), vec],
        out_shape=[jax.ShapeDtypeStruct((T, D_MODEL), F32), jax.ShapeDtypeStruct((T, D_MODEL), BF16),
                   jax.ShapeDtypeStruct((T, D_MODEL), BF16), jax.ShapeDtypeStruct((T, 2 * D_FF), BF16),
                   jax.ShapeDtypeStruct((T, D_FF), BF16), jax.ShapeDtypeStruct((T, D_MODEL), F32),
                   jax.ShapeDtypeStruct((T, D_MODEL), BF16), jax.ShapeDtypeStruct((1, 1), F32),
                   jax.ShapeDtypeStruct((1, D_MODEL), F32)],
        scratch_shapes=[pltpu.VMEM((2, tm, D_MODEL), F32), pltpu.SemaphoreType.DMA((2,))],
        name="ffn_fwd", compiler_params=_params("arbitrary"),
    )(h0, y_rg, y_hg, w_out, g2, w_gu, w_down, gf, target)


def _resident(shape):
    return pl.BlockSpec(shape, lambda i: (0,) * len(shape), pipeline_mode=pl.Buffered(1))


def _ffn_bwd(dh2b, gu, w_down, w_gu, h1, g2, dh2, w_out):
    T = h1.shape[0]
    tm = _row_tile(T, 320)

    def body(d_ref, gu_ref, wd_ref, wgu_ref, h_ref, g_ref, d2_ref, wo_ref, dgu_ref, dh1_ref, dh1b_ref, dy_ref, gg_ref):
        i = pl.program_id(0)

        @pl.when(i == 0)
        def _():
            gg_ref[...] = jnp.zeros_like(gg_ref)

        dact = _dot_nt(d_ref[...], wd_ref[...]).astype(BF16)
        g = gu_ref[:, :D_FF]
        u = gu_ref[:, D_FF:]
        s = _sigmoid(g)
        dgu_ref[:, :D_FF] = dact * u * (s * (1.0 + g * (1.0 - s)))
        dgu_ref[:, D_FF:] = dact * (g * s)

        dv = _dot_nt(dgu_ref[...], wgu_ref[...])
        h1_ = h_ref[...]
        r = _rms(h1_)
        n = h1_ * r
        gg_ref[...] += jnp.sum(dv * n, axis=0, keepdims=True)
        dh1 = d2_ref[...] + _rms_bwd(dv * g_ref[...], n, r)
        dh1_ref[...] = dh1
        db = dh1.astype(BF16)
        dh1b_ref[...] = db
        dy_ref[...] = _dot_nt(db, wo_ref[...])

    row = lambda n: pl.BlockSpec((tm, n), lambda i: (i, 0))
    return pl.pallas_call(
        body, grid=(T // tm,),
        in_specs=[row(D_MODEL), row(2 * D_FF), _resident((D_FF, D_MODEL)), _resident((D_MODEL, 2 * D_FF)),
                  row(D_MODEL), pl.BlockSpec((1, D_MODEL), lambda i: (0, 0)), row(D_MODEL),
                  _resident((D_MODEL, D_MODEL))],
        out_specs=[row(2 * D_FF), row(D_MODEL), row(D_MODEL), row(D_MODEL),
                   pl.BlockSpec((1, D_MODEL), lambda i: (0, 0))],
        out_shape=[jax.ShapeDtypeStruct((T, 2 * D_FF), BF16), jax.ShapeDtypeStruct((T, D_MODEL), F32),
                   jax.ShapeDtypeStruct((T, D_MODEL), BF16), jax.ShapeDtypeStruct((T, D_MODEL), F32),
                   jax.ShapeDtypeStruct((1, D_MODEL), F32)],
        name="ffn_bwd", compiler_params=_params("arbitrary"),
    )(dh2b, gu, w_down, w_gu, h1, g2, dh2, w_out)


def _rg_bwd(p, hs, dy, dp, cw, cb, wg, bg, lam, rg_g):
    T = p.shape[0]
    tm = _row_tile(T, 832)
    nt = T // tm
    hb = tm // 8
    unroll = _scan_unroll(hb)

    def body(xg_ref, xh_ref, h_ref, hh_ref, dy_ref, dp_in_ref, cw_ref, cb_ref, w_ref, bg_ref, lam_ref, g_ref,
             dp_ref, gcw_ref, gcb_ref, gw_ref, gbg_ref, glam_ref, gg_ref,
             ext, dext, a_s, b_s, d_s, gacc, carry_d, carry_a):
        i = pl.program_id(0)
        t_idx = nt - 1 - i

        @pl.when(i == 0)
        def _():
            dext[tm:tm + 8, :] = jnp.zeros((8, D_RG), F32)
            carry_d[...] = jnp.zeros_like(carry_d)
            carry_a[...] = jnp.zeros_like(carry_a)
            gacc[...] = jnp.zeros_like(gacc)
            for ref in (gcw_ref, gcb_ref, gbg_ref, glam_ref, gg_ref, gw_ref):
                ref[...] = jnp.zeros_like(ref)

        first = t_idx == 0
        ext[0:8, :] = jnp.where(first, 0.0, xh_ref[:, :D_RG])
        ext[8:8 + tm, :] = xg_ref[:, :D_RG]
        xc = _conv(ext, cw_ref, cb_ref, tm)
        lam_ = lam_ref[...]
        r, ig, sp, a, m, inv_m = _rg_gates(xc, w_ref, bg_ref, lam_)
        row = t_idx * tm + lax.broadcasted_iota(jnp.int32, (tm, 1), 0)
        valid = row >= PAD

        gr = xg_ref[:, D_RG:]
        g, dgelu = _gelu_parts(gr)
        h = h_ref[...]
        yy = g * h
        rr = _rms(yy)
        nn = yy * rr
        dy_ = dy_ref[...]
        gg_ref[...] += jnp.sum(dy_ * nn, axis=0, keepdims=True)
        dyy = _rms_bwd(dy_ * g_ref[...], nn, rr)
        dp_ref[:, D_RG:] = (dyy * h * dgelu).astype(BF16)

        a_s[...] = a
        b_s[...] = dyy * g
        rowi = lax.broadcasted_iota(jnp.int32, (8, D_RG), 0)

        def blk(jj, c):
            cd, ca = c
            for u in range(unroll):
                o = pl.multiple_of((hb - 1 - (jj * unroll + u)) * 8, 8)
                a_blk = a_s[pl.ds(o, 8), :]
                a_next = jnp.where(rowi == 7, ca, pltpu.roll(a_blk, 7, axis=0))
                A, B = _scan_block_bwd(a_next, b_s[pl.ds(o, 8), :], rowi)
                d = B + A * cd
                d_s[pl.ds(o, 8), :] = d
                cd, ca = d[0:1, :], a_blk[0:1, :]
            return cd, ca

        cd, ca = lax.fori_loop(0, hb // unroll, blk, (carry_d[...], carry_a[...]))
        carry_d[...] = cd
        carry_a[...] = ca
        delta = d_s[...]

        h_last_prev = jnp.where(first, 0.0, hh_ref[7:8, :])
        row0 = lax.broadcasted_iota(jnp.int32, (tm, 1), 0) == 0
        h_prev = jnp.where(row0, h_last_prev, pltpu.roll(h, 1, axis=0))
        dbx = jnp.where(valid, delta, 0.0)
        da = delta * h_prev
        di = dbx * m * xc
        dm = dbx * ig * xc
        dla = a * (da - dm * a * inv_m)
        dla = jnp.where(valid, dla, 0.0)
        glam_ref[...] += jnp.sum(dla * r, axis=0, keepdims=True) * (LRU_C / (1.0 + jnp.exp(lam_)))
        dr = (-LRU_C) * sp * dla
        dpre = jnp.concatenate([dr * r * (1.0 - r), di * ig * (1.0 - ig)], axis=1)
        gbg_ref[...] += jnp.sum(dpre, axis=0, keepdims=True)
        dpre_b = dpre.astype(BF16)
        gacc[...] += _dot_tn(xc.astype(BF16), dpre_b)
        dxc = dbx * m * ig + _dot_nt(dpre_b, w_ref[...])
        gcb_ref[...] += jnp.sum(dxc, axis=0, keepdims=True)
        for j in range(CONV_W):
            gcw_ref[j:j + 1, :] += jnp.sum(dxc * ext[8 - 3 + j:8 - 3 + j + tm, :], axis=0, keepdims=True)
        dext[0:tm, :] = dxc
        dxr = cw_ref[0:1, :] * dext[3:3 + tm, :]
        for j in range(1, CONV_W):
            dxr = dxr + cw_ref[j:j + 1, :] * dext[3 - j:3 - j + tm, :]
        dp_ref[:, :D_RG] = dxr.astype(BF16)
        dext[tm:tm + 8, :] = dext[0:8, :]

        @pl.when(i == nt - 1)
        def _():
            fold = _head_fold()
            mask = _head_mask()
            for k in range(2):
                blockdiag = jnp.where(mask, gacc[:, k * D_RG:(k + 1) * D_RG], 0.0)
                gw_ref[k * D_RG:(k + 1) * D_RG, :] = jnp.dot(blockdiag, fold, precision=HIGHEST,
                                                             preferred_element_type=F32)

    vec = lambda n: pl.BlockSpec((1, n), lambda i: (0, 0))
    rev = lambda n: pl.BlockSpec((tm, n), lambda i: (nt - 1 - i, 0))
    halo = lambda n: pl.BlockSpec((8, n), lambda i: (jnp.maximum((nt - 1 - i) * hb - 1, 0), 0))
    return pl.pallas_call(
        body, grid=(nt,),
        in_specs=[rev(2 * D_RG), halo(2 * D_RG), rev(D_RG), halo(D_RG), rev(D_RG), ANY,
                  pl.BlockSpec((CONV_W, D_RG), lambda i: (0, 0)), vec(D_RG),
                  pl.BlockSpec((D_RG, 2 * D_RG), lambda i: (0, 0)), vec(2 * D_RG), vec(D_RG), vec(D_RG)],
        out_specs=[rev(2 * D_RG), pl.BlockSpec((CONV_W, D_RG), lambda i: (0, 0)), vec(D_RG),
                   pl.BlockSpec((2 * D_RG, RG_HEAD_DIM), lambda i: (0, 0)), vec(2 * D_RG), vec(D_RG), vec(D_RG)],
        input_output_aliases={5: 0},
        out_shape=[jax.ShapeDtypeStruct((T, D_IN), BF16), jax.ShapeDtypeStruct((CONV_W, D_RG), F32),
                   jax.ShapeDtypeStruct((1, D_RG), F32), jax.ShapeDtypeStruct((2 * D_RG, RG_HEAD_DIM), F32),
                   jax.ShapeDtypeStruct((1, 2 * D_RG), F32), jax.ShapeDtypeStruct((1, D_RG), F32),
                   jax.ShapeDtypeStruct((1, D_RG), F32)],
        scratch_shapes=[pltpu.VMEM((tm + 8, D_RG), F32), pltpu.VMEM((tm + 8, D_RG), F32),
                        pltpu.VMEM((tm, D_RG), F32), pltpu.VMEM((tm, D_RG), F32), pltpu.VMEM((tm, D_RG), F32),
                        pltpu.VMEM((D_RG, 2 * D_RG), F32), pltpu.VMEM((1, D_RG), F32), pltpu.VMEM((1, D_RG), F32)],
        name="rg_bwd", compiler_params=_params("arbitrary"),
    )(p, p, hs, hs, dy, dp, cw, cb, wg, bg, lam, rg_g)


def _hg_bwd(p, o_all, st_all, dy, lbraw, hg_g):
    T = p.shape[0]
    n_chunks = T // CHUNK
    cps = _chunks_per_step(n_chunks)
    rows = cps * CHUNK
    n_steps = n_chunks // cps

    def body(hq_ref, hf_ref, hi_ref, hg_ref, o_ref, st_ref, dy_ref, lb_ref, g_ref,
             dp_ref, glb_ref, gg_ref, dst):
        i = pl.program_id(0)

        @pl.when(i == 0)
        def _():
            dst[...] = jnp.zeros_like(dst)
            glb_ref[...] = jnp.zeros_like(glb_ref)
            gg_ref[...] = jnp.zeros_like(gg_ref)

        dp_ref[:, :2 * D_RG] = jnp.zeros((rows, 2 * D_RG), BF16)

        def chunk(jj, carry):
            j = cps - 1 - jj
            rs = pl.ds(pl.multiple_of(j * CHUNK, CHUNK), CHUNK)
            chunk_body((n_steps - 1 - i) * cps + j, hq_ref.at[rs, :], hf_ref.at[rs, :], hi_ref.at[rs, :],
                       hg_ref.at[rs, :], o_ref.at[rs, :], st_ref.at[pl.ds(j, 1)], dy_ref.at[rs, :], lb_ref, g_ref,
                       dp_ref.at[rs, pl.ds(2 * D_RG, 4 * D_HG)], glb_ref, gg_ref, dst)
            return carry

        lax.fori_loop(0, cps, chunk, 0, unroll=True)

    def chunk_body(n, hq_ref, hf_ref, hi_ref, hg_ref, o_ref, st_ref, dy_ref, lb_ref, g_ref,
                   dp_ref, glb_ref, gg_ref, dst):
        valid = (n * CHUNK + lax.broadcasted_iota(jnp.int32, (CHUNK, 1), 0)) >= PAD
        hq, hf, v, hg = hq_ref[...], hf_ref[...], hi_ref[...], hg_ref[...]
        lb, sq, q, sf, f, b = _hg_gates(hq, hf, lb_ref, valid)
        k = 1.0 - f
        causal = _causal()
        r_i = lax.broadcasted_iota(jnp.int32, (CHUNK, CHUNK), 0)
        c_i = lax.broadcasted_iota(jnp.int32, (CHUNK, CHUNK), 1)
        causal_t = r_i <= c_i
        is_last = lax.broadcasted_iota(jnp.int32, (CHUNK, 1), 0) == CHUNK - 1
        g_ = g_ref[...]
        db_parts, dq_parts, dk_parts = [], [], []
        gg = jnp.zeros((1, HG_HEAD_DIM), F32)
        heads = [slice(h * HG_HEAD_DIM, (h + 1) * HG_HEAD_DIM) for h in range(HG_HEADS)]

        do_parts = []
        for h, sl in enumerate(heads):
            o = o_ref[:, sl]
            ro = _rms(o)
            no = o * ro
            hgh = hg[:, sl]
            sg = _sigmoid(hgh)
            dyh = dy_ref[:, sl]
            dp_ref[:, 3 * D_HG + h * HG_HEAD_DIM:3 * D_HG + (h + 1) * HG_HEAD_DIM] = (
                dyh * no * g_ * sg * (1.0 + hgh * (1.0 - sg))).astype(BF16)
            dng = dyh * hgh * sg
            gg = gg + jnp.sum(dng * no, axis=0, keepdims=True)
            do_parts.append(_rms_bwd(dng * g_, no, ro))
        do_t = jnp.concatenate(do_parts, axis=1).T.astype(BF16)

        fac = []
        for sl, do in zip(heads, do_parts):
            qh, kh, bh = q[:, sl], k[:, sl], b[:, sl]
            blk, b_last, eb, eq, ekh, ek, q_hat, k_til = _hg_head(qh, kh, bh)
            fac.append(dict(qh=qh, kh=kh, blk=blk, e_last=jnp.exp(b_last), eb=eb, eq=eq, ekh=ekh, ek=ek,
                            q_til=qh * eb, k_hat=kh * ekh, qhb=q_hat.astype(BF16), ktb=k_til.astype(BF16),
                            vb=v[:, sl].astype(BF16), dob=do.astype(BF16)))

        first = []
        for sl, t in zip(heads, fac):
            st_h = st_ref[0, sl, :]
            dst_h = dst[sl, :]
            dstb = dst_h.astype(BF16)
            first.append(dict(
                att_t=_dot_nt(t["ktb"], t["qhb"]), datt=_dot_nt(t["dob"], t["vb"]),
                datt_t=_dot_nt(t["vb"], t["dob"]), dk_hat=_dot(t["vb"], dstb),
                dv=_dot_nt(t["k_hat"].astype(BF16), dstb), dq_til=_dot(t["dob"], st_h.astype(BF16)),
                state=t["e_last"] * jnp.sum(dst_h * st_h, axis=0, keepdims=True)))
            dst[sl, :] = dst_h * t["e_last"] + _dot(do_t[sl, :], t["q_til"].astype(BF16))

        for h, (t, m) in enumerate(zip(fac, first)):
            qh, kh, blk, eb, eq, ekh, ek = t["qh"], t["kh"], t["blk"], t["eb"], t["eq"], t["ekh"], t["ek"]
            q_til, k_hat, qhb, ktb, dob = t["q_til"], t["k_hat"], t["qhb"], t["ktb"], t["dob"]
            dk_hat, dq_til = m["dk_hat"], m["dq_til"]
            dv = m["dv"] + _dot(jnp.where(causal_t, m["att_t"], 0.0).astype(BF16), dob)
            dq_hat = _dot(jnp.where(causal, m["datt"], 0.0).astype(BF16), ktb)
            dk_til = _dot(jnp.where(causal_t, m["datt_t"], 0.0).astype(BF16), qhb)
            db_last = jnp.sum(dk_hat * k_hat, axis=0, keepdims=True) + m["state"]
            dq_sel = dq_hat[:, (N_SUB - 1) * HG_HEAD_DIM:]
            for s in range(N_SUB - 2, -1, -1):
                dq_sel = jnp.where(blk == s, dq_hat[:, s * HG_HEAD_DIM:(s + 1) * HG_HEAD_DIM], dq_sel)
            dq_a = dq_sel * eq
            dk_a = dk_til[:, :HG_HEAD_DIM] * ek[0]
            for s in range(1, N_SUB):
                dk_a = dk_a + dk_til[:, s * HG_HEAD_DIM:(s + 1) * HG_HEAD_DIM] * ek[s]
            db_att = qhb.astype(F32) * dq_hat - ktb.astype(F32) * dk_til
            db = dq_til * q_til - dk_hat * k_hat
            for s in range(N_SUB):
                db = db + db_att[:, s * HG_HEAD_DIM:(s + 1) * HG_HEAD_DIM]
            db_parts.append(jnp.where(is_last, db + db_last, db))
            dq_parts.append(dq_til * eb + dq_a)
            dk_parts.append(dk_hat * ekh + dk_a)
            dp_ref[:, 2 * D_HG + h * HG_HEAD_DIM:2 * D_HG + (h + 1) * HG_HEAD_DIM] = dv.astype(BF16)

        gg_ref[...] += gg
        db = jnp.concatenate(db_parts, axis=1)
        dq = jnp.concatenate(dq_parts, axis=1)
        dk = jnp.concatenate(dk_parts, axis=1)
        dlf = jnp.where(valid, jnp.dot(_tri(False), db, precision=HIGHEST, preferred_element_type=F32), 0.0)
        dp_ref[:, :D_HG] = (dq * sq * (1.0 + hq * (1.0 - sq))).astype(BF16)
        df = dlf / f - dk
        dlb = jnp.sum(df * (1.0 - sf), axis=0, keepdims=True) * lb * (1.0 - lb)
        glb_ref[0:1, :] += dlb
        glb_ref[1:2, :] += -dlb
        dp_ref[:, D_HG:2 * D_HG] = (df * (1.0 - lb) * sf * (1.0 - sf)).astype(BF16)

    rev = lambda j: pl.BlockSpec((rows, D_HG), lambda i: (n_steps - 1 - i, j))
    return pl.pallas_call(
        body, grid=(n_steps,),
        in_specs=[rev(2), rev(3), rev(4), rev(5), rev(0),
                  pl.BlockSpec((cps, D_HG, HG_HEAD_DIM), lambda i: (n_steps - 1 - i, 0, 0)), rev(1),
                  pl.BlockSpec((2, D_HG), lambda i: (0, 0)), pl.BlockSpec((1, HG_HEAD_DIM), lambda i: (0, 0))],
        out_specs=[pl.BlockSpec((rows, D_IN), lambda i: (n_steps - 1 - i, 0)),
                   pl.BlockSpec((2, D_HG), lambda i: (0, 0)), pl.BlockSpec((1, HG_HEAD_DIM), lambda i: (0, 0))],
        out_shape=[jax.ShapeDtypeStruct((T, D_IN), BF16), jax.ShapeDtypeStruct((2, D_HG), F32),
                   jax.ShapeDtypeStruct((1, HG_HEAD_DIM), F32)],
        scratch_shapes=[pltpu.VMEM((D_HG, HG_HEAD_DIM), F32)],
        name="hg_bwd", compiler_params=_params("arbitrary"),
    )(p, p, p, p, o_all, st_all, dy, lbraw, hg_g)


def _in_bwd(dp, w_in, h0, g1, dh1):
    T = h0.shape[0]
    tm = _row_tile(T, 416)
    n_steps = T // tm

    def body(dp_ref, w_ref, h_ref, g_ref, d1_ref, gx_hbm, gmeta_ref, gg_ref, buf, sems):
        i = pl.program_id(0)
        first, later = _window_copies(gx_hbm, buf, sems, tm)
        slot = i % 2

        @pl.when(i == 0)
        def _():
            gg_ref[...] = jnp.zeros_like(gg_ref)

        if n_steps > 2:
            @pl.when(i == 2)
            def _():
                first(False).wait()

            @pl.when(i > 2)
            def _():
                later(i - 2, slot, False).wait()

        du = _dot_nt(dp_ref[...], w_ref[...])
        h0_ = h_ref[...]
        r = _rms(h0_)
        n = h0_ * r
        gg_ref[...] += jnp.sum(du * n, axis=0, keepdims=True)
        dh0 = d1_ref[...] + _rms_bwd(du * g_ref[...], n, r)
        buf[slot] = dh0

        @pl.when(i == 0)
        def _():
            gmeta_ref[...] = dh0[PAD:HEAD, :]
            first(False).start()

        if n_steps > 1:
            @pl.when(i > 0)
            def _():
                later(i, slot, False).start()

        @pl.when(i == n_steps - 1)
        def _():
            if n_steps == 1:
                first(False).wait()
            else:
                if n_steps == 2:
                    first(False).wait()
                else:
                    later(i - 1, 1 - slot, False).wait()
                later(i, slot, False).wait()

    row = lambda n: pl.BlockSpec((tm, n), lambda i: (i, 0))
    return pl.pallas_call(
        body, grid=(n_steps,),
        in_specs=[row(D_IN), pl.BlockSpec((D_MODEL, D_IN), lambda i: (0, 0)),
                  row(D_MODEL), pl.BlockSpec((1, D_MODEL), lambda i: (0, 0)), row(D_MODEL)],
        out_specs=[pl.BlockSpec(memory_space=pl.ANY), pl.BlockSpec((N_META, D_MODEL), lambda i: (0, 0)),
                   pl.BlockSpec((1, D_MODEL), lambda i: (0, 0))],
        out_shape=[jax.ShapeDtypeStruct((T - HEAD, D_MODEL), F32), jax.ShapeDtypeStruct((N_META, D_MODEL), F32),
                   jax.ShapeDtypeStruct((1, D_MODEL), F32)],
        scratch_shapes=[pltpu.VMEM((2, tm, D_MODEL), F32), pltpu.SemaphoreType.DMA((2,))],
        name="in_bwd", compiler_params=_params("arbitrary"),
    )(dp, w_in, h0, g1, dh1)


def _col_tile(cols, target):
    best = None
    for t in range(128, min(cols, target) + 1, 128):
        if cols % t == 0:
            best = t
    assert best is not None, cols
    return best


MXU_DIM = 256


def _mxu_tile(cols, target):
    best = None
    for t in range(MXU_DIM, min(cols, target) + 1, MXU_DIM):
        if cols % t == 0:
            best = t
    assert best is not None, cols
    return best


def _weight_grad(a, b, name):
    T, M = a.shape
    N = b.shape[1]
    tm = _col_tile(M, 1408)
    tn = _mxu_tile(N, 768 if tm <= 1024 else 512)

    def body(a_ref, b_ref, o_ref):
        o_ref[...] = _dot_tn(a_ref[...], b_ref[...])

    return pl.pallas_call(
        body, grid=(M // tm, N // tn),
        in_specs=[pl.BlockSpec((T, tm), lambda m, n: (0, m)), pl.BlockSpec((T, tn), lambda m, n: (0, n))],
        out_specs=pl.BlockSpec((tm, tn), lambda m, n: (m, n)),
        out_shape=jax.ShapeDtypeStruct((M, N), F32),
        name=name, compiler_params=_params("parallel", "parallel"),
    )(a, b)


def _local_step(x, meta, target, w_in_own, w_in, w_out, w_gu, w_down, small, chip, on_ffn_grads=None,
                on_mixer_grads=None):
    wg = _gate_weights(small["w_rgate"], small["w_igate"])
    bg = jnp.concatenate([small["b_rgate"], small["b_igate"]], axis=1)

    p, u, h0 = _in_proj_local(x, meta, small["mix_norm_g"], w_in_own, chip)
    p = _in_proj_rest(u, w_in, p, chip)
    y_rg, hs = _rg_fwd(p, small["conv_w"], small["conv_b"], wg, bg, small["lru_lambda"], small["rg_norm_g"])
    y_hg, o_all, st_all = _hg_fwd(p, small["hg_lower_bound"], small["hg_norm_g"])
    h1, v, yb, gu, act, dh2, dh2b, loss, g_final = _ffn_fwd(
        h0, y_rg, y_hg, w_out, small["ffn_norm_g"], w_gu, w_down, small["final_norm_g"], target)

    g_w_down = _weight_grad(act, dh2b, "grad_w_down")
    dgu, dh1, dh1b, dy, g_ffn = _ffn_bwd(dh2b, gu, w_down, w_gu, h1, small["ffn_norm_g"], dh2, w_out)
    ffn_grads = {"w_gate_up": _weight_grad(v, dgu, "grad_w_gate_up"), "w_down": g_w_down}
    stages = on_ffn_grads(ffn_grads) if on_ffn_grads is not None else None
    dp, g_lb, g_hgn = _hg_bwd(p, o_all, st_all, dy, small["hg_lower_bound"], small["hg_norm_g"])
    early = late = None
    if stages is not None:
        chip_sums, send = stages
        sums = chip_sums()
        (dp, dy), sums = lax.optimization_barrier(((dp, dy), sums))
        early = send(sums)
    dp, g_cw, g_cb, g_wgate, g_bg, g_lam, g_rgn = _rg_bwd(
        p, hs, dy, dp, small["conv_w"], small["conv_b"], wg, bg, small["lru_lambda"], small["rg_norm_g"])
    mixer_grads = {"w_in": _weight_grad(u, dp, "grad_w_in"), "w_out": _weight_grad(yb, dh1b, "grad_w_out")}
    if on_mixer_grads is not None:
        chip_sums, send = on_mixer_grads(mixer_grads)
        sums = chip_sums()
        (dp, dh1), sums = lax.optimization_barrier(((dp, dh1), sums))
        late = send(sums)
    grad_x, g_meta, g_mix = _in_bwd(dp, w_in, h0, small["mix_norm_g"], dh1)

    grads = {
        "w_in": mixer_grads["w_in"], "w_out": mixer_grads["w_out"],
        "w_gate_up": ffn_grads["w_gate_up"], "w_down": ffn_grads["w_down"],
        "meta_tokens": g_meta, "mix_norm_g": g_mix, "conv_w": g_cw, "conv_b": g_cb, "w_gates": g_wgate,
        "b_rgate": g_bg[:, :D_RG], "b_igate": g_bg[:, D_RG:], "lru_lambda": g_lam, "rg_norm_g": g_rgn,
        "hg_lower_bound": g_lb, "hg_norm_g": g_hgn, "ffn_norm_g": g_ffn, "final_norm_g": g_final,
    }
    return loss, grad_x, grads, early, late


ANY = pl.BlockSpec(memory_space=pl.ANY)
HALF = D_MODEL // 2

BIG = {"w_in": (D_MODEL, D_IN // N_CHIPS, True), "w_gate_up": (D_MODEL, 2 * D_FF // N_CHIPS, True),
       "w_out": (D_MODEL // N_CHIPS, D_MODEL, False), "w_down": (D_FF // N_CHIPS, D_MODEL, False)}
BIG_NAMES = tuple(BIG)
N_BIG = len(BIG_NAMES)


def _full_shape(name):
    rows, cols, by_col = BIG[name]
    return (rows, cols * N_CHIPS) if by_col else (rows * N_CHIPS, cols)


def _place():
    return lax.axis_index("x"), lax.axis_index("y"), lax.axis_index("c")


def _chip_of(x, y, r):
    fx, fy = (r + 1) >> 1, (r + 1) & 1
    return (1 - x if fx else x), (1 - y if fy else y)


def _half_of(ref, by_col, half):
    start = pl.multiple_of(half * HALF, 128)
    return ref.at[pl.ds(start, HALF), :] if by_col else ref.at[:, pl.ds(start, HALF)]


def _shard_of(ref, name, chip):
    rows, cols, by_col = BIG[name]
    if by_col:
        return ref.at[:, pl.ds(pl.multiple_of(chip * cols, 128), cols)]
    return ref.at[pl.ds(pl.multiple_of(chip * rows, 16), rows), :]


def _shard_half_of(ref, name, chip, half):
    rows, cols, by_col = BIG[name]
    start = pl.multiple_of(half * HALF, 128)
    if by_col:
        return ref.at[pl.ds(start, HALF), pl.ds(pl.multiple_of(chip * cols, 128), cols)]
    return ref.at[pl.ds(pl.multiple_of(chip * rows, 16), rows), pl.ds(start, HALF)]


def _remote(src, dst, send_sems, recv_sems, k, dev):
    return pltpu.make_async_remote_copy(src_ref=src, dst_ref=dst, send_sem=send_sems.at[k], recv_sem=recv_sems.at[k],
                                        device_id=dev, device_id_type=MESH)


def _place_shards(w, small, chip):
    steps = 4
    ns = len(small)
    in_specs, out_specs = [], []
    for name in BIG_NAMES:
        rows, cols, by_col = BIG[name]
        tr = rows // steps
        in_specs.append(pl.BlockSpec((tr, cols), lambda i, s: (i, 0)))
        if by_col:
            out_specs.append(pl.BlockSpec((tr, cols), lambda i, s: (i, s[0])))
        else:
            out_specs.append(pl.BlockSpec((tr, cols), lambda i, s: (s[0] * steps + i, 0)))

    def body(s_ref, *refs):
        ins, small_in = refs[:N_BIG], refs[N_BIG:N_BIG + ns]
        outs, small_out = refs[N_BIG + ns:2 * N_BIG + ns], refs[2 * N_BIG + ns:2 * (N_BIG + ns)]
        send_sems, recv_sems, local_sems = refs[2 * (N_BIG + ns):]
        i = pl.program_id(0)
        x, y, c = _place()
        chip_ = 2 * x + y
        others = [_chip_of(x, y, r) for r in range(3)]

        def block(a, q):
            cols = small[a].shape[1]
            return small_out[a].at[:, pl.ds(pl.multiple_of(q * cols, 128), cols)]

        def local(a):
            return pltpu.make_async_copy(small_in[a], block(a, chip_), local_sems.at[a])

        def remote(a, r):
            qx, qy = others[r]
            return _remote(small_in[a], block(a, chip_), send_sems, recv_sems, 3 * a + r, (qx, qy, c))

        @pl.when(i == 0)
        def _():
            for a in range(ns):
                local(a).start()
                for r in range(3):
                    remote(a, r).start()

        for a in range(N_BIG):
            outs[a][...] = ins[a][...].astype(BF16)

        @pl.when(i == steps - 1)
        def _():
            for a in range(ns):
                for r, (qx, qy) in enumerate(others):
                    landed = block(a, 2 * qx + qy)
                    _remote(landed, landed, send_sems, recv_sems, 3 * a + r, (qx, qy, c)).wait_recv()
                for r in range(3):
                    remote(a, r).wait_send()
                local(a).wait()

    out = pl.pallas_call(
        body,
        grid_spec=pltpu.PrefetchScalarGridSpec(
            num_scalar_prefetch=1, grid=(steps,), in_specs=in_specs + [ANY] * ns, out_specs=out_specs + [ANY] * ns,
            scratch_shapes=[pltpu.SemaphoreType.DMA((3 * ns,)), pltpu.SemaphoreType.DMA((3 * ns,)),
                            pltpu.SemaphoreType.DMA((ns,))]),
        out_shape=([jax.ShapeDtypeStruct(_full_shape(name), BF16) for name in BIG_NAMES]
                   + [jax.ShapeDtypeStruct((s.shape[0], s.shape[1] * N_CHIPS), F32) for s in small]),
        name="place_shards", compiler_params=_params("arbitrary"),
    )(chip, *[w[name] for name in BIG_NAMES], *small)
    return dict(zip(BIG_NAMES, out[:N_BIG])), list(out[N_BIG:])


def _gather_weights(placed, small, names, label, collective_id):
    n, ns = len(names), len(small)
    hbm = pltpu.MemorySpace.HBM
    outs = [jax.new_ref(placed[nm], memory_space=hbm) for nm in names]
    small_in = [jax.new_ref(s, memory_space=hbm) for s in small]
    small_out = [jax.empty_ref(jax.ShapeDtypeStruct((s.shape[0], s.shape[1] * N_CHIPS), F32), memory_space=hbm)
                 for s in small]
    n_sems = 6 * n + 3 * ns

    @pl.kernel(mesh=plsc.ScalarSubcoreMesh(axis_name="seq", num_cores=1), name=label, out_type=(),
               scratch_types=(pltpu.SemaphoreType.DMA((n_sems,)), pltpu.SemaphoreType.DMA((n_sems,)),
                              pltpu.SemaphoreType.DMA((max(ns, 1),))),
               compiler_params=pltpu.CompilerParams(collective_id=collective_id))
    def launch(send_sems, recv_sems, local_sems):
        x, y, c = _place()
        chip = 2 * x + y
        sibling = (x, y, 1 - c)
        others = [_chip_of(x, y, r) for r in range(3)]
        _handshake([(qx, qy, c) for qx, qy in others] + [sibling])

        def small_block(a, q):
            cols = small[a].shape[1]
            return small_out[a].at[:, pl.ds(pl.multiple_of(q * cols, 128), cols)]

        local = [pltpu.make_async_copy(small_in[a], small_block(a, chip), local_sems.at[a]) for a in range(ns)]
        for cp in local:
            cp.start()

        sends = []
        for a, name in enumerate(names):
            mine = _shard_half_of(outs[a], name, chip, c)
            for r, (qx, qy) in enumerate(others):
                sends.append(_remote(mine, mine, send_sems, recv_sems, 6 * a + r, (qx, qy, c)))
        for a in range(ns):
            for r, (qx, qy) in enumerate(others):
                sends.append(_remote(small_in[a], small_block(a, chip), send_sems, recv_sems,
                                     6 * n + 3 * a + r, (qx, qy, c)))
        for cp in sends:
            cp.start()

        forwards = []
        for a, name in enumerate(names):
            for r, (qx, qy) in enumerate(others):
                landed = _shard_half_of(outs[a], name, 2 * qx + qy, c)
                _remote(landed, landed, send_sems, recv_sems, 6 * a + r, (qx, qy, c)).wait_recv()
                fwd = _remote(landed, landed, send_sems, recv_sems, 6 * a + 3 + r, sibling)
                fwd.start()
                forwards.append(fwd)
        for a in range(ns):
            for r, (qx, qy) in enumerate(others):
                landed = small_block(a, 2 * qx + qy)
                _remote(landed, landed, send_sems, recv_sems, 6 * n + 3 * a + r, (qx, qy, c)).wait_recv()
        for a, name in enumerate(names):
            for r, (qx, qy) in enumerate(others):
                landed = _shard_half_of(outs[a], name, 2 * qx + qy, 1 - c)
                _remote(landed, landed, send_sems, recv_sems, 6 * a + 3 + r, sibling).wait_recv()
        for cp in sends + forwards:
            cp.wait_send()
        for cp in local:
            cp.wait()

    launch()
    return {nm: ref[...] for nm, ref in zip(names, outs)}, [ref[...] for ref in small_out]


def _exchange_halves(grads, names, label, collective_id):
    n = len(names)
    sequencer = collective_id is not None

    def body(*refs):
        ins, outs = refs[:n], refs[n:2 * n]
        send_sems, recv_sems = refs[2 * n:]
        x, y, c = _place()
        if sequencer:
            _handshake([(x, y, 1 - c)])
        copies = []
        for a, name in enumerate(names):
            copies.append(_remote(_half_of(ins[a], BIG[name][2], 1 - c), outs[a], send_sems, recv_sems, a,
                                  (x, y, 1 - c)))
        for cp in copies:
            cp.start()
        for cp in copies:
            cp.wait()

    def half_shape(name):
        r, c_ = _full_shape(name)
        return (HALF, c_) if BIG[name][2] else (r, HALF)

    out_type = tuple(jax.ShapeDtypeStruct(half_shape(nm), F32) for nm in names)
    sems = (pltpu.SemaphoreType.DMA((n,)), pltpu.SemaphoreType.DMA((n,)))
    operands = [grads[nm] for nm in names]
    if sequencer:
        got = pl.kernel(
            body, mesh=plsc.ScalarSubcoreMesh(axis_name="seq", num_cores=1), name=label, out_type=out_type,
            scratch_types=sems, compiler_params=pltpu.CompilerParams(collective_id=collective_id),
        )(*operands)
    else:
        got = pl.pallas_call(
            body, in_specs=[ANY] * n, out_specs=[ANY] * n, out_shape=list(out_type), scratch_shapes=list(sems),
            name=label,
        )(*operands)
    return dict(zip(names, got))


def _chip_sum(grads, got, names, core, label):
    n = len(names)
    steps = 4
    g_specs, blks = [], []
    for name in names:
        rows, cols = got[name].shape
        tr = rows // steps
        if BIG[name][2]:
            g_specs.append(pl.BlockSpec((tr, cols), lambda i, s: (s[0] * steps + i, 0)))
        else:
            g_specs.append(pl.BlockSpec((tr, HALF), lambda i, s: (i, s[0])))
        blks.append(pl.BlockSpec((tr, cols), lambda i, s: (i, 0)))

    def body(s_ref, *refs):
        for a in range(n):
            t = refs[a][...] + refs[n + a][...]
            refs[2 * n + a][...] = t
            refs[3 * n + a][...] = t.astype(BF16)

    out = pl.pallas_call(
        body,
        grid_spec=pltpu.PrefetchScalarGridSpec(num_scalar_prefetch=1, grid=(steps,), in_specs=g_specs + blks,
                                               out_specs=blks + blks),
        out_shape=([jax.ShapeDtypeStruct(got[nm].shape, F32) for nm in names]
                   + [jax.ShapeDtypeStruct(got[nm].shape, BF16) for nm in names]),
        name=label, compiler_params=_params("parallel"),
    )(core, *[grads[nm] for nm in names], *[got[nm] for nm in names])
    return {nm: (out[a], out[n + a]) for a, nm in enumerate(names)}


def _piece_shape(name):
    rows, cols, by_col = BIG[name]
    return (HALF, cols) if by_col else (rows, HALF)


def _handshake(peers):
    barrier = pltpu.get_barrier_semaphore()
    for peer in peers:
        pl.semaphore_signal(barrier, inc=1, device_id=peer, device_id_type=MESH)
    pl.semaphore_wait(barrier, len(peers))


def _send_chip_sums(sums, names, label, collective_id):
    n = len(names)

    def body(*refs):
        ins, outs = refs[:n], refs[n:2 * n]
        send_sems, recv_sems = refs[2 * n:]
        x, y, c = _place()
        others = [_chip_of(x, y, r) for r in range(3)]
        _handshake([(qx, qy, c) for qx, qy in others])
        copies = []
        for a, name in enumerate(names):
            for r, (qx, qy) in enumerate(others):
                copies.append(_remote(_shard_of(ins[a], name, 2 * qx + qy), outs[a].at[r], send_sems, recv_sems,
                                      3 * a + r, (qx, qy, c)))
        for cp in copies:
            cp.start()
        for cp in copies:
            cp.wait()

    return pl.kernel(
        body, mesh=plsc.ScalarSubcoreMesh(axis_name="seq", num_cores=1), name=label,
        out_type=tuple(jax.ShapeDtypeStruct((3,) + _piece_shape(nm), BF16) for nm in names),
        scratch_types=(pltpu.SemaphoreType.DMA((3 * n,)), pltpu.SemaphoreType.DMA((3 * n,))),
        compiler_params=pltpu.CompilerParams(collective_id=collective_id),
    )(*[sums[nm] for nm in names])


def _total(parts, chip_core):
    steps = 2
    in_specs, out_specs, operands = [], [], []
    for name in BIG_NAMES:
        by_col = BIG[name][2]
        pr, pc = _piece_shape(name)
        tr = pr // steps
        if by_col:
            in_specs.append(pl.BlockSpec((tr, pc), lambda i, s: (i, s[0])))
            out_specs.append(pl.BlockSpec((tr, pc), lambda i, s: (s[1] * steps + i, 0)))
        else:
            in_specs.append(pl.BlockSpec((tr, pc), lambda i, s: (s[0] * steps + i, 0)))
            out_specs.append(pl.BlockSpec((tr, pc), lambda i, s: (i, s[1])))
        for r in range(3):
            in_specs.append(pl.BlockSpec((None, tr, pc), lambda i, s, r=r: (r, i, 0)))
        own, got = parts[name]
        operands += [own, got, got, got]

    def body(s_ref, *refs):
        for a in range(N_BIG):
            o_ref, a_ref, b_ref, c_ref = refs[4 * a:4 * a + 4]
            refs[4 * N_BIG + a][...] = (((o_ref[...] + a_ref[...].astype(F32)) + b_ref[...].astype(F32))
                                        + c_ref[...].astype(F32))

    totals = pl.pallas_call(
        body,
        grid_spec=pltpu.PrefetchScalarGridSpec(num_scalar_prefetch=1, grid=(steps,), in_specs=in_specs,
                                               out_specs=out_specs),
        out_shape=[jax.ShapeDtypeStruct(BIG[name][:2], F32) for name in BIG_NAMES],
        name="totals", compiler_params=_params("parallel"),
    )(chip_core, *operands)
    return dict(zip(BIG_NAMES, totals))


def _share_totals(totals):
    def body(*refs):
        outs = refs[N_BIG:2 * N_BIG]
        send_sems, recv_sems = refs[2 * N_BIG:]
        x, y, c = _place()
        copies = []
        for a, name in enumerate(BIG_NAMES):
            mine = _half_of(outs[a], BIG[name][2], c)
            copies.append(_remote(mine, mine, send_sems, recv_sems, a, (x, y, 1 - c)))
        for cp in copies:
            cp.start()
        for a, name in enumerate(BIG_NAMES):
            theirs = _half_of(outs[a], BIG[name][2], 1 - c)
            _remote(theirs, theirs, send_sems, recv_sems, a, (x, y, 1 - c)).wait_recv()
        for cp in copies:
            cp.wait_send()

    return pl.pallas_call(
        body, in_specs=[ANY] * N_BIG, out_specs=[ANY] * N_BIG,
        out_shape=[jax.ShapeDtypeStruct(BIG[n][:2], F32) for n in BIG_NAMES],
        input_output_aliases={a: a for a in range(N_BIG)},
        scratch_shapes=[pltpu.SemaphoreType.DMA((N_BIG,)), pltpu.SemaphoreType.DMA((N_BIG,))],
        name="share_totals",
    )(*[totals[n] for n in BIG_NAMES])


VEC_ROWS = 32
VEC_ROW = {"mix_norm_g": 0, "conv_b": 1, "b_rgate": 2, "b_igate": 3, "lru_lambda": 4, "rg_norm_g": 5,
           "hg_lower_bound": 6, "hg_norm_g": 8, "ffn_norm_g": 9, "final_norm_g": 10, "loss": 11,
           "conv_w": 12, "meta_tokens": 16}
N_DEV = 8


def _all_reduce_small(pieces, gates):
    names = list(pieces)
    hv, hg = VEC_ROWS // 2, gates.shape[0] // 2

    def body(*refs):
        ins = refs[:len(names)]
        (g_ref, vec_ref, gsum_ref, mine_v, sib_v, sib_g, chip_v, chip_g, got_v, got_g,
         send_sems, recv_sems) = refs[len(names):]
        x, y, c = _place()
        chip = 2 * x + y
        sibling = (x, y, 1 - c)
        mine_v[...] = jnp.zeros_like(mine_v)
        for name, ref in zip(names, ins):
            nr, w = ref.shape
            mine_v[VEC_ROW[name]:VEC_ROW[name] + nr, 0:w] = ref[...]

        swap = [_remote(mine_v, sib_v, send_sems, recv_sems, 0, sibling),
                _remote(g_ref, sib_g, send_sems, recv_sems, 1, sibling)]
        for cp in swap:
            cp.start()
        for cp in swap:
            cp.wait()
        chip_v[...] = mine_v[...] + sib_v[...]
        chip_g[...] = g_ref[...] + sib_g[...]

        rows_v = pl.ds(pl.multiple_of(c * hv, 8), hv)
        rows_g = pl.ds(pl.multiple_of(c * hg, 8), hg)
        got_v[chip] = chip_v[rows_v, :]
        got_g[chip] = chip_g[rows_g, :]
        sends = []
        for r in range(3):
            qx, qy = _chip_of(x, y, r)
            sends.append(_remote(chip_v.at[rows_v, :], got_v.at[chip], send_sems, recv_sems, 2 + r, (qx, qy, c)))
            sends.append(_remote(chip_g.at[rows_g, :], got_g.at[chip], send_sems, recv_sems, 5 + r, (qx, qy, c)))
        for cp in sends:
            cp.start()
        for cp in sends:
            cp.wait()
        vec_ref[rows_v, :] = ((got_v[0] + got_v[1]) + got_v[2]) + got_v[3]
        gsum_ref[rows_g, :] = ((got_g[0] + got_g[1]) + got_g[2]) + got_g[3]

        back = [_remote(vec_ref.at[rows_v, :], vec_ref.at[rows_v, :], send_sems, recv_sems, 8, sibling),
                _remote(gsum_ref.at[rows_g, :], gsum_ref.at[rows_g, :], send_sems, recv_sems, 9, sibling)]
        for cp in back:
            cp.start()
        theirs_v = vec_ref.at[pl.ds(pl.multiple_of((1 - c) * hv, 8), hv), :]
        theirs_g = gsum_ref.at[pl.ds(pl.multiple_of((1 - c) * hg, 8), hg), :]
        _remote(theirs_v, theirs_v, send_sems, recv_sems, 8, sibling).wait_recv()
        _remote(theirs_g, theirs_g, send_sems, recv_sems, 9, sibling).wait_recv()
        for cp in back:
            cp.wait_send()

    vmem = pl.BlockSpec(memory_space=pltpu.VMEM)
    n_sems = 10
    return pl.pallas_call(
        body, in_specs=[vmem] * (len(names) + 1), out_specs=[vmem, vmem],
        out_shape=[jax.ShapeDtypeStruct((VEC_ROWS, D_MODEL), F32), jax.ShapeDtypeStruct(gates.shape, F32)],
        scratch_shapes=[pltpu.VMEM((VEC_ROWS, D_MODEL), F32), pltpu.VMEM((VEC_ROWS, D_MODEL), F32),
                        pltpu.VMEM(gates.shape, F32), pltpu.VMEM((VEC_ROWS, D_MODEL), F32),
                        pltpu.VMEM(gates.shape, F32), pltpu.VMEM((N_CHIPS, hv, D_MODEL), F32),
                        pltpu.VMEM((N_CHIPS, hg) + gates.shape[1:], F32),
                        pltpu.SemaphoreType.DMA((n_sems,)), pltpu.SemaphoreType.DMA((n_sems,))],
        name="all_reduce_small",
    )(*[pieces[n] for n in names], gates)


def _adamw_math(w, g, m, v):
    m = ADAM_B1 * m + (1.0 - ADAM_B1) * g
    v = ADAM_B2 * v + (1.0 - ADAM_B2) * (g * g)
    m_hat = m / (1.0 - ADAM_B1 ** ADAM_STEP)
    v_hat = v / (1.0 - ADAM_B2 ** ADAM_STEP)
    delta = -ADAM_LR * (m_hat / (jnp.sqrt(v_hat) + ADAM_EPS) + ADAM_WD * w)
    return delta, m, v


def _adamw_big(w, g, m, v):
    steps = 8
    blks = []
    for name in BIG_NAMES:
        rows, cols, _ = BIG[name]
        blks.append(pl.BlockSpec((rows // steps, cols), lambda i: (i, 0)))

    def body(*refs):
        ins, outs = refs[:4 * N_BIG], refs[4 * N_BIG:]
        for a in range(N_BIG):
            w_ref, g_ref, m_ref, v_ref = (ins[k * N_BIG + a] for k in range(4))
            d, nm, nv = _adamw_math(w_ref[...], g_ref[...], m_ref[...], v_ref[...])
            outs[a][...] = d
            outs[N_BIG + a][...] = nm
            outs[2 * N_BIG + a][...] = nv

    shapes = [jax.ShapeDtypeStruct(BIG[name][:2], F32) for name in BIG_NAMES]
    out = pl.pallas_call(
        body, grid=(steps,), in_specs=blks * 4, out_specs=blks * 3, out_shape=shapes * 3,
        name="adamw_big", compiler_params=_params("parallel"),
    )(*[t[name] for t in (w, g, m, v) for name in BIG_NAMES])
    return {name: (out[a], out[N_BIG + a], out[2 * N_BIG + a]) for a, name in enumerate(BIG_NAMES)}


SMALL = {"meta_tokens": (N_META, D_MODEL // N_CHIPS), "mix_norm_g": (1, D_MODEL), "conv_w": (CONV_W, D_RG // N_CHIPS),
         "conv_b": (1, D_RG), "w_rgate": (D_RG, RG_HEAD_DIM), "b_rgate": (1, D_RG), "w_igate": (D_RG, RG_HEAD_DIM),
         "b_igate": (1, D_RG), "lru_lambda": (1, D_RG), "rg_norm_g": (1, D_RG), "hg_lower_bound": (2, D_HG),
         "hg_norm_g": (1, HG_HEAD_DIM), "ffn_norm_g": (1, D_MODEL), "final_norm_g": (1, D_MODEL)}
SMALL_NAMES = tuple(SMALL)
SHARDED_SMALL = ("meta_tokens", "conv_w")


def _adamw_small(vec, gates, w, m, v):
    n = len(SMALL_NAMES)

    def body(*refs):
        vec_ref, gates_ref = refs[:2]
        w_refs, m_refs, v_refs = refs[2:2 + n], refs[2 + n:2 + 2 * n], refs[2 + 2 * n:2 + 3 * n]
        outs = refs[2 + 3 * n:]
        loss_ref = outs[0]
        x, y, _ = _place()
        chip = 2 * x + y
        loss_ref[...] = vec_ref[VEC_ROW["loss"]:VEC_ROW["loss"] + 1, 0:1]

        def update(k, g):
            g_ref, d_ref, nm_ref, nv_ref = outs[1 + 4 * k:5 + 4 * k]
            g_ref[...] = g
            d_ref[...], nm_ref[...], nv_ref[...] = _adamw_math(w_refs[k][...], g, m_refs[k][...], v_refs[k][...])

        for k, name in enumerate(SMALL_NAMES):
            nr, w_ = SMALL[name]
            if name == "w_rgate":
                update(k, gates_ref[0:D_RG, :])
            elif name == "w_igate":
                update(k, gates_ref[D_RG:2 * D_RG, :])
            elif name in SHARDED_SMALL:
                r0 = VEC_ROW[name]
                for q in range(N_CHIPS):
                    @pl.when(chip == q)
                    def _(k=k, r0=r0, nr=nr, w_=w_, q=q):
                        update(k, vec_ref[r0:r0 + nr, q * w_:(q + 1) * w_])
            else:
                r0 = VEC_ROW[name]
                update(k, vec_ref[r0:r0 + nr, 0:w_])

    vmem = pl.BlockSpec(memory_space=pltpu.VMEM)
    out_shape = [jax.ShapeDtypeStruct((1, 1), F32)]
    for name in SMALL_NAMES:
        out_shape += [jax.ShapeDtypeStruct(SMALL[name], F32)] * 4
    outs = pl.pallas_call(
        body, in_specs=[vmem] * (2 + 3 * n), out_specs=[vmem] * len(out_shape), out_shape=out_shape,
        name="adamw_small",
    )(vec, gates, *[w[k] for k in SMALL_NAMES], *[m[k] for k in SMALL_NAMES], *[v[k] for k in SMALL_NAMES])
    loss = outs[0]
    res = {name: tuple(outs[1 + 4 * k:5 + 4 * k]) for k, name in enumerate(SMALL_NAMES)}
    return loss, res


WEIGHT_NAMES = ("meta_tokens", "mix_norm_g", "w_in", "conv_w", "conv_b", "w_rgate", "b_rgate", "w_igate", "b_igate",
                "lru_lambda", "rg_norm_g", "hg_lower_bound", "hg_norm_g", "w_out", "ffn_norm_g", "w_gate_up", "w_down",
                "final_norm_g")


def _to_2d(name, a):
    if name in BIG:
        return a.reshape(BIG[name][:2])
    return a.reshape(SMALL[name])


def kernel(x, meta_tokens, mix_norm_g, w_in, conv_w, conv_b, w_rgate, b_rgate, w_igate, b_igate, lru_lambda, rg_norm_g, hg_lower_bound, hg_norm_g, w_out, ffn_norm_g, w_gate_up, w_down, final_norm_g, loss_target, m_meta_tokens, m_mix_norm_g, m_w_in, m_conv_w, m_conv_b, m_w_rgate, m_b_rgate, m_w_igate, m_b_igate, m_lru_lambda, m_rg_norm_g, m_hg_lower_bound, m_hg_norm_g, m_w_out, m_ffn_norm_g, m_w_gate_up, m_w_down, m_final_norm_g, v_meta_tokens, v_mix_norm_g, v_w_in, v_conv_w, v_conv_b, v_w_rgate, v_b_rgate, v_w_igate, v_b_igate, v_lru_lambda, v_rg_norm_g, v_hg_lower_bound, v_hg_norm_g, v_w_out, v_ffn_norm_g, v_w_gate_up, v_w_down, v_final_norm_g):
    w_raw = dict(zip(WEIGHT_NAMES, (meta_tokens, mix_norm_g, w_in, conv_w, conv_b, w_rgate, b_rgate, w_igate, b_igate,
                                    lru_lambda, rg_norm_g, hg_lower_bound, hg_norm_g, w_out, ffn_norm_g, w_gate_up,
                                    w_down, final_norm_g)))
    m_raw = dict(zip(WEIGHT_NAMES, (m_meta_tokens, m_mix_norm_g, m_w_in, m_conv_w, m_conv_b, m_w_rgate, m_b_rgate,
                                    m_w_igate, m_b_igate, m_lru_lambda, m_rg_norm_g, m_hg_lower_bound, m_hg_norm_g,
                                    m_w_out, m_ffn_norm_g, m_w_gate_up, m_w_down, m_final_norm_g)))
    v_raw = dict(zip(WEIGHT_NAMES, (v_meta_tokens, v_mix_norm_g, v_w_in, v_conv_w, v_conv_b, v_w_rgate, v_b_rgate,
                                    v_w_igate, v_b_igate, v_lru_lambda, v_rg_norm_g, v_hg_lower_bound, v_hg_norm_g,
                                    v_w_out, v_ffn_norm_g, v_w_gate_up, v_w_down, v_final_norm_g)))
    w = {k: _to_2d(k, a) for k, a in w_raw.items()}
    m = {k: _to_2d(k, a) for k, a in m_raw.items()}
    v = {k: _to_2d(k, a) for k, a in v_raw.items()}

    x_i, y_i, c_i = _place()
    core = jnp.reshape(c_i, (1,)).astype(jnp.int32)
    chip = jnp.reshape(2 * x_i + y_i, (1,)).astype(jnp.int32)
    chip_core = jnp.concatenate([chip, core])

    placed, (meta_full, cw_full) = _place_shards(w, [w["meta_tokens"], w["conv_w"]], chip)
    first, _ = _gather_weights(placed, [], ("w_in",), "gather_first", 1)
    rest, _ = _gather_weights(placed, [], ("w_out", "w_gate_up", "w_down"), "gather_rest", 2)
    full = {**first, **rest}

    seq = x.shape[1]
    small ={k: w[k] for k in SMALL_NAMES if k not in SHARDED_SMALL}
    small["conv_w"] = cw_full

    def reduce_to_chips(grads, names, tag, collective_ids):
        got = _exchange_halves(grads, names, "exchange_halves_" + tag, collective_ids[0])

        def chip_sums():
            return _chip_sum(grads, got, names, core, "chip_sum_" + tag)

        def send(sums):
            arrived = _send_chip_sums({n: sums[n][1] for n in names}, names, "send_chip_sums_" + tag,
                                      collective_ids[1])
            return {n: (sums[n][0], a) for n, a in zip(names, arrived)}

        return chip_sums, send

    ffn_names, mixer_names = ("w_gate_up", "w_down"), ("w_in", "w_out")
    loss, grad_x, grads, parts, parts_mixer = _local_step(
        x.reshape(seq, D_MODEL), meta_full, loss_target.reshape(seq, D_MODEL),
        w["w_in"], full["w_in"], full["w_out"], full["w_gate_up"], full["w_down"], small, chip,
        on_ffn_grads=lambda g: reduce_to_chips(g, ffn_names, "ffn", (3, 4)),
        on_mixer_grads=lambda g: reduce_to_chips(g, mixer_names, "mixer", (None, 5)))
    parts.update(parts_mixer)
    totals = _total(parts, chip_core)
    g_big = dict(zip(BIG_NAMES, _share_totals(totals)))

    pieces = {k: grads[k] for k in VEC_ROW if k != "loss"}
    pieces["loss"] = loss
    vec, gates = _all_reduce_small(pieces, grads["w_gates"])
    loss_sum, res = _adamw_small(vec, gates, w, m, v)
    updates = _adamw_big(w, g_big, m, v)
    for n in BIG_NAMES:
        res[n] = (g_big[n],) + updates[n]

    out = [loss_sum.reshape(()), grad_x.reshape(1, seq, D_MODEL)]
    for j in range(4):
        out += [res[n][j].reshape(w_raw[n].shape) for n in WEIGHT_NAMES]
    return tuple(out)
```

```python
import functools
import math

import jax
import jax.numpy as jnp
from jax import lax
from jax.experimental import pallas as pl
from jax.experimental.pallas import tpu as pltpu
from jax.experimental.pallas import tpu_sc as plsc

F32 = jnp.float32
BF16 = jnp.bfloat16
HIGHEST = lax.Precision.HIGHEST
MESH = pl.DeviceIdType.MESH

D_MODEL = 1024
D_RG = 512
RG_HEAD_DIM = 64
D_HG = 512
HG_HEAD_DIM = 128
HG_HEADS = 4
CHUNK = 64
SUB = 16
N_SUB = CHUNK // SUB
N_META = 16
PAD = CHUNK - N_META
D_IN = 3072
D_FF = 2816
CONV_W = 4
LRU_C = 8.0
EPS = 1e-6
EXP_CLAMP = 80.0
GELU_C = math.sqrt(2.0 / math.pi)
GELU_A = 0.044715
N_CHIPS = 4

ADAM_LR = 0.001
ADAM_B1 = 0.9
ADAM_B2 = 0.999
ADAM_EPS = 1e-08
ADAM_WD = 0.01
ADAM_STEP = 10

VMEM_LIMIT = 56 * 1024 * 1024


def _params(*sem):
    return pltpu.CompilerParams(dimension_semantics=sem, vmem_limit_bytes=VMEM_LIMIT)


def _row_tile(rows, target):
    best = None
    for t in range(16, min(rows, target) + 1, 16):
        if rows % t == 0:
            best = t
    assert best is not None, rows
    return best


def _sigmoid(x):
    return 0.5 * jnp.tanh(0.5 * x) + 0.5


def _dot(a, b):
    return jnp.dot(a, b, preferred_element_type=F32)


def _dot_nt(a, b):
    return lax.dot_general(a, b, (((1,), (1,)), ((), ())), preferred_element_type=F32)


def _dot_tn(a, b):
    return lax.dot_general(a, b, (((0,), (0,)), ((), ())), preferred_element_type=F32)


def _rms(x):
    return lax.rsqrt(jnp.mean(x * x, axis=-1, keepdims=True) + EPS)


def _rms_bwd(dn, n, r):
    return r * (dn - n * jnp.mean(dn * n, axis=-1, keepdims=True))


def _gelu_parts(x):
    t = jnp.tanh(GELU_C * (x + GELU_A * x * x * x))
    g = 0.5 * x * (1.0 + t)
    dg = 0.5 * (1.0 + t) + 0.5 * x * (1.0 - t * t) * GELU_C * (1.0 + 3.0 * GELU_A * x * x)
    return g, dg


def _softplus_neg(lam):
    e = jnp.exp(-jnp.abs(lam))
    w = 1.0 + e
    log1p = jnp.where(w == 1.0, e, jnp.log(w) * e / (w - 1.0))
    return jnp.maximum(-lam, 0.0) + log1p


def _head_mask():
    r = lax.broadcasted_iota(jnp.int32, (D_RG, D_RG), 0) // RG_HEAD_DIM
    c = lax.broadcasted_iota(jnp.int32, (D_RG, D_RG), 1) // RG_HEAD_DIM
    return r == c


def _head_fold():
    r = lax.broadcasted_iota(jnp.int32, (D_RG, RG_HEAD_DIM), 0) % RG_HEAD_DIM
    c = lax.broadcasted_iota(jnp.int32, (D_RG, RG_HEAD_DIM), 1)
    return (r == c).astype(F32)


def _gate_weights(w_r, w_i):
    def body(wr_ref, wi_ref, o_ref):
        fold = _head_fold()
        mask = _head_mask()
        for k, ref in enumerate((wr_ref, wi_ref)):
            full = lax.dot_general(ref[...], fold, (((1,), (1,)), ((), ())),
                                   precision=HIGHEST, preferred_element_type=F32)
            o_ref[:, k * D_RG:(k + 1) * D_RG] = jnp.where(mask, full, 0.0).astype(BF16)

    return pl.pallas_call(
        body, out_shape=jax.ShapeDtypeStruct((D_RG, 2 * D_RG), BF16), name="gate_weights",
    )(w_r, w_i)


HEAD = PAD + N_META


def _window_copies(seq_hbm, buf, sems, tm):
    def first(to_vmem):
        seq, vm = seq_hbm.at[pl.ds(0, tm - HEAD)], buf.at[0, pl.ds(HEAD, tm - HEAD)]
        return pltpu.make_async_copy(seq, vm, sems.at[0]) if to_vmem else pltpu.make_async_copy(vm, seq, sems.at[0])

    def later(j, slot, to_vmem):
        seq, vm = seq_hbm.at[pl.ds(pl.multiple_of(j * tm - HEAD, 8), tm)], buf.at[slot]
        if to_vmem:
            return pltpu.make_async_copy(seq, vm, sems.at[slot])
        return pltpu.make_async_copy(vm, seq, sems.at[slot])

    return first, later


def _fetch_window(seq_hbm, buf, sems, i, n_steps, tm):
    first, later = _window_copies(seq_hbm, buf, sems, tm)
    slot = i % 2

    @pl.when(i == 0)
    def _():
        first(True).start()

    if n_steps > 1:
        @pl.when(i + 1 < n_steps)
        def _():
            later(i + 1, 1 - slot, True).start()

    @pl.when(i == 0)
    def _():
        first(True).wait()

    if n_steps > 1:
        @pl.when(i > 0)
        def _():
            later(i, slot, True).wait()

    return slot


def _in_proj_local(x, meta, g1, w_own, chip):
    T = x.shape[0] + HEAD
    tm = _row_tile(T, 832)
    n_steps = T // tm
    cols = BIG["w_in"][1]

    def body(s_ref, x_hbm, meta_ref, g_ref, w_ref, p_ref, u_ref, h_ref, buf, sems, wb):
        i = pl.program_id(0)
        slot = _fetch_window(x_hbm, buf, sems, i, n_steps, tm)

        @pl.when(i == 0)
        def _():
            buf[0, 0:PAD, :] = jnp.zeros((PAD, D_MODEL), F32)
            buf[0, PAD:HEAD, :] = meta_ref[...]
            wb[...] = w_ref[...].astype(BF16)

        h = buf[slot]
        h_ref[...] = h
        u = (h * _rms(h) * g_ref[...]).astype(BF16)
        u_ref[...] = u
        p_ref[...] = _dot(u, wb[...])

    return pl.pallas_call(
        body,
        grid_spec=pltpu.PrefetchScalarGridSpec(
            num_scalar_prefetch=1, grid=(n_steps,),
            in_specs=[pl.BlockSpec(memory_space=pl.ANY),
                      pl.BlockSpec((N_META, D_MODEL), lambda i, s: (0, 0)),
                      pl.BlockSpec((1, D_MODEL), lambda i, s: (0, 0)),
                      pl.BlockSpec((D_MODEL, cols), lambda i, s: (0, 0))],
            out_specs=[pl.BlockSpec((tm, cols), lambda i, s: (i, s[0])),
                       pl.BlockSpec((tm, D_MODEL), lambda i, s: (i, 0)),
                       pl.BlockSpec((tm, D_MODEL), lambda i, s: (i, 0))],
            scratch_shapes=[pltpu.VMEM((2, tm, D_MODEL), F32), pltpu.SemaphoreType.DMA((2,)),
                            pltpu.VMEM((D_MODEL, cols), BF16)]),
        out_shape=[jax.ShapeDtypeStruct((T, D_IN), F32), jax.ShapeDtypeStruct((T, D_MODEL), BF16),
                   jax.ShapeDtypeStruct((T, D_MODEL), F32)],
        name="in_proj_local", compiler_params=_params("arbitrary"),
    )(chip, x, meta, g1, w_own)


def _in_proj_rest(u, w_in, p, chip):
    T = u.shape[0]
    tm = _row_tile(T, 2080)
    cols = BIG["w_in"][1]
    block = lambda j, s: (s[0] + 1 + j) % N_CHIPS

    def body(s_ref, u_ref, w_ref, p_in_ref, p_ref):
        p_ref[...] = _dot(u_ref[...], w_ref[...])

    return pl.pallas_call(
        body,
        grid_spec=pltpu.PrefetchScalarGridSpec(
            num_scalar_prefetch=1, grid=(N_CHIPS - 1, T // tm),
            in_specs=[pl.BlockSpec((tm, D_MODEL), lambda j, i, s: (i, 0)),
                      pl.BlockSpec((D_MODEL, cols), lambda j, i, s: (0, block(j, s))), ANY],
            out_specs=pl.BlockSpec((tm, cols), lambda j, i, s: (i, block(j, s)))),
        out_shape=jax.ShapeDtypeStruct((T, D_IN), F32),
        input_output_aliases={3: 0},
        name="in_proj_rest", compiler_params=_params("arbitrary", "arbitrary"),
    )(chip, u, w_in, p)


def _scan_block_fwd(A, B, rowi):
    for d in (1, 2, 4):
        a_sh = pltpu.roll(A, d, axis=0)
        b_sh = pltpu.roll(B, d, axis=0)
        m = rowi >= d
        B = jnp.where(m, A * b_sh + B, B)
        A = jnp.where(m, A * a_sh, A)
    return A, B


def _scan_block_bwd(A, B, rowi):
    for d in (1, 2, 4):
        a_sh = pltpu.roll(A, 8 - d, axis=0)
        b_sh = pltpu.roll(B, 8 - d, axis=0)
        m = rowi < 8 - d
        B = jnp.where(m, A * b_sh + B, B)
        A = jnp.where(m, A * a_sh, A)
    return A, B


def _rg_gates(xc, w_ref, bg_ref, lam):
    pre = _dot(xc.astype(BF16), w_ref[...]) + bg_ref[...]
    r = _sigmoid(pre[:, :D_RG])
    ig = _sigmoid(pre[:, D_RG:])
    sp = _softplus_neg(lam)
    la = -LRU_C * sp * r
    a = jnp.exp(la)
    th = jnp.tanh(la)
    u = 1.0 - th
    rc = pl.reciprocal(u, approx=True)
    rc = rc * (2.0 - u * rc)
    rc = rc * (2.0 - u * rc)
    m2 = -2.0 * th * rc
    inv_m = lax.rsqrt(jnp.maximum(m2, 1e-30))
    return r, ig, sp, a, m2 * inv_m, inv_m


def _conv(ext, cw_ref, cb_ref, tm):
    xc = cb_ref[...] + cw_ref[0:1, :] * ext[8 - 3:8 - 3 + tm, :]
    for j in range(1, CONV_W):
        xc = xc + cw_ref[j:j + 1, :] * ext[8 - 3 + j:8 - 3 + j + tm, :]
    return xc


def _scan_unroll(blocks):
    return 4 if blocks % 4 == 0 else 2 if blocks % 2 == 0 else 1


def _rg_fwd(p, cw, cb, wg, bg, lam, rg_g):
    T = p.shape[0]
    tm = _row_tile(T, 832)
    unroll = _scan_unroll(tm // 8)

    def body(xg_ref, cw_ref, cb_ref, w_ref, bg_ref, lam_ref, g_ref, y_ref, h_ref, ext, a_s, b_s, carry):
        i = pl.program_id(0)

        @pl.when(i == 0)
        def _():
            ext[0:8, :] = jnp.zeros((8, D_RG), F32)
            carry[...] = jnp.zeros((1, D_RG), F32)

        ext[8:8 + tm, :] = xg_ref[:, :D_RG]
        xc = _conv(ext, cw_ref, cb_ref, tm)
        r, ig, sp, a, m, _ = _rg_gates(xc, w_ref, bg_ref, lam_ref[...])
        row = i * tm + lax.broadcasted_iota(jnp.int32, (tm, 1), 0)
        a_s[...] = a
        b_s[...] = jnp.where(row >= PAD, m * ig * xc, 0.0)
        rowi = lax.broadcasted_iota(jnp.int32, (8, D_RG), 0)

        def blk(j, c):
            for u in range(unroll):
                o = pl.multiple_of((j * unroll + u) * 8, 8)
                A, B = _scan_block_fwd(a_s[pl.ds(o, 8), :], b_s[pl.ds(o, 8), :], rowi)
                h = B + A * c
                h_ref[pl.ds(o, 8), :] = h
                c = h[7:8, :]
            return c

        carry[...] = lax.fori_loop(0, tm // (8 * unroll), blk, carry[...])
        ext[0:8, :] = ext[tm:tm + 8, :]
        g, _ = _gelu_parts(xg_ref[:, D_RG:])
        yy = g * h_ref[...]
        y_ref[...] = (yy * _rms(yy) * g_ref[...]).astype(BF16)

    vec = lambda n: pl.BlockSpec((1, n), lambda i: (0, 0))
    return pl.pallas_call(
        body, grid=(T // tm,),
        in_specs=[pl.BlockSpec((tm, 2 * D_RG), lambda i: (i, 0)),
                  pl.BlockSpec((CONV_W, D_RG), lambda i: (0, 0)), vec(D_RG),
                  pl.BlockSpec((D_RG, 2 * D_RG), lambda i: (0, 0)), vec(2 * D_RG), vec(D_RG), vec(D_RG)],
        out_specs=[pl.BlockSpec((tm, D_RG), lambda i: (i, 0)), pl.BlockSpec((tm, D_RG), lambda i: (i, 0))],
        out_shape=[jax.ShapeDtypeStruct((T, D_RG), BF16), jax.ShapeDtypeStruct((T, D_RG), F32)],
        scratch_shapes=[pltpu.VMEM((tm + 8, D_RG), F32), pltpu.VMEM((tm, D_RG), F32),
                        pltpu.VMEM((tm, D_RG), F32), pltpu.VMEM((1, D_RG), F32)],
        name="rg_fwd", compiler_params=_params("arbitrary"),
    )(p, cw, cb, wg, bg, lam, rg_g)


def _tri(lower):
    r = lax.broadcasted_iota(jnp.int32, (CHUNK, CHUNK), 0)
    c = lax.broadcasted_iota(jnp.int32, (CHUNK, CHUNK), 1)
    return ((c <= r) if lower else (c >= r)).astype(F32)


def _hg_gates(hq, hf, lbraw_ref, valid):
    lb = _sigmoid(lbraw_ref[0:1, :] - lbraw_ref[1:2, :])
    sq = _sigmoid(hq)
    q = hq * sq
    sf = _sigmoid(hf)
    f = lb + (1.0 - lb) * sf
    lf = jnp.where(valid, jnp.log(f), 0.0)
    b = jnp.dot(_tri(True), lf, precision=HIGHEST, preferred_element_type=F32)
    return lb, sq, q, sf, f, b


def _hg_head(qh, kh, bh):
    blk = lax.broadcasted_iota(jnp.int32, (CHUNK, 1), 0) // SUB
    b_last = bh[CHUNK - 1:CHUNK, :]
    refs = [bh[SUB * s:SUB * s + 1, :] for s in range(N_SUB)]
    r_sel = refs[N_SUB - 1]
    for s in range(N_SUB - 2, -1, -1):
        r_sel = jnp.where(blk == s, refs[s], r_sel)
    eb = jnp.exp(bh)
    eq = jnp.exp(bh - r_sel)
    ekh = jnp.exp(b_last - bh)
    ek = [jnp.exp(jnp.minimum(refs[s] - bh, EXP_CLAMP)) for s in range(N_SUB)]
    qe = qh * eq
    q_hat = jnp.concatenate([jnp.where(blk == s, qe, 0.0) for s in range(N_SUB)], axis=1)
    k_til = jnp.concatenate([kh * ek[s] for s in range(N_SUB)], axis=1)
    return blk, b_last, eb, eq, ekh, ek, q_hat, k_til


def _causal():
    r = lax.broadcasted_iota(jnp.int32, (CHUNK, CHUNK), 0)
    c = lax.broadcasted_iota(jnp.int32, (CHUNK, CHUNK), 1)
    return r >= c


def _chunks_per_step(n_chunks):
    for c in (5, 4, 3, 2):
        if n_chunks % c == 0:
            return c
    return 1


def _hg_fwd(p, lbraw, hg_g):
    T = p.shape[0]
    n_chunks = T // CHUNK
    cps = _chunks_per_step(n_chunks)
    rows = cps * CHUNK

    def body(hq_ref, hf_ref, hi_ref, hg_ref, lb_ref, g_ref, y_ref, o_ref, st_all_ref, st):
        i = pl.program_id(0)

        @pl.when(i == 0)
        def _():
            st[...] = jnp.zeros_like(st)

        def chunk(j, carry):
            rs = pl.ds(pl.multiple_of(j * CHUNK, CHUNK), CHUNK)
            chunk_body(i * cps + j, hq_ref.at[rs, :], hf_ref.at[rs, :], hi_ref.at[rs, :], hg_ref.at[rs, :], lb_ref,
                       g_ref, y_ref.at[rs, :], o_ref.at[rs, :], st_all_ref.at[pl.ds(j, 1)], st)
            return carry

        lax.fori_loop(0, cps, chunk, 0, unroll=True)

    def chunk_body(n, hq_ref, hf_ref, hi_ref, hg_ref, lb_ref, g_ref, y_ref, o_ref, st_all_ref, st):
        valid = (n * CHUNK + lax.broadcasted_iota(jnp.int32, (CHUNK, 1), 0)) >= PAD
        hq, hf, v, hg = hq_ref[...], hf_ref[...], hi_ref[...], hg_ref[...]
        lb, sq, q, sf, f, b = _hg_gates(hq, hf, lb_ref, valid)
        k = 1.0 - f
        st_all_ref[0] = st[...]
        causal = _causal()
        v_t = v.T.astype(BF16)
        heads = [slice(h * HG_HEAD_DIM, (h + 1) * HG_HEAD_DIM) for h in range(HG_HEADS)]
        fac = []
        for sl in heads:
            qh, kh, bh = q[:, sl], k[:, sl], b[:, sl]
            _, b_last, eb, _, ekh, _, q_hat, k_til = _hg_head(qh, kh, bh)
            fac.append((jnp.exp(b_last), (qh * eb).astype(BF16), q_hat.astype(BF16), k_til.astype(BF16),
                        (kh * ekh).astype(BF16), v[:, sl].astype(BF16)))
        raw = []
        for sl, (_, q_til, q_hat, k_til, k_hat, _) in zip(heads, fac):
            st_h = st[sl, :]
            raw.append((_dot_nt(q_til, st_h.astype(BF16)), _dot_nt(q_hat, k_til), _dot(v_t[sl, :], k_hat), st_h))
        for sl, (e_last, _, _, _, _, vb), (inter, att, upd, st_h) in zip(heads, fac, raw):
            o = inter + _dot(jnp.where(causal, att, 0.0).astype(BF16), vb)
            st[sl, :] = st_h * e_last + upd
            o_ref[:, sl] = o
            hgh = hg[:, sl]
            y_ref[:, sl] = (o * _rms(o) * g_ref[...] * (hgh * _sigmoid(hgh))).astype(BF16)

    col = lambda j: pl.BlockSpec((rows, D_HG), lambda n: (n, j))
    return pl.pallas_call(
        body, grid=(n_chunks // cps,),
        in_specs=[col(2), col(3), col(4), col(5),
                  pl.BlockSpec((2, D_HG), lambda n: (0, 0)), pl.BlockSpec((1, HG_HEAD_DIM), lambda n: (0, 0))],
        out_specs=[pl.BlockSpec((rows, D_HG), lambda n: (n, 0)), pl.BlockSpec((rows, D_HG), lambda n: (n, 0)),
                   pl.BlockSpec((cps, D_HG, HG_HEAD_DIM), lambda n: (n, 0, 0))],
        out_shape=[jax.ShapeDtypeStruct((T, D_HG), BF16), jax.ShapeDtypeStruct((T, D_HG), F32),
                   jax.ShapeDtypeStruct((n_chunks, D_HG, HG_HEAD_DIM), F32)],
        scratch_shapes=[pltpu.VMEM((D_HG, HG_HEAD_DIM), F32)],
        name="hg_fwd", compiler_params=_params("arbitrary"),
    )(p, p, p, p, lbraw, hg_g)


def _ffn_fwd(h0, y_rg, y_hg, w_out, g2, w_gu, w_down, gf, target):
    T = h0.shape[0]
    tm = _row_tile(T, 320)
    n_steps = T // tm

    def body(h_ref, yr_ref, yh_ref, wo_ref, g2_ref, wgu_ref, wd_ref, gf_ref, t_hbm,
             h1_ref, v_ref, y_ref, gu_ref, act_ref, dh2_ref, dh2b_ref, loss_ref, gg_ref, tbuf, sems):
        i = pl.program_id(0)
        slot = _fetch_window(t_hbm, tbuf, sems, i, n_steps, tm)

        @pl.when(i == 0)
        def _():
            loss_ref[...] = jnp.zeros_like(loss_ref)
            gg_ref[...] = jnp.zeros_like(gg_ref)
            tbuf[0, 0:HEAD, :] = jnp.zeros((HEAD, D_MODEL), F32)

        y_ref[:, :D_RG] = yr_ref[...]
        y_ref[:, D_RG:] = yh_ref[...]
        h1 = h_ref[...] + _dot(y_ref[...], wo_ref[...])
        h1_ref[...] = h1
        v = (h1 * _rms(h1) * g2_ref[...]).astype(BF16)
        v_ref[...] = v

        gu = _dot(v, wgu_ref[...])
        gu_ref[...] = gu.astype(BF16)
        g = gu[:, :D_FF]
        act = (g * _sigmoid(g) * gu[:, D_FF:]).astype(BF16)
        act_ref[...] = act

        h2 = h1 + _dot(act, wd_ref[...])
        r = _rms(h2)
        n = h2 * r
        gf_ = gf_ref[...]
        row = i * tm + lax.broadcasted_iota(jnp.int32, (tm, 1), 0)
        err = jnp.where(row >= HEAD, n * gf_ - tbuf[slot], 0.0)
        loss_ref[...] += 0.5 * jnp.sum(jnp.mean(err * err, axis=-1, keepdims=True), axis=0, keepdims=True)
        dy = err * (1.0 / D_MODEL)
        gg_ref[...] += jnp.sum(dy * n, axis=0, keepdims=True)
        dh2 = _rms_bwd(dy * gf_, n, r)
        dh2_ref[...] = dh2
        dh2b_ref[...] = dh2.astype(BF16)

    row_spec = lambda n: pl.BlockSpec((tm, n), lambda i: (i, 0))
    vec = pl.BlockSpec((1, D_MODEL), lambda i: (0, 0))
    return pl.pallas_call(
        body, grid=(n_steps,),
        in_specs=[row_spec(D_MODEL), row_spec(D_RG), row_spec(D_HG), _resident((D_MODEL, D_MODEL)), vec,
                  _resident((D_MODEL, 2 * D_FF)), _resident((D_FF, D_MODEL)), vec,
                  pl.BlockSpec(memory_space=pl.ANY)],
        out_specs=[row_spec(D_MODEL), row_spec(D_MODEL), row_spec(D_MODEL), row_spec(2 * D_FF), row_spec(D_FF),
                   row_spec(D_MODEL), row_spec(D_MODEL), pl.BlockSpec((1, 1), lambda i: (0, 0)), vec],
        out_shape=[jax.ShapeDtypeStruct((T, D_MODEL), F32), jax.ShapeDtypeStruct((T, D_MODEL), BF16),
                   jax.ShapeDtypeStruct((T, D_MODEL), BF16), jax.ShapeDtypeStruct((T, 2 * D_FF), BF16),
                   jax.ShapeDtypeStruct((T, D_FF), BF16), jax.ShapeDtypeStruct((T, D_MODEL), F32),
                   jax.ShapeDtypeStruct((T, D_MODEL), BF16), jax.ShapeDtypeStruct((1, 1), F32),
                   jax.ShapeDtypeStruct((1, D_MODEL), F32)],
        scratch_shapes=[pltpu.VMEM((2, tm, D_MODEL), F32), pltpu.SemaphoreType.DMA((2,))],
        name="ffn_fwd", compiler_params=_params("arbitrary"),
    )(h0, y_rg, y_hg, w_out, g2, w_gu, w_down, gf, target)


def _resident(shape):
    return pl.BlockSpec(shape, lambda i: (0,) * len(shape), pipeline_mode=pl.Buffered(1))


def _ffn_bwd(dh2b, gu, w_down, w_gu, h1, g2, dh2, w_out):
    T = h1.shape[0]
    tm = _row_tile(T, 320)

    def body(d_ref, gu_ref, wd_ref, wgu_ref, h_ref, g_ref, d2_ref, wo_ref, dgu_ref, dh1_ref, dh1b_ref, dy_ref, gg_ref):
        i = pl.program_id(0)

        @pl.when(i == 0)
        def _():
            gg_ref[...] = jnp.zeros_like(gg_ref)

        dact = _dot_nt(d_ref[...], wd_ref[...]).astype(BF16)
        g = gu_ref[:, :D_FF]
        u = gu_ref[:, D_FF:]
        s = _sigmoid(g)
        dgu_ref[:, :D_FF] = dact * u * (s * (1.0 + g * (1.0 - s)))
        dgu_ref[:, D_FF:] = dact * (g * s)

        dv = _dot_nt(dgu_ref[...], wgu_ref[...])
        h1_ = h_ref[...]
        r = _rms(h1_)
        n = h1_ * r
        gg_ref[...] += jnp.sum(dv * n, axis=0, keepdims=True)
        dh1 = d2_ref[...] + _rms_bwd(dv * g_ref[...], n, r)
        dh1_ref[...] = dh1
        db = dh1.astype(BF16)
        dh1b_ref[...] = db
        dy_ref[...] = _dot_nt(db, wo_ref[...])

    row = lambda n: pl.BlockSpec((tm, n), lambda i: (i, 0))
    return pl.pallas_call(
        body, grid=(T // tm,),
        in_specs=[row(D_MODEL), row(2 * D_FF), _resident((D_FF, D_MODEL)), _resident((D_MODEL, 2 * D_FF)),
                  row(D_MODEL), pl.BlockSpec((1, D_MODEL), lambda i: (0, 0)), row(D_MODEL),
                  _resident((D_MODEL, D_MODEL))],
        out_specs=[row(2 * D_FF), row(D_MODEL), row(D_MODEL), row(D_MODEL),
                   pl.BlockSpec((1, D_MODEL), lambda i: (0, 0))],
        out_shape=[jax.ShapeDtypeStruct((T, 2 * D_FF), BF16), jax.ShapeDtypeStruct((T, D_MODEL), F32),
                   jax.ShapeDtypeStruct((T, D_MODEL), BF16), jax.ShapeDtypeStruct((T, D_MODEL), F32),
                   jax.ShapeDtypeStruct((1, D_MODEL), F32)],
        name="ffn_bwd", compiler_params=_params("arbitrary"),
    )(dh2b, gu, w_down, w_gu, h1, g2, dh2, w_out)


def _rg_bwd(p, hs, dy, dp, cw, cb, wg, bg, lam, rg_g):
    T = p.shape[0]
    tm = _row_tile(T, 832)
    nt = T // tm
    hb = tm // 8
    unroll = _scan_unroll(hb)

    def body(xg_ref, xh_ref, h_ref, hh_ref, dy_ref, dp_in_ref, cw_ref, cb_ref, w_ref, bg_ref, lam_ref, g_ref,
             dp_ref, gcw_ref, gcb_ref, gw_ref, gbg_ref, glam_ref, gg_ref,
             ext, dext, a_s, b_s, d_s, gacc, carry_d, carry_a):
        i = pl.program_id(0)
        t_idx = nt - 1 - i

        @pl.when(i == 0)
        def _():
            dext[tm:tm + 8, :] = jnp.zeros((8, D_RG), F32)
            carry_d[...] = jnp.zeros_like(carry_d)
            carry_a[...] = jnp.zeros_like(carry_a)
            gacc[...] = jnp.zeros_like(gacc)
            for ref in (gcw_ref, gcb_ref, gbg_ref, glam_ref, gg_ref, gw_ref):
                ref[...] = jnp.zeros_like(ref)

        first = t_idx == 0
        ext[0:8, :] = jnp.where(first, 0.0, xh_ref[:, :D_RG])
        ext[8:8 + tm, :] = xg_ref[:, :D_RG]
        xc = _conv(ext, cw_ref, cb_ref, tm)
        lam_ = lam_ref[...]
        r, ig, sp, a, m, inv_m = _rg_gates(xc, w_ref, bg_ref, lam_)
        row = t_idx * tm + lax.broadcasted_iota(jnp.int32, (tm, 1), 0)
        valid = row >= PAD

        gr = xg_ref[:, D_RG:]
        g, dgelu = _gelu_parts(gr)
        h = h_ref[...]
        yy = g * h
        rr = _rms(yy)
        nn = yy * rr
        dy_ = dy_ref[...]
        gg_ref[...] += jnp.sum(dy_ * nn, axis=0, keepdims=True)
        dyy = _rms_bwd(dy_ * g_ref[...], nn, rr)
        dp_ref[:, D_RG:] = (dyy * h * dgelu).astype(BF16)

        a_s[...] = a
        b_s[...] = dyy * g
        rowi = lax.broadcasted_iota(jnp.int32, (8, D_RG), 0)

        def blk(jj, c):
            cd, ca = c
            for u in range(unroll):
                o = pl.multiple_of((hb - 1 - (jj * unroll + u)) * 8, 8)
                a_blk = a_s[pl.ds(o, 8), :]
                a_next = jnp.where(rowi == 7, ca, pltpu.roll(a_blk, 7, axis=0))
                A, B = _scan_block_bwd(a_next, b_s[pl.ds(o, 8), :], rowi)
                d = B + A * cd
                d_s[pl.ds(o, 8), :] = d
                cd, ca = d[0:1, :], a_blk[0:1, :]
            return cd, ca

        cd, ca = lax.fori_loop(0, hb // unroll, blk, (carry_d[...], carry_a[...]))
        carry_d[...] = cd
        carry_a[...] = ca
        delta = d_s[...]

        h_last_prev = jnp.where(first, 0.0, hh_ref[7:8, :])
        row0 = lax.broadcasted_iota(jnp.int32, (tm, 1), 0) == 0
        h_prev = jnp.where(row0, h_last_prev, pltpu.roll(h, 1, axis=0))
        dbx = jnp.where(valid, delta, 0.0)
        da = delta * h_prev
        di = dbx * m * xc
        dm = dbx * ig * xc
        dla = a * (da - dm * a * inv_m)
        dla = jnp.where(valid, dla, 0.0)
        glam_ref[...] += jnp.sum(dla * r, axis=0, keepdims=True) * (LRU_C / (1.0 + jnp.exp(lam_)))
        dr = (-LRU_C) * sp * dla
        dpre = jnp.concatenate([dr * r * (1.0 - r), di * ig * (1.0 - ig)], axis=1)
        gbg_ref[...] += jnp.sum(dpre, axis=0, keepdims=True)
        dpre_b = dpre.astype(BF16)
        gacc[...] += _dot_tn(xc.astype(BF16), dpre_b)
        dxc = dbx * m * ig + _dot_nt(dpre_b, w_ref[...])
        gcb_ref[...] += jnp.sum(dxc, axis=0, keepdims=True)
        for j in range(CONV_W):
            gcw_ref[j:j + 1, :] += jnp.sum(dxc * ext[8 - 3 + j:8 - 3 + j + tm, :], axis=0, keepdims=True)
        dext[0:tm, :] = dxc
        dxr = cw_ref[0:1, :] * dext[3:3 + tm, :]
        for j in range(1, CONV_W):
            dxr = dxr + cw_ref[j:j + 1, :] * dext[3 - j:3 - j + tm, :]
        dp_ref[:, :D_RG] = dxr.astype(BF16)
        dext[tm:tm + 8, :] = dext[0:8, :]

        @pl.when(i == nt - 1)
        def _():
            fold = _head_fold()
            mask = _head_mask()
            for k in range(2):
                blockdiag = jnp.where(mask, gacc[:, k * D_RG:(k + 1) * D_RG], 0.0)
                gw_ref[k * D_RG:(k + 1) * D_RG, :] = jnp.dot(blockdiag, fold, precision=HIGHEST,
                                                             preferred_element_type=F32)

    vec = lambda n: pl.BlockSpec((1, n), lambda i: (0, 0))
    rev = lambda n: pl.BlockSpec((tm, n), lambda i: (nt - 1 - i, 0))
    halo = lambda n: pl.BlockSpec((8, n), lambda i: (jnp.maximum((nt - 1 - i) * hb - 1, 0), 0))
    return pl.pallas_call(
        body, grid=(nt,),
        in_specs=[rev(2 * D_RG), halo(2 * D_RG), rev(D_RG), halo(D_RG), rev(D_RG), ANY,
                  pl.BlockSpec((CONV_W, D_RG), lambda i: (0, 0)), vec(D_RG),
                  pl.BlockSpec((D_RG, 2 * D_RG), lambda i: (0, 0)), vec(2 * D_RG), vec(D_RG), vec(D_RG)],
        out_specs=[rev(2 * D_RG), pl.BlockSpec((CONV_W, D_RG), lambda i: (0, 0)), vec(D_RG),
                   pl.BlockSpec((2 * D_RG, RG_HEAD_DIM), lambda i: (0, 0)), vec(2 * D_RG), vec(D_RG), vec(D_RG)],
        input_output_aliases={5: 0},
        out_shape=[jax.ShapeDtypeStruct((T, D_IN), BF16), jax.ShapeDtypeStruct((CONV_W, D_RG), F32),
                   jax.ShapeDtypeStruct((1, D_RG), F32), jax.ShapeDtypeStruct((2 * D_RG, RG_HEAD_DIM), F32),
                   jax.ShapeDtypeStruct((1, 2 * D_RG), F32), jax.ShapeDtypeStruct((1, D_RG), F32),
                   jax.ShapeDtypeStruct((1, D_RG), F32)],
        scratch_shapes=[pltpu.VMEM((tm + 8, D_RG), F32), pltpu.VMEM((tm + 8, D_RG), F32),
                        pltpu.VMEM((tm, D_RG), F32), pltpu.VMEM((tm, D_RG), F32), pltpu.VMEM((tm, D_RG), F32),
                        pltpu.VMEM((D_RG, 2 * D_RG), F32), pltpu.VMEM((1, D_RG), F32), pltpu.VMEM((1, D_RG), F32)],
        name="rg_bwd", compiler_params=_params("arbitrary"),
    )(p, p, hs, hs, dy, dp, cw, cb, wg, bg, lam, rg_g)


def _hg_bwd(p, o_all, st_all, dy, lbraw, hg_g):
    T = p.shape[0]
    n_chunks = T // CHUNK
    cps = _chunks_per_step(n_chunks)
    rows = cps * CHUNK
    n_steps = n_chunks // cps

    def body(hq_ref, hf_ref, hi_ref, hg_ref, o_ref, st_ref, dy_ref, lb_ref, g_ref,
             dp_ref, glb_ref, gg_ref, dst):
        i = pl.program_id(0)

        @pl.when(i == 0)
        def _():
            dst[...] = jnp.zeros_like(dst)
            glb_ref[...] = jnp.zeros_like(glb_ref)
            gg_ref[...] = jnp.zeros_like(gg_ref)

        dp_ref[:, :2 * D_RG] = jnp.zeros((rows, 2 * D_RG), BF16)

        def chunk(jj, carry):
            j = cps - 1 - jj
            rs = pl.ds(pl.multiple_of(j * CHUNK, CHUNK), CHUNK)
            chunk_body((n_steps - 1 - i) * cps + j, hq_ref.at[rs, :], hf_ref.at[rs, :], hi_ref.at[rs, :],
                       hg_ref.at[rs, :], o_ref.at[rs, :], st_ref.at[pl.ds(j, 1)], dy_ref.at[rs, :], lb_ref, g_ref,
                       dp_ref.at[rs, pl.ds(2 * D_RG, 4 * D_HG)], glb_ref, gg_ref, dst)
            return carry

        lax.fori_loop(0, cps, chunk, 0, unroll=True)

    def chunk_body(n, hq_ref, hf_ref, hi_ref, hg_ref, o_ref, st_ref, dy_ref, lb_ref, g_ref,
                   dp_ref, glb_ref, gg_ref, dst):
        valid = (n * CHUNK + lax.broadcasted_iota(jnp.int32, (CHUNK, 1), 0)) >= PAD
        hq, hf, v, hg = hq_ref[...], hf_ref[...], hi_ref[...], hg_ref[...]
        lb, sq, q, sf, f, b = _hg_gates(hq, hf, lb_ref, valid)
        k = 1.0 - f
        causal = _causal()
        r_i = lax.broadcasted_iota(jnp.int32, (CHUNK, CHUNK), 0)
        c_i = lax.broadcasted_iota(jnp.int32, (CHUNK, CHUNK), 1)
        causal_t = r_i <= c_i
        is_last = lax.broadcasted_iota(jnp.int32, (CHUNK, 1), 0) == CHUNK - 1
        g_ = g_ref[...]
        db_parts, dq_parts, dk_parts = [], [], []
        gg = jnp.zeros((1, HG_HEAD_DIM), F32)
        heads = [slice(h * HG_HEAD_DIM, (h + 1) * HG_HEAD_DIM) for h in range(HG_HEADS)]

        do_parts = []
        for h, sl in enumerate(heads):
            o = o_ref[:, sl]
            ro = _rms(o)
            no = o * ro
            hgh = hg[:, sl]
            sg = _sigmoid(hgh)
            dyh = dy_ref[:, sl]
            dp_ref[:, 3 * D_HG + h * HG_HEAD_DIM:3 * D_HG + (h + 1) * HG_HEAD_DIM] = (
                dyh * no * g_ * sg * (1.0 + hgh * (1.0 - sg))).astype(BF16)
            dng = dyh * hgh * sg
            gg = gg + jnp.sum(dng * no, axis=0, keepdims=True)
            do_parts.append(_rms_bwd(dng * g_, no, ro))
        do_t = jnp.concatenate(do_parts, axis=1).T.astype(BF16)

        fac = []
        for sl, do in zip(heads, do_parts):
            qh, kh, bh = q[:, sl], k[:, sl], b[:, sl]
            blk, b_last, eb, eq, ekh, ek, q_hat, k_til = _hg_head(qh, kh, bh)
            fac.append(dict(qh=qh, kh=kh, blk=blk, e_last=jnp.exp(b_last), eb=eb, eq=eq, ekh=ekh, ek=ek,
                            q_til=qh * eb, k_hat=kh * ekh, qhb=q_hat.astype(BF16), ktb=k_til.astype(BF16),
                            vb=v[:, sl].astype(BF16), dob=do.astype(BF16)))

        first = []
        for sl, t in zip(heads, fac):
            st_h = st_ref[0, sl, :]
            dst_h = dst[sl, :]
            dstb = dst_h.astype(BF16)
            first.append(dict(
                att_t=_dot_nt(t["ktb"], t["qhb"]), datt=_dot_nt(t["dob"], t["vb"]),
                datt_t=_dot_nt(t["vb"], t["dob"]), dk_hat=_dot(t["vb"], dstb),
                dv=_dot_nt(t["k_hat"].astype(BF16), dstb), dq_til=_dot(t["dob"], st_h.astype(BF16)),
                state=t["e_last"] * jnp.sum(dst_h * st_h, axis=0, keepdims=True)))
            dst[sl, :] = dst_h * t["e_last"] + _dot(do_t[sl, :], t["q_til"].astype(BF16))

        for h, (t, m) in enumerate(zip(fac, first)):
            qh, kh, blk, eb, eq, ekh, ek = t["qh"], t["kh"], t["blk"], t["eb"], t["eq"], t["ekh"], t["ek"]
            q_til, k_hat, qhb, ktb, dob = t["q_til"], t["k_hat"], t["qhb"], t["ktb"], t["dob"]
            dk_hat, dq_til = m["dk_hat"], m["dq_til"]
            dv = m["dv"] + _dot(jnp.where(causal_t, m["att_t"], 0.0).astype(BF16), dob)
            dq_hat = _dot(jnp.where(causal, m["datt"], 0.0).astype(BF16), ktb)
            dk_til = _dot(jnp.where(causal_t, m["datt_t"], 0.0).astype(BF16), qhb)
            db_last = jnp.sum(dk_hat * k_hat, axis=0, keepdims=True) + m["state"]
            dq_sel = dq_hat[:, (N_SUB - 1) * HG_HEAD_DIM:]
            for s in range(N_SUB - 2, -1, -1):
                dq_sel = jnp.where(blk == s, dq_hat[:, s * HG_HEAD_DIM:(s + 1) * HG_HEAD_DIM], dq_sel)
            dq_a = dq_sel * eq
            dk_a = dk_til[:, :HG_HEAD_DIM] * ek[0]
            for s in range(1, N_SUB):
                dk_a = dk_a + dk_til[:, s * HG_HEAD_DIM:(s + 1) * HG_HEAD_DIM] * ek[s]
            db_att = qhb.astype(F32) * dq_hat - ktb.astype(F32) * dk_til
            db = dq_til * q_til - dk_hat * k_hat
            for s in range(N_SUB):
                db = db + db_att[:, s * HG_HEAD_DIM:(s + 1) * HG_HEAD_DIM]
            db_parts.append(jnp.where(is_last, db + db_last, db))
            dq_parts.append(dq_til * eb + dq_a)
            dk_parts.append(dk_hat * ekh + dk_a)
            dp_ref[:, 2 * D_HG + h * HG_HEAD_DIM:2 * D_HG + (h + 1) * HG_HEAD_DIM] = dv.astype(BF16)

        gg_ref[...] += gg
        db = jnp.concatenate(db_parts, axis=1)
        dq = jnp.concatenate(dq_parts, axis=1)
        dk = jnp.concatenate(dk_parts, axis=1)
        dlf = jnp.where(valid, jnp.dot(_tri(False), db, precision=HIGHEST, preferred_element_type=F32), 0.0)
        dp_ref[:, :D_HG] = (dq * sq * (1.0 + hq * (1.0 - sq))).astype(BF16)
        df = dlf / f - dk
        dlb = jnp.sum(df * (1.0 - sf), axis=0, keepdims=True) * lb * (1.0 - lb)
        glb_ref[0:1, :] += dlb
        glb_ref[1:2, :] += -dlb
        dp_ref[:, D_HG:2 * D_HG] = (df * (1.0 - lb) * sf * (1.0 - sf)).astype(BF16)

    rev = lambda j: pl.BlockSpec((rows, D_HG), lambda i: (n_steps - 1 - i, j))
    return pl.pallas_call(
        body, grid=(n_steps,),
        in_specs=[rev(2), rev(3), rev(4), rev(5), rev(0),
                  pl.BlockSpec((cps, D_HG, HG_HEAD_DIM), lambda i: (n_steps - 1 - i, 0, 0)), rev(1),
                  pl.BlockSpec((2, D_HG), lambda i: (0, 0)), pl.BlockSpec((1, HG_HEAD_DIM), lambda i: (0, 0))],
        out_specs=[pl.BlockSpec((rows, D_IN), lambda i: (n_steps - 1 - i, 0)),
                   pl.BlockSpec((2, D_HG), lambda i: (0, 0)), pl.BlockSpec((1, HG_HEAD_DIM), lambda i: (0, 0))],
        out_shape=[jax.ShapeDtypeStruct((T, D_IN), BF16), jax.ShapeDtypeStruct((2, D_HG), F32),
                   jax.ShapeDtypeStruct((1, HG_HEAD_DIM), F32)],
        scratch_shapes=[pltpu.VMEM((D_HG, HG_HEAD_DIM), F32)],
        name="hg_bwd", compiler_params=_params("arbitrary"),
    )(p, p, p, p, o_all, st_all, dy, lbraw, hg_g)


def _in_bwd(dp, w_in, h0, g1, dh1):
    T = h0.shape[0]
    tm = _row_tile(T, 416)
    n_steps = T // tm

    def body(dp_ref, w_ref, h_ref, g_ref, d1_ref, gx_hbm, gmeta_ref, gg_ref, buf, sems):
        i = pl.program_id(0)
        first, later = _window_copies(gx_hbm, buf, sems, tm)
        slot = i % 2

        @pl.when(i == 0)
        def _():
            gg_ref[...] = jnp.zeros_like(gg_ref)

        if n_steps > 2:
            @pl.when(i == 2)
            def _():
                first(False).wait()

            @pl.when(i > 2)
            def _():
                later(i - 2, slot, False).wait()

        du = _dot_nt(dp_ref[...], w_ref[...])
        h0_ = h_ref[...]
        r = _rms(h0_)
        n = h0_ * r
        gg_ref[...] += jnp.sum(du * n, axis=0, keepdims=True)
        dh0 = d1_ref[...] + _rms_bwd(du * g_ref[...], n, r)
        buf[slot] = dh0

        @pl.when(i == 0)
        def _():
            gmeta_ref[...] = dh0[PAD:HEAD, :]
            first(False).start()

        if n_steps > 1:
            @pl.when(i > 0)
            def _():
                later(i, slot, False).start()

        @pl.when(i == n_steps - 1)
        def _():
            if n_steps == 1:
                first(False).wait()
            else:
                if n_steps == 2:
                    first(False).wait()
                else:
                    later(i - 1, 1 - slot, False).wait()
                later(i, slot, False).wait()

    row = lambda n: pl.BlockSpec((tm, n), lambda i: (i, 0))
    return pl.pallas_call(
        body, grid=(n_steps,),
        in_specs=[row(D_IN), pl.BlockSpec((D_MODEL, D_IN), lambda i: (0, 0)),
                  row(D_MODEL), pl.BlockSpec((1, D_MODEL), lambda i: (0, 0)), row(D_MODEL)],
        out_specs=[pl.BlockSpec(memory_space=pl.ANY), pl.BlockSpec((N_META, D_MODEL), lambda i: (0, 0)),
                   pl.BlockSpec((1, D_MODEL), lambda i: (0, 0))],
        out_shape=[jax.ShapeDtypeStruct((T - HEAD, D_MODEL), F32), jax.ShapeDtypeStruct((N_META, D_MODEL), F32),
                   jax.ShapeDtypeStruct((1, D_MODEL), F32)],
        scratch_shapes=[pltpu.VMEM((2, tm, D_MODEL), F32), pltpu.SemaphoreType.DMA((2,))],
        name="in_bwd", compiler_params=_params("arbitrary"),
    )(dp, w_in, h0, g1, dh1)


def _col_tile(cols, target):
    best = None
    for t in range(128, min(cols, target) + 1, 128):
        if cols % t == 0:
            best = t
    assert best is not None, cols
    return best


MXU_DIM = 256


def _mxu_tile(cols, target):
    best = None
    for t in range(MXU_DIM, min(cols, target) + 1, MXU_DIM):
        if cols % t == 0:
            best = t
    assert best is not None, cols
    return best


def _weight_grad(a, b, name):
    T, M = a.shape
    N = b.shape[1]
    tm = _col_tile(M, 1408)
    tn = _mxu_tile(N, 768 if tm <= 1024 else 512)

    def body(a_ref, b_ref, o_ref):
        o_ref[...] = _dot_tn(a_ref[...], b_ref[...])

    return pl.pallas_call(
        body, grid=(M // tm, N // tn),
        in_specs=[pl.BlockSpec((T, tm), lambda m, n: (0, m)), pl.BlockSpec((T, tn), lambda m, n: (0, n))],
        out_specs=pl.BlockSpec((tm, tn), lambda m, n: (m, n)),
        out_shape=jax.ShapeDtypeStruct((M, N), F32),
        name=name, compiler_params=_params("parallel", "parallel"),
    )(a, b)


def _local_step(x, meta, target, w_in_own, w_in, w_out, w_gu, w_down, small, chip, on_ffn_grads=None,
                on_mixer_grads=None):
    wg = _gate_weights(small["w_rgate"], small["w_igate"])
    bg = jnp.concatenate([small["b_rgate"], small["b_igate"]], axis=1)

    p, u, h0 = _in_proj_local(x, meta, small["mix_norm_g"], w_in_own, chip)
    p = _in_proj_rest(u, w_in, p, chip)
    y_rg, hs = _rg_fwd(p, small["conv_w"], small["conv_b"], wg, bg, small["lru_lambda"], small["rg_norm_g"])
    y_hg, o_all, st_all = _hg_fwd(p, small["hg_lower_bound"], small["hg_norm_g"])
    h1, v, yb, gu, act, dh2, dh2b, loss, g_final = _ffn_fwd(
        h0, y_rg, y_hg, w_out, small["ffn_norm_g"], w_gu, w_down, small["final_norm_g"], target)

    g_w_down = _weight_grad(act, dh2b, "grad_w_down")
    dgu, dh1, dh1b, dy, g_ffn = _ffn_bwd(dh2b, gu, w_down, w_gu, h1, small["ffn_norm_g"], dh2, w_out)
    ffn_grads = {"w_gate_up": _weight_grad(v, dgu, "grad_w_gate_up"), "w_down": g_w_down,
                 "w_out": _weight_grad(yb, dh1b, "grad_w_out")}
    stages = on_ffn_grads(ffn_grads) if on_ffn_grads is not None else None
    dp, g_lb, g_hgn = _hg_bwd(p, o_all, st_all, dy, small["hg_lower_bound"], small["hg_norm_g"])
    early = late = None
    if stages is not None:
        chip_sums, send = stages
        sums = chip_sums()
        (dp, dy), sums = lax.optimization_barrier(((dp, dy), sums))
        early = send(sums)
    dp, g_cw, g_cb, g_wgate, g_bg, g_lam, g_rgn = _rg_bwd(
        p, hs, dy, dp, small["conv_w"], small["conv_b"], wg, bg, small["lru_lambda"], small["rg_norm_g"])
    mixer_grads = {"w_in": _weight_grad(u, dp, "grad_w_in")}
    if on_mixer_grads is not None:
        chip_sums, send = on_mixer_grads(mixer_grads)
        sums = chip_sums()
        (dp, dh1), sums = lax.optimization_barrier(((dp, dh1), sums))
        late = send(sums)
    grad_x, g_meta, g_mix = _in_bwd(dp, w_in, h0, small["mix_norm_g"], dh1)

    grads = {
        "w_in": mixer_grads["w_in"], "w_out": ffn_grads["w_out"],
        "w_gate_up": ffn_grads["w_gate_up"], "w_down": ffn_grads["w_down"],
        "meta_tokens": g_meta, "mix_norm_g": g_mix, "conv_w": g_cw, "conv_b": g_cb, "w_gates": g_wgate,
        "b_rgate": g_bg[:, :D_RG], "b_igate": g_bg[:, D_RG:], "lru_lambda": g_lam, "rg_norm_g": g_rgn,
        "hg_lower_bound": g_lb, "hg_norm_g": g_hgn, "ffn_norm_g": g_ffn, "final_norm_g": g_final,
    }
    return loss, grad_x, grads, early, late


ANY = pl.BlockSpec(memory_space=pl.ANY)
HALF = D_MODEL // 2

BIG = {"w_in": (D_MODEL, D_IN // N_CHIPS, True), "w_gate_up": (D_MODEL, 2 * D_FF // N_CHIPS, True),
       "w_out": (D_MODEL // N_CHIPS, D_MODEL, False), "w_down": (D_FF // N_CHIPS, D_MODEL, False)}
BIG_NAMES = tuple(BIG)
N_BIG = len(BIG_NAMES)


def _full_shape(name):
    rows, cols, by_col = BIG[name]
    return (rows, cols * N_CHIPS) if by_col else (rows * N_CHIPS, cols)


def _place():
    return lax.axis_index("x"), lax.axis_index("y"), lax.axis_index("c")


def _chip_of(x, y, r):
    fx, fy = (r + 1) >> 1, (r + 1) & 1
    return (1 - x if fx else x), (1 - y if fy else y)


def _half_of(ref, by_col, half):
    start = pl.multiple_of(half * HALF, 128)
    return ref.at[pl.ds(start, HALF), :] if by_col else ref.at[:, pl.ds(start, HALF)]


def _shard_of(ref, name, chip):
    rows, cols, by_col = BIG[name]
    if by_col:
        return ref.at[:, pl.ds(pl.multiple_of(chip * cols, 128), cols)]
    return ref.at[pl.ds(pl.multiple_of(chip * rows, 16), rows), :]


def _shard_half_of(ref, name, chip, half):
    rows, cols, by_col = BIG[name]
    start = pl.multiple_of(half * HALF, 128)
    if by_col:
        return ref.at[pl.ds(start, HALF), pl.ds(pl.multiple_of(chip * cols, 128), cols)]
    return ref.at[pl.ds(pl.multiple_of(chip * rows, 16), rows), pl.ds(start, HALF)]


def _remote(src, dst, send_sems, recv_sems, k, dev):
    return pltpu.make_async_remote_copy(src_ref=src, dst_ref=dst, send_sem=send_sems.at[k], recv_sem=recv_sems.at[k],
                                        device_id=dev, device_id_type=MESH)


def _place_shards(w, small, chip):
    steps = 4
    ns = len(small)
    in_specs, out_specs = [], []
    for name in BIG_NAMES:
        rows, cols, by_col = BIG[name]
        tr = rows // steps
        in_specs.append(pl.BlockSpec((tr, cols), lambda i, s: (i, 0)))
        if by_col:
            out_specs.append(pl.BlockSpec((tr, cols), lambda i, s: (i, s[0])))
        else:
            out_specs.append(pl.BlockSpec((tr, cols), lambda i, s: (s[0] * steps + i, 0)))

    def body(s_ref, *refs):
        ins, small_in = refs[:N_BIG], refs[N_BIG:N_BIG + ns]
        outs, small_out = refs[N_BIG + ns:2 * N_BIG + ns], refs[2 * N_BIG + ns:2 * (N_BIG + ns)]
        send_sems, recv_sems, local_sems = refs[2 * (N_BIG + ns):]
        i = pl.program_id(0)
        x, y, c = _place()
        chip_ = 2 * x + y
        others = [_chip_of(x, y, r) for r in range(3)]

        def block(a, q):
            cols = small[a].shape[1]
            return small_out[a].at[:, pl.ds(pl.multiple_of(q * cols, 128), cols)]

        def local(a):
            return pltpu.make_async_copy(small_in[a], block(a, chip_), local_sems.at[a])

        def remote(a, r):
            qx, qy = others[r]
            return _remote(small_in[a], block(a, chip_), send_sems, recv_sems, 3 * a + r, (qx, qy, c))

        @pl.when(i == 0)
        def _():
            for a in range(ns):
                local(a).start()
                for r in range(3):
                    remote(a, r).start()

        for a in range(N_BIG):
            outs[a][...] = ins[a][...].astype(BF16)

        @pl.when(i == steps - 1)
        def _():
            for a in range(ns):
                for r, (qx, qy) in enumerate(others):
                    landed = block(a, 2 * qx + qy)
                    _remote(landed, landed, send_sems, recv_sems, 3 * a + r, (qx, qy, c)).wait_recv()
                for r in range(3):
                    remote(a, r).wait_send()
                local(a).wait()

    out = pl.pallas_call(
        body,
        grid_spec=pltpu.PrefetchScalarGridSpec(
            num_scalar_prefetch=1, grid=(steps,), in_specs=in_specs + [ANY] * ns, out_specs=out_specs + [ANY] * ns,
            scratch_shapes=[pltpu.SemaphoreType.DMA((3 * ns,)), pltpu.SemaphoreType.DMA((3 * ns,)),
                            pltpu.SemaphoreType.DMA((ns,))]),
        out_shape=([jax.ShapeDtypeStruct(_full_shape(name), BF16) for name in BIG_NAMES]
                   + [jax.ShapeDtypeStruct((s.shape[0], s.shape[1] * N_CHIPS), F32) for s in small]),
        name="place_shards", compiler_params=_params("arbitrary"),
    )(chip, *[w[name] for name in BIG_NAMES], *small)
    return dict(zip(BIG_NAMES, out[:N_BIG])), list(out[N_BIG:])


def _gather_weights(placed, small, names, label, collective_id):
    n, ns = len(names), len(small)
    hbm = pltpu.MemorySpace.HBM
    outs = [jax.new_ref(placed[nm], memory_space=hbm) for nm in names]
    small_in = [jax.new_ref(s, memory_space=hbm) for s in small]
    small_out = [jax.empty_ref(jax.ShapeDtypeStruct((s.shape[0], s.shape[1] * N_CHIPS), F32), memory_space=hbm)
                 for s in small]
    n_sems = 6 * n + 3 * ns

    @pl.kernel(mesh=plsc.ScalarSubcoreMesh(axis_name="seq", num_cores=1), name=label, out_type=(),
               scratch_types=(pltpu.SemaphoreType.DMA((n_sems,)), pltpu.SemaphoreType.DMA((n_sems,)),
                              pltpu.SemaphoreType.DMA((max(ns, 1),))),
               compiler_params=pltpu.CompilerParams(collective_id=collective_id))
    def launch(send_sems, recv_sems, local_sems):
        x, y, c = _place()
        chip = 2 * x + y
        sibling = (x, y, 1 - c)
        others = [_chip_of(x, y, r) for r in range(3)]
        _handshake([(qx, qy, c) for qx, qy in others] + [sibling])

        def small_block(a, q):
            cols = small[a].shape[1]
            return small_out[a].at[:, pl.ds(pl.multiple_of(q * cols, 128), cols)]

        local = [pltpu.make_async_copy(small_in[a], small_block(a, chip), local_sems.at[a]) for a in range(ns)]
        for cp in local:
            cp.start()

        sends = []
        for a, name in enumerate(names):
            mine = _shard_half_of(outs[a], name, chip, c)
            for r, (qx, qy) in enumerate(others):
                sends.append(_remote(mine, mine, send_sems, recv_sems, 6 * a + r, (qx, qy, c)))
        for a in range(ns):
            for r, (qx, qy) in enumerate(others):
                sends.append(_remote(small_in[a], small_block(a, chip), send_sems, recv_sems,
                                     6 * n + 3 * a + r, (qx, qy, c)))
        for cp in sends:
            cp.start()

        forwards = []
        for a, name in enumerate(names):
            for r, (qx, qy) in enumerate(others):
                landed = _shard_half_of(outs[a], name, 2 * qx + qy, c)
                _remote(landed, landed, send_sems, recv_sems, 6 * a + r, (qx, qy, c)).wait_recv()
                fwd = _remote(landed, landed, send_sems, recv_sems, 6 * a + 3 + r, sibling)
                fwd.start()
                forwards.append(fwd)
        for a in range(ns):
            for r, (qx, qy) in enumerate(others):
                landed = small_block(a, 2 * qx + qy)
                _remote(landed, landed, send_sems, recv_sems, 6 * n + 3 * a + r, (qx, qy, c)).wait_recv()
        for a, name in enumerate(names):
            for r, (qx, qy) in enumerate(others):
                landed = _shard_half_of(outs[a], name, 2 * qx + qy, 1 - c)
                _remote(landed, landed, send_sems, recv_sems, 6 * a + 3 + r, sibling).wait_recv()
        for cp in sends + forwards:
            cp.wait_send()
        for cp in local:
            cp.wait()

    launch()
    return {nm: ref[...] for nm, ref in zip(names, outs)}, [ref[...] for ref in small_out]


def _exchange_halves(grads, names, label, collective_id):
    n = len(names)
    sequencer = collective_id is not None

    def body(*refs):
        ins, outs = refs[:n], refs[n:2 * n]
        send_sems, recv_sems = refs[2 * n:]
        x, y, c = _place()
        if sequencer:
            _handshake([(x, y, 1 - c)])
        copies = []
        for a, name in enumerate(names):
            copies.append(_remote(_half_of(ins[a], BIG[name][2], 1 - c), outs[a], send_sems, recv_sems, a,
                                  (x, y, 1 - c)))
        for cp in copies:
            cp.start()
        for cp in copies:
            cp.wait()

    def half_shape(name):
        r, c_ = _full_shape(name)
        return (HALF, c_) if BIG[name][2] else (r, HALF)

    out_type = tuple(jax.ShapeDtypeStruct(half_shape(nm), F32) for nm in names)
    sems = (pltpu.SemaphoreType.DMA((n,)), pltpu.SemaphoreType.DMA((n,)))
    operands = [grads[nm] for nm in names]
    if sequencer:
        got = pl.kernel(
            body, mesh=plsc.ScalarSubcoreMesh(axis_name="seq", num_cores=1), name=label, out_type=out_type,
            scratch_types=sems, compiler_params=pltpu.CompilerParams(collective_id=collective_id),
        )(*operands)
    else:
        got = pl.pallas_call(
            body, in_specs=[ANY] * n, out_specs=[ANY] * n, out_shape=list(out_type), scratch_shapes=list(sems),
            name=label,
        )(*operands)
    return dict(zip(names, got))


def _chip_sum(grads, got, names, core, label):
    n = len(names)
    steps = 4
    g_specs, blks = [], []
    for name in names:
        rows, cols = got[name].shape
        tr = rows // steps
        if BIG[name][2]:
            g_specs.append(pl.BlockSpec((tr, cols), lambda i, s: (s[0] * steps + i, 0)))
        else:
            g_specs.append(pl.BlockSpec((tr, HALF), lambda i, s: (i, s[0])))
        blks.append(pl.BlockSpec((tr, cols), lambda i, s: (i, 0)))

    def body(s_ref, *refs):
        for a in range(n):
            t = refs[a][...] + refs[n + a][...]
            refs[2 * n + a][...] = t
            refs[3 * n + a][...] = t.astype(BF16)

    out = pl.pallas_call(
        body,
        grid_spec=pltpu.PrefetchScalarGridSpec(num_scalar_prefetch=1, grid=(steps,), in_specs=g_specs + blks,
                                               out_specs=blks + blks),
        out_shape=([jax.ShapeDtypeStruct(got[nm].shape, F32) for nm in names]
                   + [jax.ShapeDtypeStruct(got[nm].shape, BF16) for nm in names]),
        name=label, compiler_params=_params("parallel"),
    )(core, *[grads[nm] for nm in names], *[got[nm] for nm in names])
    return {nm: (out[a], out[n + a]) for a, nm in enumerate(names)}


def _piece_shape(name):
    rows, cols, by_col = BIG[name]
    return (HALF, cols) if by_col else (rows, HALF)


def _handshake(peers):
    barrier = pltpu.get_barrier_semaphore()
    for peer in peers:
        pl.semaphore_signal(barrier, inc=1, device_id=peer, device_id_type=MESH)
    pl.semaphore_wait(barrier, len(peers))


def _send_chip_sums(sums, names, label, collective_id):
    n = len(names)

    def body(*refs):
        ins, outs = refs[:n], refs[n:2 * n]
        send_sems, recv_sems = refs[2 * n:]
        x, y, c = _place()
        others = [_chip_of(x, y, r) for r in range(3)]
        _handshake([(qx, qy, c) for qx, qy in others])
        copies = []
        for a, name in enumerate(names):
            for r, (qx, qy) in enumerate(others):
                copies.append(_remote(_shard_of(ins[a], name, 2 * qx + qy), outs[a].at[r], send_sems, recv_sems,
                                      3 * a + r, (qx, qy, c)))
        for cp in copies:
            cp.start()
        for cp in copies:
            cp.wait()

    return pl.kernel(
        body, mesh=plsc.ScalarSubcoreMesh(axis_name="seq", num_cores=1), name=label,
        out_type=tuple(jax.ShapeDtypeStruct((3,) + _piece_shape(nm), BF16) for nm in names),
        scratch_types=(pltpu.SemaphoreType.DMA((3 * n,)), pltpu.SemaphoreType.DMA((3 * n,))),
        compiler_params=pltpu.CompilerParams(collective_id=collective_id),
    )(*[sums[nm] for nm in names])


def _total(parts, chip_core):
    steps = 2
    in_specs, out_specs, operands = [], [], []
    for name in BIG_NAMES:
        by_col = BIG[name][2]
        pr, pc = _piece_shape(name)
        tr = pr // steps
        if by_col:
            in_specs.append(pl.BlockSpec((tr, pc), lambda i, s: (i, s[0])))
            out_specs.append(pl.BlockSpec((tr, pc), lambda i, s: (s[1] * steps + i, 0)))
        else:
            in_specs.append(pl.BlockSpec((tr, pc), lambda i, s: (s[0] * steps + i, 0)))
            out_specs.append(pl.BlockSpec((tr, pc), lambda i, s: (i, s[1])))
        for r in range(3):
            in_specs.append(pl.BlockSpec((None, tr, pc), lambda i, s, r=r: (r, i, 0)))
        own, got = parts[name]
        operands += [own, got, got, got]

    def body(s_ref, *refs):
        for a in range(N_BIG):
            o_ref, a_ref, b_ref, c_ref = refs[4 * a:4 * a + 4]
            refs[4 * N_BIG + a][...] = (((o_ref[...] + a_ref[...].astype(F32)) + b_ref[...].astype(F32))
                                        + c_ref[...].astype(F32))

    totals = pl.pallas_call(
        body,
        grid_spec=pltpu.PrefetchScalarGridSpec(num_scalar_prefetch=1, grid=(steps,), in_specs=in_specs,
                                               out_specs=out_specs),
        out_shape=[jax.ShapeDtypeStruct(BIG[name][:2], F32) for name in BIG_NAMES],
        name="totals", compiler_params=_params("parallel"),
    )(chip_core, *operands)
    return dict(zip(BIG_NAMES, totals))


def _share_totals(totals):
    def body(*refs):
        outs = refs[N_BIG:2 * N_BIG]
        send_sems, recv_sems = refs[2 * N_BIG:]
        x, y, c = _place()
        copies = []
        for a, name in enumerate(BIG_NAMES):
            mine = _half_of(outs[a], BIG[name][2], c)
            copies.append(_remote(mine, mine, send_sems, recv_sems, a, (x, y, 1 - c)))
        for cp in copies:
            cp.start()
        for a, name in enumerate(BIG_NAMES):
            theirs = _half_of(outs[a], BIG[name][2], 1 - c)
            _remote(theirs, theirs, send_sems, recv_sems, a, (x, y, 1 - c)).wait_recv()
        for cp in copies:
            cp.wait_send()

    return pl.pallas_call(
        body, in_specs=[ANY] * N_BIG, out_specs=[ANY] * N_BIG,
        out_shape=[jax.ShapeDtypeStruct(BIG[n][:2], F32) for n in BIG_NAMES],
        input_output_aliases={a: a for a in range(N_BIG)},
        scratch_shapes=[pltpu.SemaphoreType.DMA((N_BIG,)), pltpu.SemaphoreType.DMA((N_BIG,))],
        name="share_totals",
    )(*[totals[n] for n in BIG_NAMES])


VEC_ROWS = 32
VEC_ROW = {"mix_norm_g": 0, "conv_b": 1, "b_rgate": 2, "b_igate": 3, "lru_lambda": 4, "rg_norm_g": 5,
           "hg_lower_bound": 6, "hg_norm_g": 8, "ffn_norm_g": 9, "final_norm_g": 10, "loss": 11,
           "conv_w": 12, "meta_tokens": 16}
N_DEV = 8


def _all_reduce_small(pieces, gates):
    names = list(pieces)
    hv, hg = VEC_ROWS // 2, gates.shape[0] // 2

    def body(*refs):
        ins = refs[:len(names)]
        (g_ref, vec_ref, gsum_ref, mine_v, sib_v, sib_g, chip_v, chip_g, got_v, got_g,
         send_sems, recv_sems) = refs[len(names):]
        x, y, c = _place()
        chip = 2 * x + y
        sibling = (x, y, 1 - c)
        mine_v[...] = jnp.zeros_like(mine_v)
        for name, ref in zip(names, ins):
            nr, w = ref.shape
            mine_v[VEC_ROW[name]:VEC_ROW[name] + nr, 0:w] = ref[...]

        swap = [_remote(mine_v, sib_v, send_sems, recv_sems, 0, sibling),
                _remote(g_ref, sib_g, send_sems, recv_sems, 1, sibling)]
        for cp in swap:
            cp.start()
        for cp in swap:
            cp.wait()
        chip_v[...] = mine_v[...] + sib_v[...]
        chip_g[...] = g_ref[...] + sib_g[...]

        rows_v = pl.ds(pl.multiple_of(c * hv, 8), hv)
        rows_g = pl.ds(pl.multiple_of(c * hg, 8), hg)
        got_v[chip] = chip_v[rows_v, :]
        got_g[chip] = chip_g[rows_g, :]
        sends = []
        for r in range(3):
            qx, qy = _chip_of(x, y, r)
            sends.append(_remote(chip_v.at[rows_v, :], got_v.at[chip], send_sems, recv_sems, 2 + r, (qx, qy, c)))
            sends.append(_remote(chip_g.at[rows_g, :], got_g.at[chip], send_sems, recv_sems, 5 + r, (qx, qy, c)))
        for cp in sends:
            cp.start()
        for cp in sends:
            cp.wait()
        vec_ref[rows_v, :] = ((got_v[0] + got_v[1]) + got_v[2]) + got_v[3]
        gsum_ref[rows_g, :] = ((got_g[0] + got_g[1]) + got_g[2]) + got_g[3]

        back = [_remote(vec_ref.at[rows_v, :], vec_ref.at[rows_v, :], send_sems, recv_sems, 8, sibling),
                _remote(gsum_ref.at[rows_g, :], gsum_ref.at[rows_g, :], send_sems, recv_sems, 9, sibling)]
        for cp in back:
            cp.start()
        theirs_v = vec_ref.at[pl.ds(pl.multiple_of((1 - c) * hv, 8), hv), :]
        theirs_g = gsum_ref.at[pl.ds(pl.multiple_of((1 - c) * hg, 8), hg), :]
        _remote(theirs_v, theirs_v, send_sems, recv_sems, 8, sibling).wait_recv()
        _remote(theirs_g, theirs_g, send_sems, recv_sems, 9, sibling).wait_recv()
        for cp in back:
            cp.wait_send()

    vmem = pl.BlockSpec(memory_space=pltpu.VMEM)
    n_sems = 10
    return pl.pallas_call(
        body, in_specs=[vmem] * (len(names) + 1), out_specs=[vmem, vmem],
        out_shape=[jax.ShapeDtypeStruct((VEC_ROWS, D_MODEL), F32), jax.ShapeDtypeStruct(gates.shape, F32)],
        scratch_shapes=[pltpu.VMEM((VEC_ROWS, D_MODEL), F32), pltpu.VMEM((VEC_ROWS, D_MODEL), F32),
                        pltpu.VMEM(gates.shape, F32), pltpu.VMEM((VEC_ROWS, D_MODEL), F32),
                        pltpu.VMEM(gates.shape, F32), pltpu.VMEM((N_CHIPS, hv, D_MODEL), F32),
                        pltpu.VMEM((N_CHIPS, hg) + gates.shape[1:], F32),
                        pltpu.SemaphoreType.DMA((n_sems,)), pltpu.SemaphoreType.DMA((n_sems,))],
        name="all_reduce_small",
    )(*[pieces[n] for n in names], gates)


def _adamw_math(w, g, m, v):
    m = ADAM_B1 * m + (1.0 - ADAM_B1) * g
    v = ADAM_B2 * v + (1.0 - ADAM_B2) * (g * g)
    m_hat = m / (1.0 - ADAM_B1 ** ADAM_STEP)
    v_hat = v / (1.0 - ADAM_B2 ** ADAM_STEP)
    delta = -ADAM_LR * (m_hat / (jnp.sqrt(v_hat) + ADAM_EPS) + ADAM_WD * w)
    return delta, m, v


def _adamw_big(w, g, m, v):
    steps = 8
    blks = []
    for name in BIG_NAMES:
        rows, cols, _ = BIG[name]
        blks.append(pl.BlockSpec((rows // steps, cols), lambda i: (i, 0)))

    def body(*refs):
        ins, outs = refs[:4 * N_BIG], refs[4 * N_BIG:]
        for a in range(N_BIG):
            w_ref, g_ref, m_ref, v_ref = (ins[k * N_BIG + a] for k in range(4))
            d, nm, nv = _adamw_math(w_ref[...], g_ref[...], m_ref[...], v_ref[...])
            outs[a][...] = d
            outs[N_BIG + a][...] = nm
            outs[2 * N_BIG + a][...] = nv

    shapes = [jax.ShapeDtypeStruct(BIG[name][:2], F32) for name in BIG_NAMES]
    out = pl.pallas_call(
        body, grid=(steps,), in_specs=blks * 4, out_specs=blks * 3, out_shape=shapes * 3,
        name="adamw_big", compiler_params=_params("parallel"),
    )(*[t[name] for t in (w, g, m, v) for name in BIG_NAMES])
    return {name: (out[a], out[N_BIG + a], out[2 * N_BIG + a]) for a, name in enumerate(BIG_NAMES)}


SMALL = {"meta_tokens": (N_META, D_MODEL // N_CHIPS), "mix_norm_g": (1, D_MODEL), "conv_w": (CONV_W, D_RG // N_CHIPS),
         "conv_b": (1, D_RG), "w_rgate": (D_RG, RG_HEAD_DIM), "b_rgate": (1, D_RG), "w_igate": (D_RG, RG_HEAD_DIM),
         "b_igate": (1, D_RG), "lru_lambda": (1, D_RG), "rg_norm_g": (1, D_RG), "hg_lower_bound": (2, D_HG),
         "hg_norm_g": (1, HG_HEAD_DIM), "ffn_norm_g": (1, D_MODEL), "final_norm_g": (1, D_MODEL)}
SMALL_NAMES = tuple(SMALL)
SHARDED_SMALL = ("meta_tokens", "conv_w")


def _adamw_small(vec, gates, w, m, v):
    n = len(SMALL_NAMES)

    def body(*refs):
        vec_ref, gates_ref = refs[:2]
        w_refs, m_refs, v_refs = refs[2:2 + n], refs[2 + n:2 + 2 * n], refs[2 + 2 * n:2 + 3 * n]
        outs = refs[2 + 3 * n:]
        loss_ref = outs[0]
        x, y, _ = _place()
        chip = 2 * x + y
        loss_ref[...] = vec_ref[VEC_ROW["loss"]:VEC_ROW["loss"] + 1, 0:1]

        def update(k, g):
            g_ref, d_ref, nm_ref, nv_ref = outs[1 + 4 * k:5 + 4 * k]
            g_ref[...] = g
            d_ref[...], nm_ref[...], nv_ref[...] = _adamw_math(w_refs[k][...], g, m_refs[k][...], v_refs[k][...])

        for k, name in enumerate(SMALL_NAMES):
            nr, w_ = SMALL[name]
            if name == "w_rgate":
                update(k, gates_ref[0:D_RG, :])
            elif name == "w_igate":
                update(k, gates_ref[D_RG:2 * D_RG, :])
            elif name in SHARDED_SMALL:
                r0 = VEC_ROW[name]
                for q in range(N_CHIPS):
                    @pl.when(chip == q)
                    def _(k=k, r0=r0, nr=nr, w_=w_, q=q):
                        update(k, vec_ref[r0:r0 + nr, q * w_:(q + 1) * w_])
            else:
                r0 = VEC_ROW[name]
                update(k, vec_ref[r0:r0 + nr, 0:w_])

    vmem = pl.BlockSpec(memory_space=pltpu.VMEM)
    out_shape = [jax.ShapeDtypeStruct((1, 1), F32)]
    for name in SMALL_NAMES:
        out_shape += [jax.ShapeDtypeStruct(SMALL[name], F32)] * 4
    outs = pl.pallas_call(
        body, in_specs=[vmem] * (2 + 3 * n), out_specs=[vmem] * len(out_shape), out_shape=out_shape,
        name="adamw_small",
    )(vec, gates, *[w[k] for k in SMALL_NAMES], *[m[k] for k in SMALL_NAMES], *[v[k] for k in SMALL_NAMES])
    loss = outs[0]
    res = {name: tuple(outs[1 + 4 * k:5 + 4 * k]) for k, name in enumerate(SMALL_NAMES)}
    return loss, res


WEIGHT_NAMES = ("meta_tokens", "mix_norm_g", "w_in", "conv_w", "conv_b", "w_rgate", "b_rgate", "w_igate", "b_igate",
                "lru_lambda", "rg_norm_g", "hg_lower_bound", "hg_norm_g", "w_out", "ffn_norm_g", "w_gate_up", "w_down",
                "final_norm_g")


def _to_2d(name, a):
    if name in BIG:
        return a.reshape(BIG[name][:2])
    return a.reshape(SMALL[name])


def kernel(x, meta_tokens, mix_norm_g, w_in, conv_w, conv_b, w_rgate, b_rgate, w_igate, b_igate, lru_lambda, rg_norm_g, hg_lower_bound, hg_norm_g, w_out, ffn_norm_g, w_gate_up, w_down, final_norm_g, loss_target, m_meta_tokens, m_mix_norm_g, m_w_in, m_conv_w, m_conv_b, m_w_rgate, m_b_rgate, m_w_igate, m_b_igate, m_lru_lambda, m_rg_norm_g, m_hg_lower_bound, m_hg_norm_g, m_w_out, m_ffn_norm_g, m_w_gate_up, m_w_down, m_final_norm_g, v_meta_tokens, v_mix_norm_g, v_w_in, v_conv_w, v_conv_b, v_w_rgate, v_b_rgate, v_w_igate, v_b_igate, v_lru_lambda, v_rg_norm_g, v_hg_lower_bound, v_hg_norm_g, v_w_out, v_ffn_norm_g, v_w_gate_up, v_w_down, v_final_norm_g):
    w_raw = dict(zip(WEIGHT_NAMES, (meta_tokens, mix_norm_g, w_in, conv_w, conv_b, w_rgate, b_rgate, w_igate, b_igate,
                                    lru_lambda, rg_norm_g, hg_lower_bound, hg_norm_g, w_out, ffn_norm_g, w_gate_up,
                                    w_down, final_norm_g)))
    m_raw = dict(zip(WEIGHT_NAMES, (m_meta_tokens, m_mix_norm_g, m_w_in, m_conv_w, m_conv_b, m_w_rgate, m_b_rgate,
                                    m_w_igate, m_b_igate, m_lru_lambda, m_rg_norm_g, m_hg_lower_bound, m_hg_norm_g,
                                    m_w_out, m_ffn_norm_g, m_w_gate_up, m_w_down, m_final_norm_g)))
    v_raw = dict(zip(WEIGHT_NAMES, (v_meta_tokens, v_mix_norm_g, v_w_in, v_conv_w, v_conv_b, v_w_rgate, v_b_rgate,
                                    v_w_igate, v_b_igate, v_lru_lambda, v_rg_norm_g, v_hg_lower_bound, v_hg_norm_g,
                                    v_w_out, v_ffn_norm_g, v_w_gate_up, v_w_down, v_final_norm_g)))
    w = {k: _to_2d(k, a) for k, a in w_raw.items()}
    m = {k: _to_2d(k, a) for k, a in m_raw.items()}
    v = {k: _to_2d(k, a) for k, a in v_raw.items()}

    x_i, y_i, c_i = _place()
    core = jnp.reshape(c_i, (1,)).astype(jnp.int32)
    chip = jnp.reshape(2 * x_i + y_i, (1,)).astype(jnp.int32)
    chip_core = jnp.concatenate([chip, core])

    placed, (meta_full, cw_full) = _place_shards(w, [w["meta_tokens"], w["conv_w"]], chip)
    first, _ = _gather_weights(placed, [], ("w_in",), "gather_first", 1)
    rest, _ = _gather_weights(placed, [], ("w_out", "w_gate_up", "w_down"), "gather_rest", 2)
    full = {**first, **rest}

    seq = x.shape[1]
    small ={k: w[k] for k in SMALL_NAMES if k not in SHARDED_SMALL}
    small["conv_w"] = cw_full

    def reduce_to_chips(grads, names, tag, collective_ids):
        got = _exchange_halves(grads, names, "exchange_halves_" + tag, collective_ids[0])

        def chip_sums():
            return _chip_sum(grads, got, names, core, "chip_sum_" + tag)

        def send(sums):
            arrived = _send_chip_sums({n: sums[n][1] for n in names}, names, "send_chip_sums_" + tag,
                                      collective_ids[1])
            return {n: (sums[n][0], a) for n, a in zip(names, arrived)}

        return chip_sums, send

    ffn_names, mixer_names = ("w_gate_up", "w_down", "w_out"), ("w_in",)
    loss, grad_x, grads, parts, parts_mixer = _local_step(
        x.reshape(seq, D_MODEL), meta_full, loss_target.reshape(seq, D_MODEL),
        w["w_in"], full["w_in"], full["w_out"], full["w_gate_up"], full["w_down"], small, chip,
        on_ffn_grads=lambda g: reduce_to_chips(g, ffn_names, "ffn", (3, 4)),
        on_mixer_grads=lambda g: reduce_to_chips(g, mixer_names, "mixer", (None, 5)))
    parts.update(parts_mixer)
    totals = _total(parts, chip_core)
    g_big = dict(zip(BIG_NAMES, _share_totals(totals)))

    pieces = {k: grads[k] for k in VEC_ROW if k != "loss"}
    pieces["loss"] = loss
    vec, gates = _all_reduce_small(pieces, grads["w_gates"])
    loss_sum, res = _adamw_small(vec, gates, w, m, v)
    updates = _adamw_big(w, g_big, m, v)
    for n in BIG_NAMES:
        res[n] = (g_big[n],) + updates[n]

    out = [loss_sum.reshape(()), grad_x.reshape(1, seq, D_MODEL)]
    for j in range(4):
        out += [res[n][j].reshape(w_raw[n].shape) for n in WEIGHT_NAMES]
    return tuple(out)
```

```python
import functools
import math

import jax
import jax.numpy as jnp
from jax import lax
from jax.experimental import pallas as pl
from jax.experimental.pallas import tpu as pltpu
from jax.experimental.pallas import tpu_sc as plsc

F32 = jnp.float32
BF16 = jnp.bfloat16
HIGHEST = lax.Precision.HIGHEST
MESH = pl.DeviceIdType.MESH

D_MODEL = 1024
D_RG = 512
RG_HEAD_DIM = 64
D_HG = 512
HG_HEAD_DIM = 128
HG_HEADS = 4
CHUNK = 64
SUB = 16
N_SUB = CHUNK // SUB
N_META = 16
PAD = CHUNK - N_META
D_IN = 3072
D_FF = 2816
CONV_W = 4
LRU_C = 8.0
EPS = 1e-6
EXP_CLAMP = 80.0
GELU_C = math.sqrt(2.0 / math.pi)
GELU_A = 0.044715
N_CHIPS = 4

ADAM_LR = 0.001
ADAM_B1 = 0.9
ADAM_B2 = 0.999
ADAM_EPS = 1e-08
ADAM_WD = 0.01
ADAM_STEP = 10

VMEM_LIMIT = 56 * 1024 * 1024


def _params(*sem):
    return pltpu.CompilerParams(dimension_semantics=sem, vmem_limit_bytes=VMEM_LIMIT)


def _row_tile(rows, target):
    best = None
    for t in range(16, min(rows, target) + 1, 16):
        if rows % t == 0:
            best = t
    assert best is not None, rows
    return best


def _sigmoid(x):
    return 0.5 * jnp.tanh(0.5 * x) + 0.5


def _dot(a, b):
    return jnp.dot(a, b, preferred_element_type=F32)


def _dot_nt(a, b):
    return lax.dot_general(a, b, (((1,), (1,)), ((), ())), preferred_element_type=F32)


def _dot_tn(a, b):
    return lax.dot_general(a, b, (((0,), (0,)), ((), ())), preferred_element_type=F32)


def _rms(x):
    return lax.rsqrt(jnp.mean(x * x, axis=-1, keepdims=True) + EPS)


def _rms_bwd(dn, n, r):
    return r * (dn - n * jnp.mean(dn * n, axis=-1, keepdims=True))


def _gelu_parts(x):
    t = jnp.tanh(GELU_C * (x + GELU_A * x * x * x))
    g = 0.5 * x * (1.0 + t)
    dg = 0.5 * (1.0 + t) + 0.5 * x * (1.0 - t * t) * GELU_C * (1.0 + 3.0 * GELU_A * x * x)
    return g, dg


def _softplus_neg(lam):
    e = jnp.exp(-jnp.abs(lam))
    w = 1.0 + e
    log1p = jnp.where(w == 1.0, e, jnp.log(w) * e / (w - 1.0))
    return jnp.maximum(-lam, 0.0) + log1p


def _head_mask():
    r = lax.broadcasted_iota(jnp.int32, (D_RG, D_RG), 0) // RG_HEAD_DIM
    c = lax.broadcasted_iota(jnp.int32, (D_RG, D_RG), 1) // RG_HEAD_DIM
    return r == c


def _head_fold():
    r = lax.broadcasted_iota(jnp.int32, (D_RG, RG_HEAD_DIM), 0) % RG_HEAD_DIM
    c = lax.broadcasted_iota(jnp.int32, (D_RG, RG_HEAD_DIM), 1)
    return (r == c).astype(F32)


def _gate_weights(w_r, w_i):
    def body(wr_ref, wi_ref, o_ref):
        fold = _head_fold()
        mask = _head_mask()
        for k, ref in enumerate((wr_ref, wi_ref)):
            full = lax.dot_general(ref[...], fold, (((1,), (1,)), ((), ())),
                                   precision=HIGHEST, preferred_element_type=F32)
            o_ref[:, k * D_RG:(k + 1) * D_RG] = jnp.where(mask, full, 0.0).astype(BF16)

    return pl.pallas_call(
        body, out_shape=jax.ShapeDtypeStruct((D_RG, 2 * D_RG), BF16), name="gate_weights",
    )(w_r, w_i)


HEAD = PAD + N_META


def _window_copies(seq_hbm, buf, sems, tm):
    def first(to_vmem):
        seq, vm = seq_hbm.at[pl.ds(0, tm - HEAD)], buf.at[0, pl.ds(HEAD, tm - HEAD)]
        return pltpu.make_async_copy(seq, vm, sems.at[0]) if to_vmem else pltpu.make_async_copy(vm, seq, sems.at[0])

    def later(j, slot, to_vmem):
        seq, vm = seq_hbm.at[pl.ds(pl.multiple_of(j * tm - HEAD, 8), tm)], buf.at[slot]
        if to_vmem:
            return pltpu.make_async_copy(seq, vm, sems.at[slot])
        return pltpu.make_async_copy(vm, seq, sems.at[slot])

    return first, later


def _fetch_window(seq_hbm, buf, sems, i, n_steps, tm):
    first, later = _window_copies(seq_hbm, buf, sems, tm)
    slot = i % 2

    @pl.when(i == 0)
    def _():
        first(True).start()

    if n_steps > 1:
        @pl.when(i + 1 < n_steps)
        def _():
            later(i + 1, 1 - slot, True).start()

    @pl.when(i == 0)
    def _():
        first(True).wait()

    if n_steps > 1:
        @pl.when(i > 0)
        def _():
            later(i, slot, True).wait()

    return slot


def _in_proj_local(x, meta, g1, w_own, chip):
    T = x.shape[0] + HEAD
    tm = _row_tile(T, 832)
    n_steps = T // tm
    cols = BIG["w_in"][1]

    def body(s_ref, x_hbm, meta_ref, g_ref, w_ref, p_ref, u_ref, h_ref, buf, sems, wb):
        i = pl.program_id(0)
        slot = _fetch_window(x_hbm, buf, sems, i, n_steps, tm)

        @pl.when(i == 0)
        def _():
            buf[0, 0:PAD, :] = jnp.zeros((PAD, D_MODEL), F32)
            buf[0, PAD:HEAD, :] = meta_ref[...]
            wb[...] = w_ref[...].astype(BF16)

        h = buf[slot]
        h_ref[...] = h
        u = (h * _rms(h) * g_ref[...]).astype(BF16)
        u_ref[...] = u
        p_ref[...] = _dot(u, wb[...])

    return pl.pallas_call(
        body,
        grid_spec=pltpu.PrefetchScalarGridSpec(
            num_scalar_prefetch=1, grid=(n_steps,),
            in_specs=[pl.BlockSpec(memory_space=pl.ANY),
                      pl.BlockSpec((N_META, D_MODEL), lambda i, s: (0, 0)),
                      pl.BlockSpec((1, D_MODEL), lambda i, s: (0, 0)),
                      pl.BlockSpec((D_MODEL, cols), lambda i, s: (0, 0))],
            out_specs=[pl.BlockSpec((tm, cols), lambda i, s: (i, s[0])),
                       pl.BlockSpec((tm, D_MODEL), lambda i, s: (i, 0)),
                       pl.BlockSpec((tm, D_MODEL), lambda i, s: (i, 0))],
            scratch_shapes=[pltpu.VMEM((2, tm, D_MODEL), F32), pltpu.SemaphoreType.DMA((2,)),
                            pltpu.VMEM((D_MODEL, cols), BF16)]),
        out_shape=[jax.ShapeDtypeStruct((T, D_IN), F32), jax.ShapeDtypeStruct((T, D_MODEL), BF16),
                   jax.ShapeDtypeStruct((T, D_MODEL), F32)],
        name="in_proj_local", compiler_params=_params("arbitrary"),
    )(chip, x, meta, g1, w_own)


def _in_proj_rest(u, w_in, p, chip):
    T = u.shape[0]
    tm = _row_tile(T, 2080)
    cols = BIG["w_in"][1]
    block = lambda j, s: (s[0] + 1 + j) % N_CHIPS

    def body(s_ref, u_ref, w_ref, p_in_ref, p_ref):
        p_ref[...] = _dot(u_ref[...], w_ref[...])

    return pl.pallas_call(
        body,
        grid_spec=pltpu.PrefetchScalarGridSpec(
            num_scalar_prefetch=1, grid=(N_CHIPS - 1, T // tm),
            in_specs=[pl.BlockSpec((tm, D_MODEL), lambda j, i, s: (i, 0)),
                      pl.BlockSpec((D_MODEL, cols), lambda j, i, s: (0, block(j, s))), ANY],
            out_specs=pl.BlockSpec((tm, cols), lambda j, i, s: (i, block(j, s)))),
        out_shape=jax.ShapeDtypeStruct((T, D_IN), F32),
        input_output_aliases={3: 0},
        name="in_proj_rest", compiler_params=_params("arbitrary", "arbitrary"),
    )(chip, u, w_in, p)


def _scan_block_fwd(A, B, rowi):
    for d in (1, 2, 4):
        a_sh = pltpu.roll(A, d, axis=0)
        b_sh = pltpu.roll(B, d, axis=0)
        m = rowi >= d
        B = jnp.where(m, A * b_sh + B, B)
        A = jnp.where(m, A * a_sh, A)
    return A, B


def _scan_block_bwd(A, B, rowi):
    for d in (1, 2, 4):
        a_sh = pltpu.roll(A, 8 - d, axis=0)
        b_sh = pltpu.roll(B, 8 - d, axis=0)
        m = rowi < 8 - d
        B = jnp.where(m, A * b_sh + B, B)
        A = jnp.where(m, A * a_sh, A)
    return A, B


def _rg_gates(xc, w_ref, bg_ref, lam):
    pre = _dot(xc.astype(BF16), w_ref[...]) + bg_ref[...]
    r = _sigmoid(pre[:, :D_RG])
    ig = _sigmoid(pre[:, D_RG:])
    sp = _softplus_neg(lam)
    la = -LRU_C * sp * r
    a = jnp.exp(la)
    th = jnp.tanh(la)
    u = 1.0 - th
    rc = pl.reciprocal(u, approx=True)
    rc = rc * (2.0 - u * rc)
    rc = rc * (2.0 - u * rc)
    m2 = -2.0 * th * rc
    inv_m = lax.rsqrt(jnp.maximum(m2, 1e-30))
    return r, ig, sp, a, m2 * inv_m, inv_m


def _conv(ext, cw_ref, cb_ref, tm):
    xc = cb_ref[...] + cw_ref[0:1, :] * ext[8 - 3:8 - 3 + tm, :]
    for j in range(1, CONV_W):
        xc = xc + cw_ref[j:j + 1, :] * ext[8 - 3 + j:8 - 3 + j + tm, :]
    return xc


def _scan_unroll(blocks):
    return 4 if blocks % 4 == 0 else 2 if blocks % 2 == 0 else 1


def _rg_fwd(p, cw, cb, wg, bg, lam, rg_g):
    T = p.shape[0]
    tm = _row_tile(T, 832)
    unroll = _scan_unroll(tm // 8)

    def body(xg_ref, cw_ref, cb_ref, w_ref, bg_ref, lam_ref, g_ref, y_ref, h_ref, ext, a_s, b_s, carry):
        i = pl.program_id(0)

        @pl.when(i == 0)
        def _():
            ext[0:8, :] = jnp.zeros((8, D_RG), F32)
            carry[...] = jnp.zeros((1, D_RG), F32)

        ext[8:8 + tm, :] = xg_ref[:, :D_RG]
        xc = _conv(ext, cw_ref, cb_ref, tm)
        r, ig, sp, a, m, _ = _rg_gates(xc, w_ref, bg_ref, lam_ref[...])
        row = i * tm + lax.broadcasted_iota(jnp.int32, (tm, 1), 0)
        a_s[...] = a
        b_s[...] = jnp.where(row >= PAD, m * ig * xc, 0.0)
        rowi = lax.broadcasted_iota(jnp.int32, (8, D_RG), 0)

        def blk(j, c):
            for u in range(unroll):
                o = pl.multiple_of((j * unroll + u) * 8, 8)
                A, B = _scan_block_fwd(a_s[pl.ds(o, 8), :], b_s[pl.ds(o, 8), :], rowi)
                h = B + A * c
                h_ref[pl.ds(o, 8), :] = h
                c = h[7:8, :]
            return c

        carry[...] = lax.fori_loop(0, tm // (8 * unroll), blk, carry[...])
        ext[0:8, :] = ext[tm:tm + 8, :]
        g, _ = _gelu_parts(xg_ref[:, D_RG:])
        yy = g * h_ref[...]
        y_ref[...] = (yy * _rms(yy) * g_ref[...]).astype(BF16)

    vec = lambda n: pl.BlockSpec((1, n), lambda i: (0, 0))
    return pl.pallas_call(
        body, grid=(T // tm,),
        in_specs=[pl.BlockSpec((tm, 2 * D_RG), lambda i: (i, 0)),
                  pl.BlockSpec((CONV_W, D_RG), lambda i: (0, 0)), vec(D_RG),
                  pl.BlockSpec((D_RG, 2 * D_RG), lambda i: (0, 0)), vec(2 * D_RG), vec(D_RG), vec(D_RG)],
        out_specs=[pl.BlockSpec((tm, D_RG), lambda i: (i, 0)), pl.BlockSpec((tm, D_RG), lambda i: (i, 0))],
        out_shape=[jax.ShapeDtypeStruct((T, D_RG), BF16), jax.ShapeDtypeStruct((T, D_RG), F32)],
        scratch_shapes=[pltpu.VMEM((tm + 8, D_RG), F32), pltpu.VMEM((tm, D_RG), F32),
                        pltpu.VMEM((tm, D_RG), F32), pltpu.VMEM((1, D_RG), F32)],
        name="rg_fwd", compiler_params=_params("arbitrary"),
    )(p, cw, cb, wg, bg, lam, rg_g)


def _tri(lower):
    r = lax.broadcasted_iota(jnp.int32, (CHUNK, CHUNK), 0)
    c = lax.broadcasted_iota(jnp.int32, (CHUNK, CHUNK), 1)
    return ((c <= r) if lower else (c >= r)).astype(F32)


def _hg_gates(hq, hf, lbraw_ref, valid):
    lb = _sigmoid(lbraw_ref[0:1, :] - lbraw_ref[1:2, :])
    sq = _sigmoid(hq)
    q = hq * sq
    sf = _sigmoid(hf)
    f = lb + (1.0 - lb) * sf
    lf = jnp.where(valid, jnp.log(f), 0.0)
    b = jnp.dot(_tri(True), lf, precision=HIGHEST, preferred_element_type=F32)
    return lb, sq, q, sf, f, b


def _hg_head(qh, kh, bh):
    blk = lax.broadcasted_iota(jnp.int32, (CHUNK, 1), 0) // SUB
    b_last = bh[CHUNK - 1:CHUNK, :]
    refs = [bh[SUB * s:SUB * s + 1, :] for s in range(N_SUB)]
    r_sel = refs[N_SUB - 1]
    for s in range(N_SUB - 2, -1, -1):
        r_sel = jnp.where(blk == s, refs[s], r_sel)
    eb = jnp.exp(bh)
    eq = jnp.exp(bh - r_sel)
    ekh = jnp.exp(b_last - bh)
    ek = [jnp.exp(jnp.minimum(refs[s] - bh, EXP_CLAMP)) for s in range(N_SUB)]
    qe = qh * eq
    q_hat = jnp.concatenate([jnp.where(blk == s, qe, 0.0) for s in range(N_SUB)], axis=1)
    k_til = jnp.concatenate([kh * ek[s] for s in range(N_SUB)], axis=1)
    return blk, b_last, eb, eq, ekh, ek, q_hat, k_til


def _causal():
    r = lax.broadcasted_iota(jnp.int32, (CHUNK, CHUNK), 0)
    c = lax.broadcasted_iota(jnp.int32, (CHUNK, CHUNK), 1)
    return r >= c


def _chunks_per_step(n_chunks):
    for c in (5, 4, 3, 2):
        if n_chunks % c == 0:
            return c
    return 1


def _hg_fwd(p, lbraw, hg_g):
    T = p.shape[0]
    n_chunks = T // CHUNK
    cps = _chunks_per_step(n_chunks)
    rows = cps * CHUNK

    def body(hq_ref, hf_ref, hi_ref, hg_ref, lb_ref, g_ref, y_ref, o_ref, st_all_ref, st):
        i = pl.program_id(0)

        @pl.when(i == 0)
        def _():
            st[...] = jnp.zeros_like(st)

        def chunk(j, carry):
            rs = pl.ds(pl.multiple_of(j * CHUNK, CHUNK), CHUNK)
            chunk_body(i * cps + j, hq_ref.at[rs, :], hf_ref.at[rs, :], hi_ref.at[rs, :], hg_ref.at[rs, :], lb_ref,
                       g_ref, y_ref.at[rs, :], o_ref.at[rs, :], st_all_ref.at[pl.ds(j, 1)], st)
            return carry

        lax.fori_loop(0, cps, chunk, 0, unroll=True)

    def chunk_body(n, hq_ref, hf_ref, hi_ref, hg_ref, lb_ref, g_ref, y_ref, o_ref, st_all_ref, st):
        valid = (n * CHUNK + lax.broadcasted_iota(jnp.int32, (CHUNK, 1), 0)) >= PAD
        hq, hf, v, hg = hq_ref[...], hf_ref[...], hi_ref[...], hg_ref[...]
        lb, sq, q, sf, f, b = _hg_gates(hq, hf, lb_ref, valid)
        k = 1.0 - f
        st_all_ref[0] = st[...]
        causal = _causal()
        v_t = v.T.astype(BF16)
        heads = [slice(h * HG_HEAD_DIM, (h + 1) * HG_HEAD_DIM) for h in range(HG_HEADS)]
        fac = []
        for sl in heads:
            qh, kh, bh = q[:, sl], k[:, sl], b[:, sl]
            _, b_last, eb, _, ekh, _, q_hat, k_til = _hg_head(qh, kh, bh)
            fac.append((jnp.exp(b_last), (qh * eb).astype(BF16), q_hat.astype(BF16), k_til.astype(BF16),
                        (kh * ekh).astype(BF16), v[:, sl].astype(BF16)))
        raw = []
        for sl, (_, q_til, q_hat, k_til, k_hat, _) in zip(heads, fac):
            st_h = st[sl, :]
            raw.append((_dot_nt(q_til, st_h.astype(BF16)), _dot_nt(q_hat, k_til), _dot(v_t[sl, :], k_hat), st_h))
        for sl, (e_last, _, _, _, _, vb), (inter, att, upd, st_h) in zip(heads, fac, raw):
            o = inter + _dot(jnp.where(causal, att, 0.0).astype(BF16), vb)
            st[sl, :] = st_h * e_last + upd
            o_ref[:, sl] = o
            hgh = hg[:, sl]
            y_ref[:, sl] = (o * _rms(o) * g_ref[...] * (hgh * _sigmoid(hgh))).astype(BF16)

    col = lambda j: pl.BlockSpec((rows, D_HG), lambda n: (n, j))
    return pl.pallas_call(
        body, grid=(n_chunks // cps,),
        in_specs=[col(2), col(3), col(4), col(5),
                  pl.BlockSpec((2, D_HG), lambda n: (0, 0)), pl.BlockSpec((1, HG_HEAD_DIM), lambda n: (0, 0))],
        out_specs=[pl.BlockSpec((rows, D_HG), lambda n: (n, 0)), pl.BlockSpec((rows, D_HG), lambda n: (n, 0)),
                   pl.BlockSpec((cps, D_HG, HG_HEAD_DIM), lambda n: (n, 0, 0))],
        out_shape=[jax.ShapeDtypeStruct((T, D_HG), BF16), jax.ShapeDtypeStruct((T, D_HG), F32),
                   jax.ShapeDtypeStruct((n_chunks, D_HG, HG_HEAD_DIM), F32)],
        scratch_shapes=[pltpu.VMEM((D_HG, HG_HEAD_DIM), F32)],
        name="hg_fwd", compiler_params=_params("arbitrary"),
    )(p, p, p, p, lbraw, hg_g)


def _ffn_fwd(h0, y_rg, y_hg, w_out, g2, w_gu, w_down, gf, target):
    T = h0.shape[0]
    tm = _row_tile(T, 320)
    n_steps = T // tm

    def body(h_ref, yr_ref, yh_ref, wo_ref, g2_ref, wgu_ref, wd_ref, gf_ref, t_hbm,
             h1_ref, v_ref, y_ref, gu_ref, act_ref, dh2_ref, dh2b_ref, loss_ref, gg_ref, tbuf, sems):
        i = pl.program_id(0)
        slot = _fetch_window(t_hbm, tbuf, sems, i, n_steps, tm)

        @pl.when(i == 0)
        def _():
            loss_ref[...] = jnp.zeros_like(loss_ref)
            gg_ref[...] = jnp.zeros_like(gg_ref)
            tbuf[0, 0:HEAD, :] = jnp.zeros((HEAD, D_MODEL), F32)

        y_ref[:, :D_RG] = yr_ref[...]
        y_ref[:, D_RG:] = yh_ref[...]
        h1 = h_ref[...] + _dot(y_ref[...], wo_ref[...])
        h1_ref[...] = h1
        v = (h1 * _rms(h1) * g2_ref[...]).astype(BF16)
        v_ref[...] = v

        gu = _dot(v, wgu_ref[...])
        gu_ref[...] = gu.astype(BF16)
        g = gu[:, :D_FF]
        act = (g * _sigmoid(g) * gu[:, D_FF:]).astype(BF16)
        act_ref[...] = act

        h2 = h1 + _dot(act, wd_ref[...])
        r = _rms(h2)
        n = h2 * r
        gf_ = gf_ref[...]
        row = i * tm + lax.broadcasted_iota(jnp.int32, (tm, 1), 0)
        err = jnp.where(row >= HEAD, n * gf_ - tbuf[slot], 0.0)
        loss_ref[...] += 0.5 * jnp.sum(jnp.mean(err * err, axis=-1, keepdims=True), axis=0, keepdims=True)
        dy = err * (1.0 / D_MODEL)
        gg_ref[...] += jnp.sum(dy * n, axis=0, keepdims=True)
        dh2 = _rms_bwd(dy * gf_, n, r)
        dh2_ref[...] = dh2
        dh2b_ref[...] = dh2.astype(BF16)

    row_spec = lambda n: pl.BlockSpec((tm, n), lambda i: (i, 0))
    vec = pl.BlockSpec((1, D_MODEL), lambda i: (0, 0))
    return pl.pallas_call(
        body, grid=(n_steps,),
        in_specs=[row_spec(D_MODEL), row_spec(D_RG), row_spec(D_HG), _resident((D_MODEL, D_MODEL)), vec,
                  _resident((D_MODEL, 2 * D_FF)), _resident((D_FF, D_MODEL)), vec,
                  pl.BlockSpec(memory_space=pl.ANY)],
        out_specs=[row_spec(D_MODEL), row_spec(D_MODEL), row_spec(D_MODEL), row_spec(2 * D_FF), row_spec(D_FF),
                   row_spec(D_MODEL), row_spec(D_MODEL), pl.BlockSpec((1, 1), lambda i: (0, 0)), vec],
        out_shape=[jax.ShapeDtypeStruct((T, D_MODEL), F32), jax.ShapeDtypeStruct((T, D_MODEL), BF16),
                   jax.ShapeDtypeStruct((T, D_MODEL), BF16), jax.ShapeDtypeStruct((T, 2 * D_FF), BF16),
                   jax.ShapeDtypeStruct((T, D_FF), BF16), jax.ShapeDtypeStruct((T, D_MODEL), F32),
                   jax.ShapeDtypeStruct((T, D_MODEL), BF16), jax.ShapeDtypeStruct((1, 1), F32),
                   jax.ShapeDtypeStruct((1, D_MODEL), F32)],
        scratch_shapes=[pltpu.VMEM((2, tm, D_MODEL), F32), pltpu.SemaphoreType.DMA((2,))],
        name="ffn_fwd", compiler_params=_params("arbitrary"),
    )(h0, y_rg, y_hg, w_out, g2, w_gu, w_down, gf, target)


def _resident(shape):
    return pl.BlockSpec(shape, lambda i: (0,) * len(shape), pipeline_mode=pl.Buffered(1))


def _ffn_bwd(dh2b, gu, w_down, w_gu, h1, g2, dh2, w_out):
    T = h1.shape[0]
    tm = _row_tile(T, 320)

    def body(d_ref, gu_ref, wd_ref, wgu_ref, h_ref, g_ref, d2_ref, wo_ref, dgu_ref, dh1_ref, dh1b_ref, dy_ref, gg_ref):
        i = pl.program_id(0)

        @pl.when(i == 0)
        def _():
            gg_ref[...] = jnp.zeros_like(gg_ref)

        dact = _dot_nt(d_ref[...], wd_ref[...]).astype(BF16)
        g = gu_ref[:, :D_FF]
        u = gu_ref[:, D_FF:]
        s = _sigmoid(g)
        dgu_ref[:, :D_FF] = dact * u * (s * (1.0 + g * (1.0 - s)))
        dgu_ref[:, D_FF:] = dact * (g * s)

        dv = _dot_nt(dgu_ref[...], wgu_ref[...])
        h1_ = h_ref[...]
        r = _rms(h1_)
        n = h1_ * r
        gg_ref[...] += jnp.sum(dv * n, axis=0, keepdims=True)
        dh1 = d2_ref[...] + _rms_bwd(dv * g_ref[...], n, r)
        dh1_ref[...] = dh1
        db = dh1.astype(BF16)
        dh1b_ref[...] = db
        dy_ref[...] = _dot_nt(db, wo_ref[...])

    row = lambda n: pl.BlockSpec((tm, n), lambda i: (i, 0))
    return pl.pallas_call(
        body, grid=(T // tm,),
        in_specs=[row(D_MODEL), row(2 * D_FF), _resident((D_FF, D_MODEL)), _resident((D_MODEL, 2 * D_FF)),
                  row(D_MODEL), pl.BlockSpec((1, D_MODEL), lambda i: (0, 0)), row(D_MODEL),
                  _resident((D_MODEL, D_MODEL))],
        out_specs=[row(2 * D_FF), row(D_MODEL), row(D_MODEL), row(D_MODEL),
                   pl.BlockSpec((1, D_MODEL), lambda i: (0, 0))],
        out_shape=[jax.ShapeDtypeStruct((T, 2 * D_FF), BF16), jax.ShapeDtypeStruct((T, D_MODEL), F32),
                   jax.ShapeDtypeStruct((T, D_MODEL), BF16), jax.ShapeDtypeStruct((T, D_MODEL), F32),
                   jax.ShapeDtypeStruct((1, D_MODEL), F32)],
        name="ffn_bwd", compiler_params=_params("arbitrary"),
    )(dh2b, gu, w_down, w_gu, h1, g2, dh2, w_out)


def _rg_bwd(p, hs, dy, dp, cw, cb, wg, bg, lam, rg_g):
    T = p.shape[0]
    tm = _row_tile(T, 832)
    nt = T // tm
    hb = tm // 8
    unroll = _scan_unroll(hb)

    def body(xg_ref, xh_ref, h_ref, hh_ref, dy_ref, dp_in_ref, cw_ref, cb_ref, w_ref, bg_ref, lam_ref, g_ref,
             dp_ref, gcw_ref, gcb_ref, gw_ref, gbg_ref, glam_ref, gg_ref,
             ext, dext, a_s, b_s, d_s, gacc, carry_d, carry_a):
        i = pl.program_id(0)
        t_idx = nt - 1 - i

        @pl.when(i == 0)
        def _():
            dext[tm:tm + 8, :] = jnp.zeros((8, D_RG), F32)
            carry_d[...] = jnp.zeros_like(carry_d)
            carry_a[...] = jnp.zeros_like(carry_a)
            gacc[...] = jnp.zeros_like(gacc)
            for ref in (gcw_ref, gcb_ref, gbg_ref, glam_ref, gg_ref, gw_ref):
                ref[...] = jnp.zeros_like(ref)

        first = t_idx == 0
        ext[0:8, :] = jnp.where(first, 0.0, xh_ref[:, :D_RG])
        ext[8:8 + tm, :] = xg_ref[:, :D_RG]
        xc = _conv(ext, cw_ref, cb_ref, tm)
        lam_ = lam_ref[...]
        r, ig, sp, a, m, inv_m = _rg_gates(xc, w_ref, bg_ref, lam_)
        row = t_idx * tm + lax.broadcasted_iota(jnp.int32, (tm, 1), 0)
        valid = row >= PAD

        gr = xg_ref[:, D_RG:]
        g, dgelu = _gelu_parts(gr)
        h = h_ref[...]
        yy = g * h
        rr = _rms(yy)
        nn = yy * rr
        dy_ = dy_ref[...]
        gg_ref[...] += jnp.sum(dy_ * nn, axis=0, keepdims=True)
        dyy = _rms_bwd(dy_ * g_ref[...], nn, rr)
        dp_ref[:, D_RG:] = (dyy * h * dgelu).astype(BF16)

        a_s[...] = a
        b_s[...] = dyy * g
        rowi = lax.broadcasted_iota(jnp.int32, (8, D_RG), 0)

        def blk(jj, c):
            cd, ca = c
            for u in range(unroll):
                o = pl.multiple_of((hb - 1 - (jj * unroll + u)) * 8, 8)
                a_blk = a_s[pl.ds(o, 8), :]
                a_next = jnp.where(rowi == 7, ca, pltpu.roll(a_blk, 7, axis=0))
                A, B = _scan_block_bwd(a_next, b_s[pl.ds(o, 8), :], rowi)
                d = B + A * cd
                d_s[pl.ds(o, 8), :] = d
                cd, ca = d[0:1, :], a_blk[0:1, :]
            return cd, ca

        cd, ca = lax.fori_loop(0, hb // unroll, blk, (carry_d[...], carry_a[...]))
        carry_d[...] = cd
        carry_a[...] = ca
        delta = d_s[...]

        h_last_prev = jnp.where(first, 0.0, hh_ref[7:8, :])
        row0 = lax.broadcasted_iota(jnp.int32, (tm, 1), 0) == 0
        h_prev = jnp.where(row0, h_last_prev, pltpu.roll(h, 1, axis=0))
        dbx = jnp.where(valid, delta, 0.0)
        da = delta * h_prev
        di = dbx * m * xc
        dm = dbx * ig * xc
        dla = a * (da - dm * a * inv_m)
        dla = jnp.where(valid, dla, 0.0)
        glam_ref[...] += jnp.sum(dla * r, axis=0, keepdims=True) * (LRU_C / (1.0 + jnp.exp(lam_)))
        dr = (-LRU_C) * sp * dla
        dpre = jnp.concatenate([dr * r * (1.0 - r), di * ig * (1.0 - ig)], axis=1)
        gbg_ref[...] += jnp.sum(dpre, axis=0, keepdims=True)
        dpre_b = dpre.astype(BF16)
        gacc[...] += _dot_tn(xc.astype(BF16), dpre_b)
        dxc = dbx * m * ig + _dot_nt(dpre_b, w_ref[...])
        gcb_ref[...] += jnp.sum(dxc, axis=0, keepdims=True)
        for j in range(CONV_W):
            gcw_ref[j:j + 1, :] += jnp.sum(dxc * ext[8 - 3 + j:8 - 3 + j + tm, :], axis=0, keepdims=True)
        dext[0:tm, :] = dxc
        dxr = cw_ref[0:1, :] * dext[3:3 + tm, :]
        for j in range(1, CONV_W):
            dxr = dxr + cw_ref[j:j + 1, :] * dext[3 - j:3 - j + tm, :]
        dp_ref[:, :D_RG] = dxr.astype(BF16)
        dext[tm:tm + 8, :] = dext[0:8, :]

        @pl.when(i == nt - 1)
        def _():
            fold = _head_fold()
            mask = _head_mask()
            for k in range(2):
                blockdiag = jnp.where(mask, gacc[:, k * D_RG:(k + 1) * D_RG], 0.0)
                gw_ref[k * D_RG:(k + 1) * D_RG, :] = jnp.dot(blockdiag, fold, precision=HIGHEST,
                                                             preferred_element_type=F32)

    vec = lambda n: pl.BlockSpec((1, n), lambda i: (0, 0))
    rev = lambda n: pl.BlockSpec((tm, n), lambda i: (nt - 1 - i, 0))
    halo = lambda n: pl.BlockSpec((8, n), lambda i: (jnp.maximum((nt - 1 - i) * hb - 1, 0), 0))
    return pl.pallas_call(
        body, grid=(nt,),
        in_specs=[rev(2 * D_RG), halo(2 * D_RG), rev(D_RG), halo(D_RG), rev(D_RG), ANY,
                  pl.BlockSpec((CONV_W, D_RG), lambda i: (0, 0)), vec(D_RG),
                  pl.BlockSpec((D_RG, 2 * D_RG), lambda i: (0, 0)), vec(2 * D_RG), vec(D_RG), vec(D_RG)],
        out_specs=[rev(2 * D_RG), pl.BlockSpec((CONV_W, D_RG), lambda i: (0, 0)), vec(D_RG),
                   pl.BlockSpec((2 * D_RG, RG_HEAD_DIM), lambda i: (0, 0)), vec(2 * D_RG), vec(D_RG), vec(D_RG)],
        input_output_aliases={5: 0},
        out_shape=[jax.ShapeDtypeStruct((T, D_IN), BF16), jax.ShapeDtypeStruct((CONV_W, D_RG), F32),
                   jax.ShapeDtypeStruct((1, D_RG), F32), jax.ShapeDtypeStruct((2 * D_RG, RG_HEAD_DIM), F32),
                   jax.ShapeDtypeStruct((1, 2 * D_RG), F32), jax.ShapeDtypeStruct((1, D_RG), F32),
                   jax.ShapeDtypeStruct((1, D_RG), F32)],
        scratch_shapes=[pltpu.VMEM((tm + 8, D_RG), F32), pltpu.VMEM((tm + 8, D_RG), F32),
                        pltpu.VMEM((tm, D_RG), F32), pltpu.VMEM((tm, D_RG), F32), pltpu.VMEM((tm, D_RG), F32),
                        pltpu.VMEM((D_RG, 2 * D_RG), F32), pltpu.VMEM((1, D_RG), F32), pltpu.VMEM((1, D_RG), F32)],
        name="rg_bwd", compiler_params=_params("arbitrary"),
    )(p, p, hs, hs, dy, dp, cw, cb, wg, bg, lam, rg_g)


def _hg_bwd(p, o_all, st_all, dy, lbraw, hg_g):
    T = p.shape[0]
    n_chunks = T // CHUNK
    cps = _chunks_per_step(n_chunks)
    rows = cps * CHUNK
    n_steps = n_chunks // cps

    def body(hq_ref, hf_ref, hi_ref, hg_ref, o_ref, st_ref, dy_ref, lb_ref, g_ref,
             dp_ref, glb_ref, gg_ref, dst):
        i = pl.program_id(0)

        @pl.when(i == 0)
        def _():
            dst[...] = jnp.zeros_like(dst)
            glb_ref[...] = jnp.zeros_like(glb_ref)
            gg_ref[...] = jnp.zeros_like(gg_ref)

        dp_ref[:, :2 * D_RG] = jnp.zeros((rows, 2 * D_RG), BF16)

        def chunk(jj, carry):
            j = cps - 1 - jj
            rs = pl.ds(pl.multiple_of(j * CHUNK, CHUNK), CHUNK)
            chunk_body((n_steps - 1 - i) * cps + j, hq_ref.at[rs, :], hf_ref.at[rs, :], hi_ref.at[rs, :],
                       hg_ref.at[rs, :], o_ref.at[rs, :], st_ref.at[pl.ds(j, 1)], dy_ref.at[rs, :], lb_ref, g_ref,
                       dp_ref.at[rs, pl.ds(2 * D_RG, 4 * D_HG)], glb_ref, gg_ref, dst)
            return carry

        lax.fori_loop(0, cps, chunk, 0, unroll=True)

    def chunk_body(n, hq_ref, hf_ref, hi_ref, hg_ref, o_ref, st_ref, dy_ref, lb_ref, g_ref,
                   dp_ref, glb_ref, gg_ref, dst):
        valid = (n * CHUNK + lax.broadcasted_iota(jnp.int32, (CHUNK, 1), 0)) >= PAD
        hq, hf, v, hg = hq_ref[...], hf_ref[...], hi_ref[...], hg_ref[...]
        lb, sq, q, sf, f, b = _hg_gates(hq, hf, lb_ref, valid)
        k = 1.0 - f
        causal = _causal()
        r_i = lax.broadcasted_iota(jnp.int32, (CHUNK, CHUNK), 0)
        c_i = lax.broadcasted_iota(jnp.int32, (CHUNK, CHUNK), 1)
        causal_t = r_i <= c_i
        is_last = lax.broadcasted_iota(jnp.int32, (CHUNK, 1), 0) == CHUNK - 1
        g_ = g_ref[...]
        db_parts, dq_parts, dk_parts = [], [], []
        gg = jnp.zeros((1, HG_HEAD_DIM), F32)
        heads = [slice(h * HG_HEAD_DIM, (h + 1) * HG_HEAD_DIM) for h in range(HG_HEADS)]

        do_parts = []
        for h, sl in enumerate(heads):
            o = o_ref[:, sl]
            ro = _rms(o)
            no = o * ro
            hgh = hg[:, sl]
            sg = _sigmoid(hgh)
            dyh = dy_ref[:, sl]
            dp_ref[:, 3 * D_HG + h * HG_HEAD_DIM:3 * D_HG + (h + 1) * HG_HEAD_DIM] = (
                dyh * no * g_ * sg * (1.0 + hgh * (1.0 - sg))).astype(BF16)
            dng = dyh * hgh * sg
            gg = gg + jnp.sum(dng * no, axis=0, keepdims=True)
            do_parts.append(_rms_bwd(dng * g_, no, ro))
        do_t = jnp.concatenate(do_parts, axis=1).T.astype(BF16)

        fac = []
        for sl, do in zip(heads, do_parts):
            qh, kh, bh = q[:, sl], k[:, sl], b[:, sl]
            blk, b_last, eb, eq, ekh, ek, q_hat, k_til = _hg_head(qh, kh, bh)
            fac.append(dict(qh=qh, kh=kh, blk=blk, e_last=jnp.exp(b_last), eb=eb, eq=eq, ekh=ekh, ek=ek,
                            q_til=qh * eb, k_hat=kh * ekh, qhb=q_hat.astype(BF16), ktb=k_til.astype(BF16),
                            vb=v[:, sl].astype(BF16), dob=do.astype(BF16)))

        first = []
        for sl, t in zip(heads, fac):
            st_h = st_ref[0, sl, :]
            dst_h = dst[sl, :]
            dstb = dst_h.astype(BF16)
            first.append(dict(
                att_t=_dot_nt(t["ktb"], t["qhb"]), datt=_dot_nt(t["dob"], t["vb"]),
                datt_t=_dot_nt(t["vb"], t["dob"]), dk_hat=_dot(t["vb"], dstb),
                dv=_dot_nt(t["k_hat"].astype(BF16), dstb), dq_til=_dot(t["dob"], st_h.astype(BF16)),
                state=t["e_last"] * jnp.sum(dst_h * st_h, axis=0, keepdims=True)))
            dst[sl, :] = dst_h * t["e_last"] + _dot(do_t[sl, :], t["q_til"].astype(BF16))

        for h, (t, m) in enumerate(zip(fac, first)):
            qh, kh, blk, eb, eq, ekh, ek = t["qh"], t["kh"], t["blk"], t["eb"], t["eq"], t["ekh"], t["ek"]
            q_til, k_hat, qhb, ktb, dob = t["q_til"], t["k_hat"], t["qhb"], t["ktb"], t["dob"]
            dk_hat, dq_til = m["dk_hat"], m["dq_til"]
            dv = m["dv"] + _dot(jnp.where(causal_t, m["att_t"], 0.0).astype(BF16), dob)
            dq_hat = _dot(jnp.where(causal, m["datt"], 0.0).astype(BF16), ktb)
            dk_til = _dot(jnp.where(causal_t, m["datt_t"], 0.0).astype(BF16), qhb)
            db_last = jnp.sum(dk_hat * k_hat, axis=0, keepdims=True) + m["state"]
            dq_sel = dq_hat[:, (N_SUB - 1) * HG_HEAD_DIM:]
            for s in range(N_SUB - 2, -1, -1):
                dq_sel = jnp.where(blk == s, dq_hat[:, s * HG_HEAD_DIM:(s + 1) * HG_HEAD_DIM], dq_sel)
            dq_a = dq_sel * eq
            dk_a = dk_til[:, :HG_HEAD_DIM] * ek[0]
            for s in range(1, N_SUB):
                dk_a = dk_a + dk_til[:, s * HG_HEAD_DIM:(s + 1) * HG_HEAD_DIM] * ek[s]
            db_att = qhb.astype(F32) * dq_hat - ktb.astype(F32) * dk_til
            db = dq_til * q_til - dk_hat * k_hat
            for s in range(N_SUB):
                db = db + db_att[:, s * HG_HEAD_DIM:(s + 1) * HG_HEAD_DIM]
            db_parts.append(jnp.where(is_last, db + db_last, db))
            dq_parts.append(dq_til * eb + dq_a)
            dk_parts.append(dk_hat * ekh + dk_a)
            dp_ref[:, 2 * D_HG + h * HG_HEAD_DIM:2 * D_HG + (h + 1) * HG_HEAD_DIM] = dv.astype(BF16)

        gg_ref[...] += gg
        db = jnp.concatenate(db_parts, axis=1)
        dq = jnp.concatenate(dq_parts, axis=1)
        dk = jnp.concatenate(dk_parts, axis=1)
        dlf = jnp.where(valid, jnp.dot(_tri(False), db, precision=HIGHEST, preferred_element_type=F32), 0.0)
        dp_ref[:, :D_HG] = (dq * sq * (1.0 + hq * (1.0 - sq))).astype(BF16)
        df = dlf / f - dk
        dlb = jnp.sum(df * (1.0 - sf), axis=0, keepdims=True) * lb * (1.0 - lb)
        glb_ref[0:1, :] += dlb
        glb_ref[1:2, :] += -dlb
        dp_ref[:, D_HG:2 * D_HG] = (df * (1.0 - lb) * sf * (1.0 - sf)).astype(BF16)

    rev = lambda j: pl.BlockSpec((rows, D_HG), lambda i: (n_steps - 1 - i, j))
    return pl.pallas_call(
        body, grid=(n_steps,),
        in_specs=[rev(2), rev(3), rev(4), rev(5), rev(0),
                  pl.BlockSpec((cps, D_HG, HG_HEAD_DIM), lambda i: (n_steps - 1 - i, 0, 0)), rev(1),
                  pl.BlockSpec((2, D_HG), lambda i: (0, 0)), pl.BlockSpec((1, HG_HEAD_DIM), lambda i: (0, 0))],
        out_specs=[pl.BlockSpec((rows, D_IN), lambda i: (n_steps - 1 - i, 0)),
                   pl.BlockSpec((2, D_HG), lambda i: (0, 0)), pl.BlockSpec((1, HG_HEAD_DIM), lambda i: (0, 0))],
        out_shape=[jax.ShapeDtypeStruct((T, D_IN), BF16), jax.ShapeDtypeStruct((2, D_HG), F32),
                   jax.ShapeDtypeStruct((1, HG_HEAD_DIM), F32)],
        scratch_shapes=[pltpu.VMEM((D_HG, HG_HEAD_DIM), F32)],
        name="hg_bwd", compiler_params=_params("arbitrary"),
    )(p, p, p, p, o_all, st_all, dy, lbraw, hg_g)


def _in_bwd(dp, w_in, h0, g1, dh1):
    T = h0.shape[0]
    tm = _row_tile(T, 416)
    n_steps = T // tm

    def body(dp_ref, w_ref, h_ref, g_ref, d1_ref, gx_hbm, gmeta_ref, gg_ref, buf, sems):
        i = pl.program_id(0)
        first, later = _window_copies(gx_hbm, buf, sems, tm)
        slot = i % 2

        @pl.when(i == 0)
        def _():
            gg_ref[...] = jnp.zeros_like(gg_ref)

        if n_steps > 2:
            @pl.when(i == 2)
            def _():
                first(False).wait()

            @pl.when(i > 2)
            def _():
                later(i - 2, slot, False).wait()

        du = _dot_nt(dp_ref[...], w_ref[...])
        h0_ = h_ref[...]
        r = _rms(h0_)
        n = h0_ * r
        gg_ref[...] += jnp.sum(du * n, axis=0, keepdims=True)
        dh0 = d1_ref[...] + _rms_bwd(du * g_ref[...], n, r)
        buf[slot] = dh0

        @pl.when(i == 0)
        def _():
            gmeta_ref[...] = dh0[PAD:HEAD, :]
            first(False).start()

        if n_steps > 1:
            @pl.when(i > 0)
            def _():
                later(i, slot, False).start()

        @pl.when(i == n_steps - 1)
        def _():
            if n_steps == 1:
                first(False).wait()
            else:
                if n_steps == 2:
                    first(False).wait()
                else:
                    later(i - 1, 1 - slot, False).wait()
                later(i, slot, False).wait()

    row = lambda n: pl.BlockSpec((tm, n), lambda i: (i, 0))
    return pl.pallas_call(
        body, grid=(n_steps,),
        in_specs=[row(D_IN), pl.BlockSpec((D_MODEL, D_IN), lambda i: (0, 0)),
                  row(D_MODEL), pl.BlockSpec((1, D_MODEL), lambda i: (0, 0)), row(D_MODEL)],
        out_specs=[pl.BlockSpec(memory_space=pl.ANY), pl.BlockSpec((N_META, D_MODEL), lambda i: (0, 0)),
                   pl.BlockSpec((1, D_MODEL), lambda i: (0, 0))],
        out_shape=[jax.ShapeDtypeStruct((T - HEAD, D_MODEL), F32), jax.ShapeDtypeStruct((N_META, D_MODEL), F32),
                   jax.ShapeDtypeStruct((1, D_MODEL), F32)],
        scratch_shapes=[pltpu.VMEM((2, tm, D_MODEL), F32), pltpu.SemaphoreType.DMA((2,))],
        name="in_bwd", compiler_params=_params("arbitrary"),
    )(dp, w_in, h0, g1, dh1)


def _col_tile(cols, target):
    best = None
    for t in range(128, min(cols, target) + 1, 128):
        if cols % t == 0:
            best = t
    assert best is not None, cols
    return best


MXU_DIM = 256


def _mxu_tile(cols, target):
    best = None
    for t in range(MXU_DIM, min(cols, target) + 1, MXU_DIM):
        if cols % t == 0:
            best = t
    assert best is not None, cols
    return best


def _weight_grad(a, b, name):
    T, M = a.shape
    N = b.shape[1]
    tm = _col_tile(M, 1408)
    tn = _mxu_tile(N, 768 if tm <= 1024 else 512)

    def body(a_ref, b_ref, o_ref):
        o_ref[...] = _dot_tn(a_ref[...], b_ref[...])

    return pl.pallas_call(
        body, grid=(M // tm, N // tn),
        in_specs=[pl.BlockSpec((T, tm), lambda m, n: (0, m)), pl.BlockSpec((T, tn), lambda m, n: (0, n))],
        out_specs=pl.BlockSpec((tm, tn), lambda m, n: (m, n)),
        out_shape=jax.ShapeDtypeStruct((M, N), F32),
        name=name, compiler_params=_params("parallel", "parallel"),
    )(a, b)


def _local_step(x, meta, target, w_in_own, w_in, w_out, w_gu, w_down, small, chip, on_ffn_grads=None,
                on_mixer_grads=None):
    wg = _gate_weights(small["w_rgate"], small["w_igate"])
    bg = jnp.concatenate([small["b_rgate"], small["b_igate"]], axis=1)

    p, u, h0 = _in_proj_local(x, meta, small["mix_norm_g"], w_in_own, chip)
    p = _in_proj_rest(u, w_in, p, chip)
    y_rg, hs = _rg_fwd(p, small["conv_w"], small["conv_b"], wg, bg, small["lru_lambda"], small["rg_norm_g"])
    y_hg, o_all, st_all = _hg_fwd(p, small["hg_lower_bound"], small["hg_norm_g"])
    h1, v, yb, gu, act, dh2, dh2b, loss, g_final = _ffn_fwd(
        h0, y_rg, y_hg, w_out, small["ffn_norm_g"], w_gu, w_down, small["final_norm_g"], target)

    g_w_down = _weight_grad(act, dh2b, "grad_w_down")
    dgu, dh1, dh1b, dy, g_ffn = _ffn_bwd(dh2b, gu, w_down, w_gu, h1, small["ffn_norm_g"], dh2, w_out)
    ffn_grads = {"w_gate_up": _weight_grad(v, dgu, "grad_w_gate_up"), "w_down": g_w_down,
                 "w_out": _weight_grad(yb, dh1b, "grad_w_out")}
    stages = on_ffn_grads(ffn_grads) if on_ffn_grads is not None else None
    dp, g_lb, g_hgn = _hg_bwd(p, o_all, st_all, dy, small["hg_lower_bound"], small["hg_norm_g"])
    early = late = None
    if stages is not None:
        chip_sums, send = stages
        sums = chip_sums()
        (dp, dy), sums = lax.optimization_barrier(((dp, dy), sums))
        early = send(sums)
    dp, g_cw, g_cb, g_wgate, g_bg, g_lam, g_rgn = _rg_bwd(
        p, hs, dy, dp, small["conv_w"], small["conv_b"], wg, bg, small["lru_lambda"], small["rg_norm_g"])
    mixer_grads = {"w_in": _weight_grad(u, dp, "grad_w_in")}
    if on_mixer_grads is not None:
        chip_sums, send = on_mixer_grads(mixer_grads)
        sums = chip_sums()
        (dp, dh1), sums = lax.optimization_barrier(((dp, dh1), sums))
        late = send(sums)
    grad_x, g_meta, g_mix = _in_bwd(dp, w_in, h0, small["mix_norm_g"], dh1)

    grads = {
        "w_in": mixer_grads["w_in"], "w_out": ffn_grads["w_out"],
        "w_gate_up": ffn_grads["w_gate_up"], "w_down": ffn_grads["w_down"],
        "meta_tokens": g_meta, "mix_norm_g": g_mix, "conv_w": g_cw, "conv_b": g_cb, "w_gates": g_wgate,
        "b_rgate": g_bg[:, :D_RG], "b_igate": g_bg[:, D_RG:], "lru_lambda": g_lam, "rg_norm_g": g_rgn,
        "hg_lower_bound": g_lb, "hg_norm_g": g_hgn, "ffn_norm_g": g_ffn, "final_norm_g": g_final,
    }
    return loss, grad_x, grads, early, late


ANY = pl.BlockSpec(memory_space=pl.ANY)
HALF = D_MODEL // 2

BIG = {"w_in": (D_MODEL, D_IN // N_CHIPS, True), "w_gate_up": (D_MODEL, 2 * D_FF // N_CHIPS, True),
       "w_out": (D_MODEL // N_CHIPS, D_MODEL, False), "w_down": (D_FF // N_CHIPS, D_MODEL, False)}
BIG_NAMES = tuple(BIG)
N_BIG = len(BIG_NAMES)


def _full_shape(name):
    rows, cols, by_col = BIG[name]
    return (rows, cols * N_CHIPS) if by_col else (rows * N_CHIPS, cols)


def _place():
    return lax.axis_index("x"), lax.axis_index("y"), lax.axis_index("c")


def _chip_of(x, y, r):
    fx, fy = (r + 1) >> 1, (r + 1) & 1
    return (1 - x if fx else x), (1 - y if fy else y)


def _half_of(ref, by_col, half):
    start = pl.multiple_of(half * HALF, 128)
    return ref.at[pl.ds(start, HALF), :] if by_col else ref.at[:, pl.ds(start, HALF)]


def _shard_of(ref, name, chip):
    rows, cols, by_col = BIG[name]
    if by_col:
        return ref.at[:, pl.ds(pl.multiple_of(chip * cols, 128), cols)]
    return ref.at[pl.ds(pl.multiple_of(chip * rows, 16), rows), :]


def _shard_half_of(ref, name, chip, half):
    rows, cols, by_col = BIG[name]
    start = pl.multiple_of(half * HALF, 128)
    if by_col:
        return ref.at[pl.ds(start, HALF), pl.ds(pl.multiple_of(chip * cols, 128), cols)]
    return ref.at[pl.ds(pl.multiple_of(chip * rows, 16), rows), pl.ds(start, HALF)]


def _remote(src, dst, send_sems, recv_sems, k, dev):
    return pltpu.make_async_remote_copy(src_ref=src, dst_ref=dst, send_sem=send_sems.at[k], recv_sem=recv_sems.at[k],
                                        device_id=dev, device_id_type=MESH)


def _place_shards(w, small, chip):
    steps = 4
    ns = len(small)
    in_specs, out_specs = [], []
    for name in BIG_NAMES:
        rows, cols, by_col = BIG[name]
        tr = rows // steps
        in_specs.append(pl.BlockSpec((tr, cols), lambda i, s: (i, 0)))
        if by_col:
            out_specs.append(pl.BlockSpec((tr, cols), lambda i, s: (i, s[0])))
        else:
            out_specs.append(pl.BlockSpec((tr, cols), lambda i, s: (s[0] * steps + i, 0)))

    def body(s_ref, *refs):
        ins, small_in = refs[:N_BIG], refs[N_BIG:N_BIG + ns]
        outs, small_out = refs[N_BIG + ns:2 * N_BIG + ns], refs[2 * N_BIG + ns:2 * (N_BIG + ns)]
        send_sems, recv_sems, local_sems = refs[2 * (N_BIG + ns):]
        i = pl.program_id(0)
        x, y, c = _place()
        chip_ = 2 * x + y
        others = [_chip_of(x, y, r) for r in range(3)]

        def block(a, q):
            cols = small[a].shape[1]
            return small_out[a].at[:, pl.ds(pl.multiple_of(q * cols, 128), cols)]

        def local(a):
            return pltpu.make_async_copy(small_in[a], block(a, chip_), local_sems.at[a])

        def remote(a, r):
            qx, qy = others[r]
            return _remote(small_in[a], block(a, chip_), send_sems, recv_sems, 3 * a + r, (qx, qy, c))

        @pl.when(i == 0)
        def _():
            for a in range(ns):
                local(a).start()
                for r in range(3):
                    remote(a, r).start()

        for a in range(N_BIG):
            outs[a][...] = ins[a][...].astype(BF16)

        @pl.when(i == steps - 1)
        def _():
            for a in range(ns):
                for r, (qx, qy) in enumerate(others):
                    landed = block(a, 2 * qx + qy)
                    _remote(landed, landed, send_sems, recv_sems, 3 * a + r, (qx, qy, c)).wait_recv()
                for r in range(3):
                    remote(a, r).wait_send()
                local(a).wait()

    out = pl.pallas_call(
        body,
        grid_spec=pltpu.PrefetchScalarGridSpec(
            num_scalar_prefetch=1, grid=(steps,), in_specs=in_specs + [ANY] * ns, out_specs=out_specs + [ANY] * ns,
            scratch_shapes=[pltpu.SemaphoreType.DMA((3 * ns,)), pltpu.SemaphoreType.DMA((3 * ns,)),
                            pltpu.SemaphoreType.DMA((ns,))]),
        out_shape=([jax.ShapeDtypeStruct(_full_shape(name), BF16) for name in BIG_NAMES]
                   + [jax.ShapeDtypeStruct((s.shape[0], s.shape[1] * N_CHIPS), F32) for s in small]),
        name="place_shards", compiler_params=_params("arbitrary"),
    )(chip, *[w[name] for name in BIG_NAMES], *small)
    return dict(zip(BIG_NAMES, out[:N_BIG])), list(out[N_BIG:])


def _gather_weights(placed, small, names, label, collective_id):
    n, ns = len(names), len(small)
    hbm = pltpu.MemorySpace.HBM
    outs = [jax.new_ref(placed[nm], memory_space=hbm) for nm in names]
    small_in = [jax.new_ref(s, memory_space=hbm) for s in small]
    small_out = [jax.empty_ref(jax.ShapeDtypeStruct((s.shape[0], s.shape[1] * N_CHIPS), F32), memory_space=hbm)
                 for s in small]
    n_sems = 6 * n + 3 * ns

    @pl.kernel(mesh=plsc.ScalarSubcoreMesh(axis_name="seq", num_cores=1), name=label, out_type=(),
               scratch_types=(pltpu.SemaphoreType.DMA((n_sems,)), pltpu.SemaphoreType.DMA((n_sems,)),
                              pltpu.SemaphoreType.DMA((max(ns, 1),))),
               compiler_params=pltpu.CompilerParams(collective_id=collective_id))
    def launch(send_sems, recv_sems, local_sems):
        x, y, c = _place()
        chip = 2 * x + y
        sibling = (x, y, 1 - c)
        others = [_chip_of(x, y, r) for r in range(3)]
        _handshake([(qx, qy, c) for qx, qy in others] + [sibling])

        def small_block(a, q):
            cols = small[a].shape[1]
            return small_out[a].at[:, pl.ds(pl.multiple_of(q * cols, 128), cols)]

        local = [pltpu.make_async_copy(small_in[a], small_block(a, chip), local_sems.at[a]) for a in range(ns)]
        for cp in local:
            cp.start()

        sends = []
        for a, name in enumerate(names):
            mine = _shard_half_of(outs[a], name, chip, c)
            for r, (qx, qy) in enumerate(others):
                sends.append(_remote(mine, mine, send_sems, recv_sems, 6 * a + r, (qx, qy, c)))
        for a in range(ns):
            for r, (qx, qy) in enumerate(others):
                sends.append(_remote(small_in[a], small_block(a, chip), send_sems, recv_sems,
                                     6 * n + 3 * a + r, (qx, qy, c)))
        for cp in sends:
            cp.start()

        forwards = []
        for a, name in enumerate(names):
            for r, (qx, qy) in enumerate(others):
                landed = _shard_half_of(outs[a], name, 2 * qx + qy, c)
                _remote(landed, landed, send_sems, recv_sems, 6 * a + r, (qx, qy, c)).wait_recv()
                fwd = _remote(landed, landed, send_sems, recv_sems, 6 * a + 3 + r, sibling)
                fwd.start()
                forwards.append(fwd)
        for a in range(ns):
            for r, (qx, qy) in enumerate(others):
                landed = small_block(a, 2 * qx + qy)
                _remote(landed, landed, send_sems, recv_sems, 6 * n + 3 * a + r, (qx, qy, c)).wait_recv()
        for a, name in enumerate(names):
            for r, (qx, qy) in enumerate(others):
                landed = _shard_half_of(outs[a], name, 2 * qx + qy, 1 - c)
                _remote(landed, landed, send_sems, recv_sems, 6 * a + 3 + r, sibling).wait_recv()
        for cp in sends + forwards:
            cp.wait_send()
        for cp in local:
            cp.wait()

    launch()
    return {nm: ref[...] for nm, ref in zip(names, outs)}, [ref[...] for ref in small_out]


def _exchange_halves(grads, names, label, collective_id):
    n = len(names)
    sequencer = collective_id is not None

    def body(*refs):
        ins, outs = refs[:n], refs[n:2 * n]
        send_sems, recv_sems = refs[2 * n:]
        x, y, c = _place()
        if sequencer:
            _handshake([(x, y, 1 - c)])
        copies = []
        for a, name in enumerate(names):
            copies.append(_remote(_half_of(ins[a], BIG[name][2], 1 - c), outs[a], send_sems, recv_sems, a,
                                  (x, y, 1 - c)))
        for cp in copies:
            cp.start()
        for cp in copies:
            cp.wait()

    def half_shape(name):
        r, c_ = _full_shape(name)
        return (HALF, c_) if BIG[name][2] else (r, HALF)

    out_type = tuple(jax.ShapeDtypeStruct(half_shape(nm), F32) for nm in names)
    sems = (pltpu.SemaphoreType.DMA((n,)), pltpu.SemaphoreType.DMA((n,)))
    operands = [grads[nm] for nm in names]
    if sequencer:
        got = pl.kernel(
            body, mesh=plsc.ScalarSubcoreMesh(axis_name="seq", num_cores=1), name=label, out_type=out_type,
            scratch_types=sems, compiler_params=pltpu.CompilerParams(collective_id=collective_id),
        )(*operands)
    else:
        got = pl.pallas_call(
            body, in_specs=[ANY] * n, out_specs=[ANY] * n, out_shape=list(out_type), scratch_shapes=list(sems),
            name=label,
        )(*operands)
    return dict(zip(names, got))


def _chip_sum(grads, got, names, core, label):
    n = len(names)
    steps = 4
    g_specs, blks = [], []
    for name in names:
        rows, cols = got[name].shape
        tr = rows // steps
        if BIG[name][2]:
            g_specs.append(pl.BlockSpec((tr, cols), lambda i, s: (s[0] * steps + i, 0)))
        else:
            g_specs.append(pl.BlockSpec((tr, HALF), lambda i, s: (i, s[0])))
        blks.append(pl.BlockSpec((tr, cols), lambda i, s: (i, 0)))

    def body(s_ref, *refs):
        for a in range(n):
            t = refs[a][...] + refs[n + a][...]
            refs[2 * n + a][...] = t
            refs[3 * n + a][...] = t.astype(BF16)

    out = pl.pallas_call(
        body,
        grid_spec=pltpu.PrefetchScalarGridSpec(num_scalar_prefetch=1, grid=(steps,), in_specs=g_specs + blks,
                                               out_specs=blks + blks),
        out_shape=([jax.ShapeDtypeStruct(got[nm].shape, F32) for nm in names]
                   + [jax.ShapeDtypeStruct(got[nm].shape, BF16) for nm in names]),
        name=label, compiler_params=_params("parallel"),
    )(core, *[grads[nm] for nm in names], *[got[nm] for nm in names])
    return {nm: (out[a], out[n + a]) for a, nm in enumerate(names)}


def _piece_shape(name):
    rows, cols, by_col = BIG[name]
    return (HALF, cols) if by_col else (rows, HALF)


def _handshake(peers):
    barrier = pltpu.get_barrier_semaphore()
    for peer in peers:
        pl.semaphore_signal(barrier, inc=1, device_id=peer, device_id_type=MESH)
    pl.semaphore_wait(barrier, len(peers))


def _send_chip_sums(sums, names, label, collective_id):
    n = len(names)

    def body(*refs):
        ins, outs = refs[:n], refs[n:2 * n]
        send_sems, recv_sems = refs[2 * n:]
        x, y, c = _place()
        others = [_chip_of(x, y, r) for r in range(3)]
        _handshake([(qx, qy, c) for qx, qy in others])
        copies = []
        for a, name in enumerate(names):
            for r, (qx, qy) in enumerate(others):
                copies.append(_remote(_shard_of(ins[a], name, 2 * qx + qy), outs[a].at[r], send_sems, recv_sems,
                                      3 * a + r, (qx, qy, c)))
        for cp in copies:
            cp.start()
        for cp in copies:
            cp.wait()

    return pl.kernel(
        body, mesh=plsc.ScalarSubcoreMesh(axis_name="seq", num_cores=1), name=label,
        out_type=tuple(jax.ShapeDtypeStruct((3,) + _piece_shape(nm), BF16) for nm in names),
        scratch_types=(pltpu.SemaphoreType.DMA((3 * n,)), pltpu.SemaphoreType.DMA((3 * n,))),
        compiler_params=pltpu.CompilerParams(collective_id=collective_id),
    )(*[sums[nm] for nm in names])


def _total(parts, chip_core):
    steps = 2
    in_specs, out_specs, operands = [], [], []
    for name in BIG_NAMES:
        by_col = BIG[name][2]
        pr, pc = _piece_shape(name)
        tr = pr // steps
        if by_col:
            in_specs.append(pl.BlockSpec((tr, pc), lambda i, s: (i, s[0])))
            out_specs.append(pl.BlockSpec((tr, pc), lambda i, s: (s[1] * steps + i, 0)))
        else:
            in_specs.append(pl.BlockSpec((tr, pc), lambda i, s: (s[0] * steps + i, 0)))
            out_specs.append(pl.BlockSpec((tr, pc), lambda i, s: (i, s[1])))
        for r in range(3):
            in_specs.append(pl.BlockSpec((None, tr, pc), lambda i, s, r=r: (r, i, 0)))
        own, got = parts[name]
        operands += [own, got, got, got]

    def body(s_ref, *refs):
        for a in range(N_BIG):
            o_ref, a_ref, b_ref, c_ref = refs[4 * a:4 * a + 4]
            refs[4 * N_BIG + a][...] = (((o_ref[...] + a_ref[...].astype(F32)) + b_ref[...].astype(F32))
                                        + c_ref[...].astype(F32))

    totals = pl.pallas_call(
        body,
        grid_spec=pltpu.PrefetchScalarGridSpec(num_scalar_prefetch=1, grid=(steps,), in_specs=in_specs,
                                               out_specs=out_specs),
        out_shape=[jax.ShapeDtypeStruct(BIG[name][:2], F32) for name in BIG_NAMES],
        name="totals", compiler_params=_params("parallel"),
    )(chip_core, *operands)
    return dict(zip(BIG_NAMES, totals))


VEC_ROWS = 32
VEC_ROW = {"mix_norm_g": 0, "conv_b": 1, "b_rgate": 2, "b_igate": 3, "lru_lambda": 4, "rg_norm_g": 5,
           "hg_lower_bound": 6, "hg_norm_g": 8, "ffn_norm_g": 9, "final_norm_g": 10, "loss": 11,
           "conv_w": 12, "meta_tokens": 16}
N_DEV = 8


def _all_reduce_small(pieces, gates, totals):
    names = list(pieces)
    n_small = 10
    hv, hg = VEC_ROWS // 2, gates.shape[0] // 2

    def body(*refs):
        ins = refs[:len(names)]
        g_ref = refs[len(names)]
        vec_ref, gsum_ref = refs[len(names) + 1 + N_BIG:len(names) + 3 + N_BIG]
        big = refs[len(names) + 3 + N_BIG:len(names) + 3 + 2 * N_BIG]
        (mine_v, sib_v, sib_g, chip_v, chip_g, got_v, got_g, send_sems, recv_sems) = refs[len(names) + 3 + 2 * N_BIG:]
        x, y, c = _place()
        chip = 2 * x + y
        sibling = (x, y, 1 - c)
        share = []
        for a, name in enumerate(BIG_NAMES):
            half = _half_of(big[a], BIG[name][2], c)
            share.append(_remote(half, half, send_sems, recv_sems, n_small + a, sibling))
        for cp in share:
            cp.start()
        mine_v[...] = jnp.zeros_like(mine_v)
        for name, ref in zip(names, ins):
            nr, w = ref.shape
            mine_v[VEC_ROW[name]:VEC_ROW[name] + nr, 0:w] = ref[...]

        swap = [_remote(mine_v, sib_v, send_sems, recv_sems, 0, sibling),
                _remote(g_ref, sib_g, send_sems, recv_sems, 1, sibling)]
        for cp in swap:
            cp.start()
        for cp in swap:
            cp.wait()
        chip_v[...] = mine_v[...] + sib_v[...]
        chip_g[...] = g_ref[...] + sib_g[...]

        rows_v = pl.ds(pl.multiple_of(c * hv, 8), hv)
        rows_g = pl.ds(pl.multiple_of(c * hg, 8), hg)
        got_v[chip] = chip_v[rows_v, :]
        got_g[chip] = chip_g[rows_g, :]
        sends = []
        for r in range(3):
            qx, qy = _chip_of(x, y, r)
            sends.append(_remote(chip_v.at[rows_v, :], got_v.at[chip], send_sems, recv_sems, 2 + r, (qx, qy, c)))
            sends.append(_remote(chip_g.at[rows_g, :], got_g.at[chip], send_sems, recv_sems, 5 + r, (qx, qy, c)))
        for cp in sends:
            cp.start()
        for cp in sends:
            cp.wait()
        vec_ref[rows_v, :] = ((got_v[0] + got_v[1]) + got_v[2]) + got_v[3]
        gsum_ref[rows_g, :] = ((got_g[0] + got_g[1]) + got_g[2]) + got_g[3]

        back = [_remote(vec_ref.at[rows_v, :], vec_ref.at[rows_v, :], send_sems, recv_sems, 8, sibling),
                _remote(gsum_ref.at[rows_g, :], gsum_ref.at[rows_g, :], send_sems, recv_sems, 9, sibling)]
        for cp in back:
            cp.start()
        theirs_v = vec_ref.at[pl.ds(pl.multiple_of((1 - c) * hv, 8), hv), :]
        theirs_g = gsum_ref.at[pl.ds(pl.multiple_of((1 - c) * hg, 8), hg), :]
        _remote(theirs_v, theirs_v, send_sems, recv_sems, 8, sibling).wait_recv()
        _remote(theirs_g, theirs_g, send_sems, recv_sems, 9, sibling).wait_recv()
        for cp in back:
            cp.wait_send()
        for a, name in enumerate(BIG_NAMES):
            theirs = _half_of(big[a], BIG[name][2], 1 - c)
            _remote(theirs, theirs, send_sems, recv_sems, n_small + a, sibling).wait_recv()
        for cp in share:
            cp.wait_send()

    vmem = pl.BlockSpec(memory_space=pltpu.VMEM)
    n_sems = n_small + N_BIG
    out = pl.pallas_call(
        body, in_specs=[vmem] * (len(names) + 1) + [ANY] * N_BIG, out_specs=[vmem, vmem] + [ANY] * N_BIG,
        out_shape=([jax.ShapeDtypeStruct((VEC_ROWS, D_MODEL), F32), jax.ShapeDtypeStruct(gates.shape, F32)]
                   + [jax.ShapeDtypeStruct(BIG[n][:2], F32) for n in BIG_NAMES]),
        input_output_aliases={len(names) + 1 + a: 2 + a for a in range(N_BIG)},
        scratch_shapes=[pltpu.VMEM((VEC_ROWS, D_MODEL), F32), pltpu.VMEM((VEC_ROWS, D_MODEL), F32),
                        pltpu.VMEM(gates.shape, F32), pltpu.VMEM((VEC_ROWS, D_MODEL), F32),
                        pltpu.VMEM(gates.shape, F32), pltpu.VMEM((N_CHIPS, hv, D_MODEL), F32),
                        pltpu.VMEM((N_CHIPS, hg) + gates.shape[1:], F32),
                        pltpu.SemaphoreType.DMA((n_sems,)), pltpu.SemaphoreType.DMA((n_sems,))],
        name="all_reduce_small",
    )(*[pieces[n] for n in names], gates, *[totals[n] for n in BIG_NAMES])
    return out[0], out[1], dict(zip(BIG_NAMES, out[2:]))


def _adamw_math(w, g, m, v):
    m = ADAM_B1 * m + (1.0 - ADAM_B1) * g
    v = ADAM_B2 * v + (1.0 - ADAM_B2) * (g * g)
    m_hat = m / (1.0 - ADAM_B1 ** ADAM_STEP)
    v_hat = v / (1.0 - ADAM_B2 ** ADAM_STEP)
    delta = -ADAM_LR * (m_hat / (jnp.sqrt(v_hat) + ADAM_EPS) + ADAM_WD * w)
    return delta, m, v


def _adamw_big(w, g, m, v):
    steps = 8
    blks = []
    for name in BIG_NAMES:
        rows, cols, _ = BIG[name]
        blks.append(pl.BlockSpec((rows // steps, cols), lambda i: (i, 0)))

    def body(*refs):
        ins, outs = refs[:4 * N_BIG], refs[4 * N_BIG:]
        for a in range(N_BIG):
            w_ref, g_ref, m_ref, v_ref = (ins[k * N_BIG + a] for k in range(4))
            d, nm, nv = _adamw_math(w_ref[...], g_ref[...], m_ref[...], v_ref[...])
            outs[a][...] = d
            outs[N_BIG + a][...] = nm
            outs[2 * N_BIG + a][...] = nv

    shapes = [jax.ShapeDtypeStruct(BIG[name][:2], F32) for name in BIG_NAMES]
    out = pl.pallas_call(
        body, grid=(steps,), in_specs=blks * 4, out_specs=blks * 3, out_shape=shapes * 3,
        name="adamw_big", compiler_params=_params("parallel"),
    )(*[t[name] for t in (w, g, m, v) for name in BIG_NAMES])
    return {name: (out[a], out[N_BIG + a], out[2 * N_BIG + a]) for a, name in enumerate(BIG_NAMES)}


SMALL = {"meta_tokens": (N_META, D_MODEL // N_CHIPS), "mix_norm_g": (1, D_MODEL), "conv_w": (CONV_W, D_RG // N_CHIPS),
         "conv_b": (1, D_RG), "w_rgate": (D_RG, RG_HEAD_DIM), "b_rgate": (1, D_RG), "w_igate": (D_RG, RG_HEAD_DIM),
         "b_igate": (1, D_RG), "lru_lambda": (1, D_RG), "rg_norm_g": (1, D_RG), "hg_lower_bound": (2, D_HG),
         "hg_norm_g": (1, HG_HEAD_DIM), "ffn_norm_g": (1, D_MODEL), "final_norm_g": (1, D_MODEL)}
SMALL_NAMES = tuple(SMALL)
SHARDED_SMALL = ("meta_tokens", "conv_w")


def _adamw_small(vec, gates, w, m, v):
    n = len(SMALL_NAMES)

    def body(*refs):
        vec_ref, gates_ref = refs[:2]
        w_refs, m_refs, v_refs = refs[2:2 + n], refs[2 + n:2 + 2 * n], refs[2 + 2 * n:2 + 3 * n]
        outs = refs[2 + 3 * n:]
        loss_ref = outs[0]
        x, y, _ = _place()
        chip = 2 * x + y
        loss_ref[...] = vec_ref[VEC_ROW["loss"]:VEC_ROW["loss"] + 1, 0:1]

        def update(k, g):
            g_ref, d_ref, nm_ref, nv_ref = outs[1 + 4 * k:5 + 4 * k]
            g_ref[...] = g
            d_ref[...], nm_ref[...], nv_ref[...] = _adamw_math(w_refs[k][...], g, m_refs[k][...], v_refs[k][...])

        for k, name in enumerate(SMALL_NAMES):
            nr, w_ = SMALL[name]
            if name == "w_rgate":
                update(k, gates_ref[0:D_RG, :])
            elif name == "w_igate":
                update(k, gates_ref[D_RG:2 * D_RG, :])
            elif name in SHARDED_SMALL:
                r0 = VEC_ROW[name]
                for q in range(N_CHIPS):
                    @pl.when(chip == q)
                    def _(k=k, r0=r0, nr=nr, w_=w_, q=q):
                        update(k, vec_ref[r0:r0 + nr, q * w_:(q + 1) * w_])
            else:
                r0 = VEC_ROW[name]
                update(k, vec_ref[r0:r0 + nr, 0:w_])

    vmem = pl.BlockSpec(memory_space=pltpu.VMEM)
    out_shape = [jax.ShapeDtypeStruct((1, 1), F32)]
    for name in SMALL_NAMES:
        out_shape += [jax.ShapeDtypeStruct(SMALL[name], F32)] * 4
    outs = pl.pallas_call(
        body, in_specs=[vmem] * (2 + 3 * n), out_specs=[vmem] * len(out_shape), out_shape=out_shape,
        name="adamw_small",
    )(vec, gates, *[w[k] for k in SMALL_NAMES], *[m[k] for k in SMALL_NAMES], *[v[k] for k in SMALL_NAMES])
    loss = outs[0]
    res = {name: tuple(outs[1 + 4 * k:5 + 4 * k]) for k, name in enumerate(SMALL_NAMES)}
    return loss, res


WEIGHT_NAMES = ("meta_tokens", "mix_norm_g", "w_in", "conv_w", "conv_b", "w_rgate", "b_rgate", "w_igate", "b_igate",
                "lru_lambda", "rg_norm_g", "hg_lower_bound", "hg_norm_g", "w_out", "ffn_norm_g", "w_gate_up", "w_down",
                "final_norm_g")


def _to_2d(name, a):
    if name in BIG:
        return a.reshape(BIG[name][:2])
    return a.reshape(SMALL[name])


def kernel(x, meta_tokens, mix_norm_g, w_in, conv_w, conv_b, w_rgate, b_rgate, w_igate, b_igate, lru_lambda, rg_norm_g, hg_lower_bound, hg_norm_g, w_out, ffn_norm_g, w_gate_up, w_down, final_norm_g, loss_target, m_meta_tokens, m_mix_norm_g, m_w_in, m_conv_w, m_conv_b, m_w_rgate, m_b_rgate, m_w_igate, m_b_igate, m_lru_lambda, m_rg_norm_g, m_hg_lower_bound, m_hg_norm_g, m_w_out, m_ffn_norm_g, m_w_gate_up, m_w_down, m_final_norm_g, v_meta_tokens, v_mix_norm_g, v_w_in, v_conv_w, v_conv_b, v_w_rgate, v_b_rgate, v_w_igate, v_b_igate, v_lru_lambda, v_rg_norm_g, v_hg_lower_bound, v_hg_norm_g, v_w_out, v_ffn_norm_g, v_w_gate_up, v_w_down, v_final_norm_g):
    w_raw = dict(zip(WEIGHT_NAMES, (meta_tokens, mix_norm_g, w_in, conv_w, conv_b, w_rgate, b_rgate, w_igate, b_igate,
                                    lru_lambda, rg_norm_g, hg_lower_bound, hg_norm_g, w_out, ffn_norm_g, w_gate_up,
                                    w_down, final_norm_g)))
    m_raw = dict(zip(WEIGHT_NAMES, (m_meta_tokens, m_mix_norm_g, m_w_in, m_conv_w, m_conv_b, m_w_rgate, m_b_rgate,
                                    m_w_igate, m_b_igate, m_lru_lambda, m_rg_norm_g, m_hg_lower_bound, m_hg_norm_g,
                                    m_w_out, m_ffn_norm_g, m_w_gate_up, m_w_down, m_final_norm_g)))
    v_raw = dict(zip(WEIGHT_NAMES, (v_meta_tokens, v_mix_norm_g, v_w_in, v_conv_w, v_conv_b, v_w_rgate, v_b_rgate,
                                    v_w_igate, v_b_igate, v_lru_lambda, v_rg_norm_g, v_hg_lower_bound, v_hg_norm_g,
                                    v_w_out, v_ffn_norm_g, v_w_gate_up, v_w_down, v_final_norm_g)))
    w = {k: _to_2d(k, a) for k, a in w_raw.items()}
    m = {k: _to_2d(k, a) for k, a in m_raw.items()}
    v = {k: _to_2d(k, a) for k, a in v_raw.items()}

    x_i, y_i, c_i = _place()
    core = jnp.reshape(c_i, (1,)).astype(jnp.int32)
    chip = jnp.reshape(2 * x_i + y_i, (1,)).astype(jnp.int32)
    chip_core = jnp.concatenate([chip, core])

    placed, (meta_full, cw_full) = _place_shards(w, [w["meta_tokens"], w["conv_w"]], chip)
    first, _ = _gather_weights(placed, [], ("w_in",), "gather_first", 1)
    rest, _ = _gather_weights(placed, [], ("w_out", "w_gate_up", "w_down"), "gather_rest", 2)
    full = {**first, **rest}

    seq = x.shape[1]
    small ={k: w[k] for k in SMALL_NAMES if k not in SHARDED_SMALL}
    small["conv_w"] = cw_full

    def reduce_to_chips(grads, names, tag, collective_ids):
        got = _exchange_halves(grads, names, "exchange_halves_" + tag, collective_ids[0])

        def chip_sums():
            return _chip_sum(grads, got, names, core, "chip_sum_" + tag)

        def send(sums):
            arrived = _send_chip_sums({n: sums[n][1] for n in names}, names, "send_chip_sums_" + tag,
                                      collective_ids[1])
            return {n: (sums[n][0], a) for n, a in zip(names, arrived)}

        return chip_sums, send

    ffn_names, mixer_names = ("w_gate_up", "w_down", "w_out"), ("w_in",)
    loss, grad_x, grads, parts, parts_mixer = _local_step(
        x.reshape(seq, D_MODEL), meta_full, loss_target.reshape(seq, D_MODEL),
        w["w_in"], full["w_in"], full["w_out"], full["w_gate_up"], full["w_down"], small, chip,
        on_ffn_grads=lambda g: reduce_to_chips(g, ffn_names, "ffn", (3, 4)),
        on_mixer_grads=lambda g: reduce_to_chips(g, mixer_names, "mixer", (None, 5)))
    parts.update(parts_mixer)
    totals = _total(parts, chip_core)
    pieces = {k: grads[k] for k in VEC_ROW if k != "loss"}
    pieces["loss"] = loss
    vec, gates, g_big = _all_reduce_small(pieces, grads["w_gates"], totals)
    loss_sum, res = _adamw_small(vec, gates, w, m, v)
    updates = _adamw_big(w, g_big, m, v)
    for n in BIG_NAMES:
        res[n] = (g_big[n],) + updates[n]

    out = [loss_sum.reshape(()), grad_x.reshape(1, seq, D_MODEL)]
    for j in range(4):
        out += [res[n][j].reshape(w_raw[n].shape) for n in WEIGHT_NAMES]
    return tuple(out)
```

```python
import math

import jax
import jax.numpy as jnp
from jax import lax
from jax.experimental import pallas as pl
from jax.experimental.pallas import tpu as pltpu
from jax.experimental.pallas import tpu_sc as plsc

F32 = jnp.float32
BF16 = jnp.bfloat16
HIGHEST = lax.Precision.HIGHEST
MESH = pl.DeviceIdType.MESH

D_MODEL = 1024
D_RG = 512
RG_HEAD_DIM = 64
D_HG = 512
HG_HEAD_DIM = 128
HG_HEADS = 4
CHUNK = 64
SUB = 16
N_SUB = CHUNK // SUB
N_META = 16
PAD = CHUNK - N_META
D_IN = 3072
D_FF = 2816
CONV_W = 4
LRU_C = 8.0
EPS = 1e-6
EXP_CLAMP = 80.0
GELU_C = math.sqrt(2.0 / math.pi)
GELU_A = 0.044715
N_CHIPS = 4

ADAM_LR = 0.001
ADAM_B1 = 0.9
ADAM_B2 = 0.999
ADAM_EPS = 1e-08
ADAM_WD = 0.01
ADAM_STEP = 10

VMEM_LIMIT = 56 * 1024 * 1024


def _params(*sem):
    return pltpu.CompilerParams(dimension_semantics=sem, vmem_limit_bytes=VMEM_LIMIT)


def _row_tile(rows, target):
    best = None
    for t in range(16, min(rows, target) + 1, 16):
        if rows % t == 0:
            best = t
    assert best is not None, rows
    return best


def _sigmoid(x):
    return 0.5 * jnp.tanh(0.5 * x) + 0.5


def _dot(a, b):
    return jnp.dot(a, b, preferred_element_type=F32)


def _dot_nt(a, b):
    return lax.dot_general(a, b, (((1,), (1,)), ((), ())), preferred_element_type=F32)


def _dot_tn(a, b):
    return lax.dot_general(a, b, (((0,), (0,)), ((), ())), preferred_element_type=F32)


def _rms(x):
    return lax.rsqrt(jnp.mean(x * x, axis=-1, keepdims=True) + EPS)


def _rms_bwd(dn, n, r):
    return r * (dn - n * jnp.mean(dn * n, axis=-1, keepdims=True))


def _gelu_parts(x):
    t = jnp.tanh(GELU_C * (x + GELU_A * x * x * x))
    g = 0.5 * x * (1.0 + t)
    dg = 0.5 * (1.0 + t) + 0.5 * x * (1.0 - t * t) * GELU_C * (1.0 + 3.0 * GELU_A * x * x)
    return g, dg


def _softplus_neg(lam):
    e = jnp.exp(-jnp.abs(lam))
    w = 1.0 + e
    log1p = jnp.where(w == 1.0, e, jnp.log(w) * e / (w - 1.0))
    return jnp.maximum(-lam, 0.0) + log1p


def _head_mask():
    r = lax.broadcasted_iota(jnp.int32, (D_RG, D_RG), 0) // RG_HEAD_DIM
    c = lax.broadcasted_iota(jnp.int32, (D_RG, D_RG), 1) // RG_HEAD_DIM
    return r == c


def _head_fold():
    r = lax.broadcasted_iota(jnp.int32, (D_RG, RG_HEAD_DIM), 0) % RG_HEAD_DIM
    c = lax.broadcasted_iota(jnp.int32, (D_RG, RG_HEAD_DIM), 1)
    return (r == c).astype(F32)


def _gate_weights(w_r, w_i):
    def body(wr_ref, wi_ref, o_ref):
        fold = _head_fold()
        mask = _head_mask()
        for k, ref in enumerate((wr_ref, wi_ref)):
            full = lax.dot_general(ref[...], fold, (((1,), (1,)), ((), ())),
                                   precision=HIGHEST, preferred_element_type=F32)
            o_ref[:, k * D_RG:(k + 1) * D_RG] = jnp.where(mask, full, 0.0).astype(BF16)

    return pl.pallas_call(
        body, out_shape=jax.ShapeDtypeStruct((D_RG, 2 * D_RG), BF16), name="gate_weights",
    )(w_r, w_i)


HEAD = PAD + N_META


def _window_copies(seq_hbm, buf, sems, tm):
    def first(to_vmem):
        seq, vm = seq_hbm.at[pl.ds(0, tm - HEAD)], buf.at[0, pl.ds(HEAD, tm - HEAD)]
        return pltpu.make_async_copy(seq, vm, sems.at[0]) if to_vmem else pltpu.make_async_copy(vm, seq, sems.at[0])

    def later(j, slot, to_vmem):
        seq, vm = seq_hbm.at[pl.ds(pl.multiple_of(j * tm - HEAD, 8), tm)], buf.at[slot]
        if to_vmem:
            return pltpu.make_async_copy(seq, vm, sems.at[slot])
        return pltpu.make_async_copy(vm, seq, sems.at[slot])

    return first, later


def _fetch_window(seq_hbm, buf, sems, i, n_steps, tm):
    first, later = _window_copies(seq_hbm, buf, sems, tm)
    slot = i % 2

    @pl.when(i == 0)
    def _():
        first(True).start()

    if n_steps > 1:
        @pl.when(i + 1 < n_steps)
        def _():
            later(i + 1, 1 - slot, True).start()

    @pl.when(i == 0)
    def _():
        first(True).wait()

    if n_steps > 1:
        @pl.when(i > 0)
        def _():
            later(i, slot, True).wait()

    return slot


def _in_proj_local(x, meta, g1, w_own, chip):
    T = x.shape[0] + HEAD
    tm = _row_tile(T, 832)
    n_steps = T // tm
    cols = BIG["w_in"][1]

    def body(s_ref, x_hbm, meta_ref, g_ref, w_ref, p_ref, u_ref, h_ref, buf, sems, wb):
        i = pl.program_id(0)
        slot = _fetch_window(x_hbm, buf, sems, i, n_steps, tm)

        @pl.when(i == 0)
        def _():
            buf[0, 0:PAD, :] = jnp.zeros((PAD, D_MODEL), F32)
            buf[0, PAD:HEAD, :] = meta_ref[...]
            wb[...] = w_ref[...].astype(BF16)

        h = buf[slot]
        h_ref[...] = h
        u = (h * _rms(h) * g_ref[...]).astype(BF16)
        u_ref[...] = u
        p_ref[...] = _dot(u, wb[...])

    return pl.pallas_call(
        body,
        grid_spec=pltpu.PrefetchScalarGridSpec(
            num_scalar_prefetch=1, grid=(n_steps,),
            in_specs=[pl.BlockSpec(memory_space=pl.ANY),
                      pl.BlockSpec((N_META, D_MODEL), lambda i, s: (0, 0)),
                      pl.BlockSpec((1, D_MODEL), lambda i, s: (0, 0)),
                      pl.BlockSpec((D_MODEL, cols), lambda i, s: (0, 0))],
            out_specs=[pl.BlockSpec((tm, cols), lambda i, s: (i, s[0])),
                       pl.BlockSpec((tm, D_MODEL), lambda i, s: (i, 0)),
                       pl.BlockSpec((tm, D_MODEL), lambda i, s: (i, 0))],
            scratch_shapes=[pltpu.VMEM((2, tm, D_MODEL), F32), pltpu.SemaphoreType.DMA((2,)),
                            pltpu.VMEM((D_MODEL, cols), BF16)]),
        out_shape=[jax.ShapeDtypeStruct((T, D_IN), F32), jax.ShapeDtypeStruct((T, D_MODEL), BF16),
                   jax.ShapeDtypeStruct((T, D_MODEL), F32)],
        name="in_proj_local", compiler_params=_params("arbitrary"),
    )(chip, x, meta, g1, w_own)


def _in_proj_rest(u, w_in, p, chip):
    T = u.shape[0]
    tm = _row_tile(T, 2080)
    cols = BIG["w_in"][1]
    block = lambda j, s: (s[0] + 1 + j) % N_CHIPS

    def body(s_ref, u_ref, w_ref, p_in_ref, p_ref):
        p_ref[...] = _dot(u_ref[...], w_ref[...])

    return pl.pallas_call(
        body,
        grid_spec=pltpu.PrefetchScalarGridSpec(
            num_scalar_prefetch=1, grid=(N_CHIPS - 1, T // tm),
            in_specs=[pl.BlockSpec((tm, D_MODEL), lambda j, i, s: (i, 0)),
                      pl.BlockSpec((D_MODEL, cols), lambda j, i, s: (0, block(j, s))), ANY],
            out_specs=pl.BlockSpec((tm, cols), lambda j, i, s: (i, block(j, s)))),
        out_shape=jax.ShapeDtypeStruct((T, D_IN), F32),
        input_output_aliases={3: 0},
        name="in_proj_rest", compiler_params=_params("arbitrary", "arbitrary"),
    )(chip, u, w_in, p)


def _scan_block_fwd(A, B, rowi):
    for d in (1, 2, 4):
        a_sh = pltpu.roll(A, d, axis=0)
        b_sh = pltpu.roll(B, d, axis=0)
        m = rowi >= d
        B = jnp.where(m, A * b_sh + B, B)
        A = jnp.where(m, A * a_sh, A)
    return A, B


def _scan_block_bwd(A, B, rowi):
    for d in (1, 2, 4):
        a_sh = pltpu.roll(A, 8 - d, axis=0)
        b_sh = pltpu.roll(B, 8 - d, axis=0)
        m = rowi < 8 - d
        B = jnp.where(m, A * b_sh + B, B)
        A = jnp.where(m, A * a_sh, A)
    return A, B


def _rg_gates(xc, w_ref, bg_ref, lam):
    pre = _dot(xc.astype(BF16), w_ref[...]) + bg_ref[...]
    r = _sigmoid(pre[:, :D_RG])
    ig = _sigmoid(pre[:, D_RG:])
    sp = _softplus_neg(lam)
    la = -LRU_C * sp * r
    a = jnp.exp(la)
    th = jnp.tanh(la)
    u = 1.0 - th
    rc = pl.reciprocal(u, approx=True)
    rc = rc * (2.0 - u * rc)
    rc = rc * (2.0 - u * rc)
    m2 = -2.0 * th * rc
    inv_m = lax.rsqrt(jnp.maximum(m2, 1e-30))
    return r, ig, sp, a, m2 * inv_m, inv_m


def _conv(ext, cw_ref, cb_ref, tm):
    xc = cb_ref[...] + cw_ref[0:1, :] * ext[8 - 3:8 - 3 + tm, :]
    for j in range(1, CONV_W):
        xc = xc + cw_ref[j:j + 1, :] * ext[8 - 3 + j:8 - 3 + j + tm, :]
    return xc


def _scan_unroll(blocks):
    return 4 if blocks % 4 == 0 else 2 if blocks % 2 == 0 else 1


def _rg_fwd(p, cw, cb, wg, bg, lam, rg_g):
    T = p.shape[0]
    tm = _row_tile(T, 832)
    unroll = _scan_unroll(tm // 8)

    def body(xg_ref, cw_ref, cb_ref, w_ref, bg_ref, lam_ref, g_ref, y_ref, h_ref, xc_ref, ext, a_s, b_s, carry):
        i = pl.program_id(0)

        @pl.when(i == 0)
        def _():
            ext[0:8, :] = jnp.zeros((8, D_RG), F32)
            carry[...] = jnp.zeros((1, D_RG), F32)

        ext[8:8 + tm, :] = xg_ref[:, :D_RG]
        xc = _conv(ext, cw_ref, cb_ref, tm)
        xc_ref[...] = xc
        r, ig, sp, a, m, _ = _rg_gates(xc, w_ref, bg_ref, lam_ref[...])
        row = i * tm + lax.broadcasted_iota(jnp.int32, (tm, 1), 0)
        a_s[...] = a
        b_s[...] = jnp.where(row >= PAD, m * ig * xc, 0.0)
        rowi = lax.broadcasted_iota(jnp.int32, (8, D_RG), 0)

        def blk(j, c):
            for u in range(unroll):
                o = pl.multiple_of((j * unroll + u) * 8, 8)
                A, B = _scan_block_fwd(a_s[pl.ds(o, 8), :], b_s[pl.ds(o, 8), :], rowi)
                h = B + A * c
                h_ref[pl.ds(o, 8), :] = h
                c = h[7:8, :]
            return c

        carry[...] = lax.fori_loop(0, tm // (8 * unroll), blk, carry[...])
        ext[0:8, :] = ext[tm:tm + 8, :]
        g, _ = _gelu_parts(xg_ref[:, D_RG:])
        yy = g * h_ref[...]
        y_ref[...] = (yy * _rms(yy) * g_ref[...]).astype(BF16)

    vec = lambda n: pl.BlockSpec((1, n), lambda i: (0, 0))
    return pl.pallas_call(
        body, grid=(T // tm,),
        in_specs=[pl.BlockSpec((tm, 2 * D_RG), lambda i: (i, 0)),
                  pl.BlockSpec((CONV_W, D_RG), lambda i: (0, 0)), vec(D_RG),
                  pl.BlockSpec((D_RG, 2 * D_RG), lambda i: (0, 0)), vec(2 * D_RG), vec(D_RG), vec(D_RG)],
        out_specs=[pl.BlockSpec((tm, D_RG), lambda i: (i, 0))] * 3,
        out_shape=[jax.ShapeDtypeStruct((T, D_RG), BF16), jax.ShapeDtypeStruct((T, D_RG), F32),
                   jax.ShapeDtypeStruct((T, D_RG), F32)],
        scratch_shapes=[pltpu.VMEM((tm + 8, D_RG), F32), pltpu.VMEM((tm, D_RG), F32),
                        pltpu.VMEM((tm, D_RG), F32), pltpu.VMEM((1, D_RG), F32)],
        name="rg_fwd", compiler_params=_params("arbitrary"),
    )(p, cw, cb, wg, bg, lam, rg_g)


def _tri(lower):
    r = lax.broadcasted_iota(jnp.int32, (CHUNK, CHUNK), 0)
    c = lax.broadcasted_iota(jnp.int32, (CHUNK, CHUNK), 1)
    return ((c <= r) if lower else (c >= r)).astype(F32)


def _hg_gates(hq, hf, lbraw_ref, valid):
    lb = _sigmoid(lbraw_ref[0:1, :] - lbraw_ref[1:2, :])
    sq = _sigmoid(hq)
    q = hq * sq
    sf = _sigmoid(hf)
    f = lb + (1.0 - lb) * sf
    lf = jnp.where(valid, jnp.log(f), 0.0)
    b = jnp.dot(_tri(True), lf, precision=HIGHEST, preferred_element_type=F32)
    return lb, sq, q, sf, f, b


def _hg_head(qh, kh, bh):
    blk = lax.broadcasted_iota(jnp.int32, (CHUNK, 1), 0) // SUB
    b_last = bh[CHUNK - 1:CHUNK, :]
    refs = [bh[SUB * s:SUB * s + 1, :] for s in range(N_SUB)]
    r_sel = refs[N_SUB - 1]
    for s in range(N_SUB - 2, -1, -1):
        r_sel = jnp.where(blk == s, refs[s], r_sel)
    eb = jnp.exp(bh)
    eq = jnp.exp(bh - r_sel)
    ekh = jnp.exp(b_last - bh)
    ek = [jnp.exp(jnp.minimum(refs[s] - bh, EXP_CLAMP)) for s in range(N_SUB)]
    qe = qh * eq
    q_hat = jnp.concatenate([jnp.where(blk == s, qe, 0.0) for s in range(N_SUB)], axis=1)
    k_til = jnp.concatenate([kh * ek[s] for s in range(N_SUB)], axis=1)
    return blk, b_last, eb, eq, ekh, ek, q_hat, k_til


def _causal():
    r = lax.broadcasted_iota(jnp.int32, (CHUNK, CHUNK), 0)
    c = lax.broadcasted_iota(jnp.int32, (CHUNK, CHUNK), 1)
    return r >= c


def _chunks_per_step(n_chunks):
    for c in (5, 4, 3, 2):
        if n_chunks % c == 0:
            return c
    return 1


def _hg_fwd(p, lbraw, hg_g):
    T = p.shape[0]
    n_chunks = T // CHUNK
    cps = _chunks_per_step(n_chunks)
    rows = cps * CHUNK

    def body(hq_ref, hf_ref, hi_ref, hg_ref, lb_ref, g_ref, y_ref, o_ref, st_all_ref, st):
        i = pl.program_id(0)

        @pl.when(i == 0)
        def _():
            st[...] = jnp.zeros_like(st)

        def chunk(j, carry):
            rs = pl.ds(pl.multiple_of(j * CHUNK, CHUNK), CHUNK)
            chunk_body(i * cps + j, hq_ref.at[rs, :], hf_ref.at[rs, :], hi_ref.at[rs, :], hg_ref.at[rs, :], lb_ref,
                       g_ref, y_ref.at[rs, :], o_ref.at[rs, :], st_all_ref.at[pl.ds(j, 1)], st)
            return carry

        lax.fori_loop(0, cps, chunk, 0, unroll=True)

    def chunk_body(n, hq_ref, hf_ref, hi_ref, hg_ref, lb_ref, g_ref, y_ref, o_ref, st_all_ref, st):
        valid = (n * CHUNK + lax.broadcasted_iota(jnp.int32, (CHUNK, 1), 0)) >= PAD
        hq, hf, v, hg = hq_ref[...], hf_ref[...], hi_ref[...], hg_ref[...]
        lb, sq, q, sf, f, b = _hg_gates(hq, hf, lb_ref, valid)
        k = 1.0 - f
        st_all_ref[0] = st[...]
        causal = _causal()
        v_t = v.T.astype(BF16)
        heads = [slice(h * HG_HEAD_DIM, (h + 1) * HG_HEAD_DIM) for h in range(HG_HEADS)]
        fac = []
        for sl in heads:
            qh, kh, bh = q[:, sl], k[:, sl], b[:, sl]
            _, b_last, eb, _, ekh, _, q_hat, k_til = _hg_head(qh, kh, bh)
            fac.append((jnp.exp(b_last), (qh * eb).astype(BF16), q_hat.astype(BF16), k_til.astype(BF16),
                        (kh * ekh).astype(BF16), v[:, sl].astype(BF16)))
        raw = []
        for sl, (_, q_til, q_hat, k_til, k_hat, _) in zip(heads, fac):
            st_h = st[sl, :]
            raw.append((_dot_nt(q_til, st_h.astype(BF16)), _dot_nt(q_hat, k_til), _dot(v_t[sl, :], k_hat), st_h))
        for sl, (e_last, _, _, _, _, vb), (inter, att, upd, st_h) in zip(heads, fac, raw):
            o = inter + _dot(jnp.where(causal, att, 0.0).astype(BF16), vb)
            st[sl, :] = st_h * e_last + upd
            o_ref[:, sl] = o
            hgh = hg[:, sl]
            y_ref[:, sl] = (o * _rms(o) * g_ref[...] * (hgh * _sigmoid(hgh))).astype(BF16)

    col = lambda j: pl.BlockSpec((rows, D_HG), lambda n: (n, j))
    return pl.pallas_call(
        body, grid=(n_chunks // cps,),
        in_specs=[col(2), col(3), col(4), col(5),
                  pl.BlockSpec((2, D_HG), lambda n: (0, 0)), pl.BlockSpec((1, HG_HEAD_DIM), lambda n: (0, 0))],
        out_specs=[pl.BlockSpec((rows, D_HG), lambda n: (n, 0)), pl.BlockSpec((rows, D_HG), lambda n: (n, 0)),
                   pl.BlockSpec((cps, D_HG, HG_HEAD_DIM), lambda n: (n, 0, 0))],
        out_shape=[jax.ShapeDtypeStruct((T, D_HG), BF16), jax.ShapeDtypeStruct((T, D_HG), F32),
                   jax.ShapeDtypeStruct((n_chunks, D_HG, HG_HEAD_DIM), F32)],
        scratch_shapes=[pltpu.VMEM((D_HG, HG_HEAD_DIM), F32)],
        name="hg_fwd", compiler_params=_params("arbitrary"),
    )(p, p, p, p, lbraw, hg_g)


def _ffn_fwd(h0, y_rg, y_hg, w_out, g2, w_gu, w_down, gf, target):
    T = h0.shape[0]
    tm = _row_tile(T, 320)
    n_steps = T // tm

    def body(h_ref, yr_ref, yh_ref, wo_ref, g2_ref, wgu_ref, wd_ref, gf_ref, t_hbm,
             h1_ref, v_ref, y_ref, gu_ref, act_ref, dh2_ref, dh2b_ref, loss_ref, gg_ref, tbuf, sems):
        i = pl.program_id(0)
        slot = _fetch_window(t_hbm, tbuf, sems, i, n_steps, tm)

        @pl.when(i == 0)
        def _():
            loss_ref[...] = jnp.zeros_like(loss_ref)
            gg_ref[...] = jnp.zeros_like(gg_ref)
            tbuf[0, 0:HEAD, :] = jnp.zeros((HEAD, D_MODEL), F32)

        y_ref[:, :D_RG] = yr_ref[...]
        y_ref[:, D_RG:] = yh_ref[...]
        h1 = h_ref[...] + _dot(y_ref[...], wo_ref[...])
        h1_ref[...] = h1
        v = (h1 * _rms(h1) * g2_ref[...]).astype(BF16)
        v_ref[...] = v

        gu = _dot(v, wgu_ref[...])
        gu_ref[...] = gu.astype(BF16)
        g = gu[:, :D_FF]
        act = (g * _sigmoid(g) * gu[:, D_FF:]).astype(BF16)
        act_ref[...] = act

        h2 = h1 + _dot(act, wd_ref[...])
        r = _rms(h2)
        n = h2 * r
        gf_ = gf_ref[...]
        row = i * tm + lax.broadcasted_iota(jnp.int32, (tm, 1), 0)
        err = jnp.where(row >= HEAD, n * gf_ - tbuf[slot], 0.0)
        loss_ref[...] += 0.5 * jnp.sum(jnp.mean(err * err, axis=-1, keepdims=True), axis=0, keepdims=True)
        dy = err * (1.0 / D_MODEL)
        gg_ref[...] += jnp.sum(dy * n, axis=0, keepdims=True)
        dh2 = _rms_bwd(dy * gf_, n, r)
        dh2_ref[...] = dh2
        dh2b_ref[...] = dh2.astype(BF16)

    row_spec = lambda n: pl.BlockSpec((tm, n), lambda i: (i, 0))
    vec = pl.BlockSpec((1, D_MODEL), lambda i: (0, 0))
    return pl.pallas_call(
        body, grid=(n_steps,),
        in_specs=[row_spec(D_MODEL), row_spec(D_RG), row_spec(D_HG), _resident((D_MODEL, D_MODEL)), vec,
                  _resident((D_MODEL, 2 * D_FF)), _resident((D_FF, D_MODEL)), vec,
                  pl.BlockSpec(memory_space=pl.ANY)],
        out_specs=[row_spec(D_MODEL), row_spec(D_MODEL), row_spec(D_MODEL), row_spec(2 * D_FF), row_spec(D_FF),
                   row_spec(D_MODEL), row_spec(D_MODEL), pl.BlockSpec((1, 1), lambda i: (0, 0)), vec],
        out_shape=[jax.ShapeDtypeStruct((T, D_MODEL), F32), jax.ShapeDtypeStruct((T, D_MODEL), BF16),
                   jax.ShapeDtypeStruct((T, D_MODEL), BF16), jax.ShapeDtypeStruct((T, 2 * D_FF), BF16),
                   jax.ShapeDtypeStruct((T, D_FF), BF16), jax.ShapeDtypeStruct((T, D_MODEL), F32),
                   jax.ShapeDtypeStruct((T, D_MODEL), BF16), jax.ShapeDtypeStruct((1, 1), F32),
                   jax.ShapeDtypeStruct((1, D_MODEL), F32)],
        scratch_shapes=[pltpu.VMEM((2, tm, D_MODEL), F32), pltpu.SemaphoreType.DMA((2,))],
        name="ffn_fwd", compiler_params=_params("arbitrary"),
    )(h0, y_rg, y_hg, w_out, g2, w_gu, w_down, gf, target)


def _resident(shape):
    return pl.BlockSpec(shape, lambda i: (0,) * len(shape), pipeline_mode=pl.Buffered(1))


def _ffn_bwd(dh2b, gu, w_down, w_gu, h1, g2, dh2, w_out):
    T = h1.shape[0]
    tm = _row_tile(T, 320)

    def body(d_ref, gu_ref, wd_ref, wgu_ref, h_ref, g_ref, d2_ref, wo_ref, dgu_ref, dh1_ref, dh1b_ref, dy_ref, gg_ref):
        i = pl.program_id(0)

        @pl.when(i == 0)
        def _():
            gg_ref[...] = jnp.zeros_like(gg_ref)

        dact = _dot_nt(d_ref[...], wd_ref[...]).astype(BF16)
        g = gu_ref[:, :D_FF]
        u = gu_ref[:, D_FF:]
        s = _sigmoid(g)
        dgu_ref[:, :D_FF] = dact * u * (s * (1.0 + g * (1.0 - s)))
        dgu_ref[:, D_FF:] = dact * (g * s)

        dv = _dot_nt(dgu_ref[...], wgu_ref[...])
        h1_ = h_ref[...]
        r = _rms(h1_)
        n = h1_ * r
        gg_ref[...] += jnp.sum(dv * n, axis=0, keepdims=True)
        dh1 = d2_ref[...] + _rms_bwd(dv * g_ref[...], n, r)
        dh1_ref[...] = dh1
        db = dh1.astype(BF16)
        dh1b_ref[...] = db
        dy_ref[...] = _dot_nt(db, wo_ref[...])

    row = lambda n: pl.BlockSpec((tm, n), lambda i: (i, 0))
    return pl.pallas_call(
        body, grid=(T // tm,),
        in_specs=[row(D_MODEL), row(2 * D_FF), _resident((D_FF, D_MODEL)), _resident((D_MODEL, 2 * D_FF)),
                  row(D_MODEL), pl.BlockSpec((1, D_MODEL), lambda i: (0, 0)), row(D_MODEL),
                  _resident((D_MODEL, D_MODEL))],
        out_specs=[row(2 * D_FF), row(D_MODEL), row(D_MODEL), row(D_MODEL),
                   pl.BlockSpec((1, D_MODEL), lambda i: (0, 0))],
        out_shape=[jax.ShapeDtypeStruct((T, 2 * D_FF), BF16), jax.ShapeDtypeStruct((T, D_MODEL), F32),
                   jax.ShapeDtypeStruct((T, D_MODEL), BF16), jax.ShapeDtypeStruct((T, D_MODEL), F32),
                   jax.ShapeDtypeStruct((1, D_MODEL), F32)],
        name="ffn_bwd", compiler_params=_params("arbitrary"),
    )(dh2b, gu, w_down, w_gu, h1, g2, dh2, w_out)


def _rg_bwd(p, xc_all, hs, dy, dp, cw, cb, wg, bg, lam, rg_g):
    T = p.shape[0]
    tm = _row_tile(T, 832)
    nt = T // tm
    hb = tm // 8
    unroll = _scan_unroll(hb)

    def body(xg_ref, xc_ref, h_ref, hh_ref, dy_ref, dp_in_ref, cw_ref, cb_ref, w_ref, bg_ref, lam_ref, g_ref,
             dp_ref, gcw_ref, gcb_ref, gw_ref, gbg_ref, glam_ref, gg_ref,
             dext, a_s, b_s, d_s, gacc, carry_d, carry_a):
        i = pl.program_id(0)
        t_idx = nt - 1 - i

        @pl.when(i == 0)
        def _():
            dext[tm:tm + 8, :] = jnp.zeros((8, D_RG), F32)
            carry_d[...] = jnp.zeros_like(carry_d)
            carry_a[...] = jnp.zeros_like(carry_a)
            gacc[...] = jnp.zeros_like(gacc)
            for ref in (gcw_ref, gcb_ref, gbg_ref, glam_ref, gg_ref, gw_ref):
                ref[...] = jnp.zeros_like(ref)

        first = t_idx == 0
        xc = xc_ref[...]
        lam_ = lam_ref[...]
        r, ig, sp, a, m, inv_m = _rg_gates(xc, w_ref, bg_ref, lam_)
        row = t_idx * tm + lax.broadcasted_iota(jnp.int32, (tm, 1), 0)
        valid = row >= PAD

        gr = xg_ref[:, D_RG:]
        g, dgelu = _gelu_parts(gr)
        h = h_ref[...]
        yy = g * h
        rr = _rms(yy)
        nn = yy * rr
        dy_ = dy_ref[...]
        gg_ref[...] += jnp.sum(dy_ * nn, axis=0, keepdims=True)
        dyy = _rms_bwd(dy_ * g_ref[...], nn, rr)
        dp_ref[:, D_RG:] = (dyy * h * dgelu).astype(BF16)

        a_s[...] = a
        b_s[...] = dyy * g
        rowi = lax.broadcasted_iota(jnp.int32, (8, D_RG), 0)

        def blk(jj, c):
            cd, ca = c
            for u in range(unroll):
                o = pl.multiple_of((hb - 1 - (jj * unroll + u)) * 8, 8)
                a_blk = a_s[pl.ds(o, 8), :]
                a_next = jnp.where(rowi == 7, ca, pltpu.roll(a_blk, 7, axis=0))
                A, B = _scan_block_bwd(a_next, b_s[pl.ds(o, 8), :], rowi)
                d = B + A * cd
                d_s[pl.ds(o, 8), :] = d
                cd, ca = d[0:1, :], a_blk[0:1, :]
            return cd, ca

        cd, ca = lax.fori_loop(0, hb // unroll, blk, (carry_d[...], carry_a[...]))
        carry_d[...] = cd
        carry_a[...] = ca
        delta = d_s[...]

        h_last_prev = jnp.where(first, 0.0, hh_ref[7:8, :])
        row0 = lax.broadcasted_iota(jnp.int32, (tm, 1), 0) == 0
        h_prev = jnp.where(row0, h_last_prev, pltpu.roll(h, 1, axis=0))
        dbx = jnp.where(valid, delta, 0.0)
        da = delta * h_prev
        di = dbx * m * xc
        dm = dbx * ig * xc
        dla = a * (da - dm * a * inv_m)
        dla = jnp.where(valid, dla, 0.0)
        glam_ref[...] += jnp.sum(dla * r, axis=0, keepdims=True) * (LRU_C / (1.0 + jnp.exp(lam_)))
        dr = (-LRU_C) * sp * dla
        dpre = jnp.concatenate([dr * r * (1.0 - r), di * ig * (1.0 - ig)], axis=1)
        gbg_ref[...] += jnp.sum(dpre, axis=0, keepdims=True)
        dpre_b = dpre.astype(BF16)
        gacc[...] += _dot_tn(xc.astype(BF16), dpre_b)
        dxc = dbx * m * ig + _dot_nt(dpre_b, w_ref[...])
        gcb_ref[...] += jnp.sum(dxc, axis=0, keepdims=True)
        dext[0:tm, :] = dxc
        xr = xg_ref[:, :D_RG]
        dxr = None
        for j in range(CONV_W):
            shifted = dext[3 - j:3 - j + tm, :]
            gcw_ref[j:j + 1, :] += jnp.sum(xr * shifted, axis=0, keepdims=True)
            tap = cw_ref[j:j + 1, :] * shifted
            dxr = tap if dxr is None else dxr + tap
        dp_ref[:, :D_RG] = dxr.astype(BF16)
        dext[tm:tm + 8, :] = dext[0:8, :]

        @pl.when(i == nt - 1)
        def _():
            fold = _head_fold()
            mask = _head_mask()
            for k in range(2):
                blockdiag = jnp.where(mask, gacc[:, k * D_RG:(k + 1) * D_RG], 0.0)
                gw_ref[k * D_RG:(k + 1) * D_RG, :] = jnp.dot(blockdiag, fold, precision=HIGHEST,
                                                             preferred_element_type=F32)

    vec = lambda n: pl.BlockSpec((1, n), lambda i: (0, 0))
    rev = lambda n: pl.BlockSpec((tm, n), lambda i: (nt - 1 - i, 0))
    halo = lambda n: pl.BlockSpec((8, n), lambda i: (jnp.maximum((nt - 1 - i) * hb - 1, 0), 0))
    return pl.pallas_call(
        body, grid=(nt,),
        in_specs=[rev(2 * D_RG), rev(D_RG), rev(D_RG), halo(D_RG), rev(D_RG), ANY,
                  pl.BlockSpec((CONV_W, D_RG), lambda i: (0, 0)), vec(D_RG),
                  pl.BlockSpec((D_RG, 2 * D_RG), lambda i: (0, 0)), vec(2 * D_RG), vec(D_RG), vec(D_RG)],
        out_specs=[rev(2 * D_RG), pl.BlockSpec((CONV_W, D_RG), lambda i: (0, 0)), vec(D_RG),
                   pl.BlockSpec((2 * D_RG, RG_HEAD_DIM), lambda i: (0, 0)), vec(2 * D_RG), vec(D_RG), vec(D_RG)],
        input_output_aliases={5: 0},
        out_shape=[jax.ShapeDtypeStruct((T, D_IN), BF16), jax.ShapeDtypeStruct((CONV_W, D_RG), F32),
                   jax.ShapeDtypeStruct((1, D_RG), F32), jax.ShapeDtypeStruct((2 * D_RG, RG_HEAD_DIM), F32),
                   jax.ShapeDtypeStruct((1, 2 * D_RG), F32), jax.ShapeDtypeStruct((1, D_RG), F32),
                   jax.ShapeDtypeStruct((1, D_RG), F32)],
        scratch_shapes=[pltpu.VMEM((tm + 8, D_RG), F32),
                        pltpu.VMEM((tm, D_RG), F32), pltpu.VMEM((tm, D_RG), F32), pltpu.VMEM((tm, D_RG), F32),
                        pltpu.VMEM((D_RG, 2 * D_RG), F32), pltpu.VMEM((1, D_RG), F32), pltpu.VMEM((1, D_RG), F32)],
        name="rg_bwd", compiler_params=_params("arbitrary"),
    )(p, xc_all, hs, hs, dy, dp, cw, cb, wg, bg, lam, rg_g)


def _hg_bwd(p, o_all, st_all, dy, lbraw, hg_g):
    T = p.shape[0]
    n_chunks = T // CHUNK
    cps = _chunks_per_step(n_chunks)
    rows = cps * CHUNK
    n_steps = n_chunks // cps

    def body(hq_ref, hf_ref, hi_ref, hg_ref, o_ref, st_ref, dy_ref, lb_ref, g_ref,
             dp_ref, glb_ref, gg_ref, dst):
        i = pl.program_id(0)

        @pl.when(i == 0)
        def _():
            dst[...] = jnp.zeros_like(dst)
            glb_ref[...] = jnp.zeros_like(glb_ref)
            gg_ref[...] = jnp.zeros_like(gg_ref)

        dp_ref[:, :2 * D_RG] = jnp.zeros((rows, 2 * D_RG), BF16)

        def chunk(jj, carry):
            j = cps - 1 - jj
            rs = pl.ds(pl.multiple_of(j * CHUNK, CHUNK), CHUNK)
            chunk_body((n_steps - 1 - i) * cps + j, hq_ref.at[rs, :], hf_ref.at[rs, :], hi_ref.at[rs, :],
                       hg_ref.at[rs, :], o_ref.at[rs, :], st_ref.at[pl.ds(j, 1)], dy_ref.at[rs, :], lb_ref, g_ref,
                       dp_ref.at[rs, pl.ds(2 * D_RG, 4 * D_HG)], glb_ref, gg_ref, dst)
            return carry

        lax.fori_loop(0, cps, chunk, 0, unroll=True)

    def chunk_body(n, hq_ref, hf_ref, hi_ref, hg_ref, o_ref, st_ref, dy_ref, lb_ref, g_ref,
                   dp_ref, glb_ref, gg_ref, dst):
        valid = (n * CHUNK + lax.broadcasted_iota(jnp.int32, (CHUNK, 1), 0)) >= PAD
        hq, hf, v, hg = hq_ref[...], hf_ref[...], hi_ref[...], hg_ref[...]
        lb, sq, q, sf, f, b = _hg_gates(hq, hf, lb_ref, valid)
        k = 1.0 - f
        causal = _causal()
        r_i = lax.broadcasted_iota(jnp.int32, (CHUNK, CHUNK), 0)
        c_i = lax.broadcasted_iota(jnp.int32, (CHUNK, CHUNK), 1)
        causal_t = r_i <= c_i
        is_last = lax.broadcasted_iota(jnp.int32, (CHUNK, 1), 0) == CHUNK - 1
        g_ = g_ref[...]
        db_parts, dq_parts, dk_parts = [], [], []
        gg = jnp.zeros((1, HG_HEAD_DIM), F32)
        heads = [slice(h * HG_HEAD_DIM, (h + 1) * HG_HEAD_DIM) for h in range(HG_HEADS)]

        do_parts = []
        for h, sl in enumerate(heads):
            o = o_ref[:, sl]
            ro = _rms(o)
            no = o * ro
            hgh = hg[:, sl]
            sg = _sigmoid(hgh)
            dyh = dy_ref[:, sl]
            dp_ref[:, 3 * D_HG + h * HG_HEAD_DIM:3 * D_HG + (h + 1) * HG_HEAD_DIM] = (
                dyh * no * g_ * sg * (1.0 + hgh * (1.0 - sg))).astype(BF16)
            dng = dyh * hgh * sg
            gg = gg + jnp.sum(dng * no, axis=0, keepdims=True)
            do_parts.append(_rms_bwd(dng * g_, no, ro))
        do_t = jnp.concatenate(do_parts, axis=1).T.astype(BF16)

        fac = []
        for sl, do in zip(heads, do_parts):
            qh, kh, bh = q[:, sl], k[:, sl], b[:, sl]
            blk, b_last, eb, eq, ekh, ek, q_hat, k_til = _hg_head(qh, kh, bh)
            fac.append(dict(qh=qh, kh=kh, blk=blk, e_last=jnp.exp(b_last), eb=eb, eq=eq, ekh=ekh, ek=ek,
                            q_til=qh * eb, k_hat=kh * ekh, qhb=q_hat.astype(BF16), ktb=k_til.astype(BF16),
                            vb=v[:, sl].astype(BF16), dob=do.astype(BF16)))

        first = []
        for sl, t in zip(heads, fac):
            st_h = st_ref[0, sl, :]
            dst_h = dst[sl, :]
            dstb = dst_h.astype(BF16)
            first.append(dict(
                att_t=_dot_nt(t["ktb"], t["qhb"]), datt=_dot_nt(t["dob"], t["vb"]),
                datt_t=_dot_nt(t["vb"], t["dob"]), dk_hat=_dot(t["vb"], dstb),
                dv=_dot_nt(t["k_hat"].astype(BF16), dstb), dq_til=_dot(t["dob"], st_h.astype(BF16)),
                state=t["e_last"] * jnp.sum(dst_h * st_h, axis=0, keepdims=True)))
            dst[sl, :] = dst_h * t["e_last"] + _dot(do_t[sl, :], t["q_til"].astype(BF16))

        for h, (t, m) in enumerate(zip(fac, first)):
            qh, kh, blk, eb, eq, ekh, ek = t["qh"], t["kh"], t["blk"], t["eb"], t["eq"], t["ekh"], t["ek"]
            q_til, k_hat, qhb, ktb, dob = t["q_til"], t["k_hat"], t["qhb"], t["ktb"], t["dob"]
            dk_hat, dq_til = m["dk_hat"], m["dq_til"]
            dv = m["dv"] + _dot(jnp.where(causal_t, m["att_t"], 0.0).astype(BF16), dob)
            dq_hat = _dot(jnp.where(causal, m["datt"], 0.0).astype(BF16), ktb)
            dk_til = _dot(jnp.where(causal_t, m["datt_t"], 0.0).astype(BF16), qhb)
            db_last = jnp.sum(dk_hat * k_hat, axis=0, keepdims=True) + m["state"]
            dq_sel = dq_hat[:, (N_SUB - 1) * HG_HEAD_DIM:]
            for s in range(N_SUB - 2, -1, -1):
                dq_sel = jnp.where(blk == s, dq_hat[:, s * HG_HEAD_DIM:(s + 1) * HG_HEAD_DIM], dq_sel)
            dq_a = dq_sel * eq
            dk_a = dk_til[:, :HG_HEAD_DIM] * ek[0]
            for s in range(1, N_SUB):
                dk_a = dk_a + dk_til[:, s * HG_HEAD_DIM:(s + 1) * HG_HEAD_DIM] * ek[s]
            db_att = qhb.astype(F32) * dq_hat - ktb.astype(F32) * dk_til
            db = dq_til * q_til - dk_hat * k_hat
            for s in range(N_SUB):
                db = db + db_att[:, s * HG_HEAD_DIM:(s + 1) * HG_HEAD_DIM]
            db_parts.append(jnp.where(is_last, db + db_last, db))
            dq_parts.append(dq_til * eb + dq_a)
            dk_parts.append(dk_hat * ekh + dk_a)
            dp_ref[:, 2 * D_HG + h * HG_HEAD_DIM:2 * D_HG + (h + 1) * HG_HEAD_DIM] = dv.astype(BF16)

        gg_ref[...] += gg
        db = jnp.concatenate(db_parts, axis=1)
        dq = jnp.concatenate(dq_parts, axis=1)
        dk = jnp.concatenate(dk_parts, axis=1)
        dlf = jnp.where(valid, jnp.dot(_tri(False), db, precision=HIGHEST, preferred_element_type=F32), 0.0)
        dp_ref[:, :D_HG] = (dq * sq * (1.0 + hq * (1.0 - sq))).astype(BF16)
        df = dlf / f - dk
        dlb = jnp.sum(df * (1.0 - sf), axis=0, keepdims=True) * lb * (1.0 - lb)
        glb_ref[0:1, :] += dlb
        glb_ref[1:2, :] += -dlb
        dp_ref[:, D_HG:2 * D_HG] = (df * (1.0 - lb) * sf * (1.0 - sf)).astype(BF16)

    rev = lambda j: pl.BlockSpec((rows, D_HG), lambda i: (n_steps - 1 - i, j))
    return pl.pallas_call(
        body, grid=(n_steps,),
        in_specs=[rev(2), rev(3), rev(4), rev(5), rev(0),
                  pl.BlockSpec((cps, D_HG, HG_HEAD_DIM), lambda i: (n_steps - 1 - i, 0, 0)), rev(1),
                  pl.BlockSpec((2, D_HG), lambda i: (0, 0)), pl.BlockSpec((1, HG_HEAD_DIM), lambda i: (0, 0))],
        out_specs=[pl.BlockSpec((rows, D_IN), lambda i: (n_steps - 1 - i, 0)),
                   pl.BlockSpec((2, D_HG), lambda i: (0, 0)), pl.BlockSpec((1, HG_HEAD_DIM), lambda i: (0, 0))],
        out_shape=[jax.ShapeDtypeStruct((T, D_IN), BF16), jax.ShapeDtypeStruct((2, D_HG), F32),
                   jax.ShapeDtypeStruct((1, HG_HEAD_DIM), F32)],
        scratch_shapes=[pltpu.VMEM((D_HG, HG_HEAD_DIM), F32)],
        name="hg_bwd", compiler_params=_params("arbitrary"),
    )(p, p, p, p, o_all, st_all, dy, lbraw, hg_g)


def _in_bwd(dp, w_in, h0, g1, dh1):
    T = h0.shape[0]
    tm = _row_tile(T, 416)
    n_steps = T // tm

    def body(dp_ref, w_ref, h_ref, g_ref, d1_ref, gx_hbm, gmeta_ref, gg_ref, buf, sems):
        i = pl.program_id(0)
        first, later = _window_copies(gx_hbm, buf, sems, tm)
        slot = i % 2

        @pl.when(i == 0)
        def _():
            gg_ref[...] = jnp.zeros_like(gg_ref)

        if n_steps > 2:
            @pl.when(i == 2)
            def _():
                first(False).wait()

            @pl.when(i > 2)
            def _():
                later(i - 2, slot, False).wait()

        du = _dot_nt(dp_ref[...], w_ref[...])
        h0_ = h_ref[...]
        r = _rms(h0_)
        n = h0_ * r
        gg_ref[...] += jnp.sum(du * n, axis=0, keepdims=True)
        dh0 = d1_ref[...] + _rms_bwd(du * g_ref[...], n, r)
        buf[slot] = dh0

        @pl.when(i == 0)
        def _():
            gmeta_ref[...] = dh0[PAD:HEAD, :]
            first(False).start()

        if n_steps > 1:
            @pl.when(i > 0)
            def _():
                later(i, slot, False).start()

        @pl.when(i == n_steps - 1)
        def _():
            if n_steps == 1:
                first(False).wait()
            else:
                if n_steps == 2:
                    first(False).wait()
                else:
                    later(i - 1, 1 - slot, False).wait()
                later(i, slot, False).wait()

    row = lambda n: pl.BlockSpec((tm, n), lambda i: (i, 0))
    return pl.pallas_call(
        body, grid=(n_steps,),
        in_specs=[row(D_IN), pl.BlockSpec((D_MODEL, D_IN), lambda i: (0, 0)),
                  row(D_MODEL), pl.BlockSpec((1, D_MODEL), lambda i: (0, 0)), row(D_MODEL)],
        out_specs=[pl.BlockSpec(memory_space=pl.ANY), pl.BlockSpec((N_META, D_MODEL), lambda i: (0, 0)),
                   pl.BlockSpec((1, D_MODEL), lambda i: (0, 0))],
        out_shape=[jax.ShapeDtypeStruct((T - HEAD, D_MODEL), F32), jax.ShapeDtypeStruct((N_META, D_MODEL), F32),
                   jax.ShapeDtypeStruct((1, D_MODEL), F32)],
        scratch_shapes=[pltpu.VMEM((2, tm, D_MODEL), F32), pltpu.SemaphoreType.DMA((2,))],
        name="in_bwd", compiler_params=_params("arbitrary"),
    )(dp, w_in, h0, g1, dh1)


def _col_tile(cols, target):
    best = None
    for t in range(128, min(cols, target) + 1, 128):
        if cols % t == 0:
            best = t
    assert best is not None, cols
    return best


MXU_DIM = 256


def _mxu_tile(cols, target):
    best = None
    for t in range(MXU_DIM, min(cols, target) + 1, MXU_DIM):
        if cols % t == 0:
            best = t
    assert best is not None, cols
    return best


def _weight_grad(a, b, name):
    T, M = a.shape
    N = b.shape[1]
    tm = _col_tile(M, 1408)
    tn = _mxu_tile(N, 768 if tm <= 1024 else 512)

    def body(a_ref, b_ref, o_ref):
        o_ref[...] = _dot_tn(a_ref[...], b_ref[...])

    return pl.pallas_call(
        body, grid=(M // tm, N // tn),
        in_specs=[pl.BlockSpec((T, tm), lambda m, n: (0, m)), pl.BlockSpec((T, tn), lambda m, n: (0, n))],
        out_specs=pl.BlockSpec((tm, tn), lambda m, n: (m, n)),
        out_shape=jax.ShapeDtypeStruct((M, N), F32),
        name=name, compiler_params=_params("parallel", "parallel"),
    )(a, b)


def _local_step(x, meta, target, w_in_own, w_in, w_out, w_gu, w_down, small, chip, on_ffn_grads=None,
                on_mixer_grads=None):
    wg = _gate_weights(small["w_rgate"], small["w_igate"])
    bg = jnp.concatenate([small["b_rgate"], small["b_igate"]], axis=1)

    p, u, h0 = _in_proj_local(x, meta, small["mix_norm_g"], w_in_own, chip)
    p = _in_proj_rest(u, w_in, p, chip)
    y_rg, hs, xc = _rg_fwd(p, small["conv_w"], small["conv_b"], wg, bg, small["lru_lambda"], small["rg_norm_g"])
    y_hg, o_all, st_all = _hg_fwd(p, small["hg_lower_bound"], small["hg_norm_g"])
    h1, v, yb, gu, act, dh2, dh2b, loss, g_final = _ffn_fwd(
        h0, y_rg, y_hg, w_out, small["ffn_norm_g"], w_gu, w_down, small["final_norm_g"], target)

    g_w_down = _weight_grad(act, dh2b, "grad_w_down")
    dgu, dh1, dh1b, dy, g_ffn = _ffn_bwd(dh2b, gu, w_down, w_gu, h1, small["ffn_norm_g"], dh2, w_out)
    ffn_grads = {"w_gate_up": _weight_grad(v, dgu, "grad_w_gate_up"), "w_down": g_w_down,
                 "w_out": _weight_grad(yb, dh1b, "grad_w_out")}
    stages = on_ffn_grads(ffn_grads) if on_ffn_grads is not None else None
    dp, g_lb, g_hgn = _hg_bwd(p, o_all, st_all, dy, small["hg_lower_bound"], small["hg_norm_g"])
    early = late = None
    if stages is not None:
        chip_sums, send = stages
        sums = chip_sums()
        (dp, dy), sums = lax.optimization_barrier(((dp, dy), sums))
        early = send(sums)
    dp, g_cw, g_cb, g_wgate, g_bg, g_lam, g_rgn = _rg_bwd(
        p, xc, hs, dy, dp, small["conv_w"], small["conv_b"], wg, bg, small["lru_lambda"], small["rg_norm_g"])
    mixer_grads = {"w_in": _weight_grad(u, dp, "grad_w_in")}
    if on_mixer_grads is not None:
        chip_sums, send = on_mixer_grads(mixer_grads)
        sums = chip_sums()
        (dp, dh1), sums = lax.optimization_barrier(((dp, dh1), sums))
        late = send(sums)
    grad_x, g_meta, g_mix = _in_bwd(dp, w_in, h0, small["mix_norm_g"], dh1)

    grads = {
        "w_in": mixer_grads["w_in"], "w_out": ffn_grads["w_out"],
        "w_gate_up": ffn_grads["w_gate_up"], "w_down": ffn_grads["w_down"],
        "meta_tokens": g_meta, "mix_norm_g": g_mix, "conv_w": g_cw, "conv_b": g_cb, "w_gates": g_wgate,
        "b_rgate": g_bg[:, :D_RG], "b_igate": g_bg[:, D_RG:], "lru_lambda": g_lam, "rg_norm_g": g_rgn,
        "hg_lower_bound": g_lb, "hg_norm_g": g_hgn, "ffn_norm_g": g_ffn, "final_norm_g": g_final,
    }
    return loss, grad_x, grads, early, late


ANY = pl.BlockSpec(memory_space=pl.ANY)
HALF = D_MODEL // 2

BIG = {"w_in": (D_MODEL, D_IN // N_CHIPS, True), "w_gate_up": (D_MODEL, 2 * D_FF // N_CHIPS, True),
       "w_out": (D_MODEL // N_CHIPS, D_MODEL, False), "w_down": (D_FF // N_CHIPS, D_MODEL, False)}
BIG_NAMES = tuple(BIG)
N_BIG = len(BIG_NAMES)


def _full_shape(name):
    rows, cols, by_col = BIG[name]
    return (rows, cols * N_CHIPS) if by_col else (rows * N_CHIPS, cols)


def _place():
    return lax.axis_index("x"), lax.axis_index("y"), lax.axis_index("c")


def _chip_of(x, y, r):
    fx, fy = (r + 1) >> 1, (r + 1) & 1
    return (1 - x if fx else x), (1 - y if fy else y)


def _half_of(ref, by_col, half):
    start = pl.multiple_of(half * HALF, 128)
    return ref.at[pl.ds(start, HALF), :] if by_col else ref.at[:, pl.ds(start, HALF)]


def _shard_of(ref, name, chip):
    rows, cols, by_col = BIG[name]
    if by_col:
        return ref.at[:, pl.ds(pl.multiple_of(chip * cols, 128), cols)]
    return ref.at[pl.ds(pl.multiple_of(chip * rows, 16), rows), :]


def _shard_half_of(ref, name, chip, half):
    rows, cols, by_col = BIG[name]
    start = pl.multiple_of(half * HALF, 128)
    if by_col:
        return ref.at[pl.ds(start, HALF), pl.ds(pl.multiple_of(chip * cols, 128), cols)]
    return ref.at[pl.ds(pl.multiple_of(chip * rows, 16), rows), pl.ds(start, HALF)]


def _remote(src, dst, send_sems, recv_sems, k, dev):
    return pltpu.make_async_remote_copy(src_ref=src, dst_ref=dst, send_sem=send_sems.at[k], recv_sem=recv_sems.at[k],
                                        device_id=dev, device_id_type=MESH)


def _place_shards(w, small, chip):
    steps = 4
    ns = len(small)
    in_specs, out_specs = [], []
    for name in BIG_NAMES:
        rows, cols, by_col = BIG[name]
        tr = rows // steps
        in_specs.append(pl.BlockSpec((tr, cols), lambda i, s: (i, 0)))
        if by_col:
            out_specs.append(pl.BlockSpec((tr, cols), lambda i, s: (i, s[0])))
        else:
            out_specs.append(pl.BlockSpec((tr, cols), lambda i, s: (s[0] * steps + i, 0)))

    def body(s_ref, *refs):
        ins, small_in = refs[:N_BIG], refs[N_BIG:N_BIG + ns]
        outs, small_out = refs[N_BIG + ns:2 * N_BIG + ns], refs[2 * N_BIG + ns:2 * (N_BIG + ns)]
        send_sems, recv_sems, local_sems = refs[2 * (N_BIG + ns):]
        i = pl.program_id(0)
        x, y, c = _place()
        chip_ = 2 * x + y
        others = [_chip_of(x, y, r) for r in range(3)]

        def block(a, q):
            cols = small[a].shape[1]
            return small_out[a].at[:, pl.ds(pl.multiple_of(q * cols, 128), cols)]

        def local(a):
            return pltpu.make_async_copy(small_in[a], block(a, chip_), local_sems.at[a])

        def remote(a, r):
            qx, qy = others[r]
            return _remote(small_in[a], block(a, chip_), send_sems, recv_sems, 3 * a + r, (qx, qy, c))

        @pl.when(i == 0)
        def _():
            for a in range(ns):
                local(a).start()
                for r in range(3):
                    remote(a, r).start()

        for a in range(N_BIG):
            outs[a][...] = ins[a][...].astype(BF16)

        @pl.when(i == steps - 1)
        def _():
            for a in range(ns):
                for r, (qx, qy) in enumerate(others):
                    landed = block(a, 2 * qx + qy)
                    _remote(landed, landed, send_sems, recv_sems, 3 * a + r, (qx, qy, c)).wait_recv()
                for r in range(3):
                    remote(a, r).wait_send()
                local(a).wait()

    out = pl.pallas_call(
        body,
        grid_spec=pltpu.PrefetchScalarGridSpec(
            num_scalar_prefetch=1, grid=(steps,), in_specs=in_specs + [ANY] * ns, out_specs=out_specs + [ANY] * ns,
            scratch_shapes=[pltpu.SemaphoreType.DMA((3 * ns,)), pltpu.SemaphoreType.DMA((3 * ns,)),
                            pltpu.SemaphoreType.DMA((ns,))]),
        out_shape=([jax.ShapeDtypeStruct(_full_shape(name), BF16) for name in BIG_NAMES]
                   + [jax.ShapeDtypeStruct((s.shape[0], s.shape[1] * N_CHIPS), F32) for s in small]),
        name="place_shards", compiler_params=_params("arbitrary"),
    )(chip, *[w[name] for name in BIG_NAMES], *small)
    return dict(zip(BIG_NAMES, out[:N_BIG])), list(out[N_BIG:])


def _gather_weights(placed, small, names, label, collective_id):
    n, ns = len(names), len(small)
    hbm = pltpu.MemorySpace.HBM
    outs = [jax.new_ref(placed[nm], memory_space=hbm) for nm in names]
    small_in = [jax.new_ref(s, memory_space=hbm) for s in small]
    small_out = [jax.empty_ref(jax.ShapeDtypeStruct((s.shape[0], s.shape[1] * N_CHIPS), F32), memory_space=hbm)
                 for s in small]
    n_sems = 6 * n + 3 * ns

    @pl.kernel(mesh=plsc.ScalarSubcoreMesh(axis_name="seq", num_cores=1), name=label, out_type=(),
               scratch_types=(pltpu.SemaphoreType.DMA((n_sems,)), pltpu.SemaphoreType.DMA((n_sems,)),
                              pltpu.SemaphoreType.DMA((max(ns, 1),))),
               compiler_params=pltpu.CompilerParams(collective_id=collective_id))
    def launch(send_sems, recv_sems, local_sems):
        x, y, c = _place()
        chip = 2 * x + y
        sibling = (x, y, 1 - c)
        others = [_chip_of(x, y, r) for r in range(3)]
        _handshake([(qx, qy, c) for qx, qy in others] + [sibling])

        def small_block(a, q):
            cols = small[a].shape[1]
            return small_out[a].at[:, pl.ds(pl.multiple_of(q * cols, 128), cols)]

        local = [pltpu.make_async_copy(small_in[a], small_block(a, chip), local_sems.at[a]) for a in range(ns)]
        for cp in local:
            cp.start()

        sends = []
        for a, name in enumerate(names):
            mine = _shard_half_of(outs[a], name, chip, c)
            for r, (qx, qy) in enumerate(others):
                sends.append(_remote(mine, mine, send_sems, recv_sems, 6 * a + r, (qx, qy, c)))
        for a in range(ns):
            for r, (qx, qy) in enumerate(others):
                sends.append(_remote(small_in[a], small_block(a, chip), send_sems, recv_sems,
                                     6 * n + 3 * a + r, (qx, qy, c)))
        for cp in sends:
            cp.start()

        forwards = []
        for a, name in enumerate(names):
            for r, (qx, qy) in enumerate(others):
                landed = _shard_half_of(outs[a], name, 2 * qx + qy, c)
                _remote(landed, landed, send_sems, recv_sems, 6 * a + r, (qx, qy, c)).wait_recv()
                fwd = _remote(landed, landed, send_sems, recv_sems, 6 * a + 3 + r, sibling)
                fwd.start()
                forwards.append(fwd)
        for a in range(ns):
            for r, (qx, qy) in enumerate(others):
                landed = small_block(a, 2 * qx + qy)
                _remote(landed, landed, send_sems, recv_sems, 6 * n + 3 * a + r, (qx, qy, c)).wait_recv()
        for a, name in enumerate(names):
            for r, (qx, qy) in enumerate(others):
                landed = _shard_half_of(outs[a], name, 2 * qx + qy, 1 - c)
                _remote(landed, landed, send_sems, recv_sems, 6 * a + 3 + r, sibling).wait_recv()
        for cp in sends + forwards:
            cp.wait_send()
        for cp in local:
            cp.wait()

    launch()
    return {nm: ref[...] for nm, ref in zip(names, outs)}, [ref[...] for ref in small_out]


def _exchange_halves(grads, names, label, collective_id):
    n = len(names)
    sequencer = collective_id is not None

    def body(*refs):
        ins, outs = refs[:n], refs[n:2 * n]
        send_sems, recv_sems = refs[2 * n:]
        x, y, c = _place()
        if sequencer:
            _handshake([(x, y, 1 - c)])
        copies = []
        for a, name in enumerate(names):
            copies.append(_remote(_half_of(ins[a], BIG[name][2], 1 - c), outs[a], send_sems, recv_sems, a,
                                  (x, y, 1 - c)))
        for cp in copies:
            cp.start()
        for cp in copies:
            cp.wait()

    def half_shape(name):
        r, c_ = _full_shape(name)
        return (HALF, c_) if BIG[name][2] else (r, HALF)

    out_type = tuple(jax.ShapeDtypeStruct(half_shape(nm), F32) for nm in names)
    sems = (pltpu.SemaphoreType.DMA((n,)), pltpu.SemaphoreType.DMA((n,)))
    operands = [grads[nm] for nm in names]
    if sequencer:
        got = pl.kernel(
            body, mesh=plsc.ScalarSubcoreMesh(axis_name="seq", num_cores=1), name=label, out_type=out_type,
            scratch_types=sems, compiler_params=pltpu.CompilerParams(collective_id=collective_id),
        )(*operands)
    else:
        got = pl.pallas_call(
            body, in_specs=[ANY] * n, out_specs=[ANY] * n, out_shape=list(out_type), scratch_shapes=list(sems),
            name=label,
        )(*operands)
    return dict(zip(names, got))


def _chip_sum(grads, got, names, core, label):
    n = len(names)
    steps = 4
    g_specs, blks = [], []
    for name in names:
        rows, cols = got[name].shape
        tr = rows // steps
        if BIG[name][2]:
            g_specs.append(pl.BlockSpec((tr, cols), lambda i, s: (s[0] * steps + i, 0)))
        else:
            g_specs.append(pl.BlockSpec((tr, HALF), lambda i, s: (i, s[0])))
        blks.append(pl.BlockSpec((tr, cols), lambda i, s: (i, 0)))

    def body(s_ref, *refs):
        for a in range(n):
            t = refs[a][...] + refs[n + a][...]
            refs[2 * n + a][...] = t
            refs[3 * n + a][...] = t.astype(BF16)

    out = pl.pallas_call(
        body,
        grid_spec=pltpu.PrefetchScalarGridSpec(num_scalar_prefetch=1, grid=(steps,), in_specs=g_specs + blks,
                                               out_specs=blks + blks),
        out_shape=([jax.ShapeDtypeStruct(got[nm].shape, F32) for nm in names]
                   + [jax.ShapeDtypeStruct(got[nm].shape, BF16) for nm in names]),
        name=label, compiler_params=_params("parallel"),
    )(core, *[grads[nm] for nm in names], *[got[nm] for nm in names])
    return {nm: (out[a], out[n + a]) for a, nm in enumerate(names)}


def _piece_shape(name):
    rows, cols, by_col = BIG[name]
    return (HALF, cols) if by_col else (rows, HALF)


def _handshake(peers):
    barrier = pltpu.get_barrier_semaphore()
    for peer in peers:
        pl.semaphore_signal(barrier, inc=1, device_id=peer, device_id_type=MESH)
    pl.semaphore_wait(barrier, len(peers))


def _send_chip_sums(sums, names, label, collective_id):
    n = len(names)

    def body(*refs):
        ins, outs = refs[:n], refs[n:2 * n]
        send_sems, recv_sems = refs[2 * n:]
        x, y, c = _place()
        others = [_chip_of(x, y, r) for r in range(3)]
        _handshake([(qx, qy, c) for qx, qy in others])
        copies = []
        for a, name in enumerate(names):
            for r, (qx, qy) in enumerate(others):
                copies.append(_remote(_shard_of(ins[a], name, 2 * qx + qy), outs[a].at[r], send_sems, recv_sems,
                                      3 * a + r, (qx, qy, c)))
        for cp in copies:
            cp.start()
        for cp in copies:
            cp.wait()

    return pl.kernel(
        body, mesh=plsc.ScalarSubcoreMesh(axis_name="seq", num_cores=1), name=label,
        out_type=tuple(jax.ShapeDtypeStruct((3,) + _piece_shape(nm), BF16) for nm in names),
        scratch_types=(pltpu.SemaphoreType.DMA((3 * n,)), pltpu.SemaphoreType.DMA((3 * n,))),
        compiler_params=pltpu.CompilerParams(collective_id=collective_id),
    )(*[sums[nm] for nm in names])


def _total(parts, chip_core):
    steps = 2
    in_specs, out_specs, operands = [], [], []
    for name in BIG_NAMES:
        by_col = BIG[name][2]
        pr, pc = _piece_shape(name)
        tr = pr // steps
        if by_col:
            in_specs.append(pl.BlockSpec((tr, pc), lambda i, s: (i, s[0])))
            out_specs.append(pl.BlockSpec((tr, pc), lambda i, s: (s[1] * steps + i, 0)))
        else:
            in_specs.append(pl.BlockSpec((tr, pc), lambda i, s: (s[0] * steps + i, 0)))
            out_specs.append(pl.BlockSpec((tr, pc), lambda i, s: (i, s[1])))
        for r in range(3):
            in_specs.append(pl.BlockSpec((None, tr, pc), lambda i, s, r=r: (r, i, 0)))
        own, got = parts[name]
        operands += [own, got, got, got]

    def body(s_ref, *refs):
        for a in range(N_BIG):
            o_ref, a_ref, b_ref, c_ref = refs[4 * a:4 * a + 4]
            refs[4 * N_BIG + a][...] = (((o_ref[...] + a_ref[...].astype(F32)) + b_ref[...].astype(F32))
                                        + c_ref[...].astype(F32))

    totals = pl.pallas_call(
        body,
        grid_spec=pltpu.PrefetchScalarGridSpec(num_scalar_prefetch=1, grid=(steps,), in_specs=in_specs,
                                               out_specs=out_specs),
        out_shape=[jax.ShapeDtypeStruct(BIG[name][:2], F32) for name in BIG_NAMES],
        name="totals", compiler_params=_params("parallel"),
    )(chip_core, *operands)
    return dict(zip(BIG_NAMES, totals))


VEC_ROWS = 32
VEC_ROW = {"mix_norm_g": 0, "conv_b": 1, "b_rgate": 2, "b_igate": 3, "lru_lambda": 4, "rg_norm_g": 5,
           "hg_lower_bound": 6, "hg_norm_g": 8, "ffn_norm_g": 9, "final_norm_g": 10, "loss": 11,
           "conv_w": 12, "meta_tokens": 16}
N_DEV = 8


def _all_reduce_small(pieces, gates, totals):
    names = list(pieces)
    n_small = 10
    hv, hg = VEC_ROWS // 2, gates.shape[0] // 2

    def body(*refs):
        ins = refs[:len(names)]
        g_ref = refs[len(names)]
        vec_ref, gsum_ref = refs[len(names) + 1 + N_BIG:len(names) + 3 + N_BIG]
        big = refs[len(names) + 3 + N_BIG:len(names) + 3 + 2 * N_BIG]
        (mine_v, sib_v, sib_g, chip_v, chip_g, got_v, got_g, send_sems, recv_sems) = refs[len(names) + 3 + 2 * N_BIG:]
        x, y, c = _place()
        chip = 2 * x + y
        sibling = (x, y, 1 - c)
        share = []
        for a, name in enumerate(BIG_NAMES):
            half = _half_of(big[a], BIG[name][2], c)
            share.append(_remote(half, half, send_sems, recv_sems, n_small + a, sibling))
        for cp in share:
            cp.start()
        mine_v[...] = jnp.zeros_like(mine_v)
        for name, ref in zip(names, ins):
            nr, w = ref.shape
            mine_v[VEC_ROW[name]:VEC_ROW[name] + nr, 0:w] = ref[...]

        swap = [_remote(mine_v, sib_v, send_sems, recv_sems, 0, sibling),
                _remote(g_ref, sib_g, send_sems, recv_sems, 1, sibling)]
        for cp in swap:
            cp.start()
        for cp in swap:
            cp.wait()
        chip_v[...] = mine_v[...] + sib_v[...]
        chip_g[...] = g_ref[...] + sib_g[...]

        rows_v = pl.ds(pl.multiple_of(c * hv, 8), hv)
        rows_g = pl.ds(pl.multiple_of(c * hg, 8), hg)
        got_v[chip] = chip_v[rows_v, :]
        got_g[chip] = chip_g[rows_g, :]
        sends = []
        for r in range(3):
            qx, qy = _chip_of(x, y, r)
            sends.append(_remote(chip_v.at[rows_v, :], got_v.at[chip], send_sems, recv_sems, 2 + r, (qx, qy, c)))
            sends.append(_remote(chip_g.at[rows_g, :], got_g.at[chip], send_sems, recv_sems, 5 + r, (qx, qy, c)))
        for cp in sends:
            cp.start()
        for cp in sends:
            cp.wait()
        vec_ref[rows_v, :] = ((got_v[0] + got_v[1]) + got_v[2]) + got_v[3]
        gsum_ref[rows_g, :] = ((got_g[0] + got_g[1]) + got_g[2]) + got_g[3]

        back = [_remote(vec_ref.at[rows_v, :], vec_ref.at[rows_v, :], send_sems, recv_sems, 8, sibling),
                _remote(gsum_ref.at[rows_g, :], gsum_ref.at[rows_g, :], send_sems, recv_sems, 9, sibling)]
        for cp in back:
            cp.start()
        theirs_v = vec_ref.at[pl.ds(pl.multiple_of((1 - c) * hv, 8), hv), :]
        theirs_g = gsum_ref.at[pl.ds(pl.multiple_of((1 - c) * hg, 8), hg), :]
        _remote(theirs_v, theirs_v, send_sems, recv_sems, 8, sibling).wait_recv()
        _remote(theirs_g, theirs_g, send_sems, recv_sems, 9, sibling).wait_recv()
        for cp in back:
            cp.wait_send()
        for a, name in enumerate(BIG_NAMES):
            theirs = _half_of(big[a], BIG[name][2], 1 - c)
            _remote(theirs, theirs, send_sems, recv_sems, n_small + a, sibling).wait_recv()
        for cp in share:
            cp.wait_send()

    vmem = pl.BlockSpec(memory_space=pltpu.VMEM)
    n_sems = n_small + N_BIG
    out = pl.pallas_call(
        body, in_specs=[vmem] * (len(names) + 1) + [ANY] * N_BIG, out_specs=[vmem, vmem] + [ANY] * N_BIG,
        out_shape=([jax.ShapeDtypeStruct((VEC_ROWS, D_MODEL), F32), jax.ShapeDtypeStruct(gates.shape, F32)]
                   + [jax.ShapeDtypeStruct(BIG[n][:2], F32) for n in BIG_NAMES]),
        input_output_aliases={len(names) + 1 + a: 2 + a for a in range(N_BIG)},
        scratch_shapes=[pltpu.VMEM((VEC_ROWS, D_MODEL), F32), pltpu.VMEM((VEC_ROWS, D_MODEL), F32),
                        pltpu.VMEM(gates.shape, F32), pltpu.VMEM((VEC_ROWS, D_MODEL), F32),
                        pltpu.VMEM(gates.shape, F32), pltpu.VMEM((N_CHIPS, hv, D_MODEL), F32),
                        pltpu.VMEM((N_CHIPS, hg) + gates.shape[1:], F32),
                        pltpu.SemaphoreType.DMA((n_sems,)), pltpu.SemaphoreType.DMA((n_sems,))],
        name="all_reduce_small",
    )(*[pieces[n] for n in names], gates, *[totals[n] for n in BIG_NAMES])
    return out[0], out[1], dict(zip(BIG_NAMES, out[2:]))


def _adamw_math(w, g, m, v):
    m = ADAM_B1 * m + (1.0 - ADAM_B1) * g
    v = ADAM_B2 * v + (1.0 - ADAM_B2) * (g * g)
    m_hat = m / (1.0 - ADAM_B1 ** ADAM_STEP)
    v_hat = v / (1.0 - ADAM_B2 ** ADAM_STEP)
    delta = -ADAM_LR * (m_hat / (jnp.sqrt(v_hat) + ADAM_EPS) + ADAM_WD * w)
    return delta, m, v


def _adamw_big(w, g, m, v):
    steps = 8
    blks = []
    for name in BIG_NAMES:
        rows, cols, _ = BIG[name]
        blks.append(pl.BlockSpec((rows // steps, cols), lambda i: (i, 0)))

    def body(*refs):
        ins, outs = refs[:4 * N_BIG], refs[4 * N_BIG:]
        for a in range(N_BIG):
            w_ref, g_ref, m_ref, v_ref = (ins[k * N_BIG + a] for k in range(4))
            d, nm, nv = _adamw_math(w_ref[...], g_ref[...], m_ref[...], v_ref[...])
            outs[a][...] = d
            outs[N_BIG + a][...] = nm
            outs[2 * N_BIG + a][...] = nv

    shapes = [jax.ShapeDtypeStruct(BIG[name][:2], F32) for name in BIG_NAMES]
    out = pl.pallas_call(
        body, grid=(steps,), in_specs=blks * 4, out_specs=blks * 3, out_shape=shapes * 3,
        name="adamw_big", compiler_params=_params("parallel"),
    )(*[t[name] for t in (w, g, m, v) for name in BIG_NAMES])
    return {name: (out[a], out[N_BIG + a], out[2 * N_BIG + a]) for a, name in enumerate(BIG_NAMES)}


SMALL = {"meta_tokens": (N_META, D_MODEL // N_CHIPS), "mix_norm_g": (1, D_MODEL), "conv_w": (CONV_W, D_RG // N_CHIPS),
         "conv_b": (1, D_RG), "w_rgate": (D_RG, RG_HEAD_DIM), "b_rgate": (1, D_RG), "w_igate": (D_RG, RG_HEAD_DIM),
         "b_igate": (1, D_RG), "lru_lambda": (1, D_RG), "rg_norm_g": (1, D_RG), "hg_lower_bound": (2, D_HG),
         "hg_norm_g": (1, HG_HEAD_DIM), "ffn_norm_g": (1, D_MODEL), "final_norm_g": (1, D_MODEL)}
SMALL_NAMES = tuple(SMALL)
SHARDED_SMALL = ("meta_tokens", "conv_w")


def _adamw_small(vec, gates, w, m, v):
    n = len(SMALL_NAMES)

    def body(*refs):
        vec_ref, gates_ref = refs[:2]
        w_refs, m_refs, v_refs = refs[2:2 + n], refs[2 + n:2 + 2 * n], refs[2 + 2 * n:2 + 3 * n]
        outs = refs[2 + 3 * n:]
        loss_ref = outs[0]
        x, y, _ = _place()
        chip = 2 * x + y
        loss_ref[...] = vec_ref[VEC_ROW["loss"]:VEC_ROW["loss"] + 1, 0:1]

        def update(k, g):
            g_ref, d_ref, nm_ref, nv_ref = outs[1 + 4 * k:5 + 4 * k]
            g_ref[...] = g
            d_ref[...], nm_ref[...], nv_ref[...] = _adamw_math(w_refs[k][...], g, m_refs[k][...], v_refs[k][...])

        for k, name in enumerate(SMALL_NAMES):
            nr, w_ = SMALL[name]
            if name == "w_rgate":
                update(k, gates_ref[0:D_RG, :])
            elif name == "w_igate":
                update(k, gates_ref[D_RG:2 * D_RG, :])
            elif name in SHARDED_SMALL:
                r0 = VEC_ROW[name]
                for q in range(N_CHIPS):
                    @pl.when(chip == q)
                    def _(k=k, r0=r0, nr=nr, w_=w_, q=q):
                        update(k, vec_ref[r0:r0 + nr, q * w_:(q + 1) * w_])
            else:
                r0 = VEC_ROW[name]
                update(k, vec_ref[r0:r0 + nr, 0:w_])

    vmem = pl.BlockSpec(memory_space=pltpu.VMEM)
    out_shape = [jax.ShapeDtypeStruct((1, 1), F32)]
    for name in SMALL_NAMES:
        out_shape += [jax.ShapeDtypeStruct(SMALL[name], F32)] * 4
    outs = pl.pallas_call(
        body, in_specs=[vmem] * (2 + 3 * n), out_specs=[vmem] * len(out_shape), out_shape=out_shape,
        name="adamw_small",
    )(vec, gates, *[w[k] for k in SMALL_NAMES], *[m[k] for k in SMALL_NAMES], *[v[k] for k in SMALL_NAMES])
    loss = outs[0]
    res = {name: tuple(outs[1 + 4 * k:5 + 4 * k]) for k, name in enumerate(SMALL_NAMES)}
    return loss, res


WEIGHT_NAMES = ("meta_tokens", "mix_norm_g", "w_in", "conv_w", "conv_b", "w_rgate", "b_rgate", "w_igate", "b_igate",
                "lru_lambda", "rg_norm_g", "hg_lower_bound", "hg_norm_g", "w_out", "ffn_norm_g", "w_gate_up", "w_down",
                "final_norm_g")


def _to_2d(name, a):
    if name in BIG:
        return a.reshape(BIG[name][:2])
    return a.reshape(SMALL[name])


def kernel(x, meta_tokens, mix_norm_g, w_in, conv_w, conv_b, w_rgate, b_rgate, w_igate, b_igate, lru_lambda, rg_norm_g, hg_lower_bound, hg_norm_g, w_out, ffn_norm_g, w_gate_up, w_down, final_norm_g, loss_target, m_meta_tokens, m_mix_norm_g, m_w_in, m_conv_w, m_conv_b, m_w_rgate, m_b_rgate, m_w_igate, m_b_igate, m_lru_lambda, m_rg_norm_g, m_hg_lower_bound, m_hg_norm_g, m_w_out, m_ffn_norm_g, m_w_gate_up, m_w_down, m_final_norm_g, v_meta_tokens, v_mix_norm_g, v_w_in, v_conv_w, v_conv_b, v_w_rgate, v_b_rgate, v_w_igate, v_b_igate, v_lru_lambda, v_rg_norm_g, v_hg_lower_bound, v_hg_norm_g, v_w_out, v_ffn_norm_g, v_w_gate_up, v_w_down, v_final_norm_g):
    w_raw = dict(zip(WEIGHT_NAMES, (meta_tokens, mix_norm_g, w_in, conv_w, conv_b, w_rgate, b_rgate, w_igate, b_igate,
                                    lru_lambda, rg_norm_g, hg_lower_bound, hg_norm_g, w_out, ffn_norm_g, w_gate_up,
                                    w_down, final_norm_g)))
    m_raw = dict(zip(WEIGHT_NAMES, (m_meta_tokens, m_mix_norm_g, m_w_in, m_conv_w, m_conv_b, m_w_rgate, m_b_rgate,
                                    m_w_igate, m_b_igate, m_lru_lambda, m_rg_norm_g, m_hg_lower_bound, m_hg_norm_g,
                                    m_w_out, m_ffn_norm_g, m_w_gate_up, m_w_down, m_final_norm_g)))
    v_raw = dict(zip(WEIGHT_NAMES, (v_meta_tokens, v_mix_norm_g, v_w_in, v_conv_w, v_conv_b, v_w_rgate, v_b_rgate,
                                    v_w_igate, v_b_igate, v_lru_lambda, v_rg_norm_g, v_hg_lower_bound, v_hg_norm_g,
                                    v_w_out, v_ffn_norm_g, v_w_gate_up, v_w_down, v_final_norm_g)))
    w = {k: _to_2d(k, a) for k, a in w_raw.items()}
    m = {k: _to_2d(k, a) for k, a in m_raw.items()}
    v = {k: _to_2d(k, a) for k, a in v_raw.items()}

    x_i, y_i, c_i = _place()
    core = jnp.reshape(c_i, (1,)).astype(jnp.int32)
    chip = jnp.reshape(2 * x_i + y_i, (1,)).astype(jnp.int32)
    chip_core = jnp.concatenate([chip, core])

    placed, (meta_full, cw_full) = _place_shards(w, [w["meta_tokens"], w["conv_w"]], chip)
    first, _ = _gather_weights(placed, [], ("w_in",), "gather_first", 1)
    rest, _ = _gather_weights(placed, [], ("w_out", "w_gate_up", "w_down"), "gather_rest", 2)
    full = {**first, **rest}

    seq = x.shape[1]
    small ={k: w[k] for k in SMALL_NAMES if k not in SHARDED_SMALL}
    small["conv_w"] = cw_full

    def reduce_to_chips(grads, names, tag, collective_ids):
        got = _exchange_halves(grads, names, "exchange_halves_" + tag, collective_ids[0])

        def chip_sums():
            return _chip_sum(grads, got, names, core, "chip_sum_" + tag)

        def send(sums):
            arrived = _send_chip_sums({n: sums[n][1] for n in names}, names, "send_chip_sums_" + tag,
                                      collective_ids[1])
            return {n: (sums[n][0], a) for n, a in zip(names, arrived)}

        return chip_sums, send

    ffn_names, mixer_names = ("w_gate_up", "w_down", "w_out"), ("w_in",)
    loss, grad_x, grads, parts, parts_mixer = _local_step(
        x.reshape(seq, D_MODEL), meta_full, loss_target.reshape(seq, D_MODEL),
        w["w_in"], full["w_in"], full["w_out"], full["w_gate_up"], full["w_down"], small, chip,
        on_ffn_grads=lambda g: reduce_to_chips(g, ffn_names, "ffn", (3, 4)),
        on_mixer_grads=lambda g: reduce_to_chips(g, mixer_names, "mixer", (None, 5)))
    parts.update(parts_mixer)
    totals = _total(parts, chip_core)
    pieces = {k: grads[k] for k in VEC_ROW if k != "loss"}
    pieces["loss"] = loss
    vec, gates, g_big = _all_reduce_small(pieces, grads["w_gates"], totals)
    loss_sum, res = _adamw_small(vec, gates, w, m, v)
    updates = _adamw_big(w, g_big, m, v)
    for n in BIG_NAMES:
        res[n] = (g_big[n],) + updates[n]

    out = [loss_sum.reshape(()), grad_x.reshape(1, seq, D_MODEL)]
    for j in range(4):
        out += [res[n][j].reshape(w_raw[n].shape) for n in WEIGHT_NAMES]
    return tuple(out)
```

```python
import math

import jax
import jax.numpy as jnp
from jax import lax
from jax.experimental import pallas as pl
from jax.experimental.pallas import tpu as pltpu
from jax.experimental.pallas import tpu_sc as plsc

F32 = jnp.float32
BF16 = jnp.bfloat16
HIGHEST = lax.Precision.HIGHEST
MESH = pl.DeviceIdType.MESH

D_MODEL = 1024
D_RG = 512
RG_HEAD_DIM = 64
D_HG = 512
HG_HEAD_DIM = 128
HG_HEADS = 4
CHUNK = 64
SUB = 16
N_SUB = CHUNK // SUB
N_META = 16
PAD = CHUNK - N_META
D_IN = 3072
D_FF = 2816
CONV_W = 4
LRU_C = 8.0
EPS = 1e-6
EXP_CLAMP = 80.0
GELU_C = math.sqrt(2.0 / math.pi)
GELU_A = 0.044715
N_CHIPS = 4

ADAM_LR = 0.001
ADAM_B1 = 0.9
ADAM_B2 = 0.999
ADAM_EPS = 1e-08
ADAM_WD = 0.01
ADAM_STEP = 10

VMEM_LIMIT = 56 * 1024 * 1024


def _params(*sem):
    return pltpu.CompilerParams(dimension_semantics=sem, vmem_limit_bytes=VMEM_LIMIT)


def _row_tile(rows, target):
    best = None
    for t in range(16, min(rows, target) + 1, 16):
        if rows % t == 0:
            best = t
    assert best is not None, rows
    return best


def _sigmoid(x):
    return 0.5 * jnp.tanh(0.5 * x) + 0.5


def _dot(a, b):
    return jnp.dot(a, b, preferred_element_type=F32)


def _dot_nt(a, b):
    return lax.dot_general(a, b, (((1,), (1,)), ((), ())), preferred_element_type=F32)


def _dot_tn(a, b):
    return lax.dot_general(a, b, (((0,), (0,)), ((), ())), preferred_element_type=F32)


def _rms(x):
    return lax.rsqrt(jnp.mean(x * x, axis=-1, keepdims=True) + EPS)


def _rms_bwd(dn, n, r):
    return r * (dn - n * jnp.mean(dn * n, axis=-1, keepdims=True))


def _gelu_parts(x):
    t = jnp.tanh(GELU_C * (x + GELU_A * x * x * x))
    g = 0.5 * x * (1.0 + t)
    dg = 0.5 * (1.0 + t) + 0.5 * x * (1.0 - t * t) * GELU_C * (1.0 + 3.0 * GELU_A * x * x)
    return g, dg


def _softplus_neg(lam):
    e = jnp.exp(-jnp.abs(lam))
    w = 1.0 + e
    log1p = jnp.where(w == 1.0, e, jnp.log(w) * e / (w - 1.0))
    return jnp.maximum(-lam, 0.0) + log1p


def _head_mask():
    r = lax.broadcasted_iota(jnp.int32, (D_RG, D_RG), 0) // RG_HEAD_DIM
    c = lax.broadcasted_iota(jnp.int32, (D_RG, D_RG), 1) // RG_HEAD_DIM
    return r == c


def _head_fold():
    r = lax.broadcasted_iota(jnp.int32, (D_RG, RG_HEAD_DIM), 0) % RG_HEAD_DIM
    c = lax.broadcasted_iota(jnp.int32, (D_RG, RG_HEAD_DIM), 1)
    return (r == c).astype(F32)


def _gate_weights(w_r, w_i):
    def body(wr_ref, wi_ref, o_ref):
        fold = _head_fold()
        mask = _head_mask()
        for k, ref in enumerate((wr_ref, wi_ref)):
            full = lax.dot_general(ref[...], fold, (((1,), (1,)), ((), ())),
                                   precision=HIGHEST, preferred_element_type=F32)
            o_ref[:, k * D_RG:(k + 1) * D_RG] = jnp.where(mask, full, 0.0).astype(BF16)

    return pl.pallas_call(
        body, out_shape=jax.ShapeDtypeStruct((D_RG, 2 * D_RG), BF16), name="gate_weights",
    )(w_r, w_i)


HEAD = PAD + N_META


def _window_copies(seq_hbm, buf, sems, tm):
    def first(to_vmem):
        seq, vm = seq_hbm.at[pl.ds(0, tm - HEAD)], buf.at[0, pl.ds(HEAD, tm - HEAD)]
        return pltpu.make_async_copy(seq, vm, sems.at[0]) if to_vmem else pltpu.make_async_copy(vm, seq, sems.at[0])

    def later(j, slot, to_vmem):
        seq, vm = seq_hbm.at[pl.ds(pl.multiple_of(j * tm - HEAD, 8), tm)], buf.at[slot]
        if to_vmem:
            return pltpu.make_async_copy(seq, vm, sems.at[slot])
        return pltpu.make_async_copy(vm, seq, sems.at[slot])

    return first, later


def _fetch_window(seq_hbm, buf, sems, i, n_steps, tm):
    first, later = _window_copies(seq_hbm, buf, sems, tm)
    slot = i % 2

    @pl.when(i == 0)
    def _():
        first(True).start()

    if n_steps > 1:
        @pl.when(i + 1 < n_steps)
        def _():
            later(i + 1, 1 - slot, True).start()

    @pl.when(i == 0)
    def _():
        first(True).wait()

    if n_steps > 1:
        @pl.when(i > 0)
        def _():
            later(i, slot, True).wait()

    return slot


def _in_proj_local(x, meta, g1, w_own, chip):
    T = x.shape[0] + HEAD
    tm = _row_tile(T, 832)
    n_steps = T // tm
    cols = BIG["w_in"][1]

    def body(s_ref, x_hbm, meta_ref, g_ref, w_ref, p_ref, u_ref, h_ref, buf, sems, wb):
        i = pl.program_id(0)
        slot = _fetch_window(x_hbm, buf, sems, i, n_steps, tm)

        @pl.when(i == 0)
        def _():
            buf[0, 0:PAD, :] = jnp.zeros((PAD, D_MODEL), F32)
            buf[0, PAD:HEAD, :] = meta_ref[...]
            wb[...] = w_ref[...].astype(BF16)

        h = buf[slot]
        h_ref[...] = h
        u = (h * _rms(h) * g_ref[...]).astype(BF16)
        u_ref[...] = u
        p_ref[...] = _dot(u, wb[...])

    return pl.pallas_call(
        body,
        grid_spec=pltpu.PrefetchScalarGridSpec(
            num_scalar_prefetch=1, grid=(n_steps,),
            in_specs=[pl.BlockSpec(memory_space=pl.ANY),
                      pl.BlockSpec((N_META, D_MODEL), lambda i, s: (0, 0)),
                      pl.BlockSpec((1, D_MODEL), lambda i, s: (0, 0)),
                      pl.BlockSpec((D_MODEL, cols), lambda i, s: (0, 0))],
            out_specs=[pl.BlockSpec((tm, cols), lambda i, s: (i, s[0])),
                       pl.BlockSpec((tm, D_MODEL), lambda i, s: (i, 0)),
                       pl.BlockSpec((tm, D_MODEL), lambda i, s: (i, 0))],
            scratch_shapes=[pltpu.VMEM((2, tm, D_MODEL), F32), pltpu.SemaphoreType.DMA((2,)),
                            pltpu.VMEM((D_MODEL, cols), BF16)]),
        out_shape=[jax.ShapeDtypeStruct((T, D_IN), F32), jax.ShapeDtypeStruct((T, D_MODEL), BF16),
                   jax.ShapeDtypeStruct((T, D_MODEL), F32)],
        name="in_proj_local", compiler_params=_params("arbitrary"),
    )(chip, x, meta, g1, w_own)


def _in_proj_rest(u, w_in, p, chip):
    T = u.shape[0]
    tm = _row_tile(T, 2080)
    cols = BIG["w_in"][1]
    block = lambda j, s: (s[0] + 1 + j) % N_CHIPS

    def body(s_ref, u_ref, w_ref, p_in_ref, p_ref):
        p_ref[...] = _dot(u_ref[...], w_ref[...])

    return pl.pallas_call(
        body,
        grid_spec=pltpu.PrefetchScalarGridSpec(
            num_scalar_prefetch=1, grid=(N_CHIPS - 1, T // tm),
            in_specs=[pl.BlockSpec((tm, D_MODEL), lambda j, i, s: (i, 0)),
                      pl.BlockSpec((D_MODEL, cols), lambda j, i, s: (0, block(j, s))), ANY],
            out_specs=pl.BlockSpec((tm, cols), lambda j, i, s: (i, block(j, s)))),
        out_shape=jax.ShapeDtypeStruct((T, D_IN), F32),
        input_output_aliases={3: 0},
        name="in_proj_rest", compiler_params=_params("arbitrary", "arbitrary"),
    )(chip, u, w_in, p)


def _scan_block_fwd(A, B, rowi):
    for d in (1, 2, 4):
        a_sh = pltpu.roll(A, d, axis=0)
        b_sh = pltpu.roll(B, d, axis=0)
        m = rowi >= d
        B = jnp.where(m, A * b_sh + B, B)
        A = jnp.where(m, A * a_sh, A)
    return A, B


def _scan_block_bwd(A, B, rowi):
    for d in (1, 2, 4):
        a_sh = pltpu.roll(A, 8 - d, axis=0)
        b_sh = pltpu.roll(B, 8 - d, axis=0)
        m = rowi < 8 - d
        B = jnp.where(m, A * b_sh + B, B)
        A = jnp.where(m, A * a_sh, A)
    return A, B


def _rg_gates(xc, w_ref, bg_ref, lam):
    pre = _dot(xc.astype(BF16), w_ref[...]) + bg_ref[...]
    r = _sigmoid(pre[:, :D_RG])
    ig = _sigmoid(pre[:, D_RG:])
    sp = _softplus_neg(lam)
    la = -LRU_C * sp * r
    a = jnp.exp(la)
    th = jnp.tanh(la)
    u = 1.0 - th
    rc = pl.reciprocal(u, approx=True)
    rc = rc * (2.0 - u * rc)
    rc = rc * (2.0 - u * rc)
    m2 = -2.0 * th * rc
    inv_m = lax.rsqrt(jnp.maximum(m2, 1e-30))
    return r, ig, sp, a, m2 * inv_m, inv_m


def _conv(ext, cw_ref, cb_ref, tm):
    xc = cb_ref[...] + cw_ref[0:1, :] * ext[8 - 3:8 - 3 + tm, :]
    for j in range(1, CONV_W):
        xc = xc + cw_ref[j:j + 1, :] * ext[8 - 3 + j:8 - 3 + j + tm, :]
    return xc


def _scan_unroll(blocks):
    return 4 if blocks % 4 == 0 else 2 if blocks % 2 == 0 else 1


def _rg_fwd(p, cw, cb, wg, bg, lam, rg_g):
    T = p.shape[0]
    tm = _row_tile(T, 832)
    unroll = _scan_unroll(tm // 8)

    def body(xg_ref, cw_ref, cb_ref, w_ref, bg_ref, lam_ref, g_ref, y_ref, h_ref, xc_ref, ext, a_s, b_s, carry):
        i = pl.program_id(0)

        @pl.when(i == 0)
        def _():
            ext[0:8, :] = jnp.zeros((8, D_RG), F32)
            carry[...] = jnp.zeros((1, D_RG), F32)

        ext[8:8 + tm, :] = xg_ref[:, :D_RG]
        xc = _conv(ext, cw_ref, cb_ref, tm)
        xc_ref[...] = xc
        r, ig, sp, a, m, _ = _rg_gates(xc, w_ref, bg_ref, lam_ref[...])
        row = i * tm + lax.broadcasted_iota(jnp.int32, (tm, 1), 0)
        a_s[...] = a
        b_s[...] = jnp.where(row >= PAD, m * ig * xc, 0.0)
        rowi = lax.broadcasted_iota(jnp.int32, (8, D_RG), 0)

        def blk(j, c):
            for u in range(unroll):
                o = pl.multiple_of((j * unroll + u) * 8, 8)
                A, B = _scan_block_fwd(a_s[pl.ds(o, 8), :], b_s[pl.ds(o, 8), :], rowi)
                h = B + A * c
                h_ref[pl.ds(o, 8), :] = h
                c = h[7:8, :]
            return c

        carry[...] = lax.fori_loop(0, tm // (8 * unroll), blk, carry[...])
        ext[0:8, :] = ext[tm:tm + 8, :]
        g, _ = _gelu_parts(xg_ref[:, D_RG:])
        yy = g * h_ref[...]
        y_ref[...] = (yy * _rms(yy) * g_ref[...]).astype(BF16)

    vec = lambda n: pl.BlockSpec((1, n), lambda i: (0, 0))
    return pl.pallas_call(
        body, grid=(T // tm,),
        in_specs=[pl.BlockSpec((tm, 2 * D_RG), lambda i: (i, 0)),
                  pl.BlockSpec((CONV_W, D_RG), lambda i: (0, 0)), vec(D_RG),
                  pl.BlockSpec((D_RG, 2 * D_RG), lambda i: (0, 0)), vec(2 * D_RG), vec(D_RG), vec(D_RG)],
        out_specs=[pl.BlockSpec((tm, D_RG), lambda i: (i, 0))] * 3,
        out_shape=[jax.ShapeDtypeStruct((T, D_RG), BF16), jax.ShapeDtypeStruct((T, D_RG), F32),
                   jax.ShapeDtypeStruct((T, D_RG), F32)],
        scratch_shapes=[pltpu.VMEM((tm + 8, D_RG), F32), pltpu.VMEM((tm, D_RG), F32),
                        pltpu.VMEM((tm, D_RG), F32), pltpu.VMEM((1, D_RG), F32)],
        name="rg_fwd", compiler_params=_params("arbitrary"),
    )(p, cw, cb, wg, bg, lam, rg_g)


def _tri(lower):
    r = lax.broadcasted_iota(jnp.int32, (CHUNK, CHUNK), 0)
    c = lax.broadcasted_iota(jnp.int32, (CHUNK, CHUNK), 1)
    return ((c <= r) if lower else (c >= r)).astype(F32)


def _hg_gates(hq, hf, lbraw_ref, valid):
    lb = _sigmoid(lbraw_ref[0:1, :] - lbraw_ref[1:2, :])
    sq = _sigmoid(hq)
    q = hq * sq
    sf = _sigmoid(hf)
    f = lb + (1.0 - lb) * sf
    lf = jnp.where(valid, jnp.log(f), 0.0)
    b = jnp.dot(_tri(True), lf, precision=HIGHEST, preferred_element_type=F32)
    return lb, sq, q, sf, f, b


def _hg_head(qh, kh, bh):
    blk = lax.broadcasted_iota(jnp.int32, (CHUNK, 1), 0) // SUB
    b_last = bh[CHUNK - 1:CHUNK, :]
    refs = [bh[SUB * s:SUB * s + 1, :] for s in range(N_SUB)]
    r_sel = refs[N_SUB - 1]
    for s in range(N_SUB - 2, -1, -1):
        r_sel = jnp.where(blk == s, refs[s], r_sel)
    eb = jnp.exp(bh)
    eq = jnp.exp(bh - r_sel)
    ekh = jnp.exp(b_last - bh)
    ek = [jnp.exp(jnp.minimum(refs[s] - bh, EXP_CLAMP)) for s in range(N_SUB)]
    qe = qh * eq
    q_hat = jnp.concatenate([jnp.where(blk == s, qe, 0.0) for s in range(N_SUB)], axis=1)
    k_til = jnp.concatenate([kh * ek[s] for s in range(N_SUB)], axis=1)
    return blk, b_last, eb, eq, ekh, ek, q_hat, k_til


def _causal():
    r = lax.broadcasted_iota(jnp.int32, (CHUNK, CHUNK), 0)
    c = lax.broadcasted_iota(jnp.int32, (CHUNK, CHUNK), 1)
    return r >= c


def _chunks_per_step(n_chunks):
    for c in (5, 4, 3, 2):
        if n_chunks % c == 0:
            return c
    return 1


def _hg_fwd(p, lbraw, hg_g):
    T = p.shape[0]
    n_chunks = T // CHUNK
    cps = _chunks_per_step(n_chunks)
    rows = cps * CHUNK

    def body(hq_ref, hf_ref, hi_ref, hg_ref, lb_ref, g_ref, y_ref, o_ref, st_all_ref, st):
        i = pl.program_id(0)

        @pl.when(i == 0)
        def _():
            st[...] = jnp.zeros_like(st)

        def chunk(j, carry):
            rs = pl.ds(pl.multiple_of(j * CHUNK, CHUNK), CHUNK)
            chunk_body(i * cps + j, hq_ref.at[rs, :], hf_ref.at[rs, :], hi_ref.at[rs, :], hg_ref.at[rs, :], lb_ref,
                       g_ref, y_ref.at[rs, :], o_ref.at[rs, :], st_all_ref.at[pl.ds(j, 1)], st)
            return carry

        lax.fori_loop(0, cps, chunk, 0, unroll=True)

    def chunk_body(n, hq_ref, hf_ref, hi_ref, hg_ref, lb_ref, g_ref, y_ref, o_ref, st_all_ref, st):
        valid = (n * CHUNK + lax.broadcasted_iota(jnp.int32, (CHUNK, 1), 0)) >= PAD
        hq, hf, v, hg = hq_ref[...], hf_ref[...], hi_ref[...], hg_ref[...]
        lb, sq, q, sf, f, b = _hg_gates(hq, hf, lb_ref, valid)
        k = 1.0 - f
        st_all_ref[0] = st[...]
        causal = _causal()
        v_t = v.T.astype(BF16)
        heads = [slice(h * HG_HEAD_DIM, (h + 1) * HG_HEAD_DIM) for h in range(HG_HEADS)]
        fac = []
        for sl in heads:
            qh, kh, bh = q[:, sl], k[:, sl], b[:, sl]
            _, b_last, eb, _, ekh, _, q_hat, k_til = _hg_head(qh, kh, bh)
            fac.append((jnp.exp(b_last), (qh * eb).astype(BF16), q_hat.astype(BF16), k_til.astype(BF16),
                        (kh * ekh).astype(BF16), v[:, sl].astype(BF16)))
        raw = []
        for sl, (_, q_til, q_hat, k_til, k_hat, _) in zip(heads, fac):
            st_h = st[sl, :]
            raw.append((_dot_nt(q_til, st_h.astype(BF16)), _dot_nt(q_hat, k_til), _dot(v_t[sl, :], k_hat), st_h))
        for sl, (e_last, _, _, _, _, vb), (inter, att, upd, st_h) in zip(heads, fac, raw):
            o = inter + _dot(jnp.where(causal, att, 0.0).astype(BF16), vb)
            st[sl, :] = st_h * e_last + upd
            o_ref[:, sl] = o
            hgh = hg[:, sl]
            y_ref[:, sl] = (o * _rms(o) * g_ref[...] * (hgh * _sigmoid(hgh))).astype(BF16)

    col = lambda j: pl.BlockSpec((rows, D_HG), lambda n: (n, j))
    return pl.pallas_call(
        body, grid=(n_chunks // cps,),
        in_specs=[col(2), col(3), col(4), col(5),
                  pl.BlockSpec((2, D_HG), lambda n: (0, 0)), pl.BlockSpec((1, HG_HEAD_DIM), lambda n: (0, 0))],
        out_specs=[pl.BlockSpec((rows, D_HG), lambda n: (n, 0)), pl.BlockSpec((rows, D_HG), lambda n: (n, 0)),
                   pl.BlockSpec((cps, D_HG, HG_HEAD_DIM), lambda n: (n, 0, 0))],
        out_shape=[jax.ShapeDtypeStruct((T, D_HG), BF16), jax.ShapeDtypeStruct((T, D_HG), F32),
                   jax.ShapeDtypeStruct((n_chunks, D_HG, HG_HEAD_DIM), F32)],
        scratch_shapes=[pltpu.VMEM((D_HG, HG_HEAD_DIM), F32)],
        name="hg_fwd", compiler_params=_params("arbitrary"),
    )(p, p, p, p, lbraw, hg_g)


def _ffn_fwd(h0, y_rg, y_hg, w_out, g2, w_gu, w_down, gf, target):
    T = h0.shape[0]
    tm = _row_tile(T, 320)
    n_steps = T // tm

    def body(h_ref, yr_ref, yh_ref, wo_ref, g2_ref, wgu_ref, wd_ref, gf_ref, t_hbm,
             h1_ref, v_ref, y_ref, gu_ref, act_ref, dh2_ref, dh2b_ref, loss_ref, gg_ref, tbuf, sems):
        i = pl.program_id(0)
        slot = _fetch_window(t_hbm, tbuf, sems, i, n_steps, tm)

        @pl.when(i == 0)
        def _():
            loss_ref[...] = jnp.zeros_like(loss_ref)
            gg_ref[...] = jnp.zeros_like(gg_ref)
            tbuf[0, 0:HEAD, :] = jnp.zeros((HEAD, D_MODEL), F32)

        y_ref[:, :D_RG] = yr_ref[...]
        y_ref[:, D_RG:] = yh_ref[...]
        h1 = h_ref[...] + _dot(y_ref[...], wo_ref[...])
        h1_ref[...] = h1
        v = (h1 * _rms(h1) * g2_ref[...]).astype(BF16)
        v_ref[...] = v

        gu = _dot(v, wgu_ref[...])
        gu_ref[...] = gu.astype(BF16)
        g = gu[:, :D_FF]
        act = (g * _sigmoid(g) * gu[:, D_FF:]).astype(BF16)
        act_ref[...] = act

        h2 = h1 + _dot(act, wd_ref[...])
        r = _rms(h2)
        n = h2 * r
        gf_ = gf_ref[...]
        row = i * tm + lax.broadcasted_iota(jnp.int32, (tm, 1), 0)
        err = jnp.where(row >= HEAD, n * gf_ - tbuf[slot], 0.0)
        loss_ref[...] += 0.5 * jnp.sum(jnp.mean(err * err, axis=-1, keepdims=True), axis=0, keepdims=True)
        dy = err * (1.0 / D_MODEL)
        gg_ref[...] += jnp.sum(dy * n, axis=0, keepdims=True)
        dh2 = _rms_bwd(dy * gf_, n, r)
        dh2_ref[...] = dh2
        dh2b_ref[...] = dh2.astype(BF16)

    row_spec = lambda n: pl.BlockSpec((tm, n), lambda i: (i, 0))
    vec = pl.BlockSpec((1, D_MODEL), lambda i: (0, 0))
    return pl.pallas_call(
        body, grid=(n_steps,),
        in_specs=[row_spec(D_MODEL), row_spec(D_RG), row_spec(D_HG), _resident((D_MODEL, D_MODEL)), vec,
                  _resident((D_MODEL, 2 * D_FF)), _resident((D_FF, D_MODEL)), vec,
                  pl.BlockSpec(memory_space=pl.ANY)],
        out_specs=[row_spec(D_MODEL), row_spec(D_MODEL), row_spec(D_MODEL), row_spec(2 * D_FF), row_spec(D_FF),
                   row_spec(D_MODEL), row_spec(D_MODEL), pl.BlockSpec((1, 1), lambda i: (0, 0)), vec],
        out_shape=[jax.ShapeDtypeStruct((T, D_MODEL), F32), jax.ShapeDtypeStruct((T, D_MODEL), BF16),
                   jax.ShapeDtypeStruct((T, D_MODEL), BF16), jax.ShapeDtypeStruct((T, 2 * D_FF), BF16),
                   jax.ShapeDtypeStruct((T, D_FF), BF16), jax.ShapeDtypeStruct((T, D_MODEL), F32),
                   jax.ShapeDtypeStruct((T, D_MODEL), BF16), jax.ShapeDtypeStruct((1, 1), F32),
                   jax.ShapeDtypeStruct((1, D_MODEL), F32)],
        scratch_shapes=[pltpu.VMEM((2, tm, D_MODEL), F32), pltpu.SemaphoreType.DMA((2,))],
        name="ffn_fwd", compiler_params=_params("arbitrary"),
    )(h0, y_rg, y_hg, w_out, g2, w_gu, w_down, gf, target)


def _resident(shape):
    return pl.BlockSpec(shape, lambda i: (0,) * len(shape), pipeline_mode=pl.Buffered(1))


def _ffn_bwd(dh2b, gu, w_down, w_gu, h1, g2, dh2, w_out):
    T = h1.shape[0]
    tm = _row_tile(T, 320)

    def body(d_ref, gu_ref, wd_ref, wgu_ref, h_ref, g_ref, d2_ref, wo_ref, dgu_ref, dh1_ref, dh1b_ref, dy_ref, gg_ref):
        i = pl.program_id(0)

        @pl.when(i == 0)
        def _():
            gg_ref[...] = jnp.zeros_like(gg_ref)

        dact = _dot_nt(d_ref[...], wd_ref[...]).astype(BF16)
        g = gu_ref[:, :D_FF]
        u = gu_ref[:, D_FF:]
        s = _sigmoid(g)
        dgu_ref[:, :D_FF] = dact * u * (s * (1.0 + g * (1.0 - s)))
        dgu_ref[:, D_FF:] = dact * (g * s)

        dv = _dot_nt(dgu_ref[...], wgu_ref[...])
        h1_ = h_ref[...]
        r = _rms(h1_)
        n = h1_ * r
        gg_ref[...] += jnp.sum(dv * n, axis=0, keepdims=True)
        dh1 = d2_ref[...] + _rms_bwd(dv * g_ref[...], n, r)
        dh1_ref[...] = dh1
        db = dh1.astype(BF16)
        dh1b_ref[...] = db
        dy_ref[...] = _dot_nt(db, wo_ref[...])

    row = lambda n: pl.BlockSpec((tm, n), lambda i: (i, 0))
    return pl.pallas_call(
        body, grid=(T // tm,),
        in_specs=[row(D_MODEL), row(2 * D_FF), _resident((D_FF, D_MODEL)), _resident((D_MODEL, 2 * D_FF)),
                  row(D_MODEL), pl.BlockSpec((1, D_MODEL), lambda i: (0, 0)), row(D_MODEL),
                  _resident((D_MODEL, D_MODEL))],
        out_specs=[row(2 * D_FF), row(D_MODEL), row(D_MODEL), row(D_MODEL),
                   pl.BlockSpec((1, D_MODEL), lambda i: (0, 0))],
        out_shape=[jax.ShapeDtypeStruct((T, 2 * D_FF), BF16), jax.ShapeDtypeStruct((T, D_MODEL), F32),
                   jax.ShapeDtypeStruct((T, D_MODEL), BF16), jax.ShapeDtypeStruct((T, D_MODEL), F32),
                   jax.ShapeDtypeStruct((1, D_MODEL), F32)],
        name="ffn_bwd", compiler_params=_params("arbitrary"),
    )(dh2b, gu, w_down, w_gu, h1, g2, dh2, w_out)


def _rg_bwd(p, xc_all, hs, dy, dp, cw, cb, wg, bg, lam, rg_g):
    T = p.shape[0]
    tm = _row_tile(T, 832)
    nt = T // tm
    hb = tm // 8
    unroll = _scan_unroll(hb)

    def body(xg_ref, xc_ref, h_ref, hh_ref, dy_ref, dp_in_ref, cw_ref, cb_ref, w_ref, bg_ref, lam_ref, g_ref,
             dp_ref, gcw_ref, gcb_ref, gw_ref, gbg_ref, glam_ref, gg_ref,
             dext, a_s, b_s, d_s, gacc, carry_d, carry_a):
        i = pl.program_id(0)
        t_idx = nt - 1 - i

        @pl.when(i == 0)
        def _():
            dext[tm:tm + 8, :] = jnp.zeros((8, D_RG), F32)
            carry_d[...] = jnp.zeros_like(carry_d)
            carry_a[...] = jnp.zeros_like(carry_a)
            gacc[...] = jnp.zeros_like(gacc)
            for ref in (gcw_ref, gcb_ref, gbg_ref, glam_ref, gg_ref, gw_ref):
                ref[...] = jnp.zeros_like(ref)

        first = t_idx == 0
        xc = xc_ref[...]
        lam_ = lam_ref[...]
        r, ig, sp, a, m, inv_m = _rg_gates(xc, w_ref, bg_ref, lam_)
        row = t_idx * tm + lax.broadcasted_iota(jnp.int32, (tm, 1), 0)
        valid = row >= PAD

        gr = xg_ref[:, D_RG:]
        g, dgelu = _gelu_parts(gr)
        h = h_ref[...]
        yy = g * h
        rr = _rms(yy)
        nn = yy * rr
        dy_ = dy_ref[...]
        gg_ref[...] += jnp.sum(dy_ * nn, axis=0, keepdims=True)
        dyy = _rms_bwd(dy_ * g_ref[...], nn, rr)
        dp_ref[:, D_RG:] = (dyy * h * dgelu).astype(BF16)

        a_s[...] = a
        b_s[...] = dyy * g
        rowi = lax.broadcasted_iota(jnp.int32, (8, D_RG), 0)

        def blk(jj, c):
            cd, ca = c
            for u in range(unroll):
                o = pl.multiple_of((hb - 1 - (jj * unroll + u)) * 8, 8)
                a_blk = a_s[pl.ds(o, 8), :]
                a_next = jnp.where(rowi == 7, ca, pltpu.roll(a_blk, 7, axis=0))
                A, B = _scan_block_bwd(a_next, b_s[pl.ds(o, 8), :], rowi)
                d = B + A * cd
                d_s[pl.ds(o, 8), :] = d
                cd, ca = d[0:1, :], a_blk[0:1, :]
            return cd, ca

        cd, ca = lax.fori_loop(0, hb // unroll, blk, (carry_d[...], carry_a[...]))
        carry_d[...] = cd
        carry_a[...] = ca
        delta = d_s[...]

        h_last_prev = jnp.where(first, 0.0, hh_ref[7:8, :])
        row0 = lax.broadcasted_iota(jnp.int32, (tm, 1), 0) == 0
        h_prev = jnp.where(row0, h_last_prev, pltpu.roll(h, 1, axis=0))
        dbx = jnp.where(valid, delta, 0.0)
        da = delta * h_prev
        di = dbx * m * xc
        dm = dbx * ig * xc
        dla = a * (da - dm * a * inv_m)
        dla = jnp.where(valid, dla, 0.0)
        glam_ref[...] += jnp.sum(dla * r, axis=0, keepdims=True) * (LRU_C / (1.0 + jnp.exp(lam_)))
        dr = (-LRU_C) * sp * dla
        dpre = jnp.concatenate([dr * r * (1.0 - r), di * ig * (1.0 - ig)], axis=1)
        gbg_ref[...] += jnp.sum(dpre, axis=0, keepdims=True)
        dpre_b = dpre.astype(BF16)
        gacc[...] += _dot_tn(xc.astype(BF16), dpre_b)
        dxc = dbx * m * ig + _dot_nt(dpre_b, w_ref[...])
        gcb_ref[...] += jnp.sum(dxc, axis=0, keepdims=True)
        dext[0:tm, :] = dxc
        xr = xg_ref[:, :D_RG]
        dxr = None
        for j in range(CONV_W):
            shifted = dext[3 - j:3 - j + tm, :]
            gcw_ref[j:j + 1, :] += jnp.sum(xr * shifted, axis=0, keepdims=True)
            tap = cw_ref[j:j + 1, :] * shifted
            dxr = tap if dxr is None else dxr + tap
        dp_ref[:, :D_RG] = dxr.astype(BF16)
        dext[tm:tm + 8, :] = dext[0:8, :]

        @pl.when(i == nt - 1)
        def _():
            fold = _head_fold()
            mask = _head_mask()
            for k in range(2):
                blockdiag = jnp.where(mask, gacc[:, k * D_RG:(k + 1) * D_RG], 0.0)
                gw_ref[k * D_RG:(k + 1) * D_RG, :] = jnp.dot(blockdiag, fold, precision=HIGHEST,
                                                             preferred_element_type=F32)

    vec = lambda n: pl.BlockSpec((1, n), lambda i: (0, 0))
    rev = lambda n: pl.BlockSpec((tm, n), lambda i: (nt - 1 - i, 0))
    halo = lambda n: pl.BlockSpec((8, n), lambda i: (jnp.maximum((nt - 1 - i) * hb - 1, 0), 0))
    return pl.pallas_call(
        body, grid=(nt,),
        in_specs=[rev(2 * D_RG), rev(D_RG), rev(D_RG), halo(D_RG), rev(D_RG), ANY,
                  pl.BlockSpec((CONV_W, D_RG), lambda i: (0, 0)), vec(D_RG),
                  pl.BlockSpec((D_RG, 2 * D_RG), lambda i: (0, 0)), vec(2 * D_RG), vec(D_RG), vec(D_RG)],
        out_specs=[rev(2 * D_RG), pl.BlockSpec((CONV_W, D_RG), lambda i: (0, 0)), vec(D_RG),
                   pl.BlockSpec((2 * D_RG, RG_HEAD_DIM), lambda i: (0, 0)), vec(2 * D_RG), vec(D_RG), vec(D_RG)],
        input_output_aliases={5: 0},
        out_shape=[jax.ShapeDtypeStruct((T, D_IN), BF16), jax.ShapeDtypeStruct((CONV_W, D_RG), F32),
                   jax.ShapeDtypeStruct((1, D_RG), F32), jax.ShapeDtypeStruct((2 * D_RG, RG_HEAD_DIM), F32),
                   jax.ShapeDtypeStruct((1, 2 * D_RG), F32), jax.ShapeDtypeStruct((1, D_RG), F32),
                   jax.ShapeDtypeStruct((1, D_RG), F32)],
        scratch_shapes=[pltpu.VMEM((tm + 8, D_RG), F32),
                        pltpu.VMEM((tm, D_RG), F32), pltpu.VMEM((tm, D_RG), F32), pltpu.VMEM((tm, D_RG), F32),
                        pltpu.VMEM((D_RG, 2 * D_RG), F32), pltpu.VMEM((1, D_RG), F32), pltpu.VMEM((1, D_RG), F32)],
        name="rg_bwd", compiler_params=_params("arbitrary"),
    )(p, xc_all, hs, hs, dy, dp, cw, cb, wg, bg, lam, rg_g)


def _hg_bwd(p, o_all, st_all, dy, lbraw, hg_g):
    T = p.shape[0]
    n_chunks = T // CHUNK
    cps = _chunks_per_step(n_chunks)
    rows = cps * CHUNK
    n_steps = n_chunks // cps

    def body(hq_ref, hf_ref, hi_ref, hg_ref, o_ref, st_ref, dy_ref, lb_ref, g_ref,
             dp_ref, glb_ref, gg_ref, dst):
        i = pl.program_id(0)

        @pl.when(i == 0)
        def _():
            dst[...] = jnp.zeros_like(dst)
            glb_ref[...] = jnp.zeros_like(glb_ref)
            gg_ref[...] = jnp.zeros_like(gg_ref)

        dp_ref[:, :2 * D_RG] = jnp.zeros((rows, 2 * D_RG), BF16)

        def chunk(jj, carry):
            j = cps - 1 - jj
            rs = pl.ds(pl.multiple_of(j * CHUNK, CHUNK), CHUNK)
            chunk_body((n_steps - 1 - i) * cps + j, hq_ref.at[rs, :], hf_ref.at[rs, :], hi_ref.at[rs, :],
                       hg_ref.at[rs, :], o_ref.at[rs, :], st_ref.at[pl.ds(j, 1)], dy_ref.at[rs, :], lb_ref, g_ref,
                       dp_ref.at[rs, pl.ds(2 * D_RG, 4 * D_HG)], glb_ref, gg_ref, dst)
            return carry

        lax.fori_loop(0, cps, chunk, 0, unroll=True)

    def chunk_body(n, hq_ref, hf_ref, hi_ref, hg_ref, o_ref, st_ref, dy_ref, lb_ref, g_ref,
                   dp_ref, glb_ref, gg_ref, dst):
        valid = (n * CHUNK + lax.broadcasted_iota(jnp.int32, (CHUNK, 1), 0)) >= PAD
        hq, hf, v, hg = hq_ref[...], hf_ref[...], hi_ref[...], hg_ref[...]
        lb, sq, q, sf, f, b = _hg_gates(hq, hf, lb_ref, valid)
        k = 1.0 - f
        causal = _causal()
        r_i = lax.broadcasted_iota(jnp.int32, (CHUNK, CHUNK), 0)
        c_i = lax.broadcasted_iota(jnp.int32, (CHUNK, CHUNK), 1)
        causal_t = r_i <= c_i
        is_last = lax.broadcasted_iota(jnp.int32, (CHUNK, 1), 0) == CHUNK - 1
        g_ = g_ref[...]
        db_parts, dq_parts, dk_parts = [], [], []
        gg = jnp.zeros((1, HG_HEAD_DIM), F32)
        heads = [slice(h * HG_HEAD_DIM, (h + 1) * HG_HEAD_DIM) for h in range(HG_HEADS)]

        do_parts = []
        for h, sl in enumerate(heads):
            o = o_ref[:, sl]
            ro = _rms(o)
            no = o * ro
            hgh = hg[:, sl]
            sg = _sigmoid(hgh)
            dyh = dy_ref[:, sl]
            dp_ref[:, 3 * D_HG + h * HG_HEAD_DIM:3 * D_HG + (h + 1) * HG_HEAD_DIM] = (
                dyh * no * g_ * sg * (1.0 + hgh * (1.0 - sg))).astype(BF16)
            dng = dyh * hgh * sg
            gg = gg + jnp.sum(dng * no, axis=0, keepdims=True)
            do_parts.append(_rms_bwd(dng * g_, no, ro))
        do_t = jnp.concatenate(do_parts, axis=1).T.astype(BF16)

        fac = []
        for sl, do in zip(heads, do_parts):
            qh, kh, bh = q[:, sl], k[:, sl], b[:, sl]
            blk, b_last, eb, eq, ekh, ek, q_hat, k_til = _hg_head(qh, kh, bh)
            fac.append(dict(qh=qh, kh=kh, blk=blk, e_last=jnp.exp(b_last), eb=eb, eq=eq, ekh=ekh, ek=ek,
                            q_til=qh * eb, k_hat=kh * ekh, qhb=q_hat.astype(BF16), ktb=k_til.astype(BF16),
                            vb=v[:, sl].astype(BF16), dob=do.astype(BF16)))

        first = []
        for sl, t in zip(heads, fac):
            st_h = st_ref[0, sl, :]
            dst_h = dst[sl, :]
            dstb = dst_h.astype(BF16)
            first.append(dict(
                att_t=_dot_nt(t["ktb"], t["qhb"]), datt=_dot_nt(t["dob"], t["vb"]),
                datt_t=_dot_nt(t["vb"], t["dob"]), dk_hat=_dot(t["vb"], dstb),
                dv=_dot_nt(t["k_hat"].astype(BF16), dstb), dq_til=_dot(t["dob"], st_h.astype(BF16)),
                state=t["e_last"] * jnp.sum(dst_h * st_h, axis=0, keepdims=True)))
            dst[sl, :] = dst_h * t["e_last"] + _dot(do_t[sl, :], t["q_til"].astype(BF16))

        for h, (t, m) in enumerate(zip(fac, first)):
            qh, kh, blk, eb, eq, ekh, ek = t["qh"], t["kh"], t["blk"], t["eb"], t["eq"], t["ekh"], t["ek"]
            q_til, k_hat, qhb, ktb, dob = t["q_til"], t["k_hat"], t["qhb"], t["ktb"], t["dob"]
            dk_hat, dq_til = m["dk_hat"], m["dq_til"]
            dv = m["dv"] + _dot(jnp.where(causal_t, m["att_t"], 0.0).astype(BF16), dob)
            dq_hat = _dot(jnp.where(causal, m["datt"], 0.0).astype(BF16), ktb)
            dk_til = _dot(jnp.where(causal_t, m["datt_t"], 0.0).astype(BF16), qhb)
            db_last = jnp.sum(dk_hat * k_hat, axis=0, keepdims=True) + m["state"]
            dq_sel = dq_hat[:, (N_SUB - 1) * HG_HEAD_DIM:]
            for s in range(N_SUB - 2, -1, -1):
                dq_sel = jnp.where(blk == s, dq_hat[:, s * HG_HEAD_DIM:(s + 1) * HG_HEAD_DIM], dq_sel)
            dq_a = dq_sel * eq
            dk_a = dk_til[:, :HG_HEAD_DIM] * ek[0]
            for s in range(1, N_SUB):
                dk_a = dk_a + dk_til[:, s * HG_HEAD_DIM:(s + 1) * HG_HEAD_DIM] * ek[s]
            db_att = qhb.astype(F32) * dq_hat - ktb.astype(F32) * dk_til
            db = dq_til * q_til - dk_hat * k_hat
            for s in range(N_SUB):
                db = db + db_att[:, s * HG_HEAD_DIM:(s + 1) * HG_HEAD_DIM]
            db_parts.append(jnp.where(is_last, db + db_last, db))
            dq_parts.append(dq_til * eb + dq_a)
            dk_parts.append(dk_hat * ekh + dk_a)
            dp_ref[:, 2 * D_HG + h * HG_HEAD_DIM:2 * D_HG + (h + 1) * HG_HEAD_DIM] = dv.astype(BF16)

        gg_ref[...] += gg
        db = jnp.concatenate(db_parts, axis=1)
        dq = jnp.concatenate(dq_parts, axis=1)
        dk = jnp.concatenate(dk_parts, axis=1)
        dlf = jnp.where(valid, jnp.dot(_tri(False), db, precision=HIGHEST, preferred_element_type=F32), 0.0)
        dp_ref[:, :D_HG] = (dq * sq * (1.0 + hq * (1.0 - sq))).astype(BF16)
        df = dlf / f - dk
        dlb = jnp.sum(df * (1.0 - sf), axis=0, keepdims=True) * lb * (1.0 - lb)
        glb_ref[0:1, :] += dlb
        glb_ref[1:2, :] += -dlb
        dp_ref[:, D_HG:2 * D_HG] = (df * (1.0 - lb) * sf * (1.0 - sf)).astype(BF16)

    rev = lambda j: pl.BlockSpec((rows, D_HG), lambda i: (n_steps - 1 - i, j))
    return pl.pallas_call(
        body, grid=(n_steps,),
        in_specs=[rev(2), rev(3), rev(4), rev(5), rev(0),
                  pl.BlockSpec((cps, D_HG, HG_HEAD_DIM), lambda i: (n_steps - 1 - i, 0, 0)), rev(1),
                  pl.BlockSpec((2, D_HG), lambda i: (0, 0)), pl.BlockSpec((1, HG_HEAD_DIM), lambda i: (0, 0))],
        out_specs=[pl.BlockSpec((rows, D_IN), lambda i: (n_steps - 1 - i, 0)),
                   pl.BlockSpec((2, D_HG), lambda i: (0, 0)), pl.BlockSpec((1, HG_HEAD_DIM), lambda i: (0, 0))],
        out_shape=[jax.ShapeDtypeStruct((T, D_IN), BF16), jax.ShapeDtypeStruct((2, D_HG), F32),
                   jax.ShapeDtypeStruct((1, HG_HEAD_DIM), F32)],
        scratch_shapes=[pltpu.VMEM((D_HG, HG_HEAD_DIM), F32)],
        name="hg_bwd", compiler_params=_params("arbitrary"),
    )(p, p, p, p, o_all, st_all, dy, lbraw, hg_g)


def _in_bwd(dp, w_in, h0, g1, dh1):
    T = h0.shape[0]
    tm = _row_tile(T, 832)
    n_steps = T // tm

    def body(dp_ref, w_ref, h_ref, g_ref, d1_ref, gx_hbm, gmeta_ref, gg_ref, buf, sems):
        i = pl.program_id(0)
        first, later = _window_copies(gx_hbm, buf, sems, tm)
        slot = i % 2

        @pl.when(i == 0)
        def _():
            gg_ref[...] = jnp.zeros_like(gg_ref)

        if n_steps > 2:
            @pl.when(i == 2)
            def _():
                first(False).wait()

            @pl.when(i > 2)
            def _():
                later(i - 2, slot, False).wait()

        du = _dot_nt(dp_ref[...], w_ref[...])
        h0_ = h_ref[...]
        r = _rms(h0_)
        n = h0_ * r
        gg_ref[...] += jnp.sum(du * n, axis=0, keepdims=True)
        dh0 = d1_ref[...] + _rms_bwd(du * g_ref[...], n, r)
        buf[slot] = dh0

        @pl.when(i == 0)
        def _():
            gmeta_ref[...] = dh0[PAD:HEAD, :]
            first(False).start()

        if n_steps > 1:
            @pl.when(i > 0)
            def _():
                later(i, slot, False).start()

        @pl.when(i == n_steps - 1)
        def _():
            if n_steps == 1:
                first(False).wait()
            else:
                if n_steps == 2:
                    first(False).wait()
                else:
                    later(i - 1, 1 - slot, False).wait()
                later(i, slot, False).wait()

    row = lambda n: pl.BlockSpec((tm, n), lambda i: (i, 0))
    return pl.pallas_call(
        body, grid=(n_steps,),
        in_specs=[row(D_IN), _resident((D_MODEL, D_IN)),
                  row(D_MODEL), pl.BlockSpec((1, D_MODEL), lambda i: (0, 0)), row(D_MODEL)],
        out_specs=[pl.BlockSpec(memory_space=pl.ANY), pl.BlockSpec((N_META, D_MODEL), lambda i: (0, 0)),
                   pl.BlockSpec((1, D_MODEL), lambda i: (0, 0))],
        out_shape=[jax.ShapeDtypeStruct((T - HEAD, D_MODEL), F32), jax.ShapeDtypeStruct((N_META, D_MODEL), F32),
                   jax.ShapeDtypeStruct((1, D_MODEL), F32)],
        scratch_shapes=[pltpu.VMEM((2, tm, D_MODEL), F32), pltpu.SemaphoreType.DMA((2,))],
        name="in_bwd", compiler_params=_params("arbitrary"),
    )(dp, w_in, h0, g1, dh1)


def _col_tile(cols, target):
    best = None
    for t in range(128, min(cols, target) + 1, 128):
        if cols % t == 0:
            best = t
    assert best is not None, cols
    return best


MXU_DIM = 256


def _mxu_tile(cols, target):
    best = None
    for t in range(MXU_DIM, min(cols, target) + 1, MXU_DIM):
        if cols % t == 0:
            best = t
    assert best is not None, cols
    return best


def _weight_grad(a, b, name):
    T, M = a.shape
    N = b.shape[1]
    tm = _col_tile(M, 1408)
    tn = _mxu_tile(N, 768 if tm <= 1024 else 512)

    def body(a_ref, b_ref, o_ref):
        o_ref[...] = _dot_tn(a_ref[...], b_ref[...])

    return pl.pallas_call(
        body, grid=(M // tm, N // tn),
        in_specs=[pl.BlockSpec((T, tm), lambda m, n: (0, m)), pl.BlockSpec((T, tn), lambda m, n: (0, n))],
        out_specs=pl.BlockSpec((tm, tn), lambda m, n: (m, n)),
        out_shape=jax.ShapeDtypeStruct((M, N), F32),
        name=name, compiler_params=_params("parallel", "parallel"),
    )(a, b)


def _local_step(x, meta, target, w_in_own, w_in, w_out, w_gu, w_down, small, chip, on_ffn_grads=None,
                on_mixer_grads=None):
    wg = _gate_weights(small["w_rgate"], small["w_igate"])
    bg = jnp.concatenate([small["b_rgate"], small["b_igate"]], axis=1)

    p, u, h0 = _in_proj_local(x, meta, small["mix_norm_g"], w_in_own, chip)
    p = _in_proj_rest(u, w_in, p, chip)
    y_rg, hs, xc = _rg_fwd(p, small["conv_w"], small["conv_b"], wg, bg, small["lru_lambda"], small["rg_norm_g"])
    y_hg, o_all, st_all = _hg_fwd(p, small["hg_lower_bound"], small["hg_norm_g"])
    h1, v, yb, gu, act, dh2, dh2b, loss, g_final = _ffn_fwd(
        h0, y_rg, y_hg, w_out, small["ffn_norm_g"], w_gu, w_down, small["final_norm_g"], target)

    g_w_down = _weight_grad(act, dh2b, "grad_w_down")
    dgu, dh1, dh1b, dy, g_ffn = _ffn_bwd(dh2b, gu, w_down, w_gu, h1, small["ffn_norm_g"], dh2, w_out)
    ffn_grads = {"w_gate_up": _weight_grad(v, dgu, "grad_w_gate_up"), "w_down": g_w_down,
                 "w_out": _weight_grad(yb, dh1b, "grad_w_out")}
    stages = on_ffn_grads(ffn_grads) if on_ffn_grads is not None else None
    dp, g_lb, g_hgn = _hg_bwd(p, o_all, st_all, dy, small["hg_lower_bound"], small["hg_norm_g"])
    early = late = None
    if stages is not None:
        chip_sums, send = stages
        sums = chip_sums()
        (dp, dy), sums = lax.optimization_barrier(((dp, dy), sums))
        early = send(sums)
    dp, g_cw, g_cb, g_wgate, g_bg, g_lam, g_rgn = _rg_bwd(
        p, xc, hs, dy, dp, small["conv_w"], small["conv_b"], wg, bg, small["lru_lambda"], small["rg_norm_g"])
    mixer_grads = {"w_in": _weight_grad(u, dp, "grad_w_in")}
    if on_mixer_grads is not None:
        chip_sums, send = on_mixer_grads(mixer_grads)
        sums = chip_sums()
        (dp, dh1), sums = lax.optimization_barrier(((dp, dh1), sums))
        late = send(sums)
    grad_x, g_meta, g_mix = _in_bwd(dp, w_in, h0, small["mix_norm_g"], dh1)

    grads = {
        "w_in": mixer_grads["w_in"], "w_out": ffn_grads["w_out"],
        "w_gate_up": ffn_grads["w_gate_up"], "w_down": ffn_grads["w_down"],
        "meta_tokens": g_meta, "mix_norm_g": g_mix, "conv_w": g_cw, "conv_b": g_cb, "w_gates": g_wgate,
        "b_rgate": g_bg[:, :D_RG], "b_igate": g_bg[:, D_RG:], "lru_lambda": g_lam, "rg_norm_g": g_rgn,
        "hg_lower_bound": g_lb, "hg_norm_g": g_hgn, "ffn_norm_g": g_ffn, "final_norm_g": g_final,
    }
    return loss, grad_x, grads, early, late


ANY = pl.BlockSpec(memory_space=pl.ANY)
HALF = D_MODEL // 2

BIG = {"w_in": (D_MODEL, D_IN // N_CHIPS, True), "w_gate_up": (D_MODEL, 2 * D_FF // N_CHIPS, True),
       "w_out": (D_MODEL // N_CHIPS, D_MODEL, False), "w_down": (D_FF // N_CHIPS, D_MODEL, False)}
BIG_NAMES = tuple(BIG)
N_BIG = len(BIG_NAMES)


def _full_shape(name):
    rows, cols, by_col = BIG[name]
    return (rows, cols * N_CHIPS) if by_col else (rows * N_CHIPS, cols)


def _place():
    return lax.axis_index("x"), lax.axis_index("y"), lax.axis_index("c")


def _chip_of(x, y, r):
    fx, fy = (r + 1) >> 1, (r + 1) & 1
    return (1 - x if fx else x), (1 - y if fy else y)


def _half_of(ref, by_col, half):
    start = pl.multiple_of(half * HALF, 128)
    return ref.at[pl.ds(start, HALF), :] if by_col else ref.at[:, pl.ds(start, HALF)]


def _shard_of(ref, name, chip):
    rows, cols, by_col = BIG[name]
    if by_col:
        return ref.at[:, pl.ds(pl.multiple_of(chip * cols, 128), cols)]
    return ref.at[pl.ds(pl.multiple_of(chip * rows, 16), rows), :]


def _shard_half_of(ref, name, chip, half):
    rows, cols, by_col = BIG[name]
    start = pl.multiple_of(half * HALF, 128)
    if by_col:
        return ref.at[pl.ds(start, HALF), pl.ds(pl.multiple_of(chip * cols, 128), cols)]
    return ref.at[pl.ds(pl.multiple_of(chip * rows, 16), rows), pl.ds(start, HALF)]


def _remote(src, dst, send_sems, recv_sems, k, dev):
    return pltpu.make_async_remote_copy(src_ref=src, dst_ref=dst, send_sem=send_sems.at[k], recv_sem=recv_sems.at[k],
                                        device_id=dev, device_id_type=MESH)


def _place_shards(w, small, chip):
    steps = 4
    ns = len(small)
    in_specs, out_specs = [], []
    for name in BIG_NAMES:
        rows, cols, by_col = BIG[name]
        tr = rows // steps
        in_specs.append(pl.BlockSpec((tr, cols), lambda i, s: (i, 0)))
        if by_col:
            out_specs.append(pl.BlockSpec((tr, cols), lambda i, s: (i, s[0])))
        else:
            out_specs.append(pl.BlockSpec((tr, cols), lambda i, s: (s[0] * steps + i, 0)))

    def body(s_ref, *refs):
        ins, small_in = refs[:N_BIG], refs[N_BIG:N_BIG + ns]
        outs, small_out = refs[N_BIG + ns:2 * N_BIG + ns], refs[2 * N_BIG + ns:2 * (N_BIG + ns)]
        send_sems, recv_sems, local_sems = refs[2 * (N_BIG + ns):]
        i = pl.program_id(0)
        x, y, c = _place()
        chip_ = 2 * x + y
        others = [_chip_of(x, y, r) for r in range(3)]

        def block(a, q):
            cols = small[a].shape[1]
            return small_out[a].at[:, pl.ds(pl.multiple_of(q * cols, 128), cols)]

        def local(a):
            return pltpu.make_async_copy(small_in[a], block(a, chip_), local_sems.at[a])

        def remote(a, r):
            qx, qy = others[r]
            return _remote(small_in[a], block(a, chip_), send_sems, recv_sems, 3 * a + r, (qx, qy, c))

        @pl.when(i == 0)
        def _():
            for a in range(ns):
                local(a).start()
                for r in range(3):
                    remote(a, r).start()

        for a in range(N_BIG):
            outs[a][...] = ins[a][...].astype(BF16)

        @pl.when(i == steps - 1)
        def _():
            for a in range(ns):
                for r, (qx, qy) in enumerate(others):
                    landed = block(a, 2 * qx + qy)
                    _remote(landed, landed, send_sems, recv_sems, 3 * a + r, (qx, qy, c)).wait_recv()
                for r in range(3):
                    remote(a, r).wait_send()
                local(a).wait()

    out = pl.pallas_call(
        body,
        grid_spec=pltpu.PrefetchScalarGridSpec(
            num_scalar_prefetch=1, grid=(steps,), in_specs=in_specs + [ANY] * ns, out_specs=out_specs + [ANY] * ns,
            scratch_shapes=[pltpu.SemaphoreType.DMA((3 * ns,)), pltpu.SemaphoreType.DMA((3 * ns,)),
                            pltpu.SemaphoreType.DMA((ns,))]),
        out_shape=([jax.ShapeDtypeStruct(_full_shape(name), BF16) for name in BIG_NAMES]
                   + [jax.ShapeDtypeStruct((s.shape[0], s.shape[1] * N_CHIPS), F32) for s in small]),
        name="place_shards", compiler_params=_params("arbitrary"),
    )(chip, *[w[name] for name in BIG_NAMES], *small)
    return dict(zip(BIG_NAMES, out[:N_BIG])), list(out[N_BIG:])


def _gather_weights(placed, small, names, label, collective_id):
    n, ns = len(names), len(small)
    hbm = pltpu.MemorySpace.HBM
    outs = [jax.new_ref(placed[nm], memory_space=hbm) for nm in names]
    small_in = [jax.new_ref(s, memory_space=hbm) for s in small]
    small_out = [jax.empty_ref(jax.ShapeDtypeStruct((s.shape[0], s.shape[1] * N_CHIPS), F32), memory_space=hbm)
                 for s in small]
    n_sems = 6 * n + 3 * ns

    @pl.kernel(mesh=plsc.ScalarSubcoreMesh(axis_name="seq", num_cores=1), name=label, out_type=(),
               scratch_types=(pltpu.SemaphoreType.DMA((n_sems,)), pltpu.SemaphoreType.DMA((n_sems,)),
                              pltpu.SemaphoreType.DMA((max(ns, 1),))),
               compiler_params=pltpu.CompilerParams(collective_id=collective_id))
    def launch(send_sems, recv_sems, local_sems):
        x, y, c = _place()
        chip = 2 * x + y
        sibling = (x, y, 1 - c)
        others = [_chip_of(x, y, r) for r in range(3)]
        _handshake([(qx, qy, c) for qx, qy in others] + [sibling])

        def small_block(a, q):
            cols = small[a].shape[1]
            return small_out[a].at[:, pl.ds(pl.multiple_of(q * cols, 128), cols)]

        local = [pltpu.make_async_copy(small_in[a], small_block(a, chip), local_sems.at[a]) for a in range(ns)]
        for cp in local:
            cp.start()

        sends = []
        for a, name in enumerate(names):
            mine = _shard_half_of(outs[a], name, chip, c)
            for r, (qx, qy) in enumerate(others):
                sends.append(_remote(mine, mine, send_sems, recv_sems, 6 * a + r, (qx, qy, c)))
        for a in range(ns):
            for r, (qx, qy) in enumerate(others):
                sends.append(_remote(small_in[a], small_block(a, chip), send_sems, recv_sems,
                                     6 * n + 3 * a + r, (qx, qy, c)))
        for cp in sends:
            cp.start()

        forwards = []
        for a, name in enumerate(names):
            for r, (qx, qy) in enumerate(others):
                landed = _shard_half_of(outs[a], name, 2 * qx + qy, c)
                _remote(landed, landed, send_sems, recv_sems, 6 * a + r, (qx, qy, c)).wait_recv()
                fwd = _remote(landed, landed, send_sems, recv_sems, 6 * a + 3 + r, sibling)
                fwd.start()
                forwards.append(fwd)
        for a in range(ns):
            for r, (qx, qy) in enumerate(others):
                landed = small_block(a, 2 * qx + qy)
                _remote(landed, landed, send_sems, recv_sems, 6 * n + 3 * a + r, (qx, qy, c)).wait_recv()
        for a, name in enumerate(names):
            for r, (qx, qy) in enumerate(others):
                landed = _shard_half_of(outs[a], name, 2 * qx + qy, 1 - c)
                _remote(landed, landed, send_sems, recv_sems, 6 * a + 3 + r, sibling).wait_recv()
        for cp in sends + forwards:
            cp.wait_send()
        for cp in local:
            cp.wait()

    launch()
    return {nm: ref[...] for nm, ref in zip(names, outs)}, [ref[...] for ref in small_out]


def _exchange_halves(grads, names, label, collective_id):
    n = len(names)
    sequencer = collective_id is not None

    def body(*refs):
        ins, outs = refs[:n], refs[n:2 * n]
        send_sems, recv_sems = refs[2 * n:]
        x, y, c = _place()
        if sequencer:
            _handshake([(x, y, 1 - c)])
        copies = []
        for a, name in enumerate(names):
            copies.append(_remote(_half_of(ins[a], BIG[name][2], 1 - c), outs[a], send_sems, recv_sems, a,
                                  (x, y, 1 - c)))
        for cp in copies:
            cp.start()
        for cp in copies:
            cp.wait()

    def half_shape(name):
        r, c_ = _full_shape(name)
        return (HALF, c_) if BIG[name][2] else (r, HALF)

    out_type = tuple(jax.ShapeDtypeStruct(half_shape(nm), F32) for nm in names)
    sems = (pltpu.SemaphoreType.DMA((n,)), pltpu.SemaphoreType.DMA((n,)))
    operands = [grads[nm] for nm in names]
    if sequencer:
        got = pl.kernel(
            body, mesh=plsc.ScalarSubcoreMesh(axis_name="seq", num_cores=1), name=label, out_type=out_type,
            scratch_types=sems, compiler_params=pltpu.CompilerParams(collective_id=collective_id),
        )(*operands)
    else:
        got = pl.pallas_call(
            body, in_specs=[ANY] * n, out_specs=[ANY] * n, out_shape=list(out_type), scratch_shapes=list(sems),
            name=label,
        )(*operands)
    return dict(zip(names, got))


def _chip_sum(grads, got, names, core, label):
    n = len(names)
    steps = 4
    g_specs, blks = [], []
    for name in names:
        rows, cols = got[name].shape
        tr = rows // steps
        if BIG[name][2]:
            g_specs.append(pl.BlockSpec((tr, cols), lambda i, s: (s[0] * steps + i, 0)))
        else:
            g_specs.append(pl.BlockSpec((tr, HALF), lambda i, s: (i, s[0])))
        blks.append(pl.BlockSpec((tr, cols), lambda i, s: (i, 0)))

    def body(s_ref, *refs):
        for a in range(n):
            t = refs[a][...] + refs[n + a][...]
            refs[2 * n + a][...] = t
            refs[3 * n + a][...] = t.astype(BF16)

    out = pl.pallas_call(
        body,
        grid_spec=pltpu.PrefetchScalarGridSpec(num_scalar_prefetch=1, grid=(steps,), in_specs=g_specs + blks,
                                               out_specs=blks + blks),
        out_shape=([jax.ShapeDtypeStruct(got[nm].shape, F32) for nm in names]
                   + [jax.ShapeDtypeStruct(got[nm].shape, BF16) for nm in names]),
        name=label, compiler_params=_params("parallel"),
    )(core, *[grads[nm] for nm in names], *[got[nm] for nm in names])
    return {nm: (out[a], out[n + a]) for a, nm in enumerate(names)}


def _piece_shape(name):
    rows, cols, by_col = BIG[name]
    return (HALF, cols) if by_col else (rows, HALF)


def _handshake(peers):
    barrier = pltpu.get_barrier_semaphore()
    for peer in peers:
        pl.semaphore_signal(barrier, inc=1, device_id=peer, device_id_type=MESH)
    pl.semaphore_wait(barrier, len(peers))


def _send_chip_sums(sums, names, label, collective_id):
    n = len(names)

    def body(*refs):
        ins, outs = refs[:n], refs[n:2 * n]
        send_sems, recv_sems = refs[2 * n:]
        x, y, c = _place()
        others = [_chip_of(x, y, r) for r in range(3)]
        _handshake([(qx, qy, c) for qx, qy in others])
        copies = []
        for a, name in enumerate(names):
            for r, (qx, qy) in enumerate(others):
                copies.append(_remote(_shard_of(ins[a], name, 2 * qx + qy), outs[a].at[r], send_sems, recv_sems,
                                      3 * a + r, (qx, qy, c)))
        for cp in copies:
            cp.start()
        for cp in copies:
            cp.wait()

    return pl.kernel(
        body, mesh=plsc.ScalarSubcoreMesh(axis_name="seq", num_cores=1), name=label,
        out_type=tuple(jax.ShapeDtypeStruct((3,) + _piece_shape(nm), BF16) for nm in names),
        scratch_types=(pltpu.SemaphoreType.DMA((3 * n,)), pltpu.SemaphoreType.DMA((3 * n,))),
        compiler_params=pltpu.CompilerParams(collective_id=collective_id),
    )(*[sums[nm] for nm in names])


def _total(parts, chip_core):
    steps = 2
    in_specs, out_specs, operands = [], [], []
    for name in BIG_NAMES:
        by_col = BIG[name][2]
        pr, pc = _piece_shape(name)
        tr = pr // steps
        if by_col:
            in_specs.append(pl.BlockSpec((tr, pc), lambda i, s: (i, s[0])))
            out_specs.append(pl.BlockSpec((tr, pc), lambda i, s: (s[1] * steps + i, 0)))
        else:
            in_specs.append(pl.BlockSpec((tr, pc), lambda i, s: (s[0] * steps + i, 0)))
            out_specs.append(pl.BlockSpec((tr, pc), lambda i, s: (i, s[1])))
        for r in range(3):
            in_specs.append(pl.BlockSpec((None, tr, pc), lambda i, s, r=r: (r, i, 0)))
        own, got = parts[name]
        operands += [own, got, got, got]

    def body(s_ref, *refs):
        for a in range(N_BIG):
            o_ref, a_ref, b_ref, c_ref = refs[4 * a:4 * a + 4]
            refs[4 * N_BIG + a][...] = (((o_ref[...] + a_ref[...].astype(F32)) + b_ref[...].astype(F32))
                                        + c_ref[...].astype(F32))

    totals = pl.pallas_call(
        body,
        grid_spec=pltpu.PrefetchScalarGridSpec(num_scalar_prefetch=1, grid=(steps,), in_specs=in_specs,
                                               out_specs=out_specs),
        out_shape=[jax.ShapeDtypeStruct(BIG[name][:2], F32) for name in BIG_NAMES],
        name="totals", compiler_params=_params("parallel"),
    )(chip_core, *operands)
    return dict(zip(BIG_NAMES, totals))


VEC_ROWS = 32
VEC_ROW = {"mix_norm_g": 0, "conv_b": 1, "b_rgate": 2, "b_igate": 3, "lru_lambda": 4, "rg_norm_g": 5,
           "hg_lower_bound": 6, "hg_norm_g": 8, "ffn_norm_g": 9, "final_norm_g": 10, "loss": 11,
           "conv_w": 12, "meta_tokens": 16}
N_DEV = 8


def _all_reduce_small(pieces, gates, totals):
    names = list(pieces)
    n_small = 10
    hv, hg = VEC_ROWS // 2, gates.shape[0] // 2

    def body(*refs):
        ins = refs[:len(names)]
        g_ref = refs[len(names)]
        vec_ref, gsum_ref = refs[len(names) + 1 + N_BIG:len(names) + 3 + N_BIG]
        big = refs[len(names) + 3 + N_BIG:len(names) + 3 + 2 * N_BIG]
        (mine_v, sib_v, sib_g, chip_v, chip_g, got_v, got_g, send_sems, recv_sems) = refs[len(names) + 3 + 2 * N_BIG:]
        x, y, c = _place()
        chip = 2 * x + y
        sibling = (x, y, 1 - c)
        share = []
        for a, name in enumerate(BIG_NAMES):
            half = _half_of(big[a], BIG[name][2], c)
            share.append(_remote(half, half, send_sems, recv_sems, n_small + a, sibling))
        for cp in share:
            cp.start()
        mine_v[...] = jnp.zeros_like(mine_v)
        for name, ref in zip(names, ins):
            nr, w = ref.shape
            mine_v[VEC_ROW[name]:VEC_ROW[name] + nr, 0:w] = ref[...]

        swap = [_remote(mine_v, sib_v, send_sems, recv_sems, 0, sibling),
                _remote(g_ref, sib_g, send_sems, recv_sems, 1, sibling)]
        for cp in swap:
            cp.start()
        for cp in swap:
            cp.wait()
        chip_v[...] = mine_v[...] + sib_v[...]
        chip_g[...] = g_ref[...] + sib_g[...]

        rows_v = pl.ds(pl.multiple_of(c * hv, 8), hv)
        rows_g = pl.ds(pl.multiple_of(c * hg, 8), hg)
        got_v[chip] = chip_v[rows_v, :]
        got_g[chip] = chip_g[rows_g, :]
        sends = []
        for r in range(3):
            qx, qy = _chip_of(x, y, r)
            sends.append(_remote(chip_v.at[rows_v, :], got_v.at[chip], send_sems, recv_sems, 2 + r, (qx, qy, c)))
            sends.append(_remote(chip_g.at[rows_g, :], got_g.at[chip], send_sems, recv_sems, 5 + r, (qx, qy, c)))
        for cp in sends:
            cp.start()
        for cp in sends:
            cp.wait()
        vec_ref[rows_v, :] = ((got_v[0] + got_v[1]) + got_v[2]) + got_v[3]
        gsum_ref[rows_g, :] = ((got_g[0] + got_g[1]) + got_g[2]) + got_g[3]

        back = [_remote(vec_ref.at[rows_v, :], vec_ref.at[rows_v, :], send_sems, recv_sems, 8, sibling),
                _remote(gsum_ref.at[rows_g, :], gsum_ref.at[rows_g, :], send_sems, recv_sems, 9, sibling)]
        for cp in back:
            cp.start()
        theirs_v = vec_ref.at[pl.ds(pl.multiple_of((1 - c) * hv, 8), hv), :]
        theirs_g = gsum_ref.at[pl.ds(pl.multiple_of((1 - c) * hg, 8), hg), :]
        _remote(theirs_v, theirs_v, send_sems, recv_sems, 8, sibling).wait_recv()
        _remote(theirs_g, theirs_g, send_sems, recv_sems, 9, sibling).wait_recv()
        for cp in back:
            cp.wait_send()
        for a, name in enumerate(BIG_NAMES):
            theirs = _half_of(big[a], BIG[name][2], 1 - c)
            _remote(theirs, theirs, send_sems, recv_sems, n_small + a, sibling).wait_recv()
        for cp in share:
            cp.wait_send()

    vmem = pl.BlockSpec(memory_space=pltpu.VMEM)
    n_sems = n_small + N_BIG
    out = pl.pallas_call(
        body, in_specs=[vmem] * (len(names) + 1) + [ANY] * N_BIG, out_specs=[vmem, vmem] + [ANY] * N_BIG,
        out_shape=([jax.ShapeDtypeStruct((VEC_ROWS, D_MODEL), F32), jax.ShapeDtypeStruct(gates.shape, F32)]
                   + [jax.ShapeDtypeStruct(BIG[n][:2], F32) for n in BIG_NAMES]),
        input_output_aliases={len(names) + 1 + a: 2 + a for a in range(N_BIG)},
        scratch_shapes=[pltpu.VMEM((VEC_ROWS, D_MODEL), F32), pltpu.VMEM((VEC_ROWS, D_MODEL), F32),
                        pltpu.VMEM(gates.shape, F32), pltpu.VMEM((VEC_ROWS, D_MODEL), F32),
                        pltpu.VMEM(gates.shape, F32), pltpu.VMEM((N_CHIPS, hv, D_MODEL), F32),
                        pltpu.VMEM((N_CHIPS, hg) + gates.shape[1:], F32),
                        pltpu.SemaphoreType.DMA((n_sems,)), pltpu.SemaphoreType.DMA((n_sems,))],
        name="all_reduce_small",
    )(*[pieces[n] for n in names], gates, *[totals[n] for n in BIG_NAMES])
    return out[0], out[1], dict(zip(BIG_NAMES, out[2:]))


def _adamw_math(w, g, m, v):
    m = ADAM_B1 * m + (1.0 - ADAM_B1) * g
    v = ADAM_B2 * v + (1.0 - ADAM_B2) * (g * g)
    m_hat = m / (1.0 - ADAM_B1 ** ADAM_STEP)
    v_hat = v / (1.0 - ADAM_B2 ** ADAM_STEP)
    delta = -ADAM_LR * (m_hat / (jnp.sqrt(v_hat) + ADAM_EPS) + ADAM_WD * w)
    return delta, m, v


def _adamw_big(w, g, m, v):
    steps = 8
    blks = []
    for name in BIG_NAMES:
        rows, cols, _ = BIG[name]
        blks.append(pl.BlockSpec((rows // steps, cols), lambda i: (i, 0)))

    def body(*refs):
        ins, outs = refs[:4 * N_BIG], refs[4 * N_BIG:]
        for a in range(N_BIG):
            w_ref, g_ref, m_ref, v_ref = (ins[k * N_BIG + a] for k in range(4))
            g = g_ref[...]
            d, nm, nv = _adamw_math(w_ref[...], g, m_ref[...], v_ref[...])
            outs[a][...] = g
            outs[N_BIG + a][...] = d
            outs[2 * N_BIG + a][...] = nm
            outs[3 * N_BIG + a][...] = nv

    shapes = [jax.ShapeDtypeStruct(BIG[name][:2], F32) for name in BIG_NAMES]
    out = pl.pallas_call(
        body, grid=(steps,), in_specs=blks * 4, out_specs=blks * 4, out_shape=shapes * 4,
        name="adamw_big", compiler_params=_params("parallel"),
    )(*[t[name] for t in (w, g, m, v) for name in BIG_NAMES])
    return {name: tuple(out[k * N_BIG + a] for k in range(4)) for a, name in enumerate(BIG_NAMES)}


SMALL = {"meta_tokens": (N_META, D_MODEL // N_CHIPS), "mix_norm_g": (1, D_MODEL), "conv_w": (CONV_W, D_RG // N_CHIPS),
         "conv_b": (1, D_RG), "w_rgate": (D_RG, RG_HEAD_DIM), "b_rgate": (1, D_RG), "w_igate": (D_RG, RG_HEAD_DIM),
         "b_igate": (1, D_RG), "lru_lambda": (1, D_RG), "rg_norm_g": (1, D_RG), "hg_lower_bound": (2, D_HG),
         "hg_norm_g": (1, HG_HEAD_DIM), "ffn_norm_g": (1, D_MODEL), "final_norm_g": (1, D_MODEL)}
SMALL_NAMES = tuple(SMALL)
SHARDED_SMALL = ("meta_tokens", "conv_w")


def _adamw_small(vec, gates, w, m, v):
    n = len(SMALL_NAMES)

    def body(*refs):
        vec_ref, gates_ref = refs[:2]
        w_refs, m_refs, v_refs = refs[2:2 + n], refs[2 + n:2 + 2 * n], refs[2 + 2 * n:2 + 3 * n]
        outs = refs[2 + 3 * n:]
        loss_ref = outs[0]
        x, y, _ = _place()
        chip = 2 * x + y
        loss_ref[...] = vec_ref[VEC_ROW["loss"]:VEC_ROW["loss"] + 1, 0:1]

        def update(k, g):
            g_ref, d_ref, nm_ref, nv_ref = outs[1 + 4 * k:5 + 4 * k]
            g_ref[...] = g
            d_ref[...], nm_ref[...], nv_ref[...] = _adamw_math(w_refs[k][...], g, m_refs[k][...], v_refs[k][...])

        for k, name in enumerate(SMALL_NAMES):
            nr, w_ = SMALL[name]
            if name == "w_rgate":
                update(k, gates_ref[0:D_RG, :])
            elif name == "w_igate":
                update(k, gates_ref[D_RG:2 * D_RG, :])
            elif name in SHARDED_SMALL:
                r0 = VEC_ROW[name]
                for q in range(N_CHIPS):
                    @pl.when(chip == q)
                    def _(k=k, r0=r0, nr=nr, w_=w_, q=q):
                        update(k, vec_ref[r0:r0 + nr, q * w_:(q + 1) * w_])
            else:
                r0 = VEC_ROW[name]
                update(k, vec_ref[r0:r0 + nr, 0:w_])

    vmem = pl.BlockSpec(memory_space=pltpu.VMEM)
    out_shape = [jax.ShapeDtypeStruct((1, 1), F32)]
    for name in SMALL_NAMES:
        out_shape += [jax.ShapeDtypeStruct(SMALL[name], F32)] * 4
    outs = pl.pallas_call(
        body, in_specs=[vmem] * (2 + 3 * n), out_specs=[vmem] * len(out_shape), out_shape=out_shape,
        name="adamw_small",
    )(vec, gates, *[w[k] for k in SMALL_NAMES], *[m[k] for k in SMALL_NAMES], *[v[k] for k in SMALL_NAMES])
    loss = outs[0]
    res = {name: tuple(outs[1 + 4 * k:5 + 4 * k]) for k, name in enumerate(SMALL_NAMES)}
    return loss, res


WEIGHT_NAMES = ("meta_tokens", "mix_norm_g", "w_in", "conv_w", "conv_b", "w_rgate", "b_rgate", "w_igate", "b_igate",
                "lru_lambda", "rg_norm_g", "hg_lower_bound", "hg_norm_g", "w_out", "ffn_norm_g", "w_gate_up", "w_down",
                "final_norm_g")


def _to_2d(name, a):
    if name in BIG:
        return a.reshape(BIG[name][:2])
    return a.reshape(SMALL[name])


def kernel(x, meta_tokens, mix_norm_g, w_in, conv_w, conv_b, w_rgate, b_rgate, w_igate, b_igate, lru_lambda, rg_norm_g, hg_lower_bound, hg_norm_g, w_out, ffn_norm_g, w_gate_up, w_down, final_norm_g, loss_target, m_meta_tokens, m_mix_norm_g, m_w_in, m_conv_w, m_conv_b, m_w_rgate, m_b_rgate, m_w_igate, m_b_igate, m_lru_lambda, m_rg_norm_g, m_hg_lower_bound, m_hg_norm_g, m_w_out, m_ffn_norm_g, m_w_gate_up, m_w_down, m_final_norm_g, v_meta_tokens, v_mix_norm_g, v_w_in, v_conv_w, v_conv_b, v_w_rgate, v_b_rgate, v_w_igate, v_b_igate, v_lru_lambda, v_rg_norm_g, v_hg_lower_bound, v_hg_norm_g, v_w_out, v_ffn_norm_g, v_w_gate_up, v_w_down, v_final_norm_g):
    w_raw = dict(zip(WEIGHT_NAMES, (meta_tokens, mix_norm_g, w_in, conv_w, conv_b, w_rgate, b_rgate, w_igate, b_igate,
                                    lru_lambda, rg_norm_g, hg_lower_bound, hg_norm_g, w_out, ffn_norm_g, w_gate_up,
                                    w_down, final_norm_g)))
    m_raw = dict(zip(WEIGHT_NAMES, (m_meta_tokens, m_mix_norm_g, m_w_in, m_conv_w, m_conv_b, m_w_rgate, m_b_rgate,
                                    m_w_igate, m_b_igate, m_lru_lambda, m_rg_norm_g, m_hg_lower_bound, m_hg_norm_g,
                                    m_w_out, m_ffn_norm_g, m_w_gate_up, m_w_down, m_final_norm_g)))
    v_raw = dict(zip(WEIGHT_NAMES, (v_meta_tokens, v_mix_norm_g, v_w_in, v_conv_w, v_conv_b, v_w_rgate, v_b_rgate,
                                    v_w_igate, v_b_igate, v_lru_lambda, v_rg_norm_g, v_hg_lower_bound, v_hg_norm_g,
                                    v_w_out, v_ffn_norm_g, v_w_gate_up, v_w_down, v_final_norm_g)))
    w = {k: _to_2d(k, a) for k, a in w_raw.items()}
    m = {k: _to_2d(k, a) for k, a in m_raw.items()}
    v = {k: _to_2d(k, a) for k, a in v_raw.items()}

    x_i, y_i, c_i = _place()
    core = jnp.reshape(c_i, (1,)).astype(jnp.int32)
    chip = jnp.reshape(2 * x_i + y_i, (1,)).astype(jnp.int32)
    chip_core = jnp.concatenate([chip, core])

    placed, (meta_full, cw_full) = _place_shards(w, [w["meta_tokens"], w["conv_w"]], chip)
    first, _ = _gather_weights(placed, [], ("w_in",), "gather_first", 1)
    rest, _ = _gather_weights(placed, [], ("w_out", "w_gate_up", "w_down"), "gather_rest", 2)
    full = {**first, **rest}

    seq = x.shape[1]
    small ={k: w[k] for k in SMALL_NAMES if k not in SHARDED_SMALL}
    small["conv_w"] = cw_full

    def reduce_to_chips(grads, names, tag, collective_ids):
        got = _exchange_halves(grads, names, "exchange_halves_" + tag, collective_ids[0])

        def chip_sums():
            return _chip_sum(grads, got, names, core, "chip_sum_" + tag)

        def send(sums):
            arrived = _send_chip_sums({n: sums[n][1] for n in names}, names, "send_chip_sums_" + tag,
                                      collective_ids[1])
            return {n: (sums[n][0], a) for n, a in zip(names, arrived)}

        return chip_sums, send

    ffn_names, mixer_names = ("w_gate_up", "w_down", "w_out"), ("w_in",)
    loss, grad_x, grads, parts, parts_mixer = _local_step(
        x.reshape(seq, D_MODEL), meta_full, loss_target.reshape(seq, D_MODEL),
        w["w_in"], full["w_in"], full["w_out"], full["w_gate_up"], full["w_down"], small, chip,
        on_ffn_grads=lambda g: reduce_to_chips(g, ffn_names, "ffn", (3, 4)),
        on_mixer_grads=lambda g: reduce_to_chips(g, mixer_names, "mixer", (None, 5)))
    parts.update(parts_mixer)
    totals = _total(parts, chip_core)
    pieces = {k: grads[k] for k in VEC_ROW if k != "loss"}
    pieces["loss"] = loss
    vec, gates, g_big = _all_reduce_small(pieces, grads["w_gates"], totals)
    loss_sum, res = _adamw_small(vec, gates, w, m, v)
    res.update(_adamw_big(w, g_big, m, v))

    out = [loss_sum.reshape(()), grad_x.reshape(1, seq, D_MODEL)]
    for j in range(4):
        out += [res[n][j].reshape(w_raw[n].shape) for n in WEIGHT_NAMES]
    return tuple(out)
```

```python
import math

import jax
import jax.numpy as jnp
from jax import lax
from jax.experimental import pallas as pl
from jax.experimental.pallas import tpu as pltpu
from jax.experimental.pallas import tpu_sc as plsc

F32 = jnp.float32
BF16 = jnp.bfloat16
HIGHEST = lax.Precision.HIGHEST
MESH = pl.DeviceIdType.MESH

D_MODEL = 1024
D_RG = 512
RG_HEAD_DIM = 64
D_HG = 512
HG_HEAD_DIM = 128
HG_HEADS = 4
CHUNK = 64
SUB = 16
N_SUB = CHUNK // SUB
N_META = 16
PAD = CHUNK - N_META
D_IN = 3072
D_FF = 2816
CONV_W = 4
LRU_C = 8.0
EPS = 1e-6
EXP_CLAMP = 80.0
GELU_C = math.sqrt(2.0 / math.pi)
GELU_A = 0.044715
N_CHIPS = 4

ADAM_LR = 0.001
ADAM_B1 = 0.9
ADAM_B2 = 0.999
ADAM_EPS = 1e-08
ADAM_WD = 0.01
ADAM_STEP = 10

VMEM_LIMIT = 56 * 1024 * 1024


def _params(*sem):
    return pltpu.CompilerParams(dimension_semantics=sem, vmem_limit_bytes=VMEM_LIMIT)


def _row_tile(rows, target):
    best = None
    for t in range(16, min(rows, target) + 1, 16):
        if rows % t == 0:
            best = t
    assert best is not None, rows
    return best


def _sigmoid(x):
    return 0.5 * jnp.tanh(0.5 * x) + 0.5


def _dot(a, b):
    return jnp.dot(a, b, preferred_element_type=F32)


def _dot_nt(a, b):
    return lax.dot_general(a, b, (((1,), (1,)), ((), ())), preferred_element_type=F32)


def _dot_tn(a, b):
    return lax.dot_general(a, b, (((0,), (0,)), ((), ())), preferred_element_type=F32)


def _rms(x):
    return lax.rsqrt(jnp.mean(x * x, axis=-1, keepdims=True) + EPS)


def _rms_bwd(dn, n, r):
    return r * (dn - n * jnp.mean(dn * n, axis=-1, keepdims=True))


def _gelu_parts(x):
    t = jnp.tanh(GELU_C * (x + GELU_A * x * x * x))
    g = 0.5 * x * (1.0 + t)
    dg = 0.5 * (1.0 + t) + 0.5 * x * (1.0 - t * t) * GELU_C * (1.0 + 3.0 * GELU_A * x * x)
    return g, dg


def _softplus_neg(lam):
    e = jnp.exp(-jnp.abs(lam))
    w = 1.0 + e
    log1p = jnp.where(w == 1.0, e, jnp.log(w) * e / (w - 1.0))
    return jnp.maximum(-lam, 0.0) + log1p


def _head_mask():
    r = lax.broadcasted_iota(jnp.int32, (D_RG, D_RG), 0) // RG_HEAD_DIM
    c = lax.broadcasted_iota(jnp.int32, (D_RG, D_RG), 1) // RG_HEAD_DIM
    return r == c


def _head_fold():
    r = lax.broadcasted_iota(jnp.int32, (D_RG, RG_HEAD_DIM), 0) % RG_HEAD_DIM
    c = lax.broadcasted_iota(jnp.int32, (D_RG, RG_HEAD_DIM), 1)
    return (r == c).astype(F32)


def _gate_weights(w_r, w_i):
    def body(wr_ref, wi_ref, o_ref):
        fold = _head_fold()
        mask = _head_mask()
        for k, ref in enumerate((wr_ref, wi_ref)):
            full = lax.dot_general(ref[...], fold, (((1,), (1,)), ((), ())),
                                   precision=HIGHEST, preferred_element_type=F32)
            o_ref[:, k * D_RG:(k + 1) * D_RG] = jnp.where(mask, full, 0.0).astype(BF16)

    return pl.pallas_call(
        body, out_shape=jax.ShapeDtypeStruct((D_RG, 2 * D_RG), BF16), name="gate_weights",
    )(w_r, w_i)


HEAD = PAD + N_META


def _window_copies(seq_hbm, buf, sems, tm):
    def first(to_vmem):
        seq, vm = seq_hbm.at[pl.ds(0, tm - HEAD)], buf.at[0, pl.ds(HEAD, tm - HEAD)]
        return pltpu.make_async_copy(seq, vm, sems.at[0]) if to_vmem else pltpu.make_async_copy(vm, seq, sems.at[0])

    def later(j, slot, to_vmem):
        seq, vm = seq_hbm.at[pl.ds(pl.multiple_of(j * tm - HEAD, 8), tm)], buf.at[slot]
        if to_vmem:
            return pltpu.make_async_copy(seq, vm, sems.at[slot])
        return pltpu.make_async_copy(vm, seq, sems.at[slot])

    return first, later


def _fetch_window(seq_hbm, buf, sems, i, n_steps, tm):
    first, later = _window_copies(seq_hbm, buf, sems, tm)
    slot = i % 2

    @pl.when(i == 0)
    def _():
        first(True).start()

    if n_steps > 1:
        @pl.when(i + 1 < n_steps)
        def _():
            later(i + 1, 1 - slot, True).start()

    @pl.when(i == 0)
    def _():
        first(True).wait()

    if n_steps > 1:
        @pl.when(i > 0)
        def _():
            later(i, slot, True).wait()

    return slot


def _in_proj_local(x, meta, g1, w_own, chip):
    T = x.shape[0] + HEAD
    tm = _row_tile(T, 832)
    n_steps = T // tm
    cols = BIG["w_in"][1]

    def body(s_ref, x_hbm, meta_ref, g_ref, w_ref, p_ref, u_ref, h_ref, buf, sems, wb):
        i = pl.program_id(0)
        slot = _fetch_window(x_hbm, buf, sems, i, n_steps, tm)

        @pl.when(i == 0)
        def _():
            buf[0, 0:PAD, :] = jnp.zeros((PAD, D_MODEL), F32)
            buf[0, PAD:HEAD, :] = meta_ref[...]
            wb[...] = w_ref[...].astype(BF16)

        h = buf[slot]
        h_ref[...] = h
        u = (h * _rms(h) * g_ref[...]).astype(BF16)
        u_ref[...] = u
        p_ref[...] = _dot(u, wb[...])

    return pl.pallas_call(
        body,
        grid_spec=pltpu.PrefetchScalarGridSpec(
            num_scalar_prefetch=1, grid=(n_steps,),
            in_specs=[pl.BlockSpec(memory_space=pl.ANY),
                      pl.BlockSpec((N_META, D_MODEL), lambda i, s: (0, 0)),
                      pl.BlockSpec((1, D_MODEL), lambda i, s: (0, 0)),
                      pl.BlockSpec((D_MODEL, cols), lambda i, s: (0, 0))],
            out_specs=[pl.BlockSpec((tm, cols), lambda i, s: (i, s[0])),
                       pl.BlockSpec((tm, D_MODEL), lambda i, s: (i, 0)),
                       pl.BlockSpec((tm, D_MODEL), lambda i, s: (i, 0))],
            scratch_shapes=[pltpu.VMEM((2, tm, D_MODEL), F32), pltpu.SemaphoreType.DMA((2,)),
                            pltpu.VMEM((D_MODEL, cols), BF16)]),
        out_shape=[jax.ShapeDtypeStruct((T, D_IN), F32), jax.ShapeDtypeStruct((T, D_MODEL), BF16),
                   jax.ShapeDtypeStruct((T, D_MODEL), F32)],
        name="in_proj_local", compiler_params=_params("arbitrary"),
    )(chip, x, meta, g1, w_own)


def _in_proj_rest(u, w_in, p, chip):
    T = u.shape[0]
    tm = _row_tile(T, 2080)
    cols = BIG["w_in"][1]
    block = lambda j, s: (s[0] + 1 + j) % N_CHIPS

    def body(s_ref, u_ref, w_ref, p_in_ref, p_ref):
        p_ref[...] = _dot(u_ref[...], w_ref[...])

    return pl.pallas_call(
        body,
        grid_spec=pltpu.PrefetchScalarGridSpec(
            num_scalar_prefetch=1, grid=(N_CHIPS - 1, T // tm),
            in_specs=[pl.BlockSpec((tm, D_MODEL), lambda j, i, s: (i, 0)),
                      pl.BlockSpec((D_MODEL, cols), lambda j, i, s: (0, block(j, s))), ANY],
            out_specs=pl.BlockSpec((tm, cols), lambda j, i, s: (i, block(j, s)))),
        out_shape=jax.ShapeDtypeStruct((T, D_IN), F32),
        input_output_aliases={3: 0},
        name="in_proj_rest", compiler_params=_params("arbitrary", "arbitrary"),
    )(chip, u, w_in, p)


def _scan_block_fwd(A, B, rowi):
    for d in (1, 2, 4):
        a_sh = pltpu.roll(A, d, axis=0)
        b_sh = pltpu.roll(B, d, axis=0)
        m = rowi >= d
        B = jnp.where(m, A * b_sh + B, B)
        A = jnp.where(m, A * a_sh, A)
    return A, B


def _scan_block_bwd(A, B, rowi):
    for d in (1, 2, 4):
        a_sh = pltpu.roll(A, 8 - d, axis=0)
        b_sh = pltpu.roll(B, 8 - d, axis=0)
        m = rowi < 8 - d
        B = jnp.where(m, A * b_sh + B, B)
        A = jnp.where(m, A * a_sh, A)
    return A, B


def _rg_gates(xc, w_ref, bg_ref, lam):
    pre = _dot(xc.astype(BF16), w_ref[...]) + bg_ref[...]
    r = _sigmoid(pre[:, :D_RG])
    ig = _sigmoid(pre[:, D_RG:])
    sp = _softplus_neg(lam)
    la = -LRU_C * sp * r
    a = jnp.exp(la)
    th = jnp.tanh(la)
    u = 1.0 - th
    rc = pl.reciprocal(u, approx=True)
    rc = rc * (2.0 - u * rc)
    rc = rc * (2.0 - u * rc)
    m2 = -2.0 * th * rc
    inv_m = lax.rsqrt(jnp.maximum(m2, 1e-30))
    return r, ig, sp, a, m2 * inv_m, inv_m


def _conv(ext, cw_ref, cb_ref, tm):
    xc = cb_ref[...] + cw_ref[0:1, :] * ext[8 - 3:8 - 3 + tm, :]
    for j in range(1, CONV_W):
        xc = xc + cw_ref[j:j + 1, :] * ext[8 - 3 + j:8 - 3 + j + tm, :]
    return xc


def _scan_unroll(blocks):
    return 4 if blocks % 4 == 0 else 2 if blocks % 2 == 0 else 1


def _rg_fwd(p, cw, cb, wg, bg, lam, rg_g):
    T = p.shape[0]
    tm = _row_tile(T, 832)
    unroll = _scan_unroll(tm // 8)

    def body(xg_ref, cw_ref, cb_ref, w_ref, bg_ref, lam_ref, g_ref, y_ref, h_ref, xc_ref, ext, a_s, b_s, carry):
        i = pl.program_id(0)

        @pl.when(i == 0)
        def _():
            ext[0:8, :] = jnp.zeros((8, D_RG), F32)
            carry[...] = jnp.zeros((1, D_RG), F32)

        ext[8:8 + tm, :] = xg_ref[:, :D_RG]
        xc = _conv(ext, cw_ref, cb_ref, tm)
        xc_ref[...] = xc
        r, ig, sp, a, m, _ = _rg_gates(xc, w_ref, bg_ref, lam_ref[...])
        row = i * tm + lax.broadcasted_iota(jnp.int32, (tm, 1), 0)
        a_s[...] = a
        b_s[...] = jnp.where(row >= PAD, m * ig * xc, 0.0)
        rowi = lax.broadcasted_iota(jnp.int32, (8, D_RG), 0)

        def blk(j, c):
            for u in range(unroll):
                o = pl.multiple_of((j * unroll + u) * 8, 8)
                A, B = _scan_block_fwd(a_s[pl.ds(o, 8), :], b_s[pl.ds(o, 8), :], rowi)
                h = B + A * c
                h_ref[pl.ds(o, 8), :] = h
                c = h[7:8, :]
            return c

        carry[...] = lax.fori_loop(0, tm // (8 * unroll), blk, carry[...])
        ext[0:8, :] = ext[tm:tm + 8, :]
        g, _ = _gelu_parts(xg_ref[:, D_RG:])
        yy = g * h_ref[...]
        y_ref[...] = (yy * _rms(yy) * g_ref[...]).astype(BF16)

    vec = lambda n: pl.BlockSpec((1, n), lambda i: (0, 0))
    return pl.pallas_call(
        body, grid=(T // tm,),
        in_specs=[pl.BlockSpec((tm, 2 * D_RG), lambda i: (i, 0)),
                  pl.BlockSpec((CONV_W, D_RG), lambda i: (0, 0)), vec(D_RG),
                  pl.BlockSpec((D_RG, 2 * D_RG), lambda i: (0, 0)), vec(2 * D_RG), vec(D_RG), vec(D_RG)],
        out_specs=[pl.BlockSpec((tm, D_RG), lambda i: (i, 0))] * 3,
        out_shape=[jax.ShapeDtypeStruct((T, D_RG), BF16), jax.ShapeDtypeStruct((T, D_RG), F32),
                   jax.ShapeDtypeStruct((T, D_RG), F32)],
        scratch_shapes=[pltpu.VMEM((tm + 8, D_RG), F32), pltpu.VMEM((tm, D_RG), F32),
                        pltpu.VMEM((tm, D_RG), F32), pltpu.VMEM((1, D_RG), F32)],
        name="rg_fwd", compiler_params=_params("arbitrary"),
    )(p, cw, cb, wg, bg, lam, rg_g)


def _tri(lower):
    r = lax.broadcasted_iota(jnp.int32, (CHUNK, CHUNK), 0)
    c = lax.broadcasted_iota(jnp.int32, (CHUNK, CHUNK), 1)
    return ((c <= r) if lower else (c >= r)).astype(F32)


def _hg_gates(hq, hf, lbraw_ref, valid):
    lb = _sigmoid(lbraw_ref[0:1, :] - lbraw_ref[1:2, :])
    sq = _sigmoid(hq)
    q = hq * sq
    sf = _sigmoid(hf)
    f = lb + (1.0 - lb) * sf
    lf = jnp.where(valid, jnp.log(f), 0.0)
    b = jnp.dot(_tri(True), lf, precision=HIGHEST, preferred_element_type=F32)
    return lb, sq, q, sf, f, b


def _hg_head(qh, kh, bh):
    blk = lax.broadcasted_iota(jnp.int32, (CHUNK, 1), 0) // SUB
    b_last = bh[CHUNK - 1:CHUNK, :]
    refs = [bh[SUB * s:SUB * s + 1, :] for s in range(N_SUB)]
    r_sel = refs[N_SUB - 1]
    for s in range(N_SUB - 2, -1, -1):
        r_sel = jnp.where(blk == s, refs[s], r_sel)
    eb = jnp.exp(bh)
    eq = jnp.exp(bh - r_sel)
    ekh = jnp.exp(b_last - bh)
    ek = [jnp.exp(jnp.minimum(refs[s] - bh, EXP_CLAMP)) for s in range(N_SUB)]
    qe = qh * eq
    q_hat = jnp.concatenate([jnp.where(blk == s, qe, 0.0) for s in range(N_SUB)], axis=1)
    k_til = jnp.concatenate([kh * ek[s] for s in range(N_SUB)], axis=1)
    return blk, b_last, eb, eq, ekh, ek, q_hat, k_til


def _causal():
    r = lax.broadcasted_iota(jnp.int32, (CHUNK, CHUNK), 0)
    c = lax.broadcasted_iota(jnp.int32, (CHUNK, CHUNK), 1)
    return r >= c


def _chunks_per_step(n_chunks):
    for c in (5, 4, 3, 2):
        if n_chunks % c == 0:
            return c
    return 1


def _hg_fwd(p, lbraw, hg_g):
    T = p.shape[0]
    n_chunks = T // CHUNK
    cps = _chunks_per_step(n_chunks)
    rows = cps * CHUNK

    def body(hq_ref, hf_ref, hi_ref, hg_ref, lb_ref, g_ref, y_ref, o_ref, st_all_ref, st):
        i = pl.program_id(0)

        @pl.when(i == 0)
        def _():
            st[...] = jnp.zeros_like(st)

        def chunk(j, carry):
            rs = pl.ds(pl.multiple_of(j * CHUNK, CHUNK), CHUNK)
            chunk_body(i * cps + j, hq_ref.at[rs, :], hf_ref.at[rs, :], hi_ref.at[rs, :], hg_ref.at[rs, :], lb_ref,
                       g_ref, y_ref.at[rs, :], o_ref.at[rs, :], st_all_ref.at[pl.ds(j, 1)], st)
            return carry

        lax.fori_loop(0, cps, chunk, 0, unroll=True)

    def chunk_body(n, hq_ref, hf_ref, hi_ref, hg_ref, lb_ref, g_ref, y_ref, o_ref, st_all_ref, st):
        valid = (n * CHUNK + lax.broadcasted_iota(jnp.int32, (CHUNK, 1), 0)) >= PAD
        hq, hf, v, hg = hq_ref[...], hf_ref[...], hi_ref[...], hg_ref[...]
        lb, sq, q, sf, f, b = _hg_gates(hq, hf, lb_ref, valid)
        k = 1.0 - f
        st_all_ref[0] = st[...]
        causal = _causal()
        v_t = v.T.astype(BF16)
        heads = [slice(h * HG_HEAD_DIM, (h + 1) * HG_HEAD_DIM) for h in range(HG_HEADS)]
        fac = []
        for sl in heads:
            qh, kh, bh = q[:, sl], k[:, sl], b[:, sl]
            _, b_last, eb, _, ekh, _, q_hat, k_til = _hg_head(qh, kh, bh)
            fac.append((jnp.exp(b_last), (qh * eb).astype(BF16), q_hat.astype(BF16), k_til.astype(BF16),
                        (kh * ekh).astype(BF16), v[:, sl].astype(BF16)))
        raw = []
        for sl, (_, q_til, q_hat, k_til, k_hat, _) in zip(heads, fac):
            st_h = st[sl, :]
            raw.append((_dot_nt(q_til, st_h.astype(BF16)), _dot_nt(q_hat, k_til), _dot(v_t[sl, :], k_hat), st_h))
        for sl, (e_last, _, _, _, _, vb), (inter, att, upd, st_h) in zip(heads, fac, raw):
            o = inter + _dot(jnp.where(causal, att, 0.0).astype(BF16), vb)
            st[sl, :] = st_h * e_last + upd
            o_ref[:, sl] = o
            hgh = hg[:, sl]
            y_ref[:, sl] = (o * _rms(o) * g_ref[...] * (hgh * _sigmoid(hgh))).astype(BF16)

    col = lambda j: pl.BlockSpec((rows, D_HG), lambda n: (n, j))
    return pl.pallas_call(
        body, grid=(n_chunks // cps,),
        in_specs=[col(2), col(3), col(4), col(5),
                  pl.BlockSpec((2, D_HG), lambda n: (0, 0)), pl.BlockSpec((1, HG_HEAD_DIM), lambda n: (0, 0))],
        out_specs=[pl.BlockSpec((rows, D_HG), lambda n: (n, 0)), pl.BlockSpec((rows, D_HG), lambda n: (n, 0)),
                   pl.BlockSpec((cps, D_HG, HG_HEAD_DIM), lambda n: (n, 0, 0))],
        out_shape=[jax.ShapeDtypeStruct((T, D_HG), BF16), jax.ShapeDtypeStruct((T, D_HG), F32),
                   jax.ShapeDtypeStruct((n_chunks, D_HG, HG_HEAD_DIM), F32)],
        scratch_shapes=[pltpu.VMEM((D_HG, HG_HEAD_DIM), F32)],
        name="hg_fwd", compiler_params=_params("arbitrary"),
    )(p, p, p, p, lbraw, hg_g)


def _ffn_fwd(h0, y_rg, y_hg, w_out, g2, w_gu, w_down, gf, target):
    T = h0.shape[0]
    tm = _row_tile(T, 320)
    n_steps = T // tm

    def body(h_ref, yr_ref, yh_ref, wo_ref, g2_ref, wgu_hbm, wd_hbm, gf_ref, t_hbm,
             h1_ref, v_ref, y_ref, gu_ref, act_ref, dh2_ref, dh2b_ref, loss_ref, gg_ref, tbuf, sems,
             wgu_ref, wd_ref, wsems):
        i = pl.program_id(0)
        late = [_late_weight(wgu_hbm, wgu_ref, wsems.at[0], i), _late_weight(wd_hbm, wd_ref, wsems.at[1], i)]
        for start, _ in late:
            start()
        slot = _fetch_window(t_hbm, tbuf, sems, i, n_steps, tm)

        @pl.when(i == 0)
        def _():
            loss_ref[...] = jnp.zeros_like(loss_ref)
            gg_ref[...] = jnp.zeros_like(gg_ref)
            tbuf[0, 0:HEAD, :] = jnp.zeros((HEAD, D_MODEL), F32)

        y_ref[:, :D_RG] = yr_ref[...]
        y_ref[:, D_RG:] = yh_ref[...]
        h1 = h_ref[...] + _dot(y_ref[...], wo_ref[...])
        h1_ref[...] = h1
        v = (h1 * _rms(h1) * g2_ref[...]).astype(BF16)
        v_ref[...] = v

        late[0][1]()
        gu = _dot(v, wgu_ref[...])
        gu_ref[...] = gu.astype(BF16)
        g = gu[:, :D_FF]
        act = (g * _sigmoid(g) * gu[:, D_FF:]).astype(BF16)
        act_ref[...] = act

        late[1][1]()
        h2 = h1 + _dot(act, wd_ref[...])
        r = _rms(h2)
        n = h2 * r
        gf_ = gf_ref[...]
        row = i * tm + lax.broadcasted_iota(jnp.int32, (tm, 1), 0)
        err = jnp.where(row >= HEAD, n * gf_ - tbuf[slot], 0.0)
        loss_ref[...] += 0.5 * jnp.sum(jnp.mean(err * err, axis=-1, keepdims=True), axis=0, keepdims=True)
        dy = err * (1.0 / D_MODEL)
        gg_ref[...] += jnp.sum(dy * n, axis=0, keepdims=True)
        dh2 = _rms_bwd(dy * gf_, n, r)
        dh2_ref[...] = dh2
        dh2b_ref[...] = dh2.astype(BF16)

    row_spec = lambda n: pl.BlockSpec((tm, n), lambda i: (i, 0))
    vec = pl.BlockSpec((1, D_MODEL), lambda i: (0, 0))
    return pl.pallas_call(
        body, grid=(n_steps,),
        in_specs=[row_spec(D_MODEL), row_spec(D_RG), row_spec(D_HG), _resident((D_MODEL, D_MODEL)), vec,
                  pl.BlockSpec(memory_space=pl.ANY), pl.BlockSpec(memory_space=pl.ANY), vec,
                  pl.BlockSpec(memory_space=pl.ANY)],
        out_specs=[row_spec(D_MODEL), row_spec(D_MODEL), row_spec(D_MODEL), row_spec(2 * D_FF), row_spec(D_FF),
                   row_spec(D_MODEL), row_spec(D_MODEL), pl.BlockSpec((1, 1), lambda i: (0, 0)), vec],
        out_shape=[jax.ShapeDtypeStruct((T, D_MODEL), F32), jax.ShapeDtypeStruct((T, D_MODEL), BF16),
                   jax.ShapeDtypeStruct((T, D_MODEL), BF16), jax.ShapeDtypeStruct((T, 2 * D_FF), BF16),
                   jax.ShapeDtypeStruct((T, D_FF), BF16), jax.ShapeDtypeStruct((T, D_MODEL), F32),
                   jax.ShapeDtypeStruct((T, D_MODEL), BF16), jax.ShapeDtypeStruct((1, 1), F32),
                   jax.ShapeDtypeStruct((1, D_MODEL), F32)],
        scratch_shapes=[pltpu.VMEM((2, tm, D_MODEL), F32), pltpu.SemaphoreType.DMA((2,)),
                        pltpu.VMEM((D_MODEL, 2 * D_FF), BF16), pltpu.VMEM((D_FF, D_MODEL), BF16),
                        pltpu.SemaphoreType.DMA((2,))],
        name="ffn_fwd", compiler_params=_params("arbitrary"),
    )(h0, y_rg, y_hg, w_out, g2, w_gu, w_down, gf, target)


def _resident(shape):
    return pl.BlockSpec(shape, lambda i: (0,) * len(shape), pipeline_mode=pl.Buffered(1))


def _late_weight(w_hbm, w_vmem, sem, step):
    copy = pltpu.make_async_copy(w_hbm, w_vmem, sem)

    def start():
        @pl.when(step == 0)
        def _():
            copy.start()

    def wait():
        @pl.when(step == 0)
        def _():
            copy.wait()

    return start, wait


def _ffn_bwd(dh2b, gu, w_down, w_gu, h1, g2, dh2, w_out):
    T = h1.shape[0]
    tm = _row_tile(T, 320)

    def body(d_ref, gu_ref, wd_ref, wgu_hbm, h_ref, g_ref, d2_ref, wo_hbm, dgu_ref, dh1_ref, dh1b_ref, dy_ref, gg_ref,
             wgu_ref, wo_ref, wsems):
        i = pl.program_id(0)
        late = [_late_weight(wgu_hbm, wgu_ref, wsems.at[0], i), _late_weight(wo_hbm, wo_ref, wsems.at[1], i)]
        for start, _ in late:
            start()

        @pl.when(i == 0)
        def _():
            gg_ref[...] = jnp.zeros_like(gg_ref)

        dact = _dot_nt(d_ref[...], wd_ref[...]).astype(BF16)
        g = gu_ref[:, :D_FF]
        u = gu_ref[:, D_FF:]
        s = _sigmoid(g)
        dgu_ref[:, :D_FF] = dact * u * (s * (1.0 + g * (1.0 - s)))
        dgu_ref[:, D_FF:] = dact * (g * s)

        late[0][1]()
        dv = _dot_nt(dgu_ref[...], wgu_ref[...])
        h1_ = h_ref[...]
        r = _rms(h1_)
        n = h1_ * r
        gg_ref[...] += jnp.sum(dv * n, axis=0, keepdims=True)
        dh1 = d2_ref[...] + _rms_bwd(dv * g_ref[...], n, r)
        dh1_ref[...] = dh1
        db = dh1.astype(BF16)
        dh1b_ref[...] = db
        late[1][1]()
        dy_ref[...] = _dot_nt(db, wo_ref[...])

    row = lambda n: pl.BlockSpec((tm, n), lambda i: (i, 0))
    return pl.pallas_call(
        body, grid=(T // tm,),
        in_specs=[row(D_MODEL), row(2 * D_FF), _resident((D_FF, D_MODEL)), pl.BlockSpec(memory_space=pl.ANY),
                  row(D_MODEL), pl.BlockSpec((1, D_MODEL), lambda i: (0, 0)), row(D_MODEL),
                  pl.BlockSpec(memory_space=pl.ANY)],
        out_specs=[row(2 * D_FF), row(D_MODEL), row(D_MODEL), row(D_MODEL),
                   pl.BlockSpec((1, D_MODEL), lambda i: (0, 0))],
        out_shape=[jax.ShapeDtypeStruct((T, 2 * D_FF), BF16), jax.ShapeDtypeStruct((T, D_MODEL), F32),
                   jax.ShapeDtypeStruct((T, D_MODEL), BF16), jax.ShapeDtypeStruct((T, D_MODEL), F32),
                   jax.ShapeDtypeStruct((1, D_MODEL), F32)],
        scratch_shapes=[pltpu.VMEM((D_MODEL, 2 * D_FF), BF16), pltpu.VMEM((D_MODEL, D_MODEL), BF16),
                        pltpu.SemaphoreType.DMA((2,))],
        name="ffn_bwd", compiler_params=_params("arbitrary"),
    )(dh2b, gu, w_down, w_gu, h1, g2, dh2, w_out)


def _rg_bwd(p, xc_all, hs, dy, dp, cw, cb, wg, bg, lam, rg_g):
    T = p.shape[0]
    tm = _row_tile(T, 832)
    nt = T // tm
    hb = tm // 8
    unroll = _scan_unroll(hb)

    def body(xg_ref, xc_ref, h_ref, hh_ref, dy_ref, dp_in_ref, cw_ref, cb_ref, w_ref, bg_ref, lam_ref, g_ref,
             dp_ref, gcw_ref, gcb_ref, gw_ref, gbg_ref, glam_ref, gg_ref,
             dext, a_s, b_s, d_s, gacc, carry_d, carry_a):
        i = pl.program_id(0)
        t_idx = nt - 1 - i

        @pl.when(i == 0)
        def _():
            dext[tm:tm + 8, :] = jnp.zeros((8, D_RG), F32)
            carry_d[...] = jnp.zeros_like(carry_d)
            carry_a[...] = jnp.zeros_like(carry_a)
            gacc[...] = jnp.zeros_like(gacc)
            for ref in (gcw_ref, gcb_ref, gbg_ref, glam_ref, gg_ref, gw_ref):
                ref[...] = jnp.zeros_like(ref)

        first = t_idx == 0
        xc = xc_ref[...]
        lam_ = lam_ref[...]
        r, ig, sp, a, m, inv_m = _rg_gates(xc, w_ref, bg_ref, lam_)
        row = t_idx * tm + lax.broadcasted_iota(jnp.int32, (tm, 1), 0)
        valid = row >= PAD

        gr = xg_ref[:, D_RG:]
        g, dgelu = _gelu_parts(gr)
        h = h_ref[...]
        yy = g * h
        rr = _rms(yy)
        nn = yy * rr
        dy_ = dy_ref[...]
        gg_ref[...] += jnp.sum(dy_ * nn, axis=0, keepdims=True)
        dyy = _rms_bwd(dy_ * g_ref[...], nn, rr)
        dp_ref[:, D_RG:] = (dyy * h * dgelu).astype(BF16)

        a_s[...] = a
        b_s[...] = dyy * g
        rowi = lax.broadcasted_iota(jnp.int32, (8, D_RG), 0)

        def blk(jj, c):
            cd, ca = c
            for u in range(unroll):
                o = pl.multiple_of((hb - 1 - (jj * unroll + u)) * 8, 8)
                a_blk = a_s[pl.ds(o, 8), :]
                a_next = jnp.where(rowi == 7, ca, pltpu.roll(a_blk, 7, axis=0))
                A, B = _scan_block_bwd(a_next, b_s[pl.ds(o, 8), :], rowi)
                d = B + A * cd
                d_s[pl.ds(o, 8), :] = d
                cd, ca = d[0:1, :], a_blk[0:1, :]
            return cd, ca

        cd, ca = lax.fori_loop(0, hb // unroll, blk, (carry_d[...], carry_a[...]))
        carry_d[...] = cd
        carry_a[...] = ca
        delta = d_s[...]

        h_last_prev = jnp.where(first, 0.0, hh_ref[7:8, :])
        row0 = lax.broadcasted_iota(jnp.int32, (tm, 1), 0) == 0
        h_prev = jnp.where(row0, h_last_prev, pltpu.roll(h, 1, axis=0))
        dbx = jnp.where(valid, delta, 0.0)
        da = delta * h_prev
        di = dbx * m * xc
        dm = dbx * ig * xc
        dla = a * (da - dm * a * inv_m)
        dla = jnp.where(valid, dla, 0.0)
        glam_ref[...] += jnp.sum(dla * r, axis=0, keepdims=True) * (LRU_C / (1.0 + jnp.exp(lam_)))
        dr = (-LRU_C) * sp * dla
        dpre = jnp.concatenate([dr * r * (1.0 - r), di * ig * (1.0 - ig)], axis=1)
        gbg_ref[...] += jnp.sum(dpre, axis=0, keepdims=True)
        dpre_b = dpre.astype(BF16)
        gacc[...] += _dot_tn(xc.astype(BF16), dpre_b)
        dxc = dbx * m * ig + _dot_nt(dpre_b, w_ref[...])
        gcb_ref[...] += jnp.sum(dxc, axis=0, keepdims=True)
        dext[0:tm, :] = dxc
        xr = xg_ref[:, :D_RG]
        dxr = None
        for j in range(CONV_W):
            shifted = dext[3 - j:3 - j + tm, :]
            gcw_ref[j:j + 1, :] += jnp.sum(xr * shifted, axis=0, keepdims=True)
            tap = cw_ref[j:j + 1, :] * shifted
            dxr = tap if dxr is None else dxr + tap
        dp_ref[:, :D_RG] = dxr.astype(BF16)
        dext[tm:tm + 8, :] = dext[0:8, :]

        @pl.when(i == nt - 1)
        def _():
            fold = _head_fold()
            mask = _head_mask()
            for k in range(2):
                blockdiag = jnp.where(mask, gacc[:, k * D_RG:(k + 1) * D_RG], 0.0)
                gw_ref[k * D_RG:(k + 1) * D_RG, :] = jnp.dot(blockdiag, fold, precision=HIGHEST,
                                                             preferred_element_type=F32)

    vec = lambda n: pl.BlockSpec((1, n), lambda i: (0, 0))
    rev = lambda n: pl.BlockSpec((tm, n), lambda i: (nt - 1 - i, 0))
    halo = lambda n: pl.BlockSpec((8, n), lambda i: (jnp.maximum((nt - 1 - i) * hb - 1, 0), 0))
    return pl.pallas_call(
        body, grid=(nt,),
        in_specs=[rev(2 * D_RG), rev(D_RG), rev(D_RG), halo(D_RG), rev(D_RG), ANY,
                  pl.BlockSpec((CONV_W, D_RG), lambda i: (0, 0)), vec(D_RG),
                  pl.BlockSpec((D_RG, 2 * D_RG), lambda i: (0, 0)), vec(2 * D_RG), vec(D_RG), vec(D_RG)],
        out_specs=[rev(2 * D_RG), pl.BlockSpec((CONV_W, D_RG), lambda i: (0, 0)), vec(D_RG),
                   pl.BlockSpec((2 * D_RG, RG_HEAD_DIM), lambda i: (0, 0)), vec(2 * D_RG), vec(D_RG), vec(D_RG)],
        input_output_aliases={5: 0},
        out_shape=[jax.ShapeDtypeStruct((T, D_IN), BF16), jax.ShapeDtypeStruct((CONV_W, D_RG), F32),
                   jax.ShapeDtypeStruct((1, D_RG), F32), jax.ShapeDtypeStruct((2 * D_RG, RG_HEAD_DIM), F32),
                   jax.ShapeDtypeStruct((1, 2 * D_RG), F32), jax.ShapeDtypeStruct((1, D_RG), F32),
                   jax.ShapeDtypeStruct((1, D_RG), F32)],
        scratch_shapes=[pltpu.VMEM((tm + 8, D_RG), F32),
                        pltpu.VMEM((tm, D_RG), F32), pltpu.VMEM((tm, D_RG), F32), pltpu.VMEM((tm, D_RG), F32),
                        pltpu.VMEM((D_RG, 2 * D_RG), F32), pltpu.VMEM((1, D_RG), F32), pltpu.VMEM((1, D_RG), F32)],
        name="rg_bwd", compiler_params=_params("arbitrary"),
    )(p, xc_all, hs, hs, dy, dp, cw, cb, wg, bg, lam, rg_g)


def _hg_bwd(p, o_all, st_all, dy, lbraw, hg_g):
    T = p.shape[0]
    n_chunks = T // CHUNK
    cps = _chunks_per_step(n_chunks)
    rows = cps * CHUNK
    n_steps = n_chunks // cps

    def body(hq_ref, hf_ref, hi_ref, hg_ref, o_ref, st_ref, dy_ref, lb_ref, g_ref,
             dp_ref, glb_ref, gg_ref, dst):
        i = pl.program_id(0)

        @pl.when(i == 0)
        def _():
            dst[...] = jnp.zeros_like(dst)
            glb_ref[...] = jnp.zeros_like(glb_ref)
            gg_ref[...] = jnp.zeros_like(gg_ref)

        dp_ref[:, :2 * D_RG] = jnp.zeros((rows, 2 * D_RG), BF16)

        def chunk(jj, carry):
            j = cps - 1 - jj
            rs = pl.ds(pl.multiple_of(j * CHUNK, CHUNK), CHUNK)
            chunk_body((n_steps - 1 - i) * cps + j, hq_ref.at[rs, :], hf_ref.at[rs, :], hi_ref.at[rs, :],
                       hg_ref.at[rs, :], o_ref.at[rs, :], st_ref.at[pl.ds(j, 1)], dy_ref.at[rs, :], lb_ref, g_ref,
                       dp_ref.at[rs, pl.ds(2 * D_RG, 4 * D_HG)], glb_ref, gg_ref, dst)
            return carry

        lax.fori_loop(0, cps, chunk, 0, unroll=True)

    def chunk_body(n, hq_ref, hf_ref, hi_ref, hg_ref, o_ref, st_ref, dy_ref, lb_ref, g_ref,
                   dp_ref, glb_ref, gg_ref, dst):
        valid = (n * CHUNK + lax.broadcasted_iota(jnp.int32, (CHUNK, 1), 0)) >= PAD
        hq, hf, v, hg = hq_ref[...], hf_ref[...], hi_ref[...], hg_ref[...]
        lb, sq, q, sf, f, b = _hg_gates(hq, hf, lb_ref, valid)
        k = 1.0 - f
        causal = _causal()
        r_i = lax.broadcasted_iota(jnp.int32, (CHUNK, CHUNK), 0)
        c_i = lax.broadcasted_iota(jnp.int32, (CHUNK, CHUNK), 1)
        causal_t = r_i <= c_i
        is_last = lax.broadcasted_iota(jnp.int32, (CHUNK, 1), 0) == CHUNK - 1
        g_ = g_ref[...]
        db_parts, dq_parts, dk_parts = [], [], []
        gg = jnp.zeros((1, HG_HEAD_DIM), F32)
        heads = [slice(h * HG_HEAD_DIM, (h + 1) * HG_HEAD_DIM) for h in range(HG_HEADS)]

        do_parts = []
        for h, sl in enumerate(heads):
            o = o_ref[:, sl]
            ro = _rms(o)
            no = o * ro
            hgh = hg[:, sl]
            sg = _sigmoid(hgh)
            dyh = dy_ref[:, sl]
            dp_ref[:, 3 * D_HG + h * HG_HEAD_DIM:3 * D_HG + (h + 1) * HG_HEAD_DIM] = (
                dyh * no * g_ * sg * (1.0 + hgh * (1.0 - sg))).astype(BF16)
            dng = dyh * hgh * sg
            gg = gg + jnp.sum(dng * no, axis=0, keepdims=True)
            do_parts.append(_rms_bwd(dng * g_, no, ro))
        do_t = jnp.concatenate(do_parts, axis=1).T.astype(BF16)

        fac = []
        for sl, do in zip(heads, do_parts):
            qh, kh, bh = q[:, sl], k[:, sl], b[:, sl]
            blk, b_last, eb, eq, ekh, ek, q_hat, k_til = _hg_head(qh, kh, bh)
            fac.append(dict(qh=qh, kh=kh, blk=blk, e_last=jnp.exp(b_last), eb=eb, eq=eq, ekh=ekh, ek=ek,
                            q_til=qh * eb, k_hat=kh * ekh, qhb=q_hat.astype(BF16), ktb=k_til.astype(BF16),
                            vb=v[:, sl].astype(BF16), dob=do.astype(BF16)))

        first = []
        for sl, t in zip(heads, fac):
            st_h = st_ref[0, sl, :]
            dst_h = dst[sl, :]
            dstb = dst_h.astype(BF16)
            first.append(dict(
                att_t=_dot_nt(t["ktb"], t["qhb"]), datt=_dot_nt(t["dob"], t["vb"]),
                datt_t=_dot_nt(t["vb"], t["dob"]), dk_hat=_dot(t["vb"], dstb),
                dv=_dot_nt(t["k_hat"].astype(BF16), dstb), dq_til=_dot(t["dob"], st_h.astype(BF16)),
                state=t["e_last"] * jnp.sum(dst_h * st_h, axis=0, keepdims=True)))
            dst[sl, :] = dst_h * t["e_last"] + _dot(do_t[sl, :], t["q_til"].astype(BF16))

        for h, (t, m) in enumerate(zip(fac, first)):
            qh, kh, blk, eb, eq, ekh, ek = t["qh"], t["kh"], t["blk"], t["eb"], t["eq"], t["ekh"], t["ek"]
            q_til, k_hat, qhb, ktb, dob = t["q_til"], t["k_hat"], t["qhb"], t["ktb"], t["dob"]
            dk_hat, dq_til = m["dk_hat"], m["dq_til"]
            dv = m["dv"] + _dot(jnp.where(causal_t, m["att_t"], 0.0).astype(BF16), dob)
            dq_hat = _dot(jnp.where(causal, m["datt"], 0.0).astype(BF16), ktb)
            dk_til = _dot(jnp.where(causal_t, m["datt_t"], 0.0).astype(BF16), qhb)
            db_last = jnp.sum(dk_hat * k_hat, axis=0, keepdims=True) + m["state"]
            dq_sel = dq_hat[:, (N_SUB - 1) * HG_HEAD_DIM:]
            for s in range(N_SUB - 2, -1, -1):
                dq_sel = jnp.where(blk == s, dq_hat[:, s * HG_HEAD_DIM:(s + 1) * HG_HEAD_DIM], dq_sel)
            dq_a = dq_sel * eq
            dk_a = dk_til[:, :HG_HEAD_DIM] * ek[0]
            for s in range(1, N_SUB):
                dk_a = dk_a + dk_til[:, s * HG_HEAD_DIM:(s + 1) * HG_HEAD_DIM] * ek[s]
            db_att = qhb.astype(F32) * dq_hat - ktb.astype(F32) * dk_til
            db = dq_til * q_til - dk_hat * k_hat
            for s in range(N_SUB):
                db = db + db_att[:, s * HG_HEAD_DIM:(s + 1) * HG_HEAD_DIM]
            db_parts.append(jnp.where(is_last, db + db_last, db))
            dq_parts.append(dq_til * eb + dq_a)
            dk_parts.append(dk_hat * ekh + dk_a)
            dp_ref[:, 2 * D_HG + h * HG_HEAD_DIM:2 * D_HG + (h + 1) * HG_HEAD_DIM] = dv.astype(BF16)

        gg_ref[...] += gg
        db = jnp.concatenate(db_parts, axis=1)
        dq = jnp.concatenate(dq_parts, axis=1)
        dk = jnp.concatenate(dk_parts, axis=1)
        dlf = jnp.where(valid, jnp.dot(_tri(False), db, precision=HIGHEST, preferred_element_type=F32), 0.0)
        dp_ref[:, :D_HG] = (dq * sq * (1.0 + hq * (1.0 - sq))).astype(BF16)
        df = dlf / f - dk
        dlb = jnp.sum(df * (1.0 - sf), axis=0, keepdims=True) * lb * (1.0 - lb)
        glb_ref[0:1, :] += dlb
        glb_ref[1:2, :] += -dlb
        dp_ref[:, D_HG:2 * D_HG] = (df * (1.0 - lb) * sf * (1.0 - sf)).astype(BF16)

    rev = lambda j: pl.BlockSpec((rows, D_HG), lambda i: (n_steps - 1 - i, j))
    return pl.pallas_call(
        body, grid=(n_steps,),
        in_specs=[rev(2), rev(3), rev(4), rev(5), rev(0),
                  pl.BlockSpec((cps, D_HG, HG_HEAD_DIM), lambda i: (n_steps - 1 - i, 0, 0)), rev(1),
                  pl.BlockSpec((2, D_HG), lambda i: (0, 0)), pl.BlockSpec((1, HG_HEAD_DIM), lambda i: (0, 0))],
        out_specs=[pl.BlockSpec((rows, D_IN), lambda i: (n_steps - 1 - i, 0)),
                   pl.BlockSpec((2, D_HG), lambda i: (0, 0)), pl.BlockSpec((1, HG_HEAD_DIM), lambda i: (0, 0))],
        out_shape=[jax.ShapeDtypeStruct((T, D_IN), BF16), jax.ShapeDtypeStruct((2, D_HG), F32),
                   jax.ShapeDtypeStruct((1, HG_HEAD_DIM), F32)],
        scratch_shapes=[pltpu.VMEM((D_HG, HG_HEAD_DIM), F32)],
        name="hg_bwd", compiler_params=_params("arbitrary"),
    )(p, p, p, p, o_all, st_all, dy, lbraw, hg_g)


def _in_bwd(dp, w_in, h0, g1, dh1):
    T = h0.shape[0]
    tm = _row_tile(T, 416)
    n_steps = T // tm

    def body(dp_ref, w_ref, h_ref, g_ref, d1_ref, gx_hbm, gmeta_ref, gg_ref, buf, sems):
        i = pl.program_id(0)
        first, later = _window_copies(gx_hbm, buf, sems, tm)
        slot = i % 2

        @pl.when(i == 0)
        def _():
            gg_ref[...] = jnp.zeros_like(gg_ref)

        if n_steps > 2:
            @pl.when(i == 2)
            def _():
                first(False).wait()

            @pl.when(i > 2)
            def _():
                later(i - 2, slot, False).wait()

        du = _dot_nt(dp_ref[...], w_ref[...])
        h0_ = h_ref[...]
        r = _rms(h0_)
        n = h0_ * r
        gg_ref[...] += jnp.sum(du * n, axis=0, keepdims=True)
        dh0 = d1_ref[...] + _rms_bwd(du * g_ref[...], n, r)
        buf[slot] = dh0

        @pl.when(i == 0)
        def _():
            gmeta_ref[...] = dh0[PAD:HEAD, :]
            first(False).start()

        if n_steps > 1:
            @pl.when(i > 0)
            def _():
                later(i, slot, False).start()

        @pl.when(i == n_steps - 1)
        def _():
            if n_steps == 1:
                first(False).wait()
            else:
                if n_steps == 2:
                    first(False).wait()
                else:
                    later(i - 1, 1 - slot, False).wait()
                later(i, slot, False).wait()

    row = lambda n: pl.BlockSpec((tm, n), lambda i: (i, 0))
    return pl.pallas_call(
        body, grid=(n_steps,),
        in_specs=[row(D_IN), _resident((D_MODEL, D_IN)),
                  row(D_MODEL), pl.BlockSpec((1, D_MODEL), lambda i: (0, 0)), row(D_MODEL)],
        out_specs=[pl.BlockSpec(memory_space=pl.ANY), pl.BlockSpec((N_META, D_MODEL), lambda i: (0, 0)),
                   pl.BlockSpec((1, D_MODEL), lambda i: (0, 0))],
        out_shape=[jax.ShapeDtypeStruct((T - HEAD, D_MODEL), F32), jax.ShapeDtypeStruct((N_META, D_MODEL), F32),
                   jax.ShapeDtypeStruct((1, D_MODEL), F32)],
        scratch_shapes=[pltpu.VMEM((2, tm, D_MODEL), F32), pltpu.SemaphoreType.DMA((2,))],
        name="in_bwd", compiler_params=_params("arbitrary"),
    )(dp, w_in, h0, g1, dh1)


def _col_tile(cols, target):
    best = None
    for t in range(128, min(cols, target) + 1, 128):
        if cols % t == 0:
            best = t
    assert best is not None, cols
    return best


MXU_DIM = 256


def _mxu_tile(cols, target):
    best = None
    for t in range(MXU_DIM, min(cols, target) + 1, MXU_DIM):
        if cols % t == 0:
            best = t
    assert best is not None, cols
    return best


def _weight_grad(a, b, name):
    T, M = a.shape
    N = b.shape[1]
    tm = _col_tile(M, 1408)
    tn = _mxu_tile(N, 768 if tm <= 1024 else 512)

    def body(a_ref, b_ref, o_ref):
        o_ref[...] = _dot_tn(a_ref[...], b_ref[...])

    return pl.pallas_call(
        body, grid=(M // tm, N // tn),
        in_specs=[pl.BlockSpec((T, tm), lambda m, n: (0, m)), pl.BlockSpec((T, tn), lambda m, n: (0, n))],
        out_specs=pl.BlockSpec((tm, tn), lambda m, n: (m, n)),
        out_shape=jax.ShapeDtypeStruct((M, N), F32),
        name=name, compiler_params=_params("parallel", "parallel"),
    )(a, b)


def _local_step(x, meta, target, w_in_own, w_in, w_out, w_gu, w_down, small, chip, on_ffn_grads=None,
                on_mixer_grads=None):
    wg = _gate_weights(small["w_rgate"], small["w_igate"])
    bg = jnp.concatenate([small["b_rgate"], small["b_igate"]], axis=1)

    p, u, h0 = _in_proj_local(x, meta, small["mix_norm_g"], w_in_own, chip)
    p = _in_proj_rest(u, w_in, p, chip)
    y_rg, hs, xc = _rg_fwd(p, small["conv_w"], small["conv_b"], wg, bg, small["lru_lambda"], small["rg_norm_g"])
    y_hg, o_all, st_all = _hg_fwd(p, small["hg_lower_bound"], small["hg_norm_g"])
    h1, v, yb, gu, act, dh2, dh2b, loss, g_final = _ffn_fwd(
        h0, y_rg, y_hg, w_out, small["ffn_norm_g"], w_gu, w_down, small["final_norm_g"], target)

    g_w_down = _weight_grad(act, dh2b, "grad_w_down")
    dgu, dh1, dh1b, dy, g_ffn = _ffn_bwd(dh2b, gu, w_down, w_gu, h1, small["ffn_norm_g"], dh2, w_out)
    ffn_grads = {"w_gate_up": _weight_grad(v, dgu, "grad_w_gate_up"), "w_down": g_w_down,
                 "w_out": _weight_grad(yb, dh1b, "grad_w_out")}
    stages = on_ffn_grads(ffn_grads) if on_ffn_grads is not None else None
    dp, g_lb, g_hgn = _hg_bwd(p, o_all, st_all, dy, small["hg_lower_bound"], small["hg_norm_g"])
    early = late = None
    if stages is not None:
        chip_sums, send = stages
        sums = chip_sums()
        (dp, dy), sums = lax.optimization_barrier(((dp, dy), sums))
        early = send(sums)
    dp, g_cw, g_cb, g_wgate, g_bg, g_lam, g_rgn = _rg_bwd(
        p, xc, hs, dy, dp, small["conv_w"], small["conv_b"], wg, bg, small["lru_lambda"], small["rg_norm_g"])
    mixer_grads = {"w_in": _weight_grad(u, dp, "grad_w_in")}
    if on_mixer_grads is not None:
        chip_sums, send = on_mixer_grads(mixer_grads)
        sums = chip_sums()
        (dp, dh1), sums = lax.optimization_barrier(((dp, dh1), sums))
        late = send(sums)
    grad_x, g_meta, g_mix = _in_bwd(dp, w_in, h0, small["mix_norm_g"], dh1)

    grads = {
        "w_in": mixer_grads["w_in"], "w_out": ffn_grads["w_out"],
        "w_gate_up": ffn_grads["w_gate_up"], "w_down": ffn_grads["w_down"],
        "meta_tokens": g_meta, "mix_norm_g": g_mix, "conv_w": g_cw, "conv_b": g_cb, "w_gates": g_wgate,
        "b_rgate": g_bg[:, :D_RG], "b_igate": g_bg[:, D_RG:], "lru_lambda": g_lam, "rg_norm_g": g_rgn,
        "hg_lower_bound": g_lb, "hg_norm_g": g_hgn, "ffn_norm_g": g_ffn, "final_norm_g": g_final,
    }
    return loss, grad_x, grads, early, late


ANY = pl.BlockSpec(memory_space=pl.ANY)
HALF = D_MODEL // 2

BIG = {"w_in": (D_MODEL, D_IN // N_CHIPS, True), "w_gate_up": (D_MODEL, 2 * D_FF // N_CHIPS, True),
       "w_out": (D_MODEL // N_CHIPS, D_MODEL, False), "w_down": (D_FF // N_CHIPS, D_MODEL, False)}
BIG_NAMES = tuple(BIG)
N_BIG = len(BIG_NAMES)


def _full_shape(name):
    rows, cols, by_col = BIG[name]
    return (rows, cols * N_CHIPS) if by_col else (rows * N_CHIPS, cols)


def _place():
    return lax.axis_index("x"), lax.axis_index("y"), lax.axis_index("c")


def _chip_of(x, y, r):
    fx, fy = (r + 1) >> 1, (r + 1) & 1
    return (1 - x if fx else x), (1 - y if fy else y)


def _half_of(ref, by_col, half):
    start = pl.multiple_of(half * HALF, 128)
    return ref.at[pl.ds(start, HALF), :] if by_col else ref.at[:, pl.ds(start, HALF)]


def _shard_of(ref, name, chip):
    rows, cols, by_col = BIG[name]
    if by_col:
        return ref.at[:, pl.ds(pl.multiple_of(chip * cols, 128), cols)]
    return ref.at[pl.ds(pl.multiple_of(chip * rows, 16), rows), :]


def _shard_half_of(ref, name, chip, half):
    rows, cols, by_col = BIG[name]
    start = pl.multiple_of(half * HALF, 128)
    if by_col:
        return ref.at[pl.ds(start, HALF), pl.ds(pl.multiple_of(chip * cols, 128), cols)]
    return ref.at[pl.ds(pl.multiple_of(chip * rows, 16), rows), pl.ds(start, HALF)]


def _remote(src, dst, send_sems, recv_sems, k, dev):
    return pltpu.make_async_remote_copy(src_ref=src, dst_ref=dst, send_sem=send_sems.at[k], recv_sem=recv_sems.at[k],
                                        device_id=dev, device_id_type=MESH)


def _place_shards(w, small, chip):
    steps = 4
    ns = len(small)
    in_specs, out_specs = [], []
    for name in BIG_NAMES:
        rows, cols, by_col = BIG[name]
        tr = rows // steps
        in_specs.append(pl.BlockSpec((tr, cols), lambda i, s: (i, 0)))
        if by_col:
            out_specs.append(pl.BlockSpec((tr, cols), lambda i, s: (i, s[0])))
        else:
            out_specs.append(pl.BlockSpec((tr, cols), lambda i, s: (s[0] * steps + i, 0)))

    def body(s_ref, *refs):
        ins, small_in = refs[:N_BIG], refs[N_BIG:N_BIG + ns]
        outs, small_out = refs[N_BIG + ns:2 * N_BIG + ns], refs[2 * N_BIG + ns:2 * (N_BIG + ns)]
        send_sems, recv_sems, local_sems = refs[2 * (N_BIG + ns):]
        i = pl.program_id(0)
        x, y, c = _place()
        chip_ = 2 * x + y
        others = [_chip_of(x, y, r) for r in range(3)]

        def block(a, q):
            cols = small[a].shape[1]
            return small_out[a].at[:, pl.ds(pl.multiple_of(q * cols, 128), cols)]

        def local(a):
            return pltpu.make_async_copy(small_in[a], block(a, chip_), local_sems.at[a])

        def remote(a, r):
            qx, qy = others[r]
            return _remote(small_in[a], block(a, chip_), send_sems, recv_sems, 3 * a + r, (qx, qy, c))

        @pl.when(i == 0)
        def _():
            for a in range(ns):
                local(a).start()
                for r in range(3):
                    remote(a, r).start()

        for a in range(N_BIG):
            outs[a][...] = ins[a][...].astype(BF16)

        @pl.when(i == steps - 1)
        def _():
            for a in range(ns):
                for r, (qx, qy) in enumerate(others):
                    landed = block(a, 2 * qx + qy)
                    _remote(landed, landed, send_sems, recv_sems, 3 * a + r, (qx, qy, c)).wait_recv()
                for r in range(3):
                    remote(a, r).wait_send()
                local(a).wait()

    out = pl.pallas_call(
        body,
        grid_spec=pltpu.PrefetchScalarGridSpec(
            num_scalar_prefetch=1, grid=(steps,), in_specs=in_specs + [ANY] * ns, out_specs=out_specs + [ANY] * ns,
            scratch_shapes=[pltpu.SemaphoreType.DMA((3 * ns,)), pltpu.SemaphoreType.DMA((3 * ns,)),
                            pltpu.SemaphoreType.DMA((ns,))]),
        out_shape=([jax.ShapeDtypeStruct(_full_shape(name), BF16) for name in BIG_NAMES]
                   + [jax.ShapeDtypeStruct((s.shape[0], s.shape[1] * N_CHIPS), F32) for s in small]),
        name="place_shards", compiler_params=_params("arbitrary"),
    )(chip, *[w[name] for name in BIG_NAMES], *small)
    return dict(zip(BIG_NAMES, out[:N_BIG])), list(out[N_BIG:])


def _gather_weights(placed, small, names, label, collective_id):
    n, ns = len(names), len(small)
    hbm = pltpu.MemorySpace.HBM
    outs = [jax.new_ref(placed[nm], memory_space=hbm) for nm in names]
    small_in = [jax.new_ref(s, memory_space=hbm) for s in small]
    small_out = [jax.empty_ref(jax.ShapeDtypeStruct((s.shape[0], s.shape[1] * N_CHIPS), F32), memory_space=hbm)
                 for s in small]
    n_sems = 6 * n + 3 * ns

    @pl.kernel(mesh=plsc.ScalarSubcoreMesh(axis_name="seq", num_cores=1), name=label, out_type=(),
               scratch_types=(pltpu.SemaphoreType.DMA((n_sems,)), pltpu.SemaphoreType.DMA((n_sems,)),
                              pltpu.SemaphoreType.DMA((max(ns, 1),))),
               compiler_params=pltpu.CompilerParams(collective_id=collective_id))
    def launch(send_sems, recv_sems, local_sems):
        x, y, c = _place()
        chip = 2 * x + y
        sibling = (x, y, 1 - c)
        others = [_chip_of(x, y, r) for r in range(3)]
        _handshake([(qx, qy, c) for qx, qy in others] + [sibling])

        def small_block(a, q):
            cols = small[a].shape[1]
            return small_out[a].at[:, pl.ds(pl.multiple_of(q * cols, 128), cols)]

        local = [pltpu.make_async_copy(small_in[a], small_block(a, chip), local_sems.at[a]) for a in range(ns)]
        for cp in local:
            cp.start()

        sends = []
        for a, name in enumerate(names):
            mine = _shard_half_of(outs[a], name, chip, c)
            for r, (qx, qy) in enumerate(others):
                sends.append(_remote(mine, mine, send_sems, recv_sems, 6 * a + r, (qx, qy, c)))
        for a in range(ns):
            for r, (qx, qy) in enumerate(others):
                sends.append(_remote(small_in[a], small_block(a, chip), send_sems, recv_sems,
                                     6 * n + 3 * a + r, (qx, qy, c)))
        for cp in sends:
            cp.start()

        forwards = []
        for a, name in enumerate(names):
            for r, (qx, qy) in enumerate(others):
                landed = _shard_half_of(outs[a], name, 2 * qx + qy, c)
                _remote(landed, landed, send_sems, recv_sems, 6 * a + r, (qx, qy, c)).wait_recv()
                fwd = _remote(landed, landed, send_sems, recv_sems, 6 * a + 3 + r, sibling)
                fwd.start()
                forwards.append(fwd)
        for a in range(ns):
            for r, (qx, qy) in enumerate(others):
                landed = small_block(a, 2 * qx + qy)
                _remote(landed, landed, send_sems, recv_sems, 6 * n + 3 * a + r, (qx, qy, c)).wait_recv()
        for a, name in enumerate(names):
            for r, (qx, qy) in enumerate(others):
                landed = _shard_half_of(outs[a], name, 2 * qx + qy, 1 - c)
                _remote(landed, landed, send_sems, recv_sems, 6 * a + 3 + r, sibling).wait_recv()
        for cp in sends + forwards:
            cp.wait_send()
        for cp in local:
            cp.wait()

    launch()
    return {nm: ref[...] for nm, ref in zip(names, outs)}, [ref[...] for ref in small_out]


def _exchange_halves(grads, names, label, collective_id):
    n = len(names)
    sequencer = collective_id is not None

    def body(*refs):
        ins, outs = refs[:n], refs[n:2 * n]
        send_sems, recv_sems = refs[2 * n:]
        x, y, c = _place()
        if sequencer:
            _handshake([(x, y, 1 - c)])
        copies = []
        for a, name in enumerate(names):
            copies.append(_remote(_half_of(ins[a], BIG[name][2], 1 - c), outs[a], send_sems, recv_sems, a,
                                  (x, y, 1 - c)))
        for cp in copies:
            cp.start()
        for cp in copies:
            cp.wait()

    def half_shape(name):
        r, c_ = _full_shape(name)
        return (HALF, c_) if BIG[name][2] else (r, HALF)

    out_type = tuple(jax.ShapeDtypeStruct(half_shape(nm), F32) for nm in names)
    sems = (pltpu.SemaphoreType.DMA((n,)), pltpu.SemaphoreType.DMA((n,)))
    operands = [grads[nm] for nm in names]
    if sequencer:
        got = pl.kernel(
            body, mesh=plsc.ScalarSubcoreMesh(axis_name="seq", num_cores=1), name=label, out_type=out_type,
            scratch_types=sems, compiler_params=pltpu.CompilerParams(collective_id=collective_id),
        )(*operands)
    else:
        got = pl.pallas_call(
            body, in_specs=[ANY] * n, out_specs=[ANY] * n, out_shape=list(out_type), scratch_shapes=list(sems),
            name=label,
        )(*operands)
    return dict(zip(names, got))


def _chip_sum(grads, got, names, core, label):
    n = len(names)
    steps = 4
    g_specs, blks = [], []
    for name in names:
        rows, cols = got[name].shape
        tr = rows // steps
        if BIG[name][2]:
            g_specs.append(pl.BlockSpec((tr, cols), lambda i, s: (s[0] * steps + i, 0)))
        else:
            g_specs.append(pl.BlockSpec((tr, HALF), lambda i, s: (i, s[0])))
        blks.append(pl.BlockSpec((tr, cols), lambda i, s: (i, 0)))

    def body(s_ref, *refs):
        for a in range(n):
            t = refs[a][...] + refs[n + a][...]
            refs[2 * n + a][...] = t
            refs[3 * n + a][...] = t.astype(BF16)

    out = pl.pallas_call(
        body,
        grid_spec=pltpu.PrefetchScalarGridSpec(num_scalar_prefetch=1, grid=(steps,), in_specs=g_specs + blks,
                                               out_specs=blks + blks),
        out_shape=([jax.ShapeDtypeStruct(got[nm].shape, F32) for nm in names]
                   + [jax.ShapeDtypeStruct(got[nm].shape, BF16) for nm in names]),
        name=label, compiler_params=_params("parallel"),
    )(core, *[grads[nm] for nm in names], *[got[nm] for nm in names])
    return {nm: (out[a], out[n + a]) for a, nm in enumerate(names)}


def _piece_shape(name):
    rows, cols, by_col = BIG[name]
    return (HALF, cols) if by_col else (rows, HALF)


def _handshake(peers):
    barrier = pltpu.get_barrier_semaphore()
    for peer in peers:
        pl.semaphore_signal(barrier, inc=1, device_id=peer, device_id_type=MESH)
    pl.semaphore_wait(barrier, len(peers))


def _send_chip_sums(sums, names, label, collective_id):
    n = len(names)

    def body(*refs):
        ins, outs = refs[:n], refs[n:2 * n]
        send_sems, recv_sems = refs[2 * n:]
        x, y, c = _place()
        others = [_chip_of(x, y, r) for r in range(3)]
        _handshake([(qx, qy, c) for qx, qy in others])
        copies = []
        for a, name in enumerate(names):
            for r, (qx, qy) in enumerate(others):
                copies.append(_remote(_shard_of(ins[a], name, 2 * qx + qy), outs[a].at[r], send_sems, recv_sems,
                                      3 * a + r, (qx, qy, c)))
        for cp in copies:
            cp.start()
        for cp in copies:
            cp.wait()

    return pl.kernel(
        body, mesh=plsc.ScalarSubcoreMesh(axis_name="seq", num_cores=1), name=label,
        out_type=tuple(jax.ShapeDtypeStruct((3,) + _piece_shape(nm), BF16) for nm in names),
        scratch_types=(pltpu.SemaphoreType.DMA((3 * n,)), pltpu.SemaphoreType.DMA((3 * n,))),
        compiler_params=pltpu.CompilerParams(collective_id=collective_id),
    )(*[sums[nm] for nm in names])


def _total(parts, chip_core):
    steps = 2
    in_specs, out_specs, operands = [], [], []
    for name in BIG_NAMES:
        by_col = BIG[name][2]
        pr, pc = _piece_shape(name)
        tr = pr // steps
        if by_col:
            in_specs.append(pl.BlockSpec((tr, pc), lambda i, s: (i, s[0])))
            out_specs.append(pl.BlockSpec((tr, pc), lambda i, s: (s[1] * steps + i, 0)))
        else:
            in_specs.append(pl.BlockSpec((tr, pc), lambda i, s: (s[0] * steps + i, 0)))
            out_specs.append(pl.BlockSpec((tr, pc), lambda i, s: (i, s[1])))
        for r in range(3):
            in_specs.append(pl.BlockSpec((None, tr, pc), lambda i, s, r=r: (r, i, 0)))
        own, got = parts[name]
        operands += [own, got, got, got]

    def body(s_ref, *refs):
        for a in range(N_BIG):
            o_ref, a_ref, b_ref, c_ref = refs[4 * a:4 * a + 4]
            refs[4 * N_BIG + a][...] = (((o_ref[...] + a_ref[...].astype(F32)) + b_ref[...].astype(F32))
                                        + c_ref[...].astype(F32))

    totals = pl.pallas_call(
        body,
        grid_spec=pltpu.PrefetchScalarGridSpec(num_scalar_prefetch=1, grid=(steps,), in_specs=in_specs,
                                               out_specs=out_specs),
        out_shape=[jax.ShapeDtypeStruct(BIG[name][:2], F32) for name in BIG_NAMES],
        name="totals", compiler_params=_params("parallel"),
    )(chip_core, *operands)
    return dict(zip(BIG_NAMES, totals))


VEC_ROWS = 32
VEC_ROW = {"mix_norm_g": 0, "conv_b": 1, "b_rgate": 2, "b_igate": 3, "lru_lambda": 4, "rg_norm_g": 5,
           "hg_lower_bound": 6, "hg_norm_g": 8, "ffn_norm_g": 9, "final_norm_g": 10, "loss": 11,
           "conv_w": 12, "meta_tokens": 16}
N_DEV = 8


def _all_reduce_small(pieces, gates, totals):
    names = list(pieces)
    n_small = 10
    hv, hg = VEC_ROWS // 2, gates.shape[0] // 2

    def body(*refs):
        ins = refs[:len(names)]
        g_ref = refs[len(names)]
        vec_ref, gsum_ref = refs[len(names) + 1 + N_BIG:len(names) + 3 + N_BIG]
        big = refs[len(names) + 3 + N_BIG:len(names) + 3 + 2 * N_BIG]
        (mine_v, sib_v, sib_g, chip_v, chip_g, got_v, got_g, send_sems, recv_sems) = refs[len(names) + 3 + 2 * N_BIG:]
        x, y, c = _place()
        chip = 2 * x + y
        sibling = (x, y, 1 - c)
        share = []
        for a, name in enumerate(BIG_NAMES):
            half = _half_of(big[a], BIG[name][2], c)
            share.append(_remote(half, half, send_sems, recv_sems, n_small + a, sibling))
        for cp in share:
            cp.start()
        mine_v[...] = jnp.zeros_like(mine_v)
        for name, ref in zip(names, ins):
            nr, w = ref.shape
            mine_v[VEC_ROW[name]:VEC_ROW[name] + nr, 0:w] = ref[...]

        swap = [_remote(mine_v, sib_v, send_sems, recv_sems, 0, sibling),
                _remote(g_ref, sib_g, send_sems, recv_sems, 1, sibling)]
        for cp in swap:
            cp.start()
        for cp in swap:
            cp.wait()
        chip_v[...] = mine_v[...] + sib_v[...]
        chip_g[...] = g_ref[...] + sib_g[...]

        rows_v = pl.ds(pl.multiple_of(c * hv, 8), hv)
        rows_g = pl.ds(pl.multiple_of(c * hg, 8), hg)
        got_v[chip] = chip_v[rows_v, :]
        got_g[chip] = chip_g[rows_g, :]
        sends = []
        for r in range(3):
            qx, qy = _chip_of(x, y, r)
            sends.append(_remote(chip_v.at[rows_v, :], got_v.at[chip], send_sems, recv_sems, 2 + r, (qx, qy, c)))
            sends.append(_remote(chip_g.at[rows_g, :], got_g.at[chip], send_sems, recv_sems, 5 + r, (qx, qy, c)))
        for cp in sends:
            cp.start()
        for cp in sends:
            cp.wait()
        vec_ref[rows_v, :] = ((got_v[0] + got_v[1]) + got_v[2]) + got_v[3]
        gsum_ref[rows_g, :] = ((got_g[0] + got_g[1]) + got_g[2]) + got_g[3]

        back = [_remote(vec_ref.at[rows_v, :], vec_ref.at[rows_v, :], send_sems, recv_sems, 8, sibling),
                _remote(gsum_ref.at[rows_g, :], gsum_ref.at[rows_g, :], send_sems, recv_sems, 9, sibling)]
        for cp in back:
            cp.start()
        theirs_v = vec_ref.at[pl.ds(pl.multiple_of((1 - c) * hv, 8), hv), :]
        theirs_g = gsum_ref.at[pl.ds(pl.multiple_of((1 - c) * hg, 8), hg), :]
        _remote(theirs_v, theirs_v, send_sems, recv_sems, 8, sibling).wait_recv()
        _remote(theirs_g, theirs_g, send_sems, recv_sems, 9, sibling).wait_recv()
        for cp in back:
            cp.wait_send()
        for a, name in enumerate(BIG_NAMES):
            theirs = _half_of(big[a], BIG[name][2], 1 - c)
            _remote(theirs, theirs, send_sems, recv_sems, n_small + a, sibling).wait_recv()
        for cp in share:
            cp.wait_send()

    vmem = pl.BlockSpec(memory_space=pltpu.VMEM)
    n_sems = n_small + N_BIG
    out = pl.pallas_call(
        body, in_specs=[vmem] * (len(names) + 1) + [ANY] * N_BIG, out_specs=[vmem, vmem] + [ANY] * N_BIG,
        out_shape=([jax.ShapeDtypeStruct((VEC_ROWS, D_MODEL), F32), jax.ShapeDtypeStruct(gates.shape, F32)]
                   + [jax.ShapeDtypeStruct(BIG[n][:2], F32) for n in BIG_NAMES]),
        input_output_aliases={len(names) + 1 + a: 2 + a for a in range(N_BIG)},
        scratch_shapes=[pltpu.VMEM((VEC_ROWS, D_MODEL), F32), pltpu.VMEM((VEC_ROWS, D_MODEL), F32),
                        pltpu.VMEM(gates.shape, F32), pltpu.VMEM((VEC_ROWS, D_MODEL), F32),
                        pltpu.VMEM(gates.shape, F32), pltpu.VMEM((N_CHIPS, hv, D_MODEL), F32),
                        pltpu.VMEM((N_CHIPS, hg) + gates.shape[1:], F32),
                        pltpu.SemaphoreType.DMA((n_sems,)), pltpu.SemaphoreType.DMA((n_sems,))],
        name="all_reduce_small",
    )(*[pieces[n] for n in names], gates, *[totals[n] for n in BIG_NAMES])
    return out[0], out[1], dict(zip(BIG_NAMES, out[2:]))


def _adamw_math(w, g, m, v):
    m = ADAM_B1 * m + (1.0 - ADAM_B1) * g
    v = ADAM_B2 * v + (1.0 - ADAM_B2) * (g * g)
    m_hat = m / (1.0 - ADAM_B1 ** ADAM_STEP)
    v_hat = v / (1.0 - ADAM_B2 ** ADAM_STEP)
    delta = -ADAM_LR * (m_hat / (jnp.sqrt(v_hat) + ADAM_EPS) + ADAM_WD * w)
    return delta, m, v


def _adamw_big(w, g, m, v):
    steps = 8
    blks = []
    for name in BIG_NAMES:
        rows, cols, _ = BIG[name]
        blks.append(pl.BlockSpec((rows // steps, cols), lambda i: (i, 0)))

    def body(*refs):
        ins, outs = refs[:4 * N_BIG], refs[4 * N_BIG:]
        for a in range(N_BIG):
            w_ref, g_ref, m_ref, v_ref = (ins[k * N_BIG + a] for k in range(4))
            g = g_ref[...]
            d, nm, nv = _adamw_math(w_ref[...], g, m_ref[...], v_ref[...])
            outs[a][...] = g
            outs[N_BIG + a][...] = d
            outs[2 * N_BIG + a][...] = nm
            outs[3 * N_BIG + a][...] = nv

    shapes = [jax.ShapeDtypeStruct(BIG[name][:2], F32) for name in BIG_NAMES]
    out = pl.pallas_call(
        body, grid=(steps,), in_specs=blks * 4, out_specs=blks * 4, out_shape=shapes * 4,
        name="adamw_big", compiler_params=_params("parallel"),
    )(*[t[name] for t in (w, g, m, v) for name in BIG_NAMES])
    return {name: tuple(out[k * N_BIG + a] for k in range(4)) for a, name in enumerate(BIG_NAMES)}


SMALL = {"meta_tokens": (N_META, D_MODEL // N_CHIPS), "mix_norm_g": (1, D_MODEL), "conv_w": (CONV_W, D_RG // N_CHIPS),
         "conv_b": (1, D_RG), "w_rgate": (D_RG, RG_HEAD_DIM), "b_rgate": (1, D_RG), "w_igate": (D_RG, RG_HEAD_DIM),
         "b_igate": (1, D_RG), "lru_lambda": (1, D_RG), "rg_norm_g": (1, D_RG), "hg_lower_bound": (2, D_HG),
         "hg_norm_g": (1, HG_HEAD_DIM), "ffn_norm_g": (1, D_MODEL), "final_norm_g": (1, D_MODEL)}
SMALL_NAMES = tuple(SMALL)
SHARDED_SMALL = ("meta_tokens", "conv_w")


def _adamw_small(vec, gates, w, m, v):
    n = len(SMALL_NAMES)

    def body(*refs):
        vec_ref, gates_ref = refs[:2]
        w_refs, m_refs, v_refs = refs[2:2 + n], refs[2 + n:2 + 2 * n], refs[2 + 2 * n:2 + 3 * n]
        outs = refs[2 + 3 * n:]
        loss_ref = outs[0]
        x, y, _ = _place()
        chip = 2 * x + y
        loss_ref[...] = vec_ref[VEC_ROW["loss"]:VEC_ROW["loss"] + 1, 0:1]

        def update(k, g):
            g_ref, d_ref, nm_ref, nv_ref = outs[1 + 4 * k:5 + 4 * k]
            g_ref[...] = g
            d_ref[...], nm_ref[...], nv_ref[...] = _adamw_math(w_refs[k][...], g, m_refs[k][...], v_refs[k][...])

        for k, name in enumerate(SMALL_NAMES):
            nr, w_ = SMALL[name]
            if name == "w_rgate":
                update(k, gates_ref[0:D_RG, :])
            elif name == "w_igate":
                update(k, gates_ref[D_RG:2 * D_RG, :])
            elif name in SHARDED_SMALL:
                r0 = VEC_ROW[name]
                for q in range(N_CHIPS):
                    @pl.when(chip == q)
                    def _(k=k, r0=r0, nr=nr, w_=w_, q=q):
                        update(k, vec_ref[r0:r0 + nr, q * w_:(q + 1) * w_])
            else:
                r0 = VEC_ROW[name]
                update(k, vec_ref[r0:r0 + nr, 0:w_])

    vmem = pl.BlockSpec(memory_space=pltpu.VMEM)
    out_shape = [jax.ShapeDtypeStruct((1, 1), F32)]
    for name in SMALL_NAMES:
        out_shape += [jax.ShapeDtypeStruct(SMALL[name], F32)] * 4
    outs = pl.pallas_call(
        body, in_specs=[vmem] * (2 + 3 * n), out_specs=[vmem] * len(out_shape), out_shape=out_shape,
        name="adamw_small",
    )(vec, gates, *[w[k] for k in SMALL_NAMES], *[m[k] for k in SMALL_NAMES], *[v[k] for k in SMALL_NAMES])
    loss = outs[0]
    res = {name: tuple(outs[1 + 4 * k:5 + 4 * k]) for k, name in enumerate(SMALL_NAMES)}
    return loss, res


WEIGHT_NAMES = ("meta_tokens", "mix_norm_g", "w_in", "conv_w", "conv_b", "w_rgate", "b_rgate", "w_igate", "b_igate",
                "lru_lambda", "rg_norm_g", "hg_lower_bound", "hg_norm_g", "w_out", "ffn_norm_g", "w_gate_up", "w_down",
                "final_norm_g")


def _to_2d(name, a):
    if name in BIG:
        return a.reshape(BIG[name][:2])
    return a.reshape(SMALL[name])


def kernel(x, meta_tokens, mix_norm_g, w_in, conv_w, conv_b, w_rgate, b_rgate, w_igate, b_igate, lru_lambda, rg_norm_g, hg_lower_bound, hg_norm_g, w_out, ffn_norm_g, w_gate_up, w_down, final_norm_g, loss_target, m_meta_tokens, m_mix_norm_g, m_w_in, m_conv_w, m_conv_b, m_w_rgate, m_b_rgate, m_w_igate, m_b_igate, m_lru_lambda, m_rg_norm_g, m_hg_lower_bound, m_hg_norm_g, m_w_out, m_ffn_norm_g, m_w_gate_up, m_w_down, m_final_norm_g, v_meta_tokens, v_mix_norm_g, v_w_in, v_conv_w, v_conv_b, v_w_rgate, v_b_rgate, v_w_igate, v_b_igate, v_lru_lambda, v_rg_norm_g, v_hg_lower_bound, v_hg_norm_g, v_w_out, v_ffn_norm_g, v_w_gate_up, v_w_down, v_final_norm_g):
    w_raw = dict(zip(WEIGHT_NAMES, (meta_tokens, mix_norm_g, w_in, conv_w, conv_b, w_rgate, b_rgate, w_igate, b_igate,
                                    lru_lambda, rg_norm_g, hg_lower_bound, hg_norm_g, w_out, ffn_norm_g, w_gate_up,
                                    w_down, final_norm_g)))
    m_raw = dict(zip(WEIGHT_NAMES, (m_meta_tokens, m_mix_norm_g, m_w_in, m_conv_w, m_conv_b, m_w_rgate, m_b_rgate,
                                    m_w_igate, m_b_igate, m_lru_lambda, m_rg_norm_g, m_hg_lower_bound, m_hg_norm_g,
                                    m_w_out, m_ffn_norm_g, m_w_gate_up, m_w_down, m_final_norm_g)))
    v_raw = dict(zip(WEIGHT_NAMES, (v_meta_tokens, v_mix_norm_g, v_w_in, v_conv_w, v_conv_b, v_w_rgate, v_b_rgate,
                                    v_w_igate, v_b_igate, v_lru_lambda, v_rg_norm_g, v_hg_lower_bound, v_hg_norm_g,
                                    v_w_out, v_ffn_norm_g, v_w_gate_up, v_w_down, v_final_norm_g)))
    w = {k: _to_2d(k, a) for k, a in w_raw.items()}
    m = {k: _to_2d(k, a) for k, a in m_raw.items()}
    v = {k: _to_2d(k, a) for k, a in v_raw.items()}

    x_i, y_i, c_i = _place()
    core = jnp.reshape(c_i, (1,)).astype(jnp.int32)
    chip = jnp.reshape(2 * x_i + y_i, (1,)).astype(jnp.int32)
    chip_core = jnp.concatenate([chip, core])

    placed, (meta_full, cw_full) = _place_shards(w, [w["meta_tokens"], w["conv_w"]], chip)
    first, _ = _gather_weights(placed, [], ("w_in",), "gather_first", 1)
    rest, _ = _gather_weights(placed, [], ("w_out", "w_gate_up", "w_down"), "gather_rest", 2)
    full = {**first, **rest}

    seq = x.shape[1]
    small ={k: w[k] for k in SMALL_NAMES if k not in SHARDED_SMALL}
    small["conv_w"] = cw_full

    def reduce_to_chips(grads, names, tag, collective_ids):
        got = _exchange_halves(grads, names, "exchange_halves_" + tag, collective_ids[0])

        def chip_sums():
            return _chip_sum(grads, got, names, core, "chip_sum_" + tag)

        def send(sums):
            arrived = _send_chip_sums({n: sums[n][1] for n in names}, names, "send_chip_sums_" + tag,
                                      collective_ids[1])
            return {n: (sums[n][0], a) for n, a in zip(names, arrived)}

        return chip_sums, send

    ffn_names, mixer_names = ("w_gate_up", "w_down", "w_out"), ("w_in",)
    loss, grad_x, grads, parts, parts_mixer = _local_step(
        x.reshape(seq, D_MODEL), meta_full, loss_target.reshape(seq, D_MODEL),
        w["w_in"], full["w_in"], full["w_out"], full["w_gate_up"], full["w_down"], small, chip,
        on_ffn_grads=lambda g: reduce_to_chips(g, ffn_names, "ffn", (3, 4)),
        on_mixer_grads=lambda g: reduce_to_chips(g, mixer_names, "mixer", (None, 5)))
    parts.update(parts_mixer)
    totals = _total(parts, chip_core)
    pieces = {k: grads[k] for k in VEC_ROW if k != "loss"}
    pieces["loss"] = loss
    vec, gates, g_big = _all_reduce_small(pieces, grads["w_gates"], totals)
    loss_sum, res = _adamw_small(vec, gates, w, m, v)
    res.update(_adamw_big(w, g_big, m, v))

    out = [loss_sum.reshape(()), grad_x.reshape(1, seq, D_MODEL)]
    for j in range(4):
        out += [res[n][j].reshape(w_raw[n].shape) for n in WEIGHT_NAMES]
    return tuple(out)
```

```python
import math

import jax
import jax.numpy as jnp
from jax import lax
from jax.experimental import pallas as pl
from jax.experimental.pallas import tpu as pltpu
from jax.experimental.pallas import tpu_sc as plsc

F32 = jnp.float32
BF16 = jnp.bfloat16
HIGHEST = lax.Precision.HIGHEST
MESH = pl.DeviceIdType.MESH

D_MODEL = 1024
D_RG = 512
RG_HEAD_DIM = 64
D_HG = 512
HG_HEAD_DIM = 128
HG_HEADS = 4
CHUNK = 64
SUB = 16
N_SUB = CHUNK // SUB
N_META = 16
PAD = CHUNK - N_META
D_IN = 3072
D_FF = 2816
CONV_W = 4
LRU_C = 8.0
EPS = 1e-6
EXP_CLAMP = 80.0
GELU_C = math.sqrt(2.0 / math.pi)
GELU_A = 0.044715
N_CHIPS = 4

ADAM_LR = 0.001
ADAM_B1 = 0.9
ADAM_B2 = 0.999
ADAM_EPS = 1e-08
ADAM_WD = 0.01
ADAM_STEP = 10

VMEM_LIMIT = 56 * 1024 * 1024


def _params(*sem):
    return pltpu.CompilerParams(dimension_semantics=sem, vmem_limit_bytes=VMEM_LIMIT)


def _row_tile(rows, target):
    best = None
    for t in range(16, min(rows, target) + 1, 16):
        if rows % t == 0:
            best = t
    assert best is not None, rows
    return best


def _sigmoid(x):
    return 0.5 * jnp.tanh(0.5 * x) + 0.5


def _dot(a, b):
    return jnp.dot(a, b, preferred_element_type=F32)


def _dot_nt(a, b):
    return lax.dot_general(a, b, (((1,), (1,)), ((), ())), preferred_element_type=F32)


def _dot_tn(a, b):
    return lax.dot_general(a, b, (((0,), (0,)), ((), ())), preferred_element_type=F32)


def _rms(x):
    return lax.rsqrt(jnp.mean(x * x, axis=-1, keepdims=True) + EPS)


def _rms_bwd(dn, n, r):
    return r * (dn - n * jnp.mean(dn * n, axis=-1, keepdims=True))


def _gelu_parts(x):
    t = jnp.tanh(GELU_C * (x + GELU_A * x * x * x))
    g = 0.5 * x * (1.0 + t)
    dg = 0.5 * (1.0 + t) + 0.5 * x * (1.0 - t * t) * GELU_C * (1.0 + 3.0 * GELU_A * x * x)
    return g, dg


def _softplus_neg(lam):
    e = jnp.exp(-jnp.abs(lam))
    w = 1.0 + e
    log1p = jnp.where(w == 1.0, e, jnp.log(w) * e / (w - 1.0))
    return jnp.maximum(-lam, 0.0) + log1p


def _head_mask():
    r = lax.broadcasted_iota(jnp.int32, (D_RG, D_RG), 0) // RG_HEAD_DIM
    c = lax.broadcasted_iota(jnp.int32, (D_RG, D_RG), 1) // RG_HEAD_DIM
    return r == c


def _head_fold():
    r = lax.broadcasted_iota(jnp.int32, (D_RG, RG_HEAD_DIM), 0) % RG_HEAD_DIM
    c = lax.broadcasted_iota(jnp.int32, (D_RG, RG_HEAD_DIM), 1)
    return (r == c).astype(F32)


def _gate_weights(w_r, w_i):
    def body(wr_ref, wi_ref, o_ref):
        fold = _head_fold()
        mask = _head_mask()
        for k, ref in enumerate((wr_ref, wi_ref)):
            full = lax.dot_general(ref[...], fold, (((1,), (1,)), ((), ())),
                                   precision=HIGHEST, preferred_element_type=F32)
            o_ref[:, k * D_RG:(k + 1) * D_RG] = jnp.where(mask, full, 0.0).astype(BF16)

    return pl.pallas_call(
        body, out_shape=jax.ShapeDtypeStruct((D_RG, 2 * D_RG), BF16), name="gate_weights",
    )(w_r, w_i)


HEAD = PAD + N_META


def _window_copies(seq_hbm, buf, sems, tm):
    def first(to_vmem):
        seq, vm = seq_hbm.at[pl.ds(0, tm - HEAD)], buf.at[0, pl.ds(HEAD, tm - HEAD)]
        return pltpu.make_async_copy(seq, vm, sems.at[0]) if to_vmem else pltpu.make_async_copy(vm, seq, sems.at[0])

    def later(j, slot, to_vmem):
        seq, vm = seq_hbm.at[pl.ds(pl.multiple_of(j * tm - HEAD, 8), tm)], buf.at[slot]
        if to_vmem:
            return pltpu.make_async_copy(seq, vm, sems.at[slot])
        return pltpu.make_async_copy(vm, seq, sems.at[slot])

    return first, later


def _fetch_window(seq_hbm, buf, sems, i, n_steps, tm):
    first, later = _window_copies(seq_hbm, buf, sems, tm)
    slot = i % 2

    @pl.when(i == 0)
    def _():
        first(True).start()

    if n_steps > 1:
        @pl.when(i + 1 < n_steps)
        def _():
            later(i + 1, 1 - slot, True).start()

    @pl.when(i == 0)
    def _():
        first(True).wait()

    if n_steps > 1:
        @pl.when(i > 0)
        def _():
            later(i, slot, True).wait()

    return slot


def _in_proj_local(x, meta, g1, w_own, chip):
    T = x.shape[0] + HEAD
    tm = _row_tile(T, 832)
    n_steps = T // tm
    cols = BIG["w_in"][1]

    def body(s_ref, x_hbm, meta_ref, g_ref, w_ref, p_ref, u_ref, h_ref, buf, sems, wb):
        i = pl.program_id(0)
        slot = _fetch_window(x_hbm, buf, sems, i, n_steps, tm)

        @pl.when(i == 0)
        def _():
            buf[0, 0:PAD, :] = jnp.zeros((PAD, D_MODEL), F32)
            buf[0, PAD:HEAD, :] = meta_ref[...]
            wb[...] = w_ref[...].astype(BF16)

        h = buf[slot]
        h_ref[...] = h
        u = (h * _rms(h) * g_ref[...]).astype(BF16)
        u_ref[...] = u
        p_ref[...] = _dot(u, wb[...])

    return pl.pallas_call(
        body,
        grid_spec=pltpu.PrefetchScalarGridSpec(
            num_scalar_prefetch=1, grid=(n_steps,),
            in_specs=[pl.BlockSpec(memory_space=pl.ANY),
                      pl.BlockSpec((N_META, D_MODEL), lambda i, s: (0, 0)),
                      pl.BlockSpec((1, D_MODEL), lambda i, s: (0, 0)),
                      pl.BlockSpec((D_MODEL, cols), lambda i, s: (0, 0))],
            out_specs=[pl.BlockSpec((tm, cols), lambda i, s: (i, s[0])),
                       pl.BlockSpec((tm, D_MODEL), lambda i, s: (i, 0)),
                       pl.BlockSpec((tm, D_MODEL), lambda i, s: (i, 0))],
            scratch_shapes=[pltpu.VMEM((2, tm, D_MODEL), F32), pltpu.SemaphoreType.DMA((2,)),
                            pltpu.VMEM((D_MODEL, cols), BF16)]),
        out_shape=[jax.ShapeDtypeStruct((T, D_IN), F32), jax.ShapeDtypeStruct((T, D_MODEL), BF16),
                   jax.ShapeDtypeStruct((T, D_MODEL), F32)],
        name="in_proj_local", compiler_params=_params("arbitrary"),
    )(chip, x, meta, g1, w_own)


def _in_proj_rest(u, w_in, p, chip):
    T = u.shape[0]
    tm = _row_tile(T, 2080)
    cols = BIG["w_in"][1]
    block = lambda j, s: (s[0] + 1 + j) % N_CHIPS

    def body(s_ref, u_ref, w_ref, p_in_ref, p_ref):
        p_ref[...] = _dot(u_ref[...], w_ref[...])

    return pl.pallas_call(
        body,
        grid_spec=pltpu.PrefetchScalarGridSpec(
            num_scalar_prefetch=1, grid=(N_CHIPS - 1, T // tm),
            in_specs=[pl.BlockSpec((tm, D_MODEL), lambda j, i, s: (i, 0)),
                      pl.BlockSpec((D_MODEL, cols), lambda j, i, s: (0, block(j, s))), ANY],
            out_specs=pl.BlockSpec((tm, cols), lambda j, i, s: (i, block(j, s)))),
        out_shape=jax.ShapeDtypeStruct((T, D_IN), F32),
        input_output_aliases={3: 0},
        name="in_proj_rest", compiler_params=_params("arbitrary", "arbitrary"),
    )(chip, u, w_in, p)


def _scan_block_fwd(A, B, rowi):
    for d in (1, 2, 4):
        a_sh = pltpu.roll(A, d, axis=0)
        b_sh = pltpu.roll(B, d, axis=0)
        m = rowi >= d
        B = jnp.where(m, A * b_sh + B, B)
        A = jnp.where(m, A * a_sh, A)
    return A, B


def _scan_block_bwd(A, B, rowi):
    for d in (1, 2, 4):
        a_sh = pltpu.roll(A, 8 - d, axis=0)
        b_sh = pltpu.roll(B, 8 - d, axis=0)
        m = rowi < 8 - d
        B = jnp.where(m, A * b_sh + B, B)
        A = jnp.where(m, A * a_sh, A)
    return A, B


def _rg_gates(xc, w_ref, bg_ref, lam):
    pre = _dot(xc.astype(BF16), w_ref[...]) + bg_ref[...]
    r = _sigmoid(pre[:, :D_RG])
    ig = _sigmoid(pre[:, D_RG:])
    sp = _softplus_neg(lam)
    la = -LRU_C * sp * r
    a = jnp.exp(la)
    th = jnp.tanh(la)
    u = 1.0 - th
    rc = pl.reciprocal(u, approx=True)
    rc = rc * (2.0 - u * rc)
    rc = rc * (2.0 - u * rc)
    m2 = -2.0 * th * rc
    inv_m = lax.rsqrt(jnp.maximum(m2, 1e-30))
    return r, ig, sp, a, m2 * inv_m, inv_m


def _conv(ext, cw_ref, cb_ref, tm):
    xc = cb_ref[...] + cw_ref[0:1, :] * ext[8 - 3:8 - 3 + tm, :]
    for j in range(1, CONV_W):
        xc = xc + cw_ref[j:j + 1, :] * ext[8 - 3 + j:8 - 3 + j + tm, :]
    return xc


def _scan_unroll(blocks):
    return 4 if blocks % 4 == 0 else 2 if blocks % 2 == 0 else 1


def _rg_fwd(p, cw, cb, wg, bg, lam, rg_g):
    T = p.shape[0]
    tm = _row_tile(T, 832)
    unroll = _scan_unroll(tm // 8)

    def body(xg_ref, cw_ref, cb_ref, w_ref, bg_ref, lam_ref, g_ref, y_ref, h_ref, xc_ref, ext, a_s, b_s, carry):
        i = pl.program_id(0)

        @pl.when(i == 0)
        def _():
            ext[0:8, :] = jnp.zeros((8, D_RG), F32)
            carry[...] = jnp.zeros((1, D_RG), F32)

        ext[8:8 + tm, :] = xg_ref[:, :D_RG]
        xc = _conv(ext, cw_ref, cb_ref, tm)
        xc_ref[...] = xc
        r, ig, sp, a, m, _ = _rg_gates(xc, w_ref, bg_ref, lam_ref[...])
        row = i * tm + lax.broadcasted_iota(jnp.int32, (tm, 1), 0)
        a_s[...] = a
        b_s[...] = jnp.where(row >= PAD, m * ig * xc, 0.0)
        rowi = lax.broadcasted_iota(jnp.int32, (8, D_RG), 0)

        def blk(j, c):
            for u in range(unroll):
                o = pl.multiple_of((j * unroll + u) * 8, 8)
                A, B = _scan_block_fwd(a_s[pl.ds(o, 8), :], b_s[pl.ds(o, 8), :], rowi)
                h = B + A * c
                h_ref[pl.ds(o, 8), :] = h
                c = h[7:8, :]
            return c

        carry[...] = lax.fori_loop(0, tm // (8 * unroll), blk, carry[...])
        ext[0:8, :] = ext[tm:tm + 8, :]
        g, _ = _gelu_parts(xg_ref[:, D_RG:])
        yy = g * h_ref[...]
        y_ref[...] = (yy * _rms(yy) * g_ref[...]).astype(BF16)

    vec = lambda n: pl.BlockSpec((1, n), lambda i: (0, 0))
    return pl.pallas_call(
        body, grid=(T // tm,),
        in_specs=[pl.BlockSpec((tm, 2 * D_RG), lambda i: (i, 0)),
                  pl.BlockSpec((CONV_W, D_RG), lambda i: (0, 0)), vec(D_RG),
                  pl.BlockSpec((D_RG, 2 * D_RG), lambda i: (0, 0)), vec(2 * D_RG), vec(D_RG), vec(D_RG)],
        out_specs=[pl.BlockSpec((tm, D_RG), lambda i: (i, 0))] * 3,
        out_shape=[jax.ShapeDtypeStruct((T, D_RG), BF16), jax.ShapeDtypeStruct((T, D_RG), F32),
                   jax.ShapeDtypeStruct((T, D_RG), F32)],
        scratch_shapes=[pltpu.VMEM((tm + 8, D_RG), F32), pltpu.VMEM((tm, D_RG), F32),
                        pltpu.VMEM((tm, D_RG), F32), pltpu.VMEM((1, D_RG), F32)],
        name="rg_fwd", compiler_params=_params("arbitrary"),
    )(p, cw, cb, wg, bg, lam, rg_g)


def _running_sum(x, down):
    r = lax.broadcasted_iota(jnp.int32, (CHUNK, CHUNK), 0)
    c = lax.broadcasted_iota(jnp.int32, (CHUNK, CHUNK), 1)
    tri = ((c <= r) if down else (c >= r)).astype(BF16)
    hi = x.astype(BF16)
    rest = x - hi.astype(F32)
    mid = rest.astype(BF16)
    lo = (rest - mid.astype(F32)).astype(BF16)
    return (_dot(tri, hi) + _dot(tri, mid)) + _dot(tri, lo)


def _hg_gates(hq, hf, lbraw_ref, valid):
    lb = _sigmoid(lbraw_ref[0:1, :] - lbraw_ref[1:2, :])
    sq = _sigmoid(hq)
    q = hq * sq
    sf = _sigmoid(hf)
    f = lb + (1.0 - lb) * sf
    lf = jnp.where(valid, jnp.log(f), 0.0)
    b = _running_sum(lf, True)
    return lb, sq, q, sf, f, b


def _hg_head(qh, kh, bh):
    blk = lax.broadcasted_iota(jnp.int32, (CHUNK, 1), 0) // SUB
    b_last = bh[CHUNK - 1:CHUNK, :]
    refs = [bh[SUB * s:SUB * s + 1, :] for s in range(N_SUB)]
    r_sel = refs[N_SUB - 1]
    for s in range(N_SUB - 2, -1, -1):
        r_sel = jnp.where(blk == s, refs[s], r_sel)
    eb = jnp.exp(bh)
    eq = jnp.exp(bh - r_sel)
    ekh = jnp.exp(b_last - bh)
    ek = [jnp.exp(jnp.minimum(refs[s] - bh, EXP_CLAMP)) for s in range(N_SUB)]
    qe = qh * eq
    q_hat = jnp.concatenate([jnp.where(blk == s, qe, 0.0) for s in range(N_SUB)], axis=1)
    k_til = jnp.concatenate([kh * ek[s] for s in range(N_SUB)], axis=1)
    return blk, b_last, eb, eq, ekh, ek, q_hat, k_til


def _causal():
    r = lax.broadcasted_iota(jnp.int32, (CHUNK, CHUNK), 0)
    c = lax.broadcasted_iota(jnp.int32, (CHUNK, CHUNK), 1)
    return r >= c


def _chunks_per_step(n_chunks):
    for c in (5, 4, 3, 2):
        if n_chunks % c == 0:
            return c
    return 1


def _hg_fwd(p, lbraw, hg_g):
    T = p.shape[0]
    n_chunks = T // CHUNK
    cps = _chunks_per_step(n_chunks)
    rows = cps * CHUNK

    def body(hq_ref, hf_ref, hi_ref, hg_ref, lb_ref, g_ref, y_ref, o_ref, st_all_ref, st):
        i = pl.program_id(0)

        @pl.when(i == 0)
        def _():
            st[...] = jnp.zeros_like(st)

        def chunk(j, carry):
            rs = pl.ds(pl.multiple_of(j * CHUNK, CHUNK), CHUNK)
            chunk_body(i * cps + j, hq_ref.at[rs, :], hf_ref.at[rs, :], hi_ref.at[rs, :], hg_ref.at[rs, :], lb_ref,
                       g_ref, y_ref.at[rs, :], o_ref.at[rs, :], st_all_ref.at[pl.ds(j, 1)], st)
            return carry

        lax.fori_loop(0, cps, chunk, 0, unroll=True)

    def chunk_body(n, hq_ref, hf_ref, hi_ref, hg_ref, lb_ref, g_ref, y_ref, o_ref, st_all_ref, st):
        valid = (n * CHUNK + lax.broadcasted_iota(jnp.int32, (CHUNK, 1), 0)) >= PAD
        hq, hf, v, hg = hq_ref[...], hf_ref[...], hi_ref[...], hg_ref[...]
        lb, sq, q, sf, f, b = _hg_gates(hq, hf, lb_ref, valid)
        k = 1.0 - f
        st_all_ref[0] = st[...]
        causal = _causal()
        v_t = v.T.astype(BF16)
        heads = [slice(h * HG_HEAD_DIM, (h + 1) * HG_HEAD_DIM) for h in range(HG_HEADS)]
        fac = []
        for sl in heads:
            qh, kh, bh = q[:, sl], k[:, sl], b[:, sl]
            _, b_last, eb, _, ekh, _, q_hat, k_til = _hg_head(qh, kh, bh)
            fac.append((jnp.exp(b_last), (qh * eb).astype(BF16), q_hat.astype(BF16), k_til.astype(BF16),
                        (kh * ekh).astype(BF16), v[:, sl].astype(BF16)))
        raw = []
        for sl, (_, q_til, q_hat, k_til, k_hat, _) in zip(heads, fac):
            st_h = st[sl, :]
            raw.append((_dot_nt(q_til, st_h.astype(BF16)), _dot_nt(q_hat, k_til), _dot(v_t[sl, :], k_hat), st_h))
        for sl, (e_last, _, _, _, _, vb), (inter, att, upd, st_h) in zip(heads, fac, raw):
            o = inter + _dot(jnp.where(causal, att, 0.0).astype(BF16), vb)
            st[sl, :] = st_h * e_last + upd
            o_ref[:, sl] = o
            hgh = hg[:, sl]
            y_ref[:, sl] = (o * _rms(o) * g_ref[...] * (hgh * _sigmoid(hgh))).astype(BF16)

    col = lambda j: pl.BlockSpec((rows, D_HG), lambda n: (n, j))
    return pl.pallas_call(
        body, grid=(n_chunks // cps,),
        in_specs=[col(2), col(3), col(4), col(5),
                  pl.BlockSpec((2, D_HG), lambda n: (0, 0)), pl.BlockSpec((1, HG_HEAD_DIM), lambda n: (0, 0))],
        out_specs=[pl.BlockSpec((rows, D_HG), lambda n: (n, 0)), pl.BlockSpec((rows, D_HG), lambda n: (n, 0)),
                   pl.BlockSpec((cps, D_HG, HG_HEAD_DIM), lambda n: (n, 0, 0))],
        out_shape=[jax.ShapeDtypeStruct((T, D_HG), BF16), jax.ShapeDtypeStruct((T, D_HG), F32),
                   jax.ShapeDtypeStruct((n_chunks, D_HG, HG_HEAD_DIM), F32)],
        scratch_shapes=[pltpu.VMEM((D_HG, HG_HEAD_DIM), F32)],
        name="hg_fwd", compiler_params=_params("arbitrary"),
    )(p, p, p, p, lbraw, hg_g)


def _ffn_fwd(h0, y_rg, y_hg, w_out, g2, w_gu, w_down, gf, target):
    T = h0.shape[0]
    tm = _row_tile(T, 320)
    n_steps = T // tm

    def body(h_ref, yr_ref, yh_ref, wo_ref, g2_ref, wgu_ref, wd_ref, gf_ref, t_hbm,
             h1_ref, v_ref, y_ref, gu_ref, act_ref, dh2_ref, dh2b_ref, loss_ref, gg_ref, tbuf, sems):
        i = pl.program_id(0)
        slot = _fetch_window(t_hbm, tbuf, sems, i, n_steps, tm)

        @pl.when(i == 0)
        def _():
            loss_ref[...] = jnp.zeros_like(loss_ref)
            gg_ref[...] = jnp.zeros_like(gg_ref)
            tbuf[0, 0:HEAD, :] = jnp.zeros((HEAD, D_MODEL), F32)

        y_ref[:, :D_RG] = yr_ref[...]
        y_ref[:, D_RG:] = yh_ref[...]
        h1 = h_ref[...] + _dot(y_ref[...], wo_ref[...])
        h1_ref[...] = h1
        v = (h1 * _rms(h1) * g2_ref[...]).astype(BF16)
        v_ref[...] = v

        gu = _dot(v, wgu_ref[...])
        gu_ref[...] = gu.astype(BF16)
        g = gu[:, :D_FF]
        act = (g * _sigmoid(g) * gu[:, D_FF:]).astype(BF16)
        act_ref[...] = act

        h2 = h1 + _dot(act, wd_ref[...])
        r = _rms(h2)
        n = h2 * r
        gf_ = gf_ref[...]
        row = i * tm + lax.broadcasted_iota(jnp.int32, (tm, 1), 0)
        err = jnp.where(row >= HEAD, n * gf_ - tbuf[slot], 0.0)
        loss_ref[...] += 0.5 * jnp.sum(jnp.mean(err * err, axis=-1, keepdims=True), axis=0, keepdims=True)
        dy = err * (1.0 / D_MODEL)
        gg_ref[...] += jnp.sum(dy * n, axis=0, keepdims=True)
        dh2 = _rms_bwd(dy * gf_, n, r)
        dh2_ref[...] = dh2
        dh2b_ref[...] = dh2.astype(BF16)

    row_spec = lambda n: pl.BlockSpec((tm, n), lambda i: (i, 0))
    vec = pl.BlockSpec((1, D_MODEL), lambda i: (0, 0))
    return pl.pallas_call(
        body, grid=(n_steps,),
        in_specs=[row_spec(D_MODEL), row_spec(D_RG), row_spec(D_HG), _resident((D_MODEL, D_MODEL)), vec,
                  _resident((D_MODEL, 2 * D_FF)), _resident((D_FF, D_MODEL)), vec,
                  pl.BlockSpec(memory_space=pl.ANY)],
        out_specs=[row_spec(D_MODEL), row_spec(D_MODEL), row_spec(D_MODEL), row_spec(2 * D_FF), row_spec(D_FF),
                   row_spec(D_MODEL), row_spec(D_MODEL), pl.BlockSpec((1, 1), lambda i: (0, 0)), vec],
        out_shape=[jax.ShapeDtypeStruct((T, D_MODEL), F32), jax.ShapeDtypeStruct((T, D_MODEL), BF16),
                   jax.ShapeDtypeStruct((T, D_MODEL), BF16), jax.ShapeDtypeStruct((T, 2 * D_FF), BF16),
                   jax.ShapeDtypeStruct((T, D_FF), BF16), jax.ShapeDtypeStruct((T, D_MODEL), F32),
                   jax.ShapeDtypeStruct((T, D_MODEL), BF16), jax.ShapeDtypeStruct((1, 1), F32),
                   jax.ShapeDtypeStruct((1, D_MODEL), F32)],
        scratch_shapes=[pltpu.VMEM((2, tm, D_MODEL), F32), pltpu.SemaphoreType.DMA((2,))],
        name="ffn_fwd", compiler_params=_params("arbitrary"),
    )(h0, y_rg, y_hg, w_out, g2, w_gu, w_down, gf, target)


def _resident(shape):
    return pl.BlockSpec(shape, lambda i: (0,) * len(shape), pipeline_mode=pl.Buffered(1))


def _ffn_bwd(dh2b, gu, w_down, w_gu, h1, g2, dh2, w_out):
    T = h1.shape[0]
    tm = _row_tile(T, 320)

    def body(d_ref, gu_ref, wd_ref, wgu_ref, h_ref, g_ref, d2_ref, wo_ref, dgu_ref, dh1_ref, dh1b_ref, dy_ref, gg_ref):
        i = pl.program_id(0)

        @pl.when(i == 0)
        def _():
            gg_ref[...] = jnp.zeros_like(gg_ref)

        dact = _dot_nt(d_ref[...], wd_ref[...]).astype(BF16)
        g = gu_ref[:, :D_FF]
        u = gu_ref[:, D_FF:]
        s = _sigmoid(g)
        dgu_ref[:, :D_FF] = dact * u * (s * (1.0 + g * (1.0 - s)))
        dgu_ref[:, D_FF:] = dact * (g * s)

        dv = _dot_nt(dgu_ref[...], wgu_ref[...])
        h1_ = h_ref[...]
        r = _rms(h1_)
        n = h1_ * r
        gg_ref[...] += jnp.sum(dv * n, axis=0, keepdims=True)
        dh1 = d2_ref[...] + _rms_bwd(dv * g_ref[...], n, r)
        dh1_ref[...] = dh1
        db = dh1.astype(BF16)
        dh1b_ref[...] = db
        dy_ref[...] = _dot_nt(db, wo_ref[...])

    row = lambda n: pl.BlockSpec((tm, n), lambda i: (i, 0))
    return pl.pallas_call(
        body, grid=(T // tm,),
        in_specs=[row(D_MODEL), row(2 * D_FF), _resident((D_FF, D_MODEL)), _resident((D_MODEL, 2 * D_FF)),
                  row(D_MODEL), pl.BlockSpec((1, D_MODEL), lambda i: (0, 0)), row(D_MODEL),
                  _resident((D_MODEL, D_MODEL))],
        out_specs=[row(2 * D_FF), row(D_MODEL), row(D_MODEL), row(D_MODEL),
                   pl.BlockSpec((1, D_MODEL), lambda i: (0, 0))],
        out_shape=[jax.ShapeDtypeStruct((T, 2 * D_FF), BF16), jax.ShapeDtypeStruct((T, D_MODEL), F32),
                   jax.ShapeDtypeStruct((T, D_MODEL), BF16), jax.ShapeDtypeStruct((T, D_MODEL), F32),
                   jax.ShapeDtypeStruct((1, D_MODEL), F32)],
        name="ffn_bwd", compiler_params=_params("arbitrary"),
    )(dh2b, gu, w_down, w_gu, h1, g2, dh2, w_out)


def _rg_bwd(p, xc_all, hs, dy, dp, cw, cb, wg, bg, lam, rg_g):
    T = p.shape[0]
    tm = _row_tile(T, 832)
    nt = T // tm
    hb = tm // 8
    unroll = _scan_unroll(hb)

    def body(xg_ref, xc_ref, h_ref, hh_ref, dy_ref, dp_in_ref, cw_ref, cb_ref, w_ref, bg_ref, lam_ref, g_ref,
             dp_ref, gcw_ref, gcb_ref, gw_ref, gbg_ref, glam_ref, gg_ref,
             dext, a_s, b_s, d_s, gacc, carry_d, carry_a):
        i = pl.program_id(0)
        t_idx = nt - 1 - i

        @pl.when(i == 0)
        def _():
            dext[tm:tm + 8, :] = jnp.zeros((8, D_RG), F32)
            carry_d[...] = jnp.zeros_like(carry_d)
            carry_a[...] = jnp.zeros_like(carry_a)
            gacc[...] = jnp.zeros_like(gacc)
            for ref in (gcw_ref, gcb_ref, gbg_ref, glam_ref, gg_ref, gw_ref):
                ref[...] = jnp.zeros_like(ref)

        first = t_idx == 0
        xc = xc_ref[...]
        lam_ = lam_ref[...]
        r, ig, sp, a, m, inv_m = _rg_gates(xc, w_ref, bg_ref, lam_)
        row = t_idx * tm + lax.broadcasted_iota(jnp.int32, (tm, 1), 0)
        valid = row >= PAD

        gr = xg_ref[:, D_RG:]
        g, dgelu = _gelu_parts(gr)
        h = h_ref[...]
        yy = g * h
        rr = _rms(yy)
        nn = yy * rr
        dy_ = dy_ref[...]
        gg_ref[...] += jnp.sum(dy_ * nn, axis=0, keepdims=True)
        dyy = _rms_bwd(dy_ * g_ref[...], nn, rr)
        dp_ref[:, D_RG:] = (dyy * h * dgelu).astype(BF16)

        a_s[...] = a
        b_s[...] = dyy * g
        rowi = lax.broadcasted_iota(jnp.int32, (8, D_RG), 0)

        def blk(jj, c):
            cd, ca = c
            for u in range(unroll):
                o = pl.multiple_of((hb - 1 - (jj * unroll + u)) * 8, 8)
                a_blk = a_s[pl.ds(o, 8), :]
                a_next = jnp.where(rowi == 7, ca, pltpu.roll(a_blk, 7, axis=0))
                A, B = _scan_block_bwd(a_next, b_s[pl.ds(o, 8), :], rowi)
                d = B + A * cd
                d_s[pl.ds(o, 8), :] = d
                cd, ca = d[0:1, :], a_blk[0:1, :]
            return cd, ca

        cd, ca = lax.fori_loop(0, hb // unroll, blk, (carry_d[...], carry_a[...]))
        carry_d[...] = cd
        carry_a[...] = ca
        delta = d_s[...]

        h_last_prev = jnp.where(first, 0.0, hh_ref[7:8, :])
        row0 = lax.broadcasted_iota(jnp.int32, (tm, 1), 0) == 0
        h_prev = jnp.where(row0, h_last_prev, pltpu.roll(h, 1, axis=0))
        dbx = jnp.where(valid, delta, 0.0)
        da = delta * h_prev
        di = dbx * m * xc
        dm = dbx * ig * xc
        dla = a * (da - dm * a * inv_m)
        dla = jnp.where(valid, dla, 0.0)
        glam_ref[...] += jnp.sum(dla * r, axis=0, keepdims=True) * (LRU_C / (1.0 + jnp.exp(lam_)))
        dr = (-LRU_C) * sp * dla
        dpre = jnp.concatenate([dr * r * (1.0 - r), di * ig * (1.0 - ig)], axis=1)
        gbg_ref[...] += jnp.sum(dpre, axis=0, keepdims=True)
        dpre_b = dpre.astype(BF16)
        gacc[...] += _dot_tn(xc.astype(BF16), dpre_b)
        dxc = dbx * m * ig + _dot_nt(dpre_b, w_ref[...])
        gcb_ref[...] += jnp.sum(dxc, axis=0, keepdims=True)
        dext[0:tm, :] = dxc
        xr = xg_ref[:, :D_RG]
        dxr = None
        for j in range(CONV_W):
            shifted = dext[3 - j:3 - j + tm, :]
            gcw_ref[j:j + 1, :] += jnp.sum(xr * shifted, axis=0, keepdims=True)
            tap = cw_ref[j:j + 1, :] * shifted
            dxr = tap if dxr is None else dxr + tap
        dp_ref[:, :D_RG] = dxr.astype(BF16)
        dext[tm:tm + 8, :] = dext[0:8, :]

        @pl.when(i == nt - 1)
        def _():
            fold = _head_fold()
            mask = _head_mask()
            for k in range(2):
                blockdiag = jnp.where(mask, gacc[:, k * D_RG:(k + 1) * D_RG], 0.0)
                gw_ref[k * D_RG:(k + 1) * D_RG, :] = jnp.dot(blockdiag, fold, precision=HIGHEST,
                                                             preferred_element_type=F32)

    vec = lambda n: pl.BlockSpec((1, n), lambda i: (0, 0))
    rev = lambda n: pl.BlockSpec((tm, n), lambda i: (nt - 1 - i, 0))
    halo = lambda n: pl.BlockSpec((8, n), lambda i: (jnp.maximum((nt - 1 - i) * hb - 1, 0), 0))
    return pl.pallas_call(
        body, grid=(nt,),
        in_specs=[rev(2 * D_RG), rev(D_RG), rev(D_RG), halo(D_RG), rev(D_RG), ANY,
                  pl.BlockSpec((CONV_W, D_RG), lambda i: (0, 0)), vec(D_RG),
                  pl.BlockSpec((D_RG, 2 * D_RG), lambda i: (0, 0)), vec(2 * D_RG), vec(D_RG), vec(D_RG)],
        out_specs=[rev(2 * D_RG), pl.BlockSpec((CONV_W, D_RG), lambda i: (0, 0)), vec(D_RG),
                   pl.BlockSpec((2 * D_RG, RG_HEAD_DIM), lambda i: (0, 0)), vec(2 * D_RG), vec(D_RG), vec(D_RG)],
        input_output_aliases={5: 0},
        out_shape=[jax.ShapeDtypeStruct((T, D_IN), BF16), jax.ShapeDtypeStruct((CONV_W, D_RG), F32),
                   jax.ShapeDtypeStruct((1, D_RG), F32), jax.ShapeDtypeStruct((2 * D_RG, RG_HEAD_DIM), F32),
                   jax.ShapeDtypeStruct((1, 2 * D_RG), F32), jax.ShapeDtypeStruct((1, D_RG), F32),
                   jax.ShapeDtypeStruct((1, D_RG), F32)],
        scratch_shapes=[pltpu.VMEM((tm + 8, D_RG), F32),
                        pltpu.VMEM((tm, D_RG), F32), pltpu.VMEM((tm, D_RG), F32), pltpu.VMEM((tm, D_RG), F32),
                        pltpu.VMEM((D_RG, 2 * D_RG), F32), pltpu.VMEM((1, D_RG), F32), pltpu.VMEM((1, D_RG), F32)],
        name="rg_bwd", compiler_params=_params("arbitrary"),
    )(p, xc_all, hs, hs, dy, dp, cw, cb, wg, bg, lam, rg_g)


def _hg_bwd(p, o_all, st_all, dy, lbraw, hg_g):
    T = p.shape[0]
    n_chunks = T // CHUNK
    cps = _chunks_per_step(n_chunks)
    rows = cps * CHUNK
    n_steps = n_chunks // cps

    def body(hq_ref, hf_ref, hi_ref, hg_ref, o_ref, st_ref, dy_ref, lb_ref, g_ref,
             dp_ref, glb_ref, gg_ref, dst):
        i = pl.program_id(0)

        @pl.when(i == 0)
        def _():
            dst[...] = jnp.zeros_like(dst)
            glb_ref[...] = jnp.zeros_like(glb_ref)
            gg_ref[...] = jnp.zeros_like(gg_ref)

        dp_ref[:, :2 * D_RG] = jnp.zeros((rows, 2 * D_RG), BF16)

        def chunk(jj, carry):
            j = cps - 1 - jj
            rs = pl.ds(pl.multiple_of(j * CHUNK, CHUNK), CHUNK)
            chunk_body((n_steps - 1 - i) * cps + j, hq_ref.at[rs, :], hf_ref.at[rs, :], hi_ref.at[rs, :],
                       hg_ref.at[rs, :], o_ref.at[rs, :], st_ref.at[pl.ds(j, 1)], dy_ref.at[rs, :], lb_ref, g_ref,
                       dp_ref.at[rs, pl.ds(2 * D_RG, 4 * D_HG)], glb_ref, gg_ref, dst)
            return carry

        lax.fori_loop(0, cps, chunk, 0, unroll=True)

    def chunk_body(n, hq_ref, hf_ref, hi_ref, hg_ref, o_ref, st_ref, dy_ref, lb_ref, g_ref,
                   dp_ref, glb_ref, gg_ref, dst):
        valid = (n * CHUNK + lax.broadcasted_iota(jnp.int32, (CHUNK, 1), 0)) >= PAD
        hq, hf, v, hg = hq_ref[...], hf_ref[...], hi_ref[...], hg_ref[...]
        lb, sq, q, sf, f, b = _hg_gates(hq, hf, lb_ref, valid)
        k = 1.0 - f
        causal = _causal()
        r_i = lax.broadcasted_iota(jnp.int32, (CHUNK, CHUNK), 0)
        c_i = lax.broadcasted_iota(jnp.int32, (CHUNK, CHUNK), 1)
        causal_t = r_i <= c_i
        is_last = lax.broadcasted_iota(jnp.int32, (CHUNK, 1), 0) == CHUNK - 1
        g_ = g_ref[...]
        db_parts, dq_parts, dk_parts = [], [], []
        gg = jnp.zeros((1, HG_HEAD_DIM), F32)
        heads = [slice(h * HG_HEAD_DIM, (h + 1) * HG_HEAD_DIM) for h in range(HG_HEADS)]

        do_parts = []
        for h, sl in enumerate(heads):
            o = o_ref[:, sl]
            ro = _rms(o)
            no = o * ro
            hgh = hg[:, sl]
            sg = _sigmoid(hgh)
            dyh = dy_ref[:, sl]
            dp_ref[:, 3 * D_HG + h * HG_HEAD_DIM:3 * D_HG + (h + 1) * HG_HEAD_DIM] = (
                dyh * no * g_ * sg * (1.0 + hgh * (1.0 - sg))).astype(BF16)
            dng = dyh * hgh * sg
            gg = gg + jnp.sum(dng * no, axis=0, keepdims=True)
            do_parts.append(_rms_bwd(dng * g_, no, ro))
        do_t = jnp.concatenate(do_parts, axis=1).T.astype(BF16)

        fac = []
        for sl, do in zip(heads, do_parts):
            qh, kh, bh = q[:, sl], k[:, sl], b[:, sl]
            blk, b_last, eb, eq, ekh, ek, q_hat, k_til = _hg_head(qh, kh, bh)
            fac.append(dict(qh=qh, kh=kh, blk=blk, e_last=jnp.exp(b_last), eb=eb, eq=eq, ekh=ekh, ek=ek,
                            q_til=qh * eb, k_hat=kh * ekh, qhb=q_hat.astype(BF16), ktb=k_til.astype(BF16),
                            vb=v[:, sl].astype(BF16), dob=do.astype(BF16)))

        first = []
        for sl, t in zip(heads, fac):
            st_h = st_ref[0, sl, :]
            dst_h = dst[sl, :]
            dstb = dst_h.astype(BF16)
            first.append(dict(
                att_t=_dot_nt(t["ktb"], t["qhb"]), datt=_dot_nt(t["dob"], t["vb"]),
                datt_t=_dot_nt(t["vb"], t["dob"]), dk_hat=_dot(t["vb"], dstb),
                dv=_dot_nt(t["k_hat"].astype(BF16), dstb), dq_til=_dot(t["dob"], st_h.astype(BF16)),
                state=t["e_last"] * jnp.sum(dst_h * st_h, axis=0, keepdims=True)))
            dst[sl, :] = dst_h * t["e_last"] + _dot(do_t[sl, :], t["q_til"].astype(BF16))

        for h, (t, m) in enumerate(zip(fac, first)):
            qh, kh, blk, eb, eq, ekh, ek = t["qh"], t["kh"], t["blk"], t["eb"], t["eq"], t["ekh"], t["ek"]
            q_til, k_hat, qhb, ktb, dob = t["q_til"], t["k_hat"], t["qhb"], t["ktb"], t["dob"]
            dk_hat, dq_til = m["dk_hat"], m["dq_til"]
            dv = m["dv"] + _dot(jnp.where(causal_t, m["att_t"], 0.0).astype(BF16), dob)
            dq_hat = _dot(jnp.where(causal, m["datt"], 0.0).astype(BF16), ktb)
            dk_til = _dot(jnp.where(causal_t, m["datt_t"], 0.0).astype(BF16), qhb)
            db_last = jnp.sum(dk_hat * k_hat, axis=0, keepdims=True) + m["state"]
            dq_sel = dq_hat[:, (N_SUB - 1) * HG_HEAD_DIM:]
            for s in range(N_SUB - 2, -1, -1):
                dq_sel = jnp.where(blk == s, dq_hat[:, s * HG_HEAD_DIM:(s + 1) * HG_HEAD_DIM], dq_sel)
            dq_a = dq_sel * eq
            dk_a = dk_til[:, :HG_HEAD_DIM] * ek[0]
            for s in range(1, N_SUB):
                dk_a = dk_a + dk_til[:, s * HG_HEAD_DIM:(s + 1) * HG_HEAD_DIM] * ek[s]
            db_att = qhb.astype(F32) * dq_hat - ktb.astype(F32) * dk_til
            db = dq_til * q_til - dk_hat * k_hat
            for s in range(N_SUB):
                db = db + db_att[:, s * HG_HEAD_DIM:(s + 1) * HG_HEAD_DIM]
            db_parts.append(jnp.where(is_last, db + db_last, db))
            dq_parts.append(dq_til * eb + dq_a)
            dk_parts.append(dk_hat * ekh + dk_a)
            dp_ref[:, 2 * D_HG + h * HG_HEAD_DIM:2 * D_HG + (h + 1) * HG_HEAD_DIM] = dv.astype(BF16)

        gg_ref[...] += gg
        db = jnp.concatenate(db_parts, axis=1)
        dq = jnp.concatenate(dq_parts, axis=1)
        dk = jnp.concatenate(dk_parts, axis=1)
        dlf = jnp.where(valid, _running_sum(db, False), 0.0)
        dp_ref[:, :D_HG] = (dq * sq * (1.0 + hq * (1.0 - sq))).astype(BF16)
        df = dlf / f - dk
        dlb = jnp.sum(df * (1.0 - sf), axis=0, keepdims=True) * lb * (1.0 - lb)
        glb_ref[0:1, :] += dlb
        glb_ref[1:2, :] += -dlb
        dp_ref[:, D_HG:2 * D_HG] = (df * (1.0 - lb) * sf * (1.0 - sf)).astype(BF16)

    rev = lambda j: pl.BlockSpec((rows, D_HG), lambda i: (n_steps - 1 - i, j))
    return pl.pallas_call(
        body, grid=(n_steps,),
        in_specs=[rev(2), rev(3), rev(4), rev(5), rev(0),
                  pl.BlockSpec((cps, D_HG, HG_HEAD_DIM), lambda i: (n_steps - 1 - i, 0, 0)), rev(1),
                  pl.BlockSpec((2, D_HG), lambda i: (0, 0)), pl.BlockSpec((1, HG_HEAD_DIM), lambda i: (0, 0))],
        out_specs=[pl.BlockSpec((rows, D_IN), lambda i: (n_steps - 1 - i, 0)),
                   pl.BlockSpec((2, D_HG), lambda i: (0, 0)), pl.BlockSpec((1, HG_HEAD_DIM), lambda i: (0, 0))],
        out_shape=[jax.ShapeDtypeStruct((T, D_IN), BF16), jax.ShapeDtypeStruct((2, D_HG), F32),
                   jax.ShapeDtypeStruct((1, HG_HEAD_DIM), F32)],
        scratch_shapes=[pltpu.VMEM((D_HG, HG_HEAD_DIM), F32)],
        name="hg_bwd", compiler_params=_params("arbitrary"),
    )(p, p, p, p, o_all, st_all, dy, lbraw, hg_g)


def _in_bwd(dp, w_in, h0, g1, dh1):
    T = h0.shape[0]
    tm = _row_tile(T, 832)
    n_steps = T // tm

    def body(dp_ref, w_ref, h_ref, g_ref, d1_ref, gx_hbm, gmeta_ref, gg_ref, buf, sems):
        i = pl.program_id(0)
        first, later = _window_copies(gx_hbm, buf, sems, tm)
        slot = i % 2

        @pl.when(i == 0)
        def _():
            gg_ref[...] = jnp.zeros_like(gg_ref)

        if n_steps > 2:
            @pl.when(i == 2)
            def _():
                first(False).wait()

            @pl.when(i > 2)
            def _():
                later(i - 2, slot, False).wait()

        du = _dot_nt(dp_ref[...], w_ref[...])
        h0_ = h_ref[...]
        r = _rms(h0_)
        n = h0_ * r
        gg_ref[...] += jnp.sum(du * n, axis=0, keepdims=True)
        dh0 = d1_ref[...] + _rms_bwd(du * g_ref[...], n, r)
        buf[slot] = dh0

        @pl.when(i == 0)
        def _():
            gmeta_ref[...] = dh0[PAD:HEAD, :]
            first(False).start()

        if n_steps > 1:
            @pl.when(i > 0)
            def _():
                later(i, slot, False).start()

        @pl.when(i == n_steps - 1)
        def _():
            if n_steps == 1:
                first(False).wait()
            else:
                if n_steps == 2:
                    first(False).wait()
                else:
                    later(i - 1, 1 - slot, False).wait()
                later(i, slot, False).wait()

    row = lambda n: pl.BlockSpec((tm, n), lambda i: (i, 0))
    return pl.pallas_call(
        body, grid=(n_steps,),
        in_specs=[row(D_IN), _resident((D_MODEL, D_IN)),
                  row(D_MODEL), pl.BlockSpec((1, D_MODEL), lambda i: (0, 0)), row(D_MODEL)],
        out_specs=[pl.BlockSpec(memory_space=pl.ANY), pl.BlockSpec((N_META, D_MODEL), lambda i: (0, 0)),
                   pl.BlockSpec((1, D_MODEL), lambda i: (0, 0))],
        out_shape=[jax.ShapeDtypeStruct((T - HEAD, D_MODEL), F32), jax.ShapeDtypeStruct((N_META, D_MODEL), F32),
                   jax.ShapeDtypeStruct((1, D_MODEL), F32)],
        scratch_shapes=[pltpu.VMEM((2, tm, D_MODEL), F32), pltpu.SemaphoreType.DMA((2,))],
        name="in_bwd", compiler_params=_params("arbitrary"),
    )(dp, w_in, h0, g1, dh1)


def _col_tile(cols, target):
    best = None
    for t in range(128, min(cols, target) + 1, 128):
        if cols % t == 0:
            best = t
    assert best is not None, cols
    return best


MXU_DIM = 256


def _mxu_tile(cols, target):
    best = None
    for t in range(MXU_DIM, min(cols, target) + 1, MXU_DIM):
        if cols % t == 0:
            best = t
    assert best is not None, cols
    return best


def _weight_grad(a, b, name):
    T, M = a.shape
    N = b.shape[1]
    tm = _col_tile(M, 1408)
    tn = _mxu_tile(N, 768 if tm <= 1024 else 512)

    def body(a_ref, b_ref, o_ref):
        o_ref[...] = _dot_tn(a_ref[...], b_ref[...])

    return pl.pallas_call(
        body, grid=(M // tm, N // tn),
        in_specs=[pl.BlockSpec((T, tm), lambda m, n: (0, m)), pl.BlockSpec((T, tn), lambda m, n: (0, n))],
        out_specs=pl.BlockSpec((tm, tn), lambda m, n: (m, n)),
        out_shape=jax.ShapeDtypeStruct((M, N), F32),
        name=name, compiler_params=_params("parallel", "parallel"),
    )(a, b)


def _local_step(x, meta, target, w_in_own, w_in, w_out, w_gu, w_down, small, chip, on_ffn_grads=None,
                on_mixer_grads=None):
    wg = _gate_weights(small["w_rgate"], small["w_igate"])
    bg = jnp.concatenate([small["b_rgate"], small["b_igate"]], axis=1)

    p, u, h0 = _in_proj_local(x, meta, small["mix_norm_g"], w_in_own, chip)
    p = _in_proj_rest(u, w_in, p, chip)
    y_rg, hs, xc = _rg_fwd(p, small["conv_w"], small["conv_b"], wg, bg, small["lru_lambda"], small["rg_norm_g"])
    y_hg, o_all, st_all = _hg_fwd(p, small["hg_lower_bound"], small["hg_norm_g"])
    h1, v, yb, gu, act, dh2, dh2b, loss, g_final = _ffn_fwd(
        h0, y_rg, y_hg, w_out, small["ffn_norm_g"], w_gu, w_down, small["final_norm_g"], target)

    g_w_down = _weight_grad(act, dh2b, "grad_w_down")
    dgu, dh1, dh1b, dy, g_ffn = _ffn_bwd(dh2b, gu, w_down, w_gu, h1, small["ffn_norm_g"], dh2, w_out)
    ffn_grads = {"w_gate_up": _weight_grad(v, dgu, "grad_w_gate_up"), "w_down": g_w_down,
                 "w_out": _weight_grad(yb, dh1b, "grad_w_out")}
    stages = on_ffn_grads(ffn_grads) if on_ffn_grads is not None else None
    dp, g_lb, g_hgn = _hg_bwd(p, o_all, st_all, dy, small["hg_lower_bound"], small["hg_norm_g"])
    early = late = None
    if stages is not None:
        chip_sums, send = stages
        sums = chip_sums()
        (dp, dy), sums = lax.optimization_barrier(((dp, dy), sums))
        early = send(sums)
    dp, g_cw, g_cb, g_wgate, g_bg, g_lam, g_rgn = _rg_bwd(
        p, xc, hs, dy, dp, small["conv_w"], small["conv_b"], wg, bg, small["lru_lambda"], small["rg_norm_g"])
    mixer_grads = {"w_in": _weight_grad(u, dp, "grad_w_in")}
    if on_mixer_grads is not None:
        chip_sums, send = on_mixer_grads(mixer_grads)
        sums = chip_sums()
        (dp, dh1), sums = lax.optimization_barrier(((dp, dh1), sums))
        late = send(sums)
    grad_x, g_meta, g_mix = _in_bwd(dp, w_in, h0, small["mix_norm_g"], dh1)

    grads = {
        "w_in": mixer_grads["w_in"], "w_out": ffn_grads["w_out"],
        "w_gate_up": ffn_grads["w_gate_up"], "w_down": ffn_grads["w_down"],
        "meta_tokens": g_meta, "mix_norm_g": g_mix, "conv_w": g_cw, "conv_b": g_cb, "w_gates": g_wgate,
        "b_rgate": g_bg[:, :D_RG], "b_igate": g_bg[:, D_RG:], "lru_lambda": g_lam, "rg_norm_g": g_rgn,
        "hg_lower_bound": g_lb, "hg_norm_g": g_hgn, "ffn_norm_g": g_ffn, "final_norm_g": g_final,
    }
    return loss, grad_x, grads, early, late


ANY = pl.BlockSpec(memory_space=pl.ANY)
HALF = D_MODEL // 2

BIG = {"w_in": (D_MODEL, D_IN // N_CHIPS, True), "w_gate_up": (D_MODEL, 2 * D_FF // N_CHIPS, True),
       "w_out": (D_MODEL // N_CHIPS, D_MODEL, False), "w_down": (D_FF // N_CHIPS, D_MODEL, False)}
BIG_NAMES = tuple(BIG)
N_BIG = len(BIG_NAMES)


def _full_shape(name):
    rows, cols, by_col = BIG[name]
    return (rows, cols * N_CHIPS) if by_col else (rows * N_CHIPS, cols)


def _place():
    return lax.axis_index("x"), lax.axis_index("y"), lax.axis_index("c")


def _chip_of(x, y, r):
    fx, fy = (r + 1) >> 1, (r + 1) & 1
    return (1 - x if fx else x), (1 - y if fy else y)


def _half_of(ref, by_col, half):
    start = pl.multiple_of(half * HALF, 128)
    return ref.at[pl.ds(start, HALF), :] if by_col else ref.at[:, pl.ds(start, HALF)]


def _shard_of(ref, name, chip):
    rows, cols, by_col = BIG[name]
    if by_col:
        return ref.at[:, pl.ds(pl.multiple_of(chip * cols, 128), cols)]
    return ref.at[pl.ds(pl.multiple_of(chip * rows, 16), rows), :]


def _shard_half_of(ref, name, chip, half):
    rows, cols, by_col = BIG[name]
    start = pl.multiple_of(half * HALF, 128)
    if by_col:
        return ref.at[pl.ds(start, HALF), pl.ds(pl.multiple_of(chip * cols, 128), cols)]
    return ref.at[pl.ds(pl.multiple_of(chip * rows, 16), rows), pl.ds(start, HALF)]


def _remote(src, dst, send_sems, recv_sems, k, dev):
    return pltpu.make_async_remote_copy(src_ref=src, dst_ref=dst, send_sem=send_sems.at[k], recv_sem=recv_sems.at[k],
                                        device_id=dev, device_id_type=MESH)


def _place_shards(w, small, chip):
    steps = 4
    ns = len(small)
    in_specs, out_specs = [], []
    for name in BIG_NAMES:
        rows, cols, by_col = BIG[name]
        tr = rows // steps
        in_specs.append(pl.BlockSpec((tr, cols), lambda i, s: (i, 0)))
        if by_col:
            out_specs.append(pl.BlockSpec((tr, cols), lambda i, s: (i, s[0])))
        else:
            out_specs.append(pl.BlockSpec((tr, cols), lambda i, s: (s[0] * steps + i, 0)))

    def body(s_ref, *refs):
        ins, small_in = refs[:N_BIG], refs[N_BIG:N_BIG + ns]
        outs, small_out = refs[N_BIG + ns:2 * N_BIG + ns], refs[2 * N_BIG + ns:2 * (N_BIG + ns)]
        send_sems, recv_sems, local_sems = refs[2 * (N_BIG + ns):]
        i = pl.program_id(0)
        x, y, c = _place()
        chip_ = 2 * x + y
        others = [_chip_of(x, y, r) for r in range(3)]

        def block(a, q):
            cols = small[a].shape[1]
            return small_out[a].at[:, pl.ds(pl.multiple_of(q * cols, 128), cols)]

        def local(a):
            return pltpu.make_async_copy(small_in[a], block(a, chip_), local_sems.at[a])

        def remote(a, r):
            qx, qy = others[r]
            return _remote(small_in[a], block(a, chip_), send_sems, recv_sems, 3 * a + r, (qx, qy, c))

        @pl.when(i == 0)
        def _():
            for a in range(ns):
                local(a).start()
                for r in range(3):
                    remote(a, r).start()

        for a in range(N_BIG):
            outs[a][...] = ins[a][...].astype(BF16)

        @pl.when(i == steps - 1)
        def _():
            for a in range(ns):
                for r, (qx, qy) in enumerate(others):
                    landed = block(a, 2 * qx + qy)
                    _remote(landed, landed, send_sems, recv_sems, 3 * a + r, (qx, qy, c)).wait_recv()
                for r in range(3):
                    remote(a, r).wait_send()
                local(a).wait()

    out = pl.pallas_call(
        body,
        grid_spec=pltpu.PrefetchScalarGridSpec(
            num_scalar_prefetch=1, grid=(steps,), in_specs=in_specs + [ANY] * ns, out_specs=out_specs + [ANY] * ns,
            scratch_shapes=[pltpu.SemaphoreType.DMA((3 * ns,)), pltpu.SemaphoreType.DMA((3 * ns,)),
                            pltpu.SemaphoreType.DMA((ns,))]),
        out_shape=([jax.ShapeDtypeStruct(_full_shape(name), BF16) for name in BIG_NAMES]
                   + [jax.ShapeDtypeStruct((s.shape[0], s.shape[1] * N_CHIPS), F32) for s in small]),
        name="place_shards", compiler_params=_params("arbitrary"),
    )(chip, *[w[name] for name in BIG_NAMES], *small)
    return dict(zip(BIG_NAMES, out[:N_BIG])), list(out[N_BIG:])


def _gather_weights(placed, small, names, label, collective_id):
    n, ns = len(names), len(small)
    hbm = pltpu.MemorySpace.HBM
    outs = [jax.new_ref(placed[nm], memory_space=hbm) for nm in names]
    small_in = [jax.new_ref(s, memory_space=hbm) for s in small]
    small_out = [jax.empty_ref(jax.ShapeDtypeStruct((s.shape[0], s.shape[1] * N_CHIPS), F32), memory_space=hbm)
                 for s in small]
    n_sems = 6 * n + 3 * ns

    @pl.kernel(mesh=plsc.ScalarSubcoreMesh(axis_name="seq", num_cores=1), name=label, out_type=(),
               scratch_types=(pltpu.SemaphoreType.DMA((n_sems,)), pltpu.SemaphoreType.DMA((n_sems,)),
                              pltpu.SemaphoreType.DMA((max(ns, 1),))),
               compiler_params=pltpu.CompilerParams(collective_id=collective_id))
    def launch(send_sems, recv_sems, local_sems):
        x, y, c = _place()
        chip = 2 * x + y
        sibling = (x, y, 1 - c)
        others = [_chip_of(x, y, r) for r in range(3)]
        _handshake([(qx, qy, c) for qx, qy in others] + [sibling])

        def small_block(a, q):
            cols = small[a].shape[1]
            return small_out[a].at[:, pl.ds(pl.multiple_of(q * cols, 128), cols)]

        local = [pltpu.make_async_copy(small_in[a], small_block(a, chip), local_sems.at[a]) for a in range(ns)]
        for cp in local:
            cp.start()

        sends = []
        for a, name in enumerate(names):
            mine = _shard_half_of(outs[a], name, chip, c)
            for r, (qx, qy) in enumerate(others):
                sends.append(_remote(mine, mine, send_sems, recv_sems, 6 * a + r, (qx, qy, c)))
        for a in range(ns):
            for r, (qx, qy) in enumerate(others):
                sends.append(_remote(small_in[a], small_block(a, chip), send_sems, recv_sems,
                                     6 * n + 3 * a + r, (qx, qy, c)))
        for cp in sends:
            cp.start()

        forwards = []
        for a, name in enumerate(names):
            for r, (qx, qy) in enumerate(others):
                landed = _shard_half_of(outs[a], name, 2 * qx + qy, c)
                _remote(landed, landed, send_sems, recv_sems, 6 * a + r, (qx, qy, c)).wait_recv()
                fwd = _remote(landed, landed, send_sems, recv_sems, 6 * a + 3 + r, sibling)
                fwd.start()
                forwards.append(fwd)
        for a in range(ns):
            for r, (qx, qy) in enumerate(others):
                landed = small_block(a, 2 * qx + qy)
                _remote(landed, landed, send_sems, recv_sems, 6 * n + 3 * a + r, (qx, qy, c)).wait_recv()
        for a, name in enumerate(names):
            for r, (qx, qy) in enumerate(others):
                landed = _shard_half_of(outs[a], name, 2 * qx + qy, 1 - c)
                _remote(landed, landed, send_sems, recv_sems, 6 * a + 3 + r, sibling).wait_recv()
        for cp in sends + forwards:
            cp.wait_send()
        for cp in local:
            cp.wait()

    launch()
    return {nm: ref[...] for nm, ref in zip(names, outs)}, [ref[...] for ref in small_out]


def _exchange_halves(grads, names, label, collective_id):
    n = len(names)
    sequencer = collective_id is not None

    def body(*refs):
        ins, outs = refs[:n], refs[n:2 * n]
        send_sems, recv_sems = refs[2 * n:]
        x, y, c = _place()
        if sequencer:
            _handshake([(x, y, 1 - c)])
        copies = []
        for a, name in enumerate(names):
            copies.append(_remote(_half_of(ins[a], BIG[name][2], 1 - c), outs[a], send_sems, recv_sems, a,
                                  (x, y, 1 - c)))
        for cp in copies:
            cp.start()
        for cp in copies:
            cp.wait()

    def half_shape(name):
        r, c_ = _full_shape(name)
        return (HALF, c_) if BIG[name][2] else (r, HALF)

    out_type = tuple(jax.ShapeDtypeStruct(half_shape(nm), F32) for nm in names)
    sems = (pltpu.SemaphoreType.DMA((n,)), pltpu.SemaphoreType.DMA((n,)))
    operands = [grads[nm] for nm in names]
    if sequencer:
        got = pl.kernel(
            body, mesh=plsc.ScalarSubcoreMesh(axis_name="seq", num_cores=1), name=label, out_type=out_type,
            scratch_types=sems, compiler_params=pltpu.CompilerParams(collective_id=collective_id),
        )(*operands)
    else:
        got = pl.pallas_call(
            body, in_specs=[ANY] * n, out_specs=[ANY] * n, out_shape=list(out_type), scratch_shapes=list(sems),
            name=label,
        )(*operands)
    return dict(zip(names, got))


def _chip_sum(grads, got, names, core, label):
    n = len(names)
    steps = 4
    g_specs, blks = [], []
    for name in names:
        rows, cols = got[name].shape
        tr = rows // steps
        if BIG[name][2]:
            g_specs.append(pl.BlockSpec((tr, cols), lambda i, s: (s[0] * steps + i, 0)))
        else:
            g_specs.append(pl.BlockSpec((tr, HALF), lambda i, s: (i, s[0])))
        blks.append(pl.BlockSpec((tr, cols), lambda i, s: (i, 0)))

    def body(s_ref, *refs):
        for a in range(n):
            t = refs[a][...] + refs[n + a][...]
            refs[2 * n + a][...] = t
            refs[3 * n + a][...] = t.astype(BF16)

    out = pl.pallas_call(
        body,
        grid_spec=pltpu.PrefetchScalarGridSpec(num_scalar_prefetch=1, grid=(steps,), in_specs=g_specs + blks,
                                               out_specs=blks + blks),
        out_shape=([jax.ShapeDtypeStruct(got[nm].shape, F32) for nm in names]
                   + [jax.ShapeDtypeStruct(got[nm].shape, BF16) for nm in names]),
        name=label, compiler_params=_params("parallel"),
    )(core, *[grads[nm] for nm in names], *[got[nm] for nm in names])
    return {nm: (out[a], out[n + a]) for a, nm in enumerate(names)}


def _piece_shape(name):
    rows, cols, by_col = BIG[name]
    return (HALF, cols) if by_col else (rows, HALF)


def _handshake(peers):
    barrier = pltpu.get_barrier_semaphore()
    for peer in peers:
        pl.semaphore_signal(barrier, inc=1, device_id=peer, device_id_type=MESH)
    pl.semaphore_wait(barrier, len(peers))


def _send_chip_sums(sums, names, label, collective_id):
    n = len(names)

    def body(*refs):
        ins, outs = refs[:n], refs[n:2 * n]
        send_sems, recv_sems = refs[2 * n:]
        x, y, c = _place()
        others = [_chip_of(x, y, r) for r in range(3)]
        _handshake([(qx, qy, c) for qx, qy in others])
        copies = []
        for a, name in enumerate(names):
            for r, (qx, qy) in enumerate(others):
                copies.append(_remote(_shard_of(ins[a], name, 2 * qx + qy), outs[a].at[r], send_sems, recv_sems,
                                      3 * a + r, (qx, qy, c)))
        for cp in copies:
            cp.start()
        for cp in copies:
            cp.wait()

    return pl.kernel(
        body, mesh=plsc.ScalarSubcoreMesh(axis_name="seq", num_cores=1), name=label,
        out_type=tuple(jax.ShapeDtypeStruct((3,) + _piece_shape(nm), BF16) for nm in names),
        scratch_types=(pltpu.SemaphoreType.DMA((3 * n,)), pltpu.SemaphoreType.DMA((3 * n,))),
        compiler_params=pltpu.CompilerParams(collective_id=collective_id),
    )(*[sums[nm] for nm in names])


def _total(parts, chip_core):
    steps = 2
    in_specs, out_specs, operands = [], [], []
    for name in BIG_NAMES:
        by_col = BIG[name][2]
        pr, pc = _piece_shape(name)
        tr = pr // steps
        if by_col:
            in_specs.append(pl.BlockSpec((tr, pc), lambda i, s: (i, s[0])))
            out_specs.append(pl.BlockSpec((tr, pc), lambda i, s: (s[1] * steps + i, 0)))
        else:
            in_specs.append(pl.BlockSpec((tr, pc), lambda i, s: (s[0] * steps + i, 0)))
            out_specs.append(pl.BlockSpec((tr, pc), lambda i, s: (i, s[1])))
        for r in range(3):
            in_specs.append(pl.BlockSpec((None, tr, pc), lambda i, s, r=r: (r, i, 0)))
        own, got = parts[name]
        operands += [own, got, got, got]

    def body(s_ref, *refs):
        for a in range(N_BIG):
            o_ref, a_ref, b_ref, c_ref = refs[4 * a:4 * a + 4]
            refs[4 * N_BIG + a][...] = (((o_ref[...] + a_ref[...].astype(F32)) + b_ref[...].astype(F32))
                                        + c_ref[...].astype(F32))

    totals = pl.pallas_call(
        body,
        grid_spec=pltpu.PrefetchScalarGridSpec(num_scalar_prefetch=1, grid=(steps,), in_specs=in_specs,
                                               out_specs=out_specs),
        out_shape=[jax.ShapeDtypeStruct(BIG[name][:2], F32) for name in BIG_NAMES],
        name="totals", compiler_params=_params("parallel"),
    )(chip_core, *operands)
    return dict(zip(BIG_NAMES, totals))


VEC_ROWS = 32
VEC_ROW = {"mix_norm_g": 0, "conv_b": 1, "b_rgate": 2, "b_igate": 3, "lru_lambda": 4, "rg_norm_g": 5,
           "hg_lower_bound": 6, "hg_norm_g": 8, "ffn_norm_g": 9, "final_norm_g": 10, "loss": 11,
           "conv_w": 12, "meta_tokens": 16}
N_DEV = 8


def _all_reduce_small(pieces, gates, totals):
    names = list(pieces)
    n_small = 10
    hv, hg = VEC_ROWS // 2, gates.shape[0] // 2

    def body(*refs):
        ins = refs[:len(names)]
        g_ref = refs[len(names)]
        vec_ref, gsum_ref = refs[len(names) + 1 + N_BIG:len(names) + 3 + N_BIG]
        big = refs[len(names) + 3 + N_BIG:len(names) + 3 + 2 * N_BIG]
        (mine_v, sib_v, sib_g, chip_v, chip_g, got_v, got_g, send_sems, recv_sems) = refs[len(names) + 3 + 2 * N_BIG:]
        x, y, c = _place()
        chip = 2 * x + y
        sibling = (x, y, 1 - c)
        share = []
        for a, name in enumerate(BIG_NAMES):
            half = _half_of(big[a], BIG[name][2], c)
            share.append(_remote(half, half, send_sems, recv_sems, n_small + a, sibling))
        for cp in share:
            cp.start()
        mine_v[...] = jnp.zeros_like(mine_v)
        for name, ref in zip(names, ins):
            nr, w = ref.shape
            mine_v[VEC_ROW[name]:VEC_ROW[name] + nr, 0:w] = ref[...]

        swap = [_remote(mine_v, sib_v, send_sems, recv_sems, 0, sibling),
                _remote(g_ref, sib_g, send_sems, recv_sems, 1, sibling)]
        for cp in swap:
            cp.start()
        for cp in swap:
            cp.wait()
        chip_v[...] = mine_v[...] + sib_v[...]
        chip_g[...] = g_ref[...] + sib_g[...]

        rows_v = pl.ds(pl.multiple_of(c * hv, 8), hv)
        rows_g = pl.ds(pl.multiple_of(c * hg, 8), hg)
        got_v[chip] = chip_v[rows_v, :]
        got_g[chip] = chip_g[rows_g, :]
        sends = []
        for r in range(3):
            qx, qy = _chip_of(x, y, r)
            sends.append(_remote(chip_v.at[rows_v, :], got_v.at[chip], send_sems, recv_sems, 2 + r, (qx, qy, c)))
            sends.append(_remote(chip_g.at[rows_g, :], got_g.at[chip], send_sems, recv_sems, 5 + r, (qx, qy, c)))
        for cp in sends:
            cp.start()
        for cp in sends:
            cp.wait()
        vec_ref[rows_v, :] = ((got_v[0] + got_v[1]) + got_v[2]) + got_v[3]
        gsum_ref[rows_g, :] = ((got_g[0] + got_g[1]) + got_g[2]) + got_g[3]

        back = [_remote(vec_ref.at[rows_v, :], vec_ref.at[rows_v, :], send_sems, recv_sems, 8, sibling),
                _remote(gsum_ref.at[rows_g, :], gsum_ref.at[rows_g, :], send_sems, recv_sems, 9, sibling)]
        for cp in back:
            cp.start()
        theirs_v = vec_ref.at[pl.ds(pl.multiple_of((1 - c) * hv, 8), hv), :]
        theirs_g = gsum_ref.at[pl.ds(pl.multiple_of((1 - c) * hg, 8), hg), :]
        _remote(theirs_v, theirs_v, send_sems, recv_sems, 8, sibling).wait_recv()
        _remote(theirs_g, theirs_g, send_sems, recv_sems, 9, sibling).wait_recv()
        for cp in back:
            cp.wait_send()
        for a, name in enumerate(BIG_NAMES):
            theirs = _half_of(big[a], BIG[name][2], 1 - c)
            _remote(theirs, theirs, send_sems, recv_sems, n_small + a, sibling).wait_recv()
        for cp in share:
            cp.wait_send()

    vmem = pl.BlockSpec(memory_space=pltpu.VMEM)
    n_sems = n_small + N_BIG
    out = pl.pallas_call(
        body, in_specs=[vmem] * (len(names) + 1) + [ANY] * N_BIG, out_specs=[vmem, vmem] + [ANY] * N_BIG,
        out_shape=([jax.ShapeDtypeStruct((VEC_ROWS, D_MODEL), F32), jax.ShapeDtypeStruct(gates.shape, F32)]
                   + [jax.ShapeDtypeStruct(BIG[n][:2], F32) for n in BIG_NAMES]),
        input_output_aliases={len(names) + 1 + a: 2 + a for a in range(N_BIG)},
        scratch_shapes=[pltpu.VMEM((VEC_ROWS, D_MODEL), F32), pltpu.VMEM((VEC_ROWS, D_MODEL), F32),
                        pltpu.VMEM(gates.shape, F32), pltpu.VMEM((VEC_ROWS, D_MODEL), F32),
                        pltpu.VMEM(gates.shape, F32), pltpu.VMEM((N_CHIPS, hv, D_MODEL), F32),
                        pltpu.VMEM((N_CHIPS, hg) + gates.shape[1:], F32),
                        pltpu.SemaphoreType.DMA((n_sems,)), pltpu.SemaphoreType.DMA((n_sems,))],
        name="all_reduce_small",
    )(*[pieces[n] for n in names], gates, *[totals[n] for n in BIG_NAMES])
    return out[0], out[1], dict(zip(BIG_NAMES, out[2:]))


def _adamw_math(w, g, m, v):
    m = ADAM_B1 * m + (1.0 - ADAM_B1) * g
    v = ADAM_B2 * v + (1.0 - ADAM_B2) * (g * g)
    m_hat = m / (1.0 - ADAM_B1 ** ADAM_STEP)
    v_hat = v / (1.0 - ADAM_B2 ** ADAM_STEP)
    delta = -ADAM_LR * (m_hat / (jnp.sqrt(v_hat) + ADAM_EPS) + ADAM_WD * w)
    return delta, m, v


def _adamw_big(w, g, m, v):
    steps = 8
    blks = []
    for name in BIG_NAMES:
        rows, cols, _ = BIG[name]
        blks.append(pl.BlockSpec((rows // steps, cols), lambda i: (i, 0)))

    def body(*refs):
        ins, outs = refs[:4 * N_BIG], refs[4 * N_BIG:]
        for a in range(N_BIG):
            w_ref, g_ref, m_ref, v_ref = (ins[k * N_BIG + a] for k in range(4))
            g = g_ref[...]
            d, nm, nv = _adamw_math(w_ref[...], g, m_ref[...], v_ref[...])
            outs[a][...] = g
            outs[N_BIG + a][...] = d
            outs[2 * N_BIG + a][...] = nm
            outs[3 * N_BIG + a][...] = nv

    shapes = [jax.ShapeDtypeStruct(BIG[name][:2], F32) for name in BIG_NAMES]
    out = pl.pallas_call(
        body, grid=(steps,), in_specs=blks * 4, out_specs=blks * 4, out_shape=shapes * 4,
        name="adamw_big", compiler_params=_params("parallel"),
    )(*[t[name] for t in (w, g, m, v) for name in BIG_NAMES])
    return {name: tuple(out[k * N_BIG + a] for k in range(4)) for a, name in enumerate(BIG_NAMES)}


SMALL = {"meta_tokens": (N_META, D_MODEL // N_CHIPS), "mix_norm_g": (1, D_MODEL), "conv_w": (CONV_W, D_RG // N_CHIPS),
         "conv_b": (1, D_RG), "w_rgate": (D_RG, RG_HEAD_DIM), "b_rgate": (1, D_RG), "w_igate": (D_RG, RG_HEAD_DIM),
         "b_igate": (1, D_RG), "lru_lambda": (1, D_RG), "rg_norm_g": (1, D_RG), "hg_lower_bound": (2, D_HG),
         "hg_norm_g": (1, HG_HEAD_DIM), "ffn_norm_g": (1, D_MODEL), "final_norm_g": (1, D_MODEL)}
SMALL_NAMES = tuple(SMALL)
SHARDED_SMALL = ("meta_tokens", "conv_w")


def _adamw_small(vec, gates, w, m, v):
    n = len(SMALL_NAMES)

    def body(*refs):
        vec_ref, gates_ref = refs[:2]
        w_refs, m_refs, v_refs = refs[2:2 + n], refs[2 + n:2 + 2 * n], refs[2 + 2 * n:2 + 3 * n]
        outs = refs[2 + 3 * n:]
        loss_ref = outs[0]
        x, y, _ = _place()
        chip = 2 * x + y
        loss_ref[...] = vec_ref[VEC_ROW["loss"]:VEC_ROW["loss"] + 1, 0:1]

        def update(k, g):
            g_ref, d_ref, nm_ref, nv_ref = outs[1 + 4 * k:5 + 4 * k]
            g_ref[...] = g
            d_ref[...], nm_ref[...], nv_ref[...] = _adamw_math(w_refs[k][...], g, m_refs[k][...], v_refs[k][...])

        for k, name in enumerate(SMALL_NAMES):
            nr, w_ = SMALL[name]
            if name == "w_rgate":
                update(k, gates_ref[0:D_RG, :])
            elif name == "w_igate":
                update(k, gates_ref[D_RG:2 * D_RG, :])
            elif name in SHARDED_SMALL:
                r0 = VEC_ROW[name]
                for q in range(N_CHIPS):
                    @pl.when(chip == q)
                    def _(k=k, r0=r0, nr=nr, w_=w_, q=q):
                        update(k, vec_ref[r0:r0 + nr, q * w_:(q + 1) * w_])
            else:
                r0 = VEC_ROW[name]
                update(k, vec_ref[r0:r0 + nr, 0:w_])

    vmem = pl.BlockSpec(memory_space=pltpu.VMEM)
    out_shape = [jax.ShapeDtypeStruct((1, 1), F32)]
    for name in SMALL_NAMES:
        out_shape += [jax.ShapeDtypeStruct(SMALL[name], F32)] * 4
    outs = pl.pallas_call(
        body, in_specs=[vmem] * (2 + 3 * n), out_specs=[vmem] * len(out_shape), out_shape=out_shape,
        name="adamw_small",
    )(vec, gates, *[w[k] for k in SMALL_NAMES], *[m[k] for k in SMALL_NAMES], *[v[k] for k in SMALL_NAMES])
    loss = outs[0]
    res = {name: tuple(outs[1 + 4 * k:5 + 4 * k]) for k, name in enumerate(SMALL_NAMES)}
    return loss, res


WEIGHT_NAMES = ("meta_tokens", "mix_norm_g", "w_in", "conv_w", "conv_b", "w_rgate", "b_rgate", "w_igate", "b_igate",
                "lru_lambda", "rg_norm_g", "hg_lower_bound", "hg_norm_g", "w_out", "ffn_norm_g", "w_gate_up", "w_down",
                "final_norm_g")


def _to_2d(name, a):
    if name in BIG:
        return a.reshape(BIG[name][:2])
    return a.reshape(SMALL[name])


def kernel(x, meta_tokens, mix_norm_g, w_in, conv_w, conv_b, w_rgate, b_rgate, w_igate, b_igate, lru_lambda, rg_norm_g, hg_lower_bound, hg_norm_g, w_out, ffn_norm_g, w_gate_up, w_down, final_norm_g, loss_target, m_meta_tokens, m_mix_norm_g, m_w_in, m_conv_w, m_conv_b, m_w_rgate, m_b_rgate, m_w_igate, m_b_igate, m_lru_lambda, m_rg_norm_g, m_hg_lower_bound, m_hg_norm_g, m_w_out, m_ffn_norm_g, m_w_gate_up, m_w_down, m_final_norm_g, v_meta_tokens, v_mix_norm_g, v_w_in, v_conv_w, v_conv_b, v_w_rgate, v_b_rgate, v_w_igate, v_b_igate, v_lru_lambda, v_rg_norm_g, v_hg_lower_bound, v_hg_norm_g, v_w_out, v_ffn_norm_g, v_w_gate_up, v_w_down, v_final_norm_g):
    w_raw = dict(zip(WEIGHT_NAMES, (meta_tokens, mix_norm_g, w_in, conv_w, conv_b, w_rgate, b_rgate, w_igate, b_igate,
                                    lru_lambda, rg_norm_g, hg_lower_bound, hg_norm_g, w_out, ffn_norm_g, w_gate_up,
                                    w_down, final_norm_g)))
    m_raw = dict(zip(WEIGHT_NAMES, (m_meta_tokens, m_mix_norm_g, m_w_in, m_conv_w, m_conv_b, m_w_rgate, m_b_rgate,
                                    m_w_igate, m_b_igate, m_lru_lambda, m_rg_norm_g, m_hg_lower_bound, m_hg_norm_g,
                                    m_w_out, m_ffn_norm_g, m_w_gate_up, m_w_down, m_final_norm_g)))
    v_raw = dict(zip(WEIGHT_NAMES, (v_meta_tokens, v_mix_norm_g, v_w_in, v_conv_w, v_conv_b, v_w_rgate, v_b_rgate,
                                    v_w_igate, v_b_igate, v_lru_lambda, v_rg_norm_g, v_hg_lower_bound, v_hg_norm_g,
                                    v_w_out, v_ffn_norm_g, v_w_gate_up, v_w_down, v_final_norm_g)))
    w = {k: _to_2d(k, a) for k, a in w_raw.items()}
    m = {k: _to_2d(k, a) for k, a in m_raw.items()}
    v = {k: _to_2d(k, a) for k, a in v_raw.items()}

    x_i, y_i, c_i = _place()
    core = jnp.reshape(c_i, (1,)).astype(jnp.int32)
    chip = jnp.reshape(2 * x_i + y_i, (1,)).astype(jnp.int32)
    chip_core = jnp.concatenate([chip, core])

    placed, (meta_full, cw_full) = _place_shards(w, [w["meta_tokens"], w["conv_w"]], chip)
    first, _ = _gather_weights(placed, [], ("w_in",), "gather_first", 1)
    rest, _ = _gather_weights(placed, [], ("w_out", "w_gate_up", "w_down"), "gather_rest", 2)
    full = {**first, **rest}

    seq = x.shape[1]
    small ={k: w[k] for k in SMALL_NAMES if k not in SHARDED_SMALL}
    small["conv_w"] = cw_full

    def reduce_to_chips(grads, names, tag, collective_ids):
        got = _exchange_halves(grads, names, "exchange_halves_" + tag, collective_ids[0])

        def chip_sums():
            return _chip_sum(grads, got, names, core, "chip_sum_" + tag)

        def send(sums):
            arrived = _send_chip_sums({n: sums[n][1] for n in names}, names, "send_chip_sums_" + tag,
                                      collective_ids[1])
            return {n: (sums[n][0], a) for n, a in zip(names, arrived)}

        return chip_sums, send

    ffn_names, mixer_names = ("w_gate_up", "w_down", "w_out"), ("w_in",)
    loss, grad_x, grads, parts, parts_mixer = _local_step(
        x.reshape(seq, D_MODEL), meta_full, loss_target.reshape(seq, D_MODEL),
        w["w_in"], full["w_in"], full["w_out"], full["w_gate_up"], full["w_down"], small, chip,
        on_ffn_grads=lambda g: reduce_to_chips(g, ffn_names, "ffn", (3, 4)),
        on_mixer_grads=lambda g: reduce_to_chips(g, mixer_names, "mixer", (None, 5)))
    parts.update(parts_mixer)
    totals = _total(parts, chip_core)
    pieces = {k: grads[k] for k in VEC_ROW if k != "loss"}
    pieces["loss"] = loss
    vec, gates, g_big = _all_reduce_small(pieces, grads["w_gates"], totals)
    loss_sum, res = _adamw_small(vec, gates, w, m, v)
    res.update(_adamw_big(w, g_big, m, v))

    out = [loss_sum.reshape(()), grad_x.reshape(1, seq, D_MODEL)]
    for j in range(4):
        out += [res[n][j].reshape(w_raw[n].shape) for n in WEIGHT_NAMES]
    return tuple(out)
```

```python
import math

import jax
import jax.numpy as jnp
from jax import lax
from jax.experimental import pallas as pl
from jax.experimental.pallas import tpu as pltpu
from jax.experimental.pallas import tpu_sc as plsc

F32 = jnp.float32
BF16 = jnp.bfloat16
MESH = pl.DeviceIdType.MESH

D_MODEL = 1024
D_RG = 512
RG_HEAD_DIM = 64
D_HG = 512
HG_HEAD_DIM = 128
HG_HEADS = 4
CHUNK = 64
SUB = 16
N_SUB = CHUNK // SUB
N_META = 16
PAD = CHUNK - N_META
D_IN = 3072
D_FF = 2816
CONV_W = 4
LRU_C = 8.0
EPS = 1e-6
EXP_CLAMP = 80.0
GELU_C = math.sqrt(2.0 / math.pi)
GELU_A = 0.044715
N_CHIPS = 4

ADAM_LR = 0.001
ADAM_B1 = 0.9
ADAM_B2 = 0.999
ADAM_EPS = 1e-08
ADAM_WD = 0.01
ADAM_STEP = 10

VMEM_LIMIT = 56 * 1024 * 1024


def _params(*sem):
    return pltpu.CompilerParams(dimension_semantics=sem, vmem_limit_bytes=VMEM_LIMIT)


def _row_tile(rows, target):
    best = None
    for t in range(16, min(rows, target) + 1, 16):
        if rows % t == 0:
            best = t
    assert best is not None, rows
    return best


def _sigmoid(x):
    return 0.5 * jnp.tanh(0.5 * x) + 0.5


def _dot(a, b):
    return jnp.dot(a, b, preferred_element_type=F32)


def _dot_nt(a, b):
    return lax.dot_general(a, b, (((1,), (1,)), ((), ())), preferred_element_type=F32)


def _dot_tn(a, b):
    return lax.dot_general(a, b, (((0,), (0,)), ((), ())), preferred_element_type=F32)


def _rms(x):
    return lax.rsqrt(jnp.mean(x * x, axis=-1, keepdims=True) + EPS)


def _rms_bwd(dn, n, r):
    return r * (dn - n * jnp.mean(dn * n, axis=-1, keepdims=True))


def _gelu_parts(x):
    t = jnp.tanh(GELU_C * (x + GELU_A * x * x * x))
    g = 0.5 * x * (1.0 + t)
    dg = 0.5 * (1.0 + t) + 0.5 * x * (1.0 - t * t) * GELU_C * (1.0 + 3.0 * GELU_A * x * x)
    return g, dg


def _softplus_neg(lam):
    e = jnp.exp(-jnp.abs(lam))
    w = 1.0 + e
    log1p = jnp.where(w == 1.0, e, jnp.log(w) * e / (w - 1.0))
    return jnp.maximum(-lam, 0.0) + log1p


def _head_mask():
    r = lax.broadcasted_iota(jnp.int32, (D_RG, D_RG), 0) // RG_HEAD_DIM
    c = lax.broadcasted_iota(jnp.int32, (D_RG, D_RG), 1) // RG_HEAD_DIM
    return r == c


def _head_fold():
    r = lax.broadcasted_iota(jnp.int32, (D_RG, RG_HEAD_DIM), 0) % RG_HEAD_DIM
    c = lax.broadcasted_iota(jnp.int32, (D_RG, RG_HEAD_DIM), 1)
    return (r == c).astype(F32)


def _gate_weights(w_r, w_i):
    def body(wr_ref, wi_ref, o_ref):
        fold = _head_fold()
        mask = _head_mask()
        for k, ref in enumerate((wr_ref, wi_ref)):
            full = _dot_nt(ref[...].astype(BF16), fold.astype(BF16))
            o_ref[:, k * D_RG:(k + 1) * D_RG] = jnp.where(mask, full, 0.0).astype(BF16)

    return pl.pallas_call(
        body, out_shape=jax.ShapeDtypeStruct((D_RG, 2 * D_RG), BF16), name="gate_weights",
    )(w_r, w_i)


HEAD = PAD + N_META


def _window_copies(seq_hbm, buf, sems, tm):
    def first(to_vmem):
        seq, vm = seq_hbm.at[pl.ds(0, tm - HEAD)], buf.at[0, pl.ds(HEAD, tm - HEAD)]
        return pltpu.make_async_copy(seq, vm, sems.at[0]) if to_vmem else pltpu.make_async_copy(vm, seq, sems.at[0])

    def later(j, slot, to_vmem):
        seq, vm = seq_hbm.at[pl.ds(pl.multiple_of(j * tm - HEAD, 8), tm)], buf.at[slot]
        if to_vmem:
            return pltpu.make_async_copy(seq, vm, sems.at[slot])
        return pltpu.make_async_copy(vm, seq, sems.at[slot])

    return first, later


def _fetch_window(seq_hbm, buf, sems, i, n_steps, tm):
    first, later = _window_copies(seq_hbm, buf, sems, tm)
    slot = i % 2

    @pl.when(i == 0)
    def _():
        first(True).start()

    if n_steps > 1:
        @pl.when(i + 1 < n_steps)
        def _():
            later(i + 1, 1 - slot, True).start()

    @pl.when(i == 0)
    def _():
        first(True).wait()

    if n_steps > 1:
        @pl.when(i > 0)
        def _():
            later(i, slot, True).wait()

    return slot


def _in_proj_local(x, meta, g1, w_own, chip):
    T = x.shape[0] + HEAD
    tm = _row_tile(T, 832)
    n_steps = T // tm
    cols = BIG["w_in"][1]

    def body(s_ref, x_hbm, meta_ref, g_ref, w_ref, p_ref, u_ref, h_ref, buf, sems, wb):
        i = pl.program_id(0)
        slot = _fetch_window(x_hbm, buf, sems, i, n_steps, tm)

        @pl.when(i == 0)
        def _():
            buf[0, 0:PAD, :] = jnp.zeros((PAD, D_MODEL), F32)
            buf[0, PAD:HEAD, :] = meta_ref[...]
            wb[...] = w_ref[...].astype(BF16)

        h = buf[slot]
        h_ref[...] = h
        u = (h * _rms(h) * g_ref[...]).astype(BF16)
        u_ref[...] = u
        p_ref[...] = _dot(u, wb[...])

    return pl.pallas_call(
        body,
        grid_spec=pltpu.PrefetchScalarGridSpec(
            num_scalar_prefetch=1, grid=(n_steps,),
            in_specs=[pl.BlockSpec(memory_space=pl.ANY),
                      pl.BlockSpec((N_META, D_MODEL), lambda i, s: (0, 0)),
                      pl.BlockSpec((1, D_MODEL), lambda i, s: (0, 0)),
                      pl.BlockSpec((D_MODEL, cols), lambda i, s: (0, 0))],
            out_specs=[pl.BlockSpec((tm, cols), lambda i, s: (i, s[0])),
                       pl.BlockSpec((tm, D_MODEL), lambda i, s: (i, 0)),
                       pl.BlockSpec((tm, D_MODEL), lambda i, s: (i, 0))],
            scratch_shapes=[pltpu.VMEM((2, tm, D_MODEL), F32), pltpu.SemaphoreType.DMA((2,)),
                            pltpu.VMEM((D_MODEL, cols), BF16)]),
        out_shape=[jax.ShapeDtypeStruct((T, D_IN), F32), jax.ShapeDtypeStruct((T, D_MODEL), BF16),
                   jax.ShapeDtypeStruct((T, D_MODEL), F32)],
        name="in_proj_local", compiler_params=_params("arbitrary"),
    )(chip, x, meta, g1, w_own)


def _in_proj_rest(u, w_in, p, chip):
    T = u.shape[0]
    tm = _row_tile(T, 2080)
    cols = BIG["w_in"][1]
    block = lambda j, s: (s[0] + 1 + j) % N_CHIPS

    def body(s_ref, u_ref, w_ref, p_in_ref, p_ref):
        p_ref[...] = _dot(u_ref[...], w_ref[...])

    return pl.pallas_call(
        body,
        grid_spec=pltpu.PrefetchScalarGridSpec(
            num_scalar_prefetch=1, grid=(N_CHIPS - 1, T // tm),
            in_specs=[pl.BlockSpec((tm, D_MODEL), lambda j, i, s: (i, 0)),
                      pl.BlockSpec((D_MODEL, cols), lambda j, i, s: (0, block(j, s))), ANY],
            out_specs=pl.BlockSpec((tm, cols), lambda j, i, s: (i, block(j, s)))),
        out_shape=jax.ShapeDtypeStruct((T, D_IN), F32),
        input_output_aliases={3: 0},
        name="in_proj_rest", compiler_params=_params("arbitrary", "arbitrary"),
    )(chip, u, w_in, p)


def _scan_block_fwd(A, B, rowi):
    for d in (1, 2, 4):
        a_sh = pltpu.roll(A, d, axis=0)
        b_sh = pltpu.roll(B, d, axis=0)
        m = rowi >= d
        B = jnp.where(m, A * b_sh + B, B)
        A = jnp.where(m, A * a_sh, A)
    return A, B


def _scan_block_bwd(A, B, rowi):
    for d in (1, 2, 4):
        a_sh = pltpu.roll(A, 8 - d, axis=0)
        b_sh = pltpu.roll(B, 8 - d, axis=0)
        m = rowi < 8 - d
        B = jnp.where(m, A * b_sh + B, B)
        A = jnp.where(m, A * a_sh, A)
    return A, B


def _rg_gates(xc, w_ref, bg_ref, lam):
    pre = _dot(xc.astype(BF16), w_ref[...]) + bg_ref[...]
    r = _sigmoid(pre[:, :D_RG])
    ig = _sigmoid(pre[:, D_RG:])
    sp = _softplus_neg(lam)
    la = -LRU_C * sp * r
    a = jnp.exp(la)
    th = jnp.tanh(la)
    u = 1.0 - th
    rc = pl.reciprocal(u, approx=True)
    rc = rc * (2.0 - u * rc)
    rc = rc * (2.0 - u * rc)
    m2 = -2.0 * th * rc
    inv_m = lax.rsqrt(jnp.maximum(m2, 1e-30))
    return r, ig, sp, a, m2 * inv_m, inv_m


def _conv(ext, cw_ref, cb_ref, tm):
    xc = cb_ref[...] + cw_ref[0:1, :] * ext[8 - 3:8 - 3 + tm, :]
    for j in range(1, CONV_W):
        xc = xc + cw_ref[j:j + 1, :] * ext[8 - 3 + j:8 - 3 + j + tm, :]
    return xc


def _scan_unroll(blocks):
    return 4 if blocks % 4 == 0 else 2 if blocks % 2 == 0 else 1


def _rg_fwd(p, cw, cb, wg, bg, lam, rg_g):
    T = p.shape[0]
    tm = _row_tile(T, 832)
    unroll = _scan_unroll(tm // 8)

    def body(xg_ref, cw_ref, cb_ref, w_ref, bg_ref, lam_ref, g_ref, y_ref, h_ref, xc_ref, ext, a_s, b_s, carry):
        i = pl.program_id(0)

        @pl.when(i == 0)
        def _():
            ext[0:8, :] = jnp.zeros((8, D_RG), F32)
            carry[...] = jnp.zeros((1, D_RG), F32)

        ext[8:8 + tm, :] = xg_ref[:, :D_RG]
        xc = _conv(ext, cw_ref, cb_ref, tm)
        xc_ref[...] = xc
        r, ig, sp, a, m, _ = _rg_gates(xc, w_ref, bg_ref, lam_ref[...])
        row = i * tm + lax.broadcasted_iota(jnp.int32, (tm, 1), 0)
        a_s[...] = a
        b_s[...] = jnp.where(row >= PAD, m * ig * xc, 0.0)
        rowi = lax.broadcasted_iota(jnp.int32, (8, D_RG), 0)

        def blk(j, c):
            for u in range(unroll):
                o = pl.multiple_of((j * unroll + u) * 8, 8)
                A, B = _scan_block_fwd(a_s[pl.ds(o, 8), :], b_s[pl.ds(o, 8), :], rowi)
                h = B + A * c
                h_ref[pl.ds(o, 8), :] = h
                c = h[7:8, :]
            return c

        carry[...] = lax.fori_loop(0, tm // (8 * unroll), blk, carry[...])
        ext[0:8, :] = ext[tm:tm + 8, :]
        g, _ = _gelu_parts(xg_ref[:, D_RG:])
        yy = g * h_ref[...]
        y_ref[...] = (yy * _rms(yy) * g_ref[...]).astype(BF16)

    vec = lambda n: pl.BlockSpec((1, n), lambda i: (0, 0))
    return pl.pallas_call(
        body, grid=(T // tm,),
        in_specs=[pl.BlockSpec((tm, 2 * D_RG), lambda i: (i, 0)),
                  pl.BlockSpec((CONV_W, D_RG), lambda i: (0, 0)), vec(D_RG),
                  pl.BlockSpec((D_RG, 2 * D_RG), lambda i: (0, 0)), vec(2 * D_RG), vec(D_RG), vec(D_RG)],
        out_specs=[pl.BlockSpec((tm, D_RG), lambda i: (i, 0))] * 3,
        out_shape=[jax.ShapeDtypeStruct((T, D_RG), BF16), jax.ShapeDtypeStruct((T, D_RG), F32),
                   jax.ShapeDtypeStruct((T, D_RG), F32)],
        scratch_shapes=[pltpu.VMEM((tm + 8, D_RG), F32), pltpu.VMEM((tm, D_RG), F32),
                        pltpu.VMEM((tm, D_RG), F32), pltpu.VMEM((1, D_RG), F32)],
        name="rg_fwd", compiler_params=_params("arbitrary"),
    )(p, cw, cb, wg, bg, lam, rg_g)


def _running_sum(x, down):
    r = lax.broadcasted_iota(jnp.int32, (CHUNK, CHUNK), 0)
    c = lax.broadcasted_iota(jnp.int32, (CHUNK, CHUNK), 1)
    tri = ((c <= r) if down else (c >= r)).astype(BF16)
    hi = x.astype(BF16)
    rest = x - hi.astype(F32)
    mid = rest.astype(BF16)
    lo = (rest - mid.astype(F32)).astype(BF16)
    return (_dot(tri, hi) + _dot(tri, mid)) + _dot(tri, lo)


def _hg_gates(hq, hf, lbraw_ref, valid):
    lb = _sigmoid(lbraw_ref[0:1, :] - lbraw_ref[1:2, :])
    sq = _sigmoid(hq)
    q = hq * sq
    sf = _sigmoid(hf)
    f = lb + (1.0 - lb) * sf
    lf = jnp.where(valid, jnp.log(f), 0.0)
    b = _running_sum(lf, True)
    return lb, sq, q, sf, f, b


def _hg_head(qh, kh, bh):
    blk = lax.broadcasted_iota(jnp.int32, (CHUNK, 1), 0) // SUB
    b_last = bh[CHUNK - 1:CHUNK, :]
    refs = [bh[SUB * s:SUB * s + 1, :] for s in range(N_SUB)]
    r_sel = refs[N_SUB - 1]
    for s in range(N_SUB - 2, -1, -1):
        r_sel = jnp.where(blk == s, refs[s], r_sel)
    eb = jnp.exp(bh)
    eq = jnp.exp(bh - r_sel)
    ekh = jnp.exp(b_last - bh)
    ek = [jnp.exp(jnp.minimum(refs[s] - bh, EXP_CLAMP)) for s in range(N_SUB)]
    qe = qh * eq
    q_hat = jnp.concatenate([jnp.where(blk == s, qe, 0.0) for s in range(N_SUB)], axis=1)
    k_til = jnp.concatenate([kh * ek[s] for s in range(N_SUB)], axis=1)
    return blk, b_last, eb, eq, ekh, ek, q_hat, k_til


def _causal():
    r = lax.broadcasted_iota(jnp.int32, (CHUNK, CHUNK), 0)
    c = lax.broadcasted_iota(jnp.int32, (CHUNK, CHUNK), 1)
    return r >= c


def _chunks_per_step(n_chunks):
    for c in (5, 4, 3, 2):
        if n_chunks % c == 0:
            return c
    return 1


def _hg_fwd(p, lbraw, hg_g):
    T = p.shape[0]
    n_chunks = T // CHUNK
    cps = _chunks_per_step(n_chunks)
    rows = cps * CHUNK

    def body(hq_ref, hf_ref, hi_ref, hg_ref, lb_ref, g_ref, y_ref, o_ref, st_all_ref, st):
        i = pl.program_id(0)

        @pl.when(i == 0)
        def _():
            st[...] = jnp.zeros_like(st)

        def chunk(j, carry):
            rs = pl.ds(pl.multiple_of(j * CHUNK, CHUNK), CHUNK)
            chunk_body(i * cps + j, hq_ref.at[rs, :], hf_ref.at[rs, :], hi_ref.at[rs, :], hg_ref.at[rs, :], lb_ref,
                       g_ref, y_ref.at[rs, :], o_ref.at[rs, :], st_all_ref.at[pl.ds(j, 1)], st)
            return carry

        lax.fori_loop(0, cps, chunk, 0, unroll=True)

    def chunk_body(n, hq_ref, hf_ref, hi_ref, hg_ref, lb_ref, g_ref, y_ref, o_ref, st_all_ref, st):
        valid = (n * CHUNK + lax.broadcasted_iota(jnp.int32, (CHUNK, 1), 0)) >= PAD
        hq, hf, v, hg = hq_ref[...], hf_ref[...], hi_ref[...], hg_ref[...]
        lb, sq, q, sf, f, b = _hg_gates(hq, hf, lb_ref, valid)
        k = 1.0 - f
        st_all_ref[0] = st[...]
        causal = _causal()
        v_t = v.T.astype(BF16)
        heads = [slice(h * HG_HEAD_DIM, (h + 1) * HG_HEAD_DIM) for h in range(HG_HEADS)]
        fac = []
        for sl in heads:
            qh, kh, bh = q[:, sl], k[:, sl], b[:, sl]
            _, b_last, eb, _, ekh, _, q_hat, k_til = _hg_head(qh, kh, bh)
            fac.append((jnp.exp(b_last), (qh * eb).astype(BF16), q_hat.astype(BF16), k_til.astype(BF16),
                        (kh * ekh).astype(BF16), v[:, sl].astype(BF16)))
        raw = []
        for sl, (_, q_til, q_hat, k_til, k_hat, _) in zip(heads, fac):
            st_h = st[sl, :]
            raw.append((_dot_nt(q_til, st_h.astype(BF16)), _dot_nt(q_hat, k_til), _dot(v_t[sl, :], k_hat), st_h))
        for sl, (e_last, _, _, _, _, vb), (inter, att, upd, st_h) in zip(heads, fac, raw):
            o = inter + _dot(jnp.where(causal, att, 0.0).astype(BF16), vb)
            st[sl, :] = st_h * e_last + upd
            o_ref[:, sl] = o
            hgh = hg[:, sl]
            y_ref[:, sl] = (o * _rms(o) * g_ref[...] * (hgh * _sigmoid(hgh))).astype(BF16)

    col = lambda j: pl.BlockSpec((rows, D_HG), lambda n: (n, j))
    return pl.pallas_call(
        body, grid=(n_chunks // cps,),
        in_specs=[col(2), col(3), col(4), col(5),
                  pl.BlockSpec((2, D_HG), lambda n: (0, 0)), pl.BlockSpec((1, HG_HEAD_DIM), lambda n: (0, 0))],
        out_specs=[pl.BlockSpec((rows, D_HG), lambda n: (n, 0)), pl.BlockSpec((rows, D_HG), lambda n: (n, 0)),
                   pl.BlockSpec((cps, D_HG, HG_HEAD_DIM), lambda n: (n, 0, 0))],
        out_shape=[jax.ShapeDtypeStruct((T, D_HG), BF16), jax.ShapeDtypeStruct((T, D_HG), F32),
                   jax.ShapeDtypeStruct((n_chunks, D_HG, HG_HEAD_DIM), F32)],
        scratch_shapes=[pltpu.VMEM((D_HG, HG_HEAD_DIM), F32)],
        name="hg_fwd", compiler_params=_params("arbitrary"),
    )(p, p, p, p, lbraw, hg_g)


def _ffn_fwd(h0, y_rg, y_hg, w_out, g2, w_gu, w_down, gf, target):
    T = h0.shape[0]
    tm = _row_tile(T, 320)
    n_steps = T // tm

    def body(h_ref, yr_ref, yh_ref, wo_ref, g2_ref, wgu_ref, wd_ref, gf_ref, t_hbm,
             h1_ref, v_ref, y_ref, gu_ref, act_ref, dh2_ref, dh2b_ref, loss_ref, gg_ref, tbuf, sems):
        i = pl.program_id(0)
        slot = _fetch_window(t_hbm, tbuf, sems, i, n_steps, tm)

        @pl.when(i == 0)
        def _():
            loss_ref[...] = jnp.zeros_like(loss_ref)
            gg_ref[...] = jnp.zeros_like(gg_ref)
            tbuf[0, 0:HEAD, :] = jnp.zeros((HEAD, D_MODEL), F32)

        y_ref[:, :D_RG] = yr_ref[...]
        y_ref[:, D_RG:] = yh_ref[...]
        h1 = h_ref[...] + _dot(y_ref[...], wo_ref[...])
        h1_ref[...] = h1
        v = (h1 * _rms(h1) * g2_ref[...]).astype(BF16)
        v_ref[...] = v

        gu = _dot(v, wgu_ref[...])
        gu_ref[...] = gu.astype(BF16)
        g = gu[:, :D_FF]
        act = (g * _sigmoid(g) * gu[:, D_FF:]).astype(BF16)
        act_ref[...] = act

        h2 = h1 + _dot(act, wd_ref[...])
        r = _rms(h2)
        n = h2 * r
        gf_ = gf_ref[...]
        row = i * tm + lax.broadcasted_iota(jnp.int32, (tm, 1), 0)
        err = jnp.where(row >= HEAD, n * gf_ - tbuf[slot], 0.0)
        loss_ref[...] += 0.5 * jnp.sum(jnp.mean(err * err, axis=-1, keepdims=True), axis=0, keepdims=True)
        dy = err * (1.0 / D_MODEL)
        gg_ref[...] += jnp.sum(dy * n, axis=0, keepdims=True)
        dh2 = _rms_bwd(dy * gf_, n, r)
        dh2_ref[...] = dh2
        dh2b_ref[...] = dh2.astype(BF16)

    row_spec = lambda n: pl.BlockSpec((tm, n), lambda i: (i, 0))
    vec = pl.BlockSpec((1, D_MODEL), lambda i: (0, 0))
    return pl.pallas_call(
        body, grid=(n_steps,),
        in_specs=[row_spec(D_MODEL), row_spec(D_RG), row_spec(D_HG), _resident((D_MODEL, D_MODEL)), vec,
                  _resident((D_MODEL, 2 * D_FF)), _resident((D_FF, D_MODEL)), vec,
                  pl.BlockSpec(memory_space=pl.ANY)],
        out_specs=[row_spec(D_MODEL), row_spec(D_MODEL), row_spec(D_MODEL), row_spec(2 * D_FF), row_spec(D_FF),
                   row_spec(D_MODEL), row_spec(D_MODEL), pl.BlockSpec((1, 1), lambda i: (0, 0)), vec],
        out_shape=[jax.ShapeDtypeStruct((T, D_MODEL), F32), jax.ShapeDtypeStruct((T, D_MODEL), BF16),
                   jax.ShapeDtypeStruct((T, D_MODEL), BF16), jax.ShapeDtypeStruct((T, 2 * D_FF), BF16),
                   jax.ShapeDtypeStruct((T, D_FF), BF16), jax.ShapeDtypeStruct((T, D_MODEL), F32),
                   jax.ShapeDtypeStruct((T, D_MODEL), BF16), jax.ShapeDtypeStruct((1, 1), F32),
                   jax.ShapeDtypeStruct((1, D_MODEL), F32)],
        scratch_shapes=[pltpu.VMEM((2, tm, D_MODEL), F32), pltpu.SemaphoreType.DMA((2,))],
        name="ffn_fwd", compiler_params=_params("arbitrary"),
    )(h0, y_rg, y_hg, w_out, g2, w_gu, w_down, gf, target)


def _resident(shape):
    return pl.BlockSpec(shape, lambda i: (0,) * len(shape), pipeline_mode=pl.Buffered(1))


def _ffn_bwd(dh2b, gu, w_down, w_gu, h1, g2, dh2, w_out):
    T = h1.shape[0]
    tm = _row_tile(T, 320)

    def body(d_ref, gu_ref, wd_ref, wgu_ref, h_ref, g_ref, d2_ref, wo_ref, dgu_ref, dh1_ref, dh1b_ref, dy_ref, gg_ref):
        i = pl.program_id(0)

        @pl.when(i == 0)
        def _():
            gg_ref[...] = jnp.zeros_like(gg_ref)

        dact = _dot_nt(d_ref[...], wd_ref[...]).astype(BF16)
        g = gu_ref[:, :D_FF]
        u = gu_ref[:, D_FF:]
        s = _sigmoid(g)
        dgu_ref[:, :D_FF] = dact * u * (s * (1.0 + g * (1.0 - s)))
        dgu_ref[:, D_FF:] = dact * (g * s)

        dv = _dot_nt(dgu_ref[...], wgu_ref[...])
        h1_ = h_ref[...]
        r = _rms(h1_)
        n = h1_ * r
        gg_ref[...] += jnp.sum(dv * n, axis=0, keepdims=True)
        dh1 = d2_ref[...] + _rms_bwd(dv * g_ref[...], n, r)
        dh1_ref[...] = dh1
        db = dh1.astype(BF16)
        dh1b_ref[...] = db
        dy_ref[...] = _dot_nt(db, wo_ref[...])

    row = lambda n: pl.BlockSpec((tm, n), lambda i: (i, 0))
    return pl.pallas_call(
        body, grid=(T // tm,),
        in_specs=[row(D_MODEL), row(2 * D_FF), _resident((D_FF, D_MODEL)), _resident((D_MODEL, 2 * D_FF)),
                  row(D_MODEL), pl.BlockSpec((1, D_MODEL), lambda i: (0, 0)), row(D_MODEL),
                  _resident((D_MODEL, D_MODEL))],
        out_specs=[row(2 * D_FF), row(D_MODEL), row(D_MODEL), row(D_MODEL),
                   pl.BlockSpec((1, D_MODEL), lambda i: (0, 0))],
        out_shape=[jax.ShapeDtypeStruct((T, 2 * D_FF), BF16), jax.ShapeDtypeStruct((T, D_MODEL), F32),
                   jax.ShapeDtypeStruct((T, D_MODEL), BF16), jax.ShapeDtypeStruct((T, D_MODEL), F32),
                   jax.ShapeDtypeStruct((1, D_MODEL), F32)],
        name="ffn_bwd", compiler_params=_params("arbitrary"),
    )(dh2b, gu, w_down, w_gu, h1, g2, dh2, w_out)


def _rg_bwd(p, xc_all, hs, dy, dp, cw, cb, wg, bg, lam, rg_g):
    T = p.shape[0]
    tm = _row_tile(T, 832)
    nt = T // tm
    hb = tm // 8
    unroll = _scan_unroll(hb)

    def body(xg_ref, xc_ref, h_ref, hh_ref, dy_ref, dp_in_ref, cw_ref, cb_ref, w_ref, bg_ref, lam_ref, g_ref,
             dp_ref, gcw_ref, gcb_ref, gw_ref, gbg_ref, glam_ref, gg_ref,
             dext, a_s, b_s, d_s, gacc, carry_d, carry_a):
        i = pl.program_id(0)
        t_idx = nt - 1 - i

        @pl.when(i == 0)
        def _():
            dext[tm:tm + 8, :] = jnp.zeros((8, D_RG), F32)
            carry_d[...] = jnp.zeros_like(carry_d)
            carry_a[...] = jnp.zeros_like(carry_a)
            gacc[...] = jnp.zeros_like(gacc)
            for ref in (gcw_ref, gcb_ref, gbg_ref, glam_ref, gg_ref, gw_ref):
                ref[...] = jnp.zeros_like(ref)

        first = t_idx == 0
        xc = xc_ref[...]
        lam_ = lam_ref[...]
        r, ig, sp, a, m, inv_m = _rg_gates(xc, w_ref, bg_ref, lam_)
        row = t_idx * tm + lax.broadcasted_iota(jnp.int32, (tm, 1), 0)
        valid = row >= PAD

        gr = xg_ref[:, D_RG:]
        g, dgelu = _gelu_parts(gr)
        h = h_ref[...]
        yy = g * h
        rr = _rms(yy)
        nn = yy * rr
        dy_ = dy_ref[...]
        gg_ref[...] += jnp.sum(dy_ * nn, axis=0, keepdims=True)
        dyy = _rms_bwd(dy_ * g_ref[...], nn, rr)
        dp_ref[:, D_RG:] = (dyy * h * dgelu).astype(BF16)

        a_s[...] = a
        b_s[...] = dyy * g
        rowi = lax.broadcasted_iota(jnp.int32, (8, D_RG), 0)

        def blk(jj, c):
            cd, ca = c
            for u in range(unroll):
                o = pl.multiple_of((hb - 1 - (jj * unroll + u)) * 8, 8)
                a_blk = a_s[pl.ds(o, 8), :]
                a_next = jnp.where(rowi == 7, ca, pltpu.roll(a_blk, 7, axis=0))
                A, B = _scan_block_bwd(a_next, b_s[pl.ds(o, 8), :], rowi)
                d = B + A * cd
                d_s[pl.ds(o, 8), :] = d
                cd, ca = d[0:1, :], a_blk[0:1, :]
            return cd, ca

        cd, ca = lax.fori_loop(0, hb // unroll, blk, (carry_d[...], carry_a[...]))
        carry_d[...] = cd
        carry_a[...] = ca
        delta = d_s[...]

        h_last_prev = jnp.where(first, 0.0, hh_ref[7:8, :])
        row0 = lax.broadcasted_iota(jnp.int32, (tm, 1), 0) == 0
        h_prev = jnp.where(row0, h_last_prev, pltpu.roll(h, 1, axis=0))
        dbx = jnp.where(valid, delta, 0.0)
        da = delta * h_prev
        di = dbx * m * xc
        dm = dbx * ig * xc
        dla = a * (da - dm * a * inv_m)
        dla = jnp.where(valid, dla, 0.0)
        glam_ref[...] += jnp.sum(dla * r, axis=0, keepdims=True) * (LRU_C / (1.0 + jnp.exp(lam_)))
        dr = (-LRU_C) * sp * dla
        dpre = jnp.concatenate([dr * r * (1.0 - r), di * ig * (1.0 - ig)], axis=1)
        gbg_ref[...] += jnp.sum(dpre, axis=0, keepdims=True)
        dpre_b = dpre.astype(BF16)
        gacc[...] += _dot_tn(xc.astype(BF16), dpre_b)
        dxc = dbx * m * ig + _dot_nt(dpre_b, w_ref[...])
        gcb_ref[...] += jnp.sum(dxc, axis=0, keepdims=True)
        dext[0:tm, :] = dxc
        xr = xg_ref[:, :D_RG]
        dxr = None
        for j in range(CONV_W):
            shifted = dext[3 - j:3 - j + tm, :]
            gcw_ref[j:j + 1, :] += jnp.sum(xr * shifted, axis=0, keepdims=True)
            tap = cw_ref[j:j + 1, :] * shifted
            dxr = tap if dxr is None else dxr + tap
        dp_ref[:, :D_RG] = dxr.astype(BF16)
        dext[tm:tm + 8, :] = dext[0:8, :]

        @pl.when(i == nt - 1)
        def _():
            fold = _head_fold()
            mask = _head_mask()
            fold_b = fold.astype(BF16)
            for k in range(2):
                blockdiag = jnp.where(mask, gacc[:, k * D_RG:(k + 1) * D_RG], 0.0)
                hi = blockdiag.astype(BF16)
                rest = blockdiag - hi.astype(F32)
                mid = rest.astype(BF16)
                lo = (rest - mid.astype(F32)).astype(BF16)
                gw_ref[k * D_RG:(k + 1) * D_RG, :] = (_dot(hi, fold_b) + _dot(mid, fold_b)) + _dot(lo, fold_b)

    vec = lambda n: pl.BlockSpec((1, n), lambda i: (0, 0))
    rev = lambda n: pl.BlockSpec((tm, n), lambda i: (nt - 1 - i, 0))
    halo = lambda n: pl.BlockSpec((8, n), lambda i: (jnp.maximum((nt - 1 - i) * hb - 1, 0), 0))
    return pl.pallas_call(
        body, grid=(nt,),
        in_specs=[rev(2 * D_RG), rev(D_RG), rev(D_RG), halo(D_RG), rev(D_RG), ANY,
                  pl.BlockSpec((CONV_W, D_RG), lambda i: (0, 0)), vec(D_RG),
                  pl.BlockSpec((D_RG, 2 * D_RG), lambda i: (0, 0)), vec(2 * D_RG), vec(D_RG), vec(D_RG)],
        out_specs=[rev(2 * D_RG), pl.BlockSpec((CONV_W, D_RG), lambda i: (0, 0)), vec(D_RG),
                   pl.BlockSpec((2 * D_RG, RG_HEAD_DIM), lambda i: (0, 0)), vec(2 * D_RG), vec(D_RG), vec(D_RG)],
        input_output_aliases={5: 0},
        out_shape=[jax.ShapeDtypeStruct((T, D_IN), BF16), jax.ShapeDtypeStruct((CONV_W, D_RG), F32),
                   jax.ShapeDtypeStruct((1, D_RG), F32), jax.ShapeDtypeStruct((2 * D_RG, RG_HEAD_DIM), F32),
                   jax.ShapeDtypeStruct((1, 2 * D_RG), F32), jax.ShapeDtypeStruct((1, D_RG), F32),
                   jax.ShapeDtypeStruct((1, D_RG), F32)],
        scratch_shapes=[pltpu.VMEM((tm + 8, D_RG), F32),
                        pltpu.VMEM((tm, D_RG), F32), pltpu.VMEM((tm, D_RG), F32), pltpu.VMEM((tm, D_RG), F32),
                        pltpu.VMEM((D_RG, 2 * D_RG), F32), pltpu.VMEM((1, D_RG), F32), pltpu.VMEM((1, D_RG), F32)],
        name="rg_bwd", compiler_params=_params("arbitrary"),
    )(p, xc_all, hs, hs, dy, dp, cw, cb, wg, bg, lam, rg_g)


def _hg_bwd(p, o_all, st_all, dy, lbraw, hg_g):
    T = p.shape[0]
    n_chunks = T // CHUNK
    cps = _chunks_per_step(n_chunks)
    rows = cps * CHUNK
    n_steps = n_chunks // cps

    def body(hq_ref, hf_ref, hi_ref, hg_ref, o_ref, st_ref, dy_ref, lb_ref, g_ref,
             dp_ref, glb_ref, gg_ref, dst):
        i = pl.program_id(0)

        @pl.when(i == 0)
        def _():
            dst[...] = jnp.zeros_like(dst)
            glb_ref[...] = jnp.zeros_like(glb_ref)
            gg_ref[...] = jnp.zeros_like(gg_ref)

        dp_ref[:, :2 * D_RG] = jnp.zeros((rows, 2 * D_RG), BF16)

        def chunk(jj, carry):
            j = cps - 1 - jj
            rs = pl.ds(pl.multiple_of(j * CHUNK, CHUNK), CHUNK)
            chunk_body((n_steps - 1 - i) * cps + j, hq_ref.at[rs, :], hf_ref.at[rs, :], hi_ref.at[rs, :],
                       hg_ref.at[rs, :], o_ref.at[rs, :], st_ref.at[pl.ds(j, 1)], dy_ref.at[rs, :], lb_ref, g_ref,
                       dp_ref.at[rs, pl.ds(2 * D_RG, 4 * D_HG)], glb_ref, gg_ref, dst)
            return carry

        lax.fori_loop(0, cps, chunk, 0, unroll=True)

    def chunk_body(n, hq_ref, hf_ref, hi_ref, hg_ref, o_ref, st_ref, dy_ref, lb_ref, g_ref,
                   dp_ref, glb_ref, gg_ref, dst):
        valid = (n * CHUNK + lax.broadcasted_iota(jnp.int32, (CHUNK, 1), 0)) >= PAD
        hq, hf, v, hg = hq_ref[...], hf_ref[...], hi_ref[...], hg_ref[...]
        lb, sq, q, sf, f, b = _hg_gates(hq, hf, lb_ref, valid)
        k = 1.0 - f
        causal = _causal()
        r_i = lax.broadcasted_iota(jnp.int32, (CHUNK, CHUNK), 0)
        c_i = lax.broadcasted_iota(jnp.int32, (CHUNK, CHUNK), 1)
        causal_t = r_i <= c_i
        is_last = lax.broadcasted_iota(jnp.int32, (CHUNK, 1), 0) == CHUNK - 1
        g_ = g_ref[...]
        db_parts, dq_parts, dk_parts = [], [], []
        gg = jnp.zeros((1, HG_HEAD_DIM), F32)
        heads = [slice(h * HG_HEAD_DIM, (h + 1) * HG_HEAD_DIM) for h in range(HG_HEADS)]

        do_parts = []
        for h, sl in enumerate(heads):
            o = o_ref[:, sl]
            ro = _rms(o)
            no = o * ro
            hgh = hg[:, sl]
            sg = _sigmoid(hgh)
            dyh = dy_ref[:, sl]
            dp_ref[:, 3 * D_HG + h * HG_HEAD_DIM:3 * D_HG + (h + 1) * HG_HEAD_DIM] = (
                dyh * no * g_ * sg * (1.0 + hgh * (1.0 - sg))).astype(BF16)
            dng = dyh * hgh * sg
            gg = gg + jnp.sum(dng * no, axis=0, keepdims=True)
            do_parts.append(_rms_bwd(dng * g_, no, ro))
        do_t = jnp.concatenate(do_parts, axis=1).T.astype(BF16)

        fac = []
        for sl, do in zip(heads, do_parts):
            qh, kh, bh = q[:, sl], k[:, sl], b[:, sl]
            blk, b_last, eb, eq, ekh, ek, q_hat, k_til = _hg_head(qh, kh, bh)
            fac.append(dict(qh=qh, kh=kh, blk=blk, e_last=jnp.exp(b_last), eb=eb, eq=eq, ekh=ekh, ek=ek,
                            q_til=qh * eb, k_hat=kh * ekh, qhb=q_hat.astype(BF16), ktb=k_til.astype(BF16),
                            vb=v[:, sl].astype(BF16), dob=do.astype(BF16)))

        first = []
        for sl, t in zip(heads, fac):
            st_h = st_ref[0, sl, :]
            dst_h = dst[sl, :]
            dstb = dst_h.astype(BF16)
            first.append(dict(
                att_t=_dot_nt(t["ktb"], t["qhb"]), datt=_dot_nt(t["dob"], t["vb"]),
                datt_t=_dot_nt(t["vb"], t["dob"]), dk_hat=_dot(t["vb"], dstb),
                dv=_dot_nt(t["k_hat"].astype(BF16), dstb), dq_til=_dot(t["dob"], st_h.astype(BF16)),
                state=t["e_last"] * jnp.sum(dst_h * st_h, axis=0, keepdims=True)))
            dst[sl, :] = dst_h * t["e_last"] + _dot(do_t[sl, :], t["q_til"].astype(BF16))

        for h, (t, m) in enumerate(zip(fac, first)):
            qh, kh, blk, eb, eq, ekh, ek = t["qh"], t["kh"], t["blk"], t["eb"], t["eq"], t["ekh"], t["ek"]
            q_til, k_hat, qhb, ktb, dob = t["q_til"], t["k_hat"], t["qhb"], t["ktb"], t["dob"]
            dk_hat, dq_til = m["dk_hat"], m["dq_til"]
            dv = m["dv"] + _dot(jnp.where(causal_t, m["att_t"], 0.0).astype(BF16), dob)
            dq_hat = _dot(jnp.where(causal, m["datt"], 0.0).astype(BF16), ktb)
            dk_til = _dot(jnp.where(causal_t, m["datt_t"], 0.0).astype(BF16), qhb)
            db_last = jnp.sum(dk_hat * k_hat, axis=0, keepdims=True) + m["state"]
            dq_sel = dq_hat[:, (N_SUB - 1) * HG_HEAD_DIM:]
            for s in range(N_SUB - 2, -1, -1):
                dq_sel = jnp.where(blk == s, dq_hat[:, s * HG_HEAD_DIM:(s + 1) * HG_HEAD_DIM], dq_sel)
            dq_a = dq_sel * eq
            dk_a = dk_til[:, :HG_HEAD_DIM] * ek[0]
            for s in range(1, N_SUB):
                dk_a = dk_a + dk_til[:, s * HG_HEAD_DIM:(s + 1) * HG_HEAD_DIM] * ek[s]
            db_att = qhb.astype(F32) * dq_hat - ktb.astype(F32) * dk_til
            db = dq_til * q_til - dk_hat * k_hat
            for s in range(N_SUB):
                db = db + db_att[:, s * HG_HEAD_DIM:(s + 1) * HG_HEAD_DIM]
            db_parts.append(jnp.where(is_last, db + db_last, db))
            dq_parts.append(dq_til * eb + dq_a)
            dk_parts.append(dk_hat * ekh + dk_a)
            dp_ref[:, 2 * D_HG + h * HG_HEAD_DIM:2 * D_HG + (h + 1) * HG_HEAD_DIM] = dv.astype(BF16)

        gg_ref[...] += gg
        db = jnp.concatenate(db_parts, axis=1)
        dq = jnp.concatenate(dq_parts, axis=1)
        dk = jnp.concatenate(dk_parts, axis=1)
        dlf = jnp.where(valid, _running_sum(db, False), 0.0)
        dp_ref[:, :D_HG] = (dq * sq * (1.0 + hq * (1.0 - sq))).astype(BF16)
        df = dlf / f - dk
        dlb = jnp.sum(df * (1.0 - sf), axis=0, keepdims=True) * lb * (1.0 - lb)
        glb_ref[0:1, :] += dlb
        glb_ref[1:2, :] += -dlb
        dp_ref[:, D_HG:2 * D_HG] = (df * (1.0 - lb) * sf * (1.0 - sf)).astype(BF16)

    rev = lambda j: pl.BlockSpec((rows, D_HG), lambda i: (n_steps - 1 - i, j))
    return pl.pallas_call(
        body, grid=(n_steps,),
        in_specs=[rev(2), rev(3), rev(4), rev(5), rev(0),
                  pl.BlockSpec((cps, D_HG, HG_HEAD_DIM), lambda i: (n_steps - 1 - i, 0, 0)), rev(1),
                  pl.BlockSpec((2, D_HG), lambda i: (0, 0)), pl.BlockSpec((1, HG_HEAD_DIM), lambda i: (0, 0))],
        out_specs=[pl.BlockSpec((rows, D_IN), lambda i: (n_steps - 1 - i, 0)),
                   pl.BlockSpec((2, D_HG), lambda i: (0, 0)), pl.BlockSpec((1, HG_HEAD_DIM), lambda i: (0, 0))],
        out_shape=[jax.ShapeDtypeStruct((T, D_IN), BF16), jax.ShapeDtypeStruct((2, D_HG), F32),
                   jax.ShapeDtypeStruct((1, HG_HEAD_DIM), F32)],
        scratch_shapes=[pltpu.VMEM((D_HG, HG_HEAD_DIM), F32)],
        name="hg_bwd", compiler_params=_params("arbitrary"),
    )(p, p, p, p, o_all, st_all, dy, lbraw, hg_g)


def _in_bwd(dp, w_in, h0, g1, dh1):
    T = h0.shape[0]
    tm = _row_tile(T, 832)
    n_steps = T // tm

    def body(dp_ref, w_ref, h_ref, g_ref, d1_ref, gx_hbm, gmeta_ref, gg_ref, buf, sems):
        i = pl.program_id(0)
        first, later = _window_copies(gx_hbm, buf, sems, tm)
        slot = i % 2

        @pl.when(i == 0)
        def _():
            gg_ref[...] = jnp.zeros_like(gg_ref)

        if n_steps > 2:
            @pl.when(i == 2)
            def _():
                first(False).wait()

            @pl.when(i > 2)
            def _():
                later(i - 2, slot, False).wait()

        du = _dot_nt(dp_ref[...], w_ref[...])
        h0_ = h_ref[...]
        r = _rms(h0_)
        n = h0_ * r
        gg_ref[...] += jnp.sum(du * n, axis=0, keepdims=True)
        dh0 = d1_ref[...] + _rms_bwd(du * g_ref[...], n, r)
        buf[slot] = dh0

        @pl.when(i == 0)
        def _():
            gmeta_ref[...] = dh0[PAD:HEAD, :]
            first(False).start()

        if n_steps > 1:
            @pl.when(i > 0)
            def _():
                later(i, slot, False).start()

        @pl.when(i == n_steps - 1)
        def _():
            if n_steps == 1:
                first(False).wait()
            else:
                if n_steps == 2:
                    first(False).wait()
                else:
                    later(i - 1, 1 - slot, False).wait()
                later(i, slot, False).wait()

    row = lambda n: pl.BlockSpec((tm, n), lambda i: (i, 0))
    return pl.pallas_call(
        body, grid=(n_steps,),
        in_specs=[row(D_IN), _resident((D_MODEL, D_IN)),
                  row(D_MODEL), pl.BlockSpec((1, D_MODEL), lambda i: (0, 0)), row(D_MODEL)],
        out_specs=[pl.BlockSpec(memory_space=pl.ANY), pl.BlockSpec((N_META, D_MODEL), lambda i: (0, 0)),
                   pl.BlockSpec((1, D_MODEL), lambda i: (0, 0))],
        out_shape=[jax.ShapeDtypeStruct((T - HEAD, D_MODEL), F32), jax.ShapeDtypeStruct((N_META, D_MODEL), F32),
                   jax.ShapeDtypeStruct((1, D_MODEL), F32)],
        scratch_shapes=[pltpu.VMEM((2, tm, D_MODEL), F32), pltpu.SemaphoreType.DMA((2,))],
        name="in_bwd", compiler_params=_params("arbitrary"),
    )(dp, w_in, h0, g1, dh1)


def _col_tile(cols, target):
    best = None
    for t in range(128, min(cols, target) + 1, 128):
        if cols % t == 0:
            best = t
    assert best is not None, cols
    return best


MXU_DIM = 256


def _mxu_tile(cols, target):
    best = None
    for t in range(MXU_DIM, min(cols, target) + 1, MXU_DIM):
        if cols % t == 0:
            best = t
    assert best is not None, cols
    return best


def _weight_grad(a, b, name):
    T, M = a.shape
    N = b.shape[1]
    tm = _col_tile(M, 1408)
    tn = _mxu_tile(N, 768 if tm <= 1024 else 512)

    def body(a_ref, b_ref, o_ref):
        o_ref[...] = _dot_tn(a_ref[...], b_ref[...])

    return pl.pallas_call(
        body, grid=(M // tm, N // tn),
        in_specs=[pl.BlockSpec((T, tm), lambda m, n: (0, m)), pl.BlockSpec((T, tn), lambda m, n: (0, n))],
        out_specs=pl.BlockSpec((tm, tn), lambda m, n: (m, n)),
        out_shape=jax.ShapeDtypeStruct((M, N), F32),
        name=name, compiler_params=_params("parallel", "parallel"),
    )(a, b)


def _local_step(x, meta, target, w_in_own, w_in, w_out, w_gu, w_down, small, chip, on_ffn_grads=None,
                on_mixer_grads=None):
    wg = _gate_weights(small["w_rgate"], small["w_igate"])
    bg = jnp.concatenate([small["b_rgate"], small["b_igate"]], axis=1)

    p, u, h0 = _in_proj_local(x, meta, small["mix_norm_g"], w_in_own, chip)
    p = _in_proj_rest(u, w_in, p, chip)
    y_rg, hs, xc = _rg_fwd(p, small["conv_w"], small["conv_b"], wg, bg, small["lru_lambda"], small["rg_norm_g"])
    y_hg, o_all, st_all = _hg_fwd(p, small["hg_lower_bound"], small["hg_norm_g"])
    h1, v, yb, gu, act, dh2, dh2b, loss, g_final = _ffn_fwd(
        h0, y_rg, y_hg, w_out, small["ffn_norm_g"], w_gu, w_down, small["final_norm_g"], target)

    g_w_down = _weight_grad(act, dh2b, "grad_w_down")
    dgu, dh1, dh1b, dy, g_ffn = _ffn_bwd(dh2b, gu, w_down, w_gu, h1, small["ffn_norm_g"], dh2, w_out)
    ffn_grads = {"w_gate_up": _weight_grad(v, dgu, "grad_w_gate_up"), "w_down": g_w_down,
                 "w_out": _weight_grad(yb, dh1b, "grad_w_out")}
    stages = on_ffn_grads(ffn_grads) if on_ffn_grads is not None else None
    dp, g_lb, g_hgn = _hg_bwd(p, o_all, st_all, dy, small["hg_lower_bound"], small["hg_norm_g"])
    early = late = None
    if stages is not None:
        chip_sums, send = stages
        sums = chip_sums()
        (dp, dy), sums = lax.optimization_barrier(((dp, dy), sums))
        early = send(sums)
    dp, g_cw, g_cb, g_wgate, g_bg, g_lam, g_rgn = _rg_bwd(
        p, xc, hs, dy, dp, small["conv_w"], small["conv_b"], wg, bg, small["lru_lambda"], small["rg_norm_g"])
    mixer_grads = {"w_in": _weight_grad(u, dp, "grad_w_in")}
    if on_mixer_grads is not None:
        chip_sums, send = on_mixer_grads(mixer_grads)
        sums = chip_sums()
        (dp, dh1), sums = lax.optimization_barrier(((dp, dh1), sums))
        late = send(sums)
    grad_x, g_meta, g_mix = _in_bwd(dp, w_in, h0, small["mix_norm_g"], dh1)

    grads = {
        "w_in": mixer_grads["w_in"], "w_out": ffn_grads["w_out"],
        "w_gate_up": ffn_grads["w_gate_up"], "w_down": ffn_grads["w_down"],
        "meta_tokens": g_meta, "mix_norm_g": g_mix, "conv_w": g_cw, "conv_b": g_cb, "w_gates": g_wgate,
        "b_rgate": g_bg[:, :D_RG], "b_igate": g_bg[:, D_RG:], "lru_lambda": g_lam, "rg_norm_g": g_rgn,
        "hg_lower_bound": g_lb, "hg_norm_g": g_hgn, "ffn_norm_g": g_ffn, "final_norm_g": g_final,
    }
    return loss, grad_x, grads, early, late


ANY = pl.BlockSpec(memory_space=pl.ANY)
HALF = D_MODEL // 2

BIG = {"w_in": (D_MODEL, D_IN // N_CHIPS, True), "w_gate_up": (D_MODEL, 2 * D_FF // N_CHIPS, True),
       "w_out": (D_MODEL // N_CHIPS, D_MODEL, False), "w_down": (D_FF // N_CHIPS, D_MODEL, False)}
BIG_NAMES = tuple(BIG)
N_BIG = len(BIG_NAMES)


def _full_shape(name):
    rows, cols, by_col = BIG[name]
    return (rows, cols * N_CHIPS) if by_col else (rows * N_CHIPS, cols)


def _place():
    return lax.axis_index("x"), lax.axis_index("y"), lax.axis_index("c")


def _chip_of(x, y, r):
    fx, fy = (r + 1) >> 1, (r + 1) & 1
    return (1 - x if fx else x), (1 - y if fy else y)


def _half_of(ref, by_col, half):
    start = pl.multiple_of(half * HALF, 128)
    return ref.at[pl.ds(start, HALF), :] if by_col else ref.at[:, pl.ds(start, HALF)]


def _shard_of(ref, name, chip):
    rows, cols, by_col = BIG[name]
    if by_col:
        return ref.at[:, pl.ds(pl.multiple_of(chip * cols, 128), cols)]
    return ref.at[pl.ds(pl.multiple_of(chip * rows, 16), rows), :]


def _shard_half_of(ref, name, chip, half):
    rows, cols, by_col = BIG[name]
    start = pl.multiple_of(half * HALF, 128)
    if by_col:
        return ref.at[pl.ds(start, HALF), pl.ds(pl.multiple_of(chip * cols, 128), cols)]
    return ref.at[pl.ds(pl.multiple_of(chip * rows, 16), rows), pl.ds(start, HALF)]


def _remote(src, dst, send_sems, recv_sems, k, dev):
    return pltpu.make_async_remote_copy(src_ref=src, dst_ref=dst, send_sem=send_sems.at[k], recv_sem=recv_sems.at[k],
                                        device_id=dev, device_id_type=MESH)


def _place_shards(w, small, chip):
    steps = 4
    ns = len(small)
    in_specs, out_specs = [], []
    for name in BIG_NAMES:
        rows, cols, by_col = BIG[name]
        tr = rows // steps
        in_specs.append(pl.BlockSpec((tr, cols), lambda i, s: (i, 0)))
        if by_col:
            out_specs.append(pl.BlockSpec((tr, cols), lambda i, s: (i, s[0])))
        else:
            out_specs.append(pl.BlockSpec((tr, cols), lambda i, s: (s[0] * steps + i, 0)))

    def body(s_ref, *refs):
        ins, small_in = refs[:N_BIG], refs[N_BIG:N_BIG + ns]
        outs, small_out = refs[N_BIG + ns:2 * N_BIG + ns], refs[2 * N_BIG + ns:2 * (N_BIG + ns)]
        send_sems, recv_sems, local_sems = refs[2 * (N_BIG + ns):]
        i = pl.program_id(0)
        x, y, c = _place()
        chip_ = 2 * x + y
        others = [_chip_of(x, y, r) for r in range(3)]

        def block(a, q):
            cols = small[a].shape[1]
            return small_out[a].at[:, pl.ds(pl.multiple_of(q * cols, 128), cols)]

        def local(a):
            return pltpu.make_async_copy(small_in[a], block(a, chip_), local_sems.at[a])

        def remote(a, r):
            qx, qy = others[r]
            return _remote(small_in[a], block(a, chip_), send_sems, recv_sems, 3 * a + r, (qx, qy, c))

        @pl.when(i == 0)
        def _():
            for a in range(ns):
                local(a).start()
                for r in range(3):
                    remote(a, r).start()

        for a in range(N_BIG):
            outs[a][...] = ins[a][...].astype(BF16)

        @pl.when(i == steps - 1)
        def _():
            for a in range(ns):
                for r, (qx, qy) in enumerate(others):
                    landed = block(a, 2 * qx + qy)
                    _remote(landed, landed, send_sems, recv_sems, 3 * a + r, (qx, qy, c)).wait_recv()
                for r in range(3):
                    remote(a, r).wait_send()
                local(a).wait()

    out = pl.pallas_call(
        body,
        grid_spec=pltpu.PrefetchScalarGridSpec(
            num_scalar_prefetch=1, grid=(steps,), in_specs=in_specs + [ANY] * ns, out_specs=out_specs + [ANY] * ns,
            scratch_shapes=[pltpu.SemaphoreType.DMA((3 * ns,)), pltpu.SemaphoreType.DMA((3 * ns,)),
                            pltpu.SemaphoreType.DMA((ns,))]),
        out_shape=([jax.ShapeDtypeStruct(_full_shape(name), BF16) for name in BIG_NAMES]
                   + [jax.ShapeDtypeStruct((s.shape[0], s.shape[1] * N_CHIPS), F32) for s in small]),
        name="place_shards", compiler_params=_params("arbitrary"),
    )(chip, *[w[name] for name in BIG_NAMES], *small)
    return dict(zip(BIG_NAMES, out[:N_BIG])), list(out[N_BIG:])


def _gather_weights(placed, small, names, label, collective_id):
    n, ns = len(names), len(small)
    hbm = pltpu.MemorySpace.HBM
    outs = [jax.new_ref(placed[nm], memory_space=hbm) for nm in names]
    small_in = [jax.new_ref(s, memory_space=hbm) for s in small]
    small_out = [jax.empty_ref(jax.ShapeDtypeStruct((s.shape[0], s.shape[1] * N_CHIPS), F32), memory_space=hbm)
                 for s in small]
    n_sems = 6 * n + 3 * ns

    @pl.kernel(mesh=plsc.ScalarSubcoreMesh(axis_name="seq", num_cores=1), name=label, out_type=(),
               scratch_types=(pltpu.SemaphoreType.DMA((n_sems,)), pltpu.SemaphoreType.DMA((n_sems,)),
                              pltpu.SemaphoreType.DMA((max(ns, 1),))),
               compiler_params=pltpu.CompilerParams(collective_id=collective_id))
    def launch(send_sems, recv_sems, local_sems):
        x, y, c = _place()
        chip = 2 * x + y
        sibling = (x, y, 1 - c)
        others = [_chip_of(x, y, r) for r in range(3)]
        _handshake([(qx, qy, c) for qx, qy in others] + [sibling])

        def small_block(a, q):
            cols = small[a].shape[1]
            return small_out[a].at[:, pl.ds(pl.multiple_of(q * cols, 128), cols)]

        local = [pltpu.make_async_copy(small_in[a], small_block(a, chip), local_sems.at[a]) for a in range(ns)]
        for cp in local:
            cp.start()

        sends = []
        for a, name in enumerate(names):
            mine = _shard_half_of(outs[a], name, chip, c)
            for r, (qx, qy) in enumerate(others):
                sends.append(_remote(mine, mine, send_sems, recv_sems, 6 * a + r, (qx, qy, c)))
        for a in range(ns):
            for r, (qx, qy) in enumerate(others):
                sends.append(_remote(small_in[a], small_block(a, chip), send_sems, recv_sems,
                                     6 * n + 3 * a + r, (qx, qy, c)))
        for cp in sends:
            cp.start()

        forwards = []
        for a, name in enumerate(names):
            for r, (qx, qy) in enumerate(others):
                landed = _shard_half_of(outs[a], name, 2 * qx + qy, c)
                _remote(landed, landed, send_sems, recv_sems, 6 * a + r, (qx, qy, c)).wait_recv()
                fwd = _remote(landed, landed, send_sems, recv_sems, 6 * a + 3 + r, sibling)
                fwd.start()
                forwards.append(fwd)
        for a in range(ns):
            for r, (qx, qy) in enumerate(others):
                landed = small_block(a, 2 * qx + qy)
                _remote(landed, landed, send_sems, recv_sems, 6 * n + 3 * a + r, (qx, qy, c)).wait_recv()
        for a, name in enumerate(names):
            for r, (qx, qy) in enumerate(others):
                landed = _shard_half_of(outs[a], name, 2 * qx + qy, 1 - c)
                _remote(landed, landed, send_sems, recv_sems, 6 * a + 3 + r, sibling).wait_recv()
        for cp in sends + forwards:
            cp.wait_send()
        for cp in local:
            cp.wait()

    launch()
    return {nm: ref[...] for nm, ref in zip(names, outs)}, [ref[...] for ref in small_out]


def _exchange_halves(grads, names, label, collective_id):
    n = len(names)
    sequencer = collective_id is not None

    def body(*refs):
        ins, outs = refs[:n], refs[n:2 * n]
        send_sems, recv_sems = refs[2 * n:]
        x, y, c = _place()
        if sequencer:
            _handshake([(x, y, 1 - c)])
        copies = []
        for a, name in enumerate(names):
            copies.append(_remote(_half_of(ins[a], BIG[name][2], 1 - c), outs[a], send_sems, recv_sems, a,
                                  (x, y, 1 - c)))
        for cp in copies:
            cp.start()
        for cp in copies:
            cp.wait()

    def half_shape(name):
        r, c_ = _full_shape(name)
        return (HALF, c_) if BIG[name][2] else (r, HALF)

    out_type = tuple(jax.ShapeDtypeStruct(half_shape(nm), F32) for nm in names)
    sems = (pltpu.SemaphoreType.DMA((n,)), pltpu.SemaphoreType.DMA((n,)))
    operands = [grads[nm] for nm in names]
    if sequencer:
        got = pl.kernel(
            body, mesh=plsc.ScalarSubcoreMesh(axis_name="seq", num_cores=1), name=label, out_type=out_type,
            scratch_types=sems, compiler_params=pltpu.CompilerParams(collective_id=collective_id),
        )(*operands)
    else:
        got = pl.pallas_call(
            body, in_specs=[ANY] * n, out_specs=[ANY] * n, out_shape=list(out_type), scratch_shapes=list(sems),
            name=label,
        )(*operands)
    return dict(zip(names, got))


def _chip_sum(grads, got, names, core, label):
    n = len(names)
    steps = 4
    g_specs, blks = [], []
    for name in names:
        rows, cols = got[name].shape
        tr = rows // steps
        if BIG[name][2]:
            g_specs.append(pl.BlockSpec((tr, cols), lambda i, s: (s[0] * steps + i, 0)))
        else:
            g_specs.append(pl.BlockSpec((tr, HALF), lambda i, s: (i, s[0])))
        blks.append(pl.BlockSpec((tr, cols), lambda i, s: (i, 0)))

    def body(s_ref, *refs):
        for a in range(n):
            t = refs[a][...] + refs[n + a][...]
            refs[2 * n + a][...] = t
            refs[3 * n + a][...] = t.astype(BF16)

    out = pl.pallas_call(
        body,
        grid_spec=pltpu.PrefetchScalarGridSpec(num_scalar_prefetch=1, grid=(steps,), in_specs=g_specs + blks,
                                               out_specs=blks + blks),
        out_shape=([jax.ShapeDtypeStruct(got[nm].shape, F32) for nm in names]
                   + [jax.ShapeDtypeStruct(got[nm].shape, BF16) for nm in names]),
        name=label, compiler_params=_params("parallel"),
    )(core, *[grads[nm] for nm in names], *[got[nm] for nm in names])
    return {nm: (out[a], out[n + a]) for a, nm in enumerate(names)}


def _piece_shape(name):
    rows, cols, by_col = BIG[name]
    return (HALF, cols) if by_col else (rows, HALF)


def _handshake(peers):
    barrier = pltpu.get_barrier_semaphore()
    for peer in peers:
        pl.semaphore_signal(barrier, inc=1, device_id=peer, device_id_type=MESH)
    pl.semaphore_wait(barrier, len(peers))


def _send_chip_sums(sums, names, label, collective_id):
    n = len(names)

    def body(*refs):
        ins, outs = refs[:n], refs[n:2 * n]
        send_sems, recv_sems = refs[2 * n:]
        x, y, c = _place()
        others = [_chip_of(x, y, r) for r in range(3)]
        _handshake([(qx, qy, c) for qx, qy in others])
        copies = []
        for a, name in enumerate(names):
            for r, (qx, qy) in enumerate(others):
                copies.append(_remote(_shard_of(ins[a], name, 2 * qx + qy), outs[a].at[r], send_sems, recv_sems,
                                      3 * a + r, (qx, qy, c)))
        for cp in copies:
            cp.start()
        for cp in copies:
            cp.wait()

    return pl.kernel(
        body, mesh=plsc.ScalarSubcoreMesh(axis_name="seq", num_cores=1), name=label,
        out_type=tuple(jax.ShapeDtypeStruct((3,) + _piece_shape(nm), BF16) for nm in names),
        scratch_types=(pltpu.SemaphoreType.DMA((3 * n,)), pltpu.SemaphoreType.DMA((3 * n,))),
        compiler_params=pltpu.CompilerParams(collective_id=collective_id),
    )(*[sums[nm] for nm in names])


def _total(parts, chip_core):
    steps = 2
    in_specs, out_specs, operands = [], [], []
    for name in BIG_NAMES:
        by_col = BIG[name][2]
        pr, pc = _piece_shape(name)
        tr = pr // steps
        if by_col:
            in_specs.append(pl.BlockSpec((tr, pc), lambda i, s: (i, s[0])))
            out_specs.append(pl.BlockSpec((tr, pc), lambda i, s: (s[1] * steps + i, 0)))
        else:
            in_specs.append(pl.BlockSpec((tr, pc), lambda i, s: (s[0] * steps + i, 0)))
            out_specs.append(pl.BlockSpec((tr, pc), lambda i, s: (i, s[1])))
        for r in range(3):
            in_specs.append(pl.BlockSpec((None, tr, pc), lambda i, s, r=r: (r, i, 0)))
        own, got = parts[name]
        operands += [own, got, got, got]

    def body(s_ref, *refs):
        for a in range(N_BIG):
            o_ref, a_ref, b_ref, c_ref = refs[4 * a:4 * a + 4]
            refs[4 * N_BIG + a][...] = (((o_ref[...] + a_ref[...].astype(F32)) + b_ref[...].astype(F32))
                                        + c_ref[...].astype(F32))

    totals = pl.pallas_call(
        body,
        grid_spec=pltpu.PrefetchScalarGridSpec(num_scalar_prefetch=1, grid=(steps,), in_specs=in_specs,
                                               out_specs=out_specs),
        out_shape=[jax.ShapeDtypeStruct(BIG[name][:2], F32) for name in BIG_NAMES],
        name="totals", compiler_params=_params("parallel"),
    )(chip_core, *operands)
    return dict(zip(BIG_NAMES, totals))


VEC_ROWS = 32
VEC_ROW = {"mix_norm_g": 0, "conv_b": 1, "b_rgate": 2, "b_igate": 3, "lru_lambda": 4, "rg_norm_g": 5,
           "hg_lower_bound": 6, "hg_norm_g": 8, "ffn_norm_g": 9, "final_norm_g": 10, "loss": 11,
           "conv_w": 12, "meta_tokens": 16}
N_DEV = 8


def _all_reduce_small(pieces, gates, totals):
    names = list(pieces)
    n_small = 10
    hv, hg = VEC_ROWS // 2, gates.shape[0] // 2

    def body(*refs):
        ins = refs[:len(names)]
        g_ref = refs[len(names)]
        vec_ref, gsum_ref = refs[len(names) + 1 + N_BIG:len(names) + 3 + N_BIG]
        big = refs[len(names) + 3 + N_BIG:len(names) + 3 + 2 * N_BIG]
        (mine_v, sib_v, sib_g, chip_v, chip_g, got_v, got_g, send_sems, recv_sems) = refs[len(names) + 3 + 2 * N_BIG:]
        x, y, c = _place()
        chip = 2 * x + y
        sibling = (x, y, 1 - c)
        share = []
        for a, name in enumerate(BIG_NAMES):
            half = _half_of(big[a], BIG[name][2], c)
            share.append(_remote(half, half, send_sems, recv_sems, n_small + a, sibling))
        for cp in share:
            cp.start()
        mine_v[...] = jnp.zeros_like(mine_v)
        for name, ref in zip(names, ins):
            nr, w = ref.shape
            mine_v[VEC_ROW[name]:VEC_ROW[name] + nr, 0:w] = ref[...]

        swap = [_remote(mine_v, sib_v, send_sems, recv_sems, 0, sibling),
                _remote(g_ref, sib_g, send_sems, recv_sems, 1, sibling)]
        for cp in swap:
            cp.start()
        for cp in swap:
            cp.wait()
        chip_v[...] = mine_v[...] + sib_v[...]
        chip_g[...] = g_ref[...] + sib_g[...]

        rows_v = pl.ds(pl.multiple_of(c * hv, 8), hv)
        rows_g = pl.ds(pl.multiple_of(c * hg, 8), hg)
        got_v[chip] = chip_v[rows_v, :]
        got_g[chip] = chip_g[rows_g, :]
        sends = []
        for r in range(3):
            qx, qy = _chip_of(x, y, r)
            sends.append(_remote(chip_v.at[rows_v, :], got_v.at[chip], send_sems, recv_sems, 2 + r, (qx, qy, c)))
            sends.append(_remote(chip_g.at[rows_g, :], got_g.at[chip], send_sems, recv_sems, 5 + r, (qx, qy, c)))
        for cp in sends:
            cp.start()
        for cp in sends:
            cp.wait()
        vec_ref[rows_v, :] = ((got_v[0] + got_v[1]) + got_v[2]) + got_v[3]
        gsum_ref[rows_g, :] = ((got_g[0] + got_g[1]) + got_g[2]) + got_g[3]

        back = [_remote(vec_ref.at[rows_v, :], vec_ref.at[rows_v, :], send_sems, recv_sems, 8, sibling),
                _remote(gsum_ref.at[rows_g, :], gsum_ref.at[rows_g, :], send_sems, recv_sems, 9, sibling)]
        for cp in back:
            cp.start()
        theirs_v = vec_ref.at[pl.ds(pl.multiple_of((1 - c) * hv, 8), hv), :]
        theirs_g = gsum_ref.at[pl.ds(pl.multiple_of((1 - c) * hg, 8), hg), :]
        _remote(theirs_v, theirs_v, send_sems, recv_sems, 8, sibling).wait_recv()
        _remote(theirs_g, theirs_g, send_sems, recv_sems, 9, sibling).wait_recv()
        for cp in back:
            cp.wait_send()
        for a, name in enumerate(BIG_NAMES):
            theirs = _half_of(big[a], BIG[name][2], 1 - c)
            _remote(theirs, theirs, send_sems, recv_sems, n_small + a, sibling).wait_recv()
        for cp in share:
            cp.wait_send()

    vmem = pl.BlockSpec(memory_space=pltpu.VMEM)
    n_sems = n_small + N_BIG
    out = pl.pallas_call(
        body, in_specs=[vmem] * (len(names) + 1) + [ANY] * N_BIG, out_specs=[vmem, vmem] + [ANY] * N_BIG,
        out_shape=([jax.ShapeDtypeStruct((VEC_ROWS, D_MODEL), F32), jax.ShapeDtypeStruct(gates.shape, F32)]
                   + [jax.ShapeDtypeStruct(BIG[n][:2], F32) for n in BIG_NAMES]),
        input_output_aliases={len(names) + 1 + a: 2 + a for a in range(N_BIG)},
        scratch_shapes=[pltpu.VMEM((VEC_ROWS, D_MODEL), F32), pltpu.VMEM((VEC_ROWS, D_MODEL), F32),
                        pltpu.VMEM(gates.shape, F32), pltpu.VMEM((VEC_ROWS, D_MODEL), F32),
                        pltpu.VMEM(gates.shape, F32), pltpu.VMEM((N_CHIPS, hv, D_MODEL), F32),
                        pltpu.VMEM((N_CHIPS, hg) + gates.shape[1:], F32),
                        pltpu.SemaphoreType.DMA((n_sems,)), pltpu.SemaphoreType.DMA((n_sems,))],
        name="all_reduce_small",
    )(*[pieces[n] for n in names], gates, *[totals[n] for n in BIG_NAMES])
    return out[0], out[1], dict(zip(BIG_NAMES, out[2:]))


def _adamw_math(w, g, m, v):
    m = ADAM_B1 * m + (1.0 - ADAM_B1) * g
    v = ADAM_B2 * v + (1.0 - ADAM_B2) * (g * g)
    m_hat = m / (1.0 - ADAM_B1 ** ADAM_STEP)
    v_hat = v / (1.0 - ADAM_B2 ** ADAM_STEP)
    delta = -ADAM_LR * (m_hat / (jnp.sqrt(v_hat) + ADAM_EPS) + ADAM_WD * w)
    return delta, m, v


def _adamw_big(w, g, m, v):
    steps = 8
    blks = []
    for name in BIG_NAMES:
        rows, cols, _ = BIG[name]
        blks.append(pl.BlockSpec((rows // steps, cols), lambda i: (i, 0)))

    def body(*refs):
        ins, outs = refs[:4 * N_BIG], refs[4 * N_BIG:]
        for a in range(N_BIG):
            w_ref, g_ref, m_ref, v_ref = (ins[k * N_BIG + a] for k in range(4))
            g = g_ref[...]
            d, nm, nv = _adamw_math(w_ref[...], g, m_ref[...], v_ref[...])
            outs[a][...] = g
            outs[N_BIG + a][...] = d
            outs[2 * N_BIG + a][...] = nm
            outs[3 * N_BIG + a][...] = nv

    shapes = [jax.ShapeDtypeStruct(BIG[name][:2], F32) for name in BIG_NAMES]
    out = pl.pallas_call(
        body, grid=(steps,), in_specs=blks * 4, out_specs=blks * 4, out_shape=shapes * 4,
        name="adamw_big", compiler_params=_params("parallel"),
    )(*[t[name] for t in (w, g, m, v) for name in BIG_NAMES])
    return {name: tuple(out[k * N_BIG + a] for k in range(4)) for a, name in enumerate(BIG_NAMES)}


SMALL = {"meta_tokens": (N_META, D_MODEL // N_CHIPS), "mix_norm_g": (1, D_MODEL), "conv_w": (CONV_W, D_RG // N_CHIPS),
         "conv_b": (1, D_RG), "w_rgate": (D_RG, RG_HEAD_DIM), "b_rgate": (1, D_RG), "w_igate": (D_RG, RG_HEAD_DIM),
         "b_igate": (1, D_RG), "lru_lambda": (1, D_RG), "rg_norm_g": (1, D_RG), "hg_lower_bound": (2, D_HG),
         "hg_norm_g": (1, HG_HEAD_DIM), "ffn_norm_g": (1, D_MODEL), "final_norm_g": (1, D_MODEL)}
SMALL_NAMES = tuple(SMALL)
SHARDED_SMALL = ("meta_tokens", "conv_w")


def _adamw_small(vec, gates, w, m, v):
    n = len(SMALL_NAMES)

    def body(*refs):
        vec_ref, gates_ref = refs[:2]
        w_refs, m_refs, v_refs = refs[2:2 + n], refs[2 + n:2 + 2 * n], refs[2 + 2 * n:2 + 3 * n]
        outs = refs[2 + 3 * n:]
        loss_ref = outs[0]
        x, y, _ = _place()
        chip = 2 * x + y
        loss_ref[...] = vec_ref[VEC_ROW["loss"]:VEC_ROW["loss"] + 1, 0:1]

        def update(k, g):
            g_ref, d_ref, nm_ref, nv_ref = outs[1 + 4 * k:5 + 4 * k]
            g_ref[...] = g
            d_ref[...], nm_ref[...], nv_ref[...] = _adamw_math(w_refs[k][...], g, m_refs[k][...], v_refs[k][...])

        for k, name in enumerate(SMALL_NAMES):
            nr, w_ = SMALL[name]
            if name == "w_rgate":
                update(k, gates_ref[0:D_RG, :])
            elif name == "w_igate":
                update(k, gates_ref[D_RG:2 * D_RG, :])
            elif name in SHARDED_SMALL:
                r0 = VEC_ROW[name]
                for q in range(N_CHIPS):
                    @pl.when(chip == q)
                    def _(k=k, r0=r0, nr=nr, w_=w_, q=q):
                        update(k, vec_ref[r0:r0 + nr, q * w_:(q + 1) * w_])
            else:
                r0 = VEC_ROW[name]
                update(k, vec_ref[r0:r0 + nr, 0:w_])

    vmem = pl.BlockSpec(memory_space=pltpu.VMEM)
    out_shape = [jax.ShapeDtypeStruct((1, 1), F32)]
    for name in SMALL_NAMES:
        out_shape += [jax.ShapeDtypeStruct(SMALL[name], F32)] * 4
    outs = pl.pallas_call(
        body, in_specs=[vmem] * (2 + 3 * n), out_specs=[vmem] * len(out_shape), out_shape=out_shape,
        name="adamw_small",
    )(vec, gates, *[w[k] for k in SMALL_NAMES], *[m[k] for k in SMALL_NAMES], *[v[k] for k in SMALL_NAMES])
    loss = outs[0]
    res = {name: tuple(outs[1 + 4 * k:5 + 4 * k]) for k, name in enumerate(SMALL_NAMES)}
    return loss, res


WEIGHT_NAMES = ("meta_tokens", "mix_norm_g", "w_in", "conv_w", "conv_b", "w_rgate", "b_rgate", "w_igate", "b_igate",
                "lru_lambda", "rg_norm_g", "hg_lower_bound", "hg_norm_g", "w_out", "ffn_norm_g", "w_gate_up", "w_down",
                "final_norm_g")


def _to_2d(name, a):
    if name in BIG:
        return a.reshape(BIG[name][:2])
    return a.reshape(SMALL[name])


def kernel(x, meta_tokens, mix_norm_g, w_in, conv_w, conv_b, w_rgate, b_rgate, w_igate, b_igate, lru_lambda, rg_norm_g, hg_lower_bound, hg_norm_g, w_out, ffn_norm_g, w_gate_up, w_down, final_norm_g, loss_target, m_meta_tokens, m_mix_norm_g, m_w_in, m_conv_w, m_conv_b, m_w_rgate, m_b_rgate, m_w_igate, m_b_igate, m_lru_lambda, m_rg_norm_g, m_hg_lower_bound, m_hg_norm_g, m_w_out, m_ffn_norm_g, m_w_gate_up, m_w_down, m_final_norm_g, v_meta_tokens, v_mix_norm_g, v_w_in, v_conv_w, v_conv_b, v_w_rgate, v_b_rgate, v_w_igate, v_b_igate, v_lru_lambda, v_rg_norm_g, v_hg_lower_bound, v_hg_norm_g, v_w_out, v_ffn_norm_g, v_w_gate_up, v_w_down, v_final_norm_g):
    w_raw = dict(zip(WEIGHT_NAMES, (meta_tokens, mix_norm_g, w_in, conv_w, conv_b, w_rgate, b_rgate, w_igate, b_igate,
                                    lru_lambda, rg_norm_g, hg_lower_bound, hg_norm_g, w_out, ffn_norm_g, w_gate_up,
                                    w_down, final_norm_g)))
    m_raw = dict(zip(WEIGHT_NAMES, (m_meta_tokens, m_mix_norm_g, m_w_in, m_conv_w, m_conv_b, m_w_rgate, m_b_rgate,
                                    m_w_igate, m_b_igate, m_lru_lambda, m_rg_norm_g, m_hg_lower_bound, m_hg_norm_g,
                                    m_w_out, m_ffn_norm_g, m_w_gate_up, m_w_down, m_final_norm_g)))
    v_raw = dict(zip(WEIGHT_NAMES, (v_meta_tokens, v_mix_norm_g, v_w_in, v_conv_w, v_conv_b, v_w_rgate, v_b_rgate,
                                    v_w_igate, v_b_igate, v_lru_lambda, v_rg_norm_g, v_hg_lower_bound, v_hg_norm_g,
                                    v_w_out, v_ffn_norm_g, v_w_gate_up, v_w_down, v_final_norm_g)))
    w = {k: _to_2d(k, a) for k, a in w_raw.items()}
    m = {k: _to_2d(k, a) for k, a in m_raw.items()}
    v = {k: _to_2d(k, a) for k, a in v_raw.items()}

    x_i, y_i, c_i = _place()
    core = jnp.reshape(c_i, (1,)).astype(jnp.int32)
    chip = jnp.reshape(2 * x_i + y_i, (1,)).astype(jnp.int32)
    chip_core = jnp.concatenate([chip, core])

    placed, (meta_full, cw_full) = _place_shards(w, [w["meta_tokens"], w["conv_w"]], chip)
    first, _ = _gather_weights(placed, [], ("w_in",), "gather_first", 1)
    rest, _ = _gather_weights(placed, [], ("w_out", "w_gate_up", "w_down"), "gather_rest", 2)
    full = {**first, **rest}

    seq = x.shape[1]
    small ={k: w[k] for k in SMALL_NAMES if k not in SHARDED_SMALL}
    small["conv_w"] = cw_full

    def reduce_to_chips(grads, names, tag, collective_ids):
        got = _exchange_halves(grads, names, "exchange_halves_" + tag, collective_ids[0])

        def chip_sums():
            return _chip_sum(grads, got, names, core, "chip_sum_" + tag)

        def send(sums):
            arrived = _send_chip_sums({n: sums[n][1] for n in names}, names, "send_chip_sums_" + tag,
                                      collective_ids[1])
            return {n: (sums[n][0], a) for n, a in zip(names, arrived)}

        return chip_sums, send

    ffn_names, mixer_names = ("w_gate_up", "w_down", "w_out"), ("w_in",)
    loss, grad_x, grads, parts, parts_mixer = _local_step(
        x.reshape(seq, D_MODEL), meta_full, loss_target.reshape(seq, D_MODEL),
        w["w_in"], full["w_in"], full["w_out"], full["w_gate_up"], full["w_down"], small, chip,
        on_ffn_grads=lambda g: reduce_to_chips(g, ffn_names, "ffn", (3, 4)),
        on_mixer_grads=lambda g: reduce_to_chips(g, mixer_names, "mixer", (None, 5)))
    parts.update(parts_mixer)
    totals = _total(parts, chip_core)
    pieces = {k: grads[k] for k in VEC_ROW if k != "loss"}
    pieces["loss"] = loss
    vec, gates, g_big = _all_reduce_small(pieces, grads["w_gates"], totals)
    loss_sum, res = _adamw_small(vec, gates, w, m, v)
    res.update(_adamw_big(w, g_big, m, v))

    out = [loss_sum.reshape(()), grad_x.reshape(1, seq, D_MODEL)]
    for j in range(4):
        out += [res[n][j].reshape(w_raw[n].shape) for n in WEIGHT_NAMES]
    return tuple(out)
```

```python
import math

import jax
import jax.numpy as jnp
from jax import lax
from jax.experimental import pallas as pl
from jax.experimental.pallas import tpu as pltpu
from jax.experimental.pallas import tpu_sc as plsc

F32 = jnp.float32
BF16 = jnp.bfloat16
MESH = pl.DeviceIdType.MESH

D_MODEL = 1024
D_RG = 512
RG_HEAD_DIM = 64
D_HG = 512
HG_HEAD_DIM = 128
HG_HEADS = 4
CHUNK = 64
SUB = 16
N_SUB = CHUNK // SUB
N_META = 16
PAD = CHUNK - N_META
D_IN = 3072
D_FF = 2816
CONV_W = 4
LRU_C = 8.0
EPS = 1e-6
EXP_CLAMP = 80.0
GELU_C = math.sqrt(2.0 / math.pi)
GELU_A = 0.044715
N_CHIPS = 4

ADAM_LR = 0.001
ADAM_B1 = 0.9
ADAM_B2 = 0.999
ADAM_EPS = 1e-08
ADAM_WD = 0.01
ADAM_STEP = 10

VMEM_LIMIT = 56 * 1024 * 1024


def _params(*sem):
    return pltpu.CompilerParams(dimension_semantics=sem, vmem_limit_bytes=VMEM_LIMIT)


def _row_tile(rows, target):
    best = None
    for t in range(16, min(rows, target) + 1, 16):
        if rows % t == 0:
            best = t
    assert best is not None, rows
    return best


def _sigmoid(x):
    return 0.5 * jnp.tanh(0.5 * x) + 0.5


def _dot(a, b):
    return jnp.dot(a, b, preferred_element_type=F32)


def _dot_nt(a, b):
    return lax.dot_general(a, b, (((1,), (1,)), ((), ())), preferred_element_type=F32)


def _dot_tn(a, b):
    return lax.dot_general(a, b, (((0,), (0,)), ((), ())), preferred_element_type=F32)


def _rms(x):
    return lax.rsqrt(jnp.mean(x * x, axis=-1, keepdims=True) + EPS)


def _rms_bwd(dn, n, r):
    return r * (dn - n * jnp.mean(dn * n, axis=-1, keepdims=True))


def _gelu_parts(x):
    t = jnp.tanh(GELU_C * (x + GELU_A * x * x * x))
    g = 0.5 * x * (1.0 + t)
    dg = 0.5 * (1.0 + t) + 0.5 * x * (1.0 - t * t) * GELU_C * (1.0 + 3.0 * GELU_A * x * x)
    return g, dg


def _softplus_neg(lam):
    e = jnp.exp(-jnp.abs(lam))
    w = 1.0 + e
    log1p = jnp.where(w == 1.0, e, jnp.log(w) * e / (w - 1.0))
    return jnp.maximum(-lam, 0.0) + log1p


def _head_mask():
    r = lax.broadcasted_iota(jnp.int32, (D_RG, D_RG), 0) // RG_HEAD_DIM
    c = lax.broadcasted_iota(jnp.int32, (D_RG, D_RG), 1) // RG_HEAD_DIM
    return r == c


def _head_fold():
    r = lax.broadcasted_iota(jnp.int32, (D_RG, RG_HEAD_DIM), 0) % RG_HEAD_DIM
    c = lax.broadcasted_iota(jnp.int32, (D_RG, RG_HEAD_DIM), 1)
    return (r == c).astype(F32)


def _gate_weights(w_r, w_i):
    def body(wr_ref, wi_ref, o_ref):
        fold = _head_fold()
        mask = _head_mask()
        for k, ref in enumerate((wr_ref, wi_ref)):
            full = _dot_nt(ref[...].astype(BF16), fold.astype(BF16))
            o_ref[:, k * D_RG:(k + 1) * D_RG] = jnp.where(mask, full, 0.0).astype(BF16)

    return pl.pallas_call(
        body, out_shape=jax.ShapeDtypeStruct((D_RG, 2 * D_RG), BF16), name="gate_weights",
    )(w_r, w_i)


HEAD = PAD + N_META


def _window_copies(seq_hbm, buf, sems, tm):
    def first(to_vmem):
        seq, vm = seq_hbm.at[pl.ds(0, tm - HEAD)], buf.at[0, pl.ds(HEAD, tm - HEAD)]
        return pltpu.make_async_copy(seq, vm, sems.at[0]) if to_vmem else pltpu.make_async_copy(vm, seq, sems.at[0])

    def later(j, slot, to_vmem):
        seq, vm = seq_hbm.at[pl.ds(pl.multiple_of(j * tm - HEAD, 8), tm)], buf.at[slot]
        if to_vmem:
            return pltpu.make_async_copy(seq, vm, sems.at[slot])
        return pltpu.make_async_copy(vm, seq, sems.at[slot])

    return first, later


def _fetch_window(seq_hbm, buf, sems, i, n_steps, tm):
    first, later = _window_copies(seq_hbm, buf, sems, tm)
    slot = i % 2

    @pl.when(i == 0)
    def _():
        first(True).start()

    if n_steps > 1:
        @pl.when(i + 1 < n_steps)
        def _():
            later(i + 1, 1 - slot, True).start()

    @pl.when(i == 0)
    def _():
        first(True).wait()

    if n_steps > 1:
        @pl.when(i > 0)
        def _():
            later(i, slot, True).wait()

    return slot


def _in_proj_local(x, meta, g1, w_own, chip):
    T = x.shape[0] + HEAD
    tm = _row_tile(T, 832)
    n_steps = T // tm
    cols = BIG["w_in"][1]

    def body(s_ref, x_hbm, meta_ref, g_ref, w_ref, p_ref, u_ref, h_ref, buf, sems, wb):
        i = pl.program_id(0)
        slot = _fetch_window(x_hbm, buf, sems, i, n_steps, tm)

        @pl.when(i == 0)
        def _():
            buf[0, 0:PAD, :] = jnp.zeros((PAD, D_MODEL), F32)
            buf[0, PAD:HEAD, :] = meta_ref[...]
            wb[...] = w_ref[...].astype(BF16)

        h = buf[slot]
        h_ref[...] = h
        u = (h * _rms(h) * g_ref[...]).astype(BF16)
        u_ref[...] = u
        p_ref[...] = _dot(u, wb[...])

    return pl.pallas_call(
        body,
        grid_spec=pltpu.PrefetchScalarGridSpec(
            num_scalar_prefetch=1, grid=(n_steps,),
            in_specs=[pl.BlockSpec(memory_space=pl.ANY),
                      pl.BlockSpec((N_META, D_MODEL), lambda i, s: (0, 0)),
                      pl.BlockSpec((1, D_MODEL), lambda i, s: (0, 0)),
                      pl.BlockSpec((D_MODEL, cols), lambda i, s: (0, 0))],
            out_specs=[pl.BlockSpec((tm, cols), lambda i, s: (i, s[0])),
                       pl.BlockSpec((tm, D_MODEL), lambda i, s: (i, 0)),
                       pl.BlockSpec((tm, D_MODEL), lambda i, s: (i, 0))],
            scratch_shapes=[pltpu.VMEM((2, tm, D_MODEL), F32), pltpu.SemaphoreType.DMA((2,)),
                            pltpu.VMEM((D_MODEL, cols), BF16)]),
        out_shape=[jax.ShapeDtypeStruct((T, D_IN), F32), jax.ShapeDtypeStruct((T, D_MODEL), BF16),
                   jax.ShapeDtypeStruct((T, D_MODEL), F32)],
        name="in_proj_local", compiler_params=_params("arbitrary"),
    )(chip, x, meta, g1, w_own)


def _in_proj_rest(u, w_in, p, chip):
    T = u.shape[0]
    tm = _row_tile(T, 4160)
    cols = BIG["w_in"][1]
    block = lambda j, s: (s[0] + 1 + j) % N_CHIPS

    def body(s_ref, u_ref, w_ref, p_in_ref, p_ref):
        p_ref[...] = _dot(u_ref[...], w_ref[...])

    return pl.pallas_call(
        body,
        grid_spec=pltpu.PrefetchScalarGridSpec(
            num_scalar_prefetch=1, grid=(N_CHIPS - 1, T // tm),
            in_specs=[pl.BlockSpec((tm, D_MODEL), lambda j, i, s: (i, 0)),
                      pl.BlockSpec((D_MODEL, cols), lambda j, i, s: (0, block(j, s))), ANY],
            out_specs=pl.BlockSpec((tm, cols), lambda j, i, s: (i, block(j, s)))),
        out_shape=jax.ShapeDtypeStruct((T, D_IN), F32),
        input_output_aliases={3: 0},
        name="in_proj_rest", compiler_params=_params("arbitrary", "arbitrary"),
    )(chip, u, w_in, p)


def _scan_block_fwd(A, B, rowi):
    for d in (1, 2, 4):
        a_sh = pltpu.roll(A, d, axis=0)
        b_sh = pltpu.roll(B, d, axis=0)
        m = rowi >= d
        B = jnp.where(m, A * b_sh + B, B)
        A = jnp.where(m, A * a_sh, A)
    return A, B


def _scan_block_bwd(A, B, rowi):
    for d in (1, 2, 4):
        a_sh = pltpu.roll(A, 8 - d, axis=0)
        b_sh = pltpu.roll(B, 8 - d, axis=0)
        m = rowi < 8 - d
        B = jnp.where(m, A * b_sh + B, B)
        A = jnp.where(m, A * a_sh, A)
    return A, B


def _rg_gates(xc, w_ref, bg_ref, lam):
    pre = _dot(xc.astype(BF16), w_ref[...]) + bg_ref[...]
    r = _sigmoid(pre[:, :D_RG])
    ig = _sigmoid(pre[:, D_RG:])
    sp = _softplus_neg(lam)
    la = -LRU_C * sp * r
    a = jnp.exp(la)
    th = jnp.tanh(la)
    u = 1.0 - th
    rc = pl.reciprocal(u, approx=True)
    rc = rc * (2.0 - u * rc)
    rc = rc * (2.0 - u * rc)
    m2 = -2.0 * th * rc
    inv_m = lax.rsqrt(jnp.maximum(m2, 1e-30))
    return r, ig, sp, a, m2 * inv_m, inv_m


def _conv(ext, cw_ref, cb_ref, tm):
    xc = cb_ref[...] + cw_ref[0:1, :] * ext[8 - 3:8 - 3 + tm, :]
    for j in range(1, CONV_W):
        xc = xc + cw_ref[j:j + 1, :] * ext[8 - 3 + j:8 - 3 + j + tm, :]
    return xc


def _scan_unroll(blocks):
    return 4 if blocks % 4 == 0 else 2 if blocks % 2 == 0 else 1


def _rg_fwd(p, cw, cb, wg, bg, lam, rg_g):
    T = p.shape[0]
    tm = _row_tile(T, 832)
    unroll = _scan_unroll(tm // 8)

    def body(xg_ref, cw_ref, cb_ref, w_ref, bg_ref, lam_ref, g_ref, y_ref, h_ref, xc_ref, ext, a_s, b_s, carry):
        i = pl.program_id(0)

        @pl.when(i == 0)
        def _():
            ext[0:8, :] = jnp.zeros((8, D_RG), F32)
            carry[...] = jnp.zeros((1, D_RG), F32)

        ext[8:8 + tm, :] = xg_ref[:, :D_RG]
        xc = _conv(ext, cw_ref, cb_ref, tm)
        xc_ref[...] = xc
        r, ig, sp, a, m, _ = _rg_gates(xc, w_ref, bg_ref, lam_ref[...])
        row = i * tm + lax.broadcasted_iota(jnp.int32, (tm, 1), 0)
        a_s[...] = a
        b_s[...] = jnp.where(row >= PAD, m * ig * xc, 0.0)
        rowi = lax.broadcasted_iota(jnp.int32, (8, D_RG), 0)

        def blk(j, c):
            for u in range(unroll):
                o = pl.multiple_of((j * unroll + u) * 8, 8)
                A, B = _scan_block_fwd(a_s[pl.ds(o, 8), :], b_s[pl.ds(o, 8), :], rowi)
                h = B + A * c
                h_ref[pl.ds(o, 8), :] = h
                c = h[7:8, :]
            return c

        carry[...] = lax.fori_loop(0, tm // (8 * unroll), blk, carry[...])
        ext[0:8, :] = ext[tm:tm + 8, :]
        g, _ = _gelu_parts(xg_ref[:, D_RG:])
        yy = g * h_ref[...]
        y_ref[...] = (yy * _rms(yy) * g_ref[...]).astype(BF16)

    vec = lambda n: pl.BlockSpec((1, n), lambda i: (0, 0))
    return pl.pallas_call(
        body, grid=(T // tm,),
        in_specs=[pl.BlockSpec((tm, 2 * D_RG), lambda i: (i, 0)),
                  pl.BlockSpec((CONV_W, D_RG), lambda i: (0, 0)), vec(D_RG),
                  pl.BlockSpec((D_RG, 2 * D_RG), lambda i: (0, 0)), vec(2 * D_RG), vec(D_RG), vec(D_RG)],
        out_specs=[pl.BlockSpec((tm, D_RG), lambda i: (i, 0))] * 3,
        out_shape=[jax.ShapeDtypeStruct((T, D_RG), BF16), jax.ShapeDtypeStruct((T, D_RG), F32),
                   jax.ShapeDtypeStruct((T, D_RG), F32)],
        scratch_shapes=[pltpu.VMEM((tm + 8, D_RG), F32), pltpu.VMEM((tm, D_RG), F32),
                        pltpu.VMEM((tm, D_RG), F32), pltpu.VMEM((1, D_RG), F32)],
        name="rg_fwd", compiler_params=_params("arbitrary"),
    )(p, cw, cb, wg, bg, lam, rg_g)


def _running_sum(x, down):
    r = lax.broadcasted_iota(jnp.int32, (CHUNK, CHUNK), 0)
    c = lax.broadcasted_iota(jnp.int32, (CHUNK, CHUNK), 1)
    tri = ((c <= r) if down else (c >= r)).astype(BF16)
    hi = x.astype(BF16)
    rest = x - hi.astype(F32)
    mid = rest.astype(BF16)
    lo = (rest - mid.astype(F32)).astype(BF16)
    return (_dot(tri, hi) + _dot(tri, mid)) + _dot(tri, lo)


def _hg_gates(hq, hf, lbraw_ref, valid):
    lb = _sigmoid(lbraw_ref[0:1, :] - lbraw_ref[1:2, :])
    sq = _sigmoid(hq)
    q = hq * sq
    sf = _sigmoid(hf)
    f = lb + (1.0 - lb) * sf
    lf = jnp.where(valid, jnp.log(f), 0.0)
    b = _running_sum(lf, True)
    return lb, sq, q, sf, f, b


def _hg_head(qh, kh, bh):
    blk = lax.broadcasted_iota(jnp.int32, (CHUNK, 1), 0) // SUB
    b_last = bh[CHUNK - 1:CHUNK, :]
    refs = [bh[SUB * s:SUB * s + 1, :] for s in range(N_SUB)]
    r_sel = refs[N_SUB - 1]
    for s in range(N_SUB - 2, -1, -1):
        r_sel = jnp.where(blk == s, refs[s], r_sel)
    eb = jnp.exp(bh)
    eq = jnp.exp(bh - r_sel)
    ekh = jnp.exp(b_last - bh)
    ek = [jnp.exp(jnp.minimum(refs[s] - bh, EXP_CLAMP)) for s in range(N_SUB)]
    qe = qh * eq
    q_hat = jnp.concatenate([jnp.where(blk == s, qe, 0.0) for s in range(N_SUB)], axis=1)
    k_til = jnp.concatenate([kh * ek[s] for s in range(N_SUB)], axis=1)
    return blk, b_last, eb, eq, ekh, ek, q_hat, k_til


def _causal():
    r = lax.broadcasted_iota(jnp.int32, (CHUNK, CHUNK), 0)
    c = lax.broadcasted_iota(jnp.int32, (CHUNK, CHUNK), 1)
    return r >= c


def _chunks_per_step(n_chunks):
    for c in (5, 4, 3, 2):
        if n_chunks % c == 0:
            return c
    return 1


def _hg_fwd(p, lbraw, hg_g):
    T = p.shape[0]
    n_chunks = T // CHUNK
    cps = _chunks_per_step(n_chunks)
    rows = cps * CHUNK

    def body(hq_ref, hf_ref, hi_ref, hg_ref, lb_ref, g_ref, y_ref, o_ref, st_all_ref, st):
        i = pl.program_id(0)

        @pl.when(i == 0)
        def _():
            st[...] = jnp.zeros_like(st)

        def chunk(j, carry):
            rs = pl.ds(pl.multiple_of(j * CHUNK, CHUNK), CHUNK)
            chunk_body(i * cps + j, hq_ref.at[rs, :], hf_ref.at[rs, :], hi_ref.at[rs, :], hg_ref.at[rs, :], lb_ref,
                       g_ref, y_ref.at[rs, :], o_ref.at[rs, :], st_all_ref.at[pl.ds(j, 1)], st)
            return carry

        lax.fori_loop(0, cps, chunk, 0, unroll=True)

    def chunk_body(n, hq_ref, hf_ref, hi_ref, hg_ref, lb_ref, g_ref, y_ref, o_ref, st_all_ref, st):
        valid = (n * CHUNK + lax.broadcasted_iota(jnp.int32, (CHUNK, 1), 0)) >= PAD
        hq, hf, v, hg = hq_ref[...], hf_ref[...], hi_ref[...], hg_ref[...]
        lb, sq, q, sf, f, b = _hg_gates(hq, hf, lb_ref, valid)
        k = 1.0 - f
        st_all_ref[0] = st[...]
        causal = _causal()
        v_t = v.T.astype(BF16)
        heads = [slice(h * HG_HEAD_DIM, (h + 1) * HG_HEAD_DIM) for h in range(HG_HEADS)]
        fac = []
        for sl in heads:
            qh, kh, bh = q[:, sl], k[:, sl], b[:, sl]
            _, b_last, eb, _, ekh, _, q_hat, k_til = _hg_head(qh, kh, bh)
            fac.append((jnp.exp(b_last), (qh * eb).astype(BF16), q_hat.astype(BF16), k_til.astype(BF16),
                        (kh * ekh).astype(BF16), v[:, sl].astype(BF16)))
        raw = []
        for sl, (_, q_til, q_hat, k_til, k_hat, _) in zip(heads, fac):
            st_h = st[sl, :]
            raw.append((_dot_nt(q_til, st_h.astype(BF16)), _dot_nt(q_hat, k_til), _dot(v_t[sl, :], k_hat), st_h))
        for sl, (e_last, _, _, _, _, vb), (inter, att, upd, st_h) in zip(heads, fac, raw):
            o = inter + _dot(jnp.where(causal, att, 0.0).astype(BF16), vb)
            st[sl, :] = st_h * e_last + upd
            o_ref[:, sl] = o
            hgh = hg[:, sl]
            y_ref[:, sl] = (o * _rms(o) * g_ref[...] * (hgh * _sigmoid(hgh))).astype(BF16)

    col = lambda j: pl.BlockSpec((rows, D_HG), lambda n: (n, j))
    return pl.pallas_call(
        body, grid=(n_chunks // cps,),
        in_specs=[col(2), col(3), col(4), col(5),
                  pl.BlockSpec((2, D_HG), lambda n: (0, 0)), pl.BlockSpec((1, HG_HEAD_DIM), lambda n: (0, 0))],
        out_specs=[pl.BlockSpec((rows, D_HG), lambda n: (n, 0)), pl.BlockSpec((rows, D_HG), lambda n: (n, 0)),
                   pl.BlockSpec((cps, D_HG, HG_HEAD_DIM), lambda n: (n, 0, 0))],
        out_shape=[jax.ShapeDtypeStruct((T, D_HG), BF16), jax.ShapeDtypeStruct((T, D_HG), F32),
                   jax.ShapeDtypeStruct((n_chunks, D_HG, HG_HEAD_DIM), F32)],
        scratch_shapes=[pltpu.VMEM((D_HG, HG_HEAD_DIM), F32)],
        name="hg_fwd", compiler_params=_params("arbitrary"),
    )(p, p, p, p, lbraw, hg_g)


def _ffn_fwd(h0, y_rg, y_hg, w_out, g2, w_gu, w_down, gf, target):
    T = h0.shape[0]
    tm = _row_tile(T, 320)
    n_steps = T // tm

    def body(h_ref, yr_ref, yh_ref, wo_ref, g2_ref, wgu_ref, wd_ref, gf_ref, t_hbm,
             h1_ref, v_ref, y_ref, gu_ref, act_ref, dh2_ref, dh2b_ref, loss_ref, gg_ref, tbuf, sems):
        i = pl.program_id(0)
        slot = _fetch_window(t_hbm, tbuf, sems, i, n_steps, tm)

        @pl.when(i == 0)
        def _():
            loss_ref[...] = jnp.zeros_like(loss_ref)
            gg_ref[...] = jnp.zeros_like(gg_ref)
            tbuf[0, 0:HEAD, :] = jnp.zeros((HEAD, D_MODEL), F32)

        y_ref[:, :D_RG] = yr_ref[...]
        y_ref[:, D_RG:] = yh_ref[...]
        h1 = h_ref[...] + _dot(y_ref[...], wo_ref[...])
        h1_ref[...] = h1
        v = (h1 * _rms(h1) * g2_ref[...]).astype(BF16)
        v_ref[...] = v

        gu = _dot(v, wgu_ref[...])
        gu_ref[...] = gu.astype(BF16)
        g = gu[:, :D_FF]
        act = (g * _sigmoid(g) * gu[:, D_FF:]).astype(BF16)
        act_ref[...] = act

        h2 = h1 + _dot(act, wd_ref[...])
        r = _rms(h2)
        n = h2 * r
        gf_ = gf_ref[...]
        row = i * tm + lax.broadcasted_iota(jnp.int32, (tm, 1), 0)
        err = jnp.where(row >= HEAD, n * gf_ - tbuf[slot], 0.0)
        loss_ref[...] += 0.5 * jnp.sum(jnp.mean(err * err, axis=-1, keepdims=True), axis=0, keepdims=True)
        dy = err * (1.0 / D_MODEL)
        gg_ref[...] += jnp.sum(dy * n, axis=0, keepdims=True)
        dh2 = _rms_bwd(dy * gf_, n, r)
        dh2_ref[...] = dh2
        dh2b_ref[...] = dh2.astype(BF16)

    row_spec = lambda n: pl.BlockSpec((tm, n), lambda i: (i, 0))
    vec = pl.BlockSpec((1, D_MODEL), lambda i: (0, 0))
    return pl.pallas_call(
        body, grid=(n_steps,),
        in_specs=[row_spec(D_MODEL), row_spec(D_RG), row_spec(D_HG), _resident((D_MODEL, D_MODEL)), vec,
                  _resident((D_MODEL, 2 * D_FF)), _resident((D_FF, D_MODEL)), vec,
                  pl.BlockSpec(memory_space=pl.ANY)],
        out_specs=[row_spec(D_MODEL), row_spec(D_MODEL), row_spec(D_MODEL), row_spec(2 * D_FF), row_spec(D_FF),
                   row_spec(D_MODEL), row_spec(D_MODEL), pl.BlockSpec((1, 1), lambda i: (0, 0)), vec],
        out_shape=[jax.ShapeDtypeStruct((T, D_MODEL), F32), jax.ShapeDtypeStruct((T, D_MODEL), BF16),
                   jax.ShapeDtypeStruct((T, D_MODEL), BF16), jax.ShapeDtypeStruct((T, 2 * D_FF), BF16),
                   jax.ShapeDtypeStruct((T, D_FF), BF16), jax.ShapeDtypeStruct((T, D_MODEL), F32),
                   jax.ShapeDtypeStruct((T, D_MODEL), BF16), jax.ShapeDtypeStruct((1, 1), F32),
                   jax.ShapeDtypeStruct((1, D_MODEL), F32)],
        scratch_shapes=[pltpu.VMEM((2, tm, D_MODEL), F32), pltpu.SemaphoreType.DMA((2,))],
        name="ffn_fwd", compiler_params=_params("arbitrary"),
    )(h0, y_rg, y_hg, w_out, g2, w_gu, w_down, gf, target)


def _resident(shape):
    return pl.BlockSpec(shape, lambda i: (0,) * len(shape), pipeline_mode=pl.Buffered(1))


def _ffn_bwd(dh2b, gu, w_down, w_gu, h1, g2, dh2, w_out):
    T = h1.shape[0]
    tm = _row_tile(T, 320)

    def body(d_ref, gu_ref, wd_ref, wgu_ref, h_ref, g_ref, d2_ref, wo_ref, dgu_ref, dh1_ref, dh1b_ref, dy_ref, gg_ref):
        i = pl.program_id(0)

        @pl.when(i == 0)
        def _():
            gg_ref[...] = jnp.zeros_like(gg_ref)

        dact = _dot_nt(d_ref[...], wd_ref[...]).astype(BF16)
        g = gu_ref[:, :D_FF]
        u = gu_ref[:, D_FF:]
        s = _sigmoid(g)
        dgu_ref[:, :D_FF] = dact * u * (s * (1.0 + g * (1.0 - s)))
        dgu_ref[:, D_FF:] = dact * (g * s)

        dv = _dot_nt(dgu_ref[...], wgu_ref[...])
        h1_ = h_ref[...]
        r = _rms(h1_)
        n = h1_ * r
        gg_ref[...] += jnp.sum(dv * n, axis=0, keepdims=True)
        dh1 = d2_ref[...] + _rms_bwd(dv * g_ref[...], n, r)
        dh1_ref[...] = dh1
        db = dh1.astype(BF16)
        dh1b_ref[...] = db
        dy_ref[...] = _dot_nt(db, wo_ref[...])

    row = lambda n: pl.BlockSpec((tm, n), lambda i: (i, 0))
    return pl.pallas_call(
        body, grid=(T // tm,),
        in_specs=[row(D_MODEL), row(2 * D_FF), _resident((D_FF, D_MODEL)), _resident((D_MODEL, 2 * D_FF)),
                  row(D_MODEL), pl.BlockSpec((1, D_MODEL), lambda i: (0, 0)), row(D_MODEL),
                  _resident((D_MODEL, D_MODEL))],
        out_specs=[row(2 * D_FF), row(D_MODEL), row(D_MODEL), row(D_MODEL),
                   pl.BlockSpec((1, D_MODEL), lambda i: (0, 0))],
        out_shape=[jax.ShapeDtypeStruct((T, 2 * D_FF), BF16), jax.ShapeDtypeStruct((T, D_MODEL), F32),
                   jax.ShapeDtypeStruct((T, D_MODEL), BF16), jax.ShapeDtypeStruct((T, D_MODEL), F32),
                   jax.ShapeDtypeStruct((1, D_MODEL), F32)],
        name="ffn_bwd", compiler_params=_params("arbitrary"),
    )(dh2b, gu, w_down, w_gu, h1, g2, dh2, w_out)


def _rg_bwd(p, xc_all, hs, dy, dp, cw, cb, wg, bg, lam, rg_g):
    T = p.shape[0]
    tm = _row_tile(T, 832)
    nt = T // tm
    hb = tm // 8
    unroll = _scan_unroll(hb)

    def body(xg_ref, xc_ref, h_ref, hh_ref, dy_ref, dp_in_ref, cw_ref, cb_ref, w_ref, bg_ref, lam_ref, g_ref,
             dp_ref, gcw_ref, gcb_ref, gw_ref, gbg_ref, glam_ref, gg_ref,
             dext, a_s, b_s, d_s, gacc, carry_d, carry_a):
        i = pl.program_id(0)
        t_idx = nt - 1 - i

        @pl.when(i == 0)
        def _():
            dext[tm:tm + 8, :] = jnp.zeros((8, D_RG), F32)
            carry_d[...] = jnp.zeros_like(carry_d)
            carry_a[...] = jnp.zeros_like(carry_a)
            gacc[...] = jnp.zeros_like(gacc)
            for ref in (gcw_ref, gcb_ref, gbg_ref, glam_ref, gg_ref, gw_ref):
                ref[...] = jnp.zeros_like(ref)

        first = t_idx == 0
        xc = xc_ref[...]
        lam_ = lam_ref[...]
        r, ig, sp, a, m, inv_m = _rg_gates(xc, w_ref, bg_ref, lam_)
        row = t_idx * tm + lax.broadcasted_iota(jnp.int32, (tm, 1), 0)
        valid = row >= PAD

        gr = xg_ref[:, D_RG:]
        g, dgelu = _gelu_parts(gr)
        h = h_ref[...]
        yy = g * h
        rr = _rms(yy)
        nn = yy * rr
        dy_ = dy_ref[...]
        gg_ref[...] += jnp.sum(dy_ * nn, axis=0, keepdims=True)
        dyy = _rms_bwd(dy_ * g_ref[...], nn, rr)
        dp_ref[:, D_RG:] = (dyy * h * dgelu).astype(BF16)

        a_s[...] = a
        b_s[...] = dyy * g
        rowi = lax.broadcasted_iota(jnp.int32, (8, D_RG), 0)

        def blk(jj, c):
            cd, ca = c
            for u in range(unroll):
                o = pl.multiple_of((hb - 1 - (jj * unroll + u)) * 8, 8)
                a_blk = a_s[pl.ds(o, 8), :]
                a_next = jnp.where(rowi == 7, ca, pltpu.roll(a_blk, 7, axis=0))
                A, B = _scan_block_bwd(a_next, b_s[pl.ds(o, 8), :], rowi)
                d = B + A * cd
                d_s[pl.ds(o, 8), :] = d
                cd, ca = d[0:1, :], a_blk[0:1, :]
            return cd, ca

        cd, ca = lax.fori_loop(0, hb // unroll, blk, (carry_d[...], carry_a[...]))
        carry_d[...] = cd
        carry_a[...] = ca
        delta = d_s[...]

        h_last_prev = jnp.where(first, 0.0, hh_ref[7:8, :])
        row0 = lax.broadcasted_iota(jnp.int32, (tm, 1), 0) == 0
        h_prev = jnp.where(row0, h_last_prev, pltpu.roll(h, 1, axis=0))
        dbx = jnp.where(valid, delta, 0.0)
        da = delta * h_prev
        di = dbx * m * xc
        dm = dbx * ig * xc
        dla = a * (da - dm * a * inv_m)
        dla = jnp.where(valid, dla, 0.0)
        glam_ref[...] += jnp.sum(dla * r, axis=0, keepdims=True) * (LRU_C / (1.0 + jnp.exp(lam_)))
        dr = (-LRU_C) * sp * dla
        dpre = jnp.concatenate([dr * r * (1.0 - r), di * ig * (1.0 - ig)], axis=1)
        gbg_ref[...] += jnp.sum(dpre, axis=0, keepdims=True)
        dpre_b = dpre.astype(BF16)
        gacc[...] += _dot_tn(xc.astype(BF16), dpre_b)
        dxc = dbx * m * ig + _dot_nt(dpre_b, w_ref[...])
        gcb_ref[...] += jnp.sum(dxc, axis=0, keepdims=True)
        dext[0:tm, :] = dxc
        xr = xg_ref[:, :D_RG]
        dxr = None
        for j in range(CONV_W):
            shifted = dext[3 - j:3 - j + tm, :]
            gcw_ref[j:j + 1, :] += jnp.sum(xr * shifted, axis=0, keepdims=True)
            tap = cw_ref[j:j + 1, :] * shifted
            dxr = tap if dxr is None else dxr + tap
        dp_ref[:, :D_RG] = dxr.astype(BF16)
        dext[tm:tm + 8, :] = dext[0:8, :]

        @pl.when(i == nt - 1)
        def _():
            fold = _head_fold()
            mask = _head_mask()
            fold_b = fold.astype(BF16)
            for k in range(2):
                blockdiag = jnp.where(mask, gacc[:, k * D_RG:(k + 1) * D_RG], 0.0)
                hi = blockdiag.astype(BF16)
                rest = blockdiag - hi.astype(F32)
                mid = rest.astype(BF16)
                lo = (rest - mid.astype(F32)).astype(BF16)
                gw_ref[k * D_RG:(k + 1) * D_RG, :] = (_dot(hi, fold_b) + _dot(mid, fold_b)) + _dot(lo, fold_b)

    vec = lambda n: pl.BlockSpec((1, n), lambda i: (0, 0))
    rev = lambda n: pl.BlockSpec((tm, n), lambda i: (nt - 1 - i, 0))
    halo = lambda n: pl.BlockSpec((8, n), lambda i: (jnp.maximum((nt - 1 - i) * hb - 1, 0), 0))
    return pl.pallas_call(
        body, grid=(nt,),
        in_specs=[rev(2 * D_RG), rev(D_RG), rev(D_RG), halo(D_RG), rev(D_RG), ANY,
                  pl.BlockSpec((CONV_W, D_RG), lambda i: (0, 0)), vec(D_RG),
                  pl.BlockSpec((D_RG, 2 * D_RG), lambda i: (0, 0)), vec(2 * D_RG), vec(D_RG), vec(D_RG)],
        out_specs=[rev(2 * D_RG), pl.BlockSpec((CONV_W, D_RG), lambda i: (0, 0)), vec(D_RG),
                   pl.BlockSpec((2 * D_RG, RG_HEAD_DIM), lambda i: (0, 0)), vec(2 * D_RG), vec(D_RG), vec(D_RG)],
        input_output_aliases={5: 0},
        out_shape=[jax.ShapeDtypeStruct((T, D_IN), BF16), jax.ShapeDtypeStruct((CONV_W, D_RG), F32),
                   jax.ShapeDtypeStruct((1, D_RG), F32), jax.ShapeDtypeStruct((2 * D_RG, RG_HEAD_DIM), F32),
                   jax.ShapeDtypeStruct((1, 2 * D_RG), F32), jax.ShapeDtypeStruct((1, D_RG), F32),
                   jax.ShapeDtypeStruct((1, D_RG), F32)],
        scratch_shapes=[pltpu.VMEM((tm + 8, D_RG), F32),
                        pltpu.VMEM((tm, D_RG), F32), pltpu.VMEM((tm, D_RG), F32), pltpu.VMEM((tm, D_RG), F32),
                        pltpu.VMEM((D_RG, 2 * D_RG), F32), pltpu.VMEM((1, D_RG), F32), pltpu.VMEM((1, D_RG), F32)],
        name="rg_bwd", compiler_params=_params("arbitrary"),
    )(p, xc_all, hs, hs, dy, dp, cw, cb, wg, bg, lam, rg_g)


def _hg_bwd(p, o_all, st_all, dy, lbraw, hg_g):
    T = p.shape[0]
    n_chunks = T // CHUNK
    cps = _chunks_per_step(n_chunks)
    rows = cps * CHUNK
    n_steps = n_chunks // cps

    def body(hq_ref, hf_ref, hi_ref, hg_ref, o_ref, st_ref, dy_ref, lb_ref, g_ref,
             dp_ref, glb_ref, gg_ref, dst):
        i = pl.program_id(0)

        @pl.when(i == 0)
        def _():
            dst[...] = jnp.zeros_like(dst)
            glb_ref[...] = jnp.zeros_like(glb_ref)
            gg_ref[...] = jnp.zeros_like(gg_ref)

        dp_ref[:, :2 * D_RG] = jnp.zeros((rows, 2 * D_RG), BF16)

        def chunk(jj, carry):
            j = cps - 1 - jj
            rs = pl.ds(pl.multiple_of(j * CHUNK, CHUNK), CHUNK)
            chunk_body((n_steps - 1 - i) * cps + j, hq_ref.at[rs, :], hf_ref.at[rs, :], hi_ref.at[rs, :],
                       hg_ref.at[rs, :], o_ref.at[rs, :], st_ref.at[pl.ds(j, 1)], dy_ref.at[rs, :], lb_ref, g_ref,
                       dp_ref.at[rs, pl.ds(2 * D_RG, 4 * D_HG)], glb_ref, gg_ref, dst)
            return carry

        lax.fori_loop(0, cps, chunk, 0, unroll=True)

    def chunk_body(n, hq_ref, hf_ref, hi_ref, hg_ref, o_ref, st_ref, dy_ref, lb_ref, g_ref,
                   dp_ref, glb_ref, gg_ref, dst):
        valid = (n * CHUNK + lax.broadcasted_iota(jnp.int32, (CHUNK, 1), 0)) >= PAD
        hq, hf, v, hg = hq_ref[...], hf_ref[...], hi_ref[...], hg_ref[...]
        lb, sq, q, sf, f, b = _hg_gates(hq, hf, lb_ref, valid)
        k = 1.0 - f
        causal = _causal()
        r_i = lax.broadcasted_iota(jnp.int32, (CHUNK, CHUNK), 0)
        c_i = lax.broadcasted_iota(jnp.int32, (CHUNK, CHUNK), 1)
        causal_t = r_i <= c_i
        is_last = lax.broadcasted_iota(jnp.int32, (CHUNK, 1), 0) == CHUNK - 1
        g_ = g_ref[...]
        db_parts, dq_parts, dk_parts = [], [], []
        gg = jnp.zeros((1, HG_HEAD_DIM), F32)
        heads = [slice(h * HG_HEAD_DIM, (h + 1) * HG_HEAD_DIM) for h in range(HG_HEADS)]

        do_parts = []
        for h, sl in enumerate(heads):
            o = o_ref[:, sl]
            ro = _rms(o)
            no = o * ro
            hgh = hg[:, sl]
            sg = _sigmoid(hgh)
            dyh = dy_ref[:, sl]
            dp_ref[:, 3 * D_HG + h * HG_HEAD_DIM:3 * D_HG + (h + 1) * HG_HEAD_DIM] = (
                dyh * no * g_ * sg * (1.0 + hgh * (1.0 - sg))).astype(BF16)
            dng = dyh * hgh * sg
            gg = gg + jnp.sum(dng * no, axis=0, keepdims=True)
            do_parts.append(_rms_bwd(dng * g_, no, ro))
        do_t = jnp.concatenate(do_parts, axis=1).T.astype(BF16)

        fac = []
        for sl, do in zip(heads, do_parts):
            qh, kh, bh = q[:, sl], k[:, sl], b[:, sl]
            blk, b_last, eb, eq, ekh, ek, q_hat, k_til = _hg_head(qh, kh, bh)
            fac.append(dict(qh=qh, kh=kh, blk=blk, e_last=jnp.exp(b_last), eb=eb, eq=eq, ekh=ekh, ek=ek,
                            q_til=qh * eb, k_hat=kh * ekh, qhb=q_hat.astype(BF16), ktb=k_til.astype(BF16),
                            vb=v[:, sl].astype(BF16), dob=do.astype(BF16)))

        first = []
        for sl, t in zip(heads, fac):
            st_h = st_ref[0, sl, :]
            dst_h = dst[sl, :]
            dstb = dst_h.astype(BF16)
            first.append(dict(
                att_t=_dot_nt(t["ktb"], t["qhb"]), datt=_dot_nt(t["dob"], t["vb"]),
                datt_t=_dot_nt(t["vb"], t["dob"]), dk_hat=_dot(t["vb"], dstb),
                dv=_dot_nt(t["k_hat"].astype(BF16), dstb), dq_til=_dot(t["dob"], st_h.astype(BF16)),
                state=t["e_last"] * jnp.sum(dst_h * st_h, axis=0, keepdims=True)))
            dst[sl, :] = dst_h * t["e_last"] + _dot(do_t[sl, :], t["q_til"].astype(BF16))

        for h, (t, m) in enumerate(zip(fac, first)):
            qh, kh, blk, eb, eq, ekh, ek = t["qh"], t["kh"], t["blk"], t["eb"], t["eq"], t["ekh"], t["ek"]
            q_til, k_hat, qhb, ktb, dob = t["q_til"], t["k_hat"], t["qhb"], t["ktb"], t["dob"]
            dk_hat, dq_til = m["dk_hat"], m["dq_til"]
            dv = m["dv"] + _dot(jnp.where(causal_t, m["att_t"], 0.0).astype(BF16), dob)
            dq_hat = _dot(jnp.where(causal, m["datt"], 0.0).astype(BF16), ktb)
            dk_til = _dot(jnp.where(causal_t, m["datt_t"], 0.0).astype(BF16), qhb)
            db_last = jnp.sum(dk_hat * k_hat, axis=0, keepdims=True) + m["state"]
            dq_sel = dq_hat[:, (N_SUB - 1) * HG_HEAD_DIM:]
            for s in range(N_SUB - 2, -1, -1):
                dq_sel = jnp.where(blk == s, dq_hat[:, s * HG_HEAD_DIM:(s + 1) * HG_HEAD_DIM], dq_sel)
            dq_a = dq_sel * eq
            dk_a = dk_til[:, :HG_HEAD_DIM] * ek[0]
            for s in range(1, N_SUB):
                dk_a = dk_a + dk_til[:, s * HG_HEAD_DIM:(s + 1) * HG_HEAD_DIM] * ek[s]
            db_att = qhb.astype(F32) * dq_hat - ktb.astype(F32) * dk_til
            db = dq_til * q_til - dk_hat * k_hat
            for s in range(N_SUB):
                db = db + db_att[:, s * HG_HEAD_DIM:(s + 1) * HG_HEAD_DIM]
            db_parts.append(jnp.where(is_last, db + db_last, db))
            dq_parts.append(dq_til * eb + dq_a)
            dk_parts.append(dk_hat * ekh + dk_a)
            dp_ref[:, 2 * D_HG + h * HG_HEAD_DIM:2 * D_HG + (h + 1) * HG_HEAD_DIM] = dv.astype(BF16)

        gg_ref[...] += gg
        db = jnp.concatenate(db_parts, axis=1)
        dq = jnp.concatenate(dq_parts, axis=1)
        dk = jnp.concatenate(dk_parts, axis=1)
        dlf = jnp.where(valid, _running_sum(db, False), 0.0)
        dp_ref[:, :D_HG] = (dq * sq * (1.0 + hq * (1.0 - sq))).astype(BF16)
        df = dlf / f - dk
        dlb = jnp.sum(df * (1.0 - sf), axis=0, keepdims=True) * lb * (1.0 - lb)
        glb_ref[0:1, :] += dlb
        glb_ref[1:2, :] += -dlb
        dp_ref[:, D_HG:2 * D_HG] = (df * (1.0 - lb) * sf * (1.0 - sf)).astype(BF16)

    rev = lambda j: pl.BlockSpec((rows, D_HG), lambda i: (n_steps - 1 - i, j))
    return pl.pallas_call(
        body, grid=(n_steps,),
        in_specs=[rev(2), rev(3), rev(4), rev(5), rev(0),
                  pl.BlockSpec((cps, D_HG, HG_HEAD_DIM), lambda i: (n_steps - 1 - i, 0, 0)), rev(1),
                  pl.BlockSpec((2, D_HG), lambda i: (0, 0)), pl.BlockSpec((1, HG_HEAD_DIM), lambda i: (0, 0))],
        out_specs=[pl.BlockSpec((rows, D_IN), lambda i: (n_steps - 1 - i, 0)),
                   pl.BlockSpec((2, D_HG), lambda i: (0, 0)), pl.BlockSpec((1, HG_HEAD_DIM), lambda i: (0, 0))],
        out_shape=[jax.ShapeDtypeStruct((T, D_IN), BF16), jax.ShapeDtypeStruct((2, D_HG), F32),
                   jax.ShapeDtypeStruct((1, HG_HEAD_DIM), F32)],
        scratch_shapes=[pltpu.VMEM((D_HG, HG_HEAD_DIM), F32)],
        name="hg_bwd", compiler_params=_params("arbitrary"),
    )(p, p, p, p, o_all, st_all, dy, lbraw, hg_g)


def _in_bwd(dp, w_in, h0, g1, dh1):
    T = h0.shape[0]
    tm = _row_tile(T, 832)
    n_steps = T // tm

    def body(dp_ref, w_ref, h_ref, g_ref, d1_ref, gx_hbm, gmeta_ref, gg_ref, buf, sems):
        i = pl.program_id(0)
        first, later = _window_copies(gx_hbm, buf, sems, tm)
        slot = i % 2

        @pl.when(i == 0)
        def _():
            gg_ref[...] = jnp.zeros_like(gg_ref)

        if n_steps > 2:
            @pl.when(i == 2)
            def _():
                first(False).wait()

            @pl.when(i > 2)
            def _():
                later(i - 2, slot, False).wait()

        du = _dot_nt(dp_ref[...], w_ref[...])
        h0_ = h_ref[...]
        r = _rms(h0_)
        n = h0_ * r
        gg_ref[...] += jnp.sum(du * n, axis=0, keepdims=True)
        dh0 = d1_ref[...] + _rms_bwd(du * g_ref[...], n, r)
        buf[slot] = dh0

        @pl.when(i == 0)
        def _():
            gmeta_ref[...] = dh0[PAD:HEAD, :]
            first(False).start()

        if n_steps > 1:
            @pl.when(i > 0)
            def _():
                later(i, slot, False).start()

        @pl.when(i == n_steps - 1)
        def _():
            if n_steps == 1:
                first(False).wait()
            else:
                if n_steps == 2:
                    first(False).wait()
                else:
                    later(i - 1, 1 - slot, False).wait()
                later(i, slot, False).wait()

    row = lambda n: pl.BlockSpec((tm, n), lambda i: (i, 0))
    return pl.pallas_call(
        body, grid=(n_steps,),
        in_specs=[row(D_IN), _resident((D_MODEL, D_IN)),
                  row(D_MODEL), pl.BlockSpec((1, D_MODEL), lambda i: (0, 0)), row(D_MODEL)],
        out_specs=[pl.BlockSpec(memory_space=pl.ANY), pl.BlockSpec((N_META, D_MODEL), lambda i: (0, 0)),
                   pl.BlockSpec((1, D_MODEL), lambda i: (0, 0))],
        out_shape=[jax.ShapeDtypeStruct((T - HEAD, D_MODEL), F32), jax.ShapeDtypeStruct((N_META, D_MODEL), F32),
                   jax.ShapeDtypeStruct((1, D_MODEL), F32)],
        scratch_shapes=[pltpu.VMEM((2, tm, D_MODEL), F32), pltpu.SemaphoreType.DMA((2,))],
        name="in_bwd", compiler_params=_params("arbitrary"),
    )(dp, w_in, h0, g1, dh1)


def _col_tile(cols, target):
    best = None
    for t in range(128, min(cols, target) + 1, 128):
        if cols % t == 0:
            best = t
    assert best is not None, cols
    return best


MXU_DIM = 256


def _mxu_tile(cols, target):
    best = None
    for t in range(MXU_DIM, min(cols, target) + 1, MXU_DIM):
        if cols % t == 0:
            best = t
    assert best is not None, cols
    return best


def _weight_grad(a, b, name):
    T, M = a.shape
    N = b.shape[1]
    tm = _col_tile(M, 1408)
    tn = _mxu_tile(N, 768 if tm <= 1024 else 512)

    def body(a_ref, b_ref, o_ref):
        o_ref[...] = _dot_tn(a_ref[...], b_ref[...])

    return pl.pallas_call(
        body, grid=(M // tm, N // tn),
        in_specs=[pl.BlockSpec((T, tm), lambda m, n: (0, m)), pl.BlockSpec((T, tn), lambda m, n: (0, n))],
        out_specs=pl.BlockSpec((tm, tn), lambda m, n: (m, n)),
        out_shape=jax.ShapeDtypeStruct((M, N), F32),
        name=name, compiler_params=_params("parallel", "parallel"),
    )(a, b)


def _weight_grad_pair(a1, b1, a2, b2, name):
    T, M = a1.shape
    assert a2.shape == a1.shape
    tn = 2 * MXU_DIM
    n1, n2 = b1.shape[1] // tn, b2.shape[1] // tn
    first = lambda i: jnp.minimum(i, n1 - 1)
    second = lambda i: jnp.maximum(i - n1, 0)

    def body(a1_ref, b1_ref, a2_ref, b2_ref, o1_ref, o2_ref):
        i = pl.program_id(0)

        @pl.when(i < n1)
        def _():
            o1_ref[...] = _dot_tn(a1_ref[...], b1_ref[...])

        @pl.when(i >= n1)
        def _():
            o2_ref[...] = _dot_tn(a2_ref[...], b2_ref[...])

    return pl.pallas_call(
        body, grid=(n1 + n2,),
        in_specs=[_resident((T, M)), pl.BlockSpec((T, tn), lambda i: (0, first(i))),
                  _resident((T, M)), pl.BlockSpec((T, tn), lambda i: (0, second(i)))],
        out_specs=[pl.BlockSpec((M, tn), lambda i: (0, first(i))), pl.BlockSpec((M, tn), lambda i: (0, second(i)))],
        out_shape=[jax.ShapeDtypeStruct((M, b1.shape[1]), F32), jax.ShapeDtypeStruct((M, b2.shape[1]), F32)],
        name=name, compiler_params=_params("arbitrary"),
    )(a1, b1, a2, b2)


def _local_step(x, meta, target, w_in_own, w_in, w_out, w_gu, w_down, small, chip, on_ffn_grads=None,
                on_mixer_grads=None):
    wg = _gate_weights(small["w_rgate"], small["w_igate"])
    bg = jnp.concatenate([small["b_rgate"], small["b_igate"]], axis=1)

    p, u, h0 = _in_proj_local(x, meta, small["mix_norm_g"], w_in_own, chip)
    p = _in_proj_rest(u, w_in, p, chip)
    y_rg, hs, xc = _rg_fwd(p, small["conv_w"], small["conv_b"], wg, bg, small["lru_lambda"], small["rg_norm_g"])
    y_hg, o_all, st_all = _hg_fwd(p, small["hg_lower_bound"], small["hg_norm_g"])
    h1, v, yb, gu, act, dh2, dh2b, loss, g_final = _ffn_fwd(
        h0, y_rg, y_hg, w_out, small["ffn_norm_g"], w_gu, w_down, small["final_norm_g"], target)

    g_w_down = _weight_grad(act, dh2b, "grad_w_down")
    dgu, dh1, dh1b, dy, g_ffn = _ffn_bwd(dh2b, gu, w_down, w_gu, h1, small["ffn_norm_g"], dh2, w_out)
    g_w_gate_up, g_w_out = _weight_grad_pair(v, dgu, yb, dh1b, "grad_w_gate_up_w_out")
    ffn_grads = {"w_gate_up": g_w_gate_up, "w_down": g_w_down, "w_out": g_w_out}
    stages = on_ffn_grads(ffn_grads) if on_ffn_grads is not None else None
    dp, g_lb, g_hgn = _hg_bwd(p, o_all, st_all, dy, small["hg_lower_bound"], small["hg_norm_g"])
    early = late = None
    if stages is not None:
        chip_sums, send = stages
        sums = chip_sums()
        (dp, dy), sums = lax.optimization_barrier(((dp, dy), sums))
        early = send(sums)
    dp, g_cw, g_cb, g_wgate, g_bg, g_lam, g_rgn = _rg_bwd(
        p, xc, hs, dy, dp, small["conv_w"], small["conv_b"], wg, bg, small["lru_lambda"], small["rg_norm_g"])
    mixer_grads = {"w_in": _weight_grad(u, dp, "grad_w_in")}
    if on_mixer_grads is not None:
        chip_sums, send = on_mixer_grads(mixer_grads)
        sums = chip_sums()
        (dp, dh1), sums = lax.optimization_barrier(((dp, dh1), sums))
        late = send(sums)
    grad_x, g_meta, g_mix = _in_bwd(dp, w_in, h0, small["mix_norm_g"], dh1)

    grads = {
        "w_in": mixer_grads["w_in"], "w_out": ffn_grads["w_out"],
        "w_gate_up": ffn_grads["w_gate_up"], "w_down": ffn_grads["w_down"],
        "meta_tokens": g_meta, "mix_norm_g": g_mix, "conv_w": g_cw, "conv_b": g_cb, "w_gates": g_wgate,
        "b_rgate": g_bg[:, :D_RG], "b_igate": g_bg[:, D_RG:], "lru_lambda": g_lam, "rg_norm_g": g_rgn,
        "hg_lower_bound": g_lb, "hg_norm_g": g_hgn, "ffn_norm_g": g_ffn, "final_norm_g": g_final,
    }
    return loss, grad_x, grads, early, late


ANY = pl.BlockSpec(memory_space=pl.ANY)
HALF = D_MODEL // 2

BIG = {"w_in": (D_MODEL, D_IN // N_CHIPS, True), "w_gate_up": (D_MODEL, 2 * D_FF // N_CHIPS, True),
       "w_out": (D_MODEL // N_CHIPS, D_MODEL, False), "w_down": (D_FF // N_CHIPS, D_MODEL, False)}
BIG_NAMES = tuple(BIG)
N_BIG = len(BIG_NAMES)


def _full_shape(name):
    rows, cols, by_col = BIG[name]
    return (rows, cols * N_CHIPS) if by_col else (rows * N_CHIPS, cols)


def _place():
    return lax.axis_index("x"), lax.axis_index("y"), lax.axis_index("c")


def _chip_of(x, y, r):
    fx, fy = (r + 1) >> 1, (r + 1) & 1
    return (1 - x if fx else x), (1 - y if fy else y)


def _half_of(ref, by_col, half):
    start = pl.multiple_of(half * HALF, 128)
    return ref.at[pl.ds(start, HALF), :] if by_col else ref.at[:, pl.ds(start, HALF)]


def _shard_of(ref, name, chip):
    rows, cols, by_col = BIG[name]
    if by_col:
        return ref.at[:, pl.ds(pl.multiple_of(chip * cols, 128), cols)]
    return ref.at[pl.ds(pl.multiple_of(chip * rows, 16), rows), :]


def _shard_half_of(ref, name, chip, half):
    rows, cols, by_col = BIG[name]
    start = pl.multiple_of(half * HALF, 128)
    if by_col:
        return ref.at[pl.ds(start, HALF), pl.ds(pl.multiple_of(chip * cols, 128), cols)]
    return ref.at[pl.ds(pl.multiple_of(chip * rows, 16), rows), pl.ds(start, HALF)]


def _remote(src, dst, send_sems, recv_sems, k, dev):
    return pltpu.make_async_remote_copy(src_ref=src, dst_ref=dst, send_sem=send_sems.at[k], recv_sem=recv_sems.at[k],
                                        device_id=dev, device_id_type=MESH)


def _place_shards(w, small, chip):
    steps = 4
    ns = len(small)
    in_specs, out_specs = [], []
    for name in BIG_NAMES:
        rows, cols, by_col = BIG[name]
        tr = rows // steps
        in_specs.append(pl.BlockSpec((tr, cols), lambda i, s: (i, 0)))
        if by_col:
            out_specs.append(pl.BlockSpec((tr, cols), lambda i, s: (i, s[0])))
        else:
            out_specs.append(pl.BlockSpec((tr, cols), lambda i, s: (s[0] * steps + i, 0)))

    def body(s_ref, *refs):
        ins, small_in = refs[:N_BIG], refs[N_BIG:N_BIG + ns]
        outs, small_out = refs[N_BIG + ns:2 * N_BIG + ns], refs[2 * N_BIG + ns:2 * (N_BIG + ns)]
        send_sems, recv_sems, local_sems = refs[2 * (N_BIG + ns):]
        i = pl.program_id(0)
        x, y, c = _place()
        chip_ = 2 * x + y
        others = [_chip_of(x, y, r) for r in range(3)]

        def block(a, q):
            cols = small[a].shape[1]
            return small_out[a].at[:, pl.ds(pl.multiple_of(q * cols, 128), cols)]

        def local(a):
            return pltpu.make_async_copy(small_in[a], block(a, chip_), local_sems.at[a])

        def remote(a, r):
            qx, qy = others[r]
            return _remote(small_in[a], block(a, chip_), send_sems, recv_sems, 3 * a + r, (qx, qy, c))

        @pl.when(i == 0)
        def _():
            for a in range(ns):
                local(a).start()
                for r in range(3):
                    remote(a, r).start()

        for a in range(N_BIG):
            outs[a][...] = ins[a][...].astype(BF16)

        @pl.when(i == steps - 1)
        def _():
            for a in range(ns):
                for r, (qx, qy) in enumerate(others):
                    landed = block(a, 2 * qx + qy)
                    _remote(landed, landed, send_sems, recv_sems, 3 * a + r, (qx, qy, c)).wait_recv()
                for r in range(3):
                    remote(a, r).wait_send()
                local(a).wait()

    out = pl.pallas_call(
        body,
        grid_spec=pltpu.PrefetchScalarGridSpec(
            num_scalar_prefetch=1, grid=(steps,), in_specs=in_specs + [ANY] * ns, out_specs=out_specs + [ANY] * ns,
            scratch_shapes=[pltpu.SemaphoreType.DMA((3 * ns,)), pltpu.SemaphoreType.DMA((3 * ns,)),
                            pltpu.SemaphoreType.DMA((ns,))]),
        out_shape=([jax.ShapeDtypeStruct(_full_shape(name), BF16) for name in BIG_NAMES]
                   + [jax.ShapeDtypeStruct((s.shape[0], s.shape[1] * N_CHIPS), F32) for s in small]),
        name="place_shards", compiler_params=_params("arbitrary"),
    )(chip, *[w[name] for name in BIG_NAMES], *small)
    return dict(zip(BIG_NAMES, out[:N_BIG])), list(out[N_BIG:])


def _gather_weights(placed, small, names, label, collective_id):
    n, ns = len(names), len(small)
    hbm = pltpu.MemorySpace.HBM
    outs = [jax.new_ref(placed[nm], memory_space=hbm) for nm in names]
    small_in = [jax.new_ref(s, memory_space=hbm) for s in small]
    small_out = [jax.empty_ref(jax.ShapeDtypeStruct((s.shape[0], s.shape[1] * N_CHIPS), F32), memory_space=hbm)
                 for s in small]
    n_sems = 6 * n + 3 * ns

    @pl.kernel(mesh=plsc.ScalarSubcoreMesh(axis_name="seq", num_cores=1), name=label, out_type=(),
               scratch_types=(pltpu.SemaphoreType.DMA((n_sems,)), pltpu.SemaphoreType.DMA((n_sems,)),
                              pltpu.SemaphoreType.DMA((max(ns, 1),))),
               compiler_params=pltpu.CompilerParams(collective_id=collective_id))
    def launch(send_sems, recv_sems, local_sems):
        x, y, c = _place()
        chip = 2 * x + y
        sibling = (x, y, 1 - c)
        others = [_chip_of(x, y, r) for r in range(3)]
        _handshake([(qx, qy, c) for qx, qy in others] + [sibling])

        def small_block(a, q):
            cols = small[a].shape[1]
            return small_out[a].at[:, pl.ds(pl.multiple_of(q * cols, 128), cols)]

        local = [pltpu.make_async_copy(small_in[a], small_block(a, chip), local_sems.at[a]) for a in range(ns)]
        for cp in local:
            cp.start()

        sends = []
        for a, name in enumerate(names):
            mine = _shard_half_of(outs[a], name, chip, c)
            for r, (qx, qy) in enumerate(others):
                sends.append(_remote(mine, mine, send_sems, recv_sems, 6 * a + r, (qx, qy, c)))
        for a in range(ns):
            for r, (qx, qy) in enumerate(others):
                sends.append(_remote(small_in[a], small_block(a, chip), send_sems, recv_sems,
                                     6 * n + 3 * a + r, (qx, qy, c)))
        for cp in sends:
            cp.start()

        forwards = []
        for a, name in enumerate(names):
            for r, (qx, qy) in enumerate(others):
                landed = _shard_half_of(outs[a], name, 2 * qx + qy, c)
                _remote(landed, landed, send_sems, recv_sems, 6 * a + r, (qx, qy, c)).wait_recv()
                fwd = _remote(landed, landed, send_sems, recv_sems, 6 * a + 3 + r, sibling)
                fwd.start()
                forwards.append(fwd)
        for a in range(ns):
            for r, (qx, qy) in enumerate(others):
                landed = small_block(a, 2 * qx + qy)
                _remote(landed, landed, send_sems, recv_sems, 6 * n + 3 * a + r, (qx, qy, c)).wait_recv()
        for a, name in enumerate(names):
            for r, (qx, qy) in enumerate(others):
                landed = _shard_half_of(outs[a], name, 2 * qx + qy, 1 - c)
                _remote(landed, landed, send_sems, recv_sems, 6 * a + 3 + r, sibling).wait_recv()
        for cp in sends + forwards:
            cp.wait_send()
        for cp in local:
            cp.wait()

    launch()
    return {nm: ref[...] for nm, ref in zip(names, outs)}, [ref[...] for ref in small_out]


def _exchange_halves(grads, names, label, collective_id):
    n = len(names)
    sequencer = collective_id is not None

    def body(*refs):
        ins, outs = refs[:n], refs[n:2 * n]
        send_sems, recv_sems = refs[2 * n:]
        x, y, c = _place()
        if sequencer:
            _handshake([(x, y, 1 - c)])
        copies = []
        for a, name in enumerate(names):
            copies.append(_remote(_half_of(ins[a], BIG[name][2], 1 - c), outs[a], send_sems, recv_sems, a,
                                  (x, y, 1 - c)))
        for cp in copies:
            cp.start()
        for cp in copies:
            cp.wait()

    def half_shape(name):
        r, c_ = _full_shape(name)
        return (HALF, c_) if BIG[name][2] else (r, HALF)

    out_type = tuple(jax.ShapeDtypeStruct(half_shape(nm), F32) for nm in names)
    sems = (pltpu.SemaphoreType.DMA((n,)), pltpu.SemaphoreType.DMA((n,)))
    operands = [grads[nm] for nm in names]
    if sequencer:
        got = pl.kernel(
            body, mesh=plsc.ScalarSubcoreMesh(axis_name="seq", num_cores=1), name=label, out_type=out_type,
            scratch_types=sems, compiler_params=pltpu.CompilerParams(collective_id=collective_id),
        )(*operands)
    else:
        got = pl.pallas_call(
            body, in_specs=[ANY] * n, out_specs=[ANY] * n, out_shape=list(out_type), scratch_shapes=list(sems),
            name=label,
        )(*operands)
    return dict(zip(names, got))


def _chip_sum(grads, got, names, core, label):
    n = len(names)
    steps = 4
    g_specs, blks = [], []
    for name in names:
        rows, cols = got[name].shape
        tr = rows // steps
        if BIG[name][2]:
            g_specs.append(pl.BlockSpec((tr, cols), lambda i, s: (s[0] * steps + i, 0)))
        else:
            g_specs.append(pl.BlockSpec((tr, HALF), lambda i, s: (i, s[0])))
        blks.append(pl.BlockSpec((tr, cols), lambda i, s: (i, 0)))

    def body(s_ref, *refs):
        for a in range(n):
            t = refs[a][...] + refs[n + a][...]
            refs[2 * n + a][...] = t
            refs[3 * n + a][...] = t.astype(BF16)

    out = pl.pallas_call(
        body,
        grid_spec=pltpu.PrefetchScalarGridSpec(num_scalar_prefetch=1, grid=(steps,), in_specs=g_specs + blks,
                                               out_specs=blks + blks),
        out_shape=([jax.ShapeDtypeStruct(got[nm].shape, F32) for nm in names]
                   + [jax.ShapeDtypeStruct(got[nm].shape, BF16) for nm in names]),
        name=label, compiler_params=_params("parallel"),
    )(core, *[grads[nm] for nm in names], *[got[nm] for nm in names])
    return {nm: (out[a], out[n + a]) for a, nm in enumerate(names)}


def _piece_shape(name):
    rows, cols, by_col = BIG[name]
    return (HALF, cols) if by_col else (rows, HALF)


def _handshake(peers):
    barrier = pltpu.get_barrier_semaphore()
    for peer in peers:
        pl.semaphore_signal(barrier, inc=1, device_id=peer, device_id_type=MESH)
    pl.semaphore_wait(barrier, len(peers))


def _send_chip_sums(sums, names, label, collective_id):
    n = len(names)

    def body(*refs):
        ins, outs = refs[:n], refs[n:2 * n]
        send_sems, recv_sems = refs[2 * n:]
        x, y, c = _place()
        others = [_chip_of(x, y, r) for r in range(3)]
        _handshake([(qx, qy, c) for qx, qy in others])
        copies = []
        for a, name in enumerate(names):
            for r, (qx, qy) in enumerate(others):
                copies.append(_remote(_shard_of(ins[a], name, 2 * qx + qy), outs[a].at[r], send_sems, recv_sems,
                                      3 * a + r, (qx, qy, c)))
        for cp in copies:
            cp.start()
        for cp in copies:
            cp.wait()

    return pl.kernel(
        body, mesh=plsc.ScalarSubcoreMesh(axis_name="seq", num_cores=1), name=label,
        out_type=tuple(jax.ShapeDtypeStruct((3,) + _piece_shape(nm), BF16) for nm in names),
        scratch_types=(pltpu.SemaphoreType.DMA((3 * n,)), pltpu.SemaphoreType.DMA((3 * n,))),
        compiler_params=pltpu.CompilerParams(collective_id=collective_id),
    )(*[sums[nm] for nm in names])


def _total(parts, chip_core):
    steps = 2
    in_specs, out_specs, operands = [], [], []
    for name in BIG_NAMES:
        by_col = BIG[name][2]
        pr, pc = _piece_shape(name)
        tr = pr // steps
        if by_col:
            in_specs.append(pl.BlockSpec((tr, pc), lambda i, s: (i, s[0])))
            out_specs.append(pl.BlockSpec((tr, pc), lambda i, s: (s[1] * steps + i, 0)))
        else:
            in_specs.append(pl.BlockSpec((tr, pc), lambda i, s: (s[0] * steps + i, 0)))
            out_specs.append(pl.BlockSpec((tr, pc), lambda i, s: (i, s[1])))
        for r in range(3):
            in_specs.append(pl.BlockSpec((None, tr, pc), lambda i, s, r=r: (r, i, 0)))
        own, got = parts[name]
        operands += [own, got, got, got]

    def body(s_ref, *refs):
        for a in range(N_BIG):
            o_ref, a_ref, b_ref, c_ref = refs[4 * a:4 * a + 4]
            refs[4 * N_BIG + a][...] = (((o_ref[...] + a_ref[...].astype(F32)) + b_ref[...].astype(F32))
                                        + c_ref[...].astype(F32))

    totals = pl.pallas_call(
        body,
        grid_spec=pltpu.PrefetchScalarGridSpec(num_scalar_prefetch=1, grid=(steps,), in_specs=in_specs,
                                               out_specs=out_specs),
        out_shape=[jax.ShapeDtypeStruct(BIG[name][:2], F32) for name in BIG_NAMES],
        name="totals", compiler_params=_params("parallel"),
    )(chip_core, *operands)
    return dict(zip(BIG_NAMES, totals))


VEC_ROWS = 32
VEC_ROW = {"mix_norm_g": 0, "conv_b": 1, "b_rgate": 2, "b_igate": 3, "lru_lambda": 4, "rg_norm_g": 5,
           "hg_lower_bound": 6, "hg_norm_g": 8, "ffn_norm_g": 9, "final_norm_g": 10, "loss": 11,
           "conv_w": 12, "meta_tokens": 16}
N_DEV = 8


def _all_reduce_small(pieces, gates, totals):
    names = list(pieces)
    n_small = 10
    hv, hg = VEC_ROWS // 2, gates.shape[0] // 2

    def body(*refs):
        ins = refs[:len(names)]
        g_ref = refs[len(names)]
        vec_ref, gsum_ref = refs[len(names) + 1 + N_BIG:len(names) + 3 + N_BIG]
        big = refs[len(names) + 3 + N_BIG:len(names) + 3 + 2 * N_BIG]
        (mine_v, sib_v, sib_g, chip_v, chip_g, got_v, got_g, send_sems, recv_sems) = refs[len(names) + 3 + 2 * N_BIG:]
        x, y, c = _place()
        chip = 2 * x + y
        sibling = (x, y, 1 - c)
        share = []
        for a, name in enumerate(BIG_NAMES):
            half = _half_of(big[a], BIG[name][2], c)
            share.append(_remote(half, half, send_sems, recv_sems, n_small + a, sibling))
        for cp in share:
            cp.start()
        mine_v[...] = jnp.zeros_like(mine_v)
        for name, ref in zip(names, ins):
            nr, w = ref.shape
            mine_v[VEC_ROW[name]:VEC_ROW[name] + nr, 0:w] = ref[...]

        swap = [_remote(mine_v, sib_v, send_sems, recv_sems, 0, sibling),
                _remote(g_ref, sib_g, send_sems, recv_sems, 1, sibling)]
        for cp in swap:
            cp.start()
        for cp in swap:
            cp.wait()
        chip_v[...] = mine_v[...] + sib_v[...]
        chip_g[...] = g_ref[...] + sib_g[...]

        rows_v = pl.ds(pl.multiple_of(c * hv, 8), hv)
        rows_g = pl.ds(pl.multiple_of(c * hg, 8), hg)
        got_v[chip] = chip_v[rows_v, :]
        got_g[chip] = chip_g[rows_g, :]
        sends = []
        for r in range(3):
            qx, qy = _chip_of(x, y, r)
            sends.append(_remote(chip_v.at[rows_v, :], got_v.at[chip], send_sems, recv_sems, 2 + r, (qx, qy, c)))
            sends.append(_remote(chip_g.at[rows_g, :], got_g.at[chip], send_sems, recv_sems, 5 + r, (qx, qy, c)))
        for cp in sends:
            cp.start()
        for cp in sends:
            cp.wait()
        vec_ref[rows_v, :] = ((got_v[0] + got_v[1]) + got_v[2]) + got_v[3]
        gsum_ref[rows_g, :] = ((got_g[0] + got_g[1]) + got_g[2]) + got_g[3]

        back = [_remote(vec_ref.at[rows_v, :], vec_ref.at[rows_v, :], send_sems, recv_sems, 8, sibling),
                _remote(gsum_ref.at[rows_g, :], gsum_ref.at[rows_g, :], send_sems, recv_sems, 9, sibling)]
        for cp in back:
            cp.start()
        theirs_v = vec_ref.at[pl.ds(pl.multiple_of((1 - c) * hv, 8), hv), :]
        theirs_g = gsum_ref.at[pl.ds(pl.multiple_of((1 - c) * hg, 8), hg), :]
        _remote(theirs_v, theirs_v, send_sems, recv_sems, 8, sibling).wait_recv()
        _remote(theirs_g, theirs_g, send_sems, recv_sems, 9, sibling).wait_recv()
        for cp in back:
            cp.wait_send()
        for a, name in enumerate(BIG_NAMES):
            theirs = _half_of(big[a], BIG[name][2], 1 - c)
            _remote(theirs, theirs, send_sems, recv_sems, n_small + a, sibling).wait_recv()
        for cp in share:
            cp.wait_send()

    vmem = pl.BlockSpec(memory_space=pltpu.VMEM)
    n_sems = n_small + N_BIG
    out = pl.pallas_call(
        body, in_specs=[vmem] * (len(names) + 1) + [ANY] * N_BIG, out_specs=[vmem, vmem] + [ANY] * N_BIG,
        out_shape=([jax.ShapeDtypeStruct((VEC_ROWS, D_MODEL), F32), jax.ShapeDtypeStruct(gates.shape, F32)]
                   + [jax.ShapeDtypeStruct(BIG[n][:2], F32) for n in BIG_NAMES]),
        input_output_aliases={len(names) + 1 + a: 2 + a for a in range(N_BIG)},
        scratch_shapes=[pltpu.VMEM((VEC_ROWS, D_MODEL), F32), pltpu.VMEM((VEC_ROWS, D_MODEL), F32),
                        pltpu.VMEM(gates.shape, F32), pltpu.VMEM((VEC_ROWS, D_MODEL), F32),
                        pltpu.VMEM(gates.shape, F32), pltpu.VMEM((N_CHIPS, hv, D_MODEL), F32),
                        pltpu.VMEM((N_CHIPS, hg) + gates.shape[1:], F32),
                        pltpu.SemaphoreType.DMA((n_sems,)), pltpu.SemaphoreType.DMA((n_sems,))],
        name="all_reduce_small",
    )(*[pieces[n] for n in names], gates, *[totals[n] for n in BIG_NAMES])
    return out[0], out[1], dict(zip(BIG_NAMES, out[2:]))


def _adamw_math(w, g, m, v):
    m = ADAM_B1 * m + (1.0 - ADAM_B1) * g
    v = ADAM_B2 * v + (1.0 - ADAM_B2) * (g * g)
    m_hat = m / (1.0 - ADAM_B1 ** ADAM_STEP)
    v_hat = v / (1.0 - ADAM_B2 ** ADAM_STEP)
    delta = -ADAM_LR * (m_hat / (jnp.sqrt(v_hat) + ADAM_EPS) + ADAM_WD * w)
    return delta, m, v


def _adamw_big(w, g, m, v):
    steps = 8
    blks = []
    for name in BIG_NAMES:
        rows, cols, _ = BIG[name]
        blks.append(pl.BlockSpec((rows // steps, cols), lambda i: (i, 0)))

    def body(*refs):
        ins, outs = refs[:4 * N_BIG], refs[4 * N_BIG:]
        for a in range(N_BIG):
            w_ref, g_ref, m_ref, v_ref = (ins[k * N_BIG + a] for k in range(4))
            g = g_ref[...]
            d, nm, nv = _adamw_math(w_ref[...], g, m_ref[...], v_ref[...])
            outs[a][...] = g
            outs[N_BIG + a][...] = d
            outs[2 * N_BIG + a][...] = nm
            outs[3 * N_BIG + a][...] = nv

    shapes = [jax.ShapeDtypeStruct(BIG[name][:2], F32) for name in BIG_NAMES]
    out = pl.pallas_call(
        body, grid=(steps,), in_specs=blks * 4, out_specs=blks * 4, out_shape=shapes * 4,
        name="adamw_big", compiler_params=_params("parallel"),
    )(*[t[name] for t in (w, g, m, v) for name in BIG_NAMES])
    return {name: tuple(out[k * N_BIG + a] for k in range(4)) for a, name in enumerate(BIG_NAMES)}


SMALL = {"meta_tokens": (N_META, D_MODEL // N_CHIPS), "mix_norm_g": (1, D_MODEL), "conv_w": (CONV_W, D_RG // N_CHIPS),
         "conv_b": (1, D_RG), "w_rgate": (D_RG, RG_HEAD_DIM), "b_rgate": (1, D_RG), "w_igate": (D_RG, RG_HEAD_DIM),
         "b_igate": (1, D_RG), "lru_lambda": (1, D_RG), "rg_norm_g": (1, D_RG), "hg_lower_bound": (2, D_HG),
         "hg_norm_g": (1, HG_HEAD_DIM), "ffn_norm_g": (1, D_MODEL), "final_norm_g": (1, D_MODEL)}
SMALL_NAMES = tuple(SMALL)
SHARDED_SMALL = ("meta_tokens", "conv_w")


def _adamw_small(vec, gates, w, m, v):
    n = len(SMALL_NAMES)

    def body(*refs):
        vec_ref, gates_ref = refs[:2]
        w_refs, m_refs, v_refs = refs[2:2 + n], refs[2 + n:2 + 2 * n], refs[2 + 2 * n:2 + 3 * n]
        outs = refs[2 + 3 * n:]
        loss_ref = outs[0]
        x, y, _ = _place()
        chip = 2 * x + y
        loss_ref[...] = vec_ref[VEC_ROW["loss"]:VEC_ROW["loss"] + 1, 0:1]

        def update(k, g):
            g_ref, d_ref, nm_ref, nv_ref = outs[1 + 4 * k:5 + 4 * k]
            g_ref[...] = g
            d_ref[...], nm_ref[...], nv_ref[...] = _adamw_math(w_refs[k][...], g, m_refs[k][...], v_refs[k][...])

        for k, name in enumerate(SMALL_NAMES):
            nr, w_ = SMALL[name]
            if name == "w_rgate":
                update(k, gates_ref[0:D_RG, :])
            elif name == "w_igate":
                update(k, gates_ref[D_RG:2 * D_RG, :])
            elif name in SHARDED_SMALL:
                r0 = VEC_ROW[name]
                for q in range(N_CHIPS):
                    @pl.when(chip == q)
                    def _(k=k, r0=r0, nr=nr, w_=w_, q=q):
                        update(k, vec_ref[r0:r0 + nr, q * w_:(q + 1) * w_])
            else:
                r0 = VEC_ROW[name]
                update(k, vec_ref[r0:r0 + nr, 0:w_])

    vmem = pl.BlockSpec(memory_space=pltpu.VMEM)
    out_shape = [jax.ShapeDtypeStruct((1, 1), F32)]
    for name in SMALL_NAMES:
        out_shape += [jax.ShapeDtypeStruct(SMALL[name], F32)] * 4
    outs = pl.pallas_call(
        body, in_specs=[vmem] * (2 + 3 * n), out_specs=[vmem] * len(out_shape), out_shape=out_shape,
        name="adamw_small",
    )(vec, gates, *[w[k] for k in SMALL_NAMES], *[m[k] for k in SMALL_NAMES], *[v[k] for k in SMALL_NAMES])
    loss = outs[0]
    res = {name: tuple(outs[1 + 4 * k:5 + 4 * k]) for k, name in enumerate(SMALL_NAMES)}
    return loss, res


WEIGHT_NAMES = ("meta_tokens", "mix_norm_g", "w_in", "conv_w", "conv_b", "w_rgate", "b_rgate", "w_igate", "b_igate",
                "lru_lambda", "rg_norm_g", "hg_lower_bound", "hg_norm_g", "w_out", "ffn_norm_g", "w_gate_up", "w_down",
                "final_norm_g")


def _to_2d(name, a):
    if name in BIG:
        return a.reshape(BIG[name][:2])
    return a.reshape(SMALL[name])


def kernel(x, meta_tokens, mix_norm_g, w_in, conv_w, conv_b, w_rgate, b_rgate, w_igate, b_igate, lru_lambda, rg_norm_g, hg_lower_bound, hg_norm_g, w_out, ffn_norm_g, w_gate_up, w_down, final_norm_g, loss_target, m_meta_tokens, m_mix_norm_g, m_w_in, m_conv_w, m_conv_b, m_w_rgate, m_b_rgate, m_w_igate, m_b_igate, m_lru_lambda, m_rg_norm_g, m_hg_lower_bound, m_hg_norm_g, m_w_out, m_ffn_norm_g, m_w_gate_up, m_w_down, m_final_norm_g, v_meta_tokens, v_mix_norm_g, v_w_in, v_conv_w, v_conv_b, v_w_rgate, v_b_rgate, v_w_igate, v_b_igate, v_lru_lambda, v_rg_norm_g, v_hg_lower_bound, v_hg_norm_g, v_w_out, v_ffn_norm_g, v_w_gate_up, v_w_down, v_final_norm_g):
    w_raw = dict(zip(WEIGHT_NAMES, (meta_tokens, mix_norm_g, w_in, conv_w, conv_b, w_rgate, b_rgate, w_igate, b_igate,
                                    lru_lambda, rg_norm_g, hg_lower_bound, hg_norm_g, w_out, ffn_norm_g, w_gate_up,
                                    w_down, final_norm_g)))
    m_raw = dict(zip(WEIGHT_NAMES, (m_meta_tokens, m_mix_norm_g, m_w_in, m_conv_w, m_conv_b, m_w_rgate, m_b_rgate,
                                    m_w_igate, m_b_igate, m_lru_lambda, m_rg_norm_g, m_hg_lower_bound, m_hg_norm_g,
                                    m_w_out, m_ffn_norm_g, m_w_gate_up, m_w_down, m_final_norm_g)))
    v_raw = dict(zip(WEIGHT_NAMES, (v_meta_tokens, v_mix_norm_g, v_w_in, v_conv_w, v_conv_b, v_w_rgate, v_b_rgate,
                                    v_w_igate, v_b_igate, v_lru_lambda, v_rg_norm_g, v_hg_lower_bound, v_hg_norm_g,
                                    v_w_out, v_ffn_norm_g, v_w_gate_up, v_w_down, v_final_norm_g)))
    w = {k: _to_2d(k, a) for k, a in w_raw.items()}
    m = {k: _to_2d(k, a) for k, a in m_raw.items()}
    v = {k: _to_2d(k, a) for k, a in v_raw.items()}

    x_i, y_i, c_i = _place()
    core = jnp.reshape(c_i, (1,)).astype(jnp.int32)
    chip = jnp.reshape(2 * x_i + y_i, (1,)).astype(jnp.int32)
    chip_core = jnp.concatenate([chip, core])

    placed, (meta_full, cw_full) = _place_shards(w, [w["meta_tokens"], w["conv_w"]], chip)
    first, _ = _gather_weights(placed, [], ("w_in",), "gather_first", 1)
    rest, _ = _gather_weights(placed, [], ("w_out", "w_gate_up", "w_down"), "gather_rest", 2)
    full = {**first, **rest}

    seq = x.shape[1]
    small ={k: w[k] for k in SMALL_NAMES if k not in SHARDED_SMALL}
    small["conv_w"] = cw_full

    def reduce_to_chips(grads, names, tag, collective_ids):
        got = _exchange_halves(grads, names, "exchange_halves_" + tag, collective_ids[0])

        def chip_sums():
            return _chip_sum(grads, got, names, core, "chip_sum_" + tag)

        def send(sums):
            arrived = _send_chip_sums({n: sums[n][1] for n in names}, names, "send_chip_sums_" + tag,
                                      collective_ids[1])
            return {n: (sums[n][0], a) for n, a in zip(names, arrived)}

        return chip_sums, send

    ffn_names, mixer_names = ("w_gate_up", "w_down", "w_out"), ("w_in",)
    loss, grad_x, grads, parts, parts_mixer = _local_step(
        x.reshape(seq, D_MODEL), meta_full, loss_target.reshape(seq, D_MODEL),
        w["w_in"], full["w_in"], full["w_out"], full["w_gate_up"], full["w_down"], small, chip,
        on_ffn_grads=lambda g: reduce_to_chips(g, ffn_names, "ffn", (3, 4)),
        on_mixer_grads=lambda g: reduce_to_chips(g, mixer_names, "mixer", (None, 5)))
    parts.update(parts_mixer)
    totals = _total(parts, chip_core)
    pieces = {k: grads[k] for k in VEC_ROW if k != "loss"}
    pieces["loss"] = loss
    vec, gates, g_big = _all_reduce_small(pieces, grads["w_gates"], totals)
    loss_sum, res = _adamw_small(vec, gates, w, m, v)
    res.update(_adamw_big(w, g_big, m, v))

    out = [loss_sum.reshape(()), grad_x.reshape(1, seq, D_MODEL)]
    for j in range(4):
        out += [res[n][j].reshape(w_raw[n].shape) for n in WEIGHT_NAMES]
    return tuple(out)
```

```python
import math

import jax
import jax.numpy as jnp
from jax import lax
from jax.experimental import pallas as pl
from jax.experimental.pallas import tpu as pltpu
from jax.experimental.pallas import tpu_sc as plsc

F32 = jnp.float32
BF16 = jnp.bfloat16
MESH = pl.DeviceIdType.MESH

D_MODEL = 1024
D_RG = 512
RG_HEAD_DIM = 64
D_HG = 512
HG_HEAD_DIM = 128
HG_HEADS = 4
CHUNK = 64
SUB = 16
N_SUB = CHUNK // SUB
N_META = 16
PAD = CHUNK - N_META
D_IN = 3072
D_FF = 2816
CONV_W = 4
LRU_C = 8.0
EPS = 1e-6
EXP_CLAMP = 80.0
GELU_C = math.sqrt(2.0 / math.pi)
GELU_A = 0.044715
N_CHIPS = 4

ADAM_LR = 0.001
ADAM_B1 = 0.9
ADAM_B2 = 0.999
ADAM_EPS = 1e-08
ADAM_WD = 0.01
ADAM_STEP = 10

VMEM_LIMIT = 56 * 1024 * 1024


def _params(*sem):
    return pltpu.CompilerParams(dimension_semantics=sem, vmem_limit_bytes=VMEM_LIMIT)


def _row_tile(rows, target):
    best = None
    for t in range(16, min(rows, target) + 1, 16):
        if rows % t == 0:
            best = t
    assert best is not None, rows
    return best


def _sigmoid(x):
    return 0.5 * jnp.tanh(0.5 * x) + 0.5


def _dot(a, b):
    return jnp.dot(a, b, preferred_element_type=F32)


def _dot_nt(a, b):
    return lax.dot_general(a, b, (((1,), (1,)), ((), ())), preferred_element_type=F32)


def _dot_tn(a, b):
    return lax.dot_general(a, b, (((0,), (0,)), ((), ())), preferred_element_type=F32)


def _rms(x):
    return lax.rsqrt(jnp.mean(x * x, axis=-1, keepdims=True) + EPS)


def _rms_bwd(dn, n, r):
    return r * (dn - n * jnp.mean(dn * n, axis=-1, keepdims=True))


def _gelu_parts(x):
    t = jnp.tanh(GELU_C * (x + GELU_A * x * x * x))
    g = 0.5 * x * (1.0 + t)
    dg = 0.5 * (1.0 + t) + 0.5 * x * (1.0 - t * t) * GELU_C * (1.0 + 3.0 * GELU_A * x * x)
    return g, dg


def _softplus_neg(lam):
    e = jnp.exp(-jnp.abs(lam))
    w = 1.0 + e
    log1p = jnp.where(w == 1.0, e, jnp.log(w) * e / (w - 1.0))
    return jnp.maximum(-lam, 0.0) + log1p


def _head_mask():
    r = lax.broadcasted_iota(jnp.int32, (D_RG, D_RG), 0) // RG_HEAD_DIM
    c = lax.broadcasted_iota(jnp.int32, (D_RG, D_RG), 1) // RG_HEAD_DIM
    return r == c


def _head_fold():
    r = lax.broadcasted_iota(jnp.int32, (D_RG, RG_HEAD_DIM), 0) % RG_HEAD_DIM
    c = lax.broadcasted_iota(jnp.int32, (D_RG, RG_HEAD_DIM), 1)
    return (r == c).astype(F32)


def _gate_weights(w_r, w_i):
    def body(wr_ref, wi_ref, o_ref):
        fold = _head_fold()
        mask = _head_mask()
        for k, ref in enumerate((wr_ref, wi_ref)):
            full = _dot_nt(ref[...].astype(BF16), fold.astype(BF16))
            o_ref[:, k * D_RG:(k + 1) * D_RG] = jnp.where(mask, full, 0.0).astype(BF16)

    return pl.pallas_call(
        body, out_shape=jax.ShapeDtypeStruct((D_RG, 2 * D_RG), BF16), name="gate_weights",
    )(w_r, w_i)


HEAD = PAD + N_META


def _window_copies(seq_hbm, buf, sems, tm):
    def first(to_vmem):
        seq, vm = seq_hbm.at[pl.ds(0, tm - HEAD)], buf.at[0, pl.ds(HEAD, tm - HEAD)]
        return pltpu.make_async_copy(seq, vm, sems.at[0]) if to_vmem else pltpu.make_async_copy(vm, seq, sems.at[0])

    def later(j, slot, to_vmem):
        seq, vm = seq_hbm.at[pl.ds(pl.multiple_of(j * tm - HEAD, 8), tm)], buf.at[slot]
        if to_vmem:
            return pltpu.make_async_copy(seq, vm, sems.at[slot])
        return pltpu.make_async_copy(vm, seq, sems.at[slot])

    return first, later


def _fetch_window(seq_hbm, buf, sems, i, n_steps, tm):
    first, later = _window_copies(seq_hbm, buf, sems, tm)
    slot = i % 2

    @pl.when(i == 0)
    def _():
        first(True).start()

    if n_steps > 1:
        @pl.when(i + 1 < n_steps)
        def _():
            later(i + 1, 1 - slot, True).start()

    @pl.when(i == 0)
    def _():
        first(True).wait()

    if n_steps > 1:
        @pl.when(i > 0)
        def _():
            later(i, slot, True).wait()

    return slot


def _in_proj_local(x, meta, g1, w_own, chip):
    T = x.shape[0] + HEAD
    tm = _row_tile(T, 832)
    n_steps = T // tm
    cols = BIG["w_in"][1]

    def body(s_ref, x_hbm, meta_ref, g_ref, w_ref, p_ref, u_ref, h_ref, buf, sems, wb):
        i = pl.program_id(0)
        slot = _fetch_window(x_hbm, buf, sems, i, n_steps, tm)

        @pl.when(i == 0)
        def _():
            buf[0, 0:PAD, :] = jnp.zeros((PAD, D_MODEL), F32)
            buf[0, PAD:HEAD, :] = meta_ref[...]
            wb[...] = w_ref[...].astype(BF16)

        h = buf[slot]
        h_ref[...] = h
        u = (h * _rms(h) * g_ref[...]).astype(BF16)
        u_ref[...] = u
        p_ref[...] = _dot(u, wb[...])

    return pl.pallas_call(
        body,
        grid_spec=pltpu.PrefetchScalarGridSpec(
            num_scalar_prefetch=1, grid=(n_steps,),
            in_specs=[pl.BlockSpec(memory_space=pl.ANY),
                      pl.BlockSpec((N_META, D_MODEL), lambda i, s: (0, 0)),
                      pl.BlockSpec((1, D_MODEL), lambda i, s: (0, 0)),
                      pl.BlockSpec((D_MODEL, cols), lambda i, s: (0, 0))],
            out_specs=[pl.BlockSpec((tm, cols), lambda i, s: (i, s[0])),
                       pl.BlockSpec((tm, D_MODEL), lambda i, s: (i, 0)),
                       pl.BlockSpec((tm, D_MODEL), lambda i, s: (i, 0))],
            scratch_shapes=[pltpu.VMEM((2, tm, D_MODEL), F32), pltpu.SemaphoreType.DMA((2,)),
                            pltpu.VMEM((D_MODEL, cols), BF16)]),
        out_shape=[jax.ShapeDtypeStruct((T, D_IN), F32), jax.ShapeDtypeStruct((T, D_MODEL), BF16),
                   jax.ShapeDtypeStruct((T, D_MODEL), F32)],
        name="in_proj_local", compiler_params=_params("arbitrary"),
    )(chip, x, meta, g1, w_own)


def _in_proj_rest(u, w_in, p, chip):
    T = u.shape[0]
    tm = _row_tile(T, 2080)
    cols = BIG["w_in"][1]
    block = lambda j, s: (s[0] + 1 + j) % N_CHIPS

    def body(s_ref, u_ref, w_ref, p_in_ref, p_ref):
        p_ref[...] = _dot(u_ref[...], w_ref[...])

    return pl.pallas_call(
        body,
        grid_spec=pltpu.PrefetchScalarGridSpec(
            num_scalar_prefetch=1, grid=(N_CHIPS - 1, T // tm),
            in_specs=[pl.BlockSpec((tm, D_MODEL), lambda j, i, s: (i, 0)),
                      pl.BlockSpec((D_MODEL, cols), lambda j, i, s: (0, block(j, s))), ANY],
            out_specs=pl.BlockSpec((tm, cols), lambda j, i, s: (i, block(j, s)))),
        out_shape=jax.ShapeDtypeStruct((T, D_IN), F32),
        input_output_aliases={3: 0},
        name="in_proj_rest", compiler_params=_params("arbitrary", "arbitrary"),
    )(chip, u, w_in, p)


def _scan_block_fwd(A, B, rowi):
    for d in (1, 2, 4):
        a_sh = pltpu.roll(A, d, axis=0)
        b_sh = pltpu.roll(B, d, axis=0)
        m = rowi >= d
        B = jnp.where(m, A * b_sh + B, B)
        A = jnp.where(m, A * a_sh, A)
    return A, B


def _scan_block_bwd(A, B, rowi):
    for d in (1, 2, 4):
        a_sh = pltpu.roll(A, 8 - d, axis=0)
        b_sh = pltpu.roll(B, 8 - d, axis=0)
        m = rowi < 8 - d
        B = jnp.where(m, A * b_sh + B, B)
        A = jnp.where(m, A * a_sh, A)
    return A, B


def _rg_gates(xc, w_ref, bg_ref, lam):
    pre = _dot(xc.astype(BF16), w_ref[...]) + bg_ref[...]
    r = _sigmoid(pre[:, :D_RG])
    ig = _sigmoid(pre[:, D_RG:])
    sp = _softplus_neg(lam)
    la = -LRU_C * sp * r
    a = jnp.exp(la)
    th = jnp.tanh(la)
    u = 1.0 - th
    rc = pl.reciprocal(u, approx=True)
    rc = rc * (2.0 - u * rc)
    rc = rc * (2.0 - u * rc)
    m2 = -2.0 * th * rc
    inv_m = lax.rsqrt(jnp.maximum(m2, 1e-30))
    return r, ig, sp, a, m2 * inv_m, inv_m


def _conv(ext, cw_ref, cb_ref, tm):
    xc = cb_ref[...] + cw_ref[0:1, :] * ext[8 - 3:8 - 3 + tm, :]
    for j in range(1, CONV_W):
        xc = xc + cw_ref[j:j + 1, :] * ext[8 - 3 + j:8 - 3 + j + tm, :]
    return xc


def _scan_unroll(blocks):
    return 4 if blocks % 4 == 0 else 2 if blocks % 2 == 0 else 1


def _rg_fwd(p, cw, cb, wg, bg, lam, rg_g):
    T = p.shape[0]
    tm = _row_tile(T, 832)
    unroll = _scan_unroll(tm // 8)

    def body(xg_ref, cw_ref, cb_ref, w_ref, bg_ref, lam_ref, g_ref, y_ref, h_ref, xc_ref, ext, a_s, b_s, carry):
        i = pl.program_id(0)

        @pl.when(i == 0)
        def _():
            ext[0:8, :] = jnp.zeros((8, D_RG), F32)
            carry[...] = jnp.zeros((1, D_RG), F32)

        ext[8:8 + tm, :] = xg_ref[:, :D_RG]
        xc = _conv(ext, cw_ref, cb_ref, tm)
        xc_ref[...] = xc
        r, ig, sp, a, m, _ = _rg_gates(xc, w_ref, bg_ref, lam_ref[...])
        row = i * tm + lax.broadcasted_iota(jnp.int32, (tm, 1), 0)
        a_s[...] = a
        b_s[...] = jnp.where(row >= PAD, m * ig * xc, 0.0)
        rowi = lax.broadcasted_iota(jnp.int32, (8, D_RG), 0)

        def blk(j, c):
            for u in range(unroll):
                o = pl.multiple_of((j * unroll + u) * 8, 8)
                A, B = _scan_block_fwd(a_s[pl.ds(o, 8), :], b_s[pl.ds(o, 8), :], rowi)
                h = B + A * c
                h_ref[pl.ds(o, 8), :] = h
                c = h[7:8, :]
            return c

        carry[...] = lax.fori_loop(0, tm // (8 * unroll), blk, carry[...])
        ext[0:8, :] = ext[tm:tm + 8, :]
        g, _ = _gelu_parts(xg_ref[:, D_RG:])
        yy = g * h_ref[...]
        y_ref[...] = (yy * _rms(yy) * g_ref[...]).astype(BF16)

    vec = lambda n: pl.BlockSpec((1, n), lambda i: (0, 0))
    return pl.pallas_call(
        body, grid=(T // tm,),
        in_specs=[pl.BlockSpec((tm, 2 * D_RG), lambda i: (i, 0)),
                  pl.BlockSpec((CONV_W, D_RG), lambda i: (0, 0)), vec(D_RG),
                  pl.BlockSpec((D_RG, 2 * D_RG), lambda i: (0, 0)), vec(2 * D_RG), vec(D_RG), vec(D_RG)],
        out_specs=[pl.BlockSpec((tm, D_RG), lambda i: (i, 0))] * 3,
        out_shape=[jax.ShapeDtypeStruct((T, D_RG), BF16), jax.ShapeDtypeStruct((T, D_RG), F32),
                   jax.ShapeDtypeStruct((T, D_RG), F32)],
        scratch_shapes=[pltpu.VMEM((tm + 8, D_RG), F32), pltpu.VMEM((tm, D_RG), F32),
                        pltpu.VMEM((tm, D_RG), F32), pltpu.VMEM((1, D_RG), F32)],
        name="rg_fwd", compiler_params=_params("arbitrary"),
    )(p, cw, cb, wg, bg, lam, rg_g)


def _running_sum(x, down):
    r = lax.broadcasted_iota(jnp.int32, (CHUNK, CHUNK), 0)
    c = lax.broadcasted_iota(jnp.int32, (CHUNK, CHUNK), 1)
    tri = ((c <= r) if down else (c >= r)).astype(BF16)
    hi = x.astype(BF16)
    rest = x - hi.astype(F32)
    mid = rest.astype(BF16)
    lo = (rest - mid.astype(F32)).astype(BF16)
    return (_dot(tri, hi) + _dot(tri, mid)) + _dot(tri, lo)


def _hg_gates(hq, hf, lbraw_ref, valid):
    lb = _sigmoid(lbraw_ref[0:1, :] - lbraw_ref[1:2, :])
    sq = _sigmoid(hq)
    q = hq * sq
    sf = _sigmoid(hf)
    f = lb + (1.0 - lb) * sf
    lf = jnp.where(valid, jnp.log(f), 0.0)
    b = _running_sum(lf, True)
    return lb, sq, q, sf, f, b


def _hg_head(qh, kh, bh):
    b_last = bh[CHUNK - 1:CHUNK, :]
    refs = [bh[SUB * s:SUB * s + 1, :] for s in range(N_SUB)]
    r_sel = jnp.concatenate([jnp.broadcast_to(refs[s], (SUB, HG_HEAD_DIM)) for s in range(N_SUB)], axis=0)
    eb = jnp.exp(bh)
    eq = jnp.exp(bh - r_sel)
    ekh = jnp.exp(b_last - bh)
    ek = []
    for s in range(N_SUB):
        n = SUB * (s + 1)
        e = jnp.exp(jnp.minimum(refs[s] - bh[:n, :], EXP_CLAMP))
        ek.append(e if n == CHUNK else jnp.concatenate([e, jnp.zeros((CHUNK - n, HG_HEAD_DIM), F32)], axis=0))
    qe = qh * eq

    def own_rows(s):
        parts = [jnp.zeros((SUB * s, HG_HEAD_DIM), F32)] if s else []
        parts.append(qe[SUB * s:SUB * (s + 1), :])
        if s < N_SUB - 1:
            parts.append(jnp.zeros((CHUNK - SUB * (s + 1), HG_HEAD_DIM), F32))
        return jnp.concatenate(parts, axis=0)

    q_hat = jnp.concatenate([own_rows(s) for s in range(N_SUB)], axis=1)
    k_til = jnp.concatenate([kh * ek[s] for s in range(N_SUB)], axis=1)
    return b_last, eb, eq, ekh, ek, q_hat, k_til


def _causal():
    r = lax.broadcasted_iota(jnp.int32, (CHUNK, CHUNK), 0)
    c = lax.broadcasted_iota(jnp.int32, (CHUNK, CHUNK), 1)
    return r >= c


def _chunks_per_step(n_chunks):
    for c in (5, 4, 3, 2):
        if n_chunks % c == 0:
            return c
    return 1


def _hg_fwd(p, lbraw, hg_g):
    T = p.shape[0]
    n_chunks = T // CHUNK
    cps = _chunks_per_step(n_chunks)
    rows = cps * CHUNK

    def body(hq_ref, hf_ref, hi_ref, hg_ref, lb_ref, g_ref, y_ref, o_ref, st_all_ref, st):
        i = pl.program_id(0)

        @pl.when(i == 0)
        def _():
            st[...] = jnp.zeros_like(st)

        def chunk(j, carry):
            rs = pl.ds(pl.multiple_of(j * CHUNK, CHUNK), CHUNK)
            chunk_body(i * cps + j, hq_ref.at[rs, :], hf_ref.at[rs, :], hi_ref.at[rs, :], hg_ref.at[rs, :], lb_ref,
                       g_ref, y_ref.at[rs, :], o_ref.at[rs, :], st_all_ref.at[pl.ds(j, 1)], st)
            return carry

        lax.fori_loop(0, cps, chunk, 0, unroll=True)

    def chunk_body(n, hq_ref, hf_ref, hi_ref, hg_ref, lb_ref, g_ref, y_ref, o_ref, st_all_ref, st):
        valid = (n * CHUNK + lax.broadcasted_iota(jnp.int32, (CHUNK, 1), 0)) >= PAD
        hq, hf, v, hg = hq_ref[...], hf_ref[...], hi_ref[...], hg_ref[...]
        lb, sq, q, sf, f, b = _hg_gates(hq, hf, lb_ref, valid)
        k = 1.0 - f
        st_all_ref[0] = st[...]
        causal = _causal()
        v_t = v.T.astype(BF16)
        heads = [slice(h * HG_HEAD_DIM, (h + 1) * HG_HEAD_DIM) for h in range(HG_HEADS)]
        fac = []
        for sl in heads:
            qh, kh, bh = q[:, sl], k[:, sl], b[:, sl]
            b_last, eb, _, ekh, _, q_hat, k_til = _hg_head(qh, kh, bh)
            fac.append((jnp.exp(b_last), (qh * eb).astype(BF16), q_hat.astype(BF16), k_til.astype(BF16),
                        (kh * ekh).astype(BF16), v[:, sl].astype(BF16)))
        raw = []
        for sl, (_, q_til, q_hat, k_til, k_hat, _) in zip(heads, fac):
            st_h = st[sl, :]
            raw.append((_dot_nt(q_til, st_h.astype(BF16)), _dot_nt(q_hat, k_til), _dot(v_t[sl, :], k_hat), st_h))
        for sl, (e_last, _, _, _, _, vb), (inter, att, upd, st_h) in zip(heads, fac, raw):
            o = inter + _dot(jnp.where(causal, att, 0.0).astype(BF16), vb)
            st[sl, :] = st_h * e_last + upd
            o_ref[:, sl] = o
            hgh = hg[:, sl]
            y_ref[:, sl] = (o * _rms(o) * g_ref[...] * (hgh * _sigmoid(hgh))).astype(BF16)

    col = lambda j: pl.BlockSpec((rows, D_HG), lambda n: (n, j))
    return pl.pallas_call(
        body, grid=(n_chunks // cps,),
        in_specs=[col(2), col(3), col(4), col(5),
                  pl.BlockSpec((2, D_HG), lambda n: (0, 0)), pl.BlockSpec((1, HG_HEAD_DIM), lambda n: (0, 0))],
        out_specs=[pl.BlockSpec((rows, D_HG), lambda n: (n, 0)), pl.BlockSpec((rows, D_HG), lambda n: (n, 0)),
                   pl.BlockSpec((cps, D_HG, HG_HEAD_DIM), lambda n: (n, 0, 0))],
        out_shape=[jax.ShapeDtypeStruct((T, D_HG), BF16), jax.ShapeDtypeStruct((T, D_HG), F32),
                   jax.ShapeDtypeStruct((n_chunks, D_HG, HG_HEAD_DIM), F32)],
        scratch_shapes=[pltpu.VMEM((D_HG, HG_HEAD_DIM), F32)],
        name="hg_fwd", compiler_params=_params("arbitrary"),
    )(p, p, p, p, lbraw, hg_g)


def _ffn_fwd(h0, y_rg, y_hg, w_out, g2, w_gu, w_down, gf, target):
    T = h0.shape[0]
    tm = _row_tile(T, 320)
    n_steps = T // tm

    def body(h_ref, yr_ref, yh_ref, wo_ref, g2_ref, wgu_ref, wd_ref, gf_ref, t_hbm,
             h1_ref, v_ref, y_ref, gu_ref, act_ref, dh2_ref, dh2b_ref, loss_ref, gg_ref, tbuf, sems):
        i = pl.program_id(0)
        slot = _fetch_window(t_hbm, tbuf, sems, i, n_steps, tm)

        @pl.when(i == 0)
        def _():
            loss_ref[...] = jnp.zeros_like(loss_ref)
            gg_ref[...] = jnp.zeros_like(gg_ref)
            tbuf[0, 0:HEAD, :] = jnp.zeros((HEAD, D_MODEL), F32)

        y_ref[:, :D_RG] = yr_ref[...]
        y_ref[:, D_RG:] = yh_ref[...]
        h1 = h_ref[...] + _dot(y_ref[...], wo_ref[...])
        h1_ref[...] = h1
        v = (h1 * _rms(h1) * g2_ref[...]).astype(BF16)
        v_ref[...] = v

        gu = _dot(v, wgu_ref[...])
        gu_ref[...] = gu.astype(BF16)
        g = gu[:, :D_FF]
        act = (g * _sigmoid(g) * gu[:, D_FF:]).astype(BF16)
        act_ref[...] = act

        h2 = h1 + _dot(act, wd_ref[...])
        r = _rms(h2)
        n = h2 * r
        gf_ = gf_ref[...]
        row = i * tm + lax.broadcasted_iota(jnp.int32, (tm, 1), 0)
        err = jnp.where(row >= HEAD, n * gf_ - tbuf[slot], 0.0)
        loss_ref[...] += 0.5 * jnp.sum(jnp.mean(err * err, axis=-1, keepdims=True), axis=0, keepdims=True)
        dy = err * (1.0 / D_MODEL)
        gg_ref[...] += jnp.sum(dy * n, axis=0, keepdims=True)
        dh2 = _rms_bwd(dy * gf_, n, r)
        dh2_ref[...] = dh2
        dh2b_ref[...] = dh2.astype(BF16)

    row_spec = lambda n: pl.BlockSpec((tm, n), lambda i: (i, 0))
    vec = pl.BlockSpec((1, D_MODEL), lambda i: (0, 0))
    return pl.pallas_call(
        body, grid=(n_steps,),
        in_specs=[row_spec(D_MODEL), row_spec(D_RG), row_spec(D_HG), _resident((D_MODEL, D_MODEL)), vec,
                  _resident((D_MODEL, 2 * D_FF)), _resident((D_FF, D_MODEL)), vec,
                  pl.BlockSpec(memory_space=pl.ANY)],
        out_specs=[row_spec(D_MODEL), row_spec(D_MODEL), row_spec(D_MODEL), row_spec(2 * D_FF), row_spec(D_FF),
                   row_spec(D_MODEL), row_spec(D_MODEL), pl.BlockSpec((1, 1), lambda i: (0, 0)), vec],
        out_shape=[jax.ShapeDtypeStruct((T, D_MODEL), F32), jax.ShapeDtypeStruct((T, D_MODEL), BF16),
                   jax.ShapeDtypeStruct((T, D_MODEL), BF16), jax.ShapeDtypeStruct((T, 2 * D_FF), BF16),
                   jax.ShapeDtypeStruct((T, D_FF), BF16), jax.ShapeDtypeStruct((T, D_MODEL), F32),
                   jax.ShapeDtypeStruct((T, D_MODEL), BF16), jax.ShapeDtypeStruct((1, 1), F32),
                   jax.ShapeDtypeStruct((1, D_MODEL), F32)],
        scratch_shapes=[pltpu.VMEM((2, tm, D_MODEL), F32), pltpu.SemaphoreType.DMA((2,))],
        name="ffn_fwd", compiler_params=_params("arbitrary"),
    )(h0, y_rg, y_hg, w_out, g2, w_gu, w_down, gf, target)


def _resident(shape):
    return pl.BlockSpec(shape, lambda i: (0,) * len(shape), pipeline_mode=pl.Buffered(1))


def _ffn_bwd(dh2b, gu, w_down, w_gu, h1, g2, dh2, w_out):
    T = h1.shape[0]
    tm = _row_tile(T, 320)

    def body(d_ref, gu_ref, wd_ref, wgu_ref, h_ref, g_ref, d2_ref, wo_ref, dgu_ref, dh1_ref, dh1b_ref, dy_ref, gg_ref):
        i = pl.program_id(0)

        @pl.when(i == 0)
        def _():
            gg_ref[...] = jnp.zeros_like(gg_ref)

        dact = _dot_nt(d_ref[...], wd_ref[...]).astype(BF16)
        g = gu_ref[:, :D_FF]
        u = gu_ref[:, D_FF:]
        s = _sigmoid(g)
        dgu_ref[:, :D_FF] = dact * u * (s * (1.0 + g * (1.0 - s)))
        dgu_ref[:, D_FF:] = dact * (g * s)

        dv = _dot_nt(dgu_ref[...], wgu_ref[...])
        h1_ = h_ref[...]
        r = _rms(h1_)
        n = h1_ * r
        gg_ref[...] += jnp.sum(dv * n, axis=0, keepdims=True)
        dh1 = d2_ref[...] + _rms_bwd(dv * g_ref[...], n, r)
        dh1_ref[...] = dh1
        db = dh1.astype(BF16)
        dh1b_ref[...] = db
        dy_ref[...] = _dot_nt(db, wo_ref[...])

    row = lambda n: pl.BlockSpec((tm, n), lambda i: (i, 0))
    return pl.pallas_call(
        body, grid=(T // tm,),
        in_specs=[row(D_MODEL), row(2 * D_FF), _resident((D_FF, D_MODEL)), _resident((D_MODEL, 2 * D_FF)),
                  row(D_MODEL), pl.BlockSpec((1, D_MODEL), lambda i: (0, 0)), row(D_MODEL),
                  _resident((D_MODEL, D_MODEL))],
        out_specs=[row(2 * D_FF), row(D_MODEL), row(D_MODEL), row(D_MODEL),
                   pl.BlockSpec((1, D_MODEL), lambda i: (0, 0))],
        out_shape=[jax.ShapeDtypeStruct((T, 2 * D_FF), BF16), jax.ShapeDtypeStruct((T, D_MODEL), F32),
                   jax.ShapeDtypeStruct((T, D_MODEL), BF16), jax.ShapeDtypeStruct((T, D_MODEL), F32),
                   jax.ShapeDtypeStruct((1, D_MODEL), F32)],
        name="ffn_bwd", compiler_params=_params("arbitrary"),
    )(dh2b, gu, w_down, w_gu, h1, g2, dh2, w_out)


def _rg_bwd(p, xc_all, hs, dy, dp, cw, cb, wg, bg, lam, rg_g):
    T = p.shape[0]
    tm = _row_tile(T, 832)
    nt = T // tm
    hb = tm // 8
    unroll = _scan_unroll(hb)

    def body(xg_ref, xc_ref, h_ref, hh_ref, dy_ref, dp_in_ref, cw_ref, cb_ref, w_ref, bg_ref, lam_ref, g_ref,
             dp_ref, gcw_ref, gcb_ref, gw_ref, gbg_ref, glam_ref, gg_ref,
             dext, a_s, b_s, d_s, gacc, carry_d, carry_a):
        i = pl.program_id(0)
        t_idx = nt - 1 - i

        @pl.when(i == 0)
        def _():
            dext[tm:tm + 8, :] = jnp.zeros((8, D_RG), F32)
            carry_d[...] = jnp.zeros_like(carry_d)
            carry_a[...] = jnp.zeros_like(carry_a)
            gacc[...] = jnp.zeros_like(gacc)
            for ref in (gcw_ref, gcb_ref, gbg_ref, glam_ref, gg_ref, gw_ref):
                ref[...] = jnp.zeros_like(ref)

        first = t_idx == 0
        xc = xc_ref[...]
        lam_ = lam_ref[...]
        r, ig, sp, a, m, inv_m = _rg_gates(xc, w_ref, bg_ref, lam_)
        row = t_idx * tm + lax.broadcasted_iota(jnp.int32, (tm, 1), 0)
        valid = row >= PAD

        gr = xg_ref[:, D_RG:]
        g, dgelu = _gelu_parts(gr)
        h = h_ref[...]
        yy = g * h
        rr = _rms(yy)
        nn = yy * rr
        dy_ = dy_ref[...]
        gg_ref[...] += jnp.sum(dy_ * nn, axis=0, keepdims=True)
        dyy = _rms_bwd(dy_ * g_ref[...], nn, rr)
        dp_ref[:, D_RG:] = (dyy * h * dgelu).astype(BF16)

        a_s[...] = a
        b_s[...] = dyy * g
        rowi = lax.broadcasted_iota(jnp.int32, (8, D_RG), 0)

        def blk(jj, c):
            cd, ca = c
            for u in range(unroll):
                o = pl.multiple_of((hb - 1 - (jj * unroll + u)) * 8, 8)
                a_blk = a_s[pl.ds(o, 8), :]
                a_next = jnp.where(rowi == 7, ca, pltpu.roll(a_blk, 7, axis=0))
                A, B = _scan_block_bwd(a_next, b_s[pl.ds(o, 8), :], rowi)
                d = B + A * cd
                d_s[pl.ds(o, 8), :] = d
                cd, ca = d[0:1, :], a_blk[0:1, :]
            return cd, ca

        cd, ca = lax.fori_loop(0, hb // unroll, blk, (carry_d[...], carry_a[...]))
        carry_d[...] = cd
        carry_a[...] = ca
        delta = d_s[...]

        h_last_prev = jnp.where(first, 0.0, hh_ref[7:8, :])
        row0 = lax.broadcasted_iota(jnp.int32, (tm, 1), 0) == 0
        h_prev = jnp.where(row0, h_last_prev, pltpu.roll(h, 1, axis=0))
        dbx = jnp.where(valid, delta, 0.0)
        da = delta * h_prev
        di = dbx * m * xc
        dm = dbx * ig * xc
        dla = a * (da - dm * a * inv_m)
        dla = jnp.where(valid, dla, 0.0)
        glam_ref[...] += jnp.sum(dla * r, axis=0, keepdims=True) * (LRU_C / (1.0 + jnp.exp(lam_)))
        dr = (-LRU_C) * sp * dla
        dpre = jnp.concatenate([dr * r * (1.0 - r), di * ig * (1.0 - ig)], axis=1)
        gbg_ref[...] += jnp.sum(dpre, axis=0, keepdims=True)
        dpre_b = dpre.astype(BF16)
        gacc[...] += _dot_tn(xc.astype(BF16), dpre_b)
        dxc = dbx * m * ig + _dot_nt(dpre_b, w_ref[...])
        gcb_ref[...] += jnp.sum(dxc, axis=0, keepdims=True)
        dext[0:tm, :] = dxc
        xr = xg_ref[:, :D_RG]
        dxr = None
        for j in range(CONV_W):
            shifted = dext[3 - j:3 - j + tm, :]
            gcw_ref[j:j + 1, :] += jnp.sum(xr * shifted, axis=0, keepdims=True)
            tap = cw_ref[j:j + 1, :] * shifted
            dxr = tap if dxr is None else dxr + tap
        dp_ref[:, :D_RG] = dxr.astype(BF16)
        dext[tm:tm + 8, :] = dext[0:8, :]

        @pl.when(i == nt - 1)
        def _():
            fold = _head_fold()
            mask = _head_mask()
            fold_b = fold.astype(BF16)
            for k in range(2):
                blockdiag = jnp.where(mask, gacc[:, k * D_RG:(k + 1) * D_RG], 0.0)
                hi = blockdiag.astype(BF16)
                rest = blockdiag - hi.astype(F32)
                mid = rest.astype(BF16)
                lo = (rest - mid.astype(F32)).astype(BF16)
                gw_ref[k * D_RG:(k + 1) * D_RG, :] = (_dot(hi, fold_b) + _dot(mid, fold_b)) + _dot(lo, fold_b)

    vec = lambda n: pl.BlockSpec((1, n), lambda i: (0, 0))
    rev = lambda n: pl.BlockSpec((tm, n), lambda i: (nt - 1 - i, 0))
    halo = lambda n: pl.BlockSpec((8, n), lambda i: (jnp.maximum((nt - 1 - i) * hb - 1, 0), 0))
    return pl.pallas_call(
        body, grid=(nt,),
        in_specs=[rev(2 * D_RG), rev(D_RG), rev(D_RG), halo(D_RG), rev(D_RG), ANY,
                  pl.BlockSpec((CONV_W, D_RG), lambda i: (0, 0)), vec(D_RG),
                  pl.BlockSpec((D_RG, 2 * D_RG), lambda i: (0, 0)), vec(2 * D_RG), vec(D_RG), vec(D_RG)],
        out_specs=[rev(2 * D_RG), pl.BlockSpec((CONV_W, D_RG), lambda i: (0, 0)), vec(D_RG),
                   pl.BlockSpec((2 * D_RG, RG_HEAD_DIM), lambda i: (0, 0)), vec(2 * D_RG), vec(D_RG), vec(D_RG)],
        input_output_aliases={5: 0},
        out_shape=[jax.ShapeDtypeStruct((T, D_IN), BF16), jax.ShapeDtypeStruct((CONV_W, D_RG), F32),
                   jax.ShapeDtypeStruct((1, D_RG), F32), jax.ShapeDtypeStruct((2 * D_RG, RG_HEAD_DIM), F32),
                   jax.ShapeDtypeStruct((1, 2 * D_RG), F32), jax.ShapeDtypeStruct((1, D_RG), F32),
                   jax.ShapeDtypeStruct((1, D_RG), F32)],
        scratch_shapes=[pltpu.VMEM((tm + 8, D_RG), F32),
                        pltpu.VMEM((tm, D_RG), F32), pltpu.VMEM((tm, D_RG), F32), pltpu.VMEM((tm, D_RG), F32),
                        pltpu.VMEM((D_RG, 2 * D_RG), F32), pltpu.VMEM((1, D_RG), F32), pltpu.VMEM((1, D_RG), F32)],
        name="rg_bwd", compiler_params=_params("arbitrary"),
    )(p, xc_all, hs, hs, dy, dp, cw, cb, wg, bg, lam, rg_g)


def _hg_bwd(p, o_all, st_all, dy, lbraw, hg_g):
    T = p.shape[0]
    n_chunks = T // CHUNK
    cps = _chunks_per_step(n_chunks)
    rows = cps * CHUNK
    n_steps = n_chunks // cps

    def body(hq_ref, hf_ref, hi_ref, hg_ref, o_ref, st_ref, dy_ref, lb_ref, g_ref,
             dp_ref, glb_ref, gg_ref, dst):
        i = pl.program_id(0)

        @pl.when(i == 0)
        def _():
            dst[...] = jnp.zeros_like(dst)
            glb_ref[...] = jnp.zeros_like(glb_ref)
            gg_ref[...] = jnp.zeros_like(gg_ref)

        dp_ref[:, :2 * D_RG] = jnp.zeros((rows, 2 * D_RG), BF16)

        def chunk(jj, carry):
            j = cps - 1 - jj
            rs = pl.ds(pl.multiple_of(j * CHUNK, CHUNK), CHUNK)
            chunk_body((n_steps - 1 - i) * cps + j, hq_ref.at[rs, :], hf_ref.at[rs, :], hi_ref.at[rs, :],
                       hg_ref.at[rs, :], o_ref.at[rs, :], st_ref.at[pl.ds(j, 1)], dy_ref.at[rs, :], lb_ref, g_ref,
                       dp_ref.at[rs, pl.ds(2 * D_RG, 4 * D_HG)], glb_ref, gg_ref, dst)
            return carry

        lax.fori_loop(0, cps, chunk, 0, unroll=True)

    def chunk_body(n, hq_ref, hf_ref, hi_ref, hg_ref, o_ref, st_ref, dy_ref, lb_ref, g_ref,
                   dp_ref, glb_ref, gg_ref, dst):
        valid = (n * CHUNK + lax.broadcasted_iota(jnp.int32, (CHUNK, 1), 0)) >= PAD
        hq, hf, v, hg = hq_ref[...], hf_ref[...], hi_ref[...], hg_ref[...]
        lb, sq, q, sf, f, b = _hg_gates(hq, hf, lb_ref, valid)
        k = 1.0 - f
        causal = _causal()
        r_i = lax.broadcasted_iota(jnp.int32, (CHUNK, CHUNK), 0)
        c_i = lax.broadcasted_iota(jnp.int32, (CHUNK, CHUNK), 1)
        causal_t = r_i <= c_i
        is_last = lax.broadcasted_iota(jnp.int32, (CHUNK, 1), 0) == CHUNK - 1
        g_ = g_ref[...]
        db_parts, dq_parts, dk_parts = [], [], []
        gg = jnp.zeros((1, HG_HEAD_DIM), F32)
        heads = [slice(h * HG_HEAD_DIM, (h + 1) * HG_HEAD_DIM) for h in range(HG_HEADS)]

        do_parts = []
        for h, sl in enumerate(heads):
            o = o_ref[:, sl]
            ro = _rms(o)
            no = o * ro
            hgh = hg[:, sl]
            sg = _sigmoid(hgh)
            dyh = dy_ref[:, sl]
            dp_ref[:, 3 * D_HG + h * HG_HEAD_DIM:3 * D_HG + (h + 1) * HG_HEAD_DIM] = (
                dyh * no * g_ * sg * (1.0 + hgh * (1.0 - sg))).astype(BF16)
            dng = dyh * hgh * sg
            gg = gg + jnp.sum(dng * no, axis=0, keepdims=True)
            do_parts.append(_rms_bwd(dng * g_, no, ro))
        do_t = jnp.concatenate(do_parts, axis=1).T.astype(BF16)

        fac = []
        for sl, do in zip(heads, do_parts):
            qh, kh, bh = q[:, sl], k[:, sl], b[:, sl]
            b_last, eb, eq, ekh, ek, q_hat, k_til = _hg_head(qh, kh, bh)
            fac.append(dict(qh=qh, kh=kh, e_last=jnp.exp(b_last), eb=eb, eq=eq, ekh=ekh, ek=ek,
                            q_til=qh * eb, k_hat=kh * ekh, qhb=q_hat.astype(BF16), ktb=k_til.astype(BF16),
                            vb=v[:, sl].astype(BF16), dob=do.astype(BF16)))

        first = []
        for sl, t in zip(heads, fac):
            st_h = st_ref[0, sl, :]
            dst_h = dst[sl, :]
            dstb = dst_h.astype(BF16)
            first.append(dict(
                att_t=_dot_nt(t["ktb"], t["qhb"]), datt=_dot_nt(t["dob"], t["vb"]),
                datt_t=_dot_nt(t["vb"], t["dob"]), dk_hat=_dot(t["vb"], dstb),
                dv=_dot_nt(t["k_hat"].astype(BF16), dstb), dq_til=_dot(t["dob"], st_h.astype(BF16)),
                state=t["e_last"] * jnp.sum(dst_h * st_h, axis=0, keepdims=True)))
            dst[sl, :] = dst_h * t["e_last"] + _dot(do_t[sl, :], t["q_til"].astype(BF16))

        for h, (t, m) in enumerate(zip(fac, first)):
            qh, kh, eb, eq, ekh, ek = t["qh"], t["kh"], t["eb"], t["eq"], t["ekh"], t["ek"]
            q_til, k_hat, qhb, ktb, dob = t["q_til"], t["k_hat"], t["qhb"], t["ktb"], t["dob"]
            dk_hat, dq_til = m["dk_hat"], m["dq_til"]
            dv = m["dv"] + _dot(jnp.where(causal_t, m["att_t"], 0.0).astype(BF16), dob)
            dq_hat = _dot(jnp.where(causal, m["datt"], 0.0).astype(BF16), ktb)
            dk_til = _dot(jnp.where(causal_t, m["datt_t"], 0.0).astype(BF16), qhb)
            db_last = jnp.sum(dk_hat * k_hat, axis=0, keepdims=True) + m["state"]
            dq_sel = jnp.concatenate([dq_hat[SUB * s:SUB * (s + 1), s * HG_HEAD_DIM:(s + 1) * HG_HEAD_DIM]
                                      for s in range(N_SUB)], axis=0)
            dq_a = dq_sel * eq
            dk_a = dk_til[:, :HG_HEAD_DIM] * ek[0]
            for s in range(1, N_SUB):
                dk_a = dk_a + dk_til[:, s * HG_HEAD_DIM:(s + 1) * HG_HEAD_DIM] * ek[s]
            k_att = ktb.astype(F32) * dk_til
            db = dq_til * q_til - dk_hat * k_hat + (qh * eq).astype(BF16).astype(F32) * dq_sel
            for s in range(N_SUB):
                db = db - k_att[:, s * HG_HEAD_DIM:(s + 1) * HG_HEAD_DIM]
            db_parts.append(jnp.where(is_last, db + db_last, db))
            dq_parts.append(dq_til * eb + dq_a)
            dk_parts.append(dk_hat * ekh + dk_a)
            dp_ref[:, 2 * D_HG + h * HG_HEAD_DIM:2 * D_HG + (h + 1) * HG_HEAD_DIM] = dv.astype(BF16)

        gg_ref[...] += gg
        db = jnp.concatenate(db_parts, axis=1)
        dq = jnp.concatenate(dq_parts, axis=1)
        dk = jnp.concatenate(dk_parts, axis=1)
        dlf = jnp.where(valid, _running_sum(db, False), 0.0)
        dp_ref[:, :D_HG] = (dq * sq * (1.0 + hq * (1.0 - sq))).astype(BF16)
        df = dlf / f - dk
        dlb = jnp.sum(df * (1.0 - sf), axis=0, keepdims=True) * lb * (1.0 - lb)
        glb_ref[0:1, :] += dlb
        glb_ref[1:2, :] += -dlb
        dp_ref[:, D_HG:2 * D_HG] = (df * (1.0 - lb) * sf * (1.0 - sf)).astype(BF16)

    rev = lambda j: pl.BlockSpec((rows, D_HG), lambda i: (n_steps - 1 - i, j))
    return pl.pallas_call(
        body, grid=(n_steps,),
        in_specs=[rev(2), rev(3), rev(4), rev(5), rev(0),
                  pl.BlockSpec((cps, D_HG, HG_HEAD_DIM), lambda i: (n_steps - 1 - i, 0, 0)), rev(1),
                  pl.BlockSpec((2, D_HG), lambda i: (0, 0)), pl.BlockSpec((1, HG_HEAD_DIM), lambda i: (0, 0))],
        out_specs=[pl.BlockSpec((rows, D_IN), lambda i: (n_steps - 1 - i, 0)),
                   pl.BlockSpec((2, D_HG), lambda i: (0, 0)), pl.BlockSpec((1, HG_HEAD_DIM), lambda i: (0, 0))],
        out_shape=[jax.ShapeDtypeStruct((T, D_IN), BF16), jax.ShapeDtypeStruct((2, D_HG), F32),
                   jax.ShapeDtypeStruct((1, HG_HEAD_DIM), F32)],
        scratch_shapes=[pltpu.VMEM((D_HG, HG_HEAD_DIM), F32)],
        name="hg_bwd", compiler_params=_params("arbitrary"),
    )(p, p, p, p, o_all, st_all, dy, lbraw, hg_g)


def _in_bwd(dp, w_in, h0, g1, dh1):
    T = h0.shape[0]
    tm = _row_tile(T, 832)
    n_steps = T // tm

    def body(dp_ref, w_ref, h_ref, g_ref, d1_ref, gx_hbm, gmeta_ref, gg_ref, buf, sems):
        i = pl.program_id(0)
        first, later = _window_copies(gx_hbm, buf, sems, tm)
        slot = i % 2

        @pl.when(i == 0)
        def _():
            gg_ref[...] = jnp.zeros_like(gg_ref)

        if n_steps > 2:
            @pl.when(i == 2)
            def _():
                first(False).wait()

            @pl.when(i > 2)
            def _():
                later(i - 2, slot, False).wait()

        du = _dot_nt(dp_ref[...], w_ref[...])
        h0_ = h_ref[...]
        r = _rms(h0_)
        n = h0_ * r
        gg_ref[...] += jnp.sum(du * n, axis=0, keepdims=True)
        dh0 = d1_ref[...] + _rms_bwd(du * g_ref[...], n, r)
        buf[slot] = dh0

        @pl.when(i == 0)
        def _():
            gmeta_ref[...] = dh0[PAD:HEAD, :]
            first(False).start()

        if n_steps > 1:
            @pl.when(i > 0)
            def _():
                later(i, slot, False).start()

        @pl.when(i == n_steps - 1)
        def _():
            if n_steps == 1:
                first(False).wait()
            else:
                if n_steps == 2:
                    first(False).wait()
                else:
                    later(i - 1, 1 - slot, False).wait()
                later(i, slot, False).wait()

    row = lambda n: pl.BlockSpec((tm, n), lambda i: (i, 0))
    return pl.pallas_call(
        body, grid=(n_steps,),
        in_specs=[row(D_IN), _resident((D_MODEL, D_IN)),
                  row(D_MODEL), pl.BlockSpec((1, D_MODEL), lambda i: (0, 0)), row(D_MODEL)],
        out_specs=[pl.BlockSpec(memory_space=pl.ANY), pl.BlockSpec((N_META, D_MODEL), lambda i: (0, 0)),
                   pl.BlockSpec((1, D_MODEL), lambda i: (0, 0))],
        out_shape=[jax.ShapeDtypeStruct((T - HEAD, D_MODEL), F32), jax.ShapeDtypeStruct((N_META, D_MODEL), F32),
                   jax.ShapeDtypeStruct((1, D_MODEL), F32)],
        scratch_shapes=[pltpu.VMEM((2, tm, D_MODEL), F32), pltpu.SemaphoreType.DMA((2,))],
        name="in_bwd", compiler_params=_params("arbitrary"),
    )(dp, w_in, h0, g1, dh1)


def _col_tile(cols, target):
    best = None
    for t in range(128, min(cols, target) + 1, 128):
        if cols % t == 0:
            best = t
    assert best is not None, cols
    return best


MXU_DIM = 256


def _mxu_tile(cols, target):
    best = None
    for t in range(MXU_DIM, min(cols, target) + 1, MXU_DIM):
        if cols % t == 0:
            best = t
    assert best is not None, cols
    return best


def _weight_grad(a, b, name):
    T, M = a.shape
    N = b.shape[1]
    tm = _col_tile(M, 1408)
    tn = _mxu_tile(N, 768 if tm <= 1024 else 512)

    def body(a_ref, b_ref, o_ref):
        o_ref[...] = _dot_tn(a_ref[...], b_ref[...])

    return pl.pallas_call(
        body, grid=(M // tm, N // tn),
        in_specs=[pl.BlockSpec((T, tm), lambda m, n: (0, m)), pl.BlockSpec((T, tn), lambda m, n: (0, n))],
        out_specs=pl.BlockSpec((tm, tn), lambda m, n: (m, n)),
        out_shape=jax.ShapeDtypeStruct((M, N), F32),
        name=name, compiler_params=_params("parallel", "parallel"),
    )(a, b)


def _local_step(x, meta, target, w_in_own, w_in, w_out, w_gu, w_down, small, chip, on_ffn_grads=None,
                on_mixer_grads=None):
    wg = _gate_weights(small["w_rgate"], small["w_igate"])
    bg = jnp.concatenate([small["b_rgate"], small["b_igate"]], axis=1)

    p, u, h0 = _in_proj_local(x, meta, small["mix_norm_g"], w_in_own, chip)
    p = _in_proj_rest(u, w_in, p, chip)
    y_rg, hs, xc = _rg_fwd(p, small["conv_w"], small["conv_b"], wg, bg, small["lru_lambda"], small["rg_norm_g"])
    y_hg, o_all, st_all = _hg_fwd(p, small["hg_lower_bound"], small["hg_norm_g"])
    h1, v, yb, gu, act, dh2, dh2b, loss, g_final = _ffn_fwd(
        h0, y_rg, y_hg, w_out, small["ffn_norm_g"], w_gu, w_down, small["final_norm_g"], target)

    g_w_down = _weight_grad(act, dh2b, "grad_w_down")
    dgu, dh1, dh1b, dy, g_ffn = _ffn_bwd(dh2b, gu, w_down, w_gu, h1, small["ffn_norm_g"], dh2, w_out)
    ffn_grads = {"w_gate_up": _weight_grad(v, dgu, "grad_w_gate_up"), "w_down": g_w_down,
                 "w_out": _weight_grad(yb, dh1b, "grad_w_out")}
    stages = on_ffn_grads(ffn_grads) if on_ffn_grads is not None else None
    dp, g_lb, g_hgn = _hg_bwd(p, o_all, st_all, dy, small["hg_lower_bound"], small["hg_norm_g"])
    early = late = None
    if stages is not None:
        chip_sums, send = stages
        sums = chip_sums()
        (dp, dy), sums = lax.optimization_barrier(((dp, dy), sums))
        early = send(sums)
    dp, g_cw, g_cb, g_wgate, g_bg, g_lam, g_rgn = _rg_bwd(
        p, xc, hs, dy, dp, small["conv_w"], small["conv_b"], wg, bg, small["lru_lambda"], small["rg_norm_g"])
    mixer_grads = {"w_in": _weight_grad(u, dp, "grad_w_in")}
    if on_mixer_grads is not None:
        chip_sums, send = on_mixer_grads(mixer_grads)
        sums = chip_sums()
        (dp, dh1), sums = lax.optimization_barrier(((dp, dh1), sums))
        late = send(sums)
    grad_x, g_meta, g_mix = _in_bwd(dp, w_in, h0, small["mix_norm_g"], dh1)

    grads = {
        "w_in": mixer_grads["w_in"], "w_out": ffn_grads["w_out"],
        "w_gate_up": ffn_grads["w_gate_up"], "w_down": ffn_grads["w_down"],
        "meta_tokens": g_meta, "mix_norm_g": g_mix, "conv_w": g_cw, "conv_b": g_cb, "w_gates": g_wgate,
        "b_rgate": g_bg[:, :D_RG], "b_igate": g_bg[:, D_RG:], "lru_lambda": g_lam, "rg_norm_g": g_rgn,
        "hg_lower_bound": g_lb, "hg_norm_g": g_hgn, "ffn_norm_g": g_ffn, "final_norm_g": g_final,
    }
    return loss, grad_x, grads, early, late


ANY = pl.BlockSpec(memory_space=pl.ANY)
HALF = D_MODEL // 2

BIG = {"w_in": (D_MODEL, D_IN // N_CHIPS, True), "w_gate_up": (D_MODEL, 2 * D_FF // N_CHIPS, True),
       "w_out": (D_MODEL // N_CHIPS, D_MODEL, False), "w_down": (D_FF // N_CHIPS, D_MODEL, False)}
BIG_NAMES = tuple(BIG)
N_BIG = len(BIG_NAMES)


def _full_shape(name):
    rows, cols, by_col = BIG[name]
    return (rows, cols * N_CHIPS) if by_col else (rows * N_CHIPS, cols)


def _place():
    return lax.axis_index("x"), lax.axis_index("y"), lax.axis_index("c")


def _chip_of(x, y, r):
    fx, fy = (r + 1) >> 1, (r + 1) & 1
    return (1 - x if fx else x), (1 - y if fy else y)


def _half_of(ref, by_col, half):
    start = pl.multiple_of(half * HALF, 128)
    return ref.at[pl.ds(start, HALF), :] if by_col else ref.at[:, pl.ds(start, HALF)]


def _shard_of(ref, name, chip):
    rows, cols, by_col = BIG[name]
    if by_col:
        return ref.at[:, pl.ds(pl.multiple_of(chip * cols, 128), cols)]
    return ref.at[pl.ds(pl.multiple_of(chip * rows, 16), rows), :]


def _shard_half_of(ref, name, chip, half):
    rows, cols, by_col = BIG[name]
    start = pl.multiple_of(half * HALF, 128)
    if by_col:
        return ref.at[pl.ds(start, HALF), pl.ds(pl.multiple_of(chip * cols, 128), cols)]
    return ref.at[pl.ds(pl.multiple_of(chip * rows, 16), rows), pl.ds(start, HALF)]


def _remote(src, dst, send_sems, recv_sems, k, dev):
    return pltpu.make_async_remote_copy(src_ref=src, dst_ref=dst, send_sem=send_sems.at[k], recv_sem=recv_sems.at[k],
                                        device_id=dev, device_id_type=MESH)


def _place_shards(w, small, chip):
    steps = 4
    ns = len(small)
    in_specs, out_specs = [], []
    for name in BIG_NAMES:
        rows, cols, by_col = BIG[name]
        tr = rows // steps
        in_specs.append(pl.BlockSpec((tr, cols), lambda i, s: (i, 0)))
        if by_col:
            out_specs.append(pl.BlockSpec((tr, cols), lambda i, s: (i, s[0])))
        else:
            out_specs.append(pl.BlockSpec((tr, cols), lambda i, s: (s[0] * steps + i, 0)))

    def body(s_ref, *refs):
        ins, small_in = refs[:N_BIG], refs[N_BIG:N_BIG + ns]
        outs, small_out = refs[N_BIG + ns:2 * N_BIG + ns], refs[2 * N_BIG + ns:2 * (N_BIG + ns)]
        send_sems, recv_sems, local_sems = refs[2 * (N_BIG + ns):]
        i = pl.program_id(0)
        x, y, c = _place()
        chip_ = 2 * x + y
        others = [_chip_of(x, y, r) for r in range(3)]

        def block(a, q):
            cols = small[a].shape[1]
            return small_out[a].at[:, pl.ds(pl.multiple_of(q * cols, 128), cols)]

        def local(a):
            return pltpu.make_async_copy(small_in[a], block(a, chip_), local_sems.at[a])

        def remote(a, r):
            qx, qy = others[r]
            return _remote(small_in[a], block(a, chip_), send_sems, recv_sems, 3 * a + r, (qx, qy, c))

        @pl.when(i == 0)
        def _():
            for a in range(ns):
                local(a).start()
                for r in range(3):
                    remote(a, r).start()

        for a in range(N_BIG):
            outs[a][...] = ins[a][...].astype(BF16)

        @pl.when(i == steps - 1)
        def _():
            for a in range(ns):
                for r, (qx, qy) in enumerate(others):
                    landed = block(a, 2 * qx + qy)
                    _remote(landed, landed, send_sems, recv_sems, 3 * a + r, (qx, qy, c)).wait_recv()
                for r in range(3):
                    remote(a, r).wait_send()
                local(a).wait()

    out = pl.pallas_call(
        body,
        grid_spec=pltpu.PrefetchScalarGridSpec(
            num_scalar_prefetch=1, grid=(steps,), in_specs=in_specs + [ANY] * ns, out_specs=out_specs + [ANY] * ns,
            scratch_shapes=[pltpu.SemaphoreType.DMA((3 * ns,)), pltpu.SemaphoreType.DMA((3 * ns,)),
                            pltpu.SemaphoreType.DMA((ns,))]),
        out_shape=([jax.ShapeDtypeStruct(_full_shape(name), BF16) for name in BIG_NAMES]
                   + [jax.ShapeDtypeStruct((s.shape[0], s.shape[1] * N_CHIPS), F32) for s in small]),
        name="place_shards", compiler_params=_params("arbitrary"),
    )(chip, *[w[name] for name in BIG_NAMES], *small)
    return dict(zip(BIG_NAMES, out[:N_BIG])), list(out[N_BIG:])


def _gather_weights(placed, small, names, label, collective_id):
    n, ns = len(names), len(small)
    hbm = pltpu.MemorySpace.HBM
    outs = [jax.new_ref(placed[nm], memory_space=hbm) for nm in names]
    small_in = [jax.new_ref(s, memory_space=hbm) for s in small]
    small_out = [jax.empty_ref(jax.ShapeDtypeStruct((s.shape[0], s.shape[1] * N_CHIPS), F32), memory_space=hbm)
                 for s in small]
    n_sems = 6 * n + 3 * ns

    @pl.kernel(mesh=plsc.ScalarSubcoreMesh(axis_name="seq", num_cores=1), name=label, out_type=(),
               scratch_types=(pltpu.SemaphoreType.DMA((n_sems,)), pltpu.SemaphoreType.DMA((n_sems,)),
                              pltpu.SemaphoreType.DMA((max(ns, 1),))),
               compiler_params=pltpu.CompilerParams(collective_id=collective_id))
    def launch(send_sems, recv_sems, local_sems):
        x, y, c = _place()
        chip = 2 * x + y
        sibling = (x, y, 1 - c)
        others = [_chip_of(x, y, r) for r in range(3)]
        _handshake([(qx, qy, c) for qx, qy in others] + [sibling])

        def small_block(a, q):
            cols = small[a].shape[1]
            return small_out[a].at[:, pl.ds(pl.multiple_of(q * cols, 128), cols)]

        local = [pltpu.make_async_copy(small_in[a], small_block(a, chip), local_sems.at[a]) for a in range(ns)]
        for cp in local:
            cp.start()

        sends = []
        for a, name in enumerate(names):
            mine = _shard_half_of(outs[a], name, chip, c)
            for r, (qx, qy) in enumerate(others):
                sends.append(_remote(mine, mine, send_sems, recv_sems, 6 * a + r, (qx, qy, c)))
        for a in range(ns):
            for r, (qx, qy) in enumerate(others):
                sends.append(_remote(small_in[a], small_block(a, chip), send_sems, recv_sems,
                                     6 * n + 3 * a + r, (qx, qy, c)))
        for cp in sends:
            cp.start()

        forwards = []
        for a, name in enumerate(names):
            for r, (qx, qy) in enumerate(others):
                landed = _shard_half_of(outs[a], name, 2 * qx + qy, c)
                _remote(landed, landed, send_sems, recv_sems, 6 * a + r, (qx, qy, c)).wait_recv()
                fwd = _remote(landed, landed, send_sems, recv_sems, 6 * a + 3 + r, sibling)
                fwd.start()
                forwards.append(fwd)
        for a in range(ns):
            for r, (qx, qy) in enumerate(others):
                landed = small_block(a, 2 * qx + qy)
                _remote(landed, landed, send_sems, recv_sems, 6 * n + 3 * a + r, (qx, qy, c)).wait_recv()
        for a, name in enumerate(names):
            for r, (qx, qy) in enumerate(others):
                landed = _shard_half_of(outs[a], name, 2 * qx + qy, 1 - c)
                _remote(landed, landed, send_sems, recv_sems, 6 * a + 3 + r, sibling).wait_recv()
        for cp in sends + forwards:
            cp.wait_send()
        for cp in local:
            cp.wait()

    launch()
    return {nm: ref[...] for nm, ref in zip(names, outs)}, [ref[...] for ref in small_out]


def _exchange_halves(grads, names, label, collective_id):
    n = len(names)
    sequencer = collective_id is not None

    def body(*refs):
        ins, outs = refs[:n], refs[n:2 * n]
        send_sems, recv_sems = refs[2 * n:]
        x, y, c = _place()
        if sequencer:
            _handshake([(x, y, 1 - c)])
        copies = []
        for a, name in enumerate(names):
            copies.append(_remote(_half_of(ins[a], BIG[name][2], 1 - c), outs[a], send_sems, recv_sems, a,
                                  (x, y, 1 - c)))
        for cp in copies:
            cp.start()
        for cp in copies:
            cp.wait()

    def half_shape(name):
        r, c_ = _full_shape(name)
        return (HALF, c_) if BIG[name][2] else (r, HALF)

    out_type = tuple(jax.ShapeDtypeStruct(half_shape(nm), F32) for nm in names)
    sems = (pltpu.SemaphoreType.DMA((n,)), pltpu.SemaphoreType.DMA((n,)))
    operands = [grads[nm] for nm in names]
    if sequencer:
        got = pl.kernel(
            body, mesh=plsc.ScalarSubcoreMesh(axis_name="seq", num_cores=1), name=label, out_type=out_type,
            scratch_types=sems, compiler_params=pltpu.CompilerParams(collective_id=collective_id),
        )(*operands)
    else:
        got = pl.pallas_call(
            body, in_specs=[ANY] * n, out_specs=[ANY] * n, out_shape=list(out_type), scratch_shapes=list(sems),
            name=label,
        )(*operands)
    return dict(zip(names, got))


def _chip_sum(grads, got, names, core, label):
    n = len(names)
    steps = 4
    g_specs, blks = [], []
    for name in names:
        rows, cols = got[name].shape
        tr = rows // steps
        if BIG[name][2]:
            g_specs.append(pl.BlockSpec((tr, cols), lambda i, s: (s[0] * steps + i, 0)))
        else:
            g_specs.append(pl.BlockSpec((tr, HALF), lambda i, s: (i, s[0])))
        blks.append(pl.BlockSpec((tr, cols), lambda i, s: (i, 0)))

    def body(s_ref, *refs):
        for a in range(n):
            t = refs[a][...] + refs[n + a][...]
            refs[2 * n + a][...] = t
            refs[3 * n + a][...] = t.astype(BF16)

    out = pl.pallas_call(
        body,
        grid_spec=pltpu.PrefetchScalarGridSpec(num_scalar_prefetch=1, grid=(steps,), in_specs=g_specs + blks,
                                               out_specs=blks + blks),
        out_shape=([jax.ShapeDtypeStruct(got[nm].shape, F32) for nm in names]
                   + [jax.ShapeDtypeStruct(got[nm].shape, BF16) for nm in names]),
        name=label, compiler_params=_params("parallel"),
    )(core, *[grads[nm] for nm in names], *[got[nm] for nm in names])
    return {nm: (out[a], out[n + a]) for a, nm in enumerate(names)}


def _piece_shape(name):
    rows, cols, by_col = BIG[name]
    return (HALF, cols) if by_col else (rows, HALF)


def _handshake(peers):
    barrier = pltpu.get_barrier_semaphore()
    for peer in peers:
        pl.semaphore_signal(barrier, inc=1, device_id=peer, device_id_type=MESH)
    pl.semaphore_wait(barrier, len(peers))


def _send_chip_sums(sums, names, label, collective_id):
    n = len(names)

    def body(*refs):
        ins, outs = refs[:n], refs[n:2 * n]
        send_sems, recv_sems = refs[2 * n:]
        x, y, c = _place()
        others = [_chip_of(x, y, r) for r in range(3)]
        _handshake([(qx, qy, c) for qx, qy in others])
        copies = []
        for a, name in enumerate(names):
            for r, (qx, qy) in enumerate(others):
                copies.append(_remote(_shard_of(ins[a], name, 2 * qx + qy), outs[a].at[r], send_sems, recv_sems,
                                      3 * a + r, (qx, qy, c)))
        for cp in copies:
            cp.start()
        for cp in copies:
            cp.wait()

    return pl.kernel(
        body, mesh=plsc.ScalarSubcoreMesh(axis_name="seq", num_cores=1), name=label,
        out_type=tuple(jax.ShapeDtypeStruct((3,) + _piece_shape(nm), BF16) for nm in names),
        scratch_types=(pltpu.SemaphoreType.DMA((3 * n,)), pltpu.SemaphoreType.DMA((3 * n,))),
        compiler_params=pltpu.CompilerParams(collective_id=collective_id),
    )(*[sums[nm] for nm in names])


def _total(parts, chip_core):
    steps = 2
    in_specs, out_specs, operands = [], [], []
    for name in BIG_NAMES:
        by_col = BIG[name][2]
        pr, pc = _piece_shape(name)
        tr = pr // steps
        if by_col:
            in_specs.append(pl.BlockSpec((tr, pc), lambda i, s: (i, s[0])))
            out_specs.append(pl.BlockSpec((tr, pc), lambda i, s: (s[1] * steps + i, 0)))
        else:
            in_specs.append(pl.BlockSpec((tr, pc), lambda i, s: (s[0] * steps + i, 0)))
            out_specs.append(pl.BlockSpec((tr, pc), lambda i, s: (i, s[1])))
        for r in range(3):
            in_specs.append(pl.BlockSpec((None, tr, pc), lambda i, s, r=r: (r, i, 0)))
        own, got = parts[name]
        operands += [own, got, got, got]

    def body(s_ref, *refs):
        for a in range(N_BIG):
            o_ref, a_ref, b_ref, c_ref = refs[4 * a:4 * a + 4]
            refs[4 * N_BIG + a][...] = (((o_ref[...] + a_ref[...].astype(F32)) + b_ref[...].astype(F32))
                                        + c_ref[...].astype(F32))

    totals = pl.pallas_call(
        body,
        grid_spec=pltpu.PrefetchScalarGridSpec(num_scalar_prefetch=1, grid=(steps,), in_specs=in_specs,
                                               out_specs=out_specs),
        out_shape=[jax.ShapeDtypeStruct(BIG[name][:2], F32) for name in BIG_NAMES],
        name="totals", compiler_params=_params("parallel"),
    )(chip_core, *operands)
    return dict(zip(BIG_NAMES, totals))


VEC_ROWS = 32
VEC_ROW = {"mix_norm_g": 0, "conv_b": 1, "b_rgate": 2, "b_igate": 3, "lru_lambda": 4, "rg_norm_g": 5,
           "hg_lower_bound": 6, "hg_norm_g": 8, "ffn_norm_g": 9, "final_norm_g": 10, "loss": 11,
           "conv_w": 12, "meta_tokens": 16}
N_DEV = 8


def _all_reduce_small(pieces, gates, totals):
    names = list(pieces)
    n_small = 10
    hv, hg = VEC_ROWS // 2, gates.shape[0] // 2

    def body(*refs):
        ins = refs[:len(names)]
        g_ref = refs[len(names)]
        vec_ref, gsum_ref = refs[len(names) + 1 + N_BIG:len(names) + 3 + N_BIG]
        big = refs[len(names) + 3 + N_BIG:len(names) + 3 + 2 * N_BIG]
        (mine_v, sib_v, sib_g, chip_v, chip_g, got_v, got_g, send_sems, recv_sems) = refs[len(names) + 3 + 2 * N_BIG:]
        x, y, c = _place()
        chip = 2 * x + y
        sibling = (x, y, 1 - c)
        share = []
        for a, name in enumerate(BIG_NAMES):
            half = _half_of(big[a], BIG[name][2], c)
            share.append(_remote(half, half, send_sems, recv_sems, n_small + a, sibling))
        for cp in share:
            cp.start()
        mine_v[...] = jnp.zeros_like(mine_v)
        for name, ref in zip(names, ins):
            nr, w = ref.shape
            mine_v[VEC_ROW[name]:VEC_ROW[name] + nr, 0:w] = ref[...]

        swap = [_remote(mine_v, sib_v, send_sems, recv_sems, 0, sibling),
                _remote(g_ref, sib_g, send_sems, recv_sems, 1, sibling)]
        for cp in swap:
            cp.start()
        for cp in swap:
            cp.wait()
        chip_v[...] = mine_v[...] + sib_v[...]
        chip_g[...] = g_ref[...] + sib_g[...]

        rows_v = pl.ds(pl.multiple_of(c * hv, 8), hv)
        rows_g = pl.ds(pl.multiple_of(c * hg, 8), hg)
        got_v[chip] = chip_v[rows_v, :]
        got_g[chip] = chip_g[rows_g, :]
        sends = []
        for r in range(3):
            qx, qy = _chip_of(x, y, r)
            sends.append(_remote(chip_v.at[rows_v, :], got_v.at[chip], send_sems, recv_sems, 2 + r, (qx, qy, c)))
            sends.append(_remote(chip_g.at[rows_g, :], got_g.at[chip], send_sems, recv_sems, 5 + r, (qx, qy, c)))
        for cp in sends:
            cp.start()
        for cp in sends:
            cp.wait()
        vec_ref[rows_v, :] = ((got_v[0] + got_v[1]) + got_v[2]) + got_v[3]
        gsum_ref[rows_g, :] = ((got_g[0] + got_g[1]) + got_g[2]) + got_g[3]

        back = [_remote(vec_ref.at[rows_v, :], vec_ref.at[rows_v, :], send_sems, recv_sems, 8, sibling),
                _remote(gsum_ref.at[rows_g, :], gsum_ref.at[rows_g, :], send_sems, recv_sems, 9, sibling)]
        for cp in back:
            cp.start()
        theirs_v = vec_ref.at[pl.ds(pl.multiple_of((1 - c) * hv, 8), hv), :]
        theirs_g = gsum_ref.at[pl.ds(pl.multiple_of((1 - c) * hg, 8), hg), :]
        _remote(theirs_v, theirs_v, send_sems, recv_sems, 8, sibling).wait_recv()
        _remote(theirs_g, theirs_g, send_sems, recv_sems, 9, sibling).wait_recv()
        for cp in back:
            cp.wait_send()
        for a, name in enumerate(BIG_NAMES):
            theirs = _half_of(big[a], BIG[name][2], 1 - c)
            _remote(theirs, theirs, send_sems, recv_sems, n_small + a, sibling).wait_recv()
        for cp in share:
            cp.wait_send()

    vmem = pl.BlockSpec(memory_space=pltpu.VMEM)
    n_sems = n_small + N_BIG
    out = pl.pallas_call(
        body, in_specs=[vmem] * (len(names) + 1) + [ANY] * N_BIG, out_specs=[vmem, vmem] + [ANY] * N_BIG,
        out_shape=([jax.ShapeDtypeStruct((VEC_ROWS, D_MODEL), F32), jax.ShapeDtypeStruct(gates.shape, F32)]
                   + [jax.ShapeDtypeStruct(BIG[n][:2], F32) for n in BIG_NAMES]),
        input_output_aliases={len(names) + 1 + a: 2 + a for a in range(N_BIG)},
        scratch_shapes=[pltpu.VMEM((VEC_ROWS, D_MODEL), F32), pltpu.VMEM((VEC_ROWS, D_MODEL), F32),
                        pltpu.VMEM(gates.shape, F32), pltpu.VMEM((VEC_ROWS, D_MODEL), F32),
                        pltpu.VMEM(gates.shape, F32), pltpu.VMEM((N_CHIPS, hv, D_MODEL), F32),
                        pltpu.VMEM((N_CHIPS, hg) + gates.shape[1:], F32),
                        pltpu.SemaphoreType.DMA((n_sems,)), pltpu.SemaphoreType.DMA((n_sems,))],
        name="all_reduce_small",
    )(*[pieces[n] for n in names], gates, *[totals[n] for n in BIG_NAMES])
    return out[0], out[1], dict(zip(BIG_NAMES, out[2:]))


def _adamw_math(w, g, m, v):
    m = ADAM_B1 * m + (1.0 - ADAM_B1) * g
    v = ADAM_B2 * v + (1.0 - ADAM_B2) * (g * g)
    m_hat = m / (1.0 - ADAM_B1 ** ADAM_STEP)
    v_hat = v / (1.0 - ADAM_B2 ** ADAM_STEP)
    delta = -ADAM_LR * (m_hat / (jnp.sqrt(v_hat) + ADAM_EPS) + ADAM_WD * w)
    return delta, m, v


def _adamw_big(w, g, m, v):
    steps = 8
    blks = []
    for name in BIG_NAMES:
        rows, cols, _ = BIG[name]
        blks.append(pl.BlockSpec((rows // steps, cols), lambda i: (i, 0)))

    def body(*refs):
        ins, outs = refs[:4 * N_BIG], refs[4 * N_BIG:]
        for a in range(N_BIG):
            w_ref, g_ref, m_ref, v_ref = (ins[k * N_BIG + a] for k in range(4))
            g = g_ref[...]
            d, nm, nv = _adamw_math(w_ref[...], g, m_ref[...], v_ref[...])
            outs[a][...] = g
            outs[N_BIG + a][...] = d
            outs[2 * N_BIG + a][...] = nm
            outs[3 * N_BIG + a][...] = nv

    shapes = [jax.ShapeDtypeStruct(BIG[name][:2], F32) for name in BIG_NAMES]
    out = pl.pallas_call(
        body, grid=(steps,), in_specs=blks * 4, out_specs=blks * 4, out_shape=shapes * 4,
        name="adamw_big", compiler_params=_params("parallel"),
    )(*[t[name] for t in (w, g, m, v) for name in BIG_NAMES])
    return {name: tuple(out[k * N_BIG + a] for k in range(4)) for a, name in enumerate(BIG_NAMES)}


SMALL = {"meta_tokens": (N_META, D_MODEL // N_CHIPS), "mix_norm_g": (1, D_MODEL), "conv_w": (CONV_W, D_RG // N_CHIPS),
         "conv_b": (1, D_RG), "w_rgate": (D_RG, RG_HEAD_DIM), "b_rgate": (1, D_RG), "w_igate": (D_RG, RG_HEAD_DIM),
         "b_igate": (1, D_RG), "lru_lambda": (1, D_RG), "rg_norm_g": (1, D_RG), "hg_lower_bound": (2, D_HG),
         "hg_norm_g": (1, HG_HEAD_DIM), "ffn_norm_g": (1, D_MODEL), "final_norm_g": (1, D_MODEL)}
SMALL_NAMES = tuple(SMALL)
SHARDED_SMALL = ("meta_tokens", "conv_w")


def _adamw_small(vec, gates, w, m, v):
    n = len(SMALL_NAMES)

    def body(*refs):
        vec_ref, gates_ref = refs[:2]
        w_refs, m_refs, v_refs = refs[2:2 + n], refs[2 + n:2 + 2 * n], refs[2 + 2 * n:2 + 3 * n]
        outs = refs[2 + 3 * n:]
        loss_ref = outs[0]
        x, y, _ = _place()
        chip = 2 * x + y
        loss_ref[...] = vec_ref[VEC_ROW["loss"]:VEC_ROW["loss"] + 1, 0:1]

        def update(k, g):
            g_ref, d_ref, nm_ref, nv_ref = outs[1 + 4 * k:5 + 4 * k]
            g_ref[...] = g
            d_ref[...], nm_ref[...], nv_ref[...] = _adamw_math(w_refs[k][...], g, m_refs[k][...], v_refs[k][...])

        for k, name in enumerate(SMALL_NAMES):
            nr, w_ = SMALL[name]
            if name == "w_rgate":
                update(k, gates_ref[0:D_RG, :])
            elif name == "w_igate":
                update(k, gates_ref[D_RG:2 * D_RG, :])
            elif name in SHARDED_SMALL:
                r0 = VEC_ROW[name]
                for q in range(N_CHIPS):
                    @pl.when(chip == q)
                    def _(k=k, r0=r0, nr=nr, w_=w_, q=q):
                        update(k, vec_ref[r0:r0 + nr, q * w_:(q + 1) * w_])
            else:
                r0 = VEC_ROW[name]
                update(k, vec_ref[r0:r0 + nr, 0:w_])

    vmem = pl.BlockSpec(memory_space=pltpu.VMEM)
    out_shape = [jax.ShapeDtypeStruct((1, 1), F32)]
    for name in SMALL_NAMES:
        out_shape += [jax.ShapeDtypeStruct(SMALL[name], F32)] * 4
    outs = pl.pallas_call(
        body, in_specs=[vmem] * (2 + 3 * n), out_specs=[vmem] * len(out_shape), out_shape=out_shape,
        name="adamw_small",
    )(vec, gates, *[w[k] for k in SMALL_NAMES], *[m[k] for k in SMALL_NAMES], *[v[k] for k in SMALL_NAMES])
    loss = outs[0]
    res = {name: tuple(outs[1 + 4 * k:5 + 4 * k]) for k, name in enumerate(SMALL_NAMES)}
    return loss, res


WEIGHT_NAMES = ("meta_tokens", "mix_norm_g", "w_in", "conv_w", "conv_b", "w_rgate", "b_rgate", "w_igate", "b_igate",
                "lru_lambda", "rg_norm_g", "hg_lower_bound", "hg_norm_g", "w_out", "ffn_norm_g", "w_gate_up", "w_down",
                "final_norm_g")


def _to_2d(name, a):
    if name in BIG:
        return a.reshape(BIG[name][:2])
    return a.reshape(SMALL[name])


def kernel(x, meta_tokens, mix_norm_g, w_in, conv_w, conv_b, w_rgate, b_rgate, w_igate, b_igate, lru_lambda, rg_norm_g, hg_lower_bound, hg_norm_g, w_out, ffn_norm_g, w_gate_up, w_down, final_norm_g, loss_target, m_meta_tokens, m_mix_norm_g, m_w_in, m_conv_w, m_conv_b, m_w_rgate, m_b_rgate, m_w_igate, m_b_igate, m_lru_lambda, m_rg_norm_g, m_hg_lower_bound, m_hg_norm_g, m_w_out, m_ffn_norm_g, m_w_gate_up, m_w_down, m_final_norm_g, v_meta_tokens, v_mix_norm_g, v_w_in, v_conv_w, v_conv_b, v_w_rgate, v_b_rgate, v_w_igate, v_b_igate, v_lru_lambda, v_rg_norm_g, v_hg_lower_bound, v_hg_norm_g, v_w_out, v_ffn_norm_g, v_w_gate_up, v_w_down, v_final_norm_g):
    w_raw = dict(zip(WEIGHT_NAMES, (meta_tokens, mix_norm_g, w_in, conv_w, conv_b, w_rgate, b_rgate, w_igate, b_igate,
                                    lru_lambda, rg_norm_g, hg_lower_bound, hg_norm_g, w_out, ffn_norm_g, w_gate_up,
                                    w_down, final_norm_g)))
    m_raw = dict(zip(WEIGHT_NAMES, (m_meta_tokens, m_mix_norm_g, m_w_in, m_conv_w, m_conv_b, m_w_rgate, m_b_rgate,
                                    m_w_igate, m_b_igate, m_lru_lambda, m_rg_norm_g, m_hg_lower_bound, m_hg_norm_g,
                                    m_w_out, m_ffn_norm_g, m_w_gate_up, m_w_down, m_final_norm_g)))
    v_raw = dict(zip(WEIGHT_NAMES, (v_meta_tokens, v_mix_norm_g, v_w_in, v_conv_w, v_conv_b, v_w_rgate, v_b_rgate,
                                    v_w_igate, v_b_igate, v_lru_lambda, v_rg_norm_g, v_hg_lower_bound, v_hg_norm_g,
                                    v_w_out, v_ffn_norm_g, v_w_gate_up, v_w_down, v_final_norm_g)))
    w = {k: _to_2d(k, a) for k, a in w_raw.items()}
    m = {k: _to_2d(k, a) for k, a in m_raw.items()}
    v = {k: _to_2d(k, a) for k, a in v_raw.items()}

    x_i, y_i, c_i = _place()
    core = jnp.reshape(c_i, (1,)).astype(jnp.int32)
    chip = jnp.reshape(2 * x_i + y_i, (1,)).astype(jnp.int32)
    chip_core = jnp.concatenate([chip, core])

    placed, (meta_full, cw_full) = _place_shards(w, [w["meta_tokens"], w["conv_w"]], chip)
    first, _ = _gather_weights(placed, [], ("w_in",), "gather_first", 1)
    rest, _ = _gather_weights(placed, [], ("w_out", "w_gate_up", "w_down"), "gather_rest", 2)
    full = {**first, **rest}

    seq = x.shape[1]
    small ={k: w[k] for k in SMALL_NAMES if k not in SHARDED_SMALL}
    small["conv_w"] = cw_full

    def reduce_to_chips(grads, names, tag, collective_ids):
        got = _exchange_halves(grads, names, "exchange_halves_" + tag, collective_ids[0])

        def chip_sums():
            return _chip_sum(grads, got, names, core, "chip_sum_" + tag)

        def send(sums):
            arrived = _send_chip_sums({n: sums[n][1] for n in names}, names, "send_chip_sums_" + tag,
                                      collective_ids[1])
            return {n: (sums[n][0], a) for n, a in zip(names, arrived)}

        return chip_sums, send

    ffn_names, mixer_names = ("w_gate_up", "w_down", "w_out"), ("w_in",)
    loss, grad_x, grads, parts, parts_mixer = _local_step(
        x.reshape(seq, D_MODEL), meta_full, loss_target.reshape(seq, D_MODEL),
        w["w_in"], full["w_in"], full["w_out"], full["w_gate_up"], full["w_down"], small, chip,
        on_ffn_grads=lambda g: reduce_to_chips(g, ffn_names, "ffn", (3, 4)),
        on_mixer_grads=lambda g: reduce_to_chips(g, mixer_names, "mixer", (None, 5)))
    parts.update(parts_mixer)
    totals = _total(parts, chip_core)
    pieces = {k: grads[k] for k in VEC_ROW if k != "loss"}
    pieces["loss"] = loss
    vec, gates, g_big = _all_reduce_small(pieces, grads["w_gates"], totals)
    loss_sum, res = _adamw_small(vec, gates, w, m, v)
    res.update(_adamw_big(w, g_big, m, v))

    out = [loss_sum.reshape(()), grad_x.reshape(1, seq, D_MODEL)]
    for j in range(4):
        out += [res[n][j].reshape(w_raw[n].shape) for n in WEIGHT_NAMES]
    return tuple(out)
```

```python
import math

import jax
import jax.numpy as jnp
from jax import lax
from jax.experimental import pallas as pl
from jax.experimental.pallas import tpu as pltpu
from jax.experimental.pallas import tpu_sc as plsc

F32 = jnp.float32
BF16 = jnp.bfloat16
MESH = pl.DeviceIdType.MESH

D_MODEL = 1024
D_RG = 512
RG_HEAD_DIM = 64
D_HG = 512
HG_HEAD_DIM = 128
HG_HEADS = 4
CHUNK = 64
SUB = 16
N_SUB = CHUNK // SUB
N_META = 16
PAD = CHUNK - N_META
D_IN = 3072
D_FF = 2816
CONV_W = 4
LRU_C = 8.0
EPS = 1e-6
EXP_CLAMP = 80.0
GELU_C = math.sqrt(2.0 / math.pi)
GELU_A = 0.044715
N_CHIPS = 4

ADAM_LR = 0.001
ADAM_B1 = 0.9
ADAM_B2 = 0.999
ADAM_EPS = 1e-08
ADAM_WD = 0.01
ADAM_STEP = 10

VMEM_LIMIT = 56 * 1024 * 1024


def _params(*sem):
    return pltpu.CompilerParams(dimension_semantics=sem, vmem_limit_bytes=VMEM_LIMIT)


def _row_tile(rows, target):
    best = None
    for t in range(16, min(rows, target) + 1, 16):
        if rows % t == 0:
            best = t
    assert best is not None, rows
    return best


def _sigmoid(x):
    return 0.5 * jnp.tanh(0.5 * x) + 0.5


def _dot(a, b):
    return jnp.dot(a, b, preferred_element_type=F32)


def _dot_nt(a, b):
    return lax.dot_general(a, b, (((1,), (1,)), ((), ())), preferred_element_type=F32)


def _dot_tn(a, b):
    return lax.dot_general(a, b, (((0,), (0,)), ((), ())), preferred_element_type=F32)


def _rms(x):
    return lax.rsqrt(jnp.mean(x * x, axis=-1, keepdims=True) + EPS)


def _rms_bwd(dn, n, r):
    return r * (dn - n * jnp.mean(dn * n, axis=-1, keepdims=True))


def _gelu_parts(x):
    t = jnp.tanh(GELU_C * (x + GELU_A * x * x * x))
    g = 0.5 * x * (1.0 + t)
    dg = 0.5 * (1.0 + t) + 0.5 * x * (1.0 - t * t) * GELU_C * (1.0 + 3.0 * GELU_A * x * x)
    return g, dg


def _softplus_neg(lam):
    e = jnp.exp(-jnp.abs(lam))
    w = 1.0 + e
    log1p = jnp.where(w == 1.0, e, jnp.log(w) * e / (w - 1.0))
    return jnp.maximum(-lam, 0.0) + log1p


def _head_mask():
    r = lax.broadcasted_iota(jnp.int32, (D_RG, D_RG), 0) // RG_HEAD_DIM
    c = lax.broadcasted_iota(jnp.int32, (D_RG, D_RG), 1) // RG_HEAD_DIM
    return r == c


def _head_fold():
    r = lax.broadcasted_iota(jnp.int32, (D_RG, RG_HEAD_DIM), 0) % RG_HEAD_DIM
    c = lax.broadcasted_iota(jnp.int32, (D_RG, RG_HEAD_DIM), 1)
    return (r == c).astype(F32)


def _gate_weights(w_r, w_i):
    def body(wr_ref, wi_ref, o_ref):
        fold = _head_fold()
        mask = _head_mask()
        for k, ref in enumerate((wr_ref, wi_ref)):
            full = _dot_nt(ref[...].astype(BF16), fold.astype(BF16))
            o_ref[:, k * D_RG:(k + 1) * D_RG] = jnp.where(mask, full, 0.0).astype(BF16)

    return pl.pallas_call(
        body, out_shape=jax.ShapeDtypeStruct((D_RG, 2 * D_RG), BF16), name="gate_weights",
    )(w_r, w_i)


HEAD = PAD + N_META


def _window_copies(seq_hbm, buf, sems, tm):
    def first(to_vmem):
        seq, vm = seq_hbm.at[pl.ds(0, tm - HEAD)], buf.at[0, pl.ds(HEAD, tm - HEAD)]
        return pltpu.make_async_copy(seq, vm, sems.at[0]) if to_vmem else pltpu.make_async_copy(vm, seq, sems.at[0])

    def later(j, slot, to_vmem):
        seq, vm = seq_hbm.at[pl.ds(pl.multiple_of(j * tm - HEAD, 8), tm)], buf.at[slot]
        if to_vmem:
            return pltpu.make_async_copy(seq, vm, sems.at[slot])
        return pltpu.make_async_copy(vm, seq, sems.at[slot])

    return first, later


def _fetch_window(seq_hbm, buf, sems, i, n_steps, tm):
    first, later = _window_copies(seq_hbm, buf, sems, tm)
    slot = i % 2

    @pl.when(i == 0)
    def _():
        first(True).start()

    if n_steps > 1:
        @pl.when(i + 1 < n_steps)
        def _():
            later(i + 1, 1 - slot, True).start()

    @pl.when(i == 0)
    def _():
        first(True).wait()

    if n_steps > 1:
        @pl.when(i > 0)
        def _():
            later(i, slot, True).wait()

    return slot


def _in_proj_local(x, meta, g1, w_own, chip):
    T = x.shape[0] + HEAD
    tm = _row_tile(T, 832)
    n_steps = T // tm
    cols = BIG["w_in"][1]

    def body(s_ref, x_hbm, meta_ref, g_ref, w_ref, p_ref, u_ref, h_ref, buf, sems, wb):
        i = pl.program_id(0)
        slot = _fetch_window(x_hbm, buf, sems, i, n_steps, tm)

        @pl.when(i == 0)
        def _():
            buf[0, 0:PAD, :] = jnp.zeros((PAD, D_MODEL), F32)
            buf[0, PAD:HEAD, :] = meta_ref[...]
            wb[...] = w_ref[...].astype(BF16)

        h = buf[slot]
        h_ref[...] = h
        u = (h * _rms(h) * g_ref[...]).astype(BF16)
        u_ref[...] = u
        p_ref[...] = _dot(u, wb[...])

    return pl.pallas_call(
        body,
        grid_spec=pltpu.PrefetchScalarGridSpec(
            num_scalar_prefetch=1, grid=(n_steps,),
            in_specs=[pl.BlockSpec(memory_space=pl.ANY),
                      pl.BlockSpec((N_META, D_MODEL), lambda i, s: (0, 0)),
                      pl.BlockSpec((1, D_MODEL), lambda i, s: (0, 0)),
                      pl.BlockSpec((D_MODEL, cols), lambda i, s: (0, 0))],
            out_specs=[pl.BlockSpec((tm, cols), lambda i, s: (i, s[0])),
                       pl.BlockSpec((tm, D_MODEL), lambda i, s: (i, 0)),
                       pl.BlockSpec((tm, D_MODEL), lambda i, s: (i, 0))],
            scratch_shapes=[pltpu.VMEM((2, tm, D_MODEL), F32), pltpu.SemaphoreType.DMA((2,)),
                            pltpu.VMEM((D_MODEL, cols), BF16)]),
        out_shape=[jax.ShapeDtypeStruct((T, D_IN), F32), jax.ShapeDtypeStruct((T, D_MODEL), BF16),
                   jax.ShapeDtypeStruct((T, D_MODEL), F32)],
        name="in_proj_local", compiler_params=_params("arbitrary"),
    )(chip, x, meta, g1, w_own)


def _in_proj_rest(u, w_in, p, chip):
    T = u.shape[0]
    tm = _row_tile(T, 2080)
    cols = BIG["w_in"][1]
    block = lambda j, s: (s[0] + 1 + j) % N_CHIPS

    def body(s_ref, u_ref, w_ref, p_in_ref, p_ref):
        p_ref[...] = _dot(u_ref[...], w_ref[...])

    return pl.pallas_call(
        body,
        grid_spec=pltpu.PrefetchScalarGridSpec(
            num_scalar_prefetch=1, grid=(N_CHIPS - 1, T // tm),
            in_specs=[pl.BlockSpec((tm, D_MODEL), lambda j, i, s: (i, 0)),
                      pl.BlockSpec((D_MODEL, cols), lambda j, i, s: (0, block(j, s))), ANY],
            out_specs=pl.BlockSpec((tm, cols), lambda j, i, s: (i, block(j, s)))),
        out_shape=jax.ShapeDtypeStruct((T, D_IN), F32),
        input_output_aliases={3: 0},
        name="in_proj_rest", compiler_params=_params("arbitrary", "arbitrary"),
    )(chip, u, w_in, p)


def _scan_block_fwd(A, B, rowi):
    for d in (1, 2, 4):
        a_sh = pltpu.roll(A, d, axis=0)
        b_sh = pltpu.roll(B, d, axis=0)
        m = rowi >= d
        B = jnp.where(m, A * b_sh + B, B)
        A = jnp.where(m, A * a_sh, A)
    return A, B


def _scan_block_bwd(A, B, rowi):
    for d in (1, 2, 4):
        a_sh = pltpu.roll(A, 8 - d, axis=0)
        b_sh = pltpu.roll(B, 8 - d, axis=0)
        m = rowi < 8 - d
        B = jnp.where(m, A * b_sh + B, B)
        A = jnp.where(m, A * a_sh, A)
    return A, B


def _rg_gates(xc, w_ref, bg_ref, lam):
    pre = _dot(xc.astype(BF16), w_ref[...]) + bg_ref[...]
    r = _sigmoid(pre[:, :D_RG])
    ig = _sigmoid(pre[:, D_RG:])
    sp = _softplus_neg(lam)
    la = -LRU_C * sp * r
    a = jnp.exp(la)
    th = jnp.tanh(la)
    u = 1.0 - th
    rc = pl.reciprocal(u, approx=True)
    rc = rc * (2.0 - u * rc)
    rc = rc * (2.0 - u * rc)
    m2 = -2.0 * th * rc
    inv_m = lax.rsqrt(jnp.maximum(m2, 1e-30))
    return r, ig, sp, a, m2 * inv_m, inv_m


def _conv(ext, cw_ref, cb_ref, tm):
    xc = cb_ref[...] + cw_ref[0:1, :] * ext[8 - 3:8 - 3 + tm, :]
    for j in range(1, CONV_W):
        xc = xc + cw_ref[j:j + 1, :] * ext[8 - 3 + j:8 - 3 + j + tm, :]
    return xc


def _scan_unroll(blocks):
    return 4 if blocks % 4 == 0 else 2 if blocks % 2 == 0 else 1


def _rg_fwd(p, cw, cb, wg, bg, lam, rg_g):
    T = p.shape[0]
    tm = _row_tile(T, 832)
    unroll = _scan_unroll(tm // 8)

    def body(xg_ref, cw_ref, cb_ref, w_ref, bg_ref, lam_ref, g_ref, y_ref, h_ref, xc_ref, ext, a_s, b_s, carry):
        i = pl.program_id(0)

        @pl.when(i == 0)
        def _():
            ext[0:8, :] = jnp.zeros((8, D_RG), F32)
            carry[...] = jnp.zeros((1, D_RG), F32)

        ext[8:8 + tm, :] = xg_ref[:, :D_RG]
        xc = _conv(ext, cw_ref, cb_ref, tm)
        xc_ref[...] = xc
        r, ig, sp, a, m, _ = _rg_gates(xc, w_ref, bg_ref, lam_ref[...])
        row = i * tm + lax.broadcasted_iota(jnp.int32, (tm, 1), 0)
        a_s[...] = a
        b_s[...] = jnp.where(row >= PAD, m * ig * xc, 0.0)
        rowi = lax.broadcasted_iota(jnp.int32, (8, D_RG), 0)

        def blk(j, c):
            for u in range(unroll):
                o = pl.multiple_of((j * unroll + u) * 8, 8)
                A, B = _scan_block_fwd(a_s[pl.ds(o, 8), :], b_s[pl.ds(o, 8), :], rowi)
                h = B + A * c
                h_ref[pl.ds(o, 8), :] = h
                c = h[7:8, :]
            return c

        carry[...] = lax.fori_loop(0, tm // (8 * unroll), blk, carry[...])
        ext[0:8, :] = ext[tm:tm + 8, :]
        g, _ = _gelu_parts(xg_ref[:, D_RG:])
        yy = g * h_ref[...]
        y_ref[...] = (yy * _rms(yy) * g_ref[...]).astype(BF16)

    vec = lambda n: pl.BlockSpec((1, n), lambda i: (0, 0))
    return pl.pallas_call(
        body, grid=(T // tm,),
        in_specs=[pl.BlockSpec((tm, 2 * D_RG), lambda i: (i, 0)),
                  pl.BlockSpec((CONV_W, D_RG), lambda i: (0, 0)), vec(D_RG),
                  pl.BlockSpec((D_RG, 2 * D_RG), lambda i: (0, 0)), vec(2 * D_RG), vec(D_RG), vec(D_RG)],
        out_specs=[pl.BlockSpec((tm, D_RG), lambda i: (i, 0))] * 3,
        out_shape=[jax.ShapeDtypeStruct((T, D_RG), BF16), jax.ShapeDtypeStruct((T, D_RG), F32),
                   jax.ShapeDtypeStruct((T, D_RG), F32)],
        scratch_shapes=[pltpu.VMEM((tm + 8, D_RG), F32), pltpu.VMEM((tm, D_RG), F32),
                        pltpu.VMEM((tm, D_RG), F32), pltpu.VMEM((1, D_RG), F32)],
        name="rg_fwd", compiler_params=_params("arbitrary"),
    )(p, cw, cb, wg, bg, lam, rg_g)


def _running_sum(x, down):
    r = lax.broadcasted_iota(jnp.int32, (CHUNK, CHUNK), 0)
    c = lax.broadcasted_iota(jnp.int32, (CHUNK, CHUNK), 1)
    tri = ((c <= r) if down else (c >= r)).astype(BF16)
    hi = x.astype(BF16)
    rest = x - hi.astype(F32)
    mid = rest.astype(BF16)
    lo = (rest - mid.astype(F32)).astype(BF16)
    return (_dot(tri, hi) + _dot(tri, mid)) + _dot(tri, lo)


def _hg_gates(hq, hf, lbraw_ref, valid):
    lb = _sigmoid(lbraw_ref[0:1, :] - lbraw_ref[1:2, :])
    sq = _sigmoid(hq)
    q = hq * sq
    sf = _sigmoid(hf)
    f = lb + (1.0 - lb) * sf
    lf = jnp.where(valid, jnp.log(f), 0.0)
    b = _running_sum(lf, True)
    return lb, sq, q, sf, f, b


def _hg_head(qh, kh, bh):
    b_last = bh[CHUNK - 1:CHUNK, :]
    refs = [bh[SUB * s:SUB * s + 1, :] for s in range(N_SUB)]
    r_sel = jnp.concatenate([jnp.broadcast_to(refs[s], (SUB, HG_HEAD_DIM)) for s in range(N_SUB)], axis=0)
    eb = jnp.exp(bh)
    eq = jnp.exp(bh - r_sel)
    ekh = jnp.exp(b_last - bh)
    ek = [jnp.exp(jnp.minimum(refs[s] - bh[:SUB * (s + 1), :], EXP_CLAMP)) for s in range(N_SUB)]
    qe = qh * eq

    def own_rows(s):
        parts = [jnp.zeros((SUB * s, HG_HEAD_DIM), F32)] if s else []
        parts.append(qe[SUB * s:SUB * (s + 1), :])
        if s < N_SUB - 1:
            parts.append(jnp.zeros((CHUNK - SUB * (s + 1), HG_HEAD_DIM), F32))
        return jnp.concatenate(parts, axis=0)

    q_hat = jnp.concatenate([own_rows(s) for s in range(N_SUB)], axis=1)

    def met_rows(s):
        n = SUB * (s + 1)
        ke = kh[:n, :] * ek[s]
        return ke if n == CHUNK else jnp.concatenate([ke, jnp.zeros((CHUNK - n, HG_HEAD_DIM), F32)], axis=0)

    k_til = jnp.concatenate([met_rows(s) for s in range(N_SUB)], axis=1)
    return b_last, eb, eq, ekh, ek, q_hat, k_til


def _causal():
    r = lax.broadcasted_iota(jnp.int32, (CHUNK, CHUNK), 0)
    c = lax.broadcasted_iota(jnp.int32, (CHUNK, CHUNK), 1)
    return r >= c


def _chunks_per_step(n_chunks):
    for c in (5, 4, 3, 2):
        if n_chunks % c == 0:
            return c
    return 1


def _hg_fwd(p, lbraw, hg_g):
    T = p.shape[0]
    n_chunks = T // CHUNK
    cps = _chunks_per_step(n_chunks)
    rows = cps * CHUNK

    def body(hq_ref, hf_ref, hi_ref, hg_ref, lb_ref, g_ref, y_ref, o_ref, st_all_ref, st):
        i = pl.program_id(0)

        @pl.when(i == 0)
        def _():
            st[...] = jnp.zeros_like(st)

        def chunk(j, carry):
            rs = pl.ds(pl.multiple_of(j * CHUNK, CHUNK), CHUNK)
            chunk_body(i * cps + j, hq_ref.at[rs, :], hf_ref.at[rs, :], hi_ref.at[rs, :], hg_ref.at[rs, :], lb_ref,
                       g_ref, y_ref.at[rs, :], o_ref.at[rs, :], st_all_ref.at[pl.ds(j, 1)], st)
            return carry

        lax.fori_loop(0, cps, chunk, 0, unroll=True)

    def chunk_body(n, hq_ref, hf_ref, hi_ref, hg_ref, lb_ref, g_ref, y_ref, o_ref, st_all_ref, st):
        valid = (n * CHUNK + lax.broadcasted_iota(jnp.int32, (CHUNK, 1), 0)) >= PAD
        hq, hf, v, hg = hq_ref[...], hf_ref[...], hi_ref[...], hg_ref[...]
        lb, sq, q, sf, f, b = _hg_gates(hq, hf, lb_ref, valid)
        k = 1.0 - f
        st_all_ref[0] = st[...]
        causal = _causal()
        v_t = v.T.astype(BF16)
        heads = [slice(h * HG_HEAD_DIM, (h + 1) * HG_HEAD_DIM) for h in range(HG_HEADS)]
        fac = []
        for sl in heads:
            qh, kh, bh = q[:, sl], k[:, sl], b[:, sl]
            b_last, eb, _, ekh, _, q_hat, k_til = _hg_head(qh, kh, bh)
            fac.append((jnp.exp(b_last), (qh * eb).astype(BF16), q_hat.astype(BF16), k_til.astype(BF16),
                        (kh * ekh).astype(BF16), v[:, sl].astype(BF16)))
        raw = []
        for sl, (_, q_til, q_hat, k_til, k_hat, _) in zip(heads, fac):
            st_h = st[sl, :]
            raw.append((_dot_nt(q_til, st_h.astype(BF16)), _dot_nt(q_hat, k_til), _dot(v_t[sl, :], k_hat), st_h))
        for sl, (e_last, _, _, _, _, vb), (inter, att, upd, st_h) in zip(heads, fac, raw):
            o = inter + _dot(jnp.where(causal, att, 0.0).astype(BF16), vb)
            st[sl, :] = st_h * e_last + upd
            o_ref[:, sl] = o
            hgh = hg[:, sl]
            y_ref[:, sl] = (o * _rms(o) * g_ref[...] * (hgh * _sigmoid(hgh))).astype(BF16)

    col = lambda j: pl.BlockSpec((rows, D_HG), lambda n: (n, j))
    return pl.pallas_call(
        body, grid=(n_chunks // cps,),
        in_specs=[col(2), col(3), col(4), col(5),
                  pl.BlockSpec((2, D_HG), lambda n: (0, 0)), pl.BlockSpec((1, HG_HEAD_DIM), lambda n: (0, 0))],
        out_specs=[pl.BlockSpec((rows, D_HG), lambda n: (n, 0)), pl.BlockSpec((rows, D_HG), lambda n: (n, 0)),
                   pl.BlockSpec((cps, D_HG, HG_HEAD_DIM), lambda n: (n, 0, 0))],
        out_shape=[jax.ShapeDtypeStruct((T, D_HG), BF16), jax.ShapeDtypeStruct((T, D_HG), F32),
                   jax.ShapeDtypeStruct((n_chunks, D_HG, HG_HEAD_DIM), F32)],
        scratch_shapes=[pltpu.VMEM((D_HG, HG_HEAD_DIM), F32)],
        name="hg_fwd", compiler_params=_params("arbitrary"),
    )(p, p, p, p, lbraw, hg_g)


def _ffn_fwd(h0, y_rg, y_hg, w_out, g2, w_gu, w_down, gf, target):
    T = h0.shape[0]
    tm = _row_tile(T, 320)
    n_steps = T // tm

    def body(h_ref, yr_ref, yh_ref, wo_ref, g2_ref, wgu_ref, wd_ref, gf_ref, t_hbm,
             h1_ref, v_ref, y_ref, gu_ref, act_ref, dh2_ref, dh2b_ref, loss_ref, gg_ref, tbuf, sems):
        i = pl.program_id(0)
        slot = _fetch_window(t_hbm, tbuf, sems, i, n_steps, tm)

        @pl.when(i == 0)
        def _():
            loss_ref[...] = jnp.zeros_like(loss_ref)
            gg_ref[...] = jnp.zeros_like(gg_ref)
            tbuf[0, 0:HEAD, :] = jnp.zeros((HEAD, D_MODEL), F32)

        y_ref[:, :D_RG] = yr_ref[...]
        y_ref[:, D_RG:] = yh_ref[...]
        h1 = h_ref[...] + _dot(y_ref[...], wo_ref[...])
        h1_ref[...] = h1
        v = (h1 * _rms(h1) * g2_ref[...]).astype(BF16)
        v_ref[...] = v

        gu = _dot(v, wgu_ref[...])
        gu_ref[...] = gu.astype(BF16)
        g = gu[:, :D_FF]
        act = (g * _sigmoid(g) * gu[:, D_FF:]).astype(BF16)
        act_ref[...] = act

        h2 = h1 + _dot(act, wd_ref[...])
        r = _rms(h2)
        n = h2 * r
        gf_ = gf_ref[...]
        row = i * tm + lax.broadcasted_iota(jnp.int32, (tm, 1), 0)
        err = jnp.where(row >= HEAD, n * gf_ - tbuf[slot], 0.0)
        loss_ref[...] += 0.5 * jnp.sum(jnp.mean(err * err, axis=-1, keepdims=True), axis=0, keepdims=True)
        dy = err * (1.0 / D_MODEL)
        gg_ref[...] += jnp.sum(dy * n, axis=0, keepdims=True)
        dh2 = _rms_bwd(dy * gf_, n, r)
        dh2_ref[...] = dh2
        dh2b_ref[...] = dh2.astype(BF16)

    row_spec = lambda n: pl.BlockSpec((tm, n), lambda i: (i, 0))
    vec = pl.BlockSpec((1, D_MODEL), lambda i: (0, 0))
    return pl.pallas_call(
        body, grid=(n_steps,),
        in_specs=[row_spec(D_MODEL), row_spec(D_RG), row_spec(D_HG), _resident((D_MODEL, D_MODEL)), vec,
                  _resident((D_MODEL, 2 * D_FF)), _resident((D_FF, D_MODEL)), vec,
                  pl.BlockSpec(memory_space=pl.ANY)],
        out_specs=[row_spec(D_MODEL), row_spec(D_MODEL), row_spec(D_MODEL), row_spec(2 * D_FF), row_spec(D_FF),
                   row_spec(D_MODEL), row_spec(D_MODEL), pl.BlockSpec((1, 1), lambda i: (0, 0)), vec],
        out_shape=[jax.ShapeDtypeStruct((T, D_MODEL), F32), jax.ShapeDtypeStruct((T, D_MODEL), BF16),
                   jax.ShapeDtypeStruct((T, D_MODEL), BF16), jax.ShapeDtypeStruct((T, 2 * D_FF), BF16),
                   jax.ShapeDtypeStruct((T, D_FF), BF16), jax.ShapeDtypeStruct((T, D_MODEL), F32),
                   jax.ShapeDtypeStruct((T, D_MODEL), BF16), jax.ShapeDtypeStruct((1, 1), F32),
                   jax.ShapeDtypeStruct((1, D_MODEL), F32)],
        scratch_shapes=[pltpu.VMEM((2, tm, D_MODEL), F32), pltpu.SemaphoreType.DMA((2,))],
        name="ffn_fwd", compiler_params=_params("arbitrary"),
    )(h0, y_rg, y_hg, w_out, g2, w_gu, w_down, gf, target)


def _resident(shape):
    return pl.BlockSpec(shape, lambda i: (0,) * len(shape), pipeline_mode=pl.Buffered(1))


def _ffn_bwd(dh2b, gu, w_down, w_gu, h1, g2, dh2, w_out):
    T = h1.shape[0]
    tm = _row_tile(T, 320)

    def body(d_ref, gu_ref, wd_ref, wgu_ref, h_ref, g_ref, d2_ref, wo_ref, dgu_ref, dh1_ref, dh1b_ref, dy_ref, gg_ref):
        i = pl.program_id(0)

        @pl.when(i == 0)
        def _():
            gg_ref[...] = jnp.zeros_like(gg_ref)

        dact = _dot_nt(d_ref[...], wd_ref[...]).astype(BF16)
        g = gu_ref[:, :D_FF]
        u = gu_ref[:, D_FF:]
        s = _sigmoid(g)
        dgu_ref[:, :D_FF] = dact * u * (s * (1.0 + g * (1.0 - s)))
        dgu_ref[:, D_FF:] = dact * (g * s)

        dv = _dot_nt(dgu_ref[...], wgu_ref[...])
        h1_ = h_ref[...]
        r = _rms(h1_)
        n = h1_ * r
        gg_ref[...] += jnp.sum(dv * n, axis=0, keepdims=True)
        dh1 = d2_ref[...] + _rms_bwd(dv * g_ref[...], n, r)
        dh1_ref[...] = dh1
        db = dh1.astype(BF16)
        dh1b_ref[...] = db
        dy_ref[...] = _dot_nt(db, wo_ref[...])

    row = lambda n: pl.BlockSpec((tm, n), lambda i: (i, 0))
    return pl.pallas_call(
        body, grid=(T // tm,),
        in_specs=[row(D_MODEL), row(2 * D_FF), _resident((D_FF, D_MODEL)), _resident((D_MODEL, 2 * D_FF)),
                  row(D_MODEL), pl.BlockSpec((1, D_MODEL), lambda i: (0, 0)), row(D_MODEL),
                  _resident((D_MODEL, D_MODEL))],
        out_specs=[row(2 * D_FF), row(D_MODEL), row(D_MODEL), row(D_MODEL),
                   pl.BlockSpec((1, D_MODEL), lambda i: (0, 0))],
        out_shape=[jax.ShapeDtypeStruct((T, 2 * D_FF), BF16), jax.ShapeDtypeStruct((T, D_MODEL), F32),
                   jax.ShapeDtypeStruct((T, D_MODEL), BF16), jax.ShapeDtypeStruct((T, D_MODEL), F32),
                   jax.ShapeDtypeStruct((1, D_MODEL), F32)],
        name="ffn_bwd", compiler_params=_params("arbitrary"),
    )(dh2b, gu, w_down, w_gu, h1, g2, dh2, w_out)


def _rg_bwd(p, xc_all, hs, dy, dp, cw, cb, wg, bg, lam, rg_g):
    T = p.shape[0]
    tm = _row_tile(T, 832)
    nt = T // tm
    hb = tm // 8
    unroll = _scan_unroll(hb)

    def body(xg_ref, xc_ref, h_ref, hh_ref, dy_ref, dp_in_ref, cw_ref, cb_ref, w_ref, bg_ref, lam_ref, g_ref,
             dp_ref, gcw_ref, gcb_ref, gw_ref, gbg_ref, glam_ref, gg_ref,
             dext, a_s, b_s, d_s, gacc, carry_d, carry_a):
        i = pl.program_id(0)
        t_idx = nt - 1 - i

        @pl.when(i == 0)
        def _():
            dext[tm:tm + 8, :] = jnp.zeros((8, D_RG), F32)
            carry_d[...] = jnp.zeros_like(carry_d)
            carry_a[...] = jnp.zeros_like(carry_a)
            gacc[...] = jnp.zeros_like(gacc)
            for ref in (gcw_ref, gcb_ref, gbg_ref, glam_ref, gg_ref, gw_ref):
                ref[...] = jnp.zeros_like(ref)

        first = t_idx == 0
        xc = xc_ref[...]
        lam_ = lam_ref[...]
        r, ig, sp, a, m, inv_m = _rg_gates(xc, w_ref, bg_ref, lam_)
        row = t_idx * tm + lax.broadcasted_iota(jnp.int32, (tm, 1), 0)
        valid = row >= PAD

        gr = xg_ref[:, D_RG:]
        g, dgelu = _gelu_parts(gr)
        h = h_ref[...]
        yy = g * h
        rr = _rms(yy)
        nn = yy * rr
        dy_ = dy_ref[...]
        gg_ref[...] += jnp.sum(dy_ * nn, axis=0, keepdims=True)
        dyy = _rms_bwd(dy_ * g_ref[...], nn, rr)
        dp_ref[:, D_RG:] = (dyy * h * dgelu).astype(BF16)

        a_s[...] = a
        b_s[...] = dyy * g
        rowi = lax.broadcasted_iota(jnp.int32, (8, D_RG), 0)

        def blk(jj, c):
            cd, ca = c
            for u in range(unroll):
                o = pl.multiple_of((hb - 1 - (jj * unroll + u)) * 8, 8)
                a_blk = a_s[pl.ds(o, 8), :]
                a_next = jnp.where(rowi == 7, ca, pltpu.roll(a_blk, 7, axis=0))
                A, B = _scan_block_bwd(a_next, b_s[pl.ds(o, 8), :], rowi)
                d = B + A * cd
                d_s[pl.ds(o, 8), :] = d
                cd, ca = d[0:1, :], a_blk[0:1, :]
            return cd, ca

        cd, ca = lax.fori_loop(0, hb // unroll, blk, (carry_d[...], carry_a[...]))
        carry_d[...] = cd
        carry_a[...] = ca
        delta = d_s[...]

        h_last_prev = jnp.where(first, 0.0, hh_ref[7:8, :])
        row0 = lax.broadcasted_iota(jnp.int32, (tm, 1), 0) == 0
        h_prev = jnp.where(row0, h_last_prev, pltpu.roll(h, 1, axis=0))
        dbx = jnp.where(valid, delta, 0.0)
        da = delta * h_prev
        di = dbx * m * xc
        dm = dbx * ig * xc
        dla = a * (da - dm * a * inv_m)
        dla = jnp.where(valid, dla, 0.0)
        glam_ref[...] += jnp.sum(dla * r, axis=0, keepdims=True) * (LRU_C / (1.0 + jnp.exp(lam_)))
        dr = (-LRU_C) * sp * dla
        dpre = jnp.concatenate([dr * r * (1.0 - r), di * ig * (1.0 - ig)], axis=1)
        gbg_ref[...] += jnp.sum(dpre, axis=0, keepdims=True)
        dpre_b = dpre.astype(BF16)
        gacc[...] += _dot_tn(xc.astype(BF16), dpre_b)
        dxc = dbx * m * ig + _dot_nt(dpre_b, w_ref[...])
        gcb_ref[...] += jnp.sum(dxc, axis=0, keepdims=True)
        dext[0:tm, :] = dxc
        xr = xg_ref[:, :D_RG]
        dxr = None
        for j in range(CONV_W):
            shifted = dext[3 - j:3 - j + tm, :]
            gcw_ref[j:j + 1, :] += jnp.sum(xr * shifted, axis=0, keepdims=True)
            tap = cw_ref[j:j + 1, :] * shifted
            dxr = tap if dxr is None else dxr + tap
        dp_ref[:, :D_RG] = dxr.astype(BF16)
        dext[tm:tm + 8, :] = dext[0:8, :]

        @pl.when(i == nt - 1)
        def _():
            fold = _head_fold()
            mask = _head_mask()
            fold_b = fold.astype(BF16)
            for k in range(2):
                blockdiag = jnp.where(mask, gacc[:, k * D_RG:(k + 1) * D_RG], 0.0)
                hi = blockdiag.astype(BF16)
                rest = blockdiag - hi.astype(F32)
                mid = rest.astype(BF16)
                lo = (rest - mid.astype(F32)).astype(BF16)
                gw_ref[k * D_RG:(k + 1) * D_RG, :] = (_dot(hi, fold_b) + _dot(mid, fold_b)) + _dot(lo, fold_b)

    vec = lambda n: pl.BlockSpec((1, n), lambda i: (0, 0))
    rev = lambda n: pl.BlockSpec((tm, n), lambda i: (nt - 1 - i, 0))
    halo = lambda n: pl.BlockSpec((8, n), lambda i: (jnp.maximum((nt - 1 - i) * hb - 1, 0), 0))
    return pl.pallas_call(
        body, grid=(nt,),
        in_specs=[rev(2 * D_RG), rev(D_RG), rev(D_RG), halo(D_RG), rev(D_RG), ANY,
                  pl.BlockSpec((CONV_W, D_RG), lambda i: (0, 0)), vec(D_RG),
                  pl.BlockSpec((D_RG, 2 * D_RG), lambda i: (0, 0)), vec(2 * D_RG), vec(D_RG), vec(D_RG)],
        out_specs=[rev(2 * D_RG), pl.BlockSpec((CONV_W, D_RG), lambda i: (0, 0)), vec(D_RG),
                   pl.BlockSpec((2 * D_RG, RG_HEAD_DIM), lambda i: (0, 0)), vec(2 * D_RG), vec(D_RG), vec(D_RG)],
        input_output_aliases={5: 0},
        out_shape=[jax.ShapeDtypeStruct((T, D_IN), BF16), jax.ShapeDtypeStruct((CONV_W, D_RG), F32),
                   jax.ShapeDtypeStruct((1, D_RG), F32), jax.ShapeDtypeStruct((2 * D_RG, RG_HEAD_DIM), F32),
                   jax.ShapeDtypeStruct((1, 2 * D_RG), F32), jax.ShapeDtypeStruct((1, D_RG), F32),
                   jax.ShapeDtypeStruct((1, D_RG), F32)],
        scratch_shapes=[pltpu.VMEM((tm + 8, D_RG), F32),
                        pltpu.VMEM((tm, D_RG), F32), pltpu.VMEM((tm, D_RG), F32), pltpu.VMEM((tm, D_RG), F32),
                        pltpu.VMEM((D_RG, 2 * D_RG), F32), pltpu.VMEM((1, D_RG), F32), pltpu.VMEM((1, D_RG), F32)],
        name="rg_bwd", compiler_params=_params("arbitrary"),
    )(p, xc_all, hs, hs, dy, dp, cw, cb, wg, bg, lam, rg_g)


def _hg_bwd(p, o_all, st_all, dy, lbraw, hg_g):
    T = p.shape[0]
    n_chunks = T // CHUNK
    cps = _chunks_per_step(n_chunks)
    rows = cps * CHUNK
    n_steps = n_chunks // cps

    def body(hq_ref, hf_ref, hi_ref, hg_ref, o_ref, st_ref, dy_ref, lb_ref, g_ref,
             dp_ref, glb_ref, gg_ref, dst):
        i = pl.program_id(0)

        @pl.when(i == 0)
        def _():
            dst[...] = jnp.zeros_like(dst)
            glb_ref[...] = jnp.zeros_like(glb_ref)
            gg_ref[...] = jnp.zeros_like(gg_ref)

        dp_ref[:, :2 * D_RG] = jnp.zeros((rows, 2 * D_RG), BF16)

        def chunk(jj, carry):
            j = cps - 1 - jj
            rs = pl.ds(pl.multiple_of(j * CHUNK, CHUNK), CHUNK)
            chunk_body((n_steps - 1 - i) * cps + j, hq_ref.at[rs, :], hf_ref.at[rs, :], hi_ref.at[rs, :],
                       hg_ref.at[rs, :], o_ref.at[rs, :], st_ref.at[pl.ds(j, 1)], dy_ref.at[rs, :], lb_ref, g_ref,
                       dp_ref.at[rs, pl.ds(2 * D_RG, 4 * D_HG)], glb_ref, gg_ref, dst)
            return carry

        lax.fori_loop(0, cps, chunk, 0, unroll=True)

    def chunk_body(n, hq_ref, hf_ref, hi_ref, hg_ref, o_ref, st_ref, dy_ref, lb_ref, g_ref,
                   dp_ref, glb_ref, gg_ref, dst):
        valid = (n * CHUNK + lax.broadcasted_iota(jnp.int32, (CHUNK, 1), 0)) >= PAD
        hq, hf, v, hg = hq_ref[...], hf_ref[...], hi_ref[...], hg_ref[...]
        lb, sq, q, sf, f, b = _hg_gates(hq, hf, lb_ref, valid)
        k = 1.0 - f
        causal = _causal()
        r_i = lax.broadcasted_iota(jnp.int32, (CHUNK, CHUNK), 0)
        c_i = lax.broadcasted_iota(jnp.int32, (CHUNK, CHUNK), 1)
        causal_t = r_i <= c_i
        is_last = lax.broadcasted_iota(jnp.int32, (CHUNK, 1), 0) == CHUNK - 1
        g_ = g_ref[...]
        db_parts, dq_parts, dk_parts = [], [], []
        gg = jnp.zeros((1, HG_HEAD_DIM), F32)
        heads = [slice(h * HG_HEAD_DIM, (h + 1) * HG_HEAD_DIM) for h in range(HG_HEADS)]

        do_parts = []
        for h, sl in enumerate(heads):
            o = o_ref[:, sl]
            ro = _rms(o)
            no = o * ro
            hgh = hg[:, sl]
            sg = _sigmoid(hgh)
            dyh = dy_ref[:, sl]
            dp_ref[:, 3 * D_HG + h * HG_HEAD_DIM:3 * D_HG + (h + 1) * HG_HEAD_DIM] = (
                dyh * no * g_ * sg * (1.0 + hgh * (1.0 - sg))).astype(BF16)
            dng = dyh * hgh * sg
            gg = gg + jnp.sum(dng * no, axis=0, keepdims=True)
            do_parts.append(_rms_bwd(dng * g_, no, ro))
        do_t = jnp.concatenate(do_parts, axis=1).T.astype(BF16)

        fac = []
        for sl, do in zip(heads, do_parts):
            qh, kh, bh = q[:, sl], k[:, sl], b[:, sl]
            b_last, eb, eq, ekh, ek, q_hat, k_til = _hg_head(qh, kh, bh)
            fac.append(dict(qh=qh, kh=kh, e_last=jnp.exp(b_last), eb=eb, eq=eq, ekh=ekh, ek=ek,
                            q_til=qh * eb, k_hat=kh * ekh, qhb=q_hat.astype(BF16), ktb=k_til.astype(BF16),
                            vb=v[:, sl].astype(BF16), dob=do.astype(BF16)))

        first = []
        for sl, t in zip(heads, fac):
            st_h = st_ref[0, sl, :]
            dst_h = dst[sl, :]
            dstb = dst_h.astype(BF16)
            first.append(dict(
                att_t=_dot_nt(t["ktb"], t["qhb"]), datt=_dot_nt(t["dob"], t["vb"]),
                datt_t=_dot_nt(t["vb"], t["dob"]), dk_hat=_dot(t["vb"], dstb),
                dv=_dot_nt(t["k_hat"].astype(BF16), dstb), dq_til=_dot(t["dob"], st_h.astype(BF16)),
                state=t["e_last"] * jnp.sum(dst_h * st_h, axis=0, keepdims=True)))
            dst[sl, :] = dst_h * t["e_last"] + _dot(do_t[sl, :], t["q_til"].astype(BF16))

        for h, (t, m) in enumerate(zip(fac, first)):
            qh, kh, eb, eq, ekh, ek = t["qh"], t["kh"], t["eb"], t["eq"], t["ekh"], t["ek"]
            q_til, k_hat, qhb, ktb, dob = t["q_til"], t["k_hat"], t["qhb"], t["ktb"], t["dob"]
            dk_hat, dq_til = m["dk_hat"], m["dq_til"]
            dv = m["dv"] + _dot(jnp.where(causal_t, m["att_t"], 0.0).astype(BF16), dob)
            dq_hat = _dot(jnp.where(causal, m["datt"], 0.0).astype(BF16), ktb)
            dk_til = _dot(jnp.where(causal_t, m["datt_t"], 0.0).astype(BF16), qhb)
            db_last = jnp.sum(dk_hat * k_hat, axis=0, keepdims=True) + m["state"]
            dq_sel = jnp.concatenate([dq_hat[SUB * s:SUB * (s + 1), s * HG_HEAD_DIM:(s + 1) * HG_HEAD_DIM]
                                      for s in range(N_SUB)], axis=0)
            dq_a = dq_sel * eq
            dk_rows, k_att_rows = [], []
            for b_ in range(N_SUB):
                rs = slice(SUB * b_, SUB * (b_ + 1))
                dk_sum = k_att_sum = None
                for s in range(b_, N_SUB):
                    cs = slice(s * HG_HEAD_DIM, (s + 1) * HG_HEAD_DIM)
                    d = dk_til[rs, cs]
                    t_dk = d * ek[s][rs, :]
                    t_att = ktb[rs, cs].astype(F32) * d
                    dk_sum = t_dk if dk_sum is None else dk_sum + t_dk
                    k_att_sum = t_att if k_att_sum is None else k_att_sum + t_att
                dk_rows.append(dk_sum)
                k_att_rows.append(k_att_sum)
            dk_a = jnp.concatenate(dk_rows, axis=0)
            db = (dq_til * q_til - dk_hat * k_hat + (qh * eq).astype(BF16).astype(F32) * dq_sel
                  - jnp.concatenate(k_att_rows, axis=0))
            db_parts.append(jnp.where(is_last, db + db_last, db))
            dq_parts.append(dq_til * eb + dq_a)
            dk_parts.append(dk_hat * ekh + dk_a)
            dp_ref[:, 2 * D_HG + h * HG_HEAD_DIM:2 * D_HG + (h + 1) * HG_HEAD_DIM] = dv.astype(BF16)

        gg_ref[...] += gg
        db = jnp.concatenate(db_parts, axis=1)
        dq = jnp.concatenate(dq_parts, axis=1)
        dk = jnp.concatenate(dk_parts, axis=1)
        dlf = jnp.where(valid, _running_sum(db, False), 0.0)
        dp_ref[:, :D_HG] = (dq * sq * (1.0 + hq * (1.0 - sq))).astype(BF16)
        df = dlf / f - dk
        dlb = jnp.sum(df * (1.0 - sf), axis=0, keepdims=True) * lb * (1.0 - lb)
        glb_ref[0:1, :] += dlb
        glb_ref[1:2, :] += -dlb
        dp_ref[:, D_HG:2 * D_HG] = (df * (1.0 - lb) * sf * (1.0 - sf)).astype(BF16)

    rev = lambda j: pl.BlockSpec((rows, D_HG), lambda i: (n_steps - 1 - i, j))
    return pl.pallas_call(
        body, grid=(n_steps,),
        in_specs=[rev(2), rev(3), rev(4), rev(5), rev(0),
                  pl.BlockSpec((cps, D_HG, HG_HEAD_DIM), lambda i: (n_steps - 1 - i, 0, 0)), rev(1),
                  pl.BlockSpec((2, D_HG), lambda i: (0, 0)), pl.BlockSpec((1, HG_HEAD_DIM), lambda i: (0, 0))],
        out_specs=[pl.BlockSpec((rows, D_IN), lambda i: (n_steps - 1 - i, 0)),
                   pl.BlockSpec((2, D_HG), lambda i: (0, 0)), pl.BlockSpec((1, HG_HEAD_DIM), lambda i: (0, 0))],
        out_shape=[jax.ShapeDtypeStruct((T, D_IN), BF16), jax.ShapeDtypeStruct((2, D_HG), F32),
                   jax.ShapeDtypeStruct((1, HG_HEAD_DIM), F32)],
        scratch_shapes=[pltpu.VMEM((D_HG, HG_HEAD_DIM), F32)],
        name="hg_bwd", compiler_params=_params("arbitrary"),
    )(p, p, p, p, o_all, st_all, dy, lbraw, hg_g)


def _in_bwd(dp, w_in, h0, g1, dh1):
    T = h0.shape[0]
    tm = _row_tile(T, 832)
    n_steps = T // tm

    def body(dp_ref, w_ref, h_ref, g_ref, d1_ref, gx_hbm, gmeta_ref, gg_ref, buf, sems):
        i = pl.program_id(0)
        first, later = _window_copies(gx_hbm, buf, sems, tm)
        slot = i % 2

        @pl.when(i == 0)
        def _():
            gg_ref[...] = jnp.zeros_like(gg_ref)

        if n_steps > 2:
            @pl.when(i == 2)
            def _():
                first(False).wait()

            @pl.when(i > 2)
            def _():
                later(i - 2, slot, False).wait()

        du = _dot_nt(dp_ref[...], w_ref[...])
        h0_ = h_ref[...]
        r = _rms(h0_)
        n = h0_ * r
        gg_ref[...] += jnp.sum(du * n, axis=0, keepdims=True)
        dh0 = d1_ref[...] + _rms_bwd(du * g_ref[...], n, r)
        buf[slot] = dh0

        @pl.when(i == 0)
        def _():
            gmeta_ref[...] = dh0[PAD:HEAD, :]
            first(False).start()

        if n_steps > 1:
            @pl.when(i > 0)
            def _():
                later(i, slot, False).start()

        @pl.when(i == n_steps - 1)
        def _():
            if n_steps == 1:
                first(False).wait()
            else:
                if n_steps == 2:
                    first(False).wait()
                else:
                    later(i - 1, 1 - slot, False).wait()
                later(i, slot, False).wait()

    row = lambda n: pl.BlockSpec((tm, n), lambda i: (i, 0))
    return pl.pallas_call(
        body, grid=(n_steps,),
        in_specs=[row(D_IN), _resident((D_MODEL, D_IN)),
                  row(D_MODEL), pl.BlockSpec((1, D_MODEL), lambda i: (0, 0)), row(D_MODEL)],
        out_specs=[pl.BlockSpec(memory_space=pl.ANY), pl.BlockSpec((N_META, D_MODEL), lambda i: (0, 0)),
                   pl.BlockSpec((1, D_MODEL), lambda i: (0, 0))],
        out_shape=[jax.ShapeDtypeStruct((T - HEAD, D_MODEL), F32), jax.ShapeDtypeStruct((N_META, D_MODEL), F32),
                   jax.ShapeDtypeStruct((1, D_MODEL), F32)],
        scratch_shapes=[pltpu.VMEM((2, tm, D_MODEL), F32), pltpu.SemaphoreType.DMA((2,))],
        name="in_bwd", compiler_params=_params("arbitrary"),
    )(dp, w_in, h0, g1, dh1)


def _col_tile(cols, target):
    best = None
    for t in range(128, min(cols, target) + 1, 128):
        if cols % t == 0:
            best = t
    assert best is not None, cols
    return best


MXU_DIM = 256


def _mxu_tile(cols, target):
    best = None
    for t in range(MXU_DIM, min(cols, target) + 1, MXU_DIM):
        if cols % t == 0:
            best = t
    assert best is not None, cols
    return best


def _weight_grad(a, b, name):
    T, M = a.shape
    N = b.shape[1]
    tm = _col_tile(M, 1408)
    tn = _mxu_tile(N, 768 if tm <= 1024 else 512)

    def body(a_ref, b_ref, o_ref):
        o_ref[...] = _dot_tn(a_ref[...], b_ref[...])

    return pl.pallas_call(
        body, grid=(M // tm, N // tn),
        in_specs=[pl.BlockSpec((T, tm), lambda m, n: (0, m)), pl.BlockSpec((T, tn), lambda m, n: (0, n))],
        out_specs=pl.BlockSpec((tm, tn), lambda m, n: (m, n)),
        out_shape=jax.ShapeDtypeStruct((M, N), F32),
        name=name, compiler_params=_params("parallel", "parallel"),
    )(a, b)


def _local_step(x, meta, target, w_in_own, w_in, w_out, w_gu, w_down, small, chip, on_ffn_grads=None,
                on_mixer_grads=None):
    wg = _gate_weights(small["w_rgate"], small["w_igate"])
    bg = jnp.concatenate([small["b_rgate"], small["b_igate"]], axis=1)

    p, u, h0 = _in_proj_local(x, meta, small["mix_norm_g"], w_in_own, chip)
    p = _in_proj_rest(u, w_in, p, chip)
    y_rg, hs, xc = _rg_fwd(p, small["conv_w"], small["conv_b"], wg, bg, small["lru_lambda"], small["rg_norm_g"])
    y_hg, o_all, st_all = _hg_fwd(p, small["hg_lower_bound"], small["hg_norm_g"])
    h1, v, yb, gu, act, dh2, dh2b, loss, g_final = _ffn_fwd(
        h0, y_rg, y_hg, w_out, small["ffn_norm_g"], w_gu, w_down, small["final_norm_g"], target)

    g_w_down = _weight_grad(act, dh2b, "grad_w_down")
    dgu, dh1, dh1b, dy, g_ffn = _ffn_bwd(dh2b, gu, w_down, w_gu, h1, small["ffn_norm_g"], dh2, w_out)
    ffn_grads = {"w_gate_up": _weight_grad(v, dgu, "grad_w_gate_up"), "w_down": g_w_down,
                 "w_out": _weight_grad(yb, dh1b, "grad_w_out")}
    stages = on_ffn_grads(ffn_grads) if on_ffn_grads is not None else None
    dp, g_lb, g_hgn = _hg_bwd(p, o_all, st_all, dy, small["hg_lower_bound"], small["hg_norm_g"])
    early = late = None
    if stages is not None:
        chip_sums, send = stages
        sums = chip_sums()
        (dp, dy), sums = lax.optimization_barrier(((dp, dy), sums))
        early = send(sums)
    dp, g_cw, g_cb, g_wgate, g_bg, g_lam, g_rgn = _rg_bwd(
        p, xc, hs, dy, dp, small["conv_w"], small["conv_b"], wg, bg, small["lru_lambda"], small["rg_norm_g"])
    mixer_grads = {"w_in": _weight_grad(u, dp, "grad_w_in")}
    if on_mixer_grads is not None:
        chip_sums, send = on_mixer_grads(mixer_grads)
        sums = chip_sums()
        (dp, dh1), sums = lax.optimization_barrier(((dp, dh1), sums))
        late = send(sums)
    grad_x, g_meta, g_mix = _in_bwd(dp, w_in, h0, small["mix_norm_g"], dh1)

    grads = {
        "w_in": mixer_grads["w_in"], "w_out": ffn_grads["w_out"],
        "w_gate_up": ffn_grads["w_gate_up"], "w_down": ffn_grads["w_down"],
        "meta_tokens": g_meta, "mix_norm_g": g_mix, "conv_w": g_cw, "conv_b": g_cb, "w_gates": g_wgate,
        "b_rgate": g_bg[:, :D_RG], "b_igate": g_bg[:, D_RG:], "lru_lambda": g_lam, "rg_norm_g": g_rgn,
        "hg_lower_bound": g_lb, "hg_norm_g": g_hgn, "ffn_norm_g": g_ffn, "final_norm_g": g_final,
    }
    return loss, grad_x, grads, early, late


ANY = pl.BlockSpec(memory_space=pl.ANY)
HALF = D_MODEL // 2

BIG = {"w_in": (D_MODEL, D_IN // N_CHIPS, True), "w_gate_up": (D_MODEL, 2 * D_FF // N_CHIPS, True),
       "w_out": (D_MODEL // N_CHIPS, D_MODEL, False), "w_down": (D_FF // N_CHIPS, D_MODEL, False)}
BIG_NAMES = tuple(BIG)
N_BIG = len(BIG_NAMES)


def _full_shape(name):
    rows, cols, by_col = BIG[name]
    return (rows, cols * N_CHIPS) if by_col else (rows * N_CHIPS, cols)


def _place():
    return lax.axis_index("x"), lax.axis_index("y"), lax.axis_index("c")


def _chip_of(x, y, r):
    fx, fy = (r + 1) >> 1, (r + 1) & 1
    return (1 - x if fx else x), (1 - y if fy else y)


def _half_of(ref, by_col, half):
    start = pl.multiple_of(half * HALF, 128)
    return ref.at[pl.ds(start, HALF), :] if by_col else ref.at[:, pl.ds(start, HALF)]


def _shard_of(ref, name, chip):
    rows, cols, by_col = BIG[name]
    if by_col:
        return ref.at[:, pl.ds(pl.multiple_of(chip * cols, 128), cols)]
    return ref.at[pl.ds(pl.multiple_of(chip * rows, 16), rows), :]


def _shard_half_of(ref, name, chip, half):
    rows, cols, by_col = BIG[name]
    start = pl.multiple_of(half * HALF, 128)
    if by_col:
        return ref.at[pl.ds(start, HALF), pl.ds(pl.multiple_of(chip * cols, 128), cols)]
    return ref.at[pl.ds(pl.multiple_of(chip * rows, 16), rows), pl.ds(start, HALF)]


def _remote(src, dst, send_sems, recv_sems, k, dev):
    return pltpu.make_async_remote_copy(src_ref=src, dst_ref=dst, send_sem=send_sems.at[k], recv_sem=recv_sems.at[k],
                                        device_id=dev, device_id_type=MESH)


def _place_shards(w, small, chip):
    steps = 4
    ns = len(small)
    in_specs, out_specs = [], []
    for name in BIG_NAMES:
        rows, cols, by_col = BIG[name]
        tr = rows // steps
        in_specs.append(pl.BlockSpec((tr, cols), lambda i, s: (i, 0)))
        if by_col:
            out_specs.append(pl.BlockSpec((tr, cols), lambda i, s: (i, s[0])))
        else:
            out_specs.append(pl.BlockSpec((tr, cols), lambda i, s: (s[0] * steps + i, 0)))

    def body(s_ref, *refs):
        ins, small_in = refs[:N_BIG], refs[N_BIG:N_BIG + ns]
        outs, small_out = refs[N_BIG + ns:2 * N_BIG + ns], refs[2 * N_BIG + ns:2 * (N_BIG + ns)]
        send_sems, recv_sems, local_sems = refs[2 * (N_BIG + ns):]
        i = pl.program_id(0)
        x, y, c = _place()
        chip_ = 2 * x + y
        others = [_chip_of(x, y, r) for r in range(3)]

        def block(a, q):
            cols = small[a].shape[1]
            return small_out[a].at[:, pl.ds(pl.multiple_of(q * cols, 128), cols)]

        def local(a):
            return pltpu.make_async_copy(small_in[a], block(a, chip_), local_sems.at[a])

        def remote(a, r):
            qx, qy = others[r]
            return _remote(small_in[a], block(a, chip_), send_sems, recv_sems, 3 * a + r, (qx, qy, c))

        @pl.when(i == 0)
        def _():
            for a in range(ns):
                local(a).start()
                for r in range(3):
                    remote(a, r).start()

        for a in range(N_BIG):
            outs[a][...] = ins[a][...].astype(BF16)

        @pl.when(i == steps - 1)
        def _():
            for a in range(ns):
                for r, (qx, qy) in enumerate(others):
                    landed = block(a, 2 * qx + qy)
                    _remote(landed, landed, send_sems, recv_sems, 3 * a + r, (qx, qy, c)).wait_recv()
                for r in range(3):
                    remote(a, r).wait_send()
                local(a).wait()

    out = pl.pallas_call(
        body,
        grid_spec=pltpu.PrefetchScalarGridSpec(
            num_scalar_prefetch=1, grid=(steps,), in_specs=in_specs + [ANY] * ns, out_specs=out_specs + [ANY] * ns,
            scratch_shapes=[pltpu.SemaphoreType.DMA((3 * ns,)), pltpu.SemaphoreType.DMA((3 * ns,)),
                            pltpu.SemaphoreType.DMA((ns,))]),
        out_shape=([jax.ShapeDtypeStruct(_full_shape(name), BF16) for name in BIG_NAMES]
                   + [jax.ShapeDtypeStruct((s.shape[0], s.shape[1] * N_CHIPS), F32) for s in small]),
        name="place_shards", compiler_params=_params("arbitrary"),
    )(chip, *[w[name] for name in BIG_NAMES], *small)
    return dict(zip(BIG_NAMES, out[:N_BIG])), list(out[N_BIG:])


def _gather_weights(placed, small, names, label, collective_id):
    n, ns = len(names), len(small)
    hbm = pltpu.MemorySpace.HBM
    outs = [jax.new_ref(placed[nm], memory_space=hbm) for nm in names]
    small_in = [jax.new_ref(s, memory_space=hbm) for s in small]
    small_out = [jax.empty_ref(jax.ShapeDtypeStruct((s.shape[0], s.shape[1] * N_CHIPS), F32), memory_space=hbm)
                 for s in small]
    n_sems = 6 * n + 3 * ns

    @pl.kernel(mesh=plsc.ScalarSubcoreMesh(axis_name="seq", num_cores=1), name=label, out_type=(),
               scratch_types=(pltpu.SemaphoreType.DMA((n_sems,)), pltpu.SemaphoreType.DMA((n_sems,)),
                              pltpu.SemaphoreType.DMA((max(ns, 1),))),
               compiler_params=pltpu.CompilerParams(collective_id=collective_id))
    def launch(send_sems, recv_sems, local_sems):
        x, y, c = _place()
        chip = 2 * x + y
        sibling = (x, y, 1 - c)
        others = [_chip_of(x, y, r) for r in range(3)]
        _handshake([(qx, qy, c) for qx, qy in others] + [sibling])

        def small_block(a, q):
            cols = small[a].shape[1]
            return small_out[a].at[:, pl.ds(pl.multiple_of(q * cols, 128), cols)]

        local = [pltpu.make_async_copy(small_in[a], small_block(a, chip), local_sems.at[a]) for a in range(ns)]
        for cp in local:
            cp.start()

        sends = []
        for a, name in enumerate(names):
            mine = _shard_half_of(outs[a], name, chip, c)
            for r, (qx, qy) in enumerate(others):
                sends.append(_remote(mine, mine, send_sems, recv_sems, 6 * a + r, (qx, qy, c)))
        for a in range(ns):
            for r, (qx, qy) in enumerate(others):
                sends.append(_remote(small_in[a], small_block(a, chip), send_sems, recv_sems,
                                     6 * n + 3 * a + r, (qx, qy, c)))
        for cp in sends:
            cp.start()

        forwards = []
        for a, name in enumerate(names):
            for r, (qx, qy) in enumerate(others):
                landed = _shard_half_of(outs[a], name, 2 * qx + qy, c)
                _remote(landed, landed, send_sems, recv_sems, 6 * a + r, (qx, qy, c)).wait_recv()
                fwd = _remote(landed, landed, send_sems, recv_sems, 6 * a + 3 + r, sibling)
                fwd.start()
                forwards.append(fwd)
        for a in range(ns):
            for r, (qx, qy) in enumerate(others):
                landed = small_block(a, 2 * qx + qy)
                _remote(landed, landed, send_sems, recv_sems, 6 * n + 3 * a + r, (qx, qy, c)).wait_recv()
        for a, name in enumerate(names):
            for r, (qx, qy) in enumerate(others):
                landed = _shard_half_of(outs[a], name, 2 * qx + qy, 1 - c)
                _remote(landed, landed, send_sems, recv_sems, 6 * a + 3 + r, sibling).wait_recv()
        for cp in sends + forwards:
            cp.wait_send()
        for cp in local:
            cp.wait()

    launch()
    return {nm: ref[...] for nm, ref in zip(names, outs)}, [ref[...] for ref in small_out]


def _exchange_halves(grads, names, label, collective_id):
    n = len(names)
    sequencer = collective_id is not None

    def body(*refs):
        ins, outs = refs[:n], refs[n:2 * n]
        send_sems, recv_sems = refs[2 * n:]
        x, y, c = _place()
        if sequencer:
            _handshake([(x, y, 1 - c)])
        copies = []
        for a, name in enumerate(names):
            copies.append(_remote(_half_of(ins[a], BIG[name][2], 1 - c), outs[a], send_sems, recv_sems, a,
                                  (x, y, 1 - c)))
        for cp in copies:
            cp.start()
        for cp in copies:
            cp.wait()

    def half_shape(name):
        r, c_ = _full_shape(name)
        return (HALF, c_) if BIG[name][2] else (r, HALF)

    out_type = tuple(jax.ShapeDtypeStruct(half_shape(nm), F32) for nm in names)
    sems = (pltpu.SemaphoreType.DMA((n,)), pltpu.SemaphoreType.DMA((n,)))
    operands = [grads[nm] for nm in names]
    if sequencer:
        got = pl.kernel(
            body, mesh=plsc.ScalarSubcoreMesh(axis_name="seq", num_cores=1), name=label, out_type=out_type,
            scratch_types=sems, compiler_params=pltpu.CompilerParams(collective_id=collective_id),
        )(*operands)
    else:
        got = pl.pallas_call(
            body, in_specs=[ANY] * n, out_specs=[ANY] * n, out_shape=list(out_type), scratch_shapes=list(sems),
            name=label,
        )(*operands)
    return dict(zip(names, got))


def _chip_sum(grads, got, names, core, label):
    n = len(names)
    steps = 4
    g_specs, blks = [], []
    for name in names:
        rows, cols = got[name].shape
        tr = rows // steps
        if BIG[name][2]:
            g_specs.append(pl.BlockSpec((tr, cols), lambda i, s: (s[0] * steps + i, 0)))
        else:
            g_specs.append(pl.BlockSpec((tr, HALF), lambda i, s: (i, s[0])))
        blks.append(pl.BlockSpec((tr, cols), lambda i, s: (i, 0)))

    def body(s_ref, *refs):
        for a in range(n):
            t = refs[a][...] + refs[n + a][...]
            refs[2 * n + a][...] = t
            refs[3 * n + a][...] = t.astype(BF16)

    out = pl.pallas_call(
        body,
        grid_spec=pltpu.PrefetchScalarGridSpec(num_scalar_prefetch=1, grid=(steps,), in_specs=g_specs + blks,
                                               out_specs=blks + blks),
        out_shape=([jax.ShapeDtypeStruct(got[nm].shape, F32) for nm in names]
                   + [jax.ShapeDtypeStruct(got[nm].shape, BF16) for nm in names]),
        name=label, compiler_params=_params("parallel"),
    )(core, *[grads[nm] for nm in names], *[got[nm] for nm in names])
    return {nm: (out[a], out[n + a]) for a, nm in enumerate(names)}


def _piece_shape(name):
    rows, cols, by_col = BIG[name]
    return (HALF, cols) if by_col else (rows, HALF)


def _handshake(peers):
    barrier = pltpu.get_barrier_semaphore()
    for peer in peers:
        pl.semaphore_signal(barrier, inc=1, device_id=peer, device_id_type=MESH)
    pl.semaphore_wait(barrier, len(peers))


def _send_chip_sums(sums, names, label, collective_id):
    n = len(names)

    def body(*refs):
        ins, outs = refs[:n], refs[n:2 * n]
        send_sems, recv_sems = refs[2 * n:]
        x, y, c = _place()
        others = [_chip_of(x, y, r) for r in range(3)]
        _handshake([(qx, qy, c) for qx, qy in others])
        copies = []
        for a, name in enumerate(names):
            for r, (qx, qy) in enumerate(others):
                copies.append(_remote(_shard_of(ins[a], name, 2 * qx + qy), outs[a].at[r], send_sems, recv_sems,
                                      3 * a + r, (qx, qy, c)))
        for cp in copies:
            cp.start()
        for cp in copies:
            cp.wait()

    return pl.kernel(
        body, mesh=plsc.ScalarSubcoreMesh(axis_name="seq", num_cores=1), name=label,
        out_type=tuple(jax.ShapeDtypeStruct((3,) + _piece_shape(nm), BF16) for nm in names),
        scratch_types=(pltpu.SemaphoreType.DMA((3 * n,)), pltpu.SemaphoreType.DMA((3 * n,))),
        compiler_params=pltpu.CompilerParams(collective_id=collective_id),
    )(*[sums[nm] for nm in names])


def _total(parts, chip_core):
    steps = 2
    in_specs, out_specs, operands = [], [], []
    for name in BIG_NAMES:
        by_col = BIG[name][2]
        pr, pc = _piece_shape(name)
        tr = pr // steps
        if by_col:
            in_specs.append(pl.BlockSpec((tr, pc), lambda i, s: (i, s[0])))
            out_specs.append(pl.BlockSpec((tr, pc), lambda i, s: (s[1] * steps + i, 0)))
        else:
            in_specs.append(pl.BlockSpec((tr, pc), lambda i, s: (s[0] * steps + i, 0)))
            out_specs.append(pl.BlockSpec((tr, pc), lambda i, s: (i, s[1])))
        for r in range(3):
            in_specs.append(pl.BlockSpec((None, tr, pc), lambda i, s, r=r: (r, i, 0)))
        own, got = parts[name]
        operands += [own, got, got, got]

    def body(s_ref, *refs):
        for a in range(N_BIG):
            o_ref, a_ref, b_ref, c_ref = refs[4 * a:4 * a + 4]
            refs[4 * N_BIG + a][...] = (((o_ref[...] + a_ref[...].astype(F32)) + b_ref[...].astype(F32))
                                        + c_ref[...].astype(F32))

    totals = pl.pallas_call(
        body,
        grid_spec=pltpu.PrefetchScalarGridSpec(num_scalar_prefetch=1, grid=(steps,), in_specs=in_specs,
                                               out_specs=out_specs),
        out_shape=[jax.ShapeDtypeStruct(BIG[name][:2], F32) for name in BIG_NAMES],
        name="totals", compiler_params=_params("parallel"),
    )(chip_core, *operands)
    return dict(zip(BIG_NAMES, totals))


VEC_ROWS = 32
VEC_ROW = {"mix_norm_g": 0, "conv_b": 1, "b_rgate": 2, "b_igate": 3, "lru_lambda": 4, "rg_norm_g": 5,
           "hg_lower_bound": 6, "hg_norm_g": 8, "ffn_norm_g": 9, "final_norm_g": 10, "loss": 11,
           "conv_w": 12, "meta_tokens": 16}
N_DEV = 8


def _all_reduce_small(pieces, gates, totals):
    names = list(pieces)
    n_small = 10
    hv, hg = VEC_ROWS // 2, gates.shape[0] // 2

    def body(*refs):
        ins = refs[:len(names)]
        g_ref = refs[len(names)]
        vec_ref, gsum_ref = refs[len(names) + 1 + N_BIG:len(names) + 3 + N_BIG]
        big = refs[len(names) + 3 + N_BIG:len(names) + 3 + 2 * N_BIG]
        (mine_v, sib_v, sib_g, chip_v, chip_g, got_v, got_g, send_sems, recv_sems) = refs[len(names) + 3 + 2 * N_BIG:]
        x, y, c = _place()
        chip = 2 * x + y
        sibling = (x, y, 1 - c)
        share = []
        for a, name in enumerate(BIG_NAMES):
            half = _half_of(big[a], BIG[name][2], c)
            share.append(_remote(half, half, send_sems, recv_sems, n_small + a, sibling))
        mine_v[...] = jnp.zeros_like(mine_v)
        for name, ref in zip(names, ins):
            nr, w = ref.shape
            mine_v[VEC_ROW[name]:VEC_ROW[name] + nr, 0:w] = ref[...]

        swap = [_remote(mine_v, sib_v, send_sems, recv_sems, 0, sibling),
                _remote(g_ref, sib_g, send_sems, recv_sems, 1, sibling)]
        for cp in swap:
            cp.start()
        for cp in swap:
            cp.wait()
        for cp in share:
            cp.start()
        chip_v[...] = mine_v[...] + sib_v[...]
        chip_g[...] = g_ref[...] + sib_g[...]

        rows_v = pl.ds(pl.multiple_of(c * hv, 8), hv)
        rows_g = pl.ds(pl.multiple_of(c * hg, 8), hg)
        got_v[chip] = chip_v[rows_v, :]
        got_g[chip] = chip_g[rows_g, :]
        sends = []
        for r in range(3):
            qx, qy = _chip_of(x, y, r)
            sends.append(_remote(chip_v.at[rows_v, :], got_v.at[chip], send_sems, recv_sems, 2 + r, (qx, qy, c)))
            sends.append(_remote(chip_g.at[rows_g, :], got_g.at[chip], send_sems, recv_sems, 5 + r, (qx, qy, c)))
        for cp in sends:
            cp.start()
        for cp in sends:
            cp.wait()
        vec_ref[rows_v, :] = ((got_v[0] + got_v[1]) + got_v[2]) + got_v[3]
        gsum_ref[rows_g, :] = ((got_g[0] + got_g[1]) + got_g[2]) + got_g[3]

        back = [_remote(vec_ref.at[rows_v, :], vec_ref.at[rows_v, :], send_sems, recv_sems, 8, sibling),
                _remote(gsum_ref.at[rows_g, :], gsum_ref.at[rows_g, :], send_sems, recv_sems, 9, sibling)]
        for cp in back:
            cp.start()
        theirs_v = vec_ref.at[pl.ds(pl.multiple_of((1 - c) * hv, 8), hv), :]
        theirs_g = gsum_ref.at[pl.ds(pl.multiple_of((1 - c) * hg, 8), hg), :]
        _remote(theirs_v, theirs_v, send_sems, recv_sems, 8, sibling).wait_recv()
        _remote(theirs_g, theirs_g, send_sems, recv_sems, 9, sibling).wait_recv()
        for cp in back:
            cp.wait_send()
        for a, name in enumerate(BIG_NAMES):
            theirs = _half_of(big[a], BIG[name][2], 1 - c)
            _remote(theirs, theirs, send_sems, recv_sems, n_small + a, sibling).wait_recv()
        for cp in share:
            cp.wait_send()

    vmem = pl.BlockSpec(memory_space=pltpu.VMEM)
    n_sems = n_small + N_BIG
    out = pl.pallas_call(
        body, in_specs=[vmem] * (len(names) + 1) + [ANY] * N_BIG, out_specs=[vmem, vmem] + [ANY] * N_BIG,
        out_shape=([jax.ShapeDtypeStruct((VEC_ROWS, D_MODEL), F32), jax.ShapeDtypeStruct(gates.shape, F32)]
                   + [jax.ShapeDtypeStruct(BIG[n][:2], F32) for n in BIG_NAMES]),
        input_output_aliases={len(names) + 1 + a: 2 + a for a in range(N_BIG)},
        scratch_shapes=[pltpu.VMEM((VEC_ROWS, D_MODEL), F32), pltpu.VMEM((VEC_ROWS, D_MODEL), F32),
                        pltpu.VMEM(gates.shape, F32), pltpu.VMEM((VEC_ROWS, D_MODEL), F32),
                        pltpu.VMEM(gates.shape, F32), pltpu.VMEM((N_CHIPS, hv, D_MODEL), F32),
                        pltpu.VMEM((N_CHIPS, hg) + gates.shape[1:], F32),
                        pltpu.SemaphoreType.DMA((n_sems,)), pltpu.SemaphoreType.DMA((n_sems,))],
        name="all_reduce_small",
    )(*[pieces[n] for n in names], gates, *[totals[n] for n in BIG_NAMES])
    return out[0], out[1], dict(zip(BIG_NAMES, out[2:]))


def _adamw_math(w, g, m, v):
    m = ADAM_B1 * m + (1.0 - ADAM_B1) * g
    v = ADAM_B2 * v + (1.0 - ADAM_B2) * (g * g)
    m_hat = m / (1.0 - ADAM_B1 ** ADAM_STEP)
    v_hat = v / (1.0 - ADAM_B2 ** ADAM_STEP)
    delta = -ADAM_LR * (m_hat / (jnp.sqrt(v_hat) + ADAM_EPS) + ADAM_WD * w)
    return delta, m, v


def _adamw_big(w, g, m, v):
    steps = 8
    blks = []
    for name in BIG_NAMES:
        rows, cols, _ = BIG[name]
        blks.append(pl.BlockSpec((rows // steps, cols), lambda i: (i, 0)))

    def body(*refs):
        ins, outs = refs[:4 * N_BIG], refs[4 * N_BIG:]
        for a in range(N_BIG):
            w_ref, g_ref, m_ref, v_ref = (ins[k * N_BIG + a] for k in range(4))
            g = g_ref[...]
            d, nm, nv = _adamw_math(w_ref[...], g, m_ref[...], v_ref[...])
            outs[a][...] = g
            outs[N_BIG + a][...] = d
            outs[2 * N_BIG + a][...] = nm
            outs[3 * N_BIG + a][...] = nv

    shapes = [jax.ShapeDtypeStruct(BIG[name][:2], F32) for name in BIG_NAMES]
    out = pl.pallas_call(
        body, grid=(steps,), in_specs=blks * 4, out_specs=blks * 4, out_shape=shapes * 4,
        name="adamw_big", compiler_params=_params("parallel"),
    )(*[t[name] for t in (w, g, m, v) for name in BIG_NAMES])
    return {name: tuple(out[k * N_BIG + a] for k in range(4)) for a, name in enumerate(BIG_NAMES)}


SMALL = {"meta_tokens": (N_META, D_MODEL // N_CHIPS), "mix_norm_g": (1, D_MODEL), "conv_w": (CONV_W, D_RG // N_CHIPS),
         "conv_b": (1, D_RG), "w_rgate": (D_RG, RG_HEAD_DIM), "b_rgate": (1, D_RG), "w_igate": (D_RG, RG_HEAD_DIM),
         "b_igate": (1, D_RG), "lru_lambda": (1, D_RG), "rg_norm_g": (1, D_RG), "hg_lower_bound": (2, D_HG),
         "hg_norm_g": (1, HG_HEAD_DIM), "ffn_norm_g": (1, D_MODEL), "final_norm_g": (1, D_MODEL)}
SMALL_NAMES = tuple(SMALL)
SHARDED_SMALL = ("meta_tokens", "conv_w")


def _adamw_small(vec, gates, w, m, v):
    n = len(SMALL_NAMES)

    def body(*refs):
        vec_ref, gates_ref = refs[:2]
        w_refs, m_refs, v_refs = refs[2:2 + n], refs[2 + n:2 + 2 * n], refs[2 + 2 * n:2 + 3 * n]
        outs = refs[2 + 3 * n:]
        loss_ref = outs[0]
        x, y, _ = _place()
        chip = 2 * x + y
        loss_ref[...] = vec_ref[VEC_ROW["loss"]:VEC_ROW["loss"] + 1, 0:1]

        def update(k, g):
            g_ref, d_ref, nm_ref, nv_ref = outs[1 + 4 * k:5 + 4 * k]
            g_ref[...] = g
            d_ref[...], nm_ref[...], nv_ref[...] = _adamw_math(w_refs[k][...], g, m_refs[k][...], v_refs[k][...])

        for k, name in enumerate(SMALL_NAMES):
            nr, w_ = SMALL[name]
            if name == "w_rgate":
                update(k, gates_ref[0:D_RG, :])
            elif name == "w_igate":
                update(k, gates_ref[D_RG:2 * D_RG, :])
            elif name in SHARDED_SMALL:
                r0 = VEC_ROW[name]
                for q in range(N_CHIPS):
                    @pl.when(chip == q)
                    def _(k=k, r0=r0, nr=nr, w_=w_, q=q):
                        update(k, vec_ref[r0:r0 + nr, q * w_:(q + 1) * w_])
            else:
                r0 = VEC_ROW[name]
                update(k, vec_ref[r0:r0 + nr, 0:w_])

    vmem = pl.BlockSpec(memory_space=pltpu.VMEM)
    out_shape = [jax.ShapeDtypeStruct((1, 1), F32)]
    for name in SMALL_NAMES:
        out_shape += [jax.ShapeDtypeStruct(SMALL[name], F32)] * 4
    outs = pl.pallas_call(
        body, in_specs=[vmem] * (2 + 3 * n), out_specs=[vmem] * len(out_shape), out_shape=out_shape,
        name="adamw_small",
    )(vec, gates, *[w[k] for k in SMALL_NAMES], *[m[k] for k in SMALL_NAMES], *[v[k] for k in SMALL_NAMES])
    loss = outs[0]
    res = {name: tuple(outs[1 + 4 * k:5 + 4 * k]) for k, name in enumerate(SMALL_NAMES)}
    return loss, res


WEIGHT_NAMES = ("meta_tokens", "mix_norm_g", "w_in", "conv_w", "conv_b", "w_rgate", "b_rgate", "w_igate", "b_igate",
                "lru_lambda", "rg_norm_g", "hg_lower_bound", "hg_norm_g", "w_out", "ffn_norm_g", "w_gate_up", "w_down",
                "final_norm_g")


def _to_2d(name, a):
    if name in BIG:
        return a.reshape(BIG[name][:2])
    return a.reshape(SMALL[name])


def kernel(x, meta_tokens, mix_norm_g, w_in, conv_w, conv_b, w_rgate, b_rgate, w_igate, b_igate, lru_lambda, rg_norm_g, hg_lower_bound, hg_norm_g, w_out, ffn_norm_g, w_gate_up, w_down, final_norm_g, loss_target, m_meta_tokens, m_mix_norm_g, m_w_in, m_conv_w, m_conv_b, m_w_rgate, m_b_rgate, m_w_igate, m_b_igate, m_lru_lambda, m_rg_norm_g, m_hg_lower_bound, m_hg_norm_g, m_w_out, m_ffn_norm_g, m_w_gate_up, m_w_down, m_final_norm_g, v_meta_tokens, v_mix_norm_g, v_w_in, v_conv_w, v_conv_b, v_w_rgate, v_b_rgate, v_w_igate, v_b_igate, v_lru_lambda, v_rg_norm_g, v_hg_lower_bound, v_hg_norm_g, v_w_out, v_ffn_norm_g, v_w_gate_up, v_w_down, v_final_norm_g):
    w_raw = dict(zip(WEIGHT_NAMES, (meta_tokens, mix_norm_g, w_in, conv_w, conv_b, w_rgate, b_rgate, w_igate, b_igate,
                                    lru_lambda, rg_norm_g, hg_lower_bound, hg_norm_g, w_out, ffn_norm_g, w_gate_up,
                                    w_down, final_norm_g)))
    m_raw = dict(zip(WEIGHT_NAMES, (m_meta_tokens, m_mix_norm_g, m_w_in, m_conv_w, m_conv_b, m_w_rgate, m_b_rgate,
                                    m_w_igate, m_b_igate, m_lru_lambda, m_rg_norm_g, m_hg_lower_bound, m_hg_norm_g,
                                    m_w_out, m_ffn_norm_g, m_w_gate_up, m_w_down, m_final_norm_g)))
    v_raw = dict(zip(WEIGHT_NAMES, (v_meta_tokens, v_mix_norm_g, v_w_in, v_conv_w, v_conv_b, v_w_rgate, v_b_rgate,
                                    v_w_igate, v_b_igate, v_lru_lambda, v_rg_norm_g, v_hg_lower_bound, v_hg_norm_g,
                                    v_w_out, v_ffn_norm_g, v_w_gate_up, v_w_down, v_final_norm_g)))
    w = {k: _to_2d(k, a) for k, a in w_raw.items()}
    m = {k: _to_2d(k, a) for k, a in m_raw.items()}
    v = {k: _to_2d(k, a) for k, a in v_raw.items()}

    x_i, y_i, c_i = _place()
    core = jnp.reshape(c_i, (1,)).astype(jnp.int32)
    chip = jnp.reshape(2 * x_i + y_i, (1,)).astype(jnp.int32)
    chip_core = jnp.concatenate([chip, core])

    placed, (meta_full, cw_full) = _place_shards(w, [w["meta_tokens"], w["conv_w"]], chip)
    first, _ = _gather_weights(placed, [], ("w_in",), "gather_first", 1)
    rest, _ = _gather_weights(placed, [], ("w_out", "w_gate_up", "w_down"), "gather_rest", 2)
    full = {**first, **rest}

    seq = x.shape[1]
    small ={k: w[k] for k in SMALL_NAMES if k not in SHARDED_SMALL}
    small["conv_w"] = cw_full

    def reduce_to_chips(grads, names, tag, collective_ids):
        got = _exchange_halves(grads, names, "exchange_halves_" + tag, collective_ids[0])

        def chip_sums():
            return _chip_sum(grads, got, names, core, "chip_sum_" + tag)

        def send(sums):
            arrived = _send_chip_sums({n: sums[n][1] for n in names}, names, "send_chip_sums_" + tag,
                                      collective_ids[1])
            return {n: (sums[n][0], a) for n, a in zip(names, arrived)}

        return chip_sums, send

    ffn_names, mixer_names = ("w_gate_up", "w_down", "w_out"), ("w_in",)
    loss, grad_x, grads, parts, parts_mixer = _local_step(
        x.reshape(seq, D_MODEL), meta_full, loss_target.reshape(seq, D_MODEL),
        w["w_in"], full["w_in"], full["w_out"], full["w_gate_up"], full["w_down"], small, chip,
        on_ffn_grads=lambda g: reduce_to_chips(g, ffn_names, "ffn", (3, 4)),
        on_mixer_grads=lambda g: reduce_to_chips(g, mixer_names, "mixer", (None, 5)))
    parts.update(parts_mixer)
    totals = _total(parts, chip_core)
    pieces = {k: grads[k] for k in VEC_ROW if k != "loss"}
    pieces["loss"] = loss
    vec, gates, g_big = _all_reduce_small(pieces, grads["w_gates"], totals)
    loss_sum, res = _adamw_small(vec, gates, w, m, v)
    res.update(_adamw_big(w, g_big, m, v))

    out = [loss_sum.reshape(()), grad_x.reshape(1, seq, D_MODEL)]
    for j in range(4):
        out += [res[n][j].reshape(w_raw[n].shape) for n in WEIGHT_NAMES]
    return tuple(out)
```

```python
import math

import jax
import jax.numpy as jnp
from jax import lax
from jax.experimental import pallas as pl
from jax.experimental.pallas import tpu as pltpu
from jax.experimental.pallas import tpu_sc as plsc

F32 = jnp.float32
BF16 = jnp.bfloat16
MESH = pl.DeviceIdType.MESH

D_MODEL = 1024
D_RG = 512
RG_HEAD_DIM = 64
D_HG = 512
HG_HEAD_DIM = 128
HG_HEADS = 4
CHUNK = 64
SUB = 16
N_SUB = CHUNK // SUB
N_META = 16
PAD = CHUNK - N_META
D_IN = 3072
D_FF = 2816
CONV_W = 4
LRU_C = 8.0
EPS = 1e-6
EXP_CLAMP = 80.0
GELU_C = math.sqrt(2.0 / math.pi)
GELU_A = 0.044715
N_CHIPS = 4

ADAM_LR = 0.001
ADAM_B1 = 0.9
ADAM_B2 = 0.999
ADAM_EPS = 1e-08
ADAM_WD = 0.01
ADAM_STEP = 10

VMEM_LIMIT = 56 * 1024 * 1024


def _params(*sem):
    return pltpu.CompilerParams(dimension_semantics=sem, vmem_limit_bytes=VMEM_LIMIT)


def _row_tile(rows, target):
    best = None
    for t in range(16, min(rows, target) + 1, 16):
        if rows % t == 0:
            best = t
    assert best is not None, rows
    return best


def _sigmoid(x):
    return 0.5 * jnp.tanh(0.5 * x) + 0.5


def _dot(a, b):
    return jnp.dot(a, b, preferred_element_type=F32)


def _dot_nt(a, b):
    return lax.dot_general(a, b, (((1,), (1,)), ((), ())), preferred_element_type=F32)


def _dot_tn(a, b):
    return lax.dot_general(a, b, (((0,), (0,)), ((), ())), preferred_element_type=F32)


def _rms(x):
    return lax.rsqrt(jnp.mean(x * x, axis=-1, keepdims=True) + EPS)


def _rms_bwd(dn, n, r):
    return r * (dn - n * jnp.mean(dn * n, axis=-1, keepdims=True))


def _gelu_parts(x):
    t = jnp.tanh(GELU_C * (x + GELU_A * x * x * x))
    g = 0.5 * x * (1.0 + t)
    dg = 0.5 * (1.0 + t) + 0.5 * x * (1.0 - t * t) * GELU_C * (1.0 + 3.0 * GELU_A * x * x)
    return g, dg


def _softplus_neg(lam):
    e = jnp.exp(-jnp.abs(lam))
    w = 1.0 + e
    log1p = jnp.where(w == 1.0, e, jnp.log(w) * e / (w - 1.0))
    return jnp.maximum(-lam, 0.0) + log1p


def _head_mask():
    r = lax.broadcasted_iota(jnp.int32, (D_RG, D_RG), 0) // RG_HEAD_DIM
    c = lax.broadcasted_iota(jnp.int32, (D_RG, D_RG), 1) // RG_HEAD_DIM
    return r == c


def _head_fold():
    r = lax.broadcasted_iota(jnp.int32, (D_RG, RG_HEAD_DIM), 0) % RG_HEAD_DIM
    c = lax.broadcasted_iota(jnp.int32, (D_RG, RG_HEAD_DIM), 1)
    return (r == c).astype(F32)


def _gate_weights(w_r, w_i):
    def body(wr_ref, wi_ref, o_ref):
        fold = _head_fold()
        mask = _head_mask()
        for k, ref in enumerate((wr_ref, wi_ref)):
            full = _dot_nt(ref[...].astype(BF16), fold.astype(BF16))
            o_ref[:, k * D_RG:(k + 1) * D_RG] = jnp.where(mask, full, 0.0).astype(BF16)

    return pl.pallas_call(
        body, out_shape=jax.ShapeDtypeStruct((D_RG, 2 * D_RG), BF16), name="gate_weights",
    )(w_r, w_i)


HEAD = PAD + N_META


def _window_copies(seq_hbm, buf, sems, tm):
    def first(to_vmem):
        seq, vm = seq_hbm.at[pl.ds(0, tm - HEAD)], buf.at[0, pl.ds(HEAD, tm - HEAD)]
        return pltpu.make_async_copy(seq, vm, sems.at[0]) if to_vmem else pltpu.make_async_copy(vm, seq, sems.at[0])

    def later(j, slot, to_vmem):
        seq, vm = seq_hbm.at[pl.ds(pl.multiple_of(j * tm - HEAD, 8), tm)], buf.at[slot]
        if to_vmem:
            return pltpu.make_async_copy(seq, vm, sems.at[slot])
        return pltpu.make_async_copy(vm, seq, sems.at[slot])

    return first, later


def _fetch_window(seq_hbm, buf, sems, i, n_steps, tm):
    first, later = _window_copies(seq_hbm, buf, sems, tm)
    slot = i % 2

    @pl.when(i == 0)
    def _():
        first(True).start()

    if n_steps > 1:
        @pl.when(i + 1 < n_steps)
        def _():
            later(i + 1, 1 - slot, True).start()

    @pl.when(i == 0)
    def _():
        first(True).wait()

    if n_steps > 1:
        @pl.when(i > 0)
        def _():
            later(i, slot, True).wait()

    return slot


def _in_proj_local(x, meta, g1, w_own, chip):
    T = x.shape[0] + HEAD
    tm = _row_tile(T, 832)
    n_steps = T // tm
    cols = BIG["w_in"][1]

    def body(s_ref, x_hbm, meta_ref, g_ref, w_ref, p_ref, u_ref, h_ref, buf, sems, wb):
        i = pl.program_id(0)
        slot = _fetch_window(x_hbm, buf, sems, i, n_steps, tm)

        @pl.when(i == 0)
        def _():
            buf[0, 0:PAD, :] = jnp.zeros((PAD, D_MODEL), F32)
            buf[0, PAD:HEAD, :] = meta_ref[...]
            wb[...] = w_ref[...].astype(BF16)

        h = buf[slot]
        h_ref[...] = h
        u = (h * _rms(h) * g_ref[...]).astype(BF16)
        u_ref[...] = u
        p_ref[...] = _dot(u, wb[...])

    return pl.pallas_call(
        body,
        grid_spec=pltpu.PrefetchScalarGridSpec(
            num_scalar_prefetch=1, grid=(n_steps,),
            in_specs=[pl.BlockSpec(memory_space=pl.ANY),
                      pl.BlockSpec((N_META, D_MODEL), lambda i, s: (0, 0)),
                      pl.BlockSpec((1, D_MODEL), lambda i, s: (0, 0)),
                      pl.BlockSpec((D_MODEL, cols), lambda i, s: (0, 0))],
            out_specs=[pl.BlockSpec((tm, cols), lambda i, s: (i, s[0])),
                       pl.BlockSpec((tm, D_MODEL), lambda i, s: (i, 0)),
                       pl.BlockSpec((tm, D_MODEL), lambda i, s: (i, 0))],
            scratch_shapes=[pltpu.VMEM((2, tm, D_MODEL), F32), pltpu.SemaphoreType.DMA((2,)),
                            pltpu.VMEM((D_MODEL, cols), BF16)]),
        out_shape=[jax.ShapeDtypeStruct((T, D_IN), F32), jax.ShapeDtypeStruct((T, D_MODEL), BF16),
                   jax.ShapeDtypeStruct((T, D_MODEL), F32)],
        name="in_proj_local", compiler_params=_params("arbitrary"),
    )(chip, x, meta, g1, w_own)


def _in_proj_rest(u, w_in, p, chip):
    T = u.shape[0]
    tm = _row_tile(T, 2080)
    cols = BIG["w_in"][1]
    block = lambda j, s: (s[0] + 1 + j) % N_CHIPS

    def body(s_ref, u_ref, w_ref, p_in_ref, p_ref):
        p_ref[...] = _dot(u_ref[...], w_ref[...])

    return pl.pallas_call(
        body,
        grid_spec=pltpu.PrefetchScalarGridSpec(
            num_scalar_prefetch=1, grid=(N_CHIPS - 1, T // tm),
            in_specs=[pl.BlockSpec((tm, D_MODEL), lambda j, i, s: (i, 0)),
                      pl.BlockSpec((D_MODEL, cols), lambda j, i, s: (0, block(j, s))), ANY],
            out_specs=pl.BlockSpec((tm, cols), lambda j, i, s: (i, block(j, s)))),
        out_shape=jax.ShapeDtypeStruct((T, D_IN), F32),
        input_output_aliases={3: 0},
        name="in_proj_rest", compiler_params=_params("arbitrary", "arbitrary"),
    )(chip, u, w_in, p)


def _scan_block_fwd(A, B, rowi):
    for d in (1, 2, 4):
        a_sh = pltpu.roll(A, d, axis=0)
        b_sh = pltpu.roll(B, d, axis=0)
        m = rowi >= d
        B = jnp.where(m, A * b_sh + B, B)
        A = jnp.where(m, A * a_sh, A)
    return A, B


def _scan_block_bwd(A, B, rowi):
    for d in (1, 2, 4):
        a_sh = pltpu.roll(A, 8 - d, axis=0)
        b_sh = pltpu.roll(B, 8 - d, axis=0)
        m = rowi < 8 - d
        B = jnp.where(m, A * b_sh + B, B)
        A = jnp.where(m, A * a_sh, A)
    return A, B


def _rg_gates(xc, w_ref, bg_ref, lam):
    pre = _dot(xc.astype(BF16), w_ref[...]) + bg_ref[...]
    r = _sigmoid(pre[:, :D_RG])
    ig = _sigmoid(pre[:, D_RG:])
    sp = _softplus_neg(lam)
    la = -LRU_C * sp * r
    a = jnp.exp(la)
    th = jnp.tanh(la)
    u = 1.0 - th
    rc = pl.reciprocal(u, approx=True)
    rc = rc * (2.0 - u * rc)
    rc = rc * (2.0 - u * rc)
    m2 = -2.0 * th * rc
    inv_m = lax.rsqrt(jnp.maximum(m2, 1e-30))
    return r, ig, sp, a, m2 * inv_m, inv_m


def _conv(ext, cw_ref, cb_ref, tm):
    xc = cb_ref[...] + cw_ref[0:1, :] * ext[8 - 3:8 - 3 + tm, :]
    for j in range(1, CONV_W):
        xc = xc + cw_ref[j:j + 1, :] * ext[8 - 3 + j:8 - 3 + j + tm, :]
    return xc


def _scan_unroll(blocks):
    return 4 if blocks % 4 == 0 else 2 if blocks % 2 == 0 else 1


def _rg_fwd(p, cw, cb, wg, bg, lam, rg_g):
    T = p.shape[0]
    tm = _row_tile(T, 832)
    unroll = _scan_unroll(tm // 8)

    def body(xg_ref, cw_ref, cb_ref, w_ref, bg_ref, lam_ref, g_ref, y_ref, h_ref, xc_ref, ext, a_s, b_s, carry):
        i = pl.program_id(0)

        @pl.when(i == 0)
        def _():
            ext[0:8, :] = jnp.zeros((8, D_RG), F32)
            carry[...] = jnp.zeros((1, D_RG), F32)

        ext[8:8 + tm, :] = xg_ref[:, :D_RG]
        xc = _conv(ext, cw_ref, cb_ref, tm)
        xc_ref[...] = xc
        r, ig, sp, a, m, _ = _rg_gates(xc, w_ref, bg_ref, lam_ref[...])
        row = i * tm + lax.broadcasted_iota(jnp.int32, (tm, 1), 0)
        a_s[...] = a
        b_s[...] = jnp.where(row >= PAD, m * ig * xc, 0.0)
        rowi = lax.broadcasted_iota(jnp.int32, (8, D_RG), 0)

        def blk(j, c):
            for u in range(unroll):
                o = pl.multiple_of((j * unroll + u) * 8, 8)
                A, B = _scan_block_fwd(a_s[pl.ds(o, 8), :], b_s[pl.ds(o, 8), :], rowi)
                h = B + A * c
                h_ref[pl.ds(o, 8), :] = h
                c = h[7:8, :]
            return c

        carry[...] = lax.fori_loop(0, tm // (8 * unroll), blk, carry[...])
        ext[0:8, :] = ext[tm:tm + 8, :]
        g, _ = _gelu_parts(xg_ref[:, D_RG:])
        yy = g * h_ref[...]
        y_ref[...] = (yy * _rms(yy) * g_ref[...]).astype(BF16)

    vec = lambda n: pl.BlockSpec((1, n), lambda i: (0, 0))
    return pl.pallas_call(
        body, grid=(T // tm,),
        in_specs=[pl.BlockSpec((tm, 2 * D_RG), lambda i: (i, 0)),
                  pl.BlockSpec((CONV_W, D_RG), lambda i: (0, 0)), vec(D_RG),
                  pl.BlockSpec((D_RG, 2 * D_RG), lambda i: (0, 0)), vec(2 * D_RG), vec(D_RG), vec(D_RG)],
        out_specs=[pl.BlockSpec((tm, D_RG), lambda i: (i, 0))] * 3,
        out_shape=[jax.ShapeDtypeStruct((T, D_RG), BF16), jax.ShapeDtypeStruct((T, D_RG), F32),
                   jax.ShapeDtypeStruct((T, D_RG), F32)],
        scratch_shapes=[pltpu.VMEM((tm + 8, D_RG), F32), pltpu.VMEM((tm, D_RG), F32),
                        pltpu.VMEM((tm, D_RG), F32), pltpu.VMEM((1, D_RG), F32)],
        name="rg_fwd", compiler_params=_params("arbitrary"),
    )(p, cw, cb, wg, bg, lam, rg_g)


def _running_sum(x, down):
    r = lax.broadcasted_iota(jnp.int32, (CHUNK, CHUNK), 0)
    c = lax.broadcasted_iota(jnp.int32, (CHUNK, CHUNK), 1)
    tri = ((c <= r) if down else (c >= r)).astype(BF16)
    hi = x.astype(BF16)
    rest = x - hi.astype(F32)
    mid = rest.astype(BF16)
    lo = (rest - mid.astype(F32)).astype(BF16)
    return (_dot(tri, hi) + _dot(tri, mid)) + _dot(tri, lo)


def _hg_gates(hq, hf, lbraw_ref, valid):
    lb = _sigmoid(lbraw_ref[0:1, :] - lbraw_ref[1:2, :])
    sq = _sigmoid(hq)
    q = hq * sq
    sf = _sigmoid(hf)
    f = lb + (1.0 - lb) * sf
    lf = jnp.where(valid, jnp.log(f), 0.0)
    b = _running_sum(lf, True)
    return lb, sq, q, sf, f, b


def _hg_head(qh, kh, bh):
    b_last = bh[CHUNK - 1:CHUNK, :]
    refs = [bh[SUB * s:SUB * s + 1, :] for s in range(N_SUB)]
    r_sel = jnp.concatenate([jnp.broadcast_to(refs[s], (SUB, HG_HEAD_DIM)) for s in range(N_SUB)], axis=0)
    eb = jnp.exp(bh)
    eq = jnp.exp(bh - r_sel)
    ekh = jnp.exp(b_last - bh)
    ek = [jnp.exp(jnp.minimum(refs[s] - bh[:SUB * (s + 1), :], EXP_CLAMP)) for s in range(N_SUB)]
    qe = qh * eq

    def own_rows(s):
        parts = [jnp.zeros((SUB * s, HG_HEAD_DIM), F32)] if s else []
        parts.append(qe[SUB * s:SUB * (s + 1), :])
        if s < N_SUB - 1:
            parts.append(jnp.zeros((CHUNK - SUB * (s + 1), HG_HEAD_DIM), F32))
        return jnp.concatenate(parts, axis=0)

    q_hat = jnp.concatenate([own_rows(s) for s in range(N_SUB)], axis=1)

    def met_rows(s):
        n = SUB * (s + 1)
        ke = kh[:n, :] * ek[s]
        return ke if n == CHUNK else jnp.concatenate([ke, jnp.zeros((CHUNK - n, HG_HEAD_DIM), F32)], axis=0)

    k_til = jnp.concatenate([met_rows(s) for s in range(N_SUB)], axis=1)
    return b_last, eb, eq, ekh, ek, q_hat, k_til


def _causal():
    r = lax.broadcasted_iota(jnp.int32, (CHUNK, CHUNK), 0)
    c = lax.broadcasted_iota(jnp.int32, (CHUNK, CHUNK), 1)
    return r >= c


def _chunks_per_step(n_chunks):
    for c in (5, 4, 3, 2):
        if n_chunks % c == 0:
            return c
    return 1


def _hg_fwd(p, lbraw, hg_g):
    T = p.shape[0]
    n_chunks = T // CHUNK
    cps = _chunks_per_step(n_chunks)
    rows = cps * CHUNK

    def body(hq_ref, hf_ref, hi_ref, hg_ref, lb_ref, g_ref, y_ref, o_ref, st_all_ref, st):
        i = pl.program_id(0)

        @pl.when(i == 0)
        def _():
            st[...] = jnp.zeros_like(st)

        def chunk(j, carry):
            rs = pl.ds(pl.multiple_of(j * CHUNK, CHUNK), CHUNK)
            chunk_body(i * cps + j, hq_ref.at[rs, :], hf_ref.at[rs, :], hi_ref.at[rs, :], hg_ref.at[rs, :], lb_ref,
                       g_ref, y_ref.at[rs, :], o_ref.at[rs, :], st_all_ref.at[pl.ds(j, 1)], st)
            return carry

        lax.fori_loop(0, cps, chunk, 0, unroll=True)

    def chunk_body(n, hq_ref, hf_ref, hi_ref, hg_ref, lb_ref, g_ref, y_ref, o_ref, st_all_ref, st):
        valid = (n * CHUNK + lax.broadcasted_iota(jnp.int32, (CHUNK, 1), 0)) >= PAD
        hq, hf, v, hg = hq_ref[...], hf_ref[...], hi_ref[...], hg_ref[...]
        lb, sq, q, sf, f, b = _hg_gates(hq, hf, lb_ref, valid)
        k = 1.0 - f
        st_all_ref[0] = st[...]
        causal = _causal()
        v_t = v.T.astype(BF16)
        heads = [slice(h * HG_HEAD_DIM, (h + 1) * HG_HEAD_DIM) for h in range(HG_HEADS)]
        fac = []
        for sl in heads:
            qh, kh, bh = q[:, sl], k[:, sl], b[:, sl]
            b_last, eb, _, ekh, _, q_hat, k_til = _hg_head(qh, kh, bh)
            fac.append((jnp.exp(b_last), (qh * eb).astype(BF16), q_hat.astype(BF16), k_til.astype(BF16),
                        (kh * ekh).astype(BF16), v[:, sl].astype(BF16)))
        raw = []
        for sl, (_, q_til, q_hat, k_til, k_hat, _) in zip(heads, fac):
            st_h = st[sl, :]
            raw.append((_dot_nt(q_til, st_h.astype(BF16)), _dot_nt(q_hat, k_til), _dot(v_t[sl, :], k_hat), st_h))
        for sl, (e_last, _, _, _, _, vb), (inter, att, upd, st_h) in zip(heads, fac, raw):
            o = inter + _dot(jnp.where(causal, att, 0.0).astype(BF16), vb)
            st[sl, :] = st_h * e_last + upd
            o_ref[:, sl] = o
            hgh = hg[:, sl]
            y_ref[:, sl] = (o * _rms(o) * g_ref[...] * (hgh * _sigmoid(hgh))).astype(BF16)

    col = lambda j: pl.BlockSpec((rows, D_HG), lambda n: (n, j))
    return pl.pallas_call(
        body, grid=(n_chunks // cps,),
        in_specs=[col(2), col(3), col(4), col(5),
                  pl.BlockSpec((2, D_HG), lambda n: (0, 0)), pl.BlockSpec((1, HG_HEAD_DIM), lambda n: (0, 0))],
        out_specs=[pl.BlockSpec((rows, D_HG), lambda n: (n, 0)), pl.BlockSpec((rows, D_HG), lambda n: (n, 0)),
                   pl.BlockSpec((cps, D_HG, HG_HEAD_DIM), lambda n: (n, 0, 0))],
        out_shape=[jax.ShapeDtypeStruct((T, D_HG), BF16), jax.ShapeDtypeStruct((T, D_HG), F32),
                   jax.ShapeDtypeStruct((n_chunks, D_HG, HG_HEAD_DIM), F32)],
        scratch_shapes=[pltpu.VMEM((D_HG, HG_HEAD_DIM), F32)],
        name="hg_fwd", compiler_params=_params("arbitrary"),
    )(p, p, p, p, lbraw, hg_g)


def _ffn_fwd(h0, y_rg, y_hg, w_out, g2, w_gu, w_down, gf, target):
    T = h0.shape[0]
    tm = _row_tile(T, 320)
    n_steps = T // tm

    def body(h_ref, yr_ref, yh_ref, wo_ref, g2_ref, wgu_ref, wd_ref, gf_ref, t_hbm,
             h1_ref, v_ref, y_ref, gu_ref, act_ref, dh2_ref, dh2b_ref, loss_ref, gg_ref, tbuf, sems):
        i = pl.program_id(0)
        slot = _fetch_window(t_hbm, tbuf, sems, i, n_steps, tm)

        @pl.when(i == 0)
        def _():
            loss_ref[...] = jnp.zeros_like(loss_ref)
            gg_ref[...] = jnp.zeros_like(gg_ref)
            tbuf[0, 0:HEAD, :] = jnp.zeros((HEAD, D_MODEL), F32)

        y_ref[:, :D_RG] = yr_ref[...]
        y_ref[:, D_RG:] = yh_ref[...]
        h1 = h_ref[...] + _dot(y_ref[...], wo_ref[...])
        h1_ref[...] = h1
        v = (h1 * _rms(h1) * g2_ref[...]).astype(BF16)
        v_ref[...] = v

        gu = _dot(v, wgu_ref[...])
        gu_ref[...] = gu.astype(BF16)
        g = gu[:, :D_FF]
        act = (g * _sigmoid(g) * gu[:, D_FF:]).astype(BF16)
        act_ref[...] = act

        h2 = h1 + _dot(act, wd_ref[...])
        r = _rms(h2)
        n = h2 * r
        gf_ = gf_ref[...]
        row = i * tm + lax.broadcasted_iota(jnp.int32, (tm, 1), 0)
        err = jnp.where(row >= HEAD, n * gf_ - tbuf[slot], 0.0)
        loss_ref[...] += 0.5 * jnp.sum(jnp.mean(err * err, axis=-1, keepdims=True), axis=0, keepdims=True)
        dy = err * (1.0 / D_MODEL)
        gg_ref[...] += jnp.sum(dy * n, axis=0, keepdims=True)
        dh2 = _rms_bwd(dy * gf_, n, r)
        dh2_ref[...] = dh2
        dh2b_ref[...] = dh2.astype(BF16)

    row_spec = lambda n: pl.BlockSpec((tm, n), lambda i: (i, 0))
    vec = pl.BlockSpec((1, D_MODEL), lambda i: (0, 0))
    return pl.pallas_call(
        body, grid=(n_steps,),
        in_specs=[row_spec(D_MODEL), row_spec(D_RG), row_spec(D_HG), _resident((D_MODEL, D_MODEL)), vec,
                  _resident((D_MODEL, 2 * D_FF)), _resident((D_FF, D_MODEL)), vec,
                  pl.BlockSpec(memory_space=pl.ANY)],
        out_specs=[row_spec(D_MODEL), row_spec(D_MODEL), row_spec(D_MODEL), row_spec(2 * D_FF), row_spec(D_FF),
                   row_spec(D_MODEL), row_spec(D_MODEL), pl.BlockSpec((1, 1), lambda i: (0, 0)), vec],
        out_shape=[jax.ShapeDtypeStruct((T, D_MODEL), F32), jax.ShapeDtypeStruct((T, D_MODEL), BF16),
                   jax.ShapeDtypeStruct((T, D_MODEL), BF16), jax.ShapeDtypeStruct((T, 2 * D_FF), BF16),
                   jax.ShapeDtypeStruct((T, D_FF), BF16), jax.ShapeDtypeStruct((T, D_MODEL), F32),
                   jax.ShapeDtypeStruct((T, D_MODEL), BF16), jax.ShapeDtypeStruct((1, 1), F32),
                   jax.ShapeDtypeStruct((1, D_MODEL), F32)],
        scratch_shapes=[pltpu.VMEM((2, tm, D_MODEL), F32), pltpu.SemaphoreType.DMA((2,))],
        name="ffn_fwd", compiler_params=_params("arbitrary"),
    )(h0, y_rg, y_hg, w_out, g2, w_gu, w_down, gf, target)


def _resident(shape):
    return pl.BlockSpec(shape, lambda i: (0,) * len(shape), pipeline_mode=pl.Buffered(1))


def _ffn_bwd(dh2b, gu, w_down, w_gu, h1, g2, dh2, w_out):
    T = h1.shape[0]
    tm = _row_tile(T, 320)

    def body(d_ref, gu_ref, wd_ref, wgu_ref, h_ref, g_ref, d2_ref, wo_ref, dgu_ref, dh1_ref, dh1b_ref, dy_ref, gg_ref):
        i = pl.program_id(0)

        @pl.when(i == 0)
        def _():
            gg_ref[...] = jnp.zeros_like(gg_ref)

        dact = _dot_nt(d_ref[...], wd_ref[...]).astype(BF16)
        g = gu_ref[:, :D_FF]
        u = gu_ref[:, D_FF:]
        s = _sigmoid(g)
        dgu_ref[:, :D_FF] = dact * u * (s * (1.0 + g * (1.0 - s)))
        dgu_ref[:, D_FF:] = dact * (g * s)

        dv = _dot_nt(dgu_ref[...], wgu_ref[...])
        h1_ = h_ref[...]
        r = _rms(h1_)
        n = h1_ * r
        gg_ref[...] += jnp.sum(dv * n, axis=0, keepdims=True)
        dh1 = d2_ref[...] + _rms_bwd(dv * g_ref[...], n, r)
        dh1_ref[...] = dh1
        db = dh1.astype(BF16)
        dh1b_ref[...] = db
        dy_ref[...] = _dot_nt(db, wo_ref[...])

    row = lambda n: pl.BlockSpec((tm, n), lambda i: (i, 0))
    return pl.pallas_call(
        body, grid=(T // tm,),
        in_specs=[row(D_MODEL), row(2 * D_FF), _resident((D_FF, D_MODEL)), _resident((D_MODEL, 2 * D_FF)),
                  row(D_MODEL), pl.BlockSpec((1, D_MODEL), lambda i: (0, 0)), row(D_MODEL),
                  _resident((D_MODEL, D_MODEL))],
        out_specs=[row(2 * D_FF), row(D_MODEL), row(D_MODEL), row(D_MODEL),
                   pl.BlockSpec((1, D_MODEL), lambda i: (0, 0))],
        out_shape=[jax.ShapeDtypeStruct((T, 2 * D_FF), BF16), jax.ShapeDtypeStruct((T, D_MODEL), F32),
                   jax.ShapeDtypeStruct((T, D_MODEL), BF16), jax.ShapeDtypeStruct((T, D_MODEL), F32),
                   jax.ShapeDtypeStruct((1, D_MODEL), F32)],
        name="ffn_bwd", compiler_params=_params("arbitrary"),
    )(dh2b, gu, w_down, w_gu, h1, g2, dh2, w_out)


def _rg_bwd(p, xc_all, hs, dy, dp, cw, cb, wg, bg, lam, rg_g):
    T = p.shape[0]
    tm = _row_tile(T, 832)
    nt = T // tm
    hb = tm // 8
    unroll = _scan_unroll(hb)

    def body(xg_ref, xc_ref, h_ref, hh_ref, dy_ref, dp_in_ref, cw_ref, cb_ref, w_ref, bg_ref, lam_ref, g_ref,
             dp_ref, gcw_ref, gcb_ref, gw_ref, gbg_ref, glam_ref, gg_ref,
             dext, a_s, b_s, d_s, gacc, carry_d, carry_a):
        i = pl.program_id(0)
        t_idx = nt - 1 - i

        @pl.when(i == 0)
        def _():
            dext[tm:tm + 8, :] = jnp.zeros((8, D_RG), F32)
            carry_d[...] = jnp.zeros_like(carry_d)
            carry_a[...] = jnp.zeros_like(carry_a)
            gacc[...] = jnp.zeros_like(gacc)
            for ref in (gcw_ref, gcb_ref, gbg_ref, glam_ref, gg_ref, gw_ref):
                ref[...] = jnp.zeros_like(ref)

        first = t_idx == 0
        xc = xc_ref[...]
        lam_ = lam_ref[...]
        r, ig, sp, a, m, inv_m = _rg_gates(xc, w_ref, bg_ref, lam_)
        row = t_idx * tm + lax.broadcasted_iota(jnp.int32, (tm, 1), 0)
        valid = row >= PAD

        gr = xg_ref[:, D_RG:]
        g, dgelu = _gelu_parts(gr)
        h = h_ref[...]
        yy = g * h
        rr = _rms(yy)
        nn = yy * rr
        dy_ = dy_ref[...]
        gg_ref[...] += jnp.sum(dy_ * nn, axis=0, keepdims=True)
        dyy = _rms_bwd(dy_ * g_ref[...], nn, rr)
        dp_ref[:, D_RG:] = (dyy * h * dgelu).astype(BF16)

        a_s[...] = a
        b_s[...] = dyy * g
        rowi = lax.broadcasted_iota(jnp.int32, (8, D_RG), 0)

        def blk(jj, c):
            cd, ca = c
            for u in range(unroll):
                o = pl.multiple_of((hb - 1 - (jj * unroll + u)) * 8, 8)
                a_blk = a_s[pl.ds(o, 8), :]
                a_next = jnp.where(rowi == 7, ca, pltpu.roll(a_blk, 7, axis=0))
                A, B = _scan_block_bwd(a_next, b_s[pl.ds(o, 8), :], rowi)
                d = B + A * cd
                d_s[pl.ds(o, 8), :] = d
                cd, ca = d[0:1, :], a_blk[0:1, :]
            return cd, ca

        cd, ca = lax.fori_loop(0, hb // unroll, blk, (carry_d[...], carry_a[...]))
        carry_d[...] = cd
        carry_a[...] = ca
        delta = d_s[...]

        h_last_prev = jnp.where(first, 0.0, hh_ref[7:8, :])
        row0 = lax.broadcasted_iota(jnp.int32, (tm, 1), 0) == 0
        h_prev = jnp.where(row0, h_last_prev, pltpu.roll(h, 1, axis=0))
        dbx = jnp.where(valid, delta, 0.0)
        da = delta * h_prev
        di = dbx * m * xc
        dm = dbx * ig * xc
        dla = a * (da - dm * a * inv_m)
        dla = jnp.where(valid, dla, 0.0)
        glam_ref[...] += jnp.sum(dla * r, axis=0, keepdims=True) * (LRU_C / (1.0 + jnp.exp(lam_)))
        dr = (-LRU_C) * sp * dla
        dpre = jnp.concatenate([dr * r * (1.0 - r), di * ig * (1.0 - ig)], axis=1)
        gbg_ref[...] += jnp.sum(dpre, axis=0, keepdims=True)
        dpre_b = dpre.astype(BF16)
        gacc[...] += _dot_tn(xc.astype(BF16), dpre_b)
        dxc = dbx * m * ig + _dot_nt(dpre_b, w_ref[...])
        gcb_ref[...] += jnp.sum(dxc, axis=0, keepdims=True)
        dext[0:tm, :] = dxc
        xr = xg_ref[:, :D_RG]
        dxr = None
        for j in range(CONV_W):
            shifted = dext[3 - j:3 - j + tm, :]
            gcw_ref[j:j + 1, :] += jnp.sum(xr * shifted, axis=0, keepdims=True)
            tap = cw_ref[j:j + 1, :] * shifted
            dxr = tap if dxr is None else dxr + tap
        dp_ref[:, :D_RG] = dxr.astype(BF16)
        dext[tm:tm + 8, :] = dext[0:8, :]

        @pl.when(i == nt - 1)
        def _():
            fold = _head_fold()
            mask = _head_mask()
            fold_b = fold.astype(BF16)
            for k in range(2):
                blockdiag = jnp.where(mask, gacc[:, k * D_RG:(k + 1) * D_RG], 0.0)
                hi = blockdiag.astype(BF16)
                rest = blockdiag - hi.astype(F32)
                mid = rest.astype(BF16)
                lo = (rest - mid.astype(F32)).astype(BF16)
                gw_ref[k * D_RG:(k + 1) * D_RG, :] = (_dot(hi, fold_b) + _dot(mid, fold_b)) + _dot(lo, fold_b)

    vec = lambda n: pl.BlockSpec((1, n), lambda i: (0, 0))
    rev = lambda n: pl.BlockSpec((tm, n), lambda i: (nt - 1 - i, 0))
    halo = lambda n: pl.BlockSpec((8, n), lambda i: (jnp.maximum((nt - 1 - i) * hb - 1, 0), 0))
    return pl.pallas_call(
        body, grid=(nt,),
        in_specs=[rev(2 * D_RG), rev(D_RG), rev(D_RG), halo(D_RG), rev(D_RG), ANY,
                  pl.BlockSpec((CONV_W, D_RG), lambda i: (0, 0)), vec(D_RG),
                  pl.BlockSpec((D_RG, 2 * D_RG), lambda i: (0, 0)), vec(2 * D_RG), vec(D_RG), vec(D_RG)],
        out_specs=[rev(2 * D_RG), pl.BlockSpec((CONV_W, D_RG), lambda i: (0, 0)), vec(D_RG),
                   pl.BlockSpec((2 * D_RG, RG_HEAD_DIM), lambda i: (0, 0)), vec(2 * D_RG), vec(D_RG), vec(D_RG)],
        input_output_aliases={5: 0},
        out_shape=[jax.ShapeDtypeStruct((T, D_IN), BF16), jax.ShapeDtypeStruct((CONV_W, D_RG), F32),
                   jax.ShapeDtypeStruct((1, D_RG), F32), jax.ShapeDtypeStruct((2 * D_RG, RG_HEAD_DIM), F32),
                   jax.ShapeDtypeStruct((1, 2 * D_RG), F32), jax.ShapeDtypeStruct((1, D_RG), F32),
                   jax.ShapeDtypeStruct((1, D_RG), F32)],
        scratch_shapes=[pltpu.VMEM((tm + 8, D_RG), F32),
                        pltpu.VMEM((tm, D_RG), F32), pltpu.VMEM((tm, D_RG), F32), pltpu.VMEM((tm, D_RG), F32),
                        pltpu.VMEM((D_RG, 2 * D_RG), F32), pltpu.VMEM((1, D_RG), F32), pltpu.VMEM((1, D_RG), F32)],
        name="rg_bwd", compiler_params=_params("arbitrary"),
    )(p, xc_all, hs, hs, dy, dp, cw, cb, wg, bg, lam, rg_g)


def _hg_bwd(p, o_all, st_all, dy, lbraw, hg_g):
    T = p.shape[0]
    n_chunks = T // CHUNK
    cps = _chunks_per_step(n_chunks)
    rows = cps * CHUNK
    n_steps = n_chunks // cps

    def body(hq_ref, hf_ref, hi_ref, hg_ref, o_ref, st_ref, dy_ref, lb_ref, g_ref,
             dp_ref, glb_ref, gg_ref, dst):
        i = pl.program_id(0)

        @pl.when(i == 0)
        def _():
            dst[...] = jnp.zeros_like(dst)
            glb_ref[...] = jnp.zeros_like(glb_ref)
            gg_ref[...] = jnp.zeros_like(gg_ref)

        dp_ref[:, :2 * D_RG] = jnp.zeros((rows, 2 * D_RG), BF16)

        def chunk(jj, carry):
            j = cps - 1 - jj
            rs = pl.ds(pl.multiple_of(j * CHUNK, CHUNK), CHUNK)
            chunk_body((n_steps - 1 - i) * cps + j, hq_ref.at[rs, :], hf_ref.at[rs, :], hi_ref.at[rs, :],
                       hg_ref.at[rs, :], o_ref.at[rs, :], st_ref.at[pl.ds(j, 1)], dy_ref.at[rs, :], lb_ref, g_ref,
                       dp_ref.at[rs, pl.ds(2 * D_RG, 4 * D_HG)], glb_ref, gg_ref, dst)
            return carry

        lax.fori_loop(0, cps, chunk, 0, unroll=True)

    def chunk_body(n, hq_ref, hf_ref, hi_ref, hg_ref, o_ref, st_ref, dy_ref, lb_ref, g_ref,
                   dp_ref, glb_ref, gg_ref, dst):
        valid = (n * CHUNK + lax.broadcasted_iota(jnp.int32, (CHUNK, 1), 0)) >= PAD
        hq, hf, v, hg = hq_ref[...], hf_ref[...], hi_ref[...], hg_ref[...]
        lb, sq, q, sf, f, b = _hg_gates(hq, hf, lb_ref, valid)
        k = 1.0 - f
        causal = _causal()
        r_i = lax.broadcasted_iota(jnp.int32, (CHUNK, CHUNK), 0)
        c_i = lax.broadcasted_iota(jnp.int32, (CHUNK, CHUNK), 1)
        causal_t = r_i <= c_i
        is_last = lax.broadcasted_iota(jnp.int32, (CHUNK, 1), 0) == CHUNK - 1
        g_ = g_ref[...]
        db_parts, dq_parts, dk_parts = [], [], []
        gg = jnp.zeros((1, HG_HEAD_DIM), F32)
        heads = [slice(h * HG_HEAD_DIM, (h + 1) * HG_HEAD_DIM) for h in range(HG_HEADS)]

        do_parts = []
        for h, sl in enumerate(heads):
            o = o_ref[:, sl]
            ro = _rms(o)
            no = o * ro
            hgh = hg[:, sl]
            sg = _sigmoid(hgh)
            dyh = dy_ref[:, sl]
            dp_ref[:, 3 * D_HG + h * HG_HEAD_DIM:3 * D_HG + (h + 1) * HG_HEAD_DIM] = (
                dyh * no * g_ * sg * (1.0 + hgh * (1.0 - sg))).astype(BF16)
            dng = dyh * hgh * sg
            gg = gg + jnp.sum(dng * no, axis=0, keepdims=True)
            do_parts.append(_rms_bwd(dng * g_, no, ro))
        do_t = jnp.concatenate(do_parts, axis=1).T.astype(BF16)

        fac = []
        for sl, do in zip(heads, do_parts):
            qh, kh, bh = q[:, sl], k[:, sl], b[:, sl]
            b_last, eb, eq, ekh, ek, q_hat, k_til = _hg_head(qh, kh, bh)
            fac.append(dict(qh=qh, kh=kh, e_last=jnp.exp(b_last), eb=eb, eq=eq, ekh=ekh, ek=ek,
                            q_til=qh * eb, k_hat=kh * ekh, qhb=q_hat.astype(BF16), ktb=k_til.astype(BF16),
                            vb=v[:, sl].astype(BF16), dob=do.astype(BF16)))

        first = []
        for sl, t in zip(heads, fac):
            st_h = st_ref[0, sl, :]
            dst_h = dst[sl, :]
            dstb = dst_h.astype(BF16)
            first.append(dict(
                att_t=_dot_nt(t["ktb"], t["qhb"]), datt=_dot_nt(t["dob"], t["vb"]),
                datt_t=_dot_nt(t["vb"], t["dob"]), dk_hat=_dot(t["vb"], dstb),
                dv=_dot_nt(t["k_hat"].astype(BF16), dstb), dq_til=_dot(t["dob"], st_h.astype(BF16)),
                state=t["e_last"] * jnp.sum(dst_h * st_h, axis=0, keepdims=True)))
            dst[sl, :] = dst_h * t["e_last"] + _dot(do_t[sl, :], t["q_til"].astype(BF16))

        for h, (t, m) in enumerate(zip(fac, first)):
            qh, kh, eb, eq, ekh, ek = t["qh"], t["kh"], t["eb"], t["eq"], t["ekh"], t["ek"]
            q_til, k_hat, qhb, ktb, dob = t["q_til"], t["k_hat"], t["qhb"], t["ktb"], t["dob"]
            dk_hat, dq_til = m["dk_hat"], m["dq_til"]
            dv = m["dv"] + _dot(jnp.where(causal_t, m["att_t"], 0.0).astype(BF16), dob)
            dq_hat = _dot(jnp.where(causal, m["datt"], 0.0).astype(BF16), ktb)
            dk_til = _dot(jnp.where(causal_t, m["datt_t"], 0.0).astype(BF16), qhb)
            db_last = jnp.sum(dk_hat * k_hat, axis=0, keepdims=True) + m["state"]
            dq_sel = jnp.concatenate([dq_hat[SUB * s:SUB * (s + 1), s * HG_HEAD_DIM:(s + 1) * HG_HEAD_DIM]
                                      for s in range(N_SUB)], axis=0)
            dq_a = dq_sel * eq
            dk_rows, k_att_rows = [], []
            for b_ in range(N_SUB):
                rs = slice(SUB * b_, SUB * (b_ + 1))
                dk_sum = k_att_sum = None
                for s in range(b_, N_SUB):
                    cs = slice(s * HG_HEAD_DIM, (s + 1) * HG_HEAD_DIM)
                    d = dk_til[rs, cs]
                    t_dk = d * ek[s][rs, :]
                    t_att = ktb[rs, cs].astype(F32) * d
                    dk_sum = t_dk if dk_sum is None else dk_sum + t_dk
                    k_att_sum = t_att if k_att_sum is None else k_att_sum + t_att
                dk_rows.append(dk_sum)
                k_att_rows.append(k_att_sum)
            dk_a = jnp.concatenate(dk_rows, axis=0)
            db = (dq_til * q_til - dk_hat * k_hat + (qh * eq).astype(BF16).astype(F32) * dq_sel
                  - jnp.concatenate(k_att_rows, axis=0))
            db_parts.append(jnp.where(is_last, db + db_last, db))
            dq_parts.append(dq_til * eb + dq_a)
            dk_parts.append(dk_hat * ekh + dk_a)
            dp_ref[:, 2 * D_HG + h * HG_HEAD_DIM:2 * D_HG + (h + 1) * HG_HEAD_DIM] = dv.astype(BF16)

        gg_ref[...] += gg
        db = jnp.concatenate(db_parts, axis=1)
        dq = jnp.concatenate(dq_parts, axis=1)
        dk = jnp.concatenate(dk_parts, axis=1)
        dlf = jnp.where(valid, _running_sum(db, False), 0.0)
        dp_ref[:, :D_HG] = (dq * sq * (1.0 + hq * (1.0 - sq))).astype(BF16)
        df = dlf / f - dk
        dlb = jnp.sum(df * (1.0 - sf), axis=0, keepdims=True) * lb * (1.0 - lb)
        glb_ref[0:1, :] += dlb
        glb_ref[1:2, :] += -dlb
        dp_ref[:, D_HG:2 * D_HG] = (df * (1.0 - lb) * sf * (1.0 - sf)).astype(BF16)

    rev = lambda j: pl.BlockSpec((rows, D_HG), lambda i: (n_steps - 1 - i, j))
    return pl.pallas_call(
        body, grid=(n_steps,),
        in_specs=[rev(2), rev(3), rev(4), rev(5), rev(0),
                  pl.BlockSpec((cps, D_HG, HG_HEAD_DIM), lambda i: (n_steps - 1 - i, 0, 0)), rev(1),
                  pl.BlockSpec((2, D_HG), lambda i: (0, 0)), pl.BlockSpec((1, HG_HEAD_DIM), lambda i: (0, 0))],
        out_specs=[pl.BlockSpec((rows, D_IN), lambda i: (n_steps - 1 - i, 0)),
                   pl.BlockSpec((2, D_HG), lambda i: (0, 0)), pl.BlockSpec((1, HG_HEAD_DIM), lambda i: (0, 0))],
        out_shape=[jax.ShapeDtypeStruct((T, D_IN), BF16), jax.ShapeDtypeStruct((2, D_HG), F32),
                   jax.ShapeDtypeStruct((1, HG_HEAD_DIM), F32)],
        scratch_shapes=[pltpu.VMEM((D_HG, HG_HEAD_DIM), F32)],
        name="hg_bwd", compiler_params=_params("arbitrary"),
    )(p, p, p, p, o_all, st_all, dy, lbraw, hg_g)


def _in_bwd(dp, w_in, h0, g1, dh1):
    T = h0.shape[0]
    tm = _row_tile(T, 832)
    n_steps = T // tm

    def body(dp_ref, w_ref, h_ref, g_ref, d1_ref, gx_hbm, gmeta_ref, gg_ref, buf, sems):
        i = pl.program_id(0)
        first, later = _window_copies(gx_hbm, buf, sems, tm)
        slot = i % 2

        @pl.when(i == 0)
        def _():
            gg_ref[...] = jnp.zeros_like(gg_ref)

        if n_steps > 2:
            @pl.when(i == 2)
            def _():
                first(False).wait()

            @pl.when(i > 2)
            def _():
                later(i - 2, slot, False).wait()

        du = _dot_nt(dp_ref[...], w_ref[...])
        h0_ = h_ref[...]
        r = _rms(h0_)
        n = h0_ * r
        gg_ref[...] += jnp.sum(du * n, axis=0, keepdims=True)
        dh0 = d1_ref[...] + _rms_bwd(du * g_ref[...], n, r)
        buf[slot] = dh0

        @pl.when(i == 0)
        def _():
            gmeta_ref[...] = dh0[PAD:HEAD, :]
            first(False).start()

        if n_steps > 1:
            @pl.when(i > 0)
            def _():
                later(i, slot, False).start()

        @pl.when(i == n_steps - 1)
        def _():
            if n_steps == 1:
                first(False).wait()
            else:
                if n_steps == 2:
                    first(False).wait()
                else:
                    later(i - 1, 1 - slot, False).wait()
                later(i, slot, False).wait()

    row = lambda n: pl.BlockSpec((tm, n), lambda i: (i, 0))
    return pl.pallas_call(
        body, grid=(n_steps,),
        in_specs=[row(D_IN), _resident((D_MODEL, D_IN)),
                  row(D_MODEL), pl.BlockSpec((1, D_MODEL), lambda i: (0, 0)), row(D_MODEL)],
        out_specs=[pl.BlockSpec(memory_space=pl.ANY), pl.BlockSpec((N_META, D_MODEL), lambda i: (0, 0)),
                   pl.BlockSpec((1, D_MODEL), lambda i: (0, 0))],
        out_shape=[jax.ShapeDtypeStruct((T - HEAD, D_MODEL), F32), jax.ShapeDtypeStruct((N_META, D_MODEL), F32),
                   jax.ShapeDtypeStruct((1, D_MODEL), F32)],
        scratch_shapes=[pltpu.VMEM((2, tm, D_MODEL), F32), pltpu.SemaphoreType.DMA((2,))],
        name="in_bwd", compiler_params=_params("arbitrary"),
    )(dp, w_in, h0, g1, dh1)


def _col_tile(cols, target):
    best = None
    for t in range(128, min(cols, target) + 1, 128):
        if cols % t == 0:
            best = t
    assert best is not None, cols
    return best


MXU_DIM = 256


def _mxu_tile(cols, target):
    best = None
    for t in range(MXU_DIM, min(cols, target) + 1, MXU_DIM):
        if cols % t == 0:
            best = t
    assert best is not None, cols
    return best


def _weight_grad(a, b, name):
    T, M = a.shape
    N = b.shape[1]
    tm = _col_tile(M, 1408)
    tn = _mxu_tile(N, 768 if tm <= 1024 else 512)

    def body(a_ref, b_ref, o_ref):
        o_ref[...] = _dot_tn(a_ref[...], b_ref[...])

    return pl.pallas_call(
        body, grid=(M // tm, N // tn),
        in_specs=[pl.BlockSpec((T, tm), lambda m, n: (0, m)), pl.BlockSpec((T, tn), lambda m, n: (0, n))],
        out_specs=pl.BlockSpec((tm, tn), lambda m, n: (m, n)),
        out_shape=jax.ShapeDtypeStruct((M, N), F32),
        name=name, compiler_params=_params("parallel", "parallel"),
    )(a, b)


def _local_step(x, meta, target, w_in_own, w_in, w_out, w_gu, w_down, small, chip, on_ffn_grads=None,
                on_mixer_grads=None):
    wg = _gate_weights(small["w_rgate"], small["w_igate"])
    bg = jnp.concatenate([small["b_rgate"], small["b_igate"]], axis=1)

    p, u, h0 = _in_proj_local(x, meta, small["mix_norm_g"], w_in_own, chip)
    p = _in_proj_rest(u, w_in, p, chip)
    y_rg, hs, xc = _rg_fwd(p, small["conv_w"], small["conv_b"], wg, bg, small["lru_lambda"], small["rg_norm_g"])
    y_hg, o_all, st_all = _hg_fwd(p, small["hg_lower_bound"], small["hg_norm_g"])
    h1, v, yb, gu, act, dh2, dh2b, loss, g_final = _ffn_fwd(
        h0, y_rg, y_hg, w_out, small["ffn_norm_g"], w_gu, w_down, small["final_norm_g"], target)

    g_w_down = _weight_grad(act, dh2b, "grad_w_down")
    dgu, dh1, dh1b, dy, g_ffn = _ffn_bwd(dh2b, gu, w_down, w_gu, h1, small["ffn_norm_g"], dh2, w_out)
    ffn_grads = {"w_gate_up": _weight_grad(v, dgu, "grad_w_gate_up"), "w_down": g_w_down,
                 "w_out": _weight_grad(yb, dh1b, "grad_w_out")}
    stages = on_ffn_grads(ffn_grads) if on_ffn_grads is not None else None
    dp, g_lb, g_hgn = _hg_bwd(p, o_all, st_all, dy, small["hg_lower_bound"], small["hg_norm_g"])
    early = late = None
    if stages is not None:
        chip_sums, send = stages
        sums = chip_sums()
        (dp, dy), sums = lax.optimization_barrier(((dp, dy), sums))
        early = send(sums)
    dp, g_cw, g_cb, g_wgate, g_bg, g_lam, g_rgn = _rg_bwd(
        p, xc, hs, dy, dp, small["conv_w"], small["conv_b"], wg, bg, small["lru_lambda"], small["rg_norm_g"])
    mixer_grads = {"w_in": _weight_grad(u, dp, "grad_w_in")}
    if on_mixer_grads is not None:
        chip_sums, send = on_mixer_grads(mixer_grads)
        sums = chip_sums()
        (dp, dh1), sums = lax.optimization_barrier(((dp, dh1), sums))
        late = send(sums)
    grad_x, g_meta, g_mix = _in_bwd(dp, w_in, h0, small["mix_norm_g"], dh1)

    grads = {
        "w_in": mixer_grads["w_in"], "w_out": ffn_grads["w_out"],
        "w_gate_up": ffn_grads["w_gate_up"], "w_down": ffn_grads["w_down"],
        "meta_tokens": g_meta, "mix_norm_g": g_mix, "conv_w": g_cw, "conv_b": g_cb, "w_gates": g_wgate,
        "b_rgate": g_bg[:, :D_RG], "b_igate": g_bg[:, D_RG:], "lru_lambda": g_lam, "rg_norm_g": g_rgn,
        "hg_lower_bound": g_lb, "hg_norm_g": g_hgn, "ffn_norm_g": g_ffn, "final_norm_g": g_final,
    }
    return loss, grad_x, grads, early, late


ANY = pl.BlockSpec(memory_space=pl.ANY)
HALF = D_MODEL // 2

BIG = {"w_in": (D_MODEL, D_IN // N_CHIPS, True), "w_gate_up": (D_MODEL, 2 * D_FF // N_CHIPS, True),
       "w_out": (D_MODEL // N_CHIPS, D_MODEL, False), "w_down": (D_FF // N_CHIPS, D_MODEL, False)}
BIG_NAMES = tuple(BIG)
N_BIG = len(BIG_NAMES)


def _full_shape(name):
    rows, cols, by_col = BIG[name]
    return (rows, cols * N_CHIPS) if by_col else (rows * N_CHIPS, cols)


def _place():
    return lax.axis_index("x"), lax.axis_index("y"), lax.axis_index("c")


def _chip_of(x, y, r):
    fx, fy = (r + 1) >> 1, (r + 1) & 1
    return (1 - x if fx else x), (1 - y if fy else y)


def _half_of(ref, by_col, half):
    start = pl.multiple_of(half * HALF, 128)
    return ref.at[pl.ds(start, HALF), :] if by_col else ref.at[:, pl.ds(start, HALF)]


def _shard_of(ref, name, chip):
    rows, cols, by_col = BIG[name]
    if by_col:
        return ref.at[:, pl.ds(pl.multiple_of(chip * cols, 128), cols)]
    return ref.at[pl.ds(pl.multiple_of(chip * rows, 16), rows), :]


def _shard_half_of(ref, name, chip, half):
    rows, cols, by_col = BIG[name]
    start = pl.multiple_of(half * HALF, 128)
    if by_col:
        return ref.at[pl.ds(start, HALF), pl.ds(pl.multiple_of(chip * cols, 128), cols)]
    return ref.at[pl.ds(pl.multiple_of(chip * rows, 16), rows), pl.ds(start, HALF)]


def _shard_half_part_of(ref, name, chip, half, part):
    rows, cols, by_col = BIG[name]
    start = pl.multiple_of(half * HALF + part * (HALF // 2), 128)
    if by_col:
        return ref.at[pl.ds(start, HALF // 2), pl.ds(pl.multiple_of(chip * cols, 128), cols)]
    return ref.at[pl.ds(pl.multiple_of(chip * rows, 16), rows), pl.ds(start, HALF // 2)]


def _remote(src, dst, send_sems, recv_sems, k, dev):
    return pltpu.make_async_remote_copy(src_ref=src, dst_ref=dst, send_sem=send_sems.at[k], recv_sem=recv_sems.at[k],
                                        device_id=dev, device_id_type=MESH)


def _place_shards(w, small, chip):
    steps = 4
    ns = len(small)
    in_specs, out_specs = [], []
    for name in BIG_NAMES:
        rows, cols, by_col = BIG[name]
        tr = rows // steps
        in_specs.append(pl.BlockSpec((tr, cols), lambda i, s: (i, 0)))
        if by_col:
            out_specs.append(pl.BlockSpec((tr, cols), lambda i, s: (i, s[0])))
        else:
            out_specs.append(pl.BlockSpec((tr, cols), lambda i, s: (s[0] * steps + i, 0)))

    def body(s_ref, *refs):
        ins, small_in = refs[:N_BIG], refs[N_BIG:N_BIG + ns]
        outs, small_out = refs[N_BIG + ns:2 * N_BIG + ns], refs[2 * N_BIG + ns:2 * (N_BIG + ns)]
        send_sems, recv_sems, local_sems = refs[2 * (N_BIG + ns):]
        i = pl.program_id(0)
        x, y, c = _place()
        chip_ = 2 * x + y
        others = [_chip_of(x, y, r) for r in range(3)]

        def block(a, q):
            cols = small[a].shape[1]
            return small_out[a].at[:, pl.ds(pl.multiple_of(q * cols, 128), cols)]

        def local(a):
            return pltpu.make_async_copy(small_in[a], block(a, chip_), local_sems.at[a])

        def remote(a, r):
            qx, qy = others[r]
            return _remote(small_in[a], block(a, chip_), send_sems, recv_sems, 3 * a + r, (qx, qy, c))

        @pl.when(i == 0)
        def _():
            for a in range(ns):
                local(a).start()
                for r in range(3):
                    remote(a, r).start()

        for a in range(N_BIG):
            outs[a][...] = ins[a][...].astype(BF16)

        @pl.when(i == steps - 1)
        def _():
            for a in range(ns):
                for r, (qx, qy) in enumerate(others):
                    landed = block(a, 2 * qx + qy)
                    _remote(landed, landed, send_sems, recv_sems, 3 * a + r, (qx, qy, c)).wait_recv()
                for r in range(3):
                    remote(a, r).wait_send()
                local(a).wait()

    out = pl.pallas_call(
        body,
        grid_spec=pltpu.PrefetchScalarGridSpec(
            num_scalar_prefetch=1, grid=(steps,), in_specs=in_specs + [ANY] * ns, out_specs=out_specs + [ANY] * ns,
            scratch_shapes=[pltpu.SemaphoreType.DMA((3 * ns,)), pltpu.SemaphoreType.DMA((3 * ns,)),
                            pltpu.SemaphoreType.DMA((ns,))]),
        out_shape=([jax.ShapeDtypeStruct(_full_shape(name), BF16) for name in BIG_NAMES]
                   + [jax.ShapeDtypeStruct((s.shape[0], s.shape[1] * N_CHIPS), F32) for s in small]),
        name="place_shards", compiler_params=_params("arbitrary"),
    )(chip, *[w[name] for name in BIG_NAMES], *small)
    return dict(zip(BIG_NAMES, out[:N_BIG])), list(out[N_BIG:])


def _gather_weights(placed, small, names, label, collective_id):
    n, ns = len(names), len(small)
    hbm = pltpu.MemorySpace.HBM
    outs = [jax.new_ref(placed[nm], memory_space=hbm) for nm in names]
    small_in = [jax.new_ref(s, memory_space=hbm) for s in small]
    small_out = [jax.empty_ref(jax.ShapeDtypeStruct((s.shape[0], s.shape[1] * N_CHIPS), F32), memory_space=hbm)
                 for s in small]
    n_sems = 8 * n + 3 * ns

    @pl.kernel(mesh=plsc.ScalarSubcoreMesh(axis_name="seq", num_cores=1), name=label, out_type=(),
               scratch_types=(pltpu.SemaphoreType.DMA((n_sems,)), pltpu.SemaphoreType.DMA((n_sems,)),
                              pltpu.SemaphoreType.DMA((max(ns, 1),))),
               compiler_params=pltpu.CompilerParams(collective_id=collective_id))
    def launch(send_sems, recv_sems, local_sems):
        x, y, c = _place()
        chip = 2 * x + y
        sibling = (x, y, 1 - c)
        others = [_chip_of(x, y, r) for r in range(3)]
        near = others[:2]
        far = 2 * others[2][0] + others[2][1]
        _handshake([(qx, qy, c) for qx, qy in others] + [sibling])

        def small_block(a, q):
            cols = small[a].shape[1]
            return small_out[a].at[:, pl.ds(pl.multiple_of(q * cols, 128), cols)]

        local = [pltpu.make_async_copy(small_in[a], small_block(a, chip), local_sems.at[a]) for a in range(ns)]
        for cp in local:
            cp.start()

        sends = []
        for a, name in enumerate(names):
            mine = _shard_half_of(outs[a], name, chip, c)
            for r, (qx, qy) in enumerate(near):
                sends.append(_remote(mine, mine, send_sems, recv_sems, 8 * a + r, (qx, qy, c)))
        for a in range(ns):
            for r, (qx, qy) in enumerate(others):
                sends.append(_remote(small_in[a], small_block(a, chip), send_sems, recv_sems,
                                     8 * n + 3 * a + r, (qx, qy, c)))
        for cp in sends:
            cp.start()

        forwards = []

        def forward(piece, k, dev):
            cp = _remote(piece, piece, send_sems, recv_sems, k, dev)
            cp.start()
            forwards.append(cp)

        for a, name in enumerate(names):
            for r, (qx, qy) in enumerate(near):
                landed = _shard_half_of(outs[a], name, 2 * qx + qy, c)
                _remote(landed, landed, send_sems, recv_sems, 8 * a + r, (qx, qy, c)).wait_recv()
                ox, oy = near[1 - r]
                forward(_shard_half_part_of(outs[a], name, 2 * qx + qy, c, r), 8 * a + 2 + r, (ox, oy, c))
                forward(landed, 8 * a + 4 + r, sibling)
        for a, name in enumerate(names):
            for part in range(2):
                qx, qy = near[1 - part]
                landed = _shard_half_part_of(outs[a], name, far, c, part)
                _remote(landed, landed, send_sems, recv_sems, 8 * a + 2 + part, (qx, qy, c)).wait_recv()
                forward(landed, 8 * a + 6 + part, sibling)
        for a in range(ns):
            for r, (qx, qy) in enumerate(others):
                landed = small_block(a, 2 * qx + qy)
                _remote(landed, landed, send_sems, recv_sems, 8 * n + 3 * a + r, (qx, qy, c)).wait_recv()
        for a, name in enumerate(names):
            for r, (qx, qy) in enumerate(near):
                landed = _shard_half_of(outs[a], name, 2 * qx + qy, 1 - c)
                _remote(landed, landed, send_sems, recv_sems, 8 * a + 4 + r, sibling).wait_recv()
            for part in range(2):
                landed = _shard_half_part_of(outs[a], name, far, 1 - c, part)
                _remote(landed, landed, send_sems, recv_sems, 8 * a + 6 + part, sibling).wait_recv()
        for cp in sends + forwards:
            cp.wait_send()
        for cp in local:
            cp.wait()

    launch()
    return {nm: ref[...] for nm, ref in zip(names, outs)}, [ref[...] for ref in small_out]


def _exchange_halves(grads, names, label, collective_id):
    n = len(names)
    sequencer = collective_id is not None

    def body(*refs):
        ins, outs = refs[:n], refs[n:2 * n]
        send_sems, recv_sems = refs[2 * n:]
        x, y, c = _place()
        if sequencer:
            _handshake([(x, y, 1 - c)])
        copies = []
        for a, name in enumerate(names):
            copies.append(_remote(_half_of(ins[a], BIG[name][2], 1 - c), outs[a], send_sems, recv_sems, a,
                                  (x, y, 1 - c)))
        for cp in copies:
            cp.start()
        for cp in copies:
            cp.wait()

    def half_shape(name):
        r, c_ = _full_shape(name)
        return (HALF, c_) if BIG[name][2] else (r, HALF)

    out_type = tuple(jax.ShapeDtypeStruct(half_shape(nm), F32) for nm in names)
    sems = (pltpu.SemaphoreType.DMA((n,)), pltpu.SemaphoreType.DMA((n,)))
    operands = [grads[nm] for nm in names]
    if sequencer:
        got = pl.kernel(
            body, mesh=plsc.ScalarSubcoreMesh(axis_name="seq", num_cores=1), name=label, out_type=out_type,
            scratch_types=sems, compiler_params=pltpu.CompilerParams(collective_id=collective_id),
        )(*operands)
    else:
        got = pl.pallas_call(
            body, in_specs=[ANY] * n, out_specs=[ANY] * n, out_shape=list(out_type), scratch_shapes=list(sems),
            name=label,
        )(*operands)
    return dict(zip(names, got))


def _chip_sum(grads, got, names, core, label):
    n = len(names)
    steps = 4
    g_specs, blks = [], []
    for name in names:
        rows, cols = got[name].shape
        tr = rows // steps
        if BIG[name][2]:
            g_specs.append(pl.BlockSpec((tr, cols), lambda i, s: (s[0] * steps + i, 0)))
        else:
            g_specs.append(pl.BlockSpec((tr, HALF), lambda i, s: (i, s[0])))
        blks.append(pl.BlockSpec((tr, cols), lambda i, s: (i, 0)))

    def body(s_ref, *refs):
        for a in range(n):
            t = refs[a][...] + refs[n + a][...]
            refs[2 * n + a][...] = t
            refs[3 * n + a][...] = t.astype(BF16)

    out = pl.pallas_call(
        body,
        grid_spec=pltpu.PrefetchScalarGridSpec(num_scalar_prefetch=1, grid=(steps,), in_specs=g_specs + blks,
                                               out_specs=blks + blks),
        out_shape=([jax.ShapeDtypeStruct(got[nm].shape, F32) for nm in names]
                   + [jax.ShapeDtypeStruct(got[nm].shape, BF16) for nm in names]),
        name=label, compiler_params=_params("parallel"),
    )(core, *[grads[nm] for nm in names], *[got[nm] for nm in names])
    return {nm: (out[a], out[n + a]) for a, nm in enumerate(names)}


def _piece_shape(name):
    rows, cols, by_col = BIG[name]
    return (HALF, cols) if by_col else (rows, HALF)


def _handshake(peers):
    barrier = pltpu.get_barrier_semaphore()
    for peer in peers:
        pl.semaphore_signal(barrier, inc=1, device_id=peer, device_id_type=MESH)
    pl.semaphore_wait(barrier, len(peers))


def _send_chip_sums(sums, names, label, collective_id):
    n = len(names)

    def body(*refs):
        ins, outs = refs[:n], refs[n:2 * n]
        send_sems, recv_sems = refs[2 * n:]
        x, y, c = _place()
        others = [_chip_of(x, y, r) for r in range(3)]
        _handshake([(qx, qy, c) for qx, qy in others])
        copies = []
        for a, name in enumerate(names):
            for r, (qx, qy) in enumerate(others):
                copies.append(_remote(_shard_of(ins[a], name, 2 * qx + qy), outs[a].at[r], send_sems, recv_sems,
                                      3 * a + r, (qx, qy, c)))
        for cp in copies:
            cp.start()
        for cp in copies:
            cp.wait()

    return pl.kernel(
        body, mesh=plsc.ScalarSubcoreMesh(axis_name="seq", num_cores=1), name=label,
        out_type=tuple(jax.ShapeDtypeStruct((3,) + _piece_shape(nm), BF16) for nm in names),
        scratch_types=(pltpu.SemaphoreType.DMA((3 * n,)), pltpu.SemaphoreType.DMA((3 * n,))),
        compiler_params=pltpu.CompilerParams(collective_id=collective_id),
    )(*[sums[nm] for nm in names])


def _total(parts, chip_core):
    steps = 2
    in_specs, out_specs, operands = [], [], []
    for name in BIG_NAMES:
        by_col = BIG[name][2]
        pr, pc = _piece_shape(name)
        tr = pr // steps
        if by_col:
            in_specs.append(pl.BlockSpec((tr, pc), lambda i, s: (i, s[0])))
            out_specs.append(pl.BlockSpec((tr, pc), lambda i, s: (s[1] * steps + i, 0)))
        else:
            in_specs.append(pl.BlockSpec((tr, pc), lambda i, s: (s[0] * steps + i, 0)))
            out_specs.append(pl.BlockSpec((tr, pc), lambda i, s: (i, s[1])))
        for r in range(3):
            in_specs.append(pl.BlockSpec((None, tr, pc), lambda i, s, r=r: (r, i, 0)))
        own, got = parts[name]
        operands += [own, got, got, got]

    def body(s_ref, *refs):
        for a in range(N_BIG):
            o_ref, a_ref, b_ref, c_ref = refs[4 * a:4 * a + 4]
            refs[4 * N_BIG + a][...] = (((o_ref[...] + a_ref[...].astype(F32)) + b_ref[...].astype(F32))
                                        + c_ref[...].astype(F32))

    totals = pl.pallas_call(
        body,
        grid_spec=pltpu.PrefetchScalarGridSpec(num_scalar_prefetch=1, grid=(steps,), in_specs=in_specs,
                                               out_specs=out_specs),
        out_shape=[jax.ShapeDtypeStruct(BIG[name][:2], F32) for name in BIG_NAMES],
        name="totals", compiler_params=_params("parallel"),
    )(chip_core, *operands)
    return dict(zip(BIG_NAMES, totals))


VEC_ROWS = 32
VEC_ROW = {"mix_norm_g": 0, "conv_b": 1, "b_rgate": 2, "b_igate": 3, "lru_lambda": 4, "rg_norm_g": 5,
           "hg_lower_bound": 6, "hg_norm_g": 8, "ffn_norm_g": 9, "final_norm_g": 10, "loss": 11,
           "conv_w": 12, "meta_tokens": 16}
N_DEV = 8


def _all_reduce_small(pieces, gates, totals):
    names = list(pieces)
    n_small = 10
    hv, hg = VEC_ROWS // 2, gates.shape[0] // 2

    def body(*refs):
        ins = refs[:len(names)]
        g_ref = refs[len(names)]
        vec_ref, gsum_ref = refs[len(names) + 1 + N_BIG:len(names) + 3 + N_BIG]
        big = refs[len(names) + 3 + N_BIG:len(names) + 3 + 2 * N_BIG]
        (mine_v, sib_v, sib_g, chip_v, chip_g, got_v, got_g, send_sems, recv_sems) = refs[len(names) + 3 + 2 * N_BIG:]
        x, y, c = _place()
        chip = 2 * x + y
        sibling = (x, y, 1 - c)
        share = []
        for a, name in enumerate(BIG_NAMES):
            half = _half_of(big[a], BIG[name][2], c)
            share.append(_remote(half, half, send_sems, recv_sems, n_small + a, sibling))
        mine_v[...] = jnp.zeros_like(mine_v)
        for name, ref in zip(names, ins):
            nr, w = ref.shape
            mine_v[VEC_ROW[name]:VEC_ROW[name] + nr, 0:w] = ref[...]

        swap = [_remote(mine_v, sib_v, send_sems, recv_sems, 0, sibling),
                _remote(g_ref, sib_g, send_sems, recv_sems, 1, sibling)]
        for cp in swap:
            cp.start()
        for cp in swap:
            cp.wait()
        for cp in share:
            cp.start()
        chip_v[...] = mine_v[...] + sib_v[...]
        chip_g[...] = g_ref[...] + sib_g[...]

        rows_v = pl.ds(pl.multiple_of(c * hv, 8), hv)
        rows_g = pl.ds(pl.multiple_of(c * hg, 8), hg)
        got_v[chip] = chip_v[rows_v, :]
        got_g[chip] = chip_g[rows_g, :]
        sends = []
        for r in range(3):
            qx, qy = _chip_of(x, y, r)
            sends.append(_remote(chip_v.at[rows_v, :], got_v.at[chip], send_sems, recv_sems, 2 + r, (qx, qy, c)))
            sends.append(_remote(chip_g.at[rows_g, :], got_g.at[chip], send_sems, recv_sems, 5 + r, (qx, qy, c)))
        for cp in sends:
            cp.start()
        for cp in sends:
            cp.wait()
        vec_ref[rows_v, :] = ((got_v[0] + got_v[1]) + got_v[2]) + got_v[3]
        gsum_ref[rows_g, :] = ((got_g[0] + got_g[1]) + got_g[2]) + got_g[3]

        back = [_remote(vec_ref.at[rows_v, :], vec_ref.at[rows_v, :], send_sems, recv_sems, 8, sibling),
                _remote(gsum_ref.at[rows_g, :], gsum_ref.at[rows_g, :], send_sems, recv_sems, 9, sibling)]
        for cp in back:
            cp.start()
        theirs_v = vec_ref.at[pl.ds(pl.multiple_of((1 - c) * hv, 8), hv), :]
        theirs_g = gsum_ref.at[pl.ds(pl.multiple_of((1 - c) * hg, 8), hg), :]
        _remote(theirs_v, theirs_v, send_sems, recv_sems, 8, sibling).wait_recv()
        _remote(theirs_g, theirs_g, send_sems, recv_sems, 9, sibling).wait_recv()
        for cp in back:
            cp.wait_send()
        for a, name in enumerate(BIG_NAMES):
            theirs = _half_of(big[a], BIG[name][2], 1 - c)
            _remote(theirs, theirs, send_sems, recv_sems, n_small + a, sibling).wait_recv()
        for cp in share:
            cp.wait_send()

    vmem = pl.BlockSpec(memory_space=pltpu.VMEM)
    n_sems = n_small + N_BIG
    out = pl.pallas_call(
        body, in_specs=[vmem] * (len(names) + 1) + [ANY] * N_BIG, out_specs=[vmem, vmem] + [ANY] * N_BIG,
        out_shape=([jax.ShapeDtypeStruct((VEC_ROWS, D_MODEL), F32), jax.ShapeDtypeStruct(gates.shape, F32)]
                   + [jax.ShapeDtypeStruct(BIG[n][:2], F32) for n in BIG_NAMES]),
        input_output_aliases={len(names) + 1 + a: 2 + a for a in range(N_BIG)},
        scratch_shapes=[pltpu.VMEM((VEC_ROWS, D_MODEL), F32), pltpu.VMEM((VEC_ROWS, D_MODEL), F32),
                        pltpu.VMEM(gates.shape, F32), pltpu.VMEM((VEC_ROWS, D_MODEL), F32),
                        pltpu.VMEM(gates.shape, F32), pltpu.VMEM((N_CHIPS, hv, D_MODEL), F32),
                        pltpu.VMEM((N_CHIPS, hg) + gates.shape[1:], F32),
                        pltpu.SemaphoreType.DMA((n_sems,)), pltpu.SemaphoreType.DMA((n_sems,))],
        name="all_reduce_small",
    )(*[pieces[n] for n in names], gates, *[totals[n] for n in BIG_NAMES])
    return out[0], out[1], dict(zip(BIG_NAMES, out[2:]))


def _adamw_math(w, g, m, v):
    m = ADAM_B1 * m + (1.0 - ADAM_B1) * g
    v = ADAM_B2 * v + (1.0 - ADAM_B2) * (g * g)
    m_hat = m / (1.0 - ADAM_B1 ** ADAM_STEP)
    v_hat = v / (1.0 - ADAM_B2 ** ADAM_STEP)
    delta = -ADAM_LR * (m_hat / (jnp.sqrt(v_hat) + ADAM_EPS) + ADAM_WD * w)
    return delta, m, v


def _adamw_big(w, g, m, v):
    steps = 8
    blks = []
    for name in BIG_NAMES:
        rows, cols, _ = BIG[name]
        blks.append(pl.BlockSpec((rows // steps, cols), lambda i: (i, 0)))

    def body(*refs):
        ins, outs = refs[:4 * N_BIG], refs[4 * N_BIG:]
        for a in range(N_BIG):
            w_ref, g_ref, m_ref, v_ref = (ins[k * N_BIG + a] for k in range(4))
            g = g_ref[...]
            d, nm, nv = _adamw_math(w_ref[...], g, m_ref[...], v_ref[...])
            outs[a][...] = g
            outs[N_BIG + a][...] = d
            outs[2 * N_BIG + a][...] = nm
            outs[3 * N_BIG + a][...] = nv

    shapes = [jax.ShapeDtypeStruct(BIG[name][:2], F32) for name in BIG_NAMES]
    out = pl.pallas_call(
        body, grid=(steps,), in_specs=blks * 4, out_specs=blks * 4, out_shape=shapes * 4,
        name="adamw_big", compiler_params=_params("parallel"),
    )(*[t[name] for t in (w, g, m, v) for name in BIG_NAMES])
    return {name: tuple(out[k * N_BIG + a] for k in range(4)) for a, name in enumerate(BIG_NAMES)}


SMALL = {"meta_tokens": (N_META, D_MODEL // N_CHIPS), "mix_norm_g": (1, D_MODEL), "conv_w": (CONV_W, D_RG // N_CHIPS),
         "conv_b": (1, D_RG), "w_rgate": (D_RG, RG_HEAD_DIM), "b_rgate": (1, D_RG), "w_igate": (D_RG, RG_HEAD_DIM),
         "b_igate": (1, D_RG), "lru_lambda": (1, D_RG), "rg_norm_g": (1, D_RG), "hg_lower_bound": (2, D_HG),
         "hg_norm_g": (1, HG_HEAD_DIM), "ffn_norm_g": (1, D_MODEL), "final_norm_g": (1, D_MODEL)}
SMALL_NAMES = tuple(SMALL)
SHARDED_SMALL = ("meta_tokens", "conv_w")


def _adamw_small(vec, gates, w, m, v):
    n = len(SMALL_NAMES)

    def body(*refs):
        vec_ref, gates_ref = refs[:2]
        w_refs, m_refs, v_refs = refs[2:2 + n], refs[2 + n:2 + 2 * n], refs[2 + 2 * n:2 + 3 * n]
        outs = refs[2 + 3 * n:]
        loss_ref = outs[0]
        x, y, _ = _place()
        chip = 2 * x + y
        loss_ref[...] = vec_ref[VEC_ROW["loss"]:VEC_ROW["loss"] + 1, 0:1]

        def update(k, g):
            g_ref, d_ref, nm_ref, nv_ref = outs[1 + 4 * k:5 + 4 * k]
            g_ref[...] = g
            d_ref[...], nm_ref[...], nv_ref[...] = _adamw_math(w_refs[k][...], g, m_refs[k][...], v_refs[k][...])

        for k, name in enumerate(SMALL_NAMES):
            nr, w_ = SMALL[name]
            if name == "w_rgate":
                update(k, gates_ref[0:D_RG, :])
            elif name == "w_igate":
                update(k, gates_ref[D_RG:2 * D_RG, :])
            elif name in SHARDED_SMALL:
                r0 = VEC_ROW[name]
                for q in range(N_CHIPS):
                    @pl.when(chip == q)
                    def _(k=k, r0=r0, nr=nr, w_=w_, q=q):
                        update(k, vec_ref[r0:r0 + nr, q * w_:(q + 1) * w_])
            else:
                r0 = VEC_ROW[name]
                update(k, vec_ref[r0:r0 + nr, 0:w_])

    vmem = pl.BlockSpec(memory_space=pltpu.VMEM)
    out_shape = [jax.ShapeDtypeStruct((1, 1), F32)]
    for name in SMALL_NAMES:
        out_shape += [jax.ShapeDtypeStruct(SMALL[name], F32)] * 4
    outs = pl.pallas_call(
        body, in_specs=[vmem] * (2 + 3 * n), out_specs=[vmem] * len(out_shape), out_shape=out_shape,
        name="adamw_small",
    )(vec, gates, *[w[k] for k in SMALL_NAMES], *[m[k] for k in SMALL_NAMES], *[v[k] for k in SMALL_NAMES])
    loss = outs[0]
    res = {name: tuple(outs[1 + 4 * k:5 + 4 * k]) for k, name in enumerate(SMALL_NAMES)}
    return loss, res


WEIGHT_NAMES = ("meta_tokens", "mix_norm_g", "w_in", "conv_w", "conv_b", "w_rgate", "b_rgate", "w_igate", "b_igate",
                "lru_lambda", "rg_norm_g", "hg_lower_bound", "hg_norm_g", "w_out", "ffn_norm_g", "w_gate_up", "w_down",
                "final_norm_g")


def _to_2d(name, a):
    if name in BIG:
        return a.reshape(BIG[name][:2])
    return a.reshape(SMALL[name])


def kernel(x, meta_tokens, mix_norm_g, w_in, conv_w, conv_b, w_rgate, b_rgate, w_igate, b_igate, lru_lambda, rg_norm_g, hg_lower_bound, hg_norm_g, w_out, ffn_norm_g, w_gate_up, w_down, final_norm_g, loss_target, m_meta_tokens, m_mix_norm_g, m_w_in, m_conv_w, m_conv_b, m_w_rgate, m_b_rgate, m_w_igate, m_b_igate, m_lru_lambda, m_rg_norm_g, m_hg_lower_bound, m_hg_norm_g, m_w_out, m_ffn_norm_g, m_w_gate_up, m_w_down, m_final_norm_g, v_meta_tokens, v_mix_norm_g, v_w_in, v_conv_w, v_conv_b, v_w_rgate, v_b_rgate, v_w_igate, v_b_igate, v_lru_lambda, v_rg_norm_g, v_hg_lower_bound, v_hg_norm_g, v_w_out, v_ffn_norm_g, v_w_gate_up, v_w_down, v_final_norm_g):
    w_raw = dict(zip(WEIGHT_NAMES, (meta_tokens, mix_norm_g, w_in, conv_w, conv_b, w_rgate, b_rgate, w_igate, b_igate,
                                    lru_lambda, rg_norm_g, hg_lower_bound, hg_norm_g, w_out, ffn_norm_g, w_gate_up,
                                    w_down, final_norm_g)))
    m_raw = dict(zip(WEIGHT_NAMES, (m_meta_tokens, m_mix_norm_g, m_w_in, m_conv_w, m_conv_b, m_w_rgate, m_b_rgate,
                                    m_w_igate, m_b_igate, m_lru_lambda, m_rg_norm_g, m_hg_lower_bound, m_hg_norm_g,
                                    m_w_out, m_ffn_norm_g, m_w_gate_up, m_w_down, m_final_norm_g)))
    v_raw = dict(zip(WEIGHT_NAMES, (v_meta_tokens, v_mix_norm_g, v_w_in, v_conv_w, v_conv_b, v_w_rgate, v_b_rgate,
                                    v_w_igate, v_b_igate, v_lru_lambda, v_rg_norm_g, v_hg_lower_bound, v_hg_norm_g,
                                    v_w_out, v_ffn_norm_g, v_w_gate_up, v_w_down, v_final_norm_g)))
    w = {k: _to_2d(k, a) for k, a in w_raw.items()}
    m = {k: _to_2d(k, a) for k, a in m_raw.items()}
    v = {k: _to_2d(k, a) for k, a in v_raw.items()}

    x_i, y_i, c_i = _place()
    core = jnp.reshape(c_i, (1,)).astype(jnp.int32)
    chip = jnp.reshape(2 * x_i + y_i, (1,)).astype(jnp.int32)
    chip_core = jnp.concatenate([chip, core])

    placed, (meta_full, cw_full) = _place_shards(w, [w["meta_tokens"], w["conv_w"]], chip)
    first, _ = _gather_weights(placed, [], ("w_in",), "gather_first", 1)
    rest, _ = _gather_weights(placed, [], ("w_out", "w_gate_up", "w_down"), "gather_rest", 2)
    full = {**first, **rest}

    seq = x.shape[1]
    small ={k: w[k] for k in SMALL_NAMES if k not in SHARDED_SMALL}
    small["conv_w"] = cw_full

    def reduce_to_chips(grads, names, tag, collective_ids):
        got = _exchange_halves(grads, names, "exchange_halves_" + tag, collective_ids[0])

        def chip_sums():
            return _chip_sum(grads, got, names, core, "chip_sum_" + tag)

        def send(sums):
            arrived = _send_chip_sums({n: sums[n][1] for n in names}, names, "send_chip_sums_" + tag,
                                      collective_ids[1])
            return {n: (sums[n][0], a) for n, a in zip(names, arrived)}

        return chip_sums, send

    ffn_names, mixer_names = ("w_gate_up", "w_down", "w_out"), ("w_in",)
    loss, grad_x, grads, parts, parts_mixer = _local_step(
        x.reshape(seq, D_MODEL), meta_full, loss_target.reshape(seq, D_MODEL),
        w["w_in"], full["w_in"], full["w_out"], full["w_gate_up"], full["w_down"], small, chip,
        on_ffn_grads=lambda g: reduce_to_chips(g, ffn_names, "ffn", (3, 4)),
        on_mixer_grads=lambda g: reduce_to_chips(g, mixer_names, "mixer", (None, 5)))
    parts.update(parts_mixer)
    totals = _total(parts, chip_core)
    pieces = {k: grads[k] for k in VEC_ROW if k != "loss"}
    pieces["loss"] = loss
    vec, gates, g_big = _all_reduce_small(pieces, grads["w_gates"], totals)
    loss_sum, res = _adamw_small(vec, gates, w, m, v)
    res.update(_adamw_big(w, g_big, m, v))

    out = [loss_sum.reshape(()), grad_x.reshape(1, seq, D_MODEL)]
    for j in range(4):
        out += [res[n][j].reshape(w_raw[n].shape) for n in WEIGHT_NAMES]
    return tuple(out)
```

```python
import math

import jax
import jax.numpy as jnp
from jax import lax
from jax.experimental import pallas as pl
from jax.experimental.pallas import tpu as pltpu
from jax.experimental.pallas import tpu_sc as plsc

F32 = jnp.float32
BF16 = jnp.bfloat16
MESH = pl.DeviceIdType.MESH

D_MODEL = 1024
D_RG = 512
RG_HEAD_DIM = 64
D_HG = 512
HG_HEAD_DIM = 128
HG_HEADS = 4
CHUNK = 64
SUB = 16
N_SUB = CHUNK // SUB
N_META = 16
PAD = CHUNK - N_META
D_IN = 3072
D_FF = 2816
CONV_W = 4
LRU_C = 8.0
EPS = 1e-6
EXP_CLAMP = 80.0
GELU_C = math.sqrt(2.0 / math.pi)
GELU_A = 0.044715
N_CHIPS = 4

ADAM_LR = 0.001
ADAM_B1 = 0.9
ADAM_B2 = 0.999
ADAM_EPS = 1e-08
ADAM_WD = 0.01
ADAM_STEP = 10

VMEM_LIMIT = 56 * 1024 * 1024


def _params(*sem):
    return pltpu.CompilerParams(dimension_semantics=sem, vmem_limit_bytes=VMEM_LIMIT)


def _row_tile(rows, target):
    best = None
    for t in range(16, min(rows, target) + 1, 16):
        if rows % t == 0:
            best = t
    assert best is not None, rows
    return best


def _sigmoid(x):
    return 0.5 * jnp.tanh(0.5 * x) + 0.5


def _dot(a, b):
    return jnp.dot(a, b, preferred_element_type=F32)


def _dot_nt(a, b):
    return lax.dot_general(a, b, (((1,), (1,)), ((), ())), preferred_element_type=F32)


def _dot_tn(a, b):
    return lax.dot_general(a, b, (((0,), (0,)), ((), ())), preferred_element_type=F32)


def _rms(x):
    return lax.rsqrt(jnp.mean(x * x, axis=-1, keepdims=True) + EPS)


def _rms_bwd(dn, n, r):
    return r * (dn - n * jnp.mean(dn * n, axis=-1, keepdims=True))


def _gelu_parts(x):
    t = jnp.tanh(GELU_C * (x + GELU_A * x * x * x))
    g = 0.5 * x * (1.0 + t)
    dg = 0.5 * (1.0 + t) + 0.5 * x * (1.0 - t * t) * GELU_C * (1.0 + 3.0 * GELU_A * x * x)
    return g, dg


def _softplus_neg(lam):
    e = jnp.exp(-jnp.abs(lam))
    w = 1.0 + e
    log1p = jnp.where(w == 1.0, e, jnp.log(w) * e / (w - 1.0))
    return jnp.maximum(-lam, 0.0) + log1p


def _head_mask():
    r = lax.broadcasted_iota(jnp.int32, (D_RG, D_RG), 0) // RG_HEAD_DIM
    c = lax.broadcasted_iota(jnp.int32, (D_RG, D_RG), 1) // RG_HEAD_DIM
    return r == c


def _head_fold():
    r = lax.broadcasted_iota(jnp.int32, (D_RG, RG_HEAD_DIM), 0) % RG_HEAD_DIM
    c = lax.broadcasted_iota(jnp.int32, (D_RG, RG_HEAD_DIM), 1)
    return (r == c).astype(F32)


def _gate_weights(w_r, w_i):
    def body(wr_ref, wi_ref, o_ref):
        fold = _head_fold()
        mask = _head_mask()
        for k, ref in enumerate((wr_ref, wi_ref)):
            full = _dot_nt(ref[...].astype(BF16), fold.astype(BF16))
            o_ref[:, k * D_RG:(k + 1) * D_RG] = jnp.where(mask, full, 0.0).astype(BF16)

    return pl.pallas_call(
        body, out_shape=jax.ShapeDtypeStruct((D_RG, 2 * D_RG), BF16), name="gate_weights",
    )(w_r, w_i)


HEAD = PAD + N_META


def _window_copies(seq_hbm, buf, sems, tm):
    def first(to_vmem):
        seq, vm = seq_hbm.at[pl.ds(0, tm - HEAD)], buf.at[0, pl.ds(HEAD, tm - HEAD)]
        return pltpu.make_async_copy(seq, vm, sems.at[0]) if to_vmem else pltpu.make_async_copy(vm, seq, sems.at[0])

    def later(j, slot, to_vmem):
        seq, vm = seq_hbm.at[pl.ds(pl.multiple_of(j * tm - HEAD, 8), tm)], buf.at[slot]
        if to_vmem:
            return pltpu.make_async_copy(seq, vm, sems.at[slot])
        return pltpu.make_async_copy(vm, seq, sems.at[slot])

    return first, later


def _fetch_window(seq_hbm, buf, sems, i, n_steps, tm):
    first, later = _window_copies(seq_hbm, buf, sems, tm)
    slot = i % 2

    @pl.when(i == 0)
    def _():
        first(True).start()

    if n_steps > 1:
        @pl.when(i + 1 < n_steps)
        def _():
            later(i + 1, 1 - slot, True).start()

    @pl.when(i == 0)
    def _():
        first(True).wait()

    if n_steps > 1:
        @pl.when(i > 0)
        def _():
            later(i, slot, True).wait()

    return slot


def _in_proj_local(x, meta, g1, w_own, chip):
    T = x.shape[0] + HEAD
    tm = _row_tile(T, 832)
    n_steps = T // tm
    cols = BIG["w_in"][1]

    def body(s_ref, x_hbm, meta_ref, g_ref, w_ref, p_ref, u_ref, h_ref, buf, sems, wb):
        i = pl.program_id(0)
        slot = _fetch_window(x_hbm, buf, sems, i, n_steps, tm)

        @pl.when(i == 0)
        def _():
            buf[0, 0:PAD, :] = jnp.zeros((PAD, D_MODEL), F32)
            buf[0, PAD:HEAD, :] = meta_ref[...]
            wb[...] = w_ref[...].astype(BF16)

        h = buf[slot]
        h_ref[...] = h
        u = (h * _rms(h) * g_ref[...]).astype(BF16)
        u_ref[...] = u
        p_ref[...] = _dot(u, wb[...])

    return pl.pallas_call(
        body,
        grid_spec=pltpu.PrefetchScalarGridSpec(
            num_scalar_prefetch=1, grid=(n_steps,),
            in_specs=[pl.BlockSpec(memory_space=pl.ANY),
                      pl.BlockSpec((N_META, D_MODEL), lambda i, s: (0, 0)),
                      pl.BlockSpec((1, D_MODEL), lambda i, s: (0, 0)),
                      pl.BlockSpec((D_MODEL, cols), lambda i, s: (0, 0))],
            out_specs=[pl.BlockSpec((tm, cols), lambda i, s: (i, s[0])),
                       pl.BlockSpec((tm, D_MODEL), lambda i, s: (i, 0)),
                       pl.BlockSpec((tm, D_MODEL), lambda i, s: (i, 0))],
            scratch_shapes=[pltpu.VMEM((2, tm, D_MODEL), F32), pltpu.SemaphoreType.DMA((2,)),
                            pltpu.VMEM((D_MODEL, cols), BF16)]),
        out_shape=[jax.ShapeDtypeStruct((T, D_IN), F32), jax.ShapeDtypeStruct((T, D_MODEL), BF16),
                   jax.ShapeDtypeStruct((T, D_MODEL), F32)],
        name="in_proj_local", compiler_params=_params("arbitrary"),
    )(chip, x, meta, g1, w_own)


def _in_proj_rest(u, w_in, p, chip):
    T = u.shape[0]
    tm = _row_tile(T, 2080)
    cols = BIG["w_in"][1]
    block = lambda j, s: (s[0] + 1 + j) % N_CHIPS

    def body(s_ref, u_ref, w_ref, p_in_ref, p_ref):
        p_ref[...] = _dot(u_ref[...], w_ref[...])

    return pl.pallas_call(
        body,
        grid_spec=pltpu.PrefetchScalarGridSpec(
            num_scalar_prefetch=1, grid=(N_CHIPS - 1, T // tm),
            in_specs=[pl.BlockSpec((tm, D_MODEL), lambda j, i, s: (i, 0)),
                      pl.BlockSpec((D_MODEL, cols), lambda j, i, s: (0, block(j, s))), ANY],
            out_specs=pl.BlockSpec((tm, cols), lambda j, i, s: (i, block(j, s)))),
        out_shape=jax.ShapeDtypeStruct((T, D_IN), F32),
        input_output_aliases={3: 0},
        name="in_proj_rest", compiler_params=_params("arbitrary", "arbitrary"),
    )(chip, u, w_in, p)


def _scan_block_fwd(A, B, rowi):
    for d in (1, 2, 4):
        a_sh = pltpu.roll(A, d, axis=0)
        b_sh = pltpu.roll(B, d, axis=0)
        m = rowi >= d
        B = jnp.where(m, A * b_sh + B, B)
        A = jnp.where(m, A * a_sh, A)
    return A, B


def _scan_block_bwd(A, B, rowi):
    for d in (1, 2, 4):
        a_sh = pltpu.roll(A, 8 - d, axis=0)
        b_sh = pltpu.roll(B, 8 - d, axis=0)
        m = rowi < 8 - d
        B = jnp.where(m, A * b_sh + B, B)
        A = jnp.where(m, A * a_sh, A)
    return A, B


def _rg_gates(xc, w_ref, bg_ref, lam):
    pre = _dot(xc.astype(BF16), w_ref[...]) + bg_ref[...]
    r = _sigmoid(pre[:, :D_RG])
    ig = _sigmoid(pre[:, D_RG:])
    sp = _softplus_neg(lam)
    la = -LRU_C * sp * r
    a = jnp.exp(la)
    th = jnp.tanh(la)
    u = 1.0 - th
    rc = pl.reciprocal(u, approx=True)
    rc = rc * (2.0 - u * rc)
    rc = rc * (2.0 - u * rc)
    m2 = -2.0 * th * rc
    inv_m = lax.rsqrt(jnp.maximum(m2, 1e-30))
    return r, ig, sp, a, m2 * inv_m, inv_m


def _conv(ext, cw_ref, cb_ref, tm):
    xc = cb_ref[...] + cw_ref[0:1, :] * ext[8 - 3:8 - 3 + tm, :]
    for j in range(1, CONV_W):
        xc = xc + cw_ref[j:j + 1, :] * ext[8 - 3 + j:8 - 3 + j + tm, :]
    return xc


def _scan_unroll(blocks):
    return 4 if blocks % 4 == 0 else 2 if blocks % 2 == 0 else 1


def _rg_fwd(p, cw, cb, wg, bg, lam, rg_g):
    T = p.shape[0]
    tm = _row_tile(T, 832)
    unroll = _scan_unroll(tm // 8)

    def body(xg_ref, cw_ref, cb_ref, w_ref, bg_ref, lam_ref, g_ref, y_ref, h_ref, xc_ref, ext, a_s, b_s, carry):
        i = pl.program_id(0)

        @pl.when(i == 0)
        def _():
            ext[0:8, :] = jnp.zeros((8, D_RG), F32)
            carry[...] = jnp.zeros((1, D_RG), F32)

        ext[8:8 + tm, :] = xg_ref[:, :D_RG]
        xc = _conv(ext, cw_ref, cb_ref, tm)
        xc_ref[...] = xc
        r, ig, sp, a, m, _ = _rg_gates(xc, w_ref, bg_ref, lam_ref[...])
        row = i * tm + lax.broadcasted_iota(jnp.int32, (tm, 1), 0)
        a_s[...] = a
        b_s[...] = jnp.where(row >= PAD, m * ig * xc, 0.0)
        rowi = lax.broadcasted_iota(jnp.int32, (8, D_RG), 0)

        def blk(j, c):
            for u in range(unroll):
                o = pl.multiple_of((j * unroll + u) * 8, 8)
                A, B = _scan_block_fwd(a_s[pl.ds(o, 8), :], b_s[pl.ds(o, 8), :], rowi)
                h = B + A * c
                h_ref[pl.ds(o, 8), :] = h
                c = h[7:8, :]
            return c

        carry[...] = lax.fori_loop(0, tm // (8 * unroll), blk, carry[...])
        ext[0:8, :] = ext[tm:tm + 8, :]
        g, _ = _gelu_parts(xg_ref[:, D_RG:])
        yy = g * h_ref[...]
        y_ref[...] = (yy * _rms(yy) * g_ref[...]).astype(BF16)

    vec = lambda n: pl.BlockSpec((1, n), lambda i: (0, 0))
    return pl.pallas_call(
        body, grid=(T // tm,),
        in_specs=[pl.BlockSpec((tm, 2 * D_RG), lambda i: (i, 0)),
                  pl.BlockSpec((CONV_W, D_RG), lambda i: (0, 0)), vec(D_RG),
                  pl.BlockSpec((D_RG, 2 * D_RG), lambda i: (0, 0)), vec(2 * D_RG), vec(D_RG), vec(D_RG)],
        out_specs=[pl.BlockSpec((tm, D_RG), lambda i: (i, 0))] * 3,
        out_shape=[jax.ShapeDtypeStruct((T, D_RG), BF16), jax.ShapeDtypeStruct((T, D_RG), F32),
                   jax.ShapeDtypeStruct((T, D_RG), F32)],
        scratch_shapes=[pltpu.VMEM((tm + 8, D_RG), F32), pltpu.VMEM((tm, D_RG), F32),
                        pltpu.VMEM((tm, D_RG), F32), pltpu.VMEM((1, D_RG), F32)],
        name="rg_fwd", compiler_params=_params("arbitrary"),
    )(p, cw, cb, wg, bg, lam, rg_g)


def _running_sum(x, down):
    r = lax.broadcasted_iota(jnp.int32, (CHUNK, CHUNK), 0)
    c = lax.broadcasted_iota(jnp.int32, (CHUNK, CHUNK), 1)
    tri = ((c <= r) if down else (c >= r)).astype(BF16)
    hi = x.astype(BF16)
    rest = x - hi.astype(F32)
    mid = rest.astype(BF16)
    lo = (rest - mid.astype(F32)).astype(BF16)
    return (_dot(tri, hi) + _dot(tri, mid)) + _dot(tri, lo)


def _hg_gates(hq, hf, lbraw_ref, valid):
    lb = _sigmoid(lbraw_ref[0:1, :] - lbraw_ref[1:2, :])
    sq = _sigmoid(hq)
    q = hq * sq
    sf = _sigmoid(hf)
    f = lb + (1.0 - lb) * sf
    lf = jnp.where(valid, jnp.log(f), 0.0)
    b = _running_sum(lf, True)
    return lb, sq, q, sf, f, b


def _hg_head(qh, kh, bh):
    b_last = bh[CHUNK - 1:CHUNK, :]
    refs = [bh[SUB * s:SUB * s + 1, :] for s in range(N_SUB)]
    r_sel = jnp.concatenate([jnp.broadcast_to(refs[s], (SUB, HG_HEAD_DIM)) for s in range(N_SUB)], axis=0)
    eb = jnp.exp(bh)
    eq = jnp.exp(bh - r_sel)
    ekh = jnp.exp(b_last - bh)
    ek = [jnp.exp(jnp.minimum(refs[s] - bh[:SUB * (s + 1), :], EXP_CLAMP)) for s in range(N_SUB)]
    qe = qh * eq

    def own_rows(s):
        parts = [jnp.zeros((SUB * s, HG_HEAD_DIM), F32)] if s else []
        parts.append(qe[SUB * s:SUB * (s + 1), :])
        if s < N_SUB - 1:
            parts.append(jnp.zeros((CHUNK - SUB * (s + 1), HG_HEAD_DIM), F32))
        return jnp.concatenate(parts, axis=0)

    q_hat = jnp.concatenate([own_rows(s) for s in range(N_SUB)], axis=1)

    def met_rows(s):
        n = SUB * (s + 1)
        ke = kh[:n, :] * ek[s]
        return ke if n == CHUNK else jnp.concatenate([ke, jnp.zeros((CHUNK - n, HG_HEAD_DIM), F32)], axis=0)

    k_til = jnp.concatenate([met_rows(s) for s in range(N_SUB)], axis=1)
    return b_last, eb, eq, ekh, ek, q_hat, k_til


def _causal():
    r = lax.broadcasted_iota(jnp.int32, (CHUNK, CHUNK), 0)
    c = lax.broadcasted_iota(jnp.int32, (CHUNK, CHUNK), 1)
    return r >= c


def _chunks_per_step(n_chunks):
    for c in (5, 4, 3, 2):
        if n_chunks % c == 0:
            return c
    return 1


def _hg_fwd(p, lbraw, hg_g):
    T = p.shape[0]
    n_chunks = T // CHUNK
    cps = _chunks_per_step(n_chunks)
    rows = cps * CHUNK

    def body(hq_ref, hf_ref, hi_ref, hg_ref, lb_ref, g_ref, y_ref, o_ref, st_all_ref, st):
        i = pl.program_id(0)

        @pl.when(i == 0)
        def _():
            st[...] = jnp.zeros_like(st)

        def chunk(j, carry):
            rs = pl.ds(pl.multiple_of(j * CHUNK, CHUNK), CHUNK)
            chunk_body(i * cps + j, hq_ref.at[rs, :], hf_ref.at[rs, :], hi_ref.at[rs, :], hg_ref.at[rs, :], lb_ref,
                       g_ref, y_ref.at[rs, :], o_ref.at[rs, :], st_all_ref.at[pl.ds(j, 1)], st)
            return carry

        lax.fori_loop(0, cps, chunk, 0, unroll=True)

    def chunk_body(n, hq_ref, hf_ref, hi_ref, hg_ref, lb_ref, g_ref, y_ref, o_ref, st_all_ref, st):
        valid = (n * CHUNK + lax.broadcasted_iota(jnp.int32, (CHUNK, 1), 0)) >= PAD
        hq, hf, v, hg = hq_ref[...], hf_ref[...], hi_ref[...], hg_ref[...]
        lb, sq, q, sf, f, b = _hg_gates(hq, hf, lb_ref, valid)
        k = 1.0 - f
        st_all_ref[0] = st[...]
        causal = _causal()
        v_t = v.T.astype(BF16)
        heads = [slice(h * HG_HEAD_DIM, (h + 1) * HG_HEAD_DIM) for h in range(HG_HEADS)]
        fac = []
        for sl in heads:
            qh, kh, bh = q[:, sl], k[:, sl], b[:, sl]
            b_last, eb, _, ekh, _, q_hat, k_til = _hg_head(qh, kh, bh)
            fac.append((jnp.exp(b_last), (qh * eb).astype(BF16), q_hat.astype(BF16), k_til.astype(BF16),
                        (kh * ekh).astype(BF16), v[:, sl].astype(BF16)))
        raw = []
        for sl, (_, q_til, q_hat, k_til, k_hat, _) in zip(heads, fac):
            st_h = st[sl, :]
            raw.append((_dot_nt(q_til, st_h.astype(BF16)), _dot_nt(q_hat, k_til), _dot(v_t[sl, :], k_hat), st_h))
        for sl, (e_last, _, _, _, _, vb), (inter, att, upd, st_h) in zip(heads, fac, raw):
            o = inter + _dot(jnp.where(causal, att, 0.0).astype(BF16), vb)
            st[sl, :] = st_h * e_last + upd
            o_ref[:, sl] = o
            hgh = hg[:, sl]
            y_ref[:, sl] = (o * _rms(o) * g_ref[...] * (hgh * _sigmoid(hgh))).astype(BF16)

    col = lambda j: pl.BlockSpec((rows, D_HG), lambda n: (n, j))
    return pl.pallas_call(
        body, grid=(n_chunks // cps,),
        in_specs=[col(2), col(3), col(4), col(5),
                  pl.BlockSpec((2, D_HG), lambda n: (0, 0)), pl.BlockSpec((1, HG_HEAD_DIM), lambda n: (0, 0))],
        out_specs=[pl.BlockSpec((rows, D_HG), lambda n: (n, 0)), pl.BlockSpec((rows, D_HG), lambda n: (n, 0)),
                   pl.BlockSpec((cps, D_HG, HG_HEAD_DIM), lambda n: (n, 0, 0))],
        out_shape=[jax.ShapeDtypeStruct((T, D_HG), BF16), jax.ShapeDtypeStruct((T, D_HG), F32),
                   jax.ShapeDtypeStruct((n_chunks, D_HG, HG_HEAD_DIM), F32)],
        scratch_shapes=[pltpu.VMEM((D_HG, HG_HEAD_DIM), F32)],
        name="hg_fwd", compiler_params=_params("arbitrary"),
    )(p, p, p, p, lbraw, hg_g)


def _ffn_fwd(h0, y_rg, y_hg, w_out, g2, w_gu, w_down, gf, target):
    T = h0.shape[0]
    tm = _row_tile(T, 320)
    n_steps = T // tm

    def body(h_ref, yr_ref, yh_ref, wo_ref, g2_ref, wgu_ref, wd_ref, gf_ref, t_hbm,
             h1_ref, v_ref, y_ref, gu_ref, act_ref, dh2_ref, dh2b_ref, loss_ref, gg_ref, tbuf, sems):
        i = pl.program_id(0)
        slot = _fetch_window(t_hbm, tbuf, sems, i, n_steps, tm)

        @pl.when(i == 0)
        def _():
            loss_ref[...] = jnp.zeros_like(loss_ref)
            gg_ref[...] = jnp.zeros_like(gg_ref)
            tbuf[0, 0:HEAD, :] = jnp.zeros((HEAD, D_MODEL), F32)

        y_ref[:, :D_RG] = yr_ref[...]
        y_ref[:, D_RG:] = yh_ref[...]
        h1 = h_ref[...] + _dot(y_ref[...], wo_ref[...])
        h1_ref[...] = h1
        v = (h1 * _rms(h1) * g2_ref[...]).astype(BF16)
        v_ref[...] = v

        gu = _dot(v, wgu_ref[...])
        gu_ref[...] = gu.astype(BF16)
        g = gu[:, :D_FF]
        act = (g * _sigmoid(g) * gu[:, D_FF:]).astype(BF16)
        act_ref[...] = act

        h2 = h1 + _dot(act, wd_ref[...])
        r = _rms(h2)
        n = h2 * r
        gf_ = gf_ref[...]
        row = i * tm + lax.broadcasted_iota(jnp.int32, (tm, 1), 0)
        err = jnp.where(row >= HEAD, n * gf_ - tbuf[slot], 0.0)
        loss_ref[...] += 0.5 * jnp.sum(jnp.mean(err * err, axis=-1, keepdims=True), axis=0, keepdims=True)
        dy = err * (1.0 / D_MODEL)
        gg_ref[...] += jnp.sum(dy * n, axis=0, keepdims=True)
        dh2 = _rms_bwd(dy * gf_, n, r)
        dh2_ref[...] = dh2
        dh2b_ref[...] = dh2.astype(BF16)

    row_spec = lambda n: pl.BlockSpec((tm, n), lambda i: (i, 0))
    vec = pl.BlockSpec((1, D_MODEL), lambda i: (0, 0))
    return pl.pallas_call(
        body, grid=(n_steps,),
        in_specs=[row_spec(D_MODEL), row_spec(D_RG), row_spec(D_HG), _resident((D_MODEL, D_MODEL)), vec,
                  _resident((D_MODEL, 2 * D_FF)), _resident((D_FF, D_MODEL)), vec,
                  pl.BlockSpec(memory_space=pl.ANY)],
        out_specs=[row_spec(D_MODEL), row_spec(D_MODEL), row_spec(D_MODEL), row_spec(2 * D_FF), row_spec(D_FF),
                   row_spec(D_MODEL), row_spec(D_MODEL), pl.BlockSpec((1, 1), lambda i: (0, 0)), vec],
        out_shape=[jax.ShapeDtypeStruct((T, D_MODEL), F32), jax.ShapeDtypeStruct((T, D_MODEL), BF16),
                   jax.ShapeDtypeStruct((T, D_MODEL), BF16), jax.ShapeDtypeStruct((T, 2 * D_FF), BF16),
                   jax.ShapeDtypeStruct((T, D_FF), BF16), jax.ShapeDtypeStruct((T, D_MODEL), F32),
                   jax.ShapeDtypeStruct((T, D_MODEL), BF16), jax.ShapeDtypeStruct((1, 1), F32),
                   jax.ShapeDtypeStruct((1, D_MODEL), F32)],
        scratch_shapes=[pltpu.VMEM((2, tm, D_MODEL), F32), pltpu.SemaphoreType.DMA((2,))],
        name="ffn_fwd", compiler_params=_params("arbitrary"),
    )(h0, y_rg, y_hg, w_out, g2, w_gu, w_down, gf, target)


def _resident(shape):
    return pl.BlockSpec(shape, lambda i: (0,) * len(shape), pipeline_mode=pl.Buffered(1))


def _ffn_bwd(dh2b, gu, w_down, w_gu, h1, g2, dh2, w_out):
    T = h1.shape[0]
    tm = _row_tile(T, 320)

    def body(d_ref, gu_ref, wd_ref, wgu_ref, h_ref, g_ref, d2_ref, wo_ref, dgu_ref, dh1_ref, dh1b_ref, dy_ref, gg_ref):
        i = pl.program_id(0)

        @pl.when(i == 0)
        def _():
            gg_ref[...] = jnp.zeros_like(gg_ref)

        dact = _dot_nt(d_ref[...], wd_ref[...]).astype(BF16)
        g = gu_ref[:, :D_FF]
        u = gu_ref[:, D_FF:]
        s = _sigmoid(g)
        dgu_ref[:, :D_FF] = dact * u * (s * (1.0 + g * (1.0 - s)))
        dgu_ref[:, D_FF:] = dact * (g * s)

        dv = _dot_nt(dgu_ref[...], wgu_ref[...])
        h1_ = h_ref[...]
        r = _rms(h1_)
        n = h1_ * r
        gg_ref[...] += jnp.sum(dv * n, axis=0, keepdims=True)
        dh1 = d2_ref[...] + _rms_bwd(dv * g_ref[...], n, r)
        dh1_ref[...] = dh1
        db = dh1.astype(BF16)
        dh1b_ref[...] = db
        dy_ref[...] = _dot_nt(db, wo_ref[...])

    row = lambda n: pl.BlockSpec((tm, n), lambda i: (i, 0))
    return pl.pallas_call(
        body, grid=(T // tm,),
        in_specs=[row(D_MODEL), row(2 * D_FF), _resident((D_FF, D_MODEL)), _resident((D_MODEL, 2 * D_FF)),
                  row(D_MODEL), pl.BlockSpec((1, D_MODEL), lambda i: (0, 0)), row(D_MODEL),
                  _resident((D_MODEL, D_MODEL))],
        out_specs=[row(2 * D_FF), row(D_MODEL), row(D_MODEL), row(D_MODEL),
                   pl.BlockSpec((1, D_MODEL), lambda i: (0, 0))],
        out_shape=[jax.ShapeDtypeStruct((T, 2 * D_FF), BF16), jax.ShapeDtypeStruct((T, D_MODEL), F32),
                   jax.ShapeDtypeStruct((T, D_MODEL), BF16), jax.ShapeDtypeStruct((T, D_MODEL), F32),
                   jax.ShapeDtypeStruct((1, D_MODEL), F32)],
        name="ffn_bwd", compiler_params=_params("arbitrary"),
    )(dh2b, gu, w_down, w_gu, h1, g2, dh2, w_out)


def _rg_bwd(p, xc_all, hs, dy, dp, cw, cb, wg, bg, lam, rg_g):
    T = p.shape[0]
    tm = _row_tile(T, 832)
    nt = T // tm
    hb = tm // 8
    unroll = _scan_unroll(hb)

    def body(xg_ref, xc_ref, h_ref, hh_ref, dy_ref, dp_in_ref, cw_ref, cb_ref, w_ref, bg_ref, lam_ref, g_ref,
             dp_ref, gcw_ref, gcb_ref, gw_ref, gbg_ref, glam_ref, gg_ref,
             dext, a_s, b_s, d_s, gacc, carry_d, carry_a):
        i = pl.program_id(0)
        t_idx = nt - 1 - i

        @pl.when(i == 0)
        def _():
            dext[tm:tm + 8, :] = jnp.zeros((8, D_RG), F32)
            carry_d[...] = jnp.zeros_like(carry_d)
            carry_a[...] = jnp.zeros_like(carry_a)
            gacc[...] = jnp.zeros_like(gacc)
            for ref in (gcw_ref, gcb_ref, gbg_ref, glam_ref, gg_ref, gw_ref):
                ref[...] = jnp.zeros_like(ref)

        first = t_idx == 0
        xc = xc_ref[...]
        lam_ = lam_ref[...]
        r, ig, sp, a, m, inv_m = _rg_gates(xc, w_ref, bg_ref, lam_)
        row = t_idx * tm + lax.broadcasted_iota(jnp.int32, (tm, 1), 0)
        valid = row >= PAD

        gr = xg_ref[:, D_RG:]
        g, dgelu = _gelu_parts(gr)
        h = h_ref[...]
        yy = g * h
        rr = _rms(yy)
        nn = yy * rr
        dy_ = dy_ref[...]
        gg_ref[...] += jnp.sum(dy_ * nn, axis=0, keepdims=True)
        dyy = _rms_bwd(dy_ * g_ref[...], nn, rr)
        dp_ref[:, D_RG:] = (dyy * h * dgelu).astype(BF16)

        a_s[...] = a
        b_s[...] = dyy * g
        rowi = lax.broadcasted_iota(jnp.int32, (8, D_RG), 0)

        def blk(jj, c):
            cd, ca = c
            for u in range(unroll):
                o = pl.multiple_of((hb - 1 - (jj * unroll + u)) * 8, 8)
                a_blk = a_s[pl.ds(o, 8), :]
                a_next = jnp.where(rowi == 7, ca, pltpu.roll(a_blk, 7, axis=0))
                A, B = _scan_block_bwd(a_next, b_s[pl.ds(o, 8), :], rowi)
                d = B + A * cd
                d_s[pl.ds(o, 8), :] = d
                cd, ca = d[0:1, :], a_blk[0:1, :]
            return cd, ca

        cd, ca = lax.fori_loop(0, hb // unroll, blk, (carry_d[...], carry_a[...]))
        carry_d[...] = cd
        carry_a[...] = ca
        delta = d_s[...]

        h_last_prev = jnp.where(first, 0.0, hh_ref[7:8, :])
        row0 = lax.broadcasted_iota(jnp.int32, (tm, 1), 0) == 0
        h_prev = jnp.where(row0, h_last_prev, pltpu.roll(h, 1, axis=0))
        dbx = jnp.where(valid, delta, 0.0)
        da = delta * h_prev
        di = dbx * m * xc
        dm = dbx * ig * xc
        dla = a * (da - dm * a * inv_m)
        dla = jnp.where(valid, dla, 0.0)
        glam_ref[...] += jnp.sum(dla * r, axis=0, keepdims=True) * (LRU_C / (1.0 + jnp.exp(lam_)))
        dr = (-LRU_C) * sp * dla
        dpre = jnp.concatenate([dr * r * (1.0 - r), di * ig * (1.0 - ig)], axis=1)
        gbg_ref[...] += jnp.sum(dpre, axis=0, keepdims=True)
        dpre_b = dpre.astype(BF16)
        gacc[...] += _dot_tn(xc.astype(BF16), dpre_b)
        dxc = dbx * m * ig + _dot_nt(dpre_b, w_ref[...])
        gcb_ref[...] += jnp.sum(dxc, axis=0, keepdims=True)
        dext[0:tm, :] = dxc
        xr = xg_ref[:, :D_RG]
        dxr = None
        for j in range(CONV_W):
            shifted = dext[3 - j:3 - j + tm, :]
            gcw_ref[j:j + 1, :] += jnp.sum(xr * shifted, axis=0, keepdims=True)
            tap = cw_ref[j:j + 1, :] * shifted
            dxr = tap if dxr is None else dxr + tap
        dp_ref[:, :D_RG] = dxr.astype(BF16)
        dext[tm:tm + 8, :] = dext[0:8, :]

        @pl.when(i == nt - 1)
        def _():
            fold = _head_fold()
            mask = _head_mask()
            fold_b = fold.astype(BF16)
            for k in range(2):
                blockdiag = jnp.where(mask, gacc[:, k * D_RG:(k + 1) * D_RG], 0.0)
                hi = blockdiag.astype(BF16)
                rest = blockdiag - hi.astype(F32)
                mid = rest.astype(BF16)
                lo = (rest - mid.astype(F32)).astype(BF16)
                gw_ref[k * D_RG:(k + 1) * D_RG, :] = (_dot(hi, fold_b) + _dot(mid, fold_b)) + _dot(lo, fold_b)

    vec = lambda n: pl.BlockSpec((1, n), lambda i: (0, 0))
    rev = lambda n: pl.BlockSpec((tm, n), lambda i: (nt - 1 - i, 0))
    halo = lambda n: pl.BlockSpec((8, n), lambda i: (jnp.maximum((nt - 1 - i) * hb - 1, 0), 0))
    return pl.pallas_call(
        body, grid=(nt,),
        in_specs=[rev(2 * D_RG), rev(D_RG), rev(D_RG), halo(D_RG), rev(D_RG), ANY,
                  pl.BlockSpec((CONV_W, D_RG), lambda i: (0, 0)), vec(D_RG),
                  pl.BlockSpec((D_RG, 2 * D_RG), lambda i: (0, 0)), vec(2 * D_RG), vec(D_RG), vec(D_RG)],
        out_specs=[rev(2 * D_RG), pl.BlockSpec((CONV_W, D_RG), lambda i: (0, 0)), vec(D_RG),
                   pl.BlockSpec((2 * D_RG, RG_HEAD_DIM), lambda i: (0, 0)), vec(2 * D_RG), vec(D_RG), vec(D_RG)],
        input_output_aliases={5: 0},
        out_shape=[jax.ShapeDtypeStruct((T, D_IN), BF16), jax.ShapeDtypeStruct((CONV_W, D_RG), F32),
                   jax.ShapeDtypeStruct((1, D_RG), F32), jax.ShapeDtypeStruct((2 * D_RG, RG_HEAD_DIM), F32),
                   jax.ShapeDtypeStruct((1, 2 * D_RG), F32), jax.ShapeDtypeStruct((1, D_RG), F32),
                   jax.ShapeDtypeStruct((1, D_RG), F32)],
        scratch_shapes=[pltpu.VMEM((tm + 8, D_RG), F32),
                        pltpu.VMEM((tm, D_RG), F32), pltpu.VMEM((tm, D_RG), F32), pltpu.VMEM((tm, D_RG), F32),
                        pltpu.VMEM((D_RG, 2 * D_RG), F32), pltpu.VMEM((1, D_RG), F32), pltpu.VMEM((1, D_RG), F32)],
        name="rg_bwd", compiler_params=_params("arbitrary"),
    )(p, xc_all, hs, hs, dy, dp, cw, cb, wg, bg, lam, rg_g)


def _hg_bwd(p, o_all, st_all, dy, lbraw, hg_g):
    T = p.shape[0]
    n_chunks = T // CHUNK
    cps = _chunks_per_step(n_chunks)
    rows = cps * CHUNK
    n_steps = n_chunks // cps

    def body(hq_ref, hf_ref, hi_ref, hg_ref, o_ref, st_ref, dy_ref, lb_ref, g_ref,
             dp_ref, glb_ref, gg_ref, dst):
        i = pl.program_id(0)

        @pl.when(i == 0)
        def _():
            dst[...] = jnp.zeros_like(dst)
            glb_ref[...] = jnp.zeros_like(glb_ref)
            gg_ref[...] = jnp.zeros_like(gg_ref)

        dp_ref[:, :2 * D_RG] = jnp.zeros((rows, 2 * D_RG), BF16)

        def chunk(jj, carry):
            j = cps - 1 - jj
            rs = pl.ds(pl.multiple_of(j * CHUNK, CHUNK), CHUNK)
            chunk_body((n_steps - 1 - i) * cps + j, hq_ref.at[rs, :], hf_ref.at[rs, :], hi_ref.at[rs, :],
                       hg_ref.at[rs, :], o_ref.at[rs, :], st_ref.at[pl.ds(j, 1)], dy_ref.at[rs, :], lb_ref, g_ref,
                       dp_ref.at[rs, pl.ds(2 * D_RG, 4 * D_HG)], glb_ref, gg_ref, dst)
            return carry

        lax.fori_loop(0, cps, chunk, 0, unroll=True)

    def chunk_body(n, hq_ref, hf_ref, hi_ref, hg_ref, o_ref, st_ref, dy_ref, lb_ref, g_ref,
                   dp_ref, glb_ref, gg_ref, dst):
        valid = (n * CHUNK + lax.broadcasted_iota(jnp.int32, (CHUNK, 1), 0)) >= PAD
        hq, hf, v, hg = hq_ref[...], hf_ref[...], hi_ref[...], hg_ref[...]
        lb, sq, q, sf, f, b = _hg_gates(hq, hf, lb_ref, valid)
        k = 1.0 - f
        causal = _causal()
        r_i = lax.broadcasted_iota(jnp.int32, (CHUNK, CHUNK), 0)
        c_i = lax.broadcasted_iota(jnp.int32, (CHUNK, CHUNK), 1)
        causal_t = r_i <= c_i
        is_last = lax.broadcasted_iota(jnp.int32, (CHUNK, 1), 0) == CHUNK - 1
        g_ = g_ref[...]
        db_parts, dq_parts, dk_parts = [], [], []
        gg = jnp.zeros((1, HG_HEAD_DIM), F32)
        heads = [slice(h * HG_HEAD_DIM, (h + 1) * HG_HEAD_DIM) for h in range(HG_HEADS)]

        do_parts = []
        for h, sl in enumerate(heads):
            o = o_ref[:, sl]
            ro = _rms(o)
            no = o * ro
            hgh = hg[:, sl]
            sg = _sigmoid(hgh)
            dyh = dy_ref[:, sl]
            dp_ref[:, 3 * D_HG + h * HG_HEAD_DIM:3 * D_HG + (h + 1) * HG_HEAD_DIM] = (
                dyh * no * g_ * sg * (1.0 + hgh * (1.0 - sg))).astype(BF16)
            dng = dyh * hgh * sg
            gg = gg + jnp.sum(dng * no, axis=0, keepdims=True)
            do_parts.append(_rms_bwd(dng * g_, no, ro))
        do_t = jnp.concatenate(do_parts, axis=1).T.astype(BF16)

        fac = []
        for sl, do in zip(heads, do_parts):
            qh, kh, bh = q[:, sl], k[:, sl], b[:, sl]
            b_last, eb, eq, ekh, ek, q_hat, k_til = _hg_head(qh, kh, bh)
            fac.append(dict(qh=qh, kh=kh, e_last=jnp.exp(b_last), eb=eb, eq=eq, ekh=ekh, ek=ek,
                            q_til=qh * eb, k_hat=kh * ekh, qhb=q_hat.astype(BF16), ktb=k_til.astype(BF16),
                            vb=v[:, sl].astype(BF16), dob=do.astype(BF16)))

        first = []
        for sl, t in zip(heads, fac):
            st_h = st_ref[0, sl, :]
            dst_h = dst[sl, :]
            dstb = dst_h.astype(BF16)
            first.append(dict(
                att_t=_dot_nt(t["ktb"], t["qhb"]), datt=_dot_nt(t["dob"], t["vb"]),
                datt_t=_dot_nt(t["vb"], t["dob"]), dk_hat=_dot(t["vb"], dstb),
                dv=_dot_nt(t["k_hat"].astype(BF16), dstb), dq_til=_dot(t["dob"], st_h.astype(BF16)),
                state=t["e_last"] * jnp.sum(dst_h * st_h, axis=0, keepdims=True)))
            dst[sl, :] = dst_h * t["e_last"] + _dot(do_t[sl, :], t["q_til"].astype(BF16))

        for h, (t, m) in enumerate(zip(fac, first)):
            qh, kh, eb, eq, ekh, ek = t["qh"], t["kh"], t["eb"], t["eq"], t["ekh"], t["ek"]
            q_til, k_hat, qhb, ktb, dob = t["q_til"], t["k_hat"], t["qhb"], t["ktb"], t["dob"]
            dk_hat, dq_til = m["dk_hat"], m["dq_til"]
            dv = m["dv"] + _dot(jnp.where(causal_t, m["att_t"], 0.0).astype(BF16), dob)
            dq_hat = _dot(jnp.where(causal, m["datt"], 0.0).astype(BF16), ktb)
            dk_til = _dot(jnp.where(causal_t, m["datt_t"], 0.0).astype(BF16), qhb)
            db_last = jnp.sum(dk_hat * k_hat, axis=0, keepdims=True) + m["state"]
            dq_sel = jnp.concatenate([dq_hat[SUB * s:SUB * (s + 1), s * HG_HEAD_DIM:(s + 1) * HG_HEAD_DIM]
                                      for s in range(N_SUB)], axis=0)
            dq_a = dq_sel * eq
            dk_rows, k_att_rows = [], []
            for b_ in range(N_SUB):
                rs = slice(SUB * b_, SUB * (b_ + 1))
                dk_sum = k_att_sum = None
                for s in range(b_, N_SUB):
                    cs = slice(s * HG_HEAD_DIM, (s + 1) * HG_HEAD_DIM)
                    d = dk_til[rs, cs]
                    t_dk = d * ek[s][rs, :]
                    t_att = ktb[rs, cs].astype(F32) * d
                    dk_sum = t_dk if dk_sum is None else dk_sum + t_dk
                    k_att_sum = t_att if k_att_sum is None else k_att_sum + t_att
                dk_rows.append(dk_sum)
                k_att_rows.append(k_att_sum)
            dk_a = jnp.concatenate(dk_rows, axis=0)
            db = (dq_til * q_til - dk_hat * k_hat + (qh * eq).astype(BF16).astype(F32) * dq_sel
                  - jnp.concatenate(k_att_rows, axis=0))
            db_parts.append(jnp.where(is_last, db + db_last, db))
            dq_parts.append(dq_til * eb + dq_a)
            dk_parts.append(dk_hat * ekh + dk_a)
            dp_ref[:, 2 * D_HG + h * HG_HEAD_DIM:2 * D_HG + (h + 1) * HG_HEAD_DIM] = dv.astype(BF16)

        gg_ref[...] += gg
        db = jnp.concatenate(db_parts, axis=1)
        dq = jnp.concatenate(dq_parts, axis=1)
        dk = jnp.concatenate(dk_parts, axis=1)
        dlf = jnp.where(valid, _running_sum(db, False), 0.0)
        dp_ref[:, :D_HG] = (dq * sq * (1.0 + hq * (1.0 - sq))).astype(BF16)
        df = dlf / f - dk
        dlb = jnp.sum(df * (1.0 - sf), axis=0, keepdims=True) * lb * (1.0 - lb)
        glb_ref[0:1, :] += dlb
        glb_ref[1:2, :] += -dlb
        dp_ref[:, D_HG:2 * D_HG] = (df * (1.0 - lb) * sf * (1.0 - sf)).astype(BF16)

    rev = lambda j: pl.BlockSpec((rows, D_HG), lambda i: (n_steps - 1 - i, j))
    return pl.pallas_call(
        body, grid=(n_steps,),
        in_specs=[rev(2), rev(3), rev(4), rev(5), rev(0),
                  pl.BlockSpec((cps, D_HG, HG_HEAD_DIM), lambda i: (n_steps - 1 - i, 0, 0)), rev(1),
                  pl.BlockSpec((2, D_HG), lambda i: (0, 0)), pl.BlockSpec((1, HG_HEAD_DIM), lambda i: (0, 0))],
        out_specs=[pl.BlockSpec((rows, D_IN), lambda i: (n_steps - 1 - i, 0)),
                   pl.BlockSpec((2, D_HG), lambda i: (0, 0)), pl.BlockSpec((1, HG_HEAD_DIM), lambda i: (0, 0))],
        out_shape=[jax.ShapeDtypeStruct((T, D_IN), BF16), jax.ShapeDtypeStruct((2, D_HG), F32),
                   jax.ShapeDtypeStruct((1, HG_HEAD_DIM), F32)],
        scratch_shapes=[pltpu.VMEM((D_HG, HG_HEAD_DIM), F32)],
        name="hg_bwd", compiler_params=_params("arbitrary"),
    )(p, p, p, p, o_all, st_all, dy, lbraw, hg_g)


def _in_bwd(dp, w_in, h0, g1, dh1):
    T = h0.shape[0]
    tm = _row_tile(T, 832)
    n_steps = T // tm

    def body(dp_ref, w_ref, h_ref, g_ref, d1_ref, gx_hbm, gmeta_ref, gg_ref, buf, sems):
        i = pl.program_id(0)
        first, later = _window_copies(gx_hbm, buf, sems, tm)
        slot = i % 2

        @pl.when(i == 0)
        def _():
            gg_ref[...] = jnp.zeros_like(gg_ref)

        if n_steps > 2:
            @pl.when(i == 2)
            def _():
                first(False).wait()

            @pl.when(i > 2)
            def _():
                later(i - 2, slot, False).wait()

        du = _dot_nt(dp_ref[...], w_ref[...])
        h0_ = h_ref[...]
        r = _rms(h0_)
        n = h0_ * r
        gg_ref[...] += jnp.sum(du * n, axis=0, keepdims=True)
        dh0 = d1_ref[...] + _rms_bwd(du * g_ref[...], n, r)
        buf[slot] = dh0

        @pl.when(i == 0)
        def _():
            gmeta_ref[...] = dh0[PAD:HEAD, :]
            first(False).start()

        if n_steps > 1:
            @pl.when(i > 0)
            def _():
                later(i, slot, False).start()

        @pl.when(i == n_steps - 1)
        def _():
            if n_steps == 1:
                first(False).wait()
            else:
                if n_steps == 2:
                    first(False).wait()
                else:
                    later(i - 1, 1 - slot, False).wait()
                later(i, slot, False).wait()

    row = lambda n: pl.BlockSpec((tm, n), lambda i: (i, 0))
    return pl.pallas_call(
        body, grid=(n_steps,),
        in_specs=[row(D_IN), _resident((D_MODEL, D_IN)),
                  row(D_MODEL), pl.BlockSpec((1, D_MODEL), lambda i: (0, 0)), row(D_MODEL)],
        out_specs=[pl.BlockSpec(memory_space=pl.ANY), pl.BlockSpec((N_META, D_MODEL), lambda i: (0, 0)),
                   pl.BlockSpec((1, D_MODEL), lambda i: (0, 0))],
        out_shape=[jax.ShapeDtypeStruct((T - HEAD, D_MODEL), F32), jax.ShapeDtypeStruct((N_META, D_MODEL), F32),
                   jax.ShapeDtypeStruct((1, D_MODEL), F32)],
        scratch_shapes=[pltpu.VMEM((2, tm, D_MODEL), F32), pltpu.SemaphoreType.DMA((2,))],
        name="in_bwd", compiler_params=_params("arbitrary"),
    )(dp, w_in, h0, g1, dh1)


def _col_tile(cols, target):
    best = None
    for t in range(128, min(cols, target) + 1, 128):
        if cols % t == 0:
            best = t
    assert best is not None, cols
    return best


MXU_DIM = 256


def _mxu_tile(cols, target):
    best = None
    for t in range(MXU_DIM, min(cols, target) + 1, MXU_DIM):
        if cols % t == 0:
            best = t
    assert best is not None, cols
    return best


def _weight_grad(a, b, name):
    T, M = a.shape
    N = b.shape[1]
    tm = _col_tile(M, 1408)
    tn = _mxu_tile(N, 768 if tm <= 1024 else 512)

    def body(a_ref, b_ref, o_ref):
        o_ref[...] = _dot_tn(a_ref[...], b_ref[...])

    return pl.pallas_call(
        body, grid=(M // tm, N // tn),
        in_specs=[pl.BlockSpec((T, tm), lambda m, n: (0, m)), pl.BlockSpec((T, tn), lambda m, n: (0, n))],
        out_specs=pl.BlockSpec((tm, tn), lambda m, n: (m, n)),
        out_shape=jax.ShapeDtypeStruct((M, N), F32),
        name=name, compiler_params=_params("parallel", "parallel"),
    )(a, b)


def _local_step(x, meta, target, w_in_own, w_in, w_out, w_gu, w_down, small, chip, on_ffn_grads=None,
                on_mixer_grads=None):
    wg = _gate_weights(small["w_rgate"], small["w_igate"])
    bg = jnp.concatenate([small["b_rgate"], small["b_igate"]], axis=1)

    p, u, h0 = _in_proj_local(x, meta, small["mix_norm_g"], w_in_own, chip)
    p = _in_proj_rest(u, w_in, p, chip)
    y_rg, hs, xc = _rg_fwd(p, small["conv_w"], small["conv_b"], wg, bg, small["lru_lambda"], small["rg_norm_g"])
    y_hg, o_all, st_all = _hg_fwd(p, small["hg_lower_bound"], small["hg_norm_g"])
    h1, v, yb, gu, act, dh2, dh2b, loss, g_final = _ffn_fwd(
        h0, y_rg, y_hg, w_out, small["ffn_norm_g"], w_gu, w_down, small["final_norm_g"], target)

    g_w_down = _weight_grad(act, dh2b, "grad_w_down")
    dgu, dh1, dh1b, dy, g_ffn = _ffn_bwd(dh2b, gu, w_down, w_gu, h1, small["ffn_norm_g"], dh2, w_out)
    ffn_grads = {"w_gate_up": _weight_grad(v, dgu, "grad_w_gate_up"), "w_down": g_w_down,
                 "w_out": _weight_grad(yb, dh1b, "grad_w_out")}
    stages = on_ffn_grads(ffn_grads) if on_ffn_grads is not None else None
    dp, g_lb, g_hgn = _hg_bwd(p, o_all, st_all, dy, small["hg_lower_bound"], small["hg_norm_g"])
    early = late = None
    if stages is not None:
        chip_sums, send = stages
        sums = chip_sums()
        (dp, dy), sums = lax.optimization_barrier(((dp, dy), sums))
        early = send(sums)
    dp, g_cw, g_cb, g_wgate, g_bg, g_lam, g_rgn = _rg_bwd(
        p, xc, hs, dy, dp, small["conv_w"], small["conv_b"], wg, bg, small["lru_lambda"], small["rg_norm_g"])
    mixer_grads = {"w_in": _weight_grad(u, dp, "grad_w_in")}
    if on_mixer_grads is not None:
        chip_sums, send = on_mixer_grads(mixer_grads)
        sums = chip_sums()
        (dp, dh1), sums = lax.optimization_barrier(((dp, dh1), sums))
        late = send(sums)
    grad_x, g_meta, g_mix = _in_bwd(dp, w_in, h0, small["mix_norm_g"], dh1)

    grads = {
        "w_in": mixer_grads["w_in"], "w_out": ffn_grads["w_out"],
        "w_gate_up": ffn_grads["w_gate_up"], "w_down": ffn_grads["w_down"],
        "meta_tokens": g_meta, "mix_norm_g": g_mix, "conv_w": g_cw, "conv_b": g_cb, "w_gates": g_wgate,
        "b_rgate": g_bg[:, :D_RG], "b_igate": g_bg[:, D_RG:], "lru_lambda": g_lam, "rg_norm_g": g_rgn,
        "hg_lower_bound": g_lb, "hg_norm_g": g_hgn, "ffn_norm_g": g_ffn, "final_norm_g": g_final,
    }
    return loss, grad_x, grads, early, late


ANY = pl.BlockSpec(memory_space=pl.ANY)
HALF = D_MODEL // 2

BIG = {"w_in": (D_MODEL, D_IN // N_CHIPS, True), "w_gate_up": (D_MODEL, 2 * D_FF // N_CHIPS, True),
       "w_out": (D_MODEL // N_CHIPS, D_MODEL, False), "w_down": (D_FF // N_CHIPS, D_MODEL, False)}
BIG_NAMES = tuple(BIG)
N_BIG = len(BIG_NAMES)


def _full_shape(name):
    rows, cols, by_col = BIG[name]
    return (rows, cols * N_CHIPS) if by_col else (rows * N_CHIPS, cols)


def _place():
    return lax.axis_index("x"), lax.axis_index("y"), lax.axis_index("c")


def _chip_of(x, y, r):
    fx, fy = (r + 1) >> 1, (r + 1) & 1
    return (1 - x if fx else x), (1 - y if fy else y)


def _half_of(ref, by_col, half):
    start = pl.multiple_of(half * HALF, 128)
    return ref.at[pl.ds(start, HALF), :] if by_col else ref.at[:, pl.ds(start, HALF)]


def _shard_of(ref, name, chip):
    rows, cols, by_col = BIG[name]
    if by_col:
        return ref.at[:, pl.ds(pl.multiple_of(chip * cols, 128), cols)]
    return ref.at[pl.ds(pl.multiple_of(chip * rows, 16), rows), :]


def _shard_half_of(ref, name, chip, half):
    rows, cols, by_col = BIG[name]
    start = pl.multiple_of(half * HALF, 128)
    if by_col:
        return ref.at[pl.ds(start, HALF), pl.ds(pl.multiple_of(chip * cols, 128), cols)]
    return ref.at[pl.ds(pl.multiple_of(chip * rows, 16), rows), pl.ds(start, HALF)]


def _shard_half_part_of(ref, name, chip, half, part):
    rows, cols, by_col = BIG[name]
    start = pl.multiple_of(half * HALF + part * (HALF // 2), 128)
    if by_col:
        return ref.at[pl.ds(start, HALF // 2), pl.ds(pl.multiple_of(chip * cols, 128), cols)]
    return ref.at[pl.ds(pl.multiple_of(chip * rows, 16), rows), pl.ds(start, HALF // 2)]


def _remote(src, dst, send_sems, recv_sems, k, dev):
    return pltpu.make_async_remote_copy(src_ref=src, dst_ref=dst, send_sem=send_sems.at[k], recv_sem=recv_sems.at[k],
                                        device_id=dev, device_id_type=MESH)


def _place_shards(w, small, chip, names, label):
    steps = 4
    n, ns = len(names), len(small)
    in_specs, out_specs = [], []
    for name in names:
        rows, cols, by_col = BIG[name]
        tr = rows // steps
        in_specs.append(pl.BlockSpec((tr, cols), lambda i, s: (i, 0)))
        if by_col:
            out_specs.append(pl.BlockSpec((tr, cols), lambda i, s: (i, s[0])))
        else:
            out_specs.append(pl.BlockSpec((tr, cols), lambda i, s: (s[0] * steps + i, 0)))

    def body(s_ref, *refs):
        ins, small_in = refs[:n], refs[n:n + ns]
        outs, small_out = refs[n + ns:2 * n + ns], refs[2 * n + ns:2 * (n + ns)]
        send_sems, recv_sems, local_sems = refs[2 * (n + ns):]
        i = pl.program_id(0)
        x, y, c = _place()
        chip_ = 2 * x + y
        others = [_chip_of(x, y, r) for r in range(3)]

        def block(a, q):
            cols = small[a].shape[1]
            return small_out[a].at[:, pl.ds(pl.multiple_of(q * cols, 128), cols)]

        def local(a):
            return pltpu.make_async_copy(small_in[a], block(a, chip_), local_sems.at[a])

        def remote(a, r):
            qx, qy = others[r]
            return _remote(small_in[a], block(a, chip_), send_sems, recv_sems, 3 * a + r, (qx, qy, c))

        @pl.when(i == 0)
        def _():
            for a in range(ns):
                local(a).start()
                for r in range(3):
                    remote(a, r).start()

        for a in range(n):
            outs[a][...] = ins[a][...].astype(BF16)

        @pl.when(i == steps - 1)
        def _():
            for a in range(ns):
                for r, (qx, qy) in enumerate(others):
                    landed = block(a, 2 * qx + qy)
                    _remote(landed, landed, send_sems, recv_sems, 3 * a + r, (qx, qy, c)).wait_recv()
                for r in range(3):
                    remote(a, r).wait_send()
                local(a).wait()

    out = pl.pallas_call(
        body,
        grid_spec=pltpu.PrefetchScalarGridSpec(
            num_scalar_prefetch=1, grid=(steps,), in_specs=in_specs + [ANY] * ns, out_specs=out_specs + [ANY] * ns,
            scratch_shapes=[pltpu.SemaphoreType.DMA((max(3 * ns, 1),)), pltpu.SemaphoreType.DMA((max(3 * ns, 1),)),
                            pltpu.SemaphoreType.DMA((max(ns, 1),))]),
        out_shape=([jax.ShapeDtypeStruct(_full_shape(name), BF16) for name in names]
                   + [jax.ShapeDtypeStruct((s.shape[0], s.shape[1] * N_CHIPS), F32) for s in small]),
        name=label, compiler_params=_params("arbitrary"),
    )(chip, *[w[name] for name in names], *small)
    return dict(zip(names, out[:n])), list(out[n:])


def _gather_weights(placed, small, names, label, collective_id):
    n, ns = len(names), len(small)
    hbm = pltpu.MemorySpace.HBM
    outs = [jax.new_ref(placed[nm], memory_space=hbm) for nm in names]
    small_in = [jax.new_ref(s, memory_space=hbm) for s in small]
    small_out = [jax.empty_ref(jax.ShapeDtypeStruct((s.shape[0], s.shape[1] * N_CHIPS), F32), memory_space=hbm)
                 for s in small]
    n_sems = 8 * n + 3 * ns

    @pl.kernel(mesh=plsc.ScalarSubcoreMesh(axis_name="seq", num_cores=1), name=label, out_type=(),
               scratch_types=(pltpu.SemaphoreType.DMA((n_sems,)), pltpu.SemaphoreType.DMA((n_sems,)),
                              pltpu.SemaphoreType.DMA((max(ns, 1),))),
               compiler_params=pltpu.CompilerParams(collective_id=collective_id))
    def launch(send_sems, recv_sems, local_sems):
        x, y, c = _place()
        chip = 2 * x + y
        sibling = (x, y, 1 - c)
        others = [_chip_of(x, y, r) for r in range(3)]
        near = others[:2]
        far = 2 * others[2][0] + others[2][1]
        _handshake([(qx, qy, c) for qx, qy in others] + [sibling])

        def small_block(a, q):
            cols = small[a].shape[1]
            return small_out[a].at[:, pl.ds(pl.multiple_of(q * cols, 128), cols)]

        local = [pltpu.make_async_copy(small_in[a], small_block(a, chip), local_sems.at[a]) for a in range(ns)]
        for cp in local:
            cp.start()

        sends = []
        for a, name in enumerate(names):
            mine = _shard_half_of(outs[a], name, chip, c)
            for r, (qx, qy) in enumerate(near):
                sends.append(_remote(mine, mine, send_sems, recv_sems, 8 * a + r, (qx, qy, c)))
        for a in range(ns):
            for r, (qx, qy) in enumerate(others):
                sends.append(_remote(small_in[a], small_block(a, chip), send_sems, recv_sems,
                                     8 * n + 3 * a + r, (qx, qy, c)))
        for cp in sends:
            cp.start()

        forwards = []

        def forward(piece, k, dev):
            cp = _remote(piece, piece, send_sems, recv_sems, k, dev)
            cp.start()
            forwards.append(cp)

        for a, name in enumerate(names):
            for r, (qx, qy) in enumerate(near):
                landed = _shard_half_of(outs[a], name, 2 * qx + qy, c)
                _remote(landed, landed, send_sems, recv_sems, 8 * a + r, (qx, qy, c)).wait_recv()
                ox, oy = near[1 - r]
                forward(_shard_half_part_of(outs[a], name, 2 * qx + qy, c, r), 8 * a + 2 + r, (ox, oy, c))
                forward(landed, 8 * a + 4 + r, sibling)
        for a, name in enumerate(names):
            for part in range(2):
                qx, qy = near[1 - part]
                landed = _shard_half_part_of(outs[a], name, far, c, part)
                _remote(landed, landed, send_sems, recv_sems, 8 * a + 2 + part, (qx, qy, c)).wait_recv()
                forward(landed, 8 * a + 6 + part, sibling)
        for a in range(ns):
            for r, (qx, qy) in enumerate(others):
                landed = small_block(a, 2 * qx + qy)
                _remote(landed, landed, send_sems, recv_sems, 8 * n + 3 * a + r, (qx, qy, c)).wait_recv()
        for a, name in enumerate(names):
            for r, (qx, qy) in enumerate(near):
                landed = _shard_half_of(outs[a], name, 2 * qx + qy, 1 - c)
                _remote(landed, landed, send_sems, recv_sems, 8 * a + 4 + r, sibling).wait_recv()
            for part in range(2):
                landed = _shard_half_part_of(outs[a], name, far, 1 - c, part)
                _remote(landed, landed, send_sems, recv_sems, 8 * a + 6 + part, sibling).wait_recv()
        for cp in sends + forwards:
            cp.wait_send()
        for cp in local:
            cp.wait()

    launch()
    return {nm: ref[...] for nm, ref in zip(names, outs)}, [ref[...] for ref in small_out]


def _exchange_halves(grads, names, label, collective_id):
    n = len(names)
    sequencer = collective_id is not None

    def body(*refs):
        ins, outs = refs[:n], refs[n:2 * n]
        send_sems, recv_sems = refs[2 * n:]
        x, y, c = _place()
        if sequencer:
            _handshake([(x, y, 1 - c)])
        copies = []
        for a, name in enumerate(names):
            copies.append(_remote(_half_of(ins[a], BIG[name][2], 1 - c), outs[a], send_sems, recv_sems, a,
                                  (x, y, 1 - c)))
        for cp in copies:
            cp.start()
        for cp in copies:
            cp.wait()

    def half_shape(name):
        r, c_ = _full_shape(name)
        return (HALF, c_) if BIG[name][2] else (r, HALF)

    out_type = tuple(jax.ShapeDtypeStruct(half_shape(nm), F32) for nm in names)
    sems = (pltpu.SemaphoreType.DMA((n,)), pltpu.SemaphoreType.DMA((n,)))
    operands = [grads[nm] for nm in names]
    if sequencer:
        got = pl.kernel(
            body, mesh=plsc.ScalarSubcoreMesh(axis_name="seq", num_cores=1), name=label, out_type=out_type,
            scratch_types=sems, compiler_params=pltpu.CompilerParams(collective_id=collective_id),
        )(*operands)
    else:
        got = pl.pallas_call(
            body, in_specs=[ANY] * n, out_specs=[ANY] * n, out_shape=list(out_type), scratch_shapes=list(sems),
            name=label,
        )(*operands)
    return dict(zip(names, got))


def _chip_sum(grads, got, names, core, label):
    n = len(names)
    steps = 4
    g_specs, blks = [], []
    for name in names:
        rows, cols = got[name].shape
        tr = rows // steps
        if BIG[name][2]:
            g_specs.append(pl.BlockSpec((tr, cols), lambda i, s: (s[0] * steps + i, 0)))
        else:
            g_specs.append(pl.BlockSpec((tr, HALF), lambda i, s: (i, s[0])))
        blks.append(pl.BlockSpec((tr, cols), lambda i, s: (i, 0)))

    def body(s_ref, *refs):
        for a in range(n):
            t = refs[a][...] + refs[n + a][...]
            refs[2 * n + a][...] = t
            refs[3 * n + a][...] = t.astype(BF16)

    out = pl.pallas_call(
        body,
        grid_spec=pltpu.PrefetchScalarGridSpec(num_scalar_prefetch=1, grid=(steps,), in_specs=g_specs + blks,
                                               out_specs=blks + blks),
        out_shape=([jax.ShapeDtypeStruct(got[nm].shape, F32) for nm in names]
                   + [jax.ShapeDtypeStruct(got[nm].shape, BF16) for nm in names]),
        name=label, compiler_params=_params("parallel"),
    )(core, *[grads[nm] for nm in names], *[got[nm] for nm in names])
    return {nm: (out[a], out[n + a]) for a, nm in enumerate(names)}


def _piece_shape(name):
    rows, cols, by_col = BIG[name]
    return (HALF, cols) if by_col else (rows, HALF)


def _handshake(peers):
    barrier = pltpu.get_barrier_semaphore()
    for peer in peers:
        pl.semaphore_signal(barrier, inc=1, device_id=peer, device_id_type=MESH)
    pl.semaphore_wait(barrier, len(peers))


def _send_chip_sums(sums, names, label, collective_id):
    n = len(names)

    def body(*refs):
        ins, outs = refs[:n], refs[n:2 * n]
        send_sems, recv_sems = refs[2 * n:]
        x, y, c = _place()
        others = [_chip_of(x, y, r) for r in range(3)]
        _handshake([(qx, qy, c) for qx, qy in others])
        copies = []
        for a, name in enumerate(names):
            for r, (qx, qy) in enumerate(others):
                copies.append(_remote(_shard_of(ins[a], name, 2 * qx + qy), outs[a].at[r], send_sems, recv_sems,
                                      3 * a + r, (qx, qy, c)))
        for cp in copies:
            cp.start()
        for cp in copies:
            cp.wait()

    return pl.kernel(
        body, mesh=plsc.ScalarSubcoreMesh(axis_name="seq", num_cores=1), name=label,
        out_type=tuple(jax.ShapeDtypeStruct((3,) + _piece_shape(nm), BF16) for nm in names),
        scratch_types=(pltpu.SemaphoreType.DMA((3 * n,)), pltpu.SemaphoreType.DMA((3 * n,))),
        compiler_params=pltpu.CompilerParams(collective_id=collective_id),
    )(*[sums[nm] for nm in names])


def _total(parts, chip_core):
    steps = 2
    in_specs, out_specs, operands = [], [], []
    for name in BIG_NAMES:
        by_col = BIG[name][2]
        pr, pc = _piece_shape(name)
        tr = pr // steps
        if by_col:
            in_specs.append(pl.BlockSpec((tr, pc), lambda i, s: (i, s[0])))
            out_specs.append(pl.BlockSpec((tr, pc), lambda i, s: (s[1] * steps + i, 0)))
        else:
            in_specs.append(pl.BlockSpec((tr, pc), lambda i, s: (s[0] * steps + i, 0)))
            out_specs.append(pl.BlockSpec((tr, pc), lambda i, s: (i, s[1])))
        for r in range(3):
            in_specs.append(pl.BlockSpec((None, tr, pc), lambda i, s, r=r: (r, i, 0)))
        own, got = parts[name]
        operands += [own, got, got, got]

    def body(s_ref, *refs):
        for a in range(N_BIG):
            o_ref, a_ref, b_ref, c_ref = refs[4 * a:4 * a + 4]
            refs[4 * N_BIG + a][...] = (((o_ref[...] + a_ref[...].astype(F32)) + b_ref[...].astype(F32))
                                        + c_ref[...].astype(F32))

    totals = pl.pallas_call(
        body,
        grid_spec=pltpu.PrefetchScalarGridSpec(num_scalar_prefetch=1, grid=(steps,), in_specs=in_specs,
                                               out_specs=out_specs),
        out_shape=[jax.ShapeDtypeStruct(BIG[name][:2], F32) for name in BIG_NAMES],
        name="totals", compiler_params=_params("parallel"),
    )(chip_core, *operands)
    return dict(zip(BIG_NAMES, totals))


VEC_ROWS = 32
VEC_ROW = {"mix_norm_g": 0, "conv_b": 1, "b_rgate": 2, "b_igate": 3, "lru_lambda": 4, "rg_norm_g": 5,
           "hg_lower_bound": 6, "hg_norm_g": 8, "ffn_norm_g": 9, "final_norm_g": 10, "loss": 11,
           "conv_w": 12, "meta_tokens": 16}
N_DEV = 8


def _all_reduce_small(pieces, gates, totals):
    names = list(pieces)
    n_small = 10
    hv, hg = VEC_ROWS // 2, gates.shape[0] // 2

    def body(*refs):
        ins = refs[:len(names)]
        g_ref = refs[len(names)]
        vec_ref, gsum_ref = refs[len(names) + 1 + N_BIG:len(names) + 3 + N_BIG]
        big = refs[len(names) + 3 + N_BIG:len(names) + 3 + 2 * N_BIG]
        (mine_v, sib_v, sib_g, chip_v, chip_g, got_v, got_g, send_sems, recv_sems) = refs[len(names) + 3 + 2 * N_BIG:]
        x, y, c = _place()
        chip = 2 * x + y
        sibling = (x, y, 1 - c)
        share = []
        for a, name in enumerate(BIG_NAMES):
            half = _half_of(big[a], BIG[name][2], c)
            share.append(_remote(half, half, send_sems, recv_sems, n_small + a, sibling))
        mine_v[...] = jnp.zeros_like(mine_v)
        for name, ref in zip(names, ins):
            nr, w = ref.shape
            mine_v[VEC_ROW[name]:VEC_ROW[name] + nr, 0:w] = ref[...]

        swap = [_remote(mine_v, sib_v, send_sems, recv_sems, 0, sibling),
                _remote(g_ref, sib_g, send_sems, recv_sems, 1, sibling)]
        for cp in swap:
            cp.start()
        for cp in swap:
            cp.wait()
        for cp in share:
            cp.start()
        chip_v[...] = mine_v[...] + sib_v[...]
        chip_g[...] = g_ref[...] + sib_g[...]

        rows_v = pl.ds(pl.multiple_of(c * hv, 8), hv)
        rows_g = pl.ds(pl.multiple_of(c * hg, 8), hg)
        got_v[chip] = chip_v[rows_v, :]
        got_g[chip] = chip_g[rows_g, :]
        sends = []
        for r in range(3):
            qx, qy = _chip_of(x, y, r)
            sends.append(_remote(chip_v.at[rows_v, :], got_v.at[chip], send_sems, recv_sems, 2 + r, (qx, qy, c)))
            sends.append(_remote(chip_g.at[rows_g, :], got_g.at[chip], send_sems, recv_sems, 5 + r, (qx, qy, c)))
        for cp in sends:
            cp.start()
        for cp in sends:
            cp.wait()
        vec_ref[rows_v, :] = ((got_v[0] + got_v[1]) + got_v[2]) + got_v[3]
        gsum_ref[rows_g, :] = ((got_g[0] + got_g[1]) + got_g[2]) + got_g[3]

        back = [_remote(vec_ref.at[rows_v, :], vec_ref.at[rows_v, :], send_sems, recv_sems, 8, sibling),
                _remote(gsum_ref.at[rows_g, :], gsum_ref.at[rows_g, :], send_sems, recv_sems, 9, sibling)]
        for cp in back:
            cp.start()
        theirs_v = vec_ref.at[pl.ds(pl.multiple_of((1 - c) * hv, 8), hv), :]
        theirs_g = gsum_ref.at[pl.ds(pl.multiple_of((1 - c) * hg, 8), hg), :]
        _remote(theirs_v, theirs_v, send_sems, recv_sems, 8, sibling).wait_recv()
        _remote(theirs_g, theirs_g, send_sems, recv_sems, 9, sibling).wait_recv()
        for cp in back:
            cp.wait_send()
        for a, name in enumerate(BIG_NAMES):
            theirs = _half_of(big[a], BIG[name][2], 1 - c)
            _remote(theirs, theirs, send_sems, recv_sems, n_small + a, sibling).wait_recv()
        for cp in share:
            cp.wait_send()

    vmem = pl.BlockSpec(memory_space=pltpu.VMEM)
    n_sems = n_small + N_BIG
    out = pl.pallas_call(
        body, in_specs=[vmem] * (len(names) + 1) + [ANY] * N_BIG, out_specs=[vmem, vmem] + [ANY] * N_BIG,
        out_shape=([jax.ShapeDtypeStruct((VEC_ROWS, D_MODEL), F32), jax.ShapeDtypeStruct(gates.shape, F32)]
                   + [jax.ShapeDtypeStruct(BIG[n][:2], F32) for n in BIG_NAMES]),
        input_output_aliases={len(names) + 1 + a: 2 + a for a in range(N_BIG)},
        scratch_shapes=[pltpu.VMEM((VEC_ROWS, D_MODEL), F32), pltpu.VMEM((VEC_ROWS, D_MODEL), F32),
                        pltpu.VMEM(gates.shape, F32), pltpu.VMEM((VEC_ROWS, D_MODEL), F32),
                        pltpu.VMEM(gates.shape, F32), pltpu.VMEM((N_CHIPS, hv, D_MODEL), F32),
                        pltpu.VMEM((N_CHIPS, hg) + gates.shape[1:], F32),
                        pltpu.SemaphoreType.DMA((n_sems,)), pltpu.SemaphoreType.DMA((n_sems,))],
        name="all_reduce_small",
    )(*[pieces[n] for n in names], gates, *[totals[n] for n in BIG_NAMES])
    return out[0], out[1], dict(zip(BIG_NAMES, out[2:]))


def _adamw_math(w, g, m, v):
    m = ADAM_B1 * m + (1.0 - ADAM_B1) * g
    v = ADAM_B2 * v + (1.0 - ADAM_B2) * (g * g)
    m_hat = m / (1.0 - ADAM_B1 ** ADAM_STEP)
    v_hat = v / (1.0 - ADAM_B2 ** ADAM_STEP)
    delta = -ADAM_LR * (m_hat / (jnp.sqrt(v_hat) + ADAM_EPS) + ADAM_WD * w)
    return delta, m, v


def _adamw_big(w, g, m, v):
    steps = 8
    blks = []
    for name in BIG_NAMES:
        rows, cols, _ = BIG[name]
        blks.append(pl.BlockSpec((rows // steps, cols), lambda i: (i, 0)))

    def body(*refs):
        ins, outs = refs[:4 * N_BIG], refs[4 * N_BIG:]
        for a in range(N_BIG):
            w_ref, g_ref, m_ref, v_ref = (ins[k * N_BIG + a] for k in range(4))
            g = g_ref[...]
            d, nm, nv = _adamw_math(w_ref[...], g, m_ref[...], v_ref[...])
            outs[a][...] = g
            outs[N_BIG + a][...] = d
            outs[2 * N_BIG + a][...] = nm
            outs[3 * N_BIG + a][...] = nv

    shapes = [jax.ShapeDtypeStruct(BIG[name][:2], F32) for name in BIG_NAMES]
    out = pl.pallas_call(
        body, grid=(steps,), in_specs=blks * 4, out_specs=blks * 4, out_shape=shapes * 4,
        name="adamw_big", compiler_params=_params("parallel"),
    )(*[t[name] for t in (w, g, m, v) for name in BIG_NAMES])
    return {name: tuple(out[k * N_BIG + a] for k in range(4)) for a, name in enumerate(BIG_NAMES)}


SMALL = {"meta_tokens": (N_META, D_MODEL // N_CHIPS), "mix_norm_g": (1, D_MODEL), "conv_w": (CONV_W, D_RG // N_CHIPS),
         "conv_b": (1, D_RG), "w_rgate": (D_RG, RG_HEAD_DIM), "b_rgate": (1, D_RG), "w_igate": (D_RG, RG_HEAD_DIM),
         "b_igate": (1, D_RG), "lru_lambda": (1, D_RG), "rg_norm_g": (1, D_RG), "hg_lower_bound": (2, D_HG),
         "hg_norm_g": (1, HG_HEAD_DIM), "ffn_norm_g": (1, D_MODEL), "final_norm_g": (1, D_MODEL)}
SMALL_NAMES = tuple(SMALL)
SHARDED_SMALL = ("meta_tokens", "conv_w")


def _adamw_small(vec, gates, w, m, v):
    n = len(SMALL_NAMES)

    def body(*refs):
        vec_ref, gates_ref = refs[:2]
        w_refs, m_refs, v_refs = refs[2:2 + n], refs[2 + n:2 + 2 * n], refs[2 + 2 * n:2 + 3 * n]
        outs = refs[2 + 3 * n:]
        loss_ref = outs[0]
        x, y, _ = _place()
        chip = 2 * x + y
        loss_ref[...] = vec_ref[VEC_ROW["loss"]:VEC_ROW["loss"] + 1, 0:1]

        def update(k, g):
            g_ref, d_ref, nm_ref, nv_ref = outs[1 + 4 * k:5 + 4 * k]
            g_ref[...] = g
            d_ref[...], nm_ref[...], nv_ref[...] = _adamw_math(w_refs[k][...], g, m_refs[k][...], v_refs[k][...])

        for k, name in enumerate(SMALL_NAMES):
            nr, w_ = SMALL[name]
            if name == "w_rgate":
                update(k, gates_ref[0:D_RG, :])
            elif name == "w_igate":
                update(k, gates_ref[D_RG:2 * D_RG, :])
            elif name in SHARDED_SMALL:
                r0 = VEC_ROW[name]
                for q in range(N_CHIPS):
                    @pl.when(chip == q)
                    def _(k=k, r0=r0, nr=nr, w_=w_, q=q):
                        update(k, vec_ref[r0:r0 + nr, q * w_:(q + 1) * w_])
            else:
                r0 = VEC_ROW[name]
                update(k, vec_ref[r0:r0 + nr, 0:w_])

    vmem = pl.BlockSpec(memory_space=pltpu.VMEM)
    out_shape = [jax.ShapeDtypeStruct((1, 1), F32)]
    for name in SMALL_NAMES:
        out_shape += [jax.ShapeDtypeStruct(SMALL[name], F32)] * 4
    outs = pl.pallas_call(
        body, in_specs=[vmem] * (2 + 3 * n), out_specs=[vmem] * len(out_shape), out_shape=out_shape,
        name="adamw_small",
    )(vec, gates, *[w[k] for k in SMALL_NAMES], *[m[k] for k in SMALL_NAMES], *[v[k] for k in SMALL_NAMES])
    loss = outs[0]
    res = {name: tuple(outs[1 + 4 * k:5 + 4 * k]) for k, name in enumerate(SMALL_NAMES)}
    return loss, res


WEIGHT_NAMES = ("meta_tokens", "mix_norm_g", "w_in", "conv_w", "conv_b", "w_rgate", "b_rgate", "w_igate", "b_igate",
                "lru_lambda", "rg_norm_g", "hg_lower_bound", "hg_norm_g", "w_out", "ffn_norm_g", "w_gate_up", "w_down",
                "final_norm_g")


def _to_2d(name, a):
    if name in BIG:
        return a.reshape(BIG[name][:2])
    return a.reshape(SMALL[name])


def kernel(x, meta_tokens, mix_norm_g, w_in, conv_w, conv_b, w_rgate, b_rgate, w_igate, b_igate, lru_lambda, rg_norm_g, hg_lower_bound, hg_norm_g, w_out, ffn_norm_g, w_gate_up, w_down, final_norm_g, loss_target, m_meta_tokens, m_mix_norm_g, m_w_in, m_conv_w, m_conv_b, m_w_rgate, m_b_rgate, m_w_igate, m_b_igate, m_lru_lambda, m_rg_norm_g, m_hg_lower_bound, m_hg_norm_g, m_w_out, m_ffn_norm_g, m_w_gate_up, m_w_down, m_final_norm_g, v_meta_tokens, v_mix_norm_g, v_w_in, v_conv_w, v_conv_b, v_w_rgate, v_b_rgate, v_w_igate, v_b_igate, v_lru_lambda, v_rg_norm_g, v_hg_lower_bound, v_hg_norm_g, v_w_out, v_ffn_norm_g, v_w_gate_up, v_w_down, v_final_norm_g):
    w_raw = dict(zip(WEIGHT_NAMES, (meta_tokens, mix_norm_g, w_in, conv_w, conv_b, w_rgate, b_rgate, w_igate, b_igate,
                                    lru_lambda, rg_norm_g, hg_lower_bound, hg_norm_g, w_out, ffn_norm_g, w_gate_up,
                                    w_down, final_norm_g)))
    m_raw = dict(zip(WEIGHT_NAMES, (m_meta_tokens, m_mix_norm_g, m_w_in, m_conv_w, m_conv_b, m_w_rgate, m_b_rgate,
                                    m_w_igate, m_b_igate, m_lru_lambda, m_rg_norm_g, m_hg_lower_bound, m_hg_norm_g,
                                    m_w_out, m_ffn_norm_g, m_w_gate_up, m_w_down, m_final_norm_g)))
    v_raw = dict(zip(WEIGHT_NAMES, (v_meta_tokens, v_mix_norm_g, v_w_in, v_conv_w, v_conv_b, v_w_rgate, v_b_rgate,
                                    v_w_igate, v_b_igate, v_lru_lambda, v_rg_norm_g, v_hg_lower_bound, v_hg_norm_g,
                                    v_w_out, v_ffn_norm_g, v_w_gate_up, v_w_down, v_final_norm_g)))
    w = {k: _to_2d(k, a) for k, a in w_raw.items()}
    m = {k: _to_2d(k, a) for k, a in m_raw.items()}
    v = {k: _to_2d(k, a) for k, a in v_raw.items()}

    x_i, y_i, c_i = _place()
    core = jnp.reshape(c_i, (1,)).astype(jnp.int32)
    chip = jnp.reshape(2 * x_i + y_i, (1,)).astype(jnp.int32)
    chip_core = jnp.concatenate([chip, core])

    first_names, rest_names = ("w_in",), ("w_out", "w_gate_up", "w_down")
    placed, _ = _place_shards(w, [], chip, first_names, "place_first")
    first, _ = _gather_weights(placed, [], first_names, "gather_first", 1)
    placed, (meta_full, cw_full) = _place_shards(w, [w["meta_tokens"], w["conv_w"]], chip, rest_names, "place_shards")
    rest, _ = _gather_weights(placed, [], rest_names, "gather_rest", 2)
    full = {**first, **rest}

    seq = x.shape[1]
    small ={k: w[k] for k in SMALL_NAMES if k not in SHARDED_SMALL}
    small["conv_w"] = cw_full

    def reduce_to_chips(grads, names, tag, collective_ids):
        got = _exchange_halves(grads, names, "exchange_halves_" + tag, collective_ids[0])

        def chip_sums():
            return _chip_sum(grads, got, names, core, "chip_sum_" + tag)

        def send(sums):
            arrived = _send_chip_sums({n: sums[n][1] for n in names}, names, "send_chip_sums_" + tag,
                                      collective_ids[1])
            return {n: (sums[n][0], a) for n, a in zip(names, arrived)}

        return chip_sums, send

    ffn_names, mixer_names = ("w_gate_up", "w_down", "w_out"), ("w_in",)
    loss, grad_x, grads, parts, parts_mixer = _local_step(
        x.reshape(seq, D_MODEL), meta_full, loss_target.reshape(seq, D_MODEL),
        w["w_in"], full["w_in"], full["w_out"], full["w_gate_up"], full["w_down"], small, chip,
        on_ffn_grads=lambda g: reduce_to_chips(g, ffn_names, "ffn", (3, 4)),
        on_mixer_grads=lambda g: reduce_to_chips(g, mixer_names, "mixer", (None, 5)))
    parts.update(parts_mixer)
    totals = _total(parts, chip_core)
    pieces = {k: grads[k] for k in VEC_ROW if k != "loss"}
    pieces["loss"] = loss
    vec, gates, g_big = _all_reduce_small(pieces, grads["w_gates"], totals)
    loss_sum, res = _adamw_small(vec, gates, w, m, v)
    res.update(_adamw_big(w, g_big, m, v))

    out = [loss_sum.reshape(()), grad_x.reshape(1, seq, D_MODEL)]
    for j in range(4):
        out += [res[n][j].reshape(w_raw[n].shape) for n in WEIGHT_NAMES]
    return tuple(out)
```

```python
import math

import jax
import jax.numpy as jnp
from jax import lax
from jax.experimental import pallas as pl
from jax.experimental.pallas import tpu as pltpu
from jax.experimental.pallas import tpu_sc as plsc

F32 = jnp.float32
BF16 = jnp.bfloat16
MESH = pl.DeviceIdType.MESH

D_MODEL = 1024
D_RG = 512
RG_HEAD_DIM = 64
D_HG = 512
HG_HEAD_DIM = 128
HG_HEADS = 4
CHUNK = 64
SUB = 16
N_SUB = CHUNK // SUB
N_META = 16
PAD = CHUNK - N_META
D_IN = 3072
D_FF = 2816
CONV_W = 4
LRU_C = 8.0
EPS = 1e-6
EXP_CLAMP = 80.0
GELU_C = math.sqrt(2.0 / math.pi)
GELU_A = 0.044715
N_CHIPS = 4

ADAM_LR = 0.001
ADAM_B1 = 0.9
ADAM_B2 = 0.999
ADAM_EPS = 1e-08
ADAM_WD = 0.01
ADAM_STEP = 10

VMEM_LIMIT = 56 * 1024 * 1024


def _params(*sem):
    return pltpu.CompilerParams(dimension_semantics=sem, vmem_limit_bytes=VMEM_LIMIT)


def _row_tile(rows, target):
    best = None
    for t in range(16, min(rows, target) + 1, 16):
        if rows % t == 0:
            best = t
    assert best is not None, rows
    return best


def _sigmoid(x):
    return 0.5 * jnp.tanh(0.5 * x) + 0.5


def _dot(a, b):
    return jnp.dot(a, b, preferred_element_type=F32)


def _dot_nt(a, b):
    return lax.dot_general(a, b, (((1,), (1,)), ((), ())), preferred_element_type=F32)


def _dot_tn(a, b):
    return lax.dot_general(a, b, (((0,), (0,)), ((), ())), preferred_element_type=F32)


def _rms(x):
    return lax.rsqrt(jnp.mean(x * x, axis=-1, keepdims=True) + EPS)


def _rms_bwd(dn, n, r):
    return r * (dn - n * jnp.mean(dn * n, axis=-1, keepdims=True))


def _gelu_parts(x):
    t = jnp.tanh(GELU_C * (x + GELU_A * x * x * x))
    g = 0.5 * x * (1.0 + t)
    dg = 0.5 * (1.0 + t) + 0.5 * x * (1.0 - t * t) * GELU_C * (1.0 + 3.0 * GELU_A * x * x)
    return g, dg


def _softplus_neg(lam):
    e = jnp.exp(-jnp.abs(lam))
    w = 1.0 + e
    log1p = jnp.where(w == 1.0, e, jnp.log(w) * e / (w - 1.0))
    return jnp.maximum(-lam, 0.0) + log1p


def _head_mask():
    r = lax.broadcasted_iota(jnp.int32, (D_RG, D_RG), 0) // RG_HEAD_DIM
    c = lax.broadcasted_iota(jnp.int32, (D_RG, D_RG), 1) // RG_HEAD_DIM
    return r == c


def _head_fold():
    r = lax.broadcasted_iota(jnp.int32, (D_RG, RG_HEAD_DIM), 0) % RG_HEAD_DIM
    c = lax.broadcasted_iota(jnp.int32, (D_RG, RG_HEAD_DIM), 1)
    return (r == c).astype(F32)


def _gate_weights(w_r, w_i):
    def body(wr_ref, wi_ref, o_ref):
        fold = _head_fold()
        mask = _head_mask()
        for k, ref in enumerate((wr_ref, wi_ref)):
            full = _dot_nt(ref[...].astype(BF16), fold.astype(BF16))
            o_ref[:, k * D_RG:(k + 1) * D_RG] = jnp.where(mask, full, 0.0).astype(BF16)

    return pl.pallas_call(
        body, out_shape=jax.ShapeDtypeStruct((D_RG, 2 * D_RG), BF16), name="gate_weights",
    )(w_r, w_i)


HEAD = PAD + N_META


def _window_copies(seq_hbm, buf, sems, tm):
    def first(to_vmem):
        seq, vm = seq_hbm.at[pl.ds(0, tm - HEAD)], buf.at[0, pl.ds(HEAD, tm - HEAD)]
        return pltpu.make_async_copy(seq, vm, sems.at[0]) if to_vmem else pltpu.make_async_copy(vm, seq, sems.at[0])

    def later(j, slot, to_vmem):
        seq, vm = seq_hbm.at[pl.ds(pl.multiple_of(j * tm - HEAD, 8), tm)], buf.at[slot]
        if to_vmem:
            return pltpu.make_async_copy(seq, vm, sems.at[slot])
        return pltpu.make_async_copy(vm, seq, sems.at[slot])

    return first, later


def _fetch_window(seq_hbm, buf, sems, i, n_steps, tm):
    first, later = _window_copies(seq_hbm, buf, sems, tm)
    slot = i % 2

    @pl.when(i == 0)
    def _():
        first(True).start()

    if n_steps > 1:
        @pl.when(i + 1 < n_steps)
        def _():
            later(i + 1, 1 - slot, True).start()

    @pl.when(i == 0)
    def _():
        first(True).wait()

    if n_steps > 1:
        @pl.when(i > 0)
        def _():
            later(i, slot, True).wait()

    return slot


def _in_proj_local(x, meta, g1, w_own, chip):
    T = x.shape[0] + HEAD
    tm = _row_tile(T, 832)
    n_steps = T // tm
    cols = BIG["w_in"][1]

    def body(s_ref, x_hbm, meta_ref, g_ref, w_ref, p_ref, u_ref, h_ref, buf, sems, wb):
        i = pl.program_id(0)
        slot = _fetch_window(x_hbm, buf, sems, i, n_steps, tm)

        @pl.when(i == 0)
        def _():
            buf[0, 0:PAD, :] = jnp.zeros((PAD, D_MODEL), F32)
            buf[0, PAD:HEAD, :] = meta_ref[...]
            wb[...] = w_ref[...].astype(BF16)

        h = buf[slot]
        h_ref[...] = h
        u = (h * _rms(h) * g_ref[...]).astype(BF16)
        u_ref[...] = u
        p_ref[...] = _dot(u, wb[...])

    return pl.pallas_call(
        body,
        grid_spec=pltpu.PrefetchScalarGridSpec(
            num_scalar_prefetch=1, grid=(n_steps,),
            in_specs=[pl.BlockSpec(memory_space=pl.ANY),
                      pl.BlockSpec((N_META, D_MODEL), lambda i, s: (0, 0)),
                      pl.BlockSpec((1, D_MODEL), lambda i, s: (0, 0)),
                      pl.BlockSpec((D_MODEL, cols), lambda i, s: (0, 0))],
            out_specs=[pl.BlockSpec((tm, cols), lambda i, s: (i, s[0])),
                       pl.BlockSpec((tm, D_MODEL), lambda i, s: (i, 0)),
                       pl.BlockSpec((tm, D_MODEL), lambda i, s: (i, 0))],
            scratch_shapes=[pltpu.VMEM((2, tm, D_MODEL), F32), pltpu.SemaphoreType.DMA((2,)),
                            pltpu.VMEM((D_MODEL, cols), BF16)]),
        out_shape=[jax.ShapeDtypeStruct((T, D_IN), F32), jax.ShapeDtypeStruct((T, D_MODEL), BF16),
                   jax.ShapeDtypeStruct((T, D_MODEL), F32)],
        name="in_proj_local", compiler_params=_params("arbitrary"),
    )(chip, x, meta, g1, w_own)


def _in_proj_rest(u, w_in, p, chip):
    T = u.shape[0]
    tm = _row_tile(T, 2080)
    cols = BIG["w_in"][1]
    block = lambda j, s: (s[0] + 1 + j) % N_CHIPS

    def body(s_ref, u_ref, w_ref, p_in_ref, p_ref):
        p_ref[...] = _dot(u_ref[...], w_ref[...])

    return pl.pallas_call(
        body,
        grid_spec=pltpu.PrefetchScalarGridSpec(
            num_scalar_prefetch=1, grid=(N_CHIPS - 1, T // tm),
            in_specs=[pl.BlockSpec((tm, D_MODEL), lambda j, i, s: (i, 0)),
                      pl.BlockSpec((D_MODEL, cols), lambda j, i, s: (0, block(j, s))), ANY],
            out_specs=pl.BlockSpec((tm, cols), lambda j, i, s: (i, block(j, s)))),
        out_shape=jax.ShapeDtypeStruct((T, D_IN), F32),
        input_output_aliases={3: 0},
        name="in_proj_rest", compiler_params=_params("arbitrary", "arbitrary"),
    )(chip, u, w_in, p)


def _scan_block_fwd(A, B, rowi):
    for d in (1, 2, 4):
        a_sh = pltpu.roll(A, d, axis=0)
        b_sh = pltpu.roll(B, d, axis=0)
        m = rowi >= d
        B = jnp.where(m, A * b_sh + B, B)
        A = jnp.where(m, A * a_sh, A)
    return A, B


def _scan_block_bwd(A, B, rowi):
    for d in (1, 2, 4):
        a_sh = pltpu.roll(A, 8 - d, axis=0)
        b_sh = pltpu.roll(B, 8 - d, axis=0)
        m = rowi < 8 - d
        B = jnp.where(m, A * b_sh + B, B)
        A = jnp.where(m, A * a_sh, A)
    return A, B


def _rg_gates(xc, w_ref, bg_ref, lam):
    pre = _dot(xc.astype(BF16), w_ref[...]) + bg_ref[...]
    r = _sigmoid(pre[:, :D_RG])
    ig = _sigmoid(pre[:, D_RG:])
    sp = _softplus_neg(lam)
    la = -LRU_C * sp * r
    a = jnp.exp(la)
    th = jnp.tanh(la)
    u = 1.0 - th
    rc = pl.reciprocal(u, approx=True)
    rc = rc * (2.0 - u * rc)
    rc = rc * (2.0 - u * rc)
    m2 = -2.0 * th * rc
    inv_m = lax.rsqrt(jnp.maximum(m2, 1e-30))
    return r, ig, sp, a, m2 * inv_m, inv_m


def _conv(ext, cw_ref, cb_ref, tm):
    xc = cb_ref[...] + cw_ref[0:1, :] * ext[8 - 3:8 - 3 + tm, :]
    for j in range(1, CONV_W):
        xc = xc + cw_ref[j:j + 1, :] * ext[8 - 3 + j:8 - 3 + j + tm, :]
    return xc


def _scan_unroll(blocks):
    return 4 if blocks % 4 == 0 else 2 if blocks % 2 == 0 else 1


def _rg_fwd(p, cw, cb, wg, bg, lam, rg_g):
    T = p.shape[0]
    tm = _row_tile(T, 832)
    unroll = _scan_unroll(tm // 8)

    def body(xg_ref, cw_ref, cb_ref, w_ref, bg_ref, lam_ref, g_ref, y_ref, h_ref, xc_ref, ext, a_s, b_s, carry):
        i = pl.program_id(0)

        @pl.when(i == 0)
        def _():
            ext[0:8, :] = jnp.zeros((8, D_RG), F32)
            carry[...] = jnp.zeros((1, D_RG), F32)

        ext[8:8 + tm, :] = xg_ref[:, :D_RG]
        xc = _conv(ext, cw_ref, cb_ref, tm)
        xc_ref[...] = xc
        r, ig, sp, a, m, _ = _rg_gates(xc, w_ref, bg_ref, lam_ref[...])
        row = i * tm + lax.broadcasted_iota(jnp.int32, (tm, 1), 0)
        a_s[...] = a
        b_s[...] = jnp.where(row >= PAD, m * ig * xc, 0.0)
        rowi = lax.broadcasted_iota(jnp.int32, (8, D_RG), 0)

        def blk(j, c):
            for u in range(unroll):
                o = pl.multiple_of((j * unroll + u) * 8, 8)
                A, B = _scan_block_fwd(a_s[pl.ds(o, 8), :], b_s[pl.ds(o, 8), :], rowi)
                h = B + A * c
                h_ref[pl.ds(o, 8), :] = h
                c = h[7:8, :]
            return c

        carry[...] = lax.fori_loop(0, tm // (8 * unroll), blk, carry[...])
        ext[0:8, :] = ext[tm:tm + 8, :]
        g, _ = _gelu_parts(xg_ref[:, D_RG:])
        yy = g * h_ref[...]
        y_ref[...] = (yy * _rms(yy) * g_ref[...]).astype(BF16)

    vec = lambda n: pl.BlockSpec((1, n), lambda i: (0, 0))
    return pl.pallas_call(
        body, grid=(T // tm,),
        in_specs=[pl.BlockSpec((tm, 2 * D_RG), lambda i: (i, 0)),
                  pl.BlockSpec((CONV_W, D_RG), lambda i: (0, 0)), vec(D_RG),
                  pl.BlockSpec((D_RG, 2 * D_RG), lambda i: (0, 0)), vec(2 * D_RG), vec(D_RG), vec(D_RG)],
        out_specs=[pl.BlockSpec((tm, D_RG), lambda i: (i, 0))] * 3,
        out_shape=[jax.ShapeDtypeStruct((T, D_RG), BF16), jax.ShapeDtypeStruct((T, D_RG), F32),
                   jax.ShapeDtypeStruct((T, D_RG), F32)],
        scratch_shapes=[pltpu.VMEM((tm + 8, D_RG), F32), pltpu.VMEM((tm, D_RG), F32),
                        pltpu.VMEM((tm, D_RG), F32), pltpu.VMEM((1, D_RG), F32)],
        name="rg_fwd", compiler_params=_params("arbitrary"),
    )(p, cw, cb, wg, bg, lam, rg_g)


def _running_sum(x, down):
    r = lax.broadcasted_iota(jnp.int32, (CHUNK, CHUNK), 0)
    c = lax.broadcasted_iota(jnp.int32, (CHUNK, CHUNK), 1)
    tri = ((c <= r) if down else (c >= r)).astype(BF16)
    hi = x.astype(BF16)
    rest = x - hi.astype(F32)
    mid = rest.astype(BF16)
    lo = (rest - mid.astype(F32)).astype(BF16)
    return (_dot(tri, hi) + _dot(tri, mid)) + _dot(tri, lo)


def _hg_gates(hq, hf, lbraw_ref, valid):
    lb = _sigmoid(lbraw_ref[0:1, :] - lbraw_ref[1:2, :])
    sq = _sigmoid(hq)
    q = hq * sq
    sf = _sigmoid(hf)
    f = lb + (1.0 - lb) * sf
    lf = jnp.where(valid, jnp.log(f), 0.0)
    b = _running_sum(lf, True)
    return lb, sq, q, sf, f, b


def _hg_head(qh, kh, bh):
    b_last = bh[CHUNK - 1:CHUNK, :]
    refs = [bh[SUB * s:SUB * s + 1, :] for s in range(N_SUB)]
    r_sel = jnp.concatenate([jnp.broadcast_to(refs[s], (SUB, HG_HEAD_DIM)) for s in range(N_SUB)], axis=0)
    eb = jnp.exp(bh)
    eq = jnp.exp(bh - r_sel)
    ekh = jnp.exp(b_last - bh)
    ek = [jnp.exp(jnp.minimum(refs[s] - bh[:SUB * (s + 1), :], EXP_CLAMP)) for s in range(N_SUB)]
    qe = qh * eq

    def own_rows(s):
        parts = [jnp.zeros((SUB * s, HG_HEAD_DIM), F32)] if s else []
        parts.append(qe[SUB * s:SUB * (s + 1), :])
        if s < N_SUB - 1:
            parts.append(jnp.zeros((CHUNK - SUB * (s + 1), HG_HEAD_DIM), F32))
        return jnp.concatenate(parts, axis=0)

    q_hat = jnp.concatenate([own_rows(s) for s in range(N_SUB)], axis=1)

    def met_rows(s):
        n = SUB * (s + 1)
        ke = kh[:n, :] * ek[s]
        return ke if n == CHUNK else jnp.concatenate([ke, jnp.zeros((CHUNK - n, HG_HEAD_DIM), F32)], axis=0)

    k_til = jnp.concatenate([met_rows(s) for s in range(N_SUB)], axis=1)
    return b_last, eb, eq, ekh, ek, q_hat, k_til


def _causal():
    r = lax.broadcasted_iota(jnp.int32, (CHUNK, CHUNK), 0)
    c = lax.broadcasted_iota(jnp.int32, (CHUNK, CHUNK), 1)
    return r >= c


def _chunks_per_step(n_chunks):
    for c in (5, 4, 3, 2):
        if n_chunks % c == 0:
            return c
    return 1


def _hg_fwd(p, lbraw, hg_g):
    T = p.shape[0]
    n_chunks = T // CHUNK
    cps = _chunks_per_step(n_chunks)
    rows = cps * CHUNK

    def body(hq_ref, hf_ref, hi_ref, hg_ref, lb_ref, g_ref, y_ref, o_ref, st_all_ref, st):
        i = pl.program_id(0)

        @pl.when(i == 0)
        def _():
            st[...] = jnp.zeros_like(st)

        def chunk(j, carry):
            rs = pl.ds(pl.multiple_of(j * CHUNK, CHUNK), CHUNK)
            chunk_body(i * cps + j, hq_ref.at[rs, :], hf_ref.at[rs, :], hi_ref.at[rs, :], hg_ref.at[rs, :], lb_ref,
                       g_ref, y_ref.at[rs, :], o_ref.at[rs, :], st_all_ref.at[pl.ds(j, 1)], st)
            return carry

        lax.fori_loop(0, cps, chunk, 0, unroll=True)

    def chunk_body(n, hq_ref, hf_ref, hi_ref, hg_ref, lb_ref, g_ref, y_ref, o_ref, st_all_ref, st):
        valid = (n * CHUNK + lax.broadcasted_iota(jnp.int32, (CHUNK, 1), 0)) >= PAD
        hq, hf, v, hg = hq_ref[...], hf_ref[...], hi_ref[...], hg_ref[...]
        lb, sq, q, sf, f, b = _hg_gates(hq, hf, lb_ref, valid)
        k = 1.0 - f
        st_all_ref[0] = st[...]
        causal = _causal()
        v_t = v.T.astype(BF16)
        heads = [slice(h * HG_HEAD_DIM, (h + 1) * HG_HEAD_DIM) for h in range(HG_HEADS)]
        fac = []
        for sl in heads:
            qh, kh, bh = q[:, sl], k[:, sl], b[:, sl]
            b_last, eb, _, ekh, _, q_hat, k_til = _hg_head(qh, kh, bh)
            fac.append((jnp.exp(b_last), (qh * eb).astype(BF16), q_hat.astype(BF16), k_til.astype(BF16),
                        (kh * ekh).astype(BF16), v[:, sl].astype(BF16)))
        raw = []
        for sl, (_, q_til, q_hat, k_til, k_hat, _) in zip(heads, fac):
            st_h = st[sl, :]
            raw.append((_dot_nt(q_til, st_h.astype(BF16)), _dot_nt(q_hat, k_til), _dot(v_t[sl, :], k_hat), st_h))
        for sl, (e_last, _, _, _, _, vb), (inter, att, upd, st_h) in zip(heads, fac, raw):
            o = inter + _dot(jnp.where(causal, att, 0.0).astype(BF16), vb)
            st[sl, :] = st_h * e_last + upd
            o_ref[:, sl] = o
            hgh = hg[:, sl]
            y_ref[:, sl] = (o * _rms(o) * g_ref[...] * (hgh * _sigmoid(hgh))).astype(BF16)

    col = lambda j: pl.BlockSpec((rows, D_HG), lambda n: (n, j))
    return pl.pallas_call(
        body, grid=(n_chunks // cps,),
        in_specs=[col(2), col(3), col(4), col(5),
                  pl.BlockSpec((2, D_HG), lambda n: (0, 0)), pl.BlockSpec((1, HG_HEAD_DIM), lambda n: (0, 0))],
        out_specs=[pl.BlockSpec((rows, D_HG), lambda n: (n, 0)), pl.BlockSpec((rows, D_HG), lambda n: (n, 0)),
                   pl.BlockSpec((cps, D_HG, HG_HEAD_DIM), lambda n: (n, 0, 0))],
        out_shape=[jax.ShapeDtypeStruct((T, D_HG), BF16), jax.ShapeDtypeStruct((T, D_HG), F32),
                   jax.ShapeDtypeStruct((n_chunks, D_HG, HG_HEAD_DIM), F32)],
        scratch_shapes=[pltpu.VMEM((D_HG, HG_HEAD_DIM), F32)],
        name="hg_fwd", compiler_params=_params("arbitrary"),
    )(p, p, p, p, lbraw, hg_g)


def _ffn_fwd(h0, y_rg, y_hg, w_out, g2, w_gu, w_down, gf, target):
    T = h0.shape[0]
    tm = _row_tile(T, 320)
    n_steps = T // tm

    def body(h_ref, yr_ref, yh_ref, wo_ref, g2_ref, wgu_ref, wd_ref, gf_ref, t_hbm,
             h1_ref, v_ref, y_ref, gu_ref, act_ref, dh2_ref, dh2b_ref, loss_ref, gg_ref, tbuf, sems):
        i = pl.program_id(0)
        slot = _fetch_window(t_hbm, tbuf, sems, i, n_steps, tm)

        @pl.when(i == 0)
        def _():
            loss_ref[...] = jnp.zeros_like(loss_ref)
            gg_ref[...] = jnp.zeros_like(gg_ref)
            tbuf[0, 0:HEAD, :] = jnp.zeros((HEAD, D_MODEL), F32)

        y_ref[:, :D_RG] = yr_ref[...]
        y_ref[:, D_RG:] = yh_ref[...]
        h1 = h_ref[...] + _dot(y_ref[...], wo_ref[...])
        h1_ref[...] = h1
        v = (h1 * _rms(h1) * g2_ref[...]).astype(BF16)
        v_ref[...] = v

        gu = _dot(v, wgu_ref[...])
        gu_ref[...] = gu.astype(BF16)
        g = gu[:, :D_FF]
        act = (g * _sigmoid(g) * gu[:, D_FF:]).astype(BF16)
        act_ref[...] = act

        h2 = h1 + _dot(act, wd_ref[...])
        r = _rms(h2)
        n = h2 * r
        gf_ = gf_ref[...]
        row = i * tm + lax.broadcasted_iota(jnp.int32, (tm, 1), 0)
        err = jnp.where(row >= HEAD, n * gf_ - tbuf[slot], 0.0)
        loss_ref[...] += 0.5 * jnp.sum(jnp.mean(err * err, axis=-1, keepdims=True), axis=0, keepdims=True)
        dy = err * (1.0 / D_MODEL)
        gg_ref[...] += jnp.sum(dy * n, axis=0, keepdims=True)
        dh2 = _rms_bwd(dy * gf_, n, r)
        dh2_ref[...] = dh2
        dh2b_ref[...] = dh2.astype(BF16)

    row_spec = lambda n: pl.BlockSpec((tm, n), lambda i: (i, 0))
    vec = pl.BlockSpec((1, D_MODEL), lambda i: (0, 0))
    return pl.pallas_call(
        body, grid=(n_steps,),
        in_specs=[row_spec(D_MODEL), row_spec(D_RG), row_spec(D_HG), _resident((D_MODEL, D_MODEL)), vec,
                  _resident((D_MODEL, 2 * D_FF)), _resident((D_FF, D_MODEL)), vec,
                  pl.BlockSpec(memory_space=pl.ANY)],
        out_specs=[row_spec(D_MODEL), row_spec(D_MODEL), row_spec(D_MODEL), row_spec(2 * D_FF), row_spec(D_FF),
                   row_spec(D_MODEL), row_spec(D_MODEL), pl.BlockSpec((1, 1), lambda i: (0, 0)), vec],
        out_shape=[jax.ShapeDtypeStruct((T, D_MODEL), F32), jax.ShapeDtypeStruct((T, D_MODEL), BF16),
                   jax.ShapeDtypeStruct((T, D_MODEL), BF16), jax.ShapeDtypeStruct((T, 2 * D_FF), BF16),
                   jax.ShapeDtypeStruct((T, D_FF), BF16), jax.ShapeDtypeStruct((T, D_MODEL), F32),
                   jax.ShapeDtypeStruct((T, D_MODEL), BF16), jax.ShapeDtypeStruct((1, 1), F32),
                   jax.ShapeDtypeStruct((1, D_MODEL), F32)],
        scratch_shapes=[pltpu.VMEM((2, tm, D_MODEL), F32), pltpu.SemaphoreType.DMA((2,))],
        name="ffn_fwd", compiler_params=_params("arbitrary"),
    )(h0, y_rg, y_hg, w_out, g2, w_gu, w_down, gf, target)


def _resident(shape):
    return pl.BlockSpec(shape, lambda i: (0,) * len(shape), pipeline_mode=pl.Buffered(1))


def _ffn_bwd(dh2b, gu, w_down, w_gu, h1, g2, dh2, w_out):
    T = h1.shape[0]
    tm = _row_tile(T, 320)

    def body(d_ref, gu_ref, wd_ref, wgu_ref, h_ref, g_ref, d2_ref, wo_ref, dgu_ref, dh1_ref, dh1b_ref, dy_ref, gg_ref):
        i = pl.program_id(0)

        @pl.when(i == 0)
        def _():
            gg_ref[...] = jnp.zeros_like(gg_ref)

        dact = _dot_nt(d_ref[...], wd_ref[...]).astype(BF16)
        g = gu_ref[:, :D_FF]
        u = gu_ref[:, D_FF:]
        s = _sigmoid(g)
        dgu_ref[:, :D_FF] = dact * u * (s * (1.0 + g * (1.0 - s)))
        dgu_ref[:, D_FF:] = dact * (g * s)

        dv = _dot_nt(dgu_ref[...], wgu_ref[...])
        h1_ = h_ref[...]
        r = _rms(h1_)
        n = h1_ * r
        gg_ref[...] += jnp.sum(dv * n, axis=0, keepdims=True)
        dh1 = d2_ref[...] + _rms_bwd(dv * g_ref[...], n, r)
        dh1_ref[...] = dh1
        db = dh1.astype(BF16)
        dh1b_ref[...] = db
        dy_ref[...] = _dot_nt(db, wo_ref[...])

    row = lambda n: pl.BlockSpec((tm, n), lambda i: (i, 0))
    return pl.pallas_call(
        body, grid=(T // tm,),
        in_specs=[row(D_MODEL), row(2 * D_FF), _resident((D_FF, D_MODEL)), _resident((D_MODEL, 2 * D_FF)),
                  row(D_MODEL), pl.BlockSpec((1, D_MODEL), lambda i: (0, 0)), row(D_MODEL),
                  _resident((D_MODEL, D_MODEL))],
        out_specs=[row(2 * D_FF), row(D_MODEL), row(D_MODEL), row(D_MODEL),
                   pl.BlockSpec((1, D_MODEL), lambda i: (0, 0))],
        out_shape=[jax.ShapeDtypeStruct((T, 2 * D_FF), BF16), jax.ShapeDtypeStruct((T, D_MODEL), F32),
                   jax.ShapeDtypeStruct((T, D_MODEL), BF16), jax.ShapeDtypeStruct((T, D_MODEL), F32),
                   jax.ShapeDtypeStruct((1, D_MODEL), F32)],
        name="ffn_bwd", compiler_params=_params("arbitrary"),
    )(dh2b, gu, w_down, w_gu, h1, g2, dh2, w_out)


def _rg_bwd(p, xc_all, hs, dy, dp, cw, cb, wg, bg, lam, rg_g):
    T = p.shape[0]
    tm = _row_tile(T, 832)
    nt = T // tm
    hb = tm // 8
    unroll = _scan_unroll(hb)

    def body(xg_ref, xc_ref, h_ref, hh_ref, dy_ref, dp_in_ref, cw_ref, cb_ref, w_ref, bg_ref, lam_ref, g_ref,
             dp_ref, gcw_ref, gcb_ref, gw_ref, gbg_ref, glam_ref, gg_ref,
             dext, a_s, b_s, d_s, gacc, carry_d, carry_a):
        i = pl.program_id(0)
        t_idx = nt - 1 - i

        @pl.when(i == 0)
        def _():
            dext[tm:tm + 8, :] = jnp.zeros((8, D_RG), F32)
            carry_d[...] = jnp.zeros_like(carry_d)
            carry_a[...] = jnp.zeros_like(carry_a)
            gacc[...] = jnp.zeros_like(gacc)
            for ref in (gcw_ref, gcb_ref, gbg_ref, glam_ref, gg_ref, gw_ref):
                ref[...] = jnp.zeros_like(ref)

        first = t_idx == 0
        xc = xc_ref[...]
        lam_ = lam_ref[...]
        r, ig, sp, a, m, inv_m = _rg_gates(xc, w_ref, bg_ref, lam_)
        row = t_idx * tm + lax.broadcasted_iota(jnp.int32, (tm, 1), 0)
        valid = row >= PAD

        gr = xg_ref[:, D_RG:]
        g, dgelu = _gelu_parts(gr)
        h = h_ref[...]
        yy = g * h
        rr = _rms(yy)
        nn = yy * rr
        dy_ = dy_ref[...]
        gg_ref[...] += jnp.sum(dy_ * nn, axis=0, keepdims=True)
        dyy = _rms_bwd(dy_ * g_ref[...], nn, rr)
        dp_ref[:, D_RG:] = (dyy * h * dgelu).astype(BF16)

        a_s[...] = a
        b_s[...] = dyy * g
        rowi = lax.broadcasted_iota(jnp.int32, (8, D_RG), 0)

        def blk(jj, c):
            cd, ca = c
            for u in range(unroll):
                o = pl.multiple_of((hb - 1 - (jj * unroll + u)) * 8, 8)
                a_blk = a_s[pl.ds(o, 8), :]
                a_next = jnp.where(rowi == 7, ca, pltpu.roll(a_blk, 7, axis=0))
                A, B = _scan_block_bwd(a_next, b_s[pl.ds(o, 8), :], rowi)
                d = B + A * cd
                d_s[pl.ds(o, 8), :] = d
                cd, ca = d[0:1, :], a_blk[0:1, :]
            return cd, ca

        cd, ca = lax.fori_loop(0, hb // unroll, blk, (carry_d[...], carry_a[...]))
        carry_d[...] = cd
        carry_a[...] = ca
        delta = d_s[...]

        h_last_prev = jnp.where(first, 0.0, hh_ref[7:8, :])
        row0 = lax.broadcasted_iota(jnp.int32, (tm, 1), 0) == 0
        h_prev = jnp.where(row0, h_last_prev, pltpu.roll(h, 1, axis=0))
        dbx = jnp.where(valid, delta, 0.0)
        da = delta * h_prev
        di = dbx * m * xc
        dm = dbx * ig * xc
        dla = a * (da - dm * a * inv_m)
        dla = jnp.where(valid, dla, 0.0)
        glam_ref[...] += jnp.sum(dla * r, axis=0, keepdims=True) * (LRU_C / (1.0 + jnp.exp(lam_)))
        dr = (-LRU_C) * sp * dla
        dpre = jnp.concatenate([dr * r * (1.0 - r), di * ig * (1.0 - ig)], axis=1)
        gbg_ref[...] += jnp.sum(dpre, axis=0, keepdims=True)
        dpre_b = dpre.astype(BF16)
        gacc[...] += _dot_tn(xc.astype(BF16), dpre_b)
        dxc = dbx * m * ig + _dot_nt(dpre_b, w_ref[...])
        gcb_ref[...] += jnp.sum(dxc, axis=0, keepdims=True)
        dext[0:tm, :] = dxc
        xr = xg_ref[:, :D_RG]
        dxr = None
        for j in range(CONV_W):
            shifted = dext[3 - j:3 - j + tm, :]
            gcw_ref[j:j + 1, :] += jnp.sum(xr * shifted, axis=0, keepdims=True)
            tap = cw_ref[j:j + 1, :] * shifted
            dxr = tap if dxr is None else dxr + tap
        dp_ref[:, :D_RG] = dxr.astype(BF16)
        dext[tm:tm + 8, :] = dext[0:8, :]

        @pl.when(i == nt - 1)
        def _():
            fold = _head_fold()
            mask = _head_mask()
            fold_b = fold.astype(BF16)
            for k in range(2):
                blockdiag = jnp.where(mask, gacc[:, k * D_RG:(k + 1) * D_RG], 0.0)
                hi = blockdiag.astype(BF16)
                rest = blockdiag - hi.astype(F32)
                mid = rest.astype(BF16)
                lo = (rest - mid.astype(F32)).astype(BF16)
                gw_ref[k * D_RG:(k + 1) * D_RG, :] = (_dot(hi, fold_b) + _dot(mid, fold_b)) + _dot(lo, fold_b)

    vec = lambda n: pl.BlockSpec((1, n), lambda i: (0, 0))
    rev = lambda n: pl.BlockSpec((tm, n), lambda i: (nt - 1 - i, 0))
    halo = lambda n: pl.BlockSpec((8, n), lambda i: (jnp.maximum((nt - 1 - i) * hb - 1, 0), 0))
    return pl.pallas_call(
        body, grid=(nt,),
        in_specs=[rev(2 * D_RG), rev(D_RG), rev(D_RG), halo(D_RG), rev(D_RG), ANY,
                  pl.BlockSpec((CONV_W, D_RG), lambda i: (0, 0)), vec(D_RG),
                  pl.BlockSpec((D_RG, 2 * D_RG), lambda i: (0, 0)), vec(2 * D_RG), vec(D_RG), vec(D_RG)],
        out_specs=[rev(2 * D_RG), pl.BlockSpec((CONV_W, D_RG), lambda i: (0, 0)), vec(D_RG),
                   pl.BlockSpec((2 * D_RG, RG_HEAD_DIM), lambda i: (0, 0)), vec(2 * D_RG), vec(D_RG), vec(D_RG)],
        input_output_aliases={5: 0},
        out_shape=[jax.ShapeDtypeStruct((T, D_IN), BF16), jax.ShapeDtypeStruct((CONV_W, D_RG), F32),
                   jax.ShapeDtypeStruct((1, D_RG), F32), jax.ShapeDtypeStruct((2 * D_RG, RG_HEAD_DIM), F32),
                   jax.ShapeDtypeStruct((1, 2 * D_RG), F32), jax.ShapeDtypeStruct((1, D_RG), F32),
                   jax.ShapeDtypeStruct((1, D_RG), F32)],
        scratch_shapes=[pltpu.VMEM((tm + 8, D_RG), F32),
                        pltpu.VMEM((tm, D_RG), F32), pltpu.VMEM((tm, D_RG), F32), pltpu.VMEM((tm, D_RG), F32),
                        pltpu.VMEM((D_RG, 2 * D_RG), F32), pltpu.VMEM((1, D_RG), F32), pltpu.VMEM((1, D_RG), F32)],
        name="rg_bwd", compiler_params=_params("arbitrary"),
    )(p, xc_all, hs, hs, dy, dp, cw, cb, wg, bg, lam, rg_g)


def _hg_bwd(p, o_all, st_all, dy, lbraw, hg_g):
    T = p.shape[0]
    n_chunks = T // CHUNK
    cps = _chunks_per_step(n_chunks)
    rows = cps * CHUNK
    n_steps = n_chunks // cps

    def body(hq_ref, hf_ref, hi_ref, hg_ref, o_ref, st_ref, dy_ref, lb_ref, g_ref,
             dp_ref, glb_ref, gg_ref, dst):
        i = pl.program_id(0)

        @pl.when(i == 0)
        def _():
            dst[...] = jnp.zeros_like(dst)
            glb_ref[...] = jnp.zeros_like(glb_ref)
            gg_ref[...] = jnp.zeros_like(gg_ref)

        dp_ref[:, :2 * D_RG] = jnp.zeros((rows, 2 * D_RG), BF16)

        def chunk(jj, carry):
            j = cps - 1 - jj
            rs = pl.ds(pl.multiple_of(j * CHUNK, CHUNK), CHUNK)
            chunk_body((n_steps - 1 - i) * cps + j, hq_ref.at[rs, :], hf_ref.at[rs, :], hi_ref.at[rs, :],
                       hg_ref.at[rs, :], o_ref.at[rs, :], st_ref.at[pl.ds(j, 1)], dy_ref.at[rs, :], lb_ref, g_ref,
                       dp_ref.at[rs, pl.ds(2 * D_RG, 4 * D_HG)], glb_ref, gg_ref, dst)
            return carry

        lax.fori_loop(0, cps, chunk, 0, unroll=True)

    def chunk_body(n, hq_ref, hf_ref, hi_ref, hg_ref, o_ref, st_ref, dy_ref, lb_ref, g_ref,
                   dp_ref, glb_ref, gg_ref, dst):
        valid = (n * CHUNK + lax.broadcasted_iota(jnp.int32, (CHUNK, 1), 0)) >= PAD
        hq, hf, v, hg = hq_ref[...], hf_ref[...], hi_ref[...], hg_ref[...]
        lb, sq, q, sf, f, b = _hg_gates(hq, hf, lb_ref, valid)
        k = 1.0 - f
        causal = _causal()
        r_i = lax.broadcasted_iota(jnp.int32, (CHUNK, CHUNK), 0)
        c_i = lax.broadcasted_iota(jnp.int32, (CHUNK, CHUNK), 1)
        causal_t = r_i <= c_i
        is_last = lax.broadcasted_iota(jnp.int32, (CHUNK, 1), 0) == CHUNK - 1
        g_ = g_ref[...]
        db_parts, dq_parts, dk_parts = [], [], []
        gg = jnp.zeros((1, HG_HEAD_DIM), F32)
        heads = [slice(h * HG_HEAD_DIM, (h + 1) * HG_HEAD_DIM) for h in range(HG_HEADS)]

        do_parts = []
        for h, sl in enumerate(heads):
            o = o_ref[:, sl]
            ro = _rms(o)
            no = o * ro
            hgh = hg[:, sl]
            sg = _sigmoid(hgh)
            dyh = dy_ref[:, sl]
            dp_ref[:, 3 * D_HG + h * HG_HEAD_DIM:3 * D_HG + (h + 1) * HG_HEAD_DIM] = (
                dyh * no * g_ * sg * (1.0 + hgh * (1.0 - sg))).astype(BF16)
            dng = dyh * hgh * sg
            gg = gg + jnp.sum(dng * no, axis=0, keepdims=True)
            do_parts.append(_rms_bwd(dng * g_, no, ro))
        do_t = jnp.concatenate(do_parts, axis=1).T.astype(BF16)

        fac = []
        for sl, do in zip(heads, do_parts):
            qh, kh, bh = q[:, sl], k[:, sl], b[:, sl]
            b_last, eb, eq, ekh, ek, q_hat, k_til = _hg_head(qh, kh, bh)
            fac.append(dict(qh=qh, kh=kh, e_last=jnp.exp(b_last), eb=eb, eq=eq, ekh=ekh, ek=ek,
                            q_til=qh * eb, k_hat=kh * ekh, qhb=q_hat.astype(BF16), ktb=k_til.astype(BF16),
                            vb=v[:, sl].astype(BF16), dob=do.astype(BF16)))

        first = []
        for sl, t in zip(heads, fac):
            st_h = st_ref[0, sl, :]
            dst_h = dst[sl, :]
            dstb = dst_h.astype(BF16)
            first.append(dict(
                att_t=_dot_nt(t["ktb"], t["qhb"]), datt=_dot_nt(t["dob"], t["vb"]),
                datt_t=_dot_nt(t["vb"], t["dob"]), dk_hat=_dot(t["vb"], dstb),
                dv=_dot_nt(t["k_hat"].astype(BF16), dstb), dq_til=_dot(t["dob"], st_h.astype(BF16)),
                state=t["e_last"] * jnp.sum(dst_h * st_h, axis=0, keepdims=True)))
            dst[sl, :] = dst_h * t["e_last"] + _dot(do_t[sl, :], t["q_til"].astype(BF16))

        for h, (t, m) in enumerate(zip(fac, first)):
            qh, kh, eb, eq, ekh, ek = t["qh"], t["kh"], t["eb"], t["eq"], t["ekh"], t["ek"]
            q_til, k_hat, qhb, ktb, dob = t["q_til"], t["k_hat"], t["qhb"], t["ktb"], t["dob"]
            dk_hat, dq_til = m["dk_hat"], m["dq_til"]
            dv = m["dv"] + _dot(jnp.where(causal_t, m["att_t"], 0.0).astype(BF16), dob)
            dq_hat = _dot(jnp.where(causal, m["datt"], 0.0).astype(BF16), ktb)
            dk_til = _dot(jnp.where(causal_t, m["datt_t"], 0.0).astype(BF16), qhb)
            db_last = jnp.sum(dk_hat * k_hat, axis=0, keepdims=True) + m["state"]
            dq_sel = jnp.concatenate([dq_hat[SUB * s:SUB * (s + 1), s * HG_HEAD_DIM:(s + 1) * HG_HEAD_DIM]
                                      for s in range(N_SUB)], axis=0)
            dq_a = dq_sel * eq
            dk_rows, k_att_rows = [], []
            for b_ in range(N_SUB):
                rs = slice(SUB * b_, SUB * (b_ + 1))
                dk_sum = k_att_sum = None
                for s in range(b_, N_SUB):
                    cs = slice(s * HG_HEAD_DIM, (s + 1) * HG_HEAD_DIM)
                    d = dk_til[rs, cs]
                    t_dk = d * ek[s][rs, :]
                    t_att = ktb[rs, cs].astype(F32) * d
                    dk_sum = t_dk if dk_sum is None else dk_sum + t_dk
                    k_att_sum = t_att if k_att_sum is None else k_att_sum + t_att
                dk_rows.append(dk_sum)
                k_att_rows.append(k_att_sum)
            dk_a = jnp.concatenate(dk_rows, axis=0)
            db = (dq_til * q_til - dk_hat * k_hat + (qh * eq).astype(BF16).astype(F32) * dq_sel
                  - jnp.concatenate(k_att_rows, axis=0))
            db_parts.append(jnp.where(is_last, db + db_last, db))
            dq_parts.append(dq_til * eb + dq_a)
            dk_parts.append(dk_hat * ekh + dk_a)
            dp_ref[:, 2 * D_HG + h * HG_HEAD_DIM:2 * D_HG + (h + 1) * HG_HEAD_DIM] = dv.astype(BF16)

        gg_ref[...] += gg
        db = jnp.concatenate(db_parts, axis=1)
        dq = jnp.concatenate(dq_parts, axis=1)
        dk = jnp.concatenate(dk_parts, axis=1)
        dlf = jnp.where(valid, _running_sum(db, False), 0.0)
        dp_ref[:, :D_HG] = (dq * sq * (1.0 + hq * (1.0 - sq))).astype(BF16)
        df = dlf / f - dk
        dlb = jnp.sum(df * (1.0 - sf), axis=0, keepdims=True) * lb * (1.0 - lb)
        glb_ref[0:1, :] += dlb
        glb_ref[1:2, :] += -dlb
        dp_ref[:, D_HG:2 * D_HG] = (df * (1.0 - lb) * sf * (1.0 - sf)).astype(BF16)

    rev = lambda j: pl.BlockSpec((rows, D_HG), lambda i: (n_steps - 1 - i, j))
    return pl.pallas_call(
        body, grid=(n_steps,),
        in_specs=[rev(2), rev(3), rev(4), rev(5), rev(0),
                  pl.BlockSpec((cps, D_HG, HG_HEAD_DIM), lambda i: (n_steps - 1 - i, 0, 0)), rev(1),
                  pl.BlockSpec((2, D_HG), lambda i: (0, 0)), pl.BlockSpec((1, HG_HEAD_DIM), lambda i: (0, 0))],
        out_specs=[pl.BlockSpec((rows, D_IN), lambda i: (n_steps - 1 - i, 0)),
                   pl.BlockSpec((2, D_HG), lambda i: (0, 0)), pl.BlockSpec((1, HG_HEAD_DIM), lambda i: (0, 0))],
        out_shape=[jax.ShapeDtypeStruct((T, D_IN), BF16), jax.ShapeDtypeStruct((2, D_HG), F32),
                   jax.ShapeDtypeStruct((1, HG_HEAD_DIM), F32)],
        scratch_shapes=[pltpu.VMEM((D_HG, HG_HEAD_DIM), F32)],
        name="hg_bwd", compiler_params=_params("arbitrary"),
    )(p, p, p, p, o_all, st_all, dy, lbraw, hg_g)


def _in_bwd(dp, w_in, h0, g1, dh1):
    T = h0.shape[0]
    tm = _row_tile(T, 832)
    n_steps = T // tm

    def body(dp_ref, w_ref, h_ref, g_ref, d1_ref, gx_hbm, gmeta_ref, gg_ref, buf, sems):
        i = pl.program_id(0)
        first, later = _window_copies(gx_hbm, buf, sems, tm)
        slot = i % 2

        @pl.when(i == 0)
        def _():
            gg_ref[...] = jnp.zeros_like(gg_ref)

        if n_steps > 2:
            @pl.when(i == 2)
            def _():
                first(False).wait()

            @pl.when(i > 2)
            def _():
                later(i - 2, slot, False).wait()

        du = _dot_nt(dp_ref[...], w_ref[...])
        h0_ = h_ref[...]
        r = _rms(h0_)
        n = h0_ * r
        gg_ref[...] += jnp.sum(du * n, axis=0, keepdims=True)
        dh0 = d1_ref[...] + _rms_bwd(du * g_ref[...], n, r)
        buf[slot] = dh0

        @pl.when(i == 0)
        def _():
            gmeta_ref[...] = dh0[PAD:HEAD, :]
            first(False).start()

        if n_steps > 1:
            @pl.when(i > 0)
            def _():
                later(i, slot, False).start()

        @pl.when(i == n_steps - 1)
        def _():
            if n_steps == 1:
                first(False).wait()
            else:
                if n_steps == 2:
                    first(False).wait()
                else:
                    later(i - 1, 1 - slot, False).wait()
                later(i, slot, False).wait()

    row = lambda n: pl.BlockSpec((tm, n), lambda i: (i, 0))
    return pl.pallas_call(
        body, grid=(n_steps,),
        in_specs=[row(D_IN), _resident((D_MODEL, D_IN)),
                  row(D_MODEL), pl.BlockSpec((1, D_MODEL), lambda i: (0, 0)), row(D_MODEL)],
        out_specs=[pl.BlockSpec(memory_space=pl.ANY), pl.BlockSpec((N_META, D_MODEL), lambda i: (0, 0)),
                   pl.BlockSpec((1, D_MODEL), lambda i: (0, 0))],
        out_shape=[jax.ShapeDtypeStruct((T - HEAD, D_MODEL), F32), jax.ShapeDtypeStruct((N_META, D_MODEL), F32),
                   jax.ShapeDtypeStruct((1, D_MODEL), F32)],
        scratch_shapes=[pltpu.VMEM((2, tm, D_MODEL), F32), pltpu.SemaphoreType.DMA((2,))],
        name="in_bwd", compiler_params=_params("arbitrary"),
    )(dp, w_in, h0, g1, dh1)


def _col_tile(cols, target):
    best = None
    for t in range(128, min(cols, target) + 1, 128):
        if cols % t == 0:
            best = t
    assert best is not None, cols
    return best


MXU_DIM = 256


def _mxu_tile(cols, target):
    best = None
    for t in range(MXU_DIM, min(cols, target) + 1, MXU_DIM):
        if cols % t == 0:
            best = t
    assert best is not None, cols
    return best


def _weight_grad(a, b, name):
    T, M = a.shape
    N = b.shape[1]
    tm = _col_tile(M, 1408)
    tn = _mxu_tile(N, 768 if tm <= 1024 else 512)

    def body(a_ref, b_ref, o_ref):
        o_ref[...] = _dot_tn(a_ref[...], b_ref[...])

    return pl.pallas_call(
        body, grid=(M // tm, N // tn),
        in_specs=[pl.BlockSpec((T, tm), lambda m, n: (0, m)), pl.BlockSpec((T, tn), lambda m, n: (0, n))],
        out_specs=pl.BlockSpec((tm, tn), lambda m, n: (m, n)),
        out_shape=jax.ShapeDtypeStruct((M, N), F32),
        name=name, compiler_params=_params("parallel", "parallel"),
    )(a, b)


def _local_step(x, meta, target, w_in_own, w_in, w_out, w_gu, w_down, small, chip, on_ffn_grads=None,
                on_mixer_grads=None):
    wg = _gate_weights(small["w_rgate"], small["w_igate"])
    bg = jnp.concatenate([small["b_rgate"], small["b_igate"]], axis=1)

    p, u, h0 = _in_proj_local(x, meta, small["mix_norm_g"], w_in_own, chip)
    p = _in_proj_rest(u, w_in, p, chip)
    y_rg, hs, xc = _rg_fwd(p, small["conv_w"], small["conv_b"], wg, bg, small["lru_lambda"], small["rg_norm_g"])
    y_hg, o_all, st_all = _hg_fwd(p, small["hg_lower_bound"], small["hg_norm_g"])
    h1, v, yb, gu, act, dh2, dh2b, loss, g_final = _ffn_fwd(
        h0, y_rg, y_hg, w_out, small["ffn_norm_g"], w_gu, w_down, small["final_norm_g"], target)

    g_w_down = _weight_grad(act, dh2b, "grad_w_down")
    dgu, dh1, dh1b, dy, g_ffn = _ffn_bwd(dh2b, gu, w_down, w_gu, h1, small["ffn_norm_g"], dh2, w_out)
    ffn_grads = {"w_gate_up": _weight_grad(v, dgu, "grad_w_gate_up"), "w_down": g_w_down,
                 "w_out": _weight_grad(yb, dh1b, "grad_w_out")}
    stages = on_ffn_grads(ffn_grads) if on_ffn_grads is not None else None
    dp, g_lb, g_hgn = _hg_bwd(p, o_all, st_all, dy, small["hg_lower_bound"], small["hg_norm_g"])
    early = late = None
    if stages is not None:
        chip_sums, send = stages
        sums = chip_sums()
        (dp, dy), sums = lax.optimization_barrier(((dp, dy), sums))
        early = send(sums)
    dp, g_cw, g_cb, g_wgate, g_bg, g_lam, g_rgn = _rg_bwd(
        p, xc, hs, dy, dp, small["conv_w"], small["conv_b"], wg, bg, small["lru_lambda"], small["rg_norm_g"])
    mixer_grads = {"w_in": _weight_grad(u, dp, "grad_w_in")}
    if on_mixer_grads is not None:
        chip_sums, send = on_mixer_grads(mixer_grads)
        sums = chip_sums()
        (dp, dh1), sums = lax.optimization_barrier(((dp, dh1), sums))
        late = send(sums)
    grad_x, g_meta, g_mix = _in_bwd(dp, w_in, h0, small["mix_norm_g"], dh1)

    grads = {
        "w_in": mixer_grads["w_in"], "w_out": ffn_grads["w_out"],
        "w_gate_up": ffn_grads["w_gate_up"], "w_down": ffn_grads["w_down"],
        "meta_tokens": g_meta, "mix_norm_g": g_mix, "conv_w": g_cw, "conv_b": g_cb, "w_gates": g_wgate,
        "b_rgate": g_bg[:, :D_RG], "b_igate": g_bg[:, D_RG:], "lru_lambda": g_lam, "rg_norm_g": g_rgn,
        "hg_lower_bound": g_lb, "hg_norm_g": g_hgn, "ffn_norm_g": g_ffn, "final_norm_g": g_final,
    }
    return loss, grad_x, grads, early, late


ANY = pl.BlockSpec(memory_space=pl.ANY)
HALF = D_MODEL // 2

BIG = {"w_in": (D_MODEL, D_IN // N_CHIPS, True), "w_gate_up": (D_MODEL, 2 * D_FF // N_CHIPS, True),
       "w_out": (D_MODEL // N_CHIPS, D_MODEL, False), "w_down": (D_FF // N_CHIPS, D_MODEL, False)}
BIG_NAMES = tuple(BIG)
N_BIG = len(BIG_NAMES)


def _full_shape(name):
    rows, cols, by_col = BIG[name]
    return (rows, cols * N_CHIPS) if by_col else (rows * N_CHIPS, cols)


def _place():
    return lax.axis_index("x"), lax.axis_index("y"), lax.axis_index("c")


def _chip_of(x, y, r):
    fx, fy = (r + 1) >> 1, (r + 1) & 1
    return (1 - x if fx else x), (1 - y if fy else y)


def _half_of(ref, by_col, half):
    start = pl.multiple_of(half * HALF, 128)
    return ref.at[pl.ds(start, HALF), :] if by_col else ref.at[:, pl.ds(start, HALF)]


def _shard_of(ref, name, chip):
    rows, cols, by_col = BIG[name]
    if by_col:
        return ref.at[:, pl.ds(pl.multiple_of(chip * cols, 128), cols)]
    return ref.at[pl.ds(pl.multiple_of(chip * rows, 16), rows), :]


def _shard_half_of(ref, name, chip, half):
    rows, cols, by_col = BIG[name]
    start = pl.multiple_of(half * HALF, 128)
    if by_col:
        return ref.at[pl.ds(start, HALF), pl.ds(pl.multiple_of(chip * cols, 128), cols)]
    return ref.at[pl.ds(pl.multiple_of(chip * rows, 16), rows), pl.ds(start, HALF)]


def _shard_half_part_of(ref, name, chip, half, part):
    rows, cols, by_col = BIG[name]
    start = pl.multiple_of(half * HALF + part * (HALF // 2), 128)
    if by_col:
        return ref.at[pl.ds(start, HALF // 2), pl.ds(pl.multiple_of(chip * cols, 128), cols)]
    return ref.at[pl.ds(pl.multiple_of(chip * rows, 16), rows), pl.ds(start, HALF // 2)]


def _remote(src, dst, send_sems, recv_sems, k, dev):
    return pltpu.make_async_remote_copy(src_ref=src, dst_ref=dst, send_sem=send_sems.at[k], recv_sem=recv_sems.at[k],
                                        device_id=dev, device_id_type=MESH)


def _place_shards(w, small, chip, names, label):
    steps = 4
    n, ns = len(names), len(small)
    in_specs, out_specs = [], []
    for name in names:
        rows, cols, by_col = BIG[name]
        tr = rows // steps
        in_specs.append(pl.BlockSpec((tr, cols), lambda i, s: (i, 0)))
        if by_col:
            out_specs.append(pl.BlockSpec((tr, cols), lambda i, s: (i, s[0])))
        else:
            out_specs.append(pl.BlockSpec((tr, cols), lambda i, s: (s[0] * steps + i, 0)))

    def body(s_ref, *refs):
        ins, small_in = refs[:n], refs[n:n + ns]
        outs, small_out = refs[n + ns:2 * n + ns], refs[2 * n + ns:2 * (n + ns)]
        send_sems, recv_sems, local_sems = refs[2 * (n + ns):]
        i = pl.program_id(0)
        x, y, c = _place()
        chip_ = 2 * x + y
        others = [_chip_of(x, y, r) for r in range(3)]

        def block(a, q):
            cols = small[a].shape[1]
            return small_out[a].at[:, pl.ds(pl.multiple_of(q * cols, 128), cols)]

        def local(a):
            return pltpu.make_async_copy(small_in[a], block(a, chip_), local_sems.at[a])

        def remote(a, r):
            qx, qy = others[r]
            return _remote(small_in[a], block(a, chip_), send_sems, recv_sems, 3 * a + r, (qx, qy, c))

        @pl.when(i == 0)
        def _():
            for a in range(ns):
                local(a).start()
                for r in range(3):
                    remote(a, r).start()

        for a in range(n):
            outs[a][...] = ins[a][...].astype(BF16)

        @pl.when(i == steps - 1)
        def _():
            for a in range(ns):
                for r, (qx, qy) in enumerate(others):
                    landed = block(a, 2 * qx + qy)
                    _remote(landed, landed, send_sems, recv_sems, 3 * a + r, (qx, qy, c)).wait_recv()
                for r in range(3):
                    remote(a, r).wait_send()
                local(a).wait()

    out = pl.pallas_call(
        body,
        grid_spec=pltpu.PrefetchScalarGridSpec(
            num_scalar_prefetch=1, grid=(steps,), in_specs=in_specs + [ANY] * ns, out_specs=out_specs + [ANY] * ns,
            scratch_shapes=[pltpu.SemaphoreType.DMA((max(3 * ns, 1),)), pltpu.SemaphoreType.DMA((max(3 * ns, 1),)),
                            pltpu.SemaphoreType.DMA((max(ns, 1),))]),
        out_shape=([jax.ShapeDtypeStruct(_full_shape(name), BF16) for name in names]
                   + [jax.ShapeDtypeStruct((s.shape[0], s.shape[1] * N_CHIPS), F32) for s in small]),
        name=label, compiler_params=_params("arbitrary"),
    )(chip, *[w[name] for name in names], *small)
    return dict(zip(names, out[:n])), list(out[n:])


def _gather_weights(placed, small, names, label, collective_id):
    n, ns = len(names), len(small)
    hbm = pltpu.MemorySpace.HBM
    outs = [jax.new_ref(placed[nm], memory_space=hbm) for nm in names]
    small_in = [jax.new_ref(s, memory_space=hbm) for s in small]
    small_out = [jax.empty_ref(jax.ShapeDtypeStruct((s.shape[0], s.shape[1] * N_CHIPS), F32), memory_space=hbm)
                 for s in small]
    n_sems = 8 * n + 3 * ns

    @pl.kernel(mesh=plsc.ScalarSubcoreMesh(axis_name="seq", num_cores=1), name=label, out_type=(),
               scratch_types=(pltpu.SemaphoreType.DMA((n_sems,)), pltpu.SemaphoreType.DMA((n_sems,)),
                              pltpu.SemaphoreType.DMA((max(ns, 1),))),
               compiler_params=pltpu.CompilerParams(collective_id=collective_id))
    def launch(send_sems, recv_sems, local_sems):
        x, y, c = _place()
        chip = 2 * x + y
        sibling = (x, y, 1 - c)
        others = [_chip_of(x, y, r) for r in range(3)]
        near = others[:2]
        far = 2 * others[2][0] + others[2][1]
        _handshake([(qx, qy, c) for qx, qy in others] + [sibling])

        def small_block(a, q):
            cols = small[a].shape[1]
            return small_out[a].at[:, pl.ds(pl.multiple_of(q * cols, 128), cols)]

        local = [pltpu.make_async_copy(small_in[a], small_block(a, chip), local_sems.at[a]) for a in range(ns)]
        for cp in local:
            cp.start()

        sends = []
        for a, name in enumerate(names):
            mine = _shard_half_of(outs[a], name, chip, c)
            for r, (qx, qy) in enumerate(near):
                sends.append(_remote(mine, mine, send_sems, recv_sems, 8 * a + r, (qx, qy, c)))
        for a in range(ns):
            for r, (qx, qy) in enumerate(others):
                sends.append(_remote(small_in[a], small_block(a, chip), send_sems, recv_sems,
                                     8 * n + 3 * a + r, (qx, qy, c)))
        for cp in sends:
            cp.start()

        forwards = []

        def forward(piece, k, dev):
            cp = _remote(piece, piece, send_sems, recv_sems, k, dev)
            cp.start()
            forwards.append(cp)

        for a, name in enumerate(names):
            for r, (qx, qy) in enumerate(near):
                landed = _shard_half_of(outs[a], name, 2 * qx + qy, c)
                _remote(landed, landed, send_sems, recv_sems, 8 * a + r, (qx, qy, c)).wait_recv()
                ox, oy = near[1 - r]
                forward(_shard_half_part_of(outs[a], name, 2 * qx + qy, c, r), 8 * a + 2 + r, (ox, oy, c))
                forward(landed, 8 * a + 4 + r, sibling)
        for a, name in enumerate(names):
            for part in range(2):
                qx, qy = near[1 - part]
                landed = _shard_half_part_of(outs[a], name, far, c, part)
                _remote(landed, landed, send_sems, recv_sems, 8 * a + 2 + part, (qx, qy, c)).wait_recv()
                forward(landed, 8 * a + 6 + part, sibling)
        for a in range(ns):
            for r, (qx, qy) in enumerate(others):
                landed = small_block(a, 2 * qx + qy)
                _remote(landed, landed, send_sems, recv_sems, 8 * n + 3 * a + r, (qx, qy, c)).wait_recv()
        for a, name in enumerate(names):
            for r, (qx, qy) in enumerate(near):
                landed = _shard_half_of(outs[a], name, 2 * qx + qy, 1 - c)
                _remote(landed, landed, send_sems, recv_sems, 8 * a + 4 + r, sibling).wait_recv()
            for part in range(2):
                landed = _shard_half_part_of(outs[a], name, far, 1 - c, part)
                _remote(landed, landed, send_sems, recv_sems, 8 * a + 6 + part, sibling).wait_recv()
        for cp in sends + forwards:
            cp.wait_send()
        for cp in local:
            cp.wait()

    launch()
    return {nm: ref[...] for nm, ref in zip(names, outs)}, [ref[...] for ref in small_out]


def _exchange_halves(grads, names, label, collective_id):
    n = len(names)
    sequencer = collective_id is not None

    def body(*refs):
        ins, outs = refs[:n], refs[n:2 * n]
        send_sems, recv_sems = refs[2 * n:]
        x, y, c = _place()
        if sequencer:
            _handshake([(x, y, 1 - c)])
        copies = []
        for a, name in enumerate(names):
            copies.append(_remote(_half_of(ins[a], BIG[name][2], 1 - c), outs[a], send_sems, recv_sems, a,
                                  (x, y, 1 - c)))
        for cp in copies:
            cp.start()
        for cp in copies:
            cp.wait()

    def half_shape(name):
        r, c_ = _full_shape(name)
        return (HALF, c_) if BIG[name][2] else (r, HALF)

    out_type = tuple(jax.ShapeDtypeStruct(half_shape(nm), F32) for nm in names)
    sems = (pltpu.SemaphoreType.DMA((n,)), pltpu.SemaphoreType.DMA((n,)))
    operands = [grads[nm] for nm in names]
    if sequencer:
        got = pl.kernel(
            body, mesh=plsc.ScalarSubcoreMesh(axis_name="seq", num_cores=1), name=label, out_type=out_type,
            scratch_types=sems, compiler_params=pltpu.CompilerParams(collective_id=collective_id),
        )(*operands)
    else:
        got = pl.pallas_call(
            body, in_specs=[ANY] * n, out_specs=[ANY] * n, out_shape=list(out_type), scratch_shapes=list(sems),
            name=label,
        )(*operands)
    return dict(zip(names, got))


def _chip_sum(grads, got, names, core, label):
    n = len(names)
    steps = 4
    g_specs, blks = [], []
    for name in names:
        rows, cols = got[name].shape
        tr = rows // steps
        if BIG[name][2]:
            g_specs.append(pl.BlockSpec((tr, cols), lambda i, s: (s[0] * steps + i, 0)))
        else:
            g_specs.append(pl.BlockSpec((tr, HALF), lambda i, s: (i, s[0])))
        blks.append(pl.BlockSpec((tr, cols), lambda i, s: (i, 0)))

    def body(s_ref, *refs):
        for a in range(n):
            t = refs[a][...] + refs[n + a][...]
            refs[2 * n + a][...] = t
            refs[3 * n + a][...] = t.astype(BF16)

    out = pl.pallas_call(
        body,
        grid_spec=pltpu.PrefetchScalarGridSpec(num_scalar_prefetch=1, grid=(steps,), in_specs=g_specs + blks,
                                               out_specs=blks + blks),
        out_shape=([jax.ShapeDtypeStruct(got[nm].shape, F32) for nm in names]
                   + [jax.ShapeDtypeStruct(got[nm].shape, BF16) for nm in names]),
        name=label, compiler_params=_params("parallel"),
    )(core, *[grads[nm] for nm in names], *[got[nm] for nm in names])
    return {nm: (out[a], out[n + a]) for a, nm in enumerate(names)}


def _piece_shape(name):
    rows, cols, by_col = BIG[name]
    return (HALF, cols) if by_col else (rows, HALF)


def _handshake(peers):
    barrier = pltpu.get_barrier_semaphore()
    for peer in peers:
        pl.semaphore_signal(barrier, inc=1, device_id=peer, device_id_type=MESH)
    pl.semaphore_wait(barrier, len(peers))


def _send_chip_sums(sums, names, label, collective_id):
    n = len(names)

    def body(*refs):
        ins, outs = refs[:n], refs[n:2 * n]
        send_sems, recv_sems = refs[2 * n:]
        x, y, c = _place()
        others = [_chip_of(x, y, r) for r in range(3)]
        _handshake([(qx, qy, c) for qx, qy in others])
        copies = []
        for a, name in enumerate(names):
            for r, (qx, qy) in enumerate(others):
                copies.append(_remote(_shard_of(ins[a], name, 2 * qx + qy), outs[a].at[r], send_sems, recv_sems,
                                      3 * a + r, (qx, qy, c)))
        for cp in copies:
            cp.start()
        for cp in copies:
            cp.wait()

    return pl.kernel(
        body, mesh=plsc.ScalarSubcoreMesh(axis_name="seq", num_cores=1), name=label,
        out_type=tuple(jax.ShapeDtypeStruct((3,) + _piece_shape(nm), BF16) for nm in names),
        scratch_types=(pltpu.SemaphoreType.DMA((3 * n,)), pltpu.SemaphoreType.DMA((3 * n,))),
        compiler_params=pltpu.CompilerParams(collective_id=collective_id),
    )(*[sums[nm] for nm in names])


def _total(parts, chip_core):
    steps = 2
    in_specs, out_specs, operands = [], [], []
    for name in BIG_NAMES:
        by_col = BIG[name][2]
        pr, pc = _piece_shape(name)
        tr = pr // steps
        if by_col:
            in_specs.append(pl.BlockSpec((tr, pc), lambda i, s: (i, s[0])))
            out_specs.append(pl.BlockSpec((tr, pc), lambda i, s: (s[1] * steps + i, 0)))
        else:
            in_specs.append(pl.BlockSpec((tr, pc), lambda i, s: (s[0] * steps + i, 0)))
            out_specs.append(pl.BlockSpec((tr, pc), lambda i, s: (i, s[1])))
        for r in range(3):
            in_specs.append(pl.BlockSpec((None, tr, pc), lambda i, s, r=r: (r, i, 0)))
        own, got = parts[name]
        operands += [own, got, got, got]

    def body(s_ref, *refs):
        for a in range(N_BIG):
            o_ref, a_ref, b_ref, c_ref = refs[4 * a:4 * a + 4]
            refs[4 * N_BIG + a][...] = (((o_ref[...] + a_ref[...].astype(F32)) + b_ref[...].astype(F32))
                                        + c_ref[...].astype(F32))

    totals = pl.pallas_call(
        body,
        grid_spec=pltpu.PrefetchScalarGridSpec(num_scalar_prefetch=1, grid=(steps,), in_specs=in_specs,
                                               out_specs=out_specs),
        out_shape=[jax.ShapeDtypeStruct(BIG[name][:2], F32) for name in BIG_NAMES],
        name="totals", compiler_params=_params("parallel"),
    )(chip_core, *operands)
    return dict(zip(BIG_NAMES, totals))


VEC_ROWS = 32
VEC_ROW = {"mix_norm_g": 0, "conv_b": 1, "b_rgate": 2, "b_igate": 3, "lru_lambda": 4, "rg_norm_g": 5,
           "hg_lower_bound": 6, "hg_norm_g": 8, "ffn_norm_g": 9, "final_norm_g": 10, "loss": 11,
           "conv_w": 12, "meta_tokens": 16}
N_DEV = 8


def _all_reduce_small(pieces, gates, totals):
    names = list(pieces)
    n_small = 10
    hv, hg = VEC_ROWS // 2, gates.shape[0] // 2

    def body(*refs):
        ins = refs[:len(names)]
        g_ref = refs[len(names)]
        vec_ref, gsum_ref = refs[len(names) + 1 + N_BIG:len(names) + 3 + N_BIG]
        big = refs[len(names) + 3 + N_BIG:len(names) + 3 + 2 * N_BIG]
        (mine_v, sib_v, sib_g, chip_v, chip_g, got_v, got_g, send_sems, recv_sems) = refs[len(names) + 3 + 2 * N_BIG:]
        x, y, c = _place()
        chip = 2 * x + y
        sibling = (x, y, 1 - c)
        share = []
        for a, name in enumerate(BIG_NAMES):
            half = _half_of(big[a], BIG[name][2], c)
            share.append(_remote(half, half, send_sems, recv_sems, n_small + a, sibling))
        mine_v[...] = jnp.zeros_like(mine_v)
        for name, ref in zip(names, ins):
            nr, w = ref.shape
            mine_v[VEC_ROW[name]:VEC_ROW[name] + nr, 0:w] = ref[...]

        swap = [_remote(mine_v, sib_v, send_sems, recv_sems, 0, sibling),
                _remote(g_ref, sib_g, send_sems, recv_sems, 1, sibling)]
        for cp in swap:
            cp.start()
        for cp in swap:
            cp.wait()
        for cp in share:
            cp.start()
        chip_v[...] = mine_v[...] + sib_v[...]
        chip_g[...] = g_ref[...] + sib_g[...]

        rows_v = pl.ds(pl.multiple_of(c * hv, 8), hv)
        rows_g = pl.ds(pl.multiple_of(c * hg, 8), hg)
        got_v[chip] = chip_v[rows_v, :]
        got_g[chip] = chip_g[rows_g, :].astype(BF16)
        sends = []
        for r in range(3):
            qx, qy = _chip_of(x, y, r)
            sends.append(_remote(chip_v.at[rows_v, :], got_v.at[chip], send_sems, recv_sems, 2 + r, (qx, qy, c)))
            sends.append(_remote(got_g.at[chip], got_g.at[chip], send_sems, recv_sems, 5 + r, (qx, qy, c)))
        for cp in sends:
            cp.start()
        for cp in sends:
            cp.wait()
        vec_ref[rows_v, :] = ((got_v[0] + got_v[1]) + got_v[2]) + got_v[3]
        gsum_ref[rows_g, :] = ((got_g[0].astype(F32) + got_g[1].astype(F32)) + got_g[2].astype(F32)
                               ) + got_g[3].astype(F32)

        back = [_remote(vec_ref.at[rows_v, :], vec_ref.at[rows_v, :], send_sems, recv_sems, 8, sibling),
                _remote(gsum_ref.at[rows_g, :], gsum_ref.at[rows_g, :], send_sems, recv_sems, 9, sibling)]
        for cp in back:
            cp.start()
        theirs_v = vec_ref.at[pl.ds(pl.multiple_of((1 - c) * hv, 8), hv), :]
        theirs_g = gsum_ref.at[pl.ds(pl.multiple_of((1 - c) * hg, 8), hg), :]
        _remote(theirs_v, theirs_v, send_sems, recv_sems, 8, sibling).wait_recv()
        _remote(theirs_g, theirs_g, send_sems, recv_sems, 9, sibling).wait_recv()
        for cp in back:
            cp.wait_send()
        for a, name in enumerate(BIG_NAMES):
            theirs = _half_of(big[a], BIG[name][2], 1 - c)
            _remote(theirs, theirs, send_sems, recv_sems, n_small + a, sibling).wait_recv()
        for cp in share:
            cp.wait_send()

    vmem = pl.BlockSpec(memory_space=pltpu.VMEM)
    n_sems = n_small + N_BIG
    out = pl.pallas_call(
        body, in_specs=[vmem] * (len(names) + 1) + [ANY] * N_BIG, out_specs=[vmem, vmem] + [ANY] * N_BIG,
        out_shape=([jax.ShapeDtypeStruct((VEC_ROWS, D_MODEL), F32), jax.ShapeDtypeStruct(gates.shape, F32)]
                   + [jax.ShapeDtypeStruct(BIG[n][:2], F32) for n in BIG_NAMES]),
        input_output_aliases={len(names) + 1 + a: 2 + a for a in range(N_BIG)},
        scratch_shapes=[pltpu.VMEM((VEC_ROWS, D_MODEL), F32), pltpu.VMEM((VEC_ROWS, D_MODEL), F32),
                        pltpu.VMEM(gates.shape, F32), pltpu.VMEM((VEC_ROWS, D_MODEL), F32),
                        pltpu.VMEM(gates.shape, F32), pltpu.VMEM((N_CHIPS, hv, D_MODEL), F32),
                        pltpu.VMEM((N_CHIPS, hg) + gates.shape[1:], BF16),
                        pltpu.SemaphoreType.DMA((n_sems,)), pltpu.SemaphoreType.DMA((n_sems,))],
        name="all_reduce_small",
    )(*[pieces[n] for n in names], gates, *[totals[n] for n in BIG_NAMES])
    return out[0], out[1], dict(zip(BIG_NAMES, out[2:]))


def _adamw_math(w, g, m, v):
    m = ADAM_B1 * m + (1.0 - ADAM_B1) * g
    v = ADAM_B2 * v + (1.0 - ADAM_B2) * (g * g)
    m_hat = m / (1.0 - ADAM_B1 ** ADAM_STEP)
    v_hat = v / (1.0 - ADAM_B2 ** ADAM_STEP)
    delta = -ADAM_LR * (m_hat / (jnp.sqrt(v_hat) + ADAM_EPS) + ADAM_WD * w)
    return delta, m, v


def _adamw_big(w, g, m, v):
    steps = 8
    blks = []
    for name in BIG_NAMES:
        rows, cols, _ = BIG[name]
        blks.append(pl.BlockSpec((rows // steps, cols), lambda i: (i, 0)))

    def body(*refs):
        ins, outs = refs[:4 * N_BIG], refs[4 * N_BIG:]
        for a in range(N_BIG):
            w_ref, g_ref, m_ref, v_ref = (ins[k * N_BIG + a] for k in range(4))
            g = g_ref[...]
            d, nm, nv = _adamw_math(w_ref[...], g, m_ref[...], v_ref[...])
            outs[a][...] = g
            outs[N_BIG + a][...] = d
            outs[2 * N_BIG + a][...] = nm
            outs[3 * N_BIG + a][...] = nv

    shapes = [jax.ShapeDtypeStruct(BIG[name][:2], F32) for name in BIG_NAMES]
    out = pl.pallas_call(
        body, grid=(steps,), in_specs=blks * 4, out_specs=blks * 4, out_shape=shapes * 4,
        name="adamw_big", compiler_params=_params("parallel"),
    )(*[t[name] for t in (w, g, m, v) for name in BIG_NAMES])
    return {name: tuple(out[k * N_BIG + a] for k in range(4)) for a, name in enumerate(BIG_NAMES)}


SMALL = {"meta_tokens": (N_META, D_MODEL // N_CHIPS), "mix_norm_g": (1, D_MODEL), "conv_w": (CONV_W, D_RG // N_CHIPS),
         "conv_b": (1, D_RG), "w_rgate": (D_RG, RG_HEAD_DIM), "b_rgate": (1, D_RG), "w_igate": (D_RG, RG_HEAD_DIM),
         "b_igate": (1, D_RG), "lru_lambda": (1, D_RG), "rg_norm_g": (1, D_RG), "hg_lower_bound": (2, D_HG),
         "hg_norm_g": (1, HG_HEAD_DIM), "ffn_norm_g": (1, D_MODEL), "final_norm_g": (1, D_MODEL)}
SMALL_NAMES = tuple(SMALL)
SHARDED_SMALL = ("meta_tokens", "conv_w")


def _adamw_small(vec, gates, w, m, v):
    n = len(SMALL_NAMES)

    def body(*refs):
        vec_ref, gates_ref = refs[:2]
        w_refs, m_refs, v_refs = refs[2:2 + n], refs[2 + n:2 + 2 * n], refs[2 + 2 * n:2 + 3 * n]
        outs = refs[2 + 3 * n:]
        loss_ref = outs[0]
        x, y, _ = _place()
        chip = 2 * x + y
        loss_ref[...] = vec_ref[VEC_ROW["loss"]:VEC_ROW["loss"] + 1, 0:1]

        def update(k, g):
            g_ref, d_ref, nm_ref, nv_ref = outs[1 + 4 * k:5 + 4 * k]
            g_ref[...] = g
            d_ref[...], nm_ref[...], nv_ref[...] = _adamw_math(w_refs[k][...], g, m_refs[k][...], v_refs[k][...])

        for k, name in enumerate(SMALL_NAMES):
            nr, w_ = SMALL[name]
            if name == "w_rgate":
                update(k, gates_ref[0:D_RG, :])
            elif name == "w_igate":
                update(k, gates_ref[D_RG:2 * D_RG, :])
            elif name in SHARDED_SMALL:
                r0 = VEC_ROW[name]
                for q in range(N_CHIPS):
                    @pl.when(chip == q)
                    def _(k=k, r0=r0, nr=nr, w_=w_, q=q):
                        update(k, vec_ref[r0:r0 + nr, q * w_:(q + 1) * w_])
            else:
                r0 = VEC_ROW[name]
                update(k, vec_ref[r0:r0 + nr, 0:w_])

    vmem = pl.BlockSpec(memory_space=pltpu.VMEM)
    out_shape = [jax.ShapeDtypeStruct((1, 1), F32)]
    for name in SMALL_NAMES:
        out_shape += [jax.ShapeDtypeStruct(SMALL[name], F32)] * 4
    outs = pl.pallas_call(
        body, in_specs=[vmem] * (2 + 3 * n), out_specs=[vmem] * len(out_shape), out_shape=out_shape,
        name="adamw_small",
    )(vec, gates, *[w[k] for k in SMALL_NAMES], *[m[k] for k in SMALL_NAMES], *[v[k] for k in SMALL_NAMES])
    loss = outs[0]
    res = {name: tuple(outs[1 + 4 * k:5 + 4 * k]) for k, name in enumerate(SMALL_NAMES)}
    return loss, res


WEIGHT_NAMES = ("meta_tokens", "mix_norm_g", "w_in", "conv_w", "conv_b", "w_rgate", "b_rgate", "w_igate", "b_igate",
                "lru_lambda", "rg_norm_g", "hg_lower_bound", "hg_norm_g", "w_out", "ffn_norm_g", "w_gate_up", "w_down",
                "final_norm_g")


def _to_2d(name, a):
    if name in BIG:
        return a.reshape(BIG[name][:2])
    return a.reshape(SMALL[name])


def kernel(x, meta_tokens, mix_norm_g, w_in, conv_w, conv_b, w_rgate, b_rgate, w_igate, b_igate, lru_lambda, rg_norm_g, hg_lower_bound, hg_norm_g, w_out, ffn_norm_g, w_gate_up, w_down, final_norm_g, loss_target, m_meta_tokens, m_mix_norm_g, m_w_in, m_conv_w, m_conv_b, m_w_rgate, m_b_rgate, m_w_igate, m_b_igate, m_lru_lambda, m_rg_norm_g, m_hg_lower_bound, m_hg_norm_g, m_w_out, m_ffn_norm_g, m_w_gate_up, m_w_down, m_final_norm_g, v_meta_tokens, v_mix_norm_g, v_w_in, v_conv_w, v_conv_b, v_w_rgate, v_b_rgate, v_w_igate, v_b_igate, v_lru_lambda, v_rg_norm_g, v_hg_lower_bound, v_hg_norm_g, v_w_out, v_ffn_norm_g, v_w_gate_up, v_w_down, v_final_norm_g):
    w_raw = dict(zip(WEIGHT_NAMES, (meta_tokens, mix_norm_g, w_in, conv_w, conv_b, w_rgate, b_rgate, w_igate, b_igate,
                                    lru_lambda, rg_norm_g, hg_lower_bound, hg_norm_g, w_out, ffn_norm_g, w_gate_up,
                                    w_down, final_norm_g)))
    m_raw = dict(zip(WEIGHT_NAMES, (m_meta_tokens, m_mix_norm_g, m_w_in, m_conv_w, m_conv_b, m_w_rgate, m_b_rgate,
                                    m_w_igate, m_b_igate, m_lru_lambda, m_rg_norm_g, m_hg_lower_bound, m_hg_norm_g,
                                    m_w_out, m_ffn_norm_g, m_w_gate_up, m_w_down, m_final_norm_g)))
    v_raw = dict(zip(WEIGHT_NAMES, (v_meta_tokens, v_mix_norm_g, v_w_in, v_conv_w, v_conv_b, v_w_rgate, v_b_rgate,
                                    v_w_igate, v_b_igate, v_lru_lambda, v_rg_norm_g, v_hg_lower_bound, v_hg_norm_g,
                                    v_w_out, v_ffn_norm_g, v_w_gate_up, v_w_down, v_final_norm_g)))
    w = {k: _to_2d(k, a) for k, a in w_raw.items()}
    m = {k: _to_2d(k, a) for k, a in m_raw.items()}
    v = {k: _to_2d(k, a) for k, a in v_raw.items()}

    x_i, y_i, c_i = _place()
    core = jnp.reshape(c_i, (1,)).astype(jnp.int32)
    chip = jnp.reshape(2 * x_i + y_i, (1,)).astype(jnp.int32)
    chip_core = jnp.concatenate([chip, core])

    first_names, rest_names = ("w_in",), ("w_out", "w_gate_up", "w_down")
    placed, _ = _place_shards(w, [], chip, first_names, "place_first")
    first, _ = _gather_weights(placed, [], first_names, "gather_first", 1)
    placed, (meta_full, cw_full) = _place_shards(w, [w["meta_tokens"], w["conv_w"]], chip, rest_names, "place_shards")
    rest, _ = _gather_weights(placed, [], rest_names, "gather_rest", 2)
    full = {**first, **rest}

    seq = x.shape[1]
    small ={k: w[k] for k in SMALL_NAMES if k not in SHARDED_SMALL}
    small["conv_w"] = cw_full

    def reduce_to_chips(grads, names, tag, collective_ids):
        got = _exchange_halves(grads, names, "exchange_halves_" + tag, collective_ids[0])

        def chip_sums():
            return _chip_sum(grads, got, names, core, "chip_sum_" + tag)

        def send(sums):
            arrived = _send_chip_sums({n: sums[n][1] for n in names}, names, "send_chip_sums_" + tag,
                                      collective_ids[1])
            return {n: (sums[n][0], a) for n, a in zip(names, arrived)}

        return chip_sums, send

    ffn_names, mixer_names = ("w_gate_up", "w_down", "w_out"), ("w_in",)
    loss, grad_x, grads, parts, parts_mixer = _local_step(
        x.reshape(seq, D_MODEL), meta_full, loss_target.reshape(seq, D_MODEL),
        w["w_in"], full["w_in"], full["w_out"], full["w_gate_up"], full["w_down"], small, chip,
        on_ffn_grads=lambda g: reduce_to_chips(g, ffn_names, "ffn", (3, 4)),
        on_mixer_grads=lambda g: reduce_to_chips(g, mixer_names, "mixer", (None, 5)))
    parts.update(parts_mixer)
    totals = _total(parts, chip_core)
    pieces = {k: grads[k] for k in VEC_ROW if k != "loss"}
    pieces["loss"] = loss
    vec, gates, g_big = _all_reduce_small(pieces, grads["w_gates"], totals)
    loss_sum, res = _adamw_small(vec, gates, w, m, v)
    res.update(_adamw_big(w, g_big, m, v))

    out = [loss_sum.reshape(()), grad_x.reshape(1, seq, D_MODEL)]
    for j in range(4):
        out += [res[n][j].reshape(w_raw[n].shape) for n in WEIGHT_NAMES]
    return tuple(out)
```

```python
import math

import jax
import jax.numpy as jnp
from jax import lax
from jax.experimental import pallas as pl
from jax.experimental.pallas import tpu as pltpu
from jax.experimental.pallas import tpu_sc as plsc

F32 = jnp.float32
BF16 = jnp.bfloat16
MESH = pl.DeviceIdType.MESH

D_MODEL = 1024
D_RG = 512
RG_HEAD_DIM = 64
D_HG = 512
HG_HEAD_DIM = 128
HG_HEADS = 4
CHUNK = 64
SUB = 16
N_SUB = CHUNK // SUB
N_META = 16
PAD = CHUNK - N_META
D_IN = 3072
D_FF = 2816
CONV_W = 4
LRU_C = 8.0
EPS = 1e-6
EXP_CLAMP = 80.0
GELU_C = math.sqrt(2.0 / math.pi)
GELU_A = 0.044715
N_CHIPS = 4

ADAM_LR = 0.001
ADAM_B1 = 0.9
ADAM_B2 = 0.999
ADAM_EPS = 1e-08
ADAM_WD = 0.01
ADAM_STEP = 10

VMEM_LIMIT = 56 * 1024 * 1024


def _params(*sem):
    return pltpu.CompilerParams(dimension_semantics=sem, vmem_limit_bytes=VMEM_LIMIT)


def _row_tile(rows, target):
    best = None
    for t in range(16, min(rows, target) + 1, 16):
        if rows % t == 0:
            best = t
    assert best is not None, rows
    return best


def _sigmoid(x):
    return 0.5 * jnp.tanh(0.5 * x) + 0.5


def _dot(a, b):
    return jnp.dot(a, b, preferred_element_type=F32)


def _dot_nt(a, b):
    return lax.dot_general(a, b, (((1,), (1,)), ((), ())), preferred_element_type=F32)


def _dot_tn(a, b):
    return lax.dot_general(a, b, (((0,), (0,)), ((), ())), preferred_element_type=F32)


def _rms(x):
    return lax.rsqrt(jnp.mean(x * x, axis=-1, keepdims=True) + EPS)


def _rms_bwd(dn, n, r):
    return r * (dn - n * jnp.mean(dn * n, axis=-1, keepdims=True))


def _gelu_parts(x):
    t = jnp.tanh(GELU_C * (x + GELU_A * x * x * x))
    g = 0.5 * x * (1.0 + t)
    dg = 0.5 * (1.0 + t) + 0.5 * x * (1.0 - t * t) * GELU_C * (1.0 + 3.0 * GELU_A * x * x)
    return g, dg


def _softplus_neg(lam):
    e = jnp.exp(-jnp.abs(lam))
    w = 1.0 + e
    log1p = jnp.where(w == 1.0, e, jnp.log(w) * e / (w - 1.0))
    return jnp.maximum(-lam, 0.0) + log1p


def _head_mask():
    r = lax.broadcasted_iota(jnp.int32, (D_RG, D_RG), 0) // RG_HEAD_DIM
    c = lax.broadcasted_iota(jnp.int32, (D_RG, D_RG), 1) // RG_HEAD_DIM
    return r == c


def _head_fold():
    r = lax.broadcasted_iota(jnp.int32, (D_RG, RG_HEAD_DIM), 0) % RG_HEAD_DIM
    c = lax.broadcasted_iota(jnp.int32, (D_RG, RG_HEAD_DIM), 1)
    return (r == c).astype(F32)


def _gate_weights(w_r, w_i):
    def body(wr_ref, wi_ref, o_ref):
        fold = _head_fold()
        mask = _head_mask()
        for k, ref in enumerate((wr_ref, wi_ref)):
            full = _dot_nt(ref[...].astype(BF16), fold.astype(BF16))
            o_ref[:, k * D_RG:(k + 1) * D_RG] = jnp.where(mask, full, 0.0).astype(BF16)

    return pl.pallas_call(
        body, out_shape=jax.ShapeDtypeStruct((D_RG, 2 * D_RG), BF16), name="gate_weights",
    )(w_r, w_i)


HEAD = PAD + N_META


def _window_copies(seq_hbm, buf, sems, tm):
    def first(to_vmem):
        seq, vm = seq_hbm.at[pl.ds(0, tm - HEAD)], buf.at[0, pl.ds(HEAD, tm - HEAD)]
        return pltpu.make_async_copy(seq, vm, sems.at[0]) if to_vmem else pltpu.make_async_copy(vm, seq, sems.at[0])

    def later(j, slot, to_vmem):
        seq, vm = seq_hbm.at[pl.ds(pl.multiple_of(j * tm - HEAD, 8), tm)], buf.at[slot]
        if to_vmem:
            return pltpu.make_async_copy(seq, vm, sems.at[slot])
        return pltpu.make_async_copy(vm, seq, sems.at[slot])

    return first, later


def _fetch_window(seq_hbm, buf, sems, i, n_steps, tm):
    first, later = _window_copies(seq_hbm, buf, sems, tm)
    slot = i % 2

    @pl.when(i == 0)
    def _():
        first(True).start()

    if n_steps > 1:
        @pl.when(i + 1 < n_steps)
        def _():
            later(i + 1, 1 - slot, True).start()

    @pl.when(i == 0)
    def _():
        first(True).wait()

    if n_steps > 1:
        @pl.when(i > 0)
        def _():
            later(i, slot, True).wait()

    return slot


def _in_proj_local(x, meta, g1, w_own, chip):
    T = x.shape[0] + HEAD
    tm = _row_tile(T, 832)
    n_steps = T // tm
    cols = BIG["w_in"][1]

    def body(s_ref, x_hbm, meta_ref, g_ref, w_ref, p_ref, u_ref, h_ref, buf, sems, wb):
        i = pl.program_id(0)
        slot = _fetch_window(x_hbm, buf, sems, i, n_steps, tm)

        @pl.when(i == 0)
        def _():
            buf[0, 0:PAD, :] = jnp.zeros((PAD, D_MODEL), F32)
            buf[0, PAD:HEAD, :] = meta_ref[...]
            wb[...] = w_ref[...].astype(BF16)

        h = buf[slot]
        h_ref[...] = h
        u = (h * _rms(h) * g_ref[...]).astype(BF16)
        u_ref[...] = u
        p_ref[...] = _dot(u, wb[...])

    return pl.pallas_call(
        body,
        grid_spec=pltpu.PrefetchScalarGridSpec(
            num_scalar_prefetch=1, grid=(n_steps,),
            in_specs=[pl.BlockSpec(memory_space=pl.ANY),
                      pl.BlockSpec((N_META, D_MODEL), lambda i, s: (0, 0)),
                      pl.BlockSpec((1, D_MODEL), lambda i, s: (0, 0)),
                      pl.BlockSpec((D_MODEL, cols), lambda i, s: (0, 0))],
            out_specs=[pl.BlockSpec((tm, cols), lambda i, s: (i, s[0])),
                       pl.BlockSpec((tm, D_MODEL), lambda i, s: (i, 0)),
                       pl.BlockSpec((tm, D_MODEL), lambda i, s: (i, 0))],
            scratch_shapes=[pltpu.VMEM((2, tm, D_MODEL), F32), pltpu.SemaphoreType.DMA((2,)),
                            pltpu.VMEM((D_MODEL, cols), BF16)]),
        out_shape=[jax.ShapeDtypeStruct((T, D_IN), F32), jax.ShapeDtypeStruct((T, D_MODEL), BF16),
                   jax.ShapeDtypeStruct((T, D_MODEL), F32)],
        name="in_proj_local", compiler_params=_params("arbitrary"),
    )(chip, x, meta, g1, w_own)


def _in_proj_rest(u, w_in, p, chip):
    T = u.shape[0]
    tm = _row_tile(T, 2080)
    cols = BIG["w_in"][1]
    block = lambda j, s: (s[0] + 1 + j) % N_CHIPS

    def body(s_ref, u_ref, w_ref, p_in_ref, p_ref):
        p_ref[...] = _dot(u_ref[...], w_ref[...])

    return pl.pallas_call(
        body,
        grid_spec=pltpu.PrefetchScalarGridSpec(
            num_scalar_prefetch=1, grid=(N_CHIPS - 1, T // tm),
            in_specs=[pl.BlockSpec((tm, D_MODEL), lambda j, i, s: (i, 0)),
                      pl.BlockSpec((D_MODEL, cols), lambda j, i, s: (0, block(j, s))), ANY],
            out_specs=pl.BlockSpec((tm, cols), lambda j, i, s: (i, block(j, s)))),
        out_shape=jax.ShapeDtypeStruct((T, D_IN), F32),
        input_output_aliases={3: 0},
        name="in_proj_rest", compiler_params=_params("arbitrary", "arbitrary"),
    )(chip, u, w_in, p)


def _scan_block_fwd(A, B, rowi):
    for d in (1, 2, 4):
        a_sh = pltpu.roll(A, d, axis=0)
        b_sh = pltpu.roll(B, d, axis=0)
        m = rowi >= d
        B = jnp.where(m, A * b_sh + B, B)
        A = jnp.where(m, A * a_sh, A)
    return A, B


def _scan_block_bwd(A, B, rowi):
    for d in (1, 2, 4):
        a_sh = pltpu.roll(A, 8 - d, axis=0)
        b_sh = pltpu.roll(B, 8 - d, axis=0)
        m = rowi < 8 - d
        B = jnp.where(m, A * b_sh + B, B)
        A = jnp.where(m, A * a_sh, A)
    return A, B


def _rg_gates(xc, w_ref, bg_ref, lam):
    pre = _dot(xc.astype(BF16), w_ref[...]) + bg_ref[...]
    r = _sigmoid(pre[:, :D_RG])
    ig = _sigmoid(pre[:, D_RG:])
    sp = _softplus_neg(lam)
    la = -LRU_C * sp * r
    a = jnp.exp(la)
    th = jnp.tanh(la)
    u = 1.0 - th
    rc = pl.reciprocal(u, approx=True)
    rc = rc * (2.0 - u * rc)
    rc = rc * (2.0 - u * rc)
    m2 = -2.0 * th * rc
    inv_m = lax.rsqrt(jnp.maximum(m2, 1e-30))
    return r, ig, sp, a, m2 * inv_m, inv_m


def _conv(ext, cw_ref, cb_ref, tm):
    xc = cb_ref[...] + cw_ref[0:1, :] * ext[8 - 3:8 - 3 + tm, :]
    for j in range(1, CONV_W):
        xc = xc + cw_ref[j:j + 1, :] * ext[8 - 3 + j:8 - 3 + j + tm, :]
    return xc


def _scan_unroll(blocks):
    return 4 if blocks % 4 == 0 else 2 if blocks % 2 == 0 else 1


def _rg_fwd(p, cw, cb, wg, bg, lam, rg_g):
    T = p.shape[0]
    tm = _row_tile(T, 832)
    unroll = _scan_unroll(tm // 8)

    def body(xg_ref, cw_ref, cb_ref, w_ref, bg_ref, lam_ref, g_ref, y_ref, h_ref, xc_ref, ext, a_s, b_s, carry):
        i = pl.program_id(0)

        @pl.when(i == 0)
        def _():
            ext[0:8, :] = jnp.zeros((8, D_RG), F32)
            carry[...] = jnp.zeros((1, D_RG), F32)

        ext[8:8 + tm, :] = xg_ref[:, :D_RG]
        xc = _conv(ext, cw_ref, cb_ref, tm)
        xc_ref[...] = xc
        r, ig, sp, a, m, _ = _rg_gates(xc, w_ref, bg_ref, lam_ref[...])
        row = i * tm + lax.broadcasted_iota(jnp.int32, (tm, 1), 0)
        a_s[...] = a
        b_s[...] = jnp.where(row >= PAD, m * ig * xc, 0.0)
        rowi = lax.broadcasted_iota(jnp.int32, (8, D_RG), 0)

        def blk(j, c):
            for u in range(unroll):
                o = pl.multiple_of((j * unroll + u) * 8, 8)
                A, B = _scan_block_fwd(a_s[pl.ds(o, 8), :], b_s[pl.ds(o, 8), :], rowi)
                h = B + A * c
                h_ref[pl.ds(o, 8), :] = h
                c = h[7:8, :]
            return c

        carry[...] = lax.fori_loop(0, tm // (8 * unroll), blk, carry[...])
        ext[0:8, :] = ext[tm:tm + 8, :]
        g, _ = _gelu_parts(xg_ref[:, D_RG:])
        yy = g * h_ref[...]
        y_ref[...] = (yy * _rms(yy) * g_ref[...]).astype(BF16)

    vec = lambda n: pl.BlockSpec((1, n), lambda i: (0, 0))
    return pl.pallas_call(
        body, grid=(T // tm,),
        in_specs=[pl.BlockSpec((tm, 2 * D_RG), lambda i: (i, 0)),
                  pl.BlockSpec((CONV_W, D_RG), lambda i: (0, 0)), vec(D_RG),
                  pl.BlockSpec((D_RG, 2 * D_RG), lambda i: (0, 0)), vec(2 * D_RG), vec(D_RG), vec(D_RG)],
        out_specs=[pl.BlockSpec((tm, D_RG), lambda i: (i, 0))] * 3,
        out_shape=[jax.ShapeDtypeStruct((T, D_RG), BF16), jax.ShapeDtypeStruct((T, D_RG), F32),
                   jax.ShapeDtypeStruct((T, D_RG), F32)],
        scratch_shapes=[pltpu.VMEM((tm + 8, D_RG), F32), pltpu.VMEM((tm, D_RG), F32),
                        pltpu.VMEM((tm, D_RG), F32), pltpu.VMEM((1, D_RG), F32)],
        name="rg_fwd", compiler_params=_params("arbitrary"),
    )(p, cw, cb, wg, bg, lam, rg_g)


def _running_sum(x, down):
    r = lax.broadcasted_iota(jnp.int32, (CHUNK, CHUNK), 0)
    c = lax.broadcasted_iota(jnp.int32, (CHUNK, CHUNK), 1)
    tri = ((c <= r) if down else (c >= r)).astype(BF16)
    hi = x.astype(BF16)
    rest = x - hi.astype(F32)
    mid = rest.astype(BF16)
    lo = (rest - mid.astype(F32)).astype(BF16)
    return (_dot(tri, hi) + _dot(tri, mid)) + _dot(tri, lo)


def _hg_gates(hq, hf, lbraw_ref, valid):
    lb = _sigmoid(lbraw_ref[0:1, :] - lbraw_ref[1:2, :])
    sq = _sigmoid(hq)
    q = hq * sq
    sf = _sigmoid(hf)
    f = lb + (1.0 - lb) * sf
    lf = jnp.where(valid, jnp.log(f), 0.0)
    b = _running_sum(lf, True)
    return lb, sq, q, sf, f, b


def _hg_head(qh, kh, bh):
    b_last = bh[CHUNK - 1:CHUNK, :]
    refs = [bh[SUB * s:SUB * s + 1, :] for s in range(N_SUB)]
    r_sel = jnp.concatenate([jnp.broadcast_to(refs[s], (SUB, HG_HEAD_DIM)) for s in range(N_SUB)], axis=0)
    eb = jnp.exp(bh)
    eq = jnp.exp(bh - r_sel)
    ekh = jnp.exp(b_last - bh)
    ek = [jnp.exp(jnp.minimum(refs[s] - bh[:SUB * (s + 1), :], EXP_CLAMP)) for s in range(N_SUB)]
    qe = qh * eq

    def own_rows(s):
        parts = [jnp.zeros((SUB * s, HG_HEAD_DIM), F32)] if s else []
        parts.append(qe[SUB * s:SUB * (s + 1), :])
        if s < N_SUB - 1:
            parts.append(jnp.zeros((CHUNK - SUB * (s + 1), HG_HEAD_DIM), F32))
        return jnp.concatenate(parts, axis=0)

    q_hat = jnp.concatenate([own_rows(s) for s in range(N_SUB)], axis=1)

    def met_rows(s):
        n = SUB * (s + 1)
        ke = kh[:n, :] * ek[s]
        return ke if n == CHUNK else jnp.concatenate([ke, jnp.zeros((CHUNK - n, HG_HEAD_DIM), F32)], axis=0)

    k_til = jnp.concatenate([met_rows(s) for s in range(N_SUB)], axis=1)
    return b_last, eb, eq, ekh, ek, q_hat, k_til


def _causal():
    r = lax.broadcasted_iota(jnp.int32, (CHUNK, CHUNK), 0)
    c = lax.broadcasted_iota(jnp.int32, (CHUNK, CHUNK), 1)
    return r >= c


def _chunks_per_step(n_chunks):
    for c in (5, 4, 3, 2):
        if n_chunks % c == 0:
            return c
    return 1


def _hg_fwd(p, lbraw, hg_g):
    T = p.shape[0]
    n_chunks = T // CHUNK
    cps = _chunks_per_step(n_chunks)
    rows = cps * CHUNK

    def body(hq_ref, hf_ref, hi_ref, hg_ref, lb_ref, g_ref, y_ref, o_ref, st_all_ref, st):
        i = pl.program_id(0)

        @pl.when(i == 0)
        def _():
            st[...] = jnp.zeros_like(st)

        def chunk(j, carry):
            rs = pl.ds(pl.multiple_of(j * CHUNK, CHUNK), CHUNK)
            chunk_body(i * cps + j, hq_ref.at[rs, :], hf_ref.at[rs, :], hi_ref.at[rs, :], hg_ref.at[rs, :], lb_ref,
                       g_ref, y_ref.at[rs, :], o_ref.at[rs, :], st_all_ref.at[pl.ds(j, 1)], st)
            return carry

        lax.fori_loop(0, cps, chunk, 0, unroll=True)

    def chunk_body(n, hq_ref, hf_ref, hi_ref, hg_ref, lb_ref, g_ref, y_ref, o_ref, st_all_ref, st):
        valid = (n * CHUNK + lax.broadcasted_iota(jnp.int32, (CHUNK, 1), 0)) >= PAD
        hq, hf, v, hg = hq_ref[...], hf_ref[...], hi_ref[...], hg_ref[...]
        lb, sq, q, sf, f, b = _hg_gates(hq, hf, lb_ref, valid)
        k = 1.0 - f
        st_all_ref[0] = st[...]
        causal = _causal()
        v_t = v.T.astype(BF16)
        heads = [slice(h * HG_HEAD_DIM, (h + 1) * HG_HEAD_DIM) for h in range(HG_HEADS)]
        fac = []
        for sl in heads:
            qh, kh, bh = q[:, sl], k[:, sl], b[:, sl]
            b_last, eb, _, ekh, _, q_hat, k_til = _hg_head(qh, kh, bh)
            fac.append((jnp.exp(b_last), (qh * eb).astype(BF16), q_hat.astype(BF16), k_til.astype(BF16),
                        (kh * ekh).astype(BF16), v[:, sl].astype(BF16)))
        raw = []
        for sl, (_, q_til, q_hat, k_til, k_hat, _) in zip(heads, fac):
            st_h = st[sl, :]
            raw.append((_dot_nt(q_til, st_h.astype(BF16)), _dot_nt(q_hat, k_til), _dot(v_t[sl, :], k_hat), st_h))
        for sl, (e_last, _, _, _, _, vb), (inter, att, upd, st_h) in zip(heads, fac, raw):
            o = inter + _dot(jnp.where(causal, att, 0.0).astype(BF16), vb)
            st[sl, :] = st_h * e_last + upd
            o_ref[:, sl] = o
            hgh = hg[:, sl]
            y_ref[:, sl] = (o * _rms(o) * g_ref[...] * (hgh * _sigmoid(hgh))).astype(BF16)

    col = lambda j: pl.BlockSpec((rows, D_HG), lambda n: (n, j))
    return pl.pallas_call(
        body, grid=(n_chunks // cps,),
        in_specs=[col(2), col(3), col(4), col(5),
                  pl.BlockSpec((2, D_HG), lambda n: (0, 0)), pl.BlockSpec((1, HG_HEAD_DIM), lambda n: (0, 0))],
        out_specs=[pl.BlockSpec((rows, D_HG), lambda n: (n, 0)), pl.BlockSpec((rows, D_HG), lambda n: (n, 0)),
                   pl.BlockSpec((cps, D_HG, HG_HEAD_DIM), lambda n: (n, 0, 0))],
        out_shape=[jax.ShapeDtypeStruct((T, D_HG), BF16), jax.ShapeDtypeStruct((T, D_HG), F32),
                   jax.ShapeDtypeStruct((n_chunks, D_HG, HG_HEAD_DIM), F32)],
        scratch_shapes=[pltpu.VMEM((D_HG, HG_HEAD_DIM), F32)],
        name="hg_fwd", compiler_params=_params("arbitrary"),
    )(p, p, p, p, lbraw, hg_g)


def _ffn_fwd(h0, y_rg, y_hg, w_out, g2, w_gu, w_down, gf, target):
    T = h0.shape[0]
    tm = _row_tile(T, 320)
    n_steps = T // tm

    def body(h_ref, yr_ref, yh_ref, wo_ref, g2_ref, wgu_ref, wd_ref, gf_ref, t_hbm,
             h1_ref, v_ref, y_ref, gu_ref, act_ref, dh2_ref, dh2b_ref, loss_ref, gg_ref, tbuf, sems):
        i = pl.program_id(0)
        slot = _fetch_window(t_hbm, tbuf, sems, i, n_steps, tm)

        @pl.when(i == 0)
        def _():
            loss_ref[...] = jnp.zeros_like(loss_ref)
            gg_ref[...] = jnp.zeros_like(gg_ref)
            tbuf[0, 0:HEAD, :] = jnp.zeros((HEAD, D_MODEL), F32)

        y_ref[:, :D_RG] = yr_ref[...]
        y_ref[:, D_RG:] = yh_ref[...]
        h1 = h_ref[...] + _dot(y_ref[...], wo_ref[...])
        h1_ref[...] = h1
        v = (h1 * _rms(h1) * g2_ref[...]).astype(BF16)
        v_ref[...] = v

        gu = _dot(v, wgu_ref[...])
        gu_ref[...] = gu.astype(BF16)
        g = gu[:, :D_FF]
        act = (g * _sigmoid(g) * gu[:, D_FF:]).astype(BF16)
        act_ref[...] = act

        h2 = h1 + _dot(act, wd_ref[...])
        r = _rms(h2)
        n = h2 * r
        gf_ = gf_ref[...]
        row = i * tm + lax.broadcasted_iota(jnp.int32, (tm, 1), 0)
        err = jnp.where(row >= HEAD, n * gf_ - tbuf[slot], 0.0)
        loss_ref[...] += 0.5 * jnp.sum(jnp.mean(err * err, axis=-1, keepdims=True), axis=0, keepdims=True)
        dy = err * (1.0 / D_MODEL)
        gg_ref[...] += jnp.sum(dy * n, axis=0, keepdims=True)
        dh2 = _rms_bwd(dy * gf_, n, r)
        dh2_ref[...] = dh2
        dh2b_ref[...] = dh2.astype(BF16)

    row_spec = lambda n: pl.BlockSpec((tm, n), lambda i: (i, 0))
    vec = pl.BlockSpec((1, D_MODEL), lambda i: (0, 0))
    return pl.pallas_call(
        body, grid=(n_steps,),
        in_specs=[row_spec(D_MODEL), row_spec(D_RG), row_spec(D_HG), _resident((D_MODEL, D_MODEL)), vec,
                  _resident((D_MODEL, 2 * D_FF)), _resident((D_FF, D_MODEL)), vec,
                  pl.BlockSpec(memory_space=pl.ANY)],
        out_specs=[row_spec(D_MODEL), row_spec(D_MODEL), row_spec(D_MODEL), row_spec(2 * D_FF), row_spec(D_FF),
                   row_spec(D_MODEL), row_spec(D_MODEL), pl.BlockSpec((1, 1), lambda i: (0, 0)), vec],
        out_shape=[jax.ShapeDtypeStruct((T, D_MODEL), F32), jax.ShapeDtypeStruct((T, D_MODEL), BF16),
                   jax.ShapeDtypeStruct((T, D_MODEL), BF16), jax.ShapeDtypeStruct((T, 2 * D_FF), BF16),
                   jax.ShapeDtypeStruct((T, D_FF), BF16), jax.ShapeDtypeStruct((T, D_MODEL), F32),
                   jax.ShapeDtypeStruct((T, D_MODEL), BF16), jax.ShapeDtypeStruct((1, 1), F32),
                   jax.ShapeDtypeStruct((1, D_MODEL), F32)],
        scratch_shapes=[pltpu.VMEM((2, tm, D_MODEL), F32), pltpu.SemaphoreType.DMA((2,))],
        name="ffn_fwd", compiler_params=_params("arbitrary"),
    )(h0, y_rg, y_hg, w_out, g2, w_gu, w_down, gf, target)


def _resident(shape):
    return pl.BlockSpec(shape, lambda i: (0,) * len(shape), pipeline_mode=pl.Buffered(1))


def _ffn_bwd(dh2b, gu, w_down, w_gu, h1, g2, dh2, w_out):
    T = h1.shape[0]
    tm = _row_tile(T, 320)

    def body(d_ref, gu_ref, wd_ref, wgu_ref, h_ref, g_ref, d2_ref, wo_ref, dgu_ref, dh1_ref, dh1b_ref, dy_ref, gg_ref):
        i = pl.program_id(0)

        @pl.when(i == 0)
        def _():
            gg_ref[...] = jnp.zeros_like(gg_ref)

        dact = _dot_nt(d_ref[...], wd_ref[...]).astype(BF16)
        g = gu_ref[:, :D_FF]
        u = gu_ref[:, D_FF:]
        s = _sigmoid(g)
        dgu_ref[:, :D_FF] = dact * u * (s * (1.0 + g * (1.0 - s)))
        dgu_ref[:, D_FF:] = dact * (g * s)

        dv = _dot_nt(dgu_ref[...], wgu_ref[...])
        h1_ = h_ref[...]
        r = _rms(h1_)
        n = h1_ * r
        gg_ref[...] += jnp.sum(dv * n, axis=0, keepdims=True)
        dh1 = d2_ref[...] + _rms_bwd(dv * g_ref[...], n, r)
        dh1_ref[...] = dh1
        db = dh1.astype(BF16)
        dh1b_ref[...] = db
        dy_ref[...] = _dot_nt(db, wo_ref[...])

    row = lambda n: pl.BlockSpec((tm, n), lambda i: (i, 0))
    return pl.pallas_call(
        body, grid=(T // tm,),
        in_specs=[row(D_MODEL), row(2 * D_FF), _resident((D_FF, D_MODEL)), _resident((D_MODEL, 2 * D_FF)),
                  row(D_MODEL), pl.BlockSpec((1, D_MODEL), lambda i: (0, 0)), row(D_MODEL),
                  _resident((D_MODEL, D_MODEL))],
        out_specs=[row(2 * D_FF), row(D_MODEL), row(D_MODEL), row(D_MODEL),
                   pl.BlockSpec((1, D_MODEL), lambda i: (0, 0))],
        out_shape=[jax.ShapeDtypeStruct((T, 2 * D_FF), BF16), jax.ShapeDtypeStruct((T, D_MODEL), F32),
                   jax.ShapeDtypeStruct((T, D_MODEL), BF16), jax.ShapeDtypeStruct((T, D_MODEL), F32),
                   jax.ShapeDtypeStruct((1, D_MODEL), F32)],
        name="ffn_bwd", compiler_params=_params("arbitrary"),
    )(dh2b, gu, w_down, w_gu, h1, g2, dh2, w_out)


def _rg_bwd(p, xc_all, hs, dy, dp, cw, cb, wg, bg, lam, rg_g):
    T = p.shape[0]
    tm = _row_tile(T, 832)
    nt = T // tm
    hb = tm // 8
    unroll = _scan_unroll(hb)

    def body(xg_ref, xc_ref, h_ref, hh_ref, dy_ref, dp_in_ref, cw_ref, cb_ref, w_ref, bg_ref, lam_ref, g_ref,
             dp_ref, gcw_ref, gcb_ref, gw_ref, gbg_ref, glam_ref, gg_ref,
             dext, a_s, b_s, d_s, gacc, carry_d, carry_a):
        i = pl.program_id(0)
        t_idx = nt - 1 - i

        @pl.when(i == 0)
        def _():
            dext[tm:tm + 8, :] = jnp.zeros((8, D_RG), F32)
            carry_d[...] = jnp.zeros_like(carry_d)
            carry_a[...] = jnp.zeros_like(carry_a)
            gacc[...] = jnp.zeros_like(gacc)
            for ref in (gcw_ref, gcb_ref, gbg_ref, glam_ref, gg_ref, gw_ref):
                ref[...] = jnp.zeros_like(ref)

        first = t_idx == 0
        xc = xc_ref[...]
        lam_ = lam_ref[...]
        r, ig, sp, a, m, inv_m = _rg_gates(xc, w_ref, bg_ref, lam_)
        row = t_idx * tm + lax.broadcasted_iota(jnp.int32, (tm, 1), 0)
        valid = row >= PAD

        gr = xg_ref[:, D_RG:]
        g, dgelu = _gelu_parts(gr)
        h = h_ref[...]
        yy = g * h
        rr = _rms(yy)
        nn = yy * rr
        dy_ = dy_ref[...]
        gg_ref[...] += jnp.sum(dy_ * nn, axis=0, keepdims=True)
        dyy = _rms_bwd(dy_ * g_ref[...], nn, rr)
        dp_ref[:, D_RG:] = (dyy * h * dgelu).astype(BF16)

        a_s[...] = a
        b_s[...] = dyy * g
        rowi = lax.broadcasted_iota(jnp.int32, (8, D_RG), 0)

        def blk(jj, c):
            cd, ca = c
            for u in range(unroll):
                o = pl.multiple_of((hb - 1 - (jj * unroll + u)) * 8, 8)
                a_blk = a_s[pl.ds(o, 8), :]
                a_next = jnp.where(rowi == 7, ca, pltpu.roll(a_blk, 7, axis=0))
                A, B = _scan_block_bwd(a_next, b_s[pl.ds(o, 8), :], rowi)
                d = B + A * cd
                d_s[pl.ds(o, 8), :] = d
                cd, ca = d[0:1, :], a_blk[0:1, :]
            return cd, ca

        cd, ca = lax.fori_loop(0, hb // unroll, blk, (carry_d[...], carry_a[...]))
        carry_d[...] = cd
        carry_a[...] = ca
        delta = d_s[...]

        h_last_prev = jnp.where(first, 0.0, hh_ref[7:8, :])
        row0 = lax.broadcasted_iota(jnp.int32, (tm, 1), 0) == 0
        h_prev = jnp.where(row0, h_last_prev, pltpu.roll(h, 1, axis=0))
        dbx = jnp.where(valid, delta, 0.0)
        da = delta * h_prev
        di = dbx * m * xc
        dm = dbx * ig * xc
        dla = a * (da - dm * a * inv_m)
        dla = jnp.where(valid, dla, 0.0)
        glam_ref[...] += jnp.sum(dla * r, axis=0, keepdims=True) * (LRU_C / (1.0 + jnp.exp(lam_)))
        dr = (-LRU_C) * sp * dla
        dpre = jnp.concatenate([dr * r * (1.0 - r), di * ig * (1.0 - ig)], axis=1)
        gbg_ref[...] += jnp.sum(dpre, axis=0, keepdims=True)
        dpre_b = dpre.astype(BF16)
        gacc[...] += _dot_tn(xc.astype(BF16), dpre_b)
        dxc = dbx * m * ig + _dot_nt(dpre_b, w_ref[...])
        gcb_ref[...] += jnp.sum(dxc, axis=0, keepdims=True)
        dext[0:tm, :] = dxc
        xr = xg_ref[:, :D_RG]
        dxr = None
        for j in range(CONV_W):
            shifted = dext[3 - j:3 - j + tm, :]
            gcw_ref[j:j + 1, :] += jnp.sum(xr * shifted, axis=0, keepdims=True)
            tap = cw_ref[j:j + 1, :] * shifted
            dxr = tap if dxr is None else dxr + tap
        dp_ref[:, :D_RG] = dxr.astype(BF16)
        dext[tm:tm + 8, :] = dext[0:8, :]

        @pl.when(i == nt - 1)
        def _():
            fold = _head_fold()
            mask = _head_mask()
            fold_b = fold.astype(BF16)
            for k in range(2):
                blockdiag = jnp.where(mask, gacc[:, k * D_RG:(k + 1) * D_RG], 0.0)
                hi = blockdiag.astype(BF16)
                rest = blockdiag - hi.astype(F32)
                mid = rest.astype(BF16)
                lo = (rest - mid.astype(F32)).astype(BF16)
                gw_ref[k * D_RG:(k + 1) * D_RG, :] = (_dot(hi, fold_b) + _dot(mid, fold_b)) + _dot(lo, fold_b)

    vec = lambda n: pl.BlockSpec((1, n), lambda i: (0, 0))
    rev = lambda n: pl.BlockSpec((tm, n), lambda i: (nt - 1 - i, 0))
    halo = lambda n: pl.BlockSpec((8, n), lambda i: (jnp.maximum((nt - 1 - i) * hb - 1, 0), 0))
    return pl.pallas_call(
        body, grid=(nt,),
        in_specs=[rev(2 * D_RG), rev(D_RG), rev(D_RG), halo(D_RG), rev(D_RG), ANY,
                  pl.BlockSpec((CONV_W, D_RG), lambda i: (0, 0)), vec(D_RG),
                  pl.BlockSpec((D_RG, 2 * D_RG), lambda i: (0, 0)), vec(2 * D_RG), vec(D_RG), vec(D_RG)],
        out_specs=[rev(2 * D_RG), pl.BlockSpec((CONV_W, D_RG), lambda i: (0, 0)), vec(D_RG),
                   pl.BlockSpec((2 * D_RG, RG_HEAD_DIM), lambda i: (0, 0)), vec(2 * D_RG), vec(D_RG), vec(D_RG)],
        input_output_aliases={5: 0},
        out_shape=[jax.ShapeDtypeStruct((T, D_IN), BF16), jax.ShapeDtypeStruct((CONV_W, D_RG), F32),
                   jax.ShapeDtypeStruct((1, D_RG), F32), jax.ShapeDtypeStruct((2 * D_RG, RG_HEAD_DIM), F32),
                   jax.ShapeDtypeStruct((1, 2 * D_RG), F32), jax.ShapeDtypeStruct((1, D_RG), F32),
                   jax.ShapeDtypeStruct((1, D_RG), F32)],
        scratch_shapes=[pltpu.VMEM((tm + 8, D_RG), F32),
                        pltpu.VMEM((tm, D_RG), F32), pltpu.VMEM((tm, D_RG), F32), pltpu.VMEM((tm, D_RG), F32),
                        pltpu.VMEM((D_RG, 2 * D_RG), F32), pltpu.VMEM((1, D_RG), F32), pltpu.VMEM((1, D_RG), F32)],
        name="rg_bwd", compiler_params=_params("arbitrary"),
    )(p, xc_all, hs, hs, dy, dp, cw, cb, wg, bg, lam, rg_g)


def _hg_bwd(p, o_all, st_all, dy, lbraw, hg_g):
    T = p.shape[0]
    n_chunks = T // CHUNK
    cps = _chunks_per_step(n_chunks)
    rows = cps * CHUNK
    n_steps = n_chunks // cps

    def body(hq_ref, hf_ref, hi_ref, hg_ref, o_ref, st_ref, dy_ref, lb_ref, g_ref,
             dp_ref, glb_ref, gg_ref, dst):
        i = pl.program_id(0)

        @pl.when(i == 0)
        def _():
            dst[...] = jnp.zeros_like(dst)
            glb_ref[...] = jnp.zeros_like(glb_ref)
            gg_ref[...] = jnp.zeros_like(gg_ref)

        dp_ref[:, :2 * D_RG] = jnp.zeros((rows, 2 * D_RG), BF16)

        def chunk(jj, carry):
            j = cps - 1 - jj
            rs = pl.ds(pl.multiple_of(j * CHUNK, CHUNK), CHUNK)
            chunk_body((n_steps - 1 - i) * cps + j, hq_ref.at[rs, :], hf_ref.at[rs, :], hi_ref.at[rs, :],
                       hg_ref.at[rs, :], o_ref.at[rs, :], st_ref.at[pl.ds(j, 1)], dy_ref.at[rs, :], lb_ref, g_ref,
                       dp_ref.at[rs, pl.ds(2 * D_RG, 4 * D_HG)], glb_ref, gg_ref, dst)
            return carry

        lax.fori_loop(0, cps, chunk, 0, unroll=True)

    def chunk_body(n, hq_ref, hf_ref, hi_ref, hg_ref, o_ref, st_ref, dy_ref, lb_ref, g_ref,
                   dp_ref, glb_ref, gg_ref, dst):
        valid = (n * CHUNK + lax.broadcasted_iota(jnp.int32, (CHUNK, 1), 0)) >= PAD
        hq, hf, v, hg = hq_ref[...], hf_ref[...], hi_ref[...], hg_ref[...]
        lb, sq, q, sf, f, b = _hg_gates(hq, hf, lb_ref, valid)
        k = 1.0 - f
        causal = _causal()
        r_i = lax.broadcasted_iota(jnp.int32, (CHUNK, CHUNK), 0)
        c_i = lax.broadcasted_iota(jnp.int32, (CHUNK, CHUNK), 1)
        causal_t = r_i <= c_i
        is_last = lax.broadcasted_iota(jnp.int32, (CHUNK, 1), 0) == CHUNK - 1
        g_ = g_ref[...]
        db_parts, dq_parts, dk_parts = [], [], []
        gg = jnp.zeros((1, HG_HEAD_DIM), F32)
        heads = [slice(h * HG_HEAD_DIM, (h + 1) * HG_HEAD_DIM) for h in range(HG_HEADS)]

        do_parts = []
        for h, sl in enumerate(heads):
            o = o_ref[:, sl]
            ro = _rms(o)
            no = o * ro
            hgh = hg[:, sl]
            sg = _sigmoid(hgh)
            dyh = dy_ref[:, sl]
            dp_ref[:, 3 * D_HG + h * HG_HEAD_DIM:3 * D_HG + (h + 1) * HG_HEAD_DIM] = (
                dyh * no * g_ * sg * (1.0 + hgh * (1.0 - sg))).astype(BF16)
            dng = dyh * hgh * sg
            gg = gg + jnp.sum(dng * no, axis=0, keepdims=True)
            do_parts.append(_rms_bwd(dng * g_, no, ro))
        do_t = jnp.concatenate(do_parts, axis=1).T.astype(BF16)

        fac = []
        for sl, do in zip(heads, do_parts):
            qh, kh, bh = q[:, sl], k[:, sl], b[:, sl]
            b_last, eb, eq, ekh, ek, q_hat, k_til = _hg_head(qh, kh, bh)
            fac.append(dict(qh=qh, kh=kh, e_last=jnp.exp(b_last), eb=eb, eq=eq, ekh=ekh, ek=ek,
                            q_til=qh * eb, k_hat=kh * ekh, qhb=q_hat.astype(BF16), ktb=k_til.astype(BF16),
                            vb=v[:, sl].astype(BF16), dob=do.astype(BF16)))

        first = []
        for sl, t in zip(heads, fac):
            st_h = st_ref[0, sl, :]
            dst_h = dst[sl, :]
            dstb = dst_h.astype(BF16)
            first.append(dict(
                att_t=_dot_nt(t["ktb"], t["qhb"]), datt=_dot_nt(t["dob"], t["vb"]),
                datt_t=_dot_nt(t["vb"], t["dob"]), dk_hat=_dot(t["vb"], dstb),
                dv=_dot_nt(t["k_hat"].astype(BF16), dstb), dq_til=_dot(t["dob"], st_h.astype(BF16)),
                state=t["e_last"] * jnp.sum(dst_h * st_h, axis=0, keepdims=True)))
            dst[sl, :] = dst_h * t["e_last"] + _dot(do_t[sl, :], t["q_til"].astype(BF16))

        for h, (t, m) in enumerate(zip(fac, first)):
            qh, kh, eb, eq, ekh, ek = t["qh"], t["kh"], t["eb"], t["eq"], t["ekh"], t["ek"]
            q_til, k_hat, qhb, ktb, dob = t["q_til"], t["k_hat"], t["qhb"], t["ktb"], t["dob"]
            dk_hat, dq_til = m["dk_hat"], m["dq_til"]
            dv = m["dv"] + _dot(jnp.where(causal_t, m["att_t"], 0.0).astype(BF16), dob)
            dq_hat = _dot(jnp.where(causal, m["datt"], 0.0).astype(BF16), ktb)
            dk_til = _dot(jnp.where(causal_t, m["datt_t"], 0.0).astype(BF16), qhb)
            db_last = jnp.sum(dk_hat * k_hat, axis=0, keepdims=True) + m["state"]
            dq_sel = jnp.concatenate([dq_hat[SUB * s:SUB * (s + 1), s * HG_HEAD_DIM:(s + 1) * HG_HEAD_DIM]
                                      for s in range(N_SUB)], axis=0)
            dq_a = dq_sel * eq
            dk_rows, k_att_rows = [], []
            for b_ in range(N_SUB):
                rs = slice(SUB * b_, SUB * (b_ + 1))
                dk_sum = k_att_sum = None
                for s in range(b_, N_SUB):
                    cs = slice(s * HG_HEAD_DIM, (s + 1) * HG_HEAD_DIM)
                    d = dk_til[rs, cs]
                    t_dk = d * ek[s][rs, :]
                    t_att = ktb[rs, cs].astype(F32) * d
                    dk_sum = t_dk if dk_sum is None else dk_sum + t_dk
                    k_att_sum = t_att if k_att_sum is None else k_att_sum + t_att
                dk_rows.append(dk_sum)
                k_att_rows.append(k_att_sum)
            dk_a = jnp.concatenate(dk_rows, axis=0)
            db = (dq_til * q_til - dk_hat * k_hat + (qh * eq).astype(BF16).astype(F32) * dq_sel
                  - jnp.concatenate(k_att_rows, axis=0))
            db_parts.append(jnp.where(is_last, db + db_last, db))
            dq_parts.append(dq_til * eb + dq_a)
            dk_parts.append(dk_hat * ekh + dk_a)
            dp_ref[:, 2 * D_HG + h * HG_HEAD_DIM:2 * D_HG + (h + 1) * HG_HEAD_DIM] = dv.astype(BF16)

        gg_ref[...] += gg
        db = jnp.concatenate(db_parts, axis=1)
        dq = jnp.concatenate(dq_parts, axis=1)
        dk = jnp.concatenate(dk_parts, axis=1)
        dlf = jnp.where(valid, _running_sum(db, False), 0.0)
        dp_ref[:, :D_HG] = (dq * sq * (1.0 + hq * (1.0 - sq))).astype(BF16)
        df = dlf / f - dk
        dlb = jnp.sum(df * (1.0 - sf), axis=0, keepdims=True) * lb * (1.0 - lb)
        glb_ref[0:1, :] += dlb
        glb_ref[1:2, :] += -dlb
        dp_ref[:, D_HG:2 * D_HG] = (df * (1.0 - lb) * sf * (1.0 - sf)).astype(BF16)

    rev = lambda j: pl.BlockSpec((rows, D_HG), lambda i: (n_steps - 1 - i, j))
    return pl.pallas_call(
        body, grid=(n_steps,),
        in_specs=[rev(2), rev(3), rev(4), rev(5), rev(0),
                  pl.BlockSpec((cps, D_HG, HG_HEAD_DIM), lambda i: (n_steps - 1 - i, 0, 0)), rev(1),
                  pl.BlockSpec((2, D_HG), lambda i: (0, 0)), pl.BlockSpec((1, HG_HEAD_DIM), lambda i: (0, 0))],
        out_specs=[pl.BlockSpec((rows, D_IN), lambda i: (n_steps - 1 - i, 0)),
                   pl.BlockSpec((2, D_HG), lambda i: (0, 0)), pl.BlockSpec((1, HG_HEAD_DIM), lambda i: (0, 0))],
        out_shape=[jax.ShapeDtypeStruct((T, D_IN), BF16), jax.ShapeDtypeStruct((2, D_HG), F32),
                   jax.ShapeDtypeStruct((1, HG_HEAD_DIM), F32)],
        scratch_shapes=[pltpu.VMEM((D_HG, HG_HEAD_DIM), F32)],
        name="hg_bwd", compiler_params=_params("arbitrary"),
    )(p, p, p, p, o_all, st_all, dy, lbraw, hg_g)


def _in_bwd(dp, w_in, h0, g1, dh1):
    T = h0.shape[0]
    tm = _row_tile(T, 832)
    n_steps = T // tm

    def body(dp_ref, w_ref, h_ref, g_ref, d1_ref, gx_hbm, gmeta_ref, gg_ref, buf, sems):
        i = pl.program_id(0)
        first, later = _window_copies(gx_hbm, buf, sems, tm)
        slot = i % 2

        @pl.when(i == 0)
        def _():
            gg_ref[...] = jnp.zeros_like(gg_ref)

        if n_steps > 2:
            @pl.when(i == 2)
            def _():
                first(False).wait()

            @pl.when(i > 2)
            def _():
                later(i - 2, slot, False).wait()

        du = _dot_nt(dp_ref[...], w_ref[...])
        h0_ = h_ref[...]
        r = _rms(h0_)
        n = h0_ * r
        gg_ref[...] += jnp.sum(du * n, axis=0, keepdims=True)
        dh0 = d1_ref[...] + _rms_bwd(du * g_ref[...], n, r)
        buf[slot] = dh0

        @pl.when(i == 0)
        def _():
            gmeta_ref[...] = dh0[PAD:HEAD, :]
            first(False).start()

        if n_steps > 1:
            @pl.when(i > 0)
            def _():
                later(i, slot, False).start()

        @pl.when(i == n_steps - 1)
        def _():
            if n_steps == 1:
                first(False).wait()
            else:
                if n_steps == 2:
                    first(False).wait()
                else:
                    later(i - 1, 1 - slot, False).wait()
                later(i, slot, False).wait()

    row = lambda n: pl.BlockSpec((tm, n), lambda i: (i, 0))
    return pl.pallas_call(
        body, grid=(n_steps,),
        in_specs=[row(D_IN), _resident((D_MODEL, D_IN)),
                  row(D_MODEL), pl.BlockSpec((1, D_MODEL), lambda i: (0, 0)), row(D_MODEL)],
        out_specs=[pl.BlockSpec(memory_space=pl.ANY), pl.BlockSpec((N_META, D_MODEL), lambda i: (0, 0)),
                   pl.BlockSpec((1, D_MODEL), lambda i: (0, 0))],
        out_shape=[jax.ShapeDtypeStruct((T - HEAD, D_MODEL), F32), jax.ShapeDtypeStruct((N_META, D_MODEL), F32),
                   jax.ShapeDtypeStruct((1, D_MODEL), F32)],
        scratch_shapes=[pltpu.VMEM((2, tm, D_MODEL), F32), pltpu.SemaphoreType.DMA((2,))],
        name="in_bwd", compiler_params=_params("arbitrary"),
    )(dp, w_in, h0, g1, dh1)


def _col_tile(cols, target):
    best = None
    for t in range(128, min(cols, target) + 1, 128):
        if cols % t == 0:
            best = t
    assert best is not None, cols
    return best


MXU_DIM = 256


def _mxu_tile(cols, target):
    best = None
    for t in range(MXU_DIM, min(cols, target) + 1, MXU_DIM):
        if cols % t == 0:
            best = t
    assert best is not None, cols
    return best


def _weight_grad(a, b, name):
    T, M = a.shape
    N = b.shape[1]
    tm = _col_tile(M, 1408)
    tn = _mxu_tile(N, 768 if tm <= 1024 else 512)

    def body(a_ref, b_ref, o_ref, ob_ref):
        o = _dot_tn(a_ref[...], b_ref[...])
        o_ref[...] = o
        ob_ref[...] = o.astype(BF16)

    return pl.pallas_call(
        body, grid=(M // tm, N // tn),
        in_specs=[pl.BlockSpec((T, tm), lambda m, n: (0, m)), pl.BlockSpec((T, tn), lambda m, n: (0, n))],
        out_specs=[pl.BlockSpec((tm, tn), lambda m, n: (m, n))] * 2,
        out_shape=[jax.ShapeDtypeStruct((M, N), F32), jax.ShapeDtypeStruct((M, N), BF16)],
        name=name, compiler_params=_params("parallel", "parallel"),
    )(a, b)


def _local_step(x, meta, target, w_in_own, w_in, w_out, w_gu, w_down, small, chip, on_ffn_grads=None,
                on_mixer_grads=None):
    wg = _gate_weights(small["w_rgate"], small["w_igate"])
    bg = jnp.concatenate([small["b_rgate"], small["b_igate"]], axis=1)

    p, u, h0 = _in_proj_local(x, meta, small["mix_norm_g"], w_in_own, chip)
    p = _in_proj_rest(u, w_in, p, chip)
    y_rg, hs, xc = _rg_fwd(p, small["conv_w"], small["conv_b"], wg, bg, small["lru_lambda"], small["rg_norm_g"])
    y_hg, o_all, st_all = _hg_fwd(p, small["hg_lower_bound"], small["hg_norm_g"])
    h1, v, yb, gu, act, dh2, dh2b, loss, g_final = _ffn_fwd(
        h0, y_rg, y_hg, w_out, small["ffn_norm_g"], w_gu, w_down, small["final_norm_g"], target)

    g_w_down = _weight_grad(act, dh2b, "grad_w_down")
    dgu, dh1, dh1b, dy, g_ffn = _ffn_bwd(dh2b, gu, w_down, w_gu, h1, small["ffn_norm_g"], dh2, w_out)
    ffn_grads = {"w_gate_up": _weight_grad(v, dgu, "grad_w_gate_up"), "w_down": g_w_down,
                 "w_out": _weight_grad(yb, dh1b, "grad_w_out")}
    stages = on_ffn_grads(ffn_grads) if on_ffn_grads is not None else None
    dp, g_lb, g_hgn = _hg_bwd(p, o_all, st_all, dy, small["hg_lower_bound"], small["hg_norm_g"])
    early = late = None
    if stages is not None:
        chip_sums, send = stages
        sums = chip_sums()
        (dp, dy), sums = lax.optimization_barrier(((dp, dy), sums))
        early = send(sums)
    dp, g_cw, g_cb, g_wgate, g_bg, g_lam, g_rgn = _rg_bwd(
        p, xc, hs, dy, dp, small["conv_w"], small["conv_b"], wg, bg, small["lru_lambda"], small["rg_norm_g"])
    mixer_grads = {"w_in": _weight_grad(u, dp, "grad_w_in")}
    if on_mixer_grads is not None:
        chip_sums, send = on_mixer_grads(mixer_grads)
        sums = chip_sums()
        (dp, dh1), sums = lax.optimization_barrier(((dp, dh1), sums))
        late = send(sums)
    grad_x, g_meta, g_mix = _in_bwd(dp, w_in, h0, small["mix_norm_g"], dh1)

    grads = {
        "w_in": mixer_grads["w_in"][0], "w_out": ffn_grads["w_out"][0],
        "w_gate_up": ffn_grads["w_gate_up"][0], "w_down": ffn_grads["w_down"][0],
        "meta_tokens": g_meta, "mix_norm_g": g_mix, "conv_w": g_cw, "conv_b": g_cb, "w_gates": g_wgate,
        "b_rgate": g_bg[:, :D_RG], "b_igate": g_bg[:, D_RG:], "lru_lambda": g_lam, "rg_norm_g": g_rgn,
        "hg_lower_bound": g_lb, "hg_norm_g": g_hgn, "ffn_norm_g": g_ffn, "final_norm_g": g_final,
    }
    return loss, grad_x, grads, early, late


ANY = pl.BlockSpec(memory_space=pl.ANY)
HALF = D_MODEL // 2

BIG = {"w_in": (D_MODEL, D_IN // N_CHIPS, True), "w_gate_up": (D_MODEL, 2 * D_FF // N_CHIPS, True),
       "w_out": (D_MODEL // N_CHIPS, D_MODEL, False), "w_down": (D_FF // N_CHIPS, D_MODEL, False)}
BIG_NAMES = tuple(BIG)
N_BIG = len(BIG_NAMES)


def _full_shape(name):
    rows, cols, by_col = BIG[name]
    return (rows, cols * N_CHIPS) if by_col else (rows * N_CHIPS, cols)


def _place():
    return lax.axis_index("x"), lax.axis_index("y"), lax.axis_index("c")


def _chip_of(x, y, r):
    fx, fy = (r + 1) >> 1, (r + 1) & 1
    return (1 - x if fx else x), (1 - y if fy else y)


def _half_of(ref, by_col, half):
    start = pl.multiple_of(half * HALF, 128)
    return ref.at[pl.ds(start, HALF), :] if by_col else ref.at[:, pl.ds(start, HALF)]


def _shard_of(ref, name, chip):
    rows, cols, by_col = BIG[name]
    if by_col:
        return ref.at[:, pl.ds(pl.multiple_of(chip * cols, 128), cols)]
    return ref.at[pl.ds(pl.multiple_of(chip * rows, 16), rows), :]


def _shard_half_of(ref, name, chip, half):
    rows, cols, by_col = BIG[name]
    start = pl.multiple_of(half * HALF, 128)
    if by_col:
        return ref.at[pl.ds(start, HALF), pl.ds(pl.multiple_of(chip * cols, 128), cols)]
    return ref.at[pl.ds(pl.multiple_of(chip * rows, 16), rows), pl.ds(start, HALF)]


def _shard_half_part_of(ref, name, chip, half, part):
    rows, cols, by_col = BIG[name]
    start = pl.multiple_of(half * HALF + part * (HALF // 2), 128)
    if by_col:
        return ref.at[pl.ds(start, HALF // 2), pl.ds(pl.multiple_of(chip * cols, 128), cols)]
    return ref.at[pl.ds(pl.multiple_of(chip * rows, 16), rows), pl.ds(start, HALF // 2)]


def _remote(src, dst, send_sems, recv_sems, k, dev):
    return pltpu.make_async_remote_copy(src_ref=src, dst_ref=dst, send_sem=send_sems.at[k], recv_sem=recv_sems.at[k],
                                        device_id=dev, device_id_type=MESH)


def _place_shards(w, small, chip, names, label):
    steps = 4
    n, ns = len(names), len(small)
    in_specs, out_specs = [], []
    for name in names:
        rows, cols, by_col = BIG[name]
        tr = rows // steps
        in_specs.append(pl.BlockSpec((tr, cols), lambda i, s: (i, 0)))
        if by_col:
            out_specs.append(pl.BlockSpec((tr, cols), lambda i, s: (i, s[0])))
        else:
            out_specs.append(pl.BlockSpec((tr, cols), lambda i, s: (s[0] * steps + i, 0)))

    def body(s_ref, *refs):
        ins, small_in = refs[:n], refs[n:n + ns]
        outs, small_out = refs[n + ns:2 * n + ns], refs[2 * n + ns:2 * (n + ns)]
        send_sems, recv_sems, local_sems = refs[2 * (n + ns):]
        i = pl.program_id(0)
        x, y, c = _place()
        chip_ = 2 * x + y
        others = [_chip_of(x, y, r) for r in range(3)]

        def block(a, q):
            cols = small[a].shape[1]
            return small_out[a].at[:, pl.ds(pl.multiple_of(q * cols, 128), cols)]

        def local(a):
            return pltpu.make_async_copy(small_in[a], block(a, chip_), local_sems.at[a])

        def remote(a, r):
            qx, qy = others[r]
            return _remote(small_in[a], block(a, chip_), send_sems, recv_sems, 3 * a + r, (qx, qy, c))

        @pl.when(i == 0)
        def _():
            for a in range(ns):
                local(a).start()
                for r in range(3):
                    remote(a, r).start()

        for a in range(n):
            outs[a][...] = ins[a][...].astype(BF16)

        @pl.when(i == steps - 1)
        def _():
            for a in range(ns):
                for r, (qx, qy) in enumerate(others):
                    landed = block(a, 2 * qx + qy)
                    _remote(landed, landed, send_sems, recv_sems, 3 * a + r, (qx, qy, c)).wait_recv()
                for r in range(3):
                    remote(a, r).wait_send()
                local(a).wait()

    out = pl.pallas_call(
        body,
        grid_spec=pltpu.PrefetchScalarGridSpec(
            num_scalar_prefetch=1, grid=(steps,), in_specs=in_specs + [ANY] * ns, out_specs=out_specs + [ANY] * ns,
            scratch_shapes=[pltpu.SemaphoreType.DMA((max(3 * ns, 1),)), pltpu.SemaphoreType.DMA((max(3 * ns, 1),)),
                            pltpu.SemaphoreType.DMA((max(ns, 1),))]),
        out_shape=([jax.ShapeDtypeStruct(_full_shape(name), BF16) for name in names]
                   + [jax.ShapeDtypeStruct((s.shape[0], s.shape[1] * N_CHIPS), F32) for s in small]),
        name=label, compiler_params=_params("arbitrary"),
    )(chip, *[w[name] for name in names], *small)
    return dict(zip(names, out[:n])), list(out[n:])


def _gather_weights(placed, small, names, label, collective_id):
    n, ns = len(names), len(small)
    hbm = pltpu.MemorySpace.HBM
    outs = [jax.new_ref(placed[nm], memory_space=hbm) for nm in names]
    small_in = [jax.new_ref(s, memory_space=hbm) for s in small]
    small_out = [jax.empty_ref(jax.ShapeDtypeStruct((s.shape[0], s.shape[1] * N_CHIPS), F32), memory_space=hbm)
                 for s in small]
    n_sems = 8 * n + 3 * ns

    @pl.kernel(mesh=plsc.ScalarSubcoreMesh(axis_name="seq", num_cores=1), name=label, out_type=(),
               scratch_types=(pltpu.SemaphoreType.DMA((n_sems,)), pltpu.SemaphoreType.DMA((n_sems,)),
                              pltpu.SemaphoreType.DMA((max(ns, 1),))),
               compiler_params=pltpu.CompilerParams(collective_id=collective_id))
    def launch(send_sems, recv_sems, local_sems):
        x, y, c = _place()
        chip = 2 * x + y
        sibling = (x, y, 1 - c)
        others = [_chip_of(x, y, r) for r in range(3)]
        near = others[:2]
        far = 2 * others[2][0] + others[2][1]
        _handshake([(qx, qy, c) for qx, qy in others] + [sibling])

        def small_block(a, q):
            cols = small[a].shape[1]
            return small_out[a].at[:, pl.ds(pl.multiple_of(q * cols, 128), cols)]

        local = [pltpu.make_async_copy(small_in[a], small_block(a, chip), local_sems.at[a]) for a in range(ns)]
        for cp in local:
            cp.start()

        sends = []
        for a, name in enumerate(names):
            mine = _shard_half_of(outs[a], name, chip, c)
            for r, (qx, qy) in enumerate(near):
                sends.append(_remote(mine, mine, send_sems, recv_sems, 8 * a + r, (qx, qy, c)))
        for a in range(ns):
            for r, (qx, qy) in enumerate(others):
                sends.append(_remote(small_in[a], small_block(a, chip), send_sems, recv_sems,
                                     8 * n + 3 * a + r, (qx, qy, c)))
        for cp in sends:
            cp.start()

        forwards = []

        def forward(piece, k, dev):
            cp = _remote(piece, piece, send_sems, recv_sems, k, dev)
            cp.start()
            forwards.append(cp)

        for a, name in enumerate(names):
            for r, (qx, qy) in enumerate(near):
                landed = _shard_half_of(outs[a], name, 2 * qx + qy, c)
                _remote(landed, landed, send_sems, recv_sems, 8 * a + r, (qx, qy, c)).wait_recv()
                ox, oy = near[1 - r]
                forward(_shard_half_part_of(outs[a], name, 2 * qx + qy, c, r), 8 * a + 2 + r, (ox, oy, c))
                forward(landed, 8 * a + 4 + r, sibling)
        for a, name in enumerate(names):
            for part in range(2):
                qx, qy = near[1 - part]
                landed = _shard_half_part_of(outs[a], name, far, c, part)
                _remote(landed, landed, send_sems, recv_sems, 8 * a + 2 + part, (qx, qy, c)).wait_recv()
                forward(landed, 8 * a + 6 + part, sibling)
        for a in range(ns):
            for r, (qx, qy) in enumerate(others):
                landed = small_block(a, 2 * qx + qy)
                _remote(landed, landed, send_sems, recv_sems, 8 * n + 3 * a + r, (qx, qy, c)).wait_recv()
        for a, name in enumerate(names):
            for r, (qx, qy) in enumerate(near):
                landed = _shard_half_of(outs[a], name, 2 * qx + qy, 1 - c)
                _remote(landed, landed, send_sems, recv_sems, 8 * a + 4 + r, sibling).wait_recv()
            for part in range(2):
                landed = _shard_half_part_of(outs[a], name, far, 1 - c, part)
                _remote(landed, landed, send_sems, recv_sems, 8 * a + 6 + part, sibling).wait_recv()
        for cp in sends + forwards:
            cp.wait_send()
        for cp in local:
            cp.wait()

    launch()
    return {nm: ref[...] for nm, ref in zip(names, outs)}, [ref[...] for ref in small_out]


def _exchange_halves(grads, names, label, collective_id):
    n = len(names)
    sequencer = collective_id is not None

    def body(*refs):
        ins, outs = refs[:n], refs[n:2 * n]
        send_sems, recv_sems = refs[2 * n:]
        x, y, c = _place()
        if sequencer:
            _handshake([(x, y, 1 - c)])
        copies = []
        for a, name in enumerate(names):
            copies.append(_remote(_half_of(ins[a], BIG[name][2], 1 - c), outs[a], send_sems, recv_sems, a,
                                  (x, y, 1 - c)))
        for cp in copies:
            cp.start()
        for cp in copies:
            cp.wait()

    def half_shape(name):
        r, c_ = _full_shape(name)
        return (HALF, c_) if BIG[name][2] else (r, HALF)

    out_type = tuple(jax.ShapeDtypeStruct(half_shape(nm), grads[nm].dtype) for nm in names)
    sems = (pltpu.SemaphoreType.DMA((n,)), pltpu.SemaphoreType.DMA((n,)))
    operands = [grads[nm] for nm in names]
    if sequencer:
        got = pl.kernel(
            body, mesh=plsc.ScalarSubcoreMesh(axis_name="seq", num_cores=1), name=label, out_type=out_type,
            scratch_types=sems, compiler_params=pltpu.CompilerParams(collective_id=collective_id),
        )(*operands)
    else:
        got = pl.pallas_call(
            body, in_specs=[ANY] * n, out_specs=[ANY] * n, out_shape=list(out_type), scratch_shapes=list(sems),
            name=label,
        )(*operands)
    return dict(zip(names, got))


def _chip_sum(grads, got, names, core, label):
    n = len(names)
    steps = 4
    g_specs, blks = [], []
    for name in names:
        rows, cols = got[name].shape
        tr = rows // steps
        if BIG[name][2]:
            g_specs.append(pl.BlockSpec((tr, cols), lambda i, s: (s[0] * steps + i, 0)))
        else:
            g_specs.append(pl.BlockSpec((tr, HALF), lambda i, s: (i, s[0])))
        blks.append(pl.BlockSpec((tr, cols), lambda i, s: (i, 0)))

    def body(s_ref, *refs):
        for a in range(n):
            t = refs[a][...] + refs[n + a][...].astype(F32)
            refs[2 * n + a][...] = t
            refs[3 * n + a][...] = t.astype(BF16)

    out = pl.pallas_call(
        body,
        grid_spec=pltpu.PrefetchScalarGridSpec(num_scalar_prefetch=1, grid=(steps,), in_specs=g_specs + blks,
                                               out_specs=blks + blks),
        out_shape=([jax.ShapeDtypeStruct(got[nm].shape, F32) for nm in names]
                   + [jax.ShapeDtypeStruct(got[nm].shape, BF16) for nm in names]),
        name=label, compiler_params=_params("parallel"),
    )(core, *[grads[nm] for nm in names], *[got[nm] for nm in names])
    return {nm: (out[a], out[n + a]) for a, nm in enumerate(names)}


def _piece_shape(name):
    rows, cols, by_col = BIG[name]
    return (HALF, cols) if by_col else (rows, HALF)


def _handshake(peers):
    barrier = pltpu.get_barrier_semaphore()
    for peer in peers:
        pl.semaphore_signal(barrier, inc=1, device_id=peer, device_id_type=MESH)
    pl.semaphore_wait(barrier, len(peers))


def _send_chip_sums(sums, names, label, collective_id):
    n = len(names)

    def body(*refs):
        ins, outs = refs[:n], refs[n:2 * n]
        send_sems, recv_sems = refs[2 * n:]
        x, y, c = _place()
        others = [_chip_of(x, y, r) for r in range(3)]
        _handshake([(qx, qy, c) for qx, qy in others])
        copies = []
        for a, name in enumerate(names):
            for r, (qx, qy) in enumerate(others):
                copies.append(_remote(_shard_of(ins[a], name, 2 * qx + qy), outs[a].at[r], send_sems, recv_sems,
                                      3 * a + r, (qx, qy, c)))
        for cp in copies:
            cp.start()
        for cp in copies:
            cp.wait()

    return pl.kernel(
        body, mesh=plsc.ScalarSubcoreMesh(axis_name="seq", num_cores=1), name=label,
        out_type=tuple(jax.ShapeDtypeStruct((3,) + _piece_shape(nm), BF16) for nm in names),
        scratch_types=(pltpu.SemaphoreType.DMA((3 * n,)), pltpu.SemaphoreType.DMA((3 * n,))),
        compiler_params=pltpu.CompilerParams(collective_id=collective_id),
    )(*[sums[nm] for nm in names])


def _total(parts, chip_core):
    steps = 2
    in_specs, out_specs, operands = [], [], []
    for name in BIG_NAMES:
        by_col = BIG[name][2]
        pr, pc = _piece_shape(name)
        tr = pr // steps
        if by_col:
            in_specs.append(pl.BlockSpec((tr, pc), lambda i, s: (i, s[0])))
            out_specs.append(pl.BlockSpec((tr, pc), lambda i, s: (s[1] * steps + i, 0)))
        else:
            in_specs.append(pl.BlockSpec((tr, pc), lambda i, s: (s[0] * steps + i, 0)))
            out_specs.append(pl.BlockSpec((tr, pc), lambda i, s: (i, s[1])))
        for r in range(3):
            in_specs.append(pl.BlockSpec((None, tr, pc), lambda i, s, r=r: (r, i, 0)))
        own, got = parts[name]
        operands += [own, got, got, got]

    def body(s_ref, *refs):
        for a in range(N_BIG):
            o_ref, a_ref, b_ref, c_ref = refs[4 * a:4 * a + 4]
            refs[4 * N_BIG + a][...] = (((o_ref[...] + a_ref[...].astype(F32)) + b_ref[...].astype(F32))
                                        + c_ref[...].astype(F32))

    totals = pl.pallas_call(
        body,
        grid_spec=pltpu.PrefetchScalarGridSpec(num_scalar_prefetch=1, grid=(steps,), in_specs=in_specs,
                                               out_specs=out_specs),
        out_shape=[jax.ShapeDtypeStruct(BIG[name][:2], F32) for name in BIG_NAMES],
        name="totals", compiler_params=_params("parallel"),
    )(chip_core, *operands)
    return dict(zip(BIG_NAMES, totals))


VEC_ROWS = 32
VEC_ROW = {"mix_norm_g": 0, "conv_b": 1, "b_rgate": 2, "b_igate": 3, "lru_lambda": 4, "rg_norm_g": 5,
           "hg_lower_bound": 6, "hg_norm_g": 8, "ffn_norm_g": 9, "final_norm_g": 10, "loss": 11,
           "conv_w": 12, "meta_tokens": 16}
N_DEV = 8


def _all_reduce_small(pieces, gates, totals):
    names = list(pieces)
    n_small = 10
    hv, hg = VEC_ROWS // 2, gates.shape[0] // 2

    def body(*refs):
        ins = refs[:len(names)]
        g_ref = refs[len(names)]
        vec_ref, gsum_ref = refs[len(names) + 1 + N_BIG:len(names) + 3 + N_BIG]
        big = refs[len(names) + 3 + N_BIG:len(names) + 3 + 2 * N_BIG]
        (mine_v, sib_v, sib_g, chip_v, chip_g, got_v, got_g, send_sems, recv_sems) = refs[len(names) + 3 + 2 * N_BIG:]
        x, y, c = _place()
        chip = 2 * x + y
        sibling = (x, y, 1 - c)
        share = []
        for a, name in enumerate(BIG_NAMES):
            half = _half_of(big[a], BIG[name][2], c)
            share.append(_remote(half, half, send_sems, recv_sems, n_small + a, sibling))
        mine_v[...] = jnp.zeros_like(mine_v)
        for name, ref in zip(names, ins):
            nr, w = ref.shape
            mine_v[VEC_ROW[name]:VEC_ROW[name] + nr, 0:w] = ref[...]

        swap = [_remote(mine_v, sib_v, send_sems, recv_sems, 0, sibling),
                _remote(g_ref, sib_g, send_sems, recv_sems, 1, sibling)]
        for cp in swap:
            cp.start()
        for cp in swap:
            cp.wait()
        for cp in share:
            cp.start()
        chip_v[...] = mine_v[...] + sib_v[...]
        chip_g[...] = g_ref[...] + sib_g[...]

        rows_v = pl.ds(pl.multiple_of(c * hv, 8), hv)
        rows_g = pl.ds(pl.multiple_of(c * hg, 8), hg)
        got_v[chip] = chip_v[rows_v, :]
        got_g[chip] = chip_g[rows_g, :].astype(BF16)
        sends = []
        for r in range(3):
            qx, qy = _chip_of(x, y, r)
            sends.append(_remote(chip_v.at[rows_v, :], got_v.at[chip], send_sems, recv_sems, 2 + r, (qx, qy, c)))
            sends.append(_remote(got_g.at[chip], got_g.at[chip], send_sems, recv_sems, 5 + r, (qx, qy, c)))
        for cp in sends:
            cp.start()
        for cp in sends:
            cp.wait()
        vec_ref[rows_v, :] = ((got_v[0] + got_v[1]) + got_v[2]) + got_v[3]
        gsum_ref[rows_g, :] = ((got_g[0].astype(F32) + got_g[1].astype(F32)) + got_g[2].astype(F32)
                               ) + got_g[3].astype(F32)

        back = [_remote(vec_ref.at[rows_v, :], vec_ref.at[rows_v, :], send_sems, recv_sems, 8, sibling),
                _remote(gsum_ref.at[rows_g, :], gsum_ref.at[rows_g, :], send_sems, recv_sems, 9, sibling)]
        for cp in back:
            cp.start()
        theirs_v = vec_ref.at[pl.ds(pl.multiple_of((1 - c) * hv, 8), hv), :]
        theirs_g = gsum_ref.at[pl.ds(pl.multiple_of((1 - c) * hg, 8), hg), :]
        _remote(theirs_v, theirs_v, send_sems, recv_sems, 8, sibling).wait_recv()
        _remote(theirs_g, theirs_g, send_sems, recv_sems, 9, sibling).wait_recv()
        for cp in back:
            cp.wait_send()
        for a, name in enumerate(BIG_NAMES):
            theirs = _half_of(big[a], BIG[name][2], 1 - c)
            _remote(theirs, theirs, send_sems, recv_sems, n_small + a, sibling).wait_recv()
        for cp in share:
            cp.wait_send()

    vmem = pl.BlockSpec(memory_space=pltpu.VMEM)
    n_sems = n_small + N_BIG
    out = pl.pallas_call(
        body, in_specs=[vmem] * (len(names) + 1) + [ANY] * N_BIG, out_specs=[vmem, vmem] + [ANY] * N_BIG,
        out_shape=([jax.ShapeDtypeStruct((VEC_ROWS, D_MODEL), F32), jax.ShapeDtypeStruct(gates.shape, F32)]
                   + [jax.ShapeDtypeStruct(BIG[n][:2], F32) for n in BIG_NAMES]),
        input_output_aliases={len(names) + 1 + a: 2 + a for a in range(N_BIG)},
        scratch_shapes=[pltpu.VMEM((VEC_ROWS, D_MODEL), F32), pltpu.VMEM((VEC_ROWS, D_MODEL), F32),
                        pltpu.VMEM(gates.shape, F32), pltpu.VMEM((VEC_ROWS, D_MODEL), F32),
                        pltpu.VMEM(gates.shape, F32), pltpu.VMEM((N_CHIPS, hv, D_MODEL), F32),
                        pltpu.VMEM((N_CHIPS, hg) + gates.shape[1:], BF16),
                        pltpu.SemaphoreType.DMA((n_sems,)), pltpu.SemaphoreType.DMA((n_sems,))],
        name="all_reduce_small",
    )(*[pieces[n] for n in names], gates, *[totals[n] for n in BIG_NAMES])
    return out[0], out[1], dict(zip(BIG_NAMES, out[2:]))


def _adamw_math(w, g, m, v):
    m = ADAM_B1 * m + (1.0 - ADAM_B1) * g
    v = ADAM_B2 * v + (1.0 - ADAM_B2) * (g * g)
    m_hat = m / (1.0 - ADAM_B1 ** ADAM_STEP)
    v_hat = v / (1.0 - ADAM_B2 ** ADAM_STEP)
    delta = -ADAM_LR * (m_hat / (jnp.sqrt(v_hat) + ADAM_EPS) + ADAM_WD * w)
    return delta, m, v


def _adamw_big(w, g, m, v):
    steps = 8
    blks = []
    for name in BIG_NAMES:
        rows, cols, _ = BIG[name]
        blks.append(pl.BlockSpec((rows // steps, cols), lambda i: (i, 0)))

    def body(*refs):
        ins, outs = refs[:4 * N_BIG], refs[4 * N_BIG:]
        for a in range(N_BIG):
            w_ref, g_ref, m_ref, v_ref = (ins[k * N_BIG + a] for k in range(4))
            g = g_ref[...]
            d, nm, nv = _adamw_math(w_ref[...], g, m_ref[...], v_ref[...])
            outs[a][...] = g
            outs[N_BIG + a][...] = d
            outs[2 * N_BIG + a][...] = nm
            outs[3 * N_BIG + a][...] = nv

    shapes = [jax.ShapeDtypeStruct(BIG[name][:2], F32) for name in BIG_NAMES]
    out = pl.pallas_call(
        body, grid=(steps,), in_specs=blks * 4, out_specs=blks * 4, out_shape=shapes * 4,
        name="adamw_big", compiler_params=_params("parallel"),
    )(*[t[name] for t in (w, g, m, v) for name in BIG_NAMES])
    return {name: tuple(out[k * N_BIG + a] for k in range(4)) for a, name in enumerate(BIG_NAMES)}


SMALL = {"meta_tokens": (N_META, D_MODEL // N_CHIPS), "mix_norm_g": (1, D_MODEL), "conv_w": (CONV_W, D_RG // N_CHIPS),
         "conv_b": (1, D_RG), "w_rgate": (D_RG, RG_HEAD_DIM), "b_rgate": (1, D_RG), "w_igate": (D_RG, RG_HEAD_DIM),
         "b_igate": (1, D_RG), "lru_lambda": (1, D_RG), "rg_norm_g": (1, D_RG), "hg_lower_bound": (2, D_HG),
         "hg_norm_g": (1, HG_HEAD_DIM), "ffn_norm_g": (1, D_MODEL), "final_norm_g": (1, D_MODEL)}
SMALL_NAMES = tuple(SMALL)
SHARDED_SMALL = ("meta_tokens", "conv_w")


def _adamw_small(vec, gates, w, m, v):
    n = len(SMALL_NAMES)

    def body(*refs):
        vec_ref, gates_ref = refs[:2]
        w_refs, m_refs, v_refs = refs[2:2 + n], refs[2 + n:2 + 2 * n], refs[2 + 2 * n:2 + 3 * n]
        outs = refs[2 + 3 * n:]
        loss_ref = outs[0]
        x, y, _ = _place()
        chip = 2 * x + y
        loss_ref[...] = vec_ref[VEC_ROW["loss"]:VEC_ROW["loss"] + 1, 0:1]

        def update(k, g):
            g_ref, d_ref, nm_ref, nv_ref = outs[1 + 4 * k:5 + 4 * k]
            g_ref[...] = g
            d_ref[...], nm_ref[...], nv_ref[...] = _adamw_math(w_refs[k][...], g, m_refs[k][...], v_refs[k][...])

        for k, name in enumerate(SMALL_NAMES):
            nr, w_ = SMALL[name]
            if name == "w_rgate":
                update(k, gates_ref[0:D_RG, :])
            elif name == "w_igate":
                update(k, gates_ref[D_RG:2 * D_RG, :])
            elif name in SHARDED_SMALL:
                r0 = VEC_ROW[name]
                for q in range(N_CHIPS):
                    @pl.when(chip == q)
                    def _(k=k, r0=r0, nr=nr, w_=w_, q=q):
                        update(k, vec_ref[r0:r0 + nr, q * w_:(q + 1) * w_])
            else:
                r0 = VEC_ROW[name]
                update(k, vec_ref[r0:r0 + nr, 0:w_])

    vmem = pl.BlockSpec(memory_space=pltpu.VMEM)
    out_shape = [jax.ShapeDtypeStruct((1, 1), F32)]
    for name in SMALL_NAMES:
        out_shape += [jax.ShapeDtypeStruct(SMALL[name], F32)] * 4
    outs = pl.pallas_call(
        body, in_specs=[vmem] * (2 + 3 * n), out_specs=[vmem] * len(out_shape), out_shape=out_shape,
        name="adamw_small",
    )(vec, gates, *[w[k] for k in SMALL_NAMES], *[m[k] for k in SMALL_NAMES], *[v[k] for k in SMALL_NAMES])
    loss = outs[0]
    res = {name: tuple(outs[1 + 4 * k:5 + 4 * k]) for k, name in enumerate(SMALL_NAMES)}
    return loss, res


WEIGHT_NAMES = ("meta_tokens", "mix_norm_g", "w_in", "conv_w", "conv_b", "w_rgate", "b_rgate", "w_igate", "b_igate",
                "lru_lambda", "rg_norm_g", "hg_lower_bound", "hg_norm_g", "w_out", "ffn_norm_g", "w_gate_up", "w_down",
                "final_norm_g")


def _to_2d(name, a):
    if name in BIG:
        return a.reshape(BIG[name][:2])
    return a.reshape(SMALL[name])


def kernel(x, meta_tokens, mix_norm_g, w_in, conv_w, conv_b, w_rgate, b_rgate, w_igate, b_igate, lru_lambda, rg_norm_g, hg_lower_bound, hg_norm_g, w_out, ffn_norm_g, w_gate_up, w_down, final_norm_g, loss_target, m_meta_tokens, m_mix_norm_g, m_w_in, m_conv_w, m_conv_b, m_w_rgate, m_b_rgate, m_w_igate, m_b_igate, m_lru_lambda, m_rg_norm_g, m_hg_lower_bound, m_hg_norm_g, m_w_out, m_ffn_norm_g, m_w_gate_up, m_w_down, m_final_norm_g, v_meta_tokens, v_mix_norm_g, v_w_in, v_conv_w, v_conv_b, v_w_rgate, v_b_rgate, v_w_igate, v_b_igate, v_lru_lambda, v_rg_norm_g, v_hg_lower_bound, v_hg_norm_g, v_w_out, v_ffn_norm_g, v_w_gate_up, v_w_down, v_final_norm_g):
    w_raw = dict(zip(WEIGHT_NAMES, (meta_tokens, mix_norm_g, w_in, conv_w, conv_b, w_rgate, b_rgate, w_igate, b_igate,
                                    lru_lambda, rg_norm_g, hg_lower_bound, hg_norm_g, w_out, ffn_norm_g, w_gate_up,
                                    w_down, final_norm_g)))
    m_raw = dict(zip(WEIGHT_NAMES, (m_meta_tokens, m_mix_norm_g, m_w_in, m_conv_w, m_conv_b, m_w_rgate, m_b_rgate,
                                    m_w_igate, m_b_igate, m_lru_lambda, m_rg_norm_g, m_hg_lower_bound, m_hg_norm_g,
                                    m_w_out, m_ffn_norm_g, m_w_gate_up, m_w_down, m_final_norm_g)))
    v_raw = dict(zip(WEIGHT_NAMES, (v_meta_tokens, v_mix_norm_g, v_w_in, v_conv_w, v_conv_b, v_w_rgate, v_b_rgate,
                                    v_w_igate, v_b_igate, v_lru_lambda, v_rg_norm_g, v_hg_lower_bound, v_hg_norm_g,
                                    v_w_out, v_ffn_norm_g, v_w_gate_up, v_w_down, v_final_norm_g)))
    w = {k: _to_2d(k, a) for k, a in w_raw.items()}
    m = {k: _to_2d(k, a) for k, a in m_raw.items()}
    v = {k: _to_2d(k, a) for k, a in v_raw.items()}

    x_i, y_i, c_i = _place()
    core = jnp.reshape(c_i, (1,)).astype(jnp.int32)
    chip = jnp.reshape(2 * x_i + y_i, (1,)).astype(jnp.int32)
    chip_core = jnp.concatenate([chip, core])

    first_names, rest_names = ("w_in",), ("w_out", "w_gate_up", "w_down")
    placed, _ = _place_shards(w, [], chip, first_names, "place_first")
    first, _ = _gather_weights(placed, [], first_names, "gather_first", 1)
    placed, (meta_full, cw_full) = _place_shards(w, [w["meta_tokens"], w["conv_w"]], chip, rest_names, "place_shards")
    rest, _ = _gather_weights(placed, [], rest_names, "gather_rest", 2)
    full = {**first, **rest}

    seq = x.shape[1]
    small ={k: w[k] for k in SMALL_NAMES if k not in SHARDED_SMALL}
    small["conv_w"] = cw_full

    def reduce_to_chips(grads, names, tag, collective_ids):
        got = _exchange_halves({n: grads[n][1] for n in names}, names, "exchange_halves_" + tag, collective_ids[0])

        def chip_sums():
            return _chip_sum({n: grads[n][0] for n in names}, got, names, core, "chip_sum_" + tag)

        def send(sums):
            arrived = _send_chip_sums({n: sums[n][1] for n in names}, names, "send_chip_sums_" + tag,
                                      collective_ids[1])
            return {n: (sums[n][0], a) for n, a in zip(names, arrived)}

        return chip_sums, send

    ffn_names, mixer_names = ("w_gate_up", "w_down", "w_out"), ("w_in",)
    loss, grad_x, grads, parts, parts_mixer = _local_step(
        x.reshape(seq, D_MODEL), meta_full, loss_target.reshape(seq, D_MODEL),
        w["w_in"], full["w_in"], full["w_out"], full["w_gate_up"], full["w_down"], small, chip,
        on_ffn_grads=lambda g: reduce_to_chips(g, ffn_names, "ffn", (3, 4)),
        on_mixer_grads=lambda g: reduce_to_chips(g, mixer_names, "mixer", (None, 5)))
    parts.update(parts_mixer)
    totals = _total(parts, chip_core)
    pieces = {k: grads[k] for k in VEC_ROW if k != "loss"}
    pieces["loss"] = loss
    vec, gates, g_big = _all_reduce_small(pieces, grads["w_gates"], totals)
    loss_sum, res = _adamw_small(vec, gates, w, m, v)
    res.update(_adamw_big(w, g_big, m, v))

    out = [loss_sum.reshape(()), grad_x.reshape(1, seq, D_MODEL)]
    for j in range(4):
        out += [res[n][j].reshape(w_raw[n].shape) for n in WEIGHT_NAMES]
    return tuple(out)
```

```python
import math

import jax
import jax.numpy as jnp
from jax import lax
from jax.experimental import pallas as pl
from jax.experimental.pallas import tpu as pltpu
from jax.experimental.pallas import tpu_sc as plsc

F32 = jnp.float32
BF16 = jnp.bfloat16
MESH = pl.DeviceIdType.MESH

D_MODEL = 1024
D_RG = 512
RG_HEAD_DIM = 64
D_HG = 512
HG_HEAD_DIM = 128
HG_HEADS = 4
CHUNK = 64
SUB = 16
N_SUB = CHUNK // SUB
N_META = 16
PAD = CHUNK - N_META
D_IN = 3072
D_FF = 2816
CONV_W = 4
LRU_C = 8.0
EPS = 1e-6
EXP_CLAMP = 80.0
GELU_C = math.sqrt(2.0 / math.pi)
GELU_A = 0.044715
N_CHIPS = 4

ADAM_LR = 0.001
ADAM_B1 = 0.9
ADAM_B2 = 0.999
ADAM_EPS = 1e-08
ADAM_WD = 0.01
ADAM_STEP = 10

VMEM_LIMIT = 56 * 1024 * 1024


def _params(*sem):
    return pltpu.CompilerParams(dimension_semantics=sem, vmem_limit_bytes=VMEM_LIMIT)


def _row_tile(rows, target):
    best = None
    for t in range(16, min(rows, target) + 1, 16):
        if rows % t == 0:
            best = t
    assert best is not None, rows
    return best


def _sigmoid(x):
    return 0.5 * jnp.tanh(0.5 * x) + 0.5


def _dot(a, b):
    return jnp.dot(a, b, preferred_element_type=F32)


def _dot_nt(a, b):
    return lax.dot_general(a, b, (((1,), (1,)), ((), ())), preferred_element_type=F32)


def _dot_tn(a, b):
    return lax.dot_general(a, b, (((0,), (0,)), ((), ())), preferred_element_type=F32)


def _rms(x):
    return lax.rsqrt(jnp.mean(x * x, axis=-1, keepdims=True) + EPS)


def _rms_bwd(dn, n, r):
    return r * (dn - n * jnp.mean(dn * n, axis=-1, keepdims=True))


def _gelu_parts(x):
    t = jnp.tanh(GELU_C * (x + GELU_A * x * x * x))
    g = 0.5 * x * (1.0 + t)
    dg = 0.5 * (1.0 + t) + 0.5 * x * (1.0 - t * t) * GELU_C * (1.0 + 3.0 * GELU_A * x * x)
    return g, dg


def _softplus_neg(lam):
    e = jnp.exp(-jnp.abs(lam))
    w = 1.0 + e
    log1p = jnp.where(w == 1.0, e, jnp.log(w) * e / (w - 1.0))
    return jnp.maximum(-lam, 0.0) + log1p


def _head_mask():
    r = lax.broadcasted_iota(jnp.int32, (D_RG, D_RG), 0) // RG_HEAD_DIM
    c = lax.broadcasted_iota(jnp.int32, (D_RG, D_RG), 1) // RG_HEAD_DIM
    return r == c


def _head_fold():
    r = lax.broadcasted_iota(jnp.int32, (D_RG, RG_HEAD_DIM), 0) % RG_HEAD_DIM
    c = lax.broadcasted_iota(jnp.int32, (D_RG, RG_HEAD_DIM), 1)
    return (r == c).astype(F32)


def _gate_weights(w_r, w_i):
    def body(wr_ref, wi_ref, o_ref):
        fold = _head_fold()
        mask = _head_mask()
        for k, ref in enumerate((wr_ref, wi_ref)):
            full = _dot_nt(ref[...].astype(BF16), fold.astype(BF16))
            o_ref[:, k * D_RG:(k + 1) * D_RG] = jnp.where(mask, full, 0.0).astype(BF16)

    return pl.pallas_call(
        body, out_shape=jax.ShapeDtypeStruct((D_RG, 2 * D_RG), BF16), name="gate_weights",
    )(w_r, w_i)


HEAD = PAD + N_META


def _window_copies(seq_hbm, buf, sems, tm):
    def first(to_vmem):
        seq, vm = seq_hbm.at[pl.ds(0, tm - HEAD)], buf.at[0, pl.ds(HEAD, tm - HEAD)]
        return pltpu.make_async_copy(seq, vm, sems.at[0]) if to_vmem else pltpu.make_async_copy(vm, seq, sems.at[0])

    def later(j, slot, to_vmem):
        seq, vm = seq_hbm.at[pl.ds(pl.multiple_of(j * tm - HEAD, 8), tm)], buf.at[slot]
        if to_vmem:
            return pltpu.make_async_copy(seq, vm, sems.at[slot])
        return pltpu.make_async_copy(vm, seq, sems.at[slot])

    return first, later


def _fetch_window(seq_hbm, buf, sems, i, n_steps, tm):
    first, later = _window_copies(seq_hbm, buf, sems, tm)
    slot = i % 2

    @pl.when(i == 0)
    def _():
        first(True).start()

    if n_steps > 1:
        @pl.when(i + 1 < n_steps)
        def _():
            later(i + 1, 1 - slot, True).start()

    @pl.when(i == 0)
    def _():
        first(True).wait()

    if n_steps > 1:
        @pl.when(i > 0)
        def _():
            later(i, slot, True).wait()

    return slot


def _in_proj_local(x, meta, g1, w_own, chip):
    T = x.shape[0] + HEAD
    tm = _row_tile(T, 832)
    n_steps = T // tm
    cols = BIG["w_in"][1]

    def body(s_ref, x_hbm, meta_ref, g_ref, w_ref, p_ref, u_ref, h_ref, buf, sems, wb):
        i = pl.program_id(0)
        slot = _fetch_window(x_hbm, buf, sems, i, n_steps, tm)

        @pl.when(i == 0)
        def _():
            buf[0, 0:PAD, :] = jnp.zeros((PAD, D_MODEL), F32)
            buf[0, PAD:HEAD, :] = meta_ref[...]
            wb[...] = w_ref[...].astype(BF16)

        h = buf[slot]
        h_ref[...] = h
        u = (h * _rms(h) * g_ref[...]).astype(BF16)
        u_ref[...] = u
        p_ref[...] = _dot(u, wb[...])

    return pl.pallas_call(
        body,
        grid_spec=pltpu.PrefetchScalarGridSpec(
            num_scalar_prefetch=1, grid=(n_steps,),
            in_specs=[pl.BlockSpec(memory_space=pl.ANY),
                      pl.BlockSpec((N_META, D_MODEL), lambda i, s: (0, 0)),
                      pl.BlockSpec((1, D_MODEL), lambda i, s: (0, 0)),
                      pl.BlockSpec((D_MODEL, cols), lambda i, s: (0, 0))],
            out_specs=[pl.BlockSpec((tm, cols), lambda i, s: (i, s[0])),
                       pl.BlockSpec((tm, D_MODEL), lambda i, s: (i, 0)),
                       pl.BlockSpec((tm, D_MODEL), lambda i, s: (i, 0))],
            scratch_shapes=[pltpu.VMEM((2, tm, D_MODEL), F32), pltpu.SemaphoreType.DMA((2,)),
                            pltpu.VMEM((D_MODEL, cols), BF16)]),
        out_shape=[jax.ShapeDtypeStruct((T, D_IN), F32), jax.ShapeDtypeStruct((T, D_MODEL), BF16),
                   jax.ShapeDtypeStruct((T, D_MODEL), F32)],
        name="in_proj_local", compiler_params=_params("arbitrary"),
    )(chip, x, meta, g1, w_own)


def _in_proj_rest(u, w_in, p, chip):
    T = u.shape[0]
    tm = _row_tile(T, 2080)
    cols = BIG["w_in"][1]
    block = lambda j, s: (s[0] + 1 + j) % N_CHIPS

    def body(s_ref, u_ref, w_ref, p_in_ref, p_ref):
        p_ref[...] = _dot(u_ref[...], w_ref[...])

    return pl.pallas_call(
        body,
        grid_spec=pltpu.PrefetchScalarGridSpec(
            num_scalar_prefetch=1, grid=(N_CHIPS - 1, T // tm),
            in_specs=[pl.BlockSpec((tm, D_MODEL), lambda j, i, s: (i, 0)),
                      pl.BlockSpec((D_MODEL, cols), lambda j, i, s: (0, block(j, s))), ANY],
            out_specs=pl.BlockSpec((tm, cols), lambda j, i, s: (i, block(j, s)))),
        out_shape=jax.ShapeDtypeStruct((T, D_IN), F32),
        input_output_aliases={3: 0},
        name="in_proj_rest", compiler_params=_params("arbitrary", "arbitrary"),
    )(chip, u, w_in, p)


def _scan_block_fwd(A, B, rowi):
    for d in (1, 2, 4):
        a_sh = pltpu.roll(A, d, axis=0)
        b_sh = pltpu.roll(B, d, axis=0)
        m = rowi >= d
        B = jnp.where(m, A * b_sh + B, B)
        A = jnp.where(m, A * a_sh, A)
    return A, B


def _scan_block_bwd(A, B, rowi):
    for d in (1, 2, 4):
        a_sh = pltpu.roll(A, 8 - d, axis=0)
        b_sh = pltpu.roll(B, 8 - d, axis=0)
        m = rowi < 8 - d
        B = jnp.where(m, A * b_sh + B, B)
        A = jnp.where(m, A * a_sh, A)
    return A, B


def _rg_gates(xc, w_ref, bg_ref, lam):
    pre = _dot(xc.astype(BF16), w_ref[...]) + bg_ref[...]
    r = _sigmoid(pre[:, :D_RG])
    ig = _sigmoid(pre[:, D_RG:])
    sp = _softplus_neg(lam)
    la = -LRU_C * sp * r
    a = jnp.exp(la)
    th = jnp.tanh(la)
    u = 1.0 - th
    rc = pl.reciprocal(u, approx=True)
    rc = rc * (2.0 - u * rc)
    rc = rc * (2.0 - u * rc)
    m2 = -2.0 * th * rc
    inv_m = lax.rsqrt(jnp.maximum(m2, 1e-30))
    return r, ig, sp, a, m2 * inv_m, inv_m


def _conv(ext, cw_ref, cb_ref, tm):
    xc = cb_ref[...] + cw_ref[0:1, :] * ext[8 - 3:8 - 3 + tm, :]
    for j in range(1, CONV_W):
        xc = xc + cw_ref[j:j + 1, :] * ext[8 - 3 + j:8 - 3 + j + tm, :]
    return xc


def _scan_unroll(blocks):
    return 4 if blocks % 4 == 0 else 2 if blocks % 2 == 0 else 1


def _rg_fwd(p, cw, cb, wg, bg, lam, rg_g):
    T = p.shape[0]
    tm = _row_tile(T, 832)
    unroll = _scan_unroll(tm // 8)

    def body(xg_ref, cw_ref, cb_ref, w_ref, bg_ref, lam_ref, g_ref, y_ref, h_ref, xc_ref, ext, a_s, b_s, carry):
        i = pl.program_id(0)

        @pl.when(i == 0)
        def _():
            ext[0:8, :] = jnp.zeros((8, D_RG), F32)
            carry[...] = jnp.zeros((1, D_RG), F32)

        ext[8:8 + tm, :] = xg_ref[:, :D_RG]
        xc = _conv(ext, cw_ref, cb_ref, tm)
        xc_ref[...] = xc
        r, ig, sp, a, m, _ = _rg_gates(xc, w_ref, bg_ref, lam_ref[...])
        row = i * tm + lax.broadcasted_iota(jnp.int32, (tm, 1), 0)
        a_s[...] = a
        b_s[...] = jnp.where(row >= PAD, m * ig * xc, 0.0)
        rowi = lax.broadcasted_iota(jnp.int32, (8, D_RG), 0)

        def blk(j, c):
            for u in range(unroll):
                o = pl.multiple_of((j * unroll + u) * 8, 8)
                A, B = _scan_block_fwd(a_s[pl.ds(o, 8), :], b_s[pl.ds(o, 8), :], rowi)
                h = B + A * c
                h_ref[pl.ds(o, 8), :] = h
                c = h[7:8, :]
            return c

        carry[...] = lax.fori_loop(0, tm // (8 * unroll), blk, carry[...])
        ext[0:8, :] = ext[tm:tm + 8, :]
        g, _ = _gelu_parts(xg_ref[:, D_RG:])
        yy = g * h_ref[...]
        y_ref[...] = (yy * _rms(yy) * g_ref[...]).astype(BF16)

    vec = lambda n: pl.BlockSpec((1, n), lambda i: (0, 0))
    return pl.pallas_call(
        body, grid=(T // tm,),
        in_specs=[pl.BlockSpec((tm, 2 * D_RG), lambda i: (i, 0)),
                  pl.BlockSpec((CONV_W, D_RG), lambda i: (0, 0)), vec(D_RG),
                  pl.BlockSpec((D_RG, 2 * D_RG), lambda i: (0, 0)), vec(2 * D_RG), vec(D_RG), vec(D_RG)],
        out_specs=[pl.BlockSpec((tm, D_RG), lambda i: (i, 0))] * 3,
        out_shape=[jax.ShapeDtypeStruct((T, D_RG), BF16), jax.ShapeDtypeStruct((T, D_RG), F32),
                   jax.ShapeDtypeStruct((T, D_RG), F32)],
        scratch_shapes=[pltpu.VMEM((tm + 8, D_RG), F32), pltpu.VMEM((tm, D_RG), F32),
                        pltpu.VMEM((tm, D_RG), F32), pltpu.VMEM((1, D_RG), F32)],
        name="rg_fwd", compiler_params=_params("arbitrary"),
    )(p, cw, cb, wg, bg, lam, rg_g)


def _running_sum(x, down):
    r = lax.broadcasted_iota(jnp.int32, (CHUNK, CHUNK), 0)
    c = lax.broadcasted_iota(jnp.int32, (CHUNK, CHUNK), 1)
    tri = ((c <= r) if down else (c >= r)).astype(BF16)
    hi = x.astype(BF16)
    rest = x - hi.astype(F32)
    mid = rest.astype(BF16)
    lo = (rest - mid.astype(F32)).astype(BF16)
    return (_dot(tri, hi) + _dot(tri, mid)) + _dot(tri, lo)


def _hg_gates(hq, hf, lbraw_ref, valid):
    lb = _sigmoid(lbraw_ref[0:1, :] - lbraw_ref[1:2, :])
    sq = _sigmoid(hq)
    q = hq * sq
    sf = _sigmoid(hf)
    f = lb + (1.0 - lb) * sf
    lf = jnp.where(valid, jnp.log(f), 0.0)
    b = _running_sum(lf, True)
    return lb, sq, q, sf, f, b


def _hg_head(qh, kh, bh):
    b_last = bh[CHUNK - 1:CHUNK, :]
    refs = [bh[SUB * s:SUB * s + 1, :] for s in range(N_SUB)]
    r_sel = jnp.concatenate([jnp.broadcast_to(refs[s], (SUB, HG_HEAD_DIM)) for s in range(N_SUB)], axis=0)
    eb = jnp.exp(bh)
    eq = jnp.exp(bh - r_sel)
    ekh = jnp.exp(b_last - bh)
    ek = [jnp.exp(jnp.minimum(refs[s] - bh[:SUB * (s + 1), :], EXP_CLAMP)) for s in range(N_SUB)]
    qe = qh * eq

    def own_rows(s):
        parts = [jnp.zeros((SUB * s, HG_HEAD_DIM), F32)] if s else []
        parts.append(qe[SUB * s:SUB * (s + 1), :])
        if s < N_SUB - 1:
            parts.append(jnp.zeros((CHUNK - SUB * (s + 1), HG_HEAD_DIM), F32))
        return jnp.concatenate(parts, axis=0)

    q_hat = jnp.concatenate([own_rows(s) for s in range(N_SUB)], axis=1)

    def met_rows(s):
        n = SUB * (s + 1)
        ke = kh[:n, :] * ek[s]
        return ke if n == CHUNK else jnp.concatenate([ke, jnp.zeros((CHUNK - n, HG_HEAD_DIM), F32)], axis=0)

    k_til = jnp.concatenate([met_rows(s) for s in range(N_SUB)], axis=1)
    return b_last, eb, eq, ekh, ek, q_hat, k_til


def _causal():
    r = lax.broadcasted_iota(jnp.int32, (CHUNK, CHUNK), 0)
    c = lax.broadcasted_iota(jnp.int32, (CHUNK, CHUNK), 1)
    return r >= c


def _chunks_per_step(n_chunks):
    for c in (5, 4, 3, 2):
        if n_chunks % c == 0:
            return c
    return 1


def _hg_fwd(p, lbraw, hg_g):
    T = p.shape[0]
    n_chunks = T // CHUNK
    cps = _chunks_per_step(n_chunks)
    rows = cps * CHUNK

    def body(hq_ref, hf_ref, hi_ref, hg_ref, lb_ref, g_ref, y_ref, o_ref, st_all_ref, st):
        i = pl.program_id(0)

        @pl.when(i == 0)
        def _():
            st[...] = jnp.zeros_like(st)

        def chunk(j, carry):
            rs = pl.ds(pl.multiple_of(j * CHUNK, CHUNK), CHUNK)
            chunk_body(i * cps + j, hq_ref.at[rs, :], hf_ref.at[rs, :], hi_ref.at[rs, :], hg_ref.at[rs, :], lb_ref,
                       g_ref, y_ref.at[rs, :], o_ref.at[rs, :], st_all_ref.at[pl.ds(j, 1)], st)
            return carry

        lax.fori_loop(0, cps, chunk, 0, unroll=True)

    def chunk_body(n, hq_ref, hf_ref, hi_ref, hg_ref, lb_ref, g_ref, y_ref, o_ref, st_all_ref, st):
        valid = (n * CHUNK + lax.broadcasted_iota(jnp.int32, (CHUNK, 1), 0)) >= PAD
        hq, hf, v, hg = hq_ref[...], hf_ref[...], hi_ref[...], hg_ref[...]
        lb, sq, q, sf, f, b = _hg_gates(hq, hf, lb_ref, valid)
        k = 1.0 - f
        st_all_ref[0] = st[...]
        causal = _causal()
        v_t = v.T.astype(BF16)
        heads = [slice(h * HG_HEAD_DIM, (h + 1) * HG_HEAD_DIM) for h in range(HG_HEADS)]
        fac = []
        for sl in heads:
            qh, kh, bh = q[:, sl], k[:, sl], b[:, sl]
            b_last, eb, _, ekh, _, q_hat, k_til = _hg_head(qh, kh, bh)
            fac.append((jnp.exp(b_last), (qh * eb).astype(BF16), q_hat.astype(BF16), k_til.astype(BF16),
                        (kh * ekh).astype(BF16), v[:, sl].astype(BF16)))
        raw = []
        for sl, (_, q_til, q_hat, k_til, k_hat, _) in zip(heads, fac):
            st_h = st[sl, :]
            raw.append((_dot_nt(q_til, st_h.astype(BF16)), _dot_nt(q_hat, k_til), _dot(v_t[sl, :], k_hat), st_h))
        for sl, (e_last, _, _, _, _, vb), (inter, att, upd, st_h) in zip(heads, fac, raw):
            o = inter + _dot(jnp.where(causal, att, 0.0).astype(BF16), vb)
            st[sl, :] = st_h * e_last + upd
            o_ref[:, sl] = o
            hgh = hg[:, sl]
            y_ref[:, sl] = (o * _rms(o) * g_ref[...] * (hgh * _sigmoid(hgh))).astype(BF16)

    col = lambda j: pl.BlockSpec((rows, D_HG), lambda n: (n, j))
    return pl.pallas_call(
        body, grid=(n_chunks // cps,),
        in_specs=[col(2), col(3), col(4), col(5),
                  pl.BlockSpec((2, D_HG), lambda n: (0, 0)), pl.BlockSpec((1, HG_HEAD_DIM), lambda n: (0, 0))],
        out_specs=[pl.BlockSpec((rows, D_HG), lambda n: (n, 0)), pl.BlockSpec((rows, D_HG), lambda n: (n, 0)),
                   pl.BlockSpec((cps, D_HG, HG_HEAD_DIM), lambda n: (n, 0, 0))],
        out_shape=[jax.ShapeDtypeStruct((T, D_HG), BF16), jax.ShapeDtypeStruct((T, D_HG), F32),
                   jax.ShapeDtypeStruct((n_chunks, D_HG, HG_HEAD_DIM), F32)],
        scratch_shapes=[pltpu.VMEM((D_HG, HG_HEAD_DIM), F32)],
        name="hg_fwd", compiler_params=_params("arbitrary"),
    )(p, p, p, p, lbraw, hg_g)


def _ffn_fwd(h0, y_rg, y_hg, w_out, g2, w_gu, w_down, gf, target):
    T = h0.shape[0]
    tm = _row_tile(T, 320)
    n_steps = T // tm

    def body(h_ref, yr_ref, yh_ref, wo_ref, g2_ref, wgu_ref, wd_ref, gf_ref, t_hbm,
             h1_ref, v_ref, y_ref, gu_ref, act_ref, dh2_ref, dh2b_ref, loss_ref, gg_ref, tbuf, sems):
        i = pl.program_id(0)
        slot = _fetch_window(t_hbm, tbuf, sems, i, n_steps, tm)

        @pl.when(i == 0)
        def _():
            loss_ref[...] = jnp.zeros_like(loss_ref)
            gg_ref[...] = jnp.zeros_like(gg_ref)
            tbuf[0, 0:HEAD, :] = jnp.zeros((HEAD, D_MODEL), F32)

        y_ref[:, :D_RG] = yr_ref[...]
        y_ref[:, D_RG:] = yh_ref[...]
        h1 = h_ref[...] + _dot(y_ref[...], wo_ref[...])
        h1_ref[...] = h1
        v = (h1 * _rms(h1) * g2_ref[...]).astype(BF16)
        v_ref[...] = v

        gu = _dot(v, wgu_ref[...])
        gu_ref[...] = gu.astype(BF16)
        g = gu[:, :D_FF]
        act = (g * _sigmoid(g) * gu[:, D_FF:]).astype(BF16)
        act_ref[...] = act

        h2 = h1 + _dot(act, wd_ref[...])
        r = _rms(h2)
        n = h2 * r
        gf_ = gf_ref[...]
        row = i * tm + lax.broadcasted_iota(jnp.int32, (tm, 1), 0)
        err = jnp.where(row >= HEAD, n * gf_ - tbuf[slot], 0.0)
        loss_ref[...] += 0.5 * jnp.sum(jnp.mean(err * err, axis=-1, keepdims=True), axis=0, keepdims=True)
        dy = err * (1.0 / D_MODEL)
        gg_ref[...] += jnp.sum(dy * n, axis=0, keepdims=True)
        dh2 = _rms_bwd(dy * gf_, n, r)
        dh2_ref[...] = dh2
        dh2b_ref[...] = dh2.astype(BF16)

    row_spec = lambda n: pl.BlockSpec((tm, n), lambda i: (i, 0))
    vec = pl.BlockSpec((1, D_MODEL), lambda i: (0, 0))
    return pl.pallas_call(
        body, grid=(n_steps,),
        in_specs=[row_spec(D_MODEL), row_spec(D_RG), row_spec(D_HG), _resident((D_MODEL, D_MODEL)), vec,
                  _resident((D_MODEL, 2 * D_FF)), _resident((D_FF, D_MODEL)), vec,
                  pl.BlockSpec(memory_space=pl.ANY)],
        out_specs=[row_spec(D_MODEL), row_spec(D_MODEL), row_spec(D_MODEL), row_spec(2 * D_FF), row_spec(D_FF),
                   row_spec(D_MODEL), row_spec(D_MODEL), pl.BlockSpec((1, 1), lambda i: (0, 0)), vec],
        out_shape=[jax.ShapeDtypeStruct((T, D_MODEL), F32), jax.ShapeDtypeStruct((T, D_MODEL), BF16),
                   jax.ShapeDtypeStruct((T, D_MODEL), BF16), jax.ShapeDtypeStruct((T, 2 * D_FF), BF16),
                   jax.ShapeDtypeStruct((T, D_FF), BF16), jax.ShapeDtypeStruct((T, D_MODEL), F32),
                   jax.ShapeDtypeStruct((T, D_MODEL), BF16), jax.ShapeDtypeStruct((1, 1), F32),
                   jax.ShapeDtypeStruct((1, D_MODEL), F32)],
        scratch_shapes=[pltpu.VMEM((2, tm, D_MODEL), F32), pltpu.SemaphoreType.DMA((2,))],
        name="ffn_fwd", compiler_params=_params("arbitrary"),
    )(h0, y_rg, y_hg, w_out, g2, w_gu, w_down, gf, target)


def _resident(shape):
    return pl.BlockSpec(shape, lambda i: (0,) * len(shape), pipeline_mode=pl.Buffered(1))


def _ffn_bwd(dh2b, gu, w_down, w_gu, h1, g2, dh2, w_out):
    T = h1.shape[0]
    tm = _row_tile(T, 320)

    def body(d_ref, gu_ref, wd_ref, wgu_ref, h_ref, g_ref, d2_ref, wo_ref, dgu_ref, dh1_ref, dh1b_ref, dy_ref, gg_ref):
        i = pl.program_id(0)

        @pl.when(i == 0)
        def _():
            gg_ref[...] = jnp.zeros_like(gg_ref)

        dact = _dot_nt(d_ref[...], wd_ref[...]).astype(BF16)
        g = gu_ref[:, :D_FF]
        u = gu_ref[:, D_FF:]
        s = _sigmoid(g)
        dgu_ref[:, :D_FF] = dact * u * (s * (1.0 + g * (1.0 - s)))
        dgu_ref[:, D_FF:] = dact * (g * s)

        dv = _dot_nt(dgu_ref[...], wgu_ref[...])
        h1_ = h_ref[...]
        r = _rms(h1_)
        n = h1_ * r
        gg_ref[...] += jnp.sum(dv * n, axis=0, keepdims=True)
        dh1 = d2_ref[...] + _rms_bwd(dv * g_ref[...], n, r)
        dh1_ref[...] = dh1
        db = dh1.astype(BF16)
        dh1b_ref[...] = db
        dy_ref[...] = _dot_nt(db, wo_ref[...])

    row = lambda n: pl.BlockSpec((tm, n), lambda i: (i, 0))
    return pl.pallas_call(
        body, grid=(T // tm,),
        in_specs=[row(D_MODEL), row(2 * D_FF), _resident((D_FF, D_MODEL)), _resident((D_MODEL, 2 * D_FF)),
                  row(D_MODEL), pl.BlockSpec((1, D_MODEL), lambda i: (0, 0)), row(D_MODEL),
                  _resident((D_MODEL, D_MODEL))],
        out_specs=[row(2 * D_FF), row(D_MODEL), row(D_MODEL), row(D_MODEL),
                   pl.BlockSpec((1, D_MODEL), lambda i: (0, 0))],
        out_shape=[jax.ShapeDtypeStruct((T, 2 * D_FF), BF16), jax.ShapeDtypeStruct((T, D_MODEL), F32),
                   jax.ShapeDtypeStruct((T, D_MODEL), BF16), jax.ShapeDtypeStruct((T, D_MODEL), F32),
                   jax.ShapeDtypeStruct((1, D_MODEL), F32)],
        name="ffn_bwd", compiler_params=_params("arbitrary"),
    )(dh2b, gu, w_down, w_gu, h1, g2, dh2, w_out)


def _rg_bwd(p, xc_all, hs, dy, dp, cw, cb, wg, bg, lam, rg_g):
    T = p.shape[0]
    tm = _row_tile(T, 832)
    nt = T // tm
    hb = tm // 8
    unroll = _scan_unroll(hb)

    def body(xg_ref, xc_ref, h_ref, hh_ref, dy_ref, dp_in_ref, cw_ref, cb_ref, w_ref, bg_ref, lam_ref, g_ref,
             dp_ref, gcw_ref, gcb_ref, gw_ref, gbg_ref, glam_ref, gg_ref,
             dext, a_s, b_s, d_s, gacc, carry_d, carry_a):
        i = pl.program_id(0)
        t_idx = nt - 1 - i

        @pl.when(i == 0)
        def _():
            dext[tm:tm + 8, :] = jnp.zeros((8, D_RG), F32)
            carry_d[...] = jnp.zeros_like(carry_d)
            carry_a[...] = jnp.zeros_like(carry_a)
            gacc[...] = jnp.zeros_like(gacc)
            for ref in (gcw_ref, gcb_ref, gbg_ref, glam_ref, gg_ref, gw_ref):
                ref[...] = jnp.zeros_like(ref)

        first = t_idx == 0
        xc = xc_ref[...]
        lam_ = lam_ref[...]
        r, ig, sp, a, m, inv_m = _rg_gates(xc, w_ref, bg_ref, lam_)
        row = t_idx * tm + lax.broadcasted_iota(jnp.int32, (tm, 1), 0)
        valid = row >= PAD

        gr = xg_ref[:, D_RG:]
        g, dgelu = _gelu_parts(gr)
        h = h_ref[...]
        yy = g * h
        rr = _rms(yy)
        nn = yy * rr
        dy_ = dy_ref[...]
        gg_ref[...] += jnp.sum(dy_ * nn, axis=0, keepdims=True)
        dyy = _rms_bwd(dy_ * g_ref[...], nn, rr)
        dp_ref[:, D_RG:] = (dyy * h * dgelu).astype(BF16)

        a_s[...] = a
        b_s[...] = dyy * g
        rowi = lax.broadcasted_iota(jnp.int32, (8, D_RG), 0)

        def blk(jj, c):
            cd, ca = c
            for u in range(unroll):
                o = pl.multiple_of((hb - 1 - (jj * unroll + u)) * 8, 8)
                a_blk = a_s[pl.ds(o, 8), :]
                a_next = jnp.where(rowi == 7, ca, pltpu.roll(a_blk, 7, axis=0))
                A, B = _scan_block_bwd(a_next, b_s[pl.ds(o, 8), :], rowi)
                d = B + A * cd
                d_s[pl.ds(o, 8), :] = d
                cd, ca = d[0:1, :], a_blk[0:1, :]
            return cd, ca

        cd, ca = lax.fori_loop(0, hb // unroll, blk, (carry_d[...], carry_a[...]))
        carry_d[...] = cd
        carry_a[...] = ca
        delta = d_s[...]

        h_last_prev = jnp.where(first, 0.0, hh_ref[7:8, :])
        row0 = lax.broadcasted_iota(jnp.int32, (tm, 1), 0) == 0
        h_prev = jnp.where(row0, h_last_prev, pltpu.roll(h, 1, axis=0))
        dbx = jnp.where(valid, delta, 0.0)
        da = delta * h_prev
        di = dbx * m * xc
        dm = dbx * ig * xc
        dla = a * (da - dm * a * inv_m)
        dla = jnp.where(valid, dla, 0.0)
        glam_ref[...] += jnp.sum(dla * r, axis=0, keepdims=True) * (LRU_C / (1.0 + jnp.exp(lam_)))
        dr = (-LRU_C) * sp * dla
        dpre = jnp.concatenate([dr * r * (1.0 - r), di * ig * (1.0 - ig)], axis=1)
        gbg_ref[...] += jnp.sum(dpre, axis=0, keepdims=True)
        dpre_b = dpre.astype(BF16)
        gacc[...] += _dot_tn(xc.astype(BF16), dpre_b)
        dxc = dbx * m * ig + _dot_nt(dpre_b, w_ref[...])
        gcb_ref[...] += jnp.sum(dxc, axis=0, keepdims=True)
        dext[0:tm, :] = dxc
        xr = xg_ref[:, :D_RG]
        dxr = None
        for j in range(CONV_W):
            shifted = dext[3 - j:3 - j + tm, :]
            gcw_ref[j:j + 1, :] += jnp.sum(xr * shifted, axis=0, keepdims=True)
            tap = cw_ref[j:j + 1, :] * shifted
            dxr = tap if dxr is None else dxr + tap
        dp_ref[:, :D_RG] = dxr.astype(BF16)
        dext[tm:tm + 8, :] = dext[0:8, :]

        @pl.when(i == nt - 1)
        def _():
            fold = _head_fold()
            mask = _head_mask()
            fold_b = fold.astype(BF16)
            for k in range(2):
                blockdiag = jnp.where(mask, gacc[:, k * D_RG:(k + 1) * D_RG], 0.0)
                hi = blockdiag.astype(BF16)
                rest = blockdiag - hi.astype(F32)
                mid = rest.astype(BF16)
                lo = (rest - mid.astype(F32)).astype(BF16)
                gw_ref[k * D_RG:(k + 1) * D_RG, :] = (_dot(hi, fold_b) + _dot(mid, fold_b)) + _dot(lo, fold_b)

    vec = lambda n: pl.BlockSpec((1, n), lambda i: (0, 0))
    rev = lambda n: pl.BlockSpec((tm, n), lambda i: (nt - 1 - i, 0))
    halo = lambda n: pl.BlockSpec((8, n), lambda i: (jnp.maximum((nt - 1 - i) * hb - 1, 0), 0))
    return pl.pallas_call(
        body, grid=(nt,),
        in_specs=[rev(2 * D_RG), rev(D_RG), rev(D_RG), halo(D_RG), rev(D_RG), ANY,
                  pl.BlockSpec((CONV_W, D_RG), lambda i: (0, 0)), vec(D_RG),
                  pl.BlockSpec((D_RG, 2 * D_RG), lambda i: (0, 0)), vec(2 * D_RG), vec(D_RG), vec(D_RG)],
        out_specs=[rev(2 * D_RG), pl.BlockSpec((CONV_W, D_RG), lambda i: (0, 0)), vec(D_RG),
                   pl.BlockSpec((2 * D_RG, RG_HEAD_DIM), lambda i: (0, 0)), vec(2 * D_RG), vec(D_RG), vec(D_RG)],
        input_output_aliases={5: 0},
        out_shape=[jax.ShapeDtypeStruct((T, D_IN), BF16), jax.ShapeDtypeStruct((CONV_W, D_RG), F32),
                   jax.ShapeDtypeStruct((1, D_RG), F32), jax.ShapeDtypeStruct((2 * D_RG, RG_HEAD_DIM), F32),
                   jax.ShapeDtypeStruct((1, 2 * D_RG), F32), jax.ShapeDtypeStruct((1, D_RG), F32),
                   jax.ShapeDtypeStruct((1, D_RG), F32)],
        scratch_shapes=[pltpu.VMEM((tm + 8, D_RG), F32),
                        pltpu.VMEM((tm, D_RG), F32), pltpu.VMEM((tm, D_RG), F32), pltpu.VMEM((tm, D_RG), F32),
                        pltpu.VMEM((D_RG, 2 * D_RG), F32), pltpu.VMEM((1, D_RG), F32), pltpu.VMEM((1, D_RG), F32)],
        name="rg_bwd", compiler_params=_params("arbitrary"),
    )(p, xc_all, hs, hs, dy, dp, cw, cb, wg, bg, lam, rg_g)


def _hg_bwd(p, o_all, st_all, dy, lbraw, hg_g):
    T = p.shape[0]
    n_chunks = T // CHUNK
    cps = _chunks_per_step(n_chunks)
    rows = cps * CHUNK
    n_steps = n_chunks // cps

    def body(hq_ref, hf_ref, hi_ref, hg_ref, o_ref, st_ref, dy_ref, lb_ref, g_ref,
             dp_ref, glb_ref, gg_ref, dst):
        i = pl.program_id(0)

        @pl.when(i == 0)
        def _():
            dst[...] = jnp.zeros_like(dst)
            glb_ref[...] = jnp.zeros_like(glb_ref)
            gg_ref[...] = jnp.zeros_like(gg_ref)

        dp_ref[:, :2 * D_RG] = jnp.zeros((rows, 2 * D_RG), BF16)

        def chunk(jj, carry):
            j = cps - 1 - jj
            rs = pl.ds(pl.multiple_of(j * CHUNK, CHUNK), CHUNK)
            chunk_body((n_steps - 1 - i) * cps + j, hq_ref.at[rs, :], hf_ref.at[rs, :], hi_ref.at[rs, :],
                       hg_ref.at[rs, :], o_ref.at[rs, :], st_ref.at[pl.ds(j, 1)], dy_ref.at[rs, :], lb_ref, g_ref,
                       dp_ref.at[rs, pl.ds(2 * D_RG, 4 * D_HG)], glb_ref, gg_ref, dst)
            return carry

        lax.fori_loop(0, cps, chunk, 0, unroll=True)

    def chunk_body(n, hq_ref, hf_ref, hi_ref, hg_ref, o_ref, st_ref, dy_ref, lb_ref, g_ref,
                   dp_ref, glb_ref, gg_ref, dst):
        valid = (n * CHUNK + lax.broadcasted_iota(jnp.int32, (CHUNK, 1), 0)) >= PAD
        hq, hf, v, hg = hq_ref[...], hf_ref[...], hi_ref[...], hg_ref[...]
        lb, sq, q, sf, f, b = _hg_gates(hq, hf, lb_ref, valid)
        k = 1.0 - f
        causal = _causal()
        r_i = lax.broadcasted_iota(jnp.int32, (CHUNK, CHUNK), 0)
        c_i = lax.broadcasted_iota(jnp.int32, (CHUNK, CHUNK), 1)
        causal_t = r_i <= c_i
        is_last = lax.broadcasted_iota(jnp.int32, (CHUNK, 1), 0) == CHUNK - 1
        g_ = g_ref[...]
        db_parts, dq_parts, dk_parts = [], [], []
        gg = jnp.zeros((1, HG_HEAD_DIM), F32)
        heads = [slice(h * HG_HEAD_DIM, (h + 1) * HG_HEAD_DIM) for h in range(HG_HEADS)]

        do_parts = []
        for h, sl in enumerate(heads):
            o = o_ref[:, sl]
            ro = _rms(o)
            no = o * ro
            hgh = hg[:, sl]
            sg = _sigmoid(hgh)
            dyh = dy_ref[:, sl]
            dp_ref[:, 3 * D_HG + h * HG_HEAD_DIM:3 * D_HG + (h + 1) * HG_HEAD_DIM] = (
                dyh * no * g_ * sg * (1.0 + hgh * (1.0 - sg))).astype(BF16)
            dng = dyh * hgh * sg
            gg = gg + jnp.sum(dng * no, axis=0, keepdims=True)
            do_parts.append(_rms_bwd(dng * g_, no, ro))
        do_t = jnp.concatenate(do_parts, axis=1).T.astype(BF16)

        fac = []
        for sl, do in zip(heads, do_parts):
            qh, kh, bh = q[:, sl], k[:, sl], b[:, sl]
            b_last, eb, eq, ekh, ek, q_hat, k_til = _hg_head(qh, kh, bh)
            fac.append(dict(qh=qh, kh=kh, e_last=jnp.exp(b_last), eb=eb, eq=eq, ekh=ekh, ek=ek,
                            q_til=qh * eb, k_hat=kh * ekh, qhb=q_hat.astype(BF16), ktb=k_til.astype(BF16),
                            vb=v[:, sl].astype(BF16), dob=do.astype(BF16)))

        first = []
        for sl, t in zip(heads, fac):
            st_h = st_ref[0, sl, :]
            dst_h = dst[sl, :]
            dstb = dst_h.astype(BF16)
            first.append(dict(
                att_t=_dot_nt(t["ktb"], t["qhb"]), datt=_dot_nt(t["dob"], t["vb"]),
                datt_t=_dot_nt(t["vb"], t["dob"]), dk_hat=_dot(t["vb"], dstb),
                dv=_dot_nt(t["k_hat"].astype(BF16), dstb), dq_til=_dot(t["dob"], st_h.astype(BF16)),
                state=t["e_last"] * jnp.sum(dst_h * st_h, axis=0, keepdims=True)))
            dst[sl, :] = dst_h * t["e_last"] + _dot(do_t[sl, :], t["q_til"].astype(BF16))

        for h, (t, m) in enumerate(zip(fac, first)):
            qh, kh, eb, eq, ekh, ek = t["qh"], t["kh"], t["eb"], t["eq"], t["ekh"], t["ek"]
            q_til, k_hat, qhb, ktb, dob = t["q_til"], t["k_hat"], t["qhb"], t["ktb"], t["dob"]
            dk_hat, dq_til = m["dk_hat"], m["dq_til"]
            dv = m["dv"] + _dot(jnp.where(causal_t, m["att_t"], 0.0).astype(BF16), dob)
            dq_hat = _dot(jnp.where(causal, m["datt"], 0.0).astype(BF16), ktb)
            dk_til = _dot(jnp.where(causal_t, m["datt_t"], 0.0).astype(BF16), qhb)
            db_last = jnp.sum(dk_hat * k_hat, axis=0, keepdims=True) + m["state"]
            dq_sel = jnp.concatenate([dq_hat[SUB * s:SUB * (s + 1), s * HG_HEAD_DIM:(s + 1) * HG_HEAD_DIM]
                                      for s in range(N_SUB)], axis=0)
            dq_a = dq_sel * eq
            dk_rows, k_att_rows = [], []
            for b_ in range(N_SUB):
                rs = slice(SUB * b_, SUB * (b_ + 1))
                dk_sum = k_att_sum = None
                for s in range(b_, N_SUB):
                    cs = slice(s * HG_HEAD_DIM, (s + 1) * HG_HEAD_DIM)
                    d = dk_til[rs, cs]
                    t_dk = d * ek[s][rs, :]
                    t_att = ktb[rs, cs].astype(F32) * d
                    dk_sum = t_dk if dk_sum is None else dk_sum + t_dk
                    k_att_sum = t_att if k_att_sum is None else k_att_sum + t_att
                dk_rows.append(dk_sum)
                k_att_rows.append(k_att_sum)
            dk_a = jnp.concatenate(dk_rows, axis=0)
            db = (dq_til * q_til - dk_hat * k_hat + (qh * eq).astype(BF16).astype(F32) * dq_sel
                  - jnp.concatenate(k_att_rows, axis=0))
            db_parts.append(jnp.where(is_last, db + db_last, db))
            dq_parts.append(dq_til * eb + dq_a)
            dk_parts.append(dk_hat * ekh + dk_a)
            dp_ref[:, 2 * D_HG + h * HG_HEAD_DIM:2 * D_HG + (h + 1) * HG_HEAD_DIM] = dv.astype(BF16)

        gg_ref[...] += gg
        db = jnp.concatenate(db_parts, axis=1)
        dq = jnp.concatenate(dq_parts, axis=1)
        dk = jnp.concatenate(dk_parts, axis=1)
        dlf = jnp.where(valid, _running_sum(db, False), 0.0)
        dp_ref[:, :D_HG] = (dq * sq * (1.0 + hq * (1.0 - sq))).astype(BF16)
        df = dlf / f - dk
        dlb = jnp.sum(df * (1.0 - sf), axis=0, keepdims=True) * lb * (1.0 - lb)
        glb_ref[0:1, :] += dlb
        glb_ref[1:2, :] += -dlb
        dp_ref[:, D_HG:2 * D_HG] = (df * (1.0 - lb) * sf * (1.0 - sf)).astype(BF16)

    rev = lambda j: pl.BlockSpec((rows, D_HG), lambda i: (n_steps - 1 - i, j))
    return pl.pallas_call(
        body, grid=(n_steps,),
        in_specs=[rev(2), rev(3), rev(4), rev(5), rev(0),
                  pl.BlockSpec((cps, D_HG, HG_HEAD_DIM), lambda i: (n_steps - 1 - i, 0, 0)), rev(1),
                  pl.BlockSpec((2, D_HG), lambda i: (0, 0)), pl.BlockSpec((1, HG_HEAD_DIM), lambda i: (0, 0))],
        out_specs=[pl.BlockSpec((rows, D_IN), lambda i: (n_steps - 1 - i, 0)),
                   pl.BlockSpec((2, D_HG), lambda i: (0, 0)), pl.BlockSpec((1, HG_HEAD_DIM), lambda i: (0, 0))],
        out_shape=[jax.ShapeDtypeStruct((T, D_IN), BF16), jax.ShapeDtypeStruct((2, D_HG), F32),
                   jax.ShapeDtypeStruct((1, HG_HEAD_DIM), F32)],
        scratch_shapes=[pltpu.VMEM((D_HG, HG_HEAD_DIM), F32)],
        name="hg_bwd", compiler_params=_params("arbitrary"),
    )(p, p, p, p, o_all, st_all, dy, lbraw, hg_g)


def _in_bwd(dp, w_in, h0, g1, dh1):
    T = h0.shape[0]
    tm = _row_tile(T, 832)
    n_steps = T // tm

    def body(dp_ref, w_ref, h_ref, g_ref, d1_ref, gx_hbm, gmeta_ref, gg_ref, buf, sems):
        i = pl.program_id(0)
        first, later = _window_copies(gx_hbm, buf, sems, tm)
        slot = i % 2

        @pl.when(i == 0)
        def _():
            gg_ref[...] = jnp.zeros_like(gg_ref)

        if n_steps > 2:
            @pl.when(i == 2)
            def _():
                first(False).wait()

            @pl.when(i > 2)
            def _():
                later(i - 2, slot, False).wait()

        du = _dot_nt(dp_ref[...], w_ref[...])
        h0_ = h_ref[...]
        r = _rms(h0_)
        n = h0_ * r
        gg_ref[...] += jnp.sum(du * n, axis=0, keepdims=True)
        dh0 = d1_ref[...] + _rms_bwd(du * g_ref[...], n, r)
        buf[slot] = dh0

        @pl.when(i == 0)
        def _():
            gmeta_ref[...] = dh0[PAD:HEAD, :]
            first(False).start()

        if n_steps > 1:
            @pl.when(i > 0)
            def _():
                later(i, slot, False).start()

        @pl.when(i == n_steps - 1)
        def _():
            if n_steps == 1:
                first(False).wait()
            else:
                if n_steps == 2:
                    first(False).wait()
                else:
                    later(i - 1, 1 - slot, False).wait()
                later(i, slot, False).wait()

    row = lambda n: pl.BlockSpec((tm, n), lambda i: (i, 0))
    return pl.pallas_call(
        body, grid=(n_steps,),
        in_specs=[row(D_IN), _resident((D_MODEL, D_IN)),
                  row(D_MODEL), pl.BlockSpec((1, D_MODEL), lambda i: (0, 0)), row(D_MODEL)],
        out_specs=[pl.BlockSpec(memory_space=pl.ANY), pl.BlockSpec((N_META, D_MODEL), lambda i: (0, 0)),
                   pl.BlockSpec((1, D_MODEL), lambda i: (0, 0))],
        out_shape=[jax.ShapeDtypeStruct((T - HEAD, D_MODEL), F32), jax.ShapeDtypeStruct((N_META, D_MODEL), F32),
                   jax.ShapeDtypeStruct((1, D_MODEL), F32)],
        scratch_shapes=[pltpu.VMEM((2, tm, D_MODEL), F32), pltpu.SemaphoreType.DMA((2,))],
        name="in_bwd", compiler_params=_params("arbitrary"),
    )(dp, w_in, h0, g1, dh1)


def _col_tile(cols, target):
    best = None
    for t in range(128, min(cols, target) + 1, 128):
        if cols % t == 0:
            best = t
    assert best is not None, cols
    return best


MXU_DIM = 256


def _mxu_tile(cols, target):
    best = None
    for t in range(MXU_DIM, min(cols, target) + 1, MXU_DIM):
        if cols % t == 0:
            best = t
    assert best is not None, cols
    return best


def _weight_grad(a, b, name):
    T, M = a.shape
    N = b.shape[1]
    tm = _col_tile(M, 1408)
    tn = _mxu_tile(N, 768 if tm <= 1024 else 512)

    def body(a_ref, b_ref, o_ref, ob_ref):
        o = _dot_tn(a_ref[...], b_ref[...])
        o_ref[...] = o
        ob_ref[...] = o.astype(BF16)

    return pl.pallas_call(
        body, grid=(M // tm, N // tn),
        in_specs=[pl.BlockSpec((T, tm), lambda m, n: (0, m)), pl.BlockSpec((T, tn), lambda m, n: (0, n))],
        out_specs=[pl.BlockSpec((tm, tn), lambda m, n: (m, n))] * 2,
        out_shape=[jax.ShapeDtypeStruct((M, N), F32), jax.ShapeDtypeStruct((M, N), BF16)],
        name=name, compiler_params=_params("parallel", "parallel"),
    )(a, b)


def _local_step(x, meta, target, w_in_own, w_in, w_out, w_gu, w_down, small, chip, on_ffn_grads=None,
                on_mixer_grads=None):
    wg = _gate_weights(small["w_rgate"], small["w_igate"])
    bg = jnp.concatenate([small["b_rgate"], small["b_igate"]], axis=1)

    p, u, h0 = _in_proj_local(x, meta, small["mix_norm_g"], w_in_own, chip)
    p = _in_proj_rest(u, w_in, p, chip)
    y_rg, hs, xc = _rg_fwd(p, small["conv_w"], small["conv_b"], wg, bg, small["lru_lambda"], small["rg_norm_g"])
    y_hg, o_all, st_all = _hg_fwd(p, small["hg_lower_bound"], small["hg_norm_g"])
    h1, v, yb, gu, act, dh2, dh2b, loss, g_final = _ffn_fwd(
        h0, y_rg, y_hg, w_out, small["ffn_norm_g"], w_gu, w_down, small["final_norm_g"], target)

    g_w_down = _weight_grad(act, dh2b, "grad_w_down")
    dgu, dh1, dh1b, dy, g_ffn = _ffn_bwd(dh2b, gu, w_down, w_gu, h1, small["ffn_norm_g"], dh2, w_out)
    ffn_grads = {"w_gate_up": _weight_grad(v, dgu, "grad_w_gate_up"), "w_down": g_w_down,
                 "w_out": _weight_grad(yb, dh1b, "grad_w_out")}
    stages = on_ffn_grads(ffn_grads) if on_ffn_grads is not None else None
    dp, g_lb, g_hgn = _hg_bwd(p, o_all, st_all, dy, small["hg_lower_bound"], small["hg_norm_g"])
    early = late = None
    if stages is not None:
        chip_sums, send = stages
        sums = chip_sums()
        (dp, dy), sums = lax.optimization_barrier(((dp, dy), sums))
        early = send(sums)
    dp, g_cw, g_cb, g_wgate, g_bg, g_lam, g_rgn = _rg_bwd(
        p, xc, hs, dy, dp, small["conv_w"], small["conv_b"], wg, bg, small["lru_lambda"], small["rg_norm_g"])
    mixer_grads = {"w_in": _weight_grad(u, dp, "grad_w_in")}
    if on_mixer_grads is not None:
        chip_sums, send = on_mixer_grads(mixer_grads)
        sums = chip_sums()
        (dp, dh1), sums = lax.optimization_barrier(((dp, dh1), sums))
        late = send(sums)
    grad_x, g_meta, g_mix = _in_bwd(dp, w_in, h0, small["mix_norm_g"], dh1)

    grads = {
        "w_in": mixer_grads["w_in"][0], "w_out": ffn_grads["w_out"][0],
        "w_gate_up": ffn_grads["w_gate_up"][0], "w_down": ffn_grads["w_down"][0],
        "meta_tokens": g_meta, "mix_norm_g": g_mix, "conv_w": g_cw, "conv_b": g_cb, "w_gates": g_wgate,
        "b_rgate": g_bg[:, :D_RG], "b_igate": g_bg[:, D_RG:], "lru_lambda": g_lam, "rg_norm_g": g_rgn,
        "hg_lower_bound": g_lb, "hg_norm_g": g_hgn, "ffn_norm_g": g_ffn, "final_norm_g": g_final,
    }
    return loss, grad_x, grads, early, late


ANY = pl.BlockSpec(memory_space=pl.ANY)
HALF = D_MODEL // 2

BIG = {"w_in": (D_MODEL, D_IN // N_CHIPS, True), "w_gate_up": (D_MODEL, 2 * D_FF // N_CHIPS, True),
       "w_out": (D_MODEL // N_CHIPS, D_MODEL, False), "w_down": (D_FF // N_CHIPS, D_MODEL, False)}
BIG_NAMES = tuple(BIG)
N_BIG = len(BIG_NAMES)


def _full_shape(name):
    rows, cols, by_col = BIG[name]
    return (rows, cols * N_CHIPS) if by_col else (rows * N_CHIPS, cols)


def _place():
    return lax.axis_index("x"), lax.axis_index("y"), lax.axis_index("c")


def _chip_of(x, y, r):
    fx, fy = (r + 1) >> 1, (r + 1) & 1
    return (1 - x if fx else x), (1 - y if fy else y)


def _half_of(ref, by_col, half):
    start = pl.multiple_of(half * HALF, 128)
    return ref.at[pl.ds(start, HALF), :] if by_col else ref.at[:, pl.ds(start, HALF)]


def _shard_of(ref, name, chip):
    rows, cols, by_col = BIG[name]
    if by_col:
        return ref.at[:, pl.ds(pl.multiple_of(chip * cols, 128), cols)]
    return ref.at[pl.ds(pl.multiple_of(chip * rows, 16), rows), :]


def _shard_half_of(ref, name, chip, half):
    rows, cols, by_col = BIG[name]
    start = pl.multiple_of(half * HALF, 128)
    if by_col:
        return ref.at[pl.ds(start, HALF), pl.ds(pl.multiple_of(chip * cols, 128), cols)]
    return ref.at[pl.ds(pl.multiple_of(chip * rows, 16), rows), pl.ds(start, HALF)]


def _shard_half_part_of(ref, name, chip, half, part):
    rows, cols, by_col = BIG[name]
    start = pl.multiple_of(half * HALF + part * (HALF // 2), 128)
    if by_col:
        return ref.at[pl.ds(start, HALF // 2), pl.ds(pl.multiple_of(chip * cols, 128), cols)]
    return ref.at[pl.ds(pl.multiple_of(chip * rows, 16), rows), pl.ds(start, HALF // 2)]


def _remote(src, dst, send_sems, recv_sems, k, dev):
    return pltpu.make_async_remote_copy(src_ref=src, dst_ref=dst, send_sem=send_sems.at[k], recv_sem=recv_sems.at[k],
                                        device_id=dev, device_id_type=MESH)


def _place_shards(w, small, chip, names, label):
    steps = 4
    n, ns = len(names), len(small)
    in_specs, out_specs = [], []
    for name in names:
        rows, cols, by_col = BIG[name]
        tr = rows // steps
        in_specs.append(pl.BlockSpec((tr, cols), lambda i, s: (i, 0)))
        if by_col:
            out_specs.append(pl.BlockSpec((tr, cols), lambda i, s: (i, s[0])))
        else:
            out_specs.append(pl.BlockSpec((tr, cols), lambda i, s: (s[0] * steps + i, 0)))

    def body(s_ref, *refs):
        ins, small_in = refs[:n], refs[n:n + ns]
        outs, small_out = refs[n + ns:2 * n + ns], refs[2 * n + ns:2 * (n + ns)]
        send_sems, recv_sems, local_sems = refs[2 * (n + ns):]
        i = pl.program_id(0)
        x, y, c = _place()
        chip_ = 2 * x + y
        others = [_chip_of(x, y, r) for r in range(3)]

        def block(a, q):
            cols = small[a].shape[1]
            return small_out[a].at[:, pl.ds(pl.multiple_of(q * cols, 128), cols)]

        def local(a):
            return pltpu.make_async_copy(small_in[a], block(a, chip_), local_sems.at[a])

        def remote(a, r):
            qx, qy = others[r]
            return _remote(small_in[a], block(a, chip_), send_sems, recv_sems, 3 * a + r, (qx, qy, c))

        @pl.when(i == 0)
        def _():
            for a in range(ns):
                local(a).start()
                for r in range(3):
                    remote(a, r).start()

        for a in range(n):
            outs[a][...] = ins[a][...].astype(BF16)

        @pl.when(i == steps - 1)
        def _():
            for a in range(ns):
                for r, (qx, qy) in enumerate(others):
                    landed = block(a, 2 * qx + qy)
                    _remote(landed, landed, send_sems, recv_sems, 3 * a + r, (qx, qy, c)).wait_recv()
                for r in range(3):
                    remote(a, r).wait_send()
                local(a).wait()

    out = pl.pallas_call(
        body,
        grid_spec=pltpu.PrefetchScalarGridSpec(
            num_scalar_prefetch=1, grid=(steps,), in_specs=in_specs + [ANY] * ns, out_specs=out_specs + [ANY] * ns,
            scratch_shapes=[pltpu.SemaphoreType.DMA((max(3 * ns, 1),)), pltpu.SemaphoreType.DMA((max(3 * ns, 1),)),
                            pltpu.SemaphoreType.DMA((max(ns, 1),))]),
        out_shape=([jax.ShapeDtypeStruct(_full_shape(name), BF16) for name in names]
                   + [jax.ShapeDtypeStruct((s.shape[0], s.shape[1] * N_CHIPS), F32) for s in small]),
        name=label, compiler_params=_params("arbitrary"),
    )(chip, *[w[name] for name in names], *small)
    return dict(zip(names, out[:n])), list(out[n:])


def _gather_weights(placed, small, names, label, collective_id):
    n, ns = len(names), len(small)
    hbm = pltpu.MemorySpace.HBM
    outs = [jax.new_ref(placed[nm], memory_space=hbm) for nm in names]
    small_in = [jax.new_ref(s, memory_space=hbm) for s in small]
    small_out = [jax.empty_ref(jax.ShapeDtypeStruct((s.shape[0], s.shape[1] * N_CHIPS), F32), memory_space=hbm)
                 for s in small]
    n_sems = 8 * n + 3 * ns

    @pl.kernel(mesh=plsc.ScalarSubcoreMesh(axis_name="seq", num_cores=1), name=label, out_type=(),
               scratch_types=(pltpu.SemaphoreType.DMA((n_sems,)), pltpu.SemaphoreType.DMA((n_sems,)),
                              pltpu.SemaphoreType.DMA((max(ns, 1),))),
               compiler_params=pltpu.CompilerParams(collective_id=collective_id))
    def launch(send_sems, recv_sems, local_sems):
        x, y, c = _place()
        chip = 2 * x + y
        sibling = (x, y, 1 - c)
        others = [_chip_of(x, y, r) for r in range(3)]
        near = others[:2]
        far = 2 * others[2][0] + others[2][1]
        _handshake([(qx, qy, c) for qx, qy in others] + [sibling])

        def small_block(a, q):
            cols = small[a].shape[1]
            return small_out[a].at[:, pl.ds(pl.multiple_of(q * cols, 128), cols)]

        local = [pltpu.make_async_copy(small_in[a], small_block(a, chip), local_sems.at[a]) for a in range(ns)]
        for cp in local:
            cp.start()

        sends = []
        for a, name in enumerate(names):
            mine = _shard_half_of(outs[a], name, chip, c)
            for r, (qx, qy) in enumerate(near):
                sends.append(_remote(mine, mine, send_sems, recv_sems, 8 * a + r, (qx, qy, c)))
        for a in range(ns):
            for r, (qx, qy) in enumerate(others):
                sends.append(_remote(small_in[a], small_block(a, chip), send_sems, recv_sems,
                                     8 * n + 3 * a + r, (qx, qy, c)))
        for cp in sends:
            cp.start()

        forwards = []

        def forward(piece, k, dev):
            cp = _remote(piece, piece, send_sems, recv_sems, k, dev)
            cp.start()
            forwards.append(cp)

        for a, name in enumerate(names):
            for r, (qx, qy) in enumerate(near):
                landed = _shard_half_of(outs[a], name, 2 * qx + qy, c)
                _remote(landed, landed, send_sems, recv_sems, 8 * a + r, (qx, qy, c)).wait_recv()
                ox, oy = near[1 - r]
                forward(_shard_half_part_of(outs[a], name, 2 * qx + qy, c, r), 8 * a + 2 + r, (ox, oy, c))
                forward(landed, 8 * a + 4 + r, sibling)
        for a, name in enumerate(names):
            for part in range(2):
                qx, qy = near[1 - part]
                landed = _shard_half_part_of(outs[a], name, far, c, part)
                _remote(landed, landed, send_sems, recv_sems, 8 * a + 2 + part, (qx, qy, c)).wait_recv()
                forward(landed, 8 * a + 6 + part, sibling)
        for a in range(ns):
            for r, (qx, qy) in enumerate(others):
                landed = small_block(a, 2 * qx + qy)
                _remote(landed, landed, send_sems, recv_sems, 8 * n + 3 * a + r, (qx, qy, c)).wait_recv()
        for a, name in enumerate(names):
            for r, (qx, qy) in enumerate(near):
                landed = _shard_half_of(outs[a], name, 2 * qx + qy, 1 - c)
                _remote(landed, landed, send_sems, recv_sems, 8 * a + 4 + r, sibling).wait_recv()
            for part in range(2):
                landed = _shard_half_part_of(outs[a], name, far, 1 - c, part)
                _remote(landed, landed, send_sems, recv_sems, 8 * a + 6 + part, sibling).wait_recv()
        for cp in sends + forwards:
            cp.wait_send()
        for cp in local:
            cp.wait()

    launch()
    return {nm: ref[...] for nm, ref in zip(names, outs)}, [ref[...] for ref in small_out]


def _exchange_halves(grads, names, label, collective_id):
    n = len(names)
    sequencer = collective_id is not None

    def body(*refs):
        ins, outs = refs[:n], refs[n:2 * n]
        send_sems, recv_sems = refs[2 * n:]
        x, y, c = _place()
        if sequencer:
            _handshake([(x, y, 1 - c)])
        copies = []
        for a, name in enumerate(names):
            copies.append(_remote(_half_of(ins[a], BIG[name][2], 1 - c), outs[a], send_sems, recv_sems, a,
                                  (x, y, 1 - c)))
        for cp in copies:
            cp.start()
        for cp in copies:
            cp.wait()

    def half_shape(name):
        r, c_ = _full_shape(name)
        return (HALF, c_) if BIG[name][2] else (r, HALF)

    out_type = tuple(jax.ShapeDtypeStruct(half_shape(nm), grads[nm].dtype) for nm in names)
    sems = (pltpu.SemaphoreType.DMA((n,)), pltpu.SemaphoreType.DMA((n,)))
    operands = [grads[nm] for nm in names]
    if sequencer:
        got = pl.kernel(
            body, mesh=plsc.ScalarSubcoreMesh(axis_name="seq", num_cores=1), name=label, out_type=out_type,
            scratch_types=sems, compiler_params=pltpu.CompilerParams(collective_id=collective_id),
        )(*operands)
    else:
        got = pl.pallas_call(
            body, in_specs=[ANY] * n, out_specs=[ANY] * n, out_shape=list(out_type), scratch_shapes=list(sems),
            name=label,
        )(*operands)
    return dict(zip(names, got))


def _chip_sum(grads, got, names, core, label):
    n = len(names)
    steps = 8
    g_specs, blks = [], []
    for name in names:
        rows, cols = got[name].shape
        tr = rows // steps
        if BIG[name][2]:
            g_specs.append(pl.BlockSpec((tr, cols), lambda i, s: (s[0] * steps + i, 0)))
        else:
            g_specs.append(pl.BlockSpec((tr, HALF), lambda i, s: (i, s[0])))
        blks.append(pl.BlockSpec((tr, cols), lambda i, s: (i, 0)))

    def body(s_ref, *refs):
        for a in range(n):
            t = refs[a][...] + refs[n + a][...].astype(F32)
            refs[2 * n + a][...] = t
            refs[3 * n + a][...] = t.astype(BF16)

    out = pl.pallas_call(
        body,
        grid_spec=pltpu.PrefetchScalarGridSpec(num_scalar_prefetch=1, grid=(steps,), in_specs=g_specs + blks,
                                               out_specs=blks + blks),
        out_shape=([jax.ShapeDtypeStruct(got[nm].shape, F32) for nm in names]
                   + [jax.ShapeDtypeStruct(got[nm].shape, BF16) for nm in names]),
        name=label, compiler_params=_params("parallel"),
    )(core, *[grads[nm] for nm in names], *[got[nm] for nm in names])
    return {nm: (out[a], out[n + a]) for a, nm in enumerate(names)}


def _piece_shape(name):
    rows, cols, by_col = BIG[name]
    return (HALF, cols) if by_col else (rows, HALF)


def _handshake(peers):
    barrier = pltpu.get_barrier_semaphore()
    for peer in peers:
        pl.semaphore_signal(barrier, inc=1, device_id=peer, device_id_type=MESH)
    pl.semaphore_wait(barrier, len(peers))


def _send_chip_sums(sums, names, label, collective_id):
    n = len(names)

    def body(*refs):
        ins, outs = refs[:n], refs[n:2 * n]
        send_sems, recv_sems = refs[2 * n:]
        x, y, c = _place()
        others = [_chip_of(x, y, r) for r in range(3)]
        _handshake([(qx, qy, c) for qx, qy in others])
        copies = []
        for a, name in enumerate(names):
            for r, (qx, qy) in enumerate(others):
                copies.append(_remote(_shard_of(ins[a], name, 2 * qx + qy), outs[a].at[r], send_sems, recv_sems,
                                      3 * a + r, (qx, qy, c)))
        for cp in copies:
            cp.start()
        for cp in copies:
            cp.wait()

    return pl.kernel(
        body, mesh=plsc.ScalarSubcoreMesh(axis_name="seq", num_cores=1), name=label,
        out_type=tuple(jax.ShapeDtypeStruct((3,) + _piece_shape(nm), BF16) for nm in names),
        scratch_types=(pltpu.SemaphoreType.DMA((3 * n,)), pltpu.SemaphoreType.DMA((3 * n,))),
        compiler_params=pltpu.CompilerParams(collective_id=collective_id),
    )(*[sums[nm] for nm in names])


def _total(parts, chip_core):
    steps = 4
    in_specs, out_specs, operands = [], [], []
    for name in BIG_NAMES:
        by_col = BIG[name][2]
        pr, pc = _piece_shape(name)
        tr = pr // steps
        if by_col:
            in_specs.append(pl.BlockSpec((tr, pc), lambda i, s: (i, s[0])))
            out_specs.append(pl.BlockSpec((tr, pc), lambda i, s: (s[1] * steps + i, 0)))
        else:
            in_specs.append(pl.BlockSpec((tr, pc), lambda i, s: (s[0] * steps + i, 0)))
            out_specs.append(pl.BlockSpec((tr, pc), lambda i, s: (i, s[1])))
        for r in range(3):
            in_specs.append(pl.BlockSpec((None, tr, pc), lambda i, s, r=r: (r, i, 0)))
        own, got = parts[name]
        operands += [own, got, got, got]

    def body(s_ref, *refs):
        for a in range(N_BIG):
            o_ref, a_ref, b_ref, c_ref = refs[4 * a:4 * a + 4]
            refs[4 * N_BIG + a][...] = (((o_ref[...] + a_ref[...].astype(F32)) + b_ref[...].astype(F32))
                                        + c_ref[...].astype(F32))

    totals = pl.pallas_call(
        body,
        grid_spec=pltpu.PrefetchScalarGridSpec(num_scalar_prefetch=1, grid=(steps,), in_specs=in_specs,
                                               out_specs=out_specs),
        out_shape=[jax.ShapeDtypeStruct(BIG[name][:2], F32) for name in BIG_NAMES],
        name="totals", compiler_params=_params("parallel"),
    )(chip_core, *operands)
    return dict(zip(BIG_NAMES, totals))


VEC_ROWS = 32
VEC_ROW = {"mix_norm_g": 0, "conv_b": 1, "b_rgate": 2, "b_igate": 3, "lru_lambda": 4, "rg_norm_g": 5,
           "hg_lower_bound": 6, "hg_norm_g": 8, "ffn_norm_g": 9, "final_norm_g": 10, "loss": 11,
           "conv_w": 12, "meta_tokens": 16}
N_DEV = 8


def _all_reduce_small(pieces, gates, totals):
    names = list(pieces)
    n_small = 10
    hv, hg = VEC_ROWS // 2, gates.shape[0] // 2

    def body(*refs):
        ins = refs[:len(names)]
        g_ref = refs[len(names)]
        vec_ref, gsum_ref = refs[len(names) + 1 + N_BIG:len(names) + 3 + N_BIG]
        big = refs[len(names) + 3 + N_BIG:len(names) + 3 + 2 * N_BIG]
        (mine_v, sib_v, sib_g, chip_v, chip_g, got_v, got_g, send_sems, recv_sems) = refs[len(names) + 3 + 2 * N_BIG:]
        x, y, c = _place()
        chip = 2 * x + y
        sibling = (x, y, 1 - c)
        share = []
        for a, name in enumerate(BIG_NAMES):
            half = _half_of(big[a], BIG[name][2], c)
            share.append(_remote(half, half, send_sems, recv_sems, n_small + a, sibling))
        mine_v[...] = jnp.zeros_like(mine_v)
        for name, ref in zip(names, ins):
            nr, w = ref.shape
            mine_v[VEC_ROW[name]:VEC_ROW[name] + nr, 0:w] = ref[...]

        swap = [_remote(mine_v, sib_v, send_sems, recv_sems, 0, sibling),
                _remote(g_ref, sib_g, send_sems, recv_sems, 1, sibling)]
        for cp in swap:
            cp.start()
        for cp in swap:
            cp.wait()
        for cp in share:
            cp.start()
        chip_v[...] = mine_v[...] + sib_v[...]
        chip_g[...] = g_ref[...] + sib_g[...]

        rows_v = pl.ds(pl.multiple_of(c * hv, 8), hv)
        rows_g = pl.ds(pl.multiple_of(c * hg, 8), hg)
        got_v[chip] = chip_v[rows_v, :]
        got_g[chip] = chip_g[rows_g, :].astype(BF16)
        sends = []
        for r in range(3):
            qx, qy = _chip_of(x, y, r)
            sends.append(_remote(chip_v.at[rows_v, :], got_v.at[chip], send_sems, recv_sems, 2 + r, (qx, qy, c)))
            sends.append(_remote(got_g.at[chip], got_g.at[chip], send_sems, recv_sems, 5 + r, (qx, qy, c)))
        for cp in sends:
            cp.start()
        for cp in sends:
            cp.wait()
        vec_ref[rows_v, :] = ((got_v[0] + got_v[1]) + got_v[2]) + got_v[3]
        gsum_ref[rows_g, :] = ((got_g[0].astype(F32) + got_g[1].astype(F32)) + got_g[2].astype(F32)
                               ) + got_g[3].astype(F32)

        back = [_remote(vec_ref.at[rows_v, :], vec_ref.at[rows_v, :], send_sems, recv_sems, 8, sibling),
                _remote(gsum_ref.at[rows_g, :], gsum_ref.at[rows_g, :], send_sems, recv_sems, 9, sibling)]
        for cp in back:
            cp.start()
        theirs_v = vec_ref.at[pl.ds(pl.multiple_of((1 - c) * hv, 8), hv), :]
        theirs_g = gsum_ref.at[pl.ds(pl.multiple_of((1 - c) * hg, 8), hg), :]
        _remote(theirs_v, theirs_v, send_sems, recv_sems, 8, sibling).wait_recv()
        _remote(theirs_g, theirs_g, send_sems, recv_sems, 9, sibling).wait_recv()
        for cp in back:
            cp.wait_send()
        for a, name in enumerate(BIG_NAMES):
            theirs = _half_of(big[a], BIG[name][2], 1 - c)
            _remote(theirs, theirs, send_sems, recv_sems, n_small + a, sibling).wait_recv()
        for cp in share:
            cp.wait_send()

    vmem = pl.BlockSpec(memory_space=pltpu.VMEM)
    n_sems = n_small + N_BIG
    out = pl.pallas_call(
        body, in_specs=[vmem] * (len(names) + 1) + [ANY] * N_BIG, out_specs=[vmem, vmem] + [ANY] * N_BIG,
        out_shape=([jax.ShapeDtypeStruct((VEC_ROWS, D_MODEL), F32), jax.ShapeDtypeStruct(gates.shape, F32)]
                   + [jax.ShapeDtypeStruct(BIG[n][:2], F32) for n in BIG_NAMES]),
        input_output_aliases={len(names) + 1 + a: 2 + a for a in range(N_BIG)},
        scratch_shapes=[pltpu.VMEM((VEC_ROWS, D_MODEL), F32), pltpu.VMEM((VEC_ROWS, D_MODEL), F32),
                        pltpu.VMEM(gates.shape, F32), pltpu.VMEM((VEC_ROWS, D_MODEL), F32),
                        pltpu.VMEM(gates.shape, F32), pltpu.VMEM((N_CHIPS, hv, D_MODEL), F32),
                        pltpu.VMEM((N_CHIPS, hg) + gates.shape[1:], BF16),
                        pltpu.SemaphoreType.DMA((n_sems,)), pltpu.SemaphoreType.DMA((n_sems,))],
        name="all_reduce_small",
    )(*[pieces[n] for n in names], gates, *[totals[n] for n in BIG_NAMES])
    return out[0], out[1], dict(zip(BIG_NAMES, out[2:]))


def _adamw_math(w, g, m, v):
    m = ADAM_B1 * m + (1.0 - ADAM_B1) * g
    v = ADAM_B2 * v + (1.0 - ADAM_B2) * (g * g)
    m_hat = m / (1.0 - ADAM_B1 ** ADAM_STEP)
    v_hat = v / (1.0 - ADAM_B2 ** ADAM_STEP)
    delta = -ADAM_LR * (m_hat / (jnp.sqrt(v_hat) + ADAM_EPS) + ADAM_WD * w)
    return delta, m, v


def _adamw_big(w, g, m, v):
    steps = 8
    blks = []
    for name in BIG_NAMES:
        rows, cols, _ = BIG[name]
        blks.append(pl.BlockSpec((rows // steps, cols), lambda i: (i, 0)))

    def body(*refs):
        ins, outs = refs[:4 * N_BIG], refs[4 * N_BIG:]
        for a in range(N_BIG):
            w_ref, g_ref, m_ref, v_ref = (ins[k * N_BIG + a] for k in range(4))
            g = g_ref[...]
            d, nm, nv = _adamw_math(w_ref[...], g, m_ref[...], v_ref[...])
            outs[a][...] = g
            outs[N_BIG + a][...] = d
            outs[2 * N_BIG + a][...] = nm
            outs[3 * N_BIG + a][...] = nv

    shapes = [jax.ShapeDtypeStruct(BIG[name][:2], F32) for name in BIG_NAMES]
    out = pl.pallas_call(
        body, grid=(steps,), in_specs=blks * 4, out_specs=blks * 4, out_shape=shapes * 4,
        name="adamw_big", compiler_params=_params("parallel"),
    )(*[t[name] for t in (w, g, m, v) for name in BIG_NAMES])
    return {name: tuple(out[k * N_BIG + a] for k in range(4)) for a, name in enumerate(BIG_NAMES)}


SMALL = {"meta_tokens": (N_META, D_MODEL // N_CHIPS), "mix_norm_g": (1, D_MODEL), "conv_w": (CONV_W, D_RG // N_CHIPS),
         "conv_b": (1, D_RG), "w_rgate": (D_RG, RG_HEAD_DIM), "b_rgate": (1, D_RG), "w_igate": (D_RG, RG_HEAD_DIM),
         "b_igate": (1, D_RG), "lru_lambda": (1, D_RG), "rg_norm_g": (1, D_RG), "hg_lower_bound": (2, D_HG),
         "hg_norm_g": (1, HG_HEAD_DIM), "ffn_norm_g": (1, D_MODEL), "final_norm_g": (1, D_MODEL)}
SMALL_NAMES = tuple(SMALL)
SHARDED_SMALL = ("meta_tokens", "conv_w")


def _adamw_small(vec, gates, w, m, v):
    n = len(SMALL_NAMES)

    def body(*refs):
        vec_ref, gates_ref = refs[:2]
        w_refs, m_refs, v_refs = refs[2:2 + n], refs[2 + n:2 + 2 * n], refs[2 + 2 * n:2 + 3 * n]
        outs = refs[2 + 3 * n:]
        loss_ref = outs[0]
        x, y, _ = _place()
        chip = 2 * x + y
        loss_ref[...] = vec_ref[VEC_ROW["loss"]:VEC_ROW["loss"] + 1, 0:1]

        def update(k, g):
            g_ref, d_ref, nm_ref, nv_ref = outs[1 + 4 * k:5 + 4 * k]
            g_ref[...] = g
            d_ref[...], nm_ref[...], nv_ref[...] = _adamw_math(w_refs[k][...], g, m_refs[k][...], v_refs[k][...])

        for k, name in enumerate(SMALL_NAMES):
            nr, w_ = SMALL[name]
            if name == "w_rgate":
                update(k, gates_ref[0:D_RG, :])
            elif name == "w_igate":
                update(k, gates_ref[D_RG:2 * D_RG, :])
            elif name in SHARDED_SMALL:
                r0 = VEC_ROW[name]
                for q in range(N_CHIPS):
                    @pl.when(chip == q)
                    def _(k=k, r0=r0, nr=nr, w_=w_, q=q):
                        update(k, vec_ref[r0:r0 + nr, q * w_:(q + 1) * w_])
            else:
                r0 = VEC_ROW[name]
                update(k, vec_ref[r0:r0 + nr, 0:w_])

    vmem = pl.BlockSpec(memory_space=pltpu.VMEM)
    out_shape = [jax.ShapeDtypeStruct((1, 1), F32)]
    for name in SMALL_NAMES:
        out_shape += [jax.ShapeDtypeStruct(SMALL[name], F32)] * 4
    outs = pl.pallas_call(
        body, in_specs=[vmem] * (2 + 3 * n), out_specs=[vmem] * len(out_shape), out_shape=out_shape,
        name="adamw_small",
    )(vec, gates, *[w[k] for k in SMALL_NAMES], *[m[k] for k in SMALL_NAMES], *[v[k] for k in SMALL_NAMES])
    loss = outs[0]
    res = {name: tuple(outs[1 + 4 * k:5 + 4 * k]) for k, name in enumerate(SMALL_NAMES)}
    return loss, res


WEIGHT_NAMES = ("meta_tokens", "mix_norm_g", "w_in", "conv_w", "conv_b", "w_rgate", "b_rgate", "w_igate", "b_igate",
                "lru_lambda", "rg_norm_g", "hg_lower_bound", "hg_norm_g", "w_out", "ffn_norm_g", "w_gate_up", "w_down",
                "final_norm_g")


def _to_2d(name, a):
    if name in BIG:
        return a.reshape(BIG[name][:2])
    return a.reshape(SMALL[name])


def kernel(x, meta_tokens, mix_norm_g, w_in, conv_w, conv_b, w_rgate, b_rgate, w_igate, b_igate, lru_lambda, rg_norm_g, hg_lower_bound, hg_norm_g, w_out, ffn_norm_g, w_gate_up, w_down, final_norm_g, loss_target, m_meta_tokens, m_mix_norm_g, m_w_in, m_conv_w, m_conv_b, m_w_rgate, m_b_rgate, m_w_igate, m_b_igate, m_lru_lambda, m_rg_norm_g, m_hg_lower_bound, m_hg_norm_g, m_w_out, m_ffn_norm_g, m_w_gate_up, m_w_down, m_final_norm_g, v_meta_tokens, v_mix_norm_g, v_w_in, v_conv_w, v_conv_b, v_w_rgate, v_b_rgate, v_w_igate, v_b_igate, v_lru_lambda, v_rg_norm_g, v_hg_lower_bound, v_hg_norm_g, v_w_out, v_ffn_norm_g, v_w_gate_up, v_w_down, v_final_norm_g):
    w_raw = dict(zip(WEIGHT_NAMES, (meta_tokens, mix_norm_g, w_in, conv_w, conv_b, w_rgate, b_rgate, w_igate, b_igate,
                                    lru_lambda, rg_norm_g, hg_lower_bound, hg_norm_g, w_out, ffn_norm_g, w_gate_up,
                                    w_down, final_norm_g)))
    m_raw = dict(zip(WEIGHT_NAMES, (m_meta_tokens, m_mix_norm_g, m_w_in, m_conv_w, m_conv_b, m_w_rgate, m_b_rgate,
                                    m_w_igate, m_b_igate, m_lru_lambda, m_rg_norm_g, m_hg_lower_bound, m_hg_norm_g,
                                    m_w_out, m_ffn_norm_g, m_w_gate_up, m_w_down, m_final_norm_g)))
    v_raw = dict(zip(WEIGHT_NAMES, (v_meta_tokens, v_mix_norm_g, v_w_in, v_conv_w, v_conv_b, v_w_rgate, v_b_rgate,
                                    v_w_igate, v_b_igate, v_lru_lambda, v_rg_norm_g, v_hg_lower_bound, v_hg_norm_g,
                                    v_w_out, v_ffn_norm_g, v_w_gate_up, v_w_down, v_final_norm_g)))
    w = {k: _to_2d(k, a) for k, a in w_raw.items()}
    m = {k: _to_2d(k, a) for k, a in m_raw.items()}
    v = {k: _to_2d(k, a) for k, a in v_raw.items()}

    x_i, y_i, c_i = _place()
    core = jnp.reshape(c_i, (1,)).astype(jnp.int32)
    chip = jnp.reshape(2 * x_i + y_i, (1,)).astype(jnp.int32)
    chip_core = jnp.concatenate([chip, core])

    first_names, rest_names = ("w_in",), ("w_out", "w_gate_up", "w_down")
    placed, _ = _place_shards(w, [], chip, first_names, "place_first")
    first, _ = _gather_weights(placed, [], first_names, "gather_first", 1)
    placed, (meta_full, cw_full) = _place_shards(w, [w["meta_tokens"], w["conv_w"]], chip, rest_names, "place_shards")
    rest, _ = _gather_weights(placed, [], rest_names, "gather_rest", 2)
    full = {**first, **rest}

    seq = x.shape[1]
    small ={k: w[k] for k in SMALL_NAMES if k not in SHARDED_SMALL}
    small["conv_w"] = cw_full

    def reduce_to_chips(grads, names, tag, collective_ids):
        got = _exchange_halves({n: grads[n][1] for n in names}, names, "exchange_halves_" + tag, collective_ids[0])

        def chip_sums():
            return _chip_sum({n: grads[n][0] for n in names}, got, names, core, "chip_sum_" + tag)

        def send(sums):
            arrived = _send_chip_sums({n: sums[n][1] for n in names}, names, "send_chip_sums_" + tag,
                                      collective_ids[1])
            return {n: (sums[n][0], a) for n, a in zip(names, arrived)}

        return chip_sums, send

    ffn_names, mixer_names = ("w_gate_up", "w_down", "w_out"), ("w_in",)
    loss, grad_x, grads, parts, parts_mixer = _local_step(
        x.reshape(seq, D_MODEL), meta_full, loss_target.reshape(seq, D_MODEL),
        w["w_in"], full["w_in"], full["w_out"], full["w_gate_up"], full["w_down"], small, chip,
        on_ffn_grads=lambda g: reduce_to_chips(g, ffn_names, "ffn", (3, 4)),
        on_mixer_grads=lambda g: reduce_to_chips(g, mixer_names, "mixer", (None, 5)))
    parts.update(parts_mixer)
    totals = _total(parts, chip_core)
    pieces = {k: grads[k] for k in VEC_ROW if k != "loss"}
    pieces["loss"] = loss
    vec, gates, g_big = _all_reduce_small(pieces, grads["w_gates"], totals)
    loss_sum, res = _adamw_small(vec, gates, w, m, v)
    res.update(_adamw_big(w, g_big, m, v))

    out = [loss_sum.reshape(()), grad_x.reshape(1, seq, D_MODEL)]
    for j in range(4):
        out += [res[n][j].reshape(w_raw[n].shape) for n in WEIGHT_NAMES]
    return tuple(out)
```

```python
import math

import jax
import jax.numpy as jnp
from jax import lax
from jax.experimental import pallas as pl
from jax.experimental.pallas import tpu as pltpu
from jax.experimental.pallas import tpu_sc as plsc

F32 = jnp.float32
BF16 = jnp.bfloat16
MESH = pl.DeviceIdType.MESH

D_MODEL = 1024
D_RG = 512
RG_HEAD_DIM = 64
D_HG = 512
HG_HEAD_DIM = 128
HG_HEADS = 4
CHUNK = 64
SUB = 16
N_SUB = CHUNK // SUB
N_META = 16
PAD = CHUNK - N_META
D_IN = 3072
D_FF = 2816
CONV_W = 4
LRU_C = 8.0
EPS = 1e-6
EXP_CLAMP = 80.0
GELU_C = math.sqrt(2.0 / math.pi)
GELU_A = 0.044715
N_CHIPS = 4

ADAM_LR = 0.001
ADAM_B1 = 0.9
ADAM_B2 = 0.999
ADAM_EPS = 1e-08
ADAM_WD = 0.01
ADAM_STEP = 10

VMEM_LIMIT = 56 * 1024 * 1024


def _params(*sem):
    return pltpu.CompilerParams(dimension_semantics=sem, vmem_limit_bytes=VMEM_LIMIT)


def _row_tile(rows, target):
    best = None
    for t in range(16, min(rows, target) + 1, 16):
        if rows % t == 0:
            best = t
    assert best is not None, rows
    return best


def _sigmoid(x):
    return 0.5 * jnp.tanh(0.5 * x) + 0.5


def _dot(a, b):
    return jnp.dot(a, b, preferred_element_type=F32)


def _dot_nt(a, b):
    return lax.dot_general(a, b, (((1,), (1,)), ((), ())), preferred_element_type=F32)


def _dot_tn(a, b):
    return lax.dot_general(a, b, (((0,), (0,)), ((), ())), preferred_element_type=F32)


def _rms(x):
    return lax.rsqrt(jnp.mean(x * x, axis=-1, keepdims=True) + EPS)


def _rms_bwd(dn, n, r):
    return r * (dn - n * jnp.mean(dn * n, axis=-1, keepdims=True))


def _gelu_parts(x):
    t = jnp.tanh(GELU_C * (x + GELU_A * x * x * x))
    g = 0.5 * x * (1.0 + t)
    dg = 0.5 * (1.0 + t) + 0.5 * x * (1.0 - t * t) * GELU_C * (1.0 + 3.0 * GELU_A * x * x)
    return g, dg


def _softplus_neg(lam):
    e = jnp.exp(-jnp.abs(lam))
    w = 1.0 + e
    log1p = jnp.where(w == 1.0, e, jnp.log(w) * e / (w - 1.0))
    return jnp.maximum(-lam, 0.0) + log1p


def _head_mask():
    r = lax.broadcasted_iota(jnp.int32, (D_RG, D_RG), 0) // RG_HEAD_DIM
    c = lax.broadcasted_iota(jnp.int32, (D_RG, D_RG), 1) // RG_HEAD_DIM
    return r == c


def _head_fold():
    r = lax.broadcasted_iota(jnp.int32, (D_RG, RG_HEAD_DIM), 0) % RG_HEAD_DIM
    c = lax.broadcasted_iota(jnp.int32, (D_RG, RG_HEAD_DIM), 1)
    return (r == c).astype(F32)


def _gate_weights(w_r, w_i):
    def body(wr_ref, wi_ref, o_ref):
        fold = _head_fold()
        mask = _head_mask()
        for k, ref in enumerate((wr_ref, wi_ref)):
            full = _dot_nt(ref[...].astype(BF16), fold.astype(BF16))
            o_ref[:, k * D_RG:(k + 1) * D_RG] = jnp.where(mask, full, 0.0).astype(BF16)

    return pl.pallas_call(
        body, out_shape=jax.ShapeDtypeStruct((D_RG, 2 * D_RG), BF16), name="gate_weights",
    )(w_r, w_i)


HEAD = PAD + N_META


def _window_copies(seq_hbm, buf, sems, tm):
    def first(to_vmem):
        seq, vm = seq_hbm.at[pl.ds(0, tm - HEAD)], buf.at[0, pl.ds(HEAD, tm - HEAD)]
        return pltpu.make_async_copy(seq, vm, sems.at[0]) if to_vmem else pltpu.make_async_copy(vm, seq, sems.at[0])

    def later(j, slot, to_vmem):
        seq, vm = seq_hbm.at[pl.ds(pl.multiple_of(j * tm - HEAD, 8), tm)], buf.at[slot]
        if to_vmem:
            return pltpu.make_async_copy(seq, vm, sems.at[slot])
        return pltpu.make_async_copy(vm, seq, sems.at[slot])

    return first, later


def _fetch_window(seq_hbm, buf, sems, i, n_steps, tm):
    first, later = _window_copies(seq_hbm, buf, sems, tm)
    slot = i % 2

    @pl.when(i == 0)
    def _():
        first(True).start()

    if n_steps > 1:
        @pl.when(i + 1 < n_steps)
        def _():
            later(i + 1, 1 - slot, True).start()

    @pl.when(i == 0)
    def _():
        first(True).wait()

    if n_steps > 1:
        @pl.when(i > 0)
        def _():
            later(i, slot, True).wait()

    return slot


def _in_proj_local(x, meta, g1, w_own, chip):
    T = x.shape[0] + HEAD
    tm = _row_tile(T, 832)
    n_steps = T // tm
    cols = BIG["w_in"][1]

    def body(s_ref, x_hbm, meta_ref, g_ref, w_ref, p_ref, u_ref, h_ref, buf, sems, wb):
        i = pl.program_id(0)
        slot = _fetch_window(x_hbm, buf, sems, i, n_steps, tm)

        @pl.when(i == 0)
        def _():
            buf[0, 0:PAD, :] = jnp.zeros((PAD, D_MODEL), F32)
            buf[0, PAD:HEAD, :] = meta_ref[...]
            wb[...] = w_ref[...].astype(BF16)

        h = buf[slot]
        h_ref[...] = h
        u = (h * _rms(h) * g_ref[...]).astype(BF16)
        u_ref[...] = u
        p_ref[...] = _dot(u, wb[...])

    return pl.pallas_call(
        body,
        grid_spec=pltpu.PrefetchScalarGridSpec(
            num_scalar_prefetch=1, grid=(n_steps,),
            in_specs=[pl.BlockSpec(memory_space=pl.ANY),
                      pl.BlockSpec((N_META, D_MODEL), lambda i, s: (0, 0)),
                      pl.BlockSpec((1, D_MODEL), lambda i, s: (0, 0)),
                      pl.BlockSpec((D_MODEL, cols), lambda i, s: (0, 0))],
            out_specs=[pl.BlockSpec((tm, cols), lambda i, s: (i, s[0])),
                       pl.BlockSpec((tm, D_MODEL), lambda i, s: (i, 0)),
                       pl.BlockSpec((tm, D_MODEL), lambda i, s: (i, 0))],
            scratch_shapes=[pltpu.VMEM((2, tm, D_MODEL), F32), pltpu.SemaphoreType.DMA((2,)),
                            pltpu.VMEM((D_MODEL, cols), BF16)]),
        out_shape=[jax.ShapeDtypeStruct((T, D_IN), F32), jax.ShapeDtypeStruct((T, D_MODEL), BF16),
                   jax.ShapeDtypeStruct((T, D_MODEL), F32)],
        name="in_proj_local", compiler_params=_params("arbitrary"),
    )(chip, x, meta, g1, w_own)


def _in_proj_rest(u, w_in, p, chip):
    T = u.shape[0]
    tm = _row_tile(T, 2080)
    cols = BIG["w_in"][1]
    block = lambda j, s: (s[0] + 1 + j) % N_CHIPS

    def body(s_ref, u_ref, w_ref, p_in_ref, p_ref):
        p_ref[...] = _dot(u_ref[...], w_ref[...])

    return pl.pallas_call(
        body,
        grid_spec=pltpu.PrefetchScalarGridSpec(
            num_scalar_prefetch=1, grid=(N_CHIPS - 1, T // tm),
            in_specs=[pl.BlockSpec((tm, D_MODEL), lambda j, i, s: (i, 0)),
                      pl.BlockSpec((D_MODEL, cols), lambda j, i, s: (0, block(j, s))), ANY],
            out_specs=pl.BlockSpec((tm, cols), lambda j, i, s: (i, block(j, s)))),
        out_shape=jax.ShapeDtypeStruct((T, D_IN), F32),
        input_output_aliases={3: 0},
        name="in_proj_rest", compiler_params=_params("arbitrary", "arbitrary"),
    )(chip, u, w_in, p)


def _scan_block_fwd(A, B, rowi):
    for d in (1, 2, 4):
        a_sh = pltpu.roll(A, d, axis=0)
        b_sh = pltpu.roll(B, d, axis=0)
        m = rowi >= d
        B = jnp.where(m, A * b_sh + B, B)
        A = jnp.where(m, A * a_sh, A)
    return A, B


def _scan_block_bwd(A, B, rowi):
    for d in (1, 2, 4):
        a_sh = pltpu.roll(A, 8 - d, axis=0)
        b_sh = pltpu.roll(B, 8 - d, axis=0)
        m = rowi < 8 - d
        B = jnp.where(m, A * b_sh + B, B)
        A = jnp.where(m, A * a_sh, A)
    return A, B


def _rg_gates(xc, w_ref, bg_ref, lam):
    pre = _dot(xc.astype(BF16), w_ref[...]) + bg_ref[...]
    r = _sigmoid(pre[:, :D_RG])
    ig = _sigmoid(pre[:, D_RG:])
    sp = _softplus_neg(lam)
    la = -LRU_C * sp * r
    a = jnp.exp(la)
    th = jnp.tanh(la)
    u = 1.0 - th
    rc = pl.reciprocal(u, approx=True)
    rc = rc * (2.0 - u * rc)
    rc = rc * (2.0 - u * rc)
    m2 = -2.0 * th * rc
    inv_m = lax.rsqrt(jnp.maximum(m2, 1e-30))
    return r, ig, sp, a, m2 * inv_m, inv_m


def _conv(ext, cw_ref, cb_ref, tm):
    xc = cb_ref[...] + cw_ref[0:1, :] * ext[8 - 3:8 - 3 + tm, :]
    for j in range(1, CONV_W):
        xc = xc + cw_ref[j:j + 1, :] * ext[8 - 3 + j:8 - 3 + j + tm, :]
    return xc


def _scan_unroll(blocks):
    return 4 if blocks % 4 == 0 else 2 if blocks % 2 == 0 else 1


def _rg_fwd(p, cw, cb, wg, bg, lam, rg_g):
    T = p.shape[0]
    tm = _row_tile(T, 832)
    unroll = _scan_unroll(tm // 8)

    def body(xg_ref, cw_ref, cb_ref, w_ref, bg_ref, lam_ref, g_ref, y_ref, h_ref, xc_ref, ext, a_s, b_s, carry):
        i = pl.program_id(0)

        @pl.when(i == 0)
        def _():
            ext[0:8, :] = jnp.zeros((8, D_RG), F32)
            carry[...] = jnp.zeros((1, D_RG), F32)

        ext[8:8 + tm, :] = xg_ref[:, :D_RG]
        xc = _conv(ext, cw_ref, cb_ref, tm)
        xc_ref[...] = xc
        r, ig, sp, a, m, _ = _rg_gates(xc, w_ref, bg_ref, lam_ref[...])
        row = i * tm + lax.broadcasted_iota(jnp.int32, (tm, 1), 0)
        a_s[...] = a
        b_s[...] = jnp.where(row >= PAD, m * ig * xc, 0.0)
        rowi = lax.broadcasted_iota(jnp.int32, (8, D_RG), 0)

        def blk(j, c):
            for u in range(unroll):
                o = pl.multiple_of((j * unroll + u) * 8, 8)
                A, B = _scan_block_fwd(a_s[pl.ds(o, 8), :], b_s[pl.ds(o, 8), :], rowi)
                h = B + A * c
                h_ref[pl.ds(o, 8), :] = h
                c = h[7:8, :]
            return c

        carry[...] = lax.fori_loop(0, tm // (8 * unroll), blk, carry[...])
        ext[0:8, :] = ext[tm:tm + 8, :]
        g, _ = _gelu_parts(xg_ref[:, D_RG:])
        yy = g * h_ref[...]
        y_ref[...] = (yy * _rms(yy) * g_ref[...]).astype(BF16)

    vec = lambda n: pl.BlockSpec((1, n), lambda i: (0, 0))
    return pl.pallas_call(
        body, grid=(T // tm,),
        in_specs=[pl.BlockSpec((tm, 2 * D_RG), lambda i: (i, 0)),
                  pl.BlockSpec((CONV_W, D_RG), lambda i: (0, 0)), vec(D_RG),
                  pl.BlockSpec((D_RG, 2 * D_RG), lambda i: (0, 0)), vec(2 * D_RG), vec(D_RG), vec(D_RG)],
        out_specs=[pl.BlockSpec((tm, D_RG), lambda i: (i, 0))] * 3,
        out_shape=[jax.ShapeDtypeStruct((T, D_RG), BF16), jax.ShapeDtypeStruct((T, D_RG), F32),
                   jax.ShapeDtypeStruct((T, D_RG), F32)],
        scratch_shapes=[pltpu.VMEM((tm + 8, D_RG), F32), pltpu.VMEM((tm, D_RG), F32),
                        pltpu.VMEM((tm, D_RG), F32), pltpu.VMEM((1, D_RG), F32)],
        name="rg_fwd", compiler_params=_params("arbitrary"),
    )(p, cw, cb, wg, bg, lam, rg_g)


def _running_sum(x, down):
    r = lax.broadcasted_iota(jnp.int32, (CHUNK, CHUNK), 0)
    c = lax.broadcasted_iota(jnp.int32, (CHUNK, CHUNK), 1)
    tri = ((c <= r) if down else (c >= r)).astype(BF16)
    hi = x.astype(BF16)
    rest = x - hi.astype(F32)
    mid = rest.astype(BF16)
    lo = (rest - mid.astype(F32)).astype(BF16)
    return (_dot(tri, hi) + _dot(tri, mid)) + _dot(tri, lo)


def _hg_gates(hq, hf, lbraw_ref, valid):
    lb = _sigmoid(lbraw_ref[0:1, :] - lbraw_ref[1:2, :])
    sq = _sigmoid(hq)
    q = hq * sq
    sf = _sigmoid(hf)
    f = lb + (1.0 - lb) * sf
    lf = jnp.where(valid, jnp.log(f), 0.0)
    b = _running_sum(lf, True)
    return lb, sq, q, sf, f, b


def _hg_head(qh, kh, bh):
    b_last = bh[CHUNK - 1:CHUNK, :]
    refs = [bh[SUB * s:SUB * s + 1, :] for s in range(N_SUB)]
    r_sel = jnp.concatenate([jnp.broadcast_to(refs[s], (SUB, HG_HEAD_DIM)) for s in range(N_SUB)], axis=0)
    eb = jnp.exp(bh)
    eq = jnp.exp(bh - r_sel)
    ekh = jnp.exp(b_last - bh)
    ek = [jnp.exp(jnp.minimum(refs[s] - bh[:SUB * (s + 1), :], EXP_CLAMP)) for s in range(N_SUB)]
    qe = qh * eq

    def own_rows(s):
        parts = [jnp.zeros((SUB * s, HG_HEAD_DIM), F32)] if s else []
        parts.append(qe[SUB * s:SUB * (s + 1), :])
        if s < N_SUB - 1:
            parts.append(jnp.zeros((CHUNK - SUB * (s + 1), HG_HEAD_DIM), F32))
        return jnp.concatenate(parts, axis=0)

    q_hat = jnp.concatenate([own_rows(s) for s in range(N_SUB)], axis=1)

    def met_rows(s):
        n = SUB * (s + 1)
        ke = kh[:n, :] * ek[s]
        return ke if n == CHUNK else jnp.concatenate([ke, jnp.zeros((CHUNK - n, HG_HEAD_DIM), F32)], axis=0)

    k_til = jnp.concatenate([met_rows(s) for s in range(N_SUB)], axis=1)
    return b_last, eb, eq, ekh, ek, q_hat, k_til


def _causal():
    r = lax.broadcasted_iota(jnp.int32, (CHUNK, CHUNK), 0)
    c = lax.broadcasted_iota(jnp.int32, (CHUNK, CHUNK), 1)
    return r >= c


def _chunks_per_step(n_chunks):
    for c in (5, 4, 3, 2):
        if n_chunks % c == 0:
            return c
    return 1


def _hg_fwd(p, lbraw, hg_g):
    T = p.shape[0]
    n_chunks = T // CHUNK
    cps = _chunks_per_step(n_chunks)
    rows = cps * CHUNK

    def body(hq_ref, hf_ref, hi_ref, hg_ref, lb_ref, g_ref, y_ref, o_ref, st_all_ref, st):
        i = pl.program_id(0)

        @pl.when(i == 0)
        def _():
            st[...] = jnp.zeros_like(st)

        def chunk(j, carry):
            rs = pl.ds(pl.multiple_of(j * CHUNK, CHUNK), CHUNK)
            chunk_body(i * cps + j, hq_ref.at[rs, :], hf_ref.at[rs, :], hi_ref.at[rs, :], hg_ref.at[rs, :], lb_ref,
                       g_ref, y_ref.at[rs, :], o_ref.at[rs, :], st_all_ref.at[pl.ds(j, 1)], st)
            return carry

        lax.fori_loop(0, cps, chunk, 0, unroll=True)

    def chunk_body(n, hq_ref, hf_ref, hi_ref, hg_ref, lb_ref, g_ref, y_ref, o_ref, st_all_ref, st):
        valid = (n * CHUNK + lax.broadcasted_iota(jnp.int32, (CHUNK, 1), 0)) >= PAD
        hq, hf, v, hg = hq_ref[...], hf_ref[...], hi_ref[...], hg_ref[...]
        lb, sq, q, sf, f, b = _hg_gates(hq, hf, lb_ref, valid)
        k = 1.0 - f
        st_all_ref[0] = st[...]
        causal = _causal()
        v_t = v.T.astype(BF16)
        heads = [slice(h * HG_HEAD_DIM, (h + 1) * HG_HEAD_DIM) for h in range(HG_HEADS)]
        fac = []
        for sl in heads:
            qh, kh, bh = q[:, sl], k[:, sl], b[:, sl]
            b_last, eb, _, ekh, _, q_hat, k_til = _hg_head(qh, kh, bh)
            fac.append((jnp.exp(b_last), (qh * eb).astype(BF16), q_hat.astype(BF16), k_til.astype(BF16),
                        (kh * ekh).astype(BF16), v[:, sl].astype(BF16)))
        raw = []
        for sl, (_, q_til, q_hat, k_til, k_hat, _) in zip(heads, fac):
            st_h = st[sl, :]
            raw.append((_dot_nt(q_til, st_h.astype(BF16)), _dot_nt(q_hat, k_til), _dot(v_t[sl, :], k_hat), st_h))
        for sl, (e_last, _, _, _, _, vb), (inter, att, upd, st_h) in zip(heads, fac, raw):
            o = inter + _dot(jnp.where(causal, att, 0.0).astype(BF16), vb)
            st[sl, :] = st_h * e_last + upd
            o_ref[:, sl] = o
            hgh = hg[:, sl]
            y_ref[:, sl] = (o * _rms(o) * g_ref[...] * (hgh * _sigmoid(hgh))).astype(BF16)

    col = lambda j: pl.BlockSpec((rows, D_HG), lambda n: (n, j))
    return pl.pallas_call(
        body, grid=(n_chunks // cps,),
        in_specs=[col(2), col(3), col(4), col(5),
                  pl.BlockSpec((2, D_HG), lambda n: (0, 0)), pl.BlockSpec((1, HG_HEAD_DIM), lambda n: (0, 0))],
        out_specs=[pl.BlockSpec((rows, D_HG), lambda n: (n, 0)), pl.BlockSpec((rows, D_HG), lambda n: (n, 0)),
                   pl.BlockSpec((cps, D_HG, HG_HEAD_DIM), lambda n: (n, 0, 0))],
        out_shape=[jax.ShapeDtypeStruct((T, D_HG), BF16), jax.ShapeDtypeStruct((T, D_HG), F32),
                   jax.ShapeDtypeStruct((n_chunks, D_HG, HG_HEAD_DIM), F32)],
        scratch_shapes=[pltpu.VMEM((D_HG, HG_HEAD_DIM), F32)],
        name="hg_fwd", compiler_params=_params("arbitrary"),
    )(p, p, p, p, lbraw, hg_g)


def _ffn_fwd(h0, y_rg, y_hg, w_out, g2, w_gu, w_down, gf, target):
    T = h0.shape[0]
    tm = _row_tile(T, 320)
    n_steps = T // tm

    def body(h_ref, yr_ref, yh_ref, wo_ref, g2_ref, wgu_ref, wd_ref, gf_ref, t_hbm,
             h1_ref, v_ref, y_ref, gu_ref, act_ref, dh2_ref, dh2b_ref, loss_ref, gg_ref, tbuf, sems):
        i = pl.program_id(0)
        slot = _fetch_window(t_hbm, tbuf, sems, i, n_steps, tm)

        @pl.when(i == 0)
        def _():
            loss_ref[...] = jnp.zeros_like(loss_ref)
            gg_ref[...] = jnp.zeros_like(gg_ref)
            tbuf[0, 0:HEAD, :] = jnp.zeros((HEAD, D_MODEL), F32)

        y_ref[:, :D_RG] = yr_ref[...]
        y_ref[:, D_RG:] = yh_ref[...]
        h1 = h_ref[...] + _dot(y_ref[...], wo_ref[...])
        h1_ref[...] = h1
        v = (h1 * _rms(h1) * g2_ref[...]).astype(BF16)
        v_ref[...] = v

        gu = _dot(v, wgu_ref[...])
        gu_ref[...] = gu.astype(BF16)
        g = gu[:, :D_FF]
        act = (g * _sigmoid(g) * gu[:, D_FF:]).astype(BF16)
        act_ref[...] = act

        h2 = h1 + _dot(act, wd_ref[...])
        r = _rms(h2)
        n = h2 * r
        gf_ = gf_ref[...]
        row = i * tm + lax.broadcasted_iota(jnp.int32, (tm, 1), 0)
        err = jnp.where(row >= HEAD, n * gf_ - tbuf[slot], 0.0)
        loss_ref[...] += 0.5 * jnp.sum(jnp.mean(err * err, axis=-1, keepdims=True), axis=0, keepdims=True)
        dy = err * (1.0 / D_MODEL)
        gg_ref[...] += jnp.sum(dy * n, axis=0, keepdims=True)
        dh2 = _rms_bwd(dy * gf_, n, r)
        dh2_ref[...] = dh2
        dh2b_ref[...] = dh2.astype(BF16)

    row_spec = lambda n: pl.BlockSpec((tm, n), lambda i: (i, 0))
    vec = pl.BlockSpec((1, D_MODEL), lambda i: (0, 0))
    return pl.pallas_call(
        body, grid=(n_steps,),
        in_specs=[row_spec(D_MODEL), row_spec(D_RG), row_spec(D_HG), _resident((D_MODEL, D_MODEL)), vec,
                  _resident((D_MODEL, 2 * D_FF)), _resident((D_FF, D_MODEL)), vec,
                  pl.BlockSpec(memory_space=pl.ANY)],
        out_specs=[row_spec(D_MODEL), row_spec(D_MODEL), row_spec(D_MODEL), row_spec(2 * D_FF), row_spec(D_FF),
                   row_spec(D_MODEL), row_spec(D_MODEL), pl.BlockSpec((1, 1), lambda i: (0, 0)), vec],
        out_shape=[jax.ShapeDtypeStruct((T, D_MODEL), F32), jax.ShapeDtypeStruct((T, D_MODEL), BF16),
                   jax.ShapeDtypeStruct((T, D_MODEL), BF16), jax.ShapeDtypeStruct((T, 2 * D_FF), BF16),
                   jax.ShapeDtypeStruct((T, D_FF), BF16), jax.ShapeDtypeStruct((T, D_MODEL), F32),
                   jax.ShapeDtypeStruct((T, D_MODEL), BF16), jax.ShapeDtypeStruct((1, 1), F32),
                   jax.ShapeDtypeStruct((1, D_MODEL), F32)],
        scratch_shapes=[pltpu.VMEM((2, tm, D_MODEL), F32), pltpu.SemaphoreType.DMA((2,))],
        name="ffn_fwd", compiler_params=_params("arbitrary"),
    )(h0, y_rg, y_hg, w_out, g2, w_gu, w_down, gf, target)


def _resident(shape):
    return pl.BlockSpec(shape, lambda i: (0,) * len(shape), pipeline_mode=pl.Buffered(1))


def _ffn_bwd(dh2b, gu, w_down, w_gu, h1, g2, dh2, w_out):
    T = h1.shape[0]
    tm = _row_tile(T, 320)

    def body(d_ref, gu_ref, wd_ref, wgu_ref, h_ref, g_ref, d2_ref, wo_ref, dgu_ref, dh1_ref, dh1b_ref, dy_ref, gg_ref):
        i = pl.program_id(0)

        @pl.when(i == 0)
        def _():
            gg_ref[...] = jnp.zeros_like(gg_ref)

        dact = _dot_nt(d_ref[...], wd_ref[...]).astype(BF16)
        g = gu_ref[:, :D_FF]
        u = gu_ref[:, D_FF:]
        s = _sigmoid(g)
        dgu_ref[:, :D_FF] = dact * u * (s * (1.0 + g * (1.0 - s)))
        dgu_ref[:, D_FF:] = dact * (g * s)

        dv = _dot_nt(dgu_ref[...], wgu_ref[...])
        h1_ = h_ref[...]
        r = _rms(h1_)
        n = h1_ * r
        gg_ref[...] += jnp.sum(dv * n, axis=0, keepdims=True)
        dh1 = d2_ref[...] + _rms_bwd(dv * g_ref[...], n, r)
        dh1_ref[...] = dh1
        db = dh1.astype(BF16)
        dh1b_ref[...] = db
        dy_ref[...] = _dot_nt(db, wo_ref[...])

    row = lambda n: pl.BlockSpec((tm, n), lambda i: (i, 0))
    return pl.pallas_call(
        body, grid=(T // tm,),
        in_specs=[row(D_MODEL), row(2 * D_FF), _resident((D_FF, D_MODEL)), _resident((D_MODEL, 2 * D_FF)),
                  row(D_MODEL), pl.BlockSpec((1, D_MODEL), lambda i: (0, 0)), row(D_MODEL),
                  _resident((D_MODEL, D_MODEL))],
        out_specs=[row(2 * D_FF), row(D_MODEL), row(D_MODEL), row(D_MODEL),
                   pl.BlockSpec((1, D_MODEL), lambda i: (0, 0))],
        out_shape=[jax.ShapeDtypeStruct((T, 2 * D_FF), BF16), jax.ShapeDtypeStruct((T, D_MODEL), F32),
                   jax.ShapeDtypeStruct((T, D_MODEL), BF16), jax.ShapeDtypeStruct((T, D_MODEL), F32),
                   jax.ShapeDtypeStruct((1, D_MODEL), F32)],
        name="ffn_bwd", compiler_params=_params("arbitrary"),
    )(dh2b, gu, w_down, w_gu, h1, g2, dh2, w_out)


def _rg_bwd(p, xc_all, hs, dy, dp, cw, cb, wg, bg, lam, rg_g):
    T = p.shape[0]
    tm = _row_tile(T, 832)
    nt = T // tm
    hb = tm // 8
    unroll = _scan_unroll(hb)

    def body(xg_ref, xc_ref, h_ref, hh_ref, dy_ref, dp_in_ref, cw_ref, cb_ref, w_ref, bg_ref, lam_ref, g_ref,
             dp_ref, gcw_ref, gcb_ref, gw_ref, gbg_ref, glam_ref, gg_ref,
             dext, a_s, b_s, d_s, gacc, carry_d, carry_a):
        i = pl.program_id(0)
        t_idx = nt - 1 - i

        @pl.when(i == 0)
        def _():
            dext[tm:tm + 8, :] = jnp.zeros((8, D_RG), F32)
            carry_d[...] = jnp.zeros_like(carry_d)
            carry_a[...] = jnp.zeros_like(carry_a)
            gacc[...] = jnp.zeros_like(gacc)
            for ref in (gcw_ref, gcb_ref, gbg_ref, glam_ref, gg_ref, gw_ref):
                ref[...] = jnp.zeros_like(ref)

        first = t_idx == 0
        xc = xc_ref[...]
        lam_ = lam_ref[...]
        r, ig, sp, a, m, inv_m = _rg_gates(xc, w_ref, bg_ref, lam_)
        row = t_idx * tm + lax.broadcasted_iota(jnp.int32, (tm, 1), 0)
        valid = row >= PAD

        gr = xg_ref[:, D_RG:]
        g, dgelu = _gelu_parts(gr)
        h = h_ref[...]
        yy = g * h
        rr = _rms(yy)
        nn = yy * rr
        dy_ = dy_ref[...]
        gg_ref[...] += jnp.sum(dy_ * nn, axis=0, keepdims=True)
        dyy = _rms_bwd(dy_ * g_ref[...], nn, rr)
        dp_ref[:, D_RG:] = (dyy * h * dgelu).astype(BF16)

        a_s[...] = a
        b_s[...] = dyy * g
        rowi = lax.broadcasted_iota(jnp.int32, (8, D_RG), 0)

        def blk(jj, c):
            cd, ca = c
            for u in range(unroll):
                o = pl.multiple_of((hb - 1 - (jj * unroll + u)) * 8, 8)
                a_blk = a_s[pl.ds(o, 8), :]
                a_next = jnp.where(rowi == 7, ca, pltpu.roll(a_blk, 7, axis=0))
                A, B = _scan_block_bwd(a_next, b_s[pl.ds(o, 8), :], rowi)
                d = B + A * cd
                d_s[pl.ds(o, 8), :] = d
                cd, ca = d[0:1, :], a_blk[0:1, :]
            return cd, ca

        cd, ca = lax.fori_loop(0, hb // unroll, blk, (carry_d[...], carry_a[...]))
        carry_d[...] = cd
        carry_a[...] = ca
        delta = d_s[...]

        h_last_prev = jnp.where(first, 0.0, hh_ref[7:8, :])
        row0 = lax.broadcasted_iota(jnp.int32, (tm, 1), 0) == 0
        h_prev = jnp.where(row0, h_last_prev, pltpu.roll(h, 1, axis=0))
        dbx = jnp.where(valid, delta, 0.0)
        da = delta * h_prev
        di = dbx * m * xc
        dm = dbx * ig * xc
        dla = a * (da - dm * a * inv_m)
        dla = jnp.where(valid, dla, 0.0)
        glam_ref[...] += jnp.sum(dla * r, axis=0, keepdims=True) * (LRU_C / (1.0 + jnp.exp(lam_)))
        dr = (-LRU_C) * sp * dla
        dpre = jnp.concatenate([dr * r * (1.0 - r), di * ig * (1.0 - ig)], axis=1)
        gbg_ref[...] += jnp.sum(dpre, axis=0, keepdims=True)
        dpre_b = dpre.astype(BF16)
        gacc[...] += _dot_tn(xc.astype(BF16), dpre_b)
        dxc = dbx * m * ig + _dot_nt(dpre_b, w_ref[...])
        gcb_ref[...] += jnp.sum(dxc, axis=0, keepdims=True)
        dext[0:tm, :] = dxc
        xr = xg_ref[:, :D_RG]
        dxr = None
        for j in range(CONV_W):
            shifted = dext[3 - j:3 - j + tm, :]
            gcw_ref[j:j + 1, :] += jnp.sum(xr * shifted, axis=0, keepdims=True)
            tap = cw_ref[j:j + 1, :] * shifted
            dxr = tap if dxr is None else dxr + tap
        dp_ref[:, :D_RG] = dxr.astype(BF16)
        dext[tm:tm + 8, :] = dext[0:8, :]

        @pl.when(i == nt - 1)
        def _():
            fold = _head_fold()
            mask = _head_mask()
            fold_b = fold.astype(BF16)
            for k in range(2):
                blockdiag = jnp.where(mask, gacc[:, k * D_RG:(k + 1) * D_RG], 0.0)
                hi = blockdiag.astype(BF16)
                rest = blockdiag - hi.astype(F32)
                mid = rest.astype(BF16)
                lo = (rest - mid.astype(F32)).astype(BF16)
                gw_ref[k * D_RG:(k + 1) * D_RG, :] = (_dot(hi, fold_b) + _dot(mid, fold_b)) + _dot(lo, fold_b)

    vec = lambda n: pl.BlockSpec((1, n), lambda i: (0, 0))
    rev = lambda n: pl.BlockSpec((tm, n), lambda i: (nt - 1 - i, 0))
    halo = lambda n: pl.BlockSpec((8, n), lambda i: (jnp.maximum((nt - 1 - i) * hb - 1, 0), 0))
    return pl.pallas_call(
        body, grid=(nt,),
        in_specs=[rev(2 * D_RG), rev(D_RG), rev(D_RG), halo(D_RG), rev(D_RG), ANY,
                  pl.BlockSpec((CONV_W, D_RG), lambda i: (0, 0)), vec(D_RG),
                  pl.BlockSpec((D_RG, 2 * D_RG), lambda i: (0, 0)), vec(2 * D_RG), vec(D_RG), vec(D_RG)],
        out_specs=[rev(2 * D_RG), pl.BlockSpec((CONV_W, D_RG), lambda i: (0, 0)), vec(D_RG),
                   pl.BlockSpec((2 * D_RG, RG_HEAD_DIM), lambda i: (0, 0)), vec(2 * D_RG), vec(D_RG), vec(D_RG)],
        input_output_aliases={5: 0},
        out_shape=[jax.ShapeDtypeStruct((T, D_IN), BF16), jax.ShapeDtypeStruct((CONV_W, D_RG), F32),
                   jax.ShapeDtypeStruct((1, D_RG), F32), jax.ShapeDtypeStruct((2 * D_RG, RG_HEAD_DIM), F32),
                   jax.ShapeDtypeStruct((1, 2 * D_RG), F32), jax.ShapeDtypeStruct((1, D_RG), F32),
                   jax.ShapeDtypeStruct((1, D_RG), F32)],
        scratch_shapes=[pltpu.VMEM((tm + 8, D_RG), F32),
                        pltpu.VMEM((tm, D_RG), F32), pltpu.VMEM((tm, D_RG), F32), pltpu.VMEM((tm, D_RG), F32),
                        pltpu.VMEM((D_RG, 2 * D_RG), F32), pltpu.VMEM((1, D_RG), F32), pltpu.VMEM((1, D_RG), F32)],
        name="rg_bwd", compiler_params=_params("arbitrary"),
    )(p, xc_all, hs, hs, dy, dp, cw, cb, wg, bg, lam, rg_g)


def _hg_bwd(p, o_all, st_all, dy, lbraw, hg_g):
    T = p.shape[0]
    n_chunks = T // CHUNK
    cps = _chunks_per_step(n_chunks)
    rows = cps * CHUNK
    n_steps = n_chunks // cps

    def body(hq_ref, hf_ref, hi_ref, hg_ref, o_ref, st_ref, dy_ref, lb_ref, g_ref,
             dp_ref, glb_ref, gg_ref, dst):
        i = pl.program_id(0)

        @pl.when(i == 0)
        def _():
            dst[...] = jnp.zeros_like(dst)
            glb_ref[...] = jnp.zeros_like(glb_ref)
            gg_ref[...] = jnp.zeros_like(gg_ref)

        dp_ref[:, :2 * D_RG] = jnp.zeros((rows, 2 * D_RG), BF16)

        def chunk(jj, carry):
            j = cps - 1 - jj
            rs = pl.ds(pl.multiple_of(j * CHUNK, CHUNK), CHUNK)
            chunk_body((n_steps - 1 - i) * cps + j, hq_ref.at[rs, :], hf_ref.at[rs, :], hi_ref.at[rs, :],
                       hg_ref.at[rs, :], o_ref.at[rs, :], st_ref.at[pl.ds(j, 1)], dy_ref.at[rs, :], lb_ref, g_ref,
                       dp_ref.at[rs, pl.ds(2 * D_RG, 4 * D_HG)], glb_ref, gg_ref, dst)
            return carry

        lax.fori_loop(0, cps, chunk, 0, unroll=True)

    def chunk_body(n, hq_ref, hf_ref, hi_ref, hg_ref, o_ref, st_ref, dy_ref, lb_ref, g_ref,
                   dp_ref, glb_ref, gg_ref, dst):
        valid = (n * CHUNK + lax.broadcasted_iota(jnp.int32, (CHUNK, 1), 0)) >= PAD
        hq, hf, v, hg = hq_ref[...], hf_ref[...], hi_ref[...], hg_ref[...]
        lb, sq, q, sf, f, b = _hg_gates(hq, hf, lb_ref, valid)
        k = 1.0 - f
        causal = _causal()
        r_i = lax.broadcasted_iota(jnp.int32, (CHUNK, CHUNK), 0)
        c_i = lax.broadcasted_iota(jnp.int32, (CHUNK, CHUNK), 1)
        causal_t = r_i <= c_i
        is_last = lax.broadcasted_iota(jnp.int32, (CHUNK, 1), 0) == CHUNK - 1
        g_ = g_ref[...]
        db_parts, dq_parts, dk_parts = [], [], []
        gg = jnp.zeros((1, HG_HEAD_DIM), F32)
        heads = [slice(h * HG_HEAD_DIM, (h + 1) * HG_HEAD_DIM) for h in range(HG_HEADS)]

        do_parts = []
        for h, sl in enumerate(heads):
            o = o_ref[:, sl]
            ro = _rms(o)
            no = o * ro
            hgh = hg[:, sl]
            sg = _sigmoid(hgh)
            dyh = dy_ref[:, sl]
            dp_ref[:, 3 * D_HG + h * HG_HEAD_DIM:3 * D_HG + (h + 1) * HG_HEAD_DIM] = (
                dyh * no * g_ * sg * (1.0 + hgh * (1.0 - sg))).astype(BF16)
            dng = dyh * hgh * sg
            gg = gg + jnp.sum(dng * no, axis=0, keepdims=True)
            do_parts.append(_rms_bwd(dng * g_, no, ro))
        do_t = jnp.concatenate(do_parts, axis=1).T.astype(BF16)

        fac = []
        for sl, do in zip(heads, do_parts):
            qh, kh, bh = q[:, sl], k[:, sl], b[:, sl]
            b_last, eb, eq, ekh, ek, q_hat, k_til = _hg_head(qh, kh, bh)
            fac.append(dict(qh=qh, kh=kh, e_last=jnp.exp(b_last), eb=eb, eq=eq, ekh=ekh, ek=ek,
                            q_til=qh * eb, k_hat=kh * ekh, qhb=q_hat.astype(BF16), ktb=k_til.astype(BF16),
                            vb=v[:, sl].astype(BF16), dob=do.astype(BF16)))

        first = []
        for sl, t in zip(heads, fac):
            st_h = st_ref[0, sl, :]
            dst_h = dst[sl, :]
            dstb = dst_h.astype(BF16)
            first.append(dict(
                att_t=_dot_nt(t["ktb"], t["qhb"]), datt=_dot_nt(t["dob"], t["vb"]),
                datt_t=_dot_nt(t["vb"], t["dob"]), dk_hat=_dot(t["vb"], dstb),
                dv=_dot_nt(t["k_hat"].astype(BF16), dstb), dq_til=_dot(t["dob"], st_h.astype(BF16)),
                state=t["e_last"] * jnp.sum(dst_h * st_h, axis=0, keepdims=True)))
            dst[sl, :] = dst_h * t["e_last"] + _dot(do_t[sl, :], t["q_til"].astype(BF16))

        for h, (t, m) in enumerate(zip(fac, first)):
            qh, kh, eb, eq, ekh, ek = t["qh"], t["kh"], t["eb"], t["eq"], t["ekh"], t["ek"]
            q_til, k_hat, qhb, ktb, dob = t["q_til"], t["k_hat"], t["qhb"], t["ktb"], t["dob"]
            dk_hat, dq_til = m["dk_hat"], m["dq_til"]
            dv = m["dv"] + _dot(jnp.where(causal_t, m["att_t"], 0.0).astype(BF16), dob)
            dq_hat = _dot(jnp.where(causal, m["datt"], 0.0).astype(BF16), ktb)
            dk_til = _dot(jnp.where(causal_t, m["datt_t"], 0.0).astype(BF16), qhb)
            db_last = jnp.sum(dk_hat * k_hat, axis=0, keepdims=True) + m["state"]
            dq_sel = jnp.concatenate([dq_hat[SUB * s:SUB * (s + 1), s * HG_HEAD_DIM:(s + 1) * HG_HEAD_DIM]
                                      for s in range(N_SUB)], axis=0)
            dq_a = dq_sel * eq
            dk_rows, k_att_rows = [], []
            for b_ in range(N_SUB):
                rs = slice(SUB * b_, SUB * (b_ + 1))
                dk_sum = k_att_sum = None
                for s in range(b_, N_SUB):
                    cs = slice(s * HG_HEAD_DIM, (s + 1) * HG_HEAD_DIM)
                    d = dk_til[rs, cs]
                    t_dk = d * ek[s][rs, :]
                    t_att = ktb[rs, cs].astype(F32) * d
                    dk_sum = t_dk if dk_sum is None else dk_sum + t_dk
                    k_att_sum = t_att if k_att_sum is None else k_att_sum + t_att
                dk_rows.append(dk_sum)
                k_att_rows.append(k_att_sum)
            dk_a = jnp.concatenate(dk_rows, axis=0)
            db = (dq_til * q_til - dk_hat * k_hat + (qh * eq).astype(BF16).astype(F32) * dq_sel
                  - jnp.concatenate(k_att_rows, axis=0))
            db_parts.append(jnp.where(is_last, db + db_last, db))
            dq_parts.append(dq_til * eb + dq_a)
            dk_parts.append(dk_hat * ekh + dk_a)
            dp_ref[:, 2 * D_HG + h * HG_HEAD_DIM:2 * D_HG + (h + 1) * HG_HEAD_DIM] = dv.astype(BF16)

        gg_ref[...] += gg
        db = jnp.concatenate(db_parts, axis=1)
        dq = jnp.concatenate(dq_parts, axis=1)
        dk = jnp.concatenate(dk_parts, axis=1)
        dlf = jnp.where(valid, _running_sum(db, False), 0.0)
        dp_ref[:, :D_HG] = (dq * sq * (1.0 + hq * (1.0 - sq))).astype(BF16)
        df = dlf / f - dk
        dlb = jnp.sum(df * (1.0 - sf), axis=0, keepdims=True) * lb * (1.0 - lb)
        glb_ref[0:1, :] += dlb
        glb_ref[1:2, :] += -dlb
        dp_ref[:, D_HG:2 * D_HG] = (df * (1.0 - lb) * sf * (1.0 - sf)).astype(BF16)

    rev = lambda j: pl.BlockSpec((rows, D_HG), lambda i: (n_steps - 1 - i, j))
    return pl.pallas_call(
        body, grid=(n_steps,),
        in_specs=[rev(2), rev(3), rev(4), rev(5), rev(0),
                  pl.BlockSpec((cps, D_HG, HG_HEAD_DIM), lambda i: (n_steps - 1 - i, 0, 0)), rev(1),
                  pl.BlockSpec((2, D_HG), lambda i: (0, 0)), pl.BlockSpec((1, HG_HEAD_DIM), lambda i: (0, 0))],
        out_specs=[pl.BlockSpec((rows, D_IN), lambda i: (n_steps - 1 - i, 0)),
                   pl.BlockSpec((2, D_HG), lambda i: (0, 0)), pl.BlockSpec((1, HG_HEAD_DIM), lambda i: (0, 0))],
        out_shape=[jax.ShapeDtypeStruct((T, D_IN), BF16), jax.ShapeDtypeStruct((2, D_HG), F32),
                   jax.ShapeDtypeStruct((1, HG_HEAD_DIM), F32)],
        scratch_shapes=[pltpu.VMEM((D_HG, HG_HEAD_DIM), F32)],
        name="hg_bwd", compiler_params=_params("arbitrary"),
    )(p, p, p, p, o_all, st_all, dy, lbraw, hg_g)


def _in_bwd(dp, w_in, h0, g1, dh1):
    T = h0.shape[0]
    tm = _row_tile(T, 832)
    n_steps = T // tm

    def body(dp_ref, w_ref, h_ref, g_ref, d1_ref, gx_hbm, gmeta_ref, gg_ref, buf, sems):
        i = pl.program_id(0)
        first, later = _window_copies(gx_hbm, buf, sems, tm)
        slot = i % 2

        @pl.when(i == 0)
        def _():
            gg_ref[...] = jnp.zeros_like(gg_ref)

        if n_steps > 2:
            @pl.when(i == 2)
            def _():
                first(False).wait()

            @pl.when(i > 2)
            def _():
                later(i - 2, slot, False).wait()

        du = _dot_nt(dp_ref[...], w_ref[...])
        h0_ = h_ref[...]
        r = _rms(h0_)
        n = h0_ * r
        gg_ref[...] += jnp.sum(du * n, axis=0, keepdims=True)
        dh0 = d1_ref[...] + _rms_bwd(du * g_ref[...], n, r)
        buf[slot] = dh0

        @pl.when(i == 0)
        def _():
            gmeta_ref[...] = dh0[PAD:HEAD, :]
            first(False).start()

        if n_steps > 1:
            @pl.when(i > 0)
            def _():
                later(i, slot, False).start()

        @pl.when(i == n_steps - 1)
        def _():
            if n_steps == 1:
                first(False).wait()
            else:
                if n_steps == 2:
                    first(False).wait()
                else:
                    later(i - 1, 1 - slot, False).wait()
                later(i, slot, False).wait()

    row = lambda n: pl.BlockSpec((tm, n), lambda i: (i, 0))
    return pl.pallas_call(
        body, grid=(n_steps,),
        in_specs=[row(D_IN), _resident((D_MODEL, D_IN)),
                  row(D_MODEL), pl.BlockSpec((1, D_MODEL), lambda i: (0, 0)), row(D_MODEL)],
        out_specs=[pl.BlockSpec(memory_space=pl.ANY), pl.BlockSpec((N_META, D_MODEL), lambda i: (0, 0)),
                   pl.BlockSpec((1, D_MODEL), lambda i: (0, 0))],
        out_shape=[jax.ShapeDtypeStruct((T - HEAD, D_MODEL), F32), jax.ShapeDtypeStruct((N_META, D_MODEL), F32),
                   jax.ShapeDtypeStruct((1, D_MODEL), F32)],
        scratch_shapes=[pltpu.VMEM((2, tm, D_MODEL), F32), pltpu.SemaphoreType.DMA((2,))],
        name="in_bwd", compiler_params=_params("arbitrary"),
    )(dp, w_in, h0, g1, dh1)


def _col_tile(cols, target):
    best = None
    for t in range(128, min(cols, target) + 1, 128):
        if cols % t == 0:
            best = t
    assert best is not None, cols
    return best


MXU_DIM = 256


def _mxu_tile(cols, target):
    best = None
    for t in range(MXU_DIM, min(cols, target) + 1, MXU_DIM):
        if cols % t == 0:
            best = t
    assert best is not None, cols
    return best


def _weight_grad(a, b, name):
    T, M = a.shape
    N = b.shape[1]
    tm = _col_tile(M, 1408)
    tn = _mxu_tile(N, 768 if tm <= 1024 else 512)

    def body(a_ref, b_ref, o_ref, ob_ref):
        o = _dot_tn(a_ref[...], b_ref[...])
        o_ref[...] = o
        ob_ref[...] = o.astype(BF16)

    return pl.pallas_call(
        body, grid=(M // tm, N // tn),
        in_specs=[pl.BlockSpec((T, tm), lambda m, n: (0, m)), pl.BlockSpec((T, tn), lambda m, n: (0, n))],
        out_specs=[pl.BlockSpec((tm, tn), lambda m, n: (m, n))] * 2,
        out_shape=[jax.ShapeDtypeStruct((M, N), F32), jax.ShapeDtypeStruct((M, N), BF16)],
        name=name, compiler_params=_params("parallel", "parallel"),
    )(a, b)


def _weight_grad_swapped(a, b, name):
    T, M = a.shape
    N = b.shape[1]
    tn = _mxu_tile(N, 768)
    steps = N // tn
    half = M // 2

    def body(a_ref, b_ref, o_ref, got_ref, stage, send_sems, recv_sems):
        n = pl.program_id(0)
        x, y, c = _place()
        slot = n % 2

        def piece(k, s):
            cols = pl.ds(pl.multiple_of(k * tn, 128), tn)
            return _remote(stage.at[s], got_ref.at[:, cols], send_sems, recv_sems, k, (x, y, 1 - c))

        o_ref[...] = _dot_tn(a_ref[...], b_ref[...])

        @pl.when(n >= 2)
        def _():
            piece(n - 2, slot).wait_send()

        stage[slot] = o_ref[pl.ds(pl.multiple_of((1 - c) * half, 128), half), :].astype(BF16)
        piece(n, slot).start()

        @pl.when(n == steps - 1)
        def _():
            for k in range(max(steps - 2, 0), steps):
                piece(k, k % 2).wait_send()
            for k in range(steps):
                piece(k, k % 2).wait_recv()

    return pl.pallas_call(
        body, grid=(steps,),
        in_specs=[pl.BlockSpec((T, M), lambda n: (0, 0)), pl.BlockSpec((T, tn), lambda n: (0, n))],
        out_specs=[pl.BlockSpec((M, tn), lambda n: (0, n)), ANY],
        out_shape=[jax.ShapeDtypeStruct((M, N), F32), jax.ShapeDtypeStruct((half, N), BF16)],
        scratch_shapes=[pltpu.VMEM((2, half, tn), BF16), pltpu.SemaphoreType.DMA((steps,)),
                        pltpu.SemaphoreType.DMA((steps,))],
        name=name, compiler_params=_params("arbitrary"),
    )(a, b)


def _local_step(x, meta, target, w_in_own, w_in, w_out, w_gu, w_down, small, chip, on_ffn_grads=None,
                on_mixer_grads=None):
    wg = _gate_weights(small["w_rgate"], small["w_igate"])
    bg = jnp.concatenate([small["b_rgate"], small["b_igate"]], axis=1)

    p, u, h0 = _in_proj_local(x, meta, small["mix_norm_g"], w_in_own, chip)
    p = _in_proj_rest(u, w_in, p, chip)
    y_rg, hs, xc = _rg_fwd(p, small["conv_w"], small["conv_b"], wg, bg, small["lru_lambda"], small["rg_norm_g"])
    y_hg, o_all, st_all = _hg_fwd(p, small["hg_lower_bound"], small["hg_norm_g"])
    h1, v, yb, gu, act, dh2, dh2b, loss, g_final = _ffn_fwd(
        h0, y_rg, y_hg, w_out, small["ffn_norm_g"], w_gu, w_down, small["final_norm_g"], target)

    g_w_down = _weight_grad(act, dh2b, "grad_w_down")
    dgu, dh1, dh1b, dy, g_ffn = _ffn_bwd(dh2b, gu, w_down, w_gu, h1, small["ffn_norm_g"], dh2, w_out)
    ffn_grads = {"w_gate_up": _weight_grad(v, dgu, "grad_w_gate_up"), "w_down": g_w_down,
                 "w_out": _weight_grad(yb, dh1b, "grad_w_out")}
    stages = on_ffn_grads(ffn_grads) if on_ffn_grads is not None else None
    dp, g_lb, g_hgn = _hg_bwd(p, o_all, st_all, dy, small["hg_lower_bound"], small["hg_norm_g"])
    early = late = None
    if stages is not None:
        chip_sums, send = stages
        sums = chip_sums()
        (dp, dy), sums = lax.optimization_barrier(((dp, dy), sums))
        early = send(sums)
    dp, g_cw, g_cb, g_wgate, g_bg, g_lam, g_rgn = _rg_bwd(
        p, xc, hs, dy, dp, small["conv_w"], small["conv_b"], wg, bg, small["lru_lambda"], small["rg_norm_g"])
    mixer_grads = {"w_in": (_weight_grad if on_mixer_grads is None else _weight_grad_swapped)(u, dp, "grad_w_in")}
    if on_mixer_grads is not None:
        chip_sums, send = on_mixer_grads(mixer_grads)
        sums = chip_sums()
        (dp, dh1), sums = lax.optimization_barrier(((dp, dh1), sums))
        late = send(sums)
    grad_x, g_meta, g_mix = _in_bwd(dp, w_in, h0, small["mix_norm_g"], dh1)

    grads = {
        "w_in": mixer_grads["w_in"][0], "w_out": ffn_grads["w_out"][0],
        "w_gate_up": ffn_grads["w_gate_up"][0], "w_down": ffn_grads["w_down"][0],
        "meta_tokens": g_meta, "mix_norm_g": g_mix, "conv_w": g_cw, "conv_b": g_cb, "w_gates": g_wgate,
        "b_rgate": g_bg[:, :D_RG], "b_igate": g_bg[:, D_RG:], "lru_lambda": g_lam, "rg_norm_g": g_rgn,
        "hg_lower_bound": g_lb, "hg_norm_g": g_hgn, "ffn_norm_g": g_ffn, "final_norm_g": g_final,
    }
    return loss, grad_x, grads, early, late


ANY = pl.BlockSpec(memory_space=pl.ANY)
HALF = D_MODEL // 2

BIG = {"w_in": (D_MODEL, D_IN // N_CHIPS, True), "w_gate_up": (D_MODEL, 2 * D_FF // N_CHIPS, True),
       "w_out": (D_MODEL // N_CHIPS, D_MODEL, False), "w_down": (D_FF // N_CHIPS, D_MODEL, False)}
BIG_NAMES = tuple(BIG)
N_BIG = len(BIG_NAMES)


def _full_shape(name):
    rows, cols, by_col = BIG[name]
    return (rows, cols * N_CHIPS) if by_col else (rows * N_CHIPS, cols)


def _place():
    return lax.axis_index("x"), lax.axis_index("y"), lax.axis_index("c")


def _chip_of(x, y, r):
    fx, fy = (r + 1) >> 1, (r + 1) & 1
    return (1 - x if fx else x), (1 - y if fy else y)


def _half_of(ref, by_col, half):
    start = pl.multiple_of(half * HALF, 128)
    return ref.at[pl.ds(start, HALF), :] if by_col else ref.at[:, pl.ds(start, HALF)]


def _shard_of(ref, name, chip):
    rows, cols, by_col = BIG[name]
    if by_col:
        return ref.at[:, pl.ds(pl.multiple_of(chip * cols, 128), cols)]
    return ref.at[pl.ds(pl.multiple_of(chip * rows, 16), rows), :]


def _shard_half_of(ref, name, chip, half):
    rows, cols, by_col = BIG[name]
    start = pl.multiple_of(half * HALF, 128)
    if by_col:
        return ref.at[pl.ds(start, HALF), pl.ds(pl.multiple_of(chip * cols, 128), cols)]
    return ref.at[pl.ds(pl.multiple_of(chip * rows, 16), rows), pl.ds(start, HALF)]


def _shard_half_part_of(ref, name, chip, half, part):
    rows, cols, by_col = BIG[name]
    start = pl.multiple_of(half * HALF + part * (HALF // 2), 128)
    if by_col:
        return ref.at[pl.ds(start, HALF // 2), pl.ds(pl.multiple_of(chip * cols, 128), cols)]
    return ref.at[pl.ds(pl.multiple_of(chip * rows, 16), rows), pl.ds(start, HALF // 2)]


def _remote(src, dst, send_sems, recv_sems, k, dev):
    return pltpu.make_async_remote_copy(src_ref=src, dst_ref=dst, send_sem=send_sems.at[k], recv_sem=recv_sems.at[k],
                                        device_id=dev, device_id_type=MESH)


def _place_shards(w, small, chip, names, label):
    steps = 4
    n, ns = len(names), len(small)
    in_specs, out_specs = [], []
    for name in names:
        rows, cols, by_col = BIG[name]
        tr = rows // steps
        in_specs.append(pl.BlockSpec((tr, cols), lambda i, s: (i, 0)))
        if by_col:
            out_specs.append(pl.BlockSpec((tr, cols), lambda i, s: (i, s[0])))
        else:
            out_specs.append(pl.BlockSpec((tr, cols), lambda i, s: (s[0] * steps + i, 0)))

    def body(s_ref, *refs):
        ins, small_in = refs[:n], refs[n:n + ns]
        outs, small_out = refs[n + ns:2 * n + ns], refs[2 * n + ns:2 * (n + ns)]
        send_sems, recv_sems, local_sems = refs[2 * (n + ns):]
        i = pl.program_id(0)
        x, y, c = _place()
        chip_ = 2 * x + y
        others = [_chip_of(x, y, r) for r in range(3)]

        def block(a, q):
            cols = small[a].shape[1]
            return small_out[a].at[:, pl.ds(pl.multiple_of(q * cols, 128), cols)]

        def local(a):
            return pltpu.make_async_copy(small_in[a], block(a, chip_), local_sems.at[a])

        def remote(a, r):
            qx, qy = others[r]
            return _remote(small_in[a], block(a, chip_), send_sems, recv_sems, 3 * a + r, (qx, qy, c))

        @pl.when(i == 0)
        def _():
            for a in range(ns):
                local(a).start()
                for r in range(3):
                    remote(a, r).start()

        for a in range(n):
            outs[a][...] = ins[a][...].astype(BF16)

        @pl.when(i == steps - 1)
        def _():
            for a in range(ns):
                for r, (qx, qy) in enumerate(others):
                    landed = block(a, 2 * qx + qy)
                    _remote(landed, landed, send_sems, recv_sems, 3 * a + r, (qx, qy, c)).wait_recv()
                for r in range(3):
                    remote(a, r).wait_send()
                local(a).wait()

    out = pl.pallas_call(
        body,
        grid_spec=pltpu.PrefetchScalarGridSpec(
            num_scalar_prefetch=1, grid=(steps,), in_specs=in_specs + [ANY] * ns, out_specs=out_specs + [ANY] * ns,
            scratch_shapes=[pltpu.SemaphoreType.DMA((max(3 * ns, 1),)), pltpu.SemaphoreType.DMA((max(3 * ns, 1),)),
                            pltpu.SemaphoreType.DMA((max(ns, 1),))]),
        out_shape=([jax.ShapeDtypeStruct(_full_shape(name), BF16) for name in names]
                   + [jax.ShapeDtypeStruct((s.shape[0], s.shape[1] * N_CHIPS), F32) for s in small]),
        name=label, compiler_params=_params("arbitrary"),
    )(chip, *[w[name] for name in names], *small)
    return dict(zip(names, out[:n])), list(out[n:])


def _gather_weights(placed, small, names, label, collective_id):
    n, ns = len(names), len(small)
    hbm = pltpu.MemorySpace.HBM
    outs = [jax.new_ref(placed[nm], memory_space=hbm) for nm in names]
    small_in = [jax.new_ref(s, memory_space=hbm) for s in small]
    small_out = [jax.empty_ref(jax.ShapeDtypeStruct((s.shape[0], s.shape[1] * N_CHIPS), F32), memory_space=hbm)
                 for s in small]
    n_sems = 8 * n + 3 * ns

    @pl.kernel(mesh=plsc.ScalarSubcoreMesh(axis_name="seq", num_cores=1), name=label, out_type=(),
               scratch_types=(pltpu.SemaphoreType.DMA((n_sems,)), pltpu.SemaphoreType.DMA((n_sems,)),
                              pltpu.SemaphoreType.DMA((max(ns, 1),))),
               compiler_params=pltpu.CompilerParams(collective_id=collective_id))
    def launch(send_sems, recv_sems, local_sems):
        x, y, c = _place()
        chip = 2 * x + y
        sibling = (x, y, 1 - c)
        others = [_chip_of(x, y, r) for r in range(3)]
        near = others[:2]
        far = 2 * others[2][0] + others[2][1]
        _handshake([(qx, qy, c) for qx, qy in others] + [sibling])

        def small_block(a, q):
            cols = small[a].shape[1]
            return small_out[a].at[:, pl.ds(pl.multiple_of(q * cols, 128), cols)]

        local = [pltpu.make_async_copy(small_in[a], small_block(a, chip), local_sems.at[a]) for a in range(ns)]
        for cp in local:
            cp.start()

        sends = []
        for a, name in enumerate(names):
            mine = _shard_half_of(outs[a], name, chip, c)
            for r, (qx, qy) in enumerate(near):
                sends.append(_remote(mine, mine, send_sems, recv_sems, 8 * a + r, (qx, qy, c)))
        for a in range(ns):
            for r, (qx, qy) in enumerate(others):
                sends.append(_remote(small_in[a], small_block(a, chip), send_sems, recv_sems,
                                     8 * n + 3 * a + r, (qx, qy, c)))
        for cp in sends:
            cp.start()

        forwards = []

        def forward(piece, k, dev):
            cp = _remote(piece, piece, send_sems, recv_sems, k, dev)
            cp.start()
            forwards.append(cp)

        for a, name in enumerate(names):
            for r, (qx, qy) in enumerate(near):
                landed = _shard_half_of(outs[a], name, 2 * qx + qy, c)
                _remote(landed, landed, send_sems, recv_sems, 8 * a + r, (qx, qy, c)).wait_recv()
                ox, oy = near[1 - r]
                forward(_shard_half_part_of(outs[a], name, 2 * qx + qy, c, r), 8 * a + 2 + r, (ox, oy, c))
                forward(landed, 8 * a + 4 + r, sibling)
        for a, name in enumerate(names):
            for part in range(2):
                qx, qy = near[1 - part]
                landed = _shard_half_part_of(outs[a], name, far, c, part)
                _remote(landed, landed, send_sems, recv_sems, 8 * a + 2 + part, (qx, qy, c)).wait_recv()
                forward(landed, 8 * a + 6 + part, sibling)
        for a in range(ns):
            for r, (qx, qy) in enumerate(others):
                landed = small_block(a, 2 * qx + qy)
                _remote(landed, landed, send_sems, recv_sems, 8 * n + 3 * a + r, (qx, qy, c)).wait_recv()
        for a, name in enumerate(names):
            for r, (qx, qy) in enumerate(near):
                landed = _shard_half_of(outs[a], name, 2 * qx + qy, 1 - c)
                _remote(landed, landed, send_sems, recv_sems, 8 * a + 4 + r, sibling).wait_recv()
            for part in range(2):
                landed = _shard_half_part_of(outs[a], name, far, 1 - c, part)
                _remote(landed, landed, send_sems, recv_sems, 8 * a + 6 + part, sibling).wait_recv()
        for cp in sends + forwards:
            cp.wait_send()
        for cp in local:
            cp.wait()

    launch()
    return {nm: ref[...] for nm, ref in zip(names, outs)}, [ref[...] for ref in small_out]


def _exchange_halves(grads, names, label, collective_id):
    n = len(names)

    def body(*refs):
        ins, outs = refs[:n], refs[n:2 * n]
        send_sems, recv_sems = refs[2 * n:]
        x, y, c = _place()
        _handshake([(x, y, 1 - c)])
        copies = []
        for a, name in enumerate(names):
            copies.append(_remote(_half_of(ins[a], BIG[name][2], 1 - c), outs[a], send_sems, recv_sems, a,
                                  (x, y, 1 - c)))
        for cp in copies:
            cp.start()
        for cp in copies:
            cp.wait()

    def half_shape(name):
        r, c_ = _full_shape(name)
        return (HALF, c_) if BIG[name][2] else (r, HALF)

    out_type = tuple(jax.ShapeDtypeStruct(half_shape(nm), grads[nm].dtype) for nm in names)
    sems = (pltpu.SemaphoreType.DMA((n,)), pltpu.SemaphoreType.DMA((n,)))
    got = pl.kernel(
        body, mesh=plsc.ScalarSubcoreMesh(axis_name="seq", num_cores=1), name=label, out_type=out_type,
        scratch_types=sems, compiler_params=pltpu.CompilerParams(collective_id=collective_id),
    )(*[grads[nm] for nm in names])
    return dict(zip(names, got))


def _chip_sum(grads, got, names, core, label):
    n = len(names)
    steps = 4
    g_specs, blks = [], []
    for name in names:
        rows, cols = got[name].shape
        tr = rows // steps
        if BIG[name][2]:
            g_specs.append(pl.BlockSpec((tr, cols), lambda i, s: (s[0] * steps + i, 0)))
        else:
            g_specs.append(pl.BlockSpec((tr, HALF), lambda i, s: (i, s[0])))
        blks.append(pl.BlockSpec((tr, cols), lambda i, s: (i, 0)))

    def body(s_ref, *refs):
        for a in range(n):
            t = refs[a][...] + refs[n + a][...].astype(F32)
            refs[2 * n + a][...] = t
            refs[3 * n + a][...] = t.astype(BF16)

    out = pl.pallas_call(
        body,
        grid_spec=pltpu.PrefetchScalarGridSpec(num_scalar_prefetch=1, grid=(steps,), in_specs=g_specs + blks,
                                               out_specs=blks + blks),
        out_shape=([jax.ShapeDtypeStruct(got[nm].shape, F32) for nm in names]
                   + [jax.ShapeDtypeStruct(got[nm].shape, BF16) for nm in names]),
        name=label, compiler_params=_params("parallel"),
    )(core, *[grads[nm] for nm in names], *[got[nm] for nm in names])
    return {nm: (out[a], out[n + a]) for a, nm in enumerate(names)}


def _piece_shape(name):
    rows, cols, by_col = BIG[name]
    return (HALF, cols) if by_col else (rows, HALF)


def _handshake(peers):
    barrier = pltpu.get_barrier_semaphore()
    for peer in peers:
        pl.semaphore_signal(barrier, inc=1, device_id=peer, device_id_type=MESH)
    pl.semaphore_wait(barrier, len(peers))


def _send_chip_sums(sums, names, label, collective_id):
    n = len(names)

    def body(*refs):
        ins, outs = refs[:n], refs[n:2 * n]
        send_sems, recv_sems = refs[2 * n:]
        x, y, c = _place()
        others = [_chip_of(x, y, r) for r in range(3)]
        _handshake([(qx, qy, c) for qx, qy in others])
        copies = []
        for a, name in enumerate(names):
            for r, (qx, qy) in enumerate(others):
                copies.append(_remote(_shard_of(ins[a], name, 2 * qx + qy), outs[a].at[r], send_sems, recv_sems,
                                      3 * a + r, (qx, qy, c)))
        for cp in copies:
            cp.start()
        for cp in copies:
            cp.wait()

    return pl.kernel(
        body, mesh=plsc.ScalarSubcoreMesh(axis_name="seq", num_cores=1), name=label,
        out_type=tuple(jax.ShapeDtypeStruct((3,) + _piece_shape(nm), BF16) for nm in names),
        scratch_types=(pltpu.SemaphoreType.DMA((3 * n,)), pltpu.SemaphoreType.DMA((3 * n,))),
        compiler_params=pltpu.CompilerParams(collective_id=collective_id),
    )(*[sums[nm] for nm in names])


def _total(parts, chip_core):
    steps = 2
    in_specs, out_specs, operands = [], [], []
    for name in BIG_NAMES:
        by_col = BIG[name][2]
        pr, pc = _piece_shape(name)
        tr = pr // steps
        if by_col:
            in_specs.append(pl.BlockSpec((tr, pc), lambda i, s: (i, s[0])))
            out_specs.append(pl.BlockSpec((tr, pc), lambda i, s: (s[1] * steps + i, 0)))
        else:
            in_specs.append(pl.BlockSpec((tr, pc), lambda i, s: (s[0] * steps + i, 0)))
            out_specs.append(pl.BlockSpec((tr, pc), lambda i, s: (i, s[1])))
        for r in range(3):
            in_specs.append(pl.BlockSpec((None, tr, pc), lambda i, s, r=r: (r, i, 0)))
        own, got = parts[name]
        operands += [own, got, got, got]

    def body(s_ref, *refs):
        for a in range(N_BIG):
            o_ref, a_ref, b_ref, c_ref = refs[4 * a:4 * a + 4]
            refs[4 * N_BIG + a][...] = (((o_ref[...] + a_ref[...].astype(F32)) + b_ref[...].astype(F32))
                                        + c_ref[...].astype(F32))

    totals = pl.pallas_call(
        body,
        grid_spec=pltpu.PrefetchScalarGridSpec(num_scalar_prefetch=1, grid=(steps,), in_specs=in_specs,
                                               out_specs=out_specs),
        out_shape=[jax.ShapeDtypeStruct(BIG[name][:2], F32) for name in BIG_NAMES],
        name="totals", compiler_params=_params("parallel"),
    )(chip_core, *operands)
    return dict(zip(BIG_NAMES, totals))


VEC_ROWS = 32
VEC_ROW = {"mix_norm_g": 0, "conv_b": 1, "b_rgate": 2, "b_igate": 3, "lru_lambda": 4, "rg_norm_g": 5,
           "hg_lower_bound": 6, "hg_norm_g": 8, "ffn_norm_g": 9, "final_norm_g": 10, "loss": 11,
           "conv_w": 12, "meta_tokens": 16}
N_DEV = 8


def _all_reduce_small(pieces, gates, totals):
    names = list(pieces)
    n_small = 10
    hv, hg = VEC_ROWS // 2, gates.shape[0] // 2

    def body(*refs):
        ins = refs[:len(names)]
        g_ref = refs[len(names)]
        vec_ref, gsum_ref = refs[len(names) + 1 + N_BIG:len(names) + 3 + N_BIG]
        big = refs[len(names) + 3 + N_BIG:len(names) + 3 + 2 * N_BIG]
        (mine_v, sib_v, sib_g, chip_v, chip_g, got_v, got_g, send_sems, recv_sems) = refs[len(names) + 3 + 2 * N_BIG:]
        x, y, c = _place()
        chip = 2 * x + y
        sibling = (x, y, 1 - c)
        share = []
        for a, name in enumerate(BIG_NAMES):
            half = _half_of(big[a], BIG[name][2], c)
            share.append(_remote(half, half, send_sems, recv_sems, n_small + a, sibling))
        mine_v[...] = jnp.zeros_like(mine_v)
        for name, ref in zip(names, ins):
            nr, w = ref.shape
            mine_v[VEC_ROW[name]:VEC_ROW[name] + nr, 0:w] = ref[...]

        swap = [_remote(mine_v, sib_v, send_sems, recv_sems, 0, sibling),
                _remote(g_ref, sib_g, send_sems, recv_sems, 1, sibling)]
        for cp in swap:
            cp.start()
        for cp in swap:
            cp.wait()
        for cp in share:
            cp.start()
        chip_v[...] = mine_v[...] + sib_v[...]
        chip_g[...] = g_ref[...] + sib_g[...]

        rows_v = pl.ds(pl.multiple_of(c * hv, 8), hv)
        rows_g = pl.ds(pl.multiple_of(c * hg, 8), hg)
        got_v[chip] = chip_v[rows_v, :]
        got_g[chip] = chip_g[rows_g, :].astype(BF16)
        sends = []
        for r in range(3):
            qx, qy = _chip_of(x, y, r)
            sends.append(_remote(chip_v.at[rows_v, :], got_v.at[chip], send_sems, recv_sems, 2 + r, (qx, qy, c)))
            sends.append(_remote(got_g.at[chip], got_g.at[chip], send_sems, recv_sems, 5 + r, (qx, qy, c)))
        for cp in sends:
            cp.start()
        for cp in sends:
            cp.wait()
        vec_ref[rows_v, :] = ((got_v[0] + got_v[1]) + got_v[2]) + got_v[3]
        gsum_ref[rows_g, :] = ((got_g[0].astype(F32) + got_g[1].astype(F32)) + got_g[2].astype(F32)
                               ) + got_g[3].astype(F32)

        back = [_remote(vec_ref.at[rows_v, :], vec_ref.at[rows_v, :], send_sems, recv_sems, 8, sibling),
                _remote(gsum_ref.at[rows_g, :], gsum_ref.at[rows_g, :], send_sems, recv_sems, 9, sibling)]
        for cp in back:
            cp.start()
        theirs_v = vec_ref.at[pl.ds(pl.multiple_of((1 - c) * hv, 8), hv), :]
        theirs_g = gsum_ref.at[pl.ds(pl.multiple_of((1 - c) * hg, 8), hg), :]
        _remote(theirs_v, theirs_v, send_sems, recv_sems, 8, sibling).wait_recv()
        _remote(theirs_g, theirs_g, send_sems, recv_sems, 9, sibling).wait_recv()
        for cp in back:
            cp.wait_send()
        for a, name in enumerate(BIG_NAMES):
            theirs = _half_of(big[a], BIG[name][2], 1 - c)
            _remote(theirs, theirs, send_sems, recv_sems, n_small + a, sibling).wait_recv()
        for cp in share:
            cp.wait_send()

    vmem = pl.BlockSpec(memory_space=pltpu.VMEM)
    n_sems = n_small + N_BIG
    out = pl.pallas_call(
        body, in_specs=[vmem] * (len(names) + 1) + [ANY] * N_BIG, out_specs=[vmem, vmem] + [ANY] * N_BIG,
        out_shape=([jax.ShapeDtypeStruct((VEC_ROWS, D_MODEL), F32), jax.ShapeDtypeStruct(gates.shape, F32)]
                   + [jax.ShapeDtypeStruct(BIG[n][:2], F32) for n in BIG_NAMES]),
        input_output_aliases={len(names) + 1 + a: 2 + a for a in range(N_BIG)},
        scratch_shapes=[pltpu.VMEM((VEC_ROWS, D_MODEL), F32), pltpu.VMEM((VEC_ROWS, D_MODEL), F32),
                        pltpu.VMEM(gates.shape, F32), pltpu.VMEM((VEC_ROWS, D_MODEL), F32),
                        pltpu.VMEM(gates.shape, F32), pltpu.VMEM((N_CHIPS, hv, D_MODEL), F32),
                        pltpu.VMEM((N_CHIPS, hg) + gates.shape[1:], BF16),
                        pltpu.SemaphoreType.DMA((n_sems,)), pltpu.SemaphoreType.DMA((n_sems,))],
        name="all_reduce_small",
    )(*[pieces[n] for n in names], gates, *[totals[n] for n in BIG_NAMES])
    return out[0], out[1], dict(zip(BIG_NAMES, out[2:]))


def _adamw_math(w, g, m, v):
    m = ADAM_B1 * m + (1.0 - ADAM_B1) * g
    v = ADAM_B2 * v + (1.0 - ADAM_B2) * (g * g)
    m_hat = m / (1.0 - ADAM_B1 ** ADAM_STEP)
    v_hat = v / (1.0 - ADAM_B2 ** ADAM_STEP)
    delta = -ADAM_LR * (m_hat / (jnp.sqrt(v_hat) + ADAM_EPS) + ADAM_WD * w)
    return delta, m, v


def _adamw_big(w, g, m, v):
    steps = 8
    blks = []
    for name in BIG_NAMES:
        rows, cols, _ = BIG[name]
        blks.append(pl.BlockSpec((rows // steps, cols), lambda i: (i, 0)))

    def body(*refs):
        ins, outs = refs[:4 * N_BIG], refs[4 * N_BIG:]
        for a in range(N_BIG):
            w_ref, g_ref, m_ref, v_ref = (ins[k * N_BIG + a] for k in range(4))
            g = g_ref[...]
            d, nm, nv = _adamw_math(w_ref[...], g, m_ref[...], v_ref[...])
            outs[a][...] = g
            outs[N_BIG + a][...] = d
            outs[2 * N_BIG + a][...] = nm
            outs[3 * N_BIG + a][...] = nv

    shapes = [jax.ShapeDtypeStruct(BIG[name][:2], F32) for name in BIG_NAMES]
    out = pl.pallas_call(
        body, grid=(steps,), in_specs=blks * 4, out_specs=blks * 4, out_shape=shapes * 4,
        name="adamw_big", compiler_params=_params("parallel"),
    )(*[t[name] for t in (w, g, m, v) for name in BIG_NAMES])
    return {name: tuple(out[k * N_BIG + a] for k in range(4)) for a, name in enumerate(BIG_NAMES)}


SMALL = {"meta_tokens": (N_META, D_MODEL // N_CHIPS), "mix_norm_g": (1, D_MODEL), "conv_w": (CONV_W, D_RG // N_CHIPS),
         "conv_b": (1, D_RG), "w_rgate": (D_RG, RG_HEAD_DIM), "b_rgate": (1, D_RG), "w_igate": (D_RG, RG_HEAD_DIM),
         "b_igate": (1, D_RG), "lru_lambda": (1, D_RG), "rg_norm_g": (1, D_RG), "hg_lower_bound": (2, D_HG),
         "hg_norm_g": (1, HG_HEAD_DIM), "ffn_norm_g": (1, D_MODEL), "final_norm_g": (1, D_MODEL)}
SMALL_NAMES = tuple(SMALL)
SHARDED_SMALL = ("meta_tokens", "conv_w")


def _adamw_small(vec, gates, w, m, v):
    n = len(SMALL_NAMES)

    def body(*refs):
        vec_ref, gates_ref = refs[:2]
        w_refs, m_refs, v_refs = refs[2:2 + n], refs[2 + n:2 + 2 * n], refs[2 + 2 * n:2 + 3 * n]
        outs = refs[2 + 3 * n:]
        loss_ref = outs[0]
        x, y, _ = _place()
        chip = 2 * x + y
        loss_ref[...] = vec_ref[VEC_ROW["loss"]:VEC_ROW["loss"] + 1, 0:1]

        def update(k, g):
            g_ref, d_ref, nm_ref, nv_ref = outs[1 + 4 * k:5 + 4 * k]
            g_ref[...] = g
            d_ref[...], nm_ref[...], nv_ref[...] = _adamw_math(w_refs[k][...], g, m_refs[k][...], v_refs[k][...])

        for k, name in enumerate(SMALL_NAMES):
            nr, w_ = SMALL[name]
            if name == "w_rgate":
                update(k, gates_ref[0:D_RG, :])
            elif name == "w_igate":
                update(k, gates_ref[D_RG:2 * D_RG, :])
            elif name in SHARDED_SMALL:
                r0 = VEC_ROW[name]
                for q in range(N_CHIPS):
                    @pl.when(chip == q)
                    def _(k=k, r0=r0, nr=nr, w_=w_, q=q):
                        update(k, vec_ref[r0:r0 + nr, q * w_:(q + 1) * w_])
            else:
                r0 = VEC_ROW[name]
                update(k, vec_ref[r0:r0 + nr, 0:w_])

    vmem = pl.BlockSpec(memory_space=pltpu.VMEM)
    out_shape = [jax.ShapeDtypeStruct((1, 1), F32)]
    for name in SMALL_NAMES:
        out_shape += [jax.ShapeDtypeStruct(SMALL[name], F32)] * 4
    outs = pl.pallas_call(
        body, in_specs=[vmem] * (2 + 3 * n), out_specs=[vmem] * len(out_shape), out_shape=out_shape,
        name="adamw_small",
    )(vec, gates, *[w[k] for k in SMALL_NAMES], *[m[k] for k in SMALL_NAMES], *[v[k] for k in SMALL_NAMES])
    loss = outs[0]
    res = {name: tuple(outs[1 + 4 * k:5 + 4 * k]) for k, name in enumerate(SMALL_NAMES)}
    return loss, res


WEIGHT_NAMES = ("meta_tokens", "mix_norm_g", "w_in", "conv_w", "conv_b", "w_rgate", "b_rgate", "w_igate", "b_igate",
                "lru_lambda", "rg_norm_g", "hg_lower_bound", "hg_norm_g", "w_out", "ffn_norm_g", "w_gate_up", "w_down",
                "final_norm_g")


def _to_2d(name, a):
    if name in BIG:
        return a.reshape(BIG[name][:2])
    return a.reshape(SMALL[name])


def kernel(x, meta_tokens, mix_norm_g, w_in, conv_w, conv_b, w_rgate, b_rgate, w_igate, b_igate, lru_lambda, rg_norm_g, hg_lower_bound, hg_norm_g, w_out, ffn_norm_g, w_gate_up, w_down, final_norm_g, loss_target, m_meta_tokens, m_mix_norm_g, m_w_in, m_conv_w, m_conv_b, m_w_rgate, m_b_rgate, m_w_igate, m_b_igate, m_lru_lambda, m_rg_norm_g, m_hg_lower_bound, m_hg_norm_g, m_w_out, m_ffn_norm_g, m_w_gate_up, m_w_down, m_final_norm_g, v_meta_tokens, v_mix_norm_g, v_w_in, v_conv_w, v_conv_b, v_w_rgate, v_b_rgate, v_w_igate, v_b_igate, v_lru_lambda, v_rg_norm_g, v_hg_lower_bound, v_hg_norm_g, v_w_out, v_ffn_norm_g, v_w_gate_up, v_w_down, v_final_norm_g):
    w_raw = dict(zip(WEIGHT_NAMES, (meta_tokens, mix_norm_g, w_in, conv_w, conv_b, w_rgate, b_rgate, w_igate, b_igate,
                                    lru_lambda, rg_norm_g, hg_lower_bound, hg_norm_g, w_out, ffn_norm_g, w_gate_up,
                                    w_down, final_norm_g)))
    m_raw = dict(zip(WEIGHT_NAMES, (m_meta_tokens, m_mix_norm_g, m_w_in, m_conv_w, m_conv_b, m_w_rgate, m_b_rgate,
                                    m_w_igate, m_b_igate, m_lru_lambda, m_rg_norm_g, m_hg_lower_bound, m_hg_norm_g,
                                    m_w_out, m_ffn_norm_g, m_w_gate_up, m_w_down, m_final_norm_g)))
    v_raw = dict(zip(WEIGHT_NAMES, (v_meta_tokens, v_mix_norm_g, v_w_in, v_conv_w, v_conv_b, v_w_rgate, v_b_rgate,
                                    v_w_igate, v_b_igate, v_lru_lambda, v_rg_norm_g, v_hg_lower_bound, v_hg_norm_g,
                                    v_w_out, v_ffn_norm_g, v_w_gate_up, v_w_down, v_final_norm_g)))
    w = {k: _to_2d(k, a) for k, a in w_raw.items()}
    m = {k: _to_2d(k, a) for k, a in m_raw.items()}
    v = {k: _to_2d(k, a) for k, a in v_raw.items()}

    x_i, y_i, c_i = _place()
    core = jnp.reshape(c_i, (1,)).astype(jnp.int32)
    chip = jnp.reshape(2 * x_i + y_i, (1,)).astype(jnp.int32)
    chip_core = jnp.concatenate([chip, core])

    first_names, rest_names = ("w_in",), ("w_out", "w_gate_up", "w_down")
    placed, _ = _place_shards(w, [], chip, first_names, "place_first")
    first, _ = _gather_weights(placed, [], first_names, "gather_first", 1)
    placed, (meta_full, cw_full) = _place_shards(w, [w["meta_tokens"], w["conv_w"]], chip, rest_names, "place_shards")
    rest, _ = _gather_weights(placed, [], rest_names, "gather_rest", 2)
    full = {**first, **rest}

    seq = x.shape[1]
    small ={k: w[k] for k in SMALL_NAMES if k not in SHARDED_SMALL}
    small["conv_w"] = cw_full

    def reduce_to_chips(grads, names, tag, collective_ids):
        if collective_ids[0] is None:
            got = {n: grads[n][1] for n in names}
        else:
            got = _exchange_halves({n: grads[n][1] for n in names}, names, "exchange_halves_" + tag,
                                   collective_ids[0])

        def chip_sums():
            return _chip_sum({n: grads[n][0] for n in names}, got, names, core, "chip_sum_" + tag)

        def send(sums):
            arrived = _send_chip_sums({n: sums[n][1] for n in names}, names, "send_chip_sums_" + tag,
                                      collective_ids[1])
            return {n: (sums[n][0], a) for n, a in zip(names, arrived)}

        return chip_sums, send

    ffn_names, mixer_names = ("w_gate_up", "w_down", "w_out"), ("w_in",)
    loss, grad_x, grads, parts, parts_mixer = _local_step(
        x.reshape(seq, D_MODEL), meta_full, loss_target.reshape(seq, D_MODEL),
        w["w_in"], full["w_in"], full["w_out"], full["w_gate_up"], full["w_down"], small, chip,
        on_ffn_grads=lambda g: reduce_to_chips(g, ffn_names, "ffn", (3, 4)),
        on_mixer_grads=lambda g: reduce_to_chips(g, mixer_names, "mixer", (None, 5)))
    parts.update(parts_mixer)
    totals = _total(parts, chip_core)
    pieces = {k: grads[k] for k in VEC_ROW if k != "loss"}
    pieces["loss"] = loss
    vec, gates, g_big = _all_reduce_small(pieces, grads["w_gates"], totals)
    loss_sum, res = _adamw_small(vec, gates, w, m, v)
    res.update(_adamw_big(w, g_big, m, v))

    out = [loss_sum.reshape(()), grad_x.reshape(1, seq, D_MODEL)]
    for j in range(4):
        out += [res[n][j].reshape(w_raw[n].shape) for n in WEIGHT_NAMES]
    return tuple(out)
```

```python
import math

import jax
import jax.numpy as jnp
from jax import lax
from jax.experimental import pallas as pl
from jax.experimental.pallas import tpu as pltpu
from jax.experimental.pallas import tpu_sc as plsc

F32 = jnp.float32
BF16 = jnp.bfloat16
MESH = pl.DeviceIdType.MESH

D_MODEL = 1024
D_RG = 512
RG_HEAD_DIM = 64
D_HG = 512
HG_HEAD_DIM = 128
HG_HEADS = 4
CHUNK = 64
SUB = 16
N_SUB = CHUNK // SUB
N_META = 16
PAD = CHUNK - N_META
D_IN = 3072
D_FF = 2816
CONV_W = 4
LRU_C = 8.0
EPS = 1e-6
EXP_CLAMP = 80.0
GELU_C = math.sqrt(2.0 / math.pi)
GELU_A = 0.044715
N_CHIPS = 4

ADAM_LR = 0.001
ADAM_B1 = 0.9
ADAM_B2 = 0.999
ADAM_EPS = 1e-08
ADAM_WD = 0.01
ADAM_STEP = 10

VMEM_LIMIT = 56 * 1024 * 1024


def _params(*sem):
    return pltpu.CompilerParams(dimension_semantics=sem, vmem_limit_bytes=VMEM_LIMIT)


def _row_tile(rows, target):
    best = None
    for t in range(16, min(rows, target) + 1, 16):
        if rows % t == 0:
            best = t
    assert best is not None, rows
    return best


def _sigmoid(x):
    return 0.5 * jnp.tanh(0.5 * x) + 0.5


def _dot(a, b):
    return jnp.dot(a, b, preferred_element_type=F32)


def _dot_nt(a, b):
    return lax.dot_general(a, b, (((1,), (1,)), ((), ())), preferred_element_type=F32)


def _dot_tn(a, b):
    return lax.dot_general(a, b, (((0,), (0,)), ((), ())), preferred_element_type=F32)


def _rms(x):
    return lax.rsqrt(jnp.mean(x * x, axis=-1, keepdims=True) + EPS)


def _rms_bwd(dn, n, r):
    return r * (dn - n * jnp.mean(dn * n, axis=-1, keepdims=True))


def _gelu_parts(x):
    t = jnp.tanh(GELU_C * (x + GELU_A * x * x * x))
    g = 0.5 * x * (1.0 + t)
    dg = 0.5 * (1.0 + t) + 0.5 * x * (1.0 - t * t) * GELU_C * (1.0 + 3.0 * GELU_A * x * x)
    return g, dg


def _softplus_neg(lam):
    e = jnp.exp(-jnp.abs(lam))
    w = 1.0 + e
    log1p = jnp.where(w == 1.0, e, jnp.log(w) * e / (w - 1.0))
    return jnp.maximum(-lam, 0.0) + log1p


def _head_mask():
    r = lax.broadcasted_iota(jnp.int32, (D_RG, D_RG), 0) // RG_HEAD_DIM
    c = lax.broadcasted_iota(jnp.int32, (D_RG, D_RG), 1) // RG_HEAD_DIM
    return r == c


def _head_fold():
    r = lax.broadcasted_iota(jnp.int32, (D_RG, RG_HEAD_DIM), 0) % RG_HEAD_DIM
    c = lax.broadcasted_iota(jnp.int32, (D_RG, RG_HEAD_DIM), 1)
    return (r == c).astype(F32)


def _gate_weights(w_r, w_i):
    def body(wr_ref, wi_ref, o_ref):
        fold = _head_fold()
        mask = _head_mask()
        for k, ref in enumerate((wr_ref, wi_ref)):
            full = _dot_nt(ref[...].astype(BF16), fold.astype(BF16))
            o_ref[:, k * D_RG:(k + 1) * D_RG] = jnp.where(mask, full, 0.0).astype(BF16)

    return pl.pallas_call(
        body, out_shape=jax.ShapeDtypeStruct((D_RG, 2 * D_RG), BF16), name="gate_weights",
    )(w_r, w_i)


HEAD = PAD + N_META


def _window_copies(seq_hbm, buf, sems, tm):
    def first(to_vmem):
        seq, vm = seq_hbm.at[pl.ds(0, tm - HEAD)], buf.at[0, pl.ds(HEAD, tm - HEAD)]
        return pltpu.make_async_copy(seq, vm, sems.at[0]) if to_vmem else pltpu.make_async_copy(vm, seq, sems.at[0])

    def later(j, slot, to_vmem):
        seq, vm = seq_hbm.at[pl.ds(pl.multiple_of(j * tm - HEAD, 8), tm)], buf.at[slot]
        if to_vmem:
            return pltpu.make_async_copy(seq, vm, sems.at[slot])
        return pltpu.make_async_copy(vm, seq, sems.at[slot])

    return first, later


def _fetch_window(seq_hbm, buf, sems, i, n_steps, tm):
    first, later = _window_copies(seq_hbm, buf, sems, tm)
    slot = i % 2

    @pl.when(i == 0)
    def _():
        first(True).start()

    if n_steps > 1:
        @pl.when(i + 1 < n_steps)
        def _():
            later(i + 1, 1 - slot, True).start()

    @pl.when(i == 0)
    def _():
        first(True).wait()

    if n_steps > 1:
        @pl.when(i > 0)
        def _():
            later(i, slot, True).wait()

    return slot


def _in_proj_local(x, meta, g1, w_own, chip):
    T = x.shape[0] + HEAD
    tm = _row_tile(T, 832)
    n_steps = T // tm
    cols = BIG["w_in"][1]

    def body(s_ref, x_hbm, meta_ref, g_ref, w_ref, p_ref, u_ref, h_ref, buf, sems, wb):
        i = pl.program_id(0)
        slot = _fetch_window(x_hbm, buf, sems, i, n_steps, tm)

        @pl.when(i == 0)
        def _():
            buf[0, 0:PAD, :] = jnp.zeros((PAD, D_MODEL), F32)
            buf[0, PAD:HEAD, :] = meta_ref[...]
            wb[...] = w_ref[...].astype(BF16)

        h = buf[slot]
        h_ref[...] = h
        u = (h * _rms(h) * g_ref[...]).astype(BF16)
        u_ref[...] = u
        p_ref[...] = _dot(u, wb[...])

    return pl.pallas_call(
        body,
        grid_spec=pltpu.PrefetchScalarGridSpec(
            num_scalar_prefetch=1, grid=(n_steps,),
            in_specs=[pl.BlockSpec(memory_space=pl.ANY),
                      pl.BlockSpec((N_META, D_MODEL), lambda i, s: (0, 0)),
                      pl.BlockSpec((1, D_MODEL), lambda i, s: (0, 0)),
                      pl.BlockSpec((D_MODEL, cols), lambda i, s: (0, 0))],
            out_specs=[pl.BlockSpec((tm, cols), lambda i, s: (i, s[0])),
                       pl.BlockSpec((tm, D_MODEL), lambda i, s: (i, 0)),
                       pl.BlockSpec((tm, D_MODEL), lambda i, s: (i, 0))],
            scratch_shapes=[pltpu.VMEM((2, tm, D_MODEL), F32), pltpu.SemaphoreType.DMA((2,)),
                            pltpu.VMEM((D_MODEL, cols), BF16)]),
        out_shape=[jax.ShapeDtypeStruct((T, D_IN), F32), jax.ShapeDtypeStruct((T, D_MODEL), BF16),
                   jax.ShapeDtypeStruct((T, D_MODEL), F32)],
        name="in_proj_local", compiler_params=_params("arbitrary"),
    )(chip, x, meta, g1, w_own)


def _in_proj_rest(u, w_in, p, chip):
    T = u.shape[0]
    tm = _row_tile(T, 2080)
    cols = BIG["w_in"][1]
    block = lambda j, s: (s[0] + 1 + j) % N_CHIPS

    def body(s_ref, u_ref, w_ref, p_in_ref, p_ref):
        p_ref[...] = _dot(u_ref[...], w_ref[...])

    return pl.pallas_call(
        body,
        grid_spec=pltpu.PrefetchScalarGridSpec(
            num_scalar_prefetch=1, grid=(N_CHIPS - 1, T // tm),
            in_specs=[pl.BlockSpec((tm, D_MODEL), lambda j, i, s: (i, 0)),
                      pl.BlockSpec((D_MODEL, cols), lambda j, i, s: (0, block(j, s))), ANY],
            out_specs=pl.BlockSpec((tm, cols), lambda j, i, s: (i, block(j, s)))),
        out_shape=jax.ShapeDtypeStruct((T, D_IN), F32),
        input_output_aliases={3: 0},
        name="in_proj_rest", compiler_params=_params("arbitrary", "arbitrary"),
    )(chip, u, w_in, p)


def _scan_block_fwd(A, B, rowi):
    for d in (1, 2, 4):
        a_sh = pltpu.roll(A, d, axis=0)
        b_sh = pltpu.roll(B, d, axis=0)
        m = rowi >= d
        B = jnp.where(m, A * b_sh + B, B)
        A = jnp.where(m, A * a_sh, A)
    return A, B


def _scan_block_bwd(A, B, rowi):
    for d in (1, 2, 4):
        a_sh = pltpu.roll(A, 8 - d, axis=0)
        b_sh = pltpu.roll(B, 8 - d, axis=0)
        m = rowi < 8 - d
        B = jnp.where(m, A * b_sh + B, B)
        A = jnp.where(m, A * a_sh, A)
    return A, B


def _rg_gates(xc, w_ref, bg_ref, lam):
    pre = _dot(xc.astype(BF16), w_ref[...]) + bg_ref[...]
    r = _sigmoid(pre[:, :D_RG])
    ig = _sigmoid(pre[:, D_RG:])
    sp = _softplus_neg(lam)
    la = -LRU_C * sp * r
    a = jnp.exp(la)
    th = jnp.tanh(la)
    u = 1.0 - th
    rc = pl.reciprocal(u, approx=True)
    rc = rc * (2.0 - u * rc)
    rc = rc * (2.0 - u * rc)
    m2 = -2.0 * th * rc
    inv_m = lax.rsqrt(jnp.maximum(m2, 1e-30))
    return r, ig, sp, a, m2 * inv_m, inv_m


def _conv(ext, cw_ref, cb_ref, tm):
    xc = cb_ref[...] + cw_ref[0:1, :] * ext[8 - 3:8 - 3 + tm, :]
    for j in range(1, CONV_W):
        xc = xc + cw_ref[j:j + 1, :] * ext[8 - 3 + j:8 - 3 + j + tm, :]
    return xc


def _scan_unroll(blocks):
    return 4 if blocks % 4 == 0 else 2 if blocks % 2 == 0 else 1


def _rg_fwd(p, cw, cb, wg, bg, lam, rg_g):
    T = p.shape[0]
    tm = _row_tile(T, 832)
    unroll = _scan_unroll(tm // 8)

    def body(xg_ref, cw_ref, cb_ref, w_ref, bg_ref, lam_ref, g_ref, y_ref, h_ref, xc_ref, ext, a_s, b_s, carry):
        i = pl.program_id(0)

        @pl.when(i == 0)
        def _():
            ext[0:8, :] = jnp.zeros((8, D_RG), F32)
            carry[...] = jnp.zeros((1, D_RG), F32)

        ext[8:8 + tm, :] = xg_ref[:, :D_RG]
        xc = _conv(ext, cw_ref, cb_ref, tm)
        xc_ref[...] = xc
        r, ig, sp, a, m, _ = _rg_gates(xc, w_ref, bg_ref, lam_ref[...])
        row = i * tm + lax.broadcasted_iota(jnp.int32, (tm, 1), 0)
        a_s[...] = a
        b_s[...] = jnp.where(row >= PAD, m * ig * xc, 0.0)
        rowi = lax.broadcasted_iota(jnp.int32, (8, D_RG), 0)

        def blk(j, c):
            for u in range(unroll):
                o = pl.multiple_of((j * unroll + u) * 8, 8)
                A, B = _scan_block_fwd(a_s[pl.ds(o, 8), :], b_s[pl.ds(o, 8), :], rowi)
                h = B + A * c
                h_ref[pl.ds(o, 8), :] = h
                c = h[7:8, :]
            return c

        carry[...] = lax.fori_loop(0, tm // (8 * unroll), blk, carry[...])
        ext[0:8, :] = ext[tm:tm + 8, :]
        g, _ = _gelu_parts(xg_ref[:, D_RG:])
        yy = g * h_ref[...]
        y_ref[...] = (yy * _rms(yy) * g_ref[...]).astype(BF16)

    vec = lambda n: pl.BlockSpec((1, n), lambda i: (0, 0))
    return pl.pallas_call(
        body, grid=(T // tm,),
        in_specs=[pl.BlockSpec((tm, 2 * D_RG), lambda i: (i, 0)),
                  pl.BlockSpec((CONV_W, D_RG), lambda i: (0, 0)), vec(D_RG),
                  pl.BlockSpec((D_RG, 2 * D_RG), lambda i: (0, 0)), vec(2 * D_RG), vec(D_RG), vec(D_RG)],
        out_specs=[pl.BlockSpec((tm, D_RG), lambda i: (i, 0))] * 3,
        out_shape=[jax.ShapeDtypeStruct((T, D_RG), BF16), jax.ShapeDtypeStruct((T, D_RG), F32),
                   jax.ShapeDtypeStruct((T, D_RG), F32)],
        scratch_shapes=[pltpu.VMEM((tm + 8, D_RG), F32), pltpu.VMEM((tm, D_RG), F32),
                        pltpu.VMEM((tm, D_RG), F32), pltpu.VMEM((1, D_RG), F32)],
        name="rg_fwd", compiler_params=_params("arbitrary"),
    )(p, cw, cb, wg, bg, lam, rg_g)


def _running_sum(x, down):
    r = lax.broadcasted_iota(jnp.int32, (CHUNK, CHUNK), 0)
    c = lax.broadcasted_iota(jnp.int32, (CHUNK, CHUNK), 1)
    tri = ((c <= r) if down else (c >= r)).astype(BF16)
    hi = x.astype(BF16)
    rest = x - hi.astype(F32)
    mid = rest.astype(BF16)
    lo = (rest - mid.astype(F32)).astype(BF16)
    return (_dot(tri, hi) + _dot(tri, mid)) + _dot(tri, lo)


def _hg_gates(hq, hf, lbraw_ref, valid):
    lb = _sigmoid(lbraw_ref[0:1, :] - lbraw_ref[1:2, :])
    sq = _sigmoid(hq)
    q = hq * sq
    sf = _sigmoid(hf)
    f = lb + (1.0 - lb) * sf
    lf = jnp.where(valid, jnp.log(f), 0.0)
    b = _running_sum(lf, True)
    return lb, sq, q, sf, f, b


def _hg_head(qh, kh, bh):
    b_last = bh[CHUNK - 1:CHUNK, :]
    refs = [bh[SUB * s:SUB * s + 1, :] for s in range(N_SUB)]
    r_sel = jnp.concatenate([jnp.broadcast_to(refs[s], (SUB, HG_HEAD_DIM)) for s in range(N_SUB)], axis=0)
    eb = jnp.exp(bh)
    eq = jnp.exp(bh - r_sel)
    ekh = jnp.exp(b_last - bh)
    ek = [jnp.exp(jnp.minimum(refs[s] - bh[:SUB * (s + 1), :], EXP_CLAMP)) for s in range(N_SUB)]
    qe = qh * eq

    def own_rows(s):
        parts = [jnp.zeros((SUB * s, HG_HEAD_DIM), F32)] if s else []
        parts.append(qe[SUB * s:SUB * (s + 1), :])
        if s < N_SUB - 1:
            parts.append(jnp.zeros((CHUNK - SUB * (s + 1), HG_HEAD_DIM), F32))
        return jnp.concatenate(parts, axis=0)

    q_hat = jnp.concatenate([own_rows(s) for s in range(N_SUB)], axis=1)

    def met_rows(s):
        n = SUB * (s + 1)
        ke = kh[:n, :] * ek[s]
        return ke if n == CHUNK else jnp.concatenate([ke, jnp.zeros((CHUNK - n, HG_HEAD_DIM), F32)], axis=0)

    k_til = jnp.concatenate([met_rows(s) for s in range(N_SUB)], axis=1)
    return b_last, eb, eq, ekh, ek, q_hat, k_til


def _causal():
    r = lax.broadcasted_iota(jnp.int32, (CHUNK, CHUNK), 0)
    c = lax.broadcasted_iota(jnp.int32, (CHUNK, CHUNK), 1)
    return r >= c


def _chunks_per_step(n_chunks):
    for c in (5, 4, 3, 2):
        if n_chunks % c == 0:
            return c
    return 1


def _hg_fwd(p, lbraw, hg_g):
    T = p.shape[0]
    n_chunks = T // CHUNK
    cps = _chunks_per_step(n_chunks)
    rows = cps * CHUNK

    def body(hq_ref, hf_ref, hi_ref, hg_ref, lb_ref, g_ref, y_ref, o_ref, st_all_ref, st):
        i = pl.program_id(0)

        @pl.when(i == 0)
        def _():
            st[...] = jnp.zeros_like(st)

        def chunk(j, carry):
            rs = pl.ds(pl.multiple_of(j * CHUNK, CHUNK), CHUNK)
            chunk_body(i * cps + j, hq_ref.at[rs, :], hf_ref.at[rs, :], hi_ref.at[rs, :], hg_ref.at[rs, :], lb_ref,
                       g_ref, y_ref.at[rs, :], o_ref.at[rs, :], st_all_ref.at[pl.ds(j, 1)], st)
            return carry

        lax.fori_loop(0, cps, chunk, 0, unroll=True)

    def chunk_body(n, hq_ref, hf_ref, hi_ref, hg_ref, lb_ref, g_ref, y_ref, o_ref, st_all_ref, st):
        valid = (n * CHUNK + lax.broadcasted_iota(jnp.int32, (CHUNK, 1), 0)) >= PAD
        hq, hf, v, hg = hq_ref[...], hf_ref[...], hi_ref[...], hg_ref[...]
        lb, sq, q, sf, f, b = _hg_gates(hq, hf, lb_ref, valid)
        k = 1.0 - f
        st_all_ref[0] = st[...]
        causal = _causal()
        v_t = v.T.astype(BF16)
        heads = [slice(h * HG_HEAD_DIM, (h + 1) * HG_HEAD_DIM) for h in range(HG_HEADS)]
        fac = []
        for sl in heads:
            qh, kh, bh = q[:, sl], k[:, sl], b[:, sl]
            b_last, eb, _, ekh, _, q_hat, k_til = _hg_head(qh, kh, bh)
            fac.append((jnp.exp(b_last), (qh * eb).astype(BF16), q_hat.astype(BF16), k_til.astype(BF16),
                        (kh * ekh).astype(BF16), v[:, sl].astype(BF16)))
        raw = []
        for sl, (_, q_til, q_hat, k_til, k_hat, _) in zip(heads, fac):
            st_h = st[sl, :]
            raw.append((_dot_nt(q_til, st_h.astype(BF16)), _dot_nt(q_hat, k_til), _dot(v_t[sl, :], k_hat), st_h))
        for sl, (e_last, _, _, _, _, vb), (inter, att, upd, st_h) in zip(heads, fac, raw):
            o = inter + _dot(jnp.where(causal, att, 0.0).astype(BF16), vb)
            st[sl, :] = st_h * e_last + upd
            o_ref[:, sl] = o
            hgh = hg[:, sl]
            y_ref[:, sl] = (o * _rms(o) * g_ref[...] * (hgh * _sigmoid(hgh))).astype(BF16)

    col = lambda j: pl.BlockSpec((rows, D_HG), lambda n: (n, j))
    return pl.pallas_call(
        body, grid=(n_chunks // cps,),
        in_specs=[col(2), col(3), col(4), col(5),
                  pl.BlockSpec((2, D_HG), lambda n: (0, 0)), pl.BlockSpec((1, HG_HEAD_DIM), lambda n: (0, 0))],
        out_specs=[pl.BlockSpec((rows, D_HG), lambda n: (n, 0)), pl.BlockSpec((rows, D_HG), lambda n: (n, 0)),
                   pl.BlockSpec((cps, D_HG, HG_HEAD_DIM), lambda n: (n, 0, 0))],
        out_shape=[jax.ShapeDtypeStruct((T, D_HG), BF16), jax.ShapeDtypeStruct((T, D_HG), F32),
                   jax.ShapeDtypeStruct((n_chunks, D_HG, HG_HEAD_DIM), F32)],
        scratch_shapes=[pltpu.VMEM((D_HG, HG_HEAD_DIM), F32)],
        name="hg_fwd", compiler_params=_params("arbitrary"),
    )(p, p, p, p, lbraw, hg_g)


def _ffn_fwd(h0, y_rg, y_hg, w_out, g2, w_gu, w_down, gf, target):
    T = h0.shape[0]
    tm = _row_tile(T, 320)
    n_steps = T // tm

    def body(h_ref, yr_ref, yh_ref, wo_ref, g2_ref, wgu_ref, wd_ref, gf_ref, t_hbm,
             h1_ref, v_ref, y_ref, gu_ref, act_ref, dh2_ref, dh2b_ref, loss_ref, gg_ref, tbuf, sems):
        i = pl.program_id(0)
        slot = _fetch_window(t_hbm, tbuf, sems, i, n_steps, tm)

        @pl.when(i == 0)
        def _():
            loss_ref[...] = jnp.zeros_like(loss_ref)
            gg_ref[...] = jnp.zeros_like(gg_ref)
            tbuf[0, 0:HEAD, :] = jnp.zeros((HEAD, D_MODEL), F32)

        y_ref[:, :D_RG] = yr_ref[...]
        y_ref[:, D_RG:] = yh_ref[...]
        h1 = h_ref[...] + _dot(y_ref[...], wo_ref[...])
        h1_ref[...] = h1
        v = (h1 * _rms(h1) * g2_ref[...]).astype(BF16)
        v_ref[...] = v

        gu = _dot(v, wgu_ref[...])
        gu_ref[...] = gu.astype(BF16)
        g = gu[:, :D_FF]
        act = (g * _sigmoid(g) * gu[:, D_FF:]).astype(BF16)
        act_ref[...] = act

        h2 = h1 + _dot(act, wd_ref[...])
        r = _rms(h2)
        n = h2 * r
        gf_ = gf_ref[...]
        row = i * tm + lax.broadcasted_iota(jnp.int32, (tm, 1), 0)
        err = jnp.where(row >= HEAD, n * gf_ - tbuf[slot], 0.0)
        loss_ref[...] += 0.5 * jnp.sum(jnp.mean(err * err, axis=-1, keepdims=True), axis=0, keepdims=True)
        dy = err * (1.0 / D_MODEL)
        gg_ref[...] += jnp.sum(dy * n, axis=0, keepdims=True)
        dh2 = _rms_bwd(dy * gf_, n, r)
        dh2_ref[...] = dh2
        dh2b_ref[...] = dh2.astype(BF16)

    row_spec = lambda n: pl.BlockSpec((tm, n), lambda i: (i, 0))
    vec = pl.BlockSpec((1, D_MODEL), lambda i: (0, 0))
    return pl.pallas_call(
        body, grid=(n_steps,),
        in_specs=[row_spec(D_MODEL), row_spec(D_RG), row_spec(D_HG), _resident((D_MODEL, D_MODEL)), vec,
                  _resident((D_MODEL, 2 * D_FF)), _resident((D_FF, D_MODEL)), vec,
                  pl.BlockSpec(memory_space=pl.ANY)],
        out_specs=[row_spec(D_MODEL), row_spec(D_MODEL), row_spec(D_MODEL), row_spec(2 * D_FF), row_spec(D_FF),
                   row_spec(D_MODEL), row_spec(D_MODEL), pl.BlockSpec((1, 1), lambda i: (0, 0)), vec],
        out_shape=[jax.ShapeDtypeStruct((T, D_MODEL), F32), jax.ShapeDtypeStruct((T, D_MODEL), BF16),
                   jax.ShapeDtypeStruct((T, D_MODEL), BF16), jax.ShapeDtypeStruct((T, 2 * D_FF), BF16),
                   jax.ShapeDtypeStruct((T, D_FF), BF16), jax.ShapeDtypeStruct((T, D_MODEL), F32),
                   jax.ShapeDtypeStruct((T, D_MODEL), BF16), jax.ShapeDtypeStruct((1, 1), F32),
                   jax.ShapeDtypeStruct((1, D_MODEL), F32)],
        scratch_shapes=[pltpu.VMEM((2, tm, D_MODEL), F32), pltpu.SemaphoreType.DMA((2,))],
        name="ffn_fwd", compiler_params=_params("arbitrary"),
    )(h0, y_rg, y_hg, w_out, g2, w_gu, w_down, gf, target)


def _resident(shape):
    return pl.BlockSpec(shape, lambda i: (0,) * len(shape), pipeline_mode=pl.Buffered(1))


def _ffn_bwd(dh2b, gu, w_down, w_gu, h1, g2, dh2, w_out):
    T = h1.shape[0]
    tm = _row_tile(T, 320)

    def body(d_ref, gu_ref, wd_ref, wgu_ref, h_ref, g_ref, d2_ref, wo_ref, dgu_ref, dh1_ref, dh1b_ref, dy_ref, gg_ref):
        i = pl.program_id(0)

        @pl.when(i == 0)
        def _():
            gg_ref[...] = jnp.zeros_like(gg_ref)

        dact = _dot_nt(d_ref[...], wd_ref[...]).astype(BF16)
        g = gu_ref[:, :D_FF]
        u = gu_ref[:, D_FF:]
        s = _sigmoid(g)
        dgu_ref[:, :D_FF] = dact * u * (s * (1.0 + g * (1.0 - s)))
        dgu_ref[:, D_FF:] = dact * (g * s)

        dv = _dot_nt(dgu_ref[...], wgu_ref[...])
        h1_ = h_ref[...]
        r = _rms(h1_)
        n = h1_ * r
        gg_ref[...] += jnp.sum(dv * n, axis=0, keepdims=True)
        dh1 = d2_ref[...] + _rms_bwd(dv * g_ref[...], n, r)
        dh1_ref[...] = dh1
        db = dh1.astype(BF16)
        dh1b_ref[...] = db
        dy_ref[...] = _dot_nt(db, wo_ref[...])

    row = lambda n: pl.BlockSpec((tm, n), lambda i: (i, 0))
    return pl.pallas_call(
        body, grid=(T // tm,),
        in_specs=[row(D_MODEL), row(2 * D_FF), _resident((D_FF, D_MODEL)), _resident((D_MODEL, 2 * D_FF)),
                  row(D_MODEL), pl.BlockSpec((1, D_MODEL), lambda i: (0, 0)), row(D_MODEL),
                  _resident((D_MODEL, D_MODEL))],
        out_specs=[row(2 * D_FF), row(D_MODEL), row(D_MODEL), row(D_MODEL),
                   pl.BlockSpec((1, D_MODEL), lambda i: (0, 0))],
        out_shape=[jax.ShapeDtypeStruct((T, 2 * D_FF), BF16), jax.ShapeDtypeStruct((T, D_MODEL), F32),
                   jax.ShapeDtypeStruct((T, D_MODEL), BF16), jax.ShapeDtypeStruct((T, D_MODEL), F32),
                   jax.ShapeDtypeStruct((1, D_MODEL), F32)],
        name="ffn_bwd", compiler_params=_params("arbitrary"),
    )(dh2b, gu, w_down, w_gu, h1, g2, dh2, w_out)


def _rg_bwd(p, xc_all, hs, dy, dp, cw, cb, wg, bg, lam, rg_g):
    T = p.shape[0]
    tm = _row_tile(T, 832)
    nt = T // tm
    hb = tm // 8
    unroll = _scan_unroll(hb)

    def body(xg_ref, xc_ref, h_ref, hh_ref, dy_ref, dp_in_ref, cw_ref, cb_ref, w_ref, bg_ref, lam_ref, g_ref,
             dp_ref, gcw_ref, gcb_ref, gw_ref, gbg_ref, glam_ref, gg_ref,
             dext, a_s, b_s, d_s, gacc, carry_d, carry_a):
        i = pl.program_id(0)
        t_idx = nt - 1 - i

        @pl.when(i == 0)
        def _():
            dext[tm:tm + 8, :] = jnp.zeros((8, D_RG), F32)
            carry_d[...] = jnp.zeros_like(carry_d)
            carry_a[...] = jnp.zeros_like(carry_a)
            gacc[...] = jnp.zeros_like(gacc)
            for ref in (gcw_ref, gcb_ref, gbg_ref, glam_ref, gg_ref, gw_ref):
                ref[...] = jnp.zeros_like(ref)

        first = t_idx == 0
        xc = xc_ref[...]
        lam_ = lam_ref[...]
        r, ig, sp, a, m, inv_m = _rg_gates(xc, w_ref, bg_ref, lam_)
        row = t_idx * tm + lax.broadcasted_iota(jnp.int32, (tm, 1), 0)
        valid = row >= PAD

        gr = xg_ref[:, D_RG:]
        g, dgelu = _gelu_parts(gr)
        h = h_ref[...]
        yy = g * h
        rr = _rms(yy)
        nn = yy * rr
        dy_ = dy_ref[...]
        gg_ref[...] += jnp.sum(dy_ * nn, axis=0, keepdims=True)
        dyy = _rms_bwd(dy_ * g_ref[...], nn, rr)
        dp_ref[:, D_RG:] = (dyy * h * dgelu).astype(BF16)

        a_s[...] = a
        b_s[...] = dyy * g
        rowi = lax.broadcasted_iota(jnp.int32, (8, D_RG), 0)

        def blk(jj, c):
            cd, ca = c
            for u in range(unroll):
                o = pl.multiple_of((hb - 1 - (jj * unroll + u)) * 8, 8)
                a_blk = a_s[pl.ds(o, 8), :]
                a_next = jnp.where(rowi == 7, ca, pltpu.roll(a_blk, 7, axis=0))
                A, B = _scan_block_bwd(a_next, b_s[pl.ds(o, 8), :], rowi)
                d = B + A * cd
                d_s[pl.ds(o, 8), :] = d
                cd, ca = d[0:1, :], a_blk[0:1, :]
            return cd, ca

        cd, ca = lax.fori_loop(0, hb // unroll, blk, (carry_d[...], carry_a[...]))
        carry_d[...] = cd
        carry_a[...] = ca
        delta = d_s[...]

        h_last_prev = jnp.where(first, 0.0, hh_ref[7:8, :])
        row0 = lax.broadcasted_iota(jnp.int32, (tm, 1), 0) == 0
        h_prev = jnp.where(row0, h_last_prev, pltpu.roll(h, 1, axis=0))
        dbx = jnp.where(valid, delta, 0.0)
        da = delta * h_prev
        di = dbx * m * xc
        dm = dbx * ig * xc
        dla = a * (da - dm * a * inv_m)
        dla = jnp.where(valid, dla, 0.0)
        glam_ref[...] += jnp.sum(dla * r, axis=0, keepdims=True) * (LRU_C / (1.0 + jnp.exp(lam_)))
        dr = (-LRU_C) * sp * dla
        dpre = jnp.concatenate([dr * r * (1.0 - r), di * ig * (1.0 - ig)], axis=1)
        gbg_ref[...] += jnp.sum(dpre, axis=0, keepdims=True)
        dpre_b = dpre.astype(BF16)
        gacc[...] += _dot_tn(xc.astype(BF16), dpre_b)
        dxc = dbx * m * ig + _dot_nt(dpre_b, w_ref[...])
        gcb_ref[...] += jnp.sum(dxc, axis=0, keepdims=True)
        dext[0:tm, :] = dxc
        xr = xg_ref[:, :D_RG]
        dxr = None
        for j in range(CONV_W):
            shifted = dext[3 - j:3 - j + tm, :]
            gcw_ref[j:j + 1, :] += jnp.sum(xr * shifted, axis=0, keepdims=True)
            tap = cw_ref[j:j + 1, :] * shifted
            dxr = tap if dxr is None else dxr + tap
        dp_ref[:, :D_RG] = dxr.astype(BF16)
        dext[tm:tm + 8, :] = dext[0:8, :]

        @pl.when(i == nt - 1)
        def _():
            fold = _head_fold()
            mask = _head_mask()
            fold_b = fold.astype(BF16)
            for k in range(2):
                blockdiag = jnp.where(mask, gacc[:, k * D_RG:(k + 1) * D_RG], 0.0)
                hi = blockdiag.astype(BF16)
                rest = blockdiag - hi.astype(F32)
                mid = rest.astype(BF16)
                lo = (rest - mid.astype(F32)).astype(BF16)
                gw_ref[k * D_RG:(k + 1) * D_RG, :] = (_dot(hi, fold_b) + _dot(mid, fold_b)) + _dot(lo, fold_b)

    vec = lambda n: pl.BlockSpec((1, n), lambda i: (0, 0))
    rev = lambda n: pl.BlockSpec((tm, n), lambda i: (nt - 1 - i, 0))
    halo = lambda n: pl.BlockSpec((8, n), lambda i: (jnp.maximum((nt - 1 - i) * hb - 1, 0), 0))
    return pl.pallas_call(
        body, grid=(nt,),
        in_specs=[rev(2 * D_RG), rev(D_RG), rev(D_RG), halo(D_RG), rev(D_RG), ANY,
                  pl.BlockSpec((CONV_W, D_RG), lambda i: (0, 0)), vec(D_RG),
                  pl.BlockSpec((D_RG, 2 * D_RG), lambda i: (0, 0)), vec(2 * D_RG), vec(D_RG), vec(D_RG)],
        out_specs=[rev(2 * D_RG), pl.BlockSpec((CONV_W, D_RG), lambda i: (0, 0)), vec(D_RG),
                   pl.BlockSpec((2 * D_RG, RG_HEAD_DIM), lambda i: (0, 0)), vec(2 * D_RG), vec(D_RG), vec(D_RG)],
        input_output_aliases={5: 0},
        out_shape=[jax.ShapeDtypeStruct((T, D_IN), BF16), jax.ShapeDtypeStruct((CONV_W, D_RG), F32),
                   jax.ShapeDtypeStruct((1, D_RG), F32), jax.ShapeDtypeStruct((2 * D_RG, RG_HEAD_DIM), F32),
                   jax.ShapeDtypeStruct((1, 2 * D_RG), F32), jax.ShapeDtypeStruct((1, D_RG), F32),
                   jax.ShapeDtypeStruct((1, D_RG), F32)],
        scratch_shapes=[pltpu.VMEM((tm + 8, D_RG), F32),
                        pltpu.VMEM((tm, D_RG), F32), pltpu.VMEM((tm, D_RG), F32), pltpu.VMEM((tm, D_RG), F32),
                        pltpu.VMEM((D_RG, 2 * D_RG), F32), pltpu.VMEM((1, D_RG), F32), pltpu.VMEM((1, D_RG), F32)],
        name="rg_bwd", compiler_params=_params("arbitrary"),
    )(p, xc_all, hs, hs, dy, dp, cw, cb, wg, bg, lam, rg_g)


def _hg_bwd(p, o_all, st_all, dy, lbraw, hg_g):
    T = p.shape[0]
    n_chunks = T // CHUNK
    cps = _chunks_per_step(n_chunks)
    rows = cps * CHUNK
    n_steps = n_chunks // cps

    def body(hq_ref, hf_ref, hi_ref, hg_ref, o_ref, st_ref, dy_ref, lb_ref, g_ref,
             dp_ref, glb_ref, gg_ref, dst):
        i = pl.program_id(0)

        @pl.when(i == 0)
        def _():
            dst[...] = jnp.zeros_like(dst)
            glb_ref[...] = jnp.zeros_like(glb_ref)
            gg_ref[...] = jnp.zeros_like(gg_ref)

        dp_ref[:, :2 * D_RG] = jnp.zeros((rows, 2 * D_RG), BF16)

        def chunk(jj, carry):
            j = cps - 1 - jj
            rs = pl.ds(pl.multiple_of(j * CHUNK, CHUNK), CHUNK)
            chunk_body((n_steps - 1 - i) * cps + j, hq_ref.at[rs, :], hf_ref.at[rs, :], hi_ref.at[rs, :],
                       hg_ref.at[rs, :], o_ref.at[rs, :], st_ref.at[pl.ds(j, 1)], dy_ref.at[rs, :], lb_ref, g_ref,
                       dp_ref.at[rs, pl.ds(2 * D_RG, 4 * D_HG)], glb_ref, gg_ref, dst)
            return carry

        lax.fori_loop(0, cps, chunk, 0, unroll=True)

    def chunk_body(n, hq_ref, hf_ref, hi_ref, hg_ref, o_ref, st_ref, dy_ref, lb_ref, g_ref,
                   dp_ref, glb_ref, gg_ref, dst):
        valid = (n * CHUNK + lax.broadcasted_iota(jnp.int32, (CHUNK, 1), 0)) >= PAD
        hq, hf, v, hg = hq_ref[...], hf_ref[...], hi_ref[...], hg_ref[...]
        lb, sq, q, sf, f, b = _hg_gates(hq, hf, lb_ref, valid)
        k = 1.0 - f
        causal = _causal()
        r_i = lax.broadcasted_iota(jnp.int32, (CHUNK, CHUNK), 0)
        c_i = lax.broadcasted_iota(jnp.int32, (CHUNK, CHUNK), 1)
        causal_t = r_i <= c_i
        is_last = lax.broadcasted_iota(jnp.int32, (CHUNK, 1), 0) == CHUNK - 1
        g_ = g_ref[...]
        db_parts, dq_parts, dk_parts = [], [], []
        gg = jnp.zeros((1, HG_HEAD_DIM), F32)
        heads = [slice(h * HG_HEAD_DIM, (h + 1) * HG_HEAD_DIM) for h in range(HG_HEADS)]

        do_parts = []
        for h, sl in enumerate(heads):
            o = o_ref[:, sl]
            ro = _rms(o)
            no = o * ro
            hgh = hg[:, sl]
            sg = _sigmoid(hgh)
            dyh = dy_ref[:, sl]
            dp_ref[:, 3 * D_HG + h * HG_HEAD_DIM:3 * D_HG + (h + 1) * HG_HEAD_DIM] = (
                dyh * no * g_ * sg * (1.0 + hgh * (1.0 - sg))).astype(BF16)
            dng = dyh * hgh * sg
            gg = gg + jnp.sum(dng * no, axis=0, keepdims=True)
            do_parts.append(_rms_bwd(dng * g_, no, ro))
        do_t = jnp.concatenate(do_parts, axis=1).T.astype(BF16)

        fac = []
        for sl, do in zip(heads, do_parts):
            qh, kh, bh = q[:, sl], k[:, sl], b[:, sl]
            b_last, eb, eq, ekh, ek, q_hat, k_til = _hg_head(qh, kh, bh)
            fac.append(dict(qh=qh, kh=kh, e_last=jnp.exp(b_last), eb=eb, eq=eq, ekh=ekh, ek=ek,
                            q_til=qh * eb, k_hat=kh * ekh, qhb=q_hat.astype(BF16), ktb=k_til.astype(BF16),
                            vb=v[:, sl].astype(BF16), dob=do.astype(BF16)))

        first = []
        for sl, t in zip(heads, fac):
            st_h = st_ref[0, sl, :]
            dst_h = dst[sl, :]
            dstb = dst_h.astype(BF16)
            first.append(dict(
                att_t=_dot_nt(t["ktb"], t["qhb"]), datt=_dot_nt(t["dob"], t["vb"]),
                datt_t=_dot_nt(t["vb"], t["dob"]), dk_hat=_dot(t["vb"], dstb),
                dv=_dot_nt(t["k_hat"].astype(BF16), dstb), dq_til=_dot(t["dob"], st_h.astype(BF16)),
                state=t["e_last"] * jnp.sum(dst_h * st_h, axis=0, keepdims=True)))
            dst[sl, :] = dst_h * t["e_last"] + _dot(do_t[sl, :], t["q_til"].astype(BF16))

        for h, (t, m) in enumerate(zip(fac, first)):
            qh, kh, eb, eq, ekh, ek = t["qh"], t["kh"], t["eb"], t["eq"], t["ekh"], t["ek"]
            q_til, k_hat, qhb, ktb, dob = t["q_til"], t["k_hat"], t["qhb"], t["ktb"], t["dob"]
            dk_hat, dq_til = m["dk_hat"], m["dq_til"]
            dv = m["dv"] + _dot(jnp.where(causal_t, m["att_t"], 0.0).astype(BF16), dob)
            dq_hat = _dot(jnp.where(causal, m["datt"], 0.0).astype(BF16), ktb)
            dk_til = _dot(jnp.where(causal_t, m["datt_t"], 0.0).astype(BF16), qhb)
            db_last = jnp.sum(dk_hat * k_hat, axis=0, keepdims=True) + m["state"]
            dq_sel = jnp.concatenate([dq_hat[SUB * s:SUB * (s + 1), s * HG_HEAD_DIM:(s + 1) * HG_HEAD_DIM]
                                      for s in range(N_SUB)], axis=0)
            dq_a = dq_sel * eq
            dk_rows, k_att_rows = [], []
            for b_ in range(N_SUB):
                rs = slice(SUB * b_, SUB * (b_ + 1))
                dk_sum = k_att_sum = None
                for s in range(b_, N_SUB):
                    cs = slice(s * HG_HEAD_DIM, (s + 1) * HG_HEAD_DIM)
                    d = dk_til[rs, cs]
                    t_dk = d * ek[s][rs, :]
                    t_att = ktb[rs, cs].astype(F32) * d
                    dk_sum = t_dk if dk_sum is None else dk_sum + t_dk
                    k_att_sum = t_att if k_att_sum is None else k_att_sum + t_att
                dk_rows.append(dk_sum)
                k_att_rows.append(k_att_sum)
            dk_a = jnp.concatenate(dk_rows, axis=0)
            db = (dq_til * q_til - dk_hat * k_hat + (qh * eq).astype(BF16).astype(F32) * dq_sel
                  - jnp.concatenate(k_att_rows, axis=0))
            db_parts.append(jnp.where(is_last, db + db_last, db))
            dq_parts.append(dq_til * eb + dq_a)
            dk_parts.append(dk_hat * ekh + dk_a)
            dp_ref[:, 2 * D_HG + h * HG_HEAD_DIM:2 * D_HG + (h + 1) * HG_HEAD_DIM] = dv.astype(BF16)

        gg_ref[...] += gg
        db = jnp.concatenate(db_parts, axis=1)
        dq = jnp.concatenate(dq_parts, axis=1)
        dk = jnp.concatenate(dk_parts, axis=1)
        dlf = jnp.where(valid, _running_sum(db, False), 0.0)
        dp_ref[:, :D_HG] = (dq * sq * (1.0 + hq * (1.0 - sq))).astype(BF16)
        df = dlf / f - dk
        dlb = jnp.sum(df * (1.0 - sf), axis=0, keepdims=True) * lb * (1.0 - lb)
        glb_ref[0:1, :] += dlb
        glb_ref[1:2, :] += -dlb
        dp_ref[:, D_HG:2 * D_HG] = (df * (1.0 - lb) * sf * (1.0 - sf)).astype(BF16)

    rev = lambda j: pl.BlockSpec((rows, D_HG), lambda i: (n_steps - 1 - i, j))
    return pl.pallas_call(
        body, grid=(n_steps,),
        in_specs=[rev(2), rev(3), rev(4), rev(5), rev(0),
                  pl.BlockSpec((cps, D_HG, HG_HEAD_DIM), lambda i: (n_steps - 1 - i, 0, 0)), rev(1),
                  pl.BlockSpec((2, D_HG), lambda i: (0, 0)), pl.BlockSpec((1, HG_HEAD_DIM), lambda i: (0, 0))],
        out_specs=[pl.BlockSpec((rows, D_IN), lambda i: (n_steps - 1 - i, 0)),
                   pl.BlockSpec((2, D_HG), lambda i: (0, 0)), pl.BlockSpec((1, HG_HEAD_DIM), lambda i: (0, 0))],
        out_shape=[jax.ShapeDtypeStruct((T, D_IN), BF16), jax.ShapeDtypeStruct((2, D_HG), F32),
                   jax.ShapeDtypeStruct((1, HG_HEAD_DIM), F32)],
        scratch_shapes=[pltpu.VMEM((D_HG, HG_HEAD_DIM), F32)],
        name="hg_bwd", compiler_params=_params("arbitrary"),
    )(p, p, p, p, o_all, st_all, dy, lbraw, hg_g)


def _in_bwd(dp, w_in, h0, g1, dh1):
    T = h0.shape[0]
    tm = _row_tile(T, 832)
    n_steps = T // tm

    def body(dp_ref, w_ref, h_ref, g_ref, d1_ref, gx_hbm, gmeta_ref, gg_ref, buf, sems):
        i = pl.program_id(0)
        first, later = _window_copies(gx_hbm, buf, sems, tm)
        slot = i % 2

        @pl.when(i == 0)
        def _():
            gg_ref[...] = jnp.zeros_like(gg_ref)

        if n_steps > 2:
            @pl.when(i == 2)
            def _():
                first(False).wait()

            @pl.when(i > 2)
            def _():
                later(i - 2, slot, False).wait()

        du = _dot_nt(dp_ref[...], w_ref[...])
        h0_ = h_ref[...]
        r = _rms(h0_)
        n = h0_ * r
        gg_ref[...] += jnp.sum(du * n, axis=0, keepdims=True)
        dh0 = d1_ref[...] + _rms_bwd(du * g_ref[...], n, r)
        buf[slot] = dh0

        @pl.when(i == 0)
        def _():
            gmeta_ref[...] = dh0[PAD:HEAD, :]
            first(False).start()

        if n_steps > 1:
            @pl.when(i > 0)
            def _():
                later(i, slot, False).start()

        @pl.when(i == n_steps - 1)
        def _():
            if n_steps == 1:
                first(False).wait()
            else:
                if n_steps == 2:
                    first(False).wait()
                else:
                    later(i - 1, 1 - slot, False).wait()
                later(i, slot, False).wait()

    row = lambda n: pl.BlockSpec((tm, n), lambda i: (i, 0))
    return pl.pallas_call(
        body, grid=(n_steps,),
        in_specs=[row(D_IN), _resident((D_MODEL, D_IN)),
                  row(D_MODEL), pl.BlockSpec((1, D_MODEL), lambda i: (0, 0)), row(D_MODEL)],
        out_specs=[pl.BlockSpec(memory_space=pl.ANY), pl.BlockSpec((N_META, D_MODEL), lambda i: (0, 0)),
                   pl.BlockSpec((1, D_MODEL), lambda i: (0, 0))],
        out_shape=[jax.ShapeDtypeStruct((T - HEAD, D_MODEL), F32), jax.ShapeDtypeStruct((N_META, D_MODEL), F32),
                   jax.ShapeDtypeStruct((1, D_MODEL), F32)],
        scratch_shapes=[pltpu.VMEM((2, tm, D_MODEL), F32), pltpu.SemaphoreType.DMA((2,))],
        name="in_bwd", compiler_params=_params("arbitrary"),
    )(dp, w_in, h0, g1, dh1)


def _col_tile(cols, target):
    best = None
    for t in range(128, min(cols, target) + 1, 128):
        if cols % t == 0:
            best = t
    assert best is not None, cols
    return best


MXU_DIM = 256


def _mxu_tile(cols, target):
    best = None
    for t in range(MXU_DIM, min(cols, target) + 1, MXU_DIM):
        if cols % t == 0:
            best = t
    assert best is not None, cols
    return best


def _weight_grad(a, b, name):
    T, M = a.shape
    N = b.shape[1]
    tm = _col_tile(M, 1408)
    tn = _mxu_tile(N, 768 if tm <= 1024 else 512)

    def body(a_ref, b_ref, o_ref, ob_ref):
        o = _dot_tn(a_ref[...], b_ref[...])
        o_ref[...] = o
        ob_ref[...] = o.astype(BF16)

    return pl.pallas_call(
        body, grid=(M // tm, N // tn),
        in_specs=[pl.BlockSpec((T, tm), lambda m, n: (0, m)), pl.BlockSpec((T, tn), lambda m, n: (0, n))],
        out_specs=[pl.BlockSpec((tm, tn), lambda m, n: (m, n))] * 2,
        out_shape=[jax.ShapeDtypeStruct((M, N), F32), jax.ShapeDtypeStruct((M, N), BF16)],
        name=name, compiler_params=_params("parallel", "parallel"),
    )(a, b)


def _weight_grad_chip_sum(a, b, name):
    T, M = a.shape
    N = b.shape[1]
    tn = _mxu_tile(N, 768)
    steps = N // tn
    half = M // 2

    def body(a_ref, b_ref, sum_ref, sumb_ref, acc, own, got, stage, send_sems, recv_sems):
        n = pl.program_id(0)
        x, y, c = _place()
        slot = n % 2

        def piece(k, s):
            return _remote(stage.at[s], got.at[k], send_sems, recv_sems, k, (x, y, 1 - c))

        @pl.when(n < steps)
        def _():
            acc[...] = _dot_tn(a_ref[...], b_ref[...])

            @pl.when(n >= 2)
            def _():
                piece(n - 2, slot).wait_send()

            stage[slot] = acc[pl.ds(pl.multiple_of((1 - c) * half, 128), half), :].astype(BF16)
            piece(n, slot).start()

        @pl.when(n >= 1)
        def _():
            piece(n - 1, 1 - slot).wait_recv()
            t = own[1 - slot] + got[n - 1].astype(F32)
            sum_ref[...] = t
            sumb_ref[...] = t.astype(BF16)

        @pl.when(n < steps)
        def _():
            own[slot] = acc[pl.ds(pl.multiple_of(c * half, 128), half), :]

        @pl.when(n == steps)
        def _():
            for k in range(max(steps - 2, 0), steps):
                piece(k, k % 2).wait_send()

    last = steps - 1
    return pl.pallas_call(
        body, grid=(steps + 1,),
        in_specs=[_resident((T, M)), pl.BlockSpec((T, tn), lambda n: (0, jnp.minimum(n, last)))],
        out_specs=[pl.BlockSpec((half, tn), lambda n: (0, jnp.maximum(n - 1, 0)))] * 2,
        out_shape=[jax.ShapeDtypeStruct((half, N), F32), jax.ShapeDtypeStruct((half, N), BF16)],
        scratch_shapes=[pltpu.VMEM((M, tn), F32), pltpu.VMEM((2, half, tn), F32), pltpu.VMEM((steps, half, tn), BF16),
                        pltpu.VMEM((2, half, tn), BF16), pltpu.SemaphoreType.DMA((steps,)),
                        pltpu.SemaphoreType.DMA((steps,))],
        name=name, compiler_params=_params("arbitrary"),
    )(a, b)


def _local_step(x, meta, target, w_in_own, w_in, w_out, w_gu, w_down, small, chip, on_ffn_grads=None,
                on_mixer_grads=None):
    wg = _gate_weights(small["w_rgate"], small["w_igate"])
    bg = jnp.concatenate([small["b_rgate"], small["b_igate"]], axis=1)

    p, u, h0 = _in_proj_local(x, meta, small["mix_norm_g"], w_in_own, chip)
    p = _in_proj_rest(u, w_in, p, chip)
    y_rg, hs, xc = _rg_fwd(p, small["conv_w"], small["conv_b"], wg, bg, small["lru_lambda"], small["rg_norm_g"])
    y_hg, o_all, st_all = _hg_fwd(p, small["hg_lower_bound"], small["hg_norm_g"])
    h1, v, yb, gu, act, dh2, dh2b, loss, g_final = _ffn_fwd(
        h0, y_rg, y_hg, w_out, small["ffn_norm_g"], w_gu, w_down, small["final_norm_g"], target)

    g_w_down = _weight_grad(act, dh2b, "grad_w_down")
    dgu, dh1, dh1b, dy, g_ffn = _ffn_bwd(dh2b, gu, w_down, w_gu, h1, small["ffn_norm_g"], dh2, w_out)
    ffn_grads = {"w_gate_up": _weight_grad(v, dgu, "grad_w_gate_up"), "w_down": g_w_down,
                 "w_out": _weight_grad(yb, dh1b, "grad_w_out")}
    stages = on_ffn_grads(ffn_grads) if on_ffn_grads is not None else None
    dp, g_lb, g_hgn = _hg_bwd(p, o_all, st_all, dy, small["hg_lower_bound"], small["hg_norm_g"])
    early = late = None
    if stages is not None:
        chip_sums, send = stages
        sums = chip_sums()
        (dp, dy), sums = lax.optimization_barrier(((dp, dy), sums))
        early = send(sums)
    dp, g_cw, g_cb, g_wgate, g_bg, g_lam, g_rgn = _rg_bwd(
        p, xc, hs, dy, dp, small["conv_w"], small["conv_b"], wg, bg, small["lru_lambda"], small["rg_norm_g"])
    if on_mixer_grads is None:
        g_w_in = _weight_grad(u, dp, "grad_w_in")[0]
    else:
        sums = {"w_in": _weight_grad_chip_sum(u, dp, "grad_w_in")}
        (dp, dh1), sums = lax.optimization_barrier(((dp, dh1), sums))
        late = on_mixer_grads(sums)
        g_w_in = None
    grad_x, g_meta, g_mix = _in_bwd(dp, w_in, h0, small["mix_norm_g"], dh1)

    grads = {
        "w_in": g_w_in, "w_out": ffn_grads["w_out"][0],
        "w_gate_up": ffn_grads["w_gate_up"][0], "w_down": ffn_grads["w_down"][0],
        "meta_tokens": g_meta, "mix_norm_g": g_mix, "conv_w": g_cw, "conv_b": g_cb, "w_gates": g_wgate,
        "b_rgate": g_bg[:, :D_RG], "b_igate": g_bg[:, D_RG:], "lru_lambda": g_lam, "rg_norm_g": g_rgn,
        "hg_lower_bound": g_lb, "hg_norm_g": g_hgn, "ffn_norm_g": g_ffn, "final_norm_g": g_final,
    }
    return loss, grad_x, grads, early, late


ANY = pl.BlockSpec(memory_space=pl.ANY)
HALF = D_MODEL // 2

BIG = {"w_in": (D_MODEL, D_IN // N_CHIPS, True), "w_gate_up": (D_MODEL, 2 * D_FF // N_CHIPS, True),
       "w_out": (D_MODEL // N_CHIPS, D_MODEL, False), "w_down": (D_FF // N_CHIPS, D_MODEL, False)}
BIG_NAMES = tuple(BIG)
N_BIG = len(BIG_NAMES)


def _full_shape(name):
    rows, cols, by_col = BIG[name]
    return (rows, cols * N_CHIPS) if by_col else (rows * N_CHIPS, cols)


def _place():
    return lax.axis_index("x"), lax.axis_index("y"), lax.axis_index("c")


def _chip_of(x, y, r):
    fx, fy = (r + 1) >> 1, (r + 1) & 1
    return (1 - x if fx else x), (1 - y if fy else y)


def _half_of(ref, by_col, half):
    start = pl.multiple_of(half * HALF, 128)
    return ref.at[pl.ds(start, HALF), :] if by_col else ref.at[:, pl.ds(start, HALF)]


def _shard_of(ref, name, chip):
    rows, cols, by_col = BIG[name]
    if by_col:
        return ref.at[:, pl.ds(pl.multiple_of(chip * cols, 128), cols)]
    return ref.at[pl.ds(pl.multiple_of(chip * rows, 16), rows), :]


def _shard_half_of(ref, name, chip, half):
    rows, cols, by_col = BIG[name]
    start = pl.multiple_of(half * HALF, 128)
    if by_col:
        return ref.at[pl.ds(start, HALF), pl.ds(pl.multiple_of(chip * cols, 128), cols)]
    return ref.at[pl.ds(pl.multiple_of(chip * rows, 16), rows), pl.ds(start, HALF)]


def _shard_half_part_of(ref, name, chip, half, part):
    rows, cols, by_col = BIG[name]
    start = pl.multiple_of(half * HALF + part * (HALF // 2), 128)
    if by_col:
        return ref.at[pl.ds(start, HALF // 2), pl.ds(pl.multiple_of(chip * cols, 128), cols)]
    return ref.at[pl.ds(pl.multiple_of(chip * rows, 16), rows), pl.ds(start, HALF // 2)]


def _remote(src, dst, send_sems, recv_sems, k, dev):
    return pltpu.make_async_remote_copy(src_ref=src, dst_ref=dst, send_sem=send_sems.at[k], recv_sem=recv_sems.at[k],
                                        device_id=dev, device_id_type=MESH)


def _place_shards(w, small, chip, names, label):
    steps = 4
    n, ns = len(names), len(small)
    in_specs, out_specs = [], []
    for name in names:
        rows, cols, by_col = BIG[name]
        tr = rows // steps
        in_specs.append(pl.BlockSpec((tr, cols), lambda i, s: (i, 0)))
        if by_col:
            out_specs.append(pl.BlockSpec((tr, cols), lambda i, s: (i, s[0])))
        else:
            out_specs.append(pl.BlockSpec((tr, cols), lambda i, s: (s[0] * steps + i, 0)))

    def body(s_ref, *refs):
        ins, small_in = refs[:n], refs[n:n + ns]
        outs, small_out = refs[n + ns:2 * n + ns], refs[2 * n + ns:2 * (n + ns)]
        send_sems, recv_sems, local_sems = refs[2 * (n + ns):]
        i = pl.program_id(0)
        x, y, c = _place()
        chip_ = 2 * x + y
        others = [_chip_of(x, y, r) for r in range(3)]

        def block(a, q):
            cols = small[a].shape[1]
            return small_out[a].at[:, pl.ds(pl.multiple_of(q * cols, 128), cols)]

        def local(a):
            return pltpu.make_async_copy(small_in[a], block(a, chip_), local_sems.at[a])

        def remote(a, r):
            qx, qy = others[r]
            return _remote(small_in[a], block(a, chip_), send_sems, recv_sems, 3 * a + r, (qx, qy, c))

        @pl.when(i == 0)
        def _():
            for a in range(ns):
                local(a).start()
                for r in range(3):
                    remote(a, r).start()

        for a in range(n):
            outs[a][...] = ins[a][...].astype(BF16)

        @pl.when(i == steps - 1)
        def _():
            for a in range(ns):
                for r, (qx, qy) in enumerate(others):
                    landed = block(a, 2 * qx + qy)
                    _remote(landed, landed, send_sems, recv_sems, 3 * a + r, (qx, qy, c)).wait_recv()
                for r in range(3):
                    remote(a, r).wait_send()
                local(a).wait()

    out = pl.pallas_call(
        body,
        grid_spec=pltpu.PrefetchScalarGridSpec(
            num_scalar_prefetch=1, grid=(steps,), in_specs=in_specs + [ANY] * ns, out_specs=out_specs + [ANY] * ns,
            scratch_shapes=[pltpu.SemaphoreType.DMA((max(3 * ns, 1),)), pltpu.SemaphoreType.DMA((max(3 * ns, 1),)),
                            pltpu.SemaphoreType.DMA((max(ns, 1),))]),
        out_shape=([jax.ShapeDtypeStruct(_full_shape(name), BF16) for name in names]
                   + [jax.ShapeDtypeStruct((s.shape[0], s.shape[1] * N_CHIPS), F32) for s in small]),
        name=label, compiler_params=_params("arbitrary"),
    )(chip, *[w[name] for name in names], *small)
    return dict(zip(names, out[:n])), list(out[n:])


def _gather_weights(placed, small, names, label, collective_id):
    n, ns = len(names), len(small)
    hbm = pltpu.MemorySpace.HBM
    outs = [jax.new_ref(placed[nm], memory_space=hbm) for nm in names]
    small_in = [jax.new_ref(s, memory_space=hbm) for s in small]
    small_out = [jax.empty_ref(jax.ShapeDtypeStruct((s.shape[0], s.shape[1] * N_CHIPS), F32), memory_space=hbm)
                 for s in small]
    n_sems = 8 * n + 3 * ns

    @pl.kernel(mesh=plsc.ScalarSubcoreMesh(axis_name="seq", num_cores=1), name=label, out_type=(),
               scratch_types=(pltpu.SemaphoreType.DMA((n_sems,)), pltpu.SemaphoreType.DMA((n_sems,)),
                              pltpu.SemaphoreType.DMA((max(ns, 1),))),
               compiler_params=pltpu.CompilerParams(collective_id=collective_id))
    def launch(send_sems, recv_sems, local_sems):
        x, y, c = _place()
        chip = 2 * x + y
        sibling = (x, y, 1 - c)
        others = [_chip_of(x, y, r) for r in range(3)]
        near = others[:2]
        far = 2 * others[2][0] + others[2][1]
        _handshake([(qx, qy, c) for qx, qy in others] + [sibling])

        def small_block(a, q):
            cols = small[a].shape[1]
            return small_out[a].at[:, pl.ds(pl.multiple_of(q * cols, 128), cols)]

        local = [pltpu.make_async_copy(small_in[a], small_block(a, chip), local_sems.at[a]) for a in range(ns)]
        for cp in local:
            cp.start()

        sends = []
        for a, name in enumerate(names):
            mine = _shard_half_of(outs[a], name, chip, c)
            for r, (qx, qy) in enumerate(near):
                sends.append(_remote(mine, mine, send_sems, recv_sems, 8 * a + r, (qx, qy, c)))
        for a in range(ns):
            for r, (qx, qy) in enumerate(others):
                sends.append(_remote(small_in[a], small_block(a, chip), send_sems, recv_sems,
                                     8 * n + 3 * a + r, (qx, qy, c)))
        for cp in sends:
            cp.start()

        forwards = []

        def forward(piece, k, dev):
            cp = _remote(piece, piece, send_sems, recv_sems, k, dev)
            cp.start()
            forwards.append(cp)

        for a, name in enumerate(names):
            for r, (qx, qy) in enumerate(near):
                landed = _shard_half_of(outs[a], name, 2 * qx + qy, c)
                _remote(landed, landed, send_sems, recv_sems, 8 * a + r, (qx, qy, c)).wait_recv()
                ox, oy = near[1 - r]
                forward(_shard_half_part_of(outs[a], name, 2 * qx + qy, c, r), 8 * a + 2 + r, (ox, oy, c))
                forward(landed, 8 * a + 4 + r, sibling)
        for a, name in enumerate(names):
            for part in range(2):
                qx, qy = near[1 - part]
                landed = _shard_half_part_of(outs[a], name, far, c, part)
                _remote(landed, landed, send_sems, recv_sems, 8 * a + 2 + part, (qx, qy, c)).wait_recv()
                forward(landed, 8 * a + 6 + part, sibling)
        for a in range(ns):
            for r, (qx, qy) in enumerate(others):
                landed = small_block(a, 2 * qx + qy)
                _remote(landed, landed, send_sems, recv_sems, 8 * n + 3 * a + r, (qx, qy, c)).wait_recv()
        for a, name in enumerate(names):
            for r, (qx, qy) in enumerate(near):
                landed = _shard_half_of(outs[a], name, 2 * qx + qy, 1 - c)
                _remote(landed, landed, send_sems, recv_sems, 8 * a + 4 + r, sibling).wait_recv()
            for part in range(2):
                landed = _shard_half_part_of(outs[a], name, far, 1 - c, part)
                _remote(landed, landed, send_sems, recv_sems, 8 * a + 6 + part, sibling).wait_recv()
        for cp in sends + forwards:
            cp.wait_send()
        for cp in local:
            cp.wait()

    launch()
    return {nm: ref[...] for nm, ref in zip(names, outs)}, [ref[...] for ref in small_out]


def _exchange_halves(grads, names, label, collective_id):
    n = len(names)

    def body(*refs):
        ins, outs = refs[:n], refs[n:2 * n]
        send_sems, recv_sems = refs[2 * n:]
        x, y, c = _place()
        _handshake([(x, y, 1 - c)])
        copies = []
        for a, name in enumerate(names):
            copies.append(_remote(_half_of(ins[a], BIG[name][2], 1 - c), outs[a], send_sems, recv_sems, a,
                                  (x, y, 1 - c)))
        for cp in copies:
            cp.start()
        for cp in copies:
            cp.wait()

    def half_shape(name):
        r, c_ = _full_shape(name)
        return (HALF, c_) if BIG[name][2] else (r, HALF)

    out_type = tuple(jax.ShapeDtypeStruct(half_shape(nm), grads[nm].dtype) for nm in names)
    sems = (pltpu.SemaphoreType.DMA((n,)), pltpu.SemaphoreType.DMA((n,)))
    got = pl.kernel(
        body, mesh=plsc.ScalarSubcoreMesh(axis_name="seq", num_cores=1), name=label, out_type=out_type,
        scratch_types=sems, compiler_params=pltpu.CompilerParams(collective_id=collective_id),
    )(*[grads[nm] for nm in names])
    return dict(zip(names, got))


def _chip_sum(grads, got, names, core, label):
    n = len(names)
    steps = 4
    g_specs, blks = [], []
    for name in names:
        rows, cols = got[name].shape
        tr = rows // steps
        if BIG[name][2]:
            g_specs.append(pl.BlockSpec((tr, cols), lambda i, s: (s[0] * steps + i, 0)))
        else:
            g_specs.append(pl.BlockSpec((tr, HALF), lambda i, s: (i, s[0])))
        blks.append(pl.BlockSpec((tr, cols), lambda i, s: (i, 0)))

    def body(s_ref, *refs):
        for a in range(n):
            t = refs[a][...] + refs[n + a][...].astype(F32)
            refs[2 * n + a][...] = t
            refs[3 * n + a][...] = t.astype(BF16)

    out = pl.pallas_call(
        body,
        grid_spec=pltpu.PrefetchScalarGridSpec(num_scalar_prefetch=1, grid=(steps,), in_specs=g_specs + blks,
                                               out_specs=blks + blks),
        out_shape=([jax.ShapeDtypeStruct(got[nm].shape, F32) for nm in names]
                   + [jax.ShapeDtypeStruct(got[nm].shape, BF16) for nm in names]),
        name=label, compiler_params=_params("parallel"),
    )(core, *[grads[nm] for nm in names], *[got[nm] for nm in names])
    return {nm: (out[a], out[n + a]) for a, nm in enumerate(names)}


def _piece_shape(name):
    rows, cols, by_col = BIG[name]
    return (HALF, cols) if by_col else (rows, HALF)


def _handshake(peers):
    barrier = pltpu.get_barrier_semaphore()
    for peer in peers:
        pl.semaphore_signal(barrier, inc=1, device_id=peer, device_id_type=MESH)
    pl.semaphore_wait(barrier, len(peers))


def _send_chip_sums(sums, names, label, collective_id):
    n = len(names)

    def body(*refs):
        ins, outs = refs[:n], refs[n:2 * n]
        send_sems, recv_sems = refs[2 * n:]
        x, y, c = _place()
        others = [_chip_of(x, y, r) for r in range(3)]
        _handshake([(qx, qy, c) for qx, qy in others])
        copies = []
        for a, name in enumerate(names):
            for r, (qx, qy) in enumerate(others):
                copies.append(_remote(_shard_of(ins[a], name, 2 * qx + qy), outs[a].at[r], send_sems, recv_sems,
                                      3 * a + r, (qx, qy, c)))
        for cp in copies:
            cp.start()
        for cp in copies:
            cp.wait()

    return pl.kernel(
        body, mesh=plsc.ScalarSubcoreMesh(axis_name="seq", num_cores=1), name=label,
        out_type=tuple(jax.ShapeDtypeStruct((3,) + _piece_shape(nm), BF16) for nm in names),
        scratch_types=(pltpu.SemaphoreType.DMA((3 * n,)), pltpu.SemaphoreType.DMA((3 * n,))),
        compiler_params=pltpu.CompilerParams(collective_id=collective_id),
    )(*[sums[nm] for nm in names])


def _total(parts, chip_core):
    steps = 2
    in_specs, out_specs, operands = [], [], []
    for name in BIG_NAMES:
        by_col = BIG[name][2]
        pr, pc = _piece_shape(name)
        tr = pr // steps
        if by_col:
            in_specs.append(pl.BlockSpec((tr, pc), lambda i, s: (i, s[0])))
            out_specs.append(pl.BlockSpec((tr, pc), lambda i, s: (s[1] * steps + i, 0)))
        else:
            in_specs.append(pl.BlockSpec((tr, pc), lambda i, s: (s[0] * steps + i, 0)))
            out_specs.append(pl.BlockSpec((tr, pc), lambda i, s: (i, s[1])))
        for r in range(3):
            in_specs.append(pl.BlockSpec((None, tr, pc), lambda i, s, r=r: (r, i, 0)))
        own, got = parts[name]
        operands += [own, got, got, got]

    def body(s_ref, *refs):
        for a in range(N_BIG):
            o_ref, a_ref, b_ref, c_ref = refs[4 * a:4 * a + 4]
            refs[4 * N_BIG + a][...] = (((o_ref[...] + a_ref[...].astype(F32)) + b_ref[...].astype(F32))
                                        + c_ref[...].astype(F32))

    totals = pl.pallas_call(
        body,
        grid_spec=pltpu.PrefetchScalarGridSpec(num_scalar_prefetch=1, grid=(steps,), in_specs=in_specs,
                                               out_specs=out_specs),
        out_shape=[jax.ShapeDtypeStruct(BIG[name][:2], F32) for name in BIG_NAMES],
        name="totals", compiler_params=_params("parallel"),
    )(chip_core, *operands)
    return dict(zip(BIG_NAMES, totals))


VEC_ROWS = 32
VEC_ROW = {"mix_norm_g": 0, "conv_b": 1, "b_rgate": 2, "b_igate": 3, "lru_lambda": 4, "rg_norm_g": 5,
           "hg_lower_bound": 6, "hg_norm_g": 8, "ffn_norm_g": 9, "final_norm_g": 10, "loss": 11,
           "conv_w": 12, "meta_tokens": 16}
N_DEV = 8


def _all_reduce_small(pieces, gates, totals):
    names = list(pieces)
    n_small = 10
    hv, hg = VEC_ROWS // 2, gates.shape[0] // 2

    def body(*refs):
        ins = refs[:len(names)]
        g_ref = refs[len(names)]
        vec_ref, gsum_ref = refs[len(names) + 1 + N_BIG:len(names) + 3 + N_BIG]
        big = refs[len(names) + 3 + N_BIG:len(names) + 3 + 2 * N_BIG]
        (mine_v, sib_v, sib_g, chip_v, chip_g, got_v, got_g, send_sems, recv_sems) = refs[len(names) + 3 + 2 * N_BIG:]
        x, y, c = _place()
        chip = 2 * x + y
        sibling = (x, y, 1 - c)
        share = []
        for a, name in enumerate(BIG_NAMES):
            half = _half_of(big[a], BIG[name][2], c)
            share.append(_remote(half, half, send_sems, recv_sems, n_small + a, sibling))
        mine_v[...] = jnp.zeros_like(mine_v)
        for name, ref in zip(names, ins):
            nr, w = ref.shape
            mine_v[VEC_ROW[name]:VEC_ROW[name] + nr, 0:w] = ref[...]

        swap = [_remote(mine_v, sib_v, send_sems, recv_sems, 0, sibling),
                _remote(g_ref, sib_g, send_sems, recv_sems, 1, sibling)]
        for cp in swap:
            cp.start()
        for cp in swap:
            cp.wait()
        for cp in share:
            cp.start()
        chip_v[...] = mine_v[...] + sib_v[...]
        chip_g[...] = g_ref[...] + sib_g[...]

        rows_v = pl.ds(pl.multiple_of(c * hv, 8), hv)
        rows_g = pl.ds(pl.multiple_of(c * hg, 8), hg)
        got_v[chip] = chip_v[rows_v, :]
        got_g[chip] = chip_g[rows_g, :].astype(BF16)
        sends = []
        for r in range(3):
            qx, qy = _chip_of(x, y, r)
            sends.append(_remote(chip_v.at[rows_v, :], got_v.at[chip], send_sems, recv_sems, 2 + r, (qx, qy, c)))
            sends.append(_remote(got_g.at[chip], got_g.at[chip], send_sems, recv_sems, 5 + r, (qx, qy, c)))
        for cp in sends:
            cp.start()
        for cp in sends:
            cp.wait()
        vec_ref[rows_v, :] = ((got_v[0] + got_v[1]) + got_v[2]) + got_v[3]
        gsum_ref[rows_g, :] = ((got_g[0].astype(F32) + got_g[1].astype(F32)) + got_g[2].astype(F32)
                               ) + got_g[3].astype(F32)

        back = [_remote(vec_ref.at[rows_v, :], vec_ref.at[rows_v, :], send_sems, recv_sems, 8, sibling),
                _remote(gsum_ref.at[rows_g, :], gsum_ref.at[rows_g, :], send_sems, recv_sems, 9, sibling)]
        for cp in back:
            cp.start()
        theirs_v = vec_ref.at[pl.ds(pl.multiple_of((1 - c) * hv, 8), hv), :]
        theirs_g = gsum_ref.at[pl.ds(pl.multiple_of((1 - c) * hg, 8), hg), :]
        _remote(theirs_v, theirs_v, send_sems, recv_sems, 8, sibling).wait_recv()
        _remote(theirs_g, theirs_g, send_sems, recv_sems, 9, sibling).wait_recv()
        for cp in back:
            cp.wait_send()
        for a, name in enumerate(BIG_NAMES):
            theirs = _half_of(big[a], BIG[name][2], 1 - c)
            _remote(theirs, theirs, send_sems, recv_sems, n_small + a, sibling).wait_recv()
        for cp in share:
            cp.wait_send()

    vmem = pl.BlockSpec(memory_space=pltpu.VMEM)
    n_sems = n_small + N_BIG
    out = pl.pallas_call(
        body, in_specs=[vmem] * (len(names) + 1) + [ANY] * N_BIG, out_specs=[vmem, vmem] + [ANY] * N_BIG,
        out_shape=([jax.ShapeDtypeStruct((VEC_ROWS, D_MODEL), F32), jax.ShapeDtypeStruct(gates.shape, F32)]
                   + [jax.ShapeDtypeStruct(BIG[n][:2], F32) for n in BIG_NAMES]),
        input_output_aliases={len(names) + 1 + a: 2 + a for a in range(N_BIG)},
        scratch_shapes=[pltpu.VMEM((VEC_ROWS, D_MODEL), F32), pltpu.VMEM((VEC_ROWS, D_MODEL), F32),
                        pltpu.VMEM(gates.shape, F32), pltpu.VMEM((VEC_ROWS, D_MODEL), F32),
                        pltpu.VMEM(gates.shape, F32), pltpu.VMEM((N_CHIPS, hv, D_MODEL), F32),
                        pltpu.VMEM((N_CHIPS, hg) + gates.shape[1:], BF16),
                        pltpu.SemaphoreType.DMA((n_sems,)), pltpu.SemaphoreType.DMA((n_sems,))],
        name="all_reduce_small",
    )(*[pieces[n] for n in names], gates, *[totals[n] for n in BIG_NAMES])
    return out[0], out[1], dict(zip(BIG_NAMES, out[2:]))


def _adamw_math(w, g, m, v):
    m = ADAM_B1 * m + (1.0 - ADAM_B1) * g
    v = ADAM_B2 * v + (1.0 - ADAM_B2) * (g * g)
    m_hat = m / (1.0 - ADAM_B1 ** ADAM_STEP)
    v_hat = v / (1.0 - ADAM_B2 ** ADAM_STEP)
    delta = -ADAM_LR * (m_hat / (jnp.sqrt(v_hat) + ADAM_EPS) + ADAM_WD * w)
    return delta, m, v


def _adamw_big(w, g, m, v):
    steps = 8
    blks = []
    for name in BIG_NAMES:
        rows, cols, _ = BIG[name]
        blks.append(pl.BlockSpec((rows // steps, cols), lambda i: (i, 0)))

    def body(*refs):
        ins, outs = refs[:4 * N_BIG], refs[4 * N_BIG:]
        for a in range(N_BIG):
            w_ref, g_ref, m_ref, v_ref = (ins[k * N_BIG + a] for k in range(4))
            g = g_ref[...]
            d, nm, nv = _adamw_math(w_ref[...], g, m_ref[...], v_ref[...])
            outs[a][...] = g
            outs[N_BIG + a][...] = d
            outs[2 * N_BIG + a][...] = nm
            outs[3 * N_BIG + a][...] = nv

    shapes = [jax.ShapeDtypeStruct(BIG[name][:2], F32) for name in BIG_NAMES]
    out = pl.pallas_call(
        body, grid=(steps,), in_specs=blks * 4, out_specs=blks * 4, out_shape=shapes * 4,
        name="adamw_big", compiler_params=_params("parallel"),
    )(*[t[name] for t in (w, g, m, v) for name in BIG_NAMES])
    return {name: tuple(out[k * N_BIG + a] for k in range(4)) for a, name in enumerate(BIG_NAMES)}


SMALL = {"meta_tokens": (N_META, D_MODEL // N_CHIPS), "mix_norm_g": (1, D_MODEL), "conv_w": (CONV_W, D_RG // N_CHIPS),
         "conv_b": (1, D_RG), "w_rgate": (D_RG, RG_HEAD_DIM), "b_rgate": (1, D_RG), "w_igate": (D_RG, RG_HEAD_DIM),
         "b_igate": (1, D_RG), "lru_lambda": (1, D_RG), "rg_norm_g": (1, D_RG), "hg_lower_bound": (2, D_HG),
         "hg_norm_g": (1, HG_HEAD_DIM), "ffn_norm_g": (1, D_MODEL), "final_norm_g": (1, D_MODEL)}
SMALL_NAMES = tuple(SMALL)
SHARDED_SMALL = ("meta_tokens", "conv_w")


def _adamw_small(vec, gates, w, m, v):
    n = len(SMALL_NAMES)

    def body(*refs):
        vec_ref, gates_ref = refs[:2]
        w_refs, m_refs, v_refs = refs[2:2 + n], refs[2 + n:2 + 2 * n], refs[2 + 2 * n:2 + 3 * n]
        outs = refs[2 + 3 * n:]
        loss_ref = outs[0]
        x, y, _ = _place()
        chip = 2 * x + y
        loss_ref[...] = vec_ref[VEC_ROW["loss"]:VEC_ROW["loss"] + 1, 0:1]

        def update(k, g):
            g_ref, d_ref, nm_ref, nv_ref = outs[1 + 4 * k:5 + 4 * k]
            g_ref[...] = g
            d_ref[...], nm_ref[...], nv_ref[...] = _adamw_math(w_refs[k][...], g, m_refs[k][...], v_refs[k][...])

        for k, name in enumerate(SMALL_NAMES):
            nr, w_ = SMALL[name]
            if name == "w_rgate":
                update(k, gates_ref[0:D_RG, :])
            elif name == "w_igate":
                update(k, gates_ref[D_RG:2 * D_RG, :])
            elif name in SHARDED_SMALL:
                r0 = VEC_ROW[name]
                for q in range(N_CHIPS):
                    @pl.when(chip == q)
                    def _(k=k, r0=r0, nr=nr, w_=w_, q=q):
                        update(k, vec_ref[r0:r0 + nr, q * w_:(q + 1) * w_])
            else:
                r0 = VEC_ROW[name]
                update(k, vec_ref[r0:r0 + nr, 0:w_])

    vmem = pl.BlockSpec(memory_space=pltpu.VMEM)
    out_shape = [jax.ShapeDtypeStruct((1, 1), F32)]
    for name in SMALL_NAMES:
        out_shape += [jax.ShapeDtypeStruct(SMALL[name], F32)] * 4
    outs = pl.pallas_call(
        body, in_specs=[vmem] * (2 + 3 * n), out_specs=[vmem] * len(out_shape), out_shape=out_shape,
        name="adamw_small",
    )(vec, gates, *[w[k] for k in SMALL_NAMES], *[m[k] for k in SMALL_NAMES], *[v[k] for k in SMALL_NAMES])
    loss = outs[0]
    res = {name: tuple(outs[1 + 4 * k:5 + 4 * k]) for k, name in enumerate(SMALL_NAMES)}
    return loss, res


WEIGHT_NAMES = ("meta_tokens", "mix_norm_g", "w_in", "conv_w", "conv_b", "w_rgate", "b_rgate", "w_igate", "b_igate",
                "lru_lambda", "rg_norm_g", "hg_lower_bound", "hg_norm_g", "w_out", "ffn_norm_g", "w_gate_up", "w_down",
                "final_norm_g")


def _to_2d(name, a):
    if name in BIG:
        return a.reshape(BIG[name][:2])
    return a.reshape(SMALL[name])


def kernel(x, meta_tokens, mix_norm_g, w_in, conv_w, conv_b, w_rgate, b_rgate, w_igate, b_igate, lru_lambda, rg_norm_g, hg_lower_bound, hg_norm_g, w_out, ffn_norm_g, w_gate_up, w_down, final_norm_g, loss_target, m_meta_tokens, m_mix_norm_g, m_w_in, m_conv_w, m_conv_b, m_w_rgate, m_b_rgate, m_w_igate, m_b_igate, m_lru_lambda, m_rg_norm_g, m_hg_lower_bound, m_hg_norm_g, m_w_out, m_ffn_norm_g, m_w_gate_up, m_w_down, m_final_norm_g, v_meta_tokens, v_mix_norm_g, v_w_in, v_conv_w, v_conv_b, v_w_rgate, v_b_rgate, v_w_igate, v_b_igate, v_lru_lambda, v_rg_norm_g, v_hg_lower_bound, v_hg_norm_g, v_w_out, v_ffn_norm_g, v_w_gate_up, v_w_down, v_final_norm_g):
    w_raw = dict(zip(WEIGHT_NAMES, (meta_tokens, mix_norm_g, w_in, conv_w, conv_b, w_rgate, b_rgate, w_igate, b_igate,
                                    lru_lambda, rg_norm_g, hg_lower_bound, hg_norm_g, w_out, ffn_norm_g, w_gate_up,
                                    w_down, final_norm_g)))
    m_raw = dict(zip(WEIGHT_NAMES, (m_meta_tokens, m_mix_norm_g, m_w_in, m_conv_w, m_conv_b, m_w_rgate, m_b_rgate,
                                    m_w_igate, m_b_igate, m_lru_lambda, m_rg_norm_g, m_hg_lower_bound, m_hg_norm_g,
                                    m_w_out, m_ffn_norm_g, m_w_gate_up, m_w_down, m_final_norm_g)))
    v_raw = dict(zip(WEIGHT_NAMES, (v_meta_tokens, v_mix_norm_g, v_w_in, v_conv_w, v_conv_b, v_w_rgate, v_b_rgate,
                                    v_w_igate, v_b_igate, v_lru_lambda, v_rg_norm_g, v_hg_lower_bound, v_hg_norm_g,
                                    v_w_out, v_ffn_norm_g, v_w_gate_up, v_w_down, v_final_norm_g)))
    w = {k: _to_2d(k, a) for k, a in w_raw.items()}
    m = {k: _to_2d(k, a) for k, a in m_raw.items()}
    v = {k: _to_2d(k, a) for k, a in v_raw.items()}

    x_i, y_i, c_i = _place()
    core = jnp.reshape(c_i, (1,)).astype(jnp.int32)
    chip = jnp.reshape(2 * x_i + y_i, (1,)).astype(jnp.int32)
    chip_core = jnp.concatenate([chip, core])

    first_names, rest_names = ("w_in",), ("w_out", "w_gate_up", "w_down")
    placed, _ = _place_shards(w, [], chip, first_names, "place_first")
    first, _ = _gather_weights(placed, [], first_names, "gather_first", 1)
    placed, (meta_full, cw_full) = _place_shards(w, [w["meta_tokens"], w["conv_w"]], chip, rest_names, "place_shards")
    rest, _ = _gather_weights(placed, [], rest_names, "gather_rest", 2)
    full = {**first, **rest}

    seq = x.shape[1]
    small ={k: w[k] for k in SMALL_NAMES if k not in SHARDED_SMALL}
    small["conv_w"] = cw_full

    def send_to_chips(sums, names, tag, collective_id):
        arrived = _send_chip_sums({n: sums[n][1] for n in names}, names, "send_chip_sums_" + tag, collective_id)
        return {n: (sums[n][0], a) for n, a in zip(names, arrived)}

    def reduce_to_chips(grads, names, tag, collective_ids):
        got = _exchange_halves({n: grads[n][1] for n in names}, names, "exchange_halves_" + tag, collective_ids[0])

        def chip_sums():
            return _chip_sum({n: grads[n][0] for n in names}, got, names, core, "chip_sum_" + tag)

        return chip_sums, lambda sums: send_to_chips(sums, names, tag, collective_ids[1])

    ffn_names, mixer_names = ("w_gate_up", "w_down", "w_out"), ("w_in",)
    loss, grad_x, grads, parts, parts_mixer = _local_step(
        x.reshape(seq, D_MODEL), meta_full, loss_target.reshape(seq, D_MODEL),
        w["w_in"], full["w_in"], full["w_out"], full["w_gate_up"], full["w_down"], small, chip,
        on_ffn_grads=lambda g: reduce_to_chips(g, ffn_names, "ffn", (3, 4)),
        on_mixer_grads=lambda sums: send_to_chips(sums, mixer_names, "mixer", 5))
    parts.update(parts_mixer)
    totals = _total(parts, chip_core)
    pieces = {k: grads[k] for k in VEC_ROW if k != "loss"}
    pieces["loss"] = loss
    vec, gates, g_big = _all_reduce_small(pieces, grads["w_gates"], totals)
    loss_sum, res = _adamw_small(vec, gates, w, m, v)
    res.update(_adamw_big(w, g_big, m, v))

    out = [loss_sum.reshape(()), grad_x.reshape(1, seq, D_MODEL)]
    for j in range(4):
        out += [res[n][j].reshape(w_raw[n].shape) for n in WEIGHT_NAMES]
    return tuple(out)
```

```python
import math

import jax
import jax.numpy as jnp
from jax import lax
from jax.experimental import pallas as pl
from jax.experimental.pallas import tpu as pltpu
from jax.experimental.pallas import tpu_sc as plsc

F32 = jnp.float32
BF16 = jnp.bfloat16
MESH = pl.DeviceIdType.MESH

D_MODEL = 1024
D_RG = 512
RG_HEAD_DIM = 64
D_HG = 512
HG_HEAD_DIM = 128
HG_HEADS = 4
CHUNK = 64
SUB = 16
N_SUB = CHUNK // SUB
N_META = 16
PAD = CHUNK - N_META
D_IN = 3072
D_FF = 2816
CONV_W = 4
LRU_C = 8.0
EPS = 1e-6
EXP_CLAMP = 80.0
GELU_C = math.sqrt(2.0 / math.pi)
GELU_A = 0.044715
N_CHIPS = 4

ADAM_LR = 0.001
ADAM_B1 = 0.9
ADAM_B2 = 0.999
ADAM_EPS = 1e-08
ADAM_WD = 0.01
ADAM_STEP = 10

VMEM_LIMIT = 56 * 1024 * 1024


def _params(*sem):
    return pltpu.CompilerParams(dimension_semantics=sem, vmem_limit_bytes=VMEM_LIMIT)


def _row_tile(rows, target):
    best = None
    for t in range(16, min(rows, target) + 1, 16):
        if rows % t == 0:
            best = t
    assert best is not None, rows
    return best


def _sigmoid(x):
    return 0.5 * jnp.tanh(0.5 * x) + 0.5


def _dot(a, b):
    return jnp.dot(a, b, preferred_element_type=F32)


def _dot_nt(a, b):
    return lax.dot_general(a, b, (((1,), (1,)), ((), ())), preferred_element_type=F32)


def _dot_tn(a, b):
    return lax.dot_general(a, b, (((0,), (0,)), ((), ())), preferred_element_type=F32)


def _rms(x):
    return lax.rsqrt(jnp.mean(x * x, axis=-1, keepdims=True) + EPS)


def _rms_bwd(dn, n, r):
    return r * (dn - n * jnp.mean(dn * n, axis=-1, keepdims=True))


def _gelu_parts(x):
    t = jnp.tanh(GELU_C * (x + GELU_A * x * x * x))
    g = 0.5 * x * (1.0 + t)
    dg = 0.5 * (1.0 + t) + 0.5 * x * (1.0 - t * t) * GELU_C * (1.0 + 3.0 * GELU_A * x * x)
    return g, dg


def _softplus_neg(lam):
    e = jnp.exp(-jnp.abs(lam))
    w = 1.0 + e
    log1p = jnp.where(w == 1.0, e, jnp.log(w) * e / (w - 1.0))
    return jnp.maximum(-lam, 0.0) + log1p


def _head_mask():
    r = lax.broadcasted_iota(jnp.int32, (D_RG, D_RG), 0) // RG_HEAD_DIM
    c = lax.broadcasted_iota(jnp.int32, (D_RG, D_RG), 1) // RG_HEAD_DIM
    return r == c


def _head_fold():
    r = lax.broadcasted_iota(jnp.int32, (D_RG, RG_HEAD_DIM), 0) % RG_HEAD_DIM
    c = lax.broadcasted_iota(jnp.int32, (D_RG, RG_HEAD_DIM), 1)
    return (r == c).astype(F32)


def _gate_weights(w_r, w_i):
    def body(wr_ref, wi_ref, o_ref):
        fold = _head_fold()
        mask = _head_mask()
        for k, ref in enumerate((wr_ref, wi_ref)):
            full = _dot_nt(ref[...].astype(BF16), fold.astype(BF16))
            o_ref[:, k * D_RG:(k + 1) * D_RG] = jnp.where(mask, full, 0.0).astype(BF16)

    return pl.pallas_call(
        body, out_shape=jax.ShapeDtypeStruct((D_RG, 2 * D_RG), BF16), name="gate_weights",
    )(w_r, w_i)


HEAD = PAD + N_META


def _window_copies(seq_hbm, buf, sems, tm):
    def first(to_vmem):
        seq, vm = seq_hbm.at[pl.ds(0, tm - HEAD)], buf.at[0, pl.ds(HEAD, tm - HEAD)]
        return pltpu.make_async_copy(seq, vm, sems.at[0]) if to_vmem else pltpu.make_async_copy(vm, seq, sems.at[0])

    def later(j, slot, to_vmem):
        seq, vm = seq_hbm.at[pl.ds(pl.multiple_of(j * tm - HEAD, 8), tm)], buf.at[slot]
        if to_vmem:
            return pltpu.make_async_copy(seq, vm, sems.at[slot])
        return pltpu.make_async_copy(vm, seq, sems.at[slot])

    return first, later


def _fetch_window(seq_hbm, buf, sems, i, n_steps, tm):
    first, later = _window_copies(seq_hbm, buf, sems, tm)
    slot = i % 2

    @pl.when(i == 0)
    def _():
        first(True).start()

    if n_steps > 1:
        @pl.when(i + 1 < n_steps)
        def _():
            later(i + 1, 1 - slot, True).start()

    @pl.when(i == 0)
    def _():
        first(True).wait()

    if n_steps > 1:
        @pl.when(i > 0)
        def _():
            later(i, slot, True).wait()

    return slot


def _in_proj_local(x, meta, g1, w_own, chip):
    T = x.shape[0] + HEAD
    tm = _row_tile(T, 832)
    n_steps = T // tm
    cols = BIG["w_in"][1]

    def body(s_ref, x_hbm, meta_ref, g_ref, w_ref, p_ref, u_ref, h_ref, buf, sems, wb):
        i = pl.program_id(0)
        slot = _fetch_window(x_hbm, buf, sems, i, n_steps, tm)

        @pl.when(i == 0)
        def _():
            buf[0, 0:PAD, :] = jnp.zeros((PAD, D_MODEL), F32)
            buf[0, PAD:HEAD, :] = meta_ref[...]
            wb[...] = w_ref[...].astype(BF16)

        h = buf[slot]
        h_ref[...] = h
        u = (h * _rms(h) * g_ref[...]).astype(BF16)
        u_ref[...] = u
        p_ref[...] = _dot(u, wb[...])

    return pl.pallas_call(
        body,
        grid_spec=pltpu.PrefetchScalarGridSpec(
            num_scalar_prefetch=1, grid=(n_steps,),
            in_specs=[pl.BlockSpec(memory_space=pl.ANY),
                      pl.BlockSpec((N_META, D_MODEL), lambda i, s: (0, 0)),
                      pl.BlockSpec((1, D_MODEL), lambda i, s: (0, 0)),
                      pl.BlockSpec((D_MODEL, cols), lambda i, s: (0, 0))],
            out_specs=[pl.BlockSpec((tm, cols), lambda i, s: (i, s[0])),
                       pl.BlockSpec((tm, D_MODEL), lambda i, s: (i, 0)),
                       pl.BlockSpec((tm, D_MODEL), lambda i, s: (i, 0))],
            scratch_shapes=[pltpu.VMEM((2, tm, D_MODEL), F32), pltpu.SemaphoreType.DMA((2,)),
                            pltpu.VMEM((D_MODEL, cols), BF16)]),
        out_shape=[jax.ShapeDtypeStruct((T, D_IN), F32), jax.ShapeDtypeStruct((T, D_MODEL), BF16),
                   jax.ShapeDtypeStruct((T, D_MODEL), F32)],
        name="in_proj_local", compiler_params=_params("arbitrary"),
    )(chip, x, meta, g1, w_own)


def _in_proj_rest(u, w_in, p, chip):
    T = u.shape[0]
    tm = _row_tile(T, 2080)
    cols = BIG["w_in"][1]
    block = lambda j, s: (s[0] + 1 + j) % N_CHIPS

    def body(s_ref, u_ref, w_ref, p_in_ref, p_ref):
        p_ref[...] = _dot(u_ref[...], w_ref[...])

    return pl.pallas_call(
        body,
        grid_spec=pltpu.PrefetchScalarGridSpec(
            num_scalar_prefetch=1, grid=(N_CHIPS - 1, T // tm),
            in_specs=[pl.BlockSpec((tm, D_MODEL), lambda j, i, s: (i, 0)),
                      pl.BlockSpec((D_MODEL, cols), lambda j, i, s: (0, block(j, s))), ANY],
            out_specs=pl.BlockSpec((tm, cols), lambda j, i, s: (i, block(j, s)))),
        out_shape=jax.ShapeDtypeStruct((T, D_IN), F32),
        input_output_aliases={3: 0},
        name="in_proj_rest", compiler_params=_params("arbitrary", "arbitrary"),
    )(chip, u, w_in, p)


def _scan_block_fwd(A, B, rowi):
    for d in (1, 2, 4):
        a_sh = pltpu.roll(A, d, axis=0)
        b_sh = pltpu.roll(B, d, axis=0)
        m = rowi >= d
        B = jnp.where(m, A * b_sh + B, B)
        A = jnp.where(m, A * a_sh, A)
    return A, B


def _scan_block_bwd(A, B, rowi):
    for d in (1, 2, 4):
        a_sh = pltpu.roll(A, 8 - d, axis=0)
        b_sh = pltpu.roll(B, 8 - d, axis=0)
        m = rowi < 8 - d
        B = jnp.where(m, A * b_sh + B, B)
        A = jnp.where(m, A * a_sh, A)
    return A, B


def _rg_gates(xc, w_ref, bg_ref, lam):
    pre = _dot(xc.astype(BF16), w_ref[...]) + bg_ref[...]
    r = _sigmoid(pre[:, :D_RG])
    ig = _sigmoid(pre[:, D_RG:])
    sp = _softplus_neg(lam)
    la = -LRU_C * sp * r
    a = jnp.exp(la)
    th = jnp.tanh(la)
    u = 1.0 - th
    rc = pl.reciprocal(u, approx=True)
    rc = rc * (2.0 - u * rc)
    rc = rc * (2.0 - u * rc)
    m2 = -2.0 * th * rc
    inv_m = lax.rsqrt(jnp.maximum(m2, 1e-30))
    return r, ig, sp, a, m2 * inv_m, inv_m


def _conv(ext, cw_ref, cb_ref, tm):
    xc = cb_ref[...] + cw_ref[0:1, :] * ext[8 - 3:8 - 3 + tm, :]
    for j in range(1, CONV_W):
        xc = xc + cw_ref[j:j + 1, :] * ext[8 - 3 + j:8 - 3 + j + tm, :]
    return xc


def _scan_unroll(blocks):
    return 4 if blocks % 4 == 0 else 2 if blocks % 2 == 0 else 1


def _rg_fwd(p, cw, cb, wg, bg, lam, rg_g):
    T = p.shape[0]
    tm = _row_tile(T, 832)
    unroll = _scan_unroll(tm // 8)

    def body(xg_ref, cw_ref, cb_ref, w_ref, bg_ref, lam_ref, g_ref, y_ref, h_ref, xc_ref, ext, a_s, b_s, carry):
        i = pl.program_id(0)

        @pl.when(i == 0)
        def _():
            ext[0:8, :] = jnp.zeros((8, D_RG), F32)
            carry[...] = jnp.zeros((1, D_RG), F32)

        ext[8:8 + tm, :] = xg_ref[:, :D_RG]
        xc = _conv(ext, cw_ref, cb_ref, tm)
        xc_ref[...] = xc
        r, ig, sp, a, m, _ = _rg_gates(xc, w_ref, bg_ref, lam_ref[...])
        row = i * tm + lax.broadcasted_iota(jnp.int32, (tm, 1), 0)
        a_s[...] = a
        b_s[...] = jnp.where(row >= PAD, m * ig * xc, 0.0)
        rowi = lax.broadcasted_iota(jnp.int32, (8, D_RG), 0)

        def blk(j, c):
            for u in range(unroll):
                o = pl.multiple_of((j * unroll + u) * 8, 8)
                A, B = _scan_block_fwd(a_s[pl.ds(o, 8), :], b_s[pl.ds(o, 8), :], rowi)
                h = B + A * c
                h_ref[pl.ds(o, 8), :] = h
                c = h[7:8, :]
            return c

        carry[...] = lax.fori_loop(0, tm // (8 * unroll), blk, carry[...])
        ext[0:8, :] = ext[tm:tm + 8, :]
        g, _ = _gelu_parts(xg_ref[:, D_RG:])
        yy = g * h_ref[...]
        y_ref[...] = (yy * _rms(yy) * g_ref[...]).astype(BF16)

    vec = lambda n: pl.BlockSpec((1, n), lambda i: (0, 0))
    return pl.pallas_call(
        body, grid=(T // tm,),
        in_specs=[pl.BlockSpec((tm, 2 * D_RG), lambda i: (i, 0)),
                  pl.BlockSpec((CONV_W, D_RG), lambda i: (0, 0)), vec(D_RG),
                  pl.BlockSpec((D_RG, 2 * D_RG), lambda i: (0, 0)), vec(2 * D_RG), vec(D_RG), vec(D_RG)],
        out_specs=[pl.BlockSpec((tm, D_RG), lambda i: (i, 0))] * 3,
        out_shape=[jax.ShapeDtypeStruct((T, D_RG), BF16), jax.ShapeDtypeStruct((T, D_RG), F32),
                   jax.ShapeDtypeStruct((T, D_RG), F32)],
        scratch_shapes=[pltpu.VMEM((tm + 8, D_RG), F32), pltpu.VMEM((tm, D_RG), F32),
                        pltpu.VMEM((tm, D_RG), F32), pltpu.VMEM((1, D_RG), F32)],
        name="rg_fwd", compiler_params=_params("arbitrary"),
    )(p, cw, cb, wg, bg, lam, rg_g)


def _running_sum(x, down):
    r = lax.broadcasted_iota(jnp.int32, (CHUNK, CHUNK), 0)
    c = lax.broadcasted_iota(jnp.int32, (CHUNK, CHUNK), 1)
    tri = ((c <= r) if down else (c >= r)).astype(BF16)
    hi = x.astype(BF16)
    rest = x - hi.astype(F32)
    mid = rest.astype(BF16)
    lo = (rest - mid.astype(F32)).astype(BF16)
    return (_dot(tri, hi) + _dot(tri, mid)) + _dot(tri, lo)


def _hg_gates(hq, hf, lbraw_ref, valid):
    lb = _sigmoid(lbraw_ref[0:1, :] - lbraw_ref[1:2, :])
    sq = _sigmoid(hq)
    q = hq * sq
    sf = _sigmoid(hf)
    f = lb + (1.0 - lb) * sf
    lf = jnp.where(valid, jnp.log(f), 0.0)
    b = _running_sum(lf, True)
    return lb, sq, q, sf, f, b


def _hg_head(qh, kh, bh):
    b_last = bh[CHUNK - 1:CHUNK, :]
    refs = [bh[SUB * s:SUB * s + 1, :] for s in range(N_SUB)]
    r_sel = jnp.concatenate([jnp.broadcast_to(refs[s], (SUB, HG_HEAD_DIM)) for s in range(N_SUB)], axis=0)
    eb = jnp.exp(bh)
    eq = jnp.exp(bh - r_sel)
    ekh = jnp.exp(b_last - bh)
    ek = [jnp.exp(jnp.minimum(refs[s] - bh[:SUB * (s + 1), :], EXP_CLAMP)) for s in range(N_SUB)]
    qe = qh * eq

    def own_rows(s):
        parts = [jnp.zeros((SUB * s, HG_HEAD_DIM), F32)] if s else []
        parts.append(qe[SUB * s:SUB * (s + 1), :])
        if s < N_SUB - 1:
            parts.append(jnp.zeros((CHUNK - SUB * (s + 1), HG_HEAD_DIM), F32))
        return jnp.concatenate(parts, axis=0)

    q_hat = jnp.concatenate([own_rows(s) for s in range(N_SUB)], axis=1)

    def met_rows(s):
        n = SUB * (s + 1)
        ke = kh[:n, :] * ek[s]
        return ke if n == CHUNK else jnp.concatenate([ke, jnp.zeros((CHUNK - n, HG_HEAD_DIM), F32)], axis=0)

    k_til = jnp.concatenate([met_rows(s) for s in range(N_SUB)], axis=1)
    return b_last, eb, eq, ekh, ek, q_hat, k_til


def _causal():
    r = lax.broadcasted_iota(jnp.int32, (CHUNK, CHUNK), 0)
    c = lax.broadcasted_iota(jnp.int32, (CHUNK, CHUNK), 1)
    return r >= c


def _chunks_per_step(n_chunks):
    for c in (5, 4, 3, 2):
        if n_chunks % c == 0:
            return c
    return 1


def _hg_fwd(p, lbraw, hg_g):
    T = p.shape[0]
    n_chunks = T // CHUNK
    cps = _chunks_per_step(n_chunks)
    rows = cps * CHUNK

    def body(hq_ref, hf_ref, hi_ref, hg_ref, lb_ref, g_ref, y_ref, o_ref, st_all_ref, st):
        i = pl.program_id(0)

        @pl.when(i == 0)
        def _():
            st[...] = jnp.zeros_like(st)

        def chunk(j, carry):
            rs = pl.ds(pl.multiple_of(j * CHUNK, CHUNK), CHUNK)
            chunk_body(i * cps + j, hq_ref.at[rs, :], hf_ref.at[rs, :], hi_ref.at[rs, :], hg_ref.at[rs, :], lb_ref,
                       g_ref, y_ref.at[rs, :], o_ref.at[rs, :], st_all_ref.at[pl.ds(j, 1)], st)
            return carry

        lax.fori_loop(0, cps, chunk, 0, unroll=True)

    def chunk_body(n, hq_ref, hf_ref, hi_ref, hg_ref, lb_ref, g_ref, y_ref, o_ref, st_all_ref, st):
        valid = (n * CHUNK + lax.broadcasted_iota(jnp.int32, (CHUNK, 1), 0)) >= PAD
        hq, hf, v, hg = hq_ref[...], hf_ref[...], hi_ref[...], hg_ref[...]
        lb, sq, q, sf, f, b = _hg_gates(hq, hf, lb_ref, valid)
        k = 1.0 - f
        st_all_ref[0] = st[...]
        causal = _causal()
        v_t = v.T.astype(BF16)
        heads = [slice(h * HG_HEAD_DIM, (h + 1) * HG_HEAD_DIM) for h in range(HG_HEADS)]
        fac = []
        for sl in heads:
            qh, kh, bh = q[:, sl], k[:, sl], b[:, sl]
            b_last, eb, _, ekh, _, q_hat, k_til = _hg_head(qh, kh, bh)
            fac.append((jnp.exp(b_last), (qh * eb).astype(BF16), q_hat.astype(BF16), k_til.astype(BF16),
                        (kh * ekh).astype(BF16), v[:, sl].astype(BF16)))
        raw = []
        for sl, (_, q_til, q_hat, k_til, k_hat, _) in zip(heads, fac):
            st_h = st[sl, :]
            raw.append((_dot_nt(q_til, st_h.astype(BF16)), _dot_nt(q_hat, k_til), _dot(v_t[sl, :], k_hat), st_h))
        for sl, (e_last, _, _, _, _, vb), (inter, att, upd, st_h) in zip(heads, fac, raw):
            o = inter + _dot(jnp.where(causal, att, 0.0).astype(BF16), vb)
            st[sl, :] = st_h * e_last + upd
            o_ref[:, sl] = o
            hgh = hg[:, sl]
            y_ref[:, sl] = (o * _rms(o) * g_ref[...] * (hgh * _sigmoid(hgh))).astype(BF16)

    col = lambda j: pl.BlockSpec((rows, D_HG), lambda n: (n, j))
    return pl.pallas_call(
        body, grid=(n_chunks // cps,),
        in_specs=[col(2), col(3), col(4), col(5),
                  pl.BlockSpec((2, D_HG), lambda n: (0, 0)), pl.BlockSpec((1, HG_HEAD_DIM), lambda n: (0, 0))],
        out_specs=[pl.BlockSpec((rows, D_HG), lambda n: (n, 0)), pl.BlockSpec((rows, D_HG), lambda n: (n, 0)),
                   pl.BlockSpec((cps, D_HG, HG_HEAD_DIM), lambda n: (n, 0, 0))],
        out_shape=[jax.ShapeDtypeStruct((T, D_HG), BF16), jax.ShapeDtypeStruct((T, D_HG), F32),
                   jax.ShapeDtypeStruct((n_chunks, D_HG, HG_HEAD_DIM), F32)],
        scratch_shapes=[pltpu.VMEM((D_HG, HG_HEAD_DIM), F32)],
        name="hg_fwd", compiler_params=_params("arbitrary"),
    )(p, p, p, p, lbraw, hg_g)


def _ffn_fwd(h0, y_rg, y_hg, w_out, g2, w_gu, w_down, gf, target):
    T = h0.shape[0]
    tm = _row_tile(T, 320)
    n_steps = T // tm

    def body(h_ref, yr_ref, yh_ref, wo_ref, g2_ref, wgu_ref, wd_ref, gf_ref, t_hbm,
             h1_ref, v_ref, y_ref, gu_ref, act_ref, dh2_ref, dh2b_ref, loss_ref, gg_ref, tbuf, sems):
        i = pl.program_id(0)
        slot = _fetch_window(t_hbm, tbuf, sems, i, n_steps, tm)

        @pl.when(i == 0)
        def _():
            loss_ref[...] = jnp.zeros_like(loss_ref)
            gg_ref[...] = jnp.zeros_like(gg_ref)
            tbuf[0, 0:HEAD, :] = jnp.zeros((HEAD, D_MODEL), F32)

        y_ref[:, :D_RG] = yr_ref[...]
        y_ref[:, D_RG:] = yh_ref[...]
        h1 = h_ref[...] + _dot(y_ref[...], wo_ref[...])
        h1_ref[...] = h1
        v = (h1 * _rms(h1) * g2_ref[...]).astype(BF16)
        v_ref[...] = v

        gu = _dot(v, wgu_ref[...])
        gu_ref[...] = gu.astype(BF16)
        g = gu[:, :D_FF]
        act = (g * _sigmoid(g) * gu[:, D_FF:]).astype(BF16)
        act_ref[...] = act

        h2 = h1 + _dot(act, wd_ref[...])
        r = _rms(h2)
        n = h2 * r
        gf_ = gf_ref[...]
        row = i * tm + lax.broadcasted_iota(jnp.int32, (tm, 1), 0)
        err = jnp.where(row >= HEAD, n * gf_ - tbuf[slot], 0.0)
        loss_ref[...] += 0.5 * jnp.sum(jnp.mean(err * err, axis=-1, keepdims=True), axis=0, keepdims=True)
        dy = err * (1.0 / D_MODEL)
        gg_ref[...] += jnp.sum(dy * n, axis=0, keepdims=True)
        dh2 = _rms_bwd(dy * gf_, n, r)
        dh2_ref[...] = dh2
        dh2b_ref[...] = dh2.astype(BF16)

    row_spec = lambda n: pl.BlockSpec((tm, n), lambda i: (i, 0))
    vec = pl.BlockSpec((1, D_MODEL), lambda i: (0, 0))
    return pl.pallas_call(
        body, grid=(n_steps,),
        in_specs=[row_spec(D_MODEL), row_spec(D_RG), row_spec(D_HG), _resident((D_MODEL, D_MODEL)), vec,
                  _resident((D_MODEL, 2 * D_FF)), _resident((D_FF, D_MODEL)), vec,
                  pl.BlockSpec(memory_space=pl.ANY)],
        out_specs=[row_spec(D_MODEL), row_spec(D_MODEL), row_spec(D_MODEL), row_spec(2 * D_FF), row_spec(D_FF),
                   row_spec(D_MODEL), row_spec(D_MODEL), pl.BlockSpec((1, 1), lambda i: (0, 0)), vec],
        out_shape=[jax.ShapeDtypeStruct((T, D_MODEL), F32), jax.ShapeDtypeStruct((T, D_MODEL), BF16),
                   jax.ShapeDtypeStruct((T, D_MODEL), BF16), jax.ShapeDtypeStruct((T, 2 * D_FF), BF16),
                   jax.ShapeDtypeStruct((T, D_FF), BF16), jax.ShapeDtypeStruct((T, D_MODEL), F32),
                   jax.ShapeDtypeStruct((T, D_MODEL), BF16), jax.ShapeDtypeStruct((1, 1), F32),
                   jax.ShapeDtypeStruct((1, D_MODEL), F32)],
        scratch_shapes=[pltpu.VMEM((2, tm, D_MODEL), F32), pltpu.SemaphoreType.DMA((2,))],
        name="ffn_fwd", compiler_params=_params("arbitrary"),
    )(h0, y_rg, y_hg, w_out, g2, w_gu, w_down, gf, target)


def _resident(shape):
    return pl.BlockSpec(shape, lambda i: (0,) * len(shape), pipeline_mode=pl.Buffered(1))


def _ffn_bwd(dh2b, gu, w_down, w_gu, h1, g2, dh2, w_out):
    T = h1.shape[0]
    tm = _row_tile(T, 320)

    def body(d_ref, gu_ref, wd_ref, wgu_ref, h_ref, g_ref, d2_ref, wo_ref, dgu_ref, dh1_ref, dh1b_ref, dy_ref, gg_ref):
        i = pl.program_id(0)

        @pl.when(i == 0)
        def _():
            gg_ref[...] = jnp.zeros_like(gg_ref)

        dact = _dot_nt(d_ref[...], wd_ref[...]).astype(BF16)
        g = gu_ref[:, :D_FF]
        u = gu_ref[:, D_FF:]
        s = _sigmoid(g)
        dgu_ref[:, :D_FF] = dact * u * (s * (1.0 + g * (1.0 - s)))
        dgu_ref[:, D_FF:] = dact * (g * s)

        dv = _dot_nt(dgu_ref[...], wgu_ref[...])
        h1_ = h_ref[...]
        r = _rms(h1_)
        n = h1_ * r
        gg_ref[...] += jnp.sum(dv * n, axis=0, keepdims=True)
        dh1 = d2_ref[...] + _rms_bwd(dv * g_ref[...], n, r)
        dh1_ref[...] = dh1
        db = dh1.astype(BF16)
        dh1b_ref[...] = db
        dy_ref[...] = _dot_nt(db, wo_ref[...])

    row = lambda n: pl.BlockSpec((tm, n), lambda i: (i, 0))
    return pl.pallas_call(
        body, grid=(T // tm,),
        in_specs=[row(D_MODEL), row(2 * D_FF), _resident((D_FF, D_MODEL)), _resident((D_MODEL, 2 * D_FF)),
                  row(D_MODEL), pl.BlockSpec((1, D_MODEL), lambda i: (0, 0)), row(D_MODEL),
                  _resident((D_MODEL, D_MODEL))],
        out_specs=[row(2 * D_FF), row(D_MODEL), row(D_MODEL), row(D_MODEL),
                   pl.BlockSpec((1, D_MODEL), lambda i: (0, 0))],
        out_shape=[jax.ShapeDtypeStruct((T, 2 * D_FF), BF16), jax.ShapeDtypeStruct((T, D_MODEL), F32),
                   jax.ShapeDtypeStruct((T, D_MODEL), BF16), jax.ShapeDtypeStruct((T, D_MODEL), F32),
                   jax.ShapeDtypeStruct((1, D_MODEL), F32)],
        name="ffn_bwd", compiler_params=_params("arbitrary"),
    )(dh2b, gu, w_down, w_gu, h1, g2, dh2, w_out)


def _rg_bwd(p, xc_all, hs, dy, dp, cw, cb, wg, bg, lam, rg_g):
    T = p.shape[0]
    tm = _row_tile(T, 832)
    nt = T // tm
    hb = tm // 8
    unroll = _scan_unroll(hb)

    def body(xg_ref, xc_ref, h_ref, hh_ref, dy_ref, dp_in_ref, cw_ref, cb_ref, w_ref, bg_ref, lam_ref, g_ref,
             dp_ref, gcw_ref, gcb_ref, gw_ref, gbg_ref, glam_ref, gg_ref,
             dext, a_s, b_s, d_s, gacc, carry_d, carry_a):
        i = pl.program_id(0)
        t_idx = nt - 1 - i

        @pl.when(i == 0)
        def _():
            dext[tm:tm + 8, :] = jnp.zeros((8, D_RG), F32)
            carry_d[...] = jnp.zeros_like(carry_d)
            carry_a[...] = jnp.zeros_like(carry_a)
            gacc[...] = jnp.zeros_like(gacc)
            for ref in (gcw_ref, gcb_ref, gbg_ref, glam_ref, gg_ref, gw_ref):
                ref[...] = jnp.zeros_like(ref)

        first = t_idx == 0
        xc = xc_ref[...]
        lam_ = lam_ref[...]
        r, ig, sp, a, m, inv_m = _rg_gates(xc, w_ref, bg_ref, lam_)
        row = t_idx * tm + lax.broadcasted_iota(jnp.int32, (tm, 1), 0)
        valid = row >= PAD

        gr = xg_ref[:, D_RG:]
        g, dgelu = _gelu_parts(gr)
        h = h_ref[...]
        yy = g * h
        rr = _rms(yy)
        nn = yy * rr
        dy_ = dy_ref[...]
        gg_ref[...] += jnp.sum(dy_ * nn, axis=0, keepdims=True)
        dyy = _rms_bwd(dy_ * g_ref[...], nn, rr)
        dp_ref[:, D_RG:] = (dyy * h * dgelu).astype(BF16)

        a_s[...] = a
        b_s[...] = dyy * g
        rowi = lax.broadcasted_iota(jnp.int32, (8, D_RG), 0)

        def blk(jj, c):
            cd, ca = c
            for u in range(unroll):
                o = pl.multiple_of((hb - 1 - (jj * unroll + u)) * 8, 8)
                a_blk = a_s[pl.ds(o, 8), :]
                a_next = jnp.where(rowi == 7, ca, pltpu.roll(a_blk, 7, axis=0))
                A, B = _scan_block_bwd(a_next, b_s[pl.ds(o, 8), :], rowi)
                d = B + A * cd
                d_s[pl.ds(o, 8), :] = d
                cd, ca = d[0:1, :], a_blk[0:1, :]
            return cd, ca

        cd, ca = lax.fori_loop(0, hb // unroll, blk, (carry_d[...], carry_a[...]))
        carry_d[...] = cd
        carry_a[...] = ca
        delta = d_s[...]

        h_last_prev = jnp.where(first, 0.0, hh_ref[7:8, :])
        row0 = lax.broadcasted_iota(jnp.int32, (tm, 1), 0) == 0
        h_prev = jnp.where(row0, h_last_prev, pltpu.roll(h, 1, axis=0))
        dbx = jnp.where(valid, delta, 0.0)
        da = delta * h_prev
        di = dbx * m * xc
        dm = dbx * ig * xc
        dla = a * (da - dm * a * inv_m)
        dla = jnp.where(valid, dla, 0.0)
        glam_ref[...] += jnp.sum(dla * r, axis=0, keepdims=True) * (LRU_C / (1.0 + jnp.exp(lam_)))
        dr = (-LRU_C) * sp * dla
        dpre = jnp.concatenate([dr * r * (1.0 - r), di * ig * (1.0 - ig)], axis=1)
        gbg_ref[...] += jnp.sum(dpre, axis=0, keepdims=True)
        dpre_b = dpre.astype(BF16)
        gacc[...] += _dot_tn(xc.astype(BF16), dpre_b)
        dxc = dbx * m * ig + _dot_nt(dpre_b, w_ref[...])
        gcb_ref[...] += jnp.sum(dxc, axis=0, keepdims=True)
        dext[0:tm, :] = dxc
        xr = xg_ref[:, :D_RG]
        dxr = None
        for j in range(CONV_W):
            shifted = dext[3 - j:3 - j + tm, :]
            gcw_ref[j:j + 1, :] += jnp.sum(xr * shifted, axis=0, keepdims=True)
            tap = cw_ref[j:j + 1, :] * shifted
            dxr = tap if dxr is None else dxr + tap
        dp_ref[:, :D_RG] = dxr.astype(BF16)
        dext[tm:tm + 8, :] = dext[0:8, :]

        @pl.when(i == nt - 1)
        def _():
            fold = _head_fold()
            mask = _head_mask()
            fold_b = fold.astype(BF16)
            for k in range(2):
                blockdiag = jnp.where(mask, gacc[:, k * D_RG:(k + 1) * D_RG], 0.0)
                hi = blockdiag.astype(BF16)
                rest = blockdiag - hi.astype(F32)
                mid = rest.astype(BF16)
                lo = (rest - mid.astype(F32)).astype(BF16)
                gw_ref[k * D_RG:(k + 1) * D_RG, :] = (_dot(hi, fold_b) + _dot(mid, fold_b)) + _dot(lo, fold_b)

    vec = lambda n: pl.BlockSpec((1, n), lambda i: (0, 0))
    rev = lambda n: pl.BlockSpec((tm, n), lambda i: (nt - 1 - i, 0))
    halo = lambda n: pl.BlockSpec((8, n), lambda i: (jnp.maximum((nt - 1 - i) * hb - 1, 0), 0))
    return pl.pallas_call(
        body, grid=(nt,),
        in_specs=[rev(2 * D_RG), rev(D_RG), rev(D_RG), halo(D_RG), rev(D_RG), ANY,
                  pl.BlockSpec((CONV_W, D_RG), lambda i: (0, 0)), vec(D_RG),
                  pl.BlockSpec((D_RG, 2 * D_RG), lambda i: (0, 0)), vec(2 * D_RG), vec(D_RG), vec(D_RG)],
        out_specs=[rev(2 * D_RG), pl.BlockSpec((CONV_W, D_RG), lambda i: (0, 0)), vec(D_RG),
                   pl.BlockSpec((2 * D_RG, RG_HEAD_DIM), lambda i: (0, 0)), vec(2 * D_RG), vec(D_RG), vec(D_RG)],
        input_output_aliases={5: 0},
        out_shape=[jax.ShapeDtypeStruct((T, D_IN), BF16), jax.ShapeDtypeStruct((CONV_W, D_RG), F32),
                   jax.ShapeDtypeStruct((1, D_RG), F32), jax.ShapeDtypeStruct((2 * D_RG, RG_HEAD_DIM), F32),
                   jax.ShapeDtypeStruct((1, 2 * D_RG), F32), jax.ShapeDtypeStruct((1, D_RG), F32),
                   jax.ShapeDtypeStruct((1, D_RG), F32)],
        scratch_shapes=[pltpu.VMEM((tm + 8, D_RG), F32),
                        pltpu.VMEM((tm, D_RG), F32), pltpu.VMEM((tm, D_RG), F32), pltpu.VMEM((tm, D_RG), F32),
                        pltpu.VMEM((D_RG, 2 * D_RG), F32), pltpu.VMEM((1, D_RG), F32), pltpu.VMEM((1, D_RG), F32)],
        name="rg_bwd", compiler_params=_params("arbitrary"),
    )(p, xc_all, hs, hs, dy, dp, cw, cb, wg, bg, lam, rg_g)


def _hg_bwd(p, o_all, st_all, dy, lbraw, hg_g):
    T = p.shape[0]
    n_chunks = T // CHUNK
    cps = _chunks_per_step(n_chunks)
    rows = cps * CHUNK
    n_steps = n_chunks // cps

    def body(hq_ref, hf_ref, hi_ref, hg_ref, o_ref, st_ref, dy_ref, lb_ref, g_ref,
             dp_ref, glb_ref, gg_ref, dst):
        i = pl.program_id(0)

        @pl.when(i == 0)
        def _():
            dst[...] = jnp.zeros_like(dst)
            glb_ref[...] = jnp.zeros_like(glb_ref)
            gg_ref[...] = jnp.zeros_like(gg_ref)

        dp_ref[:, :2 * D_RG] = jnp.zeros((rows, 2 * D_RG), BF16)

        def chunk(jj, carry):
            j = cps - 1 - jj
            rs = pl.ds(pl.multiple_of(j * CHUNK, CHUNK), CHUNK)
            chunk_body((n_steps - 1 - i) * cps + j, hq_ref.at[rs, :], hf_ref.at[rs, :], hi_ref.at[rs, :],
                       hg_ref.at[rs, :], o_ref.at[rs, :], st_ref.at[pl.ds(j, 1)], dy_ref.at[rs, :], lb_ref, g_ref,
                       dp_ref.at[rs, pl.ds(2 * D_RG, 4 * D_HG)], glb_ref, gg_ref, dst)
            return carry

        lax.fori_loop(0, cps, chunk, 0, unroll=True)

    def chunk_body(n, hq_ref, hf_ref, hi_ref, hg_ref, o_ref, st_ref, dy_ref, lb_ref, g_ref,
                   dp_ref, glb_ref, gg_ref, dst):
        valid = (n * CHUNK + lax.broadcasted_iota(jnp.int32, (CHUNK, 1), 0)) >= PAD
        hq, hf, v, hg = hq_ref[...], hf_ref[...], hi_ref[...], hg_ref[...]
        lb, sq, q, sf, f, b = _hg_gates(hq, hf, lb_ref, valid)
        k = 1.0 - f
        causal = _causal()
        r_i = lax.broadcasted_iota(jnp.int32, (CHUNK, CHUNK), 0)
        c_i = lax.broadcasted_iota(jnp.int32, (CHUNK, CHUNK), 1)
        causal_t = r_i <= c_i
        is_last = lax.broadcasted_iota(jnp.int32, (CHUNK, 1), 0) == CHUNK - 1
        g_ = g_ref[...]
        db_parts, dq_parts, dk_parts = [], [], []
        gg = jnp.zeros((1, HG_HEAD_DIM), F32)
        heads = [slice(h * HG_HEAD_DIM, (h + 1) * HG_HEAD_DIM) for h in range(HG_HEADS)]

        do_parts = []
        for h, sl in enumerate(heads):
            o = o_ref[:, sl]
            ro = _rms(o)
            no = o * ro
            hgh = hg[:, sl]
            sg = _sigmoid(hgh)
            dyh = dy_ref[:, sl]
            dp_ref[:, 3 * D_HG + h * HG_HEAD_DIM:3 * D_HG + (h + 1) * HG_HEAD_DIM] = (
                dyh * no * g_ * sg * (1.0 + hgh * (1.0 - sg))).astype(BF16)
            dng = dyh * hgh * sg
            gg = gg + jnp.sum(dng * no, axis=0, keepdims=True)
            do_parts.append(_rms_bwd(dng * g_, no, ro))
        do_t = jnp.concatenate(do_parts, axis=1).T.astype(BF16)

        fac = []
        for sl, do in zip(heads, do_parts):
            qh, kh, bh = q[:, sl], k[:, sl], b[:, sl]
            b_last, eb, eq, ekh, ek, q_hat, k_til = _hg_head(qh, kh, bh)
            fac.append(dict(qh=qh, kh=kh, e_last=jnp.exp(b_last), eb=eb, eq=eq, ekh=ekh, ek=ek,
                            q_til=qh * eb, k_hat=kh * ekh, qhb=q_hat.astype(BF16), ktb=k_til.astype(BF16),
                            vb=v[:, sl].astype(BF16), dob=do.astype(BF16)))

        first = []
        for sl, t in zip(heads, fac):
            st_h = st_ref[0, sl, :]
            dst_h = dst[sl, :]
            dstb = dst_h.astype(BF16)
            first.append(dict(
                att_t=_dot_nt(t["ktb"], t["qhb"]), datt=_dot_nt(t["dob"], t["vb"]),
                datt_t=_dot_nt(t["vb"], t["dob"]), dk_hat=_dot(t["vb"], dstb),
                dv=_dot_nt(t["k_hat"].astype(BF16), dstb), dq_til=_dot(t["dob"], st_h.astype(BF16)),
                state=t["e_last"] * jnp.sum(dst_h * st_h, axis=0, keepdims=True)))
            dst[sl, :] = dst_h * t["e_last"] + _dot(do_t[sl, :], t["q_til"].astype(BF16))

        for h, (t, m) in enumerate(zip(fac, first)):
            qh, kh, eb, eq, ekh, ek = t["qh"], t["kh"], t["eb"], t["eq"], t["ekh"], t["ek"]
            q_til, k_hat, qhb, ktb, dob = t["q_til"], t["k_hat"], t["qhb"], t["ktb"], t["dob"]
            dk_hat, dq_til = m["dk_hat"], m["dq_til"]
            dv = m["dv"] + _dot(jnp.where(causal_t, m["att_t"], 0.0).astype(BF16), dob)
            dq_hat = _dot(jnp.where(causal, m["datt"], 0.0).astype(BF16), ktb)
            dk_til = _dot(jnp.where(causal_t, m["datt_t"], 0.0).astype(BF16), qhb)
            db_last = jnp.sum(dk_hat * k_hat, axis=0, keepdims=True) + m["state"]
            dq_sel = jnp.concatenate([dq_hat[SUB * s:SUB * (s + 1), s * HG_HEAD_DIM:(s + 1) * HG_HEAD_DIM]
                                      for s in range(N_SUB)], axis=0)
            dq_a = dq_sel * eq
            dk_rows, k_att_rows = [], []
            for b_ in range(N_SUB):
                rs = slice(SUB * b_, SUB * (b_ + 1))
                dk_sum = k_att_sum = None
                for s in range(b_, N_SUB):
                    cs = slice(s * HG_HEAD_DIM, (s + 1) * HG_HEAD_DIM)
                    d = dk_til[rs, cs]
                    t_dk = d * ek[s][rs, :]
                    t_att = ktb[rs, cs].astype(F32) * d
                    dk_sum = t_dk if dk_sum is None else dk_sum + t_dk
                    k_att_sum = t_att if k_att_sum is None else k_att_sum + t_att
                dk_rows.append(dk_sum)
                k_att_rows.append(k_att_sum)
            dk_a = jnp.concatenate(dk_rows, axis=0)
            db = (dq_til * q_til - dk_hat * k_hat + (qh * eq).astype(BF16).astype(F32) * dq_sel
                  - jnp.concatenate(k_att_rows, axis=0))
            db_parts.append(jnp.where(is_last, db + db_last, db))
            dq_parts.append(dq_til * eb + dq_a)
            dk_parts.append(dk_hat * ekh + dk_a)
            dp_ref[:, 2 * D_HG + h * HG_HEAD_DIM:2 * D_HG + (h + 1) * HG_HEAD_DIM] = dv.astype(BF16)

        gg_ref[...] += gg
        db = jnp.concatenate(db_parts, axis=1)
        dq = jnp.concatenate(dq_parts, axis=1)
        dk = jnp.concatenate(dk_parts, axis=1)
        dlf = jnp.where(valid, _running_sum(db, False), 0.0)
        dp_ref[:, :D_HG] = (dq * sq * (1.0 + hq * (1.0 - sq))).astype(BF16)
        df = dlf / f - dk
        dlb = jnp.sum(df * (1.0 - sf), axis=0, keepdims=True) * lb * (1.0 - lb)
        glb_ref[0:1, :] += dlb
        glb_ref[1:2, :] += -dlb
        dp_ref[:, D_HG:2 * D_HG] = (df * (1.0 - lb) * sf * (1.0 - sf)).astype(BF16)

    rev = lambda j: pl.BlockSpec((rows, D_HG), lambda i: (n_steps - 1 - i, j))
    return pl.pallas_call(
        body, grid=(n_steps,),
        in_specs=[rev(2), rev(3), rev(4), rev(5), rev(0),
                  pl.BlockSpec((cps, D_HG, HG_HEAD_DIM), lambda i: (n_steps - 1 - i, 0, 0)), rev(1),
                  pl.BlockSpec((2, D_HG), lambda i: (0, 0)), pl.BlockSpec((1, HG_HEAD_DIM), lambda i: (0, 0))],
        out_specs=[pl.BlockSpec((rows, D_IN), lambda i: (n_steps - 1 - i, 0)),
                   pl.BlockSpec((2, D_HG), lambda i: (0, 0)), pl.BlockSpec((1, HG_HEAD_DIM), lambda i: (0, 0))],
        out_shape=[jax.ShapeDtypeStruct((T, D_IN), BF16), jax.ShapeDtypeStruct((2, D_HG), F32),
                   jax.ShapeDtypeStruct((1, HG_HEAD_DIM), F32)],
        scratch_shapes=[pltpu.VMEM((D_HG, HG_HEAD_DIM), F32)],
        name="hg_bwd", compiler_params=_params("arbitrary"),
    )(p, p, p, p, o_all, st_all, dy, lbraw, hg_g)


def _in_bwd(dp, w_in, h0, g1, dh1):
    T = h0.shape[0]
    tm = _row_tile(T, 832)
    n_steps = T // tm

    def body(dp_ref, w_ref, h_ref, g_ref, d1_ref, gx_hbm, gmeta_ref, gg_ref, buf, sems):
        i = pl.program_id(0)
        first, later = _window_copies(gx_hbm, buf, sems, tm)
        slot = i % 2

        @pl.when(i == 0)
        def _():
            gg_ref[...] = jnp.zeros_like(gg_ref)

        if n_steps > 2:
            @pl.when(i == 2)
            def _():
                first(False).wait()

            @pl.when(i > 2)
            def _():
                later(i - 2, slot, False).wait()

        du = _dot_nt(dp_ref[...], w_ref[...])
        h0_ = h_ref[...]
        r = _rms(h0_)
        n = h0_ * r
        gg_ref[...] += jnp.sum(du * n, axis=0, keepdims=True)
        dh0 = d1_ref[...] + _rms_bwd(du * g_ref[...], n, r)
        buf[slot] = dh0

        @pl.when(i == 0)
        def _():
            gmeta_ref[...] = dh0[PAD:HEAD, :]
            first(False).start()

        if n_steps > 1:
            @pl.when(i > 0)
            def _():
                later(i, slot, False).start()

        @pl.when(i == n_steps - 1)
        def _():
            if n_steps == 1:
                first(False).wait()
            else:
                if n_steps == 2:
                    first(False).wait()
                else:
                    later(i - 1, 1 - slot, False).wait()
                later(i, slot, False).wait()

    row = lambda n: pl.BlockSpec((tm, n), lambda i: (i, 0))
    return pl.pallas_call(
        body, grid=(n_steps,),
        in_specs=[row(D_IN), _resident((D_MODEL, D_IN)),
                  row(D_MODEL), pl.BlockSpec((1, D_MODEL), lambda i: (0, 0)), row(D_MODEL)],
        out_specs=[pl.BlockSpec(memory_space=pl.ANY), pl.BlockSpec((N_META, D_MODEL), lambda i: (0, 0)),
                   pl.BlockSpec((1, D_MODEL), lambda i: (0, 0))],
        out_shape=[jax.ShapeDtypeStruct((T - HEAD, D_MODEL), F32), jax.ShapeDtypeStruct((N_META, D_MODEL), F32),
                   jax.ShapeDtypeStruct((1, D_MODEL), F32)],
        scratch_shapes=[pltpu.VMEM((2, tm, D_MODEL), F32), pltpu.SemaphoreType.DMA((2,))],
        name="in_bwd", compiler_params=_params("arbitrary"),
    )(dp, w_in, h0, g1, dh1)


def _col_tile(cols, target):
    best = None
    for t in range(128, min(cols, target) + 1, 128):
        if cols % t == 0:
            best = t
    assert best is not None, cols
    return best


MXU_DIM = 256


def _mxu_tile(cols, target):
    best = None
    for t in range(MXU_DIM, min(cols, target) + 1, MXU_DIM):
        if cols % t == 0:
            best = t
    assert best is not None, cols
    return best


def _weight_grad(a, b, name):
    T, M = a.shape
    N = b.shape[1]
    tm = _col_tile(M, 1408)
    tn = _mxu_tile(N, 768 if tm <= 1024 else 512)

    def body(a_ref, b_ref, o_ref, ob_ref):
        o = _dot_tn(a_ref[...], b_ref[...])
        o_ref[...] = o
        ob_ref[...] = o.astype(BF16)

    return pl.pallas_call(
        body, grid=(M // tm, N // tn),
        in_specs=[pl.BlockSpec((T, tm), lambda m, n: (0, m)), pl.BlockSpec((T, tn), lambda m, n: (0, n))],
        out_specs=[pl.BlockSpec((tm, tn), lambda m, n: (m, n))] * 2,
        out_shape=[jax.ShapeDtypeStruct((M, N), F32), jax.ShapeDtypeStruct((M, N), BF16)],
        name=name, compiler_params=_params("parallel", "parallel"),
    )(a, b)


def _weight_grad_chip_sum(a, b, name):
    T, M = a.shape
    N = b.shape[1]
    tn = _mxu_tile(N, 768)
    steps = N // tn
    half = M // 2

    def body(a_ref, b_ref, sum_ref, sumb_ref, acc, own, got, stage, send_sems, recv_sems):
        n = pl.program_id(0)
        x, y, c = _place()
        slot = n % 2

        def piece(k, s):
            return _remote(stage.at[s], got.at[k], send_sems, recv_sems, k, (x, y, 1 - c))

        @pl.when(n < steps)
        def _():
            acc[...] = _dot_tn(a_ref[...], b_ref[...])

            @pl.when(n >= 2)
            def _():
                piece(n - 2, slot).wait_send()

            stage[slot] = acc[pl.ds(pl.multiple_of((1 - c) * half, 128), half), :].astype(BF16)
            piece(n, slot).start()

        @pl.when(n >= 1)
        def _():
            piece(n - 1, 1 - slot).wait_recv()
            t = own[1 - slot] + got[n - 1].astype(F32)
            sum_ref[...] = t
            sumb_ref[...] = t.astype(BF16)

        @pl.when(n < steps)
        def _():
            own[slot] = acc[pl.ds(pl.multiple_of(c * half, 128), half), :]

        @pl.when(n == steps)
        def _():
            for k in range(max(steps - 2, 0), steps):
                piece(k, k % 2).wait_send()

    last = steps - 1
    return pl.pallas_call(
        body, grid=(steps + 1,),
        in_specs=[_resident((T, M)), pl.BlockSpec((T, tn), lambda n: (0, jnp.minimum(n, last)))],
        out_specs=[pl.BlockSpec((half, tn), lambda n: (0, jnp.maximum(n - 1, 0)))] * 2,
        out_shape=[jax.ShapeDtypeStruct((half, N), F32), jax.ShapeDtypeStruct((half, N), BF16)],
        scratch_shapes=[pltpu.VMEM((M, tn), F32), pltpu.VMEM((2, half, tn), F32), pltpu.VMEM((steps, half, tn), BF16),
                        pltpu.VMEM((2, half, tn), BF16), pltpu.SemaphoreType.DMA((steps,)),
                        pltpu.SemaphoreType.DMA((steps,))],
        name=name, compiler_params=_params("arbitrary"),
    )(a, b)


def _local_step(x, meta, target, w_in_own, w_in, w_out, w_gu, w_down, small, chip, on_ffn_grads=None,
                on_mixer_grads=None):
    wg = _gate_weights(small["w_rgate"], small["w_igate"])
    bg = jnp.concatenate([small["b_rgate"], small["b_igate"]], axis=1)

    p, u, h0 = _in_proj_local(x, meta, small["mix_norm_g"], w_in_own, chip)
    p = _in_proj_rest(u, w_in, p, chip)
    y_rg, hs, xc = _rg_fwd(p, small["conv_w"], small["conv_b"], wg, bg, small["lru_lambda"], small["rg_norm_g"])
    y_hg, o_all, st_all = _hg_fwd(p, small["hg_lower_bound"], small["hg_norm_g"])
    h1, v, yb, gu, act, dh2, dh2b, loss, g_final = _ffn_fwd(
        h0, y_rg, y_hg, w_out, small["ffn_norm_g"], w_gu, w_down, small["final_norm_g"], target)

    g_w_down = _weight_grad(act, dh2b, "grad_w_down")
    dgu, dh1, dh1b, dy, g_ffn = _ffn_bwd(dh2b, gu, w_down, w_gu, h1, small["ffn_norm_g"], dh2, w_out)
    ffn_grads = {"w_down": g_w_down, "w_out": _weight_grad(yb, dh1b, "grad_w_out")}
    if on_ffn_grads is None:
        ffn_grads["w_gate_up"] = _weight_grad(v, dgu, "grad_w_gate_up")
        stages = None
    else:
        gate_up_sums = _weight_grad_chip_sum(v, dgu, "grad_w_gate_up")
        ffn_grads["w_gate_up"] = (None, None)
        stages = on_ffn_grads(ffn_grads)
    dp, g_lb, g_hgn = _hg_bwd(p, o_all, st_all, dy, small["hg_lower_bound"], small["hg_norm_g"])
    early = late = None
    if stages is not None:
        chip_sums, send = stages
        sums = dict(chip_sums(), w_gate_up=gate_up_sums)
        (dp, dy), sums = lax.optimization_barrier(((dp, dy), sums))
        early = send(sums)
    dp, g_cw, g_cb, g_wgate, g_bg, g_lam, g_rgn = _rg_bwd(
        p, xc, hs, dy, dp, small["conv_w"], small["conv_b"], wg, bg, small["lru_lambda"], small["rg_norm_g"])
    if on_mixer_grads is None:
        g_w_in = _weight_grad(u, dp, "grad_w_in")[0]
    else:
        sums = {"w_in": _weight_grad_chip_sum(u, dp, "grad_w_in")}
        (dp, dh1), sums = lax.optimization_barrier(((dp, dh1), sums))
        late = on_mixer_grads(sums)
        g_w_in = None
    grad_x, g_meta, g_mix = _in_bwd(dp, w_in, h0, small["mix_norm_g"], dh1)

    grads = {
        "w_in": g_w_in, "w_out": ffn_grads["w_out"][0],
        "w_gate_up": ffn_grads["w_gate_up"][0], "w_down": ffn_grads["w_down"][0],
        "meta_tokens": g_meta, "mix_norm_g": g_mix, "conv_w": g_cw, "conv_b": g_cb, "w_gates": g_wgate,
        "b_rgate": g_bg[:, :D_RG], "b_igate": g_bg[:, D_RG:], "lru_lambda": g_lam, "rg_norm_g": g_rgn,
        "hg_lower_bound": g_lb, "hg_norm_g": g_hgn, "ffn_norm_g": g_ffn, "final_norm_g": g_final,
    }
    return loss, grad_x, grads, early, late


ANY = pl.BlockSpec(memory_space=pl.ANY)
HALF = D_MODEL // 2

BIG = {"w_in": (D_MODEL, D_IN // N_CHIPS, True), "w_gate_up": (D_MODEL, 2 * D_FF // N_CHIPS, True),
       "w_out": (D_MODEL // N_CHIPS, D_MODEL, False), "w_down": (D_FF // N_CHIPS, D_MODEL, False)}
BIG_NAMES = tuple(BIG)
N_BIG = len(BIG_NAMES)


def _full_shape(name):
    rows, cols, by_col = BIG[name]
    return (rows, cols * N_CHIPS) if by_col else (rows * N_CHIPS, cols)


def _place():
    return lax.axis_index("x"), lax.axis_index("y"), lax.axis_index("c")


def _chip_of(x, y, r):
    fx, fy = (r + 1) >> 1, (r + 1) & 1
    return (1 - x if fx else x), (1 - y if fy else y)


def _half_of(ref, by_col, half):
    start = pl.multiple_of(half * HALF, 128)
    return ref.at[pl.ds(start, HALF), :] if by_col else ref.at[:, pl.ds(start, HALF)]


def _shard_of(ref, name, chip):
    rows, cols, by_col = BIG[name]
    if by_col:
        return ref.at[:, pl.ds(pl.multiple_of(chip * cols, 128), cols)]
    return ref.at[pl.ds(pl.multiple_of(chip * rows, 16), rows), :]


def _shard_half_of(ref, name, chip, half):
    rows, cols, by_col = BIG[name]
    start = pl.multiple_of(half * HALF, 128)
    if by_col:
        return ref.at[pl.ds(start, HALF), pl.ds(pl.multiple_of(chip * cols, 128), cols)]
    return ref.at[pl.ds(pl.multiple_of(chip * rows, 16), rows), pl.ds(start, HALF)]


def _shard_half_part_of(ref, name, chip, half, part):
    rows, cols, by_col = BIG[name]
    start = pl.multiple_of(half * HALF + part * (HALF // 2), 128)
    if by_col:
        return ref.at[pl.ds(start, HALF // 2), pl.ds(pl.multiple_of(chip * cols, 128), cols)]
    return ref.at[pl.ds(pl.multiple_of(chip * rows, 16), rows), pl.ds(start, HALF // 2)]


def _remote(src, dst, send_sems, recv_sems, k, dev):
    return pltpu.make_async_remote_copy(src_ref=src, dst_ref=dst, send_sem=send_sems.at[k], recv_sem=recv_sems.at[k],
                                        device_id=dev, device_id_type=MESH)


def _place_shards(w, small, chip, names, label):
    steps = 4
    n, ns = len(names), len(small)
    in_specs, out_specs = [], []
    for name in names:
        rows, cols, by_col = BIG[name]
        tr = rows // steps
        in_specs.append(pl.BlockSpec((tr, cols), lambda i, s: (i, 0)))
        if by_col:
            out_specs.append(pl.BlockSpec((tr, cols), lambda i, s: (i, s[0])))
        else:
            out_specs.append(pl.BlockSpec((tr, cols), lambda i, s: (s[0] * steps + i, 0)))

    def body(s_ref, *refs):
        ins, small_in = refs[:n], refs[n:n + ns]
        outs, small_out = refs[n + ns:2 * n + ns], refs[2 * n + ns:2 * (n + ns)]
        send_sems, recv_sems, local_sems = refs[2 * (n + ns):]
        i = pl.program_id(0)
        x, y, c = _place()
        chip_ = 2 * x + y
        others = [_chip_of(x, y, r) for r in range(3)]

        def block(a, q):
            cols = small[a].shape[1]
            return small_out[a].at[:, pl.ds(pl.multiple_of(q * cols, 128), cols)]

        def local(a):
            return pltpu.make_async_copy(small_in[a], block(a, chip_), local_sems.at[a])

        def remote(a, r):
            qx, qy = others[r]
            return _remote(small_in[a], block(a, chip_), send_sems, recv_sems, 3 * a + r, (qx, qy, c))

        @pl.when(i == 0)
        def _():
            for a in range(ns):
                local(a).start()
                for r in range(3):
                    remote(a, r).start()

        for a in range(n):
            outs[a][...] = ins[a][...].astype(BF16)

        @pl.when(i == steps - 1)
        def _():
            for a in range(ns):
                for r, (qx, qy) in enumerate(others):
                    landed = block(a, 2 * qx + qy)
                    _remote(landed, landed, send_sems, recv_sems, 3 * a + r, (qx, qy, c)).wait_recv()
                for r in range(3):
                    remote(a, r).wait_send()
                local(a).wait()

    out = pl.pallas_call(
        body,
        grid_spec=pltpu.PrefetchScalarGridSpec(
            num_scalar_prefetch=1, grid=(steps,), in_specs=in_specs + [ANY] * ns, out_specs=out_specs + [ANY] * ns,
            scratch_shapes=[pltpu.SemaphoreType.DMA((max(3 * ns, 1),)), pltpu.SemaphoreType.DMA((max(3 * ns, 1),)),
                            pltpu.SemaphoreType.DMA((max(ns, 1),))]),
        out_shape=([jax.ShapeDtypeStruct(_full_shape(name), BF16) for name in names]
                   + [jax.ShapeDtypeStruct((s.shape[0], s.shape[1] * N_CHIPS), F32) for s in small]),
        name=label, compiler_params=_params("arbitrary"),
    )(chip, *[w[name] for name in names], *small)
    return dict(zip(names, out[:n])), list(out[n:])


def _gather_weights(placed, small, names, label, collective_id):
    n, ns = len(names), len(small)
    hbm = pltpu.MemorySpace.HBM
    outs = [jax.new_ref(placed[nm], memory_space=hbm) for nm in names]
    small_in = [jax.new_ref(s, memory_space=hbm) for s in small]
    small_out = [jax.empty_ref(jax.ShapeDtypeStruct((s.shape[0], s.shape[1] * N_CHIPS), F32), memory_space=hbm)
                 for s in small]
    n_sems = 8 * n + 3 * ns

    @pl.kernel(mesh=plsc.ScalarSubcoreMesh(axis_name="seq", num_cores=1), name=label, out_type=(),
               scratch_types=(pltpu.SemaphoreType.DMA((n_sems,)), pltpu.SemaphoreType.DMA((n_sems,)),
                              pltpu.SemaphoreType.DMA((max(ns, 1),))),
               compiler_params=pltpu.CompilerParams(collective_id=collective_id))
    def launch(send_sems, recv_sems, local_sems):
        x, y, c = _place()
        chip = 2 * x + y
        sibling = (x, y, 1 - c)
        others = [_chip_of(x, y, r) for r in range(3)]
        near = others[:2]
        far = 2 * others[2][0] + others[2][1]
        _handshake([(qx, qy, c) for qx, qy in others] + [sibling])

        def small_block(a, q):
            cols = small[a].shape[1]
            return small_out[a].at[:, pl.ds(pl.multiple_of(q * cols, 128), cols)]

        local = [pltpu.make_async_copy(small_in[a], small_block(a, chip), local_sems.at[a]) for a in range(ns)]
        for cp in local:
            cp.start()

        sends = []
        for a, name in enumerate(names):
            mine = _shard_half_of(outs[a], name, chip, c)
            for r, (qx, qy) in enumerate(near):
                sends.append(_remote(mine, mine, send_sems, recv_sems, 8 * a + r, (qx, qy, c)))
        for a in range(ns):
            for r, (qx, qy) in enumerate(others):
                sends.append(_remote(small_in[a], small_block(a, chip), send_sems, recv_sems,
                                     8 * n + 3 * a + r, (qx, qy, c)))
        for cp in sends:
            cp.start()

        forwards = []

        def forward(piece, k, dev):
            cp = _remote(piece, piece, send_sems, recv_sems, k, dev)
            cp.start()
            forwards.append(cp)

        for a, name in enumerate(names):
            for r, (qx, qy) in enumerate(near):
                landed = _shard_half_of(outs[a], name, 2 * qx + qy, c)
                _remote(landed, landed, send_sems, recv_sems, 8 * a + r, (qx, qy, c)).wait_recv()
                ox, oy = near[1 - r]
                forward(_shard_half_part_of(outs[a], name, 2 * qx + qy, c, r), 8 * a + 2 + r, (ox, oy, c))
                forward(landed, 8 * a + 4 + r, sibling)
        for a, name in enumerate(names):
            for part in range(2):
                qx, qy = near[1 - part]
                landed = _shard_half_part_of(outs[a], name, far, c, part)
                _remote(landed, landed, send_sems, recv_sems, 8 * a + 2 + part, (qx, qy, c)).wait_recv()
                forward(landed, 8 * a + 6 + part, sibling)
        for a in range(ns):
            for r, (qx, qy) in enumerate(others):
                landed = small_block(a, 2 * qx + qy)
                _remote(landed, landed, send_sems, recv_sems, 8 * n + 3 * a + r, (qx, qy, c)).wait_recv()
        for a, name in enumerate(names):
            for r, (qx, qy) in enumerate(near):
                landed = _shard_half_of(outs[a], name, 2 * qx + qy, 1 - c)
                _remote(landed, landed, send_sems, recv_sems, 8 * a + 4 + r, sibling).wait_recv()
            for part in range(2):
                landed = _shard_half_part_of(outs[a], name, far, 1 - c, part)
                _remote(landed, landed, send_sems, recv_sems, 8 * a + 6 + part, sibling).wait_recv()
        for cp in sends + forwards:
            cp.wait_send()
        for cp in local:
            cp.wait()

    launch()
    return {nm: ref[...] for nm, ref in zip(names, outs)}, [ref[...] for ref in small_out]


def _exchange_halves(grads, names, label, collective_id):
    n = len(names)

    def body(*refs):
        ins, outs = refs[:n], refs[n:2 * n]
        send_sems, recv_sems = refs[2 * n:]
        x, y, c = _place()
        _handshake([(x, y, 1 - c)])
        copies = []
        for a, name in enumerate(names):
            copies.append(_remote(_half_of(ins[a], BIG[name][2], 1 - c), outs[a], send_sems, recv_sems, a,
                                  (x, y, 1 - c)))
        for cp in copies:
            cp.start()
        for cp in copies:
            cp.wait()

    def half_shape(name):
        r, c_ = _full_shape(name)
        return (HALF, c_) if BIG[name][2] else (r, HALF)

    out_type = tuple(jax.ShapeDtypeStruct(half_shape(nm), grads[nm].dtype) for nm in names)
    sems = (pltpu.SemaphoreType.DMA((n,)), pltpu.SemaphoreType.DMA((n,)))
    got = pl.kernel(
        body, mesh=plsc.ScalarSubcoreMesh(axis_name="seq", num_cores=1), name=label, out_type=out_type,
        scratch_types=sems, compiler_params=pltpu.CompilerParams(collective_id=collective_id),
    )(*[grads[nm] for nm in names])
    return dict(zip(names, got))


def _chip_sum(grads, got, names, core, label):
    n = len(names)
    steps = 4
    g_specs, blks = [], []
    for name in names:
        rows, cols = got[name].shape
        tr = rows // steps
        if BIG[name][2]:
            g_specs.append(pl.BlockSpec((tr, cols), lambda i, s: (s[0] * steps + i, 0)))
        else:
            g_specs.append(pl.BlockSpec((tr, HALF), lambda i, s: (i, s[0])))
        blks.append(pl.BlockSpec((tr, cols), lambda i, s: (i, 0)))

    def body(s_ref, *refs):
        for a in range(n):
            t = refs[a][...] + refs[n + a][...].astype(F32)
            refs[2 * n + a][...] = t
            refs[3 * n + a][...] = t.astype(BF16)

    out = pl.pallas_call(
        body,
        grid_spec=pltpu.PrefetchScalarGridSpec(num_scalar_prefetch=1, grid=(steps,), in_specs=g_specs + blks,
                                               out_specs=blks + blks),
        out_shape=([jax.ShapeDtypeStruct(got[nm].shape, F32) for nm in names]
                   + [jax.ShapeDtypeStruct(got[nm].shape, BF16) for nm in names]),
        name=label, compiler_params=_params("parallel"),
    )(core, *[grads[nm] for nm in names], *[got[nm] for nm in names])
    return {nm: (out[a], out[n + a]) for a, nm in enumerate(names)}


def _piece_shape(name):
    rows, cols, by_col = BIG[name]
    return (HALF, cols) if by_col else (rows, HALF)


def _handshake(peers):
    barrier = pltpu.get_barrier_semaphore()
    for peer in peers:
        pl.semaphore_signal(barrier, inc=1, device_id=peer, device_id_type=MESH)
    pl.semaphore_wait(barrier, len(peers))


def _send_chip_sums(sums, names, label, collective_id):
    n = len(names)

    def body(*refs):
        ins, outs = refs[:n], refs[n:2 * n]
        send_sems, recv_sems = refs[2 * n:]
        x, y, c = _place()
        others = [_chip_of(x, y, r) for r in range(3)]
        _handshake([(qx, qy, c) for qx, qy in others])
        copies = []
        for a, name in enumerate(names):
            for r, (qx, qy) in enumerate(others):
                copies.append(_remote(_shard_of(ins[a], name, 2 * qx + qy), outs[a].at[r], send_sems, recv_sems,
                                      3 * a + r, (qx, qy, c)))
        for cp in copies:
            cp.start()
        for cp in copies:
            cp.wait()

    return pl.kernel(
        body, mesh=plsc.ScalarSubcoreMesh(axis_name="seq", num_cores=1), name=label,
        out_type=tuple(jax.ShapeDtypeStruct((3,) + _piece_shape(nm), BF16) for nm in names),
        scratch_types=(pltpu.SemaphoreType.DMA((3 * n,)), pltpu.SemaphoreType.DMA((3 * n,))),
        compiler_params=pltpu.CompilerParams(collective_id=collective_id),
    )(*[sums[nm] for nm in names])


def _total(parts, chip_core):
    steps = 2
    in_specs, out_specs, operands = [], [], []
    for name in BIG_NAMES:
        by_col = BIG[name][2]
        pr, pc = _piece_shape(name)
        tr = pr // steps
        if by_col:
            in_specs.append(pl.BlockSpec((tr, pc), lambda i, s: (i, s[0])))
            out_specs.append(pl.BlockSpec((tr, pc), lambda i, s: (s[1] * steps + i, 0)))
        else:
            in_specs.append(pl.BlockSpec((tr, pc), lambda i, s: (s[0] * steps + i, 0)))
            out_specs.append(pl.BlockSpec((tr, pc), lambda i, s: (i, s[1])))
        for r in range(3):
            in_specs.append(pl.BlockSpec((None, tr, pc), lambda i, s, r=r: (r, i, 0)))
        own, got = parts[name]
        operands += [own, got, got, got]

    def body(s_ref, *refs):
        for a in range(N_BIG):
            o_ref, a_ref, b_ref, c_ref = refs[4 * a:4 * a + 4]
            refs[4 * N_BIG + a][...] = (((o_ref[...] + a_ref[...].astype(F32)) + b_ref[...].astype(F32))
                                        + c_ref[...].astype(F32))

    totals = pl.pallas_call(
        body,
        grid_spec=pltpu.PrefetchScalarGridSpec(num_scalar_prefetch=1, grid=(steps,), in_specs=in_specs,
                                               out_specs=out_specs),
        out_shape=[jax.ShapeDtypeStruct(BIG[name][:2], F32) for name in BIG_NAMES],
        name="totals", compiler_params=_params("parallel"),
    )(chip_core, *operands)
    return dict(zip(BIG_NAMES, totals))


VEC_ROWS = 32
VEC_ROW = {"mix_norm_g": 0, "conv_b": 1, "b_rgate": 2, "b_igate": 3, "lru_lambda": 4, "rg_norm_g": 5,
           "hg_lower_bound": 6, "hg_norm_g": 8, "ffn_norm_g": 9, "final_norm_g": 10, "loss": 11,
           "conv_w": 12, "meta_tokens": 16}
N_DEV = 8


def _all_reduce_small(pieces, gates, totals):
    names = list(pieces)
    n_small = 10
    hv, hg = VEC_ROWS // 2, gates.shape[0] // 2

    def body(*refs):
        ins = refs[:len(names)]
        g_ref = refs[len(names)]
        vec_ref, gsum_ref = refs[len(names) + 1 + N_BIG:len(names) + 3 + N_BIG]
        big = refs[len(names) + 3 + N_BIG:len(names) + 3 + 2 * N_BIG]
        (mine_v, sib_v, sib_g, chip_v, chip_g, got_v, got_g, send_sems, recv_sems) = refs[len(names) + 3 + 2 * N_BIG:]
        x, y, c = _place()
        chip = 2 * x + y
        sibling = (x, y, 1 - c)
        share = []
        for a, name in enumerate(BIG_NAMES):
            half = _half_of(big[a], BIG[name][2], c)
            share.append(_remote(half, half, send_sems, recv_sems, n_small + a, sibling))
        mine_v[...] = jnp.zeros_like(mine_v)
        for name, ref in zip(names, ins):
            nr, w = ref.shape
            mine_v[VEC_ROW[name]:VEC_ROW[name] + nr, 0:w] = ref[...]

        swap = [_remote(mine_v, sib_v, send_sems, recv_sems, 0, sibling),
                _remote(g_ref, sib_g, send_sems, recv_sems, 1, sibling)]
        for cp in swap:
            cp.start()
        for cp in swap:
            cp.wait()
        for cp in share:
            cp.start()
        chip_v[...] = mine_v[...] + sib_v[...]
        chip_g[...] = g_ref[...] + sib_g[...]

        rows_v = pl.ds(pl.multiple_of(c * hv, 8), hv)
        rows_g = pl.ds(pl.multiple_of(c * hg, 8), hg)
        got_v[chip] = chip_v[rows_v, :]
        got_g[chip] = chip_g[rows_g, :].astype(BF16)
        sends = []
        for r in range(3):
            qx, qy = _chip_of(x, y, r)
            sends.append(_remote(chip_v.at[rows_v, :], got_v.at[chip], send_sems, recv_sems, 2 + r, (qx, qy, c)))
            sends.append(_remote(got_g.at[chip], got_g.at[chip], send_sems, recv_sems, 5 + r, (qx, qy, c)))
        for cp in sends:
            cp.start()
        for cp in sends:
            cp.wait()
        vec_ref[rows_v, :] = ((got_v[0] + got_v[1]) + got_v[2]) + got_v[3]
        gsum_ref[rows_g, :] = ((got_g[0].astype(F32) + got_g[1].astype(F32)) + got_g[2].astype(F32)
                               ) + got_g[3].astype(F32)

        back = [_remote(vec_ref.at[rows_v, :], vec_ref.at[rows_v, :], send_sems, recv_sems, 8, sibling),
                _remote(gsum_ref.at[rows_g, :], gsum_ref.at[rows_g, :], send_sems, recv_sems, 9, sibling)]
        for cp in back:
            cp.start()
        theirs_v = vec_ref.at[pl.ds(pl.multiple_of((1 - c) * hv, 8), hv), :]
        theirs_g = gsum_ref.at[pl.ds(pl.multiple_of((1 - c) * hg, 8), hg), :]
        _remote(theirs_v, theirs_v, send_sems, recv_sems, 8, sibling).wait_recv()
        _remote(theirs_g, theirs_g, send_sems, recv_sems, 9, sibling).wait_recv()
        for cp in back:
            cp.wait_send()
        for a, name in enumerate(BIG_NAMES):
            theirs = _half_of(big[a], BIG[name][2], 1 - c)
            _remote(theirs, theirs, send_sems, recv_sems, n_small + a, sibling).wait_recv()
        for cp in share:
            cp.wait_send()

    vmem = pl.BlockSpec(memory_space=pltpu.VMEM)
    n_sems = n_small + N_BIG
    out = pl.pallas_call(
        body, in_specs=[vmem] * (len(names) + 1) + [ANY] * N_BIG, out_specs=[vmem, vmem] + [ANY] * N_BIG,
        out_shape=([jax.ShapeDtypeStruct((VEC_ROWS, D_MODEL), F32), jax.ShapeDtypeStruct(gates.shape, F32)]
                   + [jax.ShapeDtypeStruct(BIG[n][:2], F32) for n in BIG_NAMES]),
        input_output_aliases={len(names) + 1 + a: 2 + a for a in range(N_BIG)},
        scratch_shapes=[pltpu.VMEM((VEC_ROWS, D_MODEL), F32), pltpu.VMEM((VEC_ROWS, D_MODEL), F32),
                        pltpu.VMEM(gates.shape, F32), pltpu.VMEM((VEC_ROWS, D_MODEL), F32),
                        pltpu.VMEM(gates.shape, F32), pltpu.VMEM((N_CHIPS, hv, D_MODEL), F32),
                        pltpu.VMEM((N_CHIPS, hg) + gates.shape[1:], BF16),
                        pltpu.SemaphoreType.DMA((n_sems,)), pltpu.SemaphoreType.DMA((n_sems,))],
        name="all_reduce_small",
    )(*[pieces[n] for n in names], gates, *[totals[n] for n in BIG_NAMES])
    return out[0], out[1], dict(zip(BIG_NAMES, out[2:]))


def _adamw_math(w, g, m, v):
    m = ADAM_B1 * m + (1.0 - ADAM_B1) * g
    v = ADAM_B2 * v + (1.0 - ADAM_B2) * (g * g)
    m_hat = m / (1.0 - ADAM_B1 ** ADAM_STEP)
    v_hat = v / (1.0 - ADAM_B2 ** ADAM_STEP)
    delta = -ADAM_LR * (m_hat / (jnp.sqrt(v_hat) + ADAM_EPS) + ADAM_WD * w)
    return delta, m, v


def _adamw_big(w, g, m, v):
    steps = 8
    blks = []
    for name in BIG_NAMES:
        rows, cols, _ = BIG[name]
        blks.append(pl.BlockSpec((rows // steps, cols), lambda i: (i, 0)))

    def body(*refs):
        ins, outs = refs[:4 * N_BIG], refs[4 * N_BIG:]
        for a in range(N_BIG):
            w_ref, g_ref, m_ref, v_ref = (ins[k * N_BIG + a] for k in range(4))
            g = g_ref[...]
            d, nm, nv = _adamw_math(w_ref[...], g, m_ref[...], v_ref[...])
            outs[a][...] = g
            outs[N_BIG + a][...] = d
            outs[2 * N_BIG + a][...] = nm
            outs[3 * N_BIG + a][...] = nv

    shapes = [jax.ShapeDtypeStruct(BIG[name][:2], F32) for name in BIG_NAMES]
    out = pl.pallas_call(
        body, grid=(steps,), in_specs=blks * 4, out_specs=blks * 4, out_shape=shapes * 4,
        name="adamw_big", compiler_params=_params("parallel"),
    )(*[t[name] for t in (w, g, m, v) for name in BIG_NAMES])
    return {name: tuple(out[k * N_BIG + a] for k in range(4)) for a, name in enumerate(BIG_NAMES)}


SMALL = {"meta_tokens": (N_META, D_MODEL // N_CHIPS), "mix_norm_g": (1, D_MODEL), "conv_w": (CONV_W, D_RG // N_CHIPS),
         "conv_b": (1, D_RG), "w_rgate": (D_RG, RG_HEAD_DIM), "b_rgate": (1, D_RG), "w_igate": (D_RG, RG_HEAD_DIM),
         "b_igate": (1, D_RG), "lru_lambda": (1, D_RG), "rg_norm_g": (1, D_RG), "hg_lower_bound": (2, D_HG),
         "hg_norm_g": (1, HG_HEAD_DIM), "ffn_norm_g": (1, D_MODEL), "final_norm_g": (1, D_MODEL)}
SMALL_NAMES = tuple(SMALL)
SHARDED_SMALL = ("meta_tokens", "conv_w")


def _adamw_small(vec, gates, w, m, v):
    n = len(SMALL_NAMES)

    def body(*refs):
        vec_ref, gates_ref = refs[:2]
        w_refs, m_refs, v_refs = refs[2:2 + n], refs[2 + n:2 + 2 * n], refs[2 + 2 * n:2 + 3 * n]
        outs = refs[2 + 3 * n:]
        loss_ref = outs[0]
        x, y, _ = _place()
        chip = 2 * x + y
        loss_ref[...] = vec_ref[VEC_ROW["loss"]:VEC_ROW["loss"] + 1, 0:1]

        def update(k, g):
            g_ref, d_ref, nm_ref, nv_ref = outs[1 + 4 * k:5 + 4 * k]
            g_ref[...] = g
            d_ref[...], nm_ref[...], nv_ref[...] = _adamw_math(w_refs[k][...], g, m_refs[k][...], v_refs[k][...])

        for k, name in enumerate(SMALL_NAMES):
            nr, w_ = SMALL[name]
            if name == "w_rgate":
                update(k, gates_ref[0:D_RG, :])
            elif name == "w_igate":
                update(k, gates_ref[D_RG:2 * D_RG, :])
            elif name in SHARDED_SMALL:
                r0 = VEC_ROW[name]
                for q in range(N_CHIPS):
                    @pl.when(chip == q)
                    def _(k=k, r0=r0, nr=nr, w_=w_, q=q):
                        update(k, vec_ref[r0:r0 + nr, q * w_:(q + 1) * w_])
            else:
                r0 = VEC_ROW[name]
                update(k, vec_ref[r0:r0 + nr, 0:w_])

    vmem = pl.BlockSpec(memory_space=pltpu.VMEM)
    out_shape = [jax.ShapeDtypeStruct((1, 1), F32)]
    for name in SMALL_NAMES:
        out_shape += [jax.ShapeDtypeStruct(SMALL[name], F32)] * 4
    outs = pl.pallas_call(
        body, in_specs=[vmem] * (2 + 3 * n), out_specs=[vmem] * len(out_shape), out_shape=out_shape,
        name="adamw_small",
    )(vec, gates, *[w[k] for k in SMALL_NAMES], *[m[k] for k in SMALL_NAMES], *[v[k] for k in SMALL_NAMES])
    loss = outs[0]
    res = {name: tuple(outs[1 + 4 * k:5 + 4 * k]) for k, name in enumerate(SMALL_NAMES)}
    return loss, res


WEIGHT_NAMES = ("meta_tokens", "mix_norm_g", "w_in", "conv_w", "conv_b", "w_rgate", "b_rgate", "w_igate", "b_igate",
                "lru_lambda", "rg_norm_g", "hg_lower_bound", "hg_norm_g", "w_out", "ffn_norm_g", "w_gate_up", "w_down",
                "final_norm_g")


def _to_2d(name, a):
    if name in BIG:
        return a.reshape(BIG[name][:2])
    return a.reshape(SMALL[name])


def kernel(x, meta_tokens, mix_norm_g, w_in, conv_w, conv_b, w_rgate, b_rgate, w_igate, b_igate, lru_lambda, rg_norm_g, hg_lower_bound, hg_norm_g, w_out, ffn_norm_g, w_gate_up, w_down, final_norm_g, loss_target, m_meta_tokens, m_mix_norm_g, m_w_in, m_conv_w, m_conv_b, m_w_rgate, m_b_rgate, m_w_igate, m_b_igate, m_lru_lambda, m_rg_norm_g, m_hg_lower_bound, m_hg_norm_g, m_w_out, m_ffn_norm_g, m_w_gate_up, m_w_down, m_final_norm_g, v_meta_tokens, v_mix_norm_g, v_w_in, v_conv_w, v_conv_b, v_w_rgate, v_b_rgate, v_w_igate, v_b_igate, v_lru_lambda, v_rg_norm_g, v_hg_lower_bound, v_hg_norm_g, v_w_out, v_ffn_norm_g, v_w_gate_up, v_w_down, v_final_norm_g):
    w_raw = dict(zip(WEIGHT_NAMES, (meta_tokens, mix_norm_g, w_in, conv_w, conv_b, w_rgate, b_rgate, w_igate, b_igate,
                                    lru_lambda, rg_norm_g, hg_lower_bound, hg_norm_g, w_out, ffn_norm_g, w_gate_up,
                                    w_down, final_norm_g)))
    m_raw = dict(zip(WEIGHT_NAMES, (m_meta_tokens, m_mix_norm_g, m_w_in, m_conv_w, m_conv_b, m_w_rgate, m_b_rgate,
                                    m_w_igate, m_b_igate, m_lru_lambda, m_rg_norm_g, m_hg_lower_bound, m_hg_norm_g,
                                    m_w_out, m_ffn_norm_g, m_w_gate_up, m_w_down, m_final_norm_g)))
    v_raw = dict(zip(WEIGHT_NAMES, (v_meta_tokens, v_mix_norm_g, v_w_in, v_conv_w, v_conv_b, v_w_rgate, v_b_rgate,
                                    v_w_igate, v_b_igate, v_lru_lambda, v_rg_norm_g, v_hg_lower_bound, v_hg_norm_g,
                                    v_w_out, v_ffn_norm_g, v_w_gate_up, v_w_down, v_final_norm_g)))
    w = {k: _to_2d(k, a) for k, a in w_raw.items()}
    m = {k: _to_2d(k, a) for k, a in m_raw.items()}
    v = {k: _to_2d(k, a) for k, a in v_raw.items()}

    x_i, y_i, c_i = _place()
    core = jnp.reshape(c_i, (1,)).astype(jnp.int32)
    chip = jnp.reshape(2 * x_i + y_i, (1,)).astype(jnp.int32)
    chip_core = jnp.concatenate([chip, core])

    first_names, rest_names = ("w_in",), ("w_out", "w_gate_up", "w_down")
    placed, _ = _place_shards(w, [], chip, first_names, "place_first")
    first, _ = _gather_weights(placed, [], first_names, "gather_first", 1)
    placed, (meta_full, cw_full) = _place_shards(w, [w["meta_tokens"], w["conv_w"]], chip, rest_names, "place_shards")
    rest, _ = _gather_weights(placed, [], rest_names, "gather_rest", 2)
    full = {**first, **rest}

    seq = x.shape[1]
    small ={k: w[k] for k in SMALL_NAMES if k not in SHARDED_SMALL}
    small["conv_w"] = cw_full

    def send_to_chips(sums, names, tag, collective_id):
        arrived = _send_chip_sums({n: sums[n][1] for n in names}, names, "send_chip_sums_" + tag, collective_id)
        return {n: (sums[n][0], a) for n, a in zip(names, arrived)}

    def reduce_to_chips(grads, names, send_names, tag, collective_ids):
        got = _exchange_halves({n: grads[n][1] for n in names}, names, "exchange_halves_" + tag, collective_ids[0])

        def chip_sums():
            return _chip_sum({n: grads[n][0] for n in names}, got, names, core, "chip_sum_" + tag)

        return chip_sums, lambda sums: send_to_chips(sums, send_names, tag, collective_ids[1])

    ffn_names, mixer_names = ("w_gate_up", "w_down", "w_out"), ("w_in",)
    loss, grad_x, grads, parts, parts_mixer = _local_step(
        x.reshape(seq, D_MODEL), meta_full, loss_target.reshape(seq, D_MODEL),
        w["w_in"], full["w_in"], full["w_out"], full["w_gate_up"], full["w_down"], small, chip,
        on_ffn_grads=lambda g: reduce_to_chips(g, ("w_down", "w_out"), ffn_names, "ffn", (3, 4)),
        on_mixer_grads=lambda sums: send_to_chips(sums, mixer_names, "mixer", 5))
    parts.update(parts_mixer)
    totals = _total(parts, chip_core)
    pieces = {k: grads[k] for k in VEC_ROW if k != "loss"}
    pieces["loss"] = loss
    vec, gates, g_big = _all_reduce_small(pieces, grads["w_gates"], totals)
    loss_sum, res = _adamw_small(vec, gates, w, m, v)
    res.update(_adamw_big(w, g_big, m, v))

    out = [loss_sum.reshape(()), grad_x.reshape(1, seq, D_MODEL)]
    for j in range(4):
        out += [res[n][j].reshape(w_raw[n].shape) for n in WEIGHT_NAMES]
    return tuple(out)
```

```python
import math

import jax
import jax.numpy as jnp
from jax import lax
from jax.experimental import pallas as pl
from jax.experimental.pallas import tpu as pltpu
from jax.experimental.pallas import tpu_sc as plsc

F32 = jnp.float32
BF16 = jnp.bfloat16
MESH = pl.DeviceIdType.MESH

D_MODEL = 1024
D_RG = 512
RG_HEAD_DIM = 64
D_HG = 512
HG_HEAD_DIM = 128
HG_HEADS = 4
CHUNK = 64
SUB = 16
N_SUB = CHUNK // SUB
N_META = 16
PAD = CHUNK - N_META
D_IN = 3072
D_FF = 2816
CONV_W = 4
LRU_C = 8.0
EPS = 1e-6
EXP_CLAMP = 80.0
GELU_C = math.sqrt(2.0 / math.pi)
GELU_A = 0.044715
N_CHIPS = 4

ADAM_LR = 0.001
ADAM_B1 = 0.9
ADAM_B2 = 0.999
ADAM_EPS = 1e-08
ADAM_WD = 0.01
ADAM_STEP = 10

VMEM_LIMIT = 56 * 1024 * 1024


def _params(*sem):
    return pltpu.CompilerParams(dimension_semantics=sem, vmem_limit_bytes=VMEM_LIMIT)


def _row_tile(rows, target):
    best = None
    for t in range(16, min(rows, target) + 1, 16):
        if rows % t == 0:
            best = t
    assert best is not None, rows
    return best


def _sigmoid(x):
    return 0.5 * jnp.tanh(0.5 * x) + 0.5


def _dot(a, b):
    return jnp.dot(a, b, preferred_element_type=F32)


def _dot_nt(a, b):
    return lax.dot_general(a, b, (((1,), (1,)), ((), ())), preferred_element_type=F32)


def _dot_tn(a, b):
    return lax.dot_general(a, b, (((0,), (0,)), ((), ())), preferred_element_type=F32)


def _rms(x):
    return lax.rsqrt(jnp.mean(x * x, axis=-1, keepdims=True) + EPS)


def _rms_bwd(dn, n, r):
    return r * (dn - n * jnp.mean(dn * n, axis=-1, keepdims=True))


def _gelu_parts(x):
    t = jnp.tanh(GELU_C * (x + GELU_A * x * x * x))
    g = 0.5 * x * (1.0 + t)
    dg = 0.5 * (1.0 + t) + 0.5 * x * (1.0 - t * t) * GELU_C * (1.0 + 3.0 * GELU_A * x * x)
    return g, dg


def _softplus_neg(lam):
    e = jnp.exp(-jnp.abs(lam))
    w = 1.0 + e
    log1p = jnp.where(w == 1.0, e, jnp.log(w) * e / (w - 1.0))
    return jnp.maximum(-lam, 0.0) + log1p


def _head_mask():
    r = lax.broadcasted_iota(jnp.int32, (D_RG, D_RG), 0) // RG_HEAD_DIM
    c = lax.broadcasted_iota(jnp.int32, (D_RG, D_RG), 1) // RG_HEAD_DIM
    return r == c


def _head_fold():
    r = lax.broadcasted_iota(jnp.int32, (D_RG, RG_HEAD_DIM), 0) % RG_HEAD_DIM
    c = lax.broadcasted_iota(jnp.int32, (D_RG, RG_HEAD_DIM), 1)
    return (r == c).astype(F32)


def _gate_weights(w_r, w_i):
    def body(wr_ref, wi_ref, o_ref):
        fold = _head_fold()
        mask = _head_mask()
        for k, ref in enumerate((wr_ref, wi_ref)):
            full = _dot_nt(ref[...].astype(BF16), fold.astype(BF16))
            o_ref[:, k * D_RG:(k + 1) * D_RG] = jnp.where(mask, full, 0.0).astype(BF16)

    return pl.pallas_call(
        body, out_shape=jax.ShapeDtypeStruct((D_RG, 2 * D_RG), BF16), name="gate_weights",
    )(w_r, w_i)


HEAD = PAD + N_META


def _window_copies(seq_hbm, buf, sems, tm):
    def first(to_vmem):
        seq, vm = seq_hbm.at[pl.ds(0, tm - HEAD)], buf.at[0, pl.ds(HEAD, tm - HEAD)]
        return pltpu.make_async_copy(seq, vm, sems.at[0]) if to_vmem else pltpu.make_async_copy(vm, seq, sems.at[0])

    def later(j, slot, to_vmem):
        seq, vm = seq_hbm.at[pl.ds(pl.multiple_of(j * tm - HEAD, 8), tm)], buf.at[slot]
        if to_vmem:
            return pltpu.make_async_copy(seq, vm, sems.at[slot])
        return pltpu.make_async_copy(vm, seq, sems.at[slot])

    return first, later


def _fetch_window(seq_hbm, buf, sems, i, n_steps, tm):
    first, later = _window_copies(seq_hbm, buf, sems, tm)
    slot = i % 2

    @pl.when(i == 0)
    def _():
        first(True).start()

    if n_steps > 1:
        @pl.when(i + 1 < n_steps)
        def _():
            later(i + 1, 1 - slot, True).start()

    @pl.when(i == 0)
    def _():
        first(True).wait()

    if n_steps > 1:
        @pl.when(i > 0)
        def _():
            later(i, slot, True).wait()

    return slot


def _in_proj_local(x, meta, g1, w_own, chip):
    T = x.shape[0] + HEAD
    tm = _row_tile(T, 832)
    n_steps = T // tm
    cols = BIG["w_in"][1]

    def body(s_ref, x_hbm, meta_ref, g_ref, w_ref, p_ref, u_ref, h_ref, buf, sems, wb):
        i = pl.program_id(0)
        slot = _fetch_window(x_hbm, buf, sems, i, n_steps, tm)

        @pl.when(i == 0)
        def _():
            buf[0, 0:PAD, :] = jnp.zeros((PAD, D_MODEL), F32)
            buf[0, PAD:HEAD, :] = meta_ref[...]
            wb[...] = w_ref[...].astype(BF16)

        h = buf[slot]
        h_ref[...] = h
        u = (h * _rms(h) * g_ref[...]).astype(BF16)
        u_ref[...] = u
        p_ref[...] = _dot(u, wb[...])

    return pl.pallas_call(
        body,
        grid_spec=pltpu.PrefetchScalarGridSpec(
            num_scalar_prefetch=1, grid=(n_steps,),
            in_specs=[pl.BlockSpec(memory_space=pl.ANY),
                      pl.BlockSpec((N_META, D_MODEL), lambda i, s: (0, 0)),
                      pl.BlockSpec((1, D_MODEL), lambda i, s: (0, 0)),
                      pl.BlockSpec((D_MODEL, cols), lambda i, s: (0, 0))],
            out_specs=[pl.BlockSpec((tm, cols), lambda i, s: (i, s[0])),
                       pl.BlockSpec((tm, D_MODEL), lambda i, s: (i, 0)),
                       pl.BlockSpec((tm, D_MODEL), lambda i, s: (i, 0))],
            scratch_shapes=[pltpu.VMEM((2, tm, D_MODEL), F32), pltpu.SemaphoreType.DMA((2,)),
                            pltpu.VMEM((D_MODEL, cols), BF16)]),
        out_shape=[jax.ShapeDtypeStruct((T, D_IN), F32), jax.ShapeDtypeStruct((T, D_MODEL), BF16),
                   jax.ShapeDtypeStruct((T, D_MODEL), F32)],
        name="in_proj_local", compiler_params=_params("arbitrary"),
    )(chip, x, meta, g1, w_own)


def _in_proj_rest(u, w_in, p, chip):
    T = u.shape[0]
    tm = _row_tile(T, 2080)
    cols = BIG["w_in"][1]
    block = lambda j, s: (s[0] + 1 + j) % N_CHIPS

    def body(s_ref, u_ref, w_ref, p_in_ref, p_ref):
        p_ref[...] = _dot(u_ref[...], w_ref[...])

    return pl.pallas_call(
        body,
        grid_spec=pltpu.PrefetchScalarGridSpec(
            num_scalar_prefetch=1, grid=(N_CHIPS - 1, T // tm),
            in_specs=[pl.BlockSpec((tm, D_MODEL), lambda j, i, s: (i, 0)),
                      pl.BlockSpec((D_MODEL, cols), lambda j, i, s: (0, block(j, s))), ANY],
            out_specs=pl.BlockSpec((tm, cols), lambda j, i, s: (i, block(j, s)))),
        out_shape=jax.ShapeDtypeStruct((T, D_IN), F32),
        input_output_aliases={3: 0},
        name="in_proj_rest", compiler_params=_params("arbitrary", "arbitrary"),
    )(chip, u, w_in, p)


def _scan_block_fwd(A, B, rowi):
    for d in (1, 2, 4):
        a_sh = pltpu.roll(A, d, axis=0)
        b_sh = pltpu.roll(B, d, axis=0)
        m = rowi >= d
        B = jnp.where(m, A * b_sh + B, B)
        A = jnp.where(m, A * a_sh, A)
    return A, B


def _scan_block_bwd(A, B, rowi):
    for d in (1, 2, 4):
        a_sh = pltpu.roll(A, 8 - d, axis=0)
        b_sh = pltpu.roll(B, 8 - d, axis=0)
        m = rowi < 8 - d
        B = jnp.where(m, A * b_sh + B, B)
        A = jnp.where(m, A * a_sh, A)
    return A, B


def _rg_gates(xc, w_ref, bg_ref, lam):
    pre = _dot(xc.astype(BF16), w_ref[...]) + bg_ref[...]
    r = _sigmoid(pre[:, :D_RG])
    ig = _sigmoid(pre[:, D_RG:])
    sp = _softplus_neg(lam)
    la = -LRU_C * sp * r
    a = jnp.exp(la)
    th = jnp.tanh(la)
    u = 1.0 - th
    rc = pl.reciprocal(u, approx=True)
    rc = rc * (2.0 - u * rc)
    rc = rc * (2.0 - u * rc)
    m2 = -2.0 * th * rc
    inv_m = lax.rsqrt(jnp.maximum(m2, 1e-30))
    return r, ig, sp, a, m2 * inv_m, inv_m


def _conv(ext, cw_ref, cb_ref, tm):
    xc = cb_ref[...] + cw_ref[0:1, :] * ext[8 - 3:8 - 3 + tm, :]
    for j in range(1, CONV_W):
        xc = xc + cw_ref[j:j + 1, :] * ext[8 - 3 + j:8 - 3 + j + tm, :]
    return xc


def _scan_unroll(blocks):
    return 4 if blocks % 4 == 0 else 2 if blocks % 2 == 0 else 1


def _rg_fwd(p, cw, cb, wg, bg, lam, rg_g):
    T = p.shape[0]
    tm = _row_tile(T, 832)
    unroll = _scan_unroll(tm // 8)

    def body(xg_ref, cw_ref, cb_ref, w_ref, bg_ref, lam_ref, g_ref, y_ref, h_ref, xc_ref, ext, a_s, b_s, carry):
        i = pl.program_id(0)

        @pl.when(i == 0)
        def _():
            ext[0:8, :] = jnp.zeros((8, D_RG), F32)
            carry[...] = jnp.zeros((1, D_RG), F32)

        ext[8:8 + tm, :] = xg_ref[:, :D_RG]
        xc = _conv(ext, cw_ref, cb_ref, tm)
        xc_ref[...] = xc
        r, ig, sp, a, m, _ = _rg_gates(xc, w_ref, bg_ref, lam_ref[...])
        row = i * tm + lax.broadcasted_iota(jnp.int32, (tm, 1), 0)
        a_s[...] = a
        b_s[...] = jnp.where(row >= PAD, m * ig * xc, 0.0)
        rowi = lax.broadcasted_iota(jnp.int32, (8, D_RG), 0)

        def blk(j, c):
            for u in range(unroll):
                o = pl.multiple_of((j * unroll + u) * 8, 8)
                A, B = _scan_block_fwd(a_s[pl.ds(o, 8), :], b_s[pl.ds(o, 8), :], rowi)
                h = B + A * c
                h_ref[pl.ds(o, 8), :] = h
                c = h[7:8, :]
            return c

        carry[...] = lax.fori_loop(0, tm // (8 * unroll), blk, carry[...])
        ext[0:8, :] = ext[tm:tm + 8, :]
        g, _ = _gelu_parts(xg_ref[:, D_RG:])
        yy = g * h_ref[...]
        y_ref[...] = (yy * _rms(yy) * g_ref[...]).astype(BF16)

    vec = lambda n: pl.BlockSpec((1, n), lambda i: (0, 0))
    return pl.pallas_call(
        body, grid=(T // tm,),
        in_specs=[pl.BlockSpec((tm, 2 * D_RG), lambda i: (i, 0)),
                  pl.BlockSpec((CONV_W, D_RG), lambda i: (0, 0)), vec(D_RG),
                  pl.BlockSpec((D_RG, 2 * D_RG), lambda i: (0, 0)), vec(2 * D_RG), vec(D_RG), vec(D_RG)],
        out_specs=[pl.BlockSpec((tm, D_RG), lambda i: (i, 0))] * 3,
        out_shape=[jax.ShapeDtypeStruct((T, D_RG), BF16), jax.ShapeDtypeStruct((T, D_RG), F32),
                   jax.ShapeDtypeStruct((T, D_RG), F32)],
        scratch_shapes=[pltpu.VMEM((tm + 8, D_RG), F32), pltpu.VMEM((tm, D_RG), F32),
                        pltpu.VMEM((tm, D_RG), F32), pltpu.VMEM((1, D_RG), F32)],
        name="rg_fwd", compiler_params=_params("arbitrary"),
    )(p, cw, cb, wg, bg, lam, rg_g)


def _running_sum(x, down):
    r = lax.broadcasted_iota(jnp.int32, (CHUNK, CHUNK), 0)
    c = lax.broadcasted_iota(jnp.int32, (CHUNK, CHUNK), 1)
    tri = ((c <= r) if down else (c >= r)).astype(BF16)
    hi = x.astype(BF16)
    rest = x - hi.astype(F32)
    mid = rest.astype(BF16)
    lo = (rest - mid.astype(F32)).astype(BF16)
    return (_dot(tri, hi) + _dot(tri, mid)) + _dot(tri, lo)


def _hg_gates(hq, hf, lbraw_ref, valid):
    lb = _sigmoid(lbraw_ref[0:1, :] - lbraw_ref[1:2, :])
    sq = _sigmoid(hq)
    q = hq * sq
    sf = _sigmoid(hf)
    f = lb + (1.0 - lb) * sf
    lf = jnp.where(valid, jnp.log(f), 0.0)
    b = _running_sum(lf, True)
    return lb, sq, q, sf, f, b


def _hg_head(qh, kh, bh):
    b_last = bh[CHUNK - 1:CHUNK, :]
    refs = [bh[SUB * s:SUB * s + 1, :] for s in range(N_SUB)]
    r_sel = jnp.concatenate([jnp.broadcast_to(refs[s], (SUB, HG_HEAD_DIM)) for s in range(N_SUB)], axis=0)
    eb = jnp.exp(bh)
    eq = jnp.exp(bh - r_sel)
    ekh = jnp.exp(b_last - bh)
    ek = [jnp.exp(jnp.minimum(refs[s] - bh[:SUB * (s + 1), :], EXP_CLAMP)) for s in range(N_SUB)]
    qe = qh * eq

    def own_rows(s):
        parts = [jnp.zeros((SUB * s, HG_HEAD_DIM), F32)] if s else []
        parts.append(qe[SUB * s:SUB * (s + 1), :])
        if s < N_SUB - 1:
            parts.append(jnp.zeros((CHUNK - SUB * (s + 1), HG_HEAD_DIM), F32))
        return jnp.concatenate(parts, axis=0)

    q_hat = jnp.concatenate([own_rows(s) for s in range(N_SUB)], axis=1)

    def met_rows(s):
        n = SUB * (s + 1)
        ke = kh[:n, :] * ek[s]
        return ke if n == CHUNK else jnp.concatenate([ke, jnp.zeros((CHUNK - n, HG_HEAD_DIM), F32)], axis=0)

    k_til = jnp.concatenate([met_rows(s) for s in range(N_SUB)], axis=1)
    return b_last, eb, eq, ekh, ek, q_hat, k_til


def _causal():
    r = lax.broadcasted_iota(jnp.int32, (CHUNK, CHUNK), 0)
    c = lax.broadcasted_iota(jnp.int32, (CHUNK, CHUNK), 1)
    return r >= c


def _chunks_per_step(n_chunks):
    for c in (5, 4, 3, 2):
        if n_chunks % c == 0:
            return c
    return 1


def _hg_fwd(p, lbraw, hg_g):
    T = p.shape[0]
    n_chunks = T // CHUNK
    cps = _chunks_per_step(n_chunks)
    rows = cps * CHUNK

    def body(hq_ref, hf_ref, hi_ref, hg_ref, lb_ref, g_ref, y_ref, o_ref, st_all_ref, st):
        i = pl.program_id(0)

        @pl.when(i == 0)
        def _():
            st[...] = jnp.zeros_like(st)

        def chunk(j, carry):
            rs = pl.ds(pl.multiple_of(j * CHUNK, CHUNK), CHUNK)
            chunk_body(i * cps + j, hq_ref.at[rs, :], hf_ref.at[rs, :], hi_ref.at[rs, :], hg_ref.at[rs, :], lb_ref,
                       g_ref, y_ref.at[rs, :], o_ref.at[rs, :], st_all_ref.at[pl.ds(j, 1)], st)
            return carry

        lax.fori_loop(0, cps, chunk, 0, unroll=True)

    def chunk_body(n, hq_ref, hf_ref, hi_ref, hg_ref, lb_ref, g_ref, y_ref, o_ref, st_all_ref, st):
        valid = (n * CHUNK + lax.broadcasted_iota(jnp.int32, (CHUNK, 1), 0)) >= PAD
        hq, hf, v, hg = hq_ref[...], hf_ref[...], hi_ref[...], hg_ref[...]
        lb, sq, q, sf, f, b = _hg_gates(hq, hf, lb_ref, valid)
        k = 1.0 - f
        st_all_ref[0] = st[...]
        causal = _causal()
        v_t = v.T.astype(BF16)
        heads = [slice(h * HG_HEAD_DIM, (h + 1) * HG_HEAD_DIM) for h in range(HG_HEADS)]
        fac = []
        for sl in heads:
            qh, kh, bh = q[:, sl], k[:, sl], b[:, sl]
            b_last, eb, _, ekh, _, q_hat, k_til = _hg_head(qh, kh, bh)
            fac.append((jnp.exp(b_last), (qh * eb).astype(BF16), q_hat.astype(BF16), k_til.astype(BF16),
                        (kh * ekh).astype(BF16), v[:, sl].astype(BF16)))
        raw = []
        for sl, (_, q_til, q_hat, k_til, k_hat, _) in zip(heads, fac):
            st_h = st[sl, :]
            raw.append((_dot_nt(q_til, st_h.astype(BF16)), _dot_nt(q_hat, k_til), _dot(v_t[sl, :], k_hat), st_h))
        for sl, (e_last, _, _, _, _, vb), (inter, att, upd, st_h) in zip(heads, fac, raw):
            o = inter + _dot(jnp.where(causal, att, 0.0).astype(BF16), vb)
            st[sl, :] = st_h * e_last + upd
            o_ref[:, sl] = o
            hgh = hg[:, sl]
            y_ref[:, sl] = (o * _rms(o) * g_ref[...] * (hgh * _sigmoid(hgh))).astype(BF16)

    col = lambda j: pl.BlockSpec((rows, D_HG), lambda n: (n, j))
    return pl.pallas_call(
        body, grid=(n_chunks // cps,),
        in_specs=[col(2), col(3), col(4), col(5),
                  pl.BlockSpec((2, D_HG), lambda n: (0, 0)), pl.BlockSpec((1, HG_HEAD_DIM), lambda n: (0, 0))],
        out_specs=[pl.BlockSpec((rows, D_HG), lambda n: (n, 0)), pl.BlockSpec((rows, D_HG), lambda n: (n, 0)),
                   pl.BlockSpec((cps, D_HG, HG_HEAD_DIM), lambda n: (n, 0, 0))],
        out_shape=[jax.ShapeDtypeStruct((T, D_HG), BF16), jax.ShapeDtypeStruct((T, D_HG), F32),
                   jax.ShapeDtypeStruct((n_chunks, D_HG, HG_HEAD_DIM), F32)],
        scratch_shapes=[pltpu.VMEM((D_HG, HG_HEAD_DIM), F32)],
        name="hg_fwd", compiler_params=_params("arbitrary"),
    )(p, p, p, p, lbraw, hg_g)


def _ffn_fwd(h0, y_rg, y_hg, w_out, g2, w_gu, w_down, gf, target):
    T = h0.shape[0]
    tm = _row_tile(T, 320)
    n_steps = T // tm

    def body(h_ref, yr_ref, yh_ref, wo_ref, g2_ref, wgu_ref, wd_ref, gf_ref, t_hbm,
             h1_ref, v_ref, y_ref, gu_ref, act_ref, dh2_ref, dh2b_ref, loss_ref, gg_ref, tbuf, sems):
        i = pl.program_id(0)
        slot = _fetch_window(t_hbm, tbuf, sems, i, n_steps, tm)

        @pl.when(i == 0)
        def _():
            loss_ref[...] = jnp.zeros_like(loss_ref)
            gg_ref[...] = jnp.zeros_like(gg_ref)
            tbuf[0, 0:HEAD, :] = jnp.zeros((HEAD, D_MODEL), F32)

        y_ref[:, :D_RG] = yr_ref[...]
        y_ref[:, D_RG:] = yh_ref[...]
        h1 = h_ref[...] + _dot(y_ref[...], wo_ref[...])
        h1_ref[...] = h1
        v = (h1 * _rms(h1) * g2_ref[...]).astype(BF16)
        v_ref[...] = v

        gu = _dot(v, wgu_ref[...])
        gu_ref[...] = gu.astype(BF16)
        g = gu[:, :D_FF]
        act = (g * _sigmoid(g) * gu[:, D_FF:]).astype(BF16)
        act_ref[...] = act

        h2 = h1 + _dot(act, wd_ref[...])
        r = _rms(h2)
        n = h2 * r
        gf_ = gf_ref[...]
        row = i * tm + lax.broadcasted_iota(jnp.int32, (tm, 1), 0)
        err = jnp.where(row >= HEAD, n * gf_ - tbuf[slot], 0.0)
        loss_ref[...] += 0.5 * jnp.sum(jnp.mean(err * err, axis=-1, keepdims=True), axis=0, keepdims=True)
        dy = err * (1.0 / D_MODEL)
        gg_ref[...] += jnp.sum(dy * n, axis=0, keepdims=True)
        dh2 = _rms_bwd(dy * gf_, n, r)
        dh2_ref[...] = dh2
        dh2b_ref[...] = dh2.astype(BF16)

    row_spec = lambda n: pl.BlockSpec((tm, n), lambda i: (i, 0))
    vec = pl.BlockSpec((1, D_MODEL), lambda i: (0, 0))
    return pl.pallas_call(
        body, grid=(n_steps,),
        in_specs=[row_spec(D_MODEL), row_spec(D_RG), row_spec(D_HG), _resident((D_MODEL, D_MODEL)), vec,
                  _resident((D_MODEL, 2 * D_FF)), _resident((D_FF, D_MODEL)), vec,
                  pl.BlockSpec(memory_space=pl.ANY)],
        out_specs=[row_spec(D_MODEL), row_spec(D_MODEL), row_spec(D_MODEL), row_spec(2 * D_FF), row_spec(D_FF),
                   row_spec(D_MODEL), row_spec(D_MODEL), pl.BlockSpec((1, 1), lambda i: (0, 0)), vec],
        out_shape=[jax.ShapeDtypeStruct((T, D_MODEL), F32), jax.ShapeDtypeStruct((T, D_MODEL), BF16),
                   jax.ShapeDtypeStruct((T, D_MODEL), BF16), jax.ShapeDtypeStruct((T, 2 * D_FF), BF16),
                   jax.ShapeDtypeStruct((T, D_FF), BF16), jax.ShapeDtypeStruct((T, D_MODEL), F32),
                   jax.ShapeDtypeStruct((T, D_MODEL), BF16), jax.ShapeDtypeStruct((1, 1), F32),
                   jax.ShapeDtypeStruct((1, D_MODEL), F32)],
        scratch_shapes=[pltpu.VMEM((2, tm, D_MODEL), F32), pltpu.SemaphoreType.DMA((2,))],
        name="ffn_fwd", compiler_params=_params("arbitrary"),
    )(h0, y_rg, y_hg, w_out, g2, w_gu, w_down, gf, target)


def _resident(shape):
    return pl.BlockSpec(shape, lambda i: (0,) * len(shape), pipeline_mode=pl.Buffered(1))


def _ffn_bwd(dh2b, gu, w_down, w_gu, h1, g2, dh2, w_out):
    T = h1.shape[0]
    tm = _row_tile(T, 320)

    def body(d_ref, gu_ref, wd_ref, wgu_ref, h_ref, g_ref, d2_ref, wo_ref, dgu_ref, dh1_ref, dh1b_ref, dy_ref, gg_ref):
        i = pl.program_id(0)

        @pl.when(i == 0)
        def _():
            gg_ref[...] = jnp.zeros_like(gg_ref)

        dact = _dot_nt(d_ref[...], wd_ref[...]).astype(BF16)
        g = gu_ref[:, :D_FF]
        u = gu_ref[:, D_FF:]
        s = _sigmoid(g)
        dgu_ref[:, :D_FF] = dact * u * (s * (1.0 + g * (1.0 - s)))
        dgu_ref[:, D_FF:] = dact * (g * s)

        dv = _dot_nt(dgu_ref[...], wgu_ref[...])
        h1_ = h_ref[...]
        r = _rms(h1_)
        n = h1_ * r
        gg_ref[...] += jnp.sum(dv * n, axis=0, keepdims=True)
        dh1 = d2_ref[...] + _rms_bwd(dv * g_ref[...], n, r)
        dh1_ref[...] = dh1
        db = dh1.astype(BF16)
        dh1b_ref[...] = db
        dy_ref[...] = _dot_nt(db, wo_ref[...])

    row = lambda n: pl.BlockSpec((tm, n), lambda i: (i, 0))
    return pl.pallas_call(
        body, grid=(T // tm,),
        in_specs=[row(D_MODEL), row(2 * D_FF), _resident((D_FF, D_MODEL)), _resident((D_MODEL, 2 * D_FF)),
                  row(D_MODEL), pl.BlockSpec((1, D_MODEL), lambda i: (0, 0)), row(D_MODEL),
                  _resident((D_MODEL, D_MODEL))],
        out_specs=[row(2 * D_FF), row(D_MODEL), row(D_MODEL), row(D_MODEL),
                   pl.BlockSpec((1, D_MODEL), lambda i: (0, 0))],
        out_shape=[jax.ShapeDtypeStruct((T, 2 * D_FF), BF16), jax.ShapeDtypeStruct((T, D_MODEL), F32),
                   jax.ShapeDtypeStruct((T, D_MODEL), BF16), jax.ShapeDtypeStruct((T, D_MODEL), F32),
                   jax.ShapeDtypeStruct((1, D_MODEL), F32)],
        name="ffn_bwd", compiler_params=_params("arbitrary"),
    )(dh2b, gu, w_down, w_gu, h1, g2, dh2, w_out)


def _rg_bwd(p, xc_all, hs, dy, dp, cw, cb, wg, bg, lam, rg_g):
    T = p.shape[0]
    tm = _row_tile(T, 832)
    nt = T // tm
    hb = tm // 8
    unroll = _scan_unroll(hb)

    def body(xg_ref, xc_ref, h_ref, hh_ref, dy_ref, dp_in_ref, cw_ref, cb_ref, w_ref, bg_ref, lam_ref, g_ref,
             dp_ref, gcw_ref, gcb_ref, gw_ref, gbg_ref, glam_ref, gg_ref,
             dext, a_s, b_s, d_s, gacc, carry_d, carry_a):
        i = pl.program_id(0)
        t_idx = nt - 1 - i

        @pl.when(i == 0)
        def _():
            dext[tm:tm + 8, :] = jnp.zeros((8, D_RG), F32)
            carry_d[...] = jnp.zeros_like(carry_d)
            carry_a[...] = jnp.zeros_like(carry_a)
            gacc[...] = jnp.zeros_like(gacc)
            for ref in (gcw_ref, gcb_ref, gbg_ref, glam_ref, gg_ref, gw_ref):
                ref[...] = jnp.zeros_like(ref)

        first = t_idx == 0
        xc = xc_ref[...]
        lam_ = lam_ref[...]
        r, ig, sp, a, m, inv_m = _rg_gates(xc, w_ref, bg_ref, lam_)
        row = t_idx * tm + lax.broadcasted_iota(jnp.int32, (tm, 1), 0)
        valid = row >= PAD

        gr = xg_ref[:, D_RG:]
        g, dgelu = _gelu_parts(gr)
        h = h_ref[...]
        yy = g * h
        rr = _rms(yy)
        nn = yy * rr
        dy_ = dy_ref[...]
        gg_ref[...] += jnp.sum(dy_ * nn, axis=0, keepdims=True)
        dyy = _rms_bwd(dy_ * g_ref[...], nn, rr)
        dp_ref[:, D_RG:] = (dyy * h * dgelu).astype(BF16)

        a_s[...] = a
        b_s[...] = dyy * g
        rowi = lax.broadcasted_iota(jnp.int32, (8, D_RG), 0)

        def blk(jj, c):
            cd, ca = c
            for u in range(unroll):
                o = pl.multiple_of((hb - 1 - (jj * unroll + u)) * 8, 8)
                a_blk = a_s[pl.ds(o, 8), :]
                a_next = jnp.where(rowi == 7, ca, pltpu.roll(a_blk, 7, axis=0))
                A, B = _scan_block_bwd(a_next, b_s[pl.ds(o, 8), :], rowi)
                d = B + A * cd
                d_s[pl.ds(o, 8), :] = d
                cd, ca = d[0:1, :], a_blk[0:1, :]
            return cd, ca

        cd, ca = lax.fori_loop(0, hb // unroll, blk, (carry_d[...], carry_a[...]))
        carry_d[...] = cd
        carry_a[...] = ca
        delta = d_s[...]

        h_last_prev = jnp.where(first, 0.0, hh_ref[7:8, :])
        row0 = lax.broadcasted_iota(jnp.int32, (tm, 1), 0) == 0
        h_prev = jnp.where(row0, h_last_prev, pltpu.roll(h, 1, axis=0))
        dbx = jnp.where(valid, delta, 0.0)
        da = delta * h_prev
        di = dbx * m * xc
        dm = dbx * ig * xc
        dla = a * (da - dm * a * inv_m)
        dla = jnp.where(valid, dla, 0.0)
        glam_ref[...] += jnp.sum(dla * r, axis=0, keepdims=True) * (LRU_C / (1.0 + jnp.exp(lam_)))
        dr = (-LRU_C) * sp * dla
        dpre = jnp.concatenate([dr * r * (1.0 - r), di * ig * (1.0 - ig)], axis=1)
        gbg_ref[...] += jnp.sum(dpre, axis=0, keepdims=True)
        dpre_b = dpre.astype(BF16)
        gacc[...] += _dot_tn(xc.astype(BF16), dpre_b)
        dxc = dbx * m * ig + _dot_nt(dpre_b, w_ref[...])
        gcb_ref[...] += jnp.sum(dxc, axis=0, keepdims=True)
        dext[0:tm, :] = dxc
        xr = xg_ref[:, :D_RG]
        dxr = None
        for j in range(CONV_W):
            shifted = dext[3 - j:3 - j + tm, :]
            gcw_ref[j:j + 1, :] += jnp.sum(xr * shifted, axis=0, keepdims=True)
            tap = cw_ref[j:j + 1, :] * shifted
            dxr = tap if dxr is None else dxr + tap
        dp_ref[:, :D_RG] = dxr.astype(BF16)
        dext[tm:tm + 8, :] = dext[0:8, :]

        @pl.when(i == nt - 1)
        def _():
            fold = _head_fold()
            mask = _head_mask()
            fold_b = fold.astype(BF16)
            for k in range(2):
                blockdiag = jnp.where(mask, gacc[:, k * D_RG:(k + 1) * D_RG], 0.0)
                hi = blockdiag.astype(BF16)
                rest = blockdiag - hi.astype(F32)
                mid = rest.astype(BF16)
                lo = (rest - mid.astype(F32)).astype(BF16)
                gw_ref[k * D_RG:(k + 1) * D_RG, :] = (_dot(hi, fold_b) + _dot(mid, fold_b)) + _dot(lo, fold_b)

    vec = lambda n: pl.BlockSpec((1, n), lambda i: (0, 0))
    rev = lambda n: pl.BlockSpec((tm, n), lambda i: (nt - 1 - i, 0))
    halo = lambda n: pl.BlockSpec((8, n), lambda i: (jnp.maximum((nt - 1 - i) * hb - 1, 0), 0))
    return pl.pallas_call(
        body, grid=(nt,),
        in_specs=[rev(2 * D_RG), rev(D_RG), rev(D_RG), halo(D_RG), rev(D_RG), ANY,
                  pl.BlockSpec((CONV_W, D_RG), lambda i: (0, 0)), vec(D_RG),
                  pl.BlockSpec((D_RG, 2 * D_RG), lambda i: (0, 0)), vec(2 * D_RG), vec(D_RG), vec(D_RG)],
        out_specs=[rev(2 * D_RG), pl.BlockSpec((CONV_W, D_RG), lambda i: (0, 0)), vec(D_RG),
                   pl.BlockSpec((2 * D_RG, RG_HEAD_DIM), lambda i: (0, 0)), vec(2 * D_RG), vec(D_RG), vec(D_RG)],
        input_output_aliases={5: 0},
        out_shape=[jax.ShapeDtypeStruct((T, D_IN), BF16), jax.ShapeDtypeStruct((CONV_W, D_RG), F32),
                   jax.ShapeDtypeStruct((1, D_RG), F32), jax.ShapeDtypeStruct((2 * D_RG, RG_HEAD_DIM), F32),
                   jax.ShapeDtypeStruct((1, 2 * D_RG), F32), jax.ShapeDtypeStruct((1, D_RG), F32),
                   jax.ShapeDtypeStruct((1, D_RG), F32)],
        scratch_shapes=[pltpu.VMEM((tm + 8, D_RG), F32),
                        pltpu.VMEM((tm, D_RG), F32), pltpu.VMEM((tm, D_RG), F32), pltpu.VMEM((tm, D_RG), F32),
                        pltpu.VMEM((D_RG, 2 * D_RG), F32), pltpu.VMEM((1, D_RG), F32), pltpu.VMEM((1, D_RG), F32)],
        name="rg_bwd", compiler_params=_params("arbitrary"),
    )(p, xc_all, hs, hs, dy, dp, cw, cb, wg, bg, lam, rg_g)


def _hg_bwd(p, o_all, st_all, dy, lbraw, hg_g):
    T = p.shape[0]
    n_chunks = T // CHUNK
    cps = _chunks_per_step(n_chunks)
    rows = cps * CHUNK
    n_steps = n_chunks // cps

    def body(hq_ref, hf_ref, hi_ref, hg_ref, o_ref, st_ref, dy_ref, lb_ref, g_ref,
             dp_ref, glb_ref, gg_ref, dst):
        i = pl.program_id(0)

        @pl.when(i == 0)
        def _():
            dst[...] = jnp.zeros_like(dst)
            glb_ref[...] = jnp.zeros_like(glb_ref)
            gg_ref[...] = jnp.zeros_like(gg_ref)

        dp_ref[:, :2 * D_RG] = jnp.zeros((rows, 2 * D_RG), BF16)

        def chunk(jj, carry):
            j = cps - 1 - jj
            rs = pl.ds(pl.multiple_of(j * CHUNK, CHUNK), CHUNK)
            chunk_body((n_steps - 1 - i) * cps + j, hq_ref.at[rs, :], hf_ref.at[rs, :], hi_ref.at[rs, :],
                       hg_ref.at[rs, :], o_ref.at[rs, :], st_ref.at[pl.ds(j, 1)], dy_ref.at[rs, :], lb_ref, g_ref,
                       dp_ref.at[rs, pl.ds(2 * D_RG, 4 * D_HG)], glb_ref, gg_ref, dst)
            return carry

        lax.fori_loop(0, cps, chunk, 0, unroll=True)

    def chunk_body(n, hq_ref, hf_ref, hi_ref, hg_ref, o_ref, st_ref, dy_ref, lb_ref, g_ref,
                   dp_ref, glb_ref, gg_ref, dst):
        valid = (n * CHUNK + lax.broadcasted_iota(jnp.int32, (CHUNK, 1), 0)) >= PAD
        hq, hf, v, hg = hq_ref[...], hf_ref[...], hi_ref[...], hg_ref[...]
        lb, sq, q, sf, f, b = _hg_gates(hq, hf, lb_ref, valid)
        k = 1.0 - f
        causal = _causal()
        r_i = lax.broadcasted_iota(jnp.int32, (CHUNK, CHUNK), 0)
        c_i = lax.broadcasted_iota(jnp.int32, (CHUNK, CHUNK), 1)
        causal_t = r_i <= c_i
        is_last = lax.broadcasted_iota(jnp.int32, (CHUNK, 1), 0) == CHUNK - 1
        g_ = g_ref[...]
        db_parts, dq_parts, dk_parts = [], [], []
        gg = jnp.zeros((1, HG_HEAD_DIM), F32)
        heads = [slice(h * HG_HEAD_DIM, (h + 1) * HG_HEAD_DIM) for h in range(HG_HEADS)]

        do_parts = []
        for h, sl in enumerate(heads):
            o = o_ref[:, sl]
            ro = _rms(o)
            no = o * ro
            hgh = hg[:, sl]
            sg = _sigmoid(hgh)
            dyh = dy_ref[:, sl]
            dp_ref[:, 3 * D_HG + h * HG_HEAD_DIM:3 * D_HG + (h + 1) * HG_HEAD_DIM] = (
                dyh * no * g_ * sg * (1.0 + hgh * (1.0 - sg))).astype(BF16)
            dng = dyh * hgh * sg
            gg = gg + jnp.sum(dng * no, axis=0, keepdims=True)
            do_parts.append(_rms_bwd(dng * g_, no, ro))
        do_t = jnp.concatenate(do_parts, axis=1).T.astype(BF16)

        fac = []
        for sl, do in zip(heads, do_parts):
            qh, kh, bh = q[:, sl], k[:, sl], b[:, sl]
            b_last, eb, eq, ekh, ek, q_hat, k_til = _hg_head(qh, kh, bh)
            fac.append(dict(qh=qh, kh=kh, e_last=jnp.exp(b_last), eb=eb, eq=eq, ekh=ekh, ek=ek,
                            q_til=qh * eb, k_hat=kh * ekh, qhb=q_hat.astype(BF16), ktb=k_til.astype(BF16),
                            vb=v[:, sl].astype(BF16), dob=do.astype(BF16)))

        first = []
        for sl, t in zip(heads, fac):
            st_h = st_ref[0, sl, :]
            dst_h = dst[sl, :]
            dstb = dst_h.astype(BF16)
            first.append(dict(
                att_t=_dot_nt(t["ktb"], t["qhb"]), datt=_dot_nt(t["dob"], t["vb"]),
                datt_t=_dot_nt(t["vb"], t["dob"]), dk_hat=_dot(t["vb"], dstb),
                dv=_dot_nt(t["k_hat"].astype(BF16), dstb), dq_til=_dot(t["dob"], st_h.astype(BF16)),
                state=t["e_last"] * jnp.sum(dst_h * st_h, axis=0, keepdims=True)))
            dst[sl, :] = dst_h * t["e_last"] + _dot(do_t[sl, :], t["q_til"].astype(BF16))

        for h, (t, m) in enumerate(zip(fac, first)):
            qh, kh, eb, eq, ekh, ek = t["qh"], t["kh"], t["eb"], t["eq"], t["ekh"], t["ek"]
            q_til, k_hat, qhb, ktb, dob = t["q_til"], t["k_hat"], t["qhb"], t["ktb"], t["dob"]
            dk_hat, dq_til = m["dk_hat"], m["dq_til"]
            dv = m["dv"] + _dot(jnp.where(causal_t, m["att_t"], 0.0).astype(BF16), dob)
            dq_hat = _dot(jnp.where(causal, m["datt"], 0.0).astype(BF16), ktb)
            dk_til = _dot(jnp.where(causal_t, m["datt_t"], 0.0).astype(BF16), qhb)
            db_last = jnp.sum(dk_hat * k_hat, axis=0, keepdims=True) + m["state"]
            dq_sel = jnp.concatenate([dq_hat[SUB * s:SUB * (s + 1), s * HG_HEAD_DIM:(s + 1) * HG_HEAD_DIM]
                                      for s in range(N_SUB)], axis=0)
            dq_a = dq_sel * eq
            dk_rows, k_att_rows = [], []
            for b_ in range(N_SUB):
                rs = slice(SUB * b_, SUB * (b_ + 1))
                dk_sum = k_att_sum = None
                for s in range(b_, N_SUB):
                    cs = slice(s * HG_HEAD_DIM, (s + 1) * HG_HEAD_DIM)
                    d = dk_til[rs, cs]
                    t_dk = d * ek[s][rs, :]
                    t_att = ktb[rs, cs].astype(F32) * d
                    dk_sum = t_dk if dk_sum is None else dk_sum + t_dk
                    k_att_sum = t_att if k_att_sum is None else k_att_sum + t_att
                dk_rows.append(dk_sum)
                k_att_rows.append(k_att_sum)
            dk_a = jnp.concatenate(dk_rows, axis=0)
            db = (dq_til * q_til - dk_hat * k_hat + (qh * eq).astype(BF16).astype(F32) * dq_sel
                  - jnp.concatenate(k_att_rows, axis=0))
            db_parts.append(jnp.where(is_last, db + db_last, db))
            dq_parts.append(dq_til * eb + dq_a)
            dk_parts.append(dk_hat * ekh + dk_a)
            dp_ref[:, 2 * D_HG + h * HG_HEAD_DIM:2 * D_HG + (h + 1) * HG_HEAD_DIM] = dv.astype(BF16)

        gg_ref[...] += gg
        db = jnp.concatenate(db_parts, axis=1)
        dq = jnp.concatenate(dq_parts, axis=1)
        dk = jnp.concatenate(dk_parts, axis=1)
        dlf = jnp.where(valid, _running_sum(db, False), 0.0)
        dp_ref[:, :D_HG] = (dq * sq * (1.0 + hq * (1.0 - sq))).astype(BF16)
        df = dlf / f - dk
        dlb = jnp.sum(df * (1.0 - sf), axis=0, keepdims=True) * lb * (1.0 - lb)
        glb_ref[0:1, :] += dlb
        glb_ref[1:2, :] += -dlb
        dp_ref[:, D_HG:2 * D_HG] = (df * (1.0 - lb) * sf * (1.0 - sf)).astype(BF16)

    rev = lambda j: pl.BlockSpec((rows, D_HG), lambda i: (n_steps - 1 - i, j))
    return pl.pallas_call(
        body, grid=(n_steps,),
        in_specs=[rev(2), rev(3), rev(4), rev(5), rev(0),
                  pl.BlockSpec((cps, D_HG, HG_HEAD_DIM), lambda i: (n_steps - 1 - i, 0, 0)), rev(1),
                  pl.BlockSpec((2, D_HG), lambda i: (0, 0)), pl.BlockSpec((1, HG_HEAD_DIM), lambda i: (0, 0))],
        out_specs=[pl.BlockSpec((rows, D_IN), lambda i: (n_steps - 1 - i, 0)),
                   pl.BlockSpec((2, D_HG), lambda i: (0, 0)), pl.BlockSpec((1, HG_HEAD_DIM), lambda i: (0, 0))],
        out_shape=[jax.ShapeDtypeStruct((T, D_IN), BF16), jax.ShapeDtypeStruct((2, D_HG), F32),
                   jax.ShapeDtypeStruct((1, HG_HEAD_DIM), F32)],
        scratch_shapes=[pltpu.VMEM((D_HG, HG_HEAD_DIM), F32)],
        name="hg_bwd", compiler_params=_params("arbitrary"),
    )(p, p, p, p, o_all, st_all, dy, lbraw, hg_g)


def _in_bwd(dp, w_in, h0, g1, dh1):
    T = h0.shape[0]
    tm = _row_tile(T, 832)
    n_steps = T // tm

    def body(dp_ref, w_ref, h_ref, g_ref, d1_ref, gx_hbm, gmeta_ref, gg_ref, buf, sems):
        i = pl.program_id(0)
        first, later = _window_copies(gx_hbm, buf, sems, tm)
        slot = i % 2

        @pl.when(i == 0)
        def _():
            gg_ref[...] = jnp.zeros_like(gg_ref)

        if n_steps > 2:
            @pl.when(i == 2)
            def _():
                first(False).wait()

            @pl.when(i > 2)
            def _():
                later(i - 2, slot, False).wait()

        du = _dot_nt(dp_ref[...], w_ref[...])
        h0_ = h_ref[...]
        r = _rms(h0_)
        n = h0_ * r
        gg_ref[...] += jnp.sum(du * n, axis=0, keepdims=True)
        dh0 = d1_ref[...] + _rms_bwd(du * g_ref[...], n, r)
        buf[slot] = dh0

        @pl.when(i == 0)
        def _():
            gmeta_ref[...] = dh0[PAD:HEAD, :]
            first(False).start()

        if n_steps > 1:
            @pl.when(i > 0)
            def _():
                later(i, slot, False).start()

        @pl.when(i == n_steps - 1)
        def _():
            if n_steps == 1:
                first(False).wait()
            else:
                if n_steps == 2:
                    first(False).wait()
                else:
                    later(i - 1, 1 - slot, False).wait()
                later(i, slot, False).wait()

    row = lambda n: pl.BlockSpec((tm, n), lambda i: (i, 0))
    return pl.pallas_call(
        body, grid=(n_steps,),
        in_specs=[row(D_IN), _resident((D_MODEL, D_IN)),
                  row(D_MODEL), pl.BlockSpec((1, D_MODEL), lambda i: (0, 0)), row(D_MODEL)],
        out_specs=[pl.BlockSpec(memory_space=pl.ANY), pl.BlockSpec((N_META, D_MODEL), lambda i: (0, 0)),
                   pl.BlockSpec((1, D_MODEL), lambda i: (0, 0))],
        out_shape=[jax.ShapeDtypeStruct((T - HEAD, D_MODEL), F32), jax.ShapeDtypeStruct((N_META, D_MODEL), F32),
                   jax.ShapeDtypeStruct((1, D_MODEL), F32)],
        scratch_shapes=[pltpu.VMEM((2, tm, D_MODEL), F32), pltpu.SemaphoreType.DMA((2,))],
        name="in_bwd", compiler_params=_params("arbitrary"),
    )(dp, w_in, h0, g1, dh1)


def _col_tile(cols, target):
    best = None
    for t in range(128, min(cols, target) + 1, 128):
        if cols % t == 0:
            best = t
    assert best is not None, cols
    return best


MXU_DIM = 256


def _mxu_tile(cols, target):
    best = None
    for t in range(MXU_DIM, min(cols, target) + 1, MXU_DIM):
        if cols % t == 0:
            best = t
    assert best is not None, cols
    return best


def _weight_grad(a, b, name):
    T, M = a.shape
    N = b.shape[1]
    tm = _col_tile(M, 1408)
    tn = _mxu_tile(N, 768 if tm <= 1024 else 512)

    def body(a_ref, b_ref, o_ref, ob_ref):
        o = _dot_tn(a_ref[...], b_ref[...])
        o_ref[...] = o
        ob_ref[...] = o.astype(BF16)

    return pl.pallas_call(
        body, grid=(M // tm, N // tn),
        in_specs=[pl.BlockSpec((T, tm), lambda m, n: (0, m)), pl.BlockSpec((T, tn), lambda m, n: (0, n))],
        out_specs=[pl.BlockSpec((tm, tn), lambda m, n: (m, n))] * 2,
        out_shape=[jax.ShapeDtypeStruct((M, N), F32), jax.ShapeDtypeStruct((M, N), BF16)],
        name=name, compiler_params=_params("parallel", "parallel"),
    )(a, b)


def _weight_grad_chip_sum(a, b, name):
    T, M = a.shape
    N = b.shape[1]
    tn = _mxu_tile(N, 768)
    steps = N // tn
    half = M // 2

    def body(a_ref, b_ref, sum_ref, sumb_ref, acc, own, got, stage, send_sems, recv_sems):
        n = pl.program_id(0)
        x, y, c = _place()
        slot = n % 2

        def piece(k):
            return _remote(stage.at[k], got.at[k], send_sems, recv_sems, k, (x, y, 1 - c))

        def matmul():
            acc[...] = _dot_tn(a_ref[...], b_ref[...])

        def send_and_keep():
            stage[n] = acc[pl.ds(pl.multiple_of((1 - c) * half, 128), half), :].astype(BF16)
            piece(n).start()
            own[slot] = acc[pl.ds(pl.multiple_of(c * half, 128), half), :]

        def chip_sum():
            t = own[1 - slot] + got[n - 1].astype(F32)
            sum_ref[...] = t
            sumb_ref[...] = t.astype(BF16)

        @pl.when(n == 0)
        def _():
            matmul()
            send_and_keep()

        @pl.when((n >= 1) & (n < steps))
        def _():
            piece(n - 1).wait_recv()
            matmul()
            chip_sum()
            send_and_keep()

        @pl.when(n == steps)
        def _():
            piece(n - 1).wait_recv()
            chip_sum()
            for k in range(steps):
                piece(k).wait_send()

    last = steps - 1
    return pl.pallas_call(
        body, grid=(steps + 1,),
        in_specs=[_resident((T, M)), pl.BlockSpec((T, tn), lambda n: (0, jnp.minimum(n, last)))],
        out_specs=[pl.BlockSpec((half, tn), lambda n: (0, jnp.maximum(n - 1, 0)))] * 2,
        out_shape=[jax.ShapeDtypeStruct((half, N), F32), jax.ShapeDtypeStruct((half, N), BF16)],
        scratch_shapes=[pltpu.VMEM((M, tn), F32), pltpu.VMEM((2, half, tn), F32), pltpu.VMEM((steps, half, tn), BF16),
                        pltpu.VMEM((steps, half, tn), BF16), pltpu.SemaphoreType.DMA((steps,)),
                        pltpu.SemaphoreType.DMA((steps,))],
        name=name, compiler_params=_params("arbitrary"),
    )(a, b)


def _local_step(x, meta, target, w_in_own, w_in, w_out, w_gu, w_down, small, chip, on_ffn_grads=None,
                on_mixer_grads=None):
    wg = _gate_weights(small["w_rgate"], small["w_igate"])
    bg = jnp.concatenate([small["b_rgate"], small["b_igate"]], axis=1)

    p, u, h0 = _in_proj_local(x, meta, small["mix_norm_g"], w_in_own, chip)
    p = _in_proj_rest(u, w_in, p, chip)
    y_rg, hs, xc = _rg_fwd(p, small["conv_w"], small["conv_b"], wg, bg, small["lru_lambda"], small["rg_norm_g"])
    y_hg, o_all, st_all = _hg_fwd(p, small["hg_lower_bound"], small["hg_norm_g"])
    h1, v, yb, gu, act, dh2, dh2b, loss, g_final = _ffn_fwd(
        h0, y_rg, y_hg, w_out, small["ffn_norm_g"], w_gu, w_down, small["final_norm_g"], target)

    g_w_down = _weight_grad(act, dh2b, "grad_w_down")
    dgu, dh1, dh1b, dy, g_ffn = _ffn_bwd(dh2b, gu, w_down, w_gu, h1, small["ffn_norm_g"], dh2, w_out)
    ffn_grads = {"w_down": g_w_down, "w_out": _weight_grad(yb, dh1b, "grad_w_out")}
    if on_ffn_grads is None:
        ffn_grads["w_gate_up"] = _weight_grad(v, dgu, "grad_w_gate_up")
        stages = None
    else:
        gate_up_sums = _weight_grad_chip_sum(v, dgu, "grad_w_gate_up")
        ffn_grads["w_gate_up"] = (None, None)
        stages = on_ffn_grads(ffn_grads)
    dp, g_lb, g_hgn = _hg_bwd(p, o_all, st_all, dy, small["hg_lower_bound"], small["hg_norm_g"])
    early = late = None
    if stages is not None:
        chip_sums, send = stages
        sums = dict(chip_sums(), w_gate_up=gate_up_sums)
        (dp, dy), sums = lax.optimization_barrier(((dp, dy), sums))
        early = send(sums)
    dp, g_cw, g_cb, g_wgate, g_bg, g_lam, g_rgn = _rg_bwd(
        p, xc, hs, dy, dp, small["conv_w"], small["conv_b"], wg, bg, small["lru_lambda"], small["rg_norm_g"])
    if on_mixer_grads is None:
        g_w_in = _weight_grad(u, dp, "grad_w_in")[0]
    else:
        sums = {"w_in": _weight_grad_chip_sum(u, dp, "grad_w_in")}
        (dp, dh1), sums = lax.optimization_barrier(((dp, dh1), sums))
        late = on_mixer_grads(sums)
        g_w_in = None
    grad_x, g_meta, g_mix = _in_bwd(dp, w_in, h0, small["mix_norm_g"], dh1)

    grads = {
        "w_in": g_w_in, "w_out": ffn_grads["w_out"][0],
        "w_gate_up": ffn_grads["w_gate_up"][0], "w_down": ffn_grads["w_down"][0],
        "meta_tokens": g_meta, "mix_norm_g": g_mix, "conv_w": g_cw, "conv_b": g_cb, "w_gates": g_wgate,
        "b_rgate": g_bg[:, :D_RG], "b_igate": g_bg[:, D_RG:], "lru_lambda": g_lam, "rg_norm_g": g_rgn,
        "hg_lower_bound": g_lb, "hg_norm_g": g_hgn, "ffn_norm_g": g_ffn, "final_norm_g": g_final,
    }
    return loss, grad_x, grads, early, late


ANY = pl.BlockSpec(memory_space=pl.ANY)
HALF = D_MODEL // 2

BIG = {"w_in": (D_MODEL, D_IN // N_CHIPS, True), "w_gate_up": (D_MODEL, 2 * D_FF // N_CHIPS, True),
       "w_out": (D_MODEL // N_CHIPS, D_MODEL, False), "w_down": (D_FF // N_CHIPS, D_MODEL, False)}
BIG_NAMES = tuple(BIG)
N_BIG = len(BIG_NAMES)


def _full_shape(name):
    rows, cols, by_col = BIG[name]
    return (rows, cols * N_CHIPS) if by_col else (rows * N_CHIPS, cols)


def _place():
    return lax.axis_index("x"), lax.axis_index("y"), lax.axis_index("c")


def _chip_of(x, y, r):
    fx, fy = (r + 1) >> 1, (r + 1) & 1
    return (1 - x if fx else x), (1 - y if fy else y)


def _half_of(ref, by_col, half):
    start = pl.multiple_of(half * HALF, 128)
    return ref.at[pl.ds(start, HALF), :] if by_col else ref.at[:, pl.ds(start, HALF)]


def _shard_of(ref, name, chip):
    rows, cols, by_col = BIG[name]
    if by_col:
        return ref.at[:, pl.ds(pl.multiple_of(chip * cols, 128), cols)]
    return ref.at[pl.ds(pl.multiple_of(chip * rows, 16), rows), :]


def _shard_half_of(ref, name, chip, half):
    rows, cols, by_col = BIG[name]
    start = pl.multiple_of(half * HALF, 128)
    if by_col:
        return ref.at[pl.ds(start, HALF), pl.ds(pl.multiple_of(chip * cols, 128), cols)]
    return ref.at[pl.ds(pl.multiple_of(chip * rows, 16), rows), pl.ds(start, HALF)]


def _shard_half_part_of(ref, name, chip, half, part):
    rows, cols, by_col = BIG[name]
    start = pl.multiple_of(half * HALF + part * (HALF // 2), 128)
    if by_col:
        return ref.at[pl.ds(start, HALF // 2), pl.ds(pl.multiple_of(chip * cols, 128), cols)]
    return ref.at[pl.ds(pl.multiple_of(chip * rows, 16), rows), pl.ds(start, HALF // 2)]


def _remote(src, dst, send_sems, recv_sems, k, dev):
    return pltpu.make_async_remote_copy(src_ref=src, dst_ref=dst, send_sem=send_sems.at[k], recv_sem=recv_sems.at[k],
                                        device_id=dev, device_id_type=MESH)


def _place_shards(w, small, chip, names, label):
    steps = 4
    n, ns = len(names), len(small)
    in_specs, out_specs = [], []
    for name in names:
        rows, cols, by_col = BIG[name]
        tr = rows // steps
        in_specs.append(pl.BlockSpec((tr, cols), lambda i, s: (i, 0)))
        if by_col:
            out_specs.append(pl.BlockSpec((tr, cols), lambda i, s: (i, s[0])))
        else:
            out_specs.append(pl.BlockSpec((tr, cols), lambda i, s: (s[0] * steps + i, 0)))

    def body(s_ref, *refs):
        ins, small_in = refs[:n], refs[n:n + ns]
        outs, small_out = refs[n + ns:2 * n + ns], refs[2 * n + ns:2 * (n + ns)]
        send_sems, recv_sems, local_sems = refs[2 * (n + ns):]
        i = pl.program_id(0)
        x, y, c = _place()
        chip_ = 2 * x + y
        others = [_chip_of(x, y, r) for r in range(3)]

        def block(a, q):
            cols = small[a].shape[1]
            return small_out[a].at[:, pl.ds(pl.multiple_of(q * cols, 128), cols)]

        def local(a):
            return pltpu.make_async_copy(small_in[a], block(a, chip_), local_sems.at[a])

        def remote(a, r):
            qx, qy = others[r]
            return _remote(small_in[a], block(a, chip_), send_sems, recv_sems, 3 * a + r, (qx, qy, c))

        @pl.when(i == 0)
        def _():
            for a in range(ns):
                local(a).start()
                for r in range(3):
                    remote(a, r).start()

        for a in range(n):
            outs[a][...] = ins[a][...].astype(BF16)

        @pl.when(i == steps - 1)
        def _():
            for a in range(ns):
                for r, (qx, qy) in enumerate(others):
                    landed = block(a, 2 * qx + qy)
                    _remote(landed, landed, send_sems, recv_sems, 3 * a + r, (qx, qy, c)).wait_recv()
                for r in range(3):
                    remote(a, r).wait_send()
                local(a).wait()

    out = pl.pallas_call(
        body,
        grid_spec=pltpu.PrefetchScalarGridSpec(
            num_scalar_prefetch=1, grid=(steps,), in_specs=in_specs + [ANY] * ns, out_specs=out_specs + [ANY] * ns,
            scratch_shapes=[pltpu.SemaphoreType.DMA((max(3 * ns, 1),)), pltpu.SemaphoreType.DMA((max(3 * ns, 1),)),
                            pltpu.SemaphoreType.DMA((max(ns, 1),))]),
        out_shape=([jax.ShapeDtypeStruct(_full_shape(name), BF16) for name in names]
                   + [jax.ShapeDtypeStruct((s.shape[0], s.shape[1] * N_CHIPS), F32) for s in small]),
        name=label, compiler_params=_params("arbitrary"),
    )(chip, *[w[name] for name in names], *small)
    return dict(zip(names, out[:n])), list(out[n:])


def _gather_weights(placed, small, names, label, collective_id):
    n, ns = len(names), len(small)
    hbm = pltpu.MemorySpace.HBM
    outs = [jax.new_ref(placed[nm], memory_space=hbm) for nm in names]
    small_in = [jax.new_ref(s, memory_space=hbm) for s in small]
    small_out = [jax.empty_ref(jax.ShapeDtypeStruct((s.shape[0], s.shape[1] * N_CHIPS), F32), memory_space=hbm)
                 for s in small]
    n_sems = 8 * n + 3 * ns

    @pl.kernel(mesh=plsc.ScalarSubcoreMesh(axis_name="seq", num_cores=1), name=label, out_type=(),
               scratch_types=(pltpu.SemaphoreType.DMA((n_sems,)), pltpu.SemaphoreType.DMA((n_sems,)),
                              pltpu.SemaphoreType.DMA((max(ns, 1),))),
               compiler_params=pltpu.CompilerParams(collective_id=collective_id))
    def launch(send_sems, recv_sems, local_sems):
        x, y, c = _place()
        chip = 2 * x + y
        sibling = (x, y, 1 - c)
        others = [_chip_of(x, y, r) for r in range(3)]
        near = others[:2]
        far = 2 * others[2][0] + others[2][1]
        _handshake([(qx, qy, c) for qx, qy in others] + [sibling])

        def small_block(a, q):
            cols = small[a].shape[1]
            return small_out[a].at[:, pl.ds(pl.multiple_of(q * cols, 128), cols)]

        local = [pltpu.make_async_copy(small_in[a], small_block(a, chip), local_sems.at[a]) for a in range(ns)]
        for cp in local:
            cp.start()

        sends = []
        for a, name in enumerate(names):
            mine = _shard_half_of(outs[a], name, chip, c)
            for r, (qx, qy) in enumerate(near):
                sends.append(_remote(mine, mine, send_sems, recv_sems, 8 * a + r, (qx, qy, c)))
        for a in range(ns):
            for r, (qx, qy) in enumerate(others):
                sends.append(_remote(small_in[a], small_block(a, chip), send_sems, recv_sems,
                                     8 * n + 3 * a + r, (qx, qy, c)))
        for cp in sends:
            cp.start()

        forwards = []

        def forward(piece, k, dev):
            cp = _remote(piece, piece, send_sems, recv_sems, k, dev)
            cp.start()
            forwards.append(cp)

        for a, name in enumerate(names):
            for r, (qx, qy) in enumerate(near):
                landed = _shard_half_of(outs[a], name, 2 * qx + qy, c)
                _remote(landed, landed, send_sems, recv_sems, 8 * a + r, (qx, qy, c)).wait_recv()
                ox, oy = near[1 - r]
                forward(_shard_half_part_of(outs[a], name, 2 * qx + qy, c, r), 8 * a + 2 + r, (ox, oy, c))
                forward(landed, 8 * a + 4 + r, sibling)
        for a, name in enumerate(names):
            for part in range(2):
                qx, qy = near[1 - part]
                landed = _shard_half_part_of(outs[a], name, far, c, part)
                _remote(landed, landed, send_sems, recv_sems, 8 * a + 2 + part, (qx, qy, c)).wait_recv()
                forward(landed, 8 * a + 6 + part, sibling)
        for a in range(ns):
            for r, (qx, qy) in enumerate(others):
                landed = small_block(a, 2 * qx + qy)
                _remote(landed, landed, send_sems, recv_sems, 8 * n + 3 * a + r, (qx, qy, c)).wait_recv()
        for a, name in enumerate(names):
            for r, (qx, qy) in enumerate(near):
                landed = _shard_half_of(outs[a], name, 2 * qx + qy, 1 - c)
                _remote(landed, landed, send_sems, recv_sems, 8 * a + 4 + r, sibling).wait_recv()
            for part in range(2):
                landed = _shard_half_part_of(outs[a], name, far, 1 - c, part)
                _remote(landed, landed, send_sems, recv_sems, 8 * a + 6 + part, sibling).wait_recv()
        for cp in sends + forwards:
            cp.wait_send()
        for cp in local:
            cp.wait()

    launch()
    return {nm: ref[...] for nm, ref in zip(names, outs)}, [ref[...] for ref in small_out]


def _exchange_halves(grads, names, label, collective_id):
    n = len(names)

    def body(*refs):
        ins, outs = refs[:n], refs[n:2 * n]
        send_sems, recv_sems = refs[2 * n:]
        x, y, c = _place()
        _handshake([(x, y, 1 - c)])
        copies = []
        for a, name in enumerate(names):
            copies.append(_remote(_half_of(ins[a], BIG[name][2], 1 - c), outs[a], send_sems, recv_sems, a,
                                  (x, y, 1 - c)))
        for cp in copies:
            cp.start()
        for cp in copies:
            cp.wait()

    def half_shape(name):
        r, c_ = _full_shape(name)
        return (HALF, c_) if BIG[name][2] else (r, HALF)

    out_type = tuple(jax.ShapeDtypeStruct(half_shape(nm), grads[nm].dtype) for nm in names)
    sems = (pltpu.SemaphoreType.DMA((n,)), pltpu.SemaphoreType.DMA((n,)))
    got = pl.kernel(
        body, mesh=plsc.ScalarSubcoreMesh(axis_name="seq", num_cores=1), name=label, out_type=out_type,
        scratch_types=sems, compiler_params=pltpu.CompilerParams(collective_id=collective_id),
    )(*[grads[nm] for nm in names])
    return dict(zip(names, got))


def _chip_sum(grads, got, names, core, label):
    n = len(names)
    steps = 4
    g_specs, blks = [], []
    for name in names:
        rows, cols = got[name].shape
        tr = rows // steps
        if BIG[name][2]:
            g_specs.append(pl.BlockSpec((tr, cols), lambda i, s: (s[0] * steps + i, 0)))
        else:
            g_specs.append(pl.BlockSpec((tr, HALF), lambda i, s: (i, s[0])))
        blks.append(pl.BlockSpec((tr, cols), lambda i, s: (i, 0)))

    def body(s_ref, *refs):
        for a in range(n):
            t = refs[a][...] + refs[n + a][...].astype(F32)
            refs[2 * n + a][...] = t
            refs[3 * n + a][...] = t.astype(BF16)

    out = pl.pallas_call(
        body,
        grid_spec=pltpu.PrefetchScalarGridSpec(num_scalar_prefetch=1, grid=(steps,), in_specs=g_specs + blks,
                                               out_specs=blks + blks),
        out_shape=([jax.ShapeDtypeStruct(got[nm].shape, F32) for nm in names]
                   + [jax.ShapeDtypeStruct(got[nm].shape, BF16) for nm in names]),
        name=label, compiler_params=_params("parallel"),
    )(core, *[grads[nm] for nm in names], *[got[nm] for nm in names])
    return {nm: (out[a], out[n + a]) for a, nm in enumerate(names)}


def _piece_shape(name):
    rows, cols, by_col = BIG[name]
    return (HALF, cols) if by_col else (rows, HALF)


def _handshake(peers):
    barrier = pltpu.get_barrier_semaphore()
    for peer in peers:
        pl.semaphore_signal(barrier, inc=1, device_id=peer, device_id_type=MESH)
    pl.semaphore_wait(barrier, len(peers))


def _send_chip_sums(sums, names, label, collective_id):
    n = len(names)

    def body(*refs):
        ins, outs = refs[:n], refs[n:2 * n]
        send_sems, recv_sems = refs[2 * n:]
        x, y, c = _place()
        others = [_chip_of(x, y, r) for r in range(3)]
        _handshake([(qx, qy, c) for qx, qy in others])
        copies = []
        for a, name in enumerate(names):
            for r, (qx, qy) in enumerate(others):
                copies.append(_remote(_shard_of(ins[a], name, 2 * qx + qy), outs[a].at[r], send_sems, recv_sems,
                                      3 * a + r, (qx, qy, c)))
        for cp in copies:
            cp.start()
        for cp in copies:
            cp.wait()

    return pl.kernel(
        body, mesh=plsc.ScalarSubcoreMesh(axis_name="seq", num_cores=1), name=label,
        out_type=tuple(jax.ShapeDtypeStruct((3,) + _piece_shape(nm), BF16) for nm in names),
        scratch_types=(pltpu.SemaphoreType.DMA((3 * n,)), pltpu.SemaphoreType.DMA((3 * n,))),
        compiler_params=pltpu.CompilerParams(collective_id=collective_id),
    )(*[sums[nm] for nm in names])


def _total(parts, chip_core):
    steps = 2
    in_specs, out_specs, operands = [], [], []
    for name in BIG_NAMES:
        by_col = BIG[name][2]
        pr, pc = _piece_shape(name)
        tr = pr // steps
        if by_col:
            in_specs.append(pl.BlockSpec((tr, pc), lambda i, s: (i, s[0])))
            out_specs.append(pl.BlockSpec((tr, pc), lambda i, s: (s[1] * steps + i, 0)))
        else:
            in_specs.append(pl.BlockSpec((tr, pc), lambda i, s: (s[0] * steps + i, 0)))
            out_specs.append(pl.BlockSpec((tr, pc), lambda i, s: (i, s[1])))
        for r in range(3):
            in_specs.append(pl.BlockSpec((None, tr, pc), lambda i, s, r=r: (r, i, 0)))
        own, got = parts[name]
        operands += [own, got, got, got]

    def body(s_ref, *refs):
        for a in range(N_BIG):
            o_ref, a_ref, b_ref, c_ref = refs[4 * a:4 * a + 4]
            refs[4 * N_BIG + a][...] = (((o_ref[...] + a_ref[...].astype(F32)) + b_ref[...].astype(F32))
                                        + c_ref[...].astype(F32))

    totals = pl.pallas_call(
        body,
        grid_spec=pltpu.PrefetchScalarGridSpec(num_scalar_prefetch=1, grid=(steps,), in_specs=in_specs,
                                               out_specs=out_specs),
        out_shape=[jax.ShapeDtypeStruct(BIG[name][:2], F32) for name in BIG_NAMES],
        name="totals", compiler_params=_params("parallel"),
    )(chip_core, *operands)
    return dict(zip(BIG_NAMES, totals))


VEC_ROWS = 32
VEC_ROW = {"mix_norm_g": 0, "conv_b": 1, "b_rgate": 2, "b_igate": 3, "lru_lambda": 4, "rg_norm_g": 5,
           "hg_lower_bound": 6, "hg_norm_g": 8, "ffn_norm_g": 9, "final_norm_g": 10, "loss": 11,
           "conv_w": 12, "meta_tokens": 16}
N_DEV = 8


def _all_reduce_small(pieces, gates, totals):
    names = list(pieces)
    n_small = 10
    hv, hg = VEC_ROWS // 2, gates.shape[0] // 2

    def body(*refs):
        ins = refs[:len(names)]
        g_ref = refs[len(names)]
        vec_ref, gsum_ref = refs[len(names) + 1 + N_BIG:len(names) + 3 + N_BIG]
        big = refs[len(names) + 3 + N_BIG:len(names) + 3 + 2 * N_BIG]
        (mine_v, sib_v, sib_g, chip_v, chip_g, got_v, got_g, send_sems, recv_sems) = refs[len(names) + 3 + 2 * N_BIG:]
        x, y, c = _place()
        chip = 2 * x + y
        sibling = (x, y, 1 - c)
        share = []
        for a, name in enumerate(BIG_NAMES):
            half = _half_of(big[a], BIG[name][2], c)
            share.append(_remote(half, half, send_sems, recv_sems, n_small + a, sibling))
        mine_v[...] = jnp.zeros_like(mine_v)
        for name, ref in zip(names, ins):
            nr, w = ref.shape
            mine_v[VEC_ROW[name]:VEC_ROW[name] + nr, 0:w] = ref[...]

        swap = [_remote(mine_v, sib_v, send_sems, recv_sems, 0, sibling),
                _remote(g_ref, sib_g, send_sems, recv_sems, 1, sibling)]
        for cp in swap:
            cp.start()
        for cp in swap:
            cp.wait()
        for cp in share:
            cp.start()
        chip_v[...] = mine_v[...] + sib_v[...]
        chip_g[...] = g_ref[...] + sib_g[...]

        rows_v = pl.ds(pl.multiple_of(c * hv, 8), hv)
        rows_g = pl.ds(pl.multiple_of(c * hg, 8), hg)
        got_v[chip] = chip_v[rows_v, :]
        got_g[chip] = chip_g[rows_g, :].astype(BF16)
        sends = []
        for r in range(3):
            qx, qy = _chip_of(x, y, r)
            sends.append(_remote(chip_v.at[rows_v, :], got_v.at[chip], send_sems, recv_sems, 2 + r, (qx, qy, c)))
            sends.append(_remote(got_g.at[chip], got_g.at[chip], send_sems, recv_sems, 5 + r, (qx, qy, c)))
        for cp in sends:
            cp.start()
        for cp in sends:
            cp.wait()
        vec_ref[rows_v, :] = ((got_v[0] + got_v[1]) + got_v[2]) + got_v[3]
        gsum_ref[rows_g, :] = ((got_g[0].astype(F32) + got_g[1].astype(F32)) + got_g[2].astype(F32)
                               ) + got_g[3].astype(F32)

        back = [_remote(vec_ref.at[rows_v, :], vec_ref.at[rows_v, :], send_sems, recv_sems, 8, sibling),
                _remote(gsum_ref.at[rows_g, :], gsum_ref.at[rows_g, :], send_sems, recv_sems, 9, sibling)]
        for cp in back:
            cp.start()
        theirs_v = vec_ref.at[pl.ds(pl.multiple_of((1 - c) * hv, 8), hv), :]
        theirs_g = gsum_ref.at[pl.ds(pl.multiple_of((1 - c) * hg, 8), hg), :]
        _remote(theirs_v, theirs_v, send_sems, recv_sems, 8, sibling).wait_recv()
        _remote(theirs_g, theirs_g, send_sems, recv_sems, 9, sibling).wait_recv()
        for cp in back:
            cp.wait_send()
        for a, name in enumerate(BIG_NAMES):
            theirs = _half_of(big[a], BIG[name][2], 1 - c)
            _remote(theirs, theirs, send_sems, recv_sems, n_small + a, sibling).wait_recv()
        for cp in share:
            cp.wait_send()

    vmem = pl.BlockSpec(memory_space=pltpu.VMEM)
    n_sems = n_small + N_BIG
    out = pl.pallas_call(
        body, in_specs=[vmem] * (len(names) + 1) + [ANY] * N_BIG, out_specs=[vmem, vmem] + [ANY] * N_BIG,
        out_shape=([jax.ShapeDtypeStruct((VEC_ROWS, D_MODEL), F32), jax.ShapeDtypeStruct(gates.shape, F32)]
                   + [jax.ShapeDtypeStruct(BIG[n][:2], F32) for n in BIG_NAMES]),
        input_output_aliases={len(names) + 1 + a: 2 + a for a in range(N_BIG)},
        scratch_shapes=[pltpu.VMEM((VEC_ROWS, D_MODEL), F32), pltpu.VMEM((VEC_ROWS, D_MODEL), F32),
                        pltpu.VMEM(gates.shape, F32), pltpu.VMEM((VEC_ROWS, D_MODEL), F32),
                        pltpu.VMEM(gates.shape, F32), pltpu.VMEM((N_CHIPS, hv, D_MODEL), F32),
                        pltpu.VMEM((N_CHIPS, hg) + gates.shape[1:], BF16),
                        pltpu.SemaphoreType.DMA((n_sems,)), pltpu.SemaphoreType.DMA((n_sems,))],
        name="all_reduce_small",
    )(*[pieces[n] for n in names], gates, *[totals[n] for n in BIG_NAMES])
    return out[0], out[1], dict(zip(BIG_NAMES, out[2:]))


def _adamw_math(w, g, m, v):
    m = ADAM_B1 * m + (1.0 - ADAM_B1) * g
    v = ADAM_B2 * v + (1.0 - ADAM_B2) * (g * g)
    m_hat = m / (1.0 - ADAM_B1 ** ADAM_STEP)
    v_hat = v / (1.0 - ADAM_B2 ** ADAM_STEP)
    delta = -ADAM_LR * (m_hat / (jnp.sqrt(v_hat) + ADAM_EPS) + ADAM_WD * w)
    return delta, m, v


def _adamw_big(w, g, m, v):
    steps = 8
    blks = []
    for name in BIG_NAMES:
        rows, cols, _ = BIG[name]
        blks.append(pl.BlockSpec((rows // steps, cols), lambda i: (i, 0)))

    def body(*refs):
        ins, outs = refs[:4 * N_BIG], refs[4 * N_BIG:]
        for a in range(N_BIG):
            w_ref, g_ref, m_ref, v_ref = (ins[k * N_BIG + a] for k in range(4))
            g = g_ref[...]
            d, nm, nv = _adamw_math(w_ref[...], g, m_ref[...], v_ref[...])
            outs[a][...] = g
            outs[N_BIG + a][...] = d
            outs[2 * N_BIG + a][...] = nm
            outs[3 * N_BIG + a][...] = nv

    shapes = [jax.ShapeDtypeStruct(BIG[name][:2], F32) for name in BIG_NAMES]
    out = pl.pallas_call(
        body, grid=(steps,), in_specs=blks * 4, out_specs=blks * 4, out_shape=shapes * 4,
        name="adamw_big", compiler_params=_params("parallel"),
    )(*[t[name] for t in (w, g, m, v) for name in BIG_NAMES])
    return {name: tuple(out[k * N_BIG + a] for k in range(4)) for a, name in enumerate(BIG_NAMES)}


SMALL = {"meta_tokens": (N_META, D_MODEL // N_CHIPS), "mix_norm_g": (1, D_MODEL), "conv_w": (CONV_W, D_RG // N_CHIPS),
         "conv_b": (1, D_RG), "w_rgate": (D_RG, RG_HEAD_DIM), "b_rgate": (1, D_RG), "w_igate": (D_RG, RG_HEAD_DIM),
         "b_igate": (1, D_RG), "lru_lambda": (1, D_RG), "rg_norm_g": (1, D_RG), "hg_lower_bound": (2, D_HG),
         "hg_norm_g": (1, HG_HEAD_DIM), "ffn_norm_g": (1, D_MODEL), "final_norm_g": (1, D_MODEL)}
SMALL_NAMES = tuple(SMALL)
SHARDED_SMALL = ("meta_tokens", "conv_w")


def _adamw_small(vec, gates, w, m, v):
    n = len(SMALL_NAMES)

    def body(*refs):
        vec_ref, gates_ref = refs[:2]
        w_refs, m_refs, v_refs = refs[2:2 + n], refs[2 + n:2 + 2 * n], refs[2 + 2 * n:2 + 3 * n]
        outs = refs[2 + 3 * n:]
        loss_ref = outs[0]
        x, y, _ = _place()
        chip = 2 * x + y
        loss_ref[...] = vec_ref[VEC_ROW["loss"]:VEC_ROW["loss"] + 1, 0:1]

        def update(k, g):
            g_ref, d_ref, nm_ref, nv_ref = outs[1 + 4 * k:5 + 4 * k]
            g_ref[...] = g
            d_ref[...], nm_ref[...], nv_ref[...] = _adamw_math(w_refs[k][...], g, m_refs[k][...], v_refs[k][...])

        for k, name in enumerate(SMALL_NAMES):
            nr, w_ = SMALL[name]
            if name == "w_rgate":
                update(k, gates_ref[0:D_RG, :])
            elif name == "w_igate":
                update(k, gates_ref[D_RG:2 * D_RG, :])
            elif name in SHARDED_SMALL:
                r0 = VEC_ROW[name]
                for q in range(N_CHIPS):
                    @pl.when(chip == q)
                    def _(k=k, r0=r0, nr=nr, w_=w_, q=q):
                        update(k, vec_ref[r0:r0 + nr, q * w_:(q + 1) * w_])
            else:
                r0 = VEC_ROW[name]
                update(k, vec_ref[r0:r0 + nr, 0:w_])

    vmem = pl.BlockSpec(memory_space=pltpu.VMEM)
    out_shape = [jax.ShapeDtypeStruct((1, 1), F32)]
    for name in SMALL_NAMES:
        out_shape += [jax.ShapeDtypeStruct(SMALL[name], F32)] * 4
    outs = pl.pallas_call(
        body, in_specs=[vmem] * (2 + 3 * n), out_specs=[vmem] * len(out_shape), out_shape=out_shape,
        name="adamw_small",
    )(vec, gates, *[w[k] for k in SMALL_NAMES], *[m[k] for k in SMALL_NAMES], *[v[k] for k in SMALL_NAMES])
    loss = outs[0]
    res = {name: tuple(outs[1 + 4 * k:5 + 4 * k]) for k, name in enumerate(SMALL_NAMES)}
    return loss, res


WEIGHT_NAMES = ("meta_tokens", "mix_norm_g", "w_in", "conv_w", "conv_b", "w_rgate", "b_rgate", "w_igate", "b_igate",
                "lru_lambda", "rg_norm_g", "hg_lower_bound", "hg_norm_g", "w_out", "ffn_norm_g", "w_gate_up", "w_down",
                "final_norm_g")


def _to_2d(name, a):
    if name in BIG:
        return a.reshape(BIG[name][:2])
    return a.reshape(SMALL[name])


def kernel(x, meta_tokens, mix_norm_g, w_in, conv_w, conv_b, w_rgate, b_rgate, w_igate, b_igate, lru_lambda, rg_norm_g, hg_lower_bound, hg_norm_g, w_out, ffn_norm_g, w_gate_up, w_down, final_norm_g, loss_target, m_meta_tokens, m_mix_norm_g, m_w_in, m_conv_w, m_conv_b, m_w_rgate, m_b_rgate, m_w_igate, m_b_igate, m_lru_lambda, m_rg_norm_g, m_hg_lower_bound, m_hg_norm_g, m_w_out, m_ffn_norm_g, m_w_gate_up, m_w_down, m_final_norm_g, v_meta_tokens, v_mix_norm_g, v_w_in, v_conv_w, v_conv_b, v_w_rgate, v_b_rgate, v_w_igate, v_b_igate, v_lru_lambda, v_rg_norm_g, v_hg_lower_bound, v_hg_norm_g, v_w_out, v_ffn_norm_g, v_w_gate_up, v_w_down, v_final_norm_g):
    w_raw = dict(zip(WEIGHT_NAMES, (meta_tokens, mix_norm_g, w_in, conv_w, conv_b, w_rgate, b_rgate, w_igate, b_igate,
                                    lru_lambda, rg_norm_g, hg_lower_bound, hg_norm_g, w_out, ffn_norm_g, w_gate_up,
                                    w_down, final_norm_g)))
    m_raw = dict(zip(WEIGHT_NAMES, (m_meta_tokens, m_mix_norm_g, m_w_in, m_conv_w, m_conv_b, m_w_rgate, m_b_rgate,
                                    m_w_igate, m_b_igate, m_lru_lambda, m_rg_norm_g, m_hg_lower_bound, m_hg_norm_g,
                                    m_w_out, m_ffn_norm_g, m_w_gate_up, m_w_down, m_final_norm_g)))
    v_raw = dict(zip(WEIGHT_NAMES, (v_meta_tokens, v_mix_norm_g, v_w_in, v_conv_w, v_conv_b, v_w_rgate, v_b_rgate,
                                    v_w_igate, v_b_igate, v_lru_lambda, v_rg_norm_g, v_hg_lower_bound, v_hg_norm_g,
                                    v_w_out, v_ffn_norm_g, v_w_gate_up, v_w_down, v_final_norm_g)))
    w = {k: _to_2d(k, a) for k, a in w_raw.items()}
    m = {k: _to_2d(k, a) for k, a in m_raw.items()}
    v = {k: _to_2d(k, a) for k, a in v_raw.items()}

    x_i, y_i, c_i = _place()
    core = jnp.reshape(c_i, (1,)).astype(jnp.int32)
    chip = jnp.reshape(2 * x_i + y_i, (1,)).astype(jnp.int32)
    chip_core = jnp.concatenate([chip, core])

    first_names, rest_names = ("w_in",), ("w_out", "w_gate_up", "w_down")
    placed, _ = _place_shards(w, [], chip, first_names, "place_first")
    first, _ = _gather_weights(placed, [], first_names, "gather_first", 1)
    placed, (meta_full, cw_full) = _place_shards(w, [w["meta_tokens"], w["conv_w"]], chip, rest_names, "place_shards")
    rest, _ = _gather_weights(placed, [], rest_names, "gather_rest", 2)
    full = {**first, **rest}

    seq = x.shape[1]
    small ={k: w[k] for k in SMALL_NAMES if k not in SHARDED_SMALL}
    small["conv_w"] = cw_full

    def send_to_chips(sums, names, tag, collective_id):
        arrived = _send_chip_sums({n: sums[n][1] for n in names}, names, "send_chip_sums_" + tag, collective_id)
        return {n: (sums[n][0], a) for n, a in zip(names, arrived)}

    def reduce_to_chips(grads, names, send_names, tag, collective_ids):
        got = _exchange_halves({n: grads[n][1] for n in names}, names, "exchange_halves_" + tag, collective_ids[0])

        def chip_sums():
            return _chip_sum({n: grads[n][0] for n in names}, got, names, core, "chip_sum_" + tag)

        return chip_sums, lambda sums: send_to_chips(sums, send_names, tag, collective_ids[1])

    ffn_names, mixer_names = ("w_gate_up", "w_down", "w_out"), ("w_in",)
    loss, grad_x, grads, parts, parts_mixer = _local_step(
        x.reshape(seq, D_MODEL), meta_full, loss_target.reshape(seq, D_MODEL),
        w["w_in"], full["w_in"], full["w_out"], full["w_gate_up"], full["w_down"], small, chip,
        on_ffn_grads=lambda g: reduce_to_chips(g, ("w_down", "w_out"), ffn_names, "ffn", (3, 4)),
        on_mixer_grads=lambda sums: send_to_chips(sums, mixer_names, "mixer", 5))
    parts.update(parts_mixer)
    totals = _total(parts, chip_core)
    pieces = {k: grads[k] for k in VEC_ROW if k != "loss"}
    pieces["loss"] = loss
    vec, gates, g_big = _all_reduce_small(pieces, grads["w_gates"], totals)
    loss_sum, res = _adamw_small(vec, gates, w, m, v)
    res.update(_adamw_big(w, g_big, m, v))

    out = [loss_sum.reshape(()), grad_x.reshape(1, seq, D_MODEL)]
    for j in range(4):
        out += [res[n][j].reshape(w_raw[n].shape) for n in WEIGHT_NAMES]
    return tuple(out)
```

```python
import math

import jax
import jax.numpy as jnp
from jax import lax
from jax.experimental import pallas as pl
from jax.experimental.pallas import tpu as pltpu
from jax.experimental.pallas import tpu_sc as plsc

F32 = jnp.float32
BF16 = jnp.bfloat16
MESH = pl.DeviceIdType.MESH

D_MODEL = 1024
D_RG = 512
RG_HEAD_DIM = 64
D_HG = 512
HG_HEAD_DIM = 128
HG_HEADS = 4
CHUNK = 64
SUB = 16
N_SUB = CHUNK // SUB
N_META = 16
PAD = CHUNK - N_META
D_IN = 3072
D_FF = 2816
CONV_W = 4
LRU_C = 8.0
EPS = 1e-6
EXP_CLAMP = 80.0
GELU_C = math.sqrt(2.0 / math.pi)
GELU_A = 0.044715
N_CHIPS = 4

ADAM_LR = 0.001
ADAM_B1 = 0.9
ADAM_B2 = 0.999
ADAM_EPS = 1e-08
ADAM_WD = 0.01
ADAM_STEP = 10

VMEM_LIMIT = 56 * 1024 * 1024


def _params(*sem):
    return pltpu.CompilerParams(dimension_semantics=sem, vmem_limit_bytes=VMEM_LIMIT)


def _row_tile(rows, target):
    best = None
    for t in range(16, min(rows, target) + 1, 16):
        if rows % t == 0:
            best = t
    assert best is not None, rows
    return best


def _sigmoid(x):
    return 0.5 * jnp.tanh(0.5 * x) + 0.5


def _dot(a, b):
    return jnp.dot(a, b, preferred_element_type=F32)


def _dot_nt(a, b):
    return lax.dot_general(a, b, (((1,), (1,)), ((), ())), preferred_element_type=F32)


def _dot_tn(a, b):
    return lax.dot_general(a, b, (((0,), (0,)), ((), ())), preferred_element_type=F32)


def _rms(x):
    return lax.rsqrt(jnp.mean(x * x, axis=-1, keepdims=True) + EPS)


def _rms_bwd(dn, n, r):
    return r * (dn - n * jnp.mean(dn * n, axis=-1, keepdims=True))


def _gelu_parts(x):
    t = jnp.tanh(GELU_C * (x + GELU_A * x * x * x))
    g = 0.5 * x * (1.0 + t)
    dg = 0.5 * (1.0 + t) + 0.5 * x * (1.0 - t * t) * GELU_C * (1.0 + 3.0 * GELU_A * x * x)
    return g, dg


def _softplus_neg(lam):
    e = jnp.exp(-jnp.abs(lam))
    w = 1.0 + e
    log1p = jnp.where(w == 1.0, e, jnp.log(w) * e / (w - 1.0))
    return jnp.maximum(-lam, 0.0) + log1p


def _head_mask():
    r = lax.broadcasted_iota(jnp.int32, (D_RG, D_RG), 0) // RG_HEAD_DIM
    c = lax.broadcasted_iota(jnp.int32, (D_RG, D_RG), 1) // RG_HEAD_DIM
    return r == c


def _head_fold():
    r = lax.broadcasted_iota(jnp.int32, (D_RG, RG_HEAD_DIM), 0) % RG_HEAD_DIM
    c = lax.broadcasted_iota(jnp.int32, (D_RG, RG_HEAD_DIM), 1)
    return (r == c).astype(F32)


def _gate_weights(w_r, w_i):
    def body(wr_ref, wi_ref, o_ref):
        fold = _head_fold()
        mask = _head_mask()
        for k, ref in enumerate((wr_ref, wi_ref)):
            full = _dot_nt(ref[...].astype(BF16), fold.astype(BF16))
            o_ref[:, k * D_RG:(k + 1) * D_RG] = jnp.where(mask, full, 0.0).astype(BF16)

    return pl.pallas_call(
        body, out_shape=jax.ShapeDtypeStruct((D_RG, 2 * D_RG), BF16), name="gate_weights",
    )(w_r, w_i)


HEAD = PAD + N_META


def _window_copies(seq_hbm, buf, sems, tm):
    def first(to_vmem):
        seq, vm = seq_hbm.at[pl.ds(0, tm - HEAD)], buf.at[0, pl.ds(HEAD, tm - HEAD)]
        return pltpu.make_async_copy(seq, vm, sems.at[0]) if to_vmem else pltpu.make_async_copy(vm, seq, sems.at[0])

    def later(j, slot, to_vmem):
        seq, vm = seq_hbm.at[pl.ds(pl.multiple_of(j * tm - HEAD, 8), tm)], buf.at[slot]
        if to_vmem:
            return pltpu.make_async_copy(seq, vm, sems.at[slot])
        return pltpu.make_async_copy(vm, seq, sems.at[slot])

    return first, later


def _fetch_window(seq_hbm, buf, sems, i, n_steps, tm):
    first, later = _window_copies(seq_hbm, buf, sems, tm)
    slot = i % 2

    @pl.when(i == 0)
    def _():
        first(True).start()

    if n_steps > 1:
        @pl.when(i + 1 < n_steps)
        def _():
            later(i + 1, 1 - slot, True).start()

    @pl.when(i == 0)
    def _():
        first(True).wait()

    if n_steps > 1:
        @pl.when(i > 0)
        def _():
            later(i, slot, True).wait()

    return slot


def _in_proj_local(x, meta, g1, w_own, chip):
    T = x.shape[0] + HEAD
    tm = _row_tile(T, 832)
    n_steps = T // tm
    cols = BIG["w_in"][1]

    def body(s_ref, x_hbm, meta_ref, g_ref, w_ref, p_ref, u_ref, h_ref, buf, sems, wb):
        i = pl.program_id(0)
        slot = _fetch_window(x_hbm, buf, sems, i, n_steps, tm)

        @pl.when(i == 0)
        def _():
            buf[0, 0:PAD, :] = jnp.zeros((PAD, D_MODEL), F32)
            buf[0, PAD:HEAD, :] = meta_ref[...]
            wb[...] = w_ref[...].astype(BF16)

        h = buf[slot]
        h_ref[...] = h
        u = (h * _rms(h) * g_ref[...]).astype(BF16)
        u_ref[...] = u
        p_ref[...] = _dot(u, wb[...])

    return pl.pallas_call(
        body,
        grid_spec=pltpu.PrefetchScalarGridSpec(
            num_scalar_prefetch=1, grid=(n_steps,),
            in_specs=[pl.BlockSpec(memory_space=pl.ANY),
                      pl.BlockSpec((N_META, D_MODEL), lambda i, s: (0, 0)),
                      pl.BlockSpec((1, D_MODEL), lambda i, s: (0, 0)),
                      pl.BlockSpec((D_MODEL, cols), lambda i, s: (0, 0))],
            out_specs=[pl.BlockSpec((tm, cols), lambda i, s: (i, s[0])),
                       pl.BlockSpec((tm, D_MODEL), lambda i, s: (i, 0)),
                       pl.BlockSpec((tm, D_MODEL), lambda i, s: (i, 0))],
            scratch_shapes=[pltpu.VMEM((2, tm, D_MODEL), F32), pltpu.SemaphoreType.DMA((2,)),
                            pltpu.VMEM((D_MODEL, cols), BF16)]),
        out_shape=[jax.ShapeDtypeStruct((T, D_IN), F32), jax.ShapeDtypeStruct((T, D_MODEL), BF16),
                   jax.ShapeDtypeStruct((T, D_MODEL), F32)],
        name="in_proj_local", compiler_params=_params("arbitrary"),
    )(chip, x, meta, g1, w_own)


def _in_proj_rest(u, w_in, p, chip):
    T = u.shape[0]
    tm = _row_tile(T, 2080)
    cols = BIG["w_in"][1]
    block = lambda j, s: (s[0] + 1 + j) % N_CHIPS

    def body(s_ref, u_ref, w_ref, p_in_ref, p_ref):
        p_ref[...] = _dot(u_ref[...], w_ref[...])

    return pl.pallas_call(
        body,
        grid_spec=pltpu.PrefetchScalarGridSpec(
            num_scalar_prefetch=1, grid=(N_CHIPS - 1, T // tm),
            in_specs=[pl.BlockSpec((tm, D_MODEL), lambda j, i, s: (i, 0)),
                      pl.BlockSpec((D_MODEL, cols), lambda j, i, s: (0, block(j, s))), ANY],
            out_specs=pl.BlockSpec((tm, cols), lambda j, i, s: (i, block(j, s)))),
        out_shape=jax.ShapeDtypeStruct((T, D_IN), F32),
        input_output_aliases={3: 0},
        name="in_proj_rest", compiler_params=_params("arbitrary", "arbitrary"),
    )(chip, u, w_in, p)


def _scan_block_fwd(A, B, rowi):
    for d in (1, 2, 4):
        a_sh = pltpu.roll(A, d, axis=0)
        b_sh = pltpu.roll(B, d, axis=0)
        m = rowi >= d
        B = jnp.where(m, A * b_sh + B, B)
        A = jnp.where(m, A * a_sh, A)
    return A, B


def _scan_block_bwd(A, B, rowi):
    for d in (1, 2, 4):
        a_sh = pltpu.roll(A, 8 - d, axis=0)
        b_sh = pltpu.roll(B, 8 - d, axis=0)
        m = rowi < 8 - d
        B = jnp.where(m, A * b_sh + B, B)
        A = jnp.where(m, A * a_sh, A)
    return A, B


def _rg_gates(xc, w_ref, bg_ref, lam):
    pre = _dot(xc.astype(BF16), w_ref[...]) + bg_ref[...]
    r = _sigmoid(pre[:, :D_RG])
    ig = _sigmoid(pre[:, D_RG:])
    sp = _softplus_neg(lam)
    la = -LRU_C * sp * r
    a = jnp.exp(la)
    th = jnp.tanh(la)
    u = 1.0 - th
    rc = pl.reciprocal(u, approx=True)
    rc = rc * (2.0 - u * rc)
    rc = rc * (2.0 - u * rc)
    m2 = -2.0 * th * rc
    inv_m = lax.rsqrt(jnp.maximum(m2, 1e-30))
    return r, ig, sp, a, m2 * inv_m, inv_m


def _conv(ext, cw_ref, cb_ref, tm):
    xc = cb_ref[...] + cw_ref[0:1, :] * ext[8 - 3:8 - 3 + tm, :]
    for j in range(1, CONV_W):
        xc = xc + cw_ref[j:j + 1, :] * ext[8 - 3 + j:8 - 3 + j + tm, :]
    return xc


def _scan_unroll(blocks):
    return 4 if blocks % 4 == 0 else 2 if blocks % 2 == 0 else 1


def _rg_fwd(p, cw, cb, wg, bg, lam, rg_g):
    T = p.shape[0]
    tm = _row_tile(T, 832)
    unroll = _scan_unroll(tm // 8)

    def body(xg_ref, cw_ref, cb_ref, w_ref, bg_ref, lam_ref, g_ref, y_ref, h_ref, xc_ref, ext, a_s, b_s, carry):
        i = pl.program_id(0)

        @pl.when(i == 0)
        def _():
            ext[0:8, :] = jnp.zeros((8, D_RG), F32)
            carry[...] = jnp.zeros((1, D_RG), F32)

        ext[8:8 + tm, :] = xg_ref[:, :D_RG]
        xc = _conv(ext, cw_ref, cb_ref, tm)
        xc_ref[...] = xc
        r, ig, sp, a, m, _ = _rg_gates(xc, w_ref, bg_ref, lam_ref[...])
        row = i * tm + lax.broadcasted_iota(jnp.int32, (tm, 1), 0)
        a_s[...] = a
        b_s[...] = jnp.where(row >= PAD, m * ig * xc, 0.0)
        rowi = lax.broadcasted_iota(jnp.int32, (8, D_RG), 0)

        def blk(j, c):
            for u in range(unroll):
                o = pl.multiple_of((j * unroll + u) * 8, 8)
                A, B = _scan_block_fwd(a_s[pl.ds(o, 8), :], b_s[pl.ds(o, 8), :], rowi)
                h = B + A * c
                h_ref[pl.ds(o, 8), :] = h
                c = h[7:8, :]
            return c

        carry[...] = lax.fori_loop(0, tm // (8 * unroll), blk, carry[...])
        ext[0:8, :] = ext[tm:tm + 8, :]
        g, _ = _gelu_parts(xg_ref[:, D_RG:])
        yy = g * h_ref[...]
        y_ref[...] = (yy * _rms(yy) * g_ref[...]).astype(BF16)

    vec = lambda n: pl.BlockSpec((1, n), lambda i: (0, 0))
    return pl.pallas_call(
        body, grid=(T // tm,),
        in_specs=[pl.BlockSpec((tm, 2 * D_RG), lambda i: (i, 0)),
                  pl.BlockSpec((CONV_W, D_RG), lambda i: (0, 0)), vec(D_RG),
                  pl.BlockSpec((D_RG, 2 * D_RG), lambda i: (0, 0)), vec(2 * D_RG), vec(D_RG), vec(D_RG)],
        out_specs=[pl.BlockSpec((tm, D_RG), lambda i: (i, 0))] * 3,
        out_shape=[jax.ShapeDtypeStruct((T, D_RG), BF16), jax.ShapeDtypeStruct((T, D_RG), F32),
                   jax.ShapeDtypeStruct((T, D_RG), F32)],
        scratch_shapes=[pltpu.VMEM((tm + 8, D_RG), F32), pltpu.VMEM((tm, D_RG), F32),
                        pltpu.VMEM((tm, D_RG), F32), pltpu.VMEM((1, D_RG), F32)],
        name="rg_fwd", compiler_params=_params("arbitrary"),
    )(p, cw, cb, wg, bg, lam, rg_g)


def _running_sum(x, down):
    r = lax.broadcasted_iota(jnp.int32, (CHUNK, CHUNK), 0)
    c = lax.broadcasted_iota(jnp.int32, (CHUNK, CHUNK), 1)
    tri = ((c <= r) if down else (c >= r)).astype(BF16)
    hi = x.astype(BF16)
    rest = x - hi.astype(F32)
    mid = rest.astype(BF16)
    lo = (rest - mid.astype(F32)).astype(BF16)
    return (_dot(tri, hi) + _dot(tri, mid)) + _dot(tri, lo)


def _hg_gates(hq, hf, lbraw_ref, valid):
    lb = _sigmoid(lbraw_ref[0:1, :] - lbraw_ref[1:2, :])
    sq = _sigmoid(hq)
    q = hq * sq
    sf = _sigmoid(hf)
    f = lb + (1.0 - lb) * sf
    lf = jnp.where(valid, jnp.log(f), 0.0)
    b = _running_sum(lf, True)
    return lb, sq, q, sf, f, b


def _hg_head(qh, kh, bh):
    b_last = bh[CHUNK - 1:CHUNK, :]
    refs = [bh[SUB * s:SUB * s + 1, :] for s in range(N_SUB)]
    r_sel = jnp.concatenate([jnp.broadcast_to(refs[s], (SUB, HG_HEAD_DIM)) for s in range(N_SUB)], axis=0)
    eb = jnp.exp(bh)
    eq = jnp.exp(bh - r_sel)
    ekh = jnp.exp(b_last - bh)
    ek = [jnp.exp(jnp.minimum(refs[s] - bh[:SUB * (s + 1), :], EXP_CLAMP)) for s in range(N_SUB)]
    qe = qh * eq

    def own_rows(s):
        parts = [jnp.zeros((SUB * s, HG_HEAD_DIM), F32)] if s else []
        parts.append(qe[SUB * s:SUB * (s + 1), :])
        if s < N_SUB - 1:
            parts.append(jnp.zeros((CHUNK - SUB * (s + 1), HG_HEAD_DIM), F32))
        return jnp.concatenate(parts, axis=0)

    q_hat = jnp.concatenate([own_rows(s) for s in range(N_SUB)], axis=1)

    def met_rows(s):
        n = SUB * (s + 1)
        ke = kh[:n, :] * ek[s]
        return ke if n == CHUNK else jnp.concatenate([ke, jnp.zeros((CHUNK - n, HG_HEAD_DIM), F32)], axis=0)

    k_til = jnp.concatenate([met_rows(s) for s in range(N_SUB)], axis=1)
    return b_last, eb, eq, ekh, ek, q_hat, k_til


def _causal():
    r = lax.broadcasted_iota(jnp.int32, (CHUNK, CHUNK), 0)
    c = lax.broadcasted_iota(jnp.int32, (CHUNK, CHUNK), 1)
    return r >= c


def _chunks_per_step(n_chunks):
    for c in (5, 4, 3, 2):
        if n_chunks % c == 0:
            return c
    return 1


def _hg_fwd(p, lbraw, hg_g):
    T = p.shape[0]
    n_chunks = T // CHUNK
    cps = _chunks_per_step(n_chunks)
    rows = cps * CHUNK

    def body(hq_ref, hf_ref, hi_ref, hg_ref, lb_ref, g_ref, y_ref, o_ref, st_all_ref, st):
        i = pl.program_id(0)

        @pl.when(i == 0)
        def _():
            st[...] = jnp.zeros_like(st)

        def chunk(j, carry):
            rs = pl.ds(pl.multiple_of(j * CHUNK, CHUNK), CHUNK)
            chunk_body(i * cps + j, hq_ref.at[rs, :], hf_ref.at[rs, :], hi_ref.at[rs, :], hg_ref.at[rs, :], lb_ref,
                       g_ref, y_ref.at[rs, :], o_ref.at[rs, :], st_all_ref.at[pl.ds(j, 1)], st)
            return carry

        lax.fori_loop(0, cps, chunk, 0, unroll=True)

    def chunk_body(n, hq_ref, hf_ref, hi_ref, hg_ref, lb_ref, g_ref, y_ref, o_ref, st_all_ref, st):
        valid = (n * CHUNK + lax.broadcasted_iota(jnp.int32, (CHUNK, 1), 0)) >= PAD
        hq, hf, v, hg = hq_ref[...], hf_ref[...], hi_ref[...], hg_ref[...]
        lb, sq, q, sf, f, b = _hg_gates(hq, hf, lb_ref, valid)
        k = 1.0 - f
        st_all_ref[0] = st[...]
        causal = _causal()
        v_t = v.T.astype(BF16)
        heads = [slice(h * HG_HEAD_DIM, (h + 1) * HG_HEAD_DIM) for h in range(HG_HEADS)]
        fac = []
        for sl in heads:
            qh, kh, bh = q[:, sl], k[:, sl], b[:, sl]
            b_last, eb, _, ekh, _, q_hat, k_til = _hg_head(qh, kh, bh)
            fac.append((jnp.exp(b_last), (qh * eb).astype(BF16), q_hat.astype(BF16), k_til.astype(BF16),
                        (kh * ekh).astype(BF16), v[:, sl].astype(BF16)))
        raw = []
        for sl, (_, q_til, q_hat, k_til, k_hat, _) in zip(heads, fac):
            st_h = st[sl, :]
            raw.append((_dot_nt(q_til, st_h.astype(BF16)), _dot_nt(q_hat, k_til), _dot(v_t[sl, :], k_hat), st_h))
        for sl, (e_last, _, _, _, _, vb), (inter, att, upd, st_h) in zip(heads, fac, raw):
            o = inter + _dot(jnp.where(causal, att, 0.0).astype(BF16), vb)
            st[sl, :] = st_h * e_last + upd
            o_ref[:, sl] = o
            hgh = hg[:, sl]
            y_ref[:, sl] = (o * _rms(o) * g_ref[...] * (hgh * _sigmoid(hgh))).astype(BF16)

    col = lambda j: pl.BlockSpec((rows, D_HG), lambda n: (n, j))
    return pl.pallas_call(
        body, grid=(n_chunks // cps,),
        in_specs=[col(2), col(3), col(4), col(5),
                  pl.BlockSpec((2, D_HG), lambda n: (0, 0)), pl.BlockSpec((1, HG_HEAD_DIM), lambda n: (0, 0))],
        out_specs=[pl.BlockSpec((rows, D_HG), lambda n: (n, 0)), pl.BlockSpec((rows, D_HG), lambda n: (n, 0)),
                   pl.BlockSpec((cps, D_HG, HG_HEAD_DIM), lambda n: (n, 0, 0))],
        out_shape=[jax.ShapeDtypeStruct((T, D_HG), BF16), jax.ShapeDtypeStruct((T, D_HG), F32),
                   jax.ShapeDtypeStruct((n_chunks, D_HG, HG_HEAD_DIM), F32)],
        scratch_shapes=[pltpu.VMEM((D_HG, HG_HEAD_DIM), F32)],
        name="hg_fwd", compiler_params=_params("arbitrary"),
    )(p, p, p, p, lbraw, hg_g)


def _ffn_fwd(h0, y_rg, y_hg, w_out, g2, w_gu, w_down, gf, target):
    T = h0.shape[0]
    tm = _row_tile(T, 320)
    n_steps = T // tm

    def body(h_ref, yr_ref, yh_ref, wo_ref, g2_ref, wgu_ref, wd_ref, gf_ref, t_hbm,
             h1_ref, v_ref, y_ref, gu_ref, act_ref, dh2_ref, dh2b_ref, loss_ref, gg_ref, tbuf, sems):
        i = pl.program_id(0)
        slot = _fetch_window(t_hbm, tbuf, sems, i, n_steps, tm)

        @pl.when(i == 0)
        def _():
            loss_ref[...] = jnp.zeros_like(loss_ref)
            gg_ref[...] = jnp.zeros_like(gg_ref)
            tbuf[0, 0:HEAD, :] = jnp.zeros((HEAD, D_MODEL), F32)

        y_ref[:, :D_RG] = yr_ref[...]
        y_ref[:, D_RG:] = yh_ref[...]
        h1 = h_ref[...] + _dot(y_ref[...], wo_ref[...])
        h1_ref[...] = h1
        v = (h1 * _rms(h1) * g2_ref[...]).astype(BF16)
        v_ref[...] = v

        gu = _dot(v, wgu_ref[...])
        gu_ref[...] = gu.astype(BF16)
        g = gu[:, :D_FF]
        act = (g * _sigmoid(g) * gu[:, D_FF:]).astype(BF16)
        act_ref[...] = act

        h2 = h1 + _dot(act, wd_ref[...])
        r = _rms(h2)
        n = h2 * r
        gf_ = gf_ref[...]
        row = i * tm + lax.broadcasted_iota(jnp.int32, (tm, 1), 0)
        err = jnp.where(row >= HEAD, n * gf_ - tbuf[slot], 0.0)
        loss_ref[...] += 0.5 * jnp.sum(jnp.mean(err * err, axis=-1, keepdims=True), axis=0, keepdims=True)
        dy = err * (1.0 / D_MODEL)
        gg_ref[...] += jnp.sum(dy * n, axis=0, keepdims=True)
        dh2 = _rms_bwd(dy * gf_, n, r)
        dh2_ref[...] = dh2
        dh2b_ref[...] = dh2.astype(BF16)

    row_spec = lambda n: pl.BlockSpec((tm, n), lambda i: (i, 0))
    vec = pl.BlockSpec((1, D_MODEL), lambda i: (0, 0))
    return pl.pallas_call(
        body, grid=(n_steps,),
        in_specs=[row_spec(D_MODEL), row_spec(D_RG), row_spec(D_HG), _resident((D_MODEL, D_MODEL)), vec,
                  _resident((D_MODEL, 2 * D_FF)), _resident((D_FF, D_MODEL)), vec,
                  pl.BlockSpec(memory_space=pl.ANY)],
        out_specs=[row_spec(D_MODEL), row_spec(D_MODEL), row_spec(D_MODEL), row_spec(2 * D_FF), row_spec(D_FF),
                   row_spec(D_MODEL), row_spec(D_MODEL), pl.BlockSpec((1, 1), lambda i: (0, 0)), vec],
        out_shape=[jax.ShapeDtypeStruct((T, D_MODEL), F32), jax.ShapeDtypeStruct((T, D_MODEL), BF16),
                   jax.ShapeDtypeStruct((T, D_MODEL), BF16), jax.ShapeDtypeStruct((T, 2 * D_FF), BF16),
                   jax.ShapeDtypeStruct((T, D_FF), BF16), jax.ShapeDtypeStruct((T, D_MODEL), F32),
                   jax.ShapeDtypeStruct((T, D_MODEL), BF16), jax.ShapeDtypeStruct((1, 1), F32),
                   jax.ShapeDtypeStruct((1, D_MODEL), F32)],
        scratch_shapes=[pltpu.VMEM((2, tm, D_MODEL), F32), pltpu.SemaphoreType.DMA((2,))],
        name="ffn_fwd", compiler_params=_params("arbitrary"),
    )(h0, y_rg, y_hg, w_out, g2, w_gu, w_down, gf, target)


def _resident(shape):
    return pl.BlockSpec(shape, lambda i: (0,) * len(shape), pipeline_mode=pl.Buffered(1))


def _ffn_bwd(dh2b, gu, w_down, w_gu, h1, g2, dh2, w_out):
    T = h1.shape[0]
    tm = _row_tile(T, 320)

    def body(d_ref, gu_ref, wd_ref, wgu_ref, h_ref, g_ref, d2_ref, wo_ref, dgu_ref, dh1_ref, dh1b_ref, dy_ref, gg_ref):
        i = pl.program_id(0)

        @pl.when(i == 0)
        def _():
            gg_ref[...] = jnp.zeros_like(gg_ref)

        dact = _dot_nt(d_ref[...], wd_ref[...]).astype(BF16)
        g = gu_ref[:, :D_FF]
        u = gu_ref[:, D_FF:]
        s = _sigmoid(g)
        dgu_ref[:, :D_FF] = dact * u * (s * (1.0 + g * (1.0 - s)))
        dgu_ref[:, D_FF:] = dact * (g * s)

        dv = _dot_nt(dgu_ref[...], wgu_ref[...])
        h1_ = h_ref[...]
        r = _rms(h1_)
        n = h1_ * r
        gg_ref[...] += jnp.sum(dv * n, axis=0, keepdims=True)
        dh1 = d2_ref[...] + _rms_bwd(dv * g_ref[...], n, r)
        dh1_ref[...] = dh1
        db = dh1.astype(BF16)
        dh1b_ref[...] = db
        dy_ref[...] = _dot_nt(db, wo_ref[...])

    row = lambda n: pl.BlockSpec((tm, n), lambda i: (i, 0))
    return pl.pallas_call(
        body, grid=(T // tm,),
        in_specs=[row(D_MODEL), row(2 * D_FF), _resident((D_FF, D_MODEL)), _resident((D_MODEL, 2 * D_FF)),
                  row(D_MODEL), pl.BlockSpec((1, D_MODEL), lambda i: (0, 0)), row(D_MODEL),
                  _resident((D_MODEL, D_MODEL))],
        out_specs=[row(2 * D_FF), row(D_MODEL), row(D_MODEL), row(D_MODEL),
                   pl.BlockSpec((1, D_MODEL), lambda i: (0, 0))],
        out_shape=[jax.ShapeDtypeStruct((T, 2 * D_FF), BF16), jax.ShapeDtypeStruct((T, D_MODEL), F32),
                   jax.ShapeDtypeStruct((T, D_MODEL), BF16), jax.ShapeDtypeStruct((T, D_MODEL), F32),
                   jax.ShapeDtypeStruct((1, D_MODEL), F32)],
        name="ffn_bwd", compiler_params=_params("arbitrary"),
    )(dh2b, gu, w_down, w_gu, h1, g2, dh2, w_out)


def _rg_bwd(p, xc_all, hs, dy, dp, cw, cb, wg, bg, lam, rg_g):
    T = p.shape[0]
    tm = _row_tile(T, 832)
    nt = T // tm
    hb = tm // 8
    unroll = _scan_unroll(hb)

    def body(xg_ref, xc_ref, h_ref, hh_ref, dy_ref, dp_in_ref, cw_ref, cb_ref, w_ref, bg_ref, lam_ref, g_ref,
             dp_ref, gcw_ref, gcb_ref, gw_ref, gbg_ref, glam_ref, gg_ref,
             dext, a_s, b_s, d_s, gacc, carry_d, carry_a):
        i = pl.program_id(0)
        t_idx = nt - 1 - i

        @pl.when(i == 0)
        def _():
            dext[tm:tm + 8, :] = jnp.zeros((8, D_RG), F32)
            carry_d[...] = jnp.zeros_like(carry_d)
            carry_a[...] = jnp.zeros_like(carry_a)
            gacc[...] = jnp.zeros_like(gacc)
            for ref in (gcw_ref, gcb_ref, gbg_ref, glam_ref, gg_ref, gw_ref):
                ref[...] = jnp.zeros_like(ref)

        first = t_idx == 0
        xc = xc_ref[...]
        lam_ = lam_ref[...]
        r, ig, sp, a, m, inv_m = _rg_gates(xc, w_ref, bg_ref, lam_)
        row = t_idx * tm + lax.broadcasted_iota(jnp.int32, (tm, 1), 0)
        valid = row >= PAD

        gr = xg_ref[:, D_RG:]
        g, dgelu = _gelu_parts(gr)
        h = h_ref[...]
        yy = g * h
        rr = _rms(yy)
        nn = yy * rr
        dy_ = dy_ref[...]
        gg_ref[...] += jnp.sum(dy_ * nn, axis=0, keepdims=True)
        dyy = _rms_bwd(dy_ * g_ref[...], nn, rr)
        dp_ref[:, D_RG:] = (dyy * h * dgelu).astype(BF16)

        a_s[...] = a
        b_s[...] = dyy * g
        rowi = lax.broadcasted_iota(jnp.int32, (8, D_RG), 0)

        def blk(jj, c):
            cd, ca = c
            for u in range(unroll):
                o = pl.multiple_of((hb - 1 - (jj * unroll + u)) * 8, 8)
                a_blk = a_s[pl.ds(o, 8), :]
                a_next = jnp.where(rowi == 7, ca, pltpu.roll(a_blk, 7, axis=0))
                A, B = _scan_block_bwd(a_next, b_s[pl.ds(o, 8), :], rowi)
                d = B + A * cd
                d_s[pl.ds(o, 8), :] = d
                cd, ca = d[0:1, :], a_blk[0:1, :]
            return cd, ca

        cd, ca = lax.fori_loop(0, hb // unroll, blk, (carry_d[...], carry_a[...]))
        carry_d[...] = cd
        carry_a[...] = ca
        delta = d_s[...]

        h_last_prev = jnp.where(first, 0.0, hh_ref[7:8, :])
        row0 = lax.broadcasted_iota(jnp.int32, (tm, 1), 0) == 0
        h_prev = jnp.where(row0, h_last_prev, pltpu.roll(h, 1, axis=0))
        dbx = jnp.where(valid, delta, 0.0)
        da = delta * h_prev
        di = dbx * m * xc
        dm = dbx * ig * xc
        dla = a * (da - dm * a * inv_m)
        dla = jnp.where(valid, dla, 0.0)
        glam_ref[...] += jnp.sum(dla * r, axis=0, keepdims=True) * (LRU_C / (1.0 + jnp.exp(lam_)))
        dr = (-LRU_C) * sp * dla
        dpre = jnp.concatenate([dr * r * (1.0 - r), di * ig * (1.0 - ig)], axis=1)
        gbg_ref[...] += jnp.sum(dpre, axis=0, keepdims=True)
        dpre_b = dpre.astype(BF16)
        gacc[...] += _dot_tn(xc.astype(BF16), dpre_b)
        dxc = dbx * m * ig + _dot_nt(dpre_b, w_ref[...])
        gcb_ref[...] += jnp.sum(dxc, axis=0, keepdims=True)
        dext[0:tm, :] = dxc
        xr = xg_ref[:, :D_RG]
        dxr = None
        for j in range(CONV_W):
            shifted = dext[3 - j:3 - j + tm, :]
            gcw_ref[j:j + 1, :] += jnp.sum(xr * shifted, axis=0, keepdims=True)
            tap = cw_ref[j:j + 1, :] * shifted
            dxr = tap if dxr is None else dxr + tap
        dp_ref[:, :D_RG] = dxr.astype(BF16)
        dext[tm:tm + 8, :] = dext[0:8, :]

        @pl.when(i == nt - 1)
        def _():
            fold = _head_fold()
            mask = _head_mask()
            fold_b = fold.astype(BF16)
            for k in range(2):
                blockdiag = jnp.where(mask, gacc[:, k * D_RG:(k + 1) * D_RG], 0.0)
                hi = blockdiag.astype(BF16)
                rest = blockdiag - hi.astype(F32)
                mid = rest.astype(BF16)
                lo = (rest - mid.astype(F32)).astype(BF16)
                gw_ref[k * D_RG:(k + 1) * D_RG, :] = (_dot(hi, fold_b) + _dot(mid, fold_b)) + _dot(lo, fold_b)

    vec = lambda n: pl.BlockSpec((1, n), lambda i: (0, 0))
    rev = lambda n: pl.BlockSpec((tm, n), lambda i: (nt - 1 - i, 0))
    halo = lambda n: pl.BlockSpec((8, n), lambda i: (jnp.maximum((nt - 1 - i) * hb - 1, 0), 0))
    return pl.pallas_call(
        body, grid=(nt,),
        in_specs=[rev(2 * D_RG), rev(D_RG), rev(D_RG), halo(D_RG), rev(D_RG), ANY,
                  pl.BlockSpec((CONV_W, D_RG), lambda i: (0, 0)), vec(D_RG),
                  pl.BlockSpec((D_RG, 2 * D_RG), lambda i: (0, 0)), vec(2 * D_RG), vec(D_RG), vec(D_RG)],
        out_specs=[rev(2 * D_RG), pl.BlockSpec((CONV_W, D_RG), lambda i: (0, 0)), vec(D_RG),
                   pl.BlockSpec((2 * D_RG, RG_HEAD_DIM), lambda i: (0, 0)), vec(2 * D_RG), vec(D_RG), vec(D_RG)],
        input_output_aliases={5: 0},
        out_shape=[jax.ShapeDtypeStruct((T, D_IN), BF16), jax.ShapeDtypeStruct((CONV_W, D_RG), F32),
                   jax.ShapeDtypeStruct((1, D_RG), F32), jax.ShapeDtypeStruct((2 * D_RG, RG_HEAD_DIM), F32),
                   jax.ShapeDtypeStruct((1, 2 * D_RG), F32), jax.ShapeDtypeStruct((1, D_RG), F32),
                   jax.ShapeDtypeStruct((1, D_RG), F32)],
        scratch_shapes=[pltpu.VMEM((tm + 8, D_RG), F32),
                        pltpu.VMEM((tm, D_RG), F32), pltpu.VMEM((tm, D_RG), F32), pltpu.VMEM((tm, D_RG), F32),
                        pltpu.VMEM((D_RG, 2 * D_RG), F32), pltpu.VMEM((1, D_RG), F32), pltpu.VMEM((1, D_RG), F32)],
        name="rg_bwd", compiler_params=_params("arbitrary"),
    )(p, xc_all, hs, hs, dy, dp, cw, cb, wg, bg, lam, rg_g)


def _hg_bwd(p, o_all, st_all, dy, lbraw, hg_g):
    T = p.shape[0]
    n_chunks = T // CHUNK
    cps = _chunks_per_step(n_chunks)
    rows = cps * CHUNK
    n_steps = n_chunks // cps

    def body(hq_ref, hf_ref, hi_ref, hg_ref, o_ref, st_ref, dy_ref, lb_ref, g_ref,
             dp_ref, glb_ref, gg_ref, dst):
        i = pl.program_id(0)

        @pl.when(i == 0)
        def _():
            dst[...] = jnp.zeros_like(dst)
            glb_ref[...] = jnp.zeros_like(glb_ref)
            gg_ref[...] = jnp.zeros_like(gg_ref)

        dp_ref[:, :2 * D_RG] = jnp.zeros((rows, 2 * D_RG), BF16)

        def chunk(jj, carry):
            j = cps - 1 - jj
            rs = pl.ds(pl.multiple_of(j * CHUNK, CHUNK), CHUNK)
            chunk_body((n_steps - 1 - i) * cps + j, hq_ref.at[rs, :], hf_ref.at[rs, :], hi_ref.at[rs, :],
                       hg_ref.at[rs, :], o_ref.at[rs, :], st_ref.at[pl.ds(j, 1)], dy_ref.at[rs, :], lb_ref, g_ref,
                       dp_ref.at[rs, pl.ds(2 * D_RG, 4 * D_HG)], glb_ref, gg_ref, dst)
            return carry

        lax.fori_loop(0, cps, chunk, 0, unroll=True)

    def chunk_body(n, hq_ref, hf_ref, hi_ref, hg_ref, o_ref, st_ref, dy_ref, lb_ref, g_ref,
                   dp_ref, glb_ref, gg_ref, dst):
        valid = (n * CHUNK + lax.broadcasted_iota(jnp.int32, (CHUNK, 1), 0)) >= PAD
        hq, hf, v, hg = hq_ref[...], hf_ref[...], hi_ref[...], hg_ref[...]
        lb, sq, q, sf, f, b = _hg_gates(hq, hf, lb_ref, valid)
        k = 1.0 - f
        causal = _causal()
        r_i = lax.broadcasted_iota(jnp.int32, (CHUNK, CHUNK), 0)
        c_i = lax.broadcasted_iota(jnp.int32, (CHUNK, CHUNK), 1)
        causal_t = r_i <= c_i
        is_last = lax.broadcasted_iota(jnp.int32, (CHUNK, 1), 0) == CHUNK - 1
        g_ = g_ref[...]
        db_parts, dq_parts, dk_parts = [], [], []
        gg = jnp.zeros((1, HG_HEAD_DIM), F32)
        heads = [slice(h * HG_HEAD_DIM, (h + 1) * HG_HEAD_DIM) for h in range(HG_HEADS)]

        do_parts = []
        for h, sl in enumerate(heads):
            o = o_ref[:, sl]
            ro = _rms(o)
            no = o * ro
            hgh = hg[:, sl]
            sg = _sigmoid(hgh)
            dyh = dy_ref[:, sl]
            dp_ref[:, 3 * D_HG + h * HG_HEAD_DIM:3 * D_HG + (h + 1) * HG_HEAD_DIM] = (
                dyh * no * g_ * sg * (1.0 + hgh * (1.0 - sg))).astype(BF16)
            dng = dyh * hgh * sg
            gg = gg + jnp.sum(dng * no, axis=0, keepdims=True)
            do_parts.append(_rms_bwd(dng * g_, no, ro))
        do_t = jnp.concatenate(do_parts, axis=1).T.astype(BF16)

        fac = []
        for sl, do in zip(heads, do_parts):
            qh, kh, bh = q[:, sl], k[:, sl], b[:, sl]
            b_last, eb, eq, ekh, ek, q_hat, k_til = _hg_head(qh, kh, bh)
            fac.append(dict(qh=qh, kh=kh, e_last=jnp.exp(b_last), eb=eb, eq=eq, ekh=ekh, ek=ek,
                            q_til=qh * eb, k_hat=kh * ekh, qhb=q_hat.astype(BF16), ktb=k_til.astype(BF16),
                            vb=v[:, sl].astype(BF16), dob=do.astype(BF16)))

        first = []
        for sl, t in zip(heads, fac):
            st_h = st_ref[0, sl, :]
            dst_h = dst[sl, :]
            dstb = dst_h.astype(BF16)
            first.append(dict(
                att_t=_dot_nt(t["ktb"], t["qhb"]), datt=_dot_nt(t["dob"], t["vb"]),
                datt_t=_dot_nt(t["vb"], t["dob"]), dk_hat=_dot(t["vb"], dstb),
                dv=_dot_nt(t["k_hat"].astype(BF16), dstb), dq_til=_dot(t["dob"], st_h.astype(BF16)),
                state=t["e_last"] * jnp.sum(dst_h * st_h, axis=0, keepdims=True)))
            dst[sl, :] = dst_h * t["e_last"] + _dot(do_t[sl, :], t["q_til"].astype(BF16))

        for h, (t, m) in enumerate(zip(fac, first)):
            qh, kh, eb, eq, ekh, ek = t["qh"], t["kh"], t["eb"], t["eq"], t["ekh"], t["ek"]
            q_til, k_hat, qhb, ktb, dob = t["q_til"], t["k_hat"], t["qhb"], t["ktb"], t["dob"]
            dk_hat, dq_til = m["dk_hat"], m["dq_til"]
            dv = m["dv"] + _dot(jnp.where(causal_t, m["att_t"], 0.0).astype(BF16), dob)
            dq_hat = _dot(jnp.where(causal, m["datt"], 0.0).astype(BF16), ktb)
            dk_til = _dot(jnp.where(causal_t, m["datt_t"], 0.0).astype(BF16), qhb)
            db_last = jnp.sum(dk_hat * k_hat, axis=0, keepdims=True) + m["state"]
            dq_sel = jnp.concatenate([dq_hat[SUB * s:SUB * (s + 1), s * HG_HEAD_DIM:(s + 1) * HG_HEAD_DIM]
                                      for s in range(N_SUB)], axis=0)
            dq_a = dq_sel * eq
            dk_rows, k_att_rows = [], []
            for b_ in range(N_SUB):
                rs = slice(SUB * b_, SUB * (b_ + 1))
                dk_sum = k_att_sum = None
                for s in range(b_, N_SUB):
                    cs = slice(s * HG_HEAD_DIM, (s + 1) * HG_HEAD_DIM)
                    d = dk_til[rs, cs]
                    t_dk = d * ek[s][rs, :]
                    t_att = ktb[rs, cs].astype(F32) * d
                    dk_sum = t_dk if dk_sum is None else dk_sum + t_dk
                    k_att_sum = t_att if k_att_sum is None else k_att_sum + t_att
                dk_rows.append(dk_sum)
                k_att_rows.append(k_att_sum)
            dk_a = jnp.concatenate(dk_rows, axis=0)
            db = (dq_til * q_til - dk_hat * k_hat + (qh * eq).astype(BF16).astype(F32) * dq_sel
                  - jnp.concatenate(k_att_rows, axis=0))
            db_parts.append(jnp.where(is_last, db + db_last, db))
            dq_parts.append(dq_til * eb + dq_a)
            dk_parts.append(dk_hat * ekh + dk_a)
            dp_ref[:, 2 * D_HG + h * HG_HEAD_DIM:2 * D_HG + (h + 1) * HG_HEAD_DIM] = dv.astype(BF16)

        gg_ref[...] += gg
        db = jnp.concatenate(db_parts, axis=1)
        dq = jnp.concatenate(dq_parts, axis=1)
        dk = jnp.concatenate(dk_parts, axis=1)
        dlf = jnp.where(valid, _running_sum(db, False), 0.0)
        dp_ref[:, :D_HG] = (dq * sq * (1.0 + hq * (1.0 - sq))).astype(BF16)
        df = dlf / f - dk
        dlb = jnp.sum(df * (1.0 - sf), axis=0, keepdims=True) * lb * (1.0 - lb)
        glb_ref[0:1, :] += dlb
        glb_ref[1:2, :] += -dlb
        dp_ref[:, D_HG:2 * D_HG] = (df * (1.0 - lb) * sf * (1.0 - sf)).astype(BF16)

    rev = lambda j: pl.BlockSpec((rows, D_HG), lambda i: (n_steps - 1 - i, j))
    return pl.pallas_call(
        body, grid=(n_steps,),
        in_specs=[rev(2), rev(3), rev(4), rev(5), rev(0),
                  pl.BlockSpec((cps, D_HG, HG_HEAD_DIM), lambda i: (n_steps - 1 - i, 0, 0)), rev(1),
                  pl.BlockSpec((2, D_HG), lambda i: (0, 0)), pl.BlockSpec((1, HG_HEAD_DIM), lambda i: (0, 0))],
        out_specs=[pl.BlockSpec((rows, D_IN), lambda i: (n_steps - 1 - i, 0)),
                   pl.BlockSpec((2, D_HG), lambda i: (0, 0)), pl.BlockSpec((1, HG_HEAD_DIM), lambda i: (0, 0))],
        out_shape=[jax.ShapeDtypeStruct((T, D_IN), BF16), jax.ShapeDtypeStruct((2, D_HG), F32),
                   jax.ShapeDtypeStruct((1, HG_HEAD_DIM), F32)],
        scratch_shapes=[pltpu.VMEM((D_HG, HG_HEAD_DIM), F32)],
        name="hg_bwd", compiler_params=_params("arbitrary"),
    )(p, p, p, p, o_all, st_all, dy, lbraw, hg_g)


def _in_bwd(dp, w_in, h0, g1, dh1):
    T = h0.shape[0]
    tm = _row_tile(T, 832)
    n_steps = T // tm

    def body(dp_ref, w_ref, h_ref, g_ref, d1_ref, gx_hbm, gmeta_ref, gg_ref, buf, sems):
        i = pl.program_id(0)
        first, later = _window_copies(gx_hbm, buf, sems, tm)
        slot = i % 2

        @pl.when(i == 0)
        def _():
            gg_ref[...] = jnp.zeros_like(gg_ref)

        if n_steps > 2:
            @pl.when(i == 2)
            def _():
                first(False).wait()

            @pl.when(i > 2)
            def _():
                later(i - 2, slot, False).wait()

        du = _dot_nt(dp_ref[...], w_ref[...])
        h0_ = h_ref[...]
        r = _rms(h0_)
        n = h0_ * r
        gg_ref[...] += jnp.sum(du * n, axis=0, keepdims=True)
        dh0 = d1_ref[...] + _rms_bwd(du * g_ref[...], n, r)
        buf[slot] = dh0

        @pl.when(i == 0)
        def _():
            gmeta_ref[...] = dh0[PAD:HEAD, :]
            first(False).start()

        if n_steps > 1:
            @pl.when(i > 0)
            def _():
                later(i, slot, False).start()

        @pl.when(i == n_steps - 1)
        def _():
            if n_steps == 1:
                first(False).wait()
            else:
                if n_steps == 2:
                    first(False).wait()
                else:
                    later(i - 1, 1 - slot, False).wait()
                later(i, slot, False).wait()

    row = lambda n: pl.BlockSpec((tm, n), lambda i: (i, 0))
    return pl.pallas_call(
        body, grid=(n_steps,),
        in_specs=[row(D_IN), _resident((D_MODEL, D_IN)),
                  row(D_MODEL), pl.BlockSpec((1, D_MODEL), lambda i: (0, 0)), row(D_MODEL)],
        out_specs=[pl.BlockSpec(memory_space=pl.ANY), pl.BlockSpec((N_META, D_MODEL), lambda i: (0, 0)),
                   pl.BlockSpec((1, D_MODEL), lambda i: (0, 0))],
        out_shape=[jax.ShapeDtypeStruct((T - HEAD, D_MODEL), F32), jax.ShapeDtypeStruct((N_META, D_MODEL), F32),
                   jax.ShapeDtypeStruct((1, D_MODEL), F32)],
        scratch_shapes=[pltpu.VMEM((2, tm, D_MODEL), F32), pltpu.SemaphoreType.DMA((2,))],
        name="in_bwd", compiler_params=_params("arbitrary"),
    )(dp, w_in, h0, g1, dh1)


def _col_tile(cols, target):
    best = None
    for t in range(128, min(cols, target) + 1, 128):
        if cols % t == 0:
            best = t
    assert best is not None, cols
    return best


MXU_DIM = 256


def _mxu_tile(cols, target):
    best = None
    for t in range(MXU_DIM, min(cols, target) + 1, MXU_DIM):
        if cols % t == 0:
            best = t
    assert best is not None, cols
    return best


def _weight_grad(a, b, name):
    T, M = a.shape
    N = b.shape[1]
    tm = _col_tile(M, 1408)
    tn = _mxu_tile(N, 768 if tm <= 1024 else 512)

    def body(a_ref, b_ref, o_ref, ob_ref):
        o = _dot_tn(a_ref[...], b_ref[...])
        o_ref[...] = o
        ob_ref[...] = o.astype(BF16)

    return pl.pallas_call(
        body, grid=(M // tm, N // tn),
        in_specs=[pl.BlockSpec((T, tm), lambda m, n: (0, m)), pl.BlockSpec((T, tn), lambda m, n: (0, n))],
        out_specs=[pl.BlockSpec((tm, tn), lambda m, n: (m, n))] * 2,
        out_shape=[jax.ShapeDtypeStruct((M, N), F32), jax.ShapeDtypeStruct((M, N), BF16)],
        name=name, compiler_params=_params("parallel", "parallel"),
    )(a, b)


def _weight_grad_chip_sum(a, b, name):
    T, M = a.shape
    N = b.shape[1]
    tn = _mxu_tile(N, 768)
    steps = N // tn
    half = M // 2

    def body(c_ref, a_own_ref, a_sib_ref, b_ref, sum_ref, sumb_ref, own, got, stage, send_sems, recv_sems):
        n = pl.program_id(0)
        x, y, c = _place()
        slot = n % 2

        def piece(k, s):
            return _remote(stage.at[s], got.at[k], send_sems, recv_sems, k, (x, y, 1 - c))

        @pl.when(n < steps)
        def _():
            @pl.when(n >= 2)
            def _():
                piece(n - 2, slot).wait_send()

            stage[slot] = _dot_tn(a_sib_ref[...], b_ref[...]).astype(BF16)
            piece(n, slot).start()

        @pl.when(n >= 1)
        def _():
            piece(n - 1, 1 - slot).wait_recv()
            t = own[1 - slot] + got[n - 1].astype(F32)
            sum_ref[...] = t
            sumb_ref[...] = t.astype(BF16)

        @pl.when(n < steps)
        def _():
            own[slot] = _dot_tn(a_own_ref[...], b_ref[...])

        @pl.when(n == steps)
        def _():
            for k in range(max(steps - 2, 0), steps):
                piece(k, k % 2).wait_send()

    last = steps - 1
    core = jnp.reshape(lax.axis_index("c"), (1,)).astype(jnp.int32)
    a_half = lambda which: pl.BlockSpec((T, half), lambda n, s: (0, s[0] if which == 0 else 1 - s[0]),
                                        pipeline_mode=pl.Buffered(1))
    return pl.pallas_call(
        body,
        grid_spec=pltpu.PrefetchScalarGridSpec(
            num_scalar_prefetch=1, grid=(steps + 1,),
            in_specs=[a_half(0), a_half(1), pl.BlockSpec((T, tn), lambda n, s: (0, jnp.minimum(n, last)))],
            out_specs=[pl.BlockSpec((half, tn), lambda n, s: (0, jnp.maximum(n - 1, 0)))] * 2,
            scratch_shapes=[pltpu.VMEM((2, half, tn), F32), pltpu.VMEM((steps, half, tn), BF16),
                            pltpu.VMEM((2, half, tn), BF16), pltpu.SemaphoreType.DMA((steps,)),
                            pltpu.SemaphoreType.DMA((steps,))]),
        out_shape=[jax.ShapeDtypeStruct((half, N), F32), jax.ShapeDtypeStruct((half, N), BF16)],
        name=name, compiler_params=_params("arbitrary"),
    )(core, a, a, b)


def _local_step(x, meta, target, w_in_own, w_in, w_out, w_gu, w_down, small, chip, on_ffn_grads=None,
                on_mixer_grads=None):
    wg = _gate_weights(small["w_rgate"], small["w_igate"])
    bg = jnp.concatenate([small["b_rgate"], small["b_igate"]], axis=1)

    p, u, h0 = _in_proj_local(x, meta, small["mix_norm_g"], w_in_own, chip)
    p = _in_proj_rest(u, w_in, p, chip)
    y_rg, hs, xc = _rg_fwd(p, small["conv_w"], small["conv_b"], wg, bg, small["lru_lambda"], small["rg_norm_g"])
    y_hg, o_all, st_all = _hg_fwd(p, small["hg_lower_bound"], small["hg_norm_g"])
    h1, v, yb, gu, act, dh2, dh2b, loss, g_final = _ffn_fwd(
        h0, y_rg, y_hg, w_out, small["ffn_norm_g"], w_gu, w_down, small["final_norm_g"], target)

    g_w_down = _weight_grad(act, dh2b, "grad_w_down")
    dgu, dh1, dh1b, dy, g_ffn = _ffn_bwd(dh2b, gu, w_down, w_gu, h1, small["ffn_norm_g"], dh2, w_out)
    ffn_grads = {"w_down": g_w_down, "w_out": _weight_grad(yb, dh1b, "grad_w_out")}
    if on_ffn_grads is None:
        ffn_grads["w_gate_up"] = _weight_grad(v, dgu, "grad_w_gate_up")
        stages = None
    else:
        gate_up_sums = _weight_grad_chip_sum(v, dgu, "grad_w_gate_up")
        ffn_grads["w_gate_up"] = (None, None)
        stages = on_ffn_grads(ffn_grads)
    dp, g_lb, g_hgn = _hg_bwd(p, o_all, st_all, dy, small["hg_lower_bound"], small["hg_norm_g"])
    early = late = None
    if stages is not None:
        chip_sums, send = stages
        sums = dict(chip_sums(), w_gate_up=gate_up_sums)
        (dp, dy), sums = lax.optimization_barrier(((dp, dy), sums))
        early = send(sums)
    dp, g_cw, g_cb, g_wgate, g_bg, g_lam, g_rgn = _rg_bwd(
        p, xc, hs, dy, dp, small["conv_w"], small["conv_b"], wg, bg, small["lru_lambda"], small["rg_norm_g"])
    if on_mixer_grads is None:
        g_w_in = _weight_grad(u, dp, "grad_w_in")[0]
    else:
        sums = {"w_in": _weight_grad_chip_sum(u, dp, "grad_w_in")}
        (dp, dh1), sums = lax.optimization_barrier(((dp, dh1), sums))
        late = on_mixer_grads(sums)
        g_w_in = None
    grad_x, g_meta, g_mix = _in_bwd(dp, w_in, h0, small["mix_norm_g"], dh1)

    grads = {
        "w_in": g_w_in, "w_out": ffn_grads["w_out"][0],
        "w_gate_up": ffn_grads["w_gate_up"][0], "w_down": ffn_grads["w_down"][0],
        "meta_tokens": g_meta, "mix_norm_g": g_mix, "conv_w": g_cw, "conv_b": g_cb, "w_gates": g_wgate,
        "b_rgate": g_bg[:, :D_RG], "b_igate": g_bg[:, D_RG:], "lru_lambda": g_lam, "rg_norm_g": g_rgn,
        "hg_lower_bound": g_lb, "hg_norm_g": g_hgn, "ffn_norm_g": g_ffn, "final_norm_g": g_final,
    }
    return loss, grad_x, grads, early, late


ANY = pl.BlockSpec(memory_space=pl.ANY)
HALF = D_MODEL // 2

BIG = {"w_in": (D_MODEL, D_IN // N_CHIPS, True), "w_gate_up": (D_MODEL, 2 * D_FF // N_CHIPS, True),
       "w_out": (D_MODEL // N_CHIPS, D_MODEL, False), "w_down": (D_FF // N_CHIPS, D_MODEL, False)}
BIG_NAMES = tuple(BIG)
N_BIG = len(BIG_NAMES)


def _full_shape(name):
    rows, cols, by_col = BIG[name]
    return (rows, cols * N_CHIPS) if by_col else (rows * N_CHIPS, cols)


def _place():
    return lax.axis_index("x"), lax.axis_index("y"), lax.axis_index("c")


def _chip_of(x, y, r):
    fx, fy = (r + 1) >> 1, (r + 1) & 1
    return (1 - x if fx else x), (1 - y if fy else y)


def _half_of(ref, by_col, half):
    start = pl.multiple_of(half * HALF, 128)
    return ref.at[pl.ds(start, HALF), :] if by_col else ref.at[:, pl.ds(start, HALF)]


def _shard_of(ref, name, chip):
    rows, cols, by_col = BIG[name]
    if by_col:
        return ref.at[:, pl.ds(pl.multiple_of(chip * cols, 128), cols)]
    return ref.at[pl.ds(pl.multiple_of(chip * rows, 16), rows), :]


def _shard_half_of(ref, name, chip, half):
    rows, cols, by_col = BIG[name]
    start = pl.multiple_of(half * HALF, 128)
    if by_col:
        return ref.at[pl.ds(start, HALF), pl.ds(pl.multiple_of(chip * cols, 128), cols)]
    return ref.at[pl.ds(pl.multiple_of(chip * rows, 16), rows), pl.ds(start, HALF)]


def _shard_half_part_of(ref, name, chip, half, part):
    rows, cols, by_col = BIG[name]
    start = pl.multiple_of(half * HALF + part * (HALF // 2), 128)
    if by_col:
        return ref.at[pl.ds(start, HALF // 2), pl.ds(pl.multiple_of(chip * cols, 128), cols)]
    return ref.at[pl.ds(pl.multiple_of(chip * rows, 16), rows), pl.ds(start, HALF // 2)]


def _remote(src, dst, send_sems, recv_sems, k, dev):
    return pltpu.make_async_remote_copy(src_ref=src, dst_ref=dst, send_sem=send_sems.at[k], recv_sem=recv_sems.at[k],
                                        device_id=dev, device_id_type=MESH)


def _place_shards(w, small, chip, names, label):
    steps = 4
    n, ns = len(names), len(small)
    in_specs, out_specs = [], []
    for name in names:
        rows, cols, by_col = BIG[name]
        tr = rows // steps
        in_specs.append(pl.BlockSpec((tr, cols), lambda i, s: (i, 0)))
        if by_col:
            out_specs.append(pl.BlockSpec((tr, cols), lambda i, s: (i, s[0])))
        else:
            out_specs.append(pl.BlockSpec((tr, cols), lambda i, s: (s[0] * steps + i, 0)))

    def body(s_ref, *refs):
        ins, small_in = refs[:n], refs[n:n + ns]
        outs, small_out = refs[n + ns:2 * n + ns], refs[2 * n + ns:2 * (n + ns)]
        send_sems, recv_sems, local_sems = refs[2 * (n + ns):]
        i = pl.program_id(0)
        x, y, c = _place()
        chip_ = 2 * x + y
        others = [_chip_of(x, y, r) for r in range(3)]

        def block(a, q):
            cols = small[a].shape[1]
            return small_out[a].at[:, pl.ds(pl.multiple_of(q * cols, 128), cols)]

        def local(a):
            return pltpu.make_async_copy(small_in[a], block(a, chip_), local_sems.at[a])

        def remote(a, r):
            qx, qy = others[r]
            return _remote(small_in[a], block(a, chip_), send_sems, recv_sems, 3 * a + r, (qx, qy, c))

        @pl.when(i == 0)
        def _():
            for a in range(ns):
                local(a).start()
                for r in range(3):
                    remote(a, r).start()

        for a in range(n):
            outs[a][...] = ins[a][...].astype(BF16)

        @pl.when(i == steps - 1)
        def _():
            for a in range(ns):
                for r, (qx, qy) in enumerate(others):
                    landed = block(a, 2 * qx + qy)
                    _remote(landed, landed, send_sems, recv_sems, 3 * a + r, (qx, qy, c)).wait_recv()
                for r in range(3):
                    remote(a, r).wait_send()
                local(a).wait()

    out = pl.pallas_call(
        body,
        grid_spec=pltpu.PrefetchScalarGridSpec(
            num_scalar_prefetch=1, grid=(steps,), in_specs=in_specs + [ANY] * ns, out_specs=out_specs + [ANY] * ns,
            scratch_shapes=[pltpu.SemaphoreType.DMA((max(3 * ns, 1),)), pltpu.SemaphoreType.DMA((max(3 * ns, 1),)),
                            pltpu.SemaphoreType.DMA((max(ns, 1),))]),
        out_shape=([jax.ShapeDtypeStruct(_full_shape(name), BF16) for name in names]
                   + [jax.ShapeDtypeStruct((s.shape[0], s.shape[1] * N_CHIPS), F32) for s in small]),
        name=label, compiler_params=_params("arbitrary"),
    )(chip, *[w[name] for name in names], *small)
    return dict(zip(names, out[:n])), list(out[n:])


def _gather_weights(placed, small, names, label, collective_id):
    n, ns = len(names), len(small)
    hbm = pltpu.MemorySpace.HBM
    outs = [jax.new_ref(placed[nm], memory_space=hbm) for nm in names]
    small_in = [jax.new_ref(s, memory_space=hbm) for s in small]
    small_out = [jax.empty_ref(jax.ShapeDtypeStruct((s.shape[0], s.shape[1] * N_CHIPS), F32), memory_space=hbm)
                 for s in small]
    n_sems = 8 * n + 3 * ns

    @pl.kernel(mesh=plsc.ScalarSubcoreMesh(axis_name="seq", num_cores=1), name=label, out_type=(),
               scratch_types=(pltpu.SemaphoreType.DMA((n_sems,)), pltpu.SemaphoreType.DMA((n_sems,)),
                              pltpu.SemaphoreType.DMA((max(ns, 1),))),
               compiler_params=pltpu.CompilerParams(collective_id=collective_id))
    def launch(send_sems, recv_sems, local_sems):
        x, y, c = _place()
        chip = 2 * x + y
        sibling = (x, y, 1 - c)
        others = [_chip_of(x, y, r) for r in range(3)]
        near = others[:2]
        far = 2 * others[2][0] + others[2][1]
        _handshake([(qx, qy, c) for qx, qy in others] + [sibling])

        def small_block(a, q):
            cols = small[a].shape[1]
            return small_out[a].at[:, pl.ds(pl.multiple_of(q * cols, 128), cols)]

        local = [pltpu.make_async_copy(small_in[a], small_block(a, chip), local_sems.at[a]) for a in range(ns)]
        for cp in local:
            cp.start()

        sends = []
        for a, name in enumerate(names):
            mine = _shard_half_of(outs[a], name, chip, c)
            for r, (qx, qy) in enumerate(near):
                sends.append(_remote(mine, mine, send_sems, recv_sems, 8 * a + r, (qx, qy, c)))
        for a in range(ns):
            for r, (qx, qy) in enumerate(others):
                sends.append(_remote(small_in[a], small_block(a, chip), send_sems, recv_sems,
                                     8 * n + 3 * a + r, (qx, qy, c)))
        for cp in sends:
            cp.start()

        forwards = []

        def forward(piece, k, dev):
            cp = _remote(piece, piece, send_sems, recv_sems, k, dev)
            cp.start()
            forwards.append(cp)

        for a, name in enumerate(names):
            for r, (qx, qy) in enumerate(near):
                landed = _shard_half_of(outs[a], name, 2 * qx + qy, c)
                _remote(landed, landed, send_sems, recv_sems, 8 * a + r, (qx, qy, c)).wait_recv()
                ox, oy = near[1 - r]
                forward(_shard_half_part_of(outs[a], name, 2 * qx + qy, c, r), 8 * a + 2 + r, (ox, oy, c))
                forward(landed, 8 * a + 4 + r, sibling)
        for a, name in enumerate(names):
            for part in range(2):
                qx, qy = near[1 - part]
                landed = _shard_half_part_of(outs[a], name, far, c, part)
                _remote(landed, landed, send_sems, recv_sems, 8 * a + 2 + part, (qx, qy, c)).wait_recv()
                forward(landed, 8 * a + 6 + part, sibling)
        for a in range(ns):
            for r, (qx, qy) in enumerate(others):
                landed = small_block(a, 2 * qx + qy)
                _remote(landed, landed, send_sems, recv_sems, 8 * n + 3 * a + r, (qx, qy, c)).wait_recv()
        for a, name in enumerate(names):
            for r, (qx, qy) in enumerate(near):
                landed = _shard_half_of(outs[a], name, 2 * qx + qy, 1 - c)
                _remote(landed, landed, send_sems, recv_sems, 8 * a + 4 + r, sibling).wait_recv()
            for part in range(2):
                landed = _shard_half_part_of(outs[a], name, far, 1 - c, part)
                _remote(landed, landed, send_sems, recv_sems, 8 * a + 6 + part, sibling).wait_recv()
        for cp in sends + forwards:
            cp.wait_send()
        for cp in local:
            cp.wait()

    launch()
    return {nm: ref[...] for nm, ref in zip(names, outs)}, [ref[...] for ref in small_out]


def _exchange_halves(grads, names, label, collective_id):
    n = len(names)

    def body(*refs):
        ins, outs = refs[:n], refs[n:2 * n]
        send_sems, recv_sems = refs[2 * n:]
        x, y, c = _place()
        _handshake([(x, y, 1 - c)])
        copies = []
        for a, name in enumerate(names):
            copies.append(_remote(_half_of(ins[a], BIG[name][2], 1 - c), outs[a], send_sems, recv_sems, a,
                                  (x, y, 1 - c)))
        for cp in copies:
            cp.start()
        for cp in copies:
            cp.wait()

    def half_shape(name):
        r, c_ = _full_shape(name)
        return (HALF, c_) if BIG[name][2] else (r, HALF)

    out_type = tuple(jax.ShapeDtypeStruct(half_shape(nm), grads[nm].dtype) for nm in names)
    sems = (pltpu.SemaphoreType.DMA((n,)), pltpu.SemaphoreType.DMA((n,)))
    got = pl.kernel(
        body, mesh=plsc.ScalarSubcoreMesh(axis_name="seq", num_cores=1), name=label, out_type=out_type,
        scratch_types=sems, compiler_params=pltpu.CompilerParams(collective_id=collective_id),
    )(*[grads[nm] for nm in names])
    return dict(zip(names, got))


def _chip_sum(grads, got, names, core, label):
    n = len(names)
    steps = 4
    g_specs, blks = [], []
    for name in names:
        rows, cols = got[name].shape
        tr = rows // steps
        if BIG[name][2]:
            g_specs.append(pl.BlockSpec((tr, cols), lambda i, s: (s[0] * steps + i, 0)))
        else:
            g_specs.append(pl.BlockSpec((tr, HALF), lambda i, s: (i, s[0])))
        blks.append(pl.BlockSpec((tr, cols), lambda i, s: (i, 0)))

    def body(s_ref, *refs):
        for a in range(n):
            t = refs[a][...] + refs[n + a][...].astype(F32)
            refs[2 * n + a][...] = t
            refs[3 * n + a][...] = t.astype(BF16)

    out = pl.pallas_call(
        body,
        grid_spec=pltpu.PrefetchScalarGridSpec(num_scalar_prefetch=1, grid=(steps,), in_specs=g_specs + blks,
                                               out_specs=blks + blks),
        out_shape=([jax.ShapeDtypeStruct(got[nm].shape, F32) for nm in names]
                   + [jax.ShapeDtypeStruct(got[nm].shape, BF16) for nm in names]),
        name=label, compiler_params=_params("parallel"),
    )(core, *[grads[nm] for nm in names], *[got[nm] for nm in names])
    return {nm: (out[a], out[n + a]) for a, nm in enumerate(names)}


def _piece_shape(name):
    rows, cols, by_col = BIG[name]
    return (HALF, cols) if by_col else (rows, HALF)


def _handshake(peers):
    barrier = pltpu.get_barrier_semaphore()
    for peer in peers:
        pl.semaphore_signal(barrier, inc=1, device_id=peer, device_id_type=MESH)
    pl.semaphore_wait(barrier, len(peers))


def _send_chip_sums(sums, names, label, collective_id):
    n = len(names)

    def body(*refs):
        ins, outs = refs[:n], refs[n:2 * n]
        send_sems, recv_sems = refs[2 * n:]
        x, y, c = _place()
        others = [_chip_of(x, y, r) for r in range(3)]
        _handshake([(qx, qy, c) for qx, qy in others])
        copies = []
        for a, name in enumerate(names):
            for r, (qx, qy) in enumerate(others):
                copies.append(_remote(_shard_of(ins[a], name, 2 * qx + qy), outs[a].at[r], send_sems, recv_sems,
                                      3 * a + r, (qx, qy, c)))
        for cp in copies:
            cp.start()
        for cp in copies:
            cp.wait()

    return pl.kernel(
        body, mesh=plsc.ScalarSubcoreMesh(axis_name="seq", num_cores=1), name=label,
        out_type=tuple(jax.ShapeDtypeStruct((3,) + _piece_shape(nm), BF16) for nm in names),
        scratch_types=(pltpu.SemaphoreType.DMA((3 * n,)), pltpu.SemaphoreType.DMA((3 * n,))),
        compiler_params=pltpu.CompilerParams(collective_id=collective_id),
    )(*[sums[nm] for nm in names])


def _total(parts, chip_core):
    steps = 2
    in_specs, out_specs, operands = [], [], []
    for name in BIG_NAMES:
        by_col = BIG[name][2]
        pr, pc = _piece_shape(name)
        tr = pr // steps
        if by_col:
            in_specs.append(pl.BlockSpec((tr, pc), lambda i, s: (i, s[0])))
            out_specs.append(pl.BlockSpec((tr, pc), lambda i, s: (s[1] * steps + i, 0)))
        else:
            in_specs.append(pl.BlockSpec((tr, pc), lambda i, s: (s[0] * steps + i, 0)))
            out_specs.append(pl.BlockSpec((tr, pc), lambda i, s: (i, s[1])))
        for r in range(3):
            in_specs.append(pl.BlockSpec((None, tr, pc), lambda i, s, r=r: (r, i, 0)))
        own, got = parts[name]
        operands += [own, got, got, got]

    def body(s_ref, *refs):
        for a in range(N_BIG):
            o_ref, a_ref, b_ref, c_ref = refs[4 * a:4 * a + 4]
            refs[4 * N_BIG + a][...] = (((o_ref[...] + a_ref[...].astype(F32)) + b_ref[...].astype(F32))
                                        + c_ref[...].astype(F32))

    totals = pl.pallas_call(
        body,
        grid_spec=pltpu.PrefetchScalarGridSpec(num_scalar_prefetch=1, grid=(steps,), in_specs=in_specs,
                                               out_specs=out_specs),
        out_shape=[jax.ShapeDtypeStruct(BIG[name][:2], F32) for name in BIG_NAMES],
        name="totals", compiler_params=_params("parallel"),
    )(chip_core, *operands)
    return dict(zip(BIG_NAMES, totals))


VEC_ROWS = 32
VEC_ROW = {"mix_norm_g": 0, "conv_b": 1, "b_rgate": 2, "b_igate": 3, "lru_lambda": 4, "rg_norm_g": 5,
           "hg_lower_bound": 6, "hg_norm_g": 8, "ffn_norm_g": 9, "final_norm_g": 10, "loss": 11,
           "conv_w": 12, "meta_tokens": 16}
N_DEV = 8


def _all_reduce_small(pieces, gates, totals):
    names = list(pieces)
    n_small = 10
    hv, hg = VEC_ROWS // 2, gates.shape[0] // 2

    def body(*refs):
        ins = refs[:len(names)]
        g_ref = refs[len(names)]
        vec_ref, gsum_ref = refs[len(names) + 1 + N_BIG:len(names) + 3 + N_BIG]
        big = refs[len(names) + 3 + N_BIG:len(names) + 3 + 2 * N_BIG]
        (mine_v, sib_v, sib_g, chip_v, chip_g, got_v, got_g, send_sems, recv_sems) = refs[len(names) + 3 + 2 * N_BIG:]
        x, y, c = _place()
        chip = 2 * x + y
        sibling = (x, y, 1 - c)
        share = []
        for a, name in enumerate(BIG_NAMES):
            half = _half_of(big[a], BIG[name][2], c)
            share.append(_remote(half, half, send_sems, recv_sems, n_small + a, sibling))
        mine_v[...] = jnp.zeros_like(mine_v)
        for name, ref in zip(names, ins):
            nr, w = ref.shape
            mine_v[VEC_ROW[name]:VEC_ROW[name] + nr, 0:w] = ref[...]

        swap = [_remote(mine_v, sib_v, send_sems, recv_sems, 0, sibling),
                _remote(g_ref, sib_g, send_sems, recv_sems, 1, sibling)]
        for cp in swap:
            cp.start()
        for cp in swap:
            cp.wait()
        for cp in share:
            cp.start()
        chip_v[...] = mine_v[...] + sib_v[...]
        chip_g[...] = g_ref[...] + sib_g[...]

        rows_v = pl.ds(pl.multiple_of(c * hv, 8), hv)
        rows_g = pl.ds(pl.multiple_of(c * hg, 8), hg)
        got_v[chip] = chip_v[rows_v, :]
        got_g[chip] = chip_g[rows_g, :].astype(BF16)
        sends = []
        for r in range(3):
            qx, qy = _chip_of(x, y, r)
            sends.append(_remote(chip_v.at[rows_v, :], got_v.at[chip], send_sems, recv_sems, 2 + r, (qx, qy, c)))
            sends.append(_remote(got_g.at[chip], got_g.at[chip], send_sems, recv_sems, 5 + r, (qx, qy, c)))
        for cp in sends:
            cp.start()
        for cp in sends:
            cp.wait()
        vec_ref[rows_v, :] = ((got_v[0] + got_v[1]) + got_v[2]) + got_v[3]
        gsum_ref[rows_g, :] = ((got_g[0].astype(F32) + got_g[1].astype(F32)) + got_g[2].astype(F32)
                               ) + got_g[3].astype(F32)

        back = [_remote(vec_ref.at[rows_v, :], vec_ref.at[rows_v, :], send_sems, recv_sems, 8, sibling),
                _remote(gsum_ref.at[rows_g, :], gsum_ref.at[rows_g, :], send_sems, recv_sems, 9, sibling)]
        for cp in back:
            cp.start()
        theirs_v = vec_ref.at[pl.ds(pl.multiple_of((1 - c) * hv, 8), hv), :]
        theirs_g = gsum_ref.at[pl.ds(pl.multiple_of((1 - c) * hg, 8), hg), :]
        _remote(theirs_v, theirs_v, send_sems, recv_sems, 8, sibling).wait_recv()
        _remote(theirs_g, theirs_g, send_sems, recv_sems, 9, sibling).wait_recv()
        for cp in back:
            cp.wait_send()
        for a, name in enumerate(BIG_NAMES):
            theirs = _half_of(big[a], BIG[name][2], 1 - c)
            _remote(theirs, theirs, send_sems, recv_sems, n_small + a, sibling).wait_recv()
        for cp in share:
            cp.wait_send()

    vmem = pl.BlockSpec(memory_space=pltpu.VMEM)
    n_sems = n_small + N_BIG
    out = pl.pallas_call(
        body, in_specs=[vmem] * (len(names) + 1) + [ANY] * N_BIG, out_specs=[vmem, vmem] + [ANY] * N_BIG,
        out_shape=([jax.ShapeDtypeStruct((VEC_ROWS, D_MODEL), F32), jax.ShapeDtypeStruct(gates.shape, F32)]
                   + [jax.ShapeDtypeStruct(BIG[n][:2], F32) for n in BIG_NAMES]),
        input_output_aliases={len(names) + 1 + a: 2 + a for a in range(N_BIG)},
        scratch_shapes=[pltpu.VMEM((VEC_ROWS, D_MODEL), F32), pltpu.VMEM((VEC_ROWS, D_MODEL), F32),
                        pltpu.VMEM(gates.shape, F32), pltpu.VMEM((VEC_ROWS, D_MODEL), F32),
                        pltpu.VMEM(gates.shape, F32), pltpu.VMEM((N_CHIPS, hv, D_MODEL), F32),
                        pltpu.VMEM((N_CHIPS, hg) + gates.shape[1:], BF16),
                        pltpu.SemaphoreType.DMA((n_sems,)), pltpu.SemaphoreType.DMA((n_sems,))],
        name="all_reduce_small",
    )(*[pieces[n] for n in names], gates, *[totals[n] for n in BIG_NAMES])
    return out[0], out[1], dict(zip(BIG_NAMES, out[2:]))


def _adamw_math(w, g, m, v):
    m = ADAM_B1 * m + (1.0 - ADAM_B1) * g
    v = ADAM_B2 * v + (1.0 - ADAM_B2) * (g * g)
    m_hat = m / (1.0 - ADAM_B1 ** ADAM_STEP)
    v_hat = v / (1.0 - ADAM_B2 ** ADAM_STEP)
    delta = -ADAM_LR * (m_hat / (jnp.sqrt(v_hat) + ADAM_EPS) + ADAM_WD * w)
    return delta, m, v


def _adamw_big(w, g, m, v):
    steps = 8
    blks = []
    for name in BIG_NAMES:
        rows, cols, _ = BIG[name]
        blks.append(pl.BlockSpec((rows // steps, cols), lambda i: (i, 0)))

    def body(*refs):
        ins, outs = refs[:4 * N_BIG], refs[4 * N_BIG:]
        for a in range(N_BIG):
            w_ref, g_ref, m_ref, v_ref = (ins[k * N_BIG + a] for k in range(4))
            g = g_ref[...]
            d, nm, nv = _adamw_math(w_ref[...], g, m_ref[...], v_ref[...])
            outs[a][...] = g
            outs[N_BIG + a][...] = d
            outs[2 * N_BIG + a][...] = nm
            outs[3 * N_BIG + a][...] = nv

    shapes = [jax.ShapeDtypeStruct(BIG[name][:2], F32) for name in BIG_NAMES]
    out = pl.pallas_call(
        body, grid=(steps,), in_specs=blks * 4, out_specs=blks * 4, out_shape=shapes * 4,
        name="adamw_big", compiler_params=_params("parallel"),
    )(*[t[name] for t in (w, g, m, v) for name in BIG_NAMES])
    return {name: tuple(out[k * N_BIG + a] for k in range(4)) for a, name in enumerate(BIG_NAMES)}


SMALL = {"meta_tokens": (N_META, D_MODEL // N_CHIPS), "mix_norm_g": (1, D_MODEL), "conv_w": (CONV_W, D_RG // N_CHIPS),
         "conv_b": (1, D_RG), "w_rgate": (D_RG, RG_HEAD_DIM), "b_rgate": (1, D_RG), "w_igate": (D_RG, RG_HEAD_DIM),
         "b_igate": (1, D_RG), "lru_lambda": (1, D_RG), "rg_norm_g": (1, D_RG), "hg_lower_bound": (2, D_HG),
         "hg_norm_g": (1, HG_HEAD_DIM), "ffn_norm_g": (1, D_MODEL), "final_norm_g": (1, D_MODEL)}
SMALL_NAMES = tuple(SMALL)
SHARDED_SMALL = ("meta_tokens", "conv_w")


def _adamw_small(vec, gates, w, m, v):
    n = len(SMALL_NAMES)

    def body(*refs):
        vec_ref, gates_ref = refs[:2]
        w_refs, m_refs, v_refs = refs[2:2 + n], refs[2 + n:2 + 2 * n], refs[2 + 2 * n:2 + 3 * n]
        outs = refs[2 + 3 * n:]
        loss_ref = outs[0]
        x, y, _ = _place()
        chip = 2 * x + y
        loss_ref[...] = vec_ref[VEC_ROW["loss"]:VEC_ROW["loss"] + 1, 0:1]

        def update(k, g):
            g_ref, d_ref, nm_ref, nv_ref = outs[1 + 4 * k:5 + 4 * k]
            g_ref[...] = g
            d_ref[...], nm_ref[...], nv_ref[...] = _adamw_math(w_refs[k][...], g, m_refs[k][...], v_refs[k][...])

        for k, name in enumerate(SMALL_NAMES):
            nr, w_ = SMALL[name]
            if name == "w_rgate":
                update(k, gates_ref[0:D_RG, :])
            elif name == "w_igate":
                update(k, gates_ref[D_RG:2 * D_RG, :])
            elif name in SHARDED_SMALL:
                r0 = VEC_ROW[name]
                for q in range(N_CHIPS):
                    @pl.when(chip == q)
                    def _(k=k, r0=r0, nr=nr, w_=w_, q=q):
                        update(k, vec_ref[r0:r0 + nr, q * w_:(q + 1) * w_])
            else:
                r0 = VEC_ROW[name]
                update(k, vec_ref[r0:r0 + nr, 0:w_])

    vmem = pl.BlockSpec(memory_space=pltpu.VMEM)
    out_shape = [jax.ShapeDtypeStruct((1, 1), F32)]
    for name in SMALL_NAMES:
        out_shape += [jax.ShapeDtypeStruct(SMALL[name], F32)] * 4
    outs = pl.pallas_call(
        body, in_specs=[vmem] * (2 + 3 * n), out_specs=[vmem] * len(out_shape), out_shape=out_shape,
        name="adamw_small",
    )(vec, gates, *[w[k] for k in SMALL_NAMES], *[m[k] for k in SMALL_NAMES], *[v[k] for k in SMALL_NAMES])
    loss = outs[0]
    res = {name: tuple(outs[1 + 4 * k:5 + 4 * k]) for k, name in enumerate(SMALL_NAMES)}
    return loss, res


WEIGHT_NAMES = ("meta_tokens", "mix_norm_g", "w_in", "conv_w", "conv_b", "w_rgate", "b_rgate", "w_igate", "b_igate",
                "lru_lambda", "rg_norm_g", "hg_lower_bound", "hg_norm_g", "w_out", "ffn_norm_g", "w_gate_up", "w_down",
                "final_norm_g")


def _to_2d(name, a):
    if name in BIG:
        return a.reshape(BIG[name][:2])
    return a.reshape(SMALL[name])


def kernel(x, meta_tokens, mix_norm_g, w_in, conv_w, conv_b, w_rgate, b_rgate, w_igate, b_igate, lru_lambda, rg_norm_g, hg_lower_bound, hg_norm_g, w_out, ffn_norm_g, w_gate_up, w_down, final_norm_g, loss_target, m_meta_tokens, m_mix_norm_g, m_w_in, m_conv_w, m_conv_b, m_w_rgate, m_b_rgate, m_w_igate, m_b_igate, m_lru_lambda, m_rg_norm_g, m_hg_lower_bound, m_hg_norm_g, m_w_out, m_ffn_norm_g, m_w_gate_up, m_w_down, m_final_norm_g, v_meta_tokens, v_mix_norm_g, v_w_in, v_conv_w, v_conv_b, v_w_rgate, v_b_rgate, v_w_igate, v_b_igate, v_lru_lambda, v_rg_norm_g, v_hg_lower_bound, v_hg_norm_g, v_w_out, v_ffn_norm_g, v_w_gate_up, v_w_down, v_final_norm_g):
    w_raw = dict(zip(WEIGHT_NAMES, (meta_tokens, mix_norm_g, w_in, conv_w, conv_b, w_rgate, b_rgate, w_igate, b_igate,
                                    lru_lambda, rg_norm_g, hg_lower_bound, hg_norm_g, w_out, ffn_norm_g, w_gate_up,
                                    w_down, final_norm_g)))
    m_raw = dict(zip(WEIGHT_NAMES, (m_meta_tokens, m_mix_norm_g, m_w_in, m_conv_w, m_conv_b, m_w_rgate, m_b_rgate,
                                    m_w_igate, m_b_igate, m_lru_lambda, m_rg_norm_g, m_hg_lower_bound, m_hg_norm_g,
                                    m_w_out, m_ffn_norm_g, m_w_gate_up, m_w_down, m_final_norm_g)))
    v_raw = dict(zip(WEIGHT_NAMES, (v_meta_tokens, v_mix_norm_g, v_w_in, v_conv_w, v_conv_b, v_w_rgate, v_b_rgate,
                                    v_w_igate, v_b_igate, v_lru_lambda, v_rg_norm_g, v_hg_lower_bound, v_hg_norm_g,
                                    v_w_out, v_ffn_norm_g, v_w_gate_up, v_w_down, v_final_norm_g)))
    w = {k: _to_2d(k, a) for k, a in w_raw.items()}
    m = {k: _to_2d(k, a) for k, a in m_raw.items()}
    v = {k: _to_2d(k, a) for k, a in v_raw.items()}

    x_i, y_i, c_i = _place()
    core = jnp.reshape(c_i, (1,)).astype(jnp.int32)
    chip = jnp.reshape(2 * x_i + y_i, (1,)).astype(jnp.int32)
    chip_core = jnp.concatenate([chip, core])

    first_names, rest_names = ("w_in",), ("w_out", "w_gate_up", "w_down")
    placed, _ = _place_shards(w, [], chip, first_names, "place_first")
    first, _ = _gather_weights(placed, [], first_names, "gather_first", 1)
    placed, (meta_full, cw_full) = _place_shards(w, [w["meta_tokens"], w["conv_w"]], chip, rest_names, "place_shards")
    rest, _ = _gather_weights(placed, [], rest_names, "gather_rest", 2)
    full = {**first, **rest}

    seq = x.shape[1]
    small ={k: w[k] for k in SMALL_NAMES if k not in SHARDED_SMALL}
    small["conv_w"] = cw_full

    def send_to_chips(sums, names, tag, collective_id):
        arrived = _send_chip_sums({n: sums[n][1] for n in names}, names, "send_chip_sums_" + tag, collective_id)
        return {n: (sums[n][0], a) for n, a in zip(names, arrived)}

    def reduce_to_chips(grads, names, send_names, tag, collective_ids):
        got = _exchange_halves({n: grads[n][1] for n in names}, names, "exchange_halves_" + tag, collective_ids[0])

        def chip_sums():
            return _chip_sum({n: grads[n][0] for n in names}, got, names, core, "chip_sum_" + tag)

        return chip_sums, lambda sums: send_to_chips(sums, send_names, tag, collective_ids[1])

    ffn_names, mixer_names = ("w_gate_up", "w_down", "w_out"), ("w_in",)
    loss, grad_x, grads, parts, parts_mixer = _local_step(
        x.reshape(seq, D_MODEL), meta_full, loss_target.reshape(seq, D_MODEL),
        w["w_in"], full["w_in"], full["w_out"], full["w_gate_up"], full["w_down"], small, chip,
        on_ffn_grads=lambda g: reduce_to_chips(g, ("w_down", "w_out"), ffn_names, "ffn", (3, 4)),
        on_mixer_grads=lambda sums: send_to_chips(sums, mixer_names, "mixer", 5))
    parts.update(parts_mixer)
    totals = _total(parts, chip_core)
    pieces = {k: grads[k] for k in VEC_ROW if k != "loss"}
    pieces["loss"] = loss
    vec, gates, g_big = _all_reduce_small(pieces, grads["w_gates"], totals)
    loss_sum, res = _adamw_small(vec, gates, w, m, v)
    res.update(_adamw_big(w, g_big, m, v))

    out = [loss_sum.reshape(()), grad_x.reshape(1, seq, D_MODEL)]
    for j in range(4):
        out += [res[n][j].reshape(w_raw[n].shape) for n in WEIGHT_NAMES]
    return tuple(out)
```

```python
import math

import jax
import jax.numpy as jnp
from jax import lax
from jax.experimental import pallas as pl
from jax.experimental.pallas import tpu as pltpu
from jax.experimental.pallas import tpu_sc as plsc

F32 = jnp.float32
BF16 = jnp.bfloat16
MESH = pl.DeviceIdType.MESH

D_MODEL = 1024
D_RG = 512
RG_HEAD_DIM = 64
D_HG = 512
HG_HEAD_DIM = 128
HG_HEADS = 4
CHUNK = 64
SUB = 16
N_SUB = CHUNK // SUB
N_META = 16
PAD = CHUNK - N_META
D_IN = 3072
D_FF = 2816
CONV_W = 4
LRU_C = 8.0
EPS = 1e-6
EXP_CLAMP = 80.0
GELU_C = math.sqrt(2.0 / math.pi)
GELU_A = 0.044715
N_CHIPS = 4

ADAM_LR = 0.001
ADAM_B1 = 0.9
ADAM_B2 = 0.999
ADAM_EPS = 1e-08
ADAM_WD = 0.01
ADAM_STEP = 10

VMEM_LIMIT = 56 * 1024 * 1024


def _params(*sem):
    return pltpu.CompilerParams(dimension_semantics=sem, vmem_limit_bytes=VMEM_LIMIT)


def _row_tile(rows, target):
    best = None
    for t in range(16, min(rows, target) + 1, 16):
        if rows % t == 0:
            best = t
    assert best is not None, rows
    return best


def _sigmoid(x):
    return 0.5 * jnp.tanh(0.5 * x) + 0.5


def _dot(a, b):
    return jnp.dot(a, b, preferred_element_type=F32)


def _dot_nt(a, b):
    return lax.dot_general(a, b, (((1,), (1,)), ((), ())), preferred_element_type=F32)


def _dot_tn(a, b):
    return lax.dot_general(a, b, (((0,), (0,)), ((), ())), preferred_element_type=F32)


def _rms(x):
    return lax.rsqrt(jnp.mean(x * x, axis=-1, keepdims=True) + EPS)


def _rms_bwd(dn, n, r):
    return r * (dn - n * jnp.mean(dn * n, axis=-1, keepdims=True))


def _gelu_parts(x):
    t = jnp.tanh(GELU_C * (x + GELU_A * x * x * x))
    g = 0.5 * x * (1.0 + t)
    dg = 0.5 * (1.0 + t) + 0.5 * x * (1.0 - t * t) * GELU_C * (1.0 + 3.0 * GELU_A * x * x)
    return g, dg


def _softplus_neg(lam):
    e = jnp.exp(-jnp.abs(lam))
    w = 1.0 + e
    log1p = jnp.where(w == 1.0, e, jnp.log(w) * e / (w - 1.0))
    return jnp.maximum(-lam, 0.0) + log1p


def _head_mask():
    r = lax.broadcasted_iota(jnp.int32, (D_RG, D_RG), 0) // RG_HEAD_DIM
    c = lax.broadcasted_iota(jnp.int32, (D_RG, D_RG), 1) // RG_HEAD_DIM
    return r == c


def _head_fold():
    r = lax.broadcasted_iota(jnp.int32, (D_RG, RG_HEAD_DIM), 0) % RG_HEAD_DIM
    c = lax.broadcasted_iota(jnp.int32, (D_RG, RG_HEAD_DIM), 1)
    return (r == c).astype(F32)


def _gate_weights(w_r, w_i):
    def body(wr_ref, wi_ref, o_ref):
        fold = _head_fold()
        mask = _head_mask()
        for k, ref in enumerate((wr_ref, wi_ref)):
            full = _dot_nt(ref[...].astype(BF16), fold.astype(BF16))
            o_ref[:, k * D_RG:(k + 1) * D_RG] = jnp.where(mask, full, 0.0).astype(BF16)

    return pl.pallas_call(
        body, out_shape=jax.ShapeDtypeStruct((D_RG, 2 * D_RG), BF16), name="gate_weights",
    )(w_r, w_i)


HEAD = PAD + N_META


def _window_copies(seq_hbm, buf, sems, tm):
    def first(to_vmem):
        seq, vm = seq_hbm.at[pl.ds(0, tm - HEAD)], buf.at[0, pl.ds(HEAD, tm - HEAD)]
        return pltpu.make_async_copy(seq, vm, sems.at[0]) if to_vmem else pltpu.make_async_copy(vm, seq, sems.at[0])

    def later(j, slot, to_vmem):
        seq, vm = seq_hbm.at[pl.ds(pl.multiple_of(j * tm - HEAD, 8), tm)], buf.at[slot]
        if to_vmem:
            return pltpu.make_async_copy(seq, vm, sems.at[slot])
        return pltpu.make_async_copy(vm, seq, sems.at[slot])

    return first, later


def _fetch_window(seq_hbm, buf, sems, i, n_steps, tm):
    first, later = _window_copies(seq_hbm, buf, sems, tm)
    slot = i % 2

    @pl.when(i == 0)
    def _():
        first(True).start()

    if n_steps > 1:
        @pl.when(i + 1 < n_steps)
        def _():
            later(i + 1, 1 - slot, True).start()

    @pl.when(i == 0)
    def _():
        first(True).wait()

    if n_steps > 1:
        @pl.when(i > 0)
        def _():
            later(i, slot, True).wait()

    return slot


def _in_proj_local(x, meta, g1, w_own, chip):
    T = x.shape[0] + HEAD
    tm = _row_tile(T, 832)
    n_steps = T // tm
    cols = BIG["w_in"][1]

    def body(s_ref, x_hbm, meta_ref, g_ref, w_ref, p_ref, u_ref, h_ref, buf, sems, wb):
        i = pl.program_id(0)
        slot = _fetch_window(x_hbm, buf, sems, i, n_steps, tm)

        @pl.when(i == 0)
        def _():
            buf[0, 0:PAD, :] = jnp.zeros((PAD, D_MODEL), F32)
            buf[0, PAD:HEAD, :] = meta_ref[...]
            wb[...] = w_ref[...].astype(BF16)

        h = buf[slot]
        h_ref[...] = h
        u = (h * _rms(h) * g_ref[...]).astype(BF16)
        u_ref[...] = u
        p_ref[...] = _dot(u, wb[...])

    return pl.pallas_call(
        body,
        grid_spec=pltpu.PrefetchScalarGridSpec(
            num_scalar_prefetch=1, grid=(n_steps,),
            in_specs=[pl.BlockSpec(memory_space=pl.ANY),
                      pl.BlockSpec((N_META, D_MODEL), lambda i, s: (0, 0)),
                      pl.BlockSpec((1, D_MODEL), lambda i, s: (0, 0)),
                      pl.BlockSpec((D_MODEL, cols), lambda i, s: (0, 0))],
            out_specs=[pl.BlockSpec((tm, cols), lambda i, s: (i, s[0])),
                       pl.BlockSpec((tm, D_MODEL), lambda i, s: (i, 0)),
                       pl.BlockSpec((tm, D_MODEL), lambda i, s: (i, 0))],
            scratch_shapes=[pltpu.VMEM((2, tm, D_MODEL), F32), pltpu.SemaphoreType.DMA((2,)),
                            pltpu.VMEM((D_MODEL, cols), BF16)]),
        out_shape=[jax.ShapeDtypeStruct((T, D_IN), F32), jax.ShapeDtypeStruct((T, D_MODEL), BF16),
                   jax.ShapeDtypeStruct((T, D_MODEL), F32)],
        name="in_proj_local", compiler_params=_params("arbitrary"),
    )(chip, x, meta, g1, w_own)


def _in_proj_rest(u, w_in, p, chip):
    T = u.shape[0]
    tm = _row_tile(T, 2080)
    cols = BIG["w_in"][1]
    block = lambda j, s: (s[0] + 1 + j) % N_CHIPS

    def body(s_ref, u_ref, w_ref, p_in_ref, p_ref):
        p_ref[...] = _dot(u_ref[...], w_ref[...])

    return pl.pallas_call(
        body,
        grid_spec=pltpu.PrefetchScalarGridSpec(
            num_scalar_prefetch=1, grid=(N_CHIPS - 1, T // tm),
            in_specs=[pl.BlockSpec((tm, D_MODEL), lambda j, i, s: (i, 0)),
                      pl.BlockSpec((D_MODEL, cols), lambda j, i, s: (0, block(j, s))), ANY],
            out_specs=pl.BlockSpec((tm, cols), lambda j, i, s: (i, block(j, s)))),
        out_shape=jax.ShapeDtypeStruct((T, D_IN), F32),
        input_output_aliases={3: 0},
        name="in_proj_rest", compiler_params=_params("arbitrary", "arbitrary"),
    )(chip, u, w_in, p)


def _scan_block_fwd(A, B, rowi):
    for d in (1, 2, 4):
        a_sh = pltpu.roll(A, d, axis=0)
        b_sh = pltpu.roll(B, d, axis=0)
        m = rowi >= d
        B = jnp.where(m, A * b_sh + B, B)
        A = jnp.where(m, A * a_sh, A)
    return A, B


def _scan_block_bwd(A, B, rowi):
    for d in (1, 2, 4):
        a_sh = pltpu.roll(A, 8 - d, axis=0)
        b_sh = pltpu.roll(B, 8 - d, axis=0)
        m = rowi < 8 - d
        B = jnp.where(m, A * b_sh + B, B)
        A = jnp.where(m, A * a_sh, A)
    return A, B


def _rg_gates(xc, w_ref, bg_ref, lam):
    pre = _dot(xc.astype(BF16), w_ref[...]) + bg_ref[...]
    r = _sigmoid(pre[:, :D_RG])
    ig = _sigmoid(pre[:, D_RG:])
    sp = _softplus_neg(lam)
    la = -LRU_C * sp * r
    a = jnp.exp(la)
    th = jnp.tanh(la)
    u = 1.0 - th
    rc = pl.reciprocal(u, approx=True)
    rc = rc * (2.0 - u * rc)
    rc = rc * (2.0 - u * rc)
    m2 = -2.0 * th * rc
    inv_m = lax.rsqrt(jnp.maximum(m2, 1e-30))
    return r, ig, sp, a, m2 * inv_m, inv_m


def _conv(ext, cw_ref, cb_ref, tm):
    xc = cb_ref[...] + cw_ref[0:1, :] * ext[8 - 3:8 - 3 + tm, :]
    for j in range(1, CONV_W):
        xc = xc + cw_ref[j:j + 1, :] * ext[8 - 3 + j:8 - 3 + j + tm, :]
    return xc


def _scan_unroll(blocks):
    return 4 if blocks % 4 == 0 else 2 if blocks % 2 == 0 else 1


def _rg_fwd(p, cw, cb, wg, bg, lam, rg_g):
    T = p.shape[0]
    tm = _row_tile(T, 832)
    unroll = _scan_unroll(tm // 8)

    def body(xg_ref, cw_ref, cb_ref, w_ref, bg_ref, lam_ref, g_ref, y_ref, h_ref, xc_ref, ext, a_s, b_s, carry):
        i = pl.program_id(0)

        @pl.when(i == 0)
        def _():
            ext[0:8, :] = jnp.zeros((8, D_RG), F32)
            carry[...] = jnp.zeros((1, D_RG), F32)

        ext[8:8 + tm, :] = xg_ref[:, :D_RG]
        xc = _conv(ext, cw_ref, cb_ref, tm)
        xc_ref[...] = xc
        r, ig, sp, a, m, _ = _rg_gates(xc, w_ref, bg_ref, lam_ref[...])
        row = i * tm + lax.broadcasted_iota(jnp.int32, (tm, 1), 0)
        a_s[...] = a
        b_s[...] = jnp.where(row >= PAD, m * ig * xc, 0.0)
        rowi = lax.broadcasted_iota(jnp.int32, (8, D_RG), 0)

        def blk(j, c):
            for u in range(unroll):
                o = pl.multiple_of((j * unroll + u) * 8, 8)
                A, B = _scan_block_fwd(a_s[pl.ds(o, 8), :], b_s[pl.ds(o, 8), :], rowi)
                h = B + A * c
                h_ref[pl.ds(o, 8), :] = h
                c = h[7:8, :]
            return c

        carry[...] = lax.fori_loop(0, tm // (8 * unroll), blk, carry[...])
        ext[0:8, :] = ext[tm:tm + 8, :]
        g, _ = _gelu_parts(xg_ref[:, D_RG:])
        yy = g * h_ref[...]
        y_ref[...] = (yy * _rms(yy) * g_ref[...]).astype(BF16)

    vec = lambda n: pl.BlockSpec((1, n), lambda i: (0, 0))
    return pl.pallas_call(
        body, grid=(T // tm,),
        in_specs=[pl.BlockSpec((tm, 2 * D_RG), lambda i: (i, 0)),
                  pl.BlockSpec((CONV_W, D_RG), lambda i: (0, 0)), vec(D_RG),
                  pl.BlockSpec((D_RG, 2 * D_RG), lambda i: (0, 0)), vec(2 * D_RG), vec(D_RG), vec(D_RG)],
        out_specs=[pl.BlockSpec((tm, D_RG), lambda i: (i, 0))] * 3,
        out_shape=[jax.ShapeDtypeStruct((T, D_RG), BF16), jax.ShapeDtypeStruct((T, D_RG), F32),
                   jax.ShapeDtypeStruct((T, D_RG), F32)],
        scratch_shapes=[pltpu.VMEM((tm + 8, D_RG), F32), pltpu.VMEM((tm, D_RG), F32),
                        pltpu.VMEM((tm, D_RG), F32), pltpu.VMEM((1, D_RG), F32)],
        name="rg_fwd", compiler_params=_params("arbitrary"),
    )(p, cw, cb, wg, bg, lam, rg_g)


def _running_sum(x, down):
    r = lax.broadcasted_iota(jnp.int32, (CHUNK, CHUNK), 0)
    c = lax.broadcasted_iota(jnp.int32, (CHUNK, CHUNK), 1)
    tri = ((c <= r) if down else (c >= r)).astype(BF16)
    hi = x.astype(BF16)
    rest = x - hi.astype(F32)
    mid = rest.astype(BF16)
    lo = (rest - mid.astype(F32)).astype(BF16)
    return (_dot(tri, hi) + _dot(tri, mid)) + _dot(tri, lo)


def _hg_gates(hq, hf, lbraw_ref, valid):
    lb = _sigmoid(lbraw_ref[0:1, :] - lbraw_ref[1:2, :])
    sq = _sigmoid(hq)
    q = hq * sq
    sf = _sigmoid(hf)
    f = lb + (1.0 - lb) * sf
    lf = jnp.where(valid, jnp.log(f), 0.0)
    b = _running_sum(lf, True)
    return lb, sq, q, sf, f, b


def _hg_head(qh, kh, bh):
    b_last = bh[CHUNK - 1:CHUNK, :]
    refs = [bh[SUB * s:SUB * s + 1, :] for s in range(N_SUB)]
    r_sel = jnp.concatenate([jnp.broadcast_to(refs[s], (SUB, HG_HEAD_DIM)) for s in range(N_SUB)], axis=0)
    eb = jnp.exp(bh)
    eq = jnp.exp(bh - r_sel)
    ekh = jnp.exp(b_last - bh)
    ek = [jnp.exp(jnp.minimum(refs[s] - bh[:SUB * (s + 1), :], EXP_CLAMP)) for s in range(N_SUB)]
    qe = qh * eq

    def own_rows(s):
        parts = [jnp.zeros((SUB * s, HG_HEAD_DIM), F32)] if s else []
        parts.append(qe[SUB * s:SUB * (s + 1), :])
        if s < N_SUB - 1:
            parts.append(jnp.zeros((CHUNK - SUB * (s + 1), HG_HEAD_DIM), F32))
        return jnp.concatenate(parts, axis=0)

    q_hat = jnp.concatenate([own_rows(s) for s in range(N_SUB)], axis=1)

    def met_rows(s):
        n = SUB * (s + 1)
        ke = kh[:n, :] * ek[s]
        return ke if n == CHUNK else jnp.concatenate([ke, jnp.zeros((CHUNK - n, HG_HEAD_DIM), F32)], axis=0)

    k_til = jnp.concatenate([met_rows(s) for s in range(N_SUB)], axis=1)
    return b_last, eb, eq, ekh, ek, q_hat, k_til


def _causal():
    r = lax.broadcasted_iota(jnp.int32, (CHUNK, CHUNK), 0)
    c = lax.broadcasted_iota(jnp.int32, (CHUNK, CHUNK), 1)
    return r >= c


def _chunks_per_step(n_chunks):
    for c in (5, 4, 3, 2):
        if n_chunks % c == 0:
            return c
    return 1


def _hg_fwd(p, lbraw, hg_g):
    T = p.shape[0]
    n_chunks = T // CHUNK
    cps = _chunks_per_step(n_chunks)
    rows = cps * CHUNK

    def body(hq_ref, hf_ref, hi_ref, hg_ref, lb_ref, g_ref, y_ref, o_ref, st_all_ref, st):
        i = pl.program_id(0)

        @pl.when(i == 0)
        def _():
            st[...] = jnp.zeros_like(st)

        def chunk(j, carry):
            rs = pl.ds(pl.multiple_of(j * CHUNK, CHUNK), CHUNK)
            chunk_body(i * cps + j, hq_ref.at[rs, :], hf_ref.at[rs, :], hi_ref.at[rs, :], hg_ref.at[rs, :], lb_ref,
                       g_ref, y_ref.at[rs, :], o_ref.at[rs, :], st_all_ref.at[pl.ds(j, 1)], st)
            return carry

        lax.fori_loop(0, cps, chunk, 0, unroll=True)

    def chunk_body(n, hq_ref, hf_ref, hi_ref, hg_ref, lb_ref, g_ref, y_ref, o_ref, st_all_ref, st):
        valid = (n * CHUNK + lax.broadcasted_iota(jnp.int32, (CHUNK, 1), 0)) >= PAD
        hq, hf, v, hg = hq_ref[...], hf_ref[...], hi_ref[...], hg_ref[...]
        lb, sq, q, sf, f, b = _hg_gates(hq, hf, lb_ref, valid)
        k = 1.0 - f
        st_all_ref[0] = st[...]
        causal = _causal()
        v_t = v.T.astype(BF16)
        heads = [slice(h * HG_HEAD_DIM, (h + 1) * HG_HEAD_DIM) for h in range(HG_HEADS)]
        fac = []
        for sl in heads:
            qh, kh, bh = q[:, sl], k[:, sl], b[:, sl]
            b_last, eb, _, ekh, _, q_hat, k_til = _hg_head(qh, kh, bh)
            fac.append((jnp.exp(b_last), (qh * eb).astype(BF16), q_hat.astype(BF16), k_til.astype(BF16),
                        (kh * ekh).astype(BF16), v[:, sl].astype(BF16)))
        raw = []
        for sl, (_, q_til, q_hat, k_til, k_hat, _) in zip(heads, fac):
            st_h = st[sl, :]
            raw.append((_dot_nt(q_til, st_h.astype(BF16)), _dot_nt(q_hat, k_til), _dot(v_t[sl, :], k_hat), st_h))
        for sl, (e_last, _, _, _, _, vb), (inter, att, upd, st_h) in zip(heads, fac, raw):
            o = inter + _dot(jnp.where(causal, att, 0.0).astype(BF16), vb)
            st[sl, :] = st_h * e_last + upd
            o_ref[:, sl] = o
            hgh = hg[:, sl]
            y_ref[:, sl] = (o * _rms(o) * g_ref[...] * (hgh * _sigmoid(hgh))).astype(BF16)

    col = lambda j: pl.BlockSpec((rows, D_HG), lambda n: (n, j))
    return pl.pallas_call(
        body, grid=(n_chunks // cps,),
        in_specs=[col(2), col(3), col(4), col(5),
                  pl.BlockSpec((2, D_HG), lambda n: (0, 0)), pl.BlockSpec((1, HG_HEAD_DIM), lambda n: (0, 0))],
        out_specs=[pl.BlockSpec((rows, D_HG), lambda n: (n, 0)), pl.BlockSpec((rows, D_HG), lambda n: (n, 0)),
                   pl.BlockSpec((cps, D_HG, HG_HEAD_DIM), lambda n: (n, 0, 0))],
        out_shape=[jax.ShapeDtypeStruct((T, D_HG), BF16), jax.ShapeDtypeStruct((T, D_HG), F32),
                   jax.ShapeDtypeStruct((n_chunks, D_HG, HG_HEAD_DIM), F32)],
        scratch_shapes=[pltpu.VMEM((D_HG, HG_HEAD_DIM), F32)],
        name="hg_fwd", compiler_params=_params("arbitrary"),
    )(p, p, p, p, lbraw, hg_g)


def _ffn_fwd(h0, y_rg, y_hg, w_out, g2, w_gu, w_down, gf, target):
    T = h0.shape[0]
    tm = _row_tile(T, 320)
    n_steps = T // tm

    def body(h_ref, yr_ref, yh_ref, wo_ref, g2_ref, wgu_ref, wd_ref, gf_ref, t_hbm,
             h1_ref, v_ref, y_ref, gu_ref, act_ref, dh2_ref, dh2b_ref, loss_ref, gg_ref, tbuf, sems):
        i = pl.program_id(0)
        slot = _fetch_window(t_hbm, tbuf, sems, i, n_steps, tm)

        @pl.when(i == 0)
        def _():
            loss_ref[...] = jnp.zeros_like(loss_ref)
            gg_ref[...] = jnp.zeros_like(gg_ref)
            tbuf[0, 0:HEAD, :] = jnp.zeros((HEAD, D_MODEL), F32)

        y_ref[:, :D_RG] = yr_ref[...]
        y_ref[:, D_RG:] = yh_ref[...]
        h1 = h_ref[...] + _dot(y_ref[...], wo_ref[...])
        h1_ref[...] = h1
        v = (h1 * _rms(h1) * g2_ref[...]).astype(BF16)
        v_ref[...] = v

        gu = _dot(v, wgu_ref[...])
        gu_ref[...] = gu.astype(BF16)
        g = gu[:, :D_FF]
        act = (g * _sigmoid(g) * gu[:, D_FF:]).astype(BF16)
        act_ref[...] = act

        h2 = h1 + _dot(act, wd_ref[...])
        r = _rms(h2)
        n = h2 * r
        gf_ = gf_ref[...]
        row = i * tm + lax.broadcasted_iota(jnp.int32, (tm, 1), 0)
        err = jnp.where(row >= HEAD, n * gf_ - tbuf[slot], 0.0)
        loss_ref[...] += 0.5 * jnp.sum(jnp.mean(err * err, axis=-1, keepdims=True), axis=0, keepdims=True)
        dy = err * (1.0 / D_MODEL)
        gg_ref[...] += jnp.sum(dy * n, axis=0, keepdims=True)
        dh2 = _rms_bwd(dy * gf_, n, r)
        dh2_ref[...] = dh2
        dh2b_ref[...] = dh2.astype(BF16)

    row_spec = lambda n: pl.BlockSpec((tm, n), lambda i: (i, 0))
    vec = pl.BlockSpec((1, D_MODEL), lambda i: (0, 0))
    return pl.pallas_call(
        body, grid=(n_steps,),
        in_specs=[row_spec(D_MODEL), row_spec(D_RG), row_spec(D_HG), _resident((D_MODEL, D_MODEL)), vec,
                  _resident((D_MODEL, 2 * D_FF)), _resident((D_FF, D_MODEL)), vec,
                  pl.BlockSpec(memory_space=pl.ANY)],
        out_specs=[row_spec(D_MODEL), row_spec(D_MODEL), row_spec(D_MODEL), row_spec(2 * D_FF), row_spec(D_FF),
                   row_spec(D_MODEL), row_spec(D_MODEL), pl.BlockSpec((1, 1), lambda i: (0, 0)), vec],
        out_shape=[jax.ShapeDtypeStruct((T, D_MODEL), F32), jax.ShapeDtypeStruct((T, D_MODEL), BF16),
                   jax.ShapeDtypeStruct((T, D_MODEL), BF16), jax.ShapeDtypeStruct((T, 2 * D_FF), BF16),
                   jax.ShapeDtypeStruct((T, D_FF), BF16), jax.ShapeDtypeStruct((T, D_MODEL), F32),
                   jax.ShapeDtypeStruct((T, D_MODEL), BF16), jax.ShapeDtypeStruct((1, 1), F32),
                   jax.ShapeDtypeStruct((1, D_MODEL), F32)],
        scratch_shapes=[pltpu.VMEM((2, tm, D_MODEL), F32), pltpu.SemaphoreType.DMA((2,))],
        name="ffn_fwd", compiler_params=_params("arbitrary"),
    )(h0, y_rg, y_hg, w_out, g2, w_gu, w_down, gf, target)


def _resident(shape):
    return pl.BlockSpec(shape, lambda i: (0,) * len(shape), pipeline_mode=pl.Buffered(1))


def _ffn_bwd(dh2b, gu, w_down, w_gu, h1, g2, dh2, w_out):
    T = h1.shape[0]
    tm = _row_tile(T, 320)

    def body(d_ref, gu_ref, wd_ref, wgu_ref, h_ref, g_ref, d2_ref, wo_ref, dgu_ref, dh1_ref, dh1b_ref, dy_ref, gg_ref):
        i = pl.program_id(0)

        @pl.when(i == 0)
        def _():
            gg_ref[...] = jnp.zeros_like(gg_ref)

        dact = _dot_nt(d_ref[...], wd_ref[...]).astype(BF16)
        g = gu_ref[:, :D_FF]
        u = gu_ref[:, D_FF:]
        s = _sigmoid(g)
        dgu_ref[:, :D_FF] = dact * u * (s * (1.0 + g * (1.0 - s)))
        dgu_ref[:, D_FF:] = dact * (g * s)

        dv = _dot_nt(dgu_ref[...], wgu_ref[...])
        h1_ = h_ref[...]
        r = _rms(h1_)
        n = h1_ * r
        gg_ref[...] += jnp.sum(dv * n, axis=0, keepdims=True)
        dh1 = d2_ref[...] + _rms_bwd(dv * g_ref[...], n, r)
        dh1_ref[...] = dh1
        db = dh1.astype(BF16)
        dh1b_ref[...] = db
        dy_ref[...] = _dot_nt(db, wo_ref[...])

    row = lambda n: pl.BlockSpec((tm, n), lambda i: (i, 0))
    return pl.pallas_call(
        body, grid=(T // tm,),
        in_specs=[row(D_MODEL), row(2 * D_FF), _resident((D_FF, D_MODEL)), _resident((D_MODEL, 2 * D_FF)),
                  row(D_MODEL), pl.BlockSpec((1, D_MODEL), lambda i: (0, 0)), row(D_MODEL),
                  _resident((D_MODEL, D_MODEL))],
        out_specs=[row(2 * D_FF), row(D_MODEL), row(D_MODEL), row(D_MODEL),
                   pl.BlockSpec((1, D_MODEL), lambda i: (0, 0))],
        out_shape=[jax.ShapeDtypeStruct((T, 2 * D_FF), BF16), jax.ShapeDtypeStruct((T, D_MODEL), F32),
                   jax.ShapeDtypeStruct((T, D_MODEL), BF16), jax.ShapeDtypeStruct((T, D_MODEL), F32),
                   jax.ShapeDtypeStruct((1, D_MODEL), F32)],
        name="ffn_bwd", compiler_params=_params("arbitrary"),
    )(dh2b, gu, w_down, w_gu, h1, g2, dh2, w_out)


def _rg_bwd(p, xc_all, hs, dy, dp, cw, cb, wg, bg, lam, rg_g):
    T = p.shape[0]
    tm = _row_tile(T, 832)
    nt = T // tm
    hb = tm // 8
    unroll = _scan_unroll(hb)

    def body(xg_ref, xc_ref, h_ref, hh_ref, dy_ref, dp_in_ref, cw_ref, cb_ref, w_ref, bg_ref, lam_ref, g_ref,
             dp_ref, gcw_ref, gcb_ref, gw_ref, gbg_ref, glam_ref, gg_ref,
             dext, a_s, b_s, d_s, gacc, carry_d, carry_a):
        i = pl.program_id(0)
        t_idx = nt - 1 - i

        @pl.when(i == 0)
        def _():
            dext[tm:tm + 8, :] = jnp.zeros((8, D_RG), F32)
            carry_d[...] = jnp.zeros_like(carry_d)
            carry_a[...] = jnp.zeros_like(carry_a)
            gacc[...] = jnp.zeros_like(gacc)
            for ref in (gcw_ref, gcb_ref, gbg_ref, glam_ref, gg_ref, gw_ref):
                ref[...] = jnp.zeros_like(ref)

        first = t_idx == 0
        xc = xc_ref[...]
        lam_ = lam_ref[...]
        r, ig, sp, a, m, inv_m = _rg_gates(xc, w_ref, bg_ref, lam_)
        row = t_idx * tm + lax.broadcasted_iota(jnp.int32, (tm, 1), 0)
        valid = row >= PAD

        gr = xg_ref[:, D_RG:]
        g, dgelu = _gelu_parts(gr)
        h = h_ref[...]
        yy = g * h
        rr = _rms(yy)
        nn = yy * rr
        dy_ = dy_ref[...]
        gg_ref[...] += jnp.sum(dy_ * nn, axis=0, keepdims=True)
        dyy = _rms_bwd(dy_ * g_ref[...], nn, rr)
        dp_ref[:, D_RG:] = (dyy * h * dgelu).astype(BF16)

        a_s[...] = a
        b_s[...] = dyy * g
        rowi = lax.broadcasted_iota(jnp.int32, (8, D_RG), 0)

        def blk(jj, c):
            cd, ca = c
            for u in range(unroll):
                o = pl.multiple_of((hb - 1 - (jj * unroll + u)) * 8, 8)
                a_blk = a_s[pl.ds(o, 8), :]
                a_next = jnp.where(rowi == 7, ca, pltpu.roll(a_blk, 7, axis=0))
                A, B = _scan_block_bwd(a_next, b_s[pl.ds(o, 8), :], rowi)
                d = B + A * cd
                d_s[pl.ds(o, 8), :] = d
                cd, ca = d[0:1, :], a_blk[0:1, :]
            return cd, ca

        cd, ca = lax.fori_loop(0, hb // unroll, blk, (carry_d[...], carry_a[...]))
        carry_d[...] = cd
        carry_a[...] = ca
        delta = d_s[...]

        h_last_prev = jnp.where(first, 0.0, hh_ref[7:8, :])
        row0 = lax.broadcasted_iota(jnp.int32, (tm, 1), 0) == 0
        h_prev = jnp.where(row0, h_last_prev, pltpu.roll(h, 1, axis=0))
        dbx = jnp.where(valid, delta, 0.0)
        da = delta * h_prev
        di = dbx * m * xc
        dm = dbx * ig * xc
        dla = a * (da - dm * a * inv_m)
        dla = jnp.where(valid, dla, 0.0)
        glam_ref[...] += jnp.sum(dla * r, axis=0, keepdims=True) * (LRU_C / (1.0 + jnp.exp(lam_)))
        dr = (-LRU_C) * sp * dla
        dpre = jnp.concatenate([dr * r * (1.0 - r), di * ig * (1.0 - ig)], axis=1)
        gbg_ref[...] += jnp.sum(dpre, axis=0, keepdims=True)
        dpre_b = dpre.astype(BF16)
        gacc[...] += _dot_tn(xc.astype(BF16), dpre_b)
        dxc = dbx * m * ig + _dot_nt(dpre_b, w_ref[...])
        gcb_ref[...] += jnp.sum(dxc, axis=0, keepdims=True)
        dext[0:tm, :] = dxc
        xr = xg_ref[:, :D_RG]
        dxr = None
        for j in range(CONV_W):
            shifted = dext[3 - j:3 - j + tm, :]
            gcw_ref[j:j + 1, :] += jnp.sum(xr * shifted, axis=0, keepdims=True)
            tap = cw_ref[j:j + 1, :] * shifted
            dxr = tap if dxr is None else dxr + tap
        dp_ref[:, :D_RG] = dxr.astype(BF16)
        dext[tm:tm + 8, :] = dext[0:8, :]

        @pl.when(i == nt - 1)
        def _():
            fold = _head_fold()
            mask = _head_mask()
            fold_b = fold.astype(BF16)
            for k in range(2):
                blockdiag = jnp.where(mask, gacc[:, k * D_RG:(k + 1) * D_RG], 0.0)
                hi = blockdiag.astype(BF16)
                rest = blockdiag - hi.astype(F32)
                mid = rest.astype(BF16)
                lo = (rest - mid.astype(F32)).astype(BF16)
                gw_ref[k * D_RG:(k + 1) * D_RG, :] = (_dot(hi, fold_b) + _dot(mid, fold_b)) + _dot(lo, fold_b)

    vec = lambda n: pl.BlockSpec((1, n), lambda i: (0, 0))
    rev = lambda n: pl.BlockSpec((tm, n), lambda i: (nt - 1 - i, 0))
    halo = lambda n: pl.BlockSpec((8, n), lambda i: (jnp.maximum((nt - 1 - i) * hb - 1, 0), 0))
    return pl.pallas_call(
        body, grid=(nt,),
        in_specs=[rev(2 * D_RG), rev(D_RG), rev(D_RG), halo(D_RG), rev(D_RG), ANY,
                  pl.BlockSpec((CONV_W, D_RG), lambda i: (0, 0)), vec(D_RG),
                  pl.BlockSpec((D_RG, 2 * D_RG), lambda i: (0, 0)), vec(2 * D_RG), vec(D_RG), vec(D_RG)],
        out_specs=[rev(2 * D_RG), pl.BlockSpec((CONV_W, D_RG), lambda i: (0, 0)), vec(D_RG),
                   pl.BlockSpec((2 * D_RG, RG_HEAD_DIM), lambda i: (0, 0)), vec(2 * D_RG), vec(D_RG), vec(D_RG)],
        input_output_aliases={5: 0},
        out_shape=[jax.ShapeDtypeStruct((T, D_IN), BF16), jax.ShapeDtypeStruct((CONV_W, D_RG), F32),
                   jax.ShapeDtypeStruct((1, D_RG), F32), jax.ShapeDtypeStruct((2 * D_RG, RG_HEAD_DIM), F32),
                   jax.ShapeDtypeStruct((1, 2 * D_RG), F32), jax.ShapeDtypeStruct((1, D_RG), F32),
                   jax.ShapeDtypeStruct((1, D_RG), F32)],
        scratch_shapes=[pltpu.VMEM((tm + 8, D_RG), F32),
                        pltpu.VMEM((tm, D_RG), F32), pltpu.VMEM((tm, D_RG), F32), pltpu.VMEM((tm, D_RG), F32),
                        pltpu.VMEM((D_RG, 2 * D_RG), F32), pltpu.VMEM((1, D_RG), F32), pltpu.VMEM((1, D_RG), F32)],
        name="rg_bwd", compiler_params=_params("arbitrary"),
    )(p, xc_all, hs, hs, dy, dp, cw, cb, wg, bg, lam, rg_g)


def _hg_bwd(p, o_all, st_all, dy, lbraw, hg_g):
    T = p.shape[0]
    n_chunks = T // CHUNK
    cps = _chunks_per_step(n_chunks)
    rows = cps * CHUNK
    n_steps = n_chunks // cps

    def body(hq_ref, hf_ref, hi_ref, hg_ref, o_ref, st_ref, dy_ref, lb_ref, g_ref,
             dp_ref, glb_ref, gg_ref, dst):
        i = pl.program_id(0)

        @pl.when(i == 0)
        def _():
            dst[...] = jnp.zeros_like(dst)
            glb_ref[...] = jnp.zeros_like(glb_ref)
            gg_ref[...] = jnp.zeros_like(gg_ref)

        dp_ref[:, :2 * D_RG] = jnp.zeros((rows, 2 * D_RG), BF16)

        def chunk(jj, carry):
            j = cps - 1 - jj
            rs = pl.ds(pl.multiple_of(j * CHUNK, CHUNK), CHUNK)
            chunk_body((n_steps - 1 - i) * cps + j, hq_ref.at[rs, :], hf_ref.at[rs, :], hi_ref.at[rs, :],
                       hg_ref.at[rs, :], o_ref.at[rs, :], st_ref.at[pl.ds(j, 1)], dy_ref.at[rs, :], lb_ref, g_ref,
                       dp_ref.at[rs, pl.ds(2 * D_RG, 4 * D_HG)], glb_ref, gg_ref, dst)
            return carry

        lax.fori_loop(0, cps, chunk, 0, unroll=True)

    def chunk_body(n, hq_ref, hf_ref, hi_ref, hg_ref, o_ref, st_ref, dy_ref, lb_ref, g_ref,
                   dp_ref, glb_ref, gg_ref, dst):
        valid = (n * CHUNK + lax.broadcasted_iota(jnp.int32, (CHUNK, 1), 0)) >= PAD
        hq, hf, v, hg = hq_ref[...], hf_ref[...], hi_ref[...], hg_ref[...]
        lb, sq, q, sf, f, b = _hg_gates(hq, hf, lb_ref, valid)
        k = 1.0 - f
        causal = _causal()
        r_i = lax.broadcasted_iota(jnp.int32, (CHUNK, CHUNK), 0)
        c_i = lax.broadcasted_iota(jnp.int32, (CHUNK, CHUNK), 1)
        causal_t = r_i <= c_i
        is_last = lax.broadcasted_iota(jnp.int32, (CHUNK, 1), 0) == CHUNK - 1
        g_ = g_ref[...]
        db_parts, dq_parts, dk_parts = [], [], []
        gg = jnp.zeros((1, HG_HEAD_DIM), F32)
        heads = [slice(h * HG_HEAD_DIM, (h + 1) * HG_HEAD_DIM) for h in range(HG_HEADS)]

        do_parts = []
        for h, sl in enumerate(heads):
            o = o_ref[:, sl]
            ro = _rms(o)
            no = o * ro
            hgh = hg[:, sl]
            sg = _sigmoid(hgh)
            dyh = dy_ref[:, sl]
            dp_ref[:, 3 * D_HG + h * HG_HEAD_DIM:3 * D_HG + (h + 1) * HG_HEAD_DIM] = (
                dyh * no * g_ * sg * (1.0 + hgh * (1.0 - sg))).astype(BF16)
            dng = dyh * hgh * sg
            gg = gg + jnp.sum(dng * no, axis=0, keepdims=True)
            do_parts.append(_rms_bwd(dng * g_, no, ro))
        do_t = jnp.concatenate(do_parts, axis=1).T.astype(BF16)

        fac = []
        for sl, do in zip(heads, do_parts):
            qh, kh, bh = q[:, sl], k[:, sl], b[:, sl]
            b_last, eb, eq, ekh, ek, q_hat, k_til = _hg_head(qh, kh, bh)
            fac.append(dict(qh=qh, kh=kh, e_last=jnp.exp(b_last), eb=eb, eq=eq, ekh=ekh, ek=ek,
                            q_til=qh * eb, k_hat=kh * ekh, qhb=q_hat.astype(BF16), ktb=k_til.astype(BF16),
                            vb=v[:, sl].astype(BF16), dob=do.astype(BF16)))

        first = []
        for sl, t in zip(heads, fac):
            st_h = st_ref[0, sl, :]
            dst_h = dst[sl, :]
            dstb = dst_h.astype(BF16)
            first.append(dict(
                att_t=_dot_nt(t["ktb"], t["qhb"]), datt=_dot_nt(t["dob"], t["vb"]),
                datt_t=_dot_nt(t["vb"], t["dob"]), dk_hat=_dot(t["vb"], dstb),
                dv=_dot_nt(t["k_hat"].astype(BF16), dstb), dq_til=_dot(t["dob"], st_h.astype(BF16)),
                state=t["e_last"] * jnp.sum(dst_h * st_h, axis=0, keepdims=True)))
            dst[sl, :] = dst_h * t["e_last"] + _dot(do_t[sl, :], t["q_til"].astype(BF16))

        for h, (t, m) in enumerate(zip(fac, first)):
            qh, kh, eb, eq, ekh, ek = t["qh"], t["kh"], t["eb"], t["eq"], t["ekh"], t["ek"]
            q_til, k_hat, qhb, ktb, dob = t["q_til"], t["k_hat"], t["qhb"], t["ktb"], t["dob"]
            dk_hat, dq_til = m["dk_hat"], m["dq_til"]
            dv = m["dv"] + _dot(jnp.where(causal_t, m["att_t"], 0.0).astype(BF16), dob)
            dq_hat = _dot(jnp.where(causal, m["datt"], 0.0).astype(BF16), ktb)
            dk_til = _dot(jnp.where(causal_t, m["datt_t"], 0.0).astype(BF16), qhb)
            db_last = jnp.sum(dk_hat * k_hat, axis=0, keepdims=True) + m["state"]
            dq_sel = jnp.concatenate([dq_hat[SUB * s:SUB * (s + 1), s * HG_HEAD_DIM:(s + 1) * HG_HEAD_DIM]
                                      for s in range(N_SUB)], axis=0)
            dq_a = dq_sel * eq
            dk_rows, k_att_rows = [], []
            for b_ in range(N_SUB):
                rs = slice(SUB * b_, SUB * (b_ + 1))
                dk_sum = k_att_sum = None
                for s in range(b_, N_SUB):
                    cs = slice(s * HG_HEAD_DIM, (s + 1) * HG_HEAD_DIM)
                    d = dk_til[rs, cs]
                    t_dk = d * ek[s][rs, :]
                    t_att = ktb[rs, cs].astype(F32) * d
                    dk_sum = t_dk if dk_sum is None else dk_sum + t_dk
                    k_att_sum = t_att if k_att_sum is None else k_att_sum + t_att
                dk_rows.append(dk_sum)
                k_att_rows.append(k_att_sum)
            dk_a = jnp.concatenate(dk_rows, axis=0)
            db = (dq_til * q_til - dk_hat * k_hat + (qh * eq).astype(BF16).astype(F32) * dq_sel
                  - jnp.concatenate(k_att_rows, axis=0))
            db_parts.append(jnp.where(is_last, db + db_last, db))
            dq_parts.append(dq_til * eb + dq_a)
            dk_parts.append(dk_hat * ekh + dk_a)
            dp_ref[:, 2 * D_HG + h * HG_HEAD_DIM:2 * D_HG + (h + 1) * HG_HEAD_DIM] = dv.astype(BF16)

        gg_ref[...] += gg
        db = jnp.concatenate(db_parts, axis=1)
        dq = jnp.concatenate(dq_parts, axis=1)
        dk = jnp.concatenate(dk_parts, axis=1)
        dlf = jnp.where(valid, _running_sum(db, False), 0.0)
        dp_ref[:, :D_HG] = (dq * sq * (1.0 + hq * (1.0 - sq))).astype(BF16)
        df = dlf / f - dk
        dlb = jnp.sum(df * (1.0 - sf), axis=0, keepdims=True) * lb * (1.0 - lb)
        glb_ref[0:1, :] += dlb
        glb_ref[1:2, :] += -dlb
        dp_ref[:, D_HG:2 * D_HG] = (df * (1.0 - lb) * sf * (1.0 - sf)).astype(BF16)

    rev = lambda j: pl.BlockSpec((rows, D_HG), lambda i: (n_steps - 1 - i, j))
    return pl.pallas_call(
        body, grid=(n_steps,),
        in_specs=[rev(2), rev(3), rev(4), rev(5), rev(0),
                  pl.BlockSpec((cps, D_HG, HG_HEAD_DIM), lambda i: (n_steps - 1 - i, 0, 0)), rev(1),
                  pl.BlockSpec((2, D_HG), lambda i: (0, 0)), pl.BlockSpec((1, HG_HEAD_DIM), lambda i: (0, 0))],
        out_specs=[pl.BlockSpec((rows, D_IN), lambda i: (n_steps - 1 - i, 0)),
                   pl.BlockSpec((2, D_HG), lambda i: (0, 0)), pl.BlockSpec((1, HG_HEAD_DIM), lambda i: (0, 0))],
        out_shape=[jax.ShapeDtypeStruct((T, D_IN), BF16), jax.ShapeDtypeStruct((2, D_HG), F32),
                   jax.ShapeDtypeStruct((1, HG_HEAD_DIM), F32)],
        scratch_shapes=[pltpu.VMEM((D_HG, HG_HEAD_DIM), F32)],
        name="hg_bwd", compiler_params=_params("arbitrary"),
    )(p, p, p, p, o_all, st_all, dy, lbraw, hg_g)


def _in_bwd(dp, w_in, h0, g1, dh1):
    T = h0.shape[0]
    tm = _row_tile(T, 832)
    n_steps = T // tm

    def body(dp_ref, w_ref, h_ref, g_ref, d1_ref, gx_hbm, gmeta_ref, gg_ref, buf, sems):
        i = pl.program_id(0)
        first, later = _window_copies(gx_hbm, buf, sems, tm)
        slot = i % 2

        @pl.when(i == 0)
        def _():
            gg_ref[...] = jnp.zeros_like(gg_ref)

        if n_steps > 2:
            @pl.when(i == 2)
            def _():
                first(False).wait()

            @pl.when(i > 2)
            def _():
                later(i - 2, slot, False).wait()

        du = _dot_nt(dp_ref[...], w_ref[...])
        h0_ = h_ref[...]
        r = _rms(h0_)
        n = h0_ * r
        gg_ref[...] += jnp.sum(du * n, axis=0, keepdims=True)
        dh0 = d1_ref[...] + _rms_bwd(du * g_ref[...], n, r)
        buf[slot] = dh0

        @pl.when(i == 0)
        def _():
            gmeta_ref[...] = dh0[PAD:HEAD, :]
            first(False).start()

        if n_steps > 1:
            @pl.when(i > 0)
            def _():
                later(i, slot, False).start()

        @pl.when(i == n_steps - 1)
        def _():
            if n_steps == 1:
                first(False).wait()
            else:
                if n_steps == 2:
                    first(False).wait()
                else:
                    later(i - 1, 1 - slot, False).wait()
                later(i, slot, False).wait()

    row = lambda n: pl.BlockSpec((tm, n), lambda i: (i, 0))
    return pl.pallas_call(
        body, grid=(n_steps,),
        in_specs=[row(D_IN), _resident((D_MODEL, D_IN)),
                  row(D_MODEL), pl.BlockSpec((1, D_MODEL), lambda i: (0, 0)), row(D_MODEL)],
        out_specs=[pl.BlockSpec(memory_space=pl.ANY), pl.BlockSpec((N_META, D_MODEL), lambda i: (0, 0)),
                   pl.BlockSpec((1, D_MODEL), lambda i: (0, 0))],
        out_shape=[jax.ShapeDtypeStruct((T - HEAD, D_MODEL), F32), jax.ShapeDtypeStruct((N_META, D_MODEL), F32),
                   jax.ShapeDtypeStruct((1, D_MODEL), F32)],
        scratch_shapes=[pltpu.VMEM((2, tm, D_MODEL), F32), pltpu.SemaphoreType.DMA((2,))],
        name="in_bwd", compiler_params=_params("arbitrary"),
    )(dp, w_in, h0, g1, dh1)


def _col_tile(cols, target):
    best = None
    for t in range(128, min(cols, target) + 1, 128):
        if cols % t == 0:
            best = t
    assert best is not None, cols
    return best


MXU_DIM = 256


def _mxu_tile(cols, target):
    best = None
    for t in range(MXU_DIM, min(cols, target) + 1, MXU_DIM):
        if cols % t == 0:
            best = t
    assert best is not None, cols
    return best


def _weight_grad(a, b, name):
    T, M = a.shape
    N = b.shape[1]
    tm = _col_tile(M, 1408)
    tn = _mxu_tile(N, 768 if tm <= 1024 else 512)

    def body(a_ref, b_ref, o_ref, ob_ref):
        o = _dot_tn(a_ref[...], b_ref[...])
        o_ref[...] = o
        ob_ref[...] = o.astype(BF16)

    return pl.pallas_call(
        body, grid=(M // tm, N // tn),
        in_specs=[pl.BlockSpec((T, tm), lambda m, n: (0, m)), pl.BlockSpec((T, tn), lambda m, n: (0, n))],
        out_specs=[pl.BlockSpec((tm, tn), lambda m, n: (m, n))] * 2,
        out_shape=[jax.ShapeDtypeStruct((M, N), F32), jax.ShapeDtypeStruct((M, N), BF16)],
        name=name, compiler_params=_params("parallel", "parallel"),
    )(a, b)


def _weight_grad_chip_sum(a, b, name):
    T, M = a.shape
    N = b.shape[1]
    tn = _mxu_tile(N, 768)
    steps = N // tn
    half = M // 2

    def body(a_ref, b_ref, sum_ref, sumb_ref, acc, own, got, stage, send_sems, recv_sems):
        n = pl.program_id(0)
        x, y, c = _place()
        slot = n % 2

        def piece(k, s):
            return _remote(stage.at[s], got.at[k], send_sems, recv_sems, k, (x, y, 1 - c))

        @pl.when(n < steps)
        def _():
            acc[...] = _dot_tn(a_ref[...], b_ref[...])

            @pl.when(n >= 2)
            def _():
                piece(n - 2, slot).wait_send()

            stage[slot] = acc[pl.ds(pl.multiple_of((1 - c) * half, 128), half), :].astype(BF16)
            piece(n, slot).start()

        @pl.when(n >= 1)
        def _():
            piece(n - 1, 1 - slot).wait_recv()
            t = own[1 - slot] + got[n - 1].astype(F32)
            sum_ref[...] = t
            sumb_ref[...] = t.astype(BF16)

        @pl.when(n < steps)
        def _():
            own[slot] = acc[pl.ds(pl.multiple_of(c * half, 128), half), :]

        @pl.when(n == steps)
        def _():
            for k in range(max(steps - 2, 0), steps):
                piece(k, k % 2).wait_send()

    last = steps - 1
    return pl.pallas_call(
        body, grid=(steps + 1,),
        in_specs=[_resident((T, M)), pl.BlockSpec((T, tn), lambda n: (0, jnp.minimum(n, last)))],
        out_specs=[pl.BlockSpec((half, tn), lambda n: (0, jnp.maximum(n - 1, 0)))] * 2,
        out_shape=[jax.ShapeDtypeStruct((half, N), F32), jax.ShapeDtypeStruct((half, N), BF16)],
        scratch_shapes=[pltpu.VMEM((M, tn), F32), pltpu.VMEM((2, half, tn), F32), pltpu.VMEM((steps, half, tn), BF16),
                        pltpu.VMEM((2, half, tn), BF16), pltpu.SemaphoreType.DMA((steps,)),
                        pltpu.SemaphoreType.DMA((steps,))],
        name=name, compiler_params=_params("arbitrary"),
    )(a, b)


def _local_step(x, meta, target, w_in_own, w_in, w_out, w_gu, w_down, small, chip, on_ffn_grads=None,
                on_mixer_grads=None):
    wg = _gate_weights(small["w_rgate"], small["w_igate"])
    bg = jnp.concatenate([small["b_rgate"], small["b_igate"]], axis=1)

    p, u, h0 = _in_proj_local(x, meta, small["mix_norm_g"], w_in_own, chip)
    p = _in_proj_rest(u, w_in, p, chip)
    y_rg, hs, xc = _rg_fwd(p, small["conv_w"], small["conv_b"], wg, bg, small["lru_lambda"], small["rg_norm_g"])
    y_hg, o_all, st_all = _hg_fwd(p, small["hg_lower_bound"], small["hg_norm_g"])
    h1, v, yb, gu, act, dh2, dh2b, loss, g_final = _ffn_fwd(
        h0, y_rg, y_hg, w_out, small["ffn_norm_g"], w_gu, w_down, small["final_norm_g"], target)

    g_w_down = _weight_grad(act, dh2b, "grad_w_down")
    dgu, dh1, dh1b, dy, g_ffn = _ffn_bwd(dh2b, gu, w_down, w_gu, h1, small["ffn_norm_g"], dh2, w_out)
    ffn_grads = {"w_down": g_w_down, "w_out": _weight_grad(yb, dh1b, "grad_w_out")}
    if on_ffn_grads is None:
        ffn_grads["w_gate_up"] = _weight_grad(v, dgu, "grad_w_gate_up")
        stages = None
    else:
        gate_up_sums = _weight_grad_chip_sum(v, dgu, "grad_w_gate_up")
        ffn_grads["w_gate_up"] = (None, None)
        stages = on_ffn_grads(ffn_grads)
    dp, g_lb, g_hgn = _hg_bwd(p, o_all, st_all, dy, small["hg_lower_bound"], small["hg_norm_g"])
    early = late = None
    if stages is not None:
        chip_sums, send = stages
        sums = dict(chip_sums(), w_gate_up=gate_up_sums)
        (dp, dy), sums = lax.optimization_barrier(((dp, dy), sums))
        early = send(sums)
    dp, g_cw, g_cb, g_wgate, g_bg, g_lam, g_rgn = _rg_bwd(
        p, xc, hs, dy, dp, small["conv_w"], small["conv_b"], wg, bg, small["lru_lambda"], small["rg_norm_g"])
    if on_mixer_grads is None:
        g_w_in = _weight_grad(u, dp, "grad_w_in")[0]
    else:
        sums = {"w_in": _weight_grad_chip_sum(u, dp, "grad_w_in")}
        (dp, dh1), sums = lax.optimization_barrier(((dp, dh1), sums))
        late = on_mixer_grads(sums)
        g_w_in = None
    grad_x, g_meta, g_mix = _in_bwd(dp, w_in, h0, small["mix_norm_g"], dh1)

    grads = {
        "w_in": g_w_in, "w_out": ffn_grads["w_out"][0],
        "w_gate_up": ffn_grads["w_gate_up"][0], "w_down": ffn_grads["w_down"][0],
        "meta_tokens": g_meta, "mix_norm_g": g_mix, "conv_w": g_cw, "conv_b": g_cb, "w_gates": g_wgate,
        "b_rgate": g_bg[:, :D_RG], "b_igate": g_bg[:, D_RG:], "lru_lambda": g_lam, "rg_norm_g": g_rgn,
        "hg_lower_bound": g_lb, "hg_norm_g": g_hgn, "ffn_norm_g": g_ffn, "final_norm_g": g_final,
    }
    return loss, grad_x, grads, early, late


ANY = pl.BlockSpec(memory_space=pl.ANY)
HALF = D_MODEL // 2

BIG = {"w_in": (D_MODEL, D_IN // N_CHIPS, True), "w_gate_up": (D_MODEL, 2 * D_FF // N_CHIPS, True),
       "w_out": (D_MODEL // N_CHIPS, D_MODEL, False), "w_down": (D_FF // N_CHIPS, D_MODEL, False)}
BIG_NAMES = tuple(BIG)
N_BIG = len(BIG_NAMES)


def _full_shape(name):
    rows, cols, by_col = BIG[name]
    return (rows, cols * N_CHIPS) if by_col else (rows * N_CHIPS, cols)


def _place():
    return lax.axis_index("x"), lax.axis_index("y"), lax.axis_index("c")


def _chip_of(x, y, r):
    fx, fy = (r + 1) >> 1, (r + 1) & 1
    return (1 - x if fx else x), (1 - y if fy else y)


def _half_of(ref, by_col, half):
    start = pl.multiple_of(half * HALF, 128)
    return ref.at[pl.ds(start, HALF), :] if by_col else ref.at[:, pl.ds(start, HALF)]


def _shard_of(ref, name, chip):
    rows, cols, by_col = BIG[name]
    if by_col:
        return ref.at[:, pl.ds(pl.multiple_of(chip * cols, 128), cols)]
    return ref.at[pl.ds(pl.multiple_of(chip * rows, 16), rows), :]


def _shard_half_of(ref, name, chip, half):
    rows, cols, by_col = BIG[name]
    start = pl.multiple_of(half * HALF, 128)
    if by_col:
        return ref.at[pl.ds(start, HALF), pl.ds(pl.multiple_of(chip * cols, 128), cols)]
    return ref.at[pl.ds(pl.multiple_of(chip * rows, 16), rows), pl.ds(start, HALF)]


def _shard_half_part_of(ref, name, chip, half, part):
    rows, cols, by_col = BIG[name]
    start = pl.multiple_of(half * HALF + part * (HALF // 2), 128)
    if by_col:
        return ref.at[pl.ds(start, HALF // 2), pl.ds(pl.multiple_of(chip * cols, 128), cols)]
    return ref.at[pl.ds(pl.multiple_of(chip * rows, 16), rows), pl.ds(start, HALF // 2)]


def _remote(src, dst, send_sems, recv_sems, k, dev):
    return pltpu.make_async_remote_copy(src_ref=src, dst_ref=dst, send_sem=send_sems.at[k], recv_sem=recv_sems.at[k],
                                        device_id=dev, device_id_type=MESH)


def _place_shards(w, small, chip, names, label):
    steps = 4
    n, ns = len(names), len(small)
    in_specs, out_specs = [], []
    for name in names:
        rows, cols, by_col = BIG[name]
        tr = rows // steps
        in_specs.append(pl.BlockSpec((tr, cols), lambda i, s: (i, 0)))
        if by_col:
            out_specs.append(pl.BlockSpec((tr, cols), lambda i, s: (i, s[0])))
        else:
            out_specs.append(pl.BlockSpec((tr, cols), lambda i, s: (s[0] * steps + i, 0)))

    def body(s_ref, *refs):
        ins, small_in = refs[:n], refs[n:n + ns]
        outs, small_out = refs[n + ns:2 * n + ns], refs[2 * n + ns:2 * (n + ns)]
        send_sems, recv_sems, local_sems = refs[2 * (n + ns):]
        i = pl.program_id(0)
        x, y, c = _place()
        chip_ = 2 * x + y
        others = [_chip_of(x, y, r) for r in range(3)]

        def block(a, q):
            cols = small[a].shape[1]
            return small_out[a].at[:, pl.ds(pl.multiple_of(q * cols, 128), cols)]

        def local(a):
            return pltpu.make_async_copy(small_in[a], block(a, chip_), local_sems.at[a])

        def remote(a, r):
            qx, qy = others[r]
            return _remote(small_in[a], block(a, chip_), send_sems, recv_sems, 3 * a + r, (qx, qy, c))

        @pl.when(i == 0)
        def _():
            for a in range(ns):
                local(a).start()
                for r in range(3):
                    remote(a, r).start()

        for a in range(n):
            outs[a][...] = ins[a][...].astype(BF16)

        @pl.when(i == steps - 1)
        def _():
            for a in range(ns):
                for r, (qx, qy) in enumerate(others):
                    landed = block(a, 2 * qx + qy)
                    _remote(landed, landed, send_sems, recv_sems, 3 * a + r, (qx, qy, c)).wait_recv()
                for r in range(3):
                    remote(a, r).wait_send()
                local(a).wait()

    out = pl.pallas_call(
        body,
        grid_spec=pltpu.PrefetchScalarGridSpec(
            num_scalar_prefetch=1, grid=(steps,), in_specs=in_specs + [ANY] * ns, out_specs=out_specs + [ANY] * ns,
            scratch_shapes=[pltpu.SemaphoreType.DMA((max(3 * ns, 1),)), pltpu.SemaphoreType.DMA((max(3 * ns, 1),)),
                            pltpu.SemaphoreType.DMA((max(ns, 1),))]),
        out_shape=([jax.ShapeDtypeStruct(_full_shape(name), BF16) for name in names]
                   + [jax.ShapeDtypeStruct((s.shape[0], s.shape[1] * N_CHIPS), F32) for s in small]),
        name=label, compiler_params=_params("arbitrary"),
    )(chip, *[w[name] for name in names], *small)
    return dict(zip(names, out[:n])), list(out[n:])


def _gather_weights(placed, small, names, label, collective_id):
    n, ns = len(names), len(small)
    hbm = pltpu.MemorySpace.HBM
    outs = [jax.new_ref(placed[nm], memory_space=hbm) for nm in names]
    small_in = [jax.new_ref(s, memory_space=hbm) for s in small]
    small_out = [jax.empty_ref(jax.ShapeDtypeStruct((s.shape[0], s.shape[1] * N_CHIPS), F32), memory_space=hbm)
                 for s in small]
    n_sems = 8 * n + 3 * ns

    @pl.kernel(mesh=plsc.ScalarSubcoreMesh(axis_name="seq", num_cores=1), name=label, out_type=(),
               scratch_types=(pltpu.SemaphoreType.DMA((n_sems,)), pltpu.SemaphoreType.DMA((n_sems,)),
                              pltpu.SemaphoreType.DMA((max(ns, 1),))),
               compiler_params=pltpu.CompilerParams(collective_id=collective_id))
    def launch(send_sems, recv_sems, local_sems):
        x, y, c = _place()
        chip = 2 * x + y
        sibling = (x, y, 1 - c)
        others = [_chip_of(x, y, r) for r in range(3)]
        near = others[:2]
        far = 2 * others[2][0] + others[2][1]
        _handshake([(qx, qy, c) for qx, qy in others] + [sibling])

        def small_block(a, q):
            cols = small[a].shape[1]
            return small_out[a].at[:, pl.ds(pl.multiple_of(q * cols, 128), cols)]

        local = [pltpu.make_async_copy(small_in[a], small_block(a, chip), local_sems.at[a]) for a in range(ns)]
        for cp in local:
            cp.start()

        sends = []
        for a, name in enumerate(names):
            mine = _shard_half_of(outs[a], name, chip, c)
            for r, (qx, qy) in enumerate(near):
                sends.append(_remote(mine, mine, send_sems, recv_sems, 8 * a + r, (qx, qy, c)))
        for a in range(ns):
            for r, (qx, qy) in enumerate(others):
                sends.append(_remote(small_in[a], small_block(a, chip), send_sems, recv_sems,
                                     8 * n + 3 * a + r, (qx, qy, c)))
        for cp in sends:
            cp.start()

        forwards = []

        def forward(piece, k, dev):
            cp = _remote(piece, piece, send_sems, recv_sems, k, dev)
            cp.start()
            forwards.append(cp)

        for a, name in enumerate(names):
            for r, (qx, qy) in enumerate(near):
                landed = _shard_half_of(outs[a], name, 2 * qx + qy, c)
                _remote(landed, landed, send_sems, recv_sems, 8 * a + r, (qx, qy, c)).wait_recv()
                ox, oy = near[1 - r]
                forward(_shard_half_part_of(outs[a], name, 2 * qx + qy, c, r), 8 * a + 2 + r, (ox, oy, c))
                forward(landed, 8 * a + 4 + r, sibling)
        for a, name in enumerate(names):
            for part in range(2):
                qx, qy = near[1 - part]
                landed = _shard_half_part_of(outs[a], name, far, c, part)
                _remote(landed, landed, send_sems, recv_sems, 8 * a + 2 + part, (qx, qy, c)).wait_recv()
                forward(landed, 8 * a + 6 + part, sibling)
        for a in range(ns):
            for r, (qx, qy) in enumerate(others):
                landed = small_block(a, 2 * qx + qy)
                _remote(landed, landed, send_sems, recv_sems, 8 * n + 3 * a + r, (qx, qy, c)).wait_recv()
        for a, name in enumerate(names):
            for r, (qx, qy) in enumerate(near):
                landed = _shard_half_of(outs[a], name, 2 * qx + qy, 1 - c)
                _remote(landed, landed, send_sems, recv_sems, 8 * a + 4 + r, sibling).wait_recv()
            for part in range(2):
                landed = _shard_half_part_of(outs[a], name, far, 1 - c, part)
                _remote(landed, landed, send_sems, recv_sems, 8 * a + 6 + part, sibling).wait_recv()
        for cp in sends + forwards:
            cp.wait_send()
        for cp in local:
            cp.wait()

    launch()
    return {nm: ref[...] for nm, ref in zip(names, outs)}, [ref[...] for ref in small_out]


def _exchange_halves(grads, names, label, collective_id):
    n = len(names)

    def body(*refs):
        ins, outs = refs[:n], refs[n:2 * n]
        send_sems, recv_sems = refs[2 * n:]
        x, y, c = _place()
        _handshake([(x, y, 1 - c)])
        copies = []
        for a, name in enumerate(names):
            copies.append(_remote(_half_of(ins[a], BIG[name][2], 1 - c), outs[a], send_sems, recv_sems, a,
                                  (x, y, 1 - c)))
        for cp in copies:
            cp.start()
        for cp in copies:
            cp.wait()

    def half_shape(name):
        r, c_ = _full_shape(name)
        return (HALF, c_) if BIG[name][2] else (r, HALF)

    out_type = tuple(jax.ShapeDtypeStruct(half_shape(nm), grads[nm].dtype) for nm in names)
    sems = (pltpu.SemaphoreType.DMA((n,)), pltpu.SemaphoreType.DMA((n,)))
    got = pl.kernel(
        body, mesh=plsc.ScalarSubcoreMesh(axis_name="seq", num_cores=1), name=label, out_type=out_type,
        scratch_types=sems, compiler_params=pltpu.CompilerParams(collective_id=collective_id),
    )(*[grads[nm] for nm in names])
    return dict(zip(names, got))


def _chip_sum(grads, got, names, core, label):
    n = len(names)
    steps = 4
    g_specs, blks = [], []
    for name in names:
        rows, cols = got[name].shape
        tr = rows // steps
        if BIG[name][2]:
            g_specs.append(pl.BlockSpec((tr, cols), lambda i, s: (s[0] * steps + i, 0)))
        else:
            g_specs.append(pl.BlockSpec((tr, HALF), lambda i, s: (i, s[0])))
        blks.append(pl.BlockSpec((tr, cols), lambda i, s: (i, 0)))

    def body(s_ref, *refs):
        for a in range(n):
            t = refs[a][...] + refs[n + a][...].astype(F32)
            refs[2 * n + a][...] = t
            refs[3 * n + a][...] = t.astype(BF16)

    out = pl.pallas_call(
        body,
        grid_spec=pltpu.PrefetchScalarGridSpec(num_scalar_prefetch=1, grid=(steps,), in_specs=g_specs + blks,
                                               out_specs=blks + blks),
        out_shape=([jax.ShapeDtypeStruct(got[nm].shape, F32) for nm in names]
                   + [jax.ShapeDtypeStruct(got[nm].shape, BF16) for nm in names]),
        name=label, compiler_params=_params("parallel"),
    )(core, *[grads[nm] for nm in names], *[got[nm] for nm in names])
    return {nm: (out[a], out[n + a]) for a, nm in enumerate(names)}


def _piece_shape(name):
    rows, cols, by_col = BIG[name]
    return (HALF, cols) if by_col else (rows, HALF)


def _handshake(peers):
    barrier = pltpu.get_barrier_semaphore()
    for peer in peers:
        pl.semaphore_signal(barrier, inc=1, device_id=peer, device_id_type=MESH)
    pl.semaphore_wait(barrier, len(peers))


def _send_chip_sums(sums, names, label, collective_id):
    n = len(names)

    def body(*refs):
        ins, outs = refs[:n], refs[n:2 * n]
        send_sems, recv_sems = refs[2 * n:]
        x, y, c = _place()
        others = [_chip_of(x, y, r) for r in range(3)]
        _handshake([(qx, qy, c) for qx, qy in others])
        copies = []
        for a, name in enumerate(names):
            for r, (qx, qy) in enumerate(others):
                copies.append(_remote(_shard_of(ins[a], name, 2 * qx + qy), outs[a].at[r], send_sems, recv_sems,
                                      3 * a + r, (qx, qy, c)))
        for cp in copies:
            cp.start()
        for cp in copies:
            cp.wait()

    return pl.kernel(
        body, mesh=plsc.ScalarSubcoreMesh(axis_name="seq", num_cores=1), name=label,
        out_type=tuple(jax.ShapeDtypeStruct((3,) + _piece_shape(nm), BF16) for nm in names),
        scratch_types=(pltpu.SemaphoreType.DMA((3 * n,)), pltpu.SemaphoreType.DMA((3 * n,))),
        compiler_params=pltpu.CompilerParams(collective_id=collective_id),
    )(*[sums[nm] for nm in names])


def _total(parts, chip_core):
    steps = 2
    in_specs, out_specs, operands = [], [], []
    for name in BIG_NAMES:
        by_col = BIG[name][2]
        pr, pc = _piece_shape(name)
        tr = pr // steps
        if by_col:
            in_specs.append(pl.BlockSpec((tr, pc), lambda i, s: (i, s[0])))
            out_specs.append(pl.BlockSpec((tr, pc), lambda i, s: (s[1] * steps + i, 0)))
        else:
            in_specs.append(pl.BlockSpec((tr, pc), lambda i, s: (s[0] * steps + i, 0)))
            out_specs.append(pl.BlockSpec((tr, pc), lambda i, s: (i, s[1])))
        for r in range(3):
            in_specs.append(pl.BlockSpec((None, tr, pc), lambda i, s, r=r: (r, i, 0)))
        own, got = parts[name]
        operands += [own, got, got, got]

    def body(s_ref, *refs):
        for a in range(N_BIG):
            o_ref, a_ref, b_ref, c_ref = refs[4 * a:4 * a + 4]
            refs[4 * N_BIG + a][...] = (((o_ref[...] + a_ref[...].astype(F32)) + b_ref[...].astype(F32))
                                        + c_ref[...].astype(F32))

    totals = pl.pallas_call(
        body,
        grid_spec=pltpu.PrefetchScalarGridSpec(num_scalar_prefetch=1, grid=(steps,), in_specs=in_specs,
                                               out_specs=out_specs),
        out_shape=[jax.ShapeDtypeStruct(BIG[name][:2], F32) for name in BIG_NAMES],
        name="totals", compiler_params=_params("parallel"),
    )(chip_core, *operands)
    return dict(zip(BIG_NAMES, totals))


VEC_ROWS = 32
VEC_ROW = {"mix_norm_g": 0, "conv_b": 1, "b_rgate": 2, "b_igate": 3, "lru_lambda": 4, "rg_norm_g": 5,
           "hg_lower_bound": 6, "hg_norm_g": 8, "ffn_norm_g": 9, "final_norm_g": 10, "loss": 11,
           "conv_w": 12, "meta_tokens": 16}
N_DEV = 8


def _all_reduce_small(pieces, gates, totals):
    names = list(pieces)
    n_small = 10
    hv, hg = VEC_ROWS // 2, gates.shape[0] // 2

    def body(*refs):
        ins = refs[:len(names)]
        g_ref = refs[len(names)]
        vec_ref, gsum_ref = refs[len(names) + 1 + N_BIG:len(names) + 3 + N_BIG]
        big = refs[len(names) + 3 + N_BIG:len(names) + 3 + 2 * N_BIG]
        (mine_v, sib_v, sib_g, chip_v, chip_g, got_v, got_g, send_sems, recv_sems) = refs[len(names) + 3 + 2 * N_BIG:]
        x, y, c = _place()
        chip = 2 * x + y
        sibling = (x, y, 1 - c)
        share = []
        for a, name in enumerate(BIG_NAMES):
            half = _half_of(big[a], BIG[name][2], c)
            share.append(_remote(half, half, send_sems, recv_sems, n_small + a, sibling))
        mine_v[...] = jnp.zeros_like(mine_v)
        for name, ref in zip(names, ins):
            nr, w = ref.shape
            mine_v[VEC_ROW[name]:VEC_ROW[name] + nr, 0:w] = ref[...]

        swap = [_remote(mine_v, sib_v, send_sems, recv_sems, 0, sibling),
                _remote(g_ref, sib_g, send_sems, recv_sems, 1, sibling)]
        for cp in swap:
            cp.start()
        for cp in swap:
            cp.wait()
        for cp in share:
            cp.start()
        chip_v[...] = mine_v[...] + sib_v[...]
        chip_g[...] = g_ref[...] + sib_g[...]

        rows_v = pl.ds(pl.multiple_of(c * hv, 8), hv)
        rows_g = pl.ds(pl.multiple_of(c * hg, 8), hg)
        got_v[chip] = chip_v[rows_v, :]
        got_g[chip] = chip_g[rows_g, :].astype(BF16)
        sends = []
        for r in range(3):
            qx, qy = _chip_of(x, y, r)
            sends.append(_remote(chip_v.at[rows_v, :], got_v.at[chip], send_sems, recv_sems, 2 + r, (qx, qy, c)))
            sends.append(_remote(got_g.at[chip], got_g.at[chip], send_sems, recv_sems, 5 + r, (qx, qy, c)))
        for cp in sends:
            cp.start()
        for cp in sends:
            cp.wait()
        vec_ref[rows_v, :] = ((got_v[0] + got_v[1]) + got_v[2]) + got_v[3]
        gsum_ref[rows_g, :] = ((got_g[0].astype(F32) + got_g[1].astype(F32)) + got_g[2].astype(F32)
                               ) + got_g[3].astype(F32)

        back = [_remote(vec_ref.at[rows_v, :], vec_ref.at[rows_v, :], send_sems, recv_sems, 8, sibling),
                _remote(gsum_ref.at[rows_g, :], gsum_ref.at[rows_g, :], send_sems, recv_sems, 9, sibling)]
        for cp in back:
            cp.start()
        theirs_v = vec_ref.at[pl.ds(pl.multiple_of((1 - c) * hv, 8), hv), :]
        theirs_g = gsum_ref.at[pl.ds(pl.multiple_of((1 - c) * hg, 8), hg), :]
        _remote(theirs_v, theirs_v, send_sems, recv_sems, 8, sibling).wait_recv()
        _remote(theirs_g, theirs_g, send_sems, recv_sems, 9, sibling).wait_recv()
        for cp in back:
            cp.wait_send()
        for a, name in enumerate(BIG_NAMES):
            theirs = _half_of(big[a], BIG[name][2], 1 - c)
            _remote(theirs, theirs, send_sems, recv_sems, n_small + a, sibling).wait_recv()
        for cp in share:
            cp.wait_send()

    vmem = pl.BlockSpec(memory_space=pltpu.VMEM)
    n_sems = n_small + N_BIG
    out = pl.pallas_call(
        body, in_specs=[vmem] * (len(names) + 1) + [ANY] * N_BIG, out_specs=[vmem, vmem] + [ANY] * N_BIG,
        out_shape=([jax.ShapeDtypeStruct((VEC_ROWS, D_MODEL), F32), jax.ShapeDtypeStruct(gates.shape, F32)]
                   + [jax.ShapeDtypeStruct(BIG[n][:2], F32) for n in BIG_NAMES]),
        input_output_aliases={len(names) + 1 + a: 2 + a for a in range(N_BIG)},
        scratch_shapes=[pltpu.VMEM((VEC_ROWS, D_MODEL), F32), pltpu.VMEM((VEC_ROWS, D_MODEL), F32),
                        pltpu.VMEM(gates.shape, F32), pltpu.VMEM((VEC_ROWS, D_MODEL), F32),
                        pltpu.VMEM(gates.shape, F32), pltpu.VMEM((N_CHIPS, hv, D_MODEL), F32),
                        pltpu.VMEM((N_CHIPS, hg) + gates.shape[1:], BF16),
                        pltpu.SemaphoreType.DMA((n_sems,)), pltpu.SemaphoreType.DMA((n_sems,))],
        name="all_reduce_small",
    )(*[pieces[n] for n in names], gates, *[totals[n] for n in BIG_NAMES])
    return out[0], out[1], dict(zip(BIG_NAMES, out[2:]))


def _adamw_math(w, g, m, v):
    m = ADAM_B1 * m + (1.0 - ADAM_B1) * g
    v = ADAM_B2 * v + (1.0 - ADAM_B2) * (g * g)
    m_hat = m / (1.0 - ADAM_B1 ** ADAM_STEP)
    v_hat = v / (1.0 - ADAM_B2 ** ADAM_STEP)
    delta = -ADAM_LR * (m_hat / (jnp.sqrt(v_hat) + ADAM_EPS) + ADAM_WD * w)
    return delta, m, v


def _adamw_big(w, g, m, v):
    steps = 8
    ring = 3
    n_in = 4 * N_BIG
    blks, tiles = [], []
    for name in BIG_NAMES:
        rows, cols, _ = BIG[name]
        blks.append(pl.BlockSpec((rows // steps, cols), lambda i: (i, 0)))
        tiles.append((rows // steps, cols))

    def body(*refs):
        ins, outs, slots, sems = refs[:n_in], refs[n_in:2 * n_in], refs[2 * n_in:3 * n_in], refs[3 * n_in]
        i = pl.program_id(0)

        def fetch(j, s):
            tr = slots[j].shape[1]
            return pltpu.make_async_copy(ins[j].at[pl.ds(pl.multiple_of(s * tr, 8), tr), :], slots[j].at[s % ring],
                                         sems.at[j, s % ring])

        @pl.when(i == 0)
        def _():
            for s in range(min(ring - 1, steps)):
                for j in range(n_in):
                    fetch(j, s).start()

        @pl.when(i + ring - 1 < steps)
        def _():
            for j in range(n_in):
                fetch(j, i + ring - 1).start()

        for j in range(n_in):
            fetch(j, i).wait()
        slot = i % ring
        for a in range(N_BIG):
            w_, g, m_, v_ = (slots[k * N_BIG + a][slot] for k in range(4))
            d, nm, nv = _adamw_math(w_, g, m_, v_)
            outs[a][...] = g
            outs[N_BIG + a][...] = d
            outs[2 * N_BIG + a][...] = nm
            outs[3 * N_BIG + a][...] = nv

    shapes = [jax.ShapeDtypeStruct(BIG[name][:2], F32) for name in BIG_NAMES]
    out = pl.pallas_call(
        body, grid=(steps,), in_specs=[ANY] * n_in, out_specs=blks * 4, out_shape=shapes * 4,
        scratch_shapes=[pltpu.VMEM((ring,) + t, F32) for t in tiles] * 4 + [pltpu.SemaphoreType.DMA((n_in, ring))],
        name="adamw_big", compiler_params=_params("arbitrary"),
    )(*[t[name] for t in (w, g, m, v) for name in BIG_NAMES])
    return {name: tuple(out[k * N_BIG + a] for k in range(4)) for a, name in enumerate(BIG_NAMES)}


SMALL = {"meta_tokens": (N_META, D_MODEL // N_CHIPS), "mix_norm_g": (1, D_MODEL), "conv_w": (CONV_W, D_RG // N_CHIPS),
         "conv_b": (1, D_RG), "w_rgate": (D_RG, RG_HEAD_DIM), "b_rgate": (1, D_RG), "w_igate": (D_RG, RG_HEAD_DIM),
         "b_igate": (1, D_RG), "lru_lambda": (1, D_RG), "rg_norm_g": (1, D_RG), "hg_lower_bound": (2, D_HG),
         "hg_norm_g": (1, HG_HEAD_DIM), "ffn_norm_g": (1, D_MODEL), "final_norm_g": (1, D_MODEL)}
SMALL_NAMES = tuple(SMALL)
SHARDED_SMALL = ("meta_tokens", "conv_w")


def _adamw_small(vec, gates, w, m, v):
    n = len(SMALL_NAMES)

    def body(*refs):
        vec_ref, gates_ref = refs[:2]
        w_refs, m_refs, v_refs = refs[2:2 + n], refs[2 + n:2 + 2 * n], refs[2 + 2 * n:2 + 3 * n]
        outs = refs[2 + 3 * n:]
        loss_ref = outs[0]
        x, y, _ = _place()
        chip = 2 * x + y
        loss_ref[...] = vec_ref[VEC_ROW["loss"]:VEC_ROW["loss"] + 1, 0:1]

        def update(k, g):
            g_ref, d_ref, nm_ref, nv_ref = outs[1 + 4 * k:5 + 4 * k]
            g_ref[...] = g
            d_ref[...], nm_ref[...], nv_ref[...] = _adamw_math(w_refs[k][...], g, m_refs[k][...], v_refs[k][...])

        for k, name in enumerate(SMALL_NAMES):
            nr, w_ = SMALL[name]
            if name == "w_rgate":
                update(k, gates_ref[0:D_RG, :])
            elif name == "w_igate":
                update(k, gates_ref[D_RG:2 * D_RG, :])
            elif name in SHARDED_SMALL:
                r0 = VEC_ROW[name]
                for q in range(N_CHIPS):
                    @pl.when(chip == q)
                    def _(k=k, r0=r0, nr=nr, w_=w_, q=q):
                        update(k, vec_ref[r0:r0 + nr, q * w_:(q + 1) * w_])
            else:
                r0 = VEC_ROW[name]
                update(k, vec_ref[r0:r0 + nr, 0:w_])

    vmem = pl.BlockSpec(memory_space=pltpu.VMEM)
    out_shape = [jax.ShapeDtypeStruct((1, 1), F32)]
    for name in SMALL_NAMES:
        out_shape += [jax.ShapeDtypeStruct(SMALL[name], F32)] * 4
    outs = pl.pallas_call(
        body, in_specs=[vmem] * (2 + 3 * n), out_specs=[vmem] * len(out_shape), out_shape=out_shape,
        name="adamw_small",
    )(vec, gates, *[w[k] for k in SMALL_NAMES], *[m[k] for k in SMALL_NAMES], *[v[k] for k in SMALL_NAMES])
    loss = outs[0]
    res = {name: tuple(outs[1 + 4 * k:5 + 4 * k]) for k, name in enumerate(SMALL_NAMES)}
    return loss, res


WEIGHT_NAMES = ("meta_tokens", "mix_norm_g", "w_in", "conv_w", "conv_b", "w_rgate", "b_rgate", "w_igate", "b_igate",
                "lru_lambda", "rg_norm_g", "hg_lower_bound", "hg_norm_g", "w_out", "ffn_norm_g", "w_gate_up", "w_down",
                "final_norm_g")


def _to_2d(name, a):
    if name in BIG:
        return a.reshape(BIG[name][:2])
    return a.reshape(SMALL[name])


def kernel(x, meta_tokens, mix_norm_g, w_in, conv_w, conv_b, w_rgate, b_rgate, w_igate, b_igate, lru_lambda, rg_norm_g, hg_lower_bound, hg_norm_g, w_out, ffn_norm_g, w_gate_up, w_down, final_norm_g, loss_target, m_meta_tokens, m_mix_norm_g, m_w_in, m_conv_w, m_conv_b, m_w_rgate, m_b_rgate, m_w_igate, m_b_igate, m_lru_lambda, m_rg_norm_g, m_hg_lower_bound, m_hg_norm_g, m_w_out, m_ffn_norm_g, m_w_gate_up, m_w_down, m_final_norm_g, v_meta_tokens, v_mix_norm_g, v_w_in, v_conv_w, v_conv_b, v_w_rgate, v_b_rgate, v_w_igate, v_b_igate, v_lru_lambda, v_rg_norm_g, v_hg_lower_bound, v_hg_norm_g, v_w_out, v_ffn_norm_g, v_w_gate_up, v_w_down, v_final_norm_g):
    w_raw = dict(zip(WEIGHT_NAMES, (meta_tokens, mix_norm_g, w_in, conv_w, conv_b, w_rgate, b_rgate, w_igate, b_igate,
                                    lru_lambda, rg_norm_g, hg_lower_bound, hg_norm_g, w_out, ffn_norm_g, w_gate_up,
                                    w_down, final_norm_g)))
    m_raw = dict(zip(WEIGHT_NAMES, (m_meta_tokens, m_mix_norm_g, m_w_in, m_conv_w, m_conv_b, m_w_rgate, m_b_rgate,
                                    m_w_igate, m_b_igate, m_lru_lambda, m_rg_norm_g, m_hg_lower_bound, m_hg_norm_g,
                                    m_w_out, m_ffn_norm_g, m_w_gate_up, m_w_down, m_final_norm_g)))
    v_raw = dict(zip(WEIGHT_NAMES, (v_meta_tokens, v_mix_norm_g, v_w_in, v_conv_w, v_conv_b, v_w_rgate, v_b_rgate,
                                    v_w_igate, v_b_igate, v_lru_lambda, v_rg_norm_g, v_hg_lower_bound, v_hg_norm_g,
                                    v_w_out, v_ffn_norm_g, v_w_gate_up, v_w_down, v_final_norm_g)))
    w = {k: _to_2d(k, a) for k, a in w_raw.items()}
    m = {k: _to_2d(k, a) for k, a in m_raw.items()}
    v = {k: _to_2d(k, a) for k, a in v_raw.items()}

    x_i, y_i, c_i = _place()
    core = jnp.reshape(c_i, (1,)).astype(jnp.int32)
    chip = jnp.reshape(2 * x_i + y_i, (1,)).astype(jnp.int32)
    chip_core = jnp.concatenate([chip, core])

    first_names, rest_names = ("w_in",), ("w_out", "w_gate_up", "w_down")
    placed, _ = _place_shards(w, [], chip, first_names, "place_first")
    first, _ = _gather_weights(placed, [], first_names, "gather_first", 1)
    placed, (meta_full, cw_full) = _place_shards(w, [w["meta_tokens"], w["conv_w"]], chip, rest_names, "place_shards")
    rest, _ = _gather_weights(placed, [], rest_names, "gather_rest", 2)
    full = {**first, **rest}

    seq = x.shape[1]
    small ={k: w[k] for k in SMALL_NAMES if k not in SHARDED_SMALL}
    small["conv_w"] = cw_full

    def send_to_chips(sums, names, tag, collective_id):
        arrived = _send_chip_sums({n: sums[n][1] for n in names}, names, "send_chip_sums_" + tag, collective_id)
        return {n: (sums[n][0], a) for n, a in zip(names, arrived)}

    def reduce_to_chips(grads, names, send_names, tag, collective_ids):
        got = _exchange_halves({n: grads[n][1] for n in names}, names, "exchange_halves_" + tag, collective_ids[0])

        def chip_sums():
            return _chip_sum({n: grads[n][0] for n in names}, got, names, core, "chip_sum_" + tag)

        return chip_sums, lambda sums: send_to_chips(sums, send_names, tag, collective_ids[1])

    ffn_names, mixer_names = ("w_gate_up", "w_down", "w_out"), ("w_in",)
    loss, grad_x, grads, parts, parts_mixer = _local_step(
        x.reshape(seq, D_MODEL), meta_full, loss_target.reshape(seq, D_MODEL),
        w["w_in"], full["w_in"], full["w_out"], full["w_gate_up"], full["w_down"], small, chip,
        on_ffn_grads=lambda g: reduce_to_chips(g, ("w_down", "w_out"), ffn_names, "ffn", (3, 4)),
        on_mixer_grads=lambda sums: send_to_chips(sums, mixer_names, "mixer", 5))
    parts.update(parts_mixer)
    totals = _total(parts, chip_core)
    pieces = {k: grads[k] for k in VEC_ROW if k != "loss"}
    pieces["loss"] = loss
    vec, gates, g_big = _all_reduce_small(pieces, grads["w_gates"], totals)
    loss_sum, res = _adamw_small(vec, gates, w, m, v)
    res.update(_adamw_big(w, g_big, m, v))

    out = [loss_sum.reshape(()), grad_x.reshape(1, seq, D_MODEL)]
    for j in range(4):
        out += [res[n][j].reshape(w_raw[n].shape) for n in WEIGHT_NAMES]
    return tuple(out)
```

```python
import math

import jax
import jax.numpy as jnp
from jax import lax
from jax.experimental import pallas as pl
from jax.experimental.pallas import tpu as pltpu
from jax.experimental.pallas import tpu_sc as plsc

F32 = jnp.float32
BF16 = jnp.bfloat16
MESH = pl.DeviceIdType.MESH

D_MODEL = 1024
D_RG = 512
RG_HEAD_DIM = 64
D_HG = 512
HG_HEAD_DIM = 128
HG_HEADS = 4
CHUNK = 64
SUB = 16
N_SUB = CHUNK // SUB
N_META = 16
PAD = CHUNK - N_META
D_IN = 3072
D_FF = 2816
CONV_W = 4
LRU_C = 8.0
EPS = 1e-6
EXP_CLAMP = 80.0
GELU_C = math.sqrt(2.0 / math.pi)
GELU_A = 0.044715
N_CHIPS = 4

ADAM_LR = 0.001
ADAM_B1 = 0.9
ADAM_B2 = 0.999
ADAM_EPS = 1e-08
ADAM_WD = 0.01
ADAM_STEP = 10

VMEM_LIMIT = 56 * 1024 * 1024


def _params(*sem):
    return pltpu.CompilerParams(dimension_semantics=sem, vmem_limit_bytes=VMEM_LIMIT)


def _row_tile(rows, target):
    best = None
    for t in range(16, min(rows, target) + 1, 16):
        if rows % t == 0:
            best = t
    assert best is not None, rows
    return best


def _sigmoid(x):
    return 0.5 * jnp.tanh(0.5 * x) + 0.5


def _dot(a, b):
    return jnp.dot(a, b, preferred_element_type=F32)


def _dot_nt(a, b):
    return lax.dot_general(a, b, (((1,), (1,)), ((), ())), preferred_element_type=F32)


def _dot_tn(a, b):
    return lax.dot_general(a, b, (((0,), (0,)), ((), ())), preferred_element_type=F32)


def _rms(x):
    return lax.rsqrt(jnp.mean(x * x, axis=-1, keepdims=True) + EPS)


def _rms_bwd(dn, n, r):
    return r * (dn - n * jnp.mean(dn * n, axis=-1, keepdims=True))


def _gelu_parts(x):
    t = jnp.tanh(GELU_C * (x + GELU_A * x * x * x))
    g = 0.5 * x * (1.0 + t)
    dg = 0.5 * (1.0 + t) + 0.5 * x * (1.0 - t * t) * GELU_C * (1.0 + 3.0 * GELU_A * x * x)
    return g, dg


def _softplus_neg(lam):
    e = jnp.exp(-jnp.abs(lam))
    w = 1.0 + e
    log1p = jnp.where(w == 1.0, e, jnp.log(w) * e / (w - 1.0))
    return jnp.maximum(-lam, 0.0) + log1p


def _head_mask():
    r = lax.broadcasted_iota(jnp.int32, (D_RG, D_RG), 0) // RG_HEAD_DIM
    c = lax.broadcasted_iota(jnp.int32, (D_RG, D_RG), 1) // RG_HEAD_DIM
    return r == c


def _head_fold():
    r = lax.broadcasted_iota(jnp.int32, (D_RG, RG_HEAD_DIM), 0) % RG_HEAD_DIM
    c = lax.broadcasted_iota(jnp.int32, (D_RG, RG_HEAD_DIM), 1)
    return (r == c).astype(F32)


def _gate_weights(w_r, w_i):
    def body(wr_ref, wi_ref, o_ref):
        fold = _head_fold()
        mask = _head_mask()
        for k, ref in enumerate((wr_ref, wi_ref)):
            full = _dot_nt(ref[...].astype(BF16), fold.astype(BF16))
            o_ref[:, k * D_RG:(k + 1) * D_RG] = jnp.where(mask, full, 0.0).astype(BF16)

    return pl.pallas_call(
        body, out_shape=jax.ShapeDtypeStruct((D_RG, 2 * D_RG), BF16), name="gate_weights",
    )(w_r, w_i)


HEAD = PAD + N_META


def _window_copies(seq_hbm, buf, sems, tm):
    def first(to_vmem):
        seq, vm = seq_hbm.at[pl.ds(0, tm - HEAD)], buf.at[0, pl.ds(HEAD, tm - HEAD)]
        return pltpu.make_async_copy(seq, vm, sems.at[0]) if to_vmem else pltpu.make_async_copy(vm, seq, sems.at[0])

    def later(j, slot, to_vmem):
        seq, vm = seq_hbm.at[pl.ds(pl.multiple_of(j * tm - HEAD, 8), tm)], buf.at[slot]
        if to_vmem:
            return pltpu.make_async_copy(seq, vm, sems.at[slot])
        return pltpu.make_async_copy(vm, seq, sems.at[slot])

    return first, later


def _fetch_window(seq_hbm, buf, sems, i, n_steps, tm):
    first, later = _window_copies(seq_hbm, buf, sems, tm)
    slot = i % 2

    @pl.when(i == 0)
    def _():
        first(True).start()

    if n_steps > 1:
        @pl.when(i + 1 < n_steps)
        def _():
            later(i + 1, 1 - slot, True).start()

    @pl.when(i == 0)
    def _():
        first(True).wait()

    if n_steps > 1:
        @pl.when(i > 0)
        def _():
            later(i, slot, True).wait()

    return slot


def _in_proj_local(x, meta, g1, w_own, chip):
    T = x.shape[0] + HEAD
    tm = _row_tile(T, 832)
    n_steps = T // tm
    cols = BIG["w_in"][1]

    def body(s_ref, x_hbm, meta_ref, g_ref, w_ref, p_ref, u_ref, h_ref, buf, sems, wb):
        i = pl.program_id(0)
        slot = _fetch_window(x_hbm, buf, sems, i, n_steps, tm)

        @pl.when(i == 0)
        def _():
            buf[0, 0:PAD, :] = jnp.zeros((PAD, D_MODEL), F32)
            buf[0, PAD:HEAD, :] = meta_ref[...]
            wb[...] = w_ref[...].astype(BF16)

        h = buf[slot]
        h_ref[...] = h
        u = (h * _rms(h) * g_ref[...]).astype(BF16)
        u_ref[...] = u
        p_ref[...] = _dot(u, wb[...])

    return pl.pallas_call(
        body,
        grid_spec=pltpu.PrefetchScalarGridSpec(
            num_scalar_prefetch=1, grid=(n_steps,),
            in_specs=[pl.BlockSpec(memory_space=pl.ANY),
                      pl.BlockSpec((N_META, D_MODEL), lambda i, s: (0, 0)),
                      pl.BlockSpec((1, D_MODEL), lambda i, s: (0, 0)),
                      pl.BlockSpec((D_MODEL, cols), lambda i, s: (0, 0))],
            out_specs=[pl.BlockSpec((tm, cols), lambda i, s: (i, s[0])),
                       pl.BlockSpec((tm, D_MODEL), lambda i, s: (i, 0)),
                       pl.BlockSpec((tm, D_MODEL), lambda i, s: (i, 0))],
            scratch_shapes=[pltpu.VMEM((2, tm, D_MODEL), F32), pltpu.SemaphoreType.DMA((2,)),
                            pltpu.VMEM((D_MODEL, cols), BF16)]),
        out_shape=[jax.ShapeDtypeStruct((T, D_IN), F32), jax.ShapeDtypeStruct((T, D_MODEL), BF16),
                   jax.ShapeDtypeStruct((T, D_MODEL), F32)],
        name="in_proj_local", compiler_params=_params("arbitrary"),
    )(chip, x, meta, g1, w_own)


def _in_proj_rest(u, w_in, p, chip):
    T = u.shape[0]
    tm = _row_tile(T, 2080)
    cols = BIG["w_in"][1]
    block = lambda j, s: (s[0] + 1 + j) % N_CHIPS

    def body(s_ref, u_ref, w_ref, p_in_ref, p_ref):
        p_ref[...] = _dot(u_ref[...], w_ref[...])

    return pl.pallas_call(
        body,
        grid_spec=pltpu.PrefetchScalarGridSpec(
            num_scalar_prefetch=1, grid=(N_CHIPS - 1, T // tm),
            in_specs=[pl.BlockSpec((tm, D_MODEL), lambda j, i, s: (i, 0)),
                      pl.BlockSpec((D_MODEL, cols), lambda j, i, s: (0, block(j, s))), ANY],
            out_specs=pl.BlockSpec((tm, cols), lambda j, i, s: (i, block(j, s)))),
        out_shape=jax.ShapeDtypeStruct((T, D_IN), F32),
        input_output_aliases={3: 0},
        name="in_proj_rest", compiler_params=_params("arbitrary", "arbitrary"),
    )(chip, u, w_in, p)


def _scan_block_fwd(A, B, rowi):
    for d in (1, 2, 4):
        a_sh = pltpu.roll(A, d, axis=0)
        b_sh = pltpu.roll(B, d, axis=0)
        m = rowi >= d
        B = jnp.where(m, A * b_sh + B, B)
        A = jnp.where(m, A * a_sh, A)
    return A, B


def _scan_block_bwd(A, B, rowi):
    for d in (1, 2, 4):
        a_sh = pltpu.roll(A, 8 - d, axis=0)
        b_sh = pltpu.roll(B, 8 - d, axis=0)
        m = rowi < 8 - d
        B = jnp.where(m, A * b_sh + B, B)
        A = jnp.where(m, A * a_sh, A)
    return A, B


def _rg_gates(xc, w_ref, bg_ref, lam):
    pre = _dot(xc.astype(BF16), w_ref[...]) + bg_ref[...]
    r = _sigmoid(pre[:, :D_RG])
    ig = _sigmoid(pre[:, D_RG:])
    sp = _softplus_neg(lam)
    la = -LRU_C * sp * r
    a = jnp.exp(la)
    th = jnp.tanh(la)
    u = 1.0 - th
    rc = pl.reciprocal(u, approx=True)
    rc = rc * (2.0 - u * rc)
    rc = rc * (2.0 - u * rc)
    m2 = -2.0 * th * rc
    inv_m = lax.rsqrt(jnp.maximum(m2, 1e-30))
    return r, ig, sp, a, m2 * inv_m, inv_m


def _conv(ext, cw_ref, cb_ref, tm):
    xc = cb_ref[...] + cw_ref[0:1, :] * ext[8 - 3:8 - 3 + tm, :]
    for j in range(1, CONV_W):
        xc = xc + cw_ref[j:j + 1, :] * ext[8 - 3 + j:8 - 3 + j + tm, :]
    return xc


def _scan_unroll(blocks):
    return 4 if blocks % 4 == 0 else 2 if blocks % 2 == 0 else 1


def _rg_fwd(p, cw, cb, wg, bg, lam, rg_g):
    T = p.shape[0]
    tm = _row_tile(T, 832)
    unroll = _scan_unroll(tm // 8)

    def body(xg_ref, cw_ref, cb_ref, w_ref, bg_ref, lam_ref, g_ref, y_ref, h_ref, xc_ref, ext, a_s, b_s, carry):
        i = pl.program_id(0)

        @pl.when(i == 0)
        def _():
            ext[0:8, :] = jnp.zeros((8, D_RG), F32)
            carry[...] = jnp.zeros((1, D_RG), F32)

        ext[8:8 + tm, :] = xg_ref[:, :D_RG]
        xc = _conv(ext, cw_ref, cb_ref, tm)
        xc_ref[...] = xc
        r, ig, sp, a, m, _ = _rg_gates(xc, w_ref, bg_ref, lam_ref[...])
        row = i * tm + lax.broadcasted_iota(jnp.int32, (tm, 1), 0)
        a_s[...] = a
        b_s[...] = jnp.where(row >= PAD, m * ig * xc, 0.0)
        rowi = lax.broadcasted_iota(jnp.int32, (8, D_RG), 0)

        def blk(j, c):
            for u in range(unroll):
                o = pl.multiple_of((j * unroll + u) * 8, 8)
                A, B = _scan_block_fwd(a_s[pl.ds(o, 8), :], b_s[pl.ds(o, 8), :], rowi)
                h = B + A * c
                h_ref[pl.ds(o, 8), :] = h
                c = h[7:8, :]
            return c

        carry[...] = lax.fori_loop(0, tm // (8 * unroll), blk, carry[...])
        ext[0:8, :] = ext[tm:tm + 8, :]
        g, _ = _gelu_parts(xg_ref[:, D_RG:])
        yy = g * h_ref[...]
        y_ref[...] = (yy * _rms(yy) * g_ref[...]).astype(BF16)

    vec = lambda n: pl.BlockSpec((1, n), lambda i: (0, 0))
    return pl.pallas_call(
        body, grid=(T // tm,),
        in_specs=[pl.BlockSpec((tm, 2 * D_RG), lambda i: (i, 0)),
                  pl.BlockSpec((CONV_W, D_RG), lambda i: (0, 0)), vec(D_RG),
                  pl.BlockSpec((D_RG, 2 * D_RG), lambda i: (0, 0)), vec(2 * D_RG), vec(D_RG), vec(D_RG)],
        out_specs=[pl.BlockSpec((tm, D_RG), lambda i: (i, 0))] * 3,
        out_shape=[jax.ShapeDtypeStruct((T, D_RG), BF16), jax.ShapeDtypeStruct((T, D_RG), F32),
                   jax.ShapeDtypeStruct((T, D_RG), F32)],
        scratch_shapes=[pltpu.VMEM((tm + 8, D_RG), F32), pltpu.VMEM((tm, D_RG), F32),
                        pltpu.VMEM((tm, D_RG), F32), pltpu.VMEM((1, D_RG), F32)],
        name="rg_fwd", compiler_params=_params("arbitrary"),
    )(p, cw, cb, wg, bg, lam, rg_g)


def _running_sum(x, down):
    r = lax.broadcasted_iota(jnp.int32, (CHUNK, CHUNK), 0)
    c = lax.broadcasted_iota(jnp.int32, (CHUNK, CHUNK), 1)
    tri = ((c <= r) if down else (c >= r)).astype(BF16)
    hi = x.astype(BF16)
    rest = x - hi.astype(F32)
    mid = rest.astype(BF16)
    lo = (rest - mid.astype(F32)).astype(BF16)
    return (_dot(tri, hi) + _dot(tri, mid)) + _dot(tri, lo)


def _hg_gates(hq, hf, lbraw_ref, valid):
    lb = _sigmoid(lbraw_ref[0:1, :] - lbraw_ref[1:2, :])
    sq = _sigmoid(hq)
    q = hq * sq
    sf = _sigmoid(hf)
    f = lb + (1.0 - lb) * sf
    lf = jnp.where(valid, jnp.log(f), 0.0)
    b = _running_sum(lf, True)
    return lb, sq, q, sf, f, b


def _hg_head(qh, kh, bh):
    b_last = bh[CHUNK - 1:CHUNK, :]
    refs = [bh[SUB * s:SUB * s + 1, :] for s in range(N_SUB)]
    r_sel = jnp.concatenate([jnp.broadcast_to(refs[s], (SUB, HG_HEAD_DIM)) for s in range(N_SUB)], axis=0)
    eb = jnp.exp(bh)
    eq = jnp.exp(bh - r_sel)
    ekh = jnp.exp(b_last - bh)
    ek = [jnp.exp(jnp.minimum(refs[s] - bh[:SUB * (s + 1), :], EXP_CLAMP)) for s in range(N_SUB)]
    qe = qh * eq

    def own_rows(s):
        parts = [jnp.zeros((SUB * s, HG_HEAD_DIM), F32)] if s else []
        parts.append(qe[SUB * s:SUB * (s + 1), :])
        if s < N_SUB - 1:
            parts.append(jnp.zeros((CHUNK - SUB * (s + 1), HG_HEAD_DIM), F32))
        return jnp.concatenate(parts, axis=0)

    q_hat = jnp.concatenate([own_rows(s) for s in range(N_SUB)], axis=1)

    def met_rows(s):
        n = SUB * (s + 1)
        ke = kh[:n, :] * ek[s]
        return ke if n == CHUNK else jnp.concatenate([ke, jnp.zeros((CHUNK - n, HG_HEAD_DIM), F32)], axis=0)

    k_til = jnp.concatenate([met_rows(s) for s in range(N_SUB)], axis=1)
    return b_last, eb, eq, ekh, ek, q_hat, k_til


def _causal():
    r = lax.broadcasted_iota(jnp.int32, (CHUNK, CHUNK), 0)
    c = lax.broadcasted_iota(jnp.int32, (CHUNK, CHUNK), 1)
    return r >= c


def _chunks_per_step(n_chunks):
    for c in (5, 4, 3, 2):
        if n_chunks % c == 0:
            return c
    return 1


def _hg_fwd(p, lbraw, hg_g):
    T = p.shape[0]
    n_chunks = T // CHUNK
    cps = _chunks_per_step(n_chunks)
    rows = cps * CHUNK

    def body(hq_ref, hf_ref, hi_ref, hg_ref, lb_ref, g_ref, y_ref, o_ref, st_all_ref, st):
        i = pl.program_id(0)

        @pl.when(i == 0)
        def _():
            st[...] = jnp.zeros_like(st)

        def chunk(j, carry):
            rs = pl.ds(pl.multiple_of(j * CHUNK, CHUNK), CHUNK)
            chunk_body(i * cps + j, hq_ref.at[rs, :], hf_ref.at[rs, :], hi_ref.at[rs, :], hg_ref.at[rs, :], lb_ref,
                       g_ref, y_ref.at[rs, :], o_ref.at[rs, :], st_all_ref.at[pl.ds(j, 1)], st)
            return carry

        lax.fori_loop(0, cps, chunk, 0, unroll=True)

    def chunk_body(n, hq_ref, hf_ref, hi_ref, hg_ref, lb_ref, g_ref, y_ref, o_ref, st_all_ref, st):
        valid = (n * CHUNK + lax.broadcasted_iota(jnp.int32, (CHUNK, 1), 0)) >= PAD
        hq, hf, v, hg = hq_ref[...], hf_ref[...], hi_ref[...], hg_ref[...]
        lb, sq, q, sf, f, b = _hg_gates(hq, hf, lb_ref, valid)
        k = 1.0 - f
        st_all_ref[0] = st[...]
        causal = _causal()
        v_t = v.T.astype(BF16)
        heads = [slice(h * HG_HEAD_DIM, (h + 1) * HG_HEAD_DIM) for h in range(HG_HEADS)]
        fac = []
        for sl in heads:
            qh, kh, bh = q[:, sl], k[:, sl], b[:, sl]
            b_last, eb, _, ekh, _, q_hat, k_til = _hg_head(qh, kh, bh)
            fac.append((jnp.exp(b_last), (qh * eb).astype(BF16), q_hat.astype(BF16), k_til.astype(BF16),
                        (kh * ekh).astype(BF16), v[:, sl].astype(BF16)))
        raw = []
        for sl, (_, q_til, q_hat, k_til, k_hat, _) in zip(heads, fac):
            st_h = st[sl, :]
            raw.append((_dot_nt(q_til, st_h.astype(BF16)), _dot_nt(q_hat, k_til), _dot(v_t[sl, :], k_hat), st_h))
        for sl, (e_last, _, _, _, _, vb), (inter, att, upd, st_h) in zip(heads, fac, raw):
            o = inter + _dot(jnp.where(causal, att, 0.0).astype(BF16), vb)
            st[sl, :] = st_h * e_last + upd
            o_ref[:, sl] = o
            hgh = hg[:, sl]
            y_ref[:, sl] = (o * _rms(o) * g_ref[...] * (hgh * _sigmoid(hgh))).astype(BF16)

    col = lambda j: pl.BlockSpec((rows, D_HG), lambda n: (n, j))
    return pl.pallas_call(
        body, grid=(n_chunks // cps,),
        in_specs=[col(2), col(3), col(4), col(5),
                  pl.BlockSpec((2, D_HG), lambda n: (0, 0)), pl.BlockSpec((1, HG_HEAD_DIM), lambda n: (0, 0))],
        out_specs=[pl.BlockSpec((rows, D_HG), lambda n: (n, 0)), pl.BlockSpec((rows, D_HG), lambda n: (n, 0)),
                   pl.BlockSpec((cps, D_HG, HG_HEAD_DIM), lambda n: (n, 0, 0))],
        out_shape=[jax.ShapeDtypeStruct((T, D_HG), BF16), jax.ShapeDtypeStruct((T, D_HG), F32),
                   jax.ShapeDtypeStruct((n_chunks, D_HG, HG_HEAD_DIM), F32)],
        scratch_shapes=[pltpu.VMEM((D_HG, HG_HEAD_DIM), F32)],
        name="hg_fwd", compiler_params=_params("arbitrary"),
    )(p, p, p, p, lbraw, hg_g)


def _ffn_fwd(h0, y_rg, y_hg, w_out, g2, w_gu, w_down, gf, target):
    T = h0.shape[0]
    tm = _row_tile(T, 320)
    n_steps = T // tm

    def body(h_ref, yr_ref, yh_ref, wo_ref, g2_ref, wgu_ref, wd_ref, gf_ref, t_hbm,
             h1_ref, v_ref, y_ref, gu_ref, act_ref, dh2_ref, dh2b_ref, loss_ref, gg_ref, tbuf, sems):
        i = pl.program_id(0)
        slot = _fetch_window(t_hbm, tbuf, sems, i, n_steps, tm)

        @pl.when(i == 0)
        def _():
            loss_ref[...] = jnp.zeros_like(loss_ref)
            gg_ref[...] = jnp.zeros_like(gg_ref)
            tbuf[0, 0:HEAD, :] = jnp.zeros((HEAD, D_MODEL), F32)

        y_ref[:, :D_RG] = yr_ref[...]
        y_ref[:, D_RG:] = yh_ref[...]
        h1 = h_ref[...] + _dot(y_ref[...], wo_ref[...])
        h1_ref[...] = h1
        v = (h1 * _rms(h1) * g2_ref[...]).astype(BF16)
        v_ref[...] = v

        gu = _dot(v, wgu_ref[...])
        gu_ref[...] = gu.astype(BF16)
        g = gu[:, :D_FF]
        act = (g * _sigmoid(g) * gu[:, D_FF:]).astype(BF16)
        act_ref[...] = act

        h2 = h1 + _dot(act, wd_ref[...])
        r = _rms(h2)
        n = h2 * r
        gf_ = gf_ref[...]
        row = i * tm + lax.broadcasted_iota(jnp.int32, (tm, 1), 0)
        err = jnp.where(row >= HEAD, n * gf_ - tbuf[slot], 0.0)
        loss_ref[...] += 0.5 * jnp.sum(jnp.mean(err * err, axis=-1, keepdims=True), axis=0, keepdims=True)
        dy = err * (1.0 / D_MODEL)
        gg_ref[...] += jnp.sum(dy * n, axis=0, keepdims=True)
        dh2 = _rms_bwd(dy * gf_, n, r)
        dh2_ref[...] = dh2
        dh2b_ref[...] = dh2.astype(BF16)

    row_spec = lambda n: pl.BlockSpec((tm, n), lambda i: (i, 0))
    vec = pl.BlockSpec((1, D_MODEL), lambda i: (0, 0))
    return pl.pallas_call(
        body, grid=(n_steps,),
        in_specs=[row_spec(D_MODEL), row_spec(D_RG), row_spec(D_HG), _resident((D_MODEL, D_MODEL)), vec,
                  _resident((D_MODEL, 2 * D_FF)), _resident((D_FF, D_MODEL)), vec,
                  pl.BlockSpec(memory_space=pl.ANY)],
        out_specs=[row_spec(D_MODEL), row_spec(D_MODEL), row_spec(D_MODEL), row_spec(2 * D_FF), row_spec(D_FF),
                   row_spec(D_MODEL), row_spec(D_MODEL), pl.BlockSpec((1, 1), lambda i: (0, 0)), vec],
        out_shape=[jax.ShapeDtypeStruct((T, D_MODEL), F32), jax.ShapeDtypeStruct((T, D_MODEL), BF16),
                   jax.ShapeDtypeStruct((T, D_MODEL), BF16), jax.ShapeDtypeStruct((T, 2 * D_FF), BF16),
                   jax.ShapeDtypeStruct((T, D_FF), BF16), jax.ShapeDtypeStruct((T, D_MODEL), F32),
                   jax.ShapeDtypeStruct((T, D_MODEL), BF16), jax.ShapeDtypeStruct((1, 1), F32),
                   jax.ShapeDtypeStruct((1, D_MODEL), F32)],
        scratch_shapes=[pltpu.VMEM((2, tm, D_MODEL), F32), pltpu.SemaphoreType.DMA((2,))],
        name="ffn_fwd", compiler_params=_params("arbitrary"),
    )(h0, y_rg, y_hg, w_out, g2, w_gu, w_down, gf, target)


def _resident(shape):
    return pl.BlockSpec(shape, lambda i: (0,) * len(shape), pipeline_mode=pl.Buffered(1))


def _ffn_bwd(dh2b, gu, w_down, w_gu, h1, g2, dh2, w_out):
    T = h1.shape[0]
    tm = _row_tile(T, 320)

    def body(d_ref, gu_ref, wd_ref, wgu_ref, h_ref, g_ref, d2_ref, wo_ref, dgu_ref, dh1_ref, dh1b_ref, dy_ref, gg_ref):
        i = pl.program_id(0)

        @pl.when(i == 0)
        def _():
            gg_ref[...] = jnp.zeros_like(gg_ref)

        dact = _dot_nt(d_ref[...], wd_ref[...]).astype(BF16)
        g = gu_ref[:, :D_FF]
        u = gu_ref[:, D_FF:]
        s = _sigmoid(g)
        dgu_ref[:, :D_FF] = dact * u * (s * (1.0 + g * (1.0 - s)))
        dgu_ref[:, D_FF:] = dact * (g * s)

        dv = _dot_nt(dgu_ref[...], wgu_ref[...])
        h1_ = h_ref[...]
        r = _rms(h1_)
        n = h1_ * r
        gg_ref[...] += jnp.sum(dv * n, axis=0, keepdims=True)
        dh1 = d2_ref[...] + _rms_bwd(dv * g_ref[...], n, r)
        dh1_ref[...] = dh1
        db = dh1.astype(BF16)
        dh1b_ref[...] = db
        dy_ref[...] = _dot_nt(db, wo_ref[...])

    row = lambda n: pl.BlockSpec((tm, n), lambda i: (i, 0))
    return pl.pallas_call(
        body, grid=(T // tm,),
        in_specs=[row(D_MODEL), row(2 * D_FF), _resident((D_FF, D_MODEL)), _resident((D_MODEL, 2 * D_FF)),
                  row(D_MODEL), pl.BlockSpec((1, D_MODEL), lambda i: (0, 0)), row(D_MODEL),
                  _resident((D_MODEL, D_MODEL))],
        out_specs=[row(2 * D_FF), row(D_MODEL), row(D_MODEL), row(D_MODEL),
                   pl.BlockSpec((1, D_MODEL), lambda i: (0, 0))],
        out_shape=[jax.ShapeDtypeStruct((T, 2 * D_FF), BF16), jax.ShapeDtypeStruct((T, D_MODEL), F32),
                   jax.ShapeDtypeStruct((T, D_MODEL), BF16), jax.ShapeDtypeStruct((T, D_MODEL), F32),
                   jax.ShapeDtypeStruct((1, D_MODEL), F32)],
        name="ffn_bwd", compiler_params=_params("arbitrary"),
    )(dh2b, gu, w_down, w_gu, h1, g2, dh2, w_out)


def _rg_bwd(p, xc_all, hs, dy, dp, cw, cb, wg, bg, lam, rg_g):
    T = p.shape[0]
    tm = _row_tile(T, 832)
    nt = T // tm
    hb = tm // 8
    unroll = _scan_unroll(hb)

    def body(xg_ref, xc_ref, h_ref, hh_ref, dy_ref, dp_in_ref, cw_ref, cb_ref, w_ref, bg_ref, lam_ref, g_ref,
             dp_ref, gcw_ref, gcb_ref, gw_ref, gbg_ref, glam_ref, gg_ref,
             dext, a_s, b_s, d_s, gacc, carry_d, carry_a):
        i = pl.program_id(0)
        t_idx = nt - 1 - i

        @pl.when(i == 0)
        def _():
            dext[tm:tm + 8, :] = jnp.zeros((8, D_RG), F32)
            carry_d[...] = jnp.zeros_like(carry_d)
            carry_a[...] = jnp.zeros_like(carry_a)
            gacc[...] = jnp.zeros_like(gacc)
            for ref in (gcw_ref, gcb_ref, gbg_ref, glam_ref, gg_ref, gw_ref):
                ref[...] = jnp.zeros_like(ref)

        first = t_idx == 0
        xc = xc_ref[...]
        lam_ = lam_ref[...]
        r, ig, sp, a, m, inv_m = _rg_gates(xc, w_ref, bg_ref, lam_)
        row = t_idx * tm + lax.broadcasted_iota(jnp.int32, (tm, 1), 0)
        valid = row >= PAD

        gr = xg_ref[:, D_RG:]
        g, dgelu = _gelu_parts(gr)
        h = h_ref[...]
        yy = g * h
        rr = _rms(yy)
        nn = yy * rr
        dy_ = dy_ref[...]
        gg_ref[...] += jnp.sum(dy_ * nn, axis=0, keepdims=True)
        dyy = _rms_bwd(dy_ * g_ref[...], nn, rr)
        dp_ref[:, D_RG:] = (dyy * h * dgelu).astype(BF16)

        a_s[...] = a
        b_s[...] = dyy * g
        rowi = lax.broadcasted_iota(jnp.int32, (8, D_RG), 0)

        def blk(jj, c):
            cd, ca = c
            for u in range(unroll):
                o = pl.multiple_of((hb - 1 - (jj * unroll + u)) * 8, 8)
                a_blk = a_s[pl.ds(o, 8), :]
                a_next = jnp.where(rowi == 7, ca, pltpu.roll(a_blk, 7, axis=0))
                A, B = _scan_block_bwd(a_next, b_s[pl.ds(o, 8), :], rowi)
                d = B + A * cd
                d_s[pl.ds(o, 8), :] = d
                cd, ca = d[0:1, :], a_blk[0:1, :]
            return cd, ca

        cd, ca = lax.fori_loop(0, hb // unroll, blk, (carry_d[...], carry_a[...]))
        carry_d[...] = cd
        carry_a[...] = ca
        delta = d_s[...]

        h_last_prev = jnp.where(first, 0.0, hh_ref[7:8, :])
        row0 = lax.broadcasted_iota(jnp.int32, (tm, 1), 0) == 0
        h_prev = jnp.where(row0, h_last_prev, pltpu.roll(h, 1, axis=0))
        dbx = jnp.where(valid, delta, 0.0)
        da = delta * h_prev
        di = dbx * m * xc
        dm = dbx * ig * xc
        dla = a * (da - dm * a * inv_m)
        dla = jnp.where(valid, dla, 0.0)
        glam_ref[...] += jnp.sum(dla * r, axis=0, keepdims=True) * (LRU_C / (1.0 + jnp.exp(lam_)))
        dr = (-LRU_C) * sp * dla
        dpre = jnp.concatenate([dr * r * (1.0 - r), di * ig * (1.0 - ig)], axis=1)
        gbg_ref[...] += jnp.sum(dpre, axis=0, keepdims=True)
        dpre_b = dpre.astype(BF16)
        gacc[...] += _dot_tn(xc.astype(BF16), dpre_b)
        dxc = dbx * m * ig + _dot_nt(dpre_b, w_ref[...])
        gcb_ref[...] += jnp.sum(dxc, axis=0, keepdims=True)
        dext[0:tm, :] = dxc
        xr = xg_ref[:, :D_RG]
        dxr = None
        for j in range(CONV_W):
            shifted = dext[3 - j:3 - j + tm, :]
            gcw_ref[j:j + 1, :] += jnp.sum(xr * shifted, axis=0, keepdims=True)
            tap = cw_ref[j:j + 1, :] * shifted
            dxr = tap if dxr is None else dxr + tap
        dp_ref[:, :D_RG] = dxr.astype(BF16)
        dext[tm:tm + 8, :] = dext[0:8, :]

        @pl.when(i == nt - 1)
        def _():
            fold = _head_fold()
            mask = _head_mask()
            fold_b = fold.astype(BF16)
            for k in range(2):
                blockdiag = jnp.where(mask, gacc[:, k * D_RG:(k + 1) * D_RG], 0.0)
                hi = blockdiag.astype(BF16)
                rest = blockdiag - hi.astype(F32)
                mid = rest.astype(BF16)
                lo = (rest - mid.astype(F32)).astype(BF16)
                gw_ref[k * D_RG:(k + 1) * D_RG, :] = (_dot(hi, fold_b) + _dot(mid, fold_b)) + _dot(lo, fold_b)

    vec = lambda n: pl.BlockSpec((1, n), lambda i: (0, 0))
    rev = lambda n: pl.BlockSpec((tm, n), lambda i: (nt - 1 - i, 0))
    halo = lambda n: pl.BlockSpec((8, n), lambda i: (jnp.maximum((nt - 1 - i) * hb - 1, 0), 0))
    return pl.pallas_call(
        body, grid=(nt,),
        in_specs=[rev(2 * D_RG), rev(D_RG), rev(D_RG), halo(D_RG), rev(D_RG), ANY,
                  pl.BlockSpec((CONV_W, D_RG), lambda i: (0, 0)), vec(D_RG),
                  pl.BlockSpec((D_RG, 2 * D_RG), lambda i: (0, 0)), vec(2 * D_RG), vec(D_RG), vec(D_RG)],
        out_specs=[rev(2 * D_RG), pl.BlockSpec((CONV_W, D_RG), lambda i: (0, 0)), vec(D_RG),
                   pl.BlockSpec((2 * D_RG, RG_HEAD_DIM), lambda i: (0, 0)), vec(2 * D_RG), vec(D_RG), vec(D_RG)],
        input_output_aliases={5: 0},
        out_shape=[jax.ShapeDtypeStruct((T, D_IN), BF16), jax.ShapeDtypeStruct((CONV_W, D_RG), F32),
                   jax.ShapeDtypeStruct((1, D_RG), F32), jax.ShapeDtypeStruct((2 * D_RG, RG_HEAD_DIM), F32),
                   jax.ShapeDtypeStruct((1, 2 * D_RG), F32), jax.ShapeDtypeStruct((1, D_RG), F32),
                   jax.ShapeDtypeStruct((1, D_RG), F32)],
        scratch_shapes=[pltpu.VMEM((tm + 8, D_RG), F32),
                        pltpu.VMEM((tm, D_RG), F32), pltpu.VMEM((tm, D_RG), F32), pltpu.VMEM((tm, D_RG), F32),
                        pltpu.VMEM((D_RG, 2 * D_RG), F32), pltpu.VMEM((1, D_RG), F32), pltpu.VMEM((1, D_RG), F32)],
        name="rg_bwd", compiler_params=_params("arbitrary"),
    )(p, xc_all, hs, hs, dy, dp, cw, cb, wg, bg, lam, rg_g)


def _hg_bwd(p, o_all, st_all, dy, lbraw, hg_g):
    T = p.shape[0]
    n_chunks = T // CHUNK
    cps = _chunks_per_step(n_chunks)
    rows = cps * CHUNK
    n_steps = n_chunks // cps

    def body(hq_ref, hf_ref, hi_ref, hg_ref, o_ref, st_ref, dy_ref, lb_ref, g_ref,
             dp_ref, glb_ref, gg_ref, dst):
        i = pl.program_id(0)

        @pl.when(i == 0)
        def _():
            dst[...] = jnp.zeros_like(dst)
            glb_ref[...] = jnp.zeros_like(glb_ref)
            gg_ref[...] = jnp.zeros_like(gg_ref)

        dp_ref[:, :2 * D_RG] = jnp.zeros((rows, 2 * D_RG), BF16)

        def chunk(jj, carry):
            j = cps - 1 - jj
            rs = pl.ds(pl.multiple_of(j * CHUNK, CHUNK), CHUNK)
            chunk_body((n_steps - 1 - i) * cps + j, hq_ref.at[rs, :], hf_ref.at[rs, :], hi_ref.at[rs, :],
                       hg_ref.at[rs, :], o_ref.at[rs, :], st_ref.at[pl.ds(j, 1)], dy_ref.at[rs, :], lb_ref, g_ref,
                       dp_ref.at[rs, pl.ds(2 * D_RG, 4 * D_HG)], glb_ref, gg_ref, dst)
            return carry

        lax.fori_loop(0, cps, chunk, 0, unroll=True)

    def chunk_body(n, hq_ref, hf_ref, hi_ref, hg_ref, o_ref, st_ref, dy_ref, lb_ref, g_ref,
                   dp_ref, glb_ref, gg_ref, dst):
        valid = (n * CHUNK + lax.broadcasted_iota(jnp.int32, (CHUNK, 1), 0)) >= PAD
        hq, hf, v, hg = hq_ref[...], hf_ref[...], hi_ref[...], hg_ref[...]
        lb, sq, q, sf, f, b = _hg_gates(hq, hf, lb_ref, valid)
        k = 1.0 - f
        causal = _causal()
        r_i = lax.broadcasted_iota(jnp.int32, (CHUNK, CHUNK), 0)
        c_i = lax.broadcasted_iota(jnp.int32, (CHUNK, CHUNK), 1)
        causal_t = r_i <= c_i
        is_last = lax.broadcasted_iota(jnp.int32, (CHUNK, 1), 0) == CHUNK - 1
        g_ = g_ref[...]
        db_parts, dq_parts, dk_parts = [], [], []
        gg = jnp.zeros((1, HG_HEAD_DIM), F32)
        heads = [slice(h * HG_HEAD_DIM, (h + 1) * HG_HEAD_DIM) for h in range(HG_HEADS)]

        do_parts = []
        for h, sl in enumerate(heads):
            o = o_ref[:, sl]
            ro = _rms(o)
            no = o * ro
            hgh = hg[:, sl]
            sg = _sigmoid(hgh)
            dyh = dy_ref[:, sl]
            dp_ref[:, 3 * D_HG + h * HG_HEAD_DIM:3 * D_HG + (h + 1) * HG_HEAD_DIM] = (
                dyh * no * g_ * sg * (1.0 + hgh * (1.0 - sg))).astype(BF16)
            dng = dyh * hgh * sg
            gg = gg + jnp.sum(dng * no, axis=0, keepdims=True)
            do_parts.append(_rms_bwd(dng * g_, no, ro))
        do_t = jnp.concatenate(do_parts, axis=1).T.astype(BF16)

        fac = []
        for sl, do in zip(heads, do_parts):
            qh, kh, bh = q[:, sl], k[:, sl], b[:, sl]
            b_last, eb, eq, ekh, ek, q_hat, k_til = _hg_head(qh, kh, bh)
            fac.append(dict(qh=qh, kh=kh, e_last=jnp.exp(b_last), eb=eb, eq=eq, ekh=ekh, ek=ek,
                            q_til=qh * eb, k_hat=kh * ekh, qhb=q_hat.astype(BF16), ktb=k_til.astype(BF16),
                            vb=v[:, sl].astype(BF16), dob=do.astype(BF16)))

        first = []
        for sl, t in zip(heads, fac):
            st_h = st_ref[0, sl, :]
            dst_h = dst[sl, :]
            dstb = dst_h.astype(BF16)
            first.append(dict(
                att_t=_dot_nt(t["ktb"], t["qhb"]), datt=_dot_nt(t["dob"], t["vb"]),
                datt_t=_dot_nt(t["vb"], t["dob"]), dk_hat=_dot(t["vb"], dstb),
                dv=_dot_nt(t["k_hat"].astype(BF16), dstb), dq_til=_dot(t["dob"], st_h.astype(BF16)),
                state=t["e_last"] * jnp.sum(dst_h * st_h, axis=0, keepdims=True)))
            dst[sl, :] = dst_h * t["e_last"] + _dot(do_t[sl, :], t["q_til"].astype(BF16))

        for h, (t, m) in enumerate(zip(fac, first)):
            qh, kh, eb, eq, ekh, ek = t["qh"], t["kh"], t["eb"], t["eq"], t["ekh"], t["ek"]
            q_til, k_hat, qhb, ktb, dob = t["q_til"], t["k_hat"], t["qhb"], t["ktb"], t["dob"]
            dk_hat, dq_til = m["dk_hat"], m["dq_til"]
            dv = m["dv"] + _dot(jnp.where(causal_t, m["att_t"], 0.0).astype(BF16), dob)
            dq_hat = _dot(jnp.where(causal, m["datt"], 0.0).astype(BF16), ktb)
            dk_til = _dot(jnp.where(causal_t, m["datt_t"], 0.0).astype(BF16), qhb)
            db_last = jnp.sum(dk_hat * k_hat, axis=0, keepdims=True) + m["state"]
            dq_sel = jnp.concatenate([dq_hat[SUB * s:SUB * (s + 1), s * HG_HEAD_DIM:(s + 1) * HG_HEAD_DIM]
                                      for s in range(N_SUB)], axis=0)
            dq_a = dq_sel * eq
            dk_rows, k_att_rows = [], []
            for b_ in range(N_SUB):
                rs = slice(SUB * b_, SUB * (b_ + 1))
                dk_sum = k_att_sum = None
                for s in range(b_, N_SUB):
                    cs = slice(s * HG_HEAD_DIM, (s + 1) * HG_HEAD_DIM)
                    d = dk_til[rs, cs]
                    t_dk = d * ek[s][rs, :]
                    t_att = ktb[rs, cs].astype(F32) * d
                    dk_sum = t_dk if dk_sum is None else dk_sum + t_dk
                    k_att_sum = t_att if k_att_sum is None else k_att_sum + t_att
                dk_rows.append(dk_sum)
                k_att_rows.append(k_att_sum)
            dk_a = jnp.concatenate(dk_rows, axis=0)
            db = (dq_til * q_til - dk_hat * k_hat + (qh * eq).astype(BF16).astype(F32) * dq_sel
                  - jnp.concatenate(k_att_rows, axis=0))
            db_parts.append(jnp.where(is_last, db + db_last, db))
            dq_parts.append(dq_til * eb + dq_a)
            dk_parts.append(dk_hat * ekh + dk_a)
            dp_ref[:, 2 * D_HG + h * HG_HEAD_DIM:2 * D_HG + (h + 1) * HG_HEAD_DIM] = dv.astype(BF16)

        gg_ref[...] += gg
        db = jnp.concatenate(db_parts, axis=1)
        dq = jnp.concatenate(dq_parts, axis=1)
        dk = jnp.concatenate(dk_parts, axis=1)
        dlf = jnp.where(valid, _running_sum(db, False), 0.0)
        dp_ref[:, :D_HG] = (dq * sq * (1.0 + hq * (1.0 - sq))).astype(BF16)
        df = dlf / f - dk
        dlb = jnp.sum(df * (1.0 - sf), axis=0, keepdims=True) * lb * (1.0 - lb)
        glb_ref[0:1, :] += dlb
        glb_ref[1:2, :] += -dlb
        dp_ref[:, D_HG:2 * D_HG] = (df * (1.0 - lb) * sf * (1.0 - sf)).astype(BF16)

    rev = lambda j: pl.BlockSpec((rows, D_HG), lambda i: (n_steps - 1 - i, j))
    return pl.pallas_call(
        body, grid=(n_steps,),
        in_specs=[rev(2), rev(3), rev(4), rev(5), rev(0),
                  pl.BlockSpec((cps, D_HG, HG_HEAD_DIM), lambda i: (n_steps - 1 - i, 0, 0)), rev(1),
                  pl.BlockSpec((2, D_HG), lambda i: (0, 0)), pl.BlockSpec((1, HG_HEAD_DIM), lambda i: (0, 0))],
        out_specs=[pl.BlockSpec((rows, D_IN), lambda i: (n_steps - 1 - i, 0)),
                   pl.BlockSpec((2, D_HG), lambda i: (0, 0)), pl.BlockSpec((1, HG_HEAD_DIM), lambda i: (0, 0))],
        out_shape=[jax.ShapeDtypeStruct((T, D_IN), BF16), jax.ShapeDtypeStruct((2, D_HG), F32),
                   jax.ShapeDtypeStruct((1, HG_HEAD_DIM), F32)],
        scratch_shapes=[pltpu.VMEM((D_HG, HG_HEAD_DIM), F32)],
        name="hg_bwd", compiler_params=_params("arbitrary"),
    )(p, p, p, p, o_all, st_all, dy, lbraw, hg_g)


def _in_bwd(dp, w_in, h0, g1, dh1):
    T = h0.shape[0]
    tm = _row_tile(T, 832)
    n_steps = T // tm

    def body(dp_ref, w_ref, h_ref, g_ref, d1_ref, gx_hbm, gmeta_ref, gg_ref, buf, sems):
        i = pl.program_id(0)
        first, later = _window_copies(gx_hbm, buf, sems, tm)
        slot = i % 2

        @pl.when(i == 0)
        def _():
            gg_ref[...] = jnp.zeros_like(gg_ref)

        if n_steps > 2:
            @pl.when(i == 2)
            def _():
                first(False).wait()

            @pl.when(i > 2)
            def _():
                later(i - 2, slot, False).wait()

        du = _dot_nt(dp_ref[...], w_ref[...])
        h0_ = h_ref[...]
        r = _rms(h0_)
        n = h0_ * r
        gg_ref[...] += jnp.sum(du * n, axis=0, keepdims=True)
        dh0 = d1_ref[...] + _rms_bwd(du * g_ref[...], n, r)
        buf[slot] = dh0

        @pl.when(i == 0)
        def _():
            gmeta_ref[...] = dh0[PAD:HEAD, :]
            first(False).start()

        if n_steps > 1:
            @pl.when(i > 0)
            def _():
                later(i, slot, False).start()

        @pl.when(i == n_steps - 1)
        def _():
            if n_steps == 1:
                first(False).wait()
            else:
                if n_steps == 2:
                    first(False).wait()
                else:
                    later(i - 1, 1 - slot, False).wait()
                later(i, slot, False).wait()

    row = lambda n: pl.BlockSpec((tm, n), lambda i: (i, 0))
    return pl.pallas_call(
        body, grid=(n_steps,),
        in_specs=[row(D_IN), _resident((D_MODEL, D_IN)),
                  row(D_MODEL), pl.BlockSpec((1, D_MODEL), lambda i: (0, 0)), row(D_MODEL)],
        out_specs=[pl.BlockSpec(memory_space=pl.ANY), pl.BlockSpec((N_META, D_MODEL), lambda i: (0, 0)),
                   pl.BlockSpec((1, D_MODEL), lambda i: (0, 0))],
        out_shape=[jax.ShapeDtypeStruct((T - HEAD, D_MODEL), F32), jax.ShapeDtypeStruct((N_META, D_MODEL), F32),
                   jax.ShapeDtypeStruct((1, D_MODEL), F32)],
        scratch_shapes=[pltpu.VMEM((2, tm, D_MODEL), F32), pltpu.SemaphoreType.DMA((2,))],
        name="in_bwd", compiler_params=_params("arbitrary"),
    )(dp, w_in, h0, g1, dh1)


def _col_tile(cols, target):
    best = None
    for t in range(128, min(cols, target) + 1, 128):
        if cols % t == 0:
            best = t
    assert best is not None, cols
    return best


MXU_DIM = 256


def _mxu_tile(cols, target):
    best = None
    for t in range(MXU_DIM, min(cols, target) + 1, MXU_DIM):
        if cols % t == 0:
            best = t
    assert best is not None, cols
    return best


def _weight_grad(a, b, name):
    T, M = a.shape
    N = b.shape[1]
    tm = _col_tile(M, 1408)
    tn = _mxu_tile(N, 768 if tm <= 1024 else 512)

    def body(a_ref, b_ref, o_ref, ob_ref):
        o = _dot_tn(a_ref[...], b_ref[...])
        o_ref[...] = o
        ob_ref[...] = o.astype(BF16)

    return pl.pallas_call(
        body, grid=(M // tm, N // tn),
        in_specs=[pl.BlockSpec((T, tm), lambda m, n: (0, m)), pl.BlockSpec((T, tn), lambda m, n: (0, n))],
        out_specs=[pl.BlockSpec((tm, tn), lambda m, n: (m, n))] * 2,
        out_shape=[jax.ShapeDtypeStruct((M, N), F32), jax.ShapeDtypeStruct((M, N), BF16)],
        name=name, compiler_params=_params("parallel", "parallel"),
    )(a, b)


def _weight_grad_chip_sum(a, b, name):
    T, M = a.shape
    N = b.shape[1]
    tn = _mxu_tile(N, 768)
    steps = N // tn
    half = M // 2

    def body(a_ref, b_ref, sum_ref, sumb_ref, acc, own, got, stage, send_sems, recv_sems):
        n = pl.program_id(0)
        x, y, c = _place()
        slot = n % 2

        def piece(k, s):
            return _remote(stage.at[s], got.at[k], send_sems, recv_sems, k, (x, y, 1 - c))

        @pl.when(n < steps)
        def _():
            acc[...] = _dot_tn(a_ref[...], b_ref[...])

            @pl.when(n >= 2)
            def _():
                piece(n - 2, slot).wait_send()

            stage[slot] = acc[pl.ds(pl.multiple_of((1 - c) * half, 128), half), :].astype(BF16)
            piece(n, slot).start()

        @pl.when(n >= 1)
        def _():
            piece(n - 1, 1 - slot).wait_recv()
            t = own[1 - slot] + got[n - 1].astype(F32)
            sum_ref[...] = t
            sumb_ref[...] = t.astype(BF16)

        @pl.when(n < steps)
        def _():
            own[slot] = acc[pl.ds(pl.multiple_of(c * half, 128), half), :]

        @pl.when(n == steps)
        def _():
            for k in range(max(steps - 2, 0), steps):
                piece(k, k % 2).wait_send()

    last = steps - 1
    return pl.pallas_call(
        body, grid=(steps + 1,),
        in_specs=[_resident((T, M)), pl.BlockSpec((T, tn), lambda n: (0, jnp.minimum(n, last)))],
        out_specs=[pl.BlockSpec((half, tn), lambda n: (0, jnp.maximum(n - 1, 0)))] * 2,
        out_shape=[jax.ShapeDtypeStruct((half, N), F32), jax.ShapeDtypeStruct((half, N), BF16)],
        scratch_shapes=[pltpu.VMEM((M, tn), F32), pltpu.VMEM((2, half, tn), F32), pltpu.VMEM((steps, half, tn), BF16),
                        pltpu.VMEM((2, half, tn), BF16), pltpu.SemaphoreType.DMA((steps,)),
                        pltpu.SemaphoreType.DMA((steps,))],
        name=name, compiler_params=_params("arbitrary"),
    )(a, b)


def _local_step(x, meta, target, w_in_own, w_in, w_out, w_gu, w_down, small, chip, on_ffn_grads=None,
                on_mixer_grads=None):
    wg = _gate_weights(small["w_rgate"], small["w_igate"])
    bg = jnp.concatenate([small["b_rgate"], small["b_igate"]], axis=1)

    p, u, h0 = _in_proj_local(x, meta, small["mix_norm_g"], w_in_own, chip)
    p = _in_proj_rest(u, w_in, p, chip)
    y_rg, hs, xc = _rg_fwd(p, small["conv_w"], small["conv_b"], wg, bg, small["lru_lambda"], small["rg_norm_g"])
    y_hg, o_all, st_all = _hg_fwd(p, small["hg_lower_bound"], small["hg_norm_g"])
    h1, v, yb, gu, act, dh2, dh2b, loss, g_final = _ffn_fwd(
        h0, y_rg, y_hg, w_out, small["ffn_norm_g"], w_gu, w_down, small["final_norm_g"], target)

    g_w_down = _weight_grad(act, dh2b, "grad_w_down")
    dgu, dh1, dh1b, dy, g_ffn = _ffn_bwd(dh2b, gu, w_down, w_gu, h1, small["ffn_norm_g"], dh2, w_out)
    ffn_grads = {"w_down": g_w_down, "w_out": _weight_grad(yb, dh1b, "grad_w_out")}
    if on_ffn_grads is None:
        ffn_grads["w_gate_up"] = _weight_grad(v, dgu, "grad_w_gate_up")
        stages = None
    else:
        gate_up_sums = _weight_grad_chip_sum(v, dgu, "grad_w_gate_up")
        ffn_grads["w_gate_up"] = (None, None)
        stages = on_ffn_grads(ffn_grads)
    dp, g_lb, g_hgn = _hg_bwd(p, o_all, st_all, dy, small["hg_lower_bound"], small["hg_norm_g"])
    early = late = None
    if stages is not None:
        chip_sums, send = stages
        sums = dict(chip_sums(), w_gate_up=gate_up_sums)
        (dp, dy), sums = lax.optimization_barrier(((dp, dy), sums))
        early = send(sums)
    dp, g_cw, g_cb, g_wgate, g_bg, g_lam, g_rgn = _rg_bwd(
        p, xc, hs, dy, dp, small["conv_w"], small["conv_b"], wg, bg, small["lru_lambda"], small["rg_norm_g"])
    if on_mixer_grads is None:
        g_w_in = _weight_grad(u, dp, "grad_w_in")[0]
    else:
        sums = {"w_in": _weight_grad_chip_sum(u, dp, "grad_w_in")}
        (dp, dh1), sums = lax.optimization_barrier(((dp, dh1), sums))
        late = on_mixer_grads(sums)
        g_w_in = None
    grad_x, g_meta, g_mix = _in_bwd(dp, w_in, h0, small["mix_norm_g"], dh1)

    grads = {
        "w_in": g_w_in, "w_out": ffn_grads["w_out"][0],
        "w_gate_up": ffn_grads["w_gate_up"][0], "w_down": ffn_grads["w_down"][0],
        "meta_tokens": g_meta, "mix_norm_g": g_mix, "conv_w": g_cw, "conv_b": g_cb, "w_gates": g_wgate,
        "b_rgate": g_bg[:, :D_RG], "b_igate": g_bg[:, D_RG:], "lru_lambda": g_lam, "rg_norm_g": g_rgn,
        "hg_lower_bound": g_lb, "hg_norm_g": g_hgn, "ffn_norm_g": g_ffn, "final_norm_g": g_final,
    }
    return loss, grad_x, grads, early, late


ANY = pl.BlockSpec(memory_space=pl.ANY)
HALF = D_MODEL // 2

BIG = {"w_in": (D_MODEL, D_IN // N_CHIPS, True), "w_gate_up": (D_MODEL, 2 * D_FF // N_CHIPS, True),
       "w_out": (D_MODEL // N_CHIPS, D_MODEL, False), "w_down": (D_FF // N_CHIPS, D_MODEL, False)}
BIG_NAMES = tuple(BIG)
N_BIG = len(BIG_NAMES)


def _full_shape(name):
    rows, cols, by_col = BIG[name]
    return (rows, cols * N_CHIPS) if by_col else (rows * N_CHIPS, cols)


def _place():
    return lax.axis_index("x"), lax.axis_index("y"), lax.axis_index("c")


def _chip_of(x, y, r):
    fx, fy = (r + 1) >> 1, (r + 1) & 1
    return (1 - x if fx else x), (1 - y if fy else y)


def _half_of(ref, by_col, half):
    start = pl.multiple_of(half * HALF, 128)
    return ref.at[pl.ds(start, HALF), :] if by_col else ref.at[:, pl.ds(start, HALF)]


def _shard_of(ref, name, chip):
    rows, cols, by_col = BIG[name]
    if by_col:
        return ref.at[:, pl.ds(pl.multiple_of(chip * cols, 128), cols)]
    return ref.at[pl.ds(pl.multiple_of(chip * rows, 16), rows), :]


def _shard_half_of(ref, name, chip, half):
    rows, cols, by_col = BIG[name]
    start = pl.multiple_of(half * HALF, 128)
    if by_col:
        return ref.at[pl.ds(start, HALF), pl.ds(pl.multiple_of(chip * cols, 128), cols)]
    return ref.at[pl.ds(pl.multiple_of(chip * rows, 16), rows), pl.ds(start, HALF)]


def _shard_half_part_of(ref, name, chip, half, part):
    rows, cols, by_col = BIG[name]
    start = pl.multiple_of(half * HALF + part * (HALF // 2), 128)
    if by_col:
        return ref.at[pl.ds(start, HALF // 2), pl.ds(pl.multiple_of(chip * cols, 128), cols)]
    return ref.at[pl.ds(pl.multiple_of(chip * rows, 16), rows), pl.ds(start, HALF // 2)]


def _remote(src, dst, send_sems, recv_sems, k, dev):
    return pltpu.make_async_remote_copy(src_ref=src, dst_ref=dst, send_sem=send_sems.at[k], recv_sem=recv_sems.at[k],
                                        device_id=dev, device_id_type=MESH)


def _place_shards(w, small, chip, names, label):
    steps = 4
    n, ns = len(names), len(small)
    in_specs, out_specs = [], []
    for name in names:
        rows, cols, by_col = BIG[name]
        tr = rows // steps
        in_specs.append(pl.BlockSpec((tr, cols), lambda i, s: (i, 0)))
        if by_col:
            out_specs.append(pl.BlockSpec((tr, cols), lambda i, s: (i, s[0])))
        else:
            out_specs.append(pl.BlockSpec((tr, cols), lambda i, s: (s[0] * steps + i, 0)))

    def body(s_ref, *refs):
        ins, small_in = refs[:n], refs[n:n + ns]
        outs, small_out = refs[n + ns:2 * n + ns], refs[2 * n + ns:2 * (n + ns)]
        send_sems, recv_sems, local_sems = refs[2 * (n + ns):]
        i = pl.program_id(0)
        x, y, c = _place()
        chip_ = 2 * x + y
        others = [_chip_of(x, y, r) for r in range(3)]

        def block(a, q):
            cols = small[a].shape[1]
            return small_out[a].at[:, pl.ds(pl.multiple_of(q * cols, 128), cols)]

        def local(a):
            return pltpu.make_async_copy(small_in[a], block(a, chip_), local_sems.at[a])

        def remote(a, r):
            qx, qy = others[r]
            return _remote(small_in[a], block(a, chip_), send_sems, recv_sems, 3 * a + r, (qx, qy, c))

        @pl.when(i == 0)
        def _():
            for a in range(ns):
                local(a).start()
                for r in range(3):
                    remote(a, r).start()

        for a in range(n):
            outs[a][...] = ins[a][...].astype(BF16)

        @pl.when(i == steps - 1)
        def _():
            for a in range(ns):
                for r, (qx, qy) in enumerate(others):
                    landed = block(a, 2 * qx + qy)
                    _remote(landed, landed, send_sems, recv_sems, 3 * a + r, (qx, qy, c)).wait_recv()
                for r in range(3):
                    remote(a, r).wait_send()
                local(a).wait()

    out = pl.pallas_call(
        body,
        grid_spec=pltpu.PrefetchScalarGridSpec(
            num_scalar_prefetch=1, grid=(steps,), in_specs=in_specs + [ANY] * ns, out_specs=out_specs + [ANY] * ns,
            scratch_shapes=[pltpu.SemaphoreType.DMA((max(3 * ns, 1),)), pltpu.SemaphoreType.DMA((max(3 * ns, 1),)),
                            pltpu.SemaphoreType.DMA((max(ns, 1),))]),
        out_shape=([jax.ShapeDtypeStruct(_full_shape(name), BF16) for name in names]
                   + [jax.ShapeDtypeStruct((s.shape[0], s.shape[1] * N_CHIPS), F32) for s in small]),
        name=label, compiler_params=_params("arbitrary"),
    )(chip, *[w[name] for name in names], *small)
    return dict(zip(names, out[:n])), list(out[n:])


def _gather_weights(placed, small, names, label, collective_id):
    n, ns = len(names), len(small)
    hbm = pltpu.MemorySpace.HBM
    outs = [jax.new_ref(placed[nm], memory_space=hbm) for nm in names]
    small_in = [jax.new_ref(s, memory_space=hbm) for s in small]
    small_out = [jax.empty_ref(jax.ShapeDtypeStruct((s.shape[0], s.shape[1] * N_CHIPS), F32), memory_space=hbm)
                 for s in small]
    n_sems = 8 * n + 3 * ns

    @pl.kernel(mesh=plsc.ScalarSubcoreMesh(axis_name="seq", num_cores=1), name=label, out_type=(),
               scratch_types=(pltpu.SemaphoreType.DMA((n_sems,)), pltpu.SemaphoreType.DMA((n_sems,)),
                              pltpu.SemaphoreType.DMA((max(ns, 1),))),
               compiler_params=pltpu.CompilerParams(collective_id=collective_id))
    def launch(send_sems, recv_sems, local_sems):
        x, y, c = _place()
        chip = 2 * x + y
        sibling = (x, y, 1 - c)
        others = [_chip_of(x, y, r) for r in range(3)]
        near = others[:2]
        far = 2 * others[2][0] + others[2][1]
        _handshake([(qx, qy, c) for qx, qy in others] + [sibling])

        def small_block(a, q):
            cols = small[a].shape[1]
            return small_out[a].at[:, pl.ds(pl.multiple_of(q * cols, 128), cols)]

        local = [pltpu.make_async_copy(small_in[a], small_block(a, chip), local_sems.at[a]) for a in range(ns)]
        for cp in local:
            cp.start()

        sends = []
        for a, name in enumerate(names):
            mine = _shard_half_of(outs[a], name, chip, c)
            for r, (qx, qy) in enumerate(near):
                sends.append(_remote(mine, mine, send_sems, recv_sems, 8 * a + r, (qx, qy, c)))
        for a in range(ns):
            for r, (qx, qy) in enumerate(others):
                sends.append(_remote(small_in[a], small_block(a, chip), send_sems, recv_sems,
                                     8 * n + 3 * a + r, (qx, qy, c)))
        for cp in sends:
            cp.start()

        forwards = []

        def forward(piece, k, dev):
            cp = _remote(piece, piece, send_sems, recv_sems, k, dev)
            cp.start()
            forwards.append(cp)

        for a, name in enumerate(names):
            for r, (qx, qy) in enumerate(near):
                landed = _shard_half_of(outs[a], name, 2 * qx + qy, c)
                _remote(landed, landed, send_sems, recv_sems, 8 * a + r, (qx, qy, c)).wait_recv()
                ox, oy = near[1 - r]
                forward(_shard_half_part_of(outs[a], name, 2 * qx + qy, c, r), 8 * a + 2 + r, (ox, oy, c))
                forward(landed, 8 * a + 4 + r, sibling)
        for a, name in enumerate(names):
            for part in range(2):
                qx, qy = near[1 - part]
                landed = _shard_half_part_of(outs[a], name, far, c, part)
                _remote(landed, landed, send_sems, recv_sems, 8 * a + 2 + part, (qx, qy, c)).wait_recv()
                forward(landed, 8 * a + 6 + part, sibling)
        for a in range(ns):
            for r, (qx, qy) in enumerate(others):
                landed = small_block(a, 2 * qx + qy)
                _remote(landed, landed, send_sems, recv_sems, 8 * n + 3 * a + r, (qx, qy, c)).wait_recv()
        for a, name in enumerate(names):
            for r, (qx, qy) in enumerate(near):
                landed = _shard_half_of(outs[a], name, 2 * qx + qy, 1 - c)
                _remote(landed, landed, send_sems, recv_sems, 8 * a + 4 + r, sibling).wait_recv()
            for part in range(2):
                landed = _shard_half_part_of(outs[a], name, far, 1 - c, part)
                _remote(landed, landed, send_sems, recv_sems, 8 * a + 6 + part, sibling).wait_recv()
        for cp in sends + forwards:
            cp.wait_send()
        for cp in local:
            cp.wait()

    launch()
    return {nm: ref[...] for nm, ref in zip(names, outs)}, [ref[...] for ref in small_out]


def _exchange_halves(grads, names, label, collective_id):
    n = len(names)

    def body(*refs):
        ins, outs = refs[:n], refs[n:2 * n]
        send_sems, recv_sems = refs[2 * n:]
        x, y, c = _place()
        _handshake([(x, y, 1 - c)])
        copies = []
        for a, name in enumerate(names):
            copies.append(_remote(_half_of(ins[a], BIG[name][2], 1 - c), outs[a], send_sems, recv_sems, a,
                                  (x, y, 1 - c)))
        for cp in copies:
            cp.start()
        for cp in copies:
            cp.wait()

    def half_shape(name):
        r, c_ = _full_shape(name)
        return (HALF, c_) if BIG[name][2] else (r, HALF)

    out_type = tuple(jax.ShapeDtypeStruct(half_shape(nm), grads[nm].dtype) for nm in names)
    sems = (pltpu.SemaphoreType.DMA((n,)), pltpu.SemaphoreType.DMA((n,)))
    got = pl.kernel(
        body, mesh=plsc.ScalarSubcoreMesh(axis_name="seq", num_cores=1), name=label, out_type=out_type,
        scratch_types=sems, compiler_params=pltpu.CompilerParams(collective_id=collective_id),
    )(*[grads[nm] for nm in names])
    return dict(zip(names, got))


def _chip_sum(grads, got, names, core, label):
    n = len(names)
    steps = 4
    g_specs, blks = [], []
    for name in names:
        rows, cols = got[name].shape
        tr = rows // steps
        if BIG[name][2]:
            g_specs.append(pl.BlockSpec((tr, cols), lambda i, s: (s[0] * steps + i, 0)))
        else:
            g_specs.append(pl.BlockSpec((tr, HALF), lambda i, s: (i, s[0])))
        blks.append(pl.BlockSpec((tr, cols), lambda i, s: (i, 0)))

    def body(s_ref, *refs):
        for a in range(n):
            t = refs[a][...] + refs[n + a][...].astype(F32)
            refs[2 * n + a][...] = t
            refs[3 * n + a][...] = t.astype(BF16)

    out = pl.pallas_call(
        body,
        grid_spec=pltpu.PrefetchScalarGridSpec(num_scalar_prefetch=1, grid=(steps,), in_specs=g_specs + blks,
                                               out_specs=blks + blks),
        out_shape=([jax.ShapeDtypeStruct(got[nm].shape, F32) for nm in names]
                   + [jax.ShapeDtypeStruct(got[nm].shape, BF16) for nm in names]),
        name=label, compiler_params=_params("parallel"),
    )(core, *[grads[nm] for nm in names], *[got[nm] for nm in names])
    return {nm: (out[a], out[n + a]) for a, nm in enumerate(names)}


def _piece_shape(name):
    rows, cols, by_col = BIG[name]
    return (HALF, cols) if by_col else (rows, HALF)


def _handshake(peers):
    barrier = pltpu.get_barrier_semaphore()
    for peer in peers:
        pl.semaphore_signal(barrier, inc=1, device_id=peer, device_id_type=MESH)
    pl.semaphore_wait(barrier, len(peers))


def _send_chip_sums(sums, names, label, collective_id):
    n = len(names)

    def body(*refs):
        ins, outs = refs[:n], refs[n:2 * n]
        send_sems, recv_sems = refs[2 * n:]
        x, y, c = _place()
        others = [_chip_of(x, y, r) for r in range(3)]
        _handshake([(qx, qy, c) for qx, qy in others])
        copies = []
        for a, name in enumerate(names):
            for r, (qx, qy) in enumerate(others):
                copies.append(_remote(_shard_of(ins[a], name, 2 * qx + qy), outs[a].at[r], send_sems, recv_sems,
                                      3 * a + r, (qx, qy, c)))
        for cp in copies:
            cp.start()
        for cp in copies:
            cp.wait()

    return pl.kernel(
        body, mesh=plsc.ScalarSubcoreMesh(axis_name="seq", num_cores=1), name=label,
        out_type=tuple(jax.ShapeDtypeStruct((3,) + _piece_shape(nm), BF16) for nm in names),
        scratch_types=(pltpu.SemaphoreType.DMA((3 * n,)), pltpu.SemaphoreType.DMA((3 * n,))),
        compiler_params=pltpu.CompilerParams(collective_id=collective_id),
    )(*[sums[nm] for nm in names])


def _total(parts, chip_core):
    steps = 2
    in_specs, out_specs, operands = [], [], []
    for name in BIG_NAMES:
        by_col = BIG[name][2]
        pr, pc = _piece_shape(name)
        tr = pr // steps
        if by_col:
            in_specs.append(pl.BlockSpec((tr, pc), lambda i, s: (i, s[0])))
            out_specs.append(pl.BlockSpec((tr, pc), lambda i, s: (s[1] * steps + i, 0)))
        else:
            in_specs.append(pl.BlockSpec((tr, pc), lambda i, s: (s[0] * steps + i, 0)))
            out_specs.append(pl.BlockSpec((tr, pc), lambda i, s: (i, s[1])))
        for r in range(3):
            in_specs.append(pl.BlockSpec((None, tr, pc), lambda i, s, r=r: (r, i, 0)))
        own, got = parts[name]
        operands += [own, got, got, got]

    def body(s_ref, *refs):
        for a in range(N_BIG):
            o_ref, a_ref, b_ref, c_ref = refs[4 * a:4 * a + 4]
            refs[4 * N_BIG + a][...] = (((o_ref[...] + a_ref[...].astype(F32)) + b_ref[...].astype(F32))
                                        + c_ref[...].astype(F32))

    totals = pl.pallas_call(
        body,
        grid_spec=pltpu.PrefetchScalarGridSpec(num_scalar_prefetch=1, grid=(steps,), in_specs=in_specs,
                                               out_specs=out_specs),
        out_shape=[jax.ShapeDtypeStruct(BIG[name][:2], F32) for name in BIG_NAMES],
        name="totals", compiler_params=_params("parallel"),
    )(chip_core, *operands)
    return dict(zip(BIG_NAMES, totals))


VEC_ROWS = 32
VEC_ROW = {"mix_norm_g": 0, "conv_b": 1, "b_rgate": 2, "b_igate": 3, "lru_lambda": 4, "rg_norm_g": 5,
           "hg_lower_bound": 6, "hg_norm_g": 8, "ffn_norm_g": 9, "final_norm_g": 10, "loss": 11,
           "conv_w": 12, "meta_tokens": 16}
N_DEV = 8


def _all_reduce_small(pieces, gates, totals):
    names = list(pieces)
    n_small = 10
    hv, hg = VEC_ROWS // 2, gates.shape[0] // 2

    def body(*refs):
        ins = refs[:len(names)]
        g_ref = refs[len(names)]
        vec_ref, gsum_ref = refs[len(names) + 1 + N_BIG:len(names) + 3 + N_BIG]
        big = refs[len(names) + 3 + N_BIG:len(names) + 3 + 2 * N_BIG]
        (mine_v, sib_v, sib_g, chip_v, chip_g, got_v, got_g, send_sems, recv_sems) = refs[len(names) + 3 + 2 * N_BIG:]
        x, y, c = _place()
        chip = 2 * x + y
        sibling = (x, y, 1 - c)
        share = []
        for a, name in enumerate(BIG_NAMES):
            half = _half_of(big[a], BIG[name][2], c)
            share.append(_remote(half, half, send_sems, recv_sems, n_small + a, sibling))
        mine_v[...] = jnp.zeros_like(mine_v)
        for name, ref in zip(names, ins):
            nr, w = ref.shape
            mine_v[VEC_ROW[name]:VEC_ROW[name] + nr, 0:w] = ref[...]

        swap = [_remote(mine_v, sib_v, send_sems, recv_sems, 0, sibling),
                _remote(g_ref, sib_g, send_sems, recv_sems, 1, sibling)]
        for cp in swap:
            cp.start()
        for cp in swap:
            cp.wait()
        for cp in share:
            cp.start()
        chip_v[...] = mine_v[...] + sib_v[...]
        chip_g[...] = g_ref[...] + sib_g[...]

        rows_v = pl.ds(pl.multiple_of(c * hv, 8), hv)
        rows_g = pl.ds(pl.multiple_of(c * hg, 8), hg)
        got_v[chip] = chip_v[rows_v, :]
        got_g[chip] = chip_g[rows_g, :].astype(BF16)
        sends = []
        for r in range(3):
            qx, qy = _chip_of(x, y, r)
            sends.append(_remote(chip_v.at[rows_v, :], got_v.at[chip], send_sems, recv_sems, 2 + r, (qx, qy, c)))
            sends.append(_remote(got_g.at[chip], got_g.at[chip], send_sems, recv_sems, 5 + r, (qx, qy, c)))
        for cp in sends:
            cp.start()
        for cp in sends:
            cp.wait()
        vec_ref[rows_v, :] = ((got_v[0] + got_v[1]) + got_v[2]) + got_v[3]
        gsum_ref[rows_g, :] = ((got_g[0].astype(F32) + got_g[1].astype(F32)) + got_g[2].astype(F32)
                               ) + got_g[3].astype(F32)

        back = [_remote(vec_ref.at[rows_v, :], vec_ref.at[rows_v, :], send_sems, recv_sems, 8, sibling),
                _remote(gsum_ref.at[rows_g, :], gsum_ref.at[rows_g, :], send_sems, recv_sems, 9, sibling)]
        for cp in back:
            cp.start()
        theirs_v = vec_ref.at[pl.ds(pl.multiple_of((1 - c) * hv, 8), hv), :]
        theirs_g = gsum_ref.at[pl.ds(pl.multiple_of((1 - c) * hg, 8), hg), :]
        _remote(theirs_v, theirs_v, send_sems, recv_sems, 8, sibling).wait_recv()
        _remote(theirs_g, theirs_g, send_sems, recv_sems, 9, sibling).wait_recv()
        for cp in back:
            cp.wait_send()
        for a, name in enumerate(BIG_NAMES):
            theirs = _half_of(big[a], BIG[name][2], 1 - c)
            _remote(theirs, theirs, send_sems, recv_sems, n_small + a, sibling).wait_recv()
        for cp in share:
            cp.wait_send()

    vmem = pl.BlockSpec(memory_space=pltpu.VMEM)
    n_sems = n_small + N_BIG
    out = pl.pallas_call(
        body, in_specs=[vmem] * (len(names) + 1) + [ANY] * N_BIG, out_specs=[vmem, vmem] + [ANY] * N_BIG,
        out_shape=([jax.ShapeDtypeStruct((VEC_ROWS, D_MODEL), F32), jax.ShapeDtypeStruct(gates.shape, F32)]
                   + [jax.ShapeDtypeStruct(BIG[n][:2], F32) for n in BIG_NAMES]),
        input_output_aliases={len(names) + 1 + a: 2 + a for a in range(N_BIG)},
        scratch_shapes=[pltpu.VMEM((VEC_ROWS, D_MODEL), F32), pltpu.VMEM((VEC_ROWS, D_MODEL), F32),
                        pltpu.VMEM(gates.shape, F32), pltpu.VMEM((VEC_ROWS, D_MODEL), F32),
                        pltpu.VMEM(gates.shape, F32), pltpu.VMEM((N_CHIPS, hv, D_MODEL), F32),
                        pltpu.VMEM((N_CHIPS, hg) + gates.shape[1:], BF16),
                        pltpu.SemaphoreType.DMA((n_sems,)), pltpu.SemaphoreType.DMA((n_sems,))],
        name="all_reduce_small",
    )(*[pieces[n] for n in names], gates, *[totals[n] for n in BIG_NAMES])
    return out[0], out[1], dict(zip(BIG_NAMES, out[2:]))


def _adamw_math(w, g, m, v):
    m = ADAM_B1 * m + (1.0 - ADAM_B1) * g
    v = ADAM_B2 * v + (1.0 - ADAM_B2) * (g * g)
    m_hat = m / (1.0 - ADAM_B1 ** ADAM_STEP)
    v_hat = v / (1.0 - ADAM_B2 ** ADAM_STEP)
    delta = -ADAM_LR * (m_hat / (jnp.sqrt(v_hat) + ADAM_EPS) + ADAM_WD * w)
    return delta, m, v


def _adamw_big(w, g, m, v):
    steps = 8
    ring = 3
    n_in = 4 * N_BIG
    blks, tiles = [], []
    for name in BIG_NAMES:
        rows, cols, _ = BIG[name]
        blks.append(pl.BlockSpec((rows // steps, cols), lambda i: (i, 0)))
        tiles.append((rows // steps, cols))

    def body(*refs):
        ins, outs, slots, sems = refs[:n_in], refs[n_in:2 * n_in], refs[2 * n_in:3 * n_in], refs[3 * n_in]
        i = pl.program_id(0)

        def fetch(j, s):
            tr = slots[j].shape[1]
            return pltpu.make_async_copy(ins[j].at[pl.ds(pl.multiple_of(s * tr, 8), tr), :], slots[j].at[s % ring],
                                         sems.at[j, s % ring])

        @pl.when(i == 0)
        def _():
            for s in range(min(ring - 1, steps)):
                for j in range(n_in):
                    fetch(j, s).start(priority=j % 2)

        @pl.when(i + ring - 1 < steps)
        def _():
            for j in range(n_in):
                fetch(j, i + ring - 1).start(priority=j % 2)

        for j in range(n_in):
            fetch(j, i).wait()
        slot = i % ring
        for a in range(N_BIG):
            w_, g, m_, v_ = (slots[k * N_BIG + a][slot] for k in range(4))
            d, nm, nv = _adamw_math(w_, g, m_, v_)
            outs[a][...] = g
            outs[N_BIG + a][...] = d
            outs[2 * N_BIG + a][...] = nm
            outs[3 * N_BIG + a][...] = nv

    shapes = [jax.ShapeDtypeStruct(BIG[name][:2], F32) for name in BIG_NAMES]
    out = pl.pallas_call(
        body, grid=(steps,), in_specs=[ANY] * n_in, out_specs=blks * 4, out_shape=shapes * 4,
        scratch_shapes=[pltpu.VMEM((ring,) + t, F32) for t in tiles] * 4 + [pltpu.SemaphoreType.DMA((n_in, ring))],
        name="adamw_big", compiler_params=_params("arbitrary"),
    )(*[t[name] for t in (w, g, m, v) for name in BIG_NAMES])
    return {name: tuple(out[k * N_BIG + a] for k in range(4)) for a, name in enumerate(BIG_NAMES)}


SMALL = {"meta_tokens": (N_META, D_MODEL // N_CHIPS), "mix_norm_g": (1, D_MODEL), "conv_w": (CONV_W, D_RG // N_CHIPS),
         "conv_b": (1, D_RG), "w_rgate": (D_RG, RG_HEAD_DIM), "b_rgate": (1, D_RG), "w_igate": (D_RG, RG_HEAD_DIM),
         "b_igate": (1, D_RG), "lru_lambda": (1, D_RG), "rg_norm_g": (1, D_RG), "hg_lower_bound": (2, D_HG),
         "hg_norm_g": (1, HG_HEAD_DIM), "ffn_norm_g": (1, D_MODEL), "final_norm_g": (1, D_MODEL)}
SMALL_NAMES = tuple(SMALL)
SHARDED_SMALL = ("meta_tokens", "conv_w")


def _adamw_small(vec, gates, w, m, v):
    n = len(SMALL_NAMES)

    def body(*refs):
        vec_ref, gates_ref = refs[:2]
        w_refs, m_refs, v_refs = refs[2:2 + n], refs[2 + n:2 + 2 * n], refs[2 + 2 * n:2 + 3 * n]
        outs = refs[2 + 3 * n:]
        loss_ref = outs[0]
        x, y, _ = _place()
        chip = 2 * x + y
        loss_ref[...] = vec_ref[VEC_ROW["loss"]:VEC_ROW["loss"] + 1, 0:1]

        def update(k, g):
            g_ref, d_ref, nm_ref, nv_ref = outs[1 + 4 * k:5 + 4 * k]
            g_ref[...] = g
            d_ref[...], nm_ref[...], nv_ref[...] = _adamw_math(w_refs[k][...], g, m_refs[k][...], v_refs[k][...])

        for k, name in enumerate(SMALL_NAMES):
            nr, w_ = SMALL[name]
            if name == "w_rgate":
                update(k, gates_ref[0:D_RG, :])
            elif name == "w_igate":
                update(k, gates_ref[D_RG:2 * D_RG, :])
            elif name in SHARDED_SMALL:
                r0 = VEC_ROW[name]
                for q in range(N_CHIPS):
                    @pl.when(chip == q)
                    def _(k=k, r0=r0, nr=nr, w_=w_, q=q):
                        update(k, vec_ref[r0:r0 + nr, q * w_:(q + 1) * w_])
            else:
                r0 = VEC_ROW[name]
                update(k, vec_ref[r0:r0 + nr, 0:w_])

    vmem = pl.BlockSpec(memory_space=pltpu.VMEM)
    out_shape = [jax.ShapeDtypeStruct((1, 1), F32)]
    for name in SMALL_NAMES:
        out_shape += [jax.ShapeDtypeStruct(SMALL[name], F32)] * 4
    outs = pl.pallas_call(
        body, in_specs=[vmem] * (2 + 3 * n), out_specs=[vmem] * len(out_shape), out_shape=out_shape,
        name="adamw_small",
    )(vec, gates, *[w[k] for k in SMALL_NAMES], *[m[k] for k in SMALL_NAMES], *[v[k] for k in SMALL_NAMES])
    loss = outs[0]
    res = {name: tuple(outs[1 + 4 * k:5 + 4 * k]) for k, name in enumerate(SMALL_NAMES)}
    return loss, res


WEIGHT_NAMES = ("meta_tokens", "mix_norm_g", "w_in", "conv_w", "conv_b", "w_rgate", "b_rgate", "w_igate", "b_igate",
                "lru_lambda", "rg_norm_g", "hg_lower_bound", "hg_norm_g", "w_out", "ffn_norm_g", "w_gate_up", "w_down",
                "final_norm_g")


def _to_2d(name, a):
    if name in BIG:
        return a.reshape(BIG[name][:2])
    return a.reshape(SMALL[name])


def kernel(x, meta_tokens, mix_norm_g, w_in, conv_w, conv_b, w_rgate, b_rgate, w_igate, b_igate, lru_lambda, rg_norm_g, hg_lower_bound, hg_norm_g, w_out, ffn_norm_g, w_gate_up, w_down, final_norm_g, loss_target, m_meta_tokens, m_mix_norm_g, m_w_in, m_conv_w, m_conv_b, m_w_rgate, m_b_rgate, m_w_igate, m_b_igate, m_lru_lambda, m_rg_norm_g, m_hg_lower_bound, m_hg_norm_g, m_w_out, m_ffn_norm_g, m_w_gate_up, m_w_down, m_final_norm_g, v_meta_tokens, v_mix_norm_g, v_w_in, v_conv_w, v_conv_b, v_w_rgate, v_b_rgate, v_w_igate, v_b_igate, v_lru_lambda, v_rg_norm_g, v_hg_lower_bound, v_hg_norm_g, v_w_out, v_ffn_norm_g, v_w_gate_up, v_w_down, v_final_norm_g):
    w_raw = dict(zip(WEIGHT_NAMES, (meta_tokens, mix_norm_g, w_in, conv_w, conv_b, w_rgate, b_rgate, w_igate, b_igate,
                                    lru_lambda, rg_norm_g, hg_lower_bound, hg_norm_g, w_out, ffn_norm_g, w_gate_up,
                                    w_down, final_norm_g)))
    m_raw = dict(zip(WEIGHT_NAMES, (m_meta_tokens, m_mix_norm_g, m_w_in, m_conv_w, m_conv_b, m_w_rgate, m_b_rgate,
                                    m_w_igate, m_b_igate, m_lru_lambda, m_rg_norm_g, m_hg_lower_bound, m_hg_norm_g,
                                    m_w_out, m_ffn_norm_g, m_w_gate_up, m_w_down, m_final_norm_g)))
    v_raw = dict(zip(WEIGHT_NAMES, (v_meta_tokens, v_mix_norm_g, v_w_in, v_conv_w, v_conv_b, v_w_rgate, v_b_rgate,
                                    v_w_igate, v_b_igate, v_lru_lambda, v_rg_norm_g, v_hg_lower_bound, v_hg_norm_g,
                                    v_w_out, v_ffn_norm_g, v_w_gate_up, v_w_down, v_final_norm_g)))
    w = {k: _to_2d(k, a) for k, a in w_raw.items()}
    m = {k: _to_2d(k, a) for k, a in m_raw.items()}
    v = {k: _to_2d(k, a) for k, a in v_raw.items()}

    x_i, y_i, c_i = _place()
    core = jnp.reshape(c_i, (1,)).astype(jnp.int32)
    chip = jnp.reshape(2 * x_i + y_i, (1,)).astype(jnp.int32)
    chip_core = jnp.concatenate([chip, core])

    first_names, rest_names = ("w_in",), ("w_out", "w_gate_up", "w_down")
    placed, _ = _place_shards(w, [], chip, first_names, "place_first")
    first, _ = _gather_weights(placed, [], first_names, "gather_first", 1)
    placed, (meta_full, cw_full) = _place_shards(w, [w["meta_tokens"], w["conv_w"]], chip, rest_names, "place_shards")
    rest, _ = _gather_weights(placed, [], rest_names, "gather_rest", 2)
    full = {**first, **rest}

    seq = x.shape[1]
    small ={k: w[k] for k in SMALL_NAMES if k not in SHARDED_SMALL}
    small["conv_w"] = cw_full

    def send_to_chips(sums, names, tag, collective_id):
        arrived = _send_chip_sums({n: sums[n][1] for n in names}, names, "send_chip_sums_" + tag, collective_id)
        return {n: (sums[n][0], a) for n, a in zip(names, arrived)}

    def reduce_to_chips(grads, names, send_names, tag, collective_ids):
        got = _exchange_halves({n: grads[n][1] for n in names}, names, "exchange_halves_" + tag, collective_ids[0])

        def chip_sums():
            return _chip_sum({n: grads[n][0] for n in names}, got, names, core, "chip_sum_" + tag)

        return chip_sums, lambda sums: send_to_chips(sums, send_names, tag, collective_ids[1])

    ffn_names, mixer_names = ("w_gate_up", "w_down", "w_out"), ("w_in",)
    loss, grad_x, grads, parts, parts_mixer = _local_step(
        x.reshape(seq, D_MODEL), meta_full, loss_target.reshape(seq, D_MODEL),
        w["w_in"], full["w_in"], full["w_out"], full["w_gate_up"], full["w_down"], small, chip,
        on_ffn_grads=lambda g: reduce_to_chips(g, ("w_down", "w_out"), ffn_names, "ffn", (3, 4)),
        on_mixer_grads=lambda sums: send_to_chips(sums, mixer_names, "mixer", 5))
    parts.update(parts_mixer)
    totals = _total(parts, chip_core)
    pieces = {k: grads[k] for k in VEC_ROW if k != "loss"}
    pieces["loss"] = loss
    vec, gates, g_big = _all_reduce_small(pieces, grads["w_gates"], totals)
    loss_sum, res = _adamw_small(vec, gates, w, m, v)
    res.update(_adamw_big(w, g_big, m, v))

    out = [loss_sum.reshape(()), grad_x.reshape(1, seq, D_MODEL)]
    for j in range(4):
        out += [res[n][j].reshape(w_raw[n].shape) for n in WEIGHT_NAMES]
    return tuple(out)
```
